```python
import jax, jax.numpy as jnp
from jax import lax
import numpy as np

D_MODEL = 1024
BATCH = 32
SEQ = 2048
DEPTH = 1

D_MIX = D_MODEL
SG_WIDTH = D_MIX // 2
SG_GROUPS = 8
SG_GROUP_DIM = SG_WIDTH // SG_GROUPS
SG_CHUNK = 128
DN_WIDTH = D_MIX - SG_WIDTH
DN_HEAD_DIM = 128
DN_HEADS = DN_WIDTH // DN_HEAD_DIM
DN_CHUNK = 64
CONV_K = 4
D_FF = 2816
EPS = 1e-6
IN_COLS = 2 * SG_WIDTH + 4 * DN_WIDTH + 2 * DN_HEADS

kernel_name = "hybrid_gmlp_gated_deltanet_macaron"


def rmsnorm(x, g):
    xf = x.astype(jnp.float32)
    y = xf * lax.rsqrt(jnp.mean(xf * xf, axis=-1, keepdims=True) + EPS)
    return (y * g.astype(jnp.float32)).astype(x.dtype)


def layernorm(x, g, b):
    xf = x.astype(jnp.float32)
    mu = jnp.mean(xf, axis=-1, keepdims=True)
    var = jnp.mean(jnp.square(xf - mu), axis=-1, keepdims=True)
    y = (xf - mu) * lax.rsqrt(var + EPS)
    return (y * g.astype(jnp.float32) + b.astype(jnp.float32)).astype(x.dtype)


def l2norm(x):
    return x * lax.rsqrt(jnp.sum(x * x, axis=-1, keepdims=True) + EPS)


def swiglu(h, w_gate, w_up, w_down):
    return (jax.nn.silu(h @ w_gate) * (h @ w_up)) @ w_down


def causal_dwconv(x, w):
    k_taps = w.shape[0]
    t_len = x.shape[1]
    xp = jnp.pad(x, ((0, 0), (k_taps - 1, 0), (0, 0)))
    y = xp[:, 0:t_len] * w[0]
    for j in range(1, k_taps):
        y = y + xp[:, j:j + t_len] * w[j]
    return y


def chunked_spatial_gating(u, v, ln_g, ln_b, w_s, b_s):
    bsz, t_len, _ = v.shape
    n_chunks = t_len // SG_CHUNK
    v = layernorm(v, ln_g, ln_b).reshape(bsz, n_chunks, SG_CHUNK, SG_GROUPS, SG_GROUP_DIM)
    pos = jnp.arange(SG_CHUNK)
    causal = pos[:, None] >= pos[None, :]
    w_causal = jnp.where(causal, w_s, jnp.zeros((), w_s.dtype))
    vs = jnp.einsum('gts,bnsgc->bntgc', w_causal, v) + b_s.T[:, :, None]
    return u * vs.reshape(bsz, t_len, SG_WIDTH)


def gated_delta_rule(q, k, v, g, beta):
    bsz, t_len, n_heads, dk = q.shape
    dv = v.shape[-1]
    c = DN_CHUNK
    n_chunks = t_len // c

    def chunks(a):
        return a.reshape(bsz, n_chunks, c, n_heads, a.shape[-1]).transpose(1, 0, 3, 2, 4)

    q = chunks(q) * (dk ** -0.5)
    k = chunks(k)
    v = chunks(v)
    g = chunks(g[..., None])[..., 0]
    beta = chunks(beta[..., None])[..., 0]
    gc = jnp.cumsum(g, axis=-1)
    pos = jnp.arange(c)
    incl = pos[:, None] >= pos[None, :]
    strict = pos[:, None] > pos[None, :]
    decay = jnp.exp(jnp.where(incl, gc[..., :, None] - gc[..., None, :], -jnp.inf))
    k_beta = k * beta[..., None]
    v_beta = v * beta[..., None]
    l_mat = jnp.where(strict, jnp.einsum('nbhcd,nbhsd->nbhcs', k_beta, k) * decay, 0.0)
    rhs = jnp.concatenate([v_beta, k_beta * jnp.exp(gc)[..., None]], axis=-1)
    sol = lax.linalg.triangular_solve(l_mat, rhs, left_side=True, lower=True, unit_diagonal=True)
    u_wy, w_wy = sol[..., :dv], sol[..., dv:]
    qk = jnp.einsum('nbhcd,nbhsd->nbhcs', q, k) * decay
    q_dec = q * jnp.exp(gc)[..., None]
    k_dec = k * jnp.exp(gc[..., -1:] - gc)[..., None]
    g_last = jnp.exp(gc[..., -1])

    def step(state, xs):
        q_n, k_n, u_n, w_n, qk_n, gl_n = xs
        v_new = u_n - jnp.einsum('bhcd,bhde->bhce', w_n, state)
        o_n = jnp.einsum('bhcd,bhde->bhce', q_n, state) + jnp.einsum('bhcs,bhse->bhce', qk_n, v_new)
        state = state * gl_n[..., None, None] + jnp.einsum('bhcd,bhce->bhde', k_n, v_new)
        return state, o_n

    s0 = jnp.zeros((bsz, n_heads, dk, dv), jnp.float32)
    _, o = lax.scan(step, s0, (q_dec, k_dec, u_wy, w_wy, qk, g_last))
    return o.transpose(1, 0, 3, 2, 4).reshape(bsz, t_len, n_heads, dv)


def hybrid_mixer(h, w_in, conv_w, a_log, dt_bias, dn_norm, sg_ln_g, sg_ln_b, sg_w, sg_b, w_out):
    bsz, t_len, _ = h.shape
    proj = h @ w_in
    o1 = SG_WIDTH
    o2 = 2 * SG_WIDTH
    o3 = o2 + 3 * DN_WIDTH
    o4 = o3 + DN_WIDTH
    o5 = o4 + DN_HEADS
    sg_u, sg_v, dn_qkv, dn_z, dn_b, dn_a = jnp.split(proj, [o1, o2, o3, o4, o5], axis=-1)

    sg_out = chunked_spatial_gating(jax.nn.gelu(sg_u), jax.nn.gelu(sg_v), sg_ln_g, sg_ln_b, sg_w, sg_b)

    qkv = jax.nn.silu(causal_dwconv(dn_qkv, conv_w)).astype(jnp.float32)
    q, k, v = jnp.split(qkv, 3, axis=-1)
    q = l2norm(q.reshape(bsz, t_len, DN_HEADS, DN_HEAD_DIM))
    k = l2norm(k.reshape(bsz, t_len, DN_HEADS, DN_HEAD_DIM))
    v = v.reshape(bsz, t_len, DN_HEADS, DN_HEAD_DIM)
    beta = jax.nn.sigmoid(dn_b.astype(jnp.float32))
    g = -jnp.exp(a_log.astype(jnp.float32)) * jax.nn.softplus(dn_a.astype(jnp.float32) + dt_bias.astype(jnp.float32))
    o = gated_delta_rule(q, k, v, g, beta)
    o = o * lax.rsqrt(jnp.mean(o * o, axis=-1, keepdims=True) + EPS) * dn_norm.astype(jnp.float32)
    z = dn_z.reshape(bsz, t_len, DN_HEADS, DN_HEAD_DIM).astype(jnp.float32)
    dn_out = (o * jax.nn.silu(z)).reshape(bsz, t_len, DN_WIDTH).astype(h.dtype)

    return jnp.concatenate([sg_out, dn_out], axis=-1) @ w_out


def _fwd_setup_inputs(seed: int = 0) -> dict:
    key = jax.random.key(seed)
    ks = jax.random.split(key, 24)
    f32 = jnp.float32

    def nrm(k, shape, scale):
        return jax.random.normal(k, shape, f32) * scale

    def gain(k, shape):
        return 1.0 + 0.02 * jax.random.normal(k, shape, f32)

    dt = jnp.exp(jax.random.uniform(ks[8], (DEPTH, DN_HEADS), f32, np.log(0.001), np.log(0.1)))
    return {
        "x": jax.random.normal(ks[0], (BATCH, SEQ, D_MODEL), f32),
        "ffn1_norm": gain(ks[1], (DEPTH, D_MODEL)),
        "ffn1_w_gate": nrm(ks[2], (DEPTH, D_MODEL, D_FF), D_MODEL ** -0.5),
        "ffn1_w_up": nrm(ks[3], (DEPTH, D_MODEL, D_FF), D_MODEL ** -0.5),
        "ffn1_w_down": nrm(ks[4], (DEPTH, D_FF, D_MODEL), D_FF ** -0.5),
        "mix_norm": gain(ks[5], (DEPTH, D_MODEL)),
        "w_in": nrm(ks[6], (DEPTH, D_MODEL, IN_COLS), D_MODEL ** -0.5),
        "conv_w": nrm(ks[7], (DEPTH, CONV_K, 3 * DN_WIDTH), CONV_K ** -0.5),
        "a_log": jnp.log(jax.random.uniform(ks[9], (DEPTH, DN_HEADS), f32, 1.0, 16.0)),
        "dt_bias": dt + jnp.log(-jnp.expm1(-dt)),
        "dn_norm": gain(ks[10], (DEPTH, DN_HEAD_DIM)),
        "sg_ln_g": gain(ks[11], (DEPTH, SG_WIDTH)),
        "sg_ln_b": nrm(ks[12], (DEPTH, SG_WIDTH), 0.02),
        "sg_w": nrm(ks[13], (DEPTH, SG_GROUPS, SG_CHUNK, SG_CHUNK), SG_CHUNK ** -0.5),
        "sg_b": gain(ks[14], (DEPTH, SG_GROUPS, SG_CHUNK)),
        "w_out": nrm(ks[15], (DEPTH, D_MIX, D_MODEL), D_MIX ** -0.5),
        "ffn2_norm": gain(ks[16], (DEPTH, D_MODEL)),
        "ffn2_w_gate": nrm(ks[17], (DEPTH, D_MODEL, D_FF), D_MODEL ** -0.5),
        "ffn2_w_up": nrm(ks[18], (DEPTH, D_MODEL, D_FF), D_MODEL ** -0.5),
        "ffn2_w_down": nrm(ks[19], (DEPTH, D_FF, D_MODEL), D_FF ** -0.5),
        "final_norm": gain(ks[20], (D_MODEL,)),
    }


def _fwd_reference(x, ffn1_norm, ffn1_w_gate, ffn1_w_up, ffn1_w_down, mix_norm, w_in, conv_w, a_log,
              dt_bias, dn_norm, sg_ln_g, sg_ln_b, sg_w, sg_b, w_out, ffn2_norm, ffn2_w_gate,
              ffn2_w_up, ffn2_w_down, final_norm):
    for l in range(DEPTH):
        h = rmsnorm(x, ffn1_norm[l])
        x = x + 0.5 * swiglu(h, ffn1_w_gate[l], ffn1_w_up[l], ffn1_w_down[l])
        h = rmsnorm(x, mix_norm[l])
        x = x + hybrid_mixer(h, w_in[l], conv_w[l], a_log[l], dt_bias[l], dn_norm[l],
                             sg_ln_g[l], sg_ln_b[l], sg_w[l], sg_b[l], w_out[l])
        h = rmsnorm(x, ffn2_norm[l])
        x = x + 0.5 * swiglu(h, ffn2_w_gate[l], ffn2_w_up[l], ffn2_w_down[l])
    return rmsnorm(x, final_norm)


import jax as _jax
import jax.numpy as _jnp

TWIN_FORMAT = 'train_step'
FWD_PARAMS = ['x', 'ffn1_norm', 'ffn1_w_gate', 'ffn1_w_up', 'ffn1_w_down', 'mix_norm', 'w_in', 'conv_w', 'a_log', 'dt_bias', 'dn_norm', 'sg_ln_g', 'sg_ln_b', 'sg_w', 'sg_b', 'w_out', 'ffn2_norm', 'ffn2_w_gate', 'ffn2_w_up', 'ffn2_w_down', 'final_norm']
TWIN_WEIGHTS = ['ffn1_norm', 'ffn1_w_gate', 'ffn1_w_up', 'ffn1_w_down', 'mix_norm', 'w_in', 'conv_w', 'a_log', 'dt_bias', 'dn_norm', 'sg_ln_g', 'sg_ln_b', 'sg_w', 'sg_b', 'w_out', 'ffn2_norm', 'ffn2_w_gate', 'ffn2_w_up', 'ffn2_w_down', 'final_norm']
TWIN_DIFF_INPUT = 'x'
TWIN_INPUTS = ['x', 'ffn1_norm', 'ffn1_w_gate', 'ffn1_w_up', 'ffn1_w_down', 'mix_norm', 'w_in', 'conv_w', 'a_log', 'dt_bias', 'dn_norm', 'sg_ln_g', 'sg_ln_b', 'sg_w', 'sg_b', 'w_out', 'ffn2_norm', 'ffn2_w_gate', 'ffn2_w_up', 'ffn2_w_down', 'final_norm', 'loss_target', 'm_ffn1_norm', 'm_ffn1_w_gate', 'm_ffn1_w_up', 'm_ffn1_w_down', 'm_mix_norm', 'm_w_in', 'm_conv_w', 'm_a_log', 'm_dt_bias', 'm_dn_norm', 'm_sg_ln_g', 'm_sg_ln_b', 'm_sg_w', 'm_sg_b', 'm_w_out', 'm_ffn2_norm', 'm_ffn2_w_gate', 'm_ffn2_w_up', 'm_ffn2_w_down', 'm_final_norm', 'v_ffn1_norm', 'v_ffn1_w_gate', 'v_ffn1_w_up', 'v_ffn1_w_down', 'v_mix_norm', 'v_w_in', 'v_conv_w', 'v_a_log', 'v_dt_bias', 'v_dn_norm', 'v_sg_ln_g', 'v_sg_ln_b', 'v_sg_w', 'v_sg_b', 'v_w_out', 'v_ffn2_norm', 'v_ffn2_w_gate', 'v_ffn2_w_up', 'v_ffn2_w_down', 'v_final_norm']
TWIN_OUTPUTS = ['loss', 'grad_x', 'grad_ffn1_norm', 'grad_ffn1_w_gate', 'grad_ffn1_w_up', 'grad_ffn1_w_down', 'grad_mix_norm', 'grad_w_in', 'grad_conv_w', 'grad_a_log', 'grad_dt_bias', 'grad_dn_norm', 'grad_sg_ln_g', 'grad_sg_ln_b', 'grad_sg_w', 'grad_sg_b', 'grad_w_out', 'grad_ffn2_norm', 'grad_ffn2_w_gate', 'grad_ffn2_w_up', 'grad_ffn2_w_down', 'grad_final_norm', 'delta_ffn1_norm', 'delta_ffn1_w_gate', 'delta_ffn1_w_up', 'delta_ffn1_w_down', 'delta_mix_norm', 'delta_w_in', 'delta_conv_w', 'delta_a_log', 'delta_dt_bias', 'delta_dn_norm', 'delta_sg_ln_g', 'delta_sg_ln_b', 'delta_sg_w', 'delta_sg_b', 'delta_w_out', 'delta_ffn2_norm', 'delta_ffn2_w_gate', 'delta_ffn2_w_up', 'delta_ffn2_w_down', 'delta_final_norm', 'new_m_ffn1_norm', 'new_m_ffn1_w_gate', 'new_m_ffn1_w_up', 'new_m_ffn1_w_down', 'new_m_mix_norm', 'new_m_w_in', 'new_m_conv_w', 'new_m_a_log', 'new_m_dt_bias', 'new_m_dn_norm', 'new_m_sg_ln_g', 'new_m_sg_ln_b', 'new_m_sg_w', 'new_m_sg_b', 'new_m_w_out', 'new_m_ffn2_norm', 'new_m_ffn2_w_gate', 'new_m_ffn2_w_up', 'new_m_ffn2_w_down', 'new_m_final_norm', 'new_v_ffn1_norm', 'new_v_ffn1_w_gate', 'new_v_ffn1_w_up', 'new_v_ffn1_w_down', 'new_v_mix_norm', 'new_v_w_in', 'new_v_conv_w', 'new_v_a_log', 'new_v_dt_bias', 'new_v_dn_norm', 'new_v_sg_ln_g', 'new_v_sg_ln_b', 'new_v_sg_w', 'new_v_sg_b', 'new_v_w_out', 'new_v_ffn2_norm', 'new_v_ffn2_w_gate', 'new_v_ffn2_w_up', 'new_v_ffn2_w_down', 'new_v_final_norm']
TWIN_LEAF_KINDS = {'loss': 'loss', 'grad_x': 'grad_x', 'grad_ffn1_norm': 'grad_w', 'grad_ffn1_w_gate': 'grad_w', 'grad_ffn1_w_up': 'grad_w', 'grad_ffn1_w_down': 'grad_w', 'grad_mix_norm': 'grad_w', 'grad_w_in': 'grad_w', 'grad_conv_w': 'grad_w', 'grad_a_log': 'grad_w', 'grad_dt_bias': 'grad_w', 'grad_dn_norm': 'grad_w', 'grad_sg_ln_g': 'grad_w', 'grad_sg_ln_b': 'grad_w', 'grad_sg_w': 'grad_w', 'grad_sg_b': 'grad_w', 'grad_w_out': 'grad_w', 'grad_ffn2_norm': 'grad_w', 'grad_ffn2_w_gate': 'grad_w', 'grad_ffn2_w_up': 'grad_w', 'grad_ffn2_w_down': 'grad_w', 'grad_final_norm': 'grad_w', 'delta_ffn1_norm': 'delta_w', 'delta_ffn1_w_gate': 'delta_w', 'delta_ffn1_w_up': 'delta_w', 'delta_ffn1_w_down': 'delta_w', 'delta_mix_norm': 'delta_w', 'delta_w_in': 'delta_w', 'delta_conv_w': 'delta_w', 'delta_a_log': 'delta_w', 'delta_dt_bias': 'delta_w', 'delta_dn_norm': 'delta_w', 'delta_sg_ln_g': 'delta_w', 'delta_sg_ln_b': 'delta_w', 'delta_sg_w': 'delta_w', 'delta_sg_b': 'delta_w', 'delta_w_out': 'delta_w', 'delta_ffn2_norm': 'delta_w', 'delta_ffn2_w_gate': 'delta_w', 'delta_ffn2_w_up': 'delta_w', 'delta_ffn2_w_down': 'delta_w', 'delta_final_norm': 'delta_w', 'new_m_ffn1_norm': 'new_m', 'new_m_ffn1_w_gate': 'new_m', 'new_m_ffn1_w_up': 'new_m', 'new_m_ffn1_w_down': 'new_m', 'new_m_mix_norm': 'new_m', 'new_m_w_in': 'new_m', 'new_m_conv_w': 'new_m', 'new_m_a_log': 'new_m', 'new_m_dt_bias': 'new_m', 'new_m_dn_norm': 'new_m', 'new_m_sg_ln_g': 'new_m', 'new_m_sg_ln_b': 'new_m', 'new_m_sg_w': 'new_m', 'new_m_sg_b': 'new_m', 'new_m_w_out': 'new_m', 'new_m_ffn2_norm': 'new_m', 'new_m_ffn2_w_gate': 'new_m', 'new_m_ffn2_w_up': 'new_m', 'new_m_ffn2_w_down': 'new_m', 'new_m_final_norm': 'new_m', 'new_v_ffn1_norm': 'new_v', 'new_v_ffn1_w_gate': 'new_v', 'new_v_ffn1_w_up': 'new_v', 'new_v_ffn1_w_down': 'new_v', 'new_v_mix_norm': 'new_v', 'new_v_w_in': 'new_v', 'new_v_conv_w': 'new_v', 'new_v_a_log': 'new_v', 'new_v_dt_bias': 'new_v', 'new_v_dn_norm': 'new_v', 'new_v_sg_ln_g': 'new_v', 'new_v_sg_ln_b': 'new_v', 'new_v_sg_w': 'new_v', 'new_v_sg_b': 'new_v', 'new_v_w_out': 'new_v', 'new_v_ffn2_norm': 'new_v', 'new_v_ffn2_w_gate': 'new_v', 'new_v_ffn2_w_up': 'new_v', 'new_v_ffn2_w_down': 'new_v', 'new_v_final_norm': 'new_v'}


def _forward(args):
    return _fwd_reference(*[args[k] for k in FWD_PARAMS])


def _output_shape():
    out = _jax.eval_shape(lambda: _forward(_fwd_setup_inputs(0)))
    return out.shape, out.dtype

N_MICROBATCH = 1
ADAM_LR = 0.001
ADAM_B1 = 0.9
ADAM_B2 = 0.999
ADAM_EPS = 1e-08
ADAM_WD = 0.01
ADAM_STEP = 10
PER_EXAMPLE_BATCH_AXIS = {'x': 0, 'loss_target': 0}
SHARED_INPUTS = []
_WEIGHT_DTYPES = {'ffn1_norm': _jnp.float32, 'ffn1_w_gate': _jnp.float32, 'ffn1_w_up': _jnp.float32, 'ffn1_w_down': _jnp.float32, 'mix_norm': _jnp.float32, 'w_in': _jnp.float32, 'conv_w': _jnp.float32, 'a_log': _jnp.float32, 'dt_bias': _jnp.float32, 'dn_norm': _jnp.float32, 'sg_ln_g': _jnp.float32, 'sg_ln_b': _jnp.float32, 'sg_w': _jnp.float32, 'sg_b': _jnp.float32, 'w_out': _jnp.float32, 'ffn2_norm': _jnp.float32, 'ffn2_w_gate': _jnp.float32, 'ffn2_w_up': _jnp.float32, 'ffn2_w_down': _jnp.float32, 'final_norm': _jnp.float32}
MOMENT_SCALE = {'ffn1_norm': 1.383292e-01, 'ffn1_w_gate': 5.251437e-02, 'ffn1_w_up': 5.084509e-02, 'ffn1_w_down': 8.428516e-02, 'mix_norm': 2.033117e-01, 'w_in': 1.179638e-01, 'conv_w': 1.024653e-01, 'a_log': 5.325245e-01, 'dt_bias': 5.203210e-01, 'dn_norm': 2.940833e-01, 'sg_ln_g': 1.000481e-01, 'sg_ln_b': 9.101014e-02, 'sg_w': 6.623132e-02, 'sg_b': 9.553911e-02, 'w_out': 1.486193e-01, 'ffn2_norm': 8.667406e-02, 'ffn2_w_gate': 3.627721e-02, 'ffn2_w_up': 3.513815e-02, 'ffn2_w_down': 5.827746e-02, 'final_norm': 6.390095e+01}


def _to_microbatches(a, axis):
    t = _jnp.moveaxis(a, axis, 0)
    t = t.reshape((N_MICROBATCH, t.shape[0] // N_MICROBATCH) + t.shape[1:])
    return _jnp.moveaxis(t, 1, axis + 1)


def setup_inputs(seed: int = 0) -> dict:
    inp = _fwd_setup_inputs(seed)
    key = _jax.random.fold_in(_jax.random.key(seed), 7919)
    shape, _ = _output_shape()
    out = dict(inp)
    out["loss_target"] = _jax.random.normal(_jax.random.fold_in(key, 0), shape, _jnp.float32)
    for i, name in enumerate(TWIN_WEIGHTS):
        w = inp[name].astype(_jnp.float32)
        if MOMENT_SCALE is None:
            s = _jnp.sqrt(_jnp.mean(_jnp.square(w)) + 1e-30)
        else:
            s = MOMENT_SCALE[name]
        km, kv = _jax.random.split(_jax.random.fold_in(key, i + 1))
        out[name] = w
        out["m_" + name] = s * _jax.random.normal(km, w.shape, _jnp.float32)
        out["v_" + name] = (s * s) * _jax.random.uniform(kv, w.shape, _jnp.float32, 0.5, 1.5)
    if N_MICROBATCH > 1:
        for name, axis in PER_EXAMPLE_BATCH_AXIS.items():
            out[name] = _to_microbatches(out[name], axis)
    return {'x': out['x'], 'ffn1_norm': out['ffn1_norm'], 'ffn1_w_gate': out['ffn1_w_gate'], 'ffn1_w_up': out['ffn1_w_up'], 'ffn1_w_down': out['ffn1_w_down'], 'mix_norm': out['mix_norm'], 'w_in': out['w_in'], 'conv_w': out['conv_w'], 'a_log': out['a_log'], 'dt_bias': out['dt_bias'], 'dn_norm': out['dn_norm'], 'sg_ln_g': out['sg_ln_g'], 'sg_ln_b': out['sg_ln_b'], 'sg_w': out['sg_w'], 'sg_b': out['sg_b'], 'w_out': out['w_out'], 'ffn2_norm': out['ffn2_norm'], 'ffn2_w_gate': out['ffn2_w_gate'], 'ffn2_w_up': out['ffn2_w_up'], 'ffn2_w_down': out['ffn2_w_down'], 'final_norm': out['final_norm'], 'loss_target': out['loss_target'], 'm_ffn1_norm': out['m_ffn1_norm'], 'm_ffn1_w_gate': out['m_ffn1_w_gate'], 'm_ffn1_w_up': out['m_ffn1_w_up'], 'm_ffn1_w_down': out['m_ffn1_w_down'], 'm_mix_norm': out['m_mix_norm'], 'm_w_in': out['m_w_in'], 'm_conv_w': out['m_conv_w'], 'm_a_log': out['m_a_log'], 'm_dt_bias': out['m_dt_bias'], 'm_dn_norm': out['m_dn_norm'], 'm_sg_ln_g': out['m_sg_ln_g'], 'm_sg_ln_b': out['m_sg_ln_b'], 'm_sg_w': out['m_sg_w'], 'm_sg_b': out['m_sg_b'], 'm_w_out': out['m_w_out'], 'm_ffn2_norm': out['m_ffn2_norm'], 'm_ffn2_w_gate': out['m_ffn2_w_gate'], 'm_ffn2_w_up': out['m_ffn2_w_up'], 'm_ffn2_w_down': out['m_ffn2_w_down'], 'm_final_norm': out['m_final_norm'], 'v_ffn1_norm': out['v_ffn1_norm'], 'v_ffn1_w_gate': out['v_ffn1_w_gate'], 'v_ffn1_w_up': out['v_ffn1_w_up'], 'v_ffn1_w_down': out['v_ffn1_w_down'], 'v_mix_norm': out['v_mix_norm'], 'v_w_in': out['v_w_in'], 'v_conv_w': out['v_conv_w'], 'v_a_log': out['v_a_log'], 'v_dt_bias': out['v_dt_bias'], 'v_dn_norm': out['v_dn_norm'], 'v_sg_ln_g': out['v_sg_ln_g'], 'v_sg_ln_b': out['v_sg_ln_b'], 'v_sg_w': out['v_sg_w'], 'v_sg_b': out['v_sg_b'], 'v_w_out': out['v_w_out'], 'v_ffn2_norm': out['v_ffn2_norm'], 'v_ffn2_w_gate': out['v_ffn2_w_gate'], 'v_ffn2_w_up': out['v_ffn2_w_up'], 'v_ffn2_w_down': out['v_ffn2_w_down'], 'v_final_norm': out['v_final_norm']}


def _loss(weights, diff, rest, loss_target):
    with _jax.named_scope("forward"):
        args = {**rest, TWIN_DIFF_INPUT: diff, **{k: w.astype(_WEIGHT_DTYPES[k]) for k, w in weights.items()}}
        y = _forward(args)
    with _jax.named_scope("loss_head"):
        err = _jnp.square(y.astype(_jnp.float32) - loss_target)
        return 0.5 * _jnp.sum(_jnp.mean(err, axis=-1)) if err.ndim else 0.5 * err


def _adamw(w, g, m, v):
    m = ADAM_B1 * m + (1.0 - ADAM_B1) * g
    v = ADAM_B2 * v + (1.0 - ADAM_B2) * _jnp.square(g)
    m_hat = m / (1.0 - ADAM_B1 ** ADAM_STEP)
    v_hat = v / (1.0 - ADAM_B2 ** ADAM_STEP)
    delta = -ADAM_LR * (m_hat / (_jnp.sqrt(v_hat) + ADAM_EPS) + ADAM_WD * w)
    return delta, m, v


def reference(x, ffn1_norm, ffn1_w_gate, ffn1_w_up, ffn1_w_down, mix_norm, w_in, conv_w, a_log, dt_bias, dn_norm, sg_ln_g, sg_ln_b, sg_w, sg_b, w_out, ffn2_norm, ffn2_w_gate, ffn2_w_up, ffn2_w_down, final_norm, loss_target, m_ffn1_norm, m_ffn1_w_gate, m_ffn1_w_up, m_ffn1_w_down, m_mix_norm, m_w_in, m_conv_w, m_a_log, m_dt_bias, m_dn_norm, m_sg_ln_g, m_sg_ln_b, m_sg_w, m_sg_b, m_w_out, m_ffn2_norm, m_ffn2_w_gate, m_ffn2_w_up, m_ffn2_w_down, m_final_norm, v_ffn1_norm, v_ffn1_w_gate, v_ffn1_w_up, v_ffn1_w_down, v_mix_norm, v_w_in, v_conv_w, v_a_log, v_dt_bias, v_dn_norm, v_sg_ln_g, v_sg_ln_b, v_sg_w, v_sg_b, v_w_out, v_ffn2_norm, v_ffn2_w_gate, v_ffn2_w_up, v_ffn2_w_down, v_final_norm):
    given = dict(x=x, ffn1_norm=ffn1_norm, ffn1_w_gate=ffn1_w_gate, ffn1_w_up=ffn1_w_up, ffn1_w_down=ffn1_w_down, mix_norm=mix_norm, w_in=w_in, conv_w=conv_w, a_log=a_log, dt_bias=dt_bias, dn_norm=dn_norm, sg_ln_g=sg_ln_g, sg_ln_b=sg_ln_b, sg_w=sg_w, sg_b=sg_b, w_out=w_out, ffn2_norm=ffn2_norm, ffn2_w_gate=ffn2_w_gate, ffn2_w_up=ffn2_w_up, ffn2_w_down=ffn2_w_down, final_norm=final_norm, loss_target=loss_target, m_ffn1_norm=m_ffn1_norm, m_ffn1_w_gate=m_ffn1_w_gate, m_ffn1_w_up=m_ffn1_w_up, m_ffn1_w_down=m_ffn1_w_down, m_mix_norm=m_mix_norm, m_w_in=m_w_in, m_conv_w=m_conv_w, m_a_log=m_a_log, m_dt_bias=m_dt_bias, m_dn_norm=m_dn_norm, m_sg_ln_g=m_sg_ln_g, m_sg_ln_b=m_sg_ln_b, m_sg_w=m_sg_w, m_sg_b=m_sg_b, m_w_out=m_w_out, m_ffn2_norm=m_ffn2_norm, m_ffn2_w_gate=m_ffn2_w_gate, m_ffn2_w_up=m_ffn2_w_up, m_ffn2_w_down=m_ffn2_w_down, m_final_norm=m_final_norm, v_ffn1_norm=v_ffn1_norm, v_ffn1_w_gate=v_ffn1_w_gate, v_ffn1_w_up=v_ffn1_w_up, v_ffn1_w_down=v_ffn1_w_down, v_mix_norm=v_mix_norm, v_w_in=v_w_in, v_conv_w=v_conv_w, v_a_log=v_a_log, v_dt_bias=v_dt_bias, v_dn_norm=v_dn_norm, v_sg_ln_g=v_sg_ln_g, v_sg_ln_b=v_sg_ln_b, v_sg_w=v_sg_w, v_sg_b=v_sg_b, v_w_out=v_w_out, v_ffn2_norm=v_ffn2_norm, v_ffn2_w_gate=v_ffn2_w_gate, v_ffn2_w_up=v_ffn2_w_up, v_ffn2_w_down=v_ffn2_w_down, v_final_norm=v_final_norm)
    weights = {n: given[n] for n in TWIN_WEIGHTS}
    shared = {n: given[n] for n in SHARED_INPUTS}
    per_example = {n: given[n] for n in ['x']}
    grad_fn = _jax.value_and_grad(_loss, argnums=(0, 1))

    def one_microbatch(ex, loss_target):
        ex = dict(ex)
        diff = ex.pop(TWIN_DIFF_INPUT)
        return grad_fn(weights, diff, {**shared, **ex}, loss_target)

    if N_MICROBATCH == 1:
        loss, (grad_w, grad_x) = one_microbatch(per_example, given["loss_target"])
    else:
        def body(carry, xs):
            loss_sum, grad_sum = carry
            l_k, (gw_k, gx_k) = one_microbatch(xs[0], xs[1])
            with _jax.named_scope("update"):
                return (loss_sum + l_k, _jax.tree.map(_jnp.add, grad_sum, gw_k)), gx_k

        init = (_jnp.zeros((), _jnp.float32), _jax.tree.map(_jnp.zeros_like, weights))
        (loss, grad_w), grad_x = _jax.lax.scan(body, init, (per_example, given["loss_target"]))
    with _jax.named_scope("update"):
        delta_w, new_m, new_v = {}, {}, {}
        for n in TWIN_WEIGHTS:
            delta_w[n], new_m[n], new_v[n] = _adamw(weights[n], grad_w[n], given["m_" + n], given["v_" + n])
    return (loss, grad_x, *[grad_w[n] for n in TWIN_WEIGHTS], *[delta_w[n] for n in TWIN_WEIGHTS],
            *[new_m[n] for n in TWIN_WEIGHTS], *[new_v[n] for n in TWIN_WEIGHTS])
```

```python
import functools

import jax
import jax.numpy as jnp
from jax import lax
from jax.experimental import pallas as pl
from jax.experimental.pallas import tpu as pltpu

F32 = jnp.float32
BF16 = jnp.bfloat16

D_MODEL = 1024
D_FF = 2816
SG_WIDTH = 512
SG_GROUPS = 8
SG_GROUP_DIM = 64
SG_CHUNK = 128
DN_WIDTH = 512
DN_HEAD_DIM = 128
DN_HEADS = 4
DN_CHUNK = 64
CONV_K = 4
EPS = 1e-6
N_DEV = 8
LANES = 128
HALO = 8

ADAM_LR = 0.001
ADAM_B1 = 0.9
ADAM_B2 = 0.999
ADAM_EPS = 1e-08
ADAM_WD = 0.01
ADAM_STEP = 10

VMEM_LIMIT = 60 * 1024 * 1024
TOKEN_BLOCK = 512
FF_BLOCK_FWD = 1408
FF_BLOCK_BWD = 256

_HI = lax.Precision.HIGHEST


def _cparams(sem):
    return pltpu.CompilerParams(dimension_semantics=sem, vmem_limit_bytes=VMEM_LIMIT)


def _tm(t, pref=TOKEN_BLOCK):
    return min(pref, t)


def _dg(a, b, ca, cb, exact):
    if exact:
        return lax.dot_general(a, b, (((ca,), (cb,)), ((), ())), precision=_HI, preferred_element_type=F32)
    return lax.dot_general(a.astype(BF16), b.astype(BF16), (((ca,), (cb,)), ((), ())), preferred_element_type=F32)


def _make_mm(exact):
    @jax.custom_vjp
    def mm(a, b):
        return _dg(a, b, 1, 0, exact)

    @jax.custom_vjp
    def mm_nt(a, b):
        return _dg(a, b, 1, 1, exact)

    @jax.custom_vjp
    def mm_tn(a, b):
        return _dg(a, b, 0, 0, exact)

    mm.defvjp(lambda a, b: (mm(a, b), (a, b)), lambda r, g: (mm_nt(g, r[1]), mm_tn(r[0], g)))
    mm_nt.defvjp(lambda a, b: (mm_nt(a, b), (a, b)), lambda r, g: (mm(g, r[1]), mm_tn(g, r[0])))
    mm_tn.defvjp(lambda a, b: (mm_tn(a, b), (a, b)), lambda r, g: (mm_nt(r[1], g), mm(r[0], g)))
    return mm, mm_nt, mm_tn


mm, mm_nt, mm_tn = _make_mm(False)
mmx, mmx_nt, mmx_tn = _make_mm(True)


def _sigmoid(x):
    return 1.0 / (1.0 + jnp.exp(-x))


def _silu(x):
    return x * _sigmoid(x)


def _softplus(x):
    neg_abs = jnp.where(x > 0, -x, x)
    return jnp.where(x > 0, x, 0.0) + jnp.log(1.0 + jnp.exp(neg_abs))


def _gelu(x):
    return 0.5 * x * (1.0 + jnp.tanh(0.7978845608028654 * (x + 0.044715 * (x * x * x))))


def _rms_fwd(x, g):
    r = lax.rsqrt(jnp.mean(x * x, axis=-1, keepdims=True) + EPS)
    xh = x * r
    return xh * g, xh, r


def _rms_bwd(dh, xh, r, g):
    dxh = dh * g
    dx = r * (dxh - xh * jnp.mean(dxh * xh, axis=-1, keepdims=True))
    return dx, jnp.sum(dh * xh, axis=0, keepdims=True)


def _acc_out(ref, first, val):
    @pl.when(first)
    def _():
        ref[...] = val

    @pl.when(jnp.logical_not(first))
    def _():
        ref[...] += val


def _ffn_fwd(x, nw, wg, wu, wd, tgt=None, fnw=None, *, name):
    t = x.shape[0]
    tm, fb = _tm(t), FF_BLOCK_FWD
    n_t, n_f = t // tm, D_FF // fb
    with_loss = tgt is not None

    def body(*refs):
        if with_loss:
            x_ref, nw_ref, wg_ref, wu_ref, wd_ref, tgt_ref, fnw_ref, dy_ref, loss_ref, dfn_ref, h_s, acc_s = refs
        else:
            x_ref, nw_ref, wg_ref, wu_ref, wd_ref, y_ref, h_s, acc_s = refs
        i, j = pl.program_id(0), pl.program_id(1)

        @pl.when(j == 0)
        def _():
            h, _, _ = _rms_fwd(x_ref[...], nw_ref[...])
            h_s[...] = h.astype(BF16)
            acc_s[...] = jnp.zeros_like(acc_s)

        h = h_s[...]
        g = jnp.dot(h, wg_ref[...], preferred_element_type=F32)
        u = jnp.dot(h, wu_ref[...], preferred_element_type=F32)
        a = _silu(g) * u
        acc_s[...] += jnp.dot(a.astype(BF16), wd_ref[...], preferred_element_type=F32)

        @pl.when(j == n_f - 1)
        def _():
            y = x_ref[...] + 0.5 * acc_s[...]
            if not with_loss:
                y_ref[...] = y
            else:
                gf = fnw_ref[...]
                out, xh, r = _rms_fwd(y, gf)
                err = out - tgt_ref[...]
                part = 0.5 * jnp.sum(jnp.mean(err * err, axis=-1, keepdims=True), axis=0, keepdims=True)
                d_out = err * (1.0 / D_MODEL)
                dy, dgf = _rms_bwd(d_out, xh, r, gf)
                dy_ref[...] = dy
                _acc_out(loss_ref, i == 0, jnp.broadcast_to(part, loss_ref.shape))
                _acc_out(dfn_ref, i == 0, dgf)

    row = lambda i, j: (i, 0)
    const = lambda i, j: (0, 0)
    in_specs = [
        pl.BlockSpec((tm, D_MODEL), row),
        pl.BlockSpec((1, D_MODEL), const),
        pl.BlockSpec((D_MODEL, fb), lambda i, j: (0, j)),
        pl.BlockSpec((D_MODEL, fb), lambda i, j: (0, j)),
        pl.BlockSpec((fb, D_MODEL), lambda i, j: (j, 0)),
    ]
    args = [x, nw, wg, wu, wd]
    if with_loss:
        in_specs += [pl.BlockSpec((tm, D_MODEL), row), pl.BlockSpec((1, D_MODEL), const)]
        args += [tgt, fnw]
        out_shape = (jax.ShapeDtypeStruct((t, D_MODEL), F32), jax.ShapeDtypeStruct((8, LANES), F32),
                     jax.ShapeDtypeStruct((1, D_MODEL), F32))
        out_specs = (pl.BlockSpec((tm, D_MODEL), row), pl.BlockSpec((8, LANES), const), pl.BlockSpec((1, D_MODEL), const))
        sem = ("arbitrary", "arbitrary")
    else:
        out_shape = jax.ShapeDtypeStruct((t, D_MODEL), F32)
        out_specs = pl.BlockSpec((tm, D_MODEL), row)
        sem = ("parallel", "arbitrary")
    return pl.pallas_call(
        body, name=name, grid=(n_t, n_f), in_specs=in_specs, out_specs=out_specs, out_shape=out_shape,
        scratch_shapes=[pltpu.VMEM((tm, D_MODEL), BF16), pltpu.VMEM((tm, D_MODEL), F32)],
        compiler_params=_cparams(sem),
    )(*args)


def _ffn_bwd(x, nw, wg, wu, wd, dy, *, name):
    t = x.shape[0]
    tm, fb = _tm(t), FF_BLOCK_BWD
    n_t, n_f = t // tm, D_FF // fb

    def body(x_ref, nw_ref, wg_ref, wu_ref, wd_ref, dy_ref, dx_ref, dnw_ref, dwg_hbm, dwu_hbm, dwd_hbm,
             h_s, r_s, dyh_s, ag_s, au_s, ad_s, sem):
        i, j = pl.program_id(0), pl.program_id(1)

        @pl.when(j == 0)
        def _():
            h, _, r = _rms_fwd(x_ref[...], nw_ref[...])
            h_s[...] = h.astype(BF16)
            r_s[...] = r
            dyh_s[...] = (0.5 * dy_ref[...]).astype(BF16)
            dx_ref[...] = jnp.zeros_like(dx_ref)

        h = h_s[...]
        dyh = dyh_s[...]
        wg_j, wu_j, wd_j = wg_ref[...], wu_ref[...], wd_ref[...]
        g = jnp.dot(h, wg_j, preferred_element_type=F32)
        u = jnp.dot(h, wu_j, preferred_element_type=F32)
        s = _sigmoid(g)
        gs = g * s
        da = lax.dot_general(dyh, wd_j, (((1,), (1,)), ((), ())), preferred_element_type=F32)
        dg = (da * u * (s + gs * (1.0 - s))).astype(BF16)
        du = (da * gs).astype(BF16)
        a = (gs * u).astype(BF16)
        tn = (((0,), (0,)), ((), ()))
        c_d = lax.dot_general(a, dyh, tn, preferred_element_type=F32)
        c_g = lax.dot_general(h, dg, tn, preferred_element_type=F32)
        c_u = lax.dot_general(h, du, tn, preferred_element_type=F32)

        @pl.when(i == 0)
        def _():
            ad_s[j] = c_d
            ag_s[j] = c_g
            au_s[j] = c_u

        @pl.when(i > 0)
        def _():
            ad_s[j] += c_d
            ag_s[j] += c_g
            au_s[j] += c_u

        nt = (((1,), (1,)), ((), ()))
        dx_ref[...] += (lax.dot_general(dg, wg_j, nt, preferred_element_type=F32)
                        + lax.dot_general(du, wu_j, nt, preferred_element_type=F32))

        @pl.when(j == n_f - 1)
        def _():
            r = r_s[...]
            dx, dnw = _rms_bwd(dx_ref[...], x_ref[...] * r, r, nw_ref[...])
            dx_ref[...] = dy_ref[...] + dx
            _acc_out(dnw_ref, i == 0, dnw)

        @pl.when(jnp.logical_and(i == n_t - 1, j == n_f - 1))
        def _():
            copies = [pltpu.make_async_copy(ag_s, dwg_hbm, sem.at[0]), pltpu.make_async_copy(au_s, dwu_hbm, sem.at[1]),
                      pltpu.make_async_copy(ad_s, dwd_hbm, sem.at[2])]
            for cp in copies:
                cp.start()
            for cp in copies:
                cp.wait()

    row = lambda i, j: (i, 0)
    const = lambda i, j: (0, 0)
    hbm = pl.BlockSpec(memory_space=pl.ANY)
    return pl.pallas_call(
        body, name=name, grid=(n_t, n_f),
        in_specs=[
            pl.BlockSpec((tm, D_MODEL), row),
            pl.BlockSpec((1, D_MODEL), const),
            pl.BlockSpec((D_MODEL, fb), lambda i, j: (0, j)),
            pl.BlockSpec((D_MODEL, fb), lambda i, j: (0, j)),
            pl.BlockSpec((fb, D_MODEL), lambda i, j: (j, 0)),
            pl.BlockSpec((tm, D_MODEL), row),
        ],
        out_specs=(pl.BlockSpec((tm, D_MODEL), row), pl.BlockSpec((1, D_MODEL), const), hbm, hbm, hbm),
        out_shape=(
            jax.ShapeDtypeStruct((t, D_MODEL), F32), jax.ShapeDtypeStruct((1, D_MODEL), F32),
            jax.ShapeDtypeStruct((n_f, D_MODEL, fb), F32), jax.ShapeDtypeStruct((n_f, D_MODEL, fb), F32),
            jax.ShapeDtypeStruct((n_f, fb, D_MODEL), F32),
        ),
        scratch_shapes=[
            pltpu.VMEM((tm, D_MODEL), BF16), pltpu.VMEM((tm, 1), F32), pltpu.VMEM((tm, D_MODEL), BF16),
            pltpu.VMEM((n_f, D_MODEL, fb), F32), pltpu.VMEM((n_f, D_MODEL, fb), F32), pltpu.VMEM((n_f, fb, D_MODEL), F32),
            pltpu.SemaphoreType.DMA((3,)),
        ],
        compiler_params=_cparams(("arbitrary", "arbitrary")),
    )(x, nw, wg, wu, wd, dy)


_PROJ_WIDTHS = (SG_WIDTH, SG_WIDTH, 3 * DN_WIDTH, DN_WIDTH, LANES, LANES)


def _mix_in_fwd(x, nw, ws, *, name):
    t = x.shape[0]
    tm = _tm(t)

    def body(x_ref, nw_ref, *refs):
        w_refs, o_refs = refs[:6], refs[6:]
        h, _, _ = _rms_fwd(x_ref[...], nw_ref[...])
        h = h.astype(BF16)
        for w_ref, o_ref in zip(w_refs, o_refs):
            o_ref[...] = jnp.dot(h, w_ref[...], preferred_element_type=F32)

    row = lambda i: (i, 0)
    const = lambda i: (0, 0)
    return pl.pallas_call(
        body, name=name, grid=(t // tm,),
        in_specs=[pl.BlockSpec((tm, D_MODEL), row), pl.BlockSpec((1, D_MODEL), const)]
        + [pl.BlockSpec((D_MODEL, n), const) for n in _PROJ_WIDTHS],
        out_specs=tuple(pl.BlockSpec((tm, n), row) for n in _PROJ_WIDTHS),
        out_shape=tuple(jax.ShapeDtypeStruct((t, n), F32) for n in _PROJ_WIDTHS),
        compiler_params=_cparams(("parallel",)),
    )(x, nw, *ws)


def _mix_in_bwd(x, nw, ws, dres, dps, *, name):
    t = x.shape[0]
    tm = _tm(t, 256)

    def body(x_ref, nw_ref, dres_ref, *refs):
        w_refs, dp_refs, dx_ref, dnw_ref, dw_refs = refs[:6], refs[6:12], refs[12], refs[13], refs[14:]
        i = pl.program_id(0)
        hf, xh, r = _rms_fwd(x_ref[...], nw_ref[...])
        h = hf.astype(BF16)
        dh = jnp.zeros((tm, D_MODEL), F32)
        for w_ref, dp_ref, dw_ref in zip(w_refs, dp_refs, dw_refs):
            dp = dp_ref[...].astype(BF16)
            dh = dh + lax.dot_general(dp, w_ref[...], (((1,), (1,)), ((), ())), preferred_element_type=F32)
            _acc_out(dw_ref, i == 0, lax.dot_general(h, dp, (((0,), (0,)), ((), ())), preferred_element_type=F32))
        dx, dnw = _rms_bwd(dh, xh, r, nw_ref[...])
        dx_ref[...] = dres_ref[...] + dx
        _acc_out(dnw_ref, i == 0, dnw)

    row = lambda i: (i, 0)
    const = lambda i: (0, 0)
    return pl.pallas_call(
        body, name=name, grid=(t // tm,),
        in_specs=[pl.BlockSpec((tm, D_MODEL), row), pl.BlockSpec((1, D_MODEL), const), pl.BlockSpec((tm, D_MODEL), row)]
        + [pl.BlockSpec((D_MODEL, n), const) for n in _PROJ_WIDTHS]
        + [pl.BlockSpec((tm, n), row) for n in _PROJ_WIDTHS],
        out_specs=(pl.BlockSpec((tm, D_MODEL), row), pl.BlockSpec((1, D_MODEL), const))
        + tuple(pl.BlockSpec((D_MODEL, n), const) for n in _PROJ_WIDTHS),
        out_shape=(jax.ShapeDtypeStruct((t, D_MODEL), F32), jax.ShapeDtypeStruct((1, D_MODEL), F32))
        + tuple(jax.ShapeDtypeStruct((D_MODEL, n), F32) for n in _PROJ_WIDTHS),
        compiler_params=_cparams(("arbitrary",)),
    )(x, nw, dres, *ws, *dps)


def _sg_fn(u, v, lng, lnb, wcs, sgbt):
    lane = lax.broadcasted_iota(jnp.int32, (1, SG_WIDTH), 1)
    lane_b = lax.broadcasted_iota(jnp.int32, (1, LANES), 1)
    rr = lax.broadcasted_iota(jnp.int32, (SG_CHUNK, SG_CHUNK), 0)
    cc = lax.broadcasted_iota(jnp.int32, (SG_CHUNK, SG_CHUNK), 1)
    gu, gv = _gelu(u), _gelu(v)
    mu = jnp.mean(gv, axis=-1, keepdims=True)
    cen = gv - mu
    var = jnp.mean(cen * cen, axis=-1, keepdims=True)
    ln = cen * lax.rsqrt(var + EPS) * lng + lnb
    vs = jnp.zeros_like(u)
    for g in range(SG_GROUPS):
        in_group = jnp.logical_and(lane >= g * SG_GROUP_DIM, lane < (g + 1) * SG_GROUP_DIM)
        w_causal = jnp.where(rr >= cc, wcs[g], 0.0)
        bias = jnp.sum(jnp.where(lane_b == g, sgbt, 0.0), axis=1, keepdims=True)
        vs = vs + jnp.where(in_group, mm(w_causal, ln) + bias, 0.0)
    return gu * vs


def _sg_fwd(u, v, lng, lnb, wc, sgbt, *, name):
    t = u.shape[0]
    tm = _tm(t)

    def body(u_ref, v_ref, lng_ref, lnb_ref, wc_ref, sgbt_ref, o_ref):
        wcs = [wc_ref[g] for g in range(SG_GROUPS)]
        for c in range(tm // SG_CHUNK):
            rows = pl.ds(c * SG_CHUNK, SG_CHUNK)
            o_ref[rows, :] = _sg_fn(u_ref[rows, :], v_ref[rows, :], lng_ref[...], lnb_ref[...], wcs, sgbt_ref[...])

    row = lambda i: (i, 0)
    const = lambda i: (0, 0)
    return pl.pallas_call(
        body, name=name, grid=(t // tm,),
        in_specs=[pl.BlockSpec((tm, SG_WIDTH), row), pl.BlockSpec((tm, SG_WIDTH), row),
                  pl.BlockSpec((1, SG_WIDTH), const), pl.BlockSpec((1, SG_WIDTH), const),
                  pl.BlockSpec((SG_GROUPS, SG_CHUNK, SG_CHUNK), lambda i: (0, 0, 0)), pl.BlockSpec((SG_CHUNK, LANES), const)],
        out_specs=pl.BlockSpec((tm, SG_WIDTH), row),
        out_shape=jax.ShapeDtypeStruct((t, SG_WIDTH), F32),
        compiler_params=_cparams(("parallel",)),
    )(u, v, lng, lnb, wc, sgbt)


def _sg_bwd(u, v, lng, lnb, wc, sgbt, dout, *, name):
    t = u.shape[0]
    tm = _tm(t)

    def body(u_ref, v_ref, lng_ref, lnb_ref, wc_ref, sgbt_ref, do_ref, du_ref, dv_ref, dlng_ref, dlnb_ref, dwc_ref, dsgbt_ref):
        i = pl.program_id(0)
        wcs = [wc_ref[g] for g in range(SG_GROUPS)]
        tot = None
        for c in range(tm // SG_CHUNK):
            rows = pl.ds(c * SG_CHUNK, SG_CHUNK)
            _, vjp = jax.vjp(_sg_fn, u_ref[rows, :], v_ref[rows, :], lng_ref[...], lnb_ref[...], wcs, sgbt_ref[...])
            du, dv, dlng, dlnb, dwcs, dsgbt = vjp(do_ref[rows, :])
            du_ref[rows, :] = du
            dv_ref[rows, :] = dv
            part = (dlng, dlnb, dwcs, dsgbt)
            tot = part if tot is None else jax.tree.map(jnp.add, tot, part)
        dlng, dlnb, dwcs, dsgbt = tot
        _acc_out(dlng_ref, i == 0, dlng)
        _acc_out(dlnb_ref, i == 0, dlnb)
        _acc_out(dsgbt_ref, i == 0, dsgbt)
        for g in range(SG_GROUPS):
            @pl.when(i == 0)
            def _(g=g):
                dwc_ref[g] = dwcs[g]

            @pl.when(i > 0)
            def _(g=g):
                dwc_ref[g] += dwcs[g]

    row = lambda i: (i, 0)
    const = lambda i: (0, 0)
    wspec = pl.BlockSpec((SG_GROUPS, SG_CHUNK, SG_CHUNK), lambda i: (0, 0, 0))
    return pl.pallas_call(
        body, name=name, grid=(t // tm,),
        in_specs=[pl.BlockSpec((tm, SG_WIDTH), row), pl.BlockSpec((tm, SG_WIDTH), row),
                  pl.BlockSpec((1, SG_WIDTH), const), pl.BlockSpec((1, SG_WIDTH), const), wspec,
                  pl.BlockSpec((SG_CHUNK, LANES), const), pl.BlockSpec((tm, SG_WIDTH), row)],
        out_specs=(pl.BlockSpec((tm, SG_WIDTH), row), pl.BlockSpec((tm, SG_WIDTH), row),
                   pl.BlockSpec((1, SG_WIDTH), const), pl.BlockSpec((1, SG_WIDTH), const), wspec,
                   pl.BlockSpec((SG_CHUNK, LANES), const)),
        out_shape=(jax.ShapeDtypeStruct((t, SG_WIDTH), F32), jax.ShapeDtypeStruct((t, SG_WIDTH), F32),
                   jax.ShapeDtypeStruct((1, SG_WIDTH), F32), jax.ShapeDtypeStruct((1, SG_WIDTH), F32),
                   jax.ShapeDtypeStruct((SG_GROUPS, SG_CHUNK, SG_CHUNK), F32), jax.ShapeDtypeStruct((SG_CHUNK, LANES), F32)),
        compiler_params=_cparams(("arbitrary",)),
    )(u, v, lng, lnb, wc, sgbt, dout)


def _conv_taps(ext, w, tm):
    y = None
    for j in range(CONV_K):
        s = CONV_K - 1 - j
        shifted = ext if s == 0 else pltpu.roll(ext, s, 0)
        term = w[j:j + 1, :] * shifted[HALO:HALO + tm, :]
        y = term if y is None else y + term
    return y


def _post_conv(yq, yk, yv, bpre, apre, alog, dtb):
    def l2(a):
        return a * lax.rsqrt(jnp.sum(a * a, axis=-1, keepdims=True) + EPS)

    q = [l2(_silu(a)) for a in yq]
    k = [l2(_silu(a)) for a in yk]
    return q, k, _silu(yv), _sigmoid(bpre), -jnp.exp(alog) * _softplus(apre + dtb)


def _chunk_tril(tm):
    rr = lax.broadcasted_iota(jnp.int32, (tm, tm), 0)
    cc = lax.broadcasted_iota(jnp.int32, (tm, tm), 1)
    shift = DN_CHUNK.bit_length() - 1
    same = jnp.right_shift(rr, shift) == jnp.right_shift(cc, shift)
    return jnp.where(jnp.logical_and(same, rr >= cc), 1.0, 0.0).astype(F32)


def _halo_specs(tm, width, n_blocks_seq, n_blocks):
    per = tm // HALO
    prev = pl.BlockSpec((HALO, width), lambda i: (jnp.maximum(i * per - 1, 0), 0))
    nxt = pl.BlockSpec((HALO, width), lambda i: (jnp.minimum((i + 1) * per, n_blocks * per - 1), 0))
    return prev, nxt


def _split_heads(ref, base):
    return [ref[:, base + h * DN_HEAD_DIM: base + (h + 1) * DN_HEAD_DIM] for h in range(DN_HEADS)]


def _dn_prep_fwd(qkv, bpre, apre, conv_w, alog, dtb, seq, *, name):
    t = qkv.shape[0]
    tm = _tm(t)
    bps = seq // tm
    cw = 3 * DN_WIDTH

    def body(x_ref, halo_ref, b_ref, a_ref, w_ref, alog_ref, dtb_ref, q_ref, k_ref, v_ref, beta_ref, gc_ref):
        i = pl.program_id(0)
        keep = jnp.where(i % bps == 0, 0.0, 1.0)
        ext = jnp.concatenate([halo_ref[...] * keep, x_ref[...]], axis=0)
        y = _conv_taps(ext, w_ref[...], tm)
        yq = [y[:, h * DN_HEAD_DIM:(h + 1) * DN_HEAD_DIM] for h in range(DN_HEADS)]
        yk = [y[:, DN_WIDTH + h * DN_HEAD_DIM: DN_WIDTH + (h + 1) * DN_HEAD_DIM] for h in range(DN_HEADS)]
        q, k, v, beta, g = _post_conv(yq, yk, y[:, 2 * DN_WIDTH:], b_ref[...], a_ref[...], alog_ref[...], dtb_ref[...])
        for h in range(DN_HEADS):
            q_ref[:, h * DN_HEAD_DIM:(h + 1) * DN_HEAD_DIM] = q[h]
            k_ref[:, h * DN_HEAD_DIM:(h + 1) * DN_HEAD_DIM] = k[h]
        v_ref[...] = v
        beta_ref[...] = beta
        gc_ref[...] = mmx(_chunk_tril(tm), g)

    row = lambda i: (i, 0)
    const = lambda i: (0, 0)
    prev, _ = _halo_specs(tm, cw, bps, t // tm)
    return pl.pallas_call(
        body, name=name, grid=(t // tm,),
        in_specs=[pl.BlockSpec((tm, cw), row), prev, pl.BlockSpec((tm, LANES), row), pl.BlockSpec((tm, LANES), row),
                  pl.BlockSpec((CONV_K, cw), const), pl.BlockSpec((1, LANES), const), pl.BlockSpec((1, LANES), const)],
        out_specs=tuple(pl.BlockSpec((tm, n), row) for n in (DN_WIDTH, DN_WIDTH, DN_WIDTH, LANES, LANES)),
        out_shape=tuple(jax.ShapeDtypeStruct((t, n), F32) for n in (DN_WIDTH, DN_WIDTH, DN_WIDTH, LANES, LANES)),
        compiler_params=_cparams(("parallel",)),
    )(qkv, qkv, bpre, apre, conv_w, alog, dtb)


def _dn_prep_bwd(qkv, bpre, apre, conv_w, alog, dtb, dq, dk, dv, dbeta, dgc, dgc2, seq, *, name):
    t = qkv.shape[0]
    tm = _tm(t)
    bps = seq // tm
    cw = 3 * DN_WIDTH

    def body(x_ref, halo_ref, b_ref, a_ref, w_ref, alog_ref, dtb_ref, dq_ref, dk_ref, dv_ref, dbeta_ref, dgc_ref, dgc2_ref,
             dy_ref, db_ref, da_ref, dalog_ref, ddtb_ref):
        i = pl.program_id(0)
        keep = jnp.where(i % bps == 0, 0.0, 1.0)
        ext = jnp.concatenate([halo_ref[...] * keep, x_ref[...]], axis=0)
        y = _conv_taps(ext, w_ref[...], tm)
        yq = [y[:, h * DN_HEAD_DIM:(h + 1) * DN_HEAD_DIM] for h in range(DN_HEADS)]
        yk = [y[:, DN_WIDTH + h * DN_HEAD_DIM: DN_WIDTH + (h + 1) * DN_HEAD_DIM] for h in range(DN_HEADS)]
        _, vjp = jax.vjp(_post_conv, yq, yk, y[:, 2 * DN_WIDTH:], b_ref[...], a_ref[...], alog_ref[...], dtb_ref[...])
        dg = mmx_tn(_chunk_tril(tm), dgc_ref[...] + dgc2_ref[...])
        dyq, dyk, dyv, db, da, dalog, ddtb = vjp((_split_heads(dq_ref, 0), _split_heads(dk_ref, 0), dv_ref[...],
                                                  dbeta_ref[...], dg))
        for h in range(DN_HEADS):
            dy_ref[:, h * DN_HEAD_DIM:(h + 1) * DN_HEAD_DIM] = dyq[h]
            dy_ref[:, DN_WIDTH + h * DN_HEAD_DIM: DN_WIDTH + (h + 1) * DN_HEAD_DIM] = dyk[h]
        dy_ref[:, 2 * DN_WIDTH:] = dyv
        db_ref[...] = db
        da_ref[...] = da
        _acc_out(dalog_ref, i == 0, dalog)
        _acc_out(ddtb_ref, i == 0, ddtb)

    row = lambda i: (i, 0)
    const = lambda i: (0, 0)
    prev, _ = _halo_specs(tm, cw, bps, t // tm)
    return pl.pallas_call(
        body, name=name, grid=(t // tm,),
        in_specs=[pl.BlockSpec((tm, cw), row), prev, pl.BlockSpec((tm, LANES), row), pl.BlockSpec((tm, LANES), row),
                  pl.BlockSpec((CONV_K, cw), const), pl.BlockSpec((1, LANES), const), pl.BlockSpec((1, LANES), const),
                  pl.BlockSpec((tm, DN_WIDTH), row), pl.BlockSpec((tm, DN_WIDTH), row), pl.BlockSpec((tm, DN_WIDTH), row),
                  pl.BlockSpec((tm, LANES), row), pl.BlockSpec((tm, LANES), row), pl.BlockSpec((tm, LANES), row)],
        out_specs=(pl.BlockSpec((tm, cw), row), pl.BlockSpec((tm, LANES), row), pl.BlockSpec((tm, LANES), row),
                   pl.BlockSpec((1, LANES), const), pl.BlockSpec((1, LANES), const)),
        out_shape=(jax.ShapeDtypeStruct((t, cw), F32), jax.ShapeDtypeStruct((t, LANES), F32), jax.ShapeDtypeStruct((t, LANES), F32),
                   jax.ShapeDtypeStruct((1, LANES), F32), jax.ShapeDtypeStruct((1, LANES), F32)),
        compiler_params=_cparams(("arbitrary",)),
    )(qkv, qkv, bpre, apre, conv_w, alog, dtb, dq, dk, dv, dbeta, dgc, dgc2)


def _conv_bwd(qkv, dy, conv_w, seq, *, name):
    t = qkv.shape[0]
    tm = _tm(t)
    bps = seq // tm
    cw = 3 * DN_WIDTH
    n_ext = tm + HALO

    def body(x_ref, halo_ref, dy_ref, dyn_ref, w_ref, dx_ref, dw_ref):
        i = pl.program_id(0)
        keep_prev = jnp.where(i % bps == 0, 0.0, 1.0)
        keep_next = jnp.where(i % bps == bps - 1, 0.0, 1.0)
        ext = jnp.concatenate([halo_ref[...] * keep_prev, x_ref[...]], axis=0)
        dy = dy_ref[...]
        dyext = jnp.concatenate([dy, dyn_ref[...] * keep_next], axis=0)
        w = w_ref[...]

        @pl.when(i == 0)
        def _():
            dw_ref[...] = jnp.zeros_like(dw_ref)

        dx = None
        for j in range(CONV_K):
            s = CONV_K - 1 - j
            fut = dyext if s == 0 else pltpu.roll(dyext, n_ext - s, 0)
            term = w[j:j + 1, :] * fut[0:tm, :]
            dx = term if dx is None else dx + term
            past = ext if s == 0 else pltpu.roll(ext, s, 0)
            dw_ref[j:j + 1, :] += jnp.sum(dy * past[HALO:HALO + tm, :], axis=0, keepdims=True)
        dx_ref[...] = dx

    row = lambda i: (i, 0)
    const = lambda i: (0, 0)
    prev, nxt = _halo_specs(tm, cw, bps, t // tm)
    return pl.pallas_call(
        body, name=name, grid=(t // tm,),
        in_specs=[pl.BlockSpec((tm, cw), row), prev, pl.BlockSpec((tm, cw), row), nxt, pl.BlockSpec((CONV_K, cw), const)],
        out_specs=(pl.BlockSpec((tm, cw), row), pl.BlockSpec((HALO, cw), const)),
        out_shape=(jax.ShapeDtypeStruct((t, cw), F32), jax.ShapeDtypeStruct((HALO, cw), F32)),
        compiler_params=_cparams(("arbitrary",)),
    )(qkv, qkv, dy, dy, conv_w)


def _inv_unit_lower(l_mat, eye):
    inv = eye - l_mat
    power = l_mat
    n = 2
    while n < l_mat.shape[0]:
        power = mmx(power, power)
        inv = inv + mmx(inv, power)
        n *= 2
    return inv


def _chunk_fn(q, k, v, gc, gr, b, s):
    c = q.shape[0]
    rr = lax.broadcasted_iota(jnp.int32, (c, c), 0)
    cc = lax.broadcasted_iota(jnp.int32, (c, c), 1)
    incl, strict = rr >= cc, rr > cc
    eye = jnp.where(rr == cc, 1.0, 0.0).astype(F32)
    qs = q * (DN_HEAD_DIM ** -0.5)
    decay = jnp.where(incl, jnp.exp(jnp.where(incl, gc - gr, 0.0)), 0.0)
    kb, vb = k * b, v * b
    l_mat = jnp.where(strict, mmx_nt(kb, k) * decay, 0.0)
    eg = jnp.exp(gc)
    inv = _inv_unit_lower(l_mat, eye)
    u_wy = mmx(inv, vb)
    w_wy = mmx(inv, kb * eg)
    qk = mmx_nt(qs, k) * decay
    is_last = lax.broadcasted_iota(jnp.int32, (c, 1), 0) == c - 1
    g_last = jnp.sum(jnp.where(is_last, gc, 0.0), axis=0, keepdims=True)
    k_dec = k * jnp.exp(g_last - gc)
    v_new = u_wy - mmx(w_wy, s)
    o = mmx(qs * eg, s) + mmx(qk, v_new)
    s_new = s * jnp.exp(g_last) + mmx_tn(k_dec, v_new)
    return o, s_new


def _delta_specs(n_seq, seq):
    nc = seq // DN_CHUNK
    tok = pl.BlockSpec((seq, DN_HEAD_DIM), lambda b, h: (b, h))
    col = pl.BlockSpec((None, None, seq, 1), lambda b, h: (b, h, 0, 0))
    rowv = pl.BlockSpec((None, None, nc, DN_CHUNK), lambda b, h: (b, h, 0, 0))
    state = pl.BlockSpec((None, None, nc, DN_HEAD_DIM, DN_HEAD_DIM), lambda b, h: (b, h, 0, 0, 0))
    return nc, tok, col, rowv, state


def _delta_fwd(q, k, v, gcol, grow, bcol, n_seq, seq, *, name):
    nc, tok, col, rowv, state = _delta_specs(n_seq, seq)

    def body(q_ref, k_ref, v_ref, gc_ref, gr_ref, b_ref, o_ref, s_ref):
        def step(n, s):
            rows = pl.ds(pl.multiple_of(n * DN_CHUNK, DN_CHUNK), DN_CHUNK)
            s_ref[n] = s
            o, s_new = _chunk_fn(q_ref[rows, :], k_ref[rows, :], v_ref[rows, :], gc_ref[rows, :], gr_ref[pl.ds(n, 1), :],
                                 b_ref[rows, :], s)
            o_ref[rows, :] = o
            return s_new

        lax.fori_loop(0, nc, step, jnp.zeros((DN_HEAD_DIM, DN_HEAD_DIM), F32))

    return pl.pallas_call(
        body, name=name, grid=(n_seq, DN_HEADS),
        in_specs=[tok, tok, tok, col, rowv, col],
        out_specs=(tok, state),
        out_shape=(jax.ShapeDtypeStruct((n_seq * seq, DN_WIDTH), F32),
                   jax.ShapeDtypeStruct((n_seq, DN_HEADS, nc, DN_HEAD_DIM, DN_HEAD_DIM), F32)),
        compiler_params=_cparams(("parallel", "parallel")),
    )(q, k, v, gcol, grow, bcol)


def _delta_bwd(q, k, v, gcol, grow, bcol, states, do, n_seq, seq, *, name):
    nc, tok, col, rowv, state = _delta_specs(n_seq, seq)

    def body(q_ref, k_ref, v_ref, gc_ref, gr_ref, b_ref, s_ref, do_ref, dq_ref, dk_ref, dv_ref, dgc_ref, dgr_ref, db_ref):
        def step(m, ds):
            n = nc - 1 - m
            rows = pl.ds(pl.multiple_of(n * DN_CHUNK, DN_CHUNK), DN_CHUNK)
            _, vjp = jax.vjp(_chunk_fn, q_ref[rows, :], k_ref[rows, :], v_ref[rows, :], gc_ref[rows, :],
                             gr_ref[pl.ds(n, 1), :], b_ref[rows, :], s_ref[n])
            dq, dk, dv, dgc, dgr, db, ds_in = vjp((do_ref[rows, :], ds))
            dq_ref[rows, :] = dq
            dk_ref[rows, :] = dk
            dv_ref[rows, :] = dv
            dgc_ref[rows, :] = dgc
            dgr_ref[pl.ds(n, 1), :] = dgr
            db_ref[rows, :] = db
            return ds_in

        lax.fori_loop(0, nc, step, jnp.zeros((DN_HEAD_DIM, DN_HEAD_DIM), F32))

    t = n_seq * seq
    return pl.pallas_call(
        body, name=name, grid=(n_seq, DN_HEADS),
        in_specs=[tok, tok, tok, col, rowv, col, state, tok],
        out_specs=(tok, tok, tok, col, rowv, col),
        out_shape=(jax.ShapeDtypeStruct((t, DN_WIDTH), F32), jax.ShapeDtypeStruct((t, DN_WIDTH), F32),
                   jax.ShapeDtypeStruct((t, DN_WIDTH), F32), jax.ShapeDtypeStruct((n_seq, DN_HEADS, seq, 1), F32),
                   jax.ShapeDtypeStruct((n_seq, DN_HEADS, nc, DN_CHUNK), F32), jax.ShapeDtypeStruct((n_seq, DN_HEADS, seq, 1), F32)),
        compiler_params=_cparams(("parallel", "parallel")),
    )(q, k, v, gcol, grow, bcol, states, do)


def _dn_gate(o, z, dnw):
    return o * lax.rsqrt(jnp.mean(o * o, axis=-1, keepdims=True) + EPS) * dnw * _silu(z)


def _mix_out_fwd(x, sg, o, z, wo_sg, wo_dn, dnw, *, name):
    t = x.shape[0]
    tm = _tm(t)

    def body(x_ref, sg_ref, o_ref, z_ref, wsg_ref, wdn_ref, dnw_ref, y_ref, dn_s):
        for h, (oh, zh) in enumerate(zip(_split_heads(o_ref, 0), _split_heads(z_ref, 0))):
            dn_s[:, h * DN_HEAD_DIM:(h + 1) * DN_HEAD_DIM] = _dn_gate(oh, zh, dnw_ref[...]).astype(BF16)
        y_ref[...] = (x_ref[...] + jnp.dot(sg_ref[...].astype(BF16), wsg_ref[...], preferred_element_type=F32)
                      + jnp.dot(dn_s[...], wdn_ref[...], preferred_element_type=F32))

    row = lambda i: (i, 0)
    const = lambda i: (0, 0)
    half = pl.BlockSpec((tm, DN_WIDTH), row)
    return pl.pallas_call(
        body, name=name, grid=(t // tm,),
        in_specs=[pl.BlockSpec((tm, D_MODEL), row), half, half, half, pl.BlockSpec((SG_WIDTH, D_MODEL), const),
                  pl.BlockSpec((DN_WIDTH, D_MODEL), const), pl.BlockSpec((1, DN_HEAD_DIM), const)],
        out_specs=pl.BlockSpec((tm, D_MODEL), row),
        out_shape=jax.ShapeDtypeStruct((t, D_MODEL), F32),
        scratch_shapes=[pltpu.VMEM((tm, DN_WIDTH), BF16)],
        compiler_params=_cparams(("parallel",)),
    )(x, sg, o, z, wo_sg, wo_dn, dnw)


def _mix_out_bwd(dy, sg, o, z, wo_sg, wo_dn, dnw, *, name):
    t = dy.shape[0]
    tm = _tm(t)

    def body(dy_ref, sg_ref, o_ref, z_ref, wsg_ref, wdn_ref, dnw_ref, dsg_ref, do_ref, dz_ref, dwsg_ref, dwdn_ref, ddnw_ref, dn_s):
        i = pl.program_id(0)
        dyb = dy_ref[...].astype(BF16)
        nt = (((1,), (1,)), ((), ()))
        tn = (((0,), (0,)), ((), ()))
        dsg_ref[...] = lax.dot_general(dyb, wsg_ref[...], nt, preferred_element_type=F32)
        ddn = lax.dot_general(dyb, wdn_ref[...], nt, preferred_element_type=F32)
        ddnw = None
        for h, (oh, zh) in enumerate(zip(_split_heads(o_ref, 0), _split_heads(z_ref, 0))):
            cols = slice(h * DN_HEAD_DIM, (h + 1) * DN_HEAD_DIM)
            out, vjp = jax.vjp(_dn_gate, oh, zh, dnw_ref[...])
            dn_s[:, cols] = out.astype(BF16)
            doh, dzh, dw = vjp(ddn[:, cols])
            do_ref[:, cols] = doh
            dz_ref[:, cols] = dzh
            ddnw = dw if ddnw is None else ddnw + dw
        _acc_out(ddnw_ref, i == 0, ddnw)
        _acc_out(dwsg_ref, i == 0, lax.dot_general(sg_ref[...].astype(BF16), dyb, tn, preferred_element_type=F32))
        _acc_out(dwdn_ref, i == 0, lax.dot_general(dn_s[...], dyb, tn, preferred_element_type=F32))

    row = lambda i: (i, 0)
    const = lambda i: (0, 0)
    half = pl.BlockSpec((tm, DN_WIDTH), row)
    wspec = pl.BlockSpec((DN_WIDTH, D_MODEL), const)
    return pl.pallas_call(
        body, name=name, grid=(t // tm,),
        in_specs=[pl.BlockSpec((tm, D_MODEL), row), half, half, half, wspec, wspec, pl.BlockSpec((1, DN_HEAD_DIM), const)],
        out_specs=(half, half, half, wspec, wspec, pl.BlockSpec((1, DN_HEAD_DIM), const)),
        out_shape=(jax.ShapeDtypeStruct((t, DN_WIDTH), F32),) * 3 + (jax.ShapeDtypeStruct((DN_WIDTH, D_MODEL), F32),) * 2
        + (jax.ShapeDtypeStruct((1, DN_HEAD_DIM), F32),),
        scratch_shapes=[pltpu.VMEM((tm, DN_WIDTH), BF16)],
        compiler_params=_cparams(("arbitrary",)),
    )(dy, sg, o, z, wo_sg, wo_dn, dnw)


def _exchange(arrs, gather, *, name):
    n = len(arrs)
    n_peer = N_DEV - 1

    def body(*refs):
        in_refs, out_refs = refs[:n], refs[n:2 * n]
        send_sems, recv_sems, local_sems = refs[2 * n:]
        x, y, c = lax.axis_index("x"), lax.axis_index("y"), lax.axis_index("c")
        me = 4 * x + 2 * y + c
        copies = []
        for k in range(n):
            own = in_refs[k] if gather else in_refs[k].at[me]
            copies.append(pltpu.make_async_copy(own, out_refs[k].at[me], local_sems.at[k]))
        for r in range(1, N_DEV):
            px = 1 - x if r & 4 else x
            py = 1 - y if r & 2 else y
            pc = 1 - c if r & 1 else c
            peer = 4 * px + 2 * py + pc
            for k in range(n):
                src = in_refs[k] if gather else in_refs[k].at[peer]
                copies.append(pltpu.make_async_remote_copy(
                    src_ref=src, dst_ref=out_refs[k].at[me],
                    send_sem=send_sems.at[k * n_peer + r - 1], recv_sem=recv_sems.at[k * n_peer + r - 1],
                    device_id=(px, py, pc), device_id_type=pl.DeviceIdType.MESH))
        for cp in copies:
            cp.start()
        for cp in copies:
            cp.wait()

    hbm = pl.BlockSpec(memory_space=pl.ANY)
    out_shape = tuple(jax.ShapeDtypeStruct(((N_DEV,) + a.shape) if gather else a.shape, a.dtype) for a in arrs)
    return pl.pallas_call(
        body, name=name, in_specs=[hbm] * n, out_specs=(hbm,) * n, out_shape=out_shape,
        scratch_shapes=[pltpu.SemaphoreType.DMA((n * n_peer,)), pltpu.SemaphoreType.DMA((n * n_peer,)),
                        pltpu.SemaphoreType.DMA((n,))],
    )(*arrs)


def _row_block(rows, limit=256):
    best = rows
    for cand in range(8, limit + 1, 8):
        if rows % cand == 0:
            best = cand
    return best if rows > limit else rows


def _adam(gp, w, m, v, *, name):
    p, rows, cols = gp.shape
    rb = _row_block(rows)

    def body(gp_ref, w_ref, m_ref, v_ref, g_ref, d_ref, m2_ref, v2_ref):
        g = gp_ref[0]
        for s in range(1, p):
            g = g + gp_ref[s]
        m2 = ADAM_B1 * m_ref[...] + (1.0 - ADAM_B1) * g
        v2 = ADAM_B2 * v_ref[...] + (1.0 - ADAM_B2) * (g * g)
        m_hat = m2 / (1.0 - ADAM_B1 ** ADAM_STEP)
        v_hat = v2 / (1.0 - ADAM_B2 ** ADAM_STEP)
        g_ref[...] = g
        d_ref[...] = -ADAM_LR * (m_hat / (jnp.sqrt(v_hat) + ADAM_EPS) + ADAM_WD * w_ref[...])
        m2_ref[...] = m2
        v2_ref[...] = v2

    blk = pl.BlockSpec((rb, cols), lambda i: (i, 0))
    return pl.pallas_call(
        body, name=name, grid=(rows // rb,),
        in_specs=[pl.BlockSpec((p, rb, cols), lambda i: (0, i, 0)), blk, blk, blk],
        out_specs=(blk,) * 4, out_shape=(jax.ShapeDtypeStruct((rows, cols), F32),) * 4,
        compiler_params=_cparams(("parallel",)),
    )(gp, w, m, v)


def _cols_full(g):
    return jnp.transpose(g, (1, 0, 2)).reshape(g.shape[1], N_DEV * g.shape[2])


def _cols_pieces(full):
    r, c = full.shape
    return jnp.transpose(full.reshape(r, N_DEV, c // N_DEV), (1, 0, 2))


def _pad_lanes(a, width=LANES):
    return jnp.pad(a, ((0, 0), (0, width - a.shape[1])))


def _heads_col(a, n_seq, seq):
    return jnp.transpose(a[:, :DN_HEADS].reshape(n_seq, seq, DN_HEADS), (0, 2, 1))[..., None]


def _heads_lanes(a, n_seq, seq):
    return _pad_lanes(jnp.transpose(a, (0, 2, 1)).reshape(n_seq * seq, DN_HEADS))


_SMALL = (("ffn1_norm", D_MODEL), ("mix_norm", D_MODEL), ("ffn2_norm", D_MODEL), ("final_norm", D_MODEL), ("a_log", DN_HEADS),
          ("dt_bias", DN_HEADS), ("dn_norm", DN_HEAD_DIM), ("sg_ln_g", SG_WIDTH), ("sg_ln_b", SG_WIDTH),
          ("sg_w", SG_GROUPS * SG_CHUNK * SG_CHUNK), ("sg_b", SG_GROUPS * SG_CHUNK), ("conv_w", CONV_K * 3 * DN_WIDTH))
_SMALL_ROWS = 1128
_SMALL_SHAPES = {"ffn1_norm": (1, D_MODEL), "mix_norm": (1, D_MODEL), "ffn2_norm": (1, D_MODEL), "final_norm": (D_MODEL,),
                 "a_log": (1, DN_HEADS), "dt_bias": (1, DN_HEADS), "dn_norm": (1, DN_HEAD_DIM), "sg_ln_g": (1, SG_WIDTH),
                 "sg_ln_b": (1, SG_WIDTH), "sg_w": (1, SG_GROUPS, SG_CHUNK, SG_CHUNK), "sg_b": (1, SG_GROUPS, SG_CHUNK)}


def _pack_small(d):
    flat = jnp.concatenate([d[name].reshape(-1) for name, _ in _SMALL])
    return jnp.pad(flat, (0, _SMALL_ROWS * LANES - flat.shape[0])).reshape(_SMALL_ROWS, LANES)


def _unpack_small(a):
    flat, out, at = a.reshape(-1), {}, 0
    for name, size in _SMALL:
        out[name] = flat[at:at + size]
        at += size
    return out


def kernel(x, ffn1_norm, ffn1_w_gate, ffn1_w_up, ffn1_w_down, mix_norm, w_in, conv_w, a_log, dt_bias, dn_norm, sg_ln_g, sg_ln_b, sg_w, sg_b, w_out, ffn2_norm, ffn2_w_gate, ffn2_w_up, ffn2_w_down, final_norm, loss_target, m_ffn1_norm, m_ffn1_w_gate, m_ffn1_w_up, m_ffn1_w_down, m_mix_norm, m_w_in, m_conv_w, m_a_log, m_dt_bias, m_dn_norm, m_sg_ln_g, m_sg_ln_b, m_sg_w, m_sg_b, m_w_out, m_ffn2_norm, m_ffn2_w_gate, m_ffn2_w_up, m_ffn2_w_down, m_final_norm, v_ffn1_norm, v_ffn1_w_gate, v_ffn1_w_up, v_ffn1_w_down, v_mix_norm, v_w_in, v_conv_w, v_a_log, v_dt_bias, v_dn_norm, v_sg_ln_g, v_sg_ln_b, v_sg_w, v_sg_b, v_w_out, v_ffn2_norm, v_ffn2_w_gate, v_ffn2_w_up, v_ffn2_w_down, v_final_norm):
    weights = dict(ffn1_norm=ffn1_norm, ffn1_w_gate=ffn1_w_gate, ffn1_w_up=ffn1_w_up, ffn1_w_down=ffn1_w_down, mix_norm=mix_norm, w_in=w_in, conv_w=conv_w, a_log=a_log, dt_bias=dt_bias, dn_norm=dn_norm, sg_ln_g=sg_ln_g, sg_ln_b=sg_ln_b, sg_w=sg_w, sg_b=sg_b, w_out=w_out, ffn2_norm=ffn2_norm, ffn2_w_gate=ffn2_w_gate, ffn2_w_up=ffn2_w_up, ffn2_w_down=ffn2_w_down, final_norm=final_norm)
    mom_m = dict(ffn1_norm=m_ffn1_norm, ffn1_w_gate=m_ffn1_w_gate, ffn1_w_up=m_ffn1_w_up, ffn1_w_down=m_ffn1_w_down, mix_norm=m_mix_norm, w_in=m_w_in, conv_w=m_conv_w, a_log=m_a_log, dt_bias=m_dt_bias, dn_norm=m_dn_norm, sg_ln_g=m_sg_ln_g, sg_ln_b=m_sg_ln_b, sg_w=m_sg_w, sg_b=m_sg_b, w_out=m_w_out, ffn2_norm=m_ffn2_norm, ffn2_w_gate=m_ffn2_w_gate, ffn2_w_up=m_ffn2_w_up, ffn2_w_down=m_ffn2_w_down, final_norm=m_final_norm)
    mom_v = dict(ffn1_norm=v_ffn1_norm, ffn1_w_gate=v_ffn1_w_gate, ffn1_w_up=v_ffn1_w_up, ffn1_w_down=v_ffn1_w_down, mix_norm=v_mix_norm, w_in=v_w_in, conv_w=v_conv_w, a_log=v_a_log, dt_bias=v_dt_bias, dn_norm=v_dn_norm, sg_ln_g=v_sg_ln_g, sg_ln_b=v_sg_ln_b, sg_w=v_sg_w, sg_b=v_sg_b, w_out=v_w_out, ffn2_norm=v_ffn2_norm, ffn2_w_gate=v_ffn2_w_gate, ffn2_w_up=v_ffn2_w_up, ffn2_w_down=v_ffn2_w_down, final_norm=v_final_norm)
    order = list(weights)
    big = ("ffn1_w_gate", "ffn1_w_up", "ffn1_w_down", "w_in", "w_out", "ffn2_w_gate", "ffn2_w_up", "ffn2_w_down")
    col_sharded = ("ffn1_w_gate", "ffn1_w_up", "w_in", "ffn2_w_gate", "ffn2_w_up")

    n_seq, seq, _ = x.shape
    t = n_seq * seq
    me = 4 * lax.axis_index("x") + 2 * lax.axis_index("y") + lax.axis_index("c")
    x0 = x.reshape(t, D_MODEL)
    tgt = loss_target.reshape(t, D_MODEL)

    gathered = _exchange([weights[n][0].astype(BF16) for n in big] + [conv_w[0]], True, name="gather_weights")
    full = {n: (_cols_full(g) if n in col_sharded else g.reshape(-1, g.shape[-1])) for n, g in zip(big, gathered)}
    conv_full = _cols_full(gathered[-1])
    w_in_f = full["w_in"]
    offs = (0, SG_WIDTH, 2 * SG_WIDTH, 2 * SG_WIDTH + 3 * DN_WIDTH, 2 * SG_WIDTH + 4 * DN_WIDTH)
    n_proj = offs[-1]
    ws = [w_in_f[:, offs[0]:offs[1]], w_in_f[:, offs[1]:offs[2]], w_in_f[:, offs[2]:offs[3]], w_in_f[:, offs[3]:offs[4]],
          _pad_lanes(w_in_f[:, n_proj:n_proj + DN_HEADS]), _pad_lanes(w_in_f[:, n_proj + DN_HEADS:n_proj + 2 * DN_HEADS])]
    wo_sg, wo_dn = full["w_out"][:SG_WIDTH], full["w_out"][SG_WIDTH:]
    alog, dtb = _pad_lanes(a_log), _pad_lanes(dt_bias)
    sgbt = _pad_lanes(sg_b[0].T)
    fnw = final_norm.reshape(1, D_MODEL)

    x1 = _ffn_fwd(x0, ffn1_norm, full["ffn1_w_gate"], full["ffn1_w_up"], full["ffn1_w_down"], name="ffn1_fwd")
    u, v, qkv, z, bpre, apre = _mix_in_fwd(x1, mix_norm, ws, name="mix_in_fwd")
    sg_out = _sg_fwd(u, v, sg_ln_g, sg_ln_b, sg_w[0], sgbt, name="sg_fwd")
    q, k, vv, beta, gc = _dn_prep_fwd(qkv, bpre, apre, conv_full, alog, dtb, seq, name="dn_prep_fwd")
    gcol, bcol = _heads_col(gc, n_seq, seq), _heads_col(beta, n_seq, seq)
    grow = gcol.reshape(n_seq, DN_HEADS, seq // DN_CHUNK, DN_CHUNK)
    o, states = _delta_fwd(q, k, vv, gcol, grow, bcol, n_seq, seq, name="delta_fwd")
    x2 = _mix_out_fwd(x1, sg_out, o, z, wo_sg, wo_dn, dn_norm, name="mix_out_fwd")
    dx3, loss_part, d_fn = _ffn_fwd(x2, ffn2_norm, full["ffn2_w_gate"], full["ffn2_w_up"], full["ffn2_w_down"], tgt, fnw,
                                    name="ffn2_fwd_loss")
    loss = lax.psum(loss_part[0, 0], ("x", "y", "c"))

    dx2, d_n2, d_g2, d_u2, d_d2 = _ffn_bwd(x2, ffn2_norm, full["ffn2_w_gate"], full["ffn2_w_up"], full["ffn2_w_down"], dx3,
                                           name="ffn2_bwd")
    dsg, do, dz, d_wo_sg, d_wo_dn, d_dnw = _mix_out_bwd(dx2, sg_out, o, z, wo_sg, wo_dn, dn_norm, name="mix_out_bwd")
    dq, dk, dv, dgcol, dgrow, dbcol = _delta_bwd(q, k, vv, gcol, grow, bcol, states, do, n_seq, seq, name="delta_bwd")
    dgc_a = _heads_lanes(dgcol[..., 0], n_seq, seq)
    dgc_b = _heads_lanes(dgrow.reshape(n_seq, DN_HEADS, seq), n_seq, seq)
    dbeta = _heads_lanes(dbcol[..., 0], n_seq, seq)
    dy_conv, dbpre, dapre, d_alog, d_dtb = _dn_prep_bwd(qkv, bpre, apre, conv_full, alog, dtb, dq, dk, dv, dbeta, dgc_a, dgc_b,
                                                        seq, name="dn_prep_bwd")
    dqkv, d_conv = _conv_bwd(qkv, dy_conv, conv_full, seq, name="conv_bwd")
    du, dvv, d_lng, d_lnb, d_wc, d_sgbt = _sg_bwd(u, v, sg_ln_g, sg_ln_b, sg_w[0], sgbt, dsg, name="sg_bwd")
    dx1, d_mixn, d_ws = _split3(_mix_in_bwd(x1, mix_norm, ws, dx2, (du, dvv, dqkv, dz, dbpre, dapre), name="mix_in_bwd"))
    grad_x, d_n1, d_g1, d_u1, d_d1 = _ffn_bwd(x0, ffn1_norm, full["ffn1_w_gate"], full["ffn1_w_up"], full["ffn1_w_down"], dx1,
                                              name="ffn1_bwd")

    def ff_cols(acc):
        return _cols_pieces(jnp.transpose(acc, (1, 0, 2)).reshape(D_MODEL, D_FF))

    def ff_rows(acc):
        return acc.reshape(N_DEV, D_FF // N_DEV, D_MODEL)

    d_w_in = jnp.concatenate([d_ws[0], d_ws[1], d_ws[2], d_ws[3], d_ws[4][:, :DN_HEADS], d_ws[5][:, :DN_HEADS]], axis=1)
    d_w_out = jnp.concatenate([d_wo_sg, d_wo_dn], axis=0)
    pieces = dict(ffn1_w_gate=ff_cols(d_g1), ffn1_w_up=ff_cols(d_u1), ffn1_w_down=ff_rows(d_d1), w_in=_cols_pieces(d_w_in),
                  w_out=d_w_out.reshape(N_DEV, D_MODEL // N_DEV, D_MODEL), ffn2_w_gate=ff_cols(d_g2), ffn2_w_up=ff_cols(d_u2),
                  ffn2_w_down=ff_rows(d_d2))
    received = _exchange([pieces[n] for n in big], False, name="scatter_grads")
    res = {}
    for n, gp in zip(big, received):
        res[n] = _adam(gp, weights[n][0], mom_m[n][0], mom_v[n][0], name="adam_" + n)

    small_grads = dict(ffn1_norm=d_n1, mix_norm=d_mixn, ffn2_norm=d_n2, final_norm=d_fn, a_log=d_alog[:, :DN_HEADS],
                       dt_bias=d_dtb[:, :DN_HEADS], dn_norm=d_dnw, sg_ln_g=d_lng, sg_ln_b=d_lnb, sg_w=d_wc,
                       sg_b=d_sgbt[:, :SG_GROUPS].T, conv_w=d_conv[:CONV_K])
    (small_parts,) = _exchange([_pack_small(small_grads)], True, name="gather_small_grads")
    zeros_conv = jnp.zeros((CONV_K * 3 * DN_WIDTH,), F32)
    packed = [_pack_small({**{n: src[n] for n, _ in _SMALL if n != "conv_w"}, "conv_w": zeros_conv})
              for src in (weights, mom_m, mom_v)]
    small_res = [_unpack_small(a) for a in _adam(small_parts, *packed, name="adam_small")]
    conv_grad = lax.dynamic_slice_in_dim(small_res[0]["conv_w"].reshape(CONV_K, 3 * DN_WIDTH), me * (3 * DN_WIDTH // N_DEV),
                                         3 * DN_WIDTH // N_DEV, axis=1)
    res["conv_w"] = _adam(conv_grad[None], conv_w[0], m_conv_w[0], v_conv_w[0], name="adam_conv_w")

    outs = [[], [], [], []]
    for n in order:
        for kind in range(4):
            if n in res:
                outs[kind].append(res[n][kind][None])
            else:
                outs[kind].append(small_res[kind][n].reshape(_SMALL_SHAPES[n]))
    return (loss, grad_x.reshape(x.shape), *outs[0], *outs[1], *outs[2], *outs[3])


def _split3(r):
    return r[0], r[1], r[2:]
```

```python
import functools

import jax
import jax.numpy as jnp
from jax import lax
from jax.experimental import pallas as pl
from jax.experimental.pallas import tpu as pltpu

F32 = jnp.float32
BF16 = jnp.bfloat16

D_MODEL = 1024
D_FF = 2816
SG_WIDTH = 512
SG_GROUPS = 8
SG_GROUP_DIM = 64
SG_CHUNK = 128
DN_WIDTH = 512
DN_HEAD_DIM = 128
DN_HEADS = 4
DN_CHUNK = 64
CONV_K = 4
EPS = 1e-6
N_DEV = 8
LANES = 128
HALO = 8

ADAM_LR = 0.001
ADAM_B1 = 0.9
ADAM_B2 = 0.999
ADAM_EPS = 1e-08
ADAM_WD = 0.01
ADAM_STEP = 10

VMEM_LIMIT = 60 * 1024 * 1024
TOKEN_BLOCK = 512
FF_BLOCK_FWD = 1408
FF_BLOCK_BWD = 256

_HI = lax.Precision.HIGHEST


def _cparams(sem):
    return pltpu.CompilerParams(dimension_semantics=sem, vmem_limit_bytes=VMEM_LIMIT)


def _tm(t, pref=TOKEN_BLOCK):
    return min(pref, t)


def _dg(a, b, ca, cb, exact):
    if exact:
        return lax.dot_general(a, b, (((ca,), (cb,)), ((), ())), precision=_HI, preferred_element_type=F32)
    return lax.dot_general(a.astype(BF16), b.astype(BF16), (((ca,), (cb,)), ((), ())), preferred_element_type=F32)


def _make_mm(exact):
    @jax.custom_vjp
    def mm(a, b):
        return _dg(a, b, 1, 0, exact)

    @jax.custom_vjp
    def mm_nt(a, b):
        return _dg(a, b, 1, 1, exact)

    @jax.custom_vjp
    def mm_tn(a, b):
        return _dg(a, b, 0, 0, exact)

    mm.defvjp(lambda a, b: (mm(a, b), (a, b)), lambda r, g: (mm_nt(g, r[1]), mm_tn(r[0], g)))
    mm_nt.defvjp(lambda a, b: (mm_nt(a, b), (a, b)), lambda r, g: (mm(g, r[1]), mm_tn(g, r[0])))
    mm_tn.defvjp(lambda a, b: (mm_tn(a, b), (a, b)), lambda r, g: (mm_nt(r[1], g), mm(r[0], g)))
    return mm, mm_nt, mm_tn


mm, mm_nt, mm_tn = _make_mm(False)
mmx, mmx_nt, mmx_tn = _make_mm(True)


def _sigmoid(x):
    return 1.0 / (1.0 + jnp.exp(-x))


def _silu(x):
    return x * _sigmoid(x)


def _softplus(x):
    neg_abs = jnp.where(x > 0, -x, x)
    return jnp.where(x > 0, x, 0.0) + jnp.log(1.0 + jnp.exp(neg_abs))


def _gelu(x):
    return 0.5 * x * (1.0 + jnp.tanh(0.7978845608028654 * (x + 0.044715 * (x * x * x))))


def _rms_fwd(x, g):
    r = lax.rsqrt(jnp.mean(x * x, axis=-1, keepdims=True) + EPS)
    xh = x * r
    return xh * g, xh, r


def _rms_bwd(dh, xh, r, g):
    dxh = dh * g
    dx = r * (dxh - xh * jnp.mean(dxh * xh, axis=-1, keepdims=True))
    return dx, jnp.sum(dh * xh, axis=0, keepdims=True)


def _acc_out(ref, first, val):
    @pl.when(first)
    def _():
        ref[...] = val

    @pl.when(jnp.logical_not(first))
    def _():
        ref[...] += val


def _ffn_fwd(x, nw, wg, wu, wd, tgt=None, fnw=None, *, name):
    t = x.shape[0]
    tm, fb = _tm(t), FF_BLOCK_FWD
    n_t, n_f = t // tm, D_FF // fb
    with_loss = tgt is not None

    def body(*refs):
        if with_loss:
            x_ref, nw_ref, wg_ref, wu_ref, wd_ref, tgt_ref, fnw_ref, dy_ref, loss_ref, dfn_ref, h_s, acc_s = refs
        else:
            x_ref, nw_ref, wg_ref, wu_ref, wd_ref, y_ref, h_s, acc_s = refs
        i, j = pl.program_id(0), pl.program_id(1)

        @pl.when(j == 0)
        def _():
            h, _, _ = _rms_fwd(x_ref[...], nw_ref[...])
            h_s[...] = h.astype(BF16)
            acc_s[...] = jnp.zeros_like(acc_s)

        h = h_s[...]
        g = jnp.dot(h, wg_ref[...], preferred_element_type=F32)
        u = jnp.dot(h, wu_ref[...], preferred_element_type=F32)
        a = _silu(g) * u
        acc_s[...] += jnp.dot(a.astype(BF16), wd_ref[...], preferred_element_type=F32)

        @pl.when(j == n_f - 1)
        def _():
            y = x_ref[...] + 0.5 * acc_s[...]
            if not with_loss:
                y_ref[...] = y
            else:
                gf = fnw_ref[...]
                out, xh, r = _rms_fwd(y, gf)
                err = out - tgt_ref[...]
                part = 0.5 * jnp.sum(jnp.mean(err * err, axis=-1, keepdims=True), axis=0, keepdims=True)
                d_out = err * (1.0 / D_MODEL)
                dy, dgf = _rms_bwd(d_out, xh, r, gf)
                dy_ref[...] = dy
                _acc_out(loss_ref, i == 0, jnp.broadcast_to(part, loss_ref.shape))
                _acc_out(dfn_ref, i == 0, dgf)

    row = lambda i, j: (i, 0)
    const = lambda i, j: (0, 0)
    in_specs = [
        pl.BlockSpec((tm, D_MODEL), row),
        pl.BlockSpec((1, D_MODEL), const),
        pl.BlockSpec((D_MODEL, fb), lambda i, j: (0, j)),
        pl.BlockSpec((D_MODEL, fb), lambda i, j: (0, j)),
        pl.BlockSpec((fb, D_MODEL), lambda i, j: (j, 0)),
    ]
    args = [x, nw, wg, wu, wd]
    if with_loss:
        in_specs += [pl.BlockSpec((tm, D_MODEL), row), pl.BlockSpec((1, D_MODEL), const)]
        args += [tgt, fnw]
        out_shape = (jax.ShapeDtypeStruct((t, D_MODEL), F32), jax.ShapeDtypeStruct((8, LANES), F32),
                     jax.ShapeDtypeStruct((1, D_MODEL), F32))
        out_specs = (pl.BlockSpec((tm, D_MODEL), row), pl.BlockSpec((8, LANES), const), pl.BlockSpec((1, D_MODEL), const))
        sem = ("arbitrary", "arbitrary")
    else:
        out_shape = jax.ShapeDtypeStruct((t, D_MODEL), F32)
        out_specs = pl.BlockSpec((tm, D_MODEL), row)
        sem = ("parallel", "arbitrary")
    return pl.pallas_call(
        body, name=name, grid=(n_t, n_f), in_specs=in_specs, out_specs=out_specs, out_shape=out_shape,
        scratch_shapes=[pltpu.VMEM((tm, D_MODEL), BF16), pltpu.VMEM((tm, D_MODEL), F32)],
        compiler_params=_cparams(sem),
    )(*args)


def _ffn_bwd(x, nw, wg, wu, wd, dy, *, name):
    t = x.shape[0]
    tm, fb = _tm(t), FF_BLOCK_BWD
    n_t, n_f = t // tm, D_FF // fb

    def body(x_ref, nw_ref, wg_ref, wu_ref, wd_ref, dy_ref, dx_ref, dnw_ref, dwg_hbm, dwu_hbm, dwd_hbm,
             h_s, r_s, dyh_s, ag_s, au_s, ad_s, sem):
        i, j = pl.program_id(0), pl.program_id(1)

        @pl.when(j == 0)
        def _():
            h, _, r = _rms_fwd(x_ref[...], nw_ref[...])
            h_s[...] = h.astype(BF16)
            r_s[...] = r
            dyh_s[...] = (0.5 * dy_ref[...]).astype(BF16)
            dx_ref[...] = jnp.zeros_like(dx_ref)

        h = h_s[...]
        dyh = dyh_s[...]
        wg_j, wu_j, wd_j = wg_ref[...], wu_ref[...], wd_ref[...]
        g = jnp.dot(h, wg_j, preferred_element_type=F32)
        u = jnp.dot(h, wu_j, preferred_element_type=F32)
        s = _sigmoid(g)
        gs = g * s
        da = lax.dot_general(dyh, wd_j, (((1,), (1,)), ((), ())), preferred_element_type=F32)
        dg = (da * u * (s + gs * (1.0 - s))).astype(BF16)
        du = (da * gs).astype(BF16)
        a = (gs * u).astype(BF16)
        tn = (((0,), (0,)), ((), ()))
        c_d = lax.dot_general(a, dyh, tn, preferred_element_type=F32)
        c_g = lax.dot_general(h, dg, tn, preferred_element_type=F32)
        c_u = lax.dot_general(h, du, tn, preferred_element_type=F32)

        @pl.when(i == 0)
        def _():
            ad_s[j] = c_d
            ag_s[j] = c_g
            au_s[j] = c_u

        @pl.when(i > 0)
        def _():
            ad_s[j] += c_d
            ag_s[j] += c_g
            au_s[j] += c_u

        nt = (((1,), (1,)), ((), ()))
        dx_ref[...] += (lax.dot_general(dg, wg_j, nt, preferred_element_type=F32)
                        + lax.dot_general(du, wu_j, nt, preferred_element_type=F32))

        @pl.when(j == n_f - 1)
        def _():
            r = r_s[...]
            dx, dnw = _rms_bwd(dx_ref[...], x_ref[...] * r, r, nw_ref[...])
            dx_ref[...] = dy_ref[...] + dx
            _acc_out(dnw_ref, i == 0, dnw)

        @pl.when(jnp.logical_and(i == n_t - 1, j == n_f - 1))
        def _():
            copies = [pltpu.make_async_copy(ag_s, dwg_hbm, sem.at[0]), pltpu.make_async_copy(au_s, dwu_hbm, sem.at[1]),
                      pltpu.make_async_copy(ad_s, dwd_hbm, sem.at[2])]
            for cp in copies:
                cp.start()
            for cp in copies:
                cp.wait()

    row = lambda i, j: (i, 0)
    const = lambda i, j: (0, 0)
    hbm = pl.BlockSpec(memory_space=pl.ANY)
    return pl.pallas_call(
        body, name=name, grid=(n_t, n_f),
        in_specs=[
            pl.BlockSpec((tm, D_MODEL), row),
            pl.BlockSpec((1, D_MODEL), const),
            pl.BlockSpec((D_MODEL, fb), lambda i, j: (0, j)),
            pl.BlockSpec((D_MODEL, fb), lambda i, j: (0, j)),
            pl.BlockSpec((fb, D_MODEL), lambda i, j: (j, 0)),
            pl.BlockSpec((tm, D_MODEL), row),
        ],
        out_specs=(pl.BlockSpec((tm, D_MODEL), row), pl.BlockSpec((1, D_MODEL), const), hbm, hbm, hbm),
        out_shape=(
            jax.ShapeDtypeStruct((t, D_MODEL), F32), jax.ShapeDtypeStruct((1, D_MODEL), F32),
            jax.ShapeDtypeStruct((n_f, D_MODEL, fb), F32), jax.ShapeDtypeStruct((n_f, D_MODEL, fb), F32),
            jax.ShapeDtypeStruct((n_f, fb, D_MODEL), F32),
        ),
        scratch_shapes=[
            pltpu.VMEM((tm, D_MODEL), BF16), pltpu.VMEM((tm, 1), F32), pltpu.VMEM((tm, D_MODEL), BF16),
            pltpu.VMEM((n_f, D_MODEL, fb), F32), pltpu.VMEM((n_f, D_MODEL, fb), F32), pltpu.VMEM((n_f, fb, D_MODEL), F32),
            pltpu.SemaphoreType.DMA((3,)),
        ],
        compiler_params=_cparams(("arbitrary", "arbitrary")),
    )(x, nw, wg, wu, wd, dy)


_PROJ_WIDTHS = (SG_WIDTH, SG_WIDTH, 3 * DN_WIDTH, DN_WIDTH, LANES, LANES)


def _mix_in_fwd(x, nw, ws, *, name):
    t = x.shape[0]
    tm = _tm(t)

    def body(x_ref, nw_ref, *refs):
        w_refs, o_refs = refs[:6], refs[6:]
        h, _, _ = _rms_fwd(x_ref[...], nw_ref[...])
        h = h.astype(BF16)
        for w_ref, o_ref in zip(w_refs, o_refs):
            o_ref[...] = jnp.dot(h, w_ref[...], preferred_element_type=F32)

    row = lambda i: (i, 0)
    const = lambda i: (0, 0)
    return pl.pallas_call(
        body, name=name, grid=(t // tm,),
        in_specs=[pl.BlockSpec((tm, D_MODEL), row), pl.BlockSpec((1, D_MODEL), const)]
        + [pl.BlockSpec((D_MODEL, n), const) for n in _PROJ_WIDTHS],
        out_specs=tuple(pl.BlockSpec((tm, n), row) for n in _PROJ_WIDTHS),
        out_shape=tuple(jax.ShapeDtypeStruct((t, n), F32) for n in _PROJ_WIDTHS),
        compiler_params=_cparams(("parallel",)),
    )(x, nw, *ws)


def _mix_in_bwd(x, nw, ws, dres, dps, *, name):
    t = x.shape[0]
    tm = _tm(t, 256)

    def body(x_ref, nw_ref, dres_ref, *refs):
        w_refs, dp_refs, dx_ref, dnw_ref, dw_refs = refs[:6], refs[6:12], refs[12], refs[13], refs[14:]
        i = pl.program_id(0)
        hf, xh, r = _rms_fwd(x_ref[...], nw_ref[...])
        h = hf.astype(BF16)
        dh = jnp.zeros((tm, D_MODEL), F32)
        for w_ref, dp_ref, dw_ref in zip(w_refs, dp_refs, dw_refs):
            dp = dp_ref[...].astype(BF16)
            dh = dh + lax.dot_general(dp, w_ref[...], (((1,), (1,)), ((), ())), preferred_element_type=F32)
            _acc_out(dw_ref, i == 0, lax.dot_general(h, dp, (((0,), (0,)), ((), ())), preferred_element_type=F32))
        dx, dnw = _rms_bwd(dh, xh, r, nw_ref[...])
        dx_ref[...] = dres_ref[...] + dx
        _acc_out(dnw_ref, i == 0, dnw)

    row = lambda i: (i, 0)
    const = lambda i: (0, 0)
    return pl.pallas_call(
        body, name=name, grid=(t // tm,),
        in_specs=[pl.BlockSpec((tm, D_MODEL), row), pl.BlockSpec((1, D_MODEL), const), pl.BlockSpec((tm, D_MODEL), row)]
        + [pl.BlockSpec((D_MODEL, n), const) for n in _PROJ_WIDTHS]
        + [pl.BlockSpec((tm, n), row) for n in _PROJ_WIDTHS],
        out_specs=(pl.BlockSpec((tm, D_MODEL), row), pl.BlockSpec((1, D_MODEL), const))
        + tuple(pl.BlockSpec((D_MODEL, n), const) for n in _PROJ_WIDTHS),
        out_shape=(jax.ShapeDtypeStruct((t, D_MODEL), F32), jax.ShapeDtypeStruct((1, D_MODEL), F32))
        + tuple(jax.ShapeDtypeStruct((D_MODEL, n), F32) for n in _PROJ_WIDTHS),
        compiler_params=_cparams(("arbitrary",)),
    )(x, nw, dres, *ws, *dps)


def _sg_fn(u, v, lng, lnb, wcs, sgbt):
    lane = lax.broadcasted_iota(jnp.int32, (1, SG_WIDTH), 1)
    lane_b = lax.broadcasted_iota(jnp.int32, (1, LANES), 1)
    rr = lax.broadcasted_iota(jnp.int32, (SG_CHUNK, SG_CHUNK), 0)
    cc = lax.broadcasted_iota(jnp.int32, (SG_CHUNK, SG_CHUNK), 1)
    gu, gv = _gelu(u), _gelu(v)
    mu = jnp.mean(gv, axis=-1, keepdims=True)
    cen = gv - mu
    var = jnp.mean(cen * cen, axis=-1, keepdims=True)
    ln = cen * lax.rsqrt(var + EPS) * lng + lnb
    vs = jnp.zeros_like(u)
    for g in range(SG_GROUPS):
        in_group = jnp.logical_and(lane >= g * SG_GROUP_DIM, lane < (g + 1) * SG_GROUP_DIM)
        w_causal = jnp.where(rr >= cc, wcs[g], 0.0)
        bias = jnp.sum(jnp.where(lane_b == g, sgbt, 0.0), axis=1, keepdims=True)
        vs = vs + jnp.where(in_group, mm(w_causal, ln) + bias, 0.0)
    return gu * vs


def _sg_fwd(u, v, lng, lnb, wc, sgbt, *, name):
    t = u.shape[0]
    tm = _tm(t)

    def body(u_ref, v_ref, lng_ref, lnb_ref, wc_ref, sgbt_ref, o_ref):
        wcs = [wc_ref[g] for g in range(SG_GROUPS)]
        for c in range(tm // SG_CHUNK):
            rows = pl.ds(c * SG_CHUNK, SG_CHUNK)
            o_ref[rows, :] = _sg_fn(u_ref[rows, :], v_ref[rows, :], lng_ref[...], lnb_ref[...], wcs, sgbt_ref[...])

    row = lambda i: (i, 0)
    const = lambda i: (0, 0)
    return pl.pallas_call(
        body, name=name, grid=(t // tm,),
        in_specs=[pl.BlockSpec((tm, SG_WIDTH), row), pl.BlockSpec((tm, SG_WIDTH), row),
                  pl.BlockSpec((1, SG_WIDTH), const), pl.BlockSpec((1, SG_WIDTH), const),
                  pl.BlockSpec((SG_GROUPS, SG_CHUNK, SG_CHUNK), lambda i: (0, 0, 0)), pl.BlockSpec((SG_CHUNK, LANES), const)],
        out_specs=pl.BlockSpec((tm, SG_WIDTH), row),
        out_shape=jax.ShapeDtypeStruct((t, SG_WIDTH), F32),
        compiler_params=_cparams(("parallel",)),
    )(u, v, lng, lnb, wc, sgbt)


def _sg_bwd(u, v, lng, lnb, wc, sgbt, dout, *, name):
    t = u.shape[0]
    tm = _tm(t)

    def body(u_ref, v_ref, lng_ref, lnb_ref, wc_ref, sgbt_ref, do_ref, du_ref, dv_ref, dlng_ref, dlnb_ref, dwc_ref, dsgbt_ref):
        i = pl.program_id(0)
        wcs = [wc_ref[g] for g in range(SG_GROUPS)]
        tot = None
        for c in range(tm // SG_CHUNK):
            rows = pl.ds(c * SG_CHUNK, SG_CHUNK)
            _, vjp = jax.vjp(_sg_fn, u_ref[rows, :], v_ref[rows, :], lng_ref[...], lnb_ref[...], wcs, sgbt_ref[...])
            du, dv, dlng, dlnb, dwcs, dsgbt = vjp(do_ref[rows, :])
            du_ref[rows, :] = du
            dv_ref[rows, :] = dv
            part = (dlng, dlnb, dwcs, dsgbt)
            tot = part if tot is None else jax.tree.map(jnp.add, tot, part)
        dlng, dlnb, dwcs, dsgbt = tot
        _acc_out(dlng_ref, i == 0, dlng)
        _acc_out(dlnb_ref, i == 0, dlnb)
        _acc_out(dsgbt_ref, i == 0, dsgbt)
        for g in range(SG_GROUPS):
            @pl.when(i == 0)
            def _(g=g):
                dwc_ref[g] = dwcs[g]

            @pl.when(i > 0)
            def _(g=g):
                dwc_ref[g] += dwcs[g]

    row = lambda i: (i, 0)
    const = lambda i: (0, 0)
    wspec = pl.BlockSpec((SG_GROUPS, SG_CHUNK, SG_CHUNK), lambda i: (0, 0, 0))
    return pl.pallas_call(
        body, name=name, grid=(t // tm,),
        in_specs=[pl.BlockSpec((tm, SG_WIDTH), row), pl.BlockSpec((tm, SG_WIDTH), row),
                  pl.BlockSpec((1, SG_WIDTH), const), pl.BlockSpec((1, SG_WIDTH), const), wspec,
                  pl.BlockSpec((SG_CHUNK, LANES), const), pl.BlockSpec((tm, SG_WIDTH), row)],
        out_specs=(pl.BlockSpec((tm, SG_WIDTH), row), pl.BlockSpec((tm, SG_WIDTH), row),
                   pl.BlockSpec((1, SG_WIDTH), const), pl.BlockSpec((1, SG_WIDTH), const), wspec,
                   pl.BlockSpec((SG_CHUNK, LANES), const)),
        out_shape=(jax.ShapeDtypeStruct((t, SG_WIDTH), F32), jax.ShapeDtypeStruct((t, SG_WIDTH), F32),
                   jax.ShapeDtypeStruct((1, SG_WIDTH), F32), jax.ShapeDtypeStruct((1, SG_WIDTH), F32),
                   jax.ShapeDtypeStruct((SG_GROUPS, SG_CHUNK, SG_CHUNK), F32), jax.ShapeDtypeStruct((SG_CHUNK, LANES), F32)),
        compiler_params=_cparams(("arbitrary",)),
    )(u, v, lng, lnb, wc, sgbt, dout)


def _conv_taps(ext, w, tm):
    y = None
    for j in range(CONV_K):
        s = CONV_K - 1 - j
        shifted = ext if s == 0 else pltpu.roll(ext, s, 0)
        term = w[j:j + 1, :] * shifted[HALO:HALO + tm, :]
        y = term if y is None else y + term
    return y


def _post_conv(yq, yk, yv, bpre, apre, alog, dtb):
    def l2(a):
        return a * lax.rsqrt(jnp.sum(a * a, axis=-1, keepdims=True) + EPS)

    q = [l2(_silu(a)) for a in yq]
    k = [l2(_silu(a)) for a in yk]
    return q, k, _silu(yv), _sigmoid(bpre), -jnp.exp(alog) * _softplus(apre + dtb)


def _chunk_tril(tm):
    rr = lax.broadcasted_iota(jnp.int32, (tm, tm), 0)
    cc = lax.broadcasted_iota(jnp.int32, (tm, tm), 1)
    shift = DN_CHUNK.bit_length() - 1
    same = jnp.right_shift(rr, shift) == jnp.right_shift(cc, shift)
    return jnp.where(jnp.logical_and(same, rr >= cc), 1.0, 0.0).astype(F32)


def _halo_specs(tm, width, n_blocks_seq, n_blocks):
    per = tm // HALO
    prev = pl.BlockSpec((HALO, width), lambda i: (jnp.maximum(i * per - 1, 0), 0))
    nxt = pl.BlockSpec((HALO, width), lambda i: (jnp.minimum((i + 1) * per, n_blocks * per - 1), 0))
    return prev, nxt


def _split_heads(ref, base):
    return [ref[:, base + h * DN_HEAD_DIM: base + (h + 1) * DN_HEAD_DIM] for h in range(DN_HEADS)]


def _dn_prep_fwd(qkv, bpre, apre, conv_w, alog, dtb, seq, *, name):
    t = qkv.shape[0]
    tm = _tm(t)
    bps = seq // tm
    cw = 3 * DN_WIDTH

    def body(x_ref, halo_ref, b_ref, a_ref, w_ref, alog_ref, dtb_ref, q_ref, k_ref, v_ref, beta_ref, gc_ref):
        i = pl.program_id(0)
        keep = jnp.where(i % bps == 0, 0.0, 1.0)
        ext = jnp.concatenate([halo_ref[...] * keep, x_ref[...]], axis=0)
        y = _conv_taps(ext, w_ref[...], tm)
        yq = [y[:, h * DN_HEAD_DIM:(h + 1) * DN_HEAD_DIM] for h in range(DN_HEADS)]
        yk = [y[:, DN_WIDTH + h * DN_HEAD_DIM: DN_WIDTH + (h + 1) * DN_HEAD_DIM] for h in range(DN_HEADS)]
        q, k, v, beta, g = _post_conv(yq, yk, y[:, 2 * DN_WIDTH:], b_ref[...], a_ref[...], alog_ref[...], dtb_ref[...])
        for h in range(DN_HEADS):
            q_ref[:, h * DN_HEAD_DIM:(h + 1) * DN_HEAD_DIM] = q[h]
            k_ref[:, h * DN_HEAD_DIM:(h + 1) * DN_HEAD_DIM] = k[h]
        v_ref[...] = v
        beta_ref[...] = beta
        gc_ref[...] = mmx(_chunk_tril(tm), g)

    row = lambda i: (i, 0)
    const = lambda i: (0, 0)
    prev, _ = _halo_specs(tm, cw, bps, t // tm)
    return pl.pallas_call(
        body, name=name, grid=(t // tm,),
        in_specs=[pl.BlockSpec((tm, cw), row), prev, pl.BlockSpec((tm, LANES), row), pl.BlockSpec((tm, LANES), row),
                  pl.BlockSpec((CONV_K, cw), const), pl.BlockSpec((1, LANES), const), pl.BlockSpec((1, LANES), const)],
        out_specs=tuple(pl.BlockSpec((tm, n), row) for n in (DN_WIDTH, DN_WIDTH, DN_WIDTH, LANES, LANES)),
        out_shape=tuple(jax.ShapeDtypeStruct((t, n), F32) for n in (DN_WIDTH, DN_WIDTH, DN_WIDTH, LANES, LANES)),
        compiler_params=_cparams(("parallel",)),
    )(qkv, qkv, bpre, apre, conv_w, alog, dtb)


def _dn_prep_bwd(qkv, bpre, apre, conv_w, alog, dtb, dq, dk, dv, dbeta, dgc, dgc2, seq, *, name):
    t = qkv.shape[0]
    tm = _tm(t)
    bps = seq // tm
    cw = 3 * DN_WIDTH

    def body(x_ref, halo_ref, b_ref, a_ref, w_ref, alog_ref, dtb_ref, dq_ref, dk_ref, dv_ref, dbeta_ref, dgc_ref, dgc2_ref,
             dy_ref, db_ref, da_ref, dalog_ref, ddtb_ref):
        i = pl.program_id(0)
        keep = jnp.where(i % bps == 0, 0.0, 1.0)
        ext = jnp.concatenate([halo_ref[...] * keep, x_ref[...]], axis=0)
        y = _conv_taps(ext, w_ref[...], tm)
        yq = [y[:, h * DN_HEAD_DIM:(h + 1) * DN_HEAD_DIM] for h in range(DN_HEADS)]
        yk = [y[:, DN_WIDTH + h * DN_HEAD_DIM: DN_WIDTH + (h + 1) * DN_HEAD_DIM] for h in range(DN_HEADS)]
        _, vjp = jax.vjp(_post_conv, yq, yk, y[:, 2 * DN_WIDTH:], b_ref[...], a_ref[...], alog_ref[...], dtb_ref[...])
        dg = mmx_tn(_chunk_tril(tm), dgc_ref[...] + dgc2_ref[...])
        dyq, dyk, dyv, db, da, dalog, ddtb = vjp((_split_heads(dq_ref, 0), _split_heads(dk_ref, 0), dv_ref[...],
                                                  dbeta_ref[...], dg))
        for h in range(DN_HEADS):
            dy_ref[:, h * DN_HEAD_DIM:(h + 1) * DN_HEAD_DIM] = dyq[h]
            dy_ref[:, DN_WIDTH + h * DN_HEAD_DIM: DN_WIDTH + (h + 1) * DN_HEAD_DIM] = dyk[h]
        dy_ref[:, 2 * DN_WIDTH:] = dyv
        db_ref[...] = db
        da_ref[...] = da
        _acc_out(dalog_ref, i == 0, dalog)
        _acc_out(ddtb_ref, i == 0, ddtb)

    row = lambda i: (i, 0)
    const = lambda i: (0, 0)
    prev, _ = _halo_specs(tm, cw, bps, t // tm)
    return pl.pallas_call(
        body, name=name, grid=(t // tm,),
        in_specs=[pl.BlockSpec((tm, cw), row), prev, pl.BlockSpec((tm, LANES), row), pl.BlockSpec((tm, LANES), row),
                  pl.BlockSpec((CONV_K, cw), const), pl.BlockSpec((1, LANES), const), pl.BlockSpec((1, LANES), const),
                  pl.BlockSpec((tm, DN_WIDTH), row), pl.BlockSpec((tm, DN_WIDTH), row), pl.BlockSpec((tm, DN_WIDTH), row),
                  pl.BlockSpec((tm, LANES), row), pl.BlockSpec((tm, LANES), row), pl.BlockSpec((tm, LANES), row)],
        out_specs=(pl.BlockSpec((tm, cw), row), pl.BlockSpec((tm, LANES), row), pl.BlockSpec((tm, LANES), row),
                   pl.BlockSpec((1, LANES), const), pl.BlockSpec((1, LANES), const)),
        out_shape=(jax.ShapeDtypeStruct((t, cw), F32), jax.ShapeDtypeStruct((t, LANES), F32), jax.ShapeDtypeStruct((t, LANES), F32),
                   jax.ShapeDtypeStruct((1, LANES), F32), jax.ShapeDtypeStruct((1, LANES), F32)),
        compiler_params=_cparams(("arbitrary",)),
    )(qkv, qkv, bpre, apre, conv_w, alog, dtb, dq, dk, dv, dbeta, dgc, dgc2)


def _conv_bwd(qkv, dy, conv_w, seq, *, name):
    t = qkv.shape[0]
    tm = _tm(t)
    bps = seq // tm
    cw = 3 * DN_WIDTH
    n_ext = tm + HALO

    def body(x_ref, halo_ref, dy_ref, dyn_ref, w_ref, dx_ref, dw_ref):
        i = pl.program_id(0)
        keep_prev = jnp.where(i % bps == 0, 0.0, 1.0)
        keep_next = jnp.where(i % bps == bps - 1, 0.0, 1.0)
        ext = jnp.concatenate([halo_ref[...] * keep_prev, x_ref[...]], axis=0)
        dy = dy_ref[...]
        dyext = jnp.concatenate([dy, dyn_ref[...] * keep_next], axis=0)
        w = w_ref[...]

        @pl.when(i == 0)
        def _():
            dw_ref[...] = jnp.zeros_like(dw_ref)

        dx = None
        for j in range(CONV_K):
            s = CONV_K - 1 - j
            fut = dyext if s == 0 else pltpu.roll(dyext, n_ext - s, 0)
            term = w[j:j + 1, :] * fut[0:tm, :]
            dx = term if dx is None else dx + term
            past = ext if s == 0 else pltpu.roll(ext, s, 0)
            dw_ref[j:j + 1, :] += jnp.sum(dy * past[HALO:HALO + tm, :], axis=0, keepdims=True)
        dx_ref[...] = dx

    row = lambda i: (i, 0)
    const = lambda i: (0, 0)
    prev, nxt = _halo_specs(tm, cw, bps, t // tm)
    return pl.pallas_call(
        body, name=name, grid=(t // tm,),
        in_specs=[pl.BlockSpec((tm, cw), row), prev, pl.BlockSpec((tm, cw), row), nxt, pl.BlockSpec((CONV_K, cw), const)],
        out_specs=(pl.BlockSpec((tm, cw), row), pl.BlockSpec((HALO, cw), const)),
        out_shape=(jax.ShapeDtypeStruct((t, cw), F32), jax.ShapeDtypeStruct((HALO, cw), F32)),
        compiler_params=_cparams(("arbitrary",)),
    )(qkv, qkv, dy, dy, conv_w)


def _inv_unit_lower(l_mat, eye):
    inv = eye - l_mat
    power = l_mat
    n = 2
    while n < l_mat.shape[0]:
        power = mmx(power, power)
        inv = inv + mmx(inv, power)
        n *= 2
    return inv


@jax.custom_vjp
def _solve(l_mat, rhs, inv):
    return mmx(inv, rhs)


def _solve_fwd(l_mat, rhs, inv):
    sol = mmx(inv, rhs)
    return sol, (inv, sol)


def _solve_bwd(res, d_sol):
    inv, sol = res
    d_rhs = mmx_tn(inv, d_sol)
    return -mmx_nt(d_rhs, sol), d_rhs, jnp.zeros_like(inv)


_solve.defvjp(_solve_fwd, _solve_bwd)


def _prep_fn(q, k, v, gc, gr, b, inv):
    c = q.shape[0]
    rr = lax.broadcasted_iota(jnp.int32, (c, c), 0)
    cc = lax.broadcasted_iota(jnp.int32, (c, c), 1)
    incl, strict = rr >= cc, rr > cc
    qs = q * (DN_HEAD_DIM ** -0.5)
    decay = jnp.where(incl, jnp.exp(jnp.where(incl, gc - gr, 0.0)), 0.0)
    kb, vb = k * b, v * b
    l_mat = jnp.where(strict, mm_nt(kb, k) * decay, 0.0)
    eg = jnp.exp(gc)
    if inv is None:
        inv = _inv_unit_lower(l_mat, jnp.where(rr == cc, 1.0, 0.0).astype(F32))
    u_wy = _solve(l_mat, vb, inv)
    w_wy = _solve(l_mat, kb * eg, inv)
    qk = mm_nt(qs, k) * decay
    is_last = lax.broadcasted_iota(jnp.int32, (c, 1), 0) == c - 1
    g_last = jnp.sum(jnp.where(is_last, gc, 0.0), axis=0, keepdims=True)
    k_dec = k * jnp.exp(g_last - gc)
    egl = jnp.broadcast_to(jnp.exp(g_last), (1, LANES))
    return (w_wy, u_wy, qs * eg, k_dec, qk, egl), inv


def _seq_fn(w, u, qd, kd, qk, egl, s):
    v_new = u - mm(w, s)
    o = mm(qd, s) + mm(qk, v_new)
    return o, s * egl + mm_tn(kd, v_new)


def _lane_col(a, h):
    lane = lax.broadcasted_iota(jnp.int32, (1, LANES), 1)
    return jnp.sum(jnp.where(lane == h, a, 0.0), axis=1, keepdims=True)


def _col_lane(col, h):
    lane = lax.broadcasted_iota(jnp.int32, (1, LANES), 1)
    return jnp.where(lane == h, col, 0.0)


def _head_cols(h):
    return slice(h * DN_HEAD_DIM, (h + 1) * DN_HEAD_DIM)


def _chunk_rows(n):
    return pl.ds(pl.multiple_of(n * DN_CHUNK, DN_CHUNK), DN_CHUNK)


def _delta_prep(q, k, v, gc, grow, beta, *, name):
    t = q.shape[0]
    tm = _tm(t)
    cpb = tm // DN_CHUNK
    n_chunks = t // DN_CHUNK

    def body(q_ref, k_ref, v_ref, gc_ref, gr_ref, b_ref, w_ref, u_ref, qd_ref, kd_ref, qk_ref, egl_ref, inv_ref):
        def step(n, carry):
            rows = _chunk_rows(n)
            gcb, bb, grb = gc_ref[rows, :], b_ref[rows, :], gr_ref[n]
            egl_ref[n] = jnp.zeros((HALO, LANES), F32)
            for h in range(DN_HEADS):
                cols = _head_cols(h)
                (w, u, qd, kd, qk, egl), inv = _prep_fn(q_ref[rows, cols], k_ref[rows, cols], v_ref[rows, cols],
                                                        _lane_col(gcb, h), grb[h:h + 1, :], _lane_col(bb, h), None)
                w_ref[rows, cols] = w.astype(BF16)
                u_ref[rows, cols] = u
                qd_ref[rows, cols] = qd.astype(BF16)
                kd_ref[rows, cols] = kd.astype(BF16)
                qk_ref[n, h] = qk
                inv_ref[n, h] = inv
                egl_ref[n, h:h + 1, :] = egl
            return carry

        lax.fori_loop(0, cpb, step, 0)

    row = lambda i: (i, 0)
    tok = pl.BlockSpec((tm, DN_WIDTH), row)
    lanes = pl.BlockSpec((tm, LANES), row)
    sq = pl.BlockSpec((cpb, DN_HEADS, DN_CHUNK, DN_CHUNK), lambda i: (i, 0, 0, 0))
    return pl.pallas_call(
        body, name=name, grid=(t // tm,),
        in_specs=[tok, tok, tok, lanes, pl.BlockSpec((cpb, HALO, DN_CHUNK), lambda i: (i, 0, 0)), lanes],
        out_specs=(tok, tok, tok, tok, sq, pl.BlockSpec((cpb, HALO, LANES), lambda i: (i, 0, 0)), sq),
        out_shape=(jax.ShapeDtypeStruct((t, DN_WIDTH), BF16), jax.ShapeDtypeStruct((t, DN_WIDTH), F32),
                   jax.ShapeDtypeStruct((t, DN_WIDTH), BF16), jax.ShapeDtypeStruct((t, DN_WIDTH), BF16),
                   jax.ShapeDtypeStruct((n_chunks, DN_HEADS, DN_CHUNK, DN_CHUNK), F32),
                   jax.ShapeDtypeStruct((n_chunks, HALO, LANES), F32),
                   jax.ShapeDtypeStruct((n_chunks, DN_HEADS, DN_CHUNK, DN_CHUNK), F32)),
        compiler_params=_cparams(("parallel",)),
    )(q, k, v, gc, grow, beta)


def _delta_par_bwd(q, k, v, gc, grow, beta, inv, dw, du, dqd, dkd, dqk, degl, *, name):
    t = q.shape[0]
    tm = _tm(t)
    cpb = tm // DN_CHUNK
    n_chunks = t // DN_CHUNK

    def body(q_ref, k_ref, v_ref, gc_ref, gr_ref, b_ref, inv_ref, dw_ref, du_ref, dqd_ref, dkd_ref, dqk_ref, degl_ref,
             dq_ref, dk_ref, dv_ref, dgc_ref, dgr_ref, db_ref):
        def step(n, carry):
            rows = _chunk_rows(n)
            gcb, bb, grb, deglb = gc_ref[rows, :], b_ref[rows, :], gr_ref[n], degl_ref[n]
            dgr_ref[n] = jnp.zeros((HALO, DN_CHUNK), F32)
            dgc_acc = jnp.zeros((DN_CHUNK, LANES), F32)
            db_acc = jnp.zeros((DN_CHUNK, LANES), F32)
            for h in range(DN_HEADS):
                cols = _head_cols(h)
                inv = inv_ref[n, h]
                _, vjp = jax.vjp(lambda *a: _prep_fn(*a, inv)[0], q_ref[rows, cols], k_ref[rows, cols], v_ref[rows, cols],
                                 _lane_col(gcb, h), grb[h:h + 1, :], _lane_col(bb, h))
                dq, dk, dv, dgc, dgr, db = vjp((dw_ref[rows, cols], du_ref[rows, cols], dqd_ref[rows, cols],
                                                dkd_ref[rows, cols], dqk_ref[n, h], deglb[h:h + 1, :]))
                dq_ref[rows, cols] = dq
                dk_ref[rows, cols] = dk
                dv_ref[rows, cols] = dv
                dgr_ref[n, h:h + 1, :] = dgr
                dgc_acc = dgc_acc + _col_lane(dgc, h)
                db_acc = db_acc + _col_lane(db, h)
            dgc_ref[rows, :] = dgc_acc
            db_ref[rows, :] = db_acc
            return carry

        lax.fori_loop(0, cpb, step, 0)

    row = lambda i: (i, 0)
    tok = pl.BlockSpec((tm, DN_WIDTH), row)
    lanes = pl.BlockSpec((tm, LANES), row)
    sq = pl.BlockSpec((cpb, DN_HEADS, DN_CHUNK, DN_CHUNK), lambda i: (i, 0, 0, 0))
    grs = pl.BlockSpec((cpb, HALO, DN_CHUNK), lambda i: (i, 0, 0))
    return pl.pallas_call(
        body, name=name, grid=(t // tm,),
        in_specs=[tok, tok, tok, lanes, grs, lanes, sq, tok, tok, tok, tok, sq, pl.BlockSpec((cpb, HALO, LANES), lambda i: (i, 0, 0))],
        out_specs=(tok, tok, tok, lanes, grs, lanes),
        out_shape=(jax.ShapeDtypeStruct((t, DN_WIDTH), F32),) * 3
        + (jax.ShapeDtypeStruct((t, LANES), F32), jax.ShapeDtypeStruct((n_chunks, HALO, DN_CHUNK), F32),
           jax.ShapeDtypeStruct((t, LANES), F32)),
        compiler_params=_cparams(("parallel",)),
    )(q, k, v, gc, grow, beta, inv, dw, du, dqd, dkd, dqk, degl)


def _seq_specs(n_seq, seq, reverse):
    tm = _tm(seq)
    nb = seq // tm
    cpb = tm // DN_CHUNK
    blk = (lambda b, j: b * nb + nb - 1 - j) if reverse else (lambda b, j: b * nb + j)
    tok = pl.BlockSpec((tm, DN_WIDTH), lambda b, j: (blk(b, j), 0))
    sq = pl.BlockSpec((cpb, DN_HEADS, DN_CHUNK, DN_CHUNK), lambda b, j: (blk(b, j), 0, 0, 0))
    rows8 = pl.BlockSpec((cpb, HALO, LANES), lambda b, j: (blk(b, j), 0, 0))
    state = pl.BlockSpec((cpb, DN_HEADS, DN_HEAD_DIM, DN_HEAD_DIM), lambda b, j: (blk(b, j), 0, 0, 0))
    return nb, cpb, tok, sq, rows8, state


def _delta_seq_fwd(w, u, qd, kd, qk, egl, n_seq, seq, *, name):
    nb, cpb, tok, sq, rows8, state = _seq_specs(n_seq, seq, False)
    t = n_seq * seq

    def body(w_ref, u_ref, qd_ref, kd_ref, qk_ref, egl_ref, o_ref, st_ref, s_s):
        @pl.when(pl.program_id(1) == 0)
        def _():
            s_s[...] = jnp.zeros_like(s_s)

        def step(n, carry):
            rows = _chunk_rows(n)
            eglb = egl_ref[n]
            for h in range(DN_HEADS):
                cols = _head_cols(h)
                s = s_s[h]
                st_ref[n, h] = s
                o, s_new = _seq_fn(w_ref[rows, cols], u_ref[rows, cols], qd_ref[rows, cols], kd_ref[rows, cols],
                                   qk_ref[n, h], eglb[h:h + 1, :], s)
                o_ref[rows, cols] = o
                s_s[h] = s_new
            return carry

        lax.fori_loop(0, cpb, step, 0)

    return pl.pallas_call(
        body, name=name, grid=(n_seq, nb),
        in_specs=[tok, tok, tok, tok, sq, rows8],
        out_specs=(tok, state),
        out_shape=(jax.ShapeDtypeStruct((t, DN_WIDTH), F32),
                   jax.ShapeDtypeStruct((t // DN_CHUNK, DN_HEADS, DN_HEAD_DIM, DN_HEAD_DIM), F32)),
        scratch_shapes=[pltpu.VMEM((DN_HEADS, DN_HEAD_DIM, DN_HEAD_DIM), F32)],
        compiler_params=_cparams(("parallel", "arbitrary")),
    )(w, u, qd, kd, qk, egl)


def _delta_seq_bwd(w, u, qd, kd, qk, egl, states, do, n_seq, seq, *, name):
    nb, cpb, tok, sq, rows8, state = _seq_specs(n_seq, seq, True)
    t = n_seq * seq

    def body(w_ref, u_ref, qd_ref, kd_ref, qk_ref, egl_ref, st_ref, do_ref, dw_ref, du_ref, dqd_ref, dkd_ref, dqk_ref,
             degl_ref, ds_s):
        @pl.when(pl.program_id(1) == 0)
        def _():
            ds_s[...] = jnp.zeros_like(ds_s)

        def step(m, carry):
            n = cpb - 1 - m
            rows = _chunk_rows(n)
            eglb = egl_ref[n]
            degl_ref[n] = jnp.zeros((HALO, LANES), F32)
            for h in range(DN_HEADS):
                cols = _head_cols(h)
                _, vjp = jax.vjp(_seq_fn, w_ref[rows, cols].astype(F32), u_ref[rows, cols], qd_ref[rows, cols].astype(F32),
                                 kd_ref[rows, cols].astype(F32), qk_ref[n, h], eglb[h:h + 1, :], st_ref[n, h])
                dw, du, dqd, dkd, dqk, degl, ds_in = vjp((do_ref[rows, cols], ds_s[h]))
                dw_ref[rows, cols] = dw
                du_ref[rows, cols] = du
                dqd_ref[rows, cols] = dqd
                dkd_ref[rows, cols] = dkd
                dqk_ref[n, h] = dqk
                degl_ref[n, h:h + 1, :] = degl
                ds_s[h] = ds_in
            return carry

        lax.fori_loop(0, cpb, step, 0)

    return pl.pallas_call(
        body, name=name, grid=(n_seq, nb),
        in_specs=[tok, tok, tok, tok, sq, rows8, state, tok],
        out_specs=(tok, tok, tok, tok, sq, rows8),
        out_shape=(jax.ShapeDtypeStruct((t, DN_WIDTH), F32),) * 4
        + (jax.ShapeDtypeStruct((t // DN_CHUNK, DN_HEADS, DN_CHUNK, DN_CHUNK), F32),
           jax.ShapeDtypeStruct((t // DN_CHUNK, HALO, LANES), F32)),
        scratch_shapes=[pltpu.VMEM((DN_HEADS, DN_HEAD_DIM, DN_HEAD_DIM), F32)],
        compiler_params=_cparams(("parallel", "arbitrary")),
    )(w, u, qd, kd, qk, egl, states, do)


def _dn_gate(o, z, dnw):
    return o * lax.rsqrt(jnp.mean(o * o, axis=-1, keepdims=True) + EPS) * dnw * _silu(z)


def _mix_out_fwd(x, sg, o, z, wo_sg, wo_dn, dnw, *, name):
    t = x.shape[0]
    tm = _tm(t)

    def body(x_ref, sg_ref, o_ref, z_ref, wsg_ref, wdn_ref, dnw_ref, y_ref, dn_s):
        for h, (oh, zh) in enumerate(zip(_split_heads(o_ref, 0), _split_heads(z_ref, 0))):
            dn_s[:, h * DN_HEAD_DIM:(h + 1) * DN_HEAD_DIM] = _dn_gate(oh, zh, dnw_ref[...]).astype(BF16)
        y_ref[...] = (x_ref[...] + jnp.dot(sg_ref[...].astype(BF16), wsg_ref[...], preferred_element_type=F32)
                      + jnp.dot(dn_s[...], wdn_ref[...], preferred_element_type=F32))

    row = lambda i: (i, 0)
    const = lambda i: (0, 0)
    half = pl.BlockSpec((tm, DN_WIDTH), row)
    return pl.pallas_call(
        body, name=name, grid=(t // tm,),
        in_specs=[pl.BlockSpec((tm, D_MODEL), row), half, half, half, pl.BlockSpec((SG_WIDTH, D_MODEL), const),
                  pl.BlockSpec((DN_WIDTH, D_MODEL), const), pl.BlockSpec((1, DN_HEAD_DIM), const)],
        out_specs=pl.BlockSpec((tm, D_MODEL), row),
        out_shape=jax.ShapeDtypeStruct((t, D_MODEL), F32),
        scratch_shapes=[pltpu.VMEM((tm, DN_WIDTH), BF16)],
        compiler_params=_cparams(("parallel",)),
    )(x, sg, o, z, wo_sg, wo_dn, dnw)


def _mix_out_bwd(dy, sg, o, z, wo_sg, wo_dn, dnw, *, name):
    t = dy.shape[0]
    tm = _tm(t)

    def body(dy_ref, sg_ref, o_ref, z_ref, wsg_ref, wdn_ref, dnw_ref, dsg_ref, do_ref, dz_ref, dwsg_ref, dwdn_ref, ddnw_ref, dn_s):
        i = pl.program_id(0)
        dyb = dy_ref[...].astype(BF16)
        nt = (((1,), (1,)), ((), ()))
        tn = (((0,), (0,)), ((), ()))
        dsg_ref[...] = lax.dot_general(dyb, wsg_ref[...], nt, preferred_element_type=F32)
        ddn = lax.dot_general(dyb, wdn_ref[...], nt, preferred_element_type=F32)
        ddnw = None
        for h, (oh, zh) in enumerate(zip(_split_heads(o_ref, 0), _split_heads(z_ref, 0))):
            cols = slice(h * DN_HEAD_DIM, (h + 1) * DN_HEAD_DIM)
            out, vjp = jax.vjp(_dn_gate, oh, zh, dnw_ref[...])
            dn_s[:, cols] = out.astype(BF16)
            doh, dzh, dw = vjp(ddn[:, cols])
            do_ref[:, cols] = doh
            dz_ref[:, cols] = dzh
            ddnw = dw if ddnw is None else ddnw + dw
        _acc_out(ddnw_ref, i == 0, ddnw)
        _acc_out(dwsg_ref, i == 0, lax.dot_general(sg_ref[...].astype(BF16), dyb, tn, preferred_element_type=F32))
        _acc_out(dwdn_ref, i == 0, lax.dot_general(dn_s[...], dyb, tn, preferred_element_type=F32))

    row = lambda i: (i, 0)
    const = lambda i: (0, 0)
    half = pl.BlockSpec((tm, DN_WIDTH), row)
    wspec = pl.BlockSpec((DN_WIDTH, D_MODEL), const)
    return pl.pallas_call(
        body, name=name, grid=(t // tm,),
        in_specs=[pl.BlockSpec((tm, D_MODEL), row), half, half, half, wspec, wspec, pl.BlockSpec((1, DN_HEAD_DIM), const)],
        out_specs=(half, half, half, wspec, wspec, pl.BlockSpec((1, DN_HEAD_DIM), const)),
        out_shape=(jax.ShapeDtypeStruct((t, DN_WIDTH), F32),) * 3 + (jax.ShapeDtypeStruct((DN_WIDTH, D_MODEL), F32),) * 2
        + (jax.ShapeDtypeStruct((1, DN_HEAD_DIM), F32),),
        scratch_shapes=[pltpu.VMEM((tm, DN_WIDTH), BF16)],
        compiler_params=_cparams(("arbitrary",)),
    )(dy, sg, o, z, wo_sg, wo_dn, dnw)


def _exchange(arrs, gather, *, name):
    n = len(arrs)
    n_peer = N_DEV - 1

    def body(*refs):
        in_refs, out_refs = refs[:n], refs[n:2 * n]
        send_sems, recv_sems, local_sems = refs[2 * n:]
        x, y, c = lax.axis_index("x"), lax.axis_index("y"), lax.axis_index("c")
        me = 4 * x + 2 * y + c
        copies = []
        for k in range(n):
            own = in_refs[k] if gather else in_refs[k].at[me]
            copies.append(pltpu.make_async_copy(own, out_refs[k].at[me], local_sems.at[k]))
        for r in range(1, N_DEV):
            px = 1 - x if r & 4 else x
            py = 1 - y if r & 2 else y
            pc = 1 - c if r & 1 else c
            peer = 4 * px + 2 * py + pc
            for k in range(n):
                src = in_refs[k] if gather else in_refs[k].at[peer]
                copies.append(pltpu.make_async_remote_copy(
                    src_ref=src, dst_ref=out_refs[k].at[me],
                    send_sem=send_sems.at[k * n_peer + r - 1], recv_sem=recv_sems.at[k * n_peer + r - 1],
                    device_id=(px, py, pc), device_id_type=pl.DeviceIdType.MESH))
        for cp in copies:
            cp.start()
        for cp in copies:
            cp.wait()

    hbm = pl.BlockSpec(memory_space=pl.ANY)
    out_shape = tuple(jax.ShapeDtypeStruct(((N_DEV,) + a.shape) if gather else a.shape, a.dtype) for a in arrs)
    return pl.pallas_call(
        body, name=name, in_specs=[hbm] * n, out_specs=(hbm,) * n, out_shape=out_shape,
        scratch_shapes=[pltpu.SemaphoreType.DMA((n * n_peer,)), pltpu.SemaphoreType.DMA((n * n_peer,)),
                        pltpu.SemaphoreType.DMA((n,))],
    )(*arrs)


def _row_block(rows, limit=256):
    best = rows
    for cand in range(8, limit + 1, 8):
        if rows % cand == 0:
            best = cand
    return best if rows > limit else rows


def _adam(gp, w, m, v, *, name):
    p, rows, cols = gp.shape
    rb = _row_block(rows)

    def body(gp_ref, w_ref, m_ref, v_ref, g_ref, d_ref, m2_ref, v2_ref):
        g = gp_ref[0]
        for s in range(1, p):
            g = g + gp_ref[s]
        m2 = ADAM_B1 * m_ref[...] + (1.0 - ADAM_B1) * g
        v2 = ADAM_B2 * v_ref[...] + (1.0 - ADAM_B2) * (g * g)
        m_hat = m2 / (1.0 - ADAM_B1 ** ADAM_STEP)
        v_hat = v2 / (1.0 - ADAM_B2 ** ADAM_STEP)
        g_ref[...] = g
        d_ref[...] = -ADAM_LR * (m_hat / (jnp.sqrt(v_hat) + ADAM_EPS) + ADAM_WD * w_ref[...])
        m2_ref[...] = m2
        v2_ref[...] = v2

    blk = pl.BlockSpec((rb, cols), lambda i: (i, 0))
    return pl.pallas_call(
        body, name=name, grid=(rows // rb,),
        in_specs=[pl.BlockSpec((p, rb, cols), lambda i: (0, i, 0)), blk, blk, blk],
        out_specs=(blk,) * 4, out_shape=(jax.ShapeDtypeStruct((rows, cols), F32),) * 4,
        compiler_params=_cparams(("parallel",)),
    )(gp, w, m, v)


def _cols_full(g):
    return jnp.transpose(g, (1, 0, 2)).reshape(g.shape[1], N_DEV * g.shape[2])


def _cols_pieces(full):
    r, c = full.shape
    return jnp.transpose(full.reshape(r, N_DEV, c // N_DEV), (1, 0, 2))


def _pad_lanes(a, width=LANES):
    return jnp.pad(a, ((0, 0), (0, width - a.shape[1])))


def _chunk_rows_of(a):
    by_chunk = jnp.transpose(a[:, :DN_HEADS].reshape(-1, DN_CHUNK, DN_HEADS), (0, 2, 1))
    return jnp.pad(by_chunk, ((0, 0), (0, HALO - DN_HEADS), (0, 0)))


_SMALL = (("ffn1_norm", D_MODEL), ("mix_norm", D_MODEL), ("ffn2_norm", D_MODEL), ("final_norm", D_MODEL), ("a_log", DN_HEADS),
          ("dt_bias", DN_HEADS), ("dn_norm", DN_HEAD_DIM), ("sg_ln_g", SG_WIDTH), ("sg_ln_b", SG_WIDTH),
          ("sg_w", SG_GROUPS * SG_CHUNK * SG_CHUNK), ("sg_b", SG_GROUPS * SG_CHUNK), ("conv_w", CONV_K * 3 * DN_WIDTH))
_SMALL_ROWS = 1128
_SMALL_SHAPES = {"ffn1_norm": (1, D_MODEL), "mix_norm": (1, D_MODEL), "ffn2_norm": (1, D_MODEL), "final_norm": (D_MODEL,),
                 "a_log": (1, DN_HEADS), "dt_bias": (1, DN_HEADS), "dn_norm": (1, DN_HEAD_DIM), "sg_ln_g": (1, SG_WIDTH),
                 "sg_ln_b": (1, SG_WIDTH), "sg_w": (1, SG_GROUPS, SG_CHUNK, SG_CHUNK), "sg_b": (1, SG_GROUPS, SG_CHUNK)}


def _pack_small(d):
    flat = jnp.concatenate([d[name].reshape(-1) for name, _ in _SMALL])
    return jnp.pad(flat, (0, _SMALL_ROWS * LANES - flat.shape[0])).reshape(_SMALL_ROWS, LANES)


def _unpack_small(a):
    flat, out, at = a.reshape(-1), {}, 0
    for name, size in _SMALL:
        out[name] = flat[at:at + size]
        at += size
    return out


def kernel(x, ffn1_norm, ffn1_w_gate, ffn1_w_up, ffn1_w_down, mix_norm, w_in, conv_w, a_log, dt_bias, dn_norm, sg_ln_g, sg_ln_b, sg_w, sg_b, w_out, ffn2_norm, ffn2_w_gate, ffn2_w_up, ffn2_w_down, final_norm, loss_target, m_ffn1_norm, m_ffn1_w_gate, m_ffn1_w_up, m_ffn1_w_down, m_mix_norm, m_w_in, m_conv_w, m_a_log, m_dt_bias, m_dn_norm, m_sg_ln_g, m_sg_ln_b, m_sg_w, m_sg_b, m_w_out, m_ffn2_norm, m_ffn2_w_gate, m_ffn2_w_up, m_ffn2_w_down, m_final_norm, v_ffn1_norm, v_ffn1_w_gate, v_ffn1_w_up, v_ffn1_w_down, v_mix_norm, v_w_in, v_conv_w, v_a_log, v_dt_bias, v_dn_norm, v_sg_ln_g, v_sg_ln_b, v_sg_w, v_sg_b, v_w_out, v_ffn2_norm, v_ffn2_w_gate, v_ffn2_w_up, v_ffn2_w_down, v_final_norm):
    weights = dict(ffn1_norm=ffn1_norm, ffn1_w_gate=ffn1_w_gate, ffn1_w_up=ffn1_w_up, ffn1_w_down=ffn1_w_down, mix_norm=mix_norm, w_in=w_in, conv_w=conv_w, a_log=a_log, dt_bias=dt_bias, dn_norm=dn_norm, sg_ln_g=sg_ln_g, sg_ln_b=sg_ln_b, sg_w=sg_w, sg_b=sg_b, w_out=w_out, ffn2_norm=ffn2_norm, ffn2_w_gate=ffn2_w_gate, ffn2_w_up=ffn2_w_up, ffn2_w_down=ffn2_w_down, final_norm=final_norm)
    mom_m = dict(ffn1_norm=m_ffn1_norm, ffn1_w_gate=m_ffn1_w_gate, ffn1_w_up=m_ffn1_w_up, ffn1_w_down=m_ffn1_w_down, mix_norm=m_mix_norm, w_in=m_w_in, conv_w=m_conv_w, a_log=m_a_log, dt_bias=m_dt_bias, dn_norm=m_dn_norm, sg_ln_g=m_sg_ln_g, sg_ln_b=m_sg_ln_b, sg_w=m_sg_w, sg_b=m_sg_b, w_out=m_w_out, ffn2_norm=m_ffn2_norm, ffn2_w_gate=m_ffn2_w_gate, ffn2_w_up=m_ffn2_w_up, ffn2_w_down=m_ffn2_w_down, final_norm=m_final_norm)
    mom_v = dict(ffn1_norm=v_ffn1_norm, ffn1_w_gate=v_ffn1_w_gate, ffn1_w_up=v_ffn1_w_up, ffn1_w_down=v_ffn1_w_down, mix_norm=v_mix_norm, w_in=v_w_in, conv_w=v_conv_w, a_log=v_a_log, dt_bias=v_dt_bias, dn_norm=v_dn_norm, sg_ln_g=v_sg_ln_g, sg_ln_b=v_sg_ln_b, sg_w=v_sg_w, sg_b=v_sg_b, w_out=v_w_out, ffn2_norm=v_ffn2_norm, ffn2_w_gate=v_ffn2_w_gate, ffn2_w_up=v_ffn2_w_up, ffn2_w_down=v_ffn2_w_down, final_norm=v_final_norm)
    order = list(weights)
    big = ("ffn1_w_gate", "ffn1_w_up", "ffn1_w_down", "w_in", "w_out", "ffn2_w_gate", "ffn2_w_up", "ffn2_w_down")
    col_sharded = ("ffn1_w_gate", "ffn1_w_up", "w_in", "ffn2_w_gate", "ffn2_w_up")

    n_seq, seq, _ = x.shape
    t = n_seq * seq
    me = 4 * lax.axis_index("x") + 2 * lax.axis_index("y") + lax.axis_index("c")
    x0 = x.reshape(t, D_MODEL)
    tgt = loss_target.reshape(t, D_MODEL)

    gathered = _exchange([weights[n][0].astype(BF16) for n in big] + [conv_w[0]], True, name="gather_weights")
    full = {n: (_cols_full(g) if n in col_sharded else g.reshape(-1, g.shape[-1])) for n, g in zip(big, gathered)}
    conv_full = _cols_full(gathered[-1])
    w_in_f = full["w_in"]
    offs = (0, SG_WIDTH, 2 * SG_WIDTH, 2 * SG_WIDTH + 3 * DN_WIDTH, 2 * SG_WIDTH + 4 * DN_WIDTH)
    n_proj = offs[-1]
    ws = [w_in_f[:, offs[0]:offs[1]], w_in_f[:, offs[1]:offs[2]], w_in_f[:, offs[2]:offs[3]], w_in_f[:, offs[3]:offs[4]],
          _pad_lanes(w_in_f[:, n_proj:n_proj + DN_HEADS]), _pad_lanes(w_in_f[:, n_proj + DN_HEADS:n_proj + 2 * DN_HEADS])]
    wo_sg, wo_dn = full["w_out"][:SG_WIDTH], full["w_out"][SG_WIDTH:]
    alog, dtb = _pad_lanes(a_log), _pad_lanes(dt_bias)
    sgbt = _pad_lanes(sg_b[0].T)
    fnw = final_norm.reshape(1, D_MODEL)

    x1 = _ffn_fwd(x0, ffn1_norm, full["ffn1_w_gate"], full["ffn1_w_up"], full["ffn1_w_down"], name="ffn1_fwd")
    u, v, qkv, z, bpre, apre = _mix_in_fwd(x1, mix_norm, ws, name="mix_in_fwd")
    sg_out = _sg_fwd(u, v, sg_ln_g, sg_ln_b, sg_w[0], sgbt, name="sg_fwd")
    q, k, vv, beta, gc = _dn_prep_fwd(qkv, bpre, apre, conv_full, alog, dtb, seq, name="dn_prep_fwd")
    grow = _chunk_rows_of(gc)
    wy_w, wy_u, q_dec, k_dec, qk, egl, inv = _delta_prep(q, k, vv, gc, grow, beta, name="delta_prep")
    o, states = _delta_seq_fwd(wy_w, wy_u, q_dec, k_dec, qk, egl, n_seq, seq, name="delta_seq_fwd")
    x2 = _mix_out_fwd(x1, sg_out, o, z, wo_sg, wo_dn, dn_norm, name="mix_out_fwd")
    dx3, loss_part, d_fn = _ffn_fwd(x2, ffn2_norm, full["ffn2_w_gate"], full["ffn2_w_up"], full["ffn2_w_down"], tgt, fnw,
                                    name="ffn2_fwd_loss")
    loss = lax.psum(loss_part[0, 0], ("x", "y", "c"))

    dx2, d_n2, d_g2, d_u2, d_d2 = _ffn_bwd(x2, ffn2_norm, full["ffn2_w_gate"], full["ffn2_w_up"], full["ffn2_w_down"], dx3,
                                           name="ffn2_bwd")
    dsg, do, dz, d_wo_sg, d_wo_dn, d_dnw = _mix_out_bwd(dx2, sg_out, o, z, wo_sg, wo_dn, dn_norm, name="mix_out_bwd")
    d_seq = _delta_seq_bwd(wy_w, wy_u, q_dec, k_dec, qk, egl, states, do, n_seq, seq, name="delta_seq_bwd")
    dq, dk, dv, dgc_a, dgrow, dbeta = _delta_par_bwd(q, k, vv, gc, grow, beta, inv, *d_seq, name="delta_par_bwd")
    dgc_b = _pad_lanes(jnp.transpose(dgrow[:, :DN_HEADS, :], (0, 2, 1)).reshape(t, DN_HEADS))
    dy_conv, dbpre, dapre, d_alog, d_dtb = _dn_prep_bwd(qkv, bpre, apre, conv_full, alog, dtb, dq, dk, dv, dbeta, dgc_a, dgc_b,
                                                        seq, name="dn_prep_bwd")
    dqkv, d_conv = _conv_bwd(qkv, dy_conv, conv_full, seq, name="conv_bwd")
    du, dvv, d_lng, d_lnb, d_wc, d_sgbt = _sg_bwd(u, v, sg_ln_g, sg_ln_b, sg_w[0], sgbt, dsg, name="sg_bwd")
    dx1, d_mixn, d_ws = _split3(_mix_in_bwd(x1, mix_norm, ws, dx2, (du, dvv, dqkv, dz, dbpre, dapre), name="mix_in_bwd"))
    grad_x, d_n1, d_g1, d_u1, d_d1 = _ffn_bwd(x0, ffn1_norm, full["ffn1_w_gate"], full["ffn1_w_up"], full["ffn1_w_down"], dx1,
                                              name="ffn1_bwd")

    def ff_cols(acc):
        return _cols_pieces(jnp.transpose(acc, (1, 0, 2)).reshape(D_MODEL, D_FF))

    def ff_rows(acc):
        return acc.reshape(N_DEV, D_FF // N_DEV, D_MODEL)

    d_w_in = jnp.concatenate([d_ws[0], d_ws[1], d_ws[2], d_ws[3], d_ws[4][:, :DN_HEADS], d_ws[5][:, :DN_HEADS]], axis=1)
    d_w_out = jnp.concatenate([d_wo_sg, d_wo_dn], axis=0)
    pieces = dict(ffn1_w_gate=ff_cols(d_g1), ffn1_w_up=ff_cols(d_u1), ffn1_w_down=ff_rows(d_d1), w_in=_cols_pieces(d_w_in),
                  w_out=d_w_out.reshape(N_DEV, D_MODEL // N_DEV, D_MODEL), ffn2_w_gate=ff_cols(d_g2), ffn2_w_up=ff_cols(d_u2),
                  ffn2_w_down=ff_rows(d_d2))
    received = _exchange([pieces[n] for n in big], False, name="scatter_grads")
    res = {}
    for n, gp in zip(big, received):
        res[n] = _adam(gp, weights[n][0], mom_m[n][0], mom_v[n][0], name="adam_" + n)

    small_grads = dict(ffn1_norm=d_n1, mix_norm=d_mixn, ffn2_norm=d_n2, final_norm=d_fn, a_log=d_alog[:, :DN_HEADS],
                       dt_bias=d_dtb[:, :DN_HEADS], dn_norm=d_dnw, sg_ln_g=d_lng, sg_ln_b=d_lnb, sg_w=d_wc,
                       sg_b=d_sgbt[:, :SG_GROUPS].T, conv_w=d_conv[:CONV_K])
    (small_parts,) = _exchange([_pack_small(small_grads)], True, name="gather_small_grads")
    zeros_conv = jnp.zeros((CONV_K * 3 * DN_WIDTH,), F32)
    packed = [_pack_small({**{n: src[n] for n, _ in _SMALL if n != "conv_w"}, "conv_w": zeros_conv})
              for src in (weights, mom_m, mom_v)]
    small_res = [_unpack_small(a) for a in _adam(small_parts, *packed, name="adam_small")]
    conv_grad = lax.dynamic_slice_in_dim(small_res[0]["conv_w"].reshape(CONV_K, 3 * DN_WIDTH), me * (3 * DN_WIDTH // N_DEV),
                                         3 * DN_WIDTH // N_DEV, axis=1)
    res["conv_w"] = _adam(conv_grad[None], conv_w[0], m_conv_w[0], v_conv_w[0], name="adam_conv_w")

    outs = [[], [], [], []]
    for n in order:
        for kind in range(4):
            if n in res:
                outs[kind].append(res[n][kind][None])
            else:
                outs[kind].append(small_res[kind][n].reshape(_SMALL_SHAPES[n]))
    return (loss, grad_x.reshape(x.shape), *outs[0], *outs[1], *outs[2], *outs[3])


def _split3(r):
    return r[0], r[1], r[2:]
```

```python
import functools

import jax
import jax.numpy as jnp
from jax import lax
from jax.experimental import pallas as pl
from jax.experimental.pallas import tpu as pltpu

F32 = jnp.float32
BF16 = jnp.bfloat16

D_MODEL = 1024
D_FF = 2816
SG_WIDTH = 512
SG_GROUPS = 8
SG_GROUP_DIM = 64
SG_CHUNK = 128
DN_WIDTH = 512
DN_HEAD_DIM = 128
DN_HEADS = 4
DN_CHUNK = 64
CONV_K = 4
EPS = 1e-6
N_DEV = 8
LANES = 128
HALO = 8

ADAM_LR = 0.001
ADAM_B1 = 0.9
ADAM_B2 = 0.999
ADAM_EPS = 1e-08
ADAM_WD = 0.01
ADAM_STEP = 10

VMEM_LIMIT = 60 * 1024 * 1024
TOKEN_BLOCK = 512
FF_BLOCK_FWD = 1408
FF_BLOCK_BWD = 256

_HI = lax.Precision.HIGHEST


def _cparams(sem):
    return pltpu.CompilerParams(dimension_semantics=sem, vmem_limit_bytes=VMEM_LIMIT)


def _tm(t, pref=TOKEN_BLOCK):
    return min(pref, t)


def _dg(a, b, ca, cb, exact):
    if exact:
        return lax.dot_general(a, b, (((ca,), (cb,)), ((), ())), precision=_HI, preferred_element_type=F32)
    return lax.dot_general(a.astype(BF16), b.astype(BF16), (((ca,), (cb,)), ((), ())), preferred_element_type=F32)


def _make_mm(exact):
    @jax.custom_vjp
    def mm(a, b):
        return _dg(a, b, 1, 0, exact)

    @jax.custom_vjp
    def mm_nt(a, b):
        return _dg(a, b, 1, 1, exact)

    @jax.custom_vjp
    def mm_tn(a, b):
        return _dg(a, b, 0, 0, exact)

    mm.defvjp(lambda a, b: (mm(a, b), (a, b)), lambda r, g: (mm_nt(g, r[1]), mm_tn(r[0], g)))
    mm_nt.defvjp(lambda a, b: (mm_nt(a, b), (a, b)), lambda r, g: (mm(g, r[1]), mm_tn(g, r[0])))
    mm_tn.defvjp(lambda a, b: (mm_tn(a, b), (a, b)), lambda r, g: (mm_nt(r[1], g), mm(r[0], g)))
    return mm, mm_nt, mm_tn


mm, mm_nt, mm_tn = _make_mm(False)
mmx, mmx_nt, mmx_tn = _make_mm(True)


def _sigmoid(x):
    return 1.0 / (1.0 + jnp.exp(-x))


def _silu(x):
    return x * _sigmoid(x)


def _softplus(x):
    neg_abs = jnp.where(x > 0, -x, x)
    return jnp.where(x > 0, x, 0.0) + jnp.log(1.0 + jnp.exp(neg_abs))


def _gelu(x):
    return 0.5 * x * (1.0 + jnp.tanh(0.7978845608028654 * (x + 0.044715 * (x * x * x))))


def _rms_fwd(x, g):
    r = lax.rsqrt(jnp.mean(x * x, axis=-1, keepdims=True) + EPS)
    xh = x * r
    return xh * g, xh, r


def _rms_bwd(dh, xh, r, g):
    dxh = dh * g
    dx = r * (dxh - xh * jnp.mean(dxh * xh, axis=-1, keepdims=True))
    return dx, jnp.sum(dh * xh, axis=0, keepdims=True)


def _acc_out(ref, first, val):
    @pl.when(first)
    def _():
        ref[...] = val

    @pl.when(jnp.logical_not(first))
    def _():
        ref[...] += val


def _ffn_fwd(x, nw, wg, wu, wd, tgt=None, fnw=None, *, name):
    t = x.shape[0]
    tm, fb = _tm(t), FF_BLOCK_FWD
    n_t, n_f = t // tm, D_FF // fb
    with_loss = tgt is not None

    def body(*refs):
        if with_loss:
            x_ref, nw_ref, wg_ref, wu_ref, wd_ref, tgt_ref, fnw_ref, dy_ref, loss_ref, dfn_ref, h_s, acc_s = refs
        else:
            x_ref, nw_ref, wg_ref, wu_ref, wd_ref, y_ref, h_s, acc_s = refs
        i, j = pl.program_id(0), pl.program_id(1)

        @pl.when(j == 0)
        def _():
            h, _, _ = _rms_fwd(x_ref[...], nw_ref[...])
            h_s[...] = h.astype(BF16)
            acc_s[...] = jnp.zeros_like(acc_s)

        h = h_s[...]
        g = jnp.dot(h, wg_ref[...], preferred_element_type=F32)
        u = jnp.dot(h, wu_ref[...], preferred_element_type=F32)
        a = _silu(g) * u
        acc_s[...] += jnp.dot(a.astype(BF16), wd_ref[...], preferred_element_type=F32)

        @pl.when(j == n_f - 1)
        def _():
            y = x_ref[...] + 0.5 * acc_s[...]
            if not with_loss:
                y_ref[...] = y
            else:
                gf = fnw_ref[...]
                out, xh, r = _rms_fwd(y, gf)
                err = out - tgt_ref[...]
                part = 0.5 * jnp.sum(jnp.mean(err * err, axis=-1, keepdims=True), axis=0, keepdims=True)
                d_out = err * (1.0 / D_MODEL)
                dy, dgf = _rms_bwd(d_out, xh, r, gf)
                dy_ref[...] = dy
                _acc_out(loss_ref, i == 0, jnp.broadcast_to(part, loss_ref.shape))
                _acc_out(dfn_ref, i == 0, dgf)

    row = lambda i, j: (i, 0)
    const = lambda i, j: (0, 0)
    in_specs = [
        pl.BlockSpec((tm, D_MODEL), row),
        pl.BlockSpec((1, D_MODEL), const),
        pl.BlockSpec((D_MODEL, fb), lambda i, j: (0, j)),
        pl.BlockSpec((D_MODEL, fb), lambda i, j: (0, j)),
        pl.BlockSpec((fb, D_MODEL), lambda i, j: (j, 0)),
    ]
    args = [x, nw, wg, wu, wd]
    if with_loss:
        in_specs += [pl.BlockSpec((tm, D_MODEL), row), pl.BlockSpec((1, D_MODEL), const)]
        args += [tgt, fnw]
        out_shape = (jax.ShapeDtypeStruct((t, D_MODEL), F32), jax.ShapeDtypeStruct((8, LANES), F32),
                     jax.ShapeDtypeStruct((1, D_MODEL), F32))
        out_specs = (pl.BlockSpec((tm, D_MODEL), row), pl.BlockSpec((8, LANES), const), pl.BlockSpec((1, D_MODEL), const))
        sem = ("arbitrary", "arbitrary")
    else:
        out_shape = jax.ShapeDtypeStruct((t, D_MODEL), F32)
        out_specs = pl.BlockSpec((tm, D_MODEL), row)
        sem = ("parallel", "arbitrary")
    return pl.pallas_call(
        body, name=name, grid=(n_t, n_f), in_specs=in_specs, out_specs=out_specs, out_shape=out_shape,
        scratch_shapes=[pltpu.VMEM((tm, D_MODEL), BF16), pltpu.VMEM((tm, D_MODEL), F32)],
        compiler_params=_cparams(sem),
    )(*args)


def _ffn_bwd(x, nw, wg, wu, wd, dy, *, name):
    t = x.shape[0]
    tm, fb = _tm(t), FF_BLOCK_BWD
    n_t, n_f = t // tm, D_FF // fb

    def body(x_ref, nw_ref, wg_ref, wu_ref, wd_ref, dy_ref, dx_ref, dnw_ref, dwg_hbm, dwu_hbm, dwd_hbm,
             h_s, r_s, dyh_s, ag_s, au_s, ad_s, sem):
        i, j = pl.program_id(0), pl.program_id(1)

        @pl.when(j == 0)
        def _():
            h, _, r = _rms_fwd(x_ref[...], nw_ref[...])
            h_s[...] = h.astype(BF16)
            r_s[...] = r
            dyh_s[...] = (0.5 * dy_ref[...]).astype(BF16)
            dx_ref[...] = jnp.zeros_like(dx_ref)

        h = h_s[...]
        dyh = dyh_s[...]
        wg_j, wu_j, wd_j = wg_ref[...], wu_ref[...], wd_ref[...]
        g = jnp.dot(h, wg_j, preferred_element_type=F32)
        u = jnp.dot(h, wu_j, preferred_element_type=F32)
        s = _sigmoid(g)
        gs = g * s
        da = lax.dot_general(dyh, wd_j, (((1,), (1,)), ((), ())), preferred_element_type=F32)
        dg = (da * u * (s + gs * (1.0 - s))).astype(BF16)
        du = (da * gs).astype(BF16)
        a = (gs * u).astype(BF16)
        tn = (((0,), (0,)), ((), ()))
        c_d = lax.dot_general(a, dyh, tn, preferred_element_type=F32)
        c_g = lax.dot_general(h, dg, tn, preferred_element_type=F32)
        c_u = lax.dot_general(h, du, tn, preferred_element_type=F32)

        @pl.when(i == 0)
        def _():
            ad_s[j] = c_d
            ag_s[j] = c_g
            au_s[j] = c_u

        @pl.when(i > 0)
        def _():
            ad_s[j] += c_d
            ag_s[j] += c_g
            au_s[j] += c_u

        nt = (((1,), (1,)), ((), ()))
        dx_ref[...] += (lax.dot_general(dg, wg_j, nt, preferred_element_type=F32)
                        + lax.dot_general(du, wu_j, nt, preferred_element_type=F32))

        @pl.when(j == n_f - 1)
        def _():
            r = r_s[...]
            dx, dnw = _rms_bwd(dx_ref[...], x_ref[...] * r, r, nw_ref[...])
            dx_ref[...] = dy_ref[...] + dx
            _acc_out(dnw_ref, i == 0, dnw)

        @pl.when(jnp.logical_and(i == n_t - 1, j == n_f - 1))
        def _():
            copies = [pltpu.make_async_copy(ag_s, dwg_hbm, sem.at[0]), pltpu.make_async_copy(au_s, dwu_hbm, sem.at[1]),
                      pltpu.make_async_copy(ad_s, dwd_hbm, sem.at[2])]
            for cp in copies:
                cp.start()
            for cp in copies:
                cp.wait()

    row = lambda i, j: (i, 0)
    const = lambda i, j: (0, 0)
    hbm = pl.BlockSpec(memory_space=pl.ANY)
    return pl.pallas_call(
        body, name=name, grid=(n_t, n_f),
        in_specs=[
            pl.BlockSpec((tm, D_MODEL), row),
            pl.BlockSpec((1, D_MODEL), const),
            pl.BlockSpec((D_MODEL, fb), lambda i, j: (0, j)),
            pl.BlockSpec((D_MODEL, fb), lambda i, j: (0, j)),
            pl.BlockSpec((fb, D_MODEL), lambda i, j: (j, 0)),
            pl.BlockSpec((tm, D_MODEL), row),
        ],
        out_specs=(pl.BlockSpec((tm, D_MODEL), row), pl.BlockSpec((1, D_MODEL), const), hbm, hbm, hbm),
        out_shape=(
            jax.ShapeDtypeStruct((t, D_MODEL), F32), jax.ShapeDtypeStruct((1, D_MODEL), F32),
            jax.ShapeDtypeStruct((n_f, D_MODEL, fb), F32), jax.ShapeDtypeStruct((n_f, D_MODEL, fb), F32),
            jax.ShapeDtypeStruct((n_f, fb, D_MODEL), F32),
        ),
        scratch_shapes=[
            pltpu.VMEM((tm, D_MODEL), BF16), pltpu.VMEM((tm, 1), F32), pltpu.VMEM((tm, D_MODEL), BF16),
            pltpu.VMEM((n_f, D_MODEL, fb), F32), pltpu.VMEM((n_f, D_MODEL, fb), F32), pltpu.VMEM((n_f, fb, D_MODEL), F32),
            pltpu.SemaphoreType.DMA((3,)),
        ],
        compiler_params=_cparams(("arbitrary", "arbitrary")),
    )(x, nw, wg, wu, wd, dy)


_PROJ_WIDTHS = (SG_WIDTH, SG_WIDTH, 3 * DN_WIDTH, DN_WIDTH, LANES, LANES)


def _mix_in_fwd(x, nw, ws, *, name):
    t = x.shape[0]
    tm = _tm(t)

    def body(x_ref, nw_ref, *refs):
        w_refs, o_refs = refs[:6], refs[6:]
        h, _, _ = _rms_fwd(x_ref[...], nw_ref[...])
        h = h.astype(BF16)
        for w_ref, o_ref in zip(w_refs, o_refs):
            o_ref[...] = jnp.dot(h, w_ref[...], preferred_element_type=F32)

    row = lambda i: (i, 0)
    const = lambda i: (0, 0)
    return pl.pallas_call(
        body, name=name, grid=(t // tm,),
        in_specs=[pl.BlockSpec((tm, D_MODEL), row), pl.BlockSpec((1, D_MODEL), const)]
        + [pl.BlockSpec((D_MODEL, n), const) for n in _PROJ_WIDTHS],
        out_specs=tuple(pl.BlockSpec((tm, n), row) for n in _PROJ_WIDTHS),
        out_shape=tuple(jax.ShapeDtypeStruct((t, n), F32) for n in _PROJ_WIDTHS),
        compiler_params=_cparams(("parallel",)),
    )(x, nw, *ws)


def _mix_in_bwd(x, nw, ws, dres, dps, *, name):
    t = x.shape[0]
    tm = _tm(t, 256)

    def body(x_ref, nw_ref, dres_ref, *refs):
        w_refs, dp_refs, dx_ref, dnw_ref, dw_refs = refs[:6], refs[6:12], refs[12], refs[13], refs[14:]
        i = pl.program_id(0)
        hf, xh, r = _rms_fwd(x_ref[...], nw_ref[...])
        h = hf.astype(BF16)
        dh = jnp.zeros((tm, D_MODEL), F32)
        for w_ref, dp_ref, dw_ref in zip(w_refs, dp_refs, dw_refs):
            dp = dp_ref[...].astype(BF16)
            dh = dh + lax.dot_general(dp, w_ref[...], (((1,), (1,)), ((), ())), preferred_element_type=F32)
            _acc_out(dw_ref, i == 0, lax.dot_general(h, dp, (((0,), (0,)), ((), ())), preferred_element_type=F32))
        dx, dnw = _rms_bwd(dh, xh, r, nw_ref[...])
        dx_ref[...] = dres_ref[...] + dx
        _acc_out(dnw_ref, i == 0, dnw)

    row = lambda i: (i, 0)
    const = lambda i: (0, 0)
    return pl.pallas_call(
        body, name=name, grid=(t // tm,),
        in_specs=[pl.BlockSpec((tm, D_MODEL), row), pl.BlockSpec((1, D_MODEL), const), pl.BlockSpec((tm, D_MODEL), row)]
        + [pl.BlockSpec((D_MODEL, n), const) for n in _PROJ_WIDTHS]
        + [pl.BlockSpec((tm, n), row) for n in _PROJ_WIDTHS],
        out_specs=(pl.BlockSpec((tm, D_MODEL), row), pl.BlockSpec((1, D_MODEL), const))
        + tuple(pl.BlockSpec((D_MODEL, n), const) for n in _PROJ_WIDTHS),
        out_shape=(jax.ShapeDtypeStruct((t, D_MODEL), F32), jax.ShapeDtypeStruct((1, D_MODEL), F32))
        + tuple(jax.ShapeDtypeStruct((D_MODEL, n), F32) for n in _PROJ_WIDTHS),
        compiler_params=_cparams(("arbitrary",)),
    )(x, nw, dres, *ws, *dps)


def _sg_fn(u, v, lng, lnb, wcs, sgbt):
    lane = lax.broadcasted_iota(jnp.int32, (1, SG_WIDTH), 1)
    lane_b = lax.broadcasted_iota(jnp.int32, (1, LANES), 1)
    rr = lax.broadcasted_iota(jnp.int32, (SG_CHUNK, SG_CHUNK), 0)
    cc = lax.broadcasted_iota(jnp.int32, (SG_CHUNK, SG_CHUNK), 1)
    gu, gv = _gelu(u), _gelu(v)
    mu = jnp.mean(gv, axis=-1, keepdims=True)
    cen = gv - mu
    var = jnp.mean(cen * cen, axis=-1, keepdims=True)
    ln = cen * lax.rsqrt(var + EPS) * lng + lnb
    vs = jnp.zeros_like(u)
    for g in range(SG_GROUPS):
        in_group = jnp.logical_and(lane >= g * SG_GROUP_DIM, lane < (g + 1) * SG_GROUP_DIM)
        w_causal = jnp.where(rr >= cc, wcs[g], 0.0)
        bias = jnp.sum(jnp.where(lane_b == g, sgbt, 0.0), axis=1, keepdims=True)
        vs = vs + jnp.where(in_group, mm(w_causal, ln) + bias, 0.0)
    return gu * vs


def _sg_fwd(u, v, lng, lnb, wc, sgbt, *, name):
    t = u.shape[0]
    tm = _tm(t)

    def body(u_ref, v_ref, lng_ref, lnb_ref, wc_ref, sgbt_ref, o_ref):
        wcs = [wc_ref[g] for g in range(SG_GROUPS)]
        for c in range(tm // SG_CHUNK):
            rows = pl.ds(c * SG_CHUNK, SG_CHUNK)
            o_ref[rows, :] = _sg_fn(u_ref[rows, :], v_ref[rows, :], lng_ref[...], lnb_ref[...], wcs, sgbt_ref[...])

    row = lambda i: (i, 0)
    const = lambda i: (0, 0)
    return pl.pallas_call(
        body, name=name, grid=(t // tm,),
        in_specs=[pl.BlockSpec((tm, SG_WIDTH), row), pl.BlockSpec((tm, SG_WIDTH), row),
                  pl.BlockSpec((1, SG_WIDTH), const), pl.BlockSpec((1, SG_WIDTH), const),
                  pl.BlockSpec((SG_GROUPS, SG_CHUNK, SG_CHUNK), lambda i: (0, 0, 0)), pl.BlockSpec((SG_CHUNK, LANES), const)],
        out_specs=pl.BlockSpec((tm, SG_WIDTH), row),
        out_shape=jax.ShapeDtypeStruct((t, SG_WIDTH), F32),
        compiler_params=_cparams(("parallel",)),
    )(u, v, lng, lnb, wc, sgbt)


def _sg_bwd(u, v, lng, lnb, wc, sgbt, dout, *, name):
    t = u.shape[0]
    tm = _tm(t)

    def body(u_ref, v_ref, lng_ref, lnb_ref, wc_ref, sgbt_ref, do_ref, du_ref, dv_ref, dlng_ref, dlnb_ref, dwc_ref, dsgbt_ref):
        i = pl.program_id(0)
        wcs = [wc_ref[g] for g in range(SG_GROUPS)]
        tot = None
        for c in range(tm // SG_CHUNK):
            rows = pl.ds(c * SG_CHUNK, SG_CHUNK)
            _, vjp = jax.vjp(_sg_fn, u_ref[rows, :], v_ref[rows, :], lng_ref[...], lnb_ref[...], wcs, sgbt_ref[...])
            du, dv, dlng, dlnb, dwcs, dsgbt = vjp(do_ref[rows, :])
            du_ref[rows, :] = du
            dv_ref[rows, :] = dv
            part = (dlng, dlnb, dwcs, dsgbt)
            tot = part if tot is None else jax.tree.map(jnp.add, tot, part)
        dlng, dlnb, dwcs, dsgbt = tot
        _acc_out(dlng_ref, i == 0, dlng)
        _acc_out(dlnb_ref, i == 0, dlnb)
        _acc_out(dsgbt_ref, i == 0, dsgbt)
        for g in range(SG_GROUPS):
            @pl.when(i == 0)
            def _(g=g):
                dwc_ref[g] = dwcs[g]

            @pl.when(i > 0)
            def _(g=g):
                dwc_ref[g] += dwcs[g]

    row = lambda i: (i, 0)
    const = lambda i: (0, 0)
    wspec = pl.BlockSpec((SG_GROUPS, SG_CHUNK, SG_CHUNK), lambda i: (0, 0, 0))
    return pl.pallas_call(
        body, name=name, grid=(t // tm,),
        in_specs=[pl.BlockSpec((tm, SG_WIDTH), row), pl.BlockSpec((tm, SG_WIDTH), row),
                  pl.BlockSpec((1, SG_WIDTH), const), pl.BlockSpec((1, SG_WIDTH), const), wspec,
                  pl.BlockSpec((SG_CHUNK, LANES), const), pl.BlockSpec((tm, SG_WIDTH), row)],
        out_specs=(pl.BlockSpec((tm, SG_WIDTH), row), pl.BlockSpec((tm, SG_WIDTH), row),
                   pl.BlockSpec((1, SG_WIDTH), const), pl.BlockSpec((1, SG_WIDTH), const), wspec,
                   pl.BlockSpec((SG_CHUNK, LANES), const)),
        out_shape=(jax.ShapeDtypeStruct((t, SG_WIDTH), F32), jax.ShapeDtypeStruct((t, SG_WIDTH), F32),
                   jax.ShapeDtypeStruct((1, SG_WIDTH), F32), jax.ShapeDtypeStruct((1, SG_WIDTH), F32),
                   jax.ShapeDtypeStruct((SG_GROUPS, SG_CHUNK, SG_CHUNK), F32), jax.ShapeDtypeStruct((SG_CHUNK, LANES), F32)),
        compiler_params=_cparams(("arbitrary",)),
    )(u, v, lng, lnb, wc, sgbt, dout)


def _conv_taps(ext, w, tm):
    y = None
    for j in range(CONV_K):
        s = CONV_K - 1 - j
        shifted = ext if s == 0 else pltpu.roll(ext, s, 0)
        term = w[j:j + 1, :] * shifted[HALO:HALO + tm, :]
        y = term if y is None else y + term
    return y


def _post_conv(yq, yk, yv, bpre, apre, alog, dtb):
    def l2(a):
        return a * lax.rsqrt(jnp.sum(a * a, axis=-1, keepdims=True) + EPS)

    q = [l2(_silu(a)) for a in yq]
    k = [l2(_silu(a)) for a in yk]
    return q, k, _silu(yv), _sigmoid(bpre), -jnp.exp(alog) * _softplus(apre + dtb)


def _chunk_tril(tm):
    rr = lax.broadcasted_iota(jnp.int32, (tm, tm), 0)
    cc = lax.broadcasted_iota(jnp.int32, (tm, tm), 1)
    shift = DN_CHUNK.bit_length() - 1
    same = jnp.right_shift(rr, shift) == jnp.right_shift(cc, shift)
    return jnp.where(jnp.logical_and(same, rr >= cc), 1.0, 0.0).astype(F32)


def _halo_specs(tm, width, n_blocks_seq, n_blocks):
    per = tm // HALO
    prev = pl.BlockSpec((HALO, width), lambda i: (jnp.maximum(i * per - 1, 0), 0))
    nxt = pl.BlockSpec((HALO, width), lambda i: (jnp.minimum((i + 1) * per, n_blocks * per - 1), 0))
    return prev, nxt


def _split_heads(ref, base):
    return [ref[:, base + h * DN_HEAD_DIM: base + (h + 1) * DN_HEAD_DIM] for h in range(DN_HEADS)]


def _dn_prep_fwd(qkv, bpre, apre, conv_w, alog, dtb, seq, *, name):
    t = qkv.shape[0]
    tm = _tm(t)
    bps = seq // tm
    cw = 3 * DN_WIDTH

    def body(x_ref, halo_ref, b_ref, a_ref, w_ref, alog_ref, dtb_ref, q_ref, k_ref, v_ref, beta_ref, gc_ref):
        i = pl.program_id(0)
        keep = jnp.where(i % bps == 0, 0.0, 1.0)
        ext = jnp.concatenate([halo_ref[...] * keep, x_ref[...]], axis=0)
        y = _conv_taps(ext, w_ref[...], tm)
        yq = [y[:, h * DN_HEAD_DIM:(h + 1) * DN_HEAD_DIM] for h in range(DN_HEADS)]
        yk = [y[:, DN_WIDTH + h * DN_HEAD_DIM: DN_WIDTH + (h + 1) * DN_HEAD_DIM] for h in range(DN_HEADS)]
        q, k, v, beta, g = _post_conv(yq, yk, y[:, 2 * DN_WIDTH:], b_ref[...], a_ref[...], alog_ref[...], dtb_ref[...])
        for h in range(DN_HEADS):
            q_ref[:, h * DN_HEAD_DIM:(h + 1) * DN_HEAD_DIM] = q[h]
            k_ref[:, h * DN_HEAD_DIM:(h + 1) * DN_HEAD_DIM] = k[h]
        v_ref[...] = v
        beta_ref[...] = beta
        gc_ref[...] = mmx(_chunk_tril(tm), g)

    row = lambda i: (i, 0)
    const = lambda i: (0, 0)
    prev, _ = _halo_specs(tm, cw, bps, t // tm)
    return pl.pallas_call(
        body, name=name, grid=(t // tm,),
        in_specs=[pl.BlockSpec((tm, cw), row), prev, pl.BlockSpec((tm, LANES), row), pl.BlockSpec((tm, LANES), row),
                  pl.BlockSpec((CONV_K, cw), const), pl.BlockSpec((1, LANES), const), pl.BlockSpec((1, LANES), const)],
        out_specs=tuple(pl.BlockSpec((tm, n), row) for n in (DN_WIDTH, DN_WIDTH, DN_WIDTH, LANES, LANES)),
        out_shape=tuple(jax.ShapeDtypeStruct((t, n), F32) for n in (DN_WIDTH, DN_WIDTH, DN_WIDTH, LANES, LANES)),
        compiler_params=_cparams(("parallel",)),
    )(qkv, qkv, bpre, apre, conv_w, alog, dtb)


def _dn_prep_bwd(qkv, bpre, apre, conv_w, alog, dtb, dq, dk, dv, dbeta, dgc, dgc2, seq, *, name):
    t = qkv.shape[0]
    tm = _tm(t)
    bps = seq // tm
    cw = 3 * DN_WIDTH

    def body(x_ref, halo_ref, b_ref, a_ref, w_ref, alog_ref, dtb_ref, dq_ref, dk_ref, dv_ref, dbeta_ref, dgc_ref, dgc2_ref,
             dy_ref, db_ref, da_ref, dalog_ref, ddtb_ref):
        i = pl.program_id(0)
        keep = jnp.where(i % bps == 0, 0.0, 1.0)
        ext = jnp.concatenate([halo_ref[...] * keep, x_ref[...]], axis=0)
        y = _conv_taps(ext, w_ref[...], tm)
        yq = [y[:, h * DN_HEAD_DIM:(h + 1) * DN_HEAD_DIM] for h in range(DN_HEADS)]
        yk = [y[:, DN_WIDTH + h * DN_HEAD_DIM: DN_WIDTH + (h + 1) * DN_HEAD_DIM] for h in range(DN_HEADS)]
        _, vjp = jax.vjp(_post_conv, yq, yk, y[:, 2 * DN_WIDTH:], b_ref[...], a_ref[...], alog_ref[...], dtb_ref[...])
        dg = mmx_tn(_chunk_tril(tm), dgc_ref[...] + dgc2_ref[...])
        dyq, dyk, dyv, db, da, dalog, ddtb = vjp((_split_heads(dq_ref, 0), _split_heads(dk_ref, 0), dv_ref[...],
                                                  dbeta_ref[...], dg))
        for h in range(DN_HEADS):
            dy_ref[:, h * DN_HEAD_DIM:(h + 1) * DN_HEAD_DIM] = dyq[h]
            dy_ref[:, DN_WIDTH + h * DN_HEAD_DIM: DN_WIDTH + (h + 1) * DN_HEAD_DIM] = dyk[h]
        dy_ref[:, 2 * DN_WIDTH:] = dyv
        db_ref[...] = db
        da_ref[...] = da
        _acc_out(dalog_ref, i == 0, dalog)
        _acc_out(ddtb_ref, i == 0, ddtb)

    row = lambda i: (i, 0)
    const = lambda i: (0, 0)
    prev, _ = _halo_specs(tm, cw, bps, t // tm)
    return pl.pallas_call(
        body, name=name, grid=(t // tm,),
        in_specs=[pl.BlockSpec((tm, cw), row), prev, pl.BlockSpec((tm, LANES), row), pl.BlockSpec((tm, LANES), row),
                  pl.BlockSpec((CONV_K, cw), const), pl.BlockSpec((1, LANES), const), pl.BlockSpec((1, LANES), const),
                  pl.BlockSpec((tm, DN_WIDTH), row), pl.BlockSpec((tm, DN_WIDTH), row), pl.BlockSpec((tm, DN_WIDTH), row),
                  pl.BlockSpec((tm, LANES), row), pl.BlockSpec((tm, LANES), row), pl.BlockSpec((tm, LANES), row)],
        out_specs=(pl.BlockSpec((tm, cw), row), pl.BlockSpec((tm, LANES), row), pl.BlockSpec((tm, LANES), row),
                   pl.BlockSpec((1, LANES), const), pl.BlockSpec((1, LANES), const)),
        out_shape=(jax.ShapeDtypeStruct((t, cw), F32), jax.ShapeDtypeStruct((t, LANES), F32), jax.ShapeDtypeStruct((t, LANES), F32),
                   jax.ShapeDtypeStruct((1, LANES), F32), jax.ShapeDtypeStruct((1, LANES), F32)),
        compiler_params=_cparams(("arbitrary",)),
    )(qkv, qkv, bpre, apre, conv_w, alog, dtb, dq, dk, dv, dbeta, dgc, dgc2)


def _conv_bwd(qkv, dy, conv_w, seq, *, name):
    t = qkv.shape[0]
    tm = _tm(t)
    bps = seq // tm
    cw = 3 * DN_WIDTH
    n_ext = tm + HALO

    def body(x_ref, halo_ref, dy_ref, dyn_ref, w_ref, dx_ref, dw_ref):
        i = pl.program_id(0)
        keep_prev = jnp.where(i % bps == 0, 0.0, 1.0)
        keep_next = jnp.where(i % bps == bps - 1, 0.0, 1.0)
        ext = jnp.concatenate([halo_ref[...] * keep_prev, x_ref[...]], axis=0)
        dy = dy_ref[...]
        dyext = jnp.concatenate([dy, dyn_ref[...] * keep_next], axis=0)
        w = w_ref[...]

        @pl.when(i == 0)
        def _():
            dw_ref[...] = jnp.zeros_like(dw_ref)

        dx = None
        for j in range(CONV_K):
            s = CONV_K - 1 - j
            fut = dyext if s == 0 else pltpu.roll(dyext, n_ext - s, 0)
            term = w[j:j + 1, :] * fut[0:tm, :]
            dx = term if dx is None else dx + term
            past = ext if s == 0 else pltpu.roll(ext, s, 0)
            dw_ref[j:j + 1, :] += jnp.sum(dy * past[HALO:HALO + tm, :], axis=0, keepdims=True)
        dx_ref[...] = dx

    row = lambda i: (i, 0)
    const = lambda i: (0, 0)
    prev, nxt = _halo_specs(tm, cw, bps, t // tm)
    return pl.pallas_call(
        body, name=name, grid=(t // tm,),
        in_specs=[pl.BlockSpec((tm, cw), row), prev, pl.BlockSpec((tm, cw), row), nxt, pl.BlockSpec((CONV_K, cw), const)],
        out_specs=(pl.BlockSpec((tm, cw), row), pl.BlockSpec((HALO, cw), const)),
        out_shape=(jax.ShapeDtypeStruct((t, cw), F32), jax.ShapeDtypeStruct((HALO, cw), F32)),
        compiler_params=_cparams(("arbitrary",)),
    )(qkv, qkv, dy, dy, conv_w)


def _inv_unit_lower(l_mat, eye):
    inv = eye - l_mat
    power = l_mat
    n = 2
    while n < l_mat.shape[0]:
        power = mmx(power, power)
        inv = inv + mmx(inv, power)
        n *= 2
    return inv


@jax.custom_vjp
def _solve(l_mat, rhs, inv):
    return mmx(inv, rhs)


def _solve_fwd(l_mat, rhs, inv):
    sol = mmx(inv, rhs)
    return sol, (inv, sol)


def _solve_bwd(res, d_sol):
    inv, sol = res
    d_rhs = mmx_tn(inv, d_sol)
    return -mmx_nt(d_rhs, sol), d_rhs, jnp.zeros_like(inv)


_solve.defvjp(_solve_fwd, _solve_bwd)


def _prep_fn(q, k, v, gc, gr, b, inv):
    c = q.shape[0]
    rr = lax.broadcasted_iota(jnp.int32, (c, c), 0)
    cc = lax.broadcasted_iota(jnp.int32, (c, c), 1)
    incl, strict = rr >= cc, rr > cc
    qs = q * (DN_HEAD_DIM ** -0.5)
    decay = jnp.where(incl, jnp.exp(jnp.where(incl, gc - gr, 0.0)), 0.0)
    kb, vb = k * b, v * b
    l_mat = jnp.where(strict, mm_nt(kb, k) * decay, 0.0)
    eg = jnp.exp(gc)
    if inv is None:
        inv = _inv_unit_lower(l_mat, jnp.where(rr == cc, 1.0, 0.0).astype(F32))
    u_wy = _solve(l_mat, vb, inv)
    w_wy = _solve(l_mat, kb * eg, inv)
    qk = mm_nt(qs, k) * decay
    is_last = lax.broadcasted_iota(jnp.int32, (c, 1), 0) == c - 1
    g_last = jnp.sum(jnp.where(is_last, gc, 0.0), axis=0, keepdims=True)
    k_dec = k * jnp.exp(g_last - gc)
    egl = jnp.broadcast_to(jnp.exp(g_last), (1, LANES))
    return (w_wy, u_wy, qs * eg, k_dec, qk, egl), inv


def _seq_fn(w, u, qd, kd, qk, egl, s):
    v_new = u - mm(w, s)
    o = mm(qd, s) + mm(qk, v_new)
    return o, s * egl + mm_tn(kd, v_new)


def _lane_col(a, h):
    lane = lax.broadcasted_iota(jnp.int32, (1, LANES), 1)
    return jnp.sum(jnp.where(lane == h, a, 0.0), axis=1, keepdims=True)


def _col_lane(col, h):
    lane = lax.broadcasted_iota(jnp.int32, (1, LANES), 1)
    return jnp.where(lane == h, col, 0.0)


def _head_cols(h):
    return slice(h * DN_HEAD_DIM, (h + 1) * DN_HEAD_DIM)


def _chunk_rows(n):
    return pl.ds(pl.multiple_of(n * DN_CHUNK, DN_CHUNK), DN_CHUNK)


def _delta_prep(q, k, v, gc, grow, beta, *, name):
    t = q.shape[0]
    tm = _tm(t)
    cpb = tm // DN_CHUNK
    n_chunks = t // DN_CHUNK

    def body(q_ref, k_ref, v_ref, gc_ref, gr_ref, b_ref, w_ref, u_ref, qd_ref, kd_ref, qk_ref, egl_ref, inv_ref):
        def step(n, carry):
            rows = _chunk_rows(n)
            gcb, bb, grb = gc_ref[rows, :], b_ref[rows, :], gr_ref[n]
            egl_ref[n] = jnp.zeros((HALO, LANES), F32)
            for h in range(DN_HEADS):
                cols = _head_cols(h)
                (w, u, qd, kd, qk, egl), inv = _prep_fn(q_ref[rows, cols], k_ref[rows, cols], v_ref[rows, cols],
                                                        _lane_col(gcb, h), grb[h:h + 1, :], _lane_col(bb, h), None)
                w_ref[rows, cols] = w.astype(BF16)
                u_ref[rows, cols] = u
                qd_ref[rows, cols] = qd.astype(BF16)
                kd_ref[rows, cols] = kd.astype(BF16)
                qk_ref[n, h] = qk
                inv_ref[n, h] = inv
                egl_ref[n, h:h + 1, :] = egl
            return carry

        lax.fori_loop(0, cpb, step, 0)

    row = lambda i: (i, 0)
    tok = pl.BlockSpec((tm, DN_WIDTH), row)
    lanes = pl.BlockSpec((tm, LANES), row)
    sq = pl.BlockSpec((cpb, DN_HEADS, DN_CHUNK, DN_CHUNK), lambda i: (i, 0, 0, 0))
    return pl.pallas_call(
        body, name=name, grid=(t // tm,),
        in_specs=[tok, tok, tok, lanes, pl.BlockSpec((cpb, HALO, DN_CHUNK), lambda i: (i, 0, 0)), lanes],
        out_specs=(tok, tok, tok, tok, sq, pl.BlockSpec((cpb, HALO, LANES), lambda i: (i, 0, 0)), sq),
        out_shape=(jax.ShapeDtypeStruct((t, DN_WIDTH), BF16), jax.ShapeDtypeStruct((t, DN_WIDTH), F32),
                   jax.ShapeDtypeStruct((t, DN_WIDTH), BF16), jax.ShapeDtypeStruct((t, DN_WIDTH), BF16),
                   jax.ShapeDtypeStruct((n_chunks, DN_HEADS, DN_CHUNK, DN_CHUNK), F32),
                   jax.ShapeDtypeStruct((n_chunks, HALO, LANES), F32),
                   jax.ShapeDtypeStruct((n_chunks, DN_HEADS, DN_CHUNK, DN_CHUNK), F32)),
        compiler_params=_cparams(("parallel",)),
    )(q, k, v, gc, grow, beta)


def _delta_par_bwd(q, k, v, gc, grow, beta, inv, dw, du, dqd, dkd, dqk, degl, *, name):
    t = q.shape[0]
    tm = _tm(t)
    cpb = tm // DN_CHUNK
    n_chunks = t // DN_CHUNK

    def body(q_ref, k_ref, v_ref, gc_ref, gr_ref, b_ref, inv_ref, dw_ref, du_ref, dqd_ref, dkd_ref, dqk_ref, degl_ref,
             dq_ref, dk_ref, dv_ref, dgc_ref, dgr_ref, db_ref):
        def step(n, carry):
            rows = _chunk_rows(n)
            gcb, bb, grb, deglb = gc_ref[rows, :], b_ref[rows, :], gr_ref[n], degl_ref[n]
            dgr_ref[n] = jnp.zeros((HALO, DN_CHUNK), F32)
            dgc_acc = jnp.zeros((DN_CHUNK, LANES), F32)
            db_acc = jnp.zeros((DN_CHUNK, LANES), F32)
            for h in range(DN_HEADS):
                cols = _head_cols(h)
                inv = inv_ref[n, h]
                _, vjp = jax.vjp(lambda *a: _prep_fn(*a, inv)[0], q_ref[rows, cols], k_ref[rows, cols], v_ref[rows, cols],
                                 _lane_col(gcb, h), grb[h:h + 1, :], _lane_col(bb, h))
                dq, dk, dv, dgc, dgr, db = vjp((dw_ref[rows, cols], du_ref[rows, cols], dqd_ref[rows, cols],
                                                dkd_ref[rows, cols], dqk_ref[n, h], deglb[h:h + 1, :]))
                dq_ref[rows, cols] = dq
                dk_ref[rows, cols] = dk
                dv_ref[rows, cols] = dv
                dgr_ref[n, h:h + 1, :] = dgr
                dgc_acc = dgc_acc + _col_lane(dgc, h)
                db_acc = db_acc + _col_lane(db, h)
            dgc_ref[rows, :] = dgc_acc
            db_ref[rows, :] = db_acc
            return carry

        lax.fori_loop(0, cpb, step, 0)

    row = lambda i: (i, 0)
    tok = pl.BlockSpec((tm, DN_WIDTH), row)
    lanes = pl.BlockSpec((tm, LANES), row)
    sq = pl.BlockSpec((cpb, DN_HEADS, DN_CHUNK, DN_CHUNK), lambda i: (i, 0, 0, 0))
    grs = pl.BlockSpec((cpb, HALO, DN_CHUNK), lambda i: (i, 0, 0))
    return pl.pallas_call(
        body, name=name, grid=(t // tm,),
        in_specs=[tok, tok, tok, lanes, grs, lanes, sq, tok, tok, tok, tok, sq, pl.BlockSpec((cpb, HALO, LANES), lambda i: (i, 0, 0))],
        out_specs=(tok, tok, tok, lanes, grs, lanes),
        out_shape=(jax.ShapeDtypeStruct((t, DN_WIDTH), F32),) * 3
        + (jax.ShapeDtypeStruct((t, LANES), F32), jax.ShapeDtypeStruct((n_chunks, HALO, DN_CHUNK), F32),
           jax.ShapeDtypeStruct((t, LANES), F32)),
        compiler_params=_cparams(("parallel",)),
    )(q, k, v, gc, grow, beta, inv, dw, du, dqd, dkd, dqk, degl)


def _seq_specs(n_seq, seq, reverse):
    tm = _tm(seq)
    nb = seq // tm
    cpb = tm // DN_CHUNK
    blk = (lambda b, j: b * nb + nb - 1 - j) if reverse else (lambda b, j: b * nb + j)
    tok = pl.BlockSpec((tm, DN_WIDTH), lambda b, j: (blk(b, j), 0))
    sq = pl.BlockSpec((cpb, DN_HEADS, DN_CHUNK, DN_CHUNK), lambda b, j: (blk(b, j), 0, 0, 0))
    rows8 = pl.BlockSpec((cpb, HALO, LANES), lambda b, j: (blk(b, j), 0, 0))
    state = pl.BlockSpec((cpb, DN_HEADS, DN_HEAD_DIM, DN_HEAD_DIM), lambda b, j: (blk(b, j), 0, 0, 0))
    return nb, cpb, tok, sq, rows8, state


def _delta_seq_fwd(w, u, qd, kd, qk, egl, n_seq, seq, *, name):
    nb, cpb, tok, sq, rows8, state = _seq_specs(n_seq, seq, False)
    t = n_seq * seq

    def body(w_ref, u_ref, qd_ref, kd_ref, qk_ref, egl_ref, o_ref, st_ref, s_s):
        @pl.when(pl.program_id(1) == 0)
        def _():
            s_s[...] = jnp.zeros_like(s_s)

        def step(n, carry):
            rows = _chunk_rows(n)
            eglb = egl_ref[n]
            for h in range(DN_HEADS):
                cols = _head_cols(h)
                s = s_s[h]
                st_ref[n, h] = s
                o, s_new = _seq_fn(w_ref[rows, cols], u_ref[rows, cols], qd_ref[rows, cols], kd_ref[rows, cols],
                                   qk_ref[n, h], eglb[h:h + 1, :], s)
                o_ref[rows, cols] = o
                s_s[h] = s_new
            return carry

        lax.fori_loop(0, cpb, step, 0)

    return pl.pallas_call(
        body, name=name, grid=(n_seq, nb),
        in_specs=[tok, tok, tok, tok, sq, rows8],
        out_specs=(tok, state),
        out_shape=(jax.ShapeDtypeStruct((t, DN_WIDTH), F32),
                   jax.ShapeDtypeStruct((t // DN_CHUNK, DN_HEADS, DN_HEAD_DIM, DN_HEAD_DIM), F32)),
        scratch_shapes=[pltpu.VMEM((DN_HEADS, DN_HEAD_DIM, DN_HEAD_DIM), F32)],
        compiler_params=_cparams(("parallel", "arbitrary")),
    )(w, u, qd, kd, qk, egl)


def _delta_seq_bwd(w, u, qd, kd, qk, egl, states, do, n_seq, seq, *, name):
    nb, cpb, tok, sq, rows8, state = _seq_specs(n_seq, seq, True)
    t = n_seq * seq

    def body(w_ref, u_ref, qd_ref, kd_ref, qk_ref, egl_ref, st_ref, do_ref, dw_ref, du_ref, dqd_ref, dkd_ref, dqk_ref,
             degl_ref, ds_s):
        @pl.when(pl.program_id(1) == 0)
        def _():
            ds_s[...] = jnp.zeros_like(ds_s)

        def step(m, carry):
            n = cpb - 1 - m
            rows = _chunk_rows(n)
            eglb = egl_ref[n]
            degl_ref[n] = jnp.zeros((HALO, LANES), F32)
            for h in range(DN_HEADS):
                cols = _head_cols(h)
                _, vjp = jax.vjp(_seq_fn, w_ref[rows, cols].astype(F32), u_ref[rows, cols], qd_ref[rows, cols].astype(F32),
                                 kd_ref[rows, cols].astype(F32), qk_ref[n, h], eglb[h:h + 1, :], st_ref[n, h])
                dw, du, dqd, dkd, dqk, degl, ds_in = vjp((do_ref[rows, cols], ds_s[h]))
                dw_ref[rows, cols] = dw
                du_ref[rows, cols] = du
                dqd_ref[rows, cols] = dqd
                dkd_ref[rows, cols] = dkd
                dqk_ref[n, h] = dqk
                degl_ref[n, h:h + 1, :] = degl
                ds_s[h] = ds_in
            return carry

        lax.fori_loop(0, cpb, step, 0)

    return pl.pallas_call(
        body, name=name, grid=(n_seq, nb),
        in_specs=[tok, tok, tok, tok, sq, rows8, state, tok],
        out_specs=(tok, tok, tok, tok, sq, rows8),
        out_shape=(jax.ShapeDtypeStruct((t, DN_WIDTH), F32),) * 4
        + (jax.ShapeDtypeStruct((t // DN_CHUNK, DN_HEADS, DN_CHUNK, DN_CHUNK), F32),
           jax.ShapeDtypeStruct((t // DN_CHUNK, HALO, LANES), F32)),
        scratch_shapes=[pltpu.VMEM((DN_HEADS, DN_HEAD_DIM, DN_HEAD_DIM), F32)],
        compiler_params=_cparams(("parallel", "arbitrary")),
    )(w, u, qd, kd, qk, egl, states, do)


def _dn_gate(o, z, dnw):
    return o * lax.rsqrt(jnp.mean(o * o, axis=-1, keepdims=True) + EPS) * dnw * _silu(z)


def _mix_out_fwd(x, sg, o, z, wo_sg, wo_dn, dnw, *, name):
    t = x.shape[0]
    tm = _tm(t)

    def body(x_ref, sg_ref, o_ref, z_ref, wsg_ref, wdn_ref, dnw_ref, y_ref, dn_s):
        for h, (oh, zh) in enumerate(zip(_split_heads(o_ref, 0), _split_heads(z_ref, 0))):
            dn_s[:, h * DN_HEAD_DIM:(h + 1) * DN_HEAD_DIM] = _dn_gate(oh, zh, dnw_ref[...]).astype(BF16)
        y_ref[...] = (x_ref[...] + jnp.dot(sg_ref[...].astype(BF16), wsg_ref[...], preferred_element_type=F32)
                      + jnp.dot(dn_s[...], wdn_ref[...], preferred_element_type=F32))

    row = lambda i: (i, 0)
    const = lambda i: (0, 0)
    half = pl.BlockSpec((tm, DN_WIDTH), row)
    return pl.pallas_call(
        body, name=name, grid=(t // tm,),
        in_specs=[pl.BlockSpec((tm, D_MODEL), row), half, half, half, pl.BlockSpec((SG_WIDTH, D_MODEL), const),
                  pl.BlockSpec((DN_WIDTH, D_MODEL), const), pl.BlockSpec((1, DN_HEAD_DIM), const)],
        out_specs=pl.BlockSpec((tm, D_MODEL), row),
        out_shape=jax.ShapeDtypeStruct((t, D_MODEL), F32),
        scratch_shapes=[pltpu.VMEM((tm, DN_WIDTH), BF16)],
        compiler_params=_cparams(("parallel",)),
    )(x, sg, o, z, wo_sg, wo_dn, dnw)


def _mix_out_bwd(dy, sg, o, z, wo_sg, wo_dn, dnw, *, name):
    t = dy.shape[0]
    tm = _tm(t)

    def body(dy_ref, sg_ref, o_ref, z_ref, wsg_ref, wdn_ref, dnw_ref, dsg_ref, do_ref, dz_ref, dwsg_ref, dwdn_ref, ddnw_ref, dn_s):
        i = pl.program_id(0)
        dyb = dy_ref[...].astype(BF16)
        nt = (((1,), (1,)), ((), ()))
        tn = (((0,), (0,)), ((), ()))
        dsg_ref[...] = lax.dot_general(dyb, wsg_ref[...], nt, preferred_element_type=F32)
        ddn = lax.dot_general(dyb, wdn_ref[...], nt, preferred_element_type=F32)
        ddnw = None
        for h, (oh, zh) in enumerate(zip(_split_heads(o_ref, 0), _split_heads(z_ref, 0))):
            cols = slice(h * DN_HEAD_DIM, (h + 1) * DN_HEAD_DIM)
            out, vjp = jax.vjp(_dn_gate, oh, zh, dnw_ref[...])
            dn_s[:, cols] = out.astype(BF16)
            doh, dzh, dw = vjp(ddn[:, cols])
            do_ref[:, cols] = doh
            dz_ref[:, cols] = dzh
            ddnw = dw if ddnw is None else ddnw + dw
        _acc_out(ddnw_ref, i == 0, ddnw)
        _acc_out(dwsg_ref, i == 0, lax.dot_general(sg_ref[...].astype(BF16), dyb, tn, preferred_element_type=F32))
        _acc_out(dwdn_ref, i == 0, lax.dot_general(dn_s[...], dyb, tn, preferred_element_type=F32))

    row = lambda i: (i, 0)
    const = lambda i: (0, 0)
    half = pl.BlockSpec((tm, DN_WIDTH), row)
    wspec = pl.BlockSpec((DN_WIDTH, D_MODEL), const)
    return pl.pallas_call(
        body, name=name, grid=(t // tm,),
        in_specs=[pl.BlockSpec((tm, D_MODEL), row), half, half, half, wspec, wspec, pl.BlockSpec((1, DN_HEAD_DIM), const)],
        out_specs=(half, half, half, wspec, wspec, pl.BlockSpec((1, DN_HEAD_DIM), const)),
        out_shape=(jax.ShapeDtypeStruct((t, DN_WIDTH), F32),) * 3 + (jax.ShapeDtypeStruct((DN_WIDTH, D_MODEL), F32),) * 2
        + (jax.ShapeDtypeStruct((1, DN_HEAD_DIM), F32),),
        scratch_shapes=[pltpu.VMEM((tm, DN_WIDTH), BF16)],
        compiler_params=_cparams(("arbitrary",)),
    )(dy, sg, o, z, wo_sg, wo_dn, dnw)


_MESH = pl.DeviceIdType.MESH
_HBM = pl.BlockSpec(memory_space=pl.ANY)


def _mesh_pos():
    x, y, c = lax.axis_index("x"), lax.axis_index("y"), lax.axis_index("c")
    return x, y, c, [(1 - x, y), (x, 1 - y), (1 - x, 1 - y)]


def _gather2(arrs, *, name):
    n = len(arrs)
    slots = N_DEV - 1

    def body(*refs):
        in_refs, out_refs = refs[:n], refs[n:2 * n]
        send_sems, recv_sems, local_sems = refs[2 * n:]
        x, y, c, chips = _mesh_pos()
        me, sibling = (x, y, c), (x, y, 1 - c)

        def copy(k, slot, block, to, src=None):
            dst = out_refs[k].at[4 * block[0] + 2 * block[1] + block[2]]
            return pltpu.make_async_remote_copy(src_ref=dst if src is None else src, dst_ref=dst,
                                                send_sem=send_sems.at[k * slots + slot], recv_sem=recv_sems.at[k * slots + slot],
                                                device_id=to, device_id_type=_MESH)

        local = [pltpu.make_async_copy(in_refs[k], out_refs[k].at[4 * x + 2 * y + c], local_sems.at[k]) for k in range(n)]
        sent = []
        for k in range(n):
            sent.append(copy(k, 0, me, sibling, src=in_refs[k]))
            sent += [copy(k, 1 + j, me, (*chip, c), src=in_refs[k]) for j, chip in enumerate(chips)]
        for cp in local + sent:
            cp.start()
        for j, chip in enumerate(chips):
            for k in range(n):
                copy(k, 1 + j, (*chip, c), me).wait_recv()
                passed = copy(k, 4 + j, (*chip, c), sibling)
                passed.start()
                sent.append(passed)
        for k in range(n):
            copy(k, 0, sibling, me).wait_recv()
            for j, chip in enumerate(chips):
                copy(k, 4 + j, (*chip, 1 - c), me).wait_recv()
        for cp in sent:
            cp.wait_send()
        for cp in local:
            cp.wait()

    return pl.pallas_call(
        body, name=name, in_specs=[_HBM] * n, out_specs=(_HBM,) * n,
        out_shape=tuple(jax.ShapeDtypeStruct((N_DEV,) + a.shape, a.dtype) for a in arrs),
        scratch_shapes=[pltpu.SemaphoreType.DMA((n * slots,)), pltpu.SemaphoreType.DMA((n * slots,)),
                        pltpu.SemaphoreType.DMA((n,))],
    )(*arrs)


def _pair_swap(arrs, *, name):
    n = len(arrs)

    def body(*refs):
        in_refs, out_refs, send_sems, recv_sems = refs[:n], refs[n:2 * n], refs[2 * n], refs[2 * n + 1]
        x, y, c, _ = _mesh_pos()
        copies = [pltpu.make_async_remote_copy(src_ref=in_refs[k].at[1 - c], dst_ref=out_refs[k], send_sem=send_sems.at[k],
                                               recv_sem=recv_sems.at[k], device_id=(x, y, 1 - c), device_id_type=_MESH)
                  for k in range(n)]
        for cp in copies:
            cp.start()
        for cp in copies:
            cp.wait()

    return pl.pallas_call(
        body, name=name, in_specs=[_HBM] * n, out_specs=(_HBM,) * n,
        out_shape=tuple(jax.ShapeDtypeStruct(a.shape[1:], a.dtype) for a in arrs),
        scratch_shapes=[pltpu.SemaphoreType.DMA((n,)), pltpu.SemaphoreType.DMA((n,))],
    )(*arrs)


def _chip_exchange(arrs, *, name):
    n = len(arrs)
    slots = 3

    def body(*refs):
        in_refs, out_refs = refs[:n], refs[n:2 * n]
        send_sems, recv_sems, local_sems = refs[2 * n:]
        x, y, c, chips = _mesh_pos()
        mine = 2 * x + y
        copies = [pltpu.make_async_copy(in_refs[k].at[mine], out_refs[k].at[mine], local_sems.at[k]) for k in range(n)]
        for j, chip in enumerate(chips):
            for k in range(n):
                copies.append(pltpu.make_async_remote_copy(
                    src_ref=in_refs[k].at[2 * chip[0] + chip[1]], dst_ref=out_refs[k].at[mine],
                    send_sem=send_sems.at[k * slots + j], recv_sem=recv_sems.at[k * slots + j],
                    device_id=(*chip, c), device_id_type=_MESH))
        for cp in copies:
            cp.start()
        for cp in copies:
            cp.wait()

    return pl.pallas_call(
        body, name=name, in_specs=[_HBM] * n, out_specs=(_HBM,) * n,
        out_shape=tuple(jax.ShapeDtypeStruct(a.shape, a.dtype) for a in arrs),
        scratch_shapes=[pltpu.SemaphoreType.DMA((n * slots,)), pltpu.SemaphoreType.DMA((n * slots,)),
                        pltpu.SemaphoreType.DMA((n,))],
    )(*arrs)


def _pair_add(p, r, core, *, name):
    _, n_chip, rows, cols = p.shape
    rb = _row_block(rows)

    def body(core_ref, p_ref, r_ref, o_ref):
        o_ref[...] = (p_ref[...] + r_ref[...]).astype(BF16)

    return pl.pallas_call(
        body, name=name,
        grid_spec=pltpu.PrefetchScalarGridSpec(
            num_scalar_prefetch=1, grid=(n_chip, rows // rb),
            in_specs=[pl.BlockSpec((None, None, rb, cols), lambda s, i, core_ref: (core_ref[0], s, i, 0)),
                      pl.BlockSpec((None, rb, cols), lambda s, i, core_ref: (s, i, 0))],
            out_specs=pl.BlockSpec((None, rb, cols), lambda s, i, core_ref: (s, i, 0))),
        out_shape=jax.ShapeDtypeStruct((n_chip, rows, cols), BF16),
        compiler_params=_cparams(("parallel", "parallel")),
    )(core, p, r)


def _row_block(rows, limit=256):
    best = rows
    for cand in range(8, limit + 1, 8):
        if rows % cand == 0:
            best = cand
    return best if rows > limit else rows


def _adam(gp, w, m, v, *, name):
    p, rows, cols = gp.shape
    rb = _row_block(rows)

    def body(gp_ref, w_ref, m_ref, v_ref, g_ref, d_ref, m2_ref, v2_ref):
        g = gp_ref[0].astype(F32)
        for s in range(1, p):
            g = g + gp_ref[s].astype(F32)
        m2 = ADAM_B1 * m_ref[...] + (1.0 - ADAM_B1) * g
        v2 = ADAM_B2 * v_ref[...] + (1.0 - ADAM_B2) * (g * g)
        m_hat = m2 / (1.0 - ADAM_B1 ** ADAM_STEP)
        v_hat = v2 / (1.0 - ADAM_B2 ** ADAM_STEP)
        g_ref[...] = g
        d_ref[...] = -ADAM_LR * (m_hat / (jnp.sqrt(v_hat) + ADAM_EPS) + ADAM_WD * w_ref[...])
        m2_ref[...] = m2
        v2_ref[...] = v2

    blk = pl.BlockSpec((rb, cols), lambda i: (i, 0))
    return pl.pallas_call(
        body, name=name, grid=(rows // rb,),
        in_specs=[pl.BlockSpec((p, rb, cols), lambda i: (0, i, 0)), blk, blk, blk],
        out_specs=(blk,) * 4, out_shape=(jax.ShapeDtypeStruct((rows, cols), F32),) * 4,
        compiler_params=_cparams(("parallel",)),
    )(gp, w, m, v)


def _cols_full(g):
    return jnp.transpose(g, (1, 0, 2)).reshape(g.shape[1], N_DEV * g.shape[2])


def _cols_pieces(full):
    r, c = full.shape
    return jnp.transpose(full.reshape(r, N_DEV, c // N_DEV), (1, 0, 2))


def _pad_lanes(a, width=LANES):
    return jnp.pad(a, ((0, 0), (0, width - a.shape[1])))


def _chunk_rows_of(a):
    by_chunk = jnp.transpose(a[:, :DN_HEADS].reshape(-1, DN_CHUNK, DN_HEADS), (0, 2, 1))
    return jnp.pad(by_chunk, ((0, 0), (0, HALO - DN_HEADS), (0, 0)))


_SMALL = (("ffn1_norm", D_MODEL), ("mix_norm", D_MODEL), ("ffn2_norm", D_MODEL), ("final_norm", D_MODEL), ("a_log", DN_HEADS),
          ("dt_bias", DN_HEADS), ("dn_norm", DN_HEAD_DIM), ("sg_ln_g", SG_WIDTH), ("sg_ln_b", SG_WIDTH),
          ("sg_w", SG_GROUPS * SG_CHUNK * SG_CHUNK), ("sg_b", SG_GROUPS * SG_CHUNK), ("conv_w", CONV_K * 3 * DN_WIDTH))
_SMALL_ROWS = 1128
_SMALL_SHAPES = {"ffn1_norm": (1, D_MODEL), "mix_norm": (1, D_MODEL), "ffn2_norm": (1, D_MODEL), "final_norm": (D_MODEL,),
                 "a_log": (1, DN_HEADS), "dt_bias": (1, DN_HEADS), "dn_norm": (1, DN_HEAD_DIM), "sg_ln_g": (1, SG_WIDTH),
                 "sg_ln_b": (1, SG_WIDTH), "sg_w": (1, SG_GROUPS, SG_CHUNK, SG_CHUNK), "sg_b": (1, SG_GROUPS, SG_CHUNK)}


def _pack_small(d):
    flat = jnp.concatenate([d[name].reshape(-1) for name, _ in _SMALL])
    return jnp.pad(flat, (0, _SMALL_ROWS * LANES - flat.shape[0])).reshape(_SMALL_ROWS, LANES)


def _unpack_small(a):
    flat, out, at = a.reshape(-1), {}, 0
    for name, size in _SMALL:
        out[name] = flat[at:at + size]
        at += size
    return out


def kernel(x, ffn1_norm, ffn1_w_gate, ffn1_w_up, ffn1_w_down, mix_norm, w_in, conv_w, a_log, dt_bias, dn_norm, sg_ln_g, sg_ln_b, sg_w, sg_b, w_out, ffn2_norm, ffn2_w_gate, ffn2_w_up, ffn2_w_down, final_norm, loss_target, m_ffn1_norm, m_ffn1_w_gate, m_ffn1_w_up, m_ffn1_w_down, m_mix_norm, m_w_in, m_conv_w, m_a_log, m_dt_bias, m_dn_norm, m_sg_ln_g, m_sg_ln_b, m_sg_w, m_sg_b, m_w_out, m_ffn2_norm, m_ffn2_w_gate, m_ffn2_w_up, m_ffn2_w_down, m_final_norm, v_ffn1_norm, v_ffn1_w_gate, v_ffn1_w_up, v_ffn1_w_down, v_mix_norm, v_w_in, v_conv_w, v_a_log, v_dt_bias, v_dn_norm, v_sg_ln_g, v_sg_ln_b, v_sg_w, v_sg_b, v_w_out, v_ffn2_norm, v_ffn2_w_gate, v_ffn2_w_up, v_ffn2_w_down, v_final_norm):
    weights = dict(ffn1_norm=ffn1_norm, ffn1_w_gate=ffn1_w_gate, ffn1_w_up=ffn1_w_up, ffn1_w_down=ffn1_w_down, mix_norm=mix_norm, w_in=w_in, conv_w=conv_w, a_log=a_log, dt_bias=dt_bias, dn_norm=dn_norm, sg_ln_g=sg_ln_g, sg_ln_b=sg_ln_b, sg_w=sg_w, sg_b=sg_b, w_out=w_out, ffn2_norm=ffn2_norm, ffn2_w_gate=ffn2_w_gate, ffn2_w_up=ffn2_w_up, ffn2_w_down=ffn2_w_down, final_norm=final_norm)
    mom_m = dict(ffn1_norm=m_ffn1_norm, ffn1_w_gate=m_ffn1_w_gate, ffn1_w_up=m_ffn1_w_up, ffn1_w_down=m_ffn1_w_down, mix_norm=m_mix_norm, w_in=m_w_in, conv_w=m_conv_w, a_log=m_a_log, dt_bias=m_dt_bias, dn_norm=m_dn_norm, sg_ln_g=m_sg_ln_g, sg_ln_b=m_sg_ln_b, sg_w=m_sg_w, sg_b=m_sg_b, w_out=m_w_out, ffn2_norm=m_ffn2_norm, ffn2_w_gate=m_ffn2_w_gate, ffn2_w_up=m_ffn2_w_up, ffn2_w_down=m_ffn2_w_down, final_norm=m_final_norm)
    mom_v = dict(ffn1_norm=v_ffn1_norm, ffn1_w_gate=v_ffn1_w_gate, ffn1_w_up=v_ffn1_w_up, ffn1_w_down=v_ffn1_w_down, mix_norm=v_mix_norm, w_in=v_w_in, conv_w=v_conv_w, a_log=v_a_log, dt_bias=v_dt_bias, dn_norm=v_dn_norm, sg_ln_g=v_sg_ln_g, sg_ln_b=v_sg_ln_b, sg_w=v_sg_w, sg_b=v_sg_b, w_out=v_w_out, ffn2_norm=v_ffn2_norm, ffn2_w_gate=v_ffn2_w_gate, ffn2_w_up=v_ffn2_w_up, ffn2_w_down=v_ffn2_w_down, final_norm=v_final_norm)
    order = list(weights)
    big = ("ffn1_w_gate", "ffn1_w_up", "ffn1_w_down", "w_in", "w_out", "ffn2_w_gate", "ffn2_w_up", "ffn2_w_down")
    col_sharded = ("ffn1_w_gate", "ffn1_w_up", "w_in", "ffn2_w_gate", "ffn2_w_up")

    n_seq, seq, _ = x.shape
    t = n_seq * seq
    me = 4 * lax.axis_index("x") + 2 * lax.axis_index("y") + lax.axis_index("c")
    x0 = x.reshape(t, D_MODEL)
    tgt = loss_target.reshape(t, D_MODEL)

    gathered = _gather2([weights[n][0].astype(BF16) for n in big] + [conv_w[0]], name="gather_weights")
    full = {n: (_cols_full(g) if n in col_sharded else g.reshape(-1, g.shape[-1])) for n, g in zip(big, gathered)}
    conv_full = _cols_full(gathered[-1])
    w_in_f = full["w_in"]
    offs = (0, SG_WIDTH, 2 * SG_WIDTH, 2 * SG_WIDTH + 3 * DN_WIDTH, 2 * SG_WIDTH + 4 * DN_WIDTH)
    n_proj = offs[-1]
    ws = [w_in_f[:, offs[0]:offs[1]], w_in_f[:, offs[1]:offs[2]], w_in_f[:, offs[2]:offs[3]], w_in_f[:, offs[3]:offs[4]],
          _pad_lanes(w_in_f[:, n_proj:n_proj + DN_HEADS]), _pad_lanes(w_in_f[:, n_proj + DN_HEADS:n_proj + 2 * DN_HEADS])]
    wo_sg, wo_dn = full["w_out"][:SG_WIDTH], full["w_out"][SG_WIDTH:]
    alog, dtb = _pad_lanes(a_log), _pad_lanes(dt_bias)
    sgbt = _pad_lanes(sg_b[0].T)
    fnw = final_norm.reshape(1, D_MODEL)

    x1 = _ffn_fwd(x0, ffn1_norm, full["ffn1_w_gate"], full["ffn1_w_up"], full["ffn1_w_down"], name="ffn1_fwd")
    u, v, qkv, z, bpre, apre = _mix_in_fwd(x1, mix_norm, ws, name="mix_in_fwd")
    sg_out = _sg_fwd(u, v, sg_ln_g, sg_ln_b, sg_w[0], sgbt, name="sg_fwd")
    q, k, vv, beta, gc = _dn_prep_fwd(qkv, bpre, apre, conv_full, alog, dtb, seq, name="dn_prep_fwd")
    grow = _chunk_rows_of(gc)
    wy_w, wy_u, q_dec, k_dec, qk, egl, inv = _delta_prep(q, k, vv, gc, grow, beta, name="delta_prep")
    o, states = _delta_seq_fwd(wy_w, wy_u, q_dec, k_dec, qk, egl, n_seq, seq, name="delta_seq_fwd")
    x2 = _mix_out_fwd(x1, sg_out, o, z, wo_sg, wo_dn, dn_norm, name="mix_out_fwd")
    dx3, loss_part, d_fn = _ffn_fwd(x2, ffn2_norm, full["ffn2_w_gate"], full["ffn2_w_up"], full["ffn2_w_down"], tgt, fnw,
                                    name="ffn2_fwd_loss")
    loss = lax.psum(loss_part[0, 0], ("x", "y", "c"))

    dx2, d_n2, d_g2, d_u2, d_d2 = _ffn_bwd(x2, ffn2_norm, full["ffn2_w_gate"], full["ffn2_w_up"], full["ffn2_w_down"], dx3,
                                           name="ffn2_bwd")
    dsg, do, dz, d_wo_sg, d_wo_dn, d_dnw = _mix_out_bwd(dx2, sg_out, o, z, wo_sg, wo_dn, dn_norm, name="mix_out_bwd")
    d_seq = _delta_seq_bwd(wy_w, wy_u, q_dec, k_dec, qk, egl, states, do, n_seq, seq, name="delta_seq_bwd")
    dq, dk, dv, dgc_a, dgrow, dbeta = _delta_par_bwd(q, k, vv, gc, grow, beta, inv, *d_seq, name="delta_par_bwd")
    dgc_b = _pad_lanes(jnp.transpose(dgrow[:, :DN_HEADS, :], (0, 2, 1)).reshape(t, DN_HEADS))
    dy_conv, dbpre, dapre, d_alog, d_dtb = _dn_prep_bwd(qkv, bpre, apre, conv_full, alog, dtb, dq, dk, dv, dbeta, dgc_a, dgc_b,
                                                        seq, name="dn_prep_bwd")
    dqkv, d_conv = _conv_bwd(qkv, dy_conv, conv_full, seq, name="conv_bwd")
    du, dvv, d_lng, d_lnb, d_wc, d_sgbt = _sg_bwd(u, v, sg_ln_g, sg_ln_b, sg_w[0], sgbt, dsg, name="sg_bwd")
    dx1, d_mixn, d_ws = _split3(_mix_in_bwd(x1, mix_norm, ws, dx2, (du, dvv, dqkv, dz, dbpre, dapre), name="mix_in_bwd"))
    grad_x, d_n1, d_g1, d_u1, d_d1 = _ffn_bwd(x0, ffn1_norm, full["ffn1_w_gate"], full["ffn1_w_up"], full["ffn1_w_down"], dx1,
                                              name="ffn1_bwd")

    def by_core(p8):
        return jnp.moveaxis(p8.reshape((4, 2) + p8.shape[1:]), 1, 0)

    def ff_cols(acc):
        return _cols_pieces(jnp.transpose(acc, (1, 0, 2)).reshape(D_MODEL, D_FF))

    def ff_rows(acc):
        return acc.reshape(N_DEV, D_FF // N_DEV, D_MODEL)

    d_w_in = jnp.concatenate([d_ws[0], d_ws[1], d_ws[2], d_ws[3], d_ws[4][:, :DN_HEADS], d_ws[5][:, :DN_HEADS]], axis=1)
    d_w_out = jnp.concatenate([d_wo_sg, d_wo_dn], axis=0)
    pieces = dict(ffn1_w_gate=ff_cols(d_g1), ffn1_w_up=ff_cols(d_u1), ffn1_w_down=ff_rows(d_d1), w_in=_cols_pieces(d_w_in),
                  w_out=d_w_out.reshape(N_DEV, D_MODEL // N_DEV, D_MODEL), ffn2_w_gate=ff_cols(d_g2), ffn2_w_up=ff_cols(d_u2),
                  ffn2_w_down=ff_rows(d_d2))
    own = [by_core(pieces[n]) for n in big]
    from_sibling = _pair_swap(own, name="grads_to_sibling")
    core = lax.axis_index("c").astype(jnp.int32).reshape(1)
    chip_sums = [_pair_add(p, r, core, name="pair_add_" + n) for n, p, r in zip(big, own, from_sibling)]
    received = _chip_exchange(chip_sums, name="grads_to_owner")
    res = {}
    for n, gp in zip(big, received):
        res[n] = _adam(gp, weights[n][0], mom_m[n][0], mom_v[n][0], name="adam_" + n)

    small_grads = dict(ffn1_norm=d_n1, mix_norm=d_mixn, ffn2_norm=d_n2, final_norm=d_fn, a_log=d_alog[:, :DN_HEADS],
                       dt_bias=d_dtb[:, :DN_HEADS], dn_norm=d_dnw, sg_ln_g=d_lng, sg_ln_b=d_lnb, sg_w=d_wc,
                       sg_b=d_sgbt[:, :SG_GROUPS].T, conv_w=d_conv[:CONV_K])
    (small_parts,) = _gather2([_pack_small(small_grads)], name="gather_small_grads")
    zeros_conv = jnp.zeros((CONV_K * 3 * DN_WIDTH,), F32)
    packed = [_pack_small({**{n: src[n] for n, _ in _SMALL if n != "conv_w"}, "conv_w": zeros_conv})
              for src in (weights, mom_m, mom_v)]
    small_res = [_unpack_small(a) for a in _adam(small_parts, *packed, name="adam_small")]
    conv_grad = lax.dynamic_slice_in_dim(small_res[0]["conv_w"].reshape(CONV_K, 3 * DN_WIDTH), me * (3 * DN_WIDTH // N_DEV),
                                         3 * DN_WIDTH // N_DEV, axis=1)
    res["conv_w"] = _adam(conv_grad[None], conv_w[0], m_conv_w[0], v_conv_w[0], name="adam_conv_w")

    outs = [[], [], [], []]
    for n in order:
        for kind in range(4):
            if n in res:
                outs[kind].append(res[n][kind][None])
            else:
                outs[kind].append(small_res[kind][n].reshape(_SMALL_SHAPES[n]))
    return (loss, grad_x.reshape(x.shape), *outs[0], *outs[1], *outs[2], *outs[3])


def _split3(r):
    return r[0], r[1], r[2:]
```

```python
import functools

import jax
import jax.numpy as jnp
from jax import lax
from jax.experimental import pallas as pl
from jax.experimental.pallas import tpu as pltpu

F32 = jnp.float32
BF16 = jnp.bfloat16

D_MODEL = 1024
D_FF = 2816
SG_WIDTH = 512
SG_GROUPS = 8
SG_GROUP_DIM = 64
SG_CHUNK = 128
DN_WIDTH = 512
DN_HEAD_DIM = 128
DN_HEADS = 4
DN_CHUNK = 64
CONV_K = 4
EPS = 1e-6
N_DEV = 8
LANES = 128
HALO = 8

ADAM_LR = 0.001
ADAM_B1 = 0.9
ADAM_B2 = 0.999
ADAM_EPS = 1e-08
ADAM_WD = 0.01
ADAM_STEP = 10

VMEM_LIMIT = 60 * 1024 * 1024
TOKEN_BLOCK = 512
FF_BLOCK_FWD = 1408
FF_BLOCK_BWD = 256

_HI = lax.Precision.HIGHEST


def _cparams(sem):
    return pltpu.CompilerParams(dimension_semantics=sem, vmem_limit_bytes=VMEM_LIMIT)


def _tm(t, pref=TOKEN_BLOCK):
    return min(pref, t)


def _dg(a, b, ca, cb, precision):
    if precision is not None:
        return lax.dot_general(a, b, (((ca,), (cb,)), ((), ())), precision=precision, preferred_element_type=F32)
    return lax.dot_general(a.astype(BF16), b.astype(BF16), (((ca,), (cb,)), ((), ())), preferred_element_type=F32)


def _make_mm(exact):
    @jax.custom_vjp
    def mm(a, b):
        return _dg(a, b, 1, 0, exact)

    @jax.custom_vjp
    def mm_nt(a, b):
        return _dg(a, b, 1, 1, exact)

    @jax.custom_vjp
    def mm_tn(a, b):
        return _dg(a, b, 0, 0, exact)

    mm.defvjp(lambda a, b: (mm(a, b), (a, b)), lambda r, g: (mm_nt(g, r[1]), mm_tn(r[0], g)))
    mm_nt.defvjp(lambda a, b: (mm_nt(a, b), (a, b)), lambda r, g: (mm(g, r[1]), mm_tn(g, r[0])))
    mm_tn.defvjp(lambda a, b: (mm_tn(a, b), (a, b)), lambda r, g: (mm_nt(r[1], g), mm(r[0], g)))
    return mm, mm_nt, mm_tn


mm, mm_nt, mm_tn = _make_mm(None)
mmx, mmx_nt, mmx_tn = _make_mm(_HI)
mmh, _, _ = _make_mm(lax.Precision.HIGH)


def _sigmoid(x):
    return 1.0 / (1.0 + jnp.exp(-x))


def _silu(x):
    return x * _sigmoid(x)


def _softplus(x):
    neg_abs = jnp.where(x > 0, -x, x)
    return jnp.where(x > 0, x, 0.0) + jnp.log(1.0 + jnp.exp(neg_abs))


def _gelu(x):
    return 0.5 * x * (1.0 + jnp.tanh(0.7978845608028654 * (x + 0.044715 * (x * x * x))))


def _rms_fwd(x, g):
    r = lax.rsqrt(jnp.mean(x * x, axis=-1, keepdims=True) + EPS)
    xh = x * r
    return xh * g, xh, r


def _rms_bwd(dh, xh, r, g):
    dxh = dh * g
    dx = r * (dxh - xh * jnp.mean(dxh * xh, axis=-1, keepdims=True))
    return dx, jnp.sum(dh * xh, axis=0, keepdims=True)


def _acc_out(ref, first, val):
    @pl.when(first)
    def _():
        ref[...] = val

    @pl.when(jnp.logical_not(first))
    def _():
        ref[...] += val


def _ffn_fwd(x, nw, wg, wu, wd, tgt=None, fnw=None, *, name):
    t = x.shape[0]
    tm, fb = _tm(t), FF_BLOCK_FWD
    n_t, n_f = t // tm, D_FF // fb
    with_loss = tgt is not None

    def body(*refs):
        if with_loss:
            x_ref, nw_ref, wg_ref, wu_ref, wd_ref, tgt_ref, fnw_ref, dy_ref, loss_ref, dfn_ref, h_s, acc_s = refs
        else:
            x_ref, nw_ref, wg_ref, wu_ref, wd_ref, y_ref, h_s, acc_s = refs
        i, j = pl.program_id(0), pl.program_id(1)

        @pl.when(j == 0)
        def _():
            h, _, _ = _rms_fwd(x_ref[...], nw_ref[...])
            h_s[...] = h.astype(BF16)
            acc_s[...] = jnp.zeros_like(acc_s)

        h = h_s[...]
        g = jnp.dot(h, wg_ref[...], preferred_element_type=F32)
        u = jnp.dot(h, wu_ref[...], preferred_element_type=F32)
        a = _silu(g) * u
        acc_s[...] += jnp.dot(a.astype(BF16), wd_ref[...], preferred_element_type=F32)

        @pl.when(j == n_f - 1)
        def _():
            y = x_ref[...] + 0.5 * acc_s[...]
            if not with_loss:
                y_ref[...] = y
            else:
                gf = fnw_ref[...]
                out, xh, r = _rms_fwd(y, gf)
                err = out - tgt_ref[...]
                part = 0.5 * jnp.sum(jnp.mean(err * err, axis=-1, keepdims=True), axis=0, keepdims=True)
                d_out = err * (1.0 / D_MODEL)
                dy, dgf = _rms_bwd(d_out, xh, r, gf)
                dy_ref[...] = dy
                _acc_out(loss_ref, i == 0, jnp.broadcast_to(part, loss_ref.shape))
                _acc_out(dfn_ref, i == 0, dgf)

    row = lambda i, j: (i, 0)
    const = lambda i, j: (0, 0)
    in_specs = [
        pl.BlockSpec((tm, D_MODEL), row),
        pl.BlockSpec((1, D_MODEL), const),
        pl.BlockSpec((D_MODEL, fb), lambda i, j: (0, j)),
        pl.BlockSpec((D_MODEL, fb), lambda i, j: (0, j)),
        pl.BlockSpec((fb, D_MODEL), lambda i, j: (j, 0)),
    ]
    args = [x, nw, wg, wu, wd]
    if with_loss:
        in_specs += [pl.BlockSpec((tm, D_MODEL), row), pl.BlockSpec((1, D_MODEL), const)]
        args += [tgt, fnw]
        out_shape = (jax.ShapeDtypeStruct((t, D_MODEL), F32), jax.ShapeDtypeStruct((8, LANES), F32),
                     jax.ShapeDtypeStruct((1, D_MODEL), F32))
        out_specs = (pl.BlockSpec((tm, D_MODEL), row), pl.BlockSpec((8, LANES), const), pl.BlockSpec((1, D_MODEL), const))
        sem = ("arbitrary", "arbitrary")
    else:
        out_shape = jax.ShapeDtypeStruct((t, D_MODEL), F32)
        out_specs = pl.BlockSpec((tm, D_MODEL), row)
        sem = ("parallel", "arbitrary")
    return pl.pallas_call(
        body, name=name, grid=(n_t, n_f), in_specs=in_specs, out_specs=out_specs, out_shape=out_shape,
        scratch_shapes=[pltpu.VMEM((tm, D_MODEL), BF16), pltpu.VMEM((tm, D_MODEL), F32)],
        compiler_params=_cparams(sem),
    )(*args)


def _ffn_bwd(x, nw, wg, wu, wd, dy, *, name):
    t = x.shape[0]
    tm, fb = _tm(t), FF_BLOCK_BWD
    n_t, n_f = t // tm, D_FF // fb

    def body(x_ref, nw_ref, wg_ref, wu_ref, wd_ref, dy_ref, dx_ref, dnw_ref, dwg_hbm, dwu_hbm, dwd_hbm,
             h_s, r_s, dyh_s, ag_s, au_s, ad_s, sem):
        i, j = pl.program_id(0), pl.program_id(1)

        @pl.when(j == 0)
        def _():
            h, _, r = _rms_fwd(x_ref[...], nw_ref[...])
            h_s[...] = h.astype(BF16)
            r_s[...] = r
            dyh_s[...] = (0.5 * dy_ref[...]).astype(BF16)
            dx_ref[...] = jnp.zeros_like(dx_ref)

        h = h_s[...]
        dyh = dyh_s[...]
        wg_j, wu_j, wd_j = wg_ref[...], wu_ref[...], wd_ref[...]
        g = jnp.dot(h, wg_j, preferred_element_type=F32)
        u = jnp.dot(h, wu_j, preferred_element_type=F32)
        s = _sigmoid(g)
        gs = g * s
        da = lax.dot_general(dyh, wd_j, (((1,), (1,)), ((), ())), preferred_element_type=F32)
        dg = (da * u * (s + gs * (1.0 - s))).astype(BF16)
        du = (da * gs).astype(BF16)
        a = (gs * u).astype(BF16)
        tn = (((0,), (0,)), ((), ()))
        c_d = lax.dot_general(a, dyh, tn, preferred_element_type=F32)
        c_g = lax.dot_general(h, dg, tn, preferred_element_type=F32)
        c_u = lax.dot_general(h, du, tn, preferred_element_type=F32)

        @pl.when(i == 0)
        def _():
            ad_s[j] = c_d
            ag_s[j] = c_g
            au_s[j] = c_u

        @pl.when(i > 0)
        def _():
            ad_s[j] += c_d
            ag_s[j] += c_g
            au_s[j] += c_u

        nt = (((1,), (1,)), ((), ()))
        dx_ref[...] += (lax.dot_general(dg, wg_j, nt, preferred_element_type=F32)
                        + lax.dot_general(du, wu_j, nt, preferred_element_type=F32))

        @pl.when(j == n_f - 1)
        def _():
            r = r_s[...]
            dx, dnw = _rms_bwd(dx_ref[...], x_ref[...] * r, r, nw_ref[...])
            dx_ref[...] = dy_ref[...] + dx
            _acc_out(dnw_ref, i == 0, dnw)

        @pl.when(jnp.logical_and(i == n_t - 1, j == n_f - 1))
        def _():
            copies = [pltpu.make_async_copy(ag_s, dwg_hbm, sem.at[0]), pltpu.make_async_copy(au_s, dwu_hbm, sem.at[1]),
                      pltpu.make_async_copy(ad_s, dwd_hbm, sem.at[2])]
            for cp in copies:
                cp.start()
            for cp in copies:
                cp.wait()

    row = lambda i, j: (i, 0)
    const = lambda i, j: (0, 0)
    hbm = pl.BlockSpec(memory_space=pl.ANY)
    return pl.pallas_call(
        body, name=name, grid=(n_t, n_f),
        in_specs=[
            pl.BlockSpec((tm, D_MODEL), row),
            pl.BlockSpec((1, D_MODEL), const),
            pl.BlockSpec((D_MODEL, fb), lambda i, j: (0, j)),
            pl.BlockSpec((D_MODEL, fb), lambda i, j: (0, j)),
            pl.BlockSpec((fb, D_MODEL), lambda i, j: (j, 0)),
            pl.BlockSpec((tm, D_MODEL), row),
        ],
        out_specs=(pl.BlockSpec((tm, D_MODEL), row), pl.BlockSpec((1, D_MODEL), const), hbm, hbm, hbm),
        out_shape=(
            jax.ShapeDtypeStruct((t, D_MODEL), F32), jax.ShapeDtypeStruct((1, D_MODEL), F32),
            jax.ShapeDtypeStruct((n_f, D_MODEL, fb), F32), jax.ShapeDtypeStruct((n_f, D_MODEL, fb), F32),
            jax.ShapeDtypeStruct((n_f, fb, D_MODEL), F32),
        ),
        scratch_shapes=[
            pltpu.VMEM((tm, D_MODEL), BF16), pltpu.VMEM((tm, 1), F32), pltpu.VMEM((tm, D_MODEL), BF16),
            pltpu.VMEM((n_f, D_MODEL, fb), F32), pltpu.VMEM((n_f, D_MODEL, fb), F32), pltpu.VMEM((n_f, fb, D_MODEL), F32),
            pltpu.SemaphoreType.DMA((3,)),
        ],
        compiler_params=_cparams(("arbitrary", "arbitrary")),
    )(x, nw, wg, wu, wd, dy)


_PROJ_WIDTHS = (SG_WIDTH, SG_WIDTH, 3 * DN_WIDTH, DN_WIDTH, LANES, LANES)


def _mix_in_fwd(x, nw, ws, *, name):
    t = x.shape[0]
    tm = _tm(t)

    def body(x_ref, nw_ref, *refs):
        w_refs, o_refs = refs[:6], refs[6:]
        h, _, _ = _rms_fwd(x_ref[...], nw_ref[...])
        h = h.astype(BF16)
        for w_ref, o_ref in zip(w_refs, o_refs):
            o_ref[...] = jnp.dot(h, w_ref[...], preferred_element_type=F32)

    row = lambda i: (i, 0)
    const = lambda i: (0, 0)
    return pl.pallas_call(
        body, name=name, grid=(t // tm,),
        in_specs=[pl.BlockSpec((tm, D_MODEL), row), pl.BlockSpec((1, D_MODEL), const)]
        + [pl.BlockSpec((D_MODEL, n), const) for n in _PROJ_WIDTHS],
        out_specs=tuple(pl.BlockSpec((tm, n), row) for n in _PROJ_WIDTHS),
        out_shape=tuple(jax.ShapeDtypeStruct((t, n), F32) for n in _PROJ_WIDTHS),
        compiler_params=_cparams(("parallel",)),
    )(x, nw, *ws)


def _mix_in_bwd(x, nw, ws, dres, dps, *, name):
    t = x.shape[0]
    tm = _tm(t, 256)

    def body(x_ref, nw_ref, dres_ref, *refs):
        w_refs, dp_refs, dx_ref, dnw_ref, dw_refs = refs[:6], refs[6:12], refs[12], refs[13], refs[14:]
        i = pl.program_id(0)
        hf, xh, r = _rms_fwd(x_ref[...], nw_ref[...])
        h = hf.astype(BF16)
        dh = jnp.zeros((tm, D_MODEL), F32)
        for w_ref, dp_ref, dw_ref in zip(w_refs, dp_refs, dw_refs):
            dp = dp_ref[...].astype(BF16)
            dh = dh + lax.dot_general(dp, w_ref[...], (((1,), (1,)), ((), ())), preferred_element_type=F32)
            _acc_out(dw_ref, i == 0, lax.dot_general(h, dp, (((0,), (0,)), ((), ())), preferred_element_type=F32))
        dx, dnw = _rms_bwd(dh, xh, r, nw_ref[...])
        dx_ref[...] = dres_ref[...] + dx
        _acc_out(dnw_ref, i == 0, dnw)

    row = lambda i: (i, 0)
    const = lambda i: (0, 0)
    return pl.pallas_call(
        body, name=name, grid=(t // tm,),
        in_specs=[pl.BlockSpec((tm, D_MODEL), row), pl.BlockSpec((1, D_MODEL), const), pl.BlockSpec((tm, D_MODEL), row)]
        + [pl.BlockSpec((D_MODEL, n), const) for n in _PROJ_WIDTHS]
        + [pl.BlockSpec((tm, n), row) for n in _PROJ_WIDTHS],
        out_specs=(pl.BlockSpec((tm, D_MODEL), row), pl.BlockSpec((1, D_MODEL), const))
        + tuple(pl.BlockSpec((D_MODEL, n), const) for n in _PROJ_WIDTHS),
        out_shape=(jax.ShapeDtypeStruct((t, D_MODEL), F32), jax.ShapeDtypeStruct((1, D_MODEL), F32))
        + tuple(jax.ShapeDtypeStruct((D_MODEL, n), F32) for n in _PROJ_WIDTHS),
        compiler_params=_cparams(("arbitrary",)),
    )(x, nw, dres, *ws, *dps)


def _sg_fn(u, v, lng, lnb, wcs, sgbt):
    lane = lax.broadcasted_iota(jnp.int32, (1, SG_WIDTH), 1)
    lane_b = lax.broadcasted_iota(jnp.int32, (1, LANES), 1)
    rr = lax.broadcasted_iota(jnp.int32, (SG_CHUNK, SG_CHUNK), 0)
    cc = lax.broadcasted_iota(jnp.int32, (SG_CHUNK, SG_CHUNK), 1)
    gu, gv = _gelu(u), _gelu(v)
    mu = jnp.mean(gv, axis=-1, keepdims=True)
    cen = gv - mu
    var = jnp.mean(cen * cen, axis=-1, keepdims=True)
    ln = cen * lax.rsqrt(var + EPS) * lng + lnb
    vs = jnp.zeros_like(u)
    for g in range(SG_GROUPS):
        in_group = jnp.logical_and(lane >= g * SG_GROUP_DIM, lane < (g + 1) * SG_GROUP_DIM)
        w_causal = jnp.where(rr >= cc, wcs[g], 0.0)
        bias = jnp.sum(jnp.where(lane_b == g, sgbt, 0.0), axis=1, keepdims=True)
        vs = vs + jnp.where(in_group, mm(w_causal, ln) + bias, 0.0)
    return gu * vs


def _sg_fwd(u, v, lng, lnb, wc, sgbt, *, name):
    t = u.shape[0]
    tm = _tm(t)

    def body(u_ref, v_ref, lng_ref, lnb_ref, wc_ref, sgbt_ref, o_ref):
        wcs = [wc_ref[g] for g in range(SG_GROUPS)]
        for c in range(tm // SG_CHUNK):
            rows = pl.ds(c * SG_CHUNK, SG_CHUNK)
            o_ref[rows, :] = _sg_fn(u_ref[rows, :], v_ref[rows, :], lng_ref[...], lnb_ref[...], wcs, sgbt_ref[...])

    row = lambda i: (i, 0)
    const = lambda i: (0, 0)
    return pl.pallas_call(
        body, name=name, grid=(t // tm,),
        in_specs=[pl.BlockSpec((tm, SG_WIDTH), row), pl.BlockSpec((tm, SG_WIDTH), row),
                  pl.BlockSpec((1, SG_WIDTH), const), pl.BlockSpec((1, SG_WIDTH), const),
                  pl.BlockSpec((SG_GROUPS, SG_CHUNK, SG_CHUNK), lambda i: (0, 0, 0)), pl.BlockSpec((SG_CHUNK, LANES), const)],
        out_specs=pl.BlockSpec((tm, SG_WIDTH), row),
        out_shape=jax.ShapeDtypeStruct((t, SG_WIDTH), F32),
        compiler_params=_cparams(("parallel",)),
    )(u, v, lng, lnb, wc, sgbt)


def _sg_bwd(u, v, lng, lnb, wc, sgbt, dout, *, name):
    t = u.shape[0]
    tm = _tm(t)

    def body(u_ref, v_ref, lng_ref, lnb_ref, wc_ref, sgbt_ref, do_ref, du_ref, dv_ref, dlng_ref, dlnb_ref, dwc_ref, dsgbt_ref):
        i = pl.program_id(0)
        wcs = [wc_ref[g] for g in range(SG_GROUPS)]
        tot = None
        for c in range(tm // SG_CHUNK):
            rows = pl.ds(c * SG_CHUNK, SG_CHUNK)
            _, vjp = jax.vjp(_sg_fn, u_ref[rows, :], v_ref[rows, :], lng_ref[...], lnb_ref[...], wcs, sgbt_ref[...])
            du, dv, dlng, dlnb, dwcs, dsgbt = vjp(do_ref[rows, :])
            du_ref[rows, :] = du
            dv_ref[rows, :] = dv
            part = (dlng, dlnb, dwcs, dsgbt)
            tot = part if tot is None else jax.tree.map(jnp.add, tot, part)
        dlng, dlnb, dwcs, dsgbt = tot
        _acc_out(dlng_ref, i == 0, dlng)
        _acc_out(dlnb_ref, i == 0, dlnb)
        _acc_out(dsgbt_ref, i == 0, dsgbt)
        for g in range(SG_GROUPS):
            @pl.when(i == 0)
            def _(g=g):
                dwc_ref[g] = dwcs[g]

            @pl.when(i > 0)
            def _(g=g):
                dwc_ref[g] += dwcs[g]

    row = lambda i: (i, 0)
    const = lambda i: (0, 0)
    wspec = pl.BlockSpec((SG_GROUPS, SG_CHUNK, SG_CHUNK), lambda i: (0, 0, 0))
    return pl.pallas_call(
        body, name=name, grid=(t // tm,),
        in_specs=[pl.BlockSpec((tm, SG_WIDTH), row), pl.BlockSpec((tm, SG_WIDTH), row),
                  pl.BlockSpec((1, SG_WIDTH), const), pl.BlockSpec((1, SG_WIDTH), const), wspec,
                  pl.BlockSpec((SG_CHUNK, LANES), const), pl.BlockSpec((tm, SG_WIDTH), row)],
        out_specs=(pl.BlockSpec((tm, SG_WIDTH), row), pl.BlockSpec((tm, SG_WIDTH), row),
                   pl.BlockSpec((1, SG_WIDTH), const), pl.BlockSpec((1, SG_WIDTH), const), wspec,
                   pl.BlockSpec((SG_CHUNK, LANES), const)),
        out_shape=(jax.ShapeDtypeStruct((t, SG_WIDTH), F32), jax.ShapeDtypeStruct((t, SG_WIDTH), F32),
                   jax.ShapeDtypeStruct((1, SG_WIDTH), F32), jax.ShapeDtypeStruct((1, SG_WIDTH), F32),
                   jax.ShapeDtypeStruct((SG_GROUPS, SG_CHUNK, SG_CHUNK), F32), jax.ShapeDtypeStruct((SG_CHUNK, LANES), F32)),
        compiler_params=_cparams(("arbitrary",)),
    )(u, v, lng, lnb, wc, sgbt, dout)


def _conv_taps(ext, w, tm):
    y = None
    for j in range(CONV_K):
        s = CONV_K - 1 - j
        shifted = ext if s == 0 else pltpu.roll(ext, s, 0)
        term = w[j:j + 1, :] * shifted[HALO:HALO + tm, :]
        y = term if y is None else y + term
    return y


def _post_conv(yq, yk, yv, bpre, apre, alog, dtb):
    def l2(a):
        return a * lax.rsqrt(jnp.sum(a * a, axis=-1, keepdims=True) + EPS)

    q = [l2(_silu(a)) for a in yq]
    k = [l2(_silu(a)) for a in yk]
    return q, k, _silu(yv), _sigmoid(bpre), -jnp.exp(alog) * _softplus(apre + dtb)


def _chunk_tril(tm):
    rr = lax.broadcasted_iota(jnp.int32, (tm, tm), 0)
    cc = lax.broadcasted_iota(jnp.int32, (tm, tm), 1)
    shift = DN_CHUNK.bit_length() - 1
    same = jnp.right_shift(rr, shift) == jnp.right_shift(cc, shift)
    return jnp.where(jnp.logical_and(same, rr >= cc), 1.0, 0.0).astype(F32)


def _halo_specs(tm, width, n_blocks_seq, n_blocks):
    per = tm // HALO
    prev = pl.BlockSpec((HALO, width), lambda i: (jnp.maximum(i * per - 1, 0), 0))
    nxt = pl.BlockSpec((HALO, width), lambda i: (jnp.minimum((i + 1) * per, n_blocks * per - 1), 0))
    return prev, nxt


def _split_heads(ref, base):
    return [ref[:, base + h * DN_HEAD_DIM: base + (h + 1) * DN_HEAD_DIM] for h in range(DN_HEADS)]


def _dn_prep_fwd(qkv, bpre, apre, conv_w, alog, dtb, seq, *, name):
    t = qkv.shape[0]
    tm = _tm(t)
    bps = seq // tm
    cw = 3 * DN_WIDTH

    def body(x_ref, halo_ref, b_ref, a_ref, w_ref, alog_ref, dtb_ref, q_ref, k_ref, v_ref, beta_ref, gc_ref):
        i = pl.program_id(0)
        keep = jnp.where(i % bps == 0, 0.0, 1.0)
        ext = jnp.concatenate([halo_ref[...] * keep, x_ref[...]], axis=0)
        y = _conv_taps(ext, w_ref[...], tm)
        yq = [y[:, h * DN_HEAD_DIM:(h + 1) * DN_HEAD_DIM] for h in range(DN_HEADS)]
        yk = [y[:, DN_WIDTH + h * DN_HEAD_DIM: DN_WIDTH + (h + 1) * DN_HEAD_DIM] for h in range(DN_HEADS)]
        q, k, v, beta, g = _post_conv(yq, yk, y[:, 2 * DN_WIDTH:], b_ref[...], a_ref[...], alog_ref[...], dtb_ref[...])
        for h in range(DN_HEADS):
            q_ref[:, h * DN_HEAD_DIM:(h + 1) * DN_HEAD_DIM] = q[h]
            k_ref[:, h * DN_HEAD_DIM:(h + 1) * DN_HEAD_DIM] = k[h]
        v_ref[...] = v
        beta_ref[...] = beta
        gc_ref[...] = mmx(_chunk_tril(tm), g)

    row = lambda i: (i, 0)
    const = lambda i: (0, 0)
    prev, _ = _halo_specs(tm, cw, bps, t // tm)
    return pl.pallas_call(
        body, name=name, grid=(t // tm,),
        in_specs=[pl.BlockSpec((tm, cw), row), prev, pl.BlockSpec((tm, LANES), row), pl.BlockSpec((tm, LANES), row),
                  pl.BlockSpec((CONV_K, cw), const), pl.BlockSpec((1, LANES), const), pl.BlockSpec((1, LANES), const)],
        out_specs=tuple(pl.BlockSpec((tm, n), row) for n in (DN_WIDTH, DN_WIDTH, DN_WIDTH, LANES, LANES)),
        out_shape=tuple(jax.ShapeDtypeStruct((t, n), F32) for n in (DN_WIDTH, DN_WIDTH, DN_WIDTH, LANES, LANES)),
        compiler_params=_cparams(("parallel",)),
    )(qkv, qkv, bpre, apre, conv_w, alog, dtb)


def _dn_prep_bwd(qkv, bpre, apre, conv_w, alog, dtb, dq, dk, dv, dbeta, dgc, dgc2, seq, *, name):
    t = qkv.shape[0]
    tm = _tm(t)
    bps = seq // tm
    cw = 3 * DN_WIDTH

    def body(x_ref, halo_ref, b_ref, a_ref, w_ref, alog_ref, dtb_ref, dq_ref, dk_ref, dv_ref, dbeta_ref, dgc_ref, dgc2_ref,
             dy_ref, db_ref, da_ref, dalog_ref, ddtb_ref):
        i = pl.program_id(0)
        keep = jnp.where(i % bps == 0, 0.0, 1.0)
        ext = jnp.concatenate([halo_ref[...] * keep, x_ref[...]], axis=0)
        y = _conv_taps(ext, w_ref[...], tm)
        yq = [y[:, h * DN_HEAD_DIM:(h + 1) * DN_HEAD_DIM] for h in range(DN_HEADS)]
        yk = [y[:, DN_WIDTH + h * DN_HEAD_DIM: DN_WIDTH + (h + 1) * DN_HEAD_DIM] for h in range(DN_HEADS)]
        _, vjp = jax.vjp(_post_conv, yq, yk, y[:, 2 * DN_WIDTH:], b_ref[...], a_ref[...], alog_ref[...], dtb_ref[...])
        dg = mmx_tn(_chunk_tril(tm), dgc_ref[...] + dgc2_ref[...])
        dyq, dyk, dyv, db, da, dalog, ddtb = vjp((_split_heads(dq_ref, 0), _split_heads(dk_ref, 0), dv_ref[...],
                                                  dbeta_ref[...], dg))
        for h in range(DN_HEADS):
            dy_ref[:, h * DN_HEAD_DIM:(h + 1) * DN_HEAD_DIM] = dyq[h]
            dy_ref[:, DN_WIDTH + h * DN_HEAD_DIM: DN_WIDTH + (h + 1) * DN_HEAD_DIM] = dyk[h]
        dy_ref[:, 2 * DN_WIDTH:] = dyv
        db_ref[...] = db
        da_ref[...] = da
        _acc_out(dalog_ref, i == 0, dalog)
        _acc_out(ddtb_ref, i == 0, ddtb)

    row = lambda i: (i, 0)
    const = lambda i: (0, 0)
    prev, _ = _halo_specs(tm, cw, bps, t // tm)
    return pl.pallas_call(
        body, name=name, grid=(t // tm,),
        in_specs=[pl.BlockSpec((tm, cw), row), prev, pl.BlockSpec((tm, LANES), row), pl.BlockSpec((tm, LANES), row),
                  pl.BlockSpec((CONV_K, cw), const), pl.BlockSpec((1, LANES), const), pl.BlockSpec((1, LANES), const),
                  pl.BlockSpec((tm, DN_WIDTH), row), pl.BlockSpec((tm, DN_WIDTH), row), pl.BlockSpec((tm, DN_WIDTH), row),
                  pl.BlockSpec((tm, LANES), row), pl.BlockSpec((tm, LANES), row), pl.BlockSpec((tm, LANES), row)],
        out_specs=(pl.BlockSpec((tm, cw), row), pl.BlockSpec((tm, LANES), row), pl.BlockSpec((tm, LANES), row),
                   pl.BlockSpec((1, LANES), const), pl.BlockSpec((1, LANES), const)),
        out_shape=(jax.ShapeDtypeStruct((t, cw), F32), jax.ShapeDtypeStruct((t, LANES), F32), jax.ShapeDtypeStruct((t, LANES), F32),
                   jax.ShapeDtypeStruct((1, LANES), F32), jax.ShapeDtypeStruct((1, LANES), F32)),
        compiler_params=_cparams(("arbitrary",)),
    )(qkv, qkv, bpre, apre, conv_w, alog, dtb, dq, dk, dv, dbeta, dgc, dgc2)


def _conv_bwd(qkv, dy, conv_w, seq, *, name):
    t = qkv.shape[0]
    tm = _tm(t)
    bps = seq // tm
    cw = 3 * DN_WIDTH
    n_ext = tm + HALO

    def body(x_ref, halo_ref, dy_ref, dyn_ref, w_ref, dx_ref, dw_ref):
        i = pl.program_id(0)
        keep_prev = jnp.where(i % bps == 0, 0.0, 1.0)
        keep_next = jnp.where(i % bps == bps - 1, 0.0, 1.0)
        ext = jnp.concatenate([halo_ref[...] * keep_prev, x_ref[...]], axis=0)
        dy = dy_ref[...]
        dyext = jnp.concatenate([dy, dyn_ref[...] * keep_next], axis=0)
        w = w_ref[...]

        @pl.when(i == 0)
        def _():
            dw_ref[...] = jnp.zeros_like(dw_ref)

        dx = None
        for j in range(CONV_K):
            s = CONV_K - 1 - j
            fut = dyext if s == 0 else pltpu.roll(dyext, n_ext - s, 0)
            term = w[j:j + 1, :] * fut[0:tm, :]
            dx = term if dx is None else dx + term
            past = ext if s == 0 else pltpu.roll(ext, s, 0)
            dw_ref[j:j + 1, :] += jnp.sum(dy * past[HALO:HALO + tm, :], axis=0, keepdims=True)
        dx_ref[...] = dx

    row = lambda i: (i, 0)
    const = lambda i: (0, 0)
    prev, nxt = _halo_specs(tm, cw, bps, t // tm)
    return pl.pallas_call(
        body, name=name, grid=(t // tm,),
        in_specs=[pl.BlockSpec((tm, cw), row), prev, pl.BlockSpec((tm, cw), row), nxt, pl.BlockSpec((CONV_K, cw), const)],
        out_specs=(pl.BlockSpec((tm, cw), row), pl.BlockSpec((HALO, cw), const)),
        out_shape=(jax.ShapeDtypeStruct((t, cw), F32), jax.ShapeDtypeStruct((HALO, cw), F32)),
        compiler_params=_cparams(("arbitrary",)),
    )(qkv, qkv, dy, dy, conv_w)


def _inv_unit_lower(l_mats, eye):
    invs = [eye - l for l in l_mats]
    powers = list(l_mats)
    n = 2
    while n < eye.shape[0]:
        powers = [mmh(p, p) for p in powers]
        invs = [inv + mmh(inv, p) for inv, p in zip(invs, powers)]
        n *= 2
    return invs


@jax.custom_vjp
def _solve(l_mat, rhs, inv):
    return mmx(inv, rhs)


def _solve_fwd(l_mat, rhs, inv):
    sol = mmx(inv, rhs)
    return sol, (inv, sol)


def _solve_bwd(res, d_sol):
    inv, sol = res
    d_rhs = mmx_tn(inv, d_sol)
    return -mmx_nt(d_rhs, sol), d_rhs, jnp.zeros_like(inv)


_solve.defvjp(_solve_fwd, _solve_bwd)


def _prep_fn(q, k, v, gc, gr, b, inv):
    ids = range(len(q))
    c = q[0].shape[0]
    rr = lax.broadcasted_iota(jnp.int32, (c, c), 0)
    cc = lax.broadcasted_iota(jnp.int32, (c, c), 1)
    incl, strict = rr >= cc, rr > cc
    is_last = lax.broadcasted_iota(jnp.int32, (c, 1), 0) == c - 1
    qs = [q[i] * (DN_HEAD_DIM ** -0.5) for i in ids]
    decay = [jnp.where(incl, jnp.exp(jnp.where(incl, gc[i] - gr[i], 0.0)), 0.0) for i in ids]
    kb = [k[i] * b[i] for i in ids]
    vb = [v[i] * b[i] for i in ids]
    kk = [mm_nt(kb[i], k[i]) for i in ids]
    l_mat = [jnp.where(strict, kk[i] * decay[i], 0.0) for i in ids]
    eg = [jnp.exp(gc[i]) for i in ids]
    if inv is None:
        inv = _inv_unit_lower(l_mat, jnp.where(rr == cc, 1.0, 0.0).astype(F32))
    u_wy = [_solve(l_mat[i], vb[i], inv[i]) for i in ids]
    w_wy = [_solve(l_mat[i], kb[i] * eg[i], inv[i]) for i in ids]
    qk = [mm_nt(qs[i], k[i]) * decay[i] for i in ids]
    g_last = [jnp.sum(jnp.where(is_last, gc[i], 0.0), axis=0, keepdims=True) for i in ids]
    k_dec = [k[i] * jnp.exp(g_last[i] - gc[i]) for i in ids]
    egl = [jnp.broadcast_to(jnp.exp(g_last[i]), (1, LANES)) for i in ids]
    return [(w_wy[i], u_wy[i], qs[i] * eg[i], k_dec[i], qk[i], egl[i]) for i in ids], inv


def _seq_fn(w, u, qd, kd, qk, egl, s):
    ids = range(len(w))
    ws = [mm(w[i], s[i]) for i in ids]
    qs = [mm(qd[i], s[i]) for i in ids]
    v_new = [u[i] - ws[i] for i in ids]
    o = [qs[i] + mm(qk[i], v_new[i]) for i in ids]
    s_new = [s[i] * egl[i] + mm_tn(kd[i], v_new[i]) for i in ids]
    return o, s_new


def _lane_col(a, h):
    lane = lax.broadcasted_iota(jnp.int32, (1, LANES), 1)
    return jnp.sum(jnp.where(lane == h, a, 0.0), axis=1, keepdims=True)


def _col_lane(col, h):
    lane = lax.broadcasted_iota(jnp.int32, (1, LANES), 1)
    return jnp.where(lane == h, col, 0.0)


def _head_cols(h):
    return slice(h * DN_HEAD_DIM, (h + 1) * DN_HEAD_DIM)


def _chunk_rows(n):
    return pl.ds(pl.multiple_of(n * DN_CHUNK, DN_CHUNK), DN_CHUNK)


def _delta_prep(q, k, v, gc, grow, beta, *, name):
    t = q.shape[0]
    tm = _tm(t)
    cpb = tm // DN_CHUNK
    n_chunks = t // DN_CHUNK
    group = 2

    def body(q_ref, k_ref, v_ref, gc_ref, gr_ref, b_ref, w_ref, u_ref, qd_ref, kd_ref, qk_ref, egl_ref, inv_ref):
        def step(m, carry):
            probs = [(m * group + e, h) for e in range(group) for h in range(DN_HEADS)]
            gcb = [gc_ref[_chunk_rows(m * group + e), :] for e in range(group)]
            bb = [b_ref[_chunk_rows(m * group + e), :] for e in range(group)]
            grb = [gr_ref[m * group + e] for e in range(group)]
            for e in range(group):
                egl_ref[m * group + e] = jnp.zeros((HALO, LANES), F32)
            outs, invs = _prep_fn(
                [q_ref[_chunk_rows(n), _head_cols(h)] for n, h in probs], [k_ref[_chunk_rows(n), _head_cols(h)] for n, h in probs],
                [v_ref[_chunk_rows(n), _head_cols(h)] for n, h in probs],
                [_lane_col(gcb[e], h) for e in range(group) for h in range(DN_HEADS)],
                [grb[e][h:h + 1, :] for e in range(group) for h in range(DN_HEADS)],
                [_lane_col(bb[e], h) for e in range(group) for h in range(DN_HEADS)], None)
            for (n, h), (w, u, qd, kd, qk, egl), inv in zip(probs, outs, invs):
                rows, cols = _chunk_rows(n), _head_cols(h)
                w_ref[rows, cols] = w.astype(BF16)
                u_ref[rows, cols] = u
                qd_ref[rows, cols] = qd.astype(BF16)
                kd_ref[rows, cols] = kd.astype(BF16)
                qk_ref[n, h] = qk
                inv_ref[n, h] = inv
                egl_ref[n, h:h + 1, :] = egl
            return carry

        lax.fori_loop(0, cpb // group, step, 0)

    row = lambda i: (i, 0)
    tok = pl.BlockSpec((tm, DN_WIDTH), row)
    lanes = pl.BlockSpec((tm, LANES), row)
    sq = pl.BlockSpec((cpb, DN_HEADS, DN_CHUNK, DN_CHUNK), lambda i: (i, 0, 0, 0))
    return pl.pallas_call(
        body, name=name, grid=(t // tm,),
        in_specs=[tok, tok, tok, lanes, pl.BlockSpec((cpb, HALO, DN_CHUNK), lambda i: (i, 0, 0)), lanes],
        out_specs=(tok, tok, tok, tok, sq, pl.BlockSpec((cpb, HALO, LANES), lambda i: (i, 0, 0)), sq),
        out_shape=(jax.ShapeDtypeStruct((t, DN_WIDTH), BF16), jax.ShapeDtypeStruct((t, DN_WIDTH), F32),
                   jax.ShapeDtypeStruct((t, DN_WIDTH), BF16), jax.ShapeDtypeStruct((t, DN_WIDTH), BF16),
                   jax.ShapeDtypeStruct((n_chunks, DN_HEADS, DN_CHUNK, DN_CHUNK), F32),
                   jax.ShapeDtypeStruct((n_chunks, HALO, LANES), F32),
                   jax.ShapeDtypeStruct((n_chunks, DN_HEADS, DN_CHUNK, DN_CHUNK), F32)),
        compiler_params=_cparams(("parallel",)),
    )(q, k, v, gc, grow, beta)


def _delta_par_bwd(q, k, v, gc, grow, beta, inv, dw, du, dqd, dkd, dqk, degl, *, name):
    t = q.shape[0]
    tm = _tm(t)
    cpb = tm // DN_CHUNK
    n_chunks = t // DN_CHUNK
    group = 2

    def body(q_ref, k_ref, v_ref, gc_ref, gr_ref, b_ref, inv_ref, dw_ref, du_ref, dqd_ref, dkd_ref, dqk_ref, degl_ref,
             dq_ref, dk_ref, dv_ref, dgc_ref, dgr_ref, db_ref):
        def step(m, carry):
            chunks = [m * group + e for e in range(group)]
            probs = [(e, h) for e in range(group) for h in range(DN_HEADS)]
            rows = [_chunk_rows(n) for n in chunks]
            gcb, bb = [gc_ref[r, :] for r in rows], [b_ref[r, :] for r in rows]
            grb, deglb = [gr_ref[n] for n in chunks], [degl_ref[n] for n in chunks]
            for n in chunks:
                dgr_ref[n] = jnp.zeros((HALO, DN_CHUNK), F32)
            invs = [inv_ref[chunks[e], h] for e, h in probs]
            _, vjp = jax.vjp(lambda *a: _prep_fn(*a, invs)[0],
                             [q_ref[rows[e], _head_cols(h)] for e, h in probs], [k_ref[rows[e], _head_cols(h)] for e, h in probs],
                             [v_ref[rows[e], _head_cols(h)] for e, h in probs], [_lane_col(gcb[e], h) for e, h in probs],
                             [grb[e][h:h + 1, :] for e, h in probs], [_lane_col(bb[e], h) for e, h in probs])
            dq, dk, dv, dgc, dgr, db = vjp([(dw_ref[rows[e], _head_cols(h)], du_ref[rows[e], _head_cols(h)],
                                             dqd_ref[rows[e], _head_cols(h)], dkd_ref[rows[e], _head_cols(h)],
                                             dqk_ref[chunks[e], h], deglb[e][h:h + 1, :]) for e, h in probs])
            dgc_acc = [jnp.zeros((DN_CHUNK, LANES), F32) for _ in chunks]
            db_acc = [jnp.zeros((DN_CHUNK, LANES), F32) for _ in chunks]
            for i, (e, h) in enumerate(probs):
                cols = _head_cols(h)
                dq_ref[rows[e], cols] = dq[i]
                dk_ref[rows[e], cols] = dk[i]
                dv_ref[rows[e], cols] = dv[i]
                dgr_ref[chunks[e], h:h + 1, :] = dgr[i]
                dgc_acc[e] = dgc_acc[e] + _col_lane(dgc[i], h)
                db_acc[e] = db_acc[e] + _col_lane(db[i], h)
            for e in range(group):
                dgc_ref[rows[e], :] = dgc_acc[e]
                db_ref[rows[e], :] = db_acc[e]
            return carry

        lax.fori_loop(0, cpb // group, step, 0)

    row = lambda i: (i, 0)
    tok = pl.BlockSpec((tm, DN_WIDTH), row)
    lanes = pl.BlockSpec((tm, LANES), row)
    sq = pl.BlockSpec((cpb, DN_HEADS, DN_CHUNK, DN_CHUNK), lambda i: (i, 0, 0, 0))
    grs = pl.BlockSpec((cpb, HALO, DN_CHUNK), lambda i: (i, 0, 0))
    return pl.pallas_call(
        body, name=name, grid=(t // tm,),
        in_specs=[tok, tok, tok, lanes, grs, lanes, sq, tok, tok, tok, tok, sq, pl.BlockSpec((cpb, HALO, LANES), lambda i: (i, 0, 0))],
        out_specs=(tok, tok, tok, lanes, grs, lanes),
        out_shape=(jax.ShapeDtypeStruct((t, DN_WIDTH), F32),) * 3
        + (jax.ShapeDtypeStruct((t, LANES), F32), jax.ShapeDtypeStruct((n_chunks, HALO, DN_CHUNK), F32),
           jax.ShapeDtypeStruct((t, LANES), F32)),
        compiler_params=_cparams(("parallel",)),
    )(q, k, v, gc, grow, beta, inv, dw, du, dqd, dkd, dqk, degl)


def _seq_specs(n_seq, seq, reverse):
    tm = _tm(seq)
    nb = seq // tm
    cpb = tm // DN_CHUNK
    blk = (lambda b, j: b * nb + nb - 1 - j) if reverse else (lambda b, j: b * nb + j)
    tok = pl.BlockSpec((tm, DN_WIDTH), lambda b, j: (blk(b, j), 0))
    sq = pl.BlockSpec((cpb, DN_HEADS, DN_CHUNK, DN_CHUNK), lambda b, j: (blk(b, j), 0, 0, 0))
    rows8 = pl.BlockSpec((cpb, HALO, LANES), lambda b, j: (blk(b, j), 0, 0))
    state = pl.BlockSpec((cpb, DN_HEADS, DN_HEAD_DIM, DN_HEAD_DIM), lambda b, j: (blk(b, j), 0, 0, 0))
    return nb, cpb, tok, sq, rows8, state


def _delta_seq_fwd(w, u, qd, kd, qk, egl, n_seq, seq, *, name):
    nb, cpb, tok, sq, rows8, state = _seq_specs(n_seq, seq, False)
    t = n_seq * seq

    def body(w_ref, u_ref, qd_ref, kd_ref, qk_ref, egl_ref, o_ref, st_ref, s_s):
        @pl.when(pl.program_id(1) == 0)
        def _():
            s_s[...] = jnp.zeros_like(s_s)

        def step(n, carry):
            rows = _chunk_rows(n)
            heads = range(DN_HEADS)
            eglb = egl_ref[n]
            s = [s_s[h] for h in heads]
            for h in heads:
                st_ref[n, h] = s[h]
            o, s_new = _seq_fn([w_ref[rows, _head_cols(h)] for h in heads], [u_ref[rows, _head_cols(h)] for h in heads],
                               [qd_ref[rows, _head_cols(h)] for h in heads], [kd_ref[rows, _head_cols(h)] for h in heads],
                               [qk_ref[n, h] for h in heads], [eglb[h:h + 1, :] for h in heads], s)
            for h in heads:
                o_ref[rows, _head_cols(h)] = o[h]
                s_s[h] = s_new[h]
            return carry

        lax.fori_loop(0, cpb, step, 0)

    return pl.pallas_call(
        body, name=name, grid=(n_seq, nb),
        in_specs=[tok, tok, tok, tok, sq, rows8],
        out_specs=(tok, state),
        out_shape=(jax.ShapeDtypeStruct((t, DN_WIDTH), F32),
                   jax.ShapeDtypeStruct((t // DN_CHUNK, DN_HEADS, DN_HEAD_DIM, DN_HEAD_DIM), F32)),
        scratch_shapes=[pltpu.VMEM((DN_HEADS, DN_HEAD_DIM, DN_HEAD_DIM), F32)],
        compiler_params=_cparams(("parallel", "arbitrary")),
    )(w, u, qd, kd, qk, egl)


def _delta_seq_bwd(w, u, qd, kd, qk, egl, states, do, n_seq, seq, *, name):
    nb, cpb, tok, sq, rows8, state = _seq_specs(n_seq, seq, True)
    t = n_seq * seq

    def body(w_ref, u_ref, qd_ref, kd_ref, qk_ref, egl_ref, st_ref, do_ref, dw_ref, du_ref, dqd_ref, dkd_ref, dqk_ref,
             degl_ref, ds_s):
        @pl.when(pl.program_id(1) == 0)
        def _():
            ds_s[...] = jnp.zeros_like(ds_s)

        def step(m, carry):
            n = cpb - 1 - m
            rows = _chunk_rows(n)
            eglb = egl_ref[n]
            degl_ref[n] = jnp.zeros((HALO, LANES), F32)
            heads = range(DN_HEADS)
            _, vjp = jax.vjp(_seq_fn, [w_ref[rows, _head_cols(h)].astype(F32) for h in heads],
                             [u_ref[rows, _head_cols(h)] for h in heads],
                             [qd_ref[rows, _head_cols(h)].astype(F32) for h in heads],
                             [kd_ref[rows, _head_cols(h)].astype(F32) for h in heads],
                             [qk_ref[n, h] for h in heads], [eglb[h:h + 1, :] for h in heads], [st_ref[n, h] for h in heads])
            dw, du, dqd, dkd, dqk, degl, ds_in = vjp(([do_ref[rows, _head_cols(h)] for h in heads], [ds_s[h] for h in heads]))
            for h in heads:
                cols = _head_cols(h)
                dw_ref[rows, cols] = dw[h]
                du_ref[rows, cols] = du[h]
                dqd_ref[rows, cols] = dqd[h]
                dkd_ref[rows, cols] = dkd[h]
                dqk_ref[n, h] = dqk[h]
                degl_ref[n, h:h + 1, :] = degl[h]
                ds_s[h] = ds_in[h]
            return carry

        lax.fori_loop(0, cpb, step, 0)

    return pl.pallas_call(
        body, name=name, grid=(n_seq, nb),
        in_specs=[tok, tok, tok, tok, sq, rows8, state, tok],
        out_specs=(tok, tok, tok, tok, sq, rows8),
        out_shape=(jax.ShapeDtypeStruct((t, DN_WIDTH), F32),) * 4
        + (jax.ShapeDtypeStruct((t // DN_CHUNK, DN_HEADS, DN_CHUNK, DN_CHUNK), F32),
           jax.ShapeDtypeStruct((t // DN_CHUNK, HALO, LANES), F32)),
        scratch_shapes=[pltpu.VMEM((DN_HEADS, DN_HEAD_DIM, DN_HEAD_DIM), F32)],
        compiler_params=_cparams(("parallel", "arbitrary")),
    )(w, u, qd, kd, qk, egl, states, do)


def _dn_gate(o, z, dnw):
    return o * lax.rsqrt(jnp.mean(o * o, axis=-1, keepdims=True) + EPS) * dnw * _silu(z)


def _mix_out_fwd(x, sg, o, z, wo_sg, wo_dn, dnw, *, name):
    t = x.shape[0]
    tm = _tm(t)

    def body(x_ref, sg_ref, o_ref, z_ref, wsg_ref, wdn_ref, dnw_ref, y_ref, dn_s):
        for h, (oh, zh) in enumerate(zip(_split_heads(o_ref, 0), _split_heads(z_ref, 0))):
            dn_s[:, h * DN_HEAD_DIM:(h + 1) * DN_HEAD_DIM] = _dn_gate(oh, zh, dnw_ref[...]).astype(BF16)
        y_ref[...] = (x_ref[...] + jnp.dot(sg_ref[...].astype(BF16), wsg_ref[...], preferred_element_type=F32)
                      + jnp.dot(dn_s[...], wdn_ref[...], preferred_element_type=F32))

    row = lambda i: (i, 0)
    const = lambda i: (0, 0)
    half = pl.BlockSpec((tm, DN_WIDTH), row)
    return pl.pallas_call(
        body, name=name, grid=(t // tm,),
        in_specs=[pl.BlockSpec((tm, D_MODEL), row), half, half, half, pl.BlockSpec((SG_WIDTH, D_MODEL), const),
                  pl.BlockSpec((DN_WIDTH, D_MODEL), const), pl.BlockSpec((1, DN_HEAD_DIM), const)],
        out_specs=pl.BlockSpec((tm, D_MODEL), row),
        out_shape=jax.ShapeDtypeStruct((t, D_MODEL), F32),
        scratch_shapes=[pltpu.VMEM((tm, DN_WIDTH), BF16)],
        compiler_params=_cparams(("parallel",)),
    )(x, sg, o, z, wo_sg, wo_dn, dnw)


def _mix_out_bwd(dy, sg, o, z, wo_sg, wo_dn, dnw, *, name):
    t = dy.shape[0]
    tm = _tm(t)

    def body(dy_ref, sg_ref, o_ref, z_ref, wsg_ref, wdn_ref, dnw_ref, dsg_ref, do_ref, dz_ref, dwsg_ref, dwdn_ref, ddnw_ref, dn_s):
        i = pl.program_id(0)
        dyb = dy_ref[...].astype(BF16)
        nt = (((1,), (1,)), ((), ()))
        tn = (((0,), (0,)), ((), ()))
        dsg_ref[...] = lax.dot_general(dyb, wsg_ref[...], nt, preferred_element_type=F32)
        ddn = lax.dot_general(dyb, wdn_ref[...], nt, preferred_element_type=F32)
        ddnw = None
        for h, (oh, zh) in enumerate(zip(_split_heads(o_ref, 0), _split_heads(z_ref, 0))):
            cols = slice(h * DN_HEAD_DIM, (h + 1) * DN_HEAD_DIM)
            out, vjp = jax.vjp(_dn_gate, oh, zh, dnw_ref[...])
            dn_s[:, cols] = out.astype(BF16)
            doh, dzh, dw = vjp(ddn[:, cols])
            do_ref[:, cols] = doh
            dz_ref[:, cols] = dzh
            ddnw = dw if ddnw is None else ddnw + dw
        _acc_out(ddnw_ref, i == 0, ddnw)
        _acc_out(dwsg_ref, i == 0, lax.dot_general(sg_ref[...].astype(BF16), dyb, tn, preferred_element_type=F32))
        _acc_out(dwdn_ref, i == 0, lax.dot_general(dn_s[...], dyb, tn, preferred_element_type=F32))

    row = lambda i: (i, 0)
    const = lambda i: (0, 0)
    half = pl.BlockSpec((tm, DN_WIDTH), row)
    wspec = pl.BlockSpec((DN_WIDTH, D_MODEL), const)
    return pl.pallas_call(
        body, name=name, grid=(t // tm,),
        in_specs=[pl.BlockSpec((tm, D_MODEL), row), half, half, half, wspec, wspec, pl.BlockSpec((1, DN_HEAD_DIM), const)],
        out_specs=(half, half, half, wspec, wspec, pl.BlockSpec((1, DN_HEAD_DIM), const)),
        out_shape=(jax.ShapeDtypeStruct((t, DN_WIDTH), F32),) * 3 + (jax.ShapeDtypeStruct((DN_WIDTH, D_MODEL), F32),) * 2
        + (jax.ShapeDtypeStruct((1, DN_HEAD_DIM), F32),),
        scratch_shapes=[pltpu.VMEM((tm, DN_WIDTH), BF16)],
        compiler_params=_cparams(("arbitrary",)),
    )(dy, sg, o, z, wo_sg, wo_dn, dnw)


_MESH = pl.DeviceIdType.MESH
_HBM = pl.BlockSpec(memory_space=pl.ANY)


def _mesh_pos():
    x, y, c = lax.axis_index("x"), lax.axis_index("y"), lax.axis_index("c")
    return x, y, c, [(1 - x, y), (x, 1 - y), (1 - x, 1 - y)]


def _gather2(arrs, *, name):
    n = len(arrs)
    slots = N_DEV - 1

    def body(*refs):
        in_refs, out_refs = refs[:n], refs[n:2 * n]
        send_sems, recv_sems, local_sems = refs[2 * n:]
        x, y, c, chips = _mesh_pos()
        me, sibling = (x, y, c), (x, y, 1 - c)

        def copy(k, slot, block, to, src=None):
            dst = out_refs[k].at[4 * block[0] + 2 * block[1] + block[2]]
            return pltpu.make_async_remote_copy(src_ref=dst if src is None else src, dst_ref=dst,
                                                send_sem=send_sems.at[k * slots + slot], recv_sem=recv_sems.at[k * slots + slot],
                                                device_id=to, device_id_type=_MESH)

        local = [pltpu.make_async_copy(in_refs[k], out_refs[k].at[4 * x + 2 * y + c], local_sems.at[k]) for k in range(n)]
        sent = []
        for k in range(n):
            sent.append(copy(k, 0, me, sibling, src=in_refs[k]))
            sent += [copy(k, 1 + j, me, (*chip, c), src=in_refs[k]) for j, chip in enumerate(chips)]
        for cp in local + sent:
            cp.start()
        for j, chip in enumerate(chips):
            for k in range(n):
                copy(k, 1 + j, (*chip, c), me).wait_recv()
                passed = copy(k, 4 + j, (*chip, c), sibling)
                passed.start()
                sent.append(passed)
        for k in range(n):
            copy(k, 0, sibling, me).wait_recv()
            for j, chip in enumerate(chips):
                copy(k, 4 + j, (*chip, 1 - c), me).wait_recv()
        for cp in sent:
            cp.wait_send()
        for cp in local:
            cp.wait()

    return pl.pallas_call(
        body, name=name, in_specs=[_HBM] * n, out_specs=(_HBM,) * n,
        out_shape=tuple(jax.ShapeDtypeStruct((N_DEV,) + a.shape, a.dtype) for a in arrs),
        scratch_shapes=[pltpu.SemaphoreType.DMA((n * slots,)), pltpu.SemaphoreType.DMA((n * slots,)),
                        pltpu.SemaphoreType.DMA((n,))],
    )(*arrs)


def _pair_swap(arrs, *, name):
    n = len(arrs)

    def body(*refs):
        in_refs, out_refs, send_sems, recv_sems = refs[:n], refs[n:2 * n], refs[2 * n], refs[2 * n + 1]
        x, y, c, _ = _mesh_pos()
        copies = [pltpu.make_async_remote_copy(src_ref=in_refs[k].at[1 - c], dst_ref=out_refs[k], send_sem=send_sems.at[k],
                                               recv_sem=recv_sems.at[k], device_id=(x, y, 1 - c), device_id_type=_MESH)
                  for k in range(n)]
        for cp in copies:
            cp.start()
        for cp in copies:
            cp.wait()

    return pl.pallas_call(
        body, name=name, in_specs=[_HBM] * n, out_specs=(_HBM,) * n,
        out_shape=tuple(jax.ShapeDtypeStruct(a.shape[1:], a.dtype) for a in arrs),
        scratch_shapes=[pltpu.SemaphoreType.DMA((n,)), pltpu.SemaphoreType.DMA((n,))],
    )(*arrs)


def _chip_exchange(arrs, *, name):
    n = len(arrs)
    slots = 3

    def body(*refs):
        in_refs, out_refs = refs[:n], refs[n:2 * n]
        send_sems, recv_sems, local_sems = refs[2 * n:]
        x, y, c, chips = _mesh_pos()
        mine = 2 * x + y
        copies = [pltpu.make_async_copy(in_refs[k].at[mine], out_refs[k].at[mine], local_sems.at[k]) for k in range(n)]
        for j, chip in enumerate(chips):
            for k in range(n):
                copies.append(pltpu.make_async_remote_copy(
                    src_ref=in_refs[k].at[2 * chip[0] + chip[1]], dst_ref=out_refs[k].at[mine],
                    send_sem=send_sems.at[k * slots + j], recv_sem=recv_sems.at[k * slots + j],
                    device_id=(*chip, c), device_id_type=_MESH))
        for cp in copies:
            cp.start()
        for cp in copies:
            cp.wait()

    return pl.pallas_call(
        body, name=name, in_specs=[_HBM] * n, out_specs=(_HBM,) * n,
        out_shape=tuple(jax.ShapeDtypeStruct(a.shape, a.dtype) for a in arrs),
        scratch_shapes=[pltpu.SemaphoreType.DMA((n * slots,)), pltpu.SemaphoreType.DMA((n * slots,)),
                        pltpu.SemaphoreType.DMA((n,))],
    )(*arrs)


def _pair_add(p, r, core, *, name):
    _, n_chip, rows, cols = p.shape
    rb = _row_block(rows)

    def body(core_ref, p_ref, r_ref, o_ref):
        o_ref[...] = (p_ref[...] + r_ref[...]).astype(BF16)

    return pl.pallas_call(
        body, name=name,
        grid_spec=pltpu.PrefetchScalarGridSpec(
            num_scalar_prefetch=1, grid=(n_chip, rows // rb),
            in_specs=[pl.BlockSpec((None, None, rb, cols), lambda s, i, core_ref: (core_ref[0], s, i, 0)),
                      pl.BlockSpec((None, rb, cols), lambda s, i, core_ref: (s, i, 0))],
            out_specs=pl.BlockSpec((None, rb, cols), lambda s, i, core_ref: (s, i, 0))),
        out_shape=jax.ShapeDtypeStruct((n_chip, rows, cols), BF16),
        compiler_params=_cparams(("parallel", "parallel")),
    )(core, p, r)


def _row_block(rows, limit=256):
    best = rows
    for cand in range(8, limit + 1, 8):
        if rows % cand == 0:
            best = cand
    return best if rows > limit else rows


def _adam(gp, w, m, v, *, name):
    p, rows, cols = gp.shape
    rb = _row_block(rows)

    def body(gp_ref, w_ref, m_ref, v_ref, g_ref, d_ref, m2_ref, v2_ref):
        g = gp_ref[0].astype(F32)
        for s in range(1, p):
            g = g + gp_ref[s].astype(F32)
        m2 = ADAM_B1 * m_ref[...] + (1.0 - ADAM_B1) * g
        v2 = ADAM_B2 * v_ref[...] + (1.0 - ADAM_B2) * (g * g)
        m_hat = m2 / (1.0 - ADAM_B1 ** ADAM_STEP)
        v_hat = v2 / (1.0 - ADAM_B2 ** ADAM_STEP)
        g_ref[...] = g
        d_ref[...] = -ADAM_LR * (m_hat / (jnp.sqrt(v_hat) + ADAM_EPS) + ADAM_WD * w_ref[...])
        m2_ref[...] = m2
        v2_ref[...] = v2

    blk = pl.BlockSpec((rb, cols), lambda i: (i, 0))
    return pl.pallas_call(
        body, name=name, grid=(rows // rb,),
        in_specs=[pl.BlockSpec((p, rb, cols), lambda i: (0, i, 0)), blk, blk, blk],
        out_specs=(blk,) * 4, out_shape=(jax.ShapeDtypeStruct((rows, cols), F32),) * 4,
        compiler_params=_cparams(("parallel",)),
    )(gp, w, m, v)


def _cols_full(g):
    return jnp.transpose(g, (1, 0, 2)).reshape(g.shape[1], N_DEV * g.shape[2])


def _cols_pieces(full):
    r, c = full.shape
    return jnp.transpose(full.reshape(r, N_DEV, c // N_DEV), (1, 0, 2))


def _pad_lanes(a, width=LANES):
    return jnp.pad(a, ((0, 0), (0, width - a.shape[1])))


def _chunk_rows_of(a):
    by_chunk = jnp.transpose(a[:, :DN_HEADS].reshape(-1, DN_CHUNK, DN_HEADS), (0, 2, 1))
    return jnp.pad(by_chunk, ((0, 0), (0, HALO - DN_HEADS), (0, 0)))


_SMALL = (("ffn1_norm", D_MODEL), ("mix_norm", D_MODEL), ("ffn2_norm", D_MODEL), ("final_norm", D_MODEL), ("a_log", DN_HEADS),
          ("dt_bias", DN_HEADS), ("dn_norm", DN_HEAD_DIM), ("sg_ln_g", SG_WIDTH), ("sg_ln_b", SG_WIDTH),
          ("sg_w", SG_GROUPS * SG_CHUNK * SG_CHUNK), ("sg_b", SG_GROUPS * SG_CHUNK), ("conv_w", CONV_K * 3 * DN_WIDTH))
_SMALL_ROWS = 1128
_SMALL_SHAPES = {"ffn1_norm": (1, D_MODEL), "mix_norm": (1, D_MODEL), "ffn2_norm": (1, D_MODEL), "final_norm": (D_MODEL,),
                 "a_log": (1, DN_HEADS), "dt_bias": (1, DN_HEADS), "dn_norm": (1, DN_HEAD_DIM), "sg_ln_g": (1, SG_WIDTH),
                 "sg_ln_b": (1, SG_WIDTH), "sg_w": (1, SG_GROUPS, SG_CHUNK, SG_CHUNK), "sg_b": (1, SG_GROUPS, SG_CHUNK)}


def _pack_small(d):
    flat = jnp.concatenate([d[name].reshape(-1) for name, _ in _SMALL])
    return jnp.pad(flat, (0, _SMALL_ROWS * LANES - flat.shape[0])).reshape(_SMALL_ROWS, LANES)


def _unpack_small(a):
    flat, out, at = a.reshape(-1), {}, 0
    for name, size in _SMALL:
        out[name] = flat[at:at + size]
        at += size
    return out


def kernel(x, ffn1_norm, ffn1_w_gate, ffn1_w_up, ffn1_w_down, mix_norm, w_in, conv_w, a_log, dt_bias, dn_norm, sg_ln_g, sg_ln_b, sg_w, sg_b, w_out, ffn2_norm, ffn2_w_gate, ffn2_w_up, ffn2_w_down, final_norm, loss_target, m_ffn1_norm, m_ffn1_w_gate, m_ffn1_w_up, m_ffn1_w_down, m_mix_norm, m_w_in, m_conv_w, m_a_log, m_dt_bias, m_dn_norm, m_sg_ln_g, m_sg_ln_b, m_sg_w, m_sg_b, m_w_out, m_ffn2_norm, m_ffn2_w_gate, m_ffn2_w_up, m_ffn2_w_down, m_final_norm, v_ffn1_norm, v_ffn1_w_gate, v_ffn1_w_up, v_ffn1_w_down, v_mix_norm, v_w_in, v_conv_w, v_a_log, v_dt_bias, v_dn_norm, v_sg_ln_g, v_sg_ln_b, v_sg_w, v_sg_b, v_w_out, v_ffn2_norm, v_ffn2_w_gate, v_ffn2_w_up, v_ffn2_w_down, v_final_norm):
    weights = dict(ffn1_norm=ffn1_norm, ffn1_w_gate=ffn1_w_gate, ffn1_w_up=ffn1_w_up, ffn1_w_down=ffn1_w_down, mix_norm=mix_norm, w_in=w_in, conv_w=conv_w, a_log=a_log, dt_bias=dt_bias, dn_norm=dn_norm, sg_ln_g=sg_ln_g, sg_ln_b=sg_ln_b, sg_w=sg_w, sg_b=sg_b, w_out=w_out, ffn2_norm=ffn2_norm, ffn2_w_gate=ffn2_w_gate, ffn2_w_up=ffn2_w_up, ffn2_w_down=ffn2_w_down, final_norm=final_norm)
    mom_m = dict(ffn1_norm=m_ffn1_norm, ffn1_w_gate=m_ffn1_w_gate, ffn1_w_up=m_ffn1_w_up, ffn1_w_down=m_ffn1_w_down, mix_norm=m_mix_norm, w_in=m_w_in, conv_w=m_conv_w, a_log=m_a_log, dt_bias=m_dt_bias, dn_norm=m_dn_norm, sg_ln_g=m_sg_ln_g, sg_ln_b=m_sg_ln_b, sg_w=m_sg_w, sg_b=m_sg_b, w_out=m_w_out, ffn2_norm=m_ffn2_norm, ffn2_w_gate=m_ffn2_w_gate, ffn2_w_up=m_ffn2_w_up, ffn2_w_down=m_ffn2_w_down, final_norm=m_final_norm)
    mom_v = dict(ffn1_norm=v_ffn1_norm, ffn1_w_gate=v_ffn1_w_gate, ffn1_w_up=v_ffn1_w_up, ffn1_w_down=v_ffn1_w_down, mix_norm=v_mix_norm, w_in=v_w_in, conv_w=v_conv_w, a_log=v_a_log, dt_bias=v_dt_bias, dn_norm=v_dn_norm, sg_ln_g=v_sg_ln_g, sg_ln_b=v_sg_ln_b, sg_w=v_sg_w, sg_b=v_sg_b, w_out=v_w_out, ffn2_norm=v_ffn2_norm, ffn2_w_gate=v_ffn2_w_gate, ffn2_w_up=v_ffn2_w_up, ffn2_w_down=v_ffn2_w_down, final_norm=v_final_norm)
    order = list(weights)
    big = ("ffn1_w_gate", "ffn1_w_up", "ffn1_w_down", "w_in", "w_out", "ffn2_w_gate", "ffn2_w_up", "ffn2_w_down")
    col_sharded = ("ffn1_w_gate", "ffn1_w_up", "w_in", "ffn2_w_gate", "ffn2_w_up")

    n_seq, seq, _ = x.shape
    t = n_seq * seq
    me = 4 * lax.axis_index("x") + 2 * lax.axis_index("y") + lax.axis_index("c")
    x0 = x.reshape(t, D_MODEL)
    tgt = loss_target.reshape(t, D_MODEL)

    gathered = _gather2([weights[n][0].astype(BF16) for n in big] + [conv_w[0]], name="gather_weights")
    full = {n: (_cols_full(g) if n in col_sharded else g.reshape(-1, g.shape[-1])) for n, g in zip(big, gathered)}
    conv_full = _cols_full(gathered[-1])
    w_in_f = full["w_in"]
    offs = (0, SG_WIDTH, 2 * SG_WIDTH, 2 * SG_WIDTH + 3 * DN_WIDTH, 2 * SG_WIDTH + 4 * DN_WIDTH)
    n_proj = offs[-1]
    ws = [w_in_f[:, offs[0]:offs[1]], w_in_f[:, offs[1]:offs[2]], w_in_f[:, offs[2]:offs[3]], w_in_f[:, offs[3]:offs[4]],
          _pad_lanes(w_in_f[:, n_proj:n_proj + DN_HEADS]), _pad_lanes(w_in_f[:, n_proj + DN_HEADS:n_proj + 2 * DN_HEADS])]
    wo_sg, wo_dn = full["w_out"][:SG_WIDTH], full["w_out"][SG_WIDTH:]
    alog, dtb = _pad_lanes(a_log), _pad_lanes(dt_bias)
    sgbt = _pad_lanes(sg_b[0].T)
    fnw = final_norm.reshape(1, D_MODEL)

    x1 = _ffn_fwd(x0, ffn1_norm, full["ffn1_w_gate"], full["ffn1_w_up"], full["ffn1_w_down"], name="ffn1_fwd")
    u, v, qkv, z, bpre, apre = _mix_in_fwd(x1, mix_norm, ws, name="mix_in_fwd")
    sg_out = _sg_fwd(u, v, sg_ln_g, sg_ln_b, sg_w[0], sgbt, name="sg_fwd")
    q, k, vv, beta, gc = _dn_prep_fwd(qkv, bpre, apre, conv_full, alog, dtb, seq, name="dn_prep_fwd")
    grow = _chunk_rows_of(gc)
    wy_w, wy_u, q_dec, k_dec, qk, egl, inv = _delta_prep(q, k, vv, gc, grow, beta, name="delta_prep")
    o, states = _delta_seq_fwd(wy_w, wy_u, q_dec, k_dec, qk, egl, n_seq, seq, name="delta_seq_fwd")
    x2 = _mix_out_fwd(x1, sg_out, o, z, wo_sg, wo_dn, dn_norm, name="mix_out_fwd")
    dx3, loss_part, d_fn = _ffn_fwd(x2, ffn2_norm, full["ffn2_w_gate"], full["ffn2_w_up"], full["ffn2_w_down"], tgt, fnw,
                                    name="ffn2_fwd_loss")
    loss = lax.psum(loss_part[0, 0], ("x", "y", "c"))

    dx2, d_n2, d_g2, d_u2, d_d2 = _ffn_bwd(x2, ffn2_norm, full["ffn2_w_gate"], full["ffn2_w_up"], full["ffn2_w_down"], dx3,
                                           name="ffn2_bwd")
    dsg, do, dz, d_wo_sg, d_wo_dn, d_dnw = _mix_out_bwd(dx2, sg_out, o, z, wo_sg, wo_dn, dn_norm, name="mix_out_bwd")
    d_seq = _delta_seq_bwd(wy_w, wy_u, q_dec, k_dec, qk, egl, states, do, n_seq, seq, name="delta_seq_bwd")
    dq, dk, dv, dgc_a, dgrow, dbeta = _delta_par_bwd(q, k, vv, gc, grow, beta, inv, *d_seq, name="delta_par_bwd")
    dgc_b = _pad_lanes(jnp.transpose(dgrow[:, :DN_HEADS, :], (0, 2, 1)).reshape(t, DN_HEADS))
    dy_conv, dbpre, dapre, d_alog, d_dtb = _dn_prep_bwd(qkv, bpre, apre, conv_full, alog, dtb, dq, dk, dv, dbeta, dgc_a, dgc_b,
                                                        seq, name="dn_prep_bwd")
    dqkv, d_conv = _conv_bwd(qkv, dy_conv, conv_full, seq, name="conv_bwd")
    du, dvv, d_lng, d_lnb, d_wc, d_sgbt = _sg_bwd(u, v, sg_ln_g, sg_ln_b, sg_w[0], sgbt, dsg, name="sg_bwd")
    dx1, d_mixn, d_ws = _split3(_mix_in_bwd(x1, mix_norm, ws, dx2, (du, dvv, dqkv, dz, dbpre, dapre), name="mix_in_bwd"))
    grad_x, d_n1, d_g1, d_u1, d_d1 = _ffn_bwd(x0, ffn1_norm, full["ffn1_w_gate"], full["ffn1_w_up"], full["ffn1_w_down"], dx1,
                                              name="ffn1_bwd")

    def by_core(p8):
        return jnp.moveaxis(p8.reshape((4, 2) + p8.shape[1:]), 1, 0)

    def ff_cols(acc):
        return _cols_pieces(jnp.transpose(acc, (1, 0, 2)).reshape(D_MODEL, D_FF))

    def ff_rows(acc):
        return acc.reshape(N_DEV, D_FF // N_DEV, D_MODEL)

    d_w_in = jnp.concatenate([d_ws[0], d_ws[1], d_ws[2], d_ws[3], d_ws[4][:, :DN_HEADS], d_ws[5][:, :DN_HEADS]], axis=1)
    d_w_out = jnp.concatenate([d_wo_sg, d_wo_dn], axis=0)
    pieces = dict(ffn1_w_gate=ff_cols(d_g1), ffn1_w_up=ff_cols(d_u1), ffn1_w_down=ff_rows(d_d1), w_in=_cols_pieces(d_w_in),
                  w_out=d_w_out.reshape(N_DEV, D_MODEL // N_DEV, D_MODEL), ffn2_w_gate=ff_cols(d_g2), ffn2_w_up=ff_cols(d_u2),
                  ffn2_w_down=ff_rows(d_d2))
    own = [by_core(pieces[n]) for n in big]
    from_sibling = _pair_swap(own, name="grads_to_sibling")
    core = lax.axis_index("c").astype(jnp.int32).reshape(1)
    chip_sums = [_pair_add(p, r, core, name="pair_add_" + n) for n, p, r in zip(big, own, from_sibling)]
    received = _chip_exchange(chip_sums, name="grads_to_owner")
    res = {}
    for n, gp in zip(big, received):
        res[n] = _adam(gp, weights[n][0], mom_m[n][0], mom_v[n][0], name="adam_" + n)

    small_grads = dict(ffn1_norm=d_n1, mix_norm=d_mixn, ffn2_norm=d_n2, final_norm=d_fn, a_log=d_alog[:, :DN_HEADS],
                       dt_bias=d_dtb[:, :DN_HEADS], dn_norm=d_dnw, sg_ln_g=d_lng, sg_ln_b=d_lnb, sg_w=d_wc,
                       sg_b=d_sgbt[:, :SG_GROUPS].T, conv_w=d_conv[:CONV_K])
    (small_parts,) = _gather2([_pack_small(small_grads)], name="gather_small_grads")
    zeros_conv = jnp.zeros((CONV_K * 3 * DN_WIDTH,), F32)
    packed = [_pack_small({**{n: src[n] for n, _ in _SMALL if n != "conv_w"}, "conv_w": zeros_conv})
              for src in (weights, mom_m, mom_v)]
    small_res = [_unpack_small(a) for a in _adam(small_parts, *packed, name="adam_small")]
    conv_grad = lax.dynamic_slice_in_dim(small_res[0]["conv_w"].reshape(CONV_K, 3 * DN_WIDTH), me * (3 * DN_WIDTH // N_DEV),
                                         3 * DN_WIDTH // N_DEV, axis=1)
    res["conv_w"] = _adam(conv_grad[None], conv_w[0], m_conv_w[0], v_conv_w[0], name="adam_conv_w")

    outs = [[], [], [], []]
    for n in order:
        for kind in range(4):
            if n in res:
                outs[kind].append(res[n][kind][None])
            else:
                outs[kind].append(small_res[kind][n].reshape(_SMALL_SHAPES[n]))
    return (loss, grad_x.reshape(x.shape), *outs[0], *outs[1], *outs[2], *outs[3])


def _split3(r):
    return r[0], r[1], r[2:]
```

```python
import functools

import jax
import jax.numpy as jnp
from jax import lax
from jax.experimental import pallas as pl
from jax.experimental.pallas import tpu as pltpu

F32 = jnp.float32
BF16 = jnp.bfloat16

D_MODEL = 1024
D_FF = 2816
SG_WIDTH = 512
SG_GROUPS = 8
SG_GROUP_DIM = 64
SG_CHUNK = 128
DN_WIDTH = 512
DN_HEAD_DIM = 128
DN_HEADS = 4
DN_CHUNK = 64
CONV_K = 4
EPS = 1e-6
N_DEV = 8
LANES = 128
HALO = 8

ADAM_LR = 0.001
ADAM_B1 = 0.9
ADAM_B2 = 0.999
ADAM_EPS = 1e-08
ADAM_WD = 0.01
ADAM_STEP = 10

VMEM_LIMIT = 60 * 1024 * 1024
TOKEN_BLOCK = 512
FF_BLOCK_FWD = 1408

_HI = lax.Precision.HIGHEST


def _cparams(sem):
    return pltpu.CompilerParams(dimension_semantics=sem, vmem_limit_bytes=VMEM_LIMIT)


def _tm(t, pref=TOKEN_BLOCK):
    return min(pref, t)


def _dg(a, b, ca, cb, precision):
    if precision is not None:
        return lax.dot_general(a, b, (((ca,), (cb,)), ((), ())), precision=precision, preferred_element_type=F32)
    return lax.dot_general(a.astype(BF16), b.astype(BF16), (((ca,), (cb,)), ((), ())), preferred_element_type=F32)


def _make_mm(exact):
    @jax.custom_vjp
    def mm(a, b):
        return _dg(a, b, 1, 0, exact)

    @jax.custom_vjp
    def mm_nt(a, b):
        return _dg(a, b, 1, 1, exact)

    @jax.custom_vjp
    def mm_tn(a, b):
        return _dg(a, b, 0, 0, exact)

    mm.defvjp(lambda a, b: (mm(a, b), (a, b)), lambda r, g: (mm_nt(g, r[1]), mm_tn(r[0], g)))
    mm_nt.defvjp(lambda a, b: (mm_nt(a, b), (a, b)), lambda r, g: (mm(g, r[1]), mm_tn(g, r[0])))
    mm_tn.defvjp(lambda a, b: (mm_tn(a, b), (a, b)), lambda r, g: (mm_nt(r[1], g), mm(r[0], g)))
    return mm, mm_nt, mm_tn


mm, mm_nt, mm_tn = _make_mm(None)
mmx, mmx_nt, mmx_tn = _make_mm(_HI)
mmh, _, _ = _make_mm(lax.Precision.HIGH)


def _sigmoid(x):
    return 1.0 / (1.0 + jnp.exp(-x))


def _silu(x):
    return x * _sigmoid(x)


def _softplus(x):
    neg_abs = jnp.where(x > 0, -x, x)
    return jnp.where(x > 0, x, 0.0) + jnp.log(1.0 + jnp.exp(neg_abs))


def _gelu(x):
    return 0.5 * x * (1.0 + jnp.tanh(0.7978845608028654 * (x + 0.044715 * (x * x * x))))


def _rms_fwd(x, g):
    r = lax.rsqrt(jnp.mean(x * x, axis=-1, keepdims=True) + EPS)
    xh = x * r
    return xh * g, xh, r


def _rms_bwd(dh, xh, r, g):
    dxh = dh * g
    dx = r * (dxh - xh * jnp.mean(dxh * xh, axis=-1, keepdims=True))
    return dx, jnp.sum(dh * xh, axis=0, keepdims=True)


def _acc_out(ref, first, val):
    @pl.when(first)
    def _():
        ref[...] = val

    @pl.when(jnp.logical_not(first))
    def _():
        ref[...] += val


def _ffn_fwd(x, nw, wg, wu, wd, tgt=None, fnw=None, *, name):
    t = x.shape[0]
    tm, fb = _tm(t), FF_BLOCK_FWD
    n_t, n_f = t // tm, D_FF // fb
    with_loss = tgt is not None

    def body(*refs):
        if with_loss:
            (x_ref, nw_ref, wg_ref, wu_ref, wd_ref, tgt_ref, fnw_ref, dy_ref, loss_ref, dfn_ref, h_ref, g_ref, u_ref,
             acc_s) = refs
        else:
            x_ref, nw_ref, wg_ref, wu_ref, wd_ref, y_ref, h_ref, g_ref, u_ref, acc_s = refs
        i, j = pl.program_id(0), pl.program_id(1)

        @pl.when(j == 0)
        def _():
            h, _, _ = _rms_fwd(x_ref[...], nw_ref[...])
            h_ref[...] = h.astype(BF16)
            acc_s[...] = jnp.zeros_like(acc_s)

        h = h_ref[...]
        g = jnp.dot(h, wg_ref[...], preferred_element_type=F32)
        u = jnp.dot(h, wu_ref[...], preferred_element_type=F32)
        g_ref[...] = g.astype(BF16)
        u_ref[...] = u.astype(BF16)
        a = _silu(g) * u
        acc_s[...] += jnp.dot(a.astype(BF16), wd_ref[...], preferred_element_type=F32)

        @pl.when(j == n_f - 1)
        def _():
            y = x_ref[...] + 0.5 * acc_s[...]
            if not with_loss:
                y_ref[...] = y
            else:
                gf = fnw_ref[...]
                out, xh, r = _rms_fwd(y, gf)
                err = out - tgt_ref[...]
                part = 0.5 * jnp.sum(jnp.mean(err * err, axis=-1, keepdims=True), axis=0, keepdims=True)
                d_out = err * (1.0 / D_MODEL)
                dy, dgf = _rms_bwd(d_out, xh, r, gf)
                dy_ref[...] = dy
                _acc_out(loss_ref, i == 0, jnp.broadcast_to(part, loss_ref.shape))
                _acc_out(dfn_ref, i == 0, dgf)

    row = lambda i, j: (i, 0)
    const = lambda i, j: (0, 0)
    in_specs = [
        pl.BlockSpec((tm, D_MODEL), row),
        pl.BlockSpec((1, D_MODEL), const),
        pl.BlockSpec((D_MODEL, fb), lambda i, j: (0, j)),
        pl.BlockSpec((D_MODEL, fb), lambda i, j: (0, j)),
        pl.BlockSpec((fb, D_MODEL), lambda i, j: (j, 0)),
    ]
    args = [x, nw, wg, wu, wd]
    saved_shape = (jax.ShapeDtypeStruct((t, D_MODEL), BF16), jax.ShapeDtypeStruct((t, D_FF), BF16),
                   jax.ShapeDtypeStruct((t, D_FF), BF16))
    saved_specs = (pl.BlockSpec((tm, D_MODEL), row), pl.BlockSpec((tm, fb), lambda i, j: (i, j)),
                   pl.BlockSpec((tm, fb), lambda i, j: (i, j)))
    if with_loss:
        in_specs += [pl.BlockSpec((tm, D_MODEL), row), pl.BlockSpec((1, D_MODEL), const)]
        args += [tgt, fnw]
        out_shape = (jax.ShapeDtypeStruct((t, D_MODEL), F32), jax.ShapeDtypeStruct((8, LANES), F32),
                     jax.ShapeDtypeStruct((1, D_MODEL), F32)) + saved_shape
        out_specs = (pl.BlockSpec((tm, D_MODEL), row), pl.BlockSpec((8, LANES), const),
                     pl.BlockSpec((1, D_MODEL), const)) + saved_specs
        sem = ("arbitrary", "arbitrary")
    else:
        out_shape = (jax.ShapeDtypeStruct((t, D_MODEL), F32),) + saved_shape
        out_specs = (pl.BlockSpec((tm, D_MODEL), row),) + saved_specs
        sem = ("parallel", "arbitrary")
    return pl.pallas_call(
        body, name=name, grid=(n_t, n_f), in_specs=in_specs, out_specs=out_specs, out_shape=out_shape,
        scratch_shapes=[pltpu.VMEM((tm, D_MODEL), F32)],
        compiler_params=_cparams(sem),
    )(*args)


def _ffn_bwd_x(x, nw, g, u, wg, wu, wd, dy, *, name):
    t = x.shape[0]
    tm = _tm(t, 256)

    def body(x_ref, nw_ref, g_ref, u_ref, wg_ref, wu_ref, wd_ref, dy_ref, dx_ref, dnw_ref, dg_ref, du_ref, a_ref, dyh_ref):
        i = pl.program_id(0)
        nt = (((1,), (1,)), ((), ()))
        dy = dy_ref[...]
        dyh = (0.5 * dy).astype(BF16)
        dyh_ref[...] = dyh
        gate, up = g_ref[...].astype(F32), u_ref[...].astype(F32)
        s = _sigmoid(gate)
        gs = gate * s
        da = lax.dot_general(dyh, wd_ref[...], nt, preferred_element_type=F32)
        dg = (da * up * (s + gs * (1.0 - s))).astype(BF16)
        du = (da * gs).astype(BF16)
        dg_ref[...] = dg
        du_ref[...] = du
        a_ref[...] = (gs * up).astype(BF16)
        dh = (lax.dot_general(dg, wg_ref[...], nt, preferred_element_type=F32)
              + lax.dot_general(du, wu_ref[...], nt, preferred_element_type=F32))
        xv = x_ref[...]
        r = lax.rsqrt(jnp.mean(xv * xv, axis=-1, keepdims=True) + EPS)
        dx, dnw = _rms_bwd(dh, xv * r, r, nw_ref[...])
        dx_ref[...] = dy + dx
        _acc_out(dnw_ref, i == 0, dnw)

    row = lambda i: (i, 0)
    const = lambda i: (0, 0)
    once = pl.Buffered(1)
    wide = pl.BlockSpec((tm, D_FF), row)
    return pl.pallas_call(
        body, name=name, grid=(t // tm,),
        in_specs=[pl.BlockSpec((tm, D_MODEL), row), pl.BlockSpec((1, D_MODEL), const), wide, wide,
                  pl.BlockSpec((D_MODEL, D_FF), const, pipeline_mode=once), pl.BlockSpec((D_MODEL, D_FF), const, pipeline_mode=once),
                  pl.BlockSpec((D_FF, D_MODEL), const, pipeline_mode=once), pl.BlockSpec((tm, D_MODEL), row)],
        out_specs=(pl.BlockSpec((tm, D_MODEL), row), pl.BlockSpec((1, D_MODEL), const), wide, wide, wide,
                   pl.BlockSpec((tm, D_MODEL), row)),
        out_shape=(jax.ShapeDtypeStruct((t, D_MODEL), F32), jax.ShapeDtypeStruct((1, D_MODEL), F32),
                   jax.ShapeDtypeStruct((t, D_FF), BF16), jax.ShapeDtypeStruct((t, D_FF), BF16),
                   jax.ShapeDtypeStruct((t, D_FF), BF16), jax.ShapeDtypeStruct((t, D_MODEL), BF16)),
        compiler_params=_cparams(("arbitrary",)),
    )(x, nw, g, u, wg, wu, wd, dy)


def _wgrad(a, b, bm, bn, *, name):
    k, m = a.shape
    n = b.shape[1]
    tk = _tm(k, 1024)

    def body(a_ref, b_ref, o_ref):
        part = lax.dot_general(a_ref[...], b_ref[...], (((0,), (0,)), ((), ())), preferred_element_type=F32)
        _acc_out(o_ref, pl.program_id(2) == 0, part)

    return pl.pallas_call(
        body, name=name, grid=(m // bm, n // bn, k // tk),
        in_specs=[pl.BlockSpec((tk, bm), lambda i, j, s: (s, i)), pl.BlockSpec((tk, bn), lambda i, j, s: (s, j))],
        out_specs=pl.BlockSpec((bm, bn), lambda i, j, s: (i, j)),
        out_shape=jax.ShapeDtypeStruct((m, n), F32),
        compiler_params=_cparams(("parallel", "parallel", "arbitrary")),
    )(a, b)


def _ffn_bwd(x, nw, h, g, u, wg, wu, wd, dy, *, name):
    dx, dnw, dg, du, a, dyh = _ffn_bwd_x(x, nw, g, u, wg, wu, wd, dy, name=name + "_x")
    half = D_FF // 2
    return (dx, dnw, _wgrad(h, dg, D_MODEL, half, name=name + "_wg"), _wgrad(h, du, D_MODEL, half, name=name + "_wu"),
            _wgrad(a, dyh, half, D_MODEL, name=name + "_wd"))


_PROJ_WIDTHS = (SG_WIDTH, SG_WIDTH, 3 * DN_WIDTH, DN_WIDTH, LANES, LANES)


def _mix_in_fwd(x, nw, ws, *, name):
    t = x.shape[0]
    tm = _tm(t)

    def body(x_ref, nw_ref, *refs):
        w_refs, o_refs = refs[:6], refs[6:]
        h, _, _ = _rms_fwd(x_ref[...], nw_ref[...])
        h = h.astype(BF16)
        for w_ref, o_ref in zip(w_refs, o_refs):
            o_ref[...] = jnp.dot(h, w_ref[...], preferred_element_type=F32)

    row = lambda i: (i, 0)
    const = lambda i: (0, 0)
    return pl.pallas_call(
        body, name=name, grid=(t // tm,),
        in_specs=[pl.BlockSpec((tm, D_MODEL), row), pl.BlockSpec((1, D_MODEL), const)]
        + [pl.BlockSpec((D_MODEL, n), const) for n in _PROJ_WIDTHS],
        out_specs=tuple(pl.BlockSpec((tm, n), row) for n in _PROJ_WIDTHS),
        out_shape=tuple(jax.ShapeDtypeStruct((t, n), F32) for n in _PROJ_WIDTHS),
        compiler_params=_cparams(("parallel",)),
    )(x, nw, *ws)


def _mix_in_bwd(x, nw, ws, dres, dps, *, name):
    t = x.shape[0]
    tm = _tm(t, 256)

    def body(x_ref, nw_ref, dres_ref, *refs):
        w_refs, dp_refs, dx_ref, dnw_ref, dw_refs = refs[:6], refs[6:12], refs[12], refs[13], refs[14:]
        i = pl.program_id(0)
        hf, xh, r = _rms_fwd(x_ref[...], nw_ref[...])
        h = hf.astype(BF16)
        dh = jnp.zeros((tm, D_MODEL), F32)
        for w_ref, dp_ref, dw_ref in zip(w_refs, dp_refs, dw_refs):
            dp = dp_ref[...].astype(BF16)
            dh = dh + lax.dot_general(dp, w_ref[...], (((1,), (1,)), ((), ())), preferred_element_type=F32)
            _acc_out(dw_ref, i == 0, lax.dot_general(h, dp, (((0,), (0,)), ((), ())), preferred_element_type=F32))
        dx, dnw = _rms_bwd(dh, xh, r, nw_ref[...])
        dx_ref[...] = dres_ref[...] + dx
        _acc_out(dnw_ref, i == 0, dnw)

    row = lambda i: (i, 0)
    const = lambda i: (0, 0)
    return pl.pallas_call(
        body, name=name, grid=(t // tm,),
        in_specs=[pl.BlockSpec((tm, D_MODEL), row), pl.BlockSpec((1, D_MODEL), const), pl.BlockSpec((tm, D_MODEL), row)]
        + [pl.BlockSpec((D_MODEL, n), const) for n in _PROJ_WIDTHS]
        + [pl.BlockSpec((tm, n), row) for n in _PROJ_WIDTHS],
        out_specs=(pl.BlockSpec((tm, D_MODEL), row), pl.BlockSpec((1, D_MODEL), const))
        + tuple(pl.BlockSpec((D_MODEL, n), const) for n in _PROJ_WIDTHS),
        out_shape=(jax.ShapeDtypeStruct((t, D_MODEL), F32), jax.ShapeDtypeStruct((1, D_MODEL), F32))
        + tuple(jax.ShapeDtypeStruct((D_MODEL, n), F32) for n in _PROJ_WIDTHS),
        compiler_params=_cparams(("arbitrary",)),
    )(x, nw, dres, *ws, *dps)


def _sg_fn(u, v, lng, lnb, wcs, sgbt):
    lane = lax.broadcasted_iota(jnp.int32, (1, SG_WIDTH), 1)
    lane_b = lax.broadcasted_iota(jnp.int32, (1, LANES), 1)
    rr = lax.broadcasted_iota(jnp.int32, (SG_CHUNK, SG_CHUNK), 0)
    cc = lax.broadcasted_iota(jnp.int32, (SG_CHUNK, SG_CHUNK), 1)
    gu, gv = _gelu(u), _gelu(v)
    mu = jnp.mean(gv, axis=-1, keepdims=True)
    cen = gv - mu
    var = jnp.mean(cen * cen, axis=-1, keepdims=True)
    ln = cen * lax.rsqrt(var + EPS) * lng + lnb
    vs = jnp.zeros_like(u)
    for g in range(SG_GROUPS):
        in_group = jnp.logical_and(lane >= g * SG_GROUP_DIM, lane < (g + 1) * SG_GROUP_DIM)
        w_causal = jnp.where(rr >= cc, wcs[g], 0.0)
        bias = jnp.sum(jnp.where(lane_b == g, sgbt, 0.0), axis=1, keepdims=True)
        vs = vs + jnp.where(in_group, mm(w_causal, ln) + bias, 0.0)
    return gu * vs


def _sg_fwd(u, v, lng, lnb, wc, sgbt, *, name):
    t = u.shape[0]
    tm = _tm(t)

    def body(u_ref, v_ref, lng_ref, lnb_ref, wc_ref, sgbt_ref, o_ref):
        wcs = [wc_ref[g] for g in range(SG_GROUPS)]
        for c in range(tm // SG_CHUNK):
            rows = pl.ds(c * SG_CHUNK, SG_CHUNK)
            o_ref[rows, :] = _sg_fn(u_ref[rows, :], v_ref[rows, :], lng_ref[...], lnb_ref[...], wcs, sgbt_ref[...])

    row = lambda i: (i, 0)
    const = lambda i: (0, 0)
    return pl.pallas_call(
        body, name=name, grid=(t // tm,),
        in_specs=[pl.BlockSpec((tm, SG_WIDTH), row), pl.BlockSpec((tm, SG_WIDTH), row),
                  pl.BlockSpec((1, SG_WIDTH), const), pl.BlockSpec((1, SG_WIDTH), const),
                  pl.BlockSpec((SG_GROUPS, SG_CHUNK, SG_CHUNK), lambda i: (0, 0, 0)), pl.BlockSpec((SG_CHUNK, LANES), const)],
        out_specs=pl.BlockSpec((tm, SG_WIDTH), row),
        out_shape=jax.ShapeDtypeStruct((t, SG_WIDTH), F32),
        compiler_params=_cparams(("parallel",)),
    )(u, v, lng, lnb, wc, sgbt)


def _sg_bwd(u, v, lng, lnb, wc, sgbt, dout, *, name):
    t = u.shape[0]
    tm = _tm(t)

    def body(u_ref, v_ref, lng_ref, lnb_ref, wc_ref, sgbt_ref, do_ref, du_ref, dv_ref, dlng_ref, dlnb_ref, dwc_ref, dsgbt_ref):
        i = pl.program_id(0)
        wcs = [wc_ref[g] for g in range(SG_GROUPS)]
        tot = None
        for c in range(tm // SG_CHUNK):
            rows = pl.ds(c * SG_CHUNK, SG_CHUNK)
            _, vjp = jax.vjp(_sg_fn, u_ref[rows, :], v_ref[rows, :], lng_ref[...], lnb_ref[...], wcs, sgbt_ref[...])
            du, dv, dlng, dlnb, dwcs, dsgbt = vjp(do_ref[rows, :])
            du_ref[rows, :] = du
            dv_ref[rows, :] = dv
            part = (dlng, dlnb, dwcs, dsgbt)
            tot = part if tot is None else jax.tree.map(jnp.add, tot, part)
        dlng, dlnb, dwcs, dsgbt = tot
        _acc_out(dlng_ref, i == 0, dlng)
        _acc_out(dlnb_ref, i == 0, dlnb)
        _acc_out(dsgbt_ref, i == 0, dsgbt)
        for g in range(SG_GROUPS):
            @pl.when(i == 0)
            def _(g=g):
                dwc_ref[g] = dwcs[g]

            @pl.when(i > 0)
            def _(g=g):
                dwc_ref[g] += dwcs[g]

    row = lambda i: (i, 0)
    const = lambda i: (0, 0)
    wspec = pl.BlockSpec((SG_GROUPS, SG_CHUNK, SG_CHUNK), lambda i: (0, 0, 0))
    return pl.pallas_call(
        body, name=name, grid=(t // tm,),
        in_specs=[pl.BlockSpec((tm, SG_WIDTH), row), pl.BlockSpec((tm, SG_WIDTH), row),
                  pl.BlockSpec((1, SG_WIDTH), const), pl.BlockSpec((1, SG_WIDTH), const), wspec,
                  pl.BlockSpec((SG_CHUNK, LANES), const), pl.BlockSpec((tm, SG_WIDTH), row)],
        out_specs=(pl.BlockSpec((tm, SG_WIDTH), row), pl.BlockSpec((tm, SG_WIDTH), row),
                   pl.BlockSpec((1, SG_WIDTH), const), pl.BlockSpec((1, SG_WIDTH), const), wspec,
                   pl.BlockSpec((SG_CHUNK, LANES), const)),
        out_shape=(jax.ShapeDtypeStruct((t, SG_WIDTH), F32), jax.ShapeDtypeStruct((t, SG_WIDTH), F32),
                   jax.ShapeDtypeStruct((1, SG_WIDTH), F32), jax.ShapeDtypeStruct((1, SG_WIDTH), F32),
                   jax.ShapeDtypeStruct((SG_GROUPS, SG_CHUNK, SG_CHUNK), F32), jax.ShapeDtypeStruct((SG_CHUNK, LANES), F32)),
        compiler_params=_cparams(("arbitrary",)),
    )(u, v, lng, lnb, wc, sgbt, dout)


def _conv_taps(ext, w, tm):
    y = None
    for j in range(CONV_K):
        s = CONV_K - 1 - j
        shifted = ext if s == 0 else pltpu.roll(ext, s, 0)
        term = w[j:j + 1, :] * shifted[HALO:HALO + tm, :]
        y = term if y is None else y + term
    return y


def _post_conv(yq, yk, yv, bpre, apre, alog, dtb):
    def l2(a):
        return a * lax.rsqrt(jnp.sum(a * a, axis=-1, keepdims=True) + EPS)

    q = [l2(_silu(a)) for a in yq]
    k = [l2(_silu(a)) for a in yk]
    return q, k, _silu(yv), _sigmoid(bpre), -jnp.exp(alog) * _softplus(apre + dtb)


def _chunk_tril(tm):
    rr = lax.broadcasted_iota(jnp.int32, (tm, tm), 0)
    cc = lax.broadcasted_iota(jnp.int32, (tm, tm), 1)
    shift = DN_CHUNK.bit_length() - 1
    same = jnp.right_shift(rr, shift) == jnp.right_shift(cc, shift)
    return jnp.where(jnp.logical_and(same, rr >= cc), 1.0, 0.0).astype(F32)


def _halo_specs(tm, width, n_blocks_seq, n_blocks):
    per = tm // HALO
    prev = pl.BlockSpec((HALO, width), lambda i: (jnp.maximum(i * per - 1, 0), 0))
    nxt = pl.BlockSpec((HALO, width), lambda i: (jnp.minimum((i + 1) * per, n_blocks * per - 1), 0))
    return prev, nxt


def _split_heads(ref, base):
    return [ref[:, base + h * DN_HEAD_DIM: base + (h + 1) * DN_HEAD_DIM] for h in range(DN_HEADS)]


def _dn_prep_fwd(qkv, bpre, apre, conv_w, alog, dtb, seq, *, name):
    t = qkv.shape[0]
    tm = _tm(t)
    bps = seq // tm
    cw = 3 * DN_WIDTH

    def body(x_ref, halo_ref, b_ref, a_ref, w_ref, alog_ref, dtb_ref, q_ref, k_ref, v_ref, beta_ref, gc_ref):
        i = pl.program_id(0)
        keep = jnp.where(i % bps == 0, 0.0, 1.0)
        ext = jnp.concatenate([halo_ref[...] * keep, x_ref[...]], axis=0)
        y = _conv_taps(ext, w_ref[...], tm)
        yq = [y[:, h * DN_HEAD_DIM:(h + 1) * DN_HEAD_DIM] for h in range(DN_HEADS)]
        yk = [y[:, DN_WIDTH + h * DN_HEAD_DIM: DN_WIDTH + (h + 1) * DN_HEAD_DIM] for h in range(DN_HEADS)]
        q, k, v, beta, g = _post_conv(yq, yk, y[:, 2 * DN_WIDTH:], b_ref[...], a_ref[...], alog_ref[...], dtb_ref[...])
        for h in range(DN_HEADS):
            q_ref[:, h * DN_HEAD_DIM:(h + 1) * DN_HEAD_DIM] = q[h]
            k_ref[:, h * DN_HEAD_DIM:(h + 1) * DN_HEAD_DIM] = k[h]
        v_ref[...] = v
        beta_ref[...] = beta
        gc_ref[...] = mmx(_chunk_tril(tm), g)

    row = lambda i: (i, 0)
    const = lambda i: (0, 0)
    prev, _ = _halo_specs(tm, cw, bps, t // tm)
    return pl.pallas_call(
        body, name=name, grid=(t // tm,),
        in_specs=[pl.BlockSpec((tm, cw), row), prev, pl.BlockSpec((tm, LANES), row), pl.BlockSpec((tm, LANES), row),
                  pl.BlockSpec((CONV_K, cw), const), pl.BlockSpec((1, LANES), const), pl.BlockSpec((1, LANES), const)],
        out_specs=tuple(pl.BlockSpec((tm, n), row) for n in (DN_WIDTH, DN_WIDTH, DN_WIDTH, LANES, LANES)),
        out_shape=tuple(jax.ShapeDtypeStruct((t, n), F32) for n in (DN_WIDTH, DN_WIDTH, DN_WIDTH, LANES, LANES)),
        compiler_params=_cparams(("parallel",)),
    )(qkv, qkv, bpre, apre, conv_w, alog, dtb)


def _dn_prep_bwd(qkv, bpre, apre, conv_w, alog, dtb, dq, dk, dv, dbeta, dgc, dgc2, seq, *, name):
    t = qkv.shape[0]
    tm = _tm(t)
    bps = seq // tm
    cw = 3 * DN_WIDTH

    def body(x_ref, halo_ref, b_ref, a_ref, w_ref, alog_ref, dtb_ref, dq_ref, dk_ref, dv_ref, dbeta_ref, dgc_ref, dgc2_ref,
             dy_ref, db_ref, da_ref, dalog_ref, ddtb_ref):
        i = pl.program_id(0)
        keep = jnp.where(i % bps == 0, 0.0, 1.0)
        ext = jnp.concatenate([halo_ref[...] * keep, x_ref[...]], axis=0)
        y = _conv_taps(ext, w_ref[...], tm)
        yq = [y[:, h * DN_HEAD_DIM:(h + 1) * DN_HEAD_DIM] for h in range(DN_HEADS)]
        yk = [y[:, DN_WIDTH + h * DN_HEAD_DIM: DN_WIDTH + (h + 1) * DN_HEAD_DIM] for h in range(DN_HEADS)]
        _, vjp = jax.vjp(_post_conv, yq, yk, y[:, 2 * DN_WIDTH:], b_ref[...], a_ref[...], alog_ref[...], dtb_ref[...])
        dg = mmx_tn(_chunk_tril(tm), dgc_ref[...] + dgc2_ref[...])
        dyq, dyk, dyv, db, da, dalog, ddtb = vjp((_split_heads(dq_ref, 0), _split_heads(dk_ref, 0), dv_ref[...],
                                                  dbeta_ref[...], dg))
        for h in range(DN_HEADS):
            dy_ref[:, h * DN_HEAD_DIM:(h + 1) * DN_HEAD_DIM] = dyq[h]
            dy_ref[:, DN_WIDTH + h * DN_HEAD_DIM: DN_WIDTH + (h + 1) * DN_HEAD_DIM] = dyk[h]
        dy_ref[:, 2 * DN_WIDTH:] = dyv
        db_ref[...] = db
        da_ref[...] = da
        _acc_out(dalog_ref, i == 0, dalog)
        _acc_out(ddtb_ref, i == 0, ddtb)

    row = lambda i: (i, 0)
    const = lambda i: (0, 0)
    prev, _ = _halo_specs(tm, cw, bps, t // tm)
    return pl.pallas_call(
        body, name=name, grid=(t // tm,),
        in_specs=[pl.BlockSpec((tm, cw), row), prev, pl.BlockSpec((tm, LANES), row), pl.BlockSpec((tm, LANES), row),
                  pl.BlockSpec((CONV_K, cw), const), pl.BlockSpec((1, LANES), const), pl.BlockSpec((1, LANES), const),
                  pl.BlockSpec((tm, DN_WIDTH), row), pl.BlockSpec((tm, DN_WIDTH), row), pl.BlockSpec((tm, DN_WIDTH), row),
                  pl.BlockSpec((tm, LANES), row), pl.BlockSpec((tm, LANES), row), pl.BlockSpec((tm, LANES), row)],
        out_specs=(pl.BlockSpec((tm, cw), row), pl.BlockSpec((tm, LANES), row), pl.BlockSpec((tm, LANES), row),
                   pl.BlockSpec((1, LANES), const), pl.BlockSpec((1, LANES), const)),
        out_shape=(jax.ShapeDtypeStruct((t, cw), F32), jax.ShapeDtypeStruct((t, LANES), F32), jax.ShapeDtypeStruct((t, LANES), F32),
                   jax.ShapeDtypeStruct((1, LANES), F32), jax.ShapeDtypeStruct((1, LANES), F32)),
        compiler_params=_cparams(("arbitrary",)),
    )(qkv, qkv, bpre, apre, conv_w, alog, dtb, dq, dk, dv, dbeta, dgc, dgc2)


def _conv_bwd(qkv, dy, conv_w, seq, *, name):
    t = qkv.shape[0]
    tm = _tm(t)
    bps = seq // tm
    cw = 3 * DN_WIDTH
    n_ext = tm + HALO

    def body(x_ref, halo_ref, dy_ref, dyn_ref, w_ref, dx_ref, dw_ref):
        i = pl.program_id(0)
        keep_prev = jnp.where(i % bps == 0, 0.0, 1.0)
        keep_next = jnp.where(i % bps == bps - 1, 0.0, 1.0)
        ext = jnp.concatenate([halo_ref[...] * keep_prev, x_ref[...]], axis=0)
        dy = dy_ref[...]
        dyext = jnp.concatenate([dy, dyn_ref[...] * keep_next], axis=0)
        w = w_ref[...]

        @pl.when(i == 0)
        def _():
            dw_ref[...] = jnp.zeros_like(dw_ref)

        dx = None
        for j in range(CONV_K):
            s = CONV_K - 1 - j
            fut = dyext if s == 0 else pltpu.roll(dyext, n_ext - s, 0)
            term = w[j:j + 1, :] * fut[0:tm, :]
            dx = term if dx is None else dx + term
            past = ext if s == 0 else pltpu.roll(ext, s, 0)
            dw_ref[j:j + 1, :] += jnp.sum(dy * past[HALO:HALO + tm, :], axis=0, keepdims=True)
        dx_ref[...] = dx

    row = lambda i: (i, 0)
    const = lambda i: (0, 0)
    prev, nxt = _halo_specs(tm, cw, bps, t // tm)
    return pl.pallas_call(
        body, name=name, grid=(t // tm,),
        in_specs=[pl.BlockSpec((tm, cw), row), prev, pl.BlockSpec((tm, cw), row), nxt, pl.BlockSpec((CONV_K, cw), const)],
        out_specs=(pl.BlockSpec((tm, cw), row), pl.BlockSpec((HALO, cw), const)),
        out_shape=(jax.ShapeDtypeStruct((t, cw), F32), jax.ShapeDtypeStruct((HALO, cw), F32)),
        compiler_params=_cparams(("arbitrary",)),
    )(qkv, qkv, dy, dy, conv_w)


def _inv_unit_lower(l_mats, eye):
    invs = [eye - l for l in l_mats]
    powers = list(l_mats)
    n = 2
    while n < eye.shape[0]:
        powers = [mmh(p, p) for p in powers]
        invs = [inv + mmh(inv, p) for inv, p in zip(invs, powers)]
        n *= 2
    return invs


@jax.custom_vjp
def _solve(l_mat, rhs, inv):
    return mmx(inv, rhs)


def _solve_fwd(l_mat, rhs, inv):
    sol = mmx(inv, rhs)
    return sol, (inv, sol)


def _solve_bwd(res, d_sol):
    inv, sol = res
    d_rhs = mmx_tn(inv, d_sol)
    return -mmx_nt(d_rhs, sol), d_rhs, jnp.zeros_like(inv)


_solve.defvjp(_solve_fwd, _solve_bwd)


def _prep_fn(q, k, v, gc, gr, b, inv):
    ids = range(len(q))
    c = q[0].shape[0]
    rr = lax.broadcasted_iota(jnp.int32, (c, c), 0)
    cc = lax.broadcasted_iota(jnp.int32, (c, c), 1)
    incl, strict = rr >= cc, rr > cc
    is_last = lax.broadcasted_iota(jnp.int32, (c, 1), 0) == c - 1
    qs = [q[i] * (DN_HEAD_DIM ** -0.5) for i in ids]
    decay = [jnp.where(incl, jnp.exp(jnp.where(incl, gc[i] - gr[i], 0.0)), 0.0) for i in ids]
    kb = [k[i] * b[i] for i in ids]
    vb = [v[i] * b[i] for i in ids]
    kk = [mm_nt(kb[i], k[i]) for i in ids]
    l_mat = [jnp.where(strict, kk[i] * decay[i], 0.0) for i in ids]
    eg = [jnp.exp(gc[i]) for i in ids]
    if inv is None:
        inv = _inv_unit_lower(l_mat, jnp.where(rr == cc, 1.0, 0.0).astype(F32))
    u_wy = [_solve(l_mat[i], vb[i], inv[i]) for i in ids]
    w_wy = [_solve(l_mat[i], kb[i] * eg[i], inv[i]) for i in ids]
    qk = [mm_nt(qs[i], k[i]) * decay[i] for i in ids]
    g_last = [jnp.sum(jnp.where(is_last, gc[i], 0.0), axis=0, keepdims=True) for i in ids]
    k_dec = [k[i] * jnp.exp(g_last[i] - gc[i]) for i in ids]
    egl = [jnp.broadcast_to(jnp.exp(g_last[i]), (1, LANES)) for i in ids]
    return [(w_wy[i], u_wy[i], qs[i] * eg[i], k_dec[i], qk[i], egl[i]) for i in ids], inv


def _seq_fn(w, u, qd, kd, qk, egl, s):
    ids = range(len(w))
    ws = [mm(w[i], s[i]) for i in ids]
    qs = [mm(qd[i], s[i]) for i in ids]
    v_new = [u[i] - ws[i] for i in ids]
    o = [qs[i] + mm(qk[i], v_new[i]) for i in ids]
    s_new = [s[i] * egl[i] + mm_tn(kd[i], v_new[i]) for i in ids]
    return o, s_new


def _lane_col(a, h):
    lane = lax.broadcasted_iota(jnp.int32, (1, LANES), 1)
    return jnp.sum(jnp.where(lane == h, a, 0.0), axis=1, keepdims=True)


def _col_lane(col, h):
    lane = lax.broadcasted_iota(jnp.int32, (1, LANES), 1)
    return jnp.where(lane == h, col, 0.0)


def _head_cols(h):
    return slice(h * DN_HEAD_DIM, (h + 1) * DN_HEAD_DIM)


def _chunk_rows(n):
    return pl.ds(pl.multiple_of(n * DN_CHUNK, DN_CHUNK), DN_CHUNK)


def _delta_prep(q, k, v, gc, grow, beta, *, name):
    t = q.shape[0]
    tm = _tm(t)
    cpb = tm // DN_CHUNK
    n_chunks = t // DN_CHUNK
    group = 2

    def body(q_ref, k_ref, v_ref, gc_ref, gr_ref, b_ref, w_ref, u_ref, qd_ref, kd_ref, qk_ref, egl_ref, inv_ref):
        def step(m, carry):
            probs = [(m * group + e, h) for e in range(group) for h in range(DN_HEADS)]
            gcb = [gc_ref[_chunk_rows(m * group + e), :] for e in range(group)]
            bb = [b_ref[_chunk_rows(m * group + e), :] for e in range(group)]
            grb = [gr_ref[m * group + e] for e in range(group)]
            for e in range(group):
                egl_ref[m * group + e] = jnp.zeros((HALO, LANES), F32)
            outs, invs = _prep_fn(
                [q_ref[_chunk_rows(n), _head_cols(h)] for n, h in probs], [k_ref[_chunk_rows(n), _head_cols(h)] for n, h in probs],
                [v_ref[_chunk_rows(n), _head_cols(h)] for n, h in probs],
                [_lane_col(gcb[e], h) for e in range(group) for h in range(DN_HEADS)],
                [grb[e][h:h + 1, :] for e in range(group) for h in range(DN_HEADS)],
                [_lane_col(bb[e], h) for e in range(group) for h in range(DN_HEADS)], None)
            for (n, h), (w, u, qd, kd, qk, egl), inv in zip(probs, outs, invs):
                rows, cols = _chunk_rows(n), _head_cols(h)
                w_ref[rows, cols] = w.astype(BF16)
                u_ref[rows, cols] = u
                qd_ref[rows, cols] = qd.astype(BF16)
                kd_ref[rows, cols] = kd.astype(BF16)
                qk_ref[n, h] = qk
                inv_ref[n, h] = inv
                egl_ref[n, h:h + 1, :] = egl
            return carry

        lax.fori_loop(0, cpb // group, step, 0)

    row = lambda i: (i, 0)
    tok = pl.BlockSpec((tm, DN_WIDTH), row)
    lanes = pl.BlockSpec((tm, LANES), row)
    sq = pl.BlockSpec((cpb, DN_HEADS, DN_CHUNK, DN_CHUNK), lambda i: (i, 0, 0, 0))
    return pl.pallas_call(
        body, name=name, grid=(t // tm,),
        in_specs=[tok, tok, tok, lanes, pl.BlockSpec((cpb, HALO, DN_CHUNK), lambda i: (i, 0, 0)), lanes],
        out_specs=(tok, tok, tok, tok, sq, pl.BlockSpec((cpb, HALO, LANES), lambda i: (i, 0, 0)), sq),
        out_shape=(jax.ShapeDtypeStruct((t, DN_WIDTH), BF16), jax.ShapeDtypeStruct((t, DN_WIDTH), F32),
                   jax.ShapeDtypeStruct((t, DN_WIDTH), BF16), jax.ShapeDtypeStruct((t, DN_WIDTH), BF16),
                   jax.ShapeDtypeStruct((n_chunks, DN_HEADS, DN_CHUNK, DN_CHUNK), F32),
                   jax.ShapeDtypeStruct((n_chunks, HALO, LANES), F32),
                   jax.ShapeDtypeStruct((n_chunks, DN_HEADS, DN_CHUNK, DN_CHUNK), F32)),
        compiler_params=_cparams(("parallel",)),
    )(q, k, v, gc, grow, beta)


def _delta_par_bwd(q, k, v, gc, grow, beta, inv, dw, du, dqd, dkd, dqk, degl, *, name):
    t = q.shape[0]
    tm = _tm(t)
    cpb = tm // DN_CHUNK
    n_chunks = t // DN_CHUNK
    group = 2

    def body(q_ref, k_ref, v_ref, gc_ref, gr_ref, b_ref, inv_ref, dw_ref, du_ref, dqd_ref, dkd_ref, dqk_ref, degl_ref,
             dq_ref, dk_ref, dv_ref, dgc_ref, dgr_ref, db_ref):
        def step(m, carry):
            chunks = [m * group + e for e in range(group)]
            probs = [(e, h) for e in range(group) for h in range(DN_HEADS)]
            rows = [_chunk_rows(n) for n in chunks]
            gcb, bb = [gc_ref[r, :] for r in rows], [b_ref[r, :] for r in rows]
            grb, deglb = [gr_ref[n] for n in chunks], [degl_ref[n] for n in chunks]
            for n in chunks:
                dgr_ref[n] = jnp.zeros((HALO, DN_CHUNK), F32)
            invs = [inv_ref[chunks[e], h] for e, h in probs]
            _, vjp = jax.vjp(lambda *a: _prep_fn(*a, invs)[0],
                             [q_ref[rows[e], _head_cols(h)] for e, h in probs], [k_ref[rows[e], _head_cols(h)] for e, h in probs],
                             [v_ref[rows[e], _head_cols(h)] for e, h in probs], [_lane_col(gcb[e], h) for e, h in probs],
                             [grb[e][h:h + 1, :] for e, h in probs], [_lane_col(bb[e], h) for e, h in probs])
            dq, dk, dv, dgc, dgr, db = vjp([(dw_ref[rows[e], _head_cols(h)], du_ref[rows[e], _head_cols(h)],
                                             dqd_ref[rows[e], _head_cols(h)], dkd_ref[rows[e], _head_cols(h)],
                                             dqk_ref[chunks[e], h], deglb[e][h:h + 1, :]) for e, h in probs])
            dgc_acc = [jnp.zeros((DN_CHUNK, LANES), F32) for _ in chunks]
            db_acc = [jnp.zeros((DN_CHUNK, LANES), F32) for _ in chunks]
            for i, (e, h) in enumerate(probs):
                cols = _head_cols(h)
                dq_ref[rows[e], cols] = dq[i]
                dk_ref[rows[e], cols] = dk[i]
                dv_ref[rows[e], cols] = dv[i]
                dgr_ref[chunks[e], h:h + 1, :] = dgr[i]
                dgc_acc[e] = dgc_acc[e] + _col_lane(dgc[i], h)
                db_acc[e] = db_acc[e] + _col_lane(db[i], h)
            for e in range(group):
                dgc_ref[rows[e], :] = dgc_acc[e]
                db_ref[rows[e], :] = db_acc[e]
            return carry

        lax.fori_loop(0, cpb // group, step, 0)

    row = lambda i: (i, 0)
    tok = pl.BlockSpec((tm, DN_WIDTH), row)
    lanes = pl.BlockSpec((tm, LANES), row)
    sq = pl.BlockSpec((cpb, DN_HEADS, DN_CHUNK, DN_CHUNK), lambda i: (i, 0, 0, 0))
    grs = pl.BlockSpec((cpb, HALO, DN_CHUNK), lambda i: (i, 0, 0))
    return pl.pallas_call(
        body, name=name, grid=(t // tm,),
        in_specs=[tok, tok, tok, lanes, grs, lanes, sq, tok, tok, tok, tok, sq, pl.BlockSpec((cpb, HALO, LANES), lambda i: (i, 0, 0))],
        out_specs=(tok, tok, tok, lanes, grs, lanes),
        out_shape=(jax.ShapeDtypeStruct((t, DN_WIDTH), F32),) * 3
        + (jax.ShapeDtypeStruct((t, LANES), F32), jax.ShapeDtypeStruct((n_chunks, HALO, DN_CHUNK), F32),
           jax.ShapeDtypeStruct((t, LANES), F32)),
        compiler_params=_cparams(("parallel",)),
    )(q, k, v, gc, grow, beta, inv, dw, du, dqd, dkd, dqk, degl)


def _seq_specs(n_seq, seq, reverse):
    tm = _tm(seq)
    nb = seq // tm
    cpb = tm // DN_CHUNK
    blk = (lambda b, j: b * nb + nb - 1 - j) if reverse else (lambda b, j: b * nb + j)
    tok = pl.BlockSpec((tm, DN_WIDTH), lambda b, j: (blk(b, j), 0))
    sq = pl.BlockSpec((cpb, DN_HEADS, DN_CHUNK, DN_CHUNK), lambda b, j: (blk(b, j), 0, 0, 0))
    rows8 = pl.BlockSpec((cpb, HALO, LANES), lambda b, j: (blk(b, j), 0, 0))
    state = pl.BlockSpec((cpb, DN_HEADS, DN_HEAD_DIM, DN_HEAD_DIM), lambda b, j: (blk(b, j), 0, 0, 0))
    return nb, cpb, tok, sq, rows8, state


def _delta_seq_fwd(w, u, qd, kd, qk, egl, n_seq, seq, *, name):
    nb, cpb, tok, sq, rows8, state = _seq_specs(n_seq, seq, False)
    t = n_seq * seq

    def body(w_ref, u_ref, qd_ref, kd_ref, qk_ref, egl_ref, o_ref, st_ref, s_s):
        @pl.when(pl.program_id(1) == 0)
        def _():
            s_s[...] = jnp.zeros_like(s_s)

        def step(n, carry):
            rows = _chunk_rows(n)
            heads = range(DN_HEADS)
            eglb = egl_ref[n]
            s = [s_s[h] for h in heads]
            for h in heads:
                st_ref[n, h] = s[h]
            o, s_new = _seq_fn([w_ref[rows, _head_cols(h)] for h in heads], [u_ref[rows, _head_cols(h)] for h in heads],
                               [qd_ref[rows, _head_cols(h)] for h in heads], [kd_ref[rows, _head_cols(h)] for h in heads],
                               [qk_ref[n, h] for h in heads], [eglb[h:h + 1, :] for h in heads], s)
            for h in heads:
                o_ref[rows, _head_cols(h)] = o[h]
                s_s[h] = s_new[h]
            return carry

        lax.fori_loop(0, cpb, step, 0)

    return pl.pallas_call(
        body, name=name, grid=(n_seq, nb),
        in_specs=[tok, tok, tok, tok, sq, rows8],
        out_specs=(tok, state),
        out_shape=(jax.ShapeDtypeStruct((t, DN_WIDTH), F32),
                   jax.ShapeDtypeStruct((t // DN_CHUNK, DN_HEADS, DN_HEAD_DIM, DN_HEAD_DIM), F32)),
        scratch_shapes=[pltpu.VMEM((DN_HEADS, DN_HEAD_DIM, DN_HEAD_DIM), F32)],
        compiler_params=_cparams(("parallel", "arbitrary")),
    )(w, u, qd, kd, qk, egl)


def _delta_seq_bwd(w, u, qd, kd, qk, egl, states, do, n_seq, seq, *, name):
    nb, cpb, tok, sq, rows8, state = _seq_specs(n_seq, seq, True)
    t = n_seq * seq

    def body(w_ref, u_ref, qd_ref, kd_ref, qk_ref, egl_ref, st_ref, do_ref, dw_ref, du_ref, dqd_ref, dkd_ref, dqk_ref,
             degl_ref, ds_s):
        @pl.when(pl.program_id(1) == 0)
        def _():
            ds_s[...] = jnp.zeros_like(ds_s)

        def step(m, carry):
            n = cpb - 1 - m
            rows = _chunk_rows(n)
            eglb = egl_ref[n]
            degl_ref[n] = jnp.zeros((HALO, LANES), F32)
            heads = range(DN_HEADS)
            _, vjp = jax.vjp(_seq_fn, [w_ref[rows, _head_cols(h)].astype(F32) for h in heads],
                             [u_ref[rows, _head_cols(h)] for h in heads],
                             [qd_ref[rows, _head_cols(h)].astype(F32) for h in heads],
                             [kd_ref[rows, _head_cols(h)].astype(F32) for h in heads],
                             [qk_ref[n, h] for h in heads], [eglb[h:h + 1, :] for h in heads], [st_ref[n, h] for h in heads])
            dw, du, dqd, dkd, dqk, degl, ds_in = vjp(([do_ref[rows, _head_cols(h)] for h in heads], [ds_s[h] for h in heads]))
            for h in heads:
                cols = _head_cols(h)
                dw_ref[rows, cols] = dw[h]
                du_ref[rows, cols] = du[h]
                dqd_ref[rows, cols] = dqd[h]
                dkd_ref[rows, cols] = dkd[h]
                dqk_ref[n, h] = dqk[h]
                degl_ref[n, h:h + 1, :] = degl[h]
                ds_s[h] = ds_in[h]
            return carry

        lax.fori_loop(0, cpb, step, 0)

    return pl.pallas_call(
        body, name=name, grid=(n_seq, nb),
        in_specs=[tok, tok, tok, tok, sq, rows8, state, tok],
        out_specs=(tok, tok, tok, tok, sq, rows8),
        out_shape=(jax.ShapeDtypeStruct((t, DN_WIDTH), F32),) * 4
        + (jax.ShapeDtypeStruct((t // DN_CHUNK, DN_HEADS, DN_CHUNK, DN_CHUNK), F32),
           jax.ShapeDtypeStruct((t // DN_CHUNK, HALO, LANES), F32)),
        scratch_shapes=[pltpu.VMEM((DN_HEADS, DN_HEAD_DIM, DN_HEAD_DIM), F32)],
        compiler_params=_cparams(("parallel", "arbitrary")),
    )(w, u, qd, kd, qk, egl, states, do)


def _dn_gate(o, z, dnw):
    return o * lax.rsqrt(jnp.mean(o * o, axis=-1, keepdims=True) + EPS) * dnw * _silu(z)


def _mix_out_fwd(x, sg, o, z, wo_sg, wo_dn, dnw, *, name):
    t = x.shape[0]
    tm = _tm(t)

    def body(x_ref, sg_ref, o_ref, z_ref, wsg_ref, wdn_ref, dnw_ref, y_ref, dn_s):
        for h, (oh, zh) in enumerate(zip(_split_heads(o_ref, 0), _split_heads(z_ref, 0))):
            dn_s[:, h * DN_HEAD_DIM:(h + 1) * DN_HEAD_DIM] = _dn_gate(oh, zh, dnw_ref[...]).astype(BF16)
        y_ref[...] = (x_ref[...] + jnp.dot(sg_ref[...].astype(BF16), wsg_ref[...], preferred_element_type=F32)
                      + jnp.dot(dn_s[...], wdn_ref[...], preferred_element_type=F32))

    row = lambda i: (i, 0)
    const = lambda i: (0, 0)
    half = pl.BlockSpec((tm, DN_WIDTH), row)
    return pl.pallas_call(
        body, name=name, grid=(t // tm,),
        in_specs=[pl.BlockSpec((tm, D_MODEL), row), half, half, half, pl.BlockSpec((SG_WIDTH, D_MODEL), const),
                  pl.BlockSpec((DN_WIDTH, D_MODEL), const), pl.BlockSpec((1, DN_HEAD_DIM), const)],
        out_specs=pl.BlockSpec((tm, D_MODEL), row),
        out_shape=jax.ShapeDtypeStruct((t, D_MODEL), F32),
        scratch_shapes=[pltpu.VMEM((tm, DN_WIDTH), BF16)],
        compiler_params=_cparams(("parallel",)),
    )(x, sg, o, z, wo_sg, wo_dn, dnw)


def _mix_out_bwd(dy, sg, o, z, wo_sg, wo_dn, dnw, *, name):
    t = dy.shape[0]
    tm = _tm(t)

    def body(dy_ref, sg_ref, o_ref, z_ref, wsg_ref, wdn_ref, dnw_ref, dsg_ref, do_ref, dz_ref, dwsg_ref, dwdn_ref, ddnw_ref, dn_s):
        i = pl.program_id(0)
        dyb = dy_ref[...].astype(BF16)
        nt = (((1,), (1,)), ((), ()))
        tn = (((0,), (0,)), ((), ()))
        dsg_ref[...] = lax.dot_general(dyb, wsg_ref[...], nt, preferred_element_type=F32)
        ddn = lax.dot_general(dyb, wdn_ref[...], nt, preferred_element_type=F32)
        ddnw = None
        for h, (oh, zh) in enumerate(zip(_split_heads(o_ref, 0), _split_heads(z_ref, 0))):
            cols = slice(h * DN_HEAD_DIM, (h + 1) * DN_HEAD_DIM)
            out, vjp = jax.vjp(_dn_gate, oh, zh, dnw_ref[...])
            dn_s[:, cols] = out.astype(BF16)
            doh, dzh, dw = vjp(ddn[:, cols])
            do_ref[:, cols] = doh
            dz_ref[:, cols] = dzh
            ddnw = dw if ddnw is None else ddnw + dw
        _acc_out(ddnw_ref, i == 0, ddnw)
        _acc_out(dwsg_ref, i == 0, lax.dot_general(sg_ref[...].astype(BF16), dyb, tn, preferred_element_type=F32))
        _acc_out(dwdn_ref, i == 0, lax.dot_general(dn_s[...], dyb, tn, preferred_element_type=F32))

    row = lambda i: (i, 0)
    const = lambda i: (0, 0)
    half = pl.BlockSpec((tm, DN_WIDTH), row)
    wspec = pl.BlockSpec((DN_WIDTH, D_MODEL), const)
    return pl.pallas_call(
        body, name=name, grid=(t // tm,),
        in_specs=[pl.BlockSpec((tm, D_MODEL), row), half, half, half, wspec, wspec, pl.BlockSpec((1, DN_HEAD_DIM), const)],
        out_specs=(half, half, half, wspec, wspec, pl.BlockSpec((1, DN_HEAD_DIM), const)),
        out_shape=(jax.ShapeDtypeStruct((t, DN_WIDTH), F32),) * 3 + (jax.ShapeDtypeStruct((DN_WIDTH, D_MODEL), F32),) * 2
        + (jax.ShapeDtypeStruct((1, DN_HEAD_DIM), F32),),
        scratch_shapes=[pltpu.VMEM((tm, DN_WIDTH), BF16)],
        compiler_params=_cparams(("arbitrary",)),
    )(dy, sg, o, z, wo_sg, wo_dn, dnw)


_MESH = pl.DeviceIdType.MESH
_HBM = pl.BlockSpec(memory_space=pl.ANY)


def _mesh_pos():
    x, y, c = lax.axis_index("x"), lax.axis_index("y"), lax.axis_index("c")
    return x, y, c, [(1 - x, y), (x, 1 - y), (1 - x, 1 - y)]


def _gather2(arrs, *, name):
    n = len(arrs)
    slots = N_DEV - 1

    def body(*refs):
        in_refs, out_refs = refs[:n], refs[n:2 * n]
        send_sems, recv_sems, local_sems = refs[2 * n:]
        x, y, c, chips = _mesh_pos()
        me, sibling = (x, y, c), (x, y, 1 - c)

        def copy(k, slot, block, to, src=None):
            dst = out_refs[k].at[4 * block[0] + 2 * block[1] + block[2]]
            return pltpu.make_async_remote_copy(src_ref=dst if src is None else src, dst_ref=dst,
                                                send_sem=send_sems.at[k * slots + slot], recv_sem=recv_sems.at[k * slots + slot],
                                                device_id=to, device_id_type=_MESH)

        local = [pltpu.make_async_copy(in_refs[k], out_refs[k].at[4 * x + 2 * y + c], local_sems.at[k]) for k in range(n)]
        sent = []
        for k in range(n):
            sent.append(copy(k, 0, me, sibling, src=in_refs[k]))
            sent += [copy(k, 1 + j, me, (*chip, c), src=in_refs[k]) for j, chip in enumerate(chips)]
        for cp in local + sent:
            cp.start()
        for j, chip in enumerate(chips):
            for k in range(n):
                copy(k, 1 + j, (*chip, c), me).wait_recv()
                passed = copy(k, 4 + j, (*chip, c), sibling)
                passed.start()
                sent.append(passed)
        for k in range(n):
            copy(k, 0, sibling, me).wait_recv()
            for j, chip in enumerate(chips):
                copy(k, 4 + j, (*chip, 1 - c), me).wait_recv()
        for cp in sent:
            cp.wait_send()
        for cp in local:
            cp.wait()

    return pl.pallas_call(
        body, name=name, in_specs=[_HBM] * n, out_specs=(_HBM,) * n,
        out_shape=tuple(jax.ShapeDtypeStruct((N_DEV,) + a.shape, a.dtype) for a in arrs),
        scratch_shapes=[pltpu.SemaphoreType.DMA((n * slots,)), pltpu.SemaphoreType.DMA((n * slots,)),
                        pltpu.SemaphoreType.DMA((n,))],
    )(*arrs)


def _pair_swap(arrs, *, name):
    n = len(arrs)

    def body(*refs):
        in_refs, out_refs, send_sems, recv_sems = refs[:n], refs[n:2 * n], refs[2 * n], refs[2 * n + 1]
        x, y, c, _ = _mesh_pos()
        copies = [pltpu.make_async_remote_copy(src_ref=in_refs[k].at[1 - c], dst_ref=out_refs[k], send_sem=send_sems.at[k],
                                               recv_sem=recv_sems.at[k], device_id=(x, y, 1 - c), device_id_type=_MESH)
                  for k in range(n)]
        for cp in copies:
            cp.start()
        for cp in copies:
            cp.wait()

    return pl.pallas_call(
        body, name=name, in_specs=[_HBM] * n, out_specs=(_HBM,) * n,
        out_shape=tuple(jax.ShapeDtypeStruct(a.shape[1:], a.dtype) for a in arrs),
        scratch_shapes=[pltpu.SemaphoreType.DMA((n,)), pltpu.SemaphoreType.DMA((n,))],
    )(*arrs)


def _chip_exchange(arrs, *, name):
    n = len(arrs)
    slots = 3

    def body(*refs):
        in_refs, out_refs = refs[:n], refs[n:2 * n]
        send_sems, recv_sems, local_sems = refs[2 * n:]
        x, y, c, chips = _mesh_pos()
        mine = 2 * x + y
        copies = [pltpu.make_async_copy(in_refs[k].at[mine], out_refs[k].at[mine], local_sems.at[k]) for k in range(n)]
        for j, chip in enumerate(chips):
            for k in range(n):
                copies.append(pltpu.make_async_remote_copy(
                    src_ref=in_refs[k].at[2 * chip[0] + chip[1]], dst_ref=out_refs[k].at[mine],
                    send_sem=send_sems.at[k * slots + j], recv_sem=recv_sems.at[k * slots + j],
                    device_id=(*chip, c), device_id_type=_MESH))
        for cp in copies:
            cp.start()
        for cp in copies:
            cp.wait()

    return pl.pallas_call(
        body, name=name, in_specs=[_HBM] * n, out_specs=(_HBM,) * n,
        out_shape=tuple(jax.ShapeDtypeStruct(a.shape, a.dtype) for a in arrs),
        scratch_shapes=[pltpu.SemaphoreType.DMA((n * slots,)), pltpu.SemaphoreType.DMA((n * slots,)),
                        pltpu.SemaphoreType.DMA((n,))],
    )(*arrs)


def _pair_add(p, r, core, *, name):
    _, n_chip, rows, cols = p.shape
    rb = _row_block(rows)

    def body(core_ref, p_ref, r_ref, o_ref):
        o_ref[...] = (p_ref[...] + r_ref[...]).astype(BF16)

    return pl.pallas_call(
        body, name=name,
        grid_spec=pltpu.PrefetchScalarGridSpec(
            num_scalar_prefetch=1, grid=(n_chip, rows // rb),
            in_specs=[pl.BlockSpec((None, None, rb, cols), lambda s, i, core_ref: (core_ref[0], s, i, 0)),
                      pl.BlockSpec((None, rb, cols), lambda s, i, core_ref: (s, i, 0))],
            out_specs=pl.BlockSpec((None, rb, cols), lambda s, i, core_ref: (s, i, 0))),
        out_shape=jax.ShapeDtypeStruct((n_chip, rows, cols), BF16),
        compiler_params=_cparams(("parallel", "parallel")),
    )(core, p, r)


def _row_block(rows, limit=256):
    best = rows
    for cand in range(8, limit + 1, 8):
        if rows % cand == 0:
            best = cand
    return best if rows > limit else rows


def _adam(gp, w, m, v, *, name):
    p, rows, cols = gp.shape
    rb = _row_block(rows)

    def body(gp_ref, w_ref, m_ref, v_ref, g_ref, d_ref, m2_ref, v2_ref):
        g = gp_ref[0].astype(F32)
        for s in range(1, p):
            g = g + gp_ref[s].astype(F32)
        m2 = ADAM_B1 * m_ref[...] + (1.0 - ADAM_B1) * g
        v2 = ADAM_B2 * v_ref[...] + (1.0 - ADAM_B2) * (g * g)
        m_hat = m2 / (1.0 - ADAM_B1 ** ADAM_STEP)
        v_hat = v2 / (1.0 - ADAM_B2 ** ADAM_STEP)
        g_ref[...] = g
        d_ref[...] = -ADAM_LR * (m_hat / (jnp.sqrt(v_hat) + ADAM_EPS) + ADAM_WD * w_ref[...])
        m2_ref[...] = m2
        v2_ref[...] = v2

    blk = pl.BlockSpec((rb, cols), lambda i: (i, 0))
    return pl.pallas_call(
        body, name=name, grid=(rows // rb,),
        in_specs=[pl.BlockSpec((p, rb, cols), lambda i: (0, i, 0)), blk, blk, blk],
        out_specs=(blk,) * 4, out_shape=(jax.ShapeDtypeStruct((rows, cols), F32),) * 4,
        compiler_params=_cparams(("parallel",)),
    )(gp, w, m, v)


def _cols_full(g):
    return jnp.transpose(g, (1, 0, 2)).reshape(g.shape[1], N_DEV * g.shape[2])


def _cols_pieces(full):
    r, c = full.shape
    return jnp.transpose(full.reshape(r, N_DEV, c // N_DEV), (1, 0, 2))


def _pad_lanes(a, width=LANES):
    return jnp.pad(a, ((0, 0), (0, width - a.shape[1])))


def _chunk_rows_of(a):
    by_chunk = jnp.transpose(a[:, :DN_HEADS].reshape(-1, DN_CHUNK, DN_HEADS), (0, 2, 1))
    return jnp.pad(by_chunk, ((0, 0), (0, HALO - DN_HEADS), (0, 0)))


_SMALL = (("ffn1_norm", D_MODEL), ("mix_norm", D_MODEL), ("ffn2_norm", D_MODEL), ("final_norm", D_MODEL), ("a_log", DN_HEADS),
          ("dt_bias", DN_HEADS), ("dn_norm", DN_HEAD_DIM), ("sg_ln_g", SG_WIDTH), ("sg_ln_b", SG_WIDTH),
          ("sg_w", SG_GROUPS * SG_CHUNK * SG_CHUNK), ("sg_b", SG_GROUPS * SG_CHUNK), ("conv_w", CONV_K * 3 * DN_WIDTH))
_SMALL_ROWS = 1128
_SMALL_SHAPES = {"ffn1_norm": (1, D_MODEL), "mix_norm": (1, D_MODEL), "ffn2_norm": (1, D_MODEL), "final_norm": (D_MODEL,),
                 "a_log": (1, DN_HEADS), "dt_bias": (1, DN_HEADS), "dn_norm": (1, DN_HEAD_DIM), "sg_ln_g": (1, SG_WIDTH),
                 "sg_ln_b": (1, SG_WIDTH), "sg_w": (1, SG_GROUPS, SG_CHUNK, SG_CHUNK), "sg_b": (1, SG_GROUPS, SG_CHUNK)}


def _pack_small(d):
    flat = jnp.concatenate([d[name].reshape(-1) for name, _ in _SMALL])
    return jnp.pad(flat, (0, _SMALL_ROWS * LANES - flat.shape[0])).reshape(_SMALL_ROWS, LANES)


def _unpack_small(a):
    flat, out, at = a.reshape(-1), {}, 0
    for name, size in _SMALL:
        out[name] = flat[at:at + size]
        at += size
    return out


def kernel(x, ffn1_norm, ffn1_w_gate, ffn1_w_up, ffn1_w_down, mix_norm, w_in, conv_w, a_log, dt_bias, dn_norm, sg_ln_g, sg_ln_b, sg_w, sg_b, w_out, ffn2_norm, ffn2_w_gate, ffn2_w_up, ffn2_w_down, final_norm, loss_target, m_ffn1_norm, m_ffn1_w_gate, m_ffn1_w_up, m_ffn1_w_down, m_mix_norm, m_w_in, m_conv_w, m_a_log, m_dt_bias, m_dn_norm, m_sg_ln_g, m_sg_ln_b, m_sg_w, m_sg_b, m_w_out, m_ffn2_norm, m_ffn2_w_gate, m_ffn2_w_up, m_ffn2_w_down, m_final_norm, v_ffn1_norm, v_ffn1_w_gate, v_ffn1_w_up, v_ffn1_w_down, v_mix_norm, v_w_in, v_conv_w, v_a_log, v_dt_bias, v_dn_norm, v_sg_ln_g, v_sg_ln_b, v_sg_w, v_sg_b, v_w_out, v_ffn2_norm, v_ffn2_w_gate, v_ffn2_w_up, v_ffn2_w_down, v_final_norm):
    weights = dict(ffn1_norm=ffn1_norm, ffn1_w_gate=ffn1_w_gate, ffn1_w_up=ffn1_w_up, ffn1_w_down=ffn1_w_down, mix_norm=mix_norm, w_in=w_in, conv_w=conv_w, a_log=a_log, dt_bias=dt_bias, dn_norm=dn_norm, sg_ln_g=sg_ln_g, sg_ln_b=sg_ln_b, sg_w=sg_w, sg_b=sg_b, w_out=w_out, ffn2_norm=ffn2_norm, ffn2_w_gate=ffn2_w_gate, ffn2_w_up=ffn2_w_up, ffn2_w_down=ffn2_w_down, final_norm=final_norm)
    mom_m = dict(ffn1_norm=m_ffn1_norm, ffn1_w_gate=m_ffn1_w_gate, ffn1_w_up=m_ffn1_w_up, ffn1_w_down=m_ffn1_w_down, mix_norm=m_mix_norm, w_in=m_w_in, conv_w=m_conv_w, a_log=m_a_log, dt_bias=m_dt_bias, dn_norm=m_dn_norm, sg_ln_g=m_sg_ln_g, sg_ln_b=m_sg_ln_b, sg_w=m_sg_w, sg_b=m_sg_b, w_out=m_w_out, ffn2_norm=m_ffn2_norm, ffn2_w_gate=m_ffn2_w_gate, ffn2_w_up=m_ffn2_w_up, ffn2_w_down=m_ffn2_w_down, final_norm=m_final_norm)
    mom_v = dict(ffn1_norm=v_ffn1_norm, ffn1_w_gate=v_ffn1_w_gate, ffn1_w_up=v_ffn1_w_up, ffn1_w_down=v_ffn1_w_down, mix_norm=v_mix_norm, w_in=v_w_in, conv_w=v_conv_w, a_log=v_a_log, dt_bias=v_dt_bias, dn_norm=v_dn_norm, sg_ln_g=v_sg_ln_g, sg_ln_b=v_sg_ln_b, sg_w=v_sg_w, sg_b=v_sg_b, w_out=v_w_out, ffn2_norm=v_ffn2_norm, ffn2_w_gate=v_ffn2_w_gate, ffn2_w_up=v_ffn2_w_up, ffn2_w_down=v_ffn2_w_down, final_norm=v_final_norm)
    order = list(weights)
    big = ("ffn1_w_gate", "ffn1_w_up", "ffn1_w_down", "w_in", "w_out", "ffn2_w_gate", "ffn2_w_up", "ffn2_w_down")
    col_sharded = ("ffn1_w_gate", "ffn1_w_up", "w_in", "ffn2_w_gate", "ffn2_w_up")

    n_seq, seq, _ = x.shape
    t = n_seq * seq
    me = 4 * lax.axis_index("x") + 2 * lax.axis_index("y") + lax.axis_index("c")
    x0 = x.reshape(t, D_MODEL)
    tgt = loss_target.reshape(t, D_MODEL)

    gathered = _gather2([weights[n][0].astype(BF16) for n in big] + [conv_w[0]], name="gather_weights")
    full = {n: (_cols_full(g) if n in col_sharded else g.reshape(-1, g.shape[-1])) for n, g in zip(big, gathered)}
    conv_full = _cols_full(gathered[-1])
    w_in_f = full["w_in"]
    offs = (0, SG_WIDTH, 2 * SG_WIDTH, 2 * SG_WIDTH + 3 * DN_WIDTH, 2 * SG_WIDTH + 4 * DN_WIDTH)
    n_proj = offs[-1]
    ws = [w_in_f[:, offs[0]:offs[1]], w_in_f[:, offs[1]:offs[2]], w_in_f[:, offs[2]:offs[3]], w_in_f[:, offs[3]:offs[4]],
          _pad_lanes(w_in_f[:, n_proj:n_proj + DN_HEADS]), _pad_lanes(w_in_f[:, n_proj + DN_HEADS:n_proj + 2 * DN_HEADS])]
    wo_sg, wo_dn = full["w_out"][:SG_WIDTH], full["w_out"][SG_WIDTH:]
    alog, dtb = _pad_lanes(a_log), _pad_lanes(dt_bias)
    sgbt = _pad_lanes(sg_b[0].T)
    fnw = final_norm.reshape(1, D_MODEL)

    x1, h1, g1, u1 = _ffn_fwd(x0, ffn1_norm, full["ffn1_w_gate"], full["ffn1_w_up"], full["ffn1_w_down"], name="ffn1_fwd")
    u, v, qkv, z, bpre, apre = _mix_in_fwd(x1, mix_norm, ws, name="mix_in_fwd")
    sg_out = _sg_fwd(u, v, sg_ln_g, sg_ln_b, sg_w[0], sgbt, name="sg_fwd")
    q, k, vv, beta, gc = _dn_prep_fwd(qkv, bpre, apre, conv_full, alog, dtb, seq, name="dn_prep_fwd")
    grow = _chunk_rows_of(gc)
    wy_w, wy_u, q_dec, k_dec, qk, egl, inv = _delta_prep(q, k, vv, gc, grow, beta, name="delta_prep")
    o, states = _delta_seq_fwd(wy_w, wy_u, q_dec, k_dec, qk, egl, n_seq, seq, name="delta_seq_fwd")
    x2 = _mix_out_fwd(x1, sg_out, o, z, wo_sg, wo_dn, dn_norm, name="mix_out_fwd")
    dx3, loss_part, d_fn, h2, g2, u2 = _ffn_fwd(x2, ffn2_norm, full["ffn2_w_gate"], full["ffn2_w_up"], full["ffn2_w_down"],
                                                tgt, fnw, name="ffn2_fwd_loss")
    loss = lax.psum(loss_part[0, 0], ("x", "y", "c"))

    dx2, d_n2, d_g2, d_u2, d_d2 = _ffn_bwd(x2, ffn2_norm, h2, g2, u2, full["ffn2_w_gate"], full["ffn2_w_up"],
                                           full["ffn2_w_down"], dx3, name="ffn2_bwd")
    dsg, do, dz, d_wo_sg, d_wo_dn, d_dnw = _mix_out_bwd(dx2, sg_out, o, z, wo_sg, wo_dn, dn_norm, name="mix_out_bwd")
    d_seq = _delta_seq_bwd(wy_w, wy_u, q_dec, k_dec, qk, egl, states, do, n_seq, seq, name="delta_seq_bwd")
    dq, dk, dv, dgc_a, dgrow, dbeta = _delta_par_bwd(q, k, vv, gc, grow, beta, inv, *d_seq, name="delta_par_bwd")
    dgc_b = _pad_lanes(jnp.transpose(dgrow[:, :DN_HEADS, :], (0, 2, 1)).reshape(t, DN_HEADS))
    dy_conv, dbpre, dapre, d_alog, d_dtb = _dn_prep_bwd(qkv, bpre, apre, conv_full, alog, dtb, dq, dk, dv, dbeta, dgc_a, dgc_b,
                                                        seq, name="dn_prep_bwd")
    dqkv, d_conv = _conv_bwd(qkv, dy_conv, conv_full, seq, name="conv_bwd")
    du, dvv, d_lng, d_lnb, d_wc, d_sgbt = _sg_bwd(u, v, sg_ln_g, sg_ln_b, sg_w[0], sgbt, dsg, name="sg_bwd")
    dx1, d_mixn, d_ws = _split3(_mix_in_bwd(x1, mix_norm, ws, dx2, (du, dvv, dqkv, dz, dbpre, dapre), name="mix_in_bwd"))
    grad_x, d_n1, d_g1, d_u1, d_d1 = _ffn_bwd(x0, ffn1_norm, h1, g1, u1, full["ffn1_w_gate"], full["ffn1_w_up"],
                                              full["ffn1_w_down"], dx1, name="ffn1_bwd")

    def by_core(p8):
        return jnp.moveaxis(p8.reshape((4, 2) + p8.shape[1:]), 1, 0)

    ff_cols = _cols_pieces

    def ff_rows(d_wd):
        return d_wd.reshape(N_DEV, D_FF // N_DEV, D_MODEL)

    d_w_in = jnp.concatenate([d_ws[0], d_ws[1], d_ws[2], d_ws[3], d_ws[4][:, :DN_HEADS], d_ws[5][:, :DN_HEADS]], axis=1)
    d_w_out = jnp.concatenate([d_wo_sg, d_wo_dn], axis=0)
    pieces = dict(ffn1_w_gate=ff_cols(d_g1), ffn1_w_up=ff_cols(d_u1), ffn1_w_down=ff_rows(d_d1), w_in=_cols_pieces(d_w_in),
                  w_out=d_w_out.reshape(N_DEV, D_MODEL // N_DEV, D_MODEL), ffn2_w_gate=ff_cols(d_g2), ffn2_w_up=ff_cols(d_u2),
                  ffn2_w_down=ff_rows(d_d2))
    own = [by_core(pieces[n]) for n in big]
    from_sibling = _pair_swap(own, name="grads_to_sibling")
    core = lax.axis_index("c").astype(jnp.int32).reshape(1)
    chip_sums = [_pair_add(p, r, core, name="pair_add_" + n) for n, p, r in zip(big, own, from_sibling)]
    received = _chip_exchange(chip_sums, name="grads_to_owner")
    res = {}
    for n, gp in zip(big, received):
        res[n] = _adam(gp, weights[n][0], mom_m[n][0], mom_v[n][0], name="adam_" + n)

    small_grads = dict(ffn1_norm=d_n1, mix_norm=d_mixn, ffn2_norm=d_n2, final_norm=d_fn, a_log=d_alog[:, :DN_HEADS],
                       dt_bias=d_dtb[:, :DN_HEADS], dn_norm=d_dnw, sg_ln_g=d_lng, sg_ln_b=d_lnb, sg_w=d_wc,
                       sg_b=d_sgbt[:, :SG_GROUPS].T, conv_w=d_conv[:CONV_K])
    (small_parts,) = _gather2([_pack_small(small_grads)], name="gather_small_grads")
    zeros_conv = jnp.zeros((CONV_K * 3 * DN_WIDTH,), F32)
    packed = [_pack_small({**{n: src[n] for n, _ in _SMALL if n != "conv_w"}, "conv_w": zeros_conv})
              for src in (weights, mom_m, mom_v)]
    small_res = [_unpack_small(a) for a in _adam(small_parts, *packed, name="adam_small")]
    conv_grad = lax.dynamic_slice_in_dim(small_res[0]["conv_w"].reshape(CONV_K, 3 * DN_WIDTH), me * (3 * DN_WIDTH // N_DEV),
                                         3 * DN_WIDTH // N_DEV, axis=1)
    res["conv_w"] = _adam(conv_grad[None], conv_w[0], m_conv_w[0], v_conv_w[0], name="adam_conv_w")

    outs = [[], [], [], []]
    for n in order:
        for kind in range(4):
            if n in res:
                outs[kind].append(res[n][kind][None])
            else:
                outs[kind].append(small_res[kind][n].reshape(_SMALL_SHAPES[n]))
    return (loss, grad_x.reshape(x.shape), *outs[0], *outs[1], *outs[2], *outs[3])


def _split3(r):
    return r[0], r[1], r[2:]
```

```python
import functools

import jax
import jax.numpy as jnp
from jax import lax
from jax.experimental import pallas as pl
from jax.experimental.pallas import tpu as pltpu

F32 = jnp.float32
BF16 = jnp.bfloat16

D_MODEL = 1024
D_FF = 2816
SG_WIDTH = 512
SG_GROUPS = 8
SG_GROUP_DIM = 64
SG_CHUNK = 128
DN_WIDTH = 512
DN_HEAD_DIM = 128
DN_HEADS = 4
DN_CHUNK = 64
CONV_K = 4
EPS = 1e-6
N_DEV = 8
LANES = 128
HALO = 8

ADAM_LR = 0.001
ADAM_B1 = 0.9
ADAM_B2 = 0.999
ADAM_EPS = 1e-08
ADAM_WD = 0.01
ADAM_STEP = 10

VMEM_LIMIT = 60 * 1024 * 1024
TOKEN_BLOCK = 512
FF_BLOCK_FWD = 1408

_HI = lax.Precision.HIGHEST


def _cparams(sem):
    return pltpu.CompilerParams(dimension_semantics=sem, vmem_limit_bytes=VMEM_LIMIT)


def _tm(t, pref=TOKEN_BLOCK):
    return min(pref, t)


def _dg(a, b, ca, cb, precision):
    if precision is not None:
        return lax.dot_general(a, b, (((ca,), (cb,)), ((), ())), precision=precision, preferred_element_type=F32)
    return lax.dot_general(a.astype(BF16), b.astype(BF16), (((ca,), (cb,)), ((), ())), preferred_element_type=F32)


def _make_mm(exact):
    @jax.custom_vjp
    def mm(a, b):
        return _dg(a, b, 1, 0, exact)

    @jax.custom_vjp
    def mm_nt(a, b):
        return _dg(a, b, 1, 1, exact)

    @jax.custom_vjp
    def mm_tn(a, b):
        return _dg(a, b, 0, 0, exact)

    mm.defvjp(lambda a, b: (mm(a, b), (a, b)), lambda r, g: (mm_nt(g, r[1]), mm_tn(r[0], g)))
    mm_nt.defvjp(lambda a, b: (mm_nt(a, b), (a, b)), lambda r, g: (mm(g, r[1]), mm_tn(g, r[0])))
    mm_tn.defvjp(lambda a, b: (mm_tn(a, b), (a, b)), lambda r, g: (mm_nt(r[1], g), mm(r[0], g)))
    return mm, mm_nt, mm_tn


mm, mm_nt, mm_tn = _make_mm(None)
mmx, mmx_nt, mmx_tn = _make_mm(_HI)
mmh, _, _ = _make_mm(lax.Precision.HIGH)


def _sigmoid(x):
    return 1.0 / (1.0 + jnp.exp(-x))


def _silu(x):
    return x * _sigmoid(x)


def _softplus(x):
    neg_abs = jnp.where(x > 0, -x, x)
    return jnp.where(x > 0, x, 0.0) + jnp.log(1.0 + jnp.exp(neg_abs))


def _gelu(x):
    return 0.5 * x * (1.0 + jnp.tanh(0.7978845608028654 * (x + 0.044715 * (x * x * x))))


def _rms_fwd(x, g):
    r = lax.rsqrt(jnp.mean(x * x, axis=-1, keepdims=True) + EPS)
    xh = x * r
    return xh * g, xh, r


def _rms_bwd(dh, xh, r, g):
    dxh = dh * g
    dx = r * (dxh - xh * jnp.mean(dxh * xh, axis=-1, keepdims=True))
    return dx, jnp.sum(dh * xh, axis=0, keepdims=True)


def _acc_out(ref, first, val):
    @pl.when(first)
    def _():
        ref[...] = val

    @pl.when(jnp.logical_not(first))
    def _():
        ref[...] += val


def _ffn_fwd(x, nw, wg, wu, wd, tgt=None, fnw=None, *, name):
    t = x.shape[0]
    tm, fb = _tm(t), FF_BLOCK_FWD
    n_t, n_f = t // tm, D_FF // fb
    with_loss = tgt is not None

    def body(*refs):
        if with_loss:
            (x_ref, nw_ref, wg_ref, wu_ref, wd_ref, tgt_ref, fnw_ref, dy_ref, loss_ref, dfn_ref, h_ref, g_ref, u_ref,
             acc_s) = refs
        else:
            x_ref, nw_ref, wg_ref, wu_ref, wd_ref, y_ref, h_ref, g_ref, u_ref, acc_s = refs
        i, j = pl.program_id(0), pl.program_id(1)

        @pl.when(j == 0)
        def _():
            h, _, _ = _rms_fwd(x_ref[...], nw_ref[...])
            h_ref[...] = h.astype(BF16)
            acc_s[...] = jnp.zeros_like(acc_s)

        h = h_ref[...]
        g = jnp.dot(h, wg_ref[...], preferred_element_type=F32)
        u = jnp.dot(h, wu_ref[...], preferred_element_type=F32)
        g_ref[...] = g.astype(BF16)
        u_ref[...] = u.astype(BF16)
        a = _silu(g) * u
        acc_s[...] += jnp.dot(a.astype(BF16), wd_ref[...], preferred_element_type=F32)

        @pl.when(j == n_f - 1)
        def _():
            y = x_ref[...] + 0.5 * acc_s[...]
            if not with_loss:
                y_ref[...] = y
            else:
                gf = fnw_ref[...]
                out, xh, r = _rms_fwd(y, gf)
                err = out - tgt_ref[...]
                part = 0.5 * jnp.sum(jnp.mean(err * err, axis=-1, keepdims=True), axis=0, keepdims=True)
                d_out = err * (1.0 / D_MODEL)
                dy, dgf = _rms_bwd(d_out, xh, r, gf)
                dy_ref[...] = dy
                _acc_out(loss_ref, i == 0, jnp.broadcast_to(part, loss_ref.shape))
                _acc_out(dfn_ref, i == 0, dgf)

    row = lambda i, j: (i, 0)
    const = lambda i, j: (0, 0)
    in_specs = [
        pl.BlockSpec((tm, D_MODEL), row),
        pl.BlockSpec((1, D_MODEL), const),
        pl.BlockSpec((D_MODEL, fb), lambda i, j: (0, j)),
        pl.BlockSpec((D_MODEL, fb), lambda i, j: (0, j)),
        pl.BlockSpec((fb, D_MODEL), lambda i, j: (j, 0)),
    ]
    args = [x, nw, wg, wu, wd]
    saved_shape = (jax.ShapeDtypeStruct((t, D_MODEL), BF16), jax.ShapeDtypeStruct((t, D_FF), BF16),
                   jax.ShapeDtypeStruct((t, D_FF), BF16))
    saved_specs = (pl.BlockSpec((tm, D_MODEL), row), pl.BlockSpec((tm, fb), lambda i, j: (i, j)),
                   pl.BlockSpec((tm, fb), lambda i, j: (i, j)))
    if with_loss:
        in_specs += [pl.BlockSpec((tm, D_MODEL), row), pl.BlockSpec((1, D_MODEL), const)]
        args += [tgt, fnw]
        out_shape = (jax.ShapeDtypeStruct((t, D_MODEL), F32), jax.ShapeDtypeStruct((8, LANES), F32),
                     jax.ShapeDtypeStruct((1, D_MODEL), F32)) + saved_shape
        out_specs = (pl.BlockSpec((tm, D_MODEL), row), pl.BlockSpec((8, LANES), const),
                     pl.BlockSpec((1, D_MODEL), const)) + saved_specs
        sem = ("arbitrary", "arbitrary")
    else:
        out_shape = (jax.ShapeDtypeStruct((t, D_MODEL), F32),) + saved_shape
        out_specs = (pl.BlockSpec((tm, D_MODEL), row),) + saved_specs
        sem = ("parallel", "arbitrary")
    return pl.pallas_call(
        body, name=name, grid=(n_t, n_f), in_specs=in_specs, out_specs=out_specs, out_shape=out_shape,
        scratch_shapes=[pltpu.VMEM((tm, D_MODEL), F32)],
        compiler_params=_cparams(sem),
    )(*args)


def _ffn_bwd_x(x, nw, g, u, wg, wu, wd, dy, *, name):
    t = x.shape[0]
    tm = _tm(t, 256)

    def body(x_ref, nw_ref, g_ref, u_ref, wg_ref, wu_ref, wd_ref, dy_ref, dx_ref, dnw_ref, dg_ref, du_ref, a_ref, dyh_ref):
        i = pl.program_id(0)
        nt = (((1,), (1,)), ((), ()))
        dy = dy_ref[...]
        dyh = (0.5 * dy).astype(BF16)
        dyh_ref[...] = dyh
        gate, up = g_ref[...].astype(F32), u_ref[...].astype(F32)
        s = _sigmoid(gate)
        gs = gate * s
        da = lax.dot_general(dyh, wd_ref[...], nt, preferred_element_type=F32)
        dg = (da * up * (s + gs * (1.0 - s))).astype(BF16)
        du = (da * gs).astype(BF16)
        dg_ref[...] = dg
        du_ref[...] = du
        a_ref[...] = (gs * up).astype(BF16)
        dh = (lax.dot_general(dg, wg_ref[...], nt, preferred_element_type=F32)
              + lax.dot_general(du, wu_ref[...], nt, preferred_element_type=F32))
        xv = x_ref[...]
        r = lax.rsqrt(jnp.mean(xv * xv, axis=-1, keepdims=True) + EPS)
        dx, dnw = _rms_bwd(dh, xv * r, r, nw_ref[...])
        dx_ref[...] = dy + dx
        _acc_out(dnw_ref, i == 0, dnw)

    row = lambda i: (i, 0)
    const = lambda i: (0, 0)
    once = pl.Buffered(1)
    wide = pl.BlockSpec((tm, D_FF), row)
    return pl.pallas_call(
        body, name=name, grid=(t // tm,),
        in_specs=[pl.BlockSpec((tm, D_MODEL), row), pl.BlockSpec((1, D_MODEL), const), wide, wide,
                  pl.BlockSpec((D_MODEL, D_FF), const, pipeline_mode=once), pl.BlockSpec((D_MODEL, D_FF), const, pipeline_mode=once),
                  pl.BlockSpec((D_FF, D_MODEL), const, pipeline_mode=once), pl.BlockSpec((tm, D_MODEL), row)],
        out_specs=(pl.BlockSpec((tm, D_MODEL), row), pl.BlockSpec((1, D_MODEL), const), wide, wide, wide,
                   pl.BlockSpec((tm, D_MODEL), row)),
        out_shape=(jax.ShapeDtypeStruct((t, D_MODEL), F32), jax.ShapeDtypeStruct((1, D_MODEL), F32),
                   jax.ShapeDtypeStruct((t, D_FF), BF16), jax.ShapeDtypeStruct((t, D_FF), BF16),
                   jax.ShapeDtypeStruct((t, D_FF), BF16), jax.ShapeDtypeStruct((t, D_MODEL), BF16)),
        compiler_params=_cparams(("arbitrary",)),
    )(x, nw, g, u, wg, wu, wd, dy)


def _wgrad(a, b, bm, bn, *, name):
    k, m = a.shape
    n = b.shape[1]
    tk = _tm(k, 1024)

    def body(a_ref, b_ref, o_ref):
        part = lax.dot_general(a_ref[...], b_ref[...], (((0,), (0,)), ((), ())), preferred_element_type=F32)
        _acc_out(o_ref, pl.program_id(2) == 0, part)

    return pl.pallas_call(
        body, name=name, grid=(m // bm, n // bn, k // tk),
        in_specs=[pl.BlockSpec((tk, bm), lambda i, j, s: (s, i)), pl.BlockSpec((tk, bn), lambda i, j, s: (s, j))],
        out_specs=pl.BlockSpec((bm, bn), lambda i, j, s: (i, j)),
        out_shape=jax.ShapeDtypeStruct((m, n), F32),
        compiler_params=_cparams(("parallel", "parallel", "arbitrary")),
    )(a, b)


def _ffn_bwd(x, nw, h, g, u, wg, wu, wd, dy, *, name):
    dx, dnw, dg, du, a, dyh = _ffn_bwd_x(x, nw, g, u, wg, wu, wd, dy, name=name + "_x")
    half = D_FF // 2
    return (dx, dnw, _wgrad(h, dg, D_MODEL, half, name=name + "_wg"), _wgrad(h, du, D_MODEL, half, name=name + "_wu"),
            _wgrad(a, dyh, half, D_MODEL, name=name + "_wd"))


_PROJ_WIDTHS = (SG_WIDTH, SG_WIDTH, 3 * DN_WIDTH, DN_WIDTH, LANES, LANES)


def _mix_in_fwd(x, nw, ws, *, name):
    t = x.shape[0]
    tm = _tm(t)

    def body(x_ref, nw_ref, *refs):
        w_refs, o_refs = refs[:6], refs[6:]
        h, _, _ = _rms_fwd(x_ref[...], nw_ref[...])
        h = h.astype(BF16)
        for w_ref, o_ref in zip(w_refs, o_refs):
            o_ref[...] = jnp.dot(h, w_ref[...], preferred_element_type=F32)

    row = lambda i: (i, 0)
    const = lambda i: (0, 0)
    return pl.pallas_call(
        body, name=name, grid=(t // tm,),
        in_specs=[pl.BlockSpec((tm, D_MODEL), row), pl.BlockSpec((1, D_MODEL), const)]
        + [pl.BlockSpec((D_MODEL, n), const) for n in _PROJ_WIDTHS],
        out_specs=tuple(pl.BlockSpec((tm, n), row) for n in _PROJ_WIDTHS),
        out_shape=tuple(jax.ShapeDtypeStruct((t, n), F32) for n in _PROJ_WIDTHS),
        compiler_params=_cparams(("parallel",)),
    )(x, nw, *ws)


def _mix_in_bwd(x, nw, ws, dres, dps, *, name):
    t = x.shape[0]
    tm = _tm(t, 256)

    def body(x_ref, nw_ref, dres_ref, *refs):
        w_refs, dp_refs, dx_ref, dnw_ref, dw_refs = refs[:6], refs[6:12], refs[12], refs[13], refs[14:]
        i = pl.program_id(0)
        hf, xh, r = _rms_fwd(x_ref[...], nw_ref[...])
        h = hf.astype(BF16)
        dh = jnp.zeros((tm, D_MODEL), F32)
        for w_ref, dp_ref, dw_ref in zip(w_refs, dp_refs, dw_refs):
            dp = dp_ref[...].astype(BF16)
            dh = dh + lax.dot_general(dp, w_ref[...], (((1,), (1,)), ((), ())), preferred_element_type=F32)
            _acc_out(dw_ref, i == 0, lax.dot_general(h, dp, (((0,), (0,)), ((), ())), preferred_element_type=F32))
        dx, dnw = _rms_bwd(dh, xh, r, nw_ref[...])
        dx_ref[...] = dres_ref[...] + dx
        _acc_out(dnw_ref, i == 0, dnw)

    row = lambda i: (i, 0)
    const = lambda i: (0, 0)
    return pl.pallas_call(
        body, name=name, grid=(t // tm,),
        in_specs=[pl.BlockSpec((tm, D_MODEL), row), pl.BlockSpec((1, D_MODEL), const), pl.BlockSpec((tm, D_MODEL), row)]
        + [pl.BlockSpec((D_MODEL, n), const) for n in _PROJ_WIDTHS]
        + [pl.BlockSpec((tm, n), row) for n in _PROJ_WIDTHS],
        out_specs=(pl.BlockSpec((tm, D_MODEL), row), pl.BlockSpec((1, D_MODEL), const))
        + tuple(pl.BlockSpec((D_MODEL, n), const) for n in _PROJ_WIDTHS),
        out_shape=(jax.ShapeDtypeStruct((t, D_MODEL), F32), jax.ShapeDtypeStruct((1, D_MODEL), F32))
        + tuple(jax.ShapeDtypeStruct((D_MODEL, n), F32) for n in _PROJ_WIDTHS),
        compiler_params=_cparams(("arbitrary",)),
    )(x, nw, dres, *ws, *dps)


def _sg_fn(u, v, lng, lnb, wcs, sgbt):
    lane = lax.broadcasted_iota(jnp.int32, (1, SG_WIDTH), 1)
    lane_b = lax.broadcasted_iota(jnp.int32, (1, LANES), 1)
    rr = lax.broadcasted_iota(jnp.int32, (SG_CHUNK, SG_CHUNK), 0)
    cc = lax.broadcasted_iota(jnp.int32, (SG_CHUNK, SG_CHUNK), 1)
    gu, gv = _gelu(u), _gelu(v)
    mu = jnp.mean(gv, axis=-1, keepdims=True)
    cen = gv - mu
    var = jnp.mean(cen * cen, axis=-1, keepdims=True)
    ln = cen * lax.rsqrt(var + EPS) * lng + lnb
    vs = jnp.zeros_like(u)
    for g in range(SG_GROUPS):
        in_group = jnp.logical_and(lane >= g * SG_GROUP_DIM, lane < (g + 1) * SG_GROUP_DIM)
        w_causal = jnp.where(rr >= cc, wcs[g], 0.0)
        bias = jnp.sum(jnp.where(lane_b == g, sgbt, 0.0), axis=1, keepdims=True)
        vs = vs + jnp.where(in_group, mm(w_causal, ln) + bias, 0.0)
    return gu * vs


def _sg_fwd(u, v, lng, lnb, wc, sgbt, *, name):
    t = u.shape[0]
    tm = _tm(t)

    def body(u_ref, v_ref, lng_ref, lnb_ref, wc_ref, sgbt_ref, o_ref):
        wcs = [wc_ref[g] for g in range(SG_GROUPS)]
        for c in range(tm // SG_CHUNK):
            rows = pl.ds(c * SG_CHUNK, SG_CHUNK)
            o_ref[rows, :] = _sg_fn(u_ref[rows, :], v_ref[rows, :], lng_ref[...], lnb_ref[...], wcs, sgbt_ref[...])

    row = lambda i: (i, 0)
    const = lambda i: (0, 0)
    return pl.pallas_call(
        body, name=name, grid=(t // tm,),
        in_specs=[pl.BlockSpec((tm, SG_WIDTH), row), pl.BlockSpec((tm, SG_WIDTH), row),
                  pl.BlockSpec((1, SG_WIDTH), const), pl.BlockSpec((1, SG_WIDTH), const),
                  pl.BlockSpec((SG_GROUPS, SG_CHUNK, SG_CHUNK), lambda i: (0, 0, 0)), pl.BlockSpec((SG_CHUNK, LANES), const)],
        out_specs=pl.BlockSpec((tm, SG_WIDTH), row),
        out_shape=jax.ShapeDtypeStruct((t, SG_WIDTH), F32),
        compiler_params=_cparams(("parallel",)),
    )(u, v, lng, lnb, wc, sgbt)


def _sg_bwd(u, v, lng, lnb, wc, sgbt, dout, *, name):
    t = u.shape[0]
    tm = _tm(t)

    def body(u_ref, v_ref, lng_ref, lnb_ref, wc_ref, sgbt_ref, do_ref, du_ref, dv_ref, dlng_ref, dlnb_ref, dwc_ref, dsgbt_ref):
        i = pl.program_id(0)
        wcs = [wc_ref[g] for g in range(SG_GROUPS)]
        tot = None
        for c in range(tm // SG_CHUNK):
            rows = pl.ds(c * SG_CHUNK, SG_CHUNK)
            _, vjp = jax.vjp(_sg_fn, u_ref[rows, :], v_ref[rows, :], lng_ref[...], lnb_ref[...], wcs, sgbt_ref[...])
            du, dv, dlng, dlnb, dwcs, dsgbt = vjp(do_ref[rows, :])
            du_ref[rows, :] = du
            dv_ref[rows, :] = dv
            part = (dlng, dlnb, dwcs, dsgbt)
            tot = part if tot is None else jax.tree.map(jnp.add, tot, part)
        dlng, dlnb, dwcs, dsgbt = tot
        _acc_out(dlng_ref, i == 0, dlng)
        _acc_out(dlnb_ref, i == 0, dlnb)
        _acc_out(dsgbt_ref, i == 0, dsgbt)
        for g in range(SG_GROUPS):
            @pl.when(i == 0)
            def _(g=g):
                dwc_ref[g] = dwcs[g]

            @pl.when(i > 0)
            def _(g=g):
                dwc_ref[g] += dwcs[g]

    row = lambda i: (i, 0)
    const = lambda i: (0, 0)
    wspec = pl.BlockSpec((SG_GROUPS, SG_CHUNK, SG_CHUNK), lambda i: (0, 0, 0))
    return pl.pallas_call(
        body, name=name, grid=(t // tm,),
        in_specs=[pl.BlockSpec((tm, SG_WIDTH), row), pl.BlockSpec((tm, SG_WIDTH), row),
                  pl.BlockSpec((1, SG_WIDTH), const), pl.BlockSpec((1, SG_WIDTH), const), wspec,
                  pl.BlockSpec((SG_CHUNK, LANES), const), pl.BlockSpec((tm, SG_WIDTH), row)],
        out_specs=(pl.BlockSpec((tm, SG_WIDTH), row), pl.BlockSpec((tm, SG_WIDTH), row),
                   pl.BlockSpec((1, SG_WIDTH), const), pl.BlockSpec((1, SG_WIDTH), const), wspec,
                   pl.BlockSpec((SG_CHUNK, LANES), const)),
        out_shape=(jax.ShapeDtypeStruct((t, SG_WIDTH), F32), jax.ShapeDtypeStruct((t, SG_WIDTH), F32),
                   jax.ShapeDtypeStruct((1, SG_WIDTH), F32), jax.ShapeDtypeStruct((1, SG_WIDTH), F32),
                   jax.ShapeDtypeStruct((SG_GROUPS, SG_CHUNK, SG_CHUNK), F32), jax.ShapeDtypeStruct((SG_CHUNK, LANES), F32)),
        compiler_params=_cparams(("arbitrary",)),
    )(u, v, lng, lnb, wc, sgbt, dout)


def _conv_taps(ext, w, tm):
    y = None
    for j in range(CONV_K):
        s = CONV_K - 1 - j
        shifted = ext if s == 0 else pltpu.roll(ext, s, 0)
        term = w[j:j + 1, :] * shifted[HALO:HALO + tm, :]
        y = term if y is None else y + term
    return y


def _post_conv(yq, yk, yv, bpre, apre, alog, dtb):
    def l2(a):
        return a * lax.rsqrt(jnp.sum(a * a, axis=-1, keepdims=True) + EPS)

    q = [l2(_silu(a)) for a in yq]
    k = [l2(_silu(a)) for a in yk]
    return q, k, _silu(yv), _sigmoid(bpre), -jnp.exp(alog) * _softplus(apre + dtb)


def _chunk_tril(tm):
    rr = lax.broadcasted_iota(jnp.int32, (tm, tm), 0)
    cc = lax.broadcasted_iota(jnp.int32, (tm, tm), 1)
    shift = DN_CHUNK.bit_length() - 1
    same = jnp.right_shift(rr, shift) == jnp.right_shift(cc, shift)
    return jnp.where(jnp.logical_and(same, rr >= cc), 1.0, 0.0).astype(F32)


def _halo_specs(tm, width, n_blocks_seq, n_blocks):
    per = tm // HALO
    prev = pl.BlockSpec((HALO, width), lambda i: (jnp.maximum(i * per - 1, 0), 0))
    nxt = pl.BlockSpec((HALO, width), lambda i: (jnp.minimum((i + 1) * per, n_blocks * per - 1), 0))
    return prev, nxt


def _split_heads(ref, base):
    return [ref[:, base + h * DN_HEAD_DIM: base + (h + 1) * DN_HEAD_DIM] for h in range(DN_HEADS)]


def _dn_prep_fwd(qkv, bpre, apre, conv_w, alog, dtb, seq, *, name):
    t = qkv.shape[0]
    tm = _tm(t)
    bps = seq // tm
    cw = 3 * DN_WIDTH

    def body(x_ref, halo_ref, b_ref, a_ref, w_ref, alog_ref, dtb_ref, q_ref, k_ref, v_ref, beta_ref, gc_ref):
        i = pl.program_id(0)
        keep = jnp.where(i % bps == 0, 0.0, 1.0)
        ext = jnp.concatenate([halo_ref[...] * keep, x_ref[...]], axis=0)
        y = _conv_taps(ext, w_ref[...], tm)
        yq = [y[:, h * DN_HEAD_DIM:(h + 1) * DN_HEAD_DIM] for h in range(DN_HEADS)]
        yk = [y[:, DN_WIDTH + h * DN_HEAD_DIM: DN_WIDTH + (h + 1) * DN_HEAD_DIM] for h in range(DN_HEADS)]
        q, k, v, beta, g = _post_conv(yq, yk, y[:, 2 * DN_WIDTH:], b_ref[...], a_ref[...], alog_ref[...], dtb_ref[...])
        for h in range(DN_HEADS):
            q_ref[:, h * DN_HEAD_DIM:(h + 1) * DN_HEAD_DIM] = q[h]
            k_ref[:, h * DN_HEAD_DIM:(h + 1) * DN_HEAD_DIM] = k[h]
        v_ref[...] = v
        beta_ref[...] = beta
        gc_ref[...] = mmx(_chunk_tril(tm), g)

    row = lambda i: (i, 0)
    const = lambda i: (0, 0)
    prev, _ = _halo_specs(tm, cw, bps, t // tm)
    return pl.pallas_call(
        body, name=name, grid=(t // tm,),
        in_specs=[pl.BlockSpec((tm, cw), row), prev, pl.BlockSpec((tm, LANES), row), pl.BlockSpec((tm, LANES), row),
                  pl.BlockSpec((CONV_K, cw), const), pl.BlockSpec((1, LANES), const), pl.BlockSpec((1, LANES), const)],
        out_specs=tuple(pl.BlockSpec((tm, n), row) for n in (DN_WIDTH, DN_WIDTH, DN_WIDTH, LANES, LANES)),
        out_shape=tuple(jax.ShapeDtypeStruct((t, n), F32) for n in (DN_WIDTH, DN_WIDTH, DN_WIDTH, LANES, LANES)),
        compiler_params=_cparams(("parallel",)),
    )(qkv, qkv, bpre, apre, conv_w, alog, dtb)


def _dn_prep_bwd(qkv, bpre, apre, conv_w, alog, dtb, dq, dk, dv, dbeta, dgc, dgc2, seq, *, name):
    t = qkv.shape[0]
    tm = _tm(t)
    bps = seq // tm
    cw = 3 * DN_WIDTH

    def body(x_ref, halo_ref, b_ref, a_ref, w_ref, alog_ref, dtb_ref, dq_ref, dk_ref, dv_ref, dbeta_ref, dgc_ref, dgc2_ref,
             dy_ref, db_ref, da_ref, dalog_ref, ddtb_ref):
        i = pl.program_id(0)
        keep = jnp.where(i % bps == 0, 0.0, 1.0)
        ext = jnp.concatenate([halo_ref[...] * keep, x_ref[...]], axis=0)
        y = _conv_taps(ext, w_ref[...], tm)
        yq = [y[:, h * DN_HEAD_DIM:(h + 1) * DN_HEAD_DIM] for h in range(DN_HEADS)]
        yk = [y[:, DN_WIDTH + h * DN_HEAD_DIM: DN_WIDTH + (h + 1) * DN_HEAD_DIM] for h in range(DN_HEADS)]
        _, vjp = jax.vjp(_post_conv, yq, yk, y[:, 2 * DN_WIDTH:], b_ref[...], a_ref[...], alog_ref[...], dtb_ref[...])
        dg = mmx_tn(_chunk_tril(tm), dgc_ref[...] + dgc2_ref[...])
        dyq, dyk, dyv, db, da, dalog, ddtb = vjp((_split_heads(dq_ref, 0), _split_heads(dk_ref, 0), dv_ref[...],
                                                  dbeta_ref[...], dg))
        for h in range(DN_HEADS):
            dy_ref[:, h * DN_HEAD_DIM:(h + 1) * DN_HEAD_DIM] = dyq[h]
            dy_ref[:, DN_WIDTH + h * DN_HEAD_DIM: DN_WIDTH + (h + 1) * DN_HEAD_DIM] = dyk[h]
        dy_ref[:, 2 * DN_WIDTH:] = dyv
        db_ref[...] = db
        da_ref[...] = da
        _acc_out(dalog_ref, i == 0, dalog)
        _acc_out(ddtb_ref, i == 0, ddtb)

    row = lambda i: (i, 0)
    const = lambda i: (0, 0)
    prev, _ = _halo_specs(tm, cw, bps, t // tm)
    return pl.pallas_call(
        body, name=name, grid=(t // tm,),
        in_specs=[pl.BlockSpec((tm, cw), row), prev, pl.BlockSpec((tm, LANES), row), pl.BlockSpec((tm, LANES), row),
                  pl.BlockSpec((CONV_K, cw), const), pl.BlockSpec((1, LANES), const), pl.BlockSpec((1, LANES), const),
                  pl.BlockSpec((tm, DN_WIDTH), row), pl.BlockSpec((tm, DN_WIDTH), row), pl.BlockSpec((tm, DN_WIDTH), row),
                  pl.BlockSpec((tm, LANES), row), pl.BlockSpec((tm, LANES), row), pl.BlockSpec((tm, LANES), row)],
        out_specs=(pl.BlockSpec((tm, cw), row), pl.BlockSpec((tm, LANES), row), pl.BlockSpec((tm, LANES), row),
                   pl.BlockSpec((1, LANES), const), pl.BlockSpec((1, LANES), const)),
        out_shape=(jax.ShapeDtypeStruct((t, cw), F32), jax.ShapeDtypeStruct((t, LANES), F32), jax.ShapeDtypeStruct((t, LANES), F32),
                   jax.ShapeDtypeStruct((1, LANES), F32), jax.ShapeDtypeStruct((1, LANES), F32)),
        compiler_params=_cparams(("arbitrary",)),
    )(qkv, qkv, bpre, apre, conv_w, alog, dtb, dq, dk, dv, dbeta, dgc, dgc2)


def _conv_bwd(qkv, dy, conv_w, seq, *, name):
    t = qkv.shape[0]
    tm = _tm(t)
    bps = seq // tm
    cw = 3 * DN_WIDTH
    n_ext = tm + HALO

    def body(x_ref, halo_ref, dy_ref, dyn_ref, w_ref, dx_ref, dw_ref):
        i = pl.program_id(0)
        keep_prev = jnp.where(i % bps == 0, 0.0, 1.0)
        keep_next = jnp.where(i % bps == bps - 1, 0.0, 1.0)
        ext = jnp.concatenate([halo_ref[...] * keep_prev, x_ref[...]], axis=0)
        dy = dy_ref[...]
        dyext = jnp.concatenate([dy, dyn_ref[...] * keep_next], axis=0)
        w = w_ref[...]

        @pl.when(i == 0)
        def _():
            dw_ref[...] = jnp.zeros_like(dw_ref)

        dx = None
        for j in range(CONV_K):
            s = CONV_K - 1 - j
            fut = dyext if s == 0 else pltpu.roll(dyext, n_ext - s, 0)
            term = w[j:j + 1, :] * fut[0:tm, :]
            dx = term if dx is None else dx + term
            past = ext if s == 0 else pltpu.roll(ext, s, 0)
            dw_ref[j:j + 1, :] += jnp.sum(dy * past[HALO:HALO + tm, :], axis=0, keepdims=True)
        dx_ref[...] = dx

    row = lambda i: (i, 0)
    const = lambda i: (0, 0)
    prev, nxt = _halo_specs(tm, cw, bps, t // tm)
    return pl.pallas_call(
        body, name=name, grid=(t // tm,),
        in_specs=[pl.BlockSpec((tm, cw), row), prev, pl.BlockSpec((tm, cw), row), nxt, pl.BlockSpec((CONV_K, cw), const)],
        out_specs=(pl.BlockSpec((tm, cw), row), pl.BlockSpec((HALO, cw), const)),
        out_shape=(jax.ShapeDtypeStruct((t, cw), F32), jax.ShapeDtypeStruct((HALO, cw), F32)),
        compiler_params=_cparams(("arbitrary",)),
    )(qkv, qkv, dy, dy, conv_w)


def _inv_unit_lower(l_mats, eye):
    invs = [eye - l for l in l_mats]
    powers = list(l_mats)
    n = 2
    while n < eye.shape[0]:
        powers = [mmh(p, p) for p in powers]
        invs = [inv + mmh(inv, p) for inv, p in zip(invs, powers)]
        n *= 2
    return invs


@jax.custom_vjp
def _solve(l_mat, rhs, inv):
    return mmx(inv, rhs)


def _solve_fwd(l_mat, rhs, inv):
    sol = mmx(inv, rhs)
    return sol, (inv, sol)


def _solve_bwd(res, d_sol):
    inv, sol = res
    d_rhs = mmx_tn(inv, d_sol)
    return -mmx_nt(d_rhs, sol), d_rhs, jnp.zeros_like(inv)


_solve.defvjp(_solve_fwd, _solve_bwd)


def _prep_fn(q, k, v, gc, gr, b, inv):
    ids = range(len(q))
    c = q[0].shape[0]
    rr = lax.broadcasted_iota(jnp.int32, (c, c), 0)
    cc = lax.broadcasted_iota(jnp.int32, (c, c), 1)
    incl, strict = rr >= cc, rr > cc
    is_last = lax.broadcasted_iota(jnp.int32, (c, 1), 0) == c - 1
    qs = [q[i] * (DN_HEAD_DIM ** -0.5) for i in ids]
    decay = [jnp.where(incl, jnp.exp(jnp.where(incl, gc[i] - gr[i], 0.0)), 0.0) for i in ids]
    kb = [k[i] * b[i] for i in ids]
    vb = [v[i] * b[i] for i in ids]
    kk = [mm_nt(kb[i], k[i]) for i in ids]
    l_mat = [jnp.where(strict, kk[i] * decay[i], 0.0) for i in ids]
    eg = [jnp.exp(gc[i]) for i in ids]
    if inv is None:
        inv = _inv_unit_lower(l_mat, jnp.where(rr == cc, 1.0, 0.0).astype(F32))
    u_wy = [_solve(l_mat[i], vb[i], inv[i]) for i in ids]
    w_wy = [_solve(l_mat[i], kb[i] * eg[i], inv[i]) for i in ids]
    qk = [mm_nt(qs[i], k[i]) * decay[i] for i in ids]
    g_last = [jnp.sum(jnp.where(is_last, gc[i], 0.0), axis=0, keepdims=True) for i in ids]
    k_dec = [k[i] * jnp.exp(g_last[i] - gc[i]) for i in ids]
    egl = [jnp.broadcast_to(jnp.exp(g_last[i]), (1, LANES)) for i in ids]
    return [(w_wy[i], u_wy[i], qs[i] * eg[i], k_dec[i], qk[i], egl[i]) for i in ids], inv


def _seq_fn(w, u, qd, kd, qk, egl, s):
    ids = range(len(w))
    ws = [mm(w[i], s[i]) for i in ids]
    qs = [mm(qd[i], s[i]) for i in ids]
    v_new = [u[i] - ws[i] for i in ids]
    o = [qs[i] + mm(qk[i], v_new[i]) for i in ids]
    s_new = [s[i] * egl[i] + mm_tn(kd[i], v_new[i]) for i in ids]
    return o, s_new


def _lane_col(a, h):
    lane = lax.broadcasted_iota(jnp.int32, (1, LANES), 1)
    return jnp.sum(jnp.where(lane == h, a, 0.0), axis=1, keepdims=True)


def _col_lane(col, h):
    lane = lax.broadcasted_iota(jnp.int32, (1, LANES), 1)
    return jnp.where(lane == h, col, 0.0)


def _head_cols(h):
    return slice(h * DN_HEAD_DIM, (h + 1) * DN_HEAD_DIM)


def _chunk_rows(n):
    return pl.ds(pl.multiple_of(n * DN_CHUNK, DN_CHUNK), DN_CHUNK)


def _delta_prep(q, k, v, gc, grow, beta, *, name):
    t = q.shape[0]
    tm = _tm(t)
    cpb = tm // DN_CHUNK
    n_chunks = t // DN_CHUNK
    group = 2

    def body(q_ref, k_ref, v_ref, gc_ref, gr_ref, b_ref, w_ref, u_ref, qd_ref, kd_ref, qk_ref, egl_ref, inv_ref):
        def step(m, carry):
            probs = [(m * group + e, h) for e in range(group) for h in range(DN_HEADS)]
            gcb = [gc_ref[_chunk_rows(m * group + e), :] for e in range(group)]
            bb = [b_ref[_chunk_rows(m * group + e), :] for e in range(group)]
            grb = [gr_ref[m * group + e] for e in range(group)]
            for e in range(group):
                egl_ref[m * group + e] = jnp.zeros((HALO, LANES), F32)
            outs, invs = _prep_fn(
                [q_ref[_chunk_rows(n), _head_cols(h)] for n, h in probs], [k_ref[_chunk_rows(n), _head_cols(h)] for n, h in probs],
                [v_ref[_chunk_rows(n), _head_cols(h)] for n, h in probs],
                [_lane_col(gcb[e], h) for e in range(group) for h in range(DN_HEADS)],
                [grb[e][h:h + 1, :] for e in range(group) for h in range(DN_HEADS)],
                [_lane_col(bb[e], h) for e in range(group) for h in range(DN_HEADS)], None)
            for (n, h), (w, u, qd, kd, qk, egl), inv in zip(probs, outs, invs):
                rows, cols = _chunk_rows(n), _head_cols(h)
                w_ref[rows, cols] = w.astype(BF16)
                u_ref[rows, cols] = u
                qd_ref[rows, cols] = qd.astype(BF16)
                kd_ref[rows, cols] = kd.astype(BF16)
                qk_ref[n, h] = qk
                inv_ref[n, h] = inv
                egl_ref[n, h:h + 1, :] = egl
            return carry

        lax.fori_loop(0, cpb // group, step, 0)

    row = lambda i: (i, 0)
    tok = pl.BlockSpec((tm, DN_WIDTH), row)
    lanes = pl.BlockSpec((tm, LANES), row)
    sq = pl.BlockSpec((cpb, DN_HEADS, DN_CHUNK, DN_CHUNK), lambda i: (i, 0, 0, 0))
    return pl.pallas_call(
        body, name=name, grid=(t // tm,),
        in_specs=[tok, tok, tok, lanes, pl.BlockSpec((cpb, HALO, DN_CHUNK), lambda i: (i, 0, 0)), lanes],
        out_specs=(tok, tok, tok, tok, sq, pl.BlockSpec((cpb, HALO, LANES), lambda i: (i, 0, 0)), sq),
        out_shape=(jax.ShapeDtypeStruct((t, DN_WIDTH), BF16), jax.ShapeDtypeStruct((t, DN_WIDTH), F32),
                   jax.ShapeDtypeStruct((t, DN_WIDTH), BF16), jax.ShapeDtypeStruct((t, DN_WIDTH), BF16),
                   jax.ShapeDtypeStruct((n_chunks, DN_HEADS, DN_CHUNK, DN_CHUNK), F32),
                   jax.ShapeDtypeStruct((n_chunks, HALO, LANES), F32),
                   jax.ShapeDtypeStruct((n_chunks, DN_HEADS, DN_CHUNK, DN_CHUNK), F32)),
        compiler_params=_cparams(("parallel",)),
    )(q, k, v, gc, grow, beta)


def _delta_par_bwd(q, k, v, gc, grow, beta, inv, dw, du, dqd, dkd, dqk, degl, *, name):
    t = q.shape[0]
    tm = _tm(t)
    cpb = tm // DN_CHUNK
    n_chunks = t // DN_CHUNK
    group = 2

    def body(q_ref, k_ref, v_ref, gc_ref, gr_ref, b_ref, inv_ref, dw_ref, du_ref, dqd_ref, dkd_ref, dqk_ref, degl_ref,
             dq_ref, dk_ref, dv_ref, dgc_ref, dgr_ref, db_ref):
        def step(m, carry):
            chunks = [m * group + e for e in range(group)]
            probs = [(e, h) for e in range(group) for h in range(DN_HEADS)]
            rows = [_chunk_rows(n) for n in chunks]
            gcb, bb = [gc_ref[r, :] for r in rows], [b_ref[r, :] for r in rows]
            grb, deglb = [gr_ref[n] for n in chunks], [degl_ref[n] for n in chunks]
            for n in chunks:
                dgr_ref[n] = jnp.zeros((HALO, DN_CHUNK), F32)
            invs = [inv_ref[chunks[e], h] for e, h in probs]
            _, vjp = jax.vjp(lambda *a: _prep_fn(*a, invs)[0],
                             [q_ref[rows[e], _head_cols(h)] for e, h in probs], [k_ref[rows[e], _head_cols(h)] for e, h in probs],
                             [v_ref[rows[e], _head_cols(h)] for e, h in probs], [_lane_col(gcb[e], h) for e, h in probs],
                             [grb[e][h:h + 1, :] for e, h in probs], [_lane_col(bb[e], h) for e, h in probs])
            dq, dk, dv, dgc, dgr, db = vjp([(dw_ref[rows[e], _head_cols(h)], du_ref[rows[e], _head_cols(h)],
                                             dqd_ref[rows[e], _head_cols(h)], dkd_ref[rows[e], _head_cols(h)],
                                             dqk_ref[chunks[e], h], deglb[e][h:h + 1, :]) for e, h in probs])
            dgc_acc = [jnp.zeros((DN_CHUNK, LANES), F32) for _ in chunks]
            db_acc = [jnp.zeros((DN_CHUNK, LANES), F32) for _ in chunks]
            for i, (e, h) in enumerate(probs):
                cols = _head_cols(h)
                dq_ref[rows[e], cols] = dq[i]
                dk_ref[rows[e], cols] = dk[i]
                dv_ref[rows[e], cols] = dv[i]
                dgr_ref[chunks[e], h:h + 1, :] = dgr[i]
                dgc_acc[e] = dgc_acc[e] + _col_lane(dgc[i], h)
                db_acc[e] = db_acc[e] + _col_lane(db[i], h)
            for e in range(group):
                dgc_ref[rows[e], :] = dgc_acc[e]
                db_ref[rows[e], :] = db_acc[e]
            return carry

        lax.fori_loop(0, cpb // group, step, 0)

    row = lambda i: (i, 0)
    tok = pl.BlockSpec((tm, DN_WIDTH), row)
    lanes = pl.BlockSpec((tm, LANES), row)
    sq = pl.BlockSpec((cpb, DN_HEADS, DN_CHUNK, DN_CHUNK), lambda i: (i, 0, 0, 0))
    grs = pl.BlockSpec((cpb, HALO, DN_CHUNK), lambda i: (i, 0, 0))
    return pl.pallas_call(
        body, name=name, grid=(t // tm,),
        in_specs=[tok, tok, tok, lanes, grs, lanes, sq, tok, tok, tok, tok, sq, pl.BlockSpec((cpb, HALO, LANES), lambda i: (i, 0, 0))],
        out_specs=(tok, tok, tok, lanes, grs, lanes),
        out_shape=(jax.ShapeDtypeStruct((t, DN_WIDTH), F32),) * 3
        + (jax.ShapeDtypeStruct((t, LANES), F32), jax.ShapeDtypeStruct((n_chunks, HALO, DN_CHUNK), F32),
           jax.ShapeDtypeStruct((t, LANES), F32)),
        compiler_params=_cparams(("parallel",)),
    )(q, k, v, gc, grow, beta, inv, dw, du, dqd, dkd, dqk, degl)


def _seq_specs(n_seq, seq, reverse):
    tm = _tm(seq)
    nb = seq // tm
    cpb = tm // DN_CHUNK
    blk = (lambda b, j: b * nb + nb - 1 - j) if reverse else (lambda b, j: b * nb + j)
    tok = pl.BlockSpec((tm, DN_WIDTH), lambda b, j: (blk(b, j), 0))
    sq = pl.BlockSpec((cpb, DN_HEADS, DN_CHUNK, DN_CHUNK), lambda b, j: (blk(b, j), 0, 0, 0))
    rows8 = pl.BlockSpec((cpb, HALO, LANES), lambda b, j: (blk(b, j), 0, 0))
    state = pl.BlockSpec((cpb, DN_HEADS, DN_HEAD_DIM, DN_HEAD_DIM), lambda b, j: (blk(b, j), 0, 0, 0))
    return nb, cpb, tok, sq, rows8, state


def _delta_seq_fwd(w, u, qd, kd, qk, egl, n_seq, seq, *, name):
    nb, cpb, tok, sq, rows8, state = _seq_specs(n_seq, seq, False)
    t = n_seq * seq

    def body(w_ref, u_ref, qd_ref, kd_ref, qk_ref, egl_ref, o_ref, st_ref, s_s):
        @pl.when(pl.program_id(1) == 0)
        def _():
            s_s[...] = jnp.zeros_like(s_s)

        def step(n, carry):
            rows = _chunk_rows(n)
            heads = range(DN_HEADS)
            eglb = egl_ref[n]
            s = [s_s[h] for h in heads]
            for h in heads:
                st_ref[n, h] = s[h]
            o, s_new = _seq_fn([w_ref[rows, _head_cols(h)] for h in heads], [u_ref[rows, _head_cols(h)] for h in heads],
                               [qd_ref[rows, _head_cols(h)] for h in heads], [kd_ref[rows, _head_cols(h)] for h in heads],
                               [qk_ref[n, h] for h in heads], [eglb[h:h + 1, :] for h in heads], s)
            for h in heads:
                o_ref[rows, _head_cols(h)] = o[h]
                s_s[h] = s_new[h]
            return carry

        lax.fori_loop(0, cpb, step, 0)

    return pl.pallas_call(
        body, name=name, grid=(n_seq, nb),
        in_specs=[tok, tok, tok, tok, sq, rows8],
        out_specs=(tok, state),
        out_shape=(jax.ShapeDtypeStruct((t, DN_WIDTH), F32),
                   jax.ShapeDtypeStruct((t // DN_CHUNK, DN_HEADS, DN_HEAD_DIM, DN_HEAD_DIM), F32)),
        scratch_shapes=[pltpu.VMEM((DN_HEADS, DN_HEAD_DIM, DN_HEAD_DIM), F32)],
        compiler_params=_cparams(("parallel", "arbitrary")),
    )(w, u, qd, kd, qk, egl)


def _delta_seq_bwd(w, u, qd, kd, qk, egl, states, do, n_seq, seq, *, name):
    nb, cpb, tok, sq, rows8, state = _seq_specs(n_seq, seq, True)
    t = n_seq * seq

    def body(w_ref, u_ref, qd_ref, kd_ref, qk_ref, egl_ref, st_ref, do_ref, dw_ref, du_ref, dqd_ref, dkd_ref, dqk_ref,
             degl_ref, ds_s):
        @pl.when(pl.program_id(1) == 0)
        def _():
            ds_s[...] = jnp.zeros_like(ds_s)

        def step(m, carry):
            n = cpb - 1 - m
            rows = _chunk_rows(n)
            eglb = egl_ref[n]
            degl_ref[n] = jnp.zeros((HALO, LANES), F32)
            heads = range(DN_HEADS)
            _, vjp = jax.vjp(_seq_fn, [w_ref[rows, _head_cols(h)].astype(F32) for h in heads],
                             [u_ref[rows, _head_cols(h)] for h in heads],
                             [qd_ref[rows, _head_cols(h)].astype(F32) for h in heads],
                             [kd_ref[rows, _head_cols(h)].astype(F32) for h in heads],
                             [qk_ref[n, h] for h in heads], [eglb[h:h + 1, :] for h in heads], [st_ref[n, h] for h in heads])
            dw, du, dqd, dkd, dqk, degl, ds_in = vjp(([do_ref[rows, _head_cols(h)] for h in heads], [ds_s[h] for h in heads]))
            for h in heads:
                cols = _head_cols(h)
                dw_ref[rows, cols] = dw[h]
                du_ref[rows, cols] = du[h]
                dqd_ref[rows, cols] = dqd[h]
                dkd_ref[rows, cols] = dkd[h]
                dqk_ref[n, h] = dqk[h]
                degl_ref[n, h:h + 1, :] = degl[h]
                ds_s[h] = ds_in[h]
            return carry

        lax.fori_loop(0, cpb, step, 0)

    return pl.pallas_call(
        body, name=name, grid=(n_seq, nb),
        in_specs=[tok, tok, tok, tok, sq, rows8, state, tok],
        out_specs=(tok, tok, tok, tok, sq, rows8),
        out_shape=(jax.ShapeDtypeStruct((t, DN_WIDTH), F32),) * 4
        + (jax.ShapeDtypeStruct((t // DN_CHUNK, DN_HEADS, DN_CHUNK, DN_CHUNK), F32),
           jax.ShapeDtypeStruct((t // DN_CHUNK, HALO, LANES), F32)),
        scratch_shapes=[pltpu.VMEM((DN_HEADS, DN_HEAD_DIM, DN_HEAD_DIM), F32)],
        compiler_params=_cparams(("parallel", "arbitrary")),
    )(w, u, qd, kd, qk, egl, states, do)


def _dn_gate(o, z, dnw):
    return o * lax.rsqrt(jnp.mean(o * o, axis=-1, keepdims=True) + EPS) * dnw * _silu(z)


def _mix_out_fwd(x, sg, o, z, wo_sg, wo_dn, dnw, *, name):
    t = x.shape[0]
    tm = _tm(t)

    def body(x_ref, sg_ref, o_ref, z_ref, wsg_ref, wdn_ref, dnw_ref, y_ref, dn_s):
        for h, (oh, zh) in enumerate(zip(_split_heads(o_ref, 0), _split_heads(z_ref, 0))):
            dn_s[:, h * DN_HEAD_DIM:(h + 1) * DN_HEAD_DIM] = _dn_gate(oh, zh, dnw_ref[...]).astype(BF16)
        y_ref[...] = (x_ref[...] + jnp.dot(sg_ref[...].astype(BF16), wsg_ref[...], preferred_element_type=F32)
                      + jnp.dot(dn_s[...], wdn_ref[...], preferred_element_type=F32))

    row = lambda i: (i, 0)
    const = lambda i: (0, 0)
    half = pl.BlockSpec((tm, DN_WIDTH), row)
    return pl.pallas_call(
        body, name=name, grid=(t // tm,),
        in_specs=[pl.BlockSpec((tm, D_MODEL), row), half, half, half, pl.BlockSpec((SG_WIDTH, D_MODEL), const),
                  pl.BlockSpec((DN_WIDTH, D_MODEL), const), pl.BlockSpec((1, DN_HEAD_DIM), const)],
        out_specs=pl.BlockSpec((tm, D_MODEL), row),
        out_shape=jax.ShapeDtypeStruct((t, D_MODEL), F32),
        scratch_shapes=[pltpu.VMEM((tm, DN_WIDTH), BF16)],
        compiler_params=_cparams(("parallel",)),
    )(x, sg, o, z, wo_sg, wo_dn, dnw)


def _mix_out_bwd(dy, sg, o, z, wo_sg, wo_dn, dnw, *, name):
    t = dy.shape[0]
    tm = _tm(t)

    def body(dy_ref, sg_ref, o_ref, z_ref, wsg_ref, wdn_ref, dnw_ref, dsg_ref, do_ref, dz_ref, dwsg_ref, dwdn_ref, ddnw_ref, dn_s):
        i = pl.program_id(0)
        dyb = dy_ref[...].astype(BF16)
        nt = (((1,), (1,)), ((), ()))
        tn = (((0,), (0,)), ((), ()))
        dsg_ref[...] = lax.dot_general(dyb, wsg_ref[...], nt, preferred_element_type=F32)
        ddn = lax.dot_general(dyb, wdn_ref[...], nt, preferred_element_type=F32)
        ddnw = None
        for h, (oh, zh) in enumerate(zip(_split_heads(o_ref, 0), _split_heads(z_ref, 0))):
            cols = slice(h * DN_HEAD_DIM, (h + 1) * DN_HEAD_DIM)
            out, vjp = jax.vjp(_dn_gate, oh, zh, dnw_ref[...])
            dn_s[:, cols] = out.astype(BF16)
            doh, dzh, dw = vjp(ddn[:, cols])
            do_ref[:, cols] = doh
            dz_ref[:, cols] = dzh
            ddnw = dw if ddnw is None else ddnw + dw
        _acc_out(ddnw_ref, i == 0, ddnw)
        _acc_out(dwsg_ref, i == 0, lax.dot_general(sg_ref[...].astype(BF16), dyb, tn, preferred_element_type=F32))
        _acc_out(dwdn_ref, i == 0, lax.dot_general(dn_s[...], dyb, tn, preferred_element_type=F32))

    row = lambda i: (i, 0)
    const = lambda i: (0, 0)
    half = pl.BlockSpec((tm, DN_WIDTH), row)
    wspec = pl.BlockSpec((DN_WIDTH, D_MODEL), const)
    return pl.pallas_call(
        body, name=name, grid=(t // tm,),
        in_specs=[pl.BlockSpec((tm, D_MODEL), row), half, half, half, wspec, wspec, pl.BlockSpec((1, DN_HEAD_DIM), const)],
        out_specs=(half, half, half, wspec, wspec, pl.BlockSpec((1, DN_HEAD_DIM), const)),
        out_shape=(jax.ShapeDtypeStruct((t, DN_WIDTH), F32),) * 3 + (jax.ShapeDtypeStruct((DN_WIDTH, D_MODEL), F32),) * 2
        + (jax.ShapeDtypeStruct((1, DN_HEAD_DIM), F32),),
        scratch_shapes=[pltpu.VMEM((tm, DN_WIDTH), BF16)],
        compiler_params=_cparams(("arbitrary",)),
    )(dy, sg, o, z, wo_sg, wo_dn, dnw)


_MESH = pl.DeviceIdType.MESH
_HBM = pl.BlockSpec(memory_space=pl.ANY)


def _mesh_pos():
    x, y, c = lax.axis_index("x"), lax.axis_index("y"), lax.axis_index("c")
    return x, y, c, [(1 - x, y), (x, 1 - y), (1 - x, 1 - y)]


def _gather2(arrs, *, name):
    n = len(arrs)
    slots = N_DEV - 1

    def body(*refs):
        in_refs, out_refs = refs[:n], refs[n:2 * n]
        send_sems, recv_sems, local_sems = refs[2 * n:]
        x, y, c, chips = _mesh_pos()
        me, sibling = (x, y, c), (x, y, 1 - c)

        def copy(k, slot, block, to, src=None):
            dst = out_refs[k].at[4 * block[0] + 2 * block[1] + block[2]]
            return pltpu.make_async_remote_copy(src_ref=dst if src is None else src, dst_ref=dst,
                                                send_sem=send_sems.at[k * slots + slot], recv_sem=recv_sems.at[k * slots + slot],
                                                device_id=to, device_id_type=_MESH)

        local = [pltpu.make_async_copy(in_refs[k], out_refs[k].at[4 * x + 2 * y + c], local_sems.at[k]) for k in range(n)]
        sent = []
        for k in range(n):
            sent.append(copy(k, 0, me, sibling, src=in_refs[k]))
            sent += [copy(k, 1 + j, me, (*chip, c), src=in_refs[k]) for j, chip in enumerate(chips)]
        for cp in local + sent:
            cp.start()
        for j, chip in enumerate(chips):
            for k in range(n):
                copy(k, 1 + j, (*chip, c), me).wait_recv()
                passed = copy(k, 4 + j, (*chip, c), sibling)
                passed.start()
                sent.append(passed)
        for k in range(n):
            copy(k, 0, sibling, me).wait_recv()
            for j, chip in enumerate(chips):
                copy(k, 4 + j, (*chip, 1 - c), me).wait_recv()
        for cp in sent:
            cp.wait_send()
        for cp in local:
            cp.wait()

    return pl.pallas_call(
        body, name=name, in_specs=[_HBM] * n, out_specs=(_HBM,) * n,
        out_shape=tuple(jax.ShapeDtypeStruct((N_DEV,) + a.shape, a.dtype) for a in arrs),
        scratch_shapes=[pltpu.SemaphoreType.DMA((n * slots,)), pltpu.SemaphoreType.DMA((n * slots,)),
                        pltpu.SemaphoreType.DMA((n,))],
    )(*arrs)


def _pair_swap(arrs, *, name):
    n = len(arrs)

    def body(*refs):
        in_refs, out_refs, send_sems, recv_sems = refs[:n], refs[n:2 * n], refs[2 * n], refs[2 * n + 1]
        x, y, c, _ = _mesh_pos()
        copies = [pltpu.make_async_remote_copy(src_ref=in_refs[k].at[1 - c], dst_ref=out_refs[k], send_sem=send_sems.at[k],
                                               recv_sem=recv_sems.at[k], device_id=(x, y, 1 - c), device_id_type=_MESH)
                  for k in range(n)]
        for cp in copies:
            cp.start()
        for cp in copies:
            cp.wait()

    return pl.pallas_call(
        body, name=name, in_specs=[_HBM] * n, out_specs=(_HBM,) * n,
        out_shape=tuple(jax.ShapeDtypeStruct(a.shape[1:], a.dtype) for a in arrs),
        scratch_shapes=[pltpu.SemaphoreType.DMA((n,)), pltpu.SemaphoreType.DMA((n,))],
    )(*arrs)


def _chip_exchange(arrs, *, name):
    n = len(arrs)
    slots = 3

    def body(*refs):
        in_refs, out_refs = refs[:n], refs[n:2 * n]
        send_sems, recv_sems, local_sems = refs[2 * n:]
        x, y, c, chips = _mesh_pos()
        mine = 2 * x + y
        copies = [pltpu.make_async_copy(in_refs[k].at[mine], out_refs[k].at[mine], local_sems.at[k]) for k in range(n)]
        for j, chip in enumerate(chips):
            for k in range(n):
                copies.append(pltpu.make_async_remote_copy(
                    src_ref=in_refs[k].at[2 * chip[0] + chip[1]], dst_ref=out_refs[k].at[mine],
                    send_sem=send_sems.at[k * slots + j], recv_sem=recv_sems.at[k * slots + j],
                    device_id=(*chip, c), device_id_type=_MESH))
        for cp in copies:
            cp.start()
        for cp in copies:
            cp.wait()

    return pl.pallas_call(
        body, name=name, in_specs=[_HBM] * n, out_specs=(_HBM,) * n,
        out_shape=tuple(jax.ShapeDtypeStruct(a.shape, a.dtype) for a in arrs),
        scratch_shapes=[pltpu.SemaphoreType.DMA((n * slots,)), pltpu.SemaphoreType.DMA((n * slots,)),
                        pltpu.SemaphoreType.DMA((n,))],
    )(*arrs)


_SEM = pl.BlockSpec(memory_space=pltpu.SEMAPHORE)
_EFFECT = pltpu.SideEffectType.DATAFLOW_SIDE_EFFECTING


def _direct_copies(src_refs, land_refs, send_sems, recv_sems, gather):
    x, y, c, _ = _mesh_pos()
    me = 4 * x + 2 * y + c
    n_peer = N_DEV - 1
    copies = []
    for r in range(1, N_DEV):
        px = 1 - x if r & 4 else x
        py = 1 - y if r & 2 else y
        pc = 1 - c if r & 1 else c
        for k, (src, land) in enumerate(zip(src_refs, land_refs)):
            copies.append(pltpu.make_async_remote_copy(
                src_ref=src if gather else src.at[4 * px + 2 * py + pc], dst_ref=land.at[me],
                send_sem=send_sems.at[k * n_peer + r - 1], recv_sem=recv_sems.at[k * n_peer + r - 1],
                device_id=(px, py, pc), device_id_type=_MESH))
    return copies


def _send_start(arrs, gather, *, name):
    n = len(arrs)
    lands = [lax.empty(((N_DEV,) + a.shape) if gather else a.shape, a.dtype) for a in arrs]

    def body(*refs):
        src_refs, land_refs, send_sems, recv_sems, token = refs[:n], refs[n:2 * n], refs[2 * n], refs[2 * n + 1], refs[-1]
        for cp in _direct_copies(src_refs, land_refs, send_sems, recv_sems, gather):
            cp.start()
        token[...] = jnp.zeros_like(token)

    n_sem = n * (N_DEV - 1)
    bufs = list(arrs) + lands
    out = pl.pallas_call(
        body, name=name,
        out_shape=(pltpu.SemaphoreType.DMA((n_sem,)), pltpu.SemaphoreType.DMA((n_sem,)))
        + tuple(pltpu.HBM(b.shape, b.dtype) for b in bufs) + (jax.ShapeDtypeStruct((HALO, LANES), F32),),
        in_specs=[_HBM] * (2 * n), out_specs=(_SEM, _SEM) + (_HBM,) * (2 * n) + (pl.BlockSpec(memory_space=pltpu.VMEM),),
        input_output_aliases={i: 2 + i for i in range(2 * n)},
        compiler_params=pltpu.CompilerParams(has_side_effects=_EFFECT),
    )(*[pltpu.with_memory_space_constraint(b, pltpu.HBM) for b in bufs])
    return (out[0], out[1], list(out[2:2 + n]), list(out[2 + n:2 + 2 * n])), out[-1]


def _send_wait(started, gather, after, *, name):
    send_sems, recv_sems, srcs, lands = started
    n = len(srcs)

    def body(*refs):
        src_refs, land_refs, send_ref, recv_ref = refs[:n], refs[n:2 * n], refs[2 * n], refs[2 * n + 1]
        for cp in _direct_copies(src_refs, land_refs, send_ref, recv_ref, gather):
            cp.wait_send()
            cp.wait_recv()

    bufs = srcs + lands
    out = pl.pallas_call(
        body, name=name, out_shape=tuple(pltpu.HBM(b.shape, b.dtype) for b in bufs),
        in_specs=[_HBM] * (2 * n) + [_SEM, _SEM, _HBM], out_specs=(_HBM,) * (2 * n),
        input_output_aliases={i: i for i in range(2 * n)},
        compiler_params=pltpu.CompilerParams(has_side_effects=_EFFECT),
    )(*bufs, send_sems, recv_sems, after)
    return list(out[n:])


def _pair_add(p, r, core, *, name):
    _, n_chip, rows, cols = p.shape
    rb = _row_block(rows)

    def body(core_ref, p_ref, r_ref, o_ref):
        o_ref[...] = (p_ref[...] + r_ref[...]).astype(BF16)

    return pl.pallas_call(
        body, name=name,
        grid_spec=pltpu.PrefetchScalarGridSpec(
            num_scalar_prefetch=1, grid=(n_chip, rows // rb),
            in_specs=[pl.BlockSpec((None, None, rb, cols), lambda s, i, core_ref: (core_ref[0], s, i, 0)),
                      pl.BlockSpec((None, rb, cols), lambda s, i, core_ref: (s, i, 0))],
            out_specs=pl.BlockSpec((None, rb, cols), lambda s, i, core_ref: (s, i, 0))),
        out_shape=jax.ShapeDtypeStruct((n_chip, rows, cols), BF16),
        compiler_params=_cparams(("parallel", "parallel")),
    )(core, p, r)


def _row_block(rows, limit=256):
    best = rows
    for cand in range(8, limit + 1, 8):
        if rows % cand == 0:
            best = cand
    return best if rows > limit else rows


def _adam(gp, w, m, v, *, name):
    p, rows, cols = gp.shape
    rb = _row_block(rows)

    def body(gp_ref, w_ref, m_ref, v_ref, g_ref, d_ref, m2_ref, v2_ref):
        g = gp_ref[0].astype(F32)
        for s in range(1, p):
            g = g + gp_ref[s].astype(F32)
        m2 = ADAM_B1 * m_ref[...] + (1.0 - ADAM_B1) * g
        v2 = ADAM_B2 * v_ref[...] + (1.0 - ADAM_B2) * (g * g)
        m_hat = m2 / (1.0 - ADAM_B1 ** ADAM_STEP)
        v_hat = v2 / (1.0 - ADAM_B2 ** ADAM_STEP)
        g_ref[...] = g
        d_ref[...] = -ADAM_LR * (m_hat / (jnp.sqrt(v_hat) + ADAM_EPS) + ADAM_WD * w_ref[...])
        m2_ref[...] = m2
        v2_ref[...] = v2

    blk = pl.BlockSpec((rb, cols), lambda i: (i, 0))
    return pl.pallas_call(
        body, name=name, grid=(rows // rb,),
        in_specs=[pl.BlockSpec((p, rb, cols), lambda i: (0, i, 0)), blk, blk, blk],
        out_specs=(blk,) * 4, out_shape=(jax.ShapeDtypeStruct((rows, cols), F32),) * 4,
        compiler_params=_cparams(("parallel",)),
    )(gp, w, m, v)


def _cols_full(g):
    return jnp.transpose(g, (1, 0, 2)).reshape(g.shape[1], N_DEV * g.shape[2])


def _cols_pieces(full):
    r, c = full.shape
    return jnp.transpose(full.reshape(r, N_DEV, c // N_DEV), (1, 0, 2))


def _pad_lanes(a, width=LANES):
    return jnp.pad(a, ((0, 0), (0, width - a.shape[1])))


def _chunk_rows_of(a):
    by_chunk = jnp.transpose(a[:, :DN_HEADS].reshape(-1, DN_CHUNK, DN_HEADS), (0, 2, 1))
    return jnp.pad(by_chunk, ((0, 0), (0, HALO - DN_HEADS), (0, 0)))


_SMALL = (("ffn1_norm", D_MODEL), ("mix_norm", D_MODEL), ("ffn2_norm", D_MODEL), ("final_norm", D_MODEL), ("a_log", DN_HEADS),
          ("dt_bias", DN_HEADS), ("dn_norm", DN_HEAD_DIM), ("sg_ln_g", SG_WIDTH), ("sg_ln_b", SG_WIDTH),
          ("sg_w", SG_GROUPS * SG_CHUNK * SG_CHUNK), ("sg_b", SG_GROUPS * SG_CHUNK), ("conv_w", CONV_K * 3 * DN_WIDTH))
_SMALL_ROWS = 1128
_SMALL_SHAPES = {"ffn1_norm": (1, D_MODEL), "mix_norm": (1, D_MODEL), "ffn2_norm": (1, D_MODEL), "final_norm": (D_MODEL,),
                 "a_log": (1, DN_HEADS), "dt_bias": (1, DN_HEADS), "dn_norm": (1, DN_HEAD_DIM), "sg_ln_g": (1, SG_WIDTH),
                 "sg_ln_b": (1, SG_WIDTH), "sg_w": (1, SG_GROUPS, SG_CHUNK, SG_CHUNK), "sg_b": (1, SG_GROUPS, SG_CHUNK)}


def _pack_small(d):
    flat = jnp.concatenate([d[name].reshape(-1) for name, _ in _SMALL])
    return jnp.pad(flat, (0, _SMALL_ROWS * LANES - flat.shape[0])).reshape(_SMALL_ROWS, LANES)


def _unpack_small(a):
    flat, out, at = a.reshape(-1), {}, 0
    for name, size in _SMALL:
        out[name] = flat[at:at + size]
        at += size
    return out


def kernel(x, ffn1_norm, ffn1_w_gate, ffn1_w_up, ffn1_w_down, mix_norm, w_in, conv_w, a_log, dt_bias, dn_norm, sg_ln_g, sg_ln_b, sg_w, sg_b, w_out, ffn2_norm, ffn2_w_gate, ffn2_w_up, ffn2_w_down, final_norm, loss_target, m_ffn1_norm, m_ffn1_w_gate, m_ffn1_w_up, m_ffn1_w_down, m_mix_norm, m_w_in, m_conv_w, m_a_log, m_dt_bias, m_dn_norm, m_sg_ln_g, m_sg_ln_b, m_sg_w, m_sg_b, m_w_out, m_ffn2_norm, m_ffn2_w_gate, m_ffn2_w_up, m_ffn2_w_down, m_final_norm, v_ffn1_norm, v_ffn1_w_gate, v_ffn1_w_up, v_ffn1_w_down, v_mix_norm, v_w_in, v_conv_w, v_a_log, v_dt_bias, v_dn_norm, v_sg_ln_g, v_sg_ln_b, v_sg_w, v_sg_b, v_w_out, v_ffn2_norm, v_ffn2_w_gate, v_ffn2_w_up, v_ffn2_w_down, v_final_norm):
    weights = dict(ffn1_norm=ffn1_norm, ffn1_w_gate=ffn1_w_gate, ffn1_w_up=ffn1_w_up, ffn1_w_down=ffn1_w_down, mix_norm=mix_norm, w_in=w_in, conv_w=conv_w, a_log=a_log, dt_bias=dt_bias, dn_norm=dn_norm, sg_ln_g=sg_ln_g, sg_ln_b=sg_ln_b, sg_w=sg_w, sg_b=sg_b, w_out=w_out, ffn2_norm=ffn2_norm, ffn2_w_gate=ffn2_w_gate, ffn2_w_up=ffn2_w_up, ffn2_w_down=ffn2_w_down, final_norm=final_norm)
    mom_m = dict(ffn1_norm=m_ffn1_norm, ffn1_w_gate=m_ffn1_w_gate, ffn1_w_up=m_ffn1_w_up, ffn1_w_down=m_ffn1_w_down, mix_norm=m_mix_norm, w_in=m_w_in, conv_w=m_conv_w, a_log=m_a_log, dt_bias=m_dt_bias, dn_norm=m_dn_norm, sg_ln_g=m_sg_ln_g, sg_ln_b=m_sg_ln_b, sg_w=m_sg_w, sg_b=m_sg_b, w_out=m_w_out, ffn2_norm=m_ffn2_norm, ffn2_w_gate=m_ffn2_w_gate, ffn2_w_up=m_ffn2_w_up, ffn2_w_down=m_ffn2_w_down, final_norm=m_final_norm)
    mom_v = dict(ffn1_norm=v_ffn1_norm, ffn1_w_gate=v_ffn1_w_gate, ffn1_w_up=v_ffn1_w_up, ffn1_w_down=v_ffn1_w_down, mix_norm=v_mix_norm, w_in=v_w_in, conv_w=v_conv_w, a_log=v_a_log, dt_bias=v_dt_bias, dn_norm=v_dn_norm, sg_ln_g=v_sg_ln_g, sg_ln_b=v_sg_ln_b, sg_w=v_sg_w, sg_b=v_sg_b, w_out=v_w_out, ffn2_norm=v_ffn2_norm, ffn2_w_gate=v_ffn2_w_gate, ffn2_w_up=v_ffn2_w_up, ffn2_w_down=v_ffn2_w_down, final_norm=v_final_norm)
    order = list(weights)
    big = ("ffn1_w_gate", "ffn1_w_up", "ffn1_w_down", "w_in", "w_out", "ffn2_w_gate", "ffn2_w_up", "ffn2_w_down")
    col_sharded = ("ffn1_w_gate", "ffn1_w_up", "w_in", "ffn2_w_gate", "ffn2_w_up")

    n_seq, seq, _ = x.shape
    t = n_seq * seq
    me = 4 * lax.axis_index("x") + 2 * lax.axis_index("y") + lax.axis_index("c")
    x0 = x.reshape(t, D_MODEL)
    tgt = loss_target.reshape(t, D_MODEL)

    def fill_own(land, own_block):
        return lax.dynamic_update_index_in_dim(land, own_block, me, 0)

    def as_full(n, g):
        return _cols_full(g) if n in col_sharded else g.reshape(-1, g.shape[-1])

    shards = {n: weights[n][0].astype(BF16) for n in big}
    ffn1_names, mix_names, ffn2_names = big[:3], big[3:5], big[5:]
    full = {n: as_full(n, g) for n, g in zip(ffn1_names, _gather2([shards[n] for n in ffn1_names], name="gather_ffn1"))}
    mix_srcs = [shards[n] for n in mix_names] + [conv_w[0]]
    mix_started, mix_token = _send_start(mix_srcs, True, name="gather_mix_start")
    ffn2_started, ffn2_token = _send_start([shards[n] for n in ffn2_names], True, name="gather_ffn2_start")
    ffn1_norm_fwd = ffn1_norm + (mix_token[:1, :1] + ffn2_token[:1, :1])
    alog, dtb = _pad_lanes(a_log), _pad_lanes(dt_bias)
    sgbt = _pad_lanes(sg_b[0].T)
    fnw = final_norm.reshape(1, D_MODEL)

    x1, h1, g1, u1 = _ffn_fwd(x0, ffn1_norm_fwd, full["ffn1_w_gate"], full["ffn1_w_up"], full["ffn1_w_down"], name="ffn1_fwd")
    mix_lands = [fill_own(land, src) for land, src in zip(_send_wait(mix_started, True, x1, name="gather_mix_wait"), mix_srcs)]
    full.update({n: as_full(n, g) for n, g in zip(mix_names, mix_lands)})
    conv_full = _cols_full(mix_lands[-1])
    w_in_f = full["w_in"]
    offs = (0, SG_WIDTH, 2 * SG_WIDTH, 2 * SG_WIDTH + 3 * DN_WIDTH, 2 * SG_WIDTH + 4 * DN_WIDTH)
    n_proj = offs[-1]
    ws = [w_in_f[:, offs[0]:offs[1]], w_in_f[:, offs[1]:offs[2]], w_in_f[:, offs[2]:offs[3]], w_in_f[:, offs[3]:offs[4]],
          _pad_lanes(w_in_f[:, n_proj:n_proj + DN_HEADS]), _pad_lanes(w_in_f[:, n_proj + DN_HEADS:n_proj + 2 * DN_HEADS])]
    wo_sg, wo_dn = full["w_out"][:SG_WIDTH], full["w_out"][SG_WIDTH:]
    u, v, qkv, z, bpre, apre = _mix_in_fwd(x1, mix_norm, ws, name="mix_in_fwd")
    sg_out = _sg_fwd(u, v, sg_ln_g, sg_ln_b, sg_w[0], sgbt, name="sg_fwd")
    q, k, vv, beta, gc = _dn_prep_fwd(qkv, bpre, apre, conv_full, alog, dtb, seq, name="dn_prep_fwd")
    grow = _chunk_rows_of(gc)
    wy_w, wy_u, q_dec, k_dec, qk, egl, inv = _delta_prep(q, k, vv, gc, grow, beta, name="delta_prep")
    o, states = _delta_seq_fwd(wy_w, wy_u, q_dec, k_dec, qk, egl, n_seq, seq, name="delta_seq_fwd")
    x2 = _mix_out_fwd(x1, sg_out, o, z, wo_sg, wo_dn, dn_norm, name="mix_out_fwd")
    ffn2_lands = _send_wait(ffn2_started, True, x2, name="gather_ffn2_wait")
    full.update({n: as_full(n, fill_own(land, shards[n])) for n, land in zip(ffn2_names, ffn2_lands)})
    dx3, loss_part, d_fn, h2, g2, u2 = _ffn_fwd(x2, ffn2_norm, full["ffn2_w_gate"], full["ffn2_w_up"], full["ffn2_w_down"],
                                                tgt, fnw, name="ffn2_fwd_loss")
    loss = lax.psum(loss_part[0, 0], ("x", "y", "c"))

    dx2, d_n2, d_g2, d_u2, d_d2 = _ffn_bwd(x2, ffn2_norm, h2, g2, u2, full["ffn2_w_gate"], full["ffn2_w_up"],
                                           full["ffn2_w_down"], dx3, name="ffn2_bwd")
    ff_cols = _cols_pieces

    def ff_rows(d_wd):
        return d_wd.reshape(N_DEV, D_FF // N_DEV, D_MODEL)

    ffn2_pieces = [p.astype(BF16) for p in (ff_cols(d_g2), ff_cols(d_u2), ff_rows(d_d2))]
    ffn2_sent, sent_token = _send_start(ffn2_pieces, False, name="grads_ffn2_start")
    dsg, do, dz, d_wo_sg, d_wo_dn, d_dnw = _mix_out_bwd(dx2, sg_out, o, z, wo_sg, wo_dn, dn_norm + sent_token[:1, :1],
                                                        name="mix_out_bwd")
    d_seq = _delta_seq_bwd(wy_w, wy_u, q_dec, k_dec, qk, egl, states, do, n_seq, seq, name="delta_seq_bwd")
    dq, dk, dv, dgc_a, dgrow, dbeta = _delta_par_bwd(q, k, vv, gc, grow, beta, inv, *d_seq, name="delta_par_bwd")
    dgc_b = _pad_lanes(jnp.transpose(dgrow[:, :DN_HEADS, :], (0, 2, 1)).reshape(t, DN_HEADS))
    dy_conv, dbpre, dapre, d_alog, d_dtb = _dn_prep_bwd(qkv, bpre, apre, conv_full, alog, dtb, dq, dk, dv, dbeta, dgc_a, dgc_b,
                                                        seq, name="dn_prep_bwd")
    dqkv, d_conv = _conv_bwd(qkv, dy_conv, conv_full, seq, name="conv_bwd")
    du, dvv, d_lng, d_lnb, d_wc, d_sgbt = _sg_bwd(u, v, sg_ln_g, sg_ln_b, sg_w[0], sgbt, dsg, name="sg_bwd")
    dx1, d_mixn, d_ws = _split3(_mix_in_bwd(x1, mix_norm, ws, dx2, (du, dvv, dqkv, dz, dbpre, dapre), name="mix_in_bwd"))
    d_w_in = jnp.concatenate([d_ws[0], d_ws[1], d_ws[2], d_ws[3], d_ws[4][:, :DN_HEADS], d_ws[5][:, :DN_HEADS]], axis=1)
    d_w_out = jnp.concatenate([d_wo_sg, d_wo_dn], axis=0)
    mix_pieces = [_cols_pieces(d_w_in).astype(BF16), d_w_out.reshape(N_DEV, D_MODEL // N_DEV, D_MODEL).astype(BF16)]
    mix_sent, sent_token = _send_start(mix_pieces, False, name="grads_mix_start")
    grad_x, d_n1, d_g1, d_u1, d_d1 = _ffn_bwd(x0, ffn1_norm + sent_token[:1, :1], h1, g1, u1, full["ffn1_w_gate"],
                                              full["ffn1_w_up"], full["ffn1_w_down"], dx1, name="ffn1_bwd")
    received = {}
    for names, sent, pieces in ((ffn2_names, ffn2_sent, ffn2_pieces), (mix_names, mix_sent, mix_pieces)):
        lands = _send_wait(sent, False, d_d1, name="grads_" + names[0] + "_wait")
        received.update({n: fill_own(land, lax.dynamic_index_in_dim(p, me, 0, keepdims=False))
                         for n, land, p in zip(names, lands, pieces)})

    def by_core(p8):
        return jnp.moveaxis(p8.reshape((4, 2) + p8.shape[1:]), 1, 0)

    own = [by_core(p) for p in (ff_cols(d_g1), ff_cols(d_u1), ff_rows(d_d1))]
    from_sibling = _pair_swap(own, name="grads_to_sibling")
    core = lax.axis_index("c").astype(jnp.int32).reshape(1)
    chip_sums = [_pair_add(p, r, core, name="pair_add_" + n) for n, p, r in zip(ffn1_names, own, from_sibling)]
    received.update(zip(ffn1_names, _chip_exchange(chip_sums, name="grads_to_owner")))
    res = {n: _adam(received[n], weights[n][0], mom_m[n][0], mom_v[n][0], name="adam_" + n) for n in big}

    small_grads = dict(ffn1_norm=d_n1, mix_norm=d_mixn, ffn2_norm=d_n2, final_norm=d_fn, a_log=d_alog[:, :DN_HEADS],
                       dt_bias=d_dtb[:, :DN_HEADS], dn_norm=d_dnw, sg_ln_g=d_lng, sg_ln_b=d_lnb, sg_w=d_wc,
                       sg_b=d_sgbt[:, :SG_GROUPS].T, conv_w=d_conv[:CONV_K])
    (small_parts,) = _gather2([_pack_small(small_grads)], name="gather_small_grads")
    zeros_conv = jnp.zeros((CONV_K * 3 * DN_WIDTH,), F32)
    packed = [_pack_small({**{n: src[n] for n, _ in _SMALL if n != "conv_w"}, "conv_w": zeros_conv})
              for src in (weights, mom_m, mom_v)]
    small_res = [_unpack_small(a) for a in _adam(small_parts, *packed, name="adam_small")]
    conv_grad = lax.dynamic_slice_in_dim(small_res[0]["conv_w"].reshape(CONV_K, 3 * DN_WIDTH), me * (3 * DN_WIDTH // N_DEV),
                                         3 * DN_WIDTH // N_DEV, axis=1)
    res["conv_w"] = _adam(conv_grad[None], conv_w[0], m_conv_w[0], v_conv_w[0], name="adam_conv_w")

    outs = [[], [], [], []]
    for n in order:
        for kind in range(4):
            if n in res:
                outs[kind].append(res[n][kind][None])
            else:
                outs[kind].append(small_res[kind][n].reshape(_SMALL_SHAPES[n]))
    return (loss, grad_x.reshape(x.shape), *outs[0], *outs[1], *outs[2], *outs[3])


def _split3(r):
    return r[0], r[1], r[2:]
```

```python
import functools

import jax
import jax.numpy as jnp
from jax import lax
from jax.experimental import pallas as pl
from jax.experimental.pallas import tpu as pltpu

F32 = jnp.float32
BF16 = jnp.bfloat16

D_MODEL = 1024
D_FF = 2816
SG_WIDTH = 512
SG_GROUPS = 8
SG_GROUP_DIM = 64
SG_CHUNK = 128
DN_WIDTH = 512
DN_HEAD_DIM = 128
DN_HEADS = 4
DN_CHUNK = 64
CONV_K = 4
EPS = 1e-6
N_DEV = 8
LANES = 128
HALO = 8

ADAM_LR = 0.001
ADAM_B1 = 0.9
ADAM_B2 = 0.999
ADAM_EPS = 1e-08
ADAM_WD = 0.01
ADAM_STEP = 10

VMEM_LIMIT = 60 * 1024 * 1024
TOKEN_BLOCK = 512
FF_BLOCK_FWD = 1408

_HI = lax.Precision.HIGHEST


def _cparams(sem):
    return pltpu.CompilerParams(dimension_semantics=sem, vmem_limit_bytes=VMEM_LIMIT)


def _tm(t, pref=TOKEN_BLOCK):
    return min(pref, t)


def _dg(a, b, ca, cb, precision):
    if precision is not None:
        return lax.dot_general(a, b, (((ca,), (cb,)), ((), ())), precision=precision, preferred_element_type=F32)
    return lax.dot_general(a.astype(BF16), b.astype(BF16), (((ca,), (cb,)), ((), ())), preferred_element_type=F32)


def _make_mm(exact):
    @jax.custom_vjp
    def mm(a, b):
        return _dg(a, b, 1, 0, exact)

    @jax.custom_vjp
    def mm_nt(a, b):
        return _dg(a, b, 1, 1, exact)

    @jax.custom_vjp
    def mm_tn(a, b):
        return _dg(a, b, 0, 0, exact)

    mm.defvjp(lambda a, b: (mm(a, b), (a, b)), lambda r, g: (mm_nt(g, r[1]), mm_tn(r[0], g)))
    mm_nt.defvjp(lambda a, b: (mm_nt(a, b), (a, b)), lambda r, g: (mm(g, r[1]), mm_tn(g, r[0])))
    mm_tn.defvjp(lambda a, b: (mm_tn(a, b), (a, b)), lambda r, g: (mm_nt(r[1], g), mm(r[0], g)))
    return mm, mm_nt, mm_tn


mm, mm_nt, mm_tn = _make_mm(None)
mmx, mmx_nt, mmx_tn = _make_mm(_HI)
mmh, mmh_nt, mmh_tn = _make_mm(lax.Precision.HIGH)


def _sigmoid(x):
    return 1.0 / (1.0 + jnp.exp(-x))


def _silu(x):
    return x * _sigmoid(x)


def _softplus(x):
    neg_abs = jnp.where(x > 0, -x, x)
    return jnp.where(x > 0, x, 0.0) + jnp.log(1.0 + jnp.exp(neg_abs))


def _gelu(x):
    return 0.5 * x * (1.0 + jnp.tanh(0.7978845608028654 * (x + 0.044715 * (x * x * x))))


def _rms_fwd(x, g):
    r = lax.rsqrt(jnp.mean(x * x, axis=-1, keepdims=True) + EPS)
    xh = x * r
    return xh * g, xh, r


def _rms_bwd(dh, xh, r, g):
    dxh = dh * g
    dx = r * (dxh - xh * jnp.mean(dxh * xh, axis=-1, keepdims=True))
    return dx, jnp.sum(dh * xh, axis=0, keepdims=True)


def _acc_out(ref, first, val):
    @pl.when(first)
    def _():
        ref[...] = val

    @pl.when(jnp.logical_not(first))
    def _():
        ref[...] += val


def _ffn_fwd(x, nw, wg, wu, wd, tgt=None, fnw=None, *, name):
    t = x.shape[0]
    tm, fb = _tm(t), FF_BLOCK_FWD
    n_t, n_f = t // tm, D_FF // fb
    with_loss = tgt is not None

    def body(*refs):
        if with_loss:
            (x_ref, nw_ref, wg_ref, wu_ref, wd_ref, tgt_ref, fnw_ref, dy_ref, loss_ref, dfn_ref, h_ref, g_ref, u_ref,
             acc_s) = refs
        else:
            x_ref, nw_ref, wg_ref, wu_ref, wd_ref, y_ref, h_ref, g_ref, u_ref, acc_s = refs
        i, j = pl.program_id(0), pl.program_id(1)

        @pl.when(j == 0)
        def _():
            h, _, _ = _rms_fwd(x_ref[...], nw_ref[...])
            h_ref[...] = h.astype(BF16)
            acc_s[...] = jnp.zeros_like(acc_s)

        h = h_ref[...]
        g = jnp.dot(h, wg_ref[...], preferred_element_type=F32)
        u = jnp.dot(h, wu_ref[...], preferred_element_type=F32)
        g_ref[...] = g.astype(BF16)
        u_ref[...] = u.astype(BF16)
        a = _silu(g) * u
        acc_s[...] += jnp.dot(a.astype(BF16), wd_ref[...], preferred_element_type=F32)

        @pl.when(j == n_f - 1)
        def _():
            y = x_ref[...] + 0.5 * acc_s[...]
            if not with_loss:
                y_ref[...] = y
            else:
                gf = fnw_ref[...]
                out, xh, r = _rms_fwd(y, gf)
                err = out - tgt_ref[...]
                part = 0.5 * jnp.sum(jnp.mean(err * err, axis=-1, keepdims=True), axis=0, keepdims=True)
                d_out = err * (1.0 / D_MODEL)
                dy, dgf = _rms_bwd(d_out, xh, r, gf)
                dy_ref[...] = dy
                _acc_out(loss_ref, i == 0, jnp.broadcast_to(part, loss_ref.shape))
                _acc_out(dfn_ref, i == 0, dgf)

    row = lambda i, j: (i, 0)
    const = lambda i, j: (0, 0)
    in_specs = [
        pl.BlockSpec((tm, D_MODEL), row),
        pl.BlockSpec((1, D_MODEL), const),
        pl.BlockSpec((D_MODEL, fb), lambda i, j: (0, j)),
        pl.BlockSpec((D_MODEL, fb), lambda i, j: (0, j)),
        pl.BlockSpec((fb, D_MODEL), lambda i, j: (j, 0)),
    ]
    args = [x, nw, wg, wu, wd]
    saved_shape = (jax.ShapeDtypeStruct((t, D_MODEL), BF16), jax.ShapeDtypeStruct((t, D_FF), BF16),
                   jax.ShapeDtypeStruct((t, D_FF), BF16))
    saved_specs = (pl.BlockSpec((tm, D_MODEL), row), pl.BlockSpec((tm, fb), lambda i, j: (i, j)),
                   pl.BlockSpec((tm, fb), lambda i, j: (i, j)))
    if with_loss:
        in_specs += [pl.BlockSpec((tm, D_MODEL), row), pl.BlockSpec((1, D_MODEL), const)]
        args += [tgt, fnw]
        out_shape = (jax.ShapeDtypeStruct((t, D_MODEL), F32), jax.ShapeDtypeStruct((8, LANES), F32),
                     jax.ShapeDtypeStruct((1, D_MODEL), F32)) + saved_shape
        out_specs = (pl.BlockSpec((tm, D_MODEL), row), pl.BlockSpec((8, LANES), const),
                     pl.BlockSpec((1, D_MODEL), const)) + saved_specs
        sem = ("arbitrary", "arbitrary")
    else:
        out_shape = (jax.ShapeDtypeStruct((t, D_MODEL), F32),) + saved_shape
        out_specs = (pl.BlockSpec((tm, D_MODEL), row),) + saved_specs
        sem = ("parallel", "arbitrary")
    return pl.pallas_call(
        body, name=name, grid=(n_t, n_f), in_specs=in_specs, out_specs=out_specs, out_shape=out_shape,
        scratch_shapes=[pltpu.VMEM((tm, D_MODEL), F32)],
        compiler_params=_cparams(sem),
    )(*args)


def _ffn_bwd_x(x, nw, g, u, wg, wu, wd, dy, *, name):
    t = x.shape[0]
    tm = _tm(t, 256)

    def body(x_ref, nw_ref, g_ref, u_ref, wg_ref, wu_ref, wd_ref, dy_ref, dx_ref, dnw_ref, dg_ref, du_ref, a_ref, dyh_ref):
        i = pl.program_id(0)
        nt = (((1,), (1,)), ((), ()))
        dy = dy_ref[...]
        dyh = (0.5 * dy).astype(BF16)
        dyh_ref[...] = dyh
        gate, up = g_ref[...].astype(F32), u_ref[...].astype(F32)
        s = _sigmoid(gate)
        gs = gate * s
        da = lax.dot_general(dyh, wd_ref[...], nt, preferred_element_type=F32)
        dg = (da * up * (s + gs * (1.0 - s))).astype(BF16)
        du = (da * gs).astype(BF16)
        dg_ref[...] = dg
        du_ref[...] = du
        a_ref[...] = (gs * up).astype(BF16)
        dh = (lax.dot_general(dg, wg_ref[...], nt, preferred_element_type=F32)
              + lax.dot_general(du, wu_ref[...], nt, preferred_element_type=F32))
        xv = x_ref[...]
        r = lax.rsqrt(jnp.mean(xv * xv, axis=-1, keepdims=True) + EPS)
        dx, dnw = _rms_bwd(dh, xv * r, r, nw_ref[...])
        dx_ref[...] = dy + dx
        _acc_out(dnw_ref, i == 0, dnw)

    row = lambda i: (i, 0)
    const = lambda i: (0, 0)
    once = pl.Buffered(1)
    wide = pl.BlockSpec((tm, D_FF), row)
    return pl.pallas_call(
        body, name=name, grid=(t // tm,),
        in_specs=[pl.BlockSpec((tm, D_MODEL), row), pl.BlockSpec((1, D_MODEL), const), wide, wide,
                  pl.BlockSpec((D_MODEL, D_FF), const, pipeline_mode=once), pl.BlockSpec((D_MODEL, D_FF), const, pipeline_mode=once),
                  pl.BlockSpec((D_FF, D_MODEL), const, pipeline_mode=once), pl.BlockSpec((tm, D_MODEL), row)],
        out_specs=(pl.BlockSpec((tm, D_MODEL), row), pl.BlockSpec((1, D_MODEL), const), wide, wide, wide,
                   pl.BlockSpec((tm, D_MODEL), row)),
        out_shape=(jax.ShapeDtypeStruct((t, D_MODEL), F32), jax.ShapeDtypeStruct((1, D_MODEL), F32),
                   jax.ShapeDtypeStruct((t, D_FF), BF16), jax.ShapeDtypeStruct((t, D_FF), BF16),
                   jax.ShapeDtypeStruct((t, D_FF), BF16), jax.ShapeDtypeStruct((t, D_MODEL), BF16)),
        compiler_params=_cparams(("arbitrary",)),
    )(x, nw, g, u, wg, wu, wd, dy)


def _wgrad(a, b, bm, bn, after=None, *, name):
    k, m = a.shape
    n = b.shape[1]
    tk = _tm(k, 2048)

    def body(a_ref, b_ref, *rest):
        o_ref = rest[-1]
        part = lax.dot_general(a_ref[...], b_ref[...], (((0,), (0,)), ((), ())), preferred_element_type=F32)
        _acc_out(o_ref, pl.program_id(2) == 0, part)

    return pl.pallas_call(
        body, name=name, grid=(m // bm, n // bn, k // tk),
        in_specs=[pl.BlockSpec((tk, bm), lambda i, j, s: (s, i)), pl.BlockSpec((tk, bn), lambda i, j, s: (s, j))]
        + ([] if after is None else [_HBM]),
        out_specs=pl.BlockSpec((bm, bn), lambda i, j, s: (i, j)),
        out_shape=jax.ShapeDtypeStruct((m, n), F32),
        compiler_params=_cparams(("parallel", "parallel", "arbitrary")),
    )(a, b, *([] if after is None else [after]))


def _ffn_wgrads(h, dg, du, a, dyh, between=None, after=None, *, name):
    half = D_FF // 2
    grads = []
    for k, (lhs, rhs, bm, bn, tag) in enumerate(((h, dg, D_MODEL, half, "_wg"), (h, du, D_MODEL, half, "_wu"),
                                                 (a, dyh, half, D_MODEL, "_wd"))):
        grads.append(_wgrad(lhs, rhs, bm, bn, after, name=name + tag))
        after = None if between is None else between(k, grads[-1])
    return grads


def _ffn_bwd(x, nw, h, g, u, wg, wu, wd, dy, *, name):
    dx, dnw, dg, du, a, dyh = _ffn_bwd_x(x, nw, g, u, wg, wu, wd, dy, name=name + "_x")
    return (dx, dnw, *_ffn_wgrads(h, dg, du, a, dyh, name=name))


_PROJ_WIDTHS = (SG_WIDTH, SG_WIDTH, 3 * DN_WIDTH, DN_WIDTH, LANES, LANES)


def _mix_in_fwd(x, nw, ws, *, name):
    t = x.shape[0]
    tm = _tm(t)

    def body(x_ref, nw_ref, *refs):
        w_refs, o_refs = refs[:6], refs[6:]
        h, _, _ = _rms_fwd(x_ref[...], nw_ref[...])
        h = h.astype(BF16)
        for w_ref, o_ref in zip(w_refs, o_refs):
            o_ref[...] = jnp.dot(h, w_ref[...], preferred_element_type=F32)

    row = lambda i: (i, 0)
    const = lambda i: (0, 0)
    return pl.pallas_call(
        body, name=name, grid=(t // tm,),
        in_specs=[pl.BlockSpec((tm, D_MODEL), row), pl.BlockSpec((1, D_MODEL), const)]
        + [pl.BlockSpec((D_MODEL, n), const) for n in _PROJ_WIDTHS],
        out_specs=tuple(pl.BlockSpec((tm, n), row) for n in _PROJ_WIDTHS),
        out_shape=tuple(jax.ShapeDtypeStruct((t, n), F32) for n in _PROJ_WIDTHS),
        compiler_params=_cparams(("parallel",)),
    )(x, nw, *ws)


def _mix_in_bwd(x, nw, ws, dres, dps, *, name):
    t = x.shape[0]
    tm = _tm(t, 256)

    def body(x_ref, nw_ref, dres_ref, *refs):
        w_refs, dp_refs, dx_ref, dnw_ref, dw_refs = refs[:6], refs[6:12], refs[12], refs[13], refs[14:]
        i = pl.program_id(0)
        hf, xh, r = _rms_fwd(x_ref[...], nw_ref[...])
        h = hf.astype(BF16)
        dh = jnp.zeros((tm, D_MODEL), F32)
        for w_ref, dp_ref, dw_ref in zip(w_refs, dp_refs, dw_refs):
            dp = dp_ref[...].astype(BF16)
            dh = dh + lax.dot_general(dp, w_ref[...], (((1,), (1,)), ((), ())), preferred_element_type=F32)
            _acc_out(dw_ref, i == 0, lax.dot_general(h, dp, (((0,), (0,)), ((), ())), preferred_element_type=F32))
        dx, dnw = _rms_bwd(dh, xh, r, nw_ref[...])
        dx_ref[...] = dres_ref[...] + dx
        _acc_out(dnw_ref, i == 0, dnw)

    row = lambda i: (i, 0)
    const = lambda i: (0, 0)
    return pl.pallas_call(
        body, name=name, grid=(t // tm,),
        in_specs=[pl.BlockSpec((tm, D_MODEL), row), pl.BlockSpec((1, D_MODEL), const), pl.BlockSpec((tm, D_MODEL), row)]
        + [pl.BlockSpec((D_MODEL, n), const) for n in _PROJ_WIDTHS]
        + [pl.BlockSpec((tm, n), row) for n in _PROJ_WIDTHS],
        out_specs=(pl.BlockSpec((tm, D_MODEL), row), pl.BlockSpec((1, D_MODEL), const))
        + tuple(pl.BlockSpec((D_MODEL, n), const) for n in _PROJ_WIDTHS),
        out_shape=(jax.ShapeDtypeStruct((t, D_MODEL), F32), jax.ShapeDtypeStruct((1, D_MODEL), F32))
        + tuple(jax.ShapeDtypeStruct((D_MODEL, n), F32) for n in _PROJ_WIDTHS),
        compiler_params=_cparams(("arbitrary",)),
    )(x, nw, dres, *ws, *dps)


def _sg_fn(u, v, lng, lnb, wcs, sgbt):
    lane = lax.broadcasted_iota(jnp.int32, (1, SG_WIDTH), 1)
    lane_b = lax.broadcasted_iota(jnp.int32, (1, LANES), 1)
    rr = lax.broadcasted_iota(jnp.int32, (SG_CHUNK, SG_CHUNK), 0)
    cc = lax.broadcasted_iota(jnp.int32, (SG_CHUNK, SG_CHUNK), 1)
    gu, gv = _gelu(u), _gelu(v)
    mu = jnp.mean(gv, axis=-1, keepdims=True)
    cen = gv - mu
    var = jnp.mean(cen * cen, axis=-1, keepdims=True)
    ln = cen * lax.rsqrt(var + EPS) * lng + lnb
    vs = jnp.zeros_like(u)
    for g in range(SG_GROUPS):
        in_group = jnp.logical_and(lane >= g * SG_GROUP_DIM, lane < (g + 1) * SG_GROUP_DIM)
        w_causal = jnp.where(rr >= cc, wcs[g], 0.0)
        bias = jnp.sum(jnp.where(lane_b == g, sgbt, 0.0), axis=1, keepdims=True)
        vs = vs + jnp.where(in_group, mm(w_causal, ln) + bias, 0.0)
    return gu * vs


def _sg_fwd(u, v, lng, lnb, wc, sgbt, *, name):
    t = u.shape[0]
    tm = _tm(t)

    def body(u_ref, v_ref, lng_ref, lnb_ref, wc_ref, sgbt_ref, o_ref):
        wcs = [wc_ref[g] for g in range(SG_GROUPS)]
        for c in range(tm // SG_CHUNK):
            rows = pl.ds(c * SG_CHUNK, SG_CHUNK)
            o_ref[rows, :] = _sg_fn(u_ref[rows, :], v_ref[rows, :], lng_ref[...], lnb_ref[...], wcs, sgbt_ref[...])

    row = lambda i: (i, 0)
    const = lambda i: (0, 0)
    return pl.pallas_call(
        body, name=name, grid=(t // tm,),
        in_specs=[pl.BlockSpec((tm, SG_WIDTH), row), pl.BlockSpec((tm, SG_WIDTH), row),
                  pl.BlockSpec((1, SG_WIDTH), const), pl.BlockSpec((1, SG_WIDTH), const),
                  pl.BlockSpec((SG_GROUPS, SG_CHUNK, SG_CHUNK), lambda i: (0, 0, 0)), pl.BlockSpec((SG_CHUNK, LANES), const)],
        out_specs=pl.BlockSpec((tm, SG_WIDTH), row),
        out_shape=jax.ShapeDtypeStruct((t, SG_WIDTH), F32),
        compiler_params=_cparams(("parallel",)),
    )(u, v, lng, lnb, wc, sgbt)


def _sg_bwd(u, v, lng, lnb, wc, sgbt, dout, *, name):
    t = u.shape[0]
    tm = _tm(t)

    def body(u_ref, v_ref, lng_ref, lnb_ref, wc_ref, sgbt_ref, do_ref, du_ref, dv_ref, dlng_ref, dlnb_ref, dwc_ref, dsgbt_ref):
        i = pl.program_id(0)
        wcs = [wc_ref[g] for g in range(SG_GROUPS)]
        tot = None
        for c in range(tm // SG_CHUNK):
            rows = pl.ds(c * SG_CHUNK, SG_CHUNK)
            _, vjp = jax.vjp(_sg_fn, u_ref[rows, :], v_ref[rows, :], lng_ref[...], lnb_ref[...], wcs, sgbt_ref[...])
            du, dv, dlng, dlnb, dwcs, dsgbt = vjp(do_ref[rows, :])
            du_ref[rows, :] = du
            dv_ref[rows, :] = dv
            part = (dlng, dlnb, dwcs, dsgbt)
            tot = part if tot is None else jax.tree.map(jnp.add, tot, part)
        dlng, dlnb, dwcs, dsgbt = tot
        _acc_out(dlng_ref, i == 0, dlng)
        _acc_out(dlnb_ref, i == 0, dlnb)
        _acc_out(dsgbt_ref, i == 0, dsgbt)
        for g in range(SG_GROUPS):
            @pl.when(i == 0)
            def _(g=g):
                dwc_ref[g] = dwcs[g]

            @pl.when(i > 0)
            def _(g=g):
                dwc_ref[g] += dwcs[g]

    row = lambda i: (i, 0)
    const = lambda i: (0, 0)
    wspec = pl.BlockSpec((SG_GROUPS, SG_CHUNK, SG_CHUNK), lambda i: (0, 0, 0))
    return pl.pallas_call(
        body, name=name, grid=(t // tm,),
        in_specs=[pl.BlockSpec((tm, SG_WIDTH), row), pl.BlockSpec((tm, SG_WIDTH), row),
                  pl.BlockSpec((1, SG_WIDTH), const), pl.BlockSpec((1, SG_WIDTH), const), wspec,
                  pl.BlockSpec((SG_CHUNK, LANES), const), pl.BlockSpec((tm, SG_WIDTH), row)],
        out_specs=(pl.BlockSpec((tm, SG_WIDTH), row), pl.BlockSpec((tm, SG_WIDTH), row),
                   pl.BlockSpec((1, SG_WIDTH), const), pl.BlockSpec((1, SG_WIDTH), const), wspec,
                   pl.BlockSpec((SG_CHUNK, LANES), const)),
        out_shape=(jax.ShapeDtypeStruct((t, SG_WIDTH), F32), jax.ShapeDtypeStruct((t, SG_WIDTH), F32),
                   jax.ShapeDtypeStruct((1, SG_WIDTH), F32), jax.ShapeDtypeStruct((1, SG_WIDTH), F32),
                   jax.ShapeDtypeStruct((SG_GROUPS, SG_CHUNK, SG_CHUNK), F32), jax.ShapeDtypeStruct((SG_CHUNK, LANES), F32)),
        compiler_params=_cparams(("arbitrary",)),
    )(u, v, lng, lnb, wc, sgbt, dout)


def _conv_taps(ext, w, tm):
    y = None
    for j in range(CONV_K):
        s = CONV_K - 1 - j
        shifted = ext if s == 0 else pltpu.roll(ext, s, 0)
        term = w[j:j + 1, :] * shifted[HALO:HALO + tm, :]
        y = term if y is None else y + term
    return y


def _post_conv(yq, yk, yv, bpre, apre, alog, dtb):
    def l2(a):
        return a * lax.rsqrt(jnp.sum(a * a, axis=-1, keepdims=True) + EPS)

    q = [l2(_silu(a)) for a in yq]
    k = [l2(_silu(a)) for a in yk]
    return q, k, _silu(yv), _sigmoid(bpre), -jnp.exp(alog) * _softplus(apre + dtb)


def _chunk_tril(tm):
    rr = lax.broadcasted_iota(jnp.int32, (tm, tm), 0)
    cc = lax.broadcasted_iota(jnp.int32, (tm, tm), 1)
    shift = DN_CHUNK.bit_length() - 1
    same = jnp.right_shift(rr, shift) == jnp.right_shift(cc, shift)
    return jnp.where(jnp.logical_and(same, rr >= cc), 1.0, 0.0).astype(F32)


def _halo_specs(tm, width, n_blocks_seq, n_blocks):
    per = tm // HALO
    prev = pl.BlockSpec((HALO, width), lambda i: (jnp.maximum(i * per - 1, 0), 0))
    nxt = pl.BlockSpec((HALO, width), lambda i: (jnp.minimum((i + 1) * per, n_blocks * per - 1), 0))
    return prev, nxt


def _split_heads(ref, base):
    return [ref[:, base + h * DN_HEAD_DIM: base + (h + 1) * DN_HEAD_DIM] for h in range(DN_HEADS)]


def _dn_prep_fwd(qkv, bpre, apre, conv_w, alog, dtb, seq, *, name):
    t = qkv.shape[0]
    tm = _tm(t)
    bps = seq // tm
    cw = 3 * DN_WIDTH

    def body(x_ref, halo_ref, b_ref, a_ref, w_ref, alog_ref, dtb_ref, q_ref, k_ref, v_ref, beta_ref, gc_ref):
        i = pl.program_id(0)
        keep = jnp.where(i % bps == 0, 0.0, 1.0)
        ext = jnp.concatenate([halo_ref[...] * keep, x_ref[...]], axis=0)
        y = _conv_taps(ext, w_ref[...], tm)
        yq = [y[:, h * DN_HEAD_DIM:(h + 1) * DN_HEAD_DIM] for h in range(DN_HEADS)]
        yk = [y[:, DN_WIDTH + h * DN_HEAD_DIM: DN_WIDTH + (h + 1) * DN_HEAD_DIM] for h in range(DN_HEADS)]
        q, k, v, beta, g = _post_conv(yq, yk, y[:, 2 * DN_WIDTH:], b_ref[...], a_ref[...], alog_ref[...], dtb_ref[...])
        for h in range(DN_HEADS):
            q_ref[:, h * DN_HEAD_DIM:(h + 1) * DN_HEAD_DIM] = q[h]
            k_ref[:, h * DN_HEAD_DIM:(h + 1) * DN_HEAD_DIM] = k[h]
        v_ref[...] = v
        beta_ref[...] = beta
        gc_ref[...] = mmx(_chunk_tril(tm), g)

    row = lambda i: (i, 0)
    const = lambda i: (0, 0)
    prev, _ = _halo_specs(tm, cw, bps, t // tm)
    return pl.pallas_call(
        body, name=name, grid=(t // tm,),
        in_specs=[pl.BlockSpec((tm, cw), row), prev, pl.BlockSpec((tm, LANES), row), pl.BlockSpec((tm, LANES), row),
                  pl.BlockSpec((CONV_K, cw), const), pl.BlockSpec((1, LANES), const), pl.BlockSpec((1, LANES), const)],
        out_specs=tuple(pl.BlockSpec((tm, n), row) for n in (DN_WIDTH, DN_WIDTH, DN_WIDTH, LANES, LANES)),
        out_shape=tuple(jax.ShapeDtypeStruct((t, n), F32) for n in (DN_WIDTH, DN_WIDTH, DN_WIDTH, LANES, LANES)),
        compiler_params=_cparams(("parallel",)),
    )(qkv, qkv, bpre, apre, conv_w, alog, dtb)


def _dn_prep_bwd(qkv, bpre, apre, conv_w, alog, dtb, dq, dk, dv, dbeta, dgc, dgc2, seq, *, name):
    t = qkv.shape[0]
    tm = _tm(t)
    bps = seq // tm
    cw = 3 * DN_WIDTH

    def body(x_ref, halo_ref, b_ref, a_ref, w_ref, alog_ref, dtb_ref, dq_ref, dk_ref, dv_ref, dbeta_ref, dgc_ref, dgc2_ref,
             dy_ref, db_ref, da_ref, dalog_ref, ddtb_ref):
        i = pl.program_id(0)
        keep = jnp.where(i % bps == 0, 0.0, 1.0)
        ext = jnp.concatenate([halo_ref[...] * keep, x_ref[...]], axis=0)
        y = _conv_taps(ext, w_ref[...], tm)
        yq = [y[:, h * DN_HEAD_DIM:(h + 1) * DN_HEAD_DIM] for h in range(DN_HEADS)]
        yk = [y[:, DN_WIDTH + h * DN_HEAD_DIM: DN_WIDTH + (h + 1) * DN_HEAD_DIM] for h in range(DN_HEADS)]
        _, vjp = jax.vjp(_post_conv, yq, yk, y[:, 2 * DN_WIDTH:], b_ref[...], a_ref[...], alog_ref[...], dtb_ref[...])
        dg = mmx_tn(_chunk_tril(tm), dgc_ref[...] + dgc2_ref[...])
        dyq, dyk, dyv, db, da, dalog, ddtb = vjp((_split_heads(dq_ref, 0), _split_heads(dk_ref, 0), dv_ref[...],
                                                  dbeta_ref[...], dg))
        for h in range(DN_HEADS):
            dy_ref[:, h * DN_HEAD_DIM:(h + 1) * DN_HEAD_DIM] = dyq[h]
            dy_ref[:, DN_WIDTH + h * DN_HEAD_DIM: DN_WIDTH + (h + 1) * DN_HEAD_DIM] = dyk[h]
        dy_ref[:, 2 * DN_WIDTH:] = dyv
        db_ref[...] = db
        da_ref[...] = da
        _acc_out(dalog_ref, i == 0, dalog)
        _acc_out(ddtb_ref, i == 0, ddtb)

    row = lambda i: (i, 0)
    const = lambda i: (0, 0)
    prev, _ = _halo_specs(tm, cw, bps, t // tm)
    return pl.pallas_call(
        body, name=name, grid=(t // tm,),
        in_specs=[pl.BlockSpec((tm, cw), row), prev, pl.BlockSpec((tm, LANES), row), pl.BlockSpec((tm, LANES), row),
                  pl.BlockSpec((CONV_K, cw), const), pl.BlockSpec((1, LANES), const), pl.BlockSpec((1, LANES), const),
                  pl.BlockSpec((tm, DN_WIDTH), row), pl.BlockSpec((tm, DN_WIDTH), row), pl.BlockSpec((tm, DN_WIDTH), row),
                  pl.BlockSpec((tm, LANES), row), pl.BlockSpec((tm, LANES), row), pl.BlockSpec((tm, LANES), row)],
        out_specs=(pl.BlockSpec((tm, cw), row), pl.BlockSpec((tm, LANES), row), pl.BlockSpec((tm, LANES), row),
                   pl.BlockSpec((1, LANES), const), pl.BlockSpec((1, LANES), const)),
        out_shape=(jax.ShapeDtypeStruct((t, cw), F32), jax.ShapeDtypeStruct((t, LANES), F32), jax.ShapeDtypeStruct((t, LANES), F32),
                   jax.ShapeDtypeStruct((1, LANES), F32), jax.ShapeDtypeStruct((1, LANES), F32)),
        compiler_params=_cparams(("arbitrary",)),
    )(qkv, qkv, bpre, apre, conv_w, alog, dtb, dq, dk, dv, dbeta, dgc, dgc2)


def _conv_bwd(qkv, dy, conv_w, seq, *, name):
    t = qkv.shape[0]
    tm = _tm(t)
    bps = seq // tm
    cw = 3 * DN_WIDTH
    n_ext = tm + HALO

    def body(x_ref, halo_ref, dy_ref, dyn_ref, w_ref, dx_ref, dw_ref):
        i = pl.program_id(0)
        keep_prev = jnp.where(i % bps == 0, 0.0, 1.0)
        keep_next = jnp.where(i % bps == bps - 1, 0.0, 1.0)
        ext = jnp.concatenate([halo_ref[...] * keep_prev, x_ref[...]], axis=0)
        dy = dy_ref[...]
        dyext = jnp.concatenate([dy, dyn_ref[...] * keep_next], axis=0)
        w = w_ref[...]

        @pl.when(i == 0)
        def _():
            dw_ref[...] = jnp.zeros_like(dw_ref)

        dx = None
        for j in range(CONV_K):
            s = CONV_K - 1 - j
            fut = dyext if s == 0 else pltpu.roll(dyext, n_ext - s, 0)
            term = w[j:j + 1, :] * fut[0:tm, :]
            dx = term if dx is None else dx + term
            past = ext if s == 0 else pltpu.roll(ext, s, 0)
            dw_ref[j:j + 1, :] += jnp.sum(dy * past[HALO:HALO + tm, :], axis=0, keepdims=True)
        dx_ref[...] = dx

    row = lambda i: (i, 0)
    const = lambda i: (0, 0)
    prev, nxt = _halo_specs(tm, cw, bps, t // tm)
    return pl.pallas_call(
        body, name=name, grid=(t // tm,),
        in_specs=[pl.BlockSpec((tm, cw), row), prev, pl.BlockSpec((tm, cw), row), nxt, pl.BlockSpec((CONV_K, cw), const)],
        out_specs=(pl.BlockSpec((tm, cw), row), pl.BlockSpec((HALO, cw), const)),
        out_shape=(jax.ShapeDtypeStruct((t, cw), F32), jax.ShapeDtypeStruct((HALO, cw), F32)),
        compiler_params=_cparams(("arbitrary",)),
    )(qkv, qkv, dy, dy, conv_w)


def _inv_unit_lower(l_mats, eye):
    invs = [eye - l for l in l_mats]
    powers = list(l_mats)
    n = 2
    while n < eye.shape[0]:
        powers = [mmh(p, p) for p in powers]
        invs = [inv + mmh(inv, p) for inv, p in zip(invs, powers)]
        n *= 2
    return invs


@jax.custom_vjp
def _solve(l_mat, rhs, inv):
    return mmh(inv, rhs)


def _solve_fwd(l_mat, rhs, inv):
    sol = mmh(inv, rhs)
    return sol, (inv, sol)


def _solve_bwd(res, d_sol):
    inv, sol = res
    d_rhs = mmh_tn(inv, d_sol)
    return -mmh_nt(d_rhs, sol), d_rhs, jnp.zeros_like(inv)


_solve.defvjp(_solve_fwd, _solve_bwd)


def _prep_fn(q, k, v, gc, gr, b, inv):
    ids = range(len(q))
    c = q[0].shape[0]
    rr = lax.broadcasted_iota(jnp.int32, (c, c), 0)
    cc = lax.broadcasted_iota(jnp.int32, (c, c), 1)
    incl, strict = rr >= cc, rr > cc
    is_last = lax.broadcasted_iota(jnp.int32, (c, 1), 0) == c - 1
    qs = [q[i] * (DN_HEAD_DIM ** -0.5) for i in ids]
    decay = [jnp.where(incl, jnp.exp(jnp.where(incl, gc[i] - gr[i], 0.0)), 0.0) for i in ids]
    kb = [k[i] * b[i] for i in ids]
    vb = [v[i] * b[i] for i in ids]
    kk = [mm_nt(kb[i], k[i]) for i in ids]
    l_mat = [jnp.where(strict, kk[i] * decay[i], 0.0) for i in ids]
    eg = [jnp.exp(gc[i]) for i in ids]
    if inv is None:
        inv = _inv_unit_lower(l_mat, jnp.where(rr == cc, 1.0, 0.0).astype(F32))
    u_wy = [_solve(l_mat[i], vb[i], inv[i]) for i in ids]
    w_wy = [_solve(l_mat[i], kb[i] * eg[i], inv[i]) for i in ids]
    qk = [mm_nt(qs[i], k[i]) * decay[i] for i in ids]
    g_last = [jnp.sum(jnp.where(is_last, gc[i], 0.0), axis=0, keepdims=True) for i in ids]
    k_dec = [k[i] * jnp.exp(g_last[i] - gc[i]) for i in ids]
    egl = [jnp.broadcast_to(jnp.exp(g_last[i]), (1, LANES)) for i in ids]
    return [(w_wy[i], u_wy[i], qs[i] * eg[i], k_dec[i], qk[i], egl[i]) for i in ids], inv


def _seq_fn(w, u, qd, kd, qk, egl, s):
    ids = range(len(w))
    ws = [mm(w[i], s[i]) for i in ids]
    qs = [mm(qd[i], s[i]) for i in ids]
    v_new = [u[i] - ws[i] for i in ids]
    o = [qs[i] + mm(qk[i], v_new[i]) for i in ids]
    s_new = [s[i] * egl[i] + mm_tn(kd[i], v_new[i]) for i in ids]
    return o, s_new


def _lane_col(a, h):
    lane = lax.broadcasted_iota(jnp.int32, (1, LANES), 1)
    return jnp.sum(jnp.where(lane == h, a, 0.0), axis=1, keepdims=True)


def _col_lane(col, h):
    lane = lax.broadcasted_iota(jnp.int32, (1, LANES), 1)
    return jnp.where(lane == h, col, 0.0)


def _head_cols(h):
    return slice(h * DN_HEAD_DIM, (h + 1) * DN_HEAD_DIM)


def _chunk_rows(n):
    return pl.ds(pl.multiple_of(n * DN_CHUNK, DN_CHUNK), DN_CHUNK)


def _delta_prep(q, k, v, gc, grow, beta, *, name):
    t = q.shape[0]
    tm = _tm(t)
    cpb = tm // DN_CHUNK
    n_chunks = t // DN_CHUNK
    group = 2

    def body(q_ref, k_ref, v_ref, gc_ref, gr_ref, b_ref, w_ref, u_ref, qd_ref, kd_ref, qk_ref, egl_ref, inv_ref):
        def step(m, carry):
            probs = [(m * group + e, h) for e in range(group) for h in range(DN_HEADS)]
            gcb = [gc_ref[_chunk_rows(m * group + e), :] for e in range(group)]
            bb = [b_ref[_chunk_rows(m * group + e), :] for e in range(group)]
            grb = [gr_ref[m * group + e] for e in range(group)]
            for e in range(group):
                egl_ref[m * group + e] = jnp.zeros((HALO, LANES), F32)
            outs, invs = _prep_fn(
                [q_ref[_chunk_rows(n), _head_cols(h)] for n, h in probs], [k_ref[_chunk_rows(n), _head_cols(h)] for n, h in probs],
                [v_ref[_chunk_rows(n), _head_cols(h)] for n, h in probs],
                [_lane_col(gcb[e], h) for e in range(group) for h in range(DN_HEADS)],
                [grb[e][h:h + 1, :] for e in range(group) for h in range(DN_HEADS)],
                [_lane_col(bb[e], h) for e in range(group) for h in range(DN_HEADS)], None)
            for (n, h), (w, u, qd, kd, qk, egl), inv in zip(probs, outs, invs):
                rows, cols = _chunk_rows(n), _head_cols(h)
                w_ref[rows, cols] = w.astype(BF16)
                u_ref[rows, cols] = u
                qd_ref[rows, cols] = qd.astype(BF16)
                kd_ref[rows, cols] = kd.astype(BF16)
                qk_ref[n, h] = qk
                inv_ref[n, h] = inv
                egl_ref[n, h:h + 1, :] = egl
            return carry

        lax.fori_loop(0, cpb // group, step, 0)

    row = lambda i: (i, 0)
    tok = pl.BlockSpec((tm, DN_WIDTH), row)
    lanes = pl.BlockSpec((tm, LANES), row)
    sq = pl.BlockSpec((cpb, DN_HEADS, DN_CHUNK, DN_CHUNK), lambda i: (i, 0, 0, 0))
    return pl.pallas_call(
        body, name=name, grid=(t // tm,),
        in_specs=[tok, tok, tok, lanes, pl.BlockSpec((cpb, HALO, DN_CHUNK), lambda i: (i, 0, 0)), lanes],
        out_specs=(tok, tok, tok, tok, sq, pl.BlockSpec((cpb, HALO, LANES), lambda i: (i, 0, 0)), sq),
        out_shape=(jax.ShapeDtypeStruct((t, DN_WIDTH), BF16), jax.ShapeDtypeStruct((t, DN_WIDTH), F32),
                   jax.ShapeDtypeStruct((t, DN_WIDTH), BF16), jax.ShapeDtypeStruct((t, DN_WIDTH), BF16),
                   jax.ShapeDtypeStruct((n_chunks, DN_HEADS, DN_CHUNK, DN_CHUNK), F32),
                   jax.ShapeDtypeStruct((n_chunks, HALO, LANES), F32),
                   jax.ShapeDtypeStruct((n_chunks, DN_HEADS, DN_CHUNK, DN_CHUNK), F32)),
        compiler_params=_cparams(("parallel",)),
    )(q, k, v, gc, grow, beta)


def _delta_par_bwd(q, k, v, gc, grow, beta, inv, dw, du, dqd, dkd, dqk, degl, *, name):
    t = q.shape[0]
    tm = _tm(t)
    cpb = tm // DN_CHUNK
    n_chunks = t // DN_CHUNK
    group = 2

    def body(q_ref, k_ref, v_ref, gc_ref, gr_ref, b_ref, inv_ref, dw_ref, du_ref, dqd_ref, dkd_ref, dqk_ref, degl_ref,
             dq_ref, dk_ref, dv_ref, dgc_ref, dgr_ref, db_ref):
        def step(m, carry):
            chunks = [m * group + e for e in range(group)]
            probs = [(e, h) for e in range(group) for h in range(DN_HEADS)]
            rows = [_chunk_rows(n) for n in chunks]
            gcb, bb = [gc_ref[r, :] for r in rows], [b_ref[r, :] for r in rows]
            grb, deglb = [gr_ref[n] for n in chunks], [degl_ref[n] for n in chunks]
            for n in chunks:
                dgr_ref[n] = jnp.zeros((HALO, DN_CHUNK), F32)
            invs = [inv_ref[chunks[e], h] for e, h in probs]
            _, vjp = jax.vjp(lambda *a: _prep_fn(*a, invs)[0],
                             [q_ref[rows[e], _head_cols(h)] for e, h in probs], [k_ref[rows[e], _head_cols(h)] for e, h in probs],
                             [v_ref[rows[e], _head_cols(h)] for e, h in probs], [_lane_col(gcb[e], h) for e, h in probs],
                             [grb[e][h:h + 1, :] for e, h in probs], [_lane_col(bb[e], h) for e, h in probs])
            dq, dk, dv, dgc, dgr, db = vjp([(dw_ref[rows[e], _head_cols(h)], du_ref[rows[e], _head_cols(h)],
                                             dqd_ref[rows[e], _head_cols(h)], dkd_ref[rows[e], _head_cols(h)],
                                             dqk_ref[chunks[e], h], deglb[e][h:h + 1, :]) for e, h in probs])
            dgc_acc = [jnp.zeros((DN_CHUNK, LANES), F32) for _ in chunks]
            db_acc = [jnp.zeros((DN_CHUNK, LANES), F32) for _ in chunks]
            for i, (e, h) in enumerate(probs):
                cols = _head_cols(h)
                dq_ref[rows[e], cols] = dq[i]
                dk_ref[rows[e], cols] = dk[i]
                dv_ref[rows[e], cols] = dv[i]
                dgr_ref[chunks[e], h:h + 1, :] = dgr[i]
                dgc_acc[e] = dgc_acc[e] + _col_lane(dgc[i], h)
                db_acc[e] = db_acc[e] + _col_lane(db[i], h)
            for e in range(group):
                dgc_ref[rows[e], :] = dgc_acc[e]
                db_ref[rows[e], :] = db_acc[e]
            return carry

        lax.fori_loop(0, cpb // group, step, 0)

    row = lambda i: (i, 0)
    tok = pl.BlockSpec((tm, DN_WIDTH), row)
    lanes = pl.BlockSpec((tm, LANES), row)
    sq = pl.BlockSpec((cpb, DN_HEADS, DN_CHUNK, DN_CHUNK), lambda i: (i, 0, 0, 0))
    grs = pl.BlockSpec((cpb, HALO, DN_CHUNK), lambda i: (i, 0, 0))
    return pl.pallas_call(
        body, name=name, grid=(t // tm,),
        in_specs=[tok, tok, tok, lanes, grs, lanes, sq, tok, tok, tok, tok, sq, pl.BlockSpec((cpb, HALO, LANES), lambda i: (i, 0, 0))],
        out_specs=(tok, tok, tok, lanes, grs, lanes),
        out_shape=(jax.ShapeDtypeStruct((t, DN_WIDTH), F32),) * 3
        + (jax.ShapeDtypeStruct((t, LANES), F32), jax.ShapeDtypeStruct((n_chunks, HALO, DN_CHUNK), F32),
           jax.ShapeDtypeStruct((t, LANES), F32)),
        compiler_params=_cparams(("parallel",)),
    )(q, k, v, gc, grow, beta, inv, dw, du, dqd, dkd, dqk, degl)


def _seq_specs(n_seq, seq, reverse):
    tm = _tm(seq)
    nb = seq // tm
    cpb = tm // DN_CHUNK
    blk = (lambda b, j: b * nb + nb - 1 - j) if reverse else (lambda b, j: b * nb + j)
    tok = pl.BlockSpec((tm, DN_WIDTH), lambda b, j: (blk(b, j), 0))
    sq = pl.BlockSpec((cpb, DN_HEADS, DN_CHUNK, DN_CHUNK), lambda b, j: (blk(b, j), 0, 0, 0))
    rows8 = pl.BlockSpec((cpb, HALO, LANES), lambda b, j: (blk(b, j), 0, 0))
    state = pl.BlockSpec((cpb, DN_HEADS, DN_HEAD_DIM, DN_HEAD_DIM), lambda b, j: (blk(b, j), 0, 0, 0))
    return nb, cpb, tok, sq, rows8, state


def _delta_seq_fwd(w, u, qd, kd, qk, egl, n_seq, seq, *, name):
    nb, cpb, tok, sq, rows8, state = _seq_specs(n_seq, seq, False)
    t = n_seq * seq

    def body(w_ref, u_ref, qd_ref, kd_ref, qk_ref, egl_ref, o_ref, st_ref, s_s):
        @pl.when(pl.program_id(1) == 0)
        def _():
            s_s[...] = jnp.zeros_like(s_s)

        def step(n, carry):
            rows = _chunk_rows(n)
            heads = range(DN_HEADS)
            eglb = egl_ref[n]
            s = [s_s[h] for h in heads]
            for h in heads:
                st_ref[n, h] = s[h]
            o, s_new = _seq_fn([w_ref[rows, _head_cols(h)] for h in heads], [u_ref[rows, _head_cols(h)] for h in heads],
                               [qd_ref[rows, _head_cols(h)] for h in heads], [kd_ref[rows, _head_cols(h)] for h in heads],
                               [qk_ref[n, h] for h in heads], [eglb[h:h + 1, :] for h in heads], s)
            for h in heads:
                o_ref[rows, _head_cols(h)] = o[h]
                s_s[h] = s_new[h]
            return carry

        lax.fori_loop(0, cpb, step, 0)

    return pl.pallas_call(
        body, name=name, grid=(n_seq, nb),
        in_specs=[tok, tok, tok, tok, sq, rows8],
        out_specs=(tok, state),
        out_shape=(jax.ShapeDtypeStruct((t, DN_WIDTH), F32),
                   jax.ShapeDtypeStruct((t // DN_CHUNK, DN_HEADS, DN_HEAD_DIM, DN_HEAD_DIM), F32)),
        scratch_shapes=[pltpu.VMEM((DN_HEADS, DN_HEAD_DIM, DN_HEAD_DIM), F32)],
        compiler_params=_cparams(("parallel", "arbitrary")),
    )(w, u, qd, kd, qk, egl)


def _delta_seq_bwd(w, u, qd, kd, qk, egl, states, do, n_seq, seq, *, name):
    nb, cpb, tok, sq, rows8, state = _seq_specs(n_seq, seq, True)
    t = n_seq * seq

    def body(w_ref, u_ref, qd_ref, kd_ref, qk_ref, egl_ref, st_ref, do_ref, dw_ref, du_ref, dqd_ref, dkd_ref, dqk_ref,
             degl_ref, ds_s):
        @pl.when(pl.program_id(1) == 0)
        def _():
            ds_s[...] = jnp.zeros_like(ds_s)

        def step(m, carry):
            n = cpb - 1 - m
            rows = _chunk_rows(n)
            eglb = egl_ref[n]
            degl_ref[n] = jnp.zeros((HALO, LANES), F32)
            heads = range(DN_HEADS)
            _, vjp = jax.vjp(_seq_fn, [w_ref[rows, _head_cols(h)].astype(F32) for h in heads],
                             [u_ref[rows, _head_cols(h)] for h in heads],
                             [qd_ref[rows, _head_cols(h)].astype(F32) for h in heads],
                             [kd_ref[rows, _head_cols(h)].astype(F32) for h in heads],
                             [qk_ref[n, h] for h in heads], [eglb[h:h + 1, :] for h in heads], [st_ref[n, h] for h in heads])
            dw, du, dqd, dkd, dqk, degl, ds_in = vjp(([do_ref[rows, _head_cols(h)] for h in heads], [ds_s[h] for h in heads]))
            for h in heads:
                cols = _head_cols(h)
                dw_ref[rows, cols] = dw[h]
                du_ref[rows, cols] = du[h]
                dqd_ref[rows, cols] = dqd[h]
                dkd_ref[rows, cols] = dkd[h]
                dqk_ref[n, h] = dqk[h]
                degl_ref[n, h:h + 1, :] = degl[h]
                ds_s[h] = ds_in[h]
            return carry

        lax.fori_loop(0, cpb, step, 0)

    return pl.pallas_call(
        body, name=name, grid=(n_seq, nb),
        in_specs=[tok, tok, tok, tok, sq, rows8, state, tok],
        out_specs=(tok, tok, tok, tok, sq, rows8),
        out_shape=(jax.ShapeDtypeStruct((t, DN_WIDTH), F32),) * 4
        + (jax.ShapeDtypeStruct((t // DN_CHUNK, DN_HEADS, DN_CHUNK, DN_CHUNK), F32),
           jax.ShapeDtypeStruct((t // DN_CHUNK, HALO, LANES), F32)),
        scratch_shapes=[pltpu.VMEM((DN_HEADS, DN_HEAD_DIM, DN_HEAD_DIM), F32)],
        compiler_params=_cparams(("parallel", "arbitrary")),
    )(w, u, qd, kd, qk, egl, states, do)


def _dn_gate(o, z, dnw):
    return o * lax.rsqrt(jnp.mean(o * o, axis=-1, keepdims=True) + EPS) * dnw * _silu(z)


def _mix_out_fwd(x, sg, o, z, wo_sg, wo_dn, dnw, *, name):
    t = x.shape[0]
    tm = _tm(t)

    def body(x_ref, sg_ref, o_ref, z_ref, wsg_ref, wdn_ref, dnw_ref, y_ref, dn_s):
        for h, (oh, zh) in enumerate(zip(_split_heads(o_ref, 0), _split_heads(z_ref, 0))):
            dn_s[:, h * DN_HEAD_DIM:(h + 1) * DN_HEAD_DIM] = _dn_gate(oh, zh, dnw_ref[...]).astype(BF16)
        y_ref[...] = (x_ref[...] + jnp.dot(sg_ref[...].astype(BF16), wsg_ref[...], preferred_element_type=F32)
                      + jnp.dot(dn_s[...], wdn_ref[...], preferred_element_type=F32))

    row = lambda i: (i, 0)
    const = lambda i: (0, 0)
    half = pl.BlockSpec((tm, DN_WIDTH), row)
    return pl.pallas_call(
        body, name=name, grid=(t // tm,),
        in_specs=[pl.BlockSpec((tm, D_MODEL), row), half, half, half, pl.BlockSpec((SG_WIDTH, D_MODEL), const),
                  pl.BlockSpec((DN_WIDTH, D_MODEL), const), pl.BlockSpec((1, DN_HEAD_DIM), const)],
        out_specs=pl.BlockSpec((tm, D_MODEL), row),
        out_shape=jax.ShapeDtypeStruct((t, D_MODEL), F32),
        scratch_shapes=[pltpu.VMEM((tm, DN_WIDTH), BF16)],
        compiler_params=_cparams(("parallel",)),
    )(x, sg, o, z, wo_sg, wo_dn, dnw)


def _mix_out_bwd(dy, sg, o, z, wo_sg, wo_dn, dnw, *, name):
    t = dy.shape[0]
    tm = _tm(t)

    def body(dy_ref, sg_ref, o_ref, z_ref, wsg_ref, wdn_ref, dnw_ref, dsg_ref, do_ref, dz_ref, dwsg_ref, dwdn_ref, ddnw_ref, dn_s):
        i = pl.program_id(0)
        dyb = dy_ref[...].astype(BF16)
        nt = (((1,), (1,)), ((), ()))
        tn = (((0,), (0,)), ((), ()))
        dsg_ref[...] = lax.dot_general(dyb, wsg_ref[...], nt, preferred_element_type=F32)
        ddn = lax.dot_general(dyb, wdn_ref[...], nt, preferred_element_type=F32)
        ddnw = None
        for h, (oh, zh) in enumerate(zip(_split_heads(o_ref, 0), _split_heads(z_ref, 0))):
            cols = slice(h * DN_HEAD_DIM, (h + 1) * DN_HEAD_DIM)
            out, vjp = jax.vjp(_dn_gate, oh, zh, dnw_ref[...])
            dn_s[:, cols] = out.astype(BF16)
            doh, dzh, dw = vjp(ddn[:, cols])
            do_ref[:, cols] = doh
            dz_ref[:, cols] = dzh
            ddnw = dw if ddnw is None else ddnw + dw
        _acc_out(ddnw_ref, i == 0, ddnw)
        _acc_out(dwsg_ref, i == 0, lax.dot_general(sg_ref[...].astype(BF16), dyb, tn, preferred_element_type=F32))
        _acc_out(dwdn_ref, i == 0, lax.dot_general(dn_s[...], dyb, tn, preferred_element_type=F32))

    row = lambda i: (i, 0)
    const = lambda i: (0, 0)
    half = pl.BlockSpec((tm, DN_WIDTH), row)
    wspec = pl.BlockSpec((DN_WIDTH, D_MODEL), const)
    return pl.pallas_call(
        body, name=name, grid=(t // tm,),
        in_specs=[pl.BlockSpec((tm, D_MODEL), row), half, half, half, wspec, wspec, pl.BlockSpec((1, DN_HEAD_DIM), const)],
        out_specs=(half, half, half, wspec, wspec, pl.BlockSpec((1, DN_HEAD_DIM), const)),
        out_shape=(jax.ShapeDtypeStruct((t, DN_WIDTH), F32),) * 3 + (jax.ShapeDtypeStruct((DN_WIDTH, D_MODEL), F32),) * 2
        + (jax.ShapeDtypeStruct((1, DN_HEAD_DIM), F32),),
        scratch_shapes=[pltpu.VMEM((tm, DN_WIDTH), BF16)],
        compiler_params=_cparams(("arbitrary",)),
    )(dy, sg, o, z, wo_sg, wo_dn, dnw)


_MESH = pl.DeviceIdType.MESH
_HBM = pl.BlockSpec(memory_space=pl.ANY)


def _mesh_pos():
    x, y, c = lax.axis_index("x"), lax.axis_index("y"), lax.axis_index("c")
    return x, y, c, [(1 - x, y), (x, 1 - y), (1 - x, 1 - y)]


def _gather2(arrs, *, name):
    n = len(arrs)
    slots = N_DEV - 1

    def body(*refs):
        in_refs, out_refs = refs[:n], refs[n:2 * n]
        send_sems, recv_sems, local_sems = refs[2 * n:]
        x, y, c, chips = _mesh_pos()
        me, sibling = (x, y, c), (x, y, 1 - c)

        def copy(k, slot, block, to, src=None):
            dst = out_refs[k].at[4 * block[0] + 2 * block[1] + block[2]]
            return pltpu.make_async_remote_copy(src_ref=dst if src is None else src, dst_ref=dst,
                                                send_sem=send_sems.at[k * slots + slot], recv_sem=recv_sems.at[k * slots + slot],
                                                device_id=to, device_id_type=_MESH)

        local = [pltpu.make_async_copy(in_refs[k], out_refs[k].at[4 * x + 2 * y + c], local_sems.at[k]) for k in range(n)]
        sent = []
        for k in range(n):
            sent.append(copy(k, 0, me, sibling, src=in_refs[k]))
            sent += [copy(k, 1 + j, me, (*chip, c), src=in_refs[k]) for j, chip in enumerate(chips)]
        for cp in local + sent:
            cp.start()
        for j, chip in enumerate(chips):
            for k in range(n):
                copy(k, 1 + j, (*chip, c), me).wait_recv()
                passed = copy(k, 4 + j, (*chip, c), sibling)
                passed.start()
                sent.append(passed)
        for k in range(n):
            copy(k, 0, sibling, me).wait_recv()
            for j, chip in enumerate(chips):
                copy(k, 4 + j, (*chip, 1 - c), me).wait_recv()
        for cp in sent:
            cp.wait_send()
        for cp in local:
            cp.wait()

    return pl.pallas_call(
        body, name=name, in_specs=[_HBM] * n, out_specs=(_HBM,) * n,
        out_shape=tuple(jax.ShapeDtypeStruct((N_DEV,) + a.shape, a.dtype) for a in arrs),
        scratch_shapes=[pltpu.SemaphoreType.DMA((n * slots,)), pltpu.SemaphoreType.DMA((n * slots,)),
                        pltpu.SemaphoreType.DMA((n,))],
    )(*arrs)


def _pair_swap(arrs, *, name):
    n = len(arrs)

    def body(*refs):
        in_refs, out_refs, send_sems, recv_sems = refs[:n], refs[n:2 * n], refs[2 * n], refs[2 * n + 1]
        x, y, c, _ = _mesh_pos()
        copies = [pltpu.make_async_remote_copy(src_ref=in_refs[k].at[1 - c], dst_ref=out_refs[k], send_sem=send_sems.at[k],
                                               recv_sem=recv_sems.at[k], device_id=(x, y, 1 - c), device_id_type=_MESH)
                  for k in range(n)]
        for cp in copies:
            cp.start()
        for cp in copies:
            cp.wait()

    return pl.pallas_call(
        body, name=name, in_specs=[_HBM] * n, out_specs=(_HBM,) * n,
        out_shape=tuple(jax.ShapeDtypeStruct(a.shape[1:], a.dtype) for a in arrs),
        scratch_shapes=[pltpu.SemaphoreType.DMA((n,)), pltpu.SemaphoreType.DMA((n,))],
    )(*arrs)


def _chip_exchange(arrs, *, name):
    n = len(arrs)
    slots = 3

    def body(*refs):
        in_refs, out_refs = refs[:n], refs[n:2 * n]
        send_sems, recv_sems, local_sems = refs[2 * n:]
        x, y, c, chips = _mesh_pos()
        mine = 2 * x + y
        copies = [pltpu.make_async_copy(in_refs[k].at[mine], out_refs[k].at[mine], local_sems.at[k]) for k in range(n)]
        for j, chip in enumerate(chips):
            for k in range(n):
                copies.append(pltpu.make_async_remote_copy(
                    src_ref=in_refs[k].at[2 * chip[0] + chip[1]], dst_ref=out_refs[k].at[mine],
                    send_sem=send_sems.at[k * slots + j], recv_sem=recv_sems.at[k * slots + j],
                    device_id=(*chip, c), device_id_type=_MESH))
        for cp in copies:
            cp.start()
        for cp in copies:
            cp.wait()

    return pl.pallas_call(
        body, name=name, in_specs=[_HBM] * n, out_specs=(_HBM,) * n,
        out_shape=tuple(jax.ShapeDtypeStruct(a.shape, a.dtype) for a in arrs),
        scratch_shapes=[pltpu.SemaphoreType.DMA((n * slots,)), pltpu.SemaphoreType.DMA((n * slots,)),
                        pltpu.SemaphoreType.DMA((n,))],
    )(*arrs)


_SEM = pl.BlockSpec(memory_space=pltpu.SEMAPHORE)
_EFFECT = pltpu.SideEffectType.DATAFLOW_SIDE_EFFECTING


def _direct_copies(src_refs, land_refs, send_sems, recv_sems, gather):
    x, y, c, _ = _mesh_pos()
    me = 4 * x + 2 * y + c
    n_peer = N_DEV - 1
    copies = []
    for r in range(1, N_DEV):
        px = 1 - x if r & 4 else x
        py = 1 - y if r & 2 else y
        pc = 1 - c if r & 1 else c
        for k, (src, land) in enumerate(zip(src_refs, land_refs)):
            copies.append(pltpu.make_async_remote_copy(
                src_ref=src if gather else src.at[4 * px + 2 * py + pc], dst_ref=land.at[me],
                send_sem=send_sems.at[k * n_peer + r - 1], recv_sem=recv_sems.at[k * n_peer + r - 1],
                device_id=(px, py, pc), device_id_type=_MESH))
    return copies


def _send_start(arrs, gather, *, name):
    n = len(arrs)
    lands = [lax.empty(((N_DEV,) + a.shape) if gather else a.shape, a.dtype) for a in arrs]

    def body(*refs):
        src_refs, land_refs, send_sems, recv_sems, token = refs[:n], refs[n:2 * n], refs[2 * n], refs[2 * n + 1], refs[-1]
        for cp in _direct_copies(src_refs, land_refs, send_sems, recv_sems, gather):
            cp.start()
        token[...] = jnp.zeros_like(token)

    n_sem = n * (N_DEV - 1)
    bufs = list(arrs) + lands
    out = pl.pallas_call(
        body, name=name,
        out_shape=(pltpu.SemaphoreType.DMA((n_sem,)), pltpu.SemaphoreType.DMA((n_sem,)))
        + tuple(pltpu.HBM(b.shape, b.dtype) for b in bufs) + (jax.ShapeDtypeStruct((HALO, LANES), F32),),
        in_specs=[_HBM] * (2 * n), out_specs=(_SEM, _SEM) + (_HBM,) * (2 * n) + (pl.BlockSpec(memory_space=pltpu.VMEM),),
        input_output_aliases={i: 2 + i for i in range(2 * n)},
        compiler_params=pltpu.CompilerParams(has_side_effects=_EFFECT),
    )(*[pltpu.with_memory_space_constraint(b, pltpu.HBM) for b in bufs])
    return (out[0], out[1], list(out[2:2 + n]), list(out[2 + n:2 + 2 * n])), out[-1]


def _send_wait(started, gather, after, *, name):
    send_sems, recv_sems, srcs, lands = started
    n = len(srcs)

    def body(*refs):
        src_refs, land_refs, send_ref, recv_ref = refs[:n], refs[n:2 * n], refs[2 * n], refs[2 * n + 1]
        for cp in _direct_copies(src_refs, land_refs, send_ref, recv_ref, gather):
            cp.wait_send()
            cp.wait_recv()

    bufs = srcs + lands
    out = pl.pallas_call(
        body, name=name, out_shape=tuple(pltpu.HBM(b.shape, b.dtype) for b in bufs),
        in_specs=[_HBM] * (2 * n) + [_SEM, _SEM, _HBM], out_specs=(_HBM,) * (2 * n),
        input_output_aliases={i: i for i in range(2 * n)},
        compiler_params=pltpu.CompilerParams(has_side_effects=_EFFECT),
    )(*bufs, send_sems, recv_sems, after)
    return list(out[n:])


def _pair_add(p, r, core, *, name):
    _, n_chip, rows, cols = p.shape
    rb = _row_block(rows)

    def body(core_ref, p_ref, r_ref, o_ref):
        o_ref[...] = (p_ref[...] + r_ref[...]).astype(BF16)

    return pl.pallas_call(
        body, name=name,
        grid_spec=pltpu.PrefetchScalarGridSpec(
            num_scalar_prefetch=1, grid=(n_chip, rows // rb),
            in_specs=[pl.BlockSpec((None, None, rb, cols), lambda s, i, core_ref: (core_ref[0], s, i, 0)),
                      pl.BlockSpec((None, rb, cols), lambda s, i, core_ref: (s, i, 0))],
            out_specs=pl.BlockSpec((None, rb, cols), lambda s, i, core_ref: (s, i, 0))),
        out_shape=jax.ShapeDtypeStruct((n_chip, rows, cols), BF16),
        compiler_params=_cparams(("parallel", "parallel")),
    )(core, p, r)


def _row_block(rows, limit=256):
    best = rows
    for cand in range(8, limit + 1, 8):
        if rows % cand == 0:
            best = cand
    return best if rows > limit else rows


def _adam(gp, w, m, v, *, name):
    p, rows, cols = gp.shape
    rb = _row_block(rows)

    def body(gp_ref, w_ref, m_ref, v_ref, g_ref, d_ref, m2_ref, v2_ref):
        g = gp_ref[0].astype(F32)
        for s in range(1, p):
            g = g + gp_ref[s].astype(F32)
        m2 = ADAM_B1 * m_ref[...] + (1.0 - ADAM_B1) * g
        v2 = ADAM_B2 * v_ref[...] + (1.0 - ADAM_B2) * (g * g)
        m_hat = m2 / (1.0 - ADAM_B1 ** ADAM_STEP)
        v_hat = v2 / (1.0 - ADAM_B2 ** ADAM_STEP)
        g_ref[...] = g
        d_ref[...] = -ADAM_LR * (m_hat / (jnp.sqrt(v_hat) + ADAM_EPS) + ADAM_WD * w_ref[...])
        m2_ref[...] = m2
        v2_ref[...] = v2

    blk = pl.BlockSpec((rb, cols), lambda i: (i, 0))
    return pl.pallas_call(
        body, name=name, grid=(rows // rb,),
        in_specs=[pl.BlockSpec((p, rb, cols), lambda i: (0, i, 0)), blk, blk, blk],
        out_specs=(blk,) * 4, out_shape=(jax.ShapeDtypeStruct((rows, cols), F32),) * 4,
        compiler_params=_cparams(("parallel",)),
    )(gp, w, m, v)


def _cols_full(g):
    return jnp.transpose(g, (1, 0, 2)).reshape(g.shape[1], N_DEV * g.shape[2])


def _cols_pieces(full):
    r, c = full.shape
    return jnp.transpose(full.reshape(r, N_DEV, c // N_DEV), (1, 0, 2))


def _pad_lanes(a, width=LANES):
    return jnp.pad(a, ((0, 0), (0, width - a.shape[1])))


def _chunk_rows_of(a):
    by_chunk = jnp.transpose(a[:, :DN_HEADS].reshape(-1, DN_CHUNK, DN_HEADS), (0, 2, 1))
    return jnp.pad(by_chunk, ((0, 0), (0, HALO - DN_HEADS), (0, 0)))


_SMALL = (("ffn1_norm", D_MODEL), ("mix_norm", D_MODEL), ("ffn2_norm", D_MODEL), ("final_norm", D_MODEL), ("a_log", DN_HEADS),
          ("dt_bias", DN_HEADS), ("dn_norm", DN_HEAD_DIM), ("sg_ln_g", SG_WIDTH), ("sg_ln_b", SG_WIDTH),
          ("sg_w", SG_GROUPS * SG_CHUNK * SG_CHUNK), ("sg_b", SG_GROUPS * SG_CHUNK), ("conv_w", CONV_K * 3 * DN_WIDTH))
_SMALL_ROWS = 1128
_SMALL_SHAPES = {"ffn1_norm": (1, D_MODEL), "mix_norm": (1, D_MODEL), "ffn2_norm": (1, D_MODEL), "final_norm": (D_MODEL,),
                 "a_log": (1, DN_HEADS), "dt_bias": (1, DN_HEADS), "dn_norm": (1, DN_HEAD_DIM), "sg_ln_g": (1, SG_WIDTH),
                 "sg_ln_b": (1, SG_WIDTH), "sg_w": (1, SG_GROUPS, SG_CHUNK, SG_CHUNK), "sg_b": (1, SG_GROUPS, SG_CHUNK)}


def _pack_small(d):
    flat = jnp.concatenate([d[name].reshape(-1) for name, _ in _SMALL])
    return jnp.pad(flat, (0, _SMALL_ROWS * LANES - flat.shape[0])).reshape(_SMALL_ROWS, LANES)


def _unpack_small(a):
    flat, out, at = a.reshape(-1), {}, 0
    for name, size in _SMALL:
        out[name] = flat[at:at + size]
        at += size
    return out


def kernel(x, ffn1_norm, ffn1_w_gate, ffn1_w_up, ffn1_w_down, mix_norm, w_in, conv_w, a_log, dt_bias, dn_norm, sg_ln_g, sg_ln_b, sg_w, sg_b, w_out, ffn2_norm, ffn2_w_gate, ffn2_w_up, ffn2_w_down, final_norm, loss_target, m_ffn1_norm, m_ffn1_w_gate, m_ffn1_w_up, m_ffn1_w_down, m_mix_norm, m_w_in, m_conv_w, m_a_log, m_dt_bias, m_dn_norm, m_sg_ln_g, m_sg_ln_b, m_sg_w, m_sg_b, m_w_out, m_ffn2_norm, m_ffn2_w_gate, m_ffn2_w_up, m_ffn2_w_down, m_final_norm, v_ffn1_norm, v_ffn1_w_gate, v_ffn1_w_up, v_ffn1_w_down, v_mix_norm, v_w_in, v_conv_w, v_a_log, v_dt_bias, v_dn_norm, v_sg_ln_g, v_sg_ln_b, v_sg_w, v_sg_b, v_w_out, v_ffn2_norm, v_ffn2_w_gate, v_ffn2_w_up, v_ffn2_w_down, v_final_norm):
    weights = dict(ffn1_norm=ffn1_norm, ffn1_w_gate=ffn1_w_gate, ffn1_w_up=ffn1_w_up, ffn1_w_down=ffn1_w_down, mix_norm=mix_norm, w_in=w_in, conv_w=conv_w, a_log=a_log, dt_bias=dt_bias, dn_norm=dn_norm, sg_ln_g=sg_ln_g, sg_ln_b=sg_ln_b, sg_w=sg_w, sg_b=sg_b, w_out=w_out, ffn2_norm=ffn2_norm, ffn2_w_gate=ffn2_w_gate, ffn2_w_up=ffn2_w_up, ffn2_w_down=ffn2_w_down, final_norm=final_norm)
    mom_m = dict(ffn1_norm=m_ffn1_norm, ffn1_w_gate=m_ffn1_w_gate, ffn1_w_up=m_ffn1_w_up, ffn1_w_down=m_ffn1_w_down, mix_norm=m_mix_norm, w_in=m_w_in, conv_w=m_conv_w, a_log=m_a_log, dt_bias=m_dt_bias, dn_norm=m_dn_norm, sg_ln_g=m_sg_ln_g, sg_ln_b=m_sg_ln_b, sg_w=m_sg_w, sg_b=m_sg_b, w_out=m_w_out, ffn2_norm=m_ffn2_norm, ffn2_w_gate=m_ffn2_w_gate, ffn2_w_up=m_ffn2_w_up, ffn2_w_down=m_ffn2_w_down, final_norm=m_final_norm)
    mom_v = dict(ffn1_norm=v_ffn1_norm, ffn1_w_gate=v_ffn1_w_gate, ffn1_w_up=v_ffn1_w_up, ffn1_w_down=v_ffn1_w_down, mix_norm=v_mix_norm, w_in=v_w_in, conv_w=v_conv_w, a_log=v_a_log, dt_bias=v_dt_bias, dn_norm=v_dn_norm, sg_ln_g=v_sg_ln_g, sg_ln_b=v_sg_ln_b, sg_w=v_sg_w, sg_b=v_sg_b, w_out=v_w_out, ffn2_norm=v_ffn2_norm, ffn2_w_gate=v_ffn2_w_gate, ffn2_w_up=v_ffn2_w_up, ffn2_w_down=v_ffn2_w_down, final_norm=v_final_norm)
    order = list(weights)
    big = ("ffn1_w_gate", "ffn1_w_up", "ffn1_w_down", "w_in", "w_out", "ffn2_w_gate", "ffn2_w_up", "ffn2_w_down")
    col_sharded = ("ffn1_w_gate", "ffn1_w_up", "w_in", "ffn2_w_gate", "ffn2_w_up")

    n_seq, seq, _ = x.shape
    t = n_seq * seq
    me = 4 * lax.axis_index("x") + 2 * lax.axis_index("y") + lax.axis_index("c")
    x0 = x.reshape(t, D_MODEL)
    tgt = loss_target.reshape(t, D_MODEL)

    def fill_own(land, own_block):
        return lax.dynamic_update_index_in_dim(land, own_block, me, 0)

    def as_full(n, g):
        return _cols_full(g) if n in col_sharded else g.reshape(-1, g.shape[-1])

    shards = {n: weights[n][0].astype(BF16) for n in big}
    ffn1_names, mix_names, ffn2_names = big[:3], big[3:5], big[5:]
    full = {n: as_full(n, g) for n, g in zip(ffn1_names, _gather2([shards[n] for n in ffn1_names], name="gather_ffn1"))}
    mix_srcs = [shards[n] for n in mix_names] + [conv_w[0]]
    mix_started, mix_token = _send_start(mix_srcs, True, name="gather_mix_start")
    ffn2_started, ffn2_token = _send_start([shards[n] for n in ffn2_names], True, name="gather_ffn2_start")
    ffn1_norm_fwd = ffn1_norm + (mix_token[:1, :1] + ffn2_token[:1, :1])
    alog, dtb = _pad_lanes(a_log), _pad_lanes(dt_bias)
    sgbt = _pad_lanes(sg_b[0].T)
    fnw = final_norm.reshape(1, D_MODEL)

    x1, h1, g1, u1 = _ffn_fwd(x0, ffn1_norm_fwd, full["ffn1_w_gate"], full["ffn1_w_up"], full["ffn1_w_down"], name="ffn1_fwd")
    mix_lands = [fill_own(land, src) for land, src in zip(_send_wait(mix_started, True, x1, name="gather_mix_wait"), mix_srcs)]
    full.update({n: as_full(n, g) for n, g in zip(mix_names, mix_lands)})
    conv_full = _cols_full(mix_lands[-1])
    w_in_f = full["w_in"]
    offs = (0, SG_WIDTH, 2 * SG_WIDTH, 2 * SG_WIDTH + 3 * DN_WIDTH, 2 * SG_WIDTH + 4 * DN_WIDTH)
    n_proj = offs[-1]
    ws = [w_in_f[:, offs[0]:offs[1]], w_in_f[:, offs[1]:offs[2]], w_in_f[:, offs[2]:offs[3]], w_in_f[:, offs[3]:offs[4]],
          _pad_lanes(w_in_f[:, n_proj:n_proj + DN_HEADS]), _pad_lanes(w_in_f[:, n_proj + DN_HEADS:n_proj + 2 * DN_HEADS])]
    wo_sg, wo_dn = full["w_out"][:SG_WIDTH], full["w_out"][SG_WIDTH:]
    u, v, qkv, z, bpre, apre = _mix_in_fwd(x1, mix_norm, ws, name="mix_in_fwd")
    sg_out = _sg_fwd(u, v, sg_ln_g, sg_ln_b, sg_w[0], sgbt, name="sg_fwd")
    q, k, vv, beta, gc = _dn_prep_fwd(qkv, bpre, apre, conv_full, alog, dtb, seq, name="dn_prep_fwd")
    grow = _chunk_rows_of(gc)
    wy_w, wy_u, q_dec, k_dec, qk, egl, inv = _delta_prep(q, k, vv, gc, grow, beta, name="delta_prep")
    o, states = _delta_seq_fwd(wy_w, wy_u, q_dec, k_dec, qk, egl, n_seq, seq, name="delta_seq_fwd")
    x2 = _mix_out_fwd(x1, sg_out, o, z, wo_sg, wo_dn, dn_norm, name="mix_out_fwd")
    ffn2_lands = _send_wait(ffn2_started, True, x2, name="gather_ffn2_wait")
    full.update({n: as_full(n, fill_own(land, shards[n])) for n, land in zip(ffn2_names, ffn2_lands)})
    dx3, loss_part, d_fn, h2, g2, u2 = _ffn_fwd(x2, ffn2_norm, full["ffn2_w_gate"], full["ffn2_w_up"], full["ffn2_w_down"],
                                                tgt, fnw, name="ffn2_fwd_loss")
    loss = lax.psum(loss_part[0, 0], ("x", "y", "c"))

    dx2, d_n2, d_g2, d_u2, d_d2 = _ffn_bwd(x2, ffn2_norm, h2, g2, u2, full["ffn2_w_gate"], full["ffn2_w_up"],
                                           full["ffn2_w_down"], dx3, name="ffn2_bwd")
    ff_cols = _cols_pieces

    def ff_rows(d_wd):
        return d_wd.reshape(N_DEV, D_FF // N_DEV, D_MODEL)

    ffn2_pieces = [p.astype(BF16) for p in (ff_cols(d_g2), ff_cols(d_u2), ff_rows(d_d2))]
    ffn2_sent, sent_token = _send_start(ffn2_pieces, False, name="grads_ffn2_start")
    dsg, do, dz, d_wo_sg, d_wo_dn, d_dnw = _mix_out_bwd(dx2, sg_out, o, z, wo_sg, wo_dn, dn_norm + sent_token[:1, :1],
                                                        name="mix_out_bwd")
    d_seq = _delta_seq_bwd(wy_w, wy_u, q_dec, k_dec, qk, egl, states, do, n_seq, seq, name="delta_seq_bwd")
    dq, dk, dv, dgc_a, dgrow, dbeta = _delta_par_bwd(q, k, vv, gc, grow, beta, inv, *d_seq, name="delta_par_bwd")
    dgc_b = _pad_lanes(jnp.transpose(dgrow[:, :DN_HEADS, :], (0, 2, 1)).reshape(t, DN_HEADS))
    dy_conv, dbpre, dapre, d_alog, d_dtb = _dn_prep_bwd(qkv, bpre, apre, conv_full, alog, dtb, dq, dk, dv, dbeta, dgc_a, dgc_b,
                                                        seq, name="dn_prep_bwd")
    dqkv, d_conv = _conv_bwd(qkv, dy_conv, conv_full, seq, name="conv_bwd")
    du, dvv, d_lng, d_lnb, d_wc, d_sgbt = _sg_bwd(u, v, sg_ln_g, sg_ln_b, sg_w[0], sgbt, dsg, name="sg_bwd")
    dx1, d_mixn, d_ws = _split3(_mix_in_bwd(x1, mix_norm, ws, dx2, (du, dvv, dqkv, dz, dbpre, dapre), name="mix_in_bwd"))
    d_w_in = jnp.concatenate([d_ws[0], d_ws[1], d_ws[2], d_ws[3], d_ws[4][:, :DN_HEADS], d_ws[5][:, :DN_HEADS]], axis=1)
    d_w_out = jnp.concatenate([d_wo_sg, d_wo_dn], axis=0)
    mix_pieces = [_cols_pieces(d_w_in).astype(BF16), d_w_out.reshape(N_DEV, D_MODEL // N_DEV, D_MODEL).astype(BF16)]
    mix_sent, sent_token = _send_start(mix_pieces, False, name="grads_mix_start")
    grad_x, d_n1, dg1, du1, a1, dyh1 = _ffn_bwd_x(x0, ffn1_norm + sent_token[:1, :1], g1, u1, full["ffn1_w_gate"],
                                                  full["ffn1_w_up"], full["ffn1_w_down"], dx1, name="ffn1_bwd_x")
    small_grads = dict(ffn1_norm=d_n1, mix_norm=d_mixn, ffn2_norm=d_n2, final_norm=d_fn, a_log=d_alog[:, :DN_HEADS],
                       dt_bias=d_dtb[:, :DN_HEADS], dn_norm=d_dnw, sg_ln_g=d_lng, sg_ln_b=d_lnb, sg_w=d_wc,
                       sg_b=d_sgbt[:, :SG_GROUPS].T, conv_w=d_conv[:CONV_K])
    small_src = _pack_small(small_grads)
    small_sent, small_token = _send_start([small_src], True, name="small_grads_start")
    late = []

    def send_early(k, grad):
        if k == 2:
            return None
        piece = ff_cols(grad).astype(BF16)
        sent, token = _send_start([piece], False, name="grads_" + ffn1_names[k] + "_start")
        late.append(((ffn1_names[k],), sent, [piece]))
        return token

    _, _, d_d1 = _ffn_wgrads(h1, dg1, du1, a1, dyh1, send_early, small_token, name="ffn1_bwd")

    def by_core(p8):
        return jnp.moveaxis(p8.reshape((4, 2) + p8.shape[1:]), 1, 0)

    own = [by_core(ff_rows(d_d1))]
    from_sibling = _pair_swap(own, name="grads_to_sibling")
    core = lax.axis_index("c").astype(jnp.int32).reshape(1)
    chip_sums = [_pair_add(own[0], from_sibling[0], core, name="pair_add_" + ffn1_names[2])]
    received = {ffn1_names[2]: _chip_exchange(chip_sums, name="grads_to_owner")[0]}
    for names, sent, pieces in [(ffn2_names, ffn2_sent, ffn2_pieces), (mix_names, mix_sent, mix_pieces)] + late:
        lands = _send_wait(sent, False, received[ffn1_names[2]], name="grads_" + names[0] + "_wait")
        received.update({n: fill_own(land, lax.dynamic_index_in_dim(p, me, 0, keepdims=False))
                         for n, land, p in zip(names, lands, pieces)})
    res = {n: _adam(received[n], weights[n][0], mom_m[n][0], mom_v[n][0], name="adam_" + n) for n in big}

    (small_land,) = _send_wait(small_sent, True, received[ffn1_names[2]], name="small_grads_wait")
    small_parts = fill_own(small_land, small_src)
    zeros_conv = jnp.zeros((CONV_K * 3 * DN_WIDTH,), F32)
    packed = [_pack_small({**{n: src[n] for n, _ in _SMALL if n != "conv_w"}, "conv_w": zeros_conv})
              for src in (weights, mom_m, mom_v)]
    small_res = [_unpack_small(a) for a in _adam(small_parts, *packed, name="adam_small")]
    conv_grad = lax.dynamic_slice_in_dim(small_res[0]["conv_w"].reshape(CONV_K, 3 * DN_WIDTH), me * (3 * DN_WIDTH // N_DEV),
                                         3 * DN_WIDTH // N_DEV, axis=1)
    res["conv_w"] = _adam(conv_grad[None], conv_w[0], m_conv_w[0], v_conv_w[0], name="adam_conv_w")

    outs = [[], [], [], []]
    for n in order:
        for kind in range(4):
            if n in res:
                outs[kind].append(res[n][kind][None])
            else:
                outs[kind].append(small_res[kind][n].reshape(_SMALL_SHAPES[n]))
    return (loss, grad_x.reshape(x.shape), *outs[0], *outs[1], *outs[2], *outs[3])


def _split3(r):
    return r[0], r[1], r[2:]
```

```python
import functools

import jax
import jax.numpy as jnp
from jax import lax
from jax.experimental import pallas as pl
from jax.experimental.pallas import tpu as pltpu

F32 = jnp.float32
BF16 = jnp.bfloat16

D_MODEL = 1024
D_FF = 2816
SG_WIDTH = 512
SG_GROUPS = 8
SG_GROUP_DIM = 64
SG_CHUNK = 128
DN_WIDTH = 512
DN_HEAD_DIM = 128
DN_HEADS = 4
DN_CHUNK = 64
CONV_K = 4
EPS = 1e-6
N_DEV = 8
LANES = 128
HALO = 8

ADAM_LR = 0.001
ADAM_B1 = 0.9
ADAM_B2 = 0.999
ADAM_EPS = 1e-08
ADAM_WD = 0.01
ADAM_STEP = 10

VMEM_LIMIT = 60 * 1024 * 1024
TOKEN_BLOCK = 512
FF_BLOCK_FWD = 1408

_HI = lax.Precision.HIGHEST


def _cparams(sem):
    return pltpu.CompilerParams(dimension_semantics=sem, vmem_limit_bytes=VMEM_LIMIT)


def _tm(t, pref=TOKEN_BLOCK):
    return min(pref, t)


def _dg(a, b, ca, cb, precision):
    if precision is not None:
        return lax.dot_general(a, b, (((ca,), (cb,)), ((), ())), precision=precision, preferred_element_type=F32)
    return lax.dot_general(a.astype(BF16), b.astype(BF16), (((ca,), (cb,)), ((), ())), preferred_element_type=F32)


def _make_mm(exact):
    @jax.custom_vjp
    def mm(a, b):
        return _dg(a, b, 1, 0, exact)

    @jax.custom_vjp
    def mm_nt(a, b):
        return _dg(a, b, 1, 1, exact)

    @jax.custom_vjp
    def mm_tn(a, b):
        return _dg(a, b, 0, 0, exact)

    mm.defvjp(lambda a, b: (mm(a, b), (a, b)), lambda r, g: (mm_nt(g, r[1]), mm_tn(r[0], g)))
    mm_nt.defvjp(lambda a, b: (mm_nt(a, b), (a, b)), lambda r, g: (mm(g, r[1]), mm_tn(g, r[0])))
    mm_tn.defvjp(lambda a, b: (mm_tn(a, b), (a, b)), lambda r, g: (mm_nt(r[1], g), mm(r[0], g)))
    return mm, mm_nt, mm_tn


mm, mm_nt, mm_tn = _make_mm(None)
mmx, mmx_nt, mmx_tn = _make_mm(_HI)
mmh, mmh_nt, mmh_tn = _make_mm(lax.Precision.HIGH)


def _sigmoid(x):
    return 1.0 / (1.0 + jnp.exp(-x))


def _silu(x):
    return x * _sigmoid(x)


def _softplus(x):
    neg_abs = jnp.where(x > 0, -x, x)
    return jnp.where(x > 0, x, 0.0) + jnp.log(1.0 + jnp.exp(neg_abs))


def _gelu(x):
    return 0.5 * x * (1.0 + jnp.tanh(0.7978845608028654 * (x + 0.044715 * (x * x * x))))


def _rms_fwd(x, g):
    r = lax.rsqrt(jnp.mean(x * x, axis=-1, keepdims=True) + EPS)
    xh = x * r
    return xh * g, xh, r


def _rms_bwd(dh, xh, r, g):
    dxh = dh * g
    dx = r * (dxh - xh * jnp.mean(dxh * xh, axis=-1, keepdims=True))
    return dx, jnp.sum(dh * xh, axis=0, keepdims=True)


def _acc_out(ref, first, val):
    @pl.when(first)
    def _():
        ref[...] = val

    @pl.when(jnp.logical_not(first))
    def _():
        ref[...] += val


def _ffn_fwd(x, nw, wg, wu, wd, tgt=None, fnw=None, *, name):
    t = x.shape[0]
    tm, fb = _tm(t), FF_BLOCK_FWD
    n_t, n_f = t // tm, D_FF // fb
    with_loss = tgt is not None

    def body(*refs):
        if with_loss:
            (x_ref, nw_ref, wg_ref, wu_ref, wd_ref, tgt_ref, fnw_ref, dy_ref, loss_ref, dfn_ref, h_ref, g_ref, u_ref,
             acc_s) = refs
        else:
            x_ref, nw_ref, wg_ref, wu_ref, wd_ref, y_ref, h_ref, g_ref, u_ref, acc_s = refs
        i, j = pl.program_id(0), pl.program_id(1)

        @pl.when(j == 0)
        def _():
            h, _, _ = _rms_fwd(x_ref[...], nw_ref[...])
            h_ref[...] = h.astype(BF16)
            acc_s[...] = jnp.zeros_like(acc_s)

        h = h_ref[...]
        g = jnp.dot(h, wg_ref[...], preferred_element_type=F32)
        u = jnp.dot(h, wu_ref[...], preferred_element_type=F32)
        g_ref[...] = g.astype(BF16)
        u_ref[...] = u.astype(BF16)
        a = _silu(g) * u
        acc_s[...] += jnp.dot(a.astype(BF16), wd_ref[...], preferred_element_type=F32)

        @pl.when(j == n_f - 1)
        def _():
            y = x_ref[...] + 0.5 * acc_s[...]
            if not with_loss:
                y_ref[...] = y
            else:
                gf = fnw_ref[...]
                out, xh, r = _rms_fwd(y, gf)
                err = out - tgt_ref[...]
                part = 0.5 * jnp.sum(jnp.mean(err * err, axis=-1, keepdims=True), axis=0, keepdims=True)
                d_out = err * (1.0 / D_MODEL)
                dy, dgf = _rms_bwd(d_out, xh, r, gf)
                dy_ref[...] = dy
                _acc_out(loss_ref, i == 0, jnp.broadcast_to(part, loss_ref.shape))
                _acc_out(dfn_ref, i == 0, dgf)

    row = lambda i, j: (i, 0)
    const = lambda i, j: (0, 0)
    in_specs = [
        pl.BlockSpec((tm, D_MODEL), row),
        pl.BlockSpec((1, D_MODEL), const),
        pl.BlockSpec((D_MODEL, fb), lambda i, j: (0, j)),
        pl.BlockSpec((D_MODEL, fb), lambda i, j: (0, j)),
        pl.BlockSpec((fb, D_MODEL), lambda i, j: (j, 0)),
    ]
    args = [x, nw, wg, wu, wd]
    saved_shape = (jax.ShapeDtypeStruct((t, D_MODEL), BF16), jax.ShapeDtypeStruct((t, D_FF), BF16),
                   jax.ShapeDtypeStruct((t, D_FF), BF16))
    saved_specs = (pl.BlockSpec((tm, D_MODEL), row), pl.BlockSpec((tm, fb), lambda i, j: (i, j)),
                   pl.BlockSpec((tm, fb), lambda i, j: (i, j)))
    if with_loss:
        in_specs += [pl.BlockSpec((tm, D_MODEL), row), pl.BlockSpec((1, D_MODEL), const)]
        args += [tgt, fnw]
        out_shape = (jax.ShapeDtypeStruct((t, D_MODEL), F32), jax.ShapeDtypeStruct((8, LANES), F32),
                     jax.ShapeDtypeStruct((1, D_MODEL), F32)) + saved_shape
        out_specs = (pl.BlockSpec((tm, D_MODEL), row), pl.BlockSpec((8, LANES), const),
                     pl.BlockSpec((1, D_MODEL), const)) + saved_specs
        sem = ("arbitrary", "arbitrary")
    else:
        out_shape = (jax.ShapeDtypeStruct((t, D_MODEL), F32),) + saved_shape
        out_specs = (pl.BlockSpec((tm, D_MODEL), row),) + saved_specs
        sem = ("parallel", "arbitrary")
    return pl.pallas_call(
        body, name=name, grid=(n_t, n_f), in_specs=in_specs, out_specs=out_specs, out_shape=out_shape,
        scratch_shapes=[pltpu.VMEM((tm, D_MODEL), F32)],
        compiler_params=_cparams(sem),
    )(*args)


def _ffn_bwd_x(x, nw, g, u, wg, wu, wd, dy, *, name):
    t = x.shape[0]
    tm = _tm(t, 256)

    def body(x_ref, nw_ref, g_ref, u_ref, wg_ref, wu_ref, wd_ref, dy_ref, dx_ref, dnw_ref, dg_ref, du_ref, a_ref, dyh_ref):
        i = pl.program_id(0)
        nt = (((1,), (1,)), ((), ()))
        dy = dy_ref[...]
        dyh = (0.5 * dy).astype(BF16)
        dyh_ref[...] = dyh
        gate, up = g_ref[...].astype(F32), u_ref[...].astype(F32)
        s = _sigmoid(gate)
        gs = gate * s
        da = lax.dot_general(dyh, wd_ref[...], nt, preferred_element_type=F32)
        dg = (da * up * (s + gs * (1.0 - s))).astype(BF16)
        du = (da * gs).astype(BF16)
        dg_ref[...] = dg
        du_ref[...] = du
        a_ref[...] = (gs * up).astype(BF16)
        dh = (lax.dot_general(dg, wg_ref[...], nt, preferred_element_type=F32)
              + lax.dot_general(du, wu_ref[...], nt, preferred_element_type=F32))
        xv = x_ref[...]
        r = lax.rsqrt(jnp.mean(xv * xv, axis=-1, keepdims=True) + EPS)
        dx, dnw = _rms_bwd(dh, xv * r, r, nw_ref[...])
        dx_ref[...] = dy + dx
        _acc_out(dnw_ref, i == 0, dnw)

    row = lambda i: (i, 0)
    const = lambda i: (0, 0)
    once = pl.Buffered(1)
    wide = pl.BlockSpec((tm, D_FF), row)
    return pl.pallas_call(
        body, name=name, grid=(t // tm,),
        in_specs=[pl.BlockSpec((tm, D_MODEL), row), pl.BlockSpec((1, D_MODEL), const), wide, wide,
                  pl.BlockSpec((D_MODEL, D_FF), const, pipeline_mode=once), pl.BlockSpec((D_MODEL, D_FF), const, pipeline_mode=once),
                  pl.BlockSpec((D_FF, D_MODEL), const, pipeline_mode=once), pl.BlockSpec((tm, D_MODEL), row)],
        out_specs=(pl.BlockSpec((tm, D_MODEL), row), pl.BlockSpec((1, D_MODEL), const), wide, wide, wide,
                   pl.BlockSpec((tm, D_MODEL), row)),
        out_shape=(jax.ShapeDtypeStruct((t, D_MODEL), F32), jax.ShapeDtypeStruct((1, D_MODEL), F32),
                   jax.ShapeDtypeStruct((t, D_FF), BF16), jax.ShapeDtypeStruct((t, D_FF), BF16),
                   jax.ShapeDtypeStruct((t, D_FF), BF16), jax.ShapeDtypeStruct((t, D_MODEL), BF16)),
        compiler_params=_cparams(("arbitrary",)),
    )(x, nw, g, u, wg, wu, wd, dy)


def _wgrad(a, b, bm, bn, after=None, *, name):
    k, m = a.shape
    n = b.shape[1]
    tk = _tm(k, 2048)
    n_k = k // tk

    def body(a_ref, b_ref, *rest):
        o_ref, acc_s = rest[-2], rest[-1]
        s = pl.program_id(2)
        part = lax.dot_general(a_ref[...], b_ref[...], (((0,), (0,)), ((), ())), preferred_element_type=F32)
        _acc_out(acc_s, s == 0, part)

        @pl.when(s == n_k - 1)
        def _():
            o_ref[...] = acc_s[...].astype(BF16)

    return pl.pallas_call(
        body, name=name, grid=(m // bm, n // bn, n_k),
        in_specs=[pl.BlockSpec((tk, bm), lambda i, j, s: (s, i)), pl.BlockSpec((tk, bn), lambda i, j, s: (s, j))]
        + ([] if after is None else [_HBM]),
        out_specs=pl.BlockSpec((bm, bn), lambda i, j, s: (i, j)),
        out_shape=jax.ShapeDtypeStruct((m, n), BF16),
        scratch_shapes=[pltpu.VMEM((bm, bn), F32)],
        compiler_params=_cparams(("parallel", "parallel", "arbitrary")),
    )(a, b, *([] if after is None else [after]))


def _ffn_wgrads(h, dg, du, a, dyh, between=None, after=None, *, name):
    half = D_FF // 2
    grads = []
    for k, (lhs, rhs, bm, bn, tag) in enumerate(((h, dg, D_MODEL, half, "_wg"), (h, du, D_MODEL, half, "_wu"),
                                                 (a, dyh, half, D_MODEL, "_wd"))):
        grads.append(_wgrad(lhs, rhs, bm, bn, after, name=name + tag))
        after = None if between is None else between(k, grads[-1])
    return grads


def _ffn_bwd(x, nw, h, g, u, wg, wu, wd, dy, *, name):
    dx, dnw, dg, du, a, dyh = _ffn_bwd_x(x, nw, g, u, wg, wu, wd, dy, name=name + "_x")
    return (dx, dnw, *_ffn_wgrads(h, dg, du, a, dyh, name=name))


_PROJ_WIDTHS = (SG_WIDTH, SG_WIDTH, 3 * DN_WIDTH, DN_WIDTH, LANES, LANES)


def _mix_in_fwd(x, nw, ws, *, name):
    t = x.shape[0]
    tm = _tm(t)

    def body(x_ref, nw_ref, *refs):
        w_refs, o_refs = refs[:6], refs[6:]
        h, _, _ = _rms_fwd(x_ref[...], nw_ref[...])
        h = h.astype(BF16)
        for w_ref, o_ref in zip(w_refs, o_refs):
            o_ref[...] = jnp.dot(h, w_ref[...], preferred_element_type=F32)

    row = lambda i: (i, 0)
    const = lambda i: (0, 0)
    return pl.pallas_call(
        body, name=name, grid=(t // tm,),
        in_specs=[pl.BlockSpec((tm, D_MODEL), row), pl.BlockSpec((1, D_MODEL), const)]
        + [pl.BlockSpec((D_MODEL, n), const) for n in _PROJ_WIDTHS],
        out_specs=tuple(pl.BlockSpec((tm, n), row) for n in _PROJ_WIDTHS),
        out_shape=tuple(jax.ShapeDtypeStruct((t, n), F32) for n in _PROJ_WIDTHS),
        compiler_params=_cparams(("parallel",)),
    )(x, nw, *ws)


_PROJ_TOTAL = sum(_PROJ_WIDTHS)
_PROJ_OFFSETS = tuple(sum(_PROJ_WIDTHS[:k]) for k in range(len(_PROJ_WIDTHS)))


def _mix_in_bwd(x, nw, ws, dres, dps, *, name):
    t = x.shape[0]
    tm = _tm(t, 256)

    def body(x_ref, nw_ref, dres_ref, *refs):
        w_refs, dp_refs, dx_ref, dnw_ref, h_ref, dpb_ref = refs[:6], refs[6:12], refs[12], refs[13], refs[14], refs[15]
        i = pl.program_id(0)
        hf, xh, r = _rms_fwd(x_ref[...], nw_ref[...])
        h_ref[...] = hf.astype(BF16)
        dh = jnp.zeros((tm, D_MODEL), F32)
        for w_ref, dp_ref, off, width in zip(w_refs, dp_refs, _PROJ_OFFSETS, _PROJ_WIDTHS):
            dp = dp_ref[...].astype(BF16)
            dpb_ref[:, off:off + width] = dp
            dh = dh + lax.dot_general(dp, w_ref[...], (((1,), (1,)), ((), ())), preferred_element_type=F32)
        dx, dnw = _rms_bwd(dh, xh, r, nw_ref[...])
        dx_ref[...] = dres_ref[...] + dx
        _acc_out(dnw_ref, i == 0, dnw)

    row = lambda i: (i, 0)
    const = lambda i: (0, 0)
    dx, dnw, h, dpb = pl.pallas_call(
        body, name=name + "_x", grid=(t // tm,),
        in_specs=[pl.BlockSpec((tm, D_MODEL), row), pl.BlockSpec((1, D_MODEL), const), pl.BlockSpec((tm, D_MODEL), row)]
        + [pl.BlockSpec((D_MODEL, n), const) for n in _PROJ_WIDTHS]
        + [pl.BlockSpec((tm, n), row) for n in _PROJ_WIDTHS],
        out_specs=(pl.BlockSpec((tm, D_MODEL), row), pl.BlockSpec((1, D_MODEL), const), pl.BlockSpec((tm, D_MODEL), row),
                   pl.BlockSpec((tm, _PROJ_TOTAL), row)),
        out_shape=(jax.ShapeDtypeStruct((t, D_MODEL), F32), jax.ShapeDtypeStruct((1, D_MODEL), F32),
                   jax.ShapeDtypeStruct((t, D_MODEL), BF16), jax.ShapeDtypeStruct((t, _PROJ_TOTAL), BF16)),
        compiler_params=_cparams(("arbitrary",)),
    )(x, nw, dres, *ws, *dps)
    return dx, dnw, _wgrad(h, dpb, D_MODEL, _PROJ_TOTAL // 2, name=name + "_w")


def _sg_fn(u, v, lng, lnb, wcs, sgbt):
    lane = lax.broadcasted_iota(jnp.int32, (1, SG_WIDTH), 1)
    lane_b = lax.broadcasted_iota(jnp.int32, (1, LANES), 1)
    rr = lax.broadcasted_iota(jnp.int32, (SG_CHUNK, SG_CHUNK), 0)
    cc = lax.broadcasted_iota(jnp.int32, (SG_CHUNK, SG_CHUNK), 1)
    gu, gv = _gelu(u), _gelu(v)
    mu = jnp.mean(gv, axis=-1, keepdims=True)
    cen = gv - mu
    var = jnp.mean(cen * cen, axis=-1, keepdims=True)
    ln = cen * lax.rsqrt(var + EPS) * lng + lnb
    vs = jnp.zeros_like(u)
    for g in range(SG_GROUPS):
        in_group = jnp.logical_and(lane >= g * SG_GROUP_DIM, lane < (g + 1) * SG_GROUP_DIM)
        w_causal = jnp.where(rr >= cc, wcs[g], 0.0)
        bias = jnp.sum(jnp.where(lane_b == g, sgbt, 0.0), axis=1, keepdims=True)
        vs = vs + jnp.where(in_group, mm(w_causal, ln) + bias, 0.0)
    return gu * vs


def _sg_fwd(u, v, lng, lnb, wc, sgbt, *, name):
    t = u.shape[0]
    tm = _tm(t)

    def body(u_ref, v_ref, lng_ref, lnb_ref, wc_ref, sgbt_ref, o_ref):
        wcs = [wc_ref[g] for g in range(SG_GROUPS)]
        for c in range(tm // SG_CHUNK):
            rows = pl.ds(c * SG_CHUNK, SG_CHUNK)
            o_ref[rows, :] = _sg_fn(u_ref[rows, :], v_ref[rows, :], lng_ref[...], lnb_ref[...], wcs, sgbt_ref[...])

    row = lambda i: (i, 0)
    const = lambda i: (0, 0)
    return pl.pallas_call(
        body, name=name, grid=(t // tm,),
        in_specs=[pl.BlockSpec((tm, SG_WIDTH), row), pl.BlockSpec((tm, SG_WIDTH), row),
                  pl.BlockSpec((1, SG_WIDTH), const), pl.BlockSpec((1, SG_WIDTH), const),
                  pl.BlockSpec((SG_GROUPS, SG_CHUNK, SG_CHUNK), lambda i: (0, 0, 0)), pl.BlockSpec((SG_CHUNK, LANES), const)],
        out_specs=pl.BlockSpec((tm, SG_WIDTH), row),
        out_shape=jax.ShapeDtypeStruct((t, SG_WIDTH), F32),
        compiler_params=_cparams(("parallel",)),
    )(u, v, lng, lnb, wc, sgbt)


def _sg_bwd(u, v, lng, lnb, wc, sgbt, dout, *, name):
    t = u.shape[0]
    tm = _tm(t)

    def body(u_ref, v_ref, lng_ref, lnb_ref, wc_ref, sgbt_ref, do_ref, du_ref, dv_ref, dlng_ref, dlnb_ref, dwc_ref, dsgbt_ref):
        i = pl.program_id(0)
        wcs = [wc_ref[g] for g in range(SG_GROUPS)]
        tot = None
        for c in range(tm // SG_CHUNK):
            rows = pl.ds(c * SG_CHUNK, SG_CHUNK)
            _, vjp = jax.vjp(_sg_fn, u_ref[rows, :], v_ref[rows, :], lng_ref[...], lnb_ref[...], wcs, sgbt_ref[...])
            du, dv, dlng, dlnb, dwcs, dsgbt = vjp(do_ref[rows, :])
            du_ref[rows, :] = du
            dv_ref[rows, :] = dv
            part = (dlng, dlnb, dwcs, dsgbt)
            tot = part if tot is None else jax.tree.map(jnp.add, tot, part)
        dlng, dlnb, dwcs, dsgbt = tot
        _acc_out(dlng_ref, i == 0, dlng)
        _acc_out(dlnb_ref, i == 0, dlnb)
        _acc_out(dsgbt_ref, i == 0, dsgbt)
        for g in range(SG_GROUPS):
            @pl.when(i == 0)
            def _(g=g):
                dwc_ref[g] = dwcs[g]

            @pl.when(i > 0)
            def _(g=g):
                dwc_ref[g] += dwcs[g]

    row = lambda i: (i, 0)
    const = lambda i: (0, 0)
    wspec = pl.BlockSpec((SG_GROUPS, SG_CHUNK, SG_CHUNK), lambda i: (0, 0, 0))
    return pl.pallas_call(
        body, name=name, grid=(t // tm,),
        in_specs=[pl.BlockSpec((tm, SG_WIDTH), row), pl.BlockSpec((tm, SG_WIDTH), row),
                  pl.BlockSpec((1, SG_WIDTH), const), pl.BlockSpec((1, SG_WIDTH), const), wspec,
                  pl.BlockSpec((SG_CHUNK, LANES), const), pl.BlockSpec((tm, SG_WIDTH), row)],
        out_specs=(pl.BlockSpec((tm, SG_WIDTH), row), pl.BlockSpec((tm, SG_WIDTH), row),
                   pl.BlockSpec((1, SG_WIDTH), const), pl.BlockSpec((1, SG_WIDTH), const), wspec,
                   pl.BlockSpec((SG_CHUNK, LANES), const)),
        out_shape=(jax.ShapeDtypeStruct((t, SG_WIDTH), F32), jax.ShapeDtypeStruct((t, SG_WIDTH), F32),
                   jax.ShapeDtypeStruct((1, SG_WIDTH), F32), jax.ShapeDtypeStruct((1, SG_WIDTH), F32),
                   jax.ShapeDtypeStruct((SG_GROUPS, SG_CHUNK, SG_CHUNK), F32), jax.ShapeDtypeStruct((SG_CHUNK, LANES), F32)),
        compiler_params=_cparams(("arbitrary",)),
    )(u, v, lng, lnb, wc, sgbt, dout)


def _conv_taps(ext, w, tm):
    y = None
    for j in range(CONV_K):
        s = CONV_K - 1 - j
        shifted = ext if s == 0 else pltpu.roll(ext, s, 0)
        term = w[j:j + 1, :] * shifted[HALO:HALO + tm, :]
        y = term if y is None else y + term
    return y


def _post_conv(yq, yk, yv, bpre, apre, alog, dtb):
    def l2(a):
        return a * lax.rsqrt(jnp.sum(a * a, axis=-1, keepdims=True) + EPS)

    q = [l2(_silu(a)) for a in yq]
    k = [l2(_silu(a)) for a in yk]
    return q, k, _silu(yv), _sigmoid(bpre), -jnp.exp(alog) * _softplus(apre + dtb)


def _chunk_tril(tm):
    rr = lax.broadcasted_iota(jnp.int32, (tm, tm), 0)
    cc = lax.broadcasted_iota(jnp.int32, (tm, tm), 1)
    shift = DN_CHUNK.bit_length() - 1
    same = jnp.right_shift(rr, shift) == jnp.right_shift(cc, shift)
    return jnp.where(jnp.logical_and(same, rr >= cc), 1.0, 0.0).astype(F32)


def _halo_specs(tm, width, n_blocks_seq, n_blocks):
    per = tm // HALO
    prev = pl.BlockSpec((HALO, width), lambda i: (jnp.maximum(i * per - 1, 0), 0))
    nxt = pl.BlockSpec((HALO, width), lambda i: (jnp.minimum((i + 1) * per, n_blocks * per - 1), 0))
    return prev, nxt


def _split_heads(ref, base):
    return [ref[:, base + h * DN_HEAD_DIM: base + (h + 1) * DN_HEAD_DIM] for h in range(DN_HEADS)]


def _dn_prep_fwd(qkv, bpre, apre, conv_w, alog, dtb, seq, *, name):
    t = qkv.shape[0]
    tm = _tm(t)
    bps = seq // tm
    cw = 3 * DN_WIDTH

    def body(x_ref, halo_ref, b_ref, a_ref, w_ref, alog_ref, dtb_ref, q_ref, k_ref, v_ref, beta_ref, gc_ref):
        i = pl.program_id(0)
        keep = jnp.where(i % bps == 0, 0.0, 1.0)
        ext = jnp.concatenate([halo_ref[...] * keep, x_ref[...]], axis=0)
        y = _conv_taps(ext, w_ref[...], tm)
        yq = [y[:, h * DN_HEAD_DIM:(h + 1) * DN_HEAD_DIM] for h in range(DN_HEADS)]
        yk = [y[:, DN_WIDTH + h * DN_HEAD_DIM: DN_WIDTH + (h + 1) * DN_HEAD_DIM] for h in range(DN_HEADS)]
        q, k, v, beta, g = _post_conv(yq, yk, y[:, 2 * DN_WIDTH:], b_ref[...], a_ref[...], alog_ref[...], dtb_ref[...])
        for h in range(DN_HEADS):
            q_ref[:, h * DN_HEAD_DIM:(h + 1) * DN_HEAD_DIM] = q[h]
            k_ref[:, h * DN_HEAD_DIM:(h + 1) * DN_HEAD_DIM] = k[h]
        v_ref[...] = v
        beta_ref[...] = beta
        gc_ref[...] = mmx(_chunk_tril(tm), g)

    row = lambda i: (i, 0)
    const = lambda i: (0, 0)
    prev, _ = _halo_specs(tm, cw, bps, t // tm)
    return pl.pallas_call(
        body, name=name, grid=(t // tm,),
        in_specs=[pl.BlockSpec((tm, cw), row), prev, pl.BlockSpec((tm, LANES), row), pl.BlockSpec((tm, LANES), row),
                  pl.BlockSpec((CONV_K, cw), const), pl.BlockSpec((1, LANES), const), pl.BlockSpec((1, LANES), const)],
        out_specs=tuple(pl.BlockSpec((tm, n), row) for n in (DN_WIDTH, DN_WIDTH, DN_WIDTH, LANES, LANES)),
        out_shape=tuple(jax.ShapeDtypeStruct((t, n), F32) for n in (DN_WIDTH, DN_WIDTH, DN_WIDTH, LANES, LANES)),
        compiler_params=_cparams(("parallel",)),
    )(qkv, qkv, bpre, apre, conv_w, alog, dtb)


def _dn_prep_bwd(qkv, bpre, apre, conv_w, alog, dtb, dq, dk, dv, dbeta, dgc, dgc2, seq, *, name):
    t = qkv.shape[0]
    tm = _tm(t)
    bps = seq // tm
    cw = 3 * DN_WIDTH

    def body(x_ref, halo_ref, b_ref, a_ref, w_ref, alog_ref, dtb_ref, dq_ref, dk_ref, dv_ref, dbeta_ref, dgc_ref, dgc2_ref,
             dy_ref, db_ref, da_ref, dalog_ref, ddtb_ref):
        i = pl.program_id(0)
        keep = jnp.where(i % bps == 0, 0.0, 1.0)
        ext = jnp.concatenate([halo_ref[...] * keep, x_ref[...]], axis=0)
        y = _conv_taps(ext, w_ref[...], tm)
        yq = [y[:, h * DN_HEAD_DIM:(h + 1) * DN_HEAD_DIM] for h in range(DN_HEADS)]
        yk = [y[:, DN_WIDTH + h * DN_HEAD_DIM: DN_WIDTH + (h + 1) * DN_HEAD_DIM] for h in range(DN_HEADS)]
        _, vjp = jax.vjp(_post_conv, yq, yk, y[:, 2 * DN_WIDTH:], b_ref[...], a_ref[...], alog_ref[...], dtb_ref[...])
        dg = mmx_tn(_chunk_tril(tm), dgc_ref[...] + dgc2_ref[...])
        dyq, dyk, dyv, db, da, dalog, ddtb = vjp((_split_heads(dq_ref, 0), _split_heads(dk_ref, 0), dv_ref[...],
                                                  dbeta_ref[...], dg))
        for h in range(DN_HEADS):
            dy_ref[:, h * DN_HEAD_DIM:(h + 1) * DN_HEAD_DIM] = dyq[h]
            dy_ref[:, DN_WIDTH + h * DN_HEAD_DIM: DN_WIDTH + (h + 1) * DN_HEAD_DIM] = dyk[h]
        dy_ref[:, 2 * DN_WIDTH:] = dyv
        db_ref[...] = db
        da_ref[...] = da
        _acc_out(dalog_ref, i == 0, dalog)
        _acc_out(ddtb_ref, i == 0, ddtb)

    row = lambda i: (i, 0)
    const = lambda i: (0, 0)
    prev, _ = _halo_specs(tm, cw, bps, t // tm)
    return pl.pallas_call(
        body, name=name, grid=(t // tm,),
        in_specs=[pl.BlockSpec((tm, cw), row), prev, pl.BlockSpec((tm, LANES), row), pl.BlockSpec((tm, LANES), row),
                  pl.BlockSpec((CONV_K, cw), const), pl.BlockSpec((1, LANES), const), pl.BlockSpec((1, LANES), const),
                  pl.BlockSpec((tm, DN_WIDTH), row), pl.BlockSpec((tm, DN_WIDTH), row), pl.BlockSpec((tm, DN_WIDTH), row),
                  pl.BlockSpec((tm, LANES), row), pl.BlockSpec((tm, LANES), row), pl.BlockSpec((tm, LANES), row)],
        out_specs=(pl.BlockSpec((tm, cw), row), pl.BlockSpec((tm, LANES), row), pl.BlockSpec((tm, LANES), row),
                   pl.BlockSpec((1, LANES), const), pl.BlockSpec((1, LANES), const)),
        out_shape=(jax.ShapeDtypeStruct((t, cw), F32), jax.ShapeDtypeStruct((t, LANES), F32), jax.ShapeDtypeStruct((t, LANES), F32),
                   jax.ShapeDtypeStruct((1, LANES), F32), jax.ShapeDtypeStruct((1, LANES), F32)),
        compiler_params=_cparams(("arbitrary",)),
    )(qkv, qkv, bpre, apre, conv_w, alog, dtb, dq, dk, dv, dbeta, dgc, dgc2)


def _conv_bwd(qkv, dy, conv_w, seq, *, name):
    t = qkv.shape[0]
    tm = _tm(t)
    bps = seq // tm
    cw = 3 * DN_WIDTH
    n_ext = tm + HALO

    def body(x_ref, halo_ref, dy_ref, dyn_ref, w_ref, dx_ref, dw_ref):
        i = pl.program_id(0)
        keep_prev = jnp.where(i % bps == 0, 0.0, 1.0)
        keep_next = jnp.where(i % bps == bps - 1, 0.0, 1.0)
        ext = jnp.concatenate([halo_ref[...] * keep_prev, x_ref[...]], axis=0)
        dy = dy_ref[...]
        dyext = jnp.concatenate([dy, dyn_ref[...] * keep_next], axis=0)
        w = w_ref[...]

        @pl.when(i == 0)
        def _():
            dw_ref[...] = jnp.zeros_like(dw_ref)

        dx = None
        for j in range(CONV_K):
            s = CONV_K - 1 - j
            fut = dyext if s == 0 else pltpu.roll(dyext, n_ext - s, 0)
            term = w[j:j + 1, :] * fut[0:tm, :]
            dx = term if dx is None else dx + term
            past = ext if s == 0 else pltpu.roll(ext, s, 0)
            dw_ref[j:j + 1, :] += jnp.sum(dy * past[HALO:HALO + tm, :], axis=0, keepdims=True)
        dx_ref[...] = dx

    row = lambda i: (i, 0)
    const = lambda i: (0, 0)
    prev, nxt = _halo_specs(tm, cw, bps, t // tm)
    return pl.pallas_call(
        body, name=name, grid=(t // tm,),
        in_specs=[pl.BlockSpec((tm, cw), row), prev, pl.BlockSpec((tm, cw), row), nxt, pl.BlockSpec((CONV_K, cw), const)],
        out_specs=(pl.BlockSpec((tm, cw), row), pl.BlockSpec((HALO, cw), const)),
        out_shape=(jax.ShapeDtypeStruct((t, cw), F32), jax.ShapeDtypeStruct((HALO, cw), F32)),
        compiler_params=_cparams(("arbitrary",)),
    )(qkv, qkv, dy, dy, conv_w)


def _inv_unit_lower(l_mats, eye):
    invs = [eye - l for l in l_mats]
    powers = list(l_mats)
    n = 2
    while n < eye.shape[0]:
        powers = [mmh(p, p) for p in powers]
        invs = [inv + mmh(inv, p) for inv, p in zip(invs, powers)]
        n *= 2
    return invs


@jax.custom_vjp
def _solve(l_mat, rhs, inv):
    return mmh(inv, rhs)


def _solve_fwd(l_mat, rhs, inv):
    sol = mmh(inv, rhs)
    return sol, (inv, sol)


def _solve_bwd(res, d_sol):
    inv, sol = res
    d_rhs = mmh_tn(inv, d_sol)
    return -mmh_nt(d_rhs, sol), d_rhs, jnp.zeros_like(inv)


_solve.defvjp(_solve_fwd, _solve_bwd)


def _prep_fn(q, k, v, gc, gr, b, inv):
    ids = range(len(q))
    c = q[0].shape[0]
    rr = lax.broadcasted_iota(jnp.int32, (c, c), 0)
    cc = lax.broadcasted_iota(jnp.int32, (c, c), 1)
    incl, strict = rr >= cc, rr > cc
    is_last = lax.broadcasted_iota(jnp.int32, (c, 1), 0) == c - 1
    qs = [q[i] * (DN_HEAD_DIM ** -0.5) for i in ids]
    decay = [jnp.where(incl, jnp.exp(jnp.where(incl, gc[i] - gr[i], 0.0)), 0.0) for i in ids]
    kb = [k[i] * b[i] for i in ids]
    vb = [v[i] * b[i] for i in ids]
    kk = [mm_nt(kb[i], k[i]) for i in ids]
    l_mat = [jnp.where(strict, kk[i] * decay[i], 0.0) for i in ids]
    eg = [jnp.exp(gc[i]) for i in ids]
    if inv is None:
        inv = _inv_unit_lower(l_mat, jnp.where(rr == cc, 1.0, 0.0).astype(F32))
    u_wy = [_solve(l_mat[i], vb[i], inv[i]) for i in ids]
    w_wy = [_solve(l_mat[i], kb[i] * eg[i], inv[i]) for i in ids]
    qk = [mm_nt(qs[i], k[i]) * decay[i] for i in ids]
    g_last = [jnp.sum(jnp.where(is_last, gc[i], 0.0), axis=0, keepdims=True) for i in ids]
    k_dec = [k[i] * jnp.exp(g_last[i] - gc[i]) for i in ids]
    egl = [jnp.broadcast_to(jnp.exp(g_last[i]), (1, LANES)) for i in ids]
    return [(w_wy[i], u_wy[i], qs[i] * eg[i], k_dec[i], qk[i], egl[i]) for i in ids], inv


def _seq_fn(w, u, qd, kd, qk, egl, s):
    ids = range(len(w))
    ws = [mm(w[i], s[i]) for i in ids]
    qs = [mm(qd[i], s[i]) for i in ids]
    v_new = [u[i] - ws[i] for i in ids]
    o = [qs[i] + mm(qk[i], v_new[i]) for i in ids]
    s_new = [s[i] * egl[i] + mm_tn(kd[i], v_new[i]) for i in ids]
    return o, s_new


def _lane_col(a, h):
    lane = lax.broadcasted_iota(jnp.int32, (1, LANES), 1)
    return jnp.sum(jnp.where(lane == h, a, 0.0), axis=1, keepdims=True)


def _col_lane(col, h):
    lane = lax.broadcasted_iota(jnp.int32, (1, LANES), 1)
    return jnp.where(lane == h, col, 0.0)


def _head_cols(h):
    return slice(h * DN_HEAD_DIM, (h + 1) * DN_HEAD_DIM)


def _chunk_rows(n):
    return pl.ds(pl.multiple_of(n * DN_CHUNK, DN_CHUNK), DN_CHUNK)


def _delta_prep(q, k, v, gc, grow, beta, *, name):
    t = q.shape[0]
    tm = _tm(t)
    cpb = tm // DN_CHUNK
    n_chunks = t // DN_CHUNK
    group = 2

    def body(q_ref, k_ref, v_ref, gc_ref, gr_ref, b_ref, w_ref, u_ref, qd_ref, kd_ref, qk_ref, egl_ref, inv_ref):
        def step(m, carry):
            probs = [(m * group + e, h) for e in range(group) for h in range(DN_HEADS)]
            gcb = [gc_ref[_chunk_rows(m * group + e), :] for e in range(group)]
            bb = [b_ref[_chunk_rows(m * group + e), :] for e in range(group)]
            grb = [gr_ref[m * group + e] for e in range(group)]
            for e in range(group):
                egl_ref[m * group + e] = jnp.zeros((HALO, LANES), F32)
            outs, invs = _prep_fn(
                [q_ref[_chunk_rows(n), _head_cols(h)] for n, h in probs], [k_ref[_chunk_rows(n), _head_cols(h)] for n, h in probs],
                [v_ref[_chunk_rows(n), _head_cols(h)] for n, h in probs],
                [_lane_col(gcb[e], h) for e in range(group) for h in range(DN_HEADS)],
                [grb[e][h:h + 1, :] for e in range(group) for h in range(DN_HEADS)],
                [_lane_col(bb[e], h) for e in range(group) for h in range(DN_HEADS)], None)
            for (n, h), (w, u, qd, kd, qk, egl), inv in zip(probs, outs, invs):
                rows, cols = _chunk_rows(n), _head_cols(h)
                w_ref[rows, cols] = w.astype(BF16)
                u_ref[rows, cols] = u
                qd_ref[rows, cols] = qd.astype(BF16)
                kd_ref[rows, cols] = kd.astype(BF16)
                qk_ref[n, h] = qk
                inv_ref[n, h] = inv
                egl_ref[n, h:h + 1, :] = egl
            return carry

        lax.fori_loop(0, cpb // group, step, 0)

    row = lambda i: (i, 0)
    tok = pl.BlockSpec((tm, DN_WIDTH), row)
    lanes = pl.BlockSpec((tm, LANES), row)
    sq = pl.BlockSpec((cpb, DN_HEADS, DN_CHUNK, DN_CHUNK), lambda i: (i, 0, 0, 0))
    return pl.pallas_call(
        body, name=name, grid=(t // tm,),
        in_specs=[tok, tok, tok, lanes, pl.BlockSpec((cpb, HALO, DN_CHUNK), lambda i: (i, 0, 0)), lanes],
        out_specs=(tok, tok, tok, tok, sq, pl.BlockSpec((cpb, HALO, LANES), lambda i: (i, 0, 0)), sq),
        out_shape=(jax.ShapeDtypeStruct((t, DN_WIDTH), BF16), jax.ShapeDtypeStruct((t, DN_WIDTH), F32),
                   jax.ShapeDtypeStruct((t, DN_WIDTH), BF16), jax.ShapeDtypeStruct((t, DN_WIDTH), BF16),
                   jax.ShapeDtypeStruct((n_chunks, DN_HEADS, DN_CHUNK, DN_CHUNK), F32),
                   jax.ShapeDtypeStruct((n_chunks, HALO, LANES), F32),
                   jax.ShapeDtypeStruct((n_chunks, DN_HEADS, DN_CHUNK, DN_CHUNK), F32)),
        compiler_params=_cparams(("parallel",)),
    )(q, k, v, gc, grow, beta)


def _delta_par_bwd(q, k, v, gc, grow, beta, inv, dw, du, dqd, dkd, dqk, degl, *, name):
    t = q.shape[0]
    tm = _tm(t)
    cpb = tm // DN_CHUNK
    n_chunks = t // DN_CHUNK
    group = 2

    def body(q_ref, k_ref, v_ref, gc_ref, gr_ref, b_ref, inv_ref, dw_ref, du_ref, dqd_ref, dkd_ref, dqk_ref, degl_ref,
             dq_ref, dk_ref, dv_ref, dgc_ref, dgr_ref, db_ref):
        def step(m, carry):
            chunks = [m * group + e for e in range(group)]
            probs = [(e, h) for e in range(group) for h in range(DN_HEADS)]
            rows = [_chunk_rows(n) for n in chunks]
            gcb, bb = [gc_ref[r, :] for r in rows], [b_ref[r, :] for r in rows]
            grb, deglb = [gr_ref[n] for n in chunks], [degl_ref[n] for n in chunks]
            for n in chunks:
                dgr_ref[n] = jnp.zeros((HALO, DN_CHUNK), F32)
            invs = [inv_ref[chunks[e], h] for e, h in probs]
            _, vjp = jax.vjp(lambda *a: _prep_fn(*a, invs)[0],
                             [q_ref[rows[e], _head_cols(h)] for e, h in probs], [k_ref[rows[e], _head_cols(h)] for e, h in probs],
                             [v_ref[rows[e], _head_cols(h)] for e, h in probs], [_lane_col(gcb[e], h) for e, h in probs],
                             [grb[e][h:h + 1, :] for e, h in probs], [_lane_col(bb[e], h) for e, h in probs])
            dq, dk, dv, dgc, dgr, db = vjp([(dw_ref[rows[e], _head_cols(h)], du_ref[rows[e], _head_cols(h)],
                                             dqd_ref[rows[e], _head_cols(h)], dkd_ref[rows[e], _head_cols(h)],
                                             dqk_ref[chunks[e], h], deglb[e][h:h + 1, :]) for e, h in probs])
            dgc_acc = [jnp.zeros((DN_CHUNK, LANES), F32) for _ in chunks]
            db_acc = [jnp.zeros((DN_CHUNK, LANES), F32) for _ in chunks]
            for i, (e, h) in enumerate(probs):
                cols = _head_cols(h)
                dq_ref[rows[e], cols] = dq[i]
                dk_ref[rows[e], cols] = dk[i]
                dv_ref[rows[e], cols] = dv[i]
                dgr_ref[chunks[e], h:h + 1, :] = dgr[i]
                dgc_acc[e] = dgc_acc[e] + _col_lane(dgc[i], h)
                db_acc[e] = db_acc[e] + _col_lane(db[i], h)
            for e in range(group):
                dgc_ref[rows[e], :] = dgc_acc[e]
                db_ref[rows[e], :] = db_acc[e]
            return carry

        lax.fori_loop(0, cpb // group, step, 0)

    row = lambda i: (i, 0)
    tok = pl.BlockSpec((tm, DN_WIDTH), row)
    lanes = pl.BlockSpec((tm, LANES), row)
    sq = pl.BlockSpec((cpb, DN_HEADS, DN_CHUNK, DN_CHUNK), lambda i: (i, 0, 0, 0))
    grs = pl.BlockSpec((cpb, HALO, DN_CHUNK), lambda i: (i, 0, 0))
    return pl.pallas_call(
        body, name=name, grid=(t // tm,),
        in_specs=[tok, tok, tok, lanes, grs, lanes, sq, tok, tok, tok, tok, sq, pl.BlockSpec((cpb, HALO, LANES), lambda i: (i, 0, 0))],
        out_specs=(tok, tok, tok, lanes, grs, lanes),
        out_shape=(jax.ShapeDtypeStruct((t, DN_WIDTH), F32),) * 3
        + (jax.ShapeDtypeStruct((t, LANES), F32), jax.ShapeDtypeStruct((n_chunks, HALO, DN_CHUNK), F32),
           jax.ShapeDtypeStruct((t, LANES), F32)),
        compiler_params=_cparams(("parallel",)),
    )(q, k, v, gc, grow, beta, inv, dw, du, dqd, dkd, dqk, degl)


def _seq_specs(n_seq, seq, reverse):
    tm = _tm(seq)
    nb = seq // tm
    cpb = tm // DN_CHUNK
    blk = (lambda b, j: b * nb + nb - 1 - j) if reverse else (lambda b, j: b * nb + j)
    tok = pl.BlockSpec((tm, DN_WIDTH), lambda b, j: (blk(b, j), 0))
    sq = pl.BlockSpec((cpb, DN_HEADS, DN_CHUNK, DN_CHUNK), lambda b, j: (blk(b, j), 0, 0, 0))
    rows8 = pl.BlockSpec((cpb, HALO, LANES), lambda b, j: (blk(b, j), 0, 0))
    state = pl.BlockSpec((cpb, DN_HEADS, DN_HEAD_DIM, DN_HEAD_DIM), lambda b, j: (blk(b, j), 0, 0, 0))
    return nb, cpb, tok, sq, rows8, state


def _delta_seq_fwd(w, u, qd, kd, qk, egl, n_seq, seq, *, name):
    nb, cpb, tok, sq, rows8, state = _seq_specs(n_seq, seq, False)
    t = n_seq * seq

    def body(w_ref, u_ref, qd_ref, kd_ref, qk_ref, egl_ref, o_ref, st_ref, s_s):
        @pl.when(pl.program_id(1) == 0)
        def _():
            s_s[...] = jnp.zeros_like(s_s)

        def step(n, carry):
            rows = _chunk_rows(n)
            heads = range(DN_HEADS)
            eglb = egl_ref[n]
            s = [s_s[h] for h in heads]
            for h in heads:
                st_ref[n, h] = s[h]
            o, s_new = _seq_fn([w_ref[rows, _head_cols(h)] for h in heads], [u_ref[rows, _head_cols(h)] for h in heads],
                               [qd_ref[rows, _head_cols(h)] for h in heads], [kd_ref[rows, _head_cols(h)] for h in heads],
                               [qk_ref[n, h] for h in heads], [eglb[h:h + 1, :] for h in heads], s)
            for h in heads:
                o_ref[rows, _head_cols(h)] = o[h]
                s_s[h] = s_new[h]
            return carry

        lax.fori_loop(0, cpb, step, 0)

    return pl.pallas_call(
        body, name=name, grid=(n_seq, nb),
        in_specs=[tok, tok, tok, tok, sq, rows8],
        out_specs=(tok, state),
        out_shape=(jax.ShapeDtypeStruct((t, DN_WIDTH), F32),
                   jax.ShapeDtypeStruct((t // DN_CHUNK, DN_HEADS, DN_HEAD_DIM, DN_HEAD_DIM), F32)),
        scratch_shapes=[pltpu.VMEM((DN_HEADS, DN_HEAD_DIM, DN_HEAD_DIM), F32)],
        compiler_params=_cparams(("parallel", "arbitrary")),
    )(w, u, qd, kd, qk, egl)


def _delta_seq_bwd(w, u, qd, kd, qk, egl, states, do, n_seq, seq, *, name):
    nb, cpb, tok, sq, rows8, state = _seq_specs(n_seq, seq, True)
    t = n_seq * seq

    def body(w_ref, u_ref, qd_ref, kd_ref, qk_ref, egl_ref, st_ref, do_ref, dw_ref, du_ref, dqd_ref, dkd_ref, dqk_ref,
             degl_ref, ds_s):
        @pl.when(pl.program_id(1) == 0)
        def _():
            ds_s[...] = jnp.zeros_like(ds_s)

        def step(m, carry):
            n = cpb - 1 - m
            rows = _chunk_rows(n)
            eglb = egl_ref[n]
            degl_ref[n] = jnp.zeros((HALO, LANES), F32)
            heads = range(DN_HEADS)
            _, vjp = jax.vjp(_seq_fn, [w_ref[rows, _head_cols(h)].astype(F32) for h in heads],
                             [u_ref[rows, _head_cols(h)] for h in heads],
                             [qd_ref[rows, _head_cols(h)].astype(F32) for h in heads],
                             [kd_ref[rows, _head_cols(h)].astype(F32) for h in heads],
                             [qk_ref[n, h] for h in heads], [eglb[h:h + 1, :] for h in heads], [st_ref[n, h] for h in heads])
            dw, du, dqd, dkd, dqk, degl, ds_in = vjp(([do_ref[rows, _head_cols(h)] for h in heads], [ds_s[h] for h in heads]))
            for h in heads:
                cols = _head_cols(h)
                dw_ref[rows, cols] = dw[h]
                du_ref[rows, cols] = du[h]
                dqd_ref[rows, cols] = dqd[h]
                dkd_ref[rows, cols] = dkd[h]
                dqk_ref[n, h] = dqk[h]
                degl_ref[n, h:h + 1, :] = degl[h]
                ds_s[h] = ds_in[h]
            return carry

        lax.fori_loop(0, cpb, step, 0)

    return pl.pallas_call(
        body, name=name, grid=(n_seq, nb),
        in_specs=[tok, tok, tok, tok, sq, rows8, state, tok],
        out_specs=(tok, tok, tok, tok, sq, rows8),
        out_shape=(jax.ShapeDtypeStruct((t, DN_WIDTH), F32),) * 4
        + (jax.ShapeDtypeStruct((t // DN_CHUNK, DN_HEADS, DN_CHUNK, DN_CHUNK), F32),
           jax.ShapeDtypeStruct((t // DN_CHUNK, HALO, LANES), F32)),
        scratch_shapes=[pltpu.VMEM((DN_HEADS, DN_HEAD_DIM, DN_HEAD_DIM), F32)],
        compiler_params=_cparams(("parallel", "arbitrary")),
    )(w, u, qd, kd, qk, egl, states, do)


def _dn_gate(o, z, dnw):
    return o * lax.rsqrt(jnp.mean(o * o, axis=-1, keepdims=True) + EPS) * dnw * _silu(z)


def _mix_out_fwd(x, sg, o, z, wo_sg, wo_dn, dnw, *, name):
    t = x.shape[0]
    tm = _tm(t)

    def body(x_ref, sg_ref, o_ref, z_ref, wsg_ref, wdn_ref, dnw_ref, y_ref, dn_s):
        for h, (oh, zh) in enumerate(zip(_split_heads(o_ref, 0), _split_heads(z_ref, 0))):
            dn_s[:, h * DN_HEAD_DIM:(h + 1) * DN_HEAD_DIM] = _dn_gate(oh, zh, dnw_ref[...]).astype(BF16)
        y_ref[...] = (x_ref[...] + jnp.dot(sg_ref[...].astype(BF16), wsg_ref[...], preferred_element_type=F32)
                      + jnp.dot(dn_s[...], wdn_ref[...], preferred_element_type=F32))

    row = lambda i: (i, 0)
    const = lambda i: (0, 0)
    half = pl.BlockSpec((tm, DN_WIDTH), row)
    return pl.pallas_call(
        body, name=name, grid=(t // tm,),
        in_specs=[pl.BlockSpec((tm, D_MODEL), row), half, half, half, pl.BlockSpec((SG_WIDTH, D_MODEL), const),
                  pl.BlockSpec((DN_WIDTH, D_MODEL), const), pl.BlockSpec((1, DN_HEAD_DIM), const)],
        out_specs=pl.BlockSpec((tm, D_MODEL), row),
        out_shape=jax.ShapeDtypeStruct((t, D_MODEL), F32),
        scratch_shapes=[pltpu.VMEM((tm, DN_WIDTH), BF16)],
        compiler_params=_cparams(("parallel",)),
    )(x, sg, o, z, wo_sg, wo_dn, dnw)


def _mix_out_bwd(dy, sg, o, z, wo_sg, wo_dn, dnw, *, name):
    t = dy.shape[0]
    tm = _tm(t)

    def body(dy_ref, sg_ref, o_ref, z_ref, wsg_ref, wdn_ref, dnw_ref, dsg_ref, do_ref, dz_ref, dwsg_ref, dwdn_ref, ddnw_ref, dn_s):
        i = pl.program_id(0)
        dyb = dy_ref[...].astype(BF16)
        nt = (((1,), (1,)), ((), ()))
        tn = (((0,), (0,)), ((), ()))
        dsg_ref[...] = lax.dot_general(dyb, wsg_ref[...], nt, preferred_element_type=F32)
        ddn = lax.dot_general(dyb, wdn_ref[...], nt, preferred_element_type=F32)
        ddnw = None
        for h, (oh, zh) in enumerate(zip(_split_heads(o_ref, 0), _split_heads(z_ref, 0))):
            cols = slice(h * DN_HEAD_DIM, (h + 1) * DN_HEAD_DIM)
            out, vjp = jax.vjp(_dn_gate, oh, zh, dnw_ref[...])
            dn_s[:, cols] = out.astype(BF16)
            doh, dzh, dw = vjp(ddn[:, cols])
            do_ref[:, cols] = doh
            dz_ref[:, cols] = dzh
            ddnw = dw if ddnw is None else ddnw + dw
        _acc_out(ddnw_ref, i == 0, ddnw)
        _acc_out(dwsg_ref, i == 0, lax.dot_general(sg_ref[...].astype(BF16), dyb, tn, preferred_element_type=F32))
        _acc_out(dwdn_ref, i == 0, lax.dot_general(dn_s[...], dyb, tn, preferred_element_type=F32))

    row = lambda i: (i, 0)
    const = lambda i: (0, 0)
    half = pl.BlockSpec((tm, DN_WIDTH), row)
    wspec = pl.BlockSpec((DN_WIDTH, D_MODEL), const)
    return pl.pallas_call(
        body, name=name, grid=(t // tm,),
        in_specs=[pl.BlockSpec((tm, D_MODEL), row), half, half, half, wspec, wspec, pl.BlockSpec((1, DN_HEAD_DIM), const)],
        out_specs=(half, half, half, wspec, wspec, pl.BlockSpec((1, DN_HEAD_DIM), const)),
        out_shape=(jax.ShapeDtypeStruct((t, DN_WIDTH), F32),) * 3 + (jax.ShapeDtypeStruct((DN_WIDTH, D_MODEL), F32),) * 2
        + (jax.ShapeDtypeStruct((1, DN_HEAD_DIM), F32),),
        scratch_shapes=[pltpu.VMEM((tm, DN_WIDTH), BF16)],
        compiler_params=_cparams(("arbitrary",)),
    )(dy, sg, o, z, wo_sg, wo_dn, dnw)


_MESH = pl.DeviceIdType.MESH
_HBM = pl.BlockSpec(memory_space=pl.ANY)


def _mesh_pos():
    x, y, c = lax.axis_index("x"), lax.axis_index("y"), lax.axis_index("c")
    return x, y, c, [(1 - x, y), (x, 1 - y), (1 - x, 1 - y)]


def _gather2(arrs, *, name):
    n = len(arrs)
    slots = N_DEV - 1

    def body(*refs):
        in_refs, out_refs = refs[:n], refs[n:2 * n]
        send_sems, recv_sems, local_sems = refs[2 * n:]
        x, y, c, chips = _mesh_pos()
        me, sibling = (x, y, c), (x, y, 1 - c)

        def copy(k, slot, block, to, src=None):
            dst = out_refs[k].at[4 * block[0] + 2 * block[1] + block[2]]
            return pltpu.make_async_remote_copy(src_ref=dst if src is None else src, dst_ref=dst,
                                                send_sem=send_sems.at[k * slots + slot], recv_sem=recv_sems.at[k * slots + slot],
                                                device_id=to, device_id_type=_MESH)

        local = [pltpu.make_async_copy(in_refs[k], out_refs[k].at[4 * x + 2 * y + c], local_sems.at[k]) for k in range(n)]
        sent = []
        for k in range(n):
            sent.append(copy(k, 0, me, sibling, src=in_refs[k]))
            sent += [copy(k, 1 + j, me, (*chip, c), src=in_refs[k]) for j, chip in enumerate(chips)]
        for cp in local + sent:
            cp.start()
        for j, chip in enumerate(chips):
            for k in range(n):
                copy(k, 1 + j, (*chip, c), me).wait_recv()
                passed = copy(k, 4 + j, (*chip, c), sibling)
                passed.start()
                sent.append(passed)
        for k in range(n):
            copy(k, 0, sibling, me).wait_recv()
            for j, chip in enumerate(chips):
                copy(k, 4 + j, (*chip, 1 - c), me).wait_recv()
        for cp in sent:
            cp.wait_send()
        for cp in local:
            cp.wait()

    return pl.pallas_call(
        body, name=name, in_specs=[_HBM] * n, out_specs=(_HBM,) * n,
        out_shape=tuple(jax.ShapeDtypeStruct((N_DEV,) + a.shape, a.dtype) for a in arrs),
        scratch_shapes=[pltpu.SemaphoreType.DMA((n * slots,)), pltpu.SemaphoreType.DMA((n * slots,)),
                        pltpu.SemaphoreType.DMA((n,))],
    )(*arrs)


def _pair_swap(arrs, *, name):
    n = len(arrs)

    def body(*refs):
        in_refs, out_refs, send_sems, recv_sems = refs[:n], refs[n:2 * n], refs[2 * n], refs[2 * n + 1]
        x, y, c, _ = _mesh_pos()
        copies = [pltpu.make_async_remote_copy(src_ref=in_refs[k].at[1 - c], dst_ref=out_refs[k], send_sem=send_sems.at[k],
                                               recv_sem=recv_sems.at[k], device_id=(x, y, 1 - c), device_id_type=_MESH)
                  for k in range(n)]
        for cp in copies:
            cp.start()
        for cp in copies:
            cp.wait()

    return pl.pallas_call(
        body, name=name, in_specs=[_HBM] * n, out_specs=(_HBM,) * n,
        out_shape=tuple(jax.ShapeDtypeStruct(a.shape[1:], a.dtype) for a in arrs),
        scratch_shapes=[pltpu.SemaphoreType.DMA((n,)), pltpu.SemaphoreType.DMA((n,))],
    )(*arrs)


def _chip_exchange(arrs, *, name):
    n = len(arrs)
    slots = 3

    def body(*refs):
        in_refs, out_refs = refs[:n], refs[n:2 * n]
        send_sems, recv_sems, local_sems = refs[2 * n:]
        x, y, c, chips = _mesh_pos()
        mine = 2 * x + y
        copies = [pltpu.make_async_copy(in_refs[k].at[mine], out_refs[k].at[mine], local_sems.at[k]) for k in range(n)]
        for j, chip in enumerate(chips):
            for k in range(n):
                copies.append(pltpu.make_async_remote_copy(
                    src_ref=in_refs[k].at[2 * chip[0] + chip[1]], dst_ref=out_refs[k].at[mine],
                    send_sem=send_sems.at[k * slots + j], recv_sem=recv_sems.at[k * slots + j],
                    device_id=(*chip, c), device_id_type=_MESH))
        for cp in copies:
            cp.start()
        for cp in copies:
            cp.wait()

    return pl.pallas_call(
        body, name=name, in_specs=[_HBM] * n, out_specs=(_HBM,) * n,
        out_shape=tuple(jax.ShapeDtypeStruct(a.shape, a.dtype) for a in arrs),
        scratch_shapes=[pltpu.SemaphoreType.DMA((n * slots,)), pltpu.SemaphoreType.DMA((n * slots,)),
                        pltpu.SemaphoreType.DMA((n,))],
    )(*arrs)


_SEM = pl.BlockSpec(memory_space=pltpu.SEMAPHORE)
_EFFECT = pltpu.SideEffectType.DATAFLOW_SIDE_EFFECTING


def _direct_copies(src_refs, land_refs, send_sems, recv_sems, gather):
    x, y, c, _ = _mesh_pos()
    me = 4 * x + 2 * y + c
    n_peer = N_DEV - 1
    copies = []
    for r in range(1, N_DEV):
        px = 1 - x if r & 4 else x
        py = 1 - y if r & 2 else y
        pc = 1 - c if r & 1 else c
        for k, (src, land) in enumerate(zip(src_refs, land_refs)):
            copies.append(pltpu.make_async_remote_copy(
                src_ref=src if gather else src.at[4 * px + 2 * py + pc], dst_ref=land.at[me],
                send_sem=send_sems.at[k * n_peer + r - 1], recv_sem=recv_sems.at[k * n_peer + r - 1],
                device_id=(px, py, pc), device_id_type=_MESH))
    return copies


def _send_start(arrs, gather, *, name):
    n = len(arrs)
    lands = [lax.empty(((N_DEV,) + a.shape) if gather else a.shape, a.dtype) for a in arrs]

    def body(*refs):
        src_refs, land_refs, send_sems, recv_sems, token = refs[:n], refs[n:2 * n], refs[2 * n], refs[2 * n + 1], refs[-1]
        for cp in _direct_copies(src_refs, land_refs, send_sems, recv_sems, gather):
            cp.start()
        token[...] = jnp.zeros_like(token)

    n_sem = n * (N_DEV - 1)
    bufs = list(arrs) + lands
    out = pl.pallas_call(
        body, name=name,
        out_shape=(pltpu.SemaphoreType.DMA((n_sem,)), pltpu.SemaphoreType.DMA((n_sem,)))
        + tuple(pltpu.HBM(b.shape, b.dtype) for b in bufs) + (jax.ShapeDtypeStruct((HALO, LANES), F32),),
        in_specs=[_HBM] * (2 * n), out_specs=(_SEM, _SEM) + (_HBM,) * (2 * n) + (pl.BlockSpec(memory_space=pltpu.VMEM),),
        input_output_aliases={i: 2 + i for i in range(2 * n)},
        compiler_params=pltpu.CompilerParams(has_side_effects=_EFFECT),
    )(*[pltpu.with_memory_space_constraint(b, pltpu.HBM) for b in bufs])
    return (out[0], out[1], list(out[2:2 + n]), list(out[2 + n:2 + 2 * n])), out[-1]


def _send_wait(started, gather, after, *, name):
    send_sems, recv_sems, srcs, lands = started
    n = len(srcs)

    def body(*refs):
        src_refs, land_refs, send_ref, recv_ref = refs[:n], refs[n:2 * n], refs[2 * n], refs[2 * n + 1]
        for cp in _direct_copies(src_refs, land_refs, send_ref, recv_ref, gather):
            cp.wait_send()
            cp.wait_recv()

    bufs = srcs + lands
    out = pl.pallas_call(
        body, name=name, out_shape=tuple(pltpu.HBM(b.shape, b.dtype) for b in bufs),
        in_specs=[_HBM] * (2 * n) + [_SEM, _SEM, _HBM], out_specs=(_HBM,) * (2 * n),
        input_output_aliases={i: i for i in range(2 * n)},
        compiler_params=pltpu.CompilerParams(has_side_effects=_EFFECT),
    )(*bufs, send_sems, recv_sems, after)
    return list(out[n:])


def _pair_add(p, r, core, *, name):
    _, n_chip, rows, cols = p.shape
    rb = _row_block(rows)

    def body(core_ref, p_ref, r_ref, o_ref):
        o_ref[...] = (p_ref[...].astype(F32) + r_ref[...].astype(F32)).astype(BF16)

    return pl.pallas_call(
        body, name=name,
        grid_spec=pltpu.PrefetchScalarGridSpec(
            num_scalar_prefetch=1, grid=(n_chip, rows // rb),
            in_specs=[pl.BlockSpec((None, None, rb, cols), lambda s, i, core_ref: (core_ref[0], s, i, 0)),
                      pl.BlockSpec((None, rb, cols), lambda s, i, core_ref: (s, i, 0))],
            out_specs=pl.BlockSpec((None, rb, cols), lambda s, i, core_ref: (s, i, 0))),
        out_shape=jax.ShapeDtypeStruct((n_chip, rows, cols), BF16),
        compiler_params=_cparams(("parallel", "parallel")),
    )(core, p, r)


def _row_block(rows, limit=256):
    best = rows
    for cand in range(8, limit + 1, 8):
        if rows % cand == 0:
            best = cand
    return best if rows > limit else rows


def _adam(gp, w, m, v, *, name):
    p, rows, cols = gp.shape
    rb = _row_block(rows)

    def body(gp_ref, w_ref, m_ref, v_ref, g_ref, d_ref, m2_ref, v2_ref):
        g = gp_ref[0].astype(F32)
        for s in range(1, p):
            g = g + gp_ref[s].astype(F32)
        m2 = ADAM_B1 * m_ref[...] + (1.0 - ADAM_B1) * g
        v2 = ADAM_B2 * v_ref[...] + (1.0 - ADAM_B2) * (g * g)
        m_hat = m2 / (1.0 - ADAM_B1 ** ADAM_STEP)
        v_hat = v2 / (1.0 - ADAM_B2 ** ADAM_STEP)
        g_ref[...] = g
        d_ref[...] = -ADAM_LR * (m_hat / (jnp.sqrt(v_hat) + ADAM_EPS) + ADAM_WD * w_ref[...])
        m2_ref[...] = m2
        v2_ref[...] = v2

    blk = pl.BlockSpec((rb, cols), lambda i: (i, 0))
    return pl.pallas_call(
        body, name=name, grid=(rows // rb,),
        in_specs=[pl.BlockSpec((p, rb, cols), lambda i: (0, i, 0)), blk, blk, blk],
        out_specs=(blk,) * 4, out_shape=(jax.ShapeDtypeStruct((rows, cols), F32),) * 4,
        compiler_params=_cparams(("parallel",)),
    )(gp, w, m, v)


def _cols_full(g):
    return jnp.transpose(g, (1, 0, 2)).reshape(g.shape[1], N_DEV * g.shape[2])


def _cols_pieces(full):
    r, c = full.shape
    return jnp.transpose(full.reshape(r, N_DEV, c // N_DEV), (1, 0, 2))


def _pad_lanes(a, width=LANES):
    return jnp.pad(a, ((0, 0), (0, width - a.shape[1])))


def _chunk_rows_of(a):
    by_chunk = jnp.transpose(a[:, :DN_HEADS].reshape(-1, DN_CHUNK, DN_HEADS), (0, 2, 1))
    return jnp.pad(by_chunk, ((0, 0), (0, HALO - DN_HEADS), (0, 0)))


_SMALL = (("ffn1_norm", D_MODEL), ("mix_norm", D_MODEL), ("ffn2_norm", D_MODEL), ("final_norm", D_MODEL), ("a_log", DN_HEADS),
          ("dt_bias", DN_HEADS), ("dn_norm", DN_HEAD_DIM), ("sg_ln_g", SG_WIDTH), ("sg_ln_b", SG_WIDTH),
          ("sg_w", SG_GROUPS * SG_CHUNK * SG_CHUNK), ("sg_b", SG_GROUPS * SG_CHUNK), ("conv_w", CONV_K * 3 * DN_WIDTH))
_SMALL_ROWS = 1128
_SMALL_SHAPES = {"ffn1_norm": (1, D_MODEL), "mix_norm": (1, D_MODEL), "ffn2_norm": (1, D_MODEL), "final_norm": (D_MODEL,),
                 "a_log": (1, DN_HEADS), "dt_bias": (1, DN_HEADS), "dn_norm": (1, DN_HEAD_DIM), "sg_ln_g": (1, SG_WIDTH),
                 "sg_ln_b": (1, SG_WIDTH), "sg_w": (1, SG_GROUPS, SG_CHUNK, SG_CHUNK), "sg_b": (1, SG_GROUPS, SG_CHUNK)}


def _pack_small(d):
    flat = jnp.concatenate([d[name].reshape(-1) for name, _ in _SMALL])
    return jnp.pad(flat, (0, _SMALL_ROWS * LANES - flat.shape[0])).reshape(_SMALL_ROWS, LANES)


def _unpack_small(a):
    flat, out, at = a.reshape(-1), {}, 0
    for name, size in _SMALL:
        out[name] = flat[at:at + size]
        at += size
    return out


def kernel(x, ffn1_norm, ffn1_w_gate, ffn1_w_up, ffn1_w_down, mix_norm, w_in, conv_w, a_log, dt_bias, dn_norm, sg_ln_g, sg_ln_b, sg_w, sg_b, w_out, ffn2_norm, ffn2_w_gate, ffn2_w_up, ffn2_w_down, final_norm, loss_target, m_ffn1_norm, m_ffn1_w_gate, m_ffn1_w_up, m_ffn1_w_down, m_mix_norm, m_w_in, m_conv_w, m_a_log, m_dt_bias, m_dn_norm, m_sg_ln_g, m_sg_ln_b, m_sg_w, m_sg_b, m_w_out, m_ffn2_norm, m_ffn2_w_gate, m_ffn2_w_up, m_ffn2_w_down, m_final_norm, v_ffn1_norm, v_ffn1_w_gate, v_ffn1_w_up, v_ffn1_w_down, v_mix_norm, v_w_in, v_conv_w, v_a_log, v_dt_bias, v_dn_norm, v_sg_ln_g, v_sg_ln_b, v_sg_w, v_sg_b, v_w_out, v_ffn2_norm, v_ffn2_w_gate, v_ffn2_w_up, v_ffn2_w_down, v_final_norm):
    weights = dict(ffn1_norm=ffn1_norm, ffn1_w_gate=ffn1_w_gate, ffn1_w_up=ffn1_w_up, ffn1_w_down=ffn1_w_down, mix_norm=mix_norm, w_in=w_in, conv_w=conv_w, a_log=a_log, dt_bias=dt_bias, dn_norm=dn_norm, sg_ln_g=sg_ln_g, sg_ln_b=sg_ln_b, sg_w=sg_w, sg_b=sg_b, w_out=w_out, ffn2_norm=ffn2_norm, ffn2_w_gate=ffn2_w_gate, ffn2_w_up=ffn2_w_up, ffn2_w_down=ffn2_w_down, final_norm=final_norm)
    mom_m = dict(ffn1_norm=m_ffn1_norm, ffn1_w_gate=m_ffn1_w_gate, ffn1_w_up=m_ffn1_w_up, ffn1_w_down=m_ffn1_w_down, mix_norm=m_mix_norm, w_in=m_w_in, conv_w=m_conv_w, a_log=m_a_log, dt_bias=m_dt_bias, dn_norm=m_dn_norm, sg_ln_g=m_sg_ln_g, sg_ln_b=m_sg_ln_b, sg_w=m_sg_w, sg_b=m_sg_b, w_out=m_w_out, ffn2_norm=m_ffn2_norm, ffn2_w_gate=m_ffn2_w_gate, ffn2_w_up=m_ffn2_w_up, ffn2_w_down=m_ffn2_w_down, final_norm=m_final_norm)
    mom_v = dict(ffn1_norm=v_ffn1_norm, ffn1_w_gate=v_ffn1_w_gate, ffn1_w_up=v_ffn1_w_up, ffn1_w_down=v_ffn1_w_down, mix_norm=v_mix_norm, w_in=v_w_in, conv_w=v_conv_w, a_log=v_a_log, dt_bias=v_dt_bias, dn_norm=v_dn_norm, sg_ln_g=v_sg_ln_g, sg_ln_b=v_sg_ln_b, sg_w=v_sg_w, sg_b=v_sg_b, w_out=v_w_out, ffn2_norm=v_ffn2_norm, ffn2_w_gate=v_ffn2_w_gate, ffn2_w_up=v_ffn2_w_up, ffn2_w_down=v_ffn2_w_down, final_norm=v_final_norm)
    order = list(weights)
    big = ("ffn1_w_gate", "ffn1_w_up", "ffn1_w_down", "w_in", "w_out", "ffn2_w_gate", "ffn2_w_up", "ffn2_w_down")
    col_sharded = ("ffn1_w_gate", "ffn1_w_up", "w_in", "ffn2_w_gate", "ffn2_w_up")

    n_seq, seq, _ = x.shape
    t = n_seq * seq
    me = 4 * lax.axis_index("x") + 2 * lax.axis_index("y") + lax.axis_index("c")
    x0 = x.reshape(t, D_MODEL)
    tgt = loss_target.reshape(t, D_MODEL)

    def fill_own(land, own_block):
        return lax.dynamic_update_index_in_dim(land, own_block, me, 0)

    def as_full(n, g):
        return _cols_full(g) if n in col_sharded else g.reshape(-1, g.shape[-1])

    shards = {n: weights[n][0].astype(BF16) for n in big}
    ffn1_names, mix_names, ffn2_names = big[:3], big[3:5], big[5:]
    full = {n: as_full(n, g) for n, g in zip(ffn1_names, _gather2([shards[n] for n in ffn1_names], name="gather_ffn1"))}
    mix_srcs = [shards[n] for n in mix_names] + [conv_w[0]]
    mix_started, mix_token = _send_start(mix_srcs, True, name="gather_mix_start")
    ffn2_started, ffn2_token = _send_start([shards[n] for n in ffn2_names], True, name="gather_ffn2_start")
    ffn1_norm_fwd = ffn1_norm + (mix_token[:1, :1] + ffn2_token[:1, :1])
    alog, dtb = _pad_lanes(a_log), _pad_lanes(dt_bias)
    sgbt = _pad_lanes(sg_b[0].T)
    fnw = final_norm.reshape(1, D_MODEL)

    x1, h1, g1, u1 = _ffn_fwd(x0, ffn1_norm_fwd, full["ffn1_w_gate"], full["ffn1_w_up"], full["ffn1_w_down"], name="ffn1_fwd")
    mix_lands = [fill_own(land, src) for land, src in zip(_send_wait(mix_started, True, x1, name="gather_mix_wait"), mix_srcs)]
    full.update({n: as_full(n, g) for n, g in zip(mix_names, mix_lands)})
    conv_full = _cols_full(mix_lands[-1])
    w_in_f = full["w_in"]
    offs = (0, SG_WIDTH, 2 * SG_WIDTH, 2 * SG_WIDTH + 3 * DN_WIDTH, 2 * SG_WIDTH + 4 * DN_WIDTH)
    n_proj = offs[-1]
    ws = [w_in_f[:, offs[0]:offs[1]], w_in_f[:, offs[1]:offs[2]], w_in_f[:, offs[2]:offs[3]], w_in_f[:, offs[3]:offs[4]],
          _pad_lanes(w_in_f[:, n_proj:n_proj + DN_HEADS]), _pad_lanes(w_in_f[:, n_proj + DN_HEADS:n_proj + 2 * DN_HEADS])]
    wo_sg, wo_dn = full["w_out"][:SG_WIDTH], full["w_out"][SG_WIDTH:]
    u, v, qkv, z, bpre, apre = _mix_in_fwd(x1, mix_norm, ws, name="mix_in_fwd")
    sg_out = _sg_fwd(u, v, sg_ln_g, sg_ln_b, sg_w[0], sgbt, name="sg_fwd")
    q, k, vv, beta, gc = _dn_prep_fwd(qkv, bpre, apre, conv_full, alog, dtb, seq, name="dn_prep_fwd")
    grow = _chunk_rows_of(gc)
    wy_w, wy_u, q_dec, k_dec, qk, egl, inv = _delta_prep(q, k, vv, gc, grow, beta, name="delta_prep")
    o, states = _delta_seq_fwd(wy_w, wy_u, q_dec, k_dec, qk, egl, n_seq, seq, name="delta_seq_fwd")
    x2 = _mix_out_fwd(x1, sg_out, o, z, wo_sg, wo_dn, dn_norm, name="mix_out_fwd")
    ffn2_lands = _send_wait(ffn2_started, True, x2, name="gather_ffn2_wait")
    full.update({n: as_full(n, fill_own(land, shards[n])) for n, land in zip(ffn2_names, ffn2_lands)})
    dx3, loss_part, d_fn, h2, g2, u2 = _ffn_fwd(x2, ffn2_norm, full["ffn2_w_gate"], full["ffn2_w_up"], full["ffn2_w_down"],
                                                tgt, fnw, name="ffn2_fwd_loss")
    loss = lax.psum(loss_part[0, 0], ("x", "y", "c"))

    dx2, d_n2, d_g2, d_u2, d_d2 = _ffn_bwd(x2, ffn2_norm, h2, g2, u2, full["ffn2_w_gate"], full["ffn2_w_up"],
                                           full["ffn2_w_down"], dx3, name="ffn2_bwd")
    ff_cols = _cols_pieces

    def ff_rows(d_wd):
        return d_wd.reshape(N_DEV, D_FF // N_DEV, D_MODEL)

    ffn2_pieces = [ff_cols(d_g2), ff_cols(d_u2), ff_rows(d_d2)]
    ffn2_sent, sent_token = _send_start(ffn2_pieces, False, name="grads_ffn2_start")
    dsg, do, dz, d_wo_sg, d_wo_dn, d_dnw = _mix_out_bwd(dx2, sg_out, o, z, wo_sg, wo_dn, dn_norm + sent_token[:1, :1],
                                                        name="mix_out_bwd")
    d_seq = _delta_seq_bwd(wy_w, wy_u, q_dec, k_dec, qk, egl, states, do, n_seq, seq, name="delta_seq_bwd")
    dq, dk, dv, dgc_a, dgrow, dbeta = _delta_par_bwd(q, k, vv, gc, grow, beta, inv, *d_seq, name="delta_par_bwd")
    dgc_b = _pad_lanes(jnp.transpose(dgrow[:, :DN_HEADS, :], (0, 2, 1)).reshape(t, DN_HEADS))
    dy_conv, dbpre, dapre, d_alog, d_dtb = _dn_prep_bwd(qkv, bpre, apre, conv_full, alog, dtb, dq, dk, dv, dbeta, dgc_a, dgc_b,
                                                        seq, name="dn_prep_bwd")
    dqkv, d_conv = _conv_bwd(qkv, dy_conv, conv_full, seq, name="conv_bwd")
    du, dvv, d_lng, d_lnb, d_wc, d_sgbt = _sg_bwd(u, v, sg_ln_g, sg_ln_b, sg_w[0], sgbt, dsg, name="sg_bwd")
    dx1, d_mixn, d_wp = _mix_in_bwd(x1, mix_norm, ws, dx2, (du, dvv, dqkv, dz, dbpre, dapre), name="mix_in_bwd")
    d_w_in = jnp.concatenate([d_wp[:, :n_proj], d_wp[:, _PROJ_OFFSETS[4]:_PROJ_OFFSETS[4] + DN_HEADS],
                              d_wp[:, _PROJ_OFFSETS[5]:_PROJ_OFFSETS[5] + DN_HEADS]], axis=1)
    d_w_out = jnp.concatenate([d_wo_sg, d_wo_dn], axis=0)
    mix_pieces = [_cols_pieces(d_w_in), d_w_out.reshape(N_DEV, D_MODEL // N_DEV, D_MODEL).astype(BF16)]
    mix_sent, sent_token = _send_start(mix_pieces, False, name="grads_mix_start")
    grad_x, d_n1, dg1, du1, a1, dyh1 = _ffn_bwd_x(x0, ffn1_norm + sent_token[:1, :1], g1, u1, full["ffn1_w_gate"],
                                                  full["ffn1_w_up"], full["ffn1_w_down"], dx1, name="ffn1_bwd_x")
    small_grads = dict(ffn1_norm=d_n1, mix_norm=d_mixn, ffn2_norm=d_n2, final_norm=d_fn, a_log=d_alog[:, :DN_HEADS],
                       dt_bias=d_dtb[:, :DN_HEADS], dn_norm=d_dnw, sg_ln_g=d_lng, sg_ln_b=d_lnb, sg_w=d_wc,
                       sg_b=d_sgbt[:, :SG_GROUPS].T, conv_w=d_conv[:CONV_K])
    small_src = _pack_small(small_grads)
    small_sent, small_token = _send_start([small_src], True, name="small_grads_start")
    late = []

    def send_early(k, grad):
        if k == 2:
            return None
        piece = ff_cols(grad)
        sent, token = _send_start([piece], False, name="grads_" + ffn1_names[k] + "_start")
        late.append(((ffn1_names[k],), sent, [piece]))
        return token

    _, _, d_d1 = _ffn_wgrads(h1, dg1, du1, a1, dyh1, send_early, small_token, name="ffn1_bwd")

    def by_core(p8):
        return jnp.moveaxis(p8.reshape((4, 2) + p8.shape[1:]), 1, 0)

    own = [by_core(ff_rows(d_d1))]
    from_sibling = _pair_swap(own, name="grads_to_sibling")
    core = lax.axis_index("c").astype(jnp.int32).reshape(1)
    chip_sums = [_pair_add(own[0], from_sibling[0], core, name="pair_add_" + ffn1_names[2])]
    received = {ffn1_names[2]: _chip_exchange(chip_sums, name="grads_to_owner")[0]}
    for names, sent, pieces in [(ffn2_names, ffn2_sent, ffn2_pieces), (mix_names, mix_sent, mix_pieces)] + late:
        lands = _send_wait(sent, False, received[ffn1_names[2]], name="grads_" + names[0] + "_wait")
        received.update({n: fill_own(land, lax.dynamic_index_in_dim(p, me, 0, keepdims=False))
                         for n, land, p in zip(names, lands, pieces)})
    res = {n: _adam(received[n], weights[n][0], mom_m[n][0], mom_v[n][0], name="adam_" + n) for n in big}

    (small_land,) = _send_wait(small_sent, True, received[ffn1_names[2]], name="small_grads_wait")
    small_parts = fill_own(small_land, small_src)
    zeros_conv = jnp.zeros((CONV_K * 3 * DN_WIDTH,), F32)
    packed = [_pack_small({**{n: src[n] for n, _ in _SMALL if n != "conv_w"}, "conv_w": zeros_conv})
              for src in (weights, mom_m, mom_v)]
    small_res = [_unpack_small(a) for a in _adam(small_parts, *packed, name="adam_small")]
    conv_grad = lax.dynamic_slice_in_dim(small_res[0]["conv_w"].reshape(CONV_K, 3 * DN_WIDTH), me * (3 * DN_WIDTH // N_DEV),
                                         3 * DN_WIDTH // N_DEV, axis=1)
    res["conv_w"] = _adam(conv_grad[None], conv_w[0], m_conv_w[0], v_conv_w[0], name="adam_conv_w")

    outs = [[], [], [], []]
    for n in order:
        for kind in range(4):
            if n in res:
                outs[kind].append(res[n][kind][None])
            else:
                outs[kind].append(small_res[kind][n].reshape(_SMALL_SHAPES[n]))
    return (loss, grad_x.reshape(x.shape), *outs[0], *outs[1], *outs[2], *outs[3])
```

```python
import functools

import jax
import jax.numpy as jnp
from jax import lax
from jax.experimental import pallas as pl
from jax.experimental.pallas import tpu as pltpu

F32 = jnp.float32
BF16 = jnp.bfloat16

D_MODEL = 1024
D_FF = 2816
SG_WIDTH = 512
SG_GROUPS = 8
SG_GROUP_DIM = 64
SG_CHUNK = 128
DN_WIDTH = 512
DN_HEAD_DIM = 128
DN_HEADS = 4
DN_CHUNK = 64
CONV_K = 4
EPS = 1e-6
N_DEV = 8
LANES = 128
HALO = 8

ADAM_LR = 0.001
ADAM_B1 = 0.9
ADAM_B2 = 0.999
ADAM_EPS = 1e-08
ADAM_WD = 0.01
ADAM_STEP = 10

VMEM_LIMIT = 60 * 1024 * 1024
TOKEN_BLOCK = 512
FF_BLOCK_FWD = 1408

_HI = lax.Precision.HIGHEST


def _cparams(sem):
    return pltpu.CompilerParams(dimension_semantics=sem, vmem_limit_bytes=VMEM_LIMIT)


def _tm(t, pref=TOKEN_BLOCK):
    return min(pref, t)


def _dg(a, b, ca, cb, precision):
    if precision is not None:
        return lax.dot_general(a, b, (((ca,), (cb,)), ((), ())), precision=precision, preferred_element_type=F32)
    return lax.dot_general(a.astype(BF16), b.astype(BF16), (((ca,), (cb,)), ((), ())), preferred_element_type=F32)


def _make_mm(exact):
    @jax.custom_vjp
    def mm(a, b):
        return _dg(a, b, 1, 0, exact)

    @jax.custom_vjp
    def mm_nt(a, b):
        return _dg(a, b, 1, 1, exact)

    @jax.custom_vjp
    def mm_tn(a, b):
        return _dg(a, b, 0, 0, exact)

    mm.defvjp(lambda a, b: (mm(a, b), (a, b)), lambda r, g: (mm_nt(g, r[1]), mm_tn(r[0], g)))
    mm_nt.defvjp(lambda a, b: (mm_nt(a, b), (a, b)), lambda r, g: (mm(g, r[1]), mm_tn(g, r[0])))
    mm_tn.defvjp(lambda a, b: (mm_tn(a, b), (a, b)), lambda r, g: (mm_nt(r[1], g), mm(r[0], g)))
    return mm, mm_nt, mm_tn


mm, mm_nt, mm_tn = _make_mm(None)
mmx, mmx_nt, mmx_tn = _make_mm(_HI)
mmh, mmh_nt, mmh_tn = _make_mm(lax.Precision.HIGH)


def _sigmoid(x):
    return 1.0 / (1.0 + jnp.exp(-x))


def _silu(x):
    return x * _sigmoid(x)


def _softplus(x):
    neg_abs = jnp.where(x > 0, -x, x)
    return jnp.where(x > 0, x, 0.0) + jnp.log(1.0 + jnp.exp(neg_abs))


def _gelu(x):
    return 0.5 * x * (1.0 + jnp.tanh(0.7978845608028654 * (x + 0.044715 * (x * x * x))))


def _rms_fwd(x, g):
    r = lax.rsqrt(jnp.mean(x * x, axis=-1, keepdims=True) + EPS)
    xh = x * r
    return xh * g, xh, r


def _rms_bwd(dh, xh, r, g):
    dxh = dh * g
    dx = r * (dxh - xh * jnp.mean(dxh * xh, axis=-1, keepdims=True))
    return dx, jnp.sum(dh * xh, axis=0, keepdims=True)


def _acc_out(ref, first, val):
    @pl.when(first)
    def _():
        ref[...] = val

    @pl.when(jnp.logical_not(first))
    def _():
        ref[...] += val


def _ffn_fwd(x, nw, wg, wu, wd, tgt=None, fnw=None, *, name):
    t = x.shape[0]
    tm, fb = _tm(t), FF_BLOCK_FWD
    n_t, n_f = t // tm, D_FF // fb
    with_loss = tgt is not None

    def body(*refs):
        if with_loss:
            (x_ref, nw_ref, wg_ref, wu_ref, wd_ref, tgt_ref, fnw_ref, dy_ref, loss_ref, dfn_ref, h_ref, g_ref, u_ref,
             acc_s) = refs
        else:
            x_ref, nw_ref, wg_ref, wu_ref, wd_ref, y_ref, h_ref, g_ref, u_ref, acc_s = refs
        i, j = pl.program_id(0), pl.program_id(1)

        @pl.when(j == 0)
        def _():
            h, _, _ = _rms_fwd(x_ref[...], nw_ref[...])
            h_ref[...] = h.astype(BF16)
            acc_s[...] = jnp.zeros_like(acc_s)

        h = h_ref[...]
        nt = (((1,), (1,)), ((), ()))
        g = lax.dot_general(h, wg_ref[...], nt, preferred_element_type=F32)
        u = lax.dot_general(h, wu_ref[...], nt, preferred_element_type=F32)
        g_ref[...] = g.astype(BF16)
        u_ref[...] = u.astype(BF16)
        a = _silu(g) * u
        acc_s[...] += jnp.dot(a.astype(BF16), wd_ref[...], preferred_element_type=F32)

        @pl.when(j == n_f - 1)
        def _():
            y = x_ref[...] + 0.5 * acc_s[...]
            if not with_loss:
                y_ref[...] = y
            else:
                gf = fnw_ref[...]
                out, xh, r = _rms_fwd(y, gf)
                err = out - tgt_ref[...]
                part = 0.5 * jnp.sum(jnp.mean(err * err, axis=-1, keepdims=True), axis=0, keepdims=True)
                d_out = err * (1.0 / D_MODEL)
                dy, dgf = _rms_bwd(d_out, xh, r, gf)
                dy_ref[...] = dy
                _acc_out(loss_ref, i == 0, jnp.broadcast_to(part, loss_ref.shape))
                _acc_out(dfn_ref, i == 0, dgf)

    row = lambda i, j: (i, 0)
    const = lambda i, j: (0, 0)
    in_specs = [
        pl.BlockSpec((tm, D_MODEL), row),
        pl.BlockSpec((1, D_MODEL), const),
        pl.BlockSpec((fb, D_MODEL), lambda i, j: (j, 0)),
        pl.BlockSpec((fb, D_MODEL), lambda i, j: (j, 0)),
        pl.BlockSpec((fb, D_MODEL), lambda i, j: (j, 0)),
    ]
    args = [x, nw, wg, wu, wd]
    saved_shape = (jax.ShapeDtypeStruct((t, D_MODEL), BF16), jax.ShapeDtypeStruct((t, D_FF), BF16),
                   jax.ShapeDtypeStruct((t, D_FF), BF16))
    saved_specs = (pl.BlockSpec((tm, D_MODEL), row), pl.BlockSpec((tm, fb), lambda i, j: (i, j)),
                   pl.BlockSpec((tm, fb), lambda i, j: (i, j)))
    if with_loss:
        in_specs += [pl.BlockSpec((tm, D_MODEL), row), pl.BlockSpec((1, D_MODEL), const)]
        args += [tgt, fnw]
        out_shape = (jax.ShapeDtypeStruct((t, D_MODEL), F32), jax.ShapeDtypeStruct((8, LANES), F32),
                     jax.ShapeDtypeStruct((1, D_MODEL), F32)) + saved_shape
        out_specs = (pl.BlockSpec((tm, D_MODEL), row), pl.BlockSpec((8, LANES), const),
                     pl.BlockSpec((1, D_MODEL), const)) + saved_specs
        sem = ("arbitrary", "arbitrary")
    else:
        out_shape = (jax.ShapeDtypeStruct((t, D_MODEL), F32),) + saved_shape
        out_specs = (pl.BlockSpec((tm, D_MODEL), row),) + saved_specs
        sem = ("parallel", "arbitrary")
    return pl.pallas_call(
        body, name=name, grid=(n_t, n_f), in_specs=in_specs, out_specs=out_specs, out_shape=out_shape,
        scratch_shapes=[pltpu.VMEM((tm, D_MODEL), F32)],
        compiler_params=_cparams(sem),
    )(*args)


def _ffn_bwd_x(x, nw, g, u, wg, wu, wd, dy, *, name):
    t = x.shape[0]
    tm = _tm(t, 256)

    def body(x_ref, nw_ref, g_ref, u_ref, wg_ref, wu_ref, wd_ref, dy_ref, dx_ref, dnw_ref, dg_ref, du_ref, a_ref, dyh_ref):
        i = pl.program_id(0)
        nt = (((1,), (1,)), ((), ()))
        dy = dy_ref[...]
        dyh = (0.5 * dy).astype(BF16)
        dyh_ref[...] = dyh
        gate, up = g_ref[...].astype(F32), u_ref[...].astype(F32)
        s = _sigmoid(gate)
        gs = gate * s
        da = lax.dot_general(dyh, wd_ref[...], nt, preferred_element_type=F32)
        dg = (da * up * (s + gs * (1.0 - s))).astype(BF16)
        du = (da * gs).astype(BF16)
        dg_ref[...] = dg
        du_ref[...] = du
        a_ref[...] = (gs * up).astype(BF16)
        dh = (jnp.dot(dg, wg_ref[...], preferred_element_type=F32)
              + jnp.dot(du, wu_ref[...], preferred_element_type=F32))
        xv = x_ref[...]
        r = lax.rsqrt(jnp.mean(xv * xv, axis=-1, keepdims=True) + EPS)
        dx, dnw = _rms_bwd(dh, xv * r, r, nw_ref[...])
        dx_ref[...] = dy + dx
        _acc_out(dnw_ref, i == 0, dnw)

    row = lambda i: (i, 0)
    const = lambda i: (0, 0)
    once = pl.Buffered(1)
    wide = pl.BlockSpec((tm, D_FF), row)
    return pl.pallas_call(
        body, name=name, grid=(t // tm,),
        in_specs=[pl.BlockSpec((tm, D_MODEL), row), pl.BlockSpec((1, D_MODEL), const), wide, wide,
                  pl.BlockSpec((D_FF, D_MODEL), const, pipeline_mode=once), pl.BlockSpec((D_FF, D_MODEL), const, pipeline_mode=once),
                  pl.BlockSpec((D_FF, D_MODEL), const, pipeline_mode=once), pl.BlockSpec((tm, D_MODEL), row)],
        out_specs=(pl.BlockSpec((tm, D_MODEL), row), pl.BlockSpec((1, D_MODEL), const), wide, wide, wide,
                   pl.BlockSpec((tm, D_MODEL), row)),
        out_shape=(jax.ShapeDtypeStruct((t, D_MODEL), F32), jax.ShapeDtypeStruct((1, D_MODEL), F32),
                   jax.ShapeDtypeStruct((t, D_FF), BF16), jax.ShapeDtypeStruct((t, D_FF), BF16),
                   jax.ShapeDtypeStruct((t, D_FF), BF16), jax.ShapeDtypeStruct((t, D_MODEL), BF16)),
        compiler_params=_cparams(("arbitrary",)),
    )(x, nw, g, u, wg, wu, wd, dy)


def _wgrad(a, b, bm, bn, after=None, *, name):
    k, m = a.shape
    n = b.shape[1]
    tk = _tm(k, 2048)
    n_k = k // tk

    def body(a_ref, b_ref, *rest):
        o_ref, acc_s = rest[-2], rest[-1]
        s = pl.program_id(2)
        part = lax.dot_general(a_ref[...], b_ref[...], (((0,), (0,)), ((), ())), preferred_element_type=F32)
        _acc_out(acc_s, s == 0, part)

        @pl.when(s == n_k - 1)
        def _():
            o_ref[...] = acc_s[...].astype(BF16)

    return pl.pallas_call(
        body, name=name, grid=(m // bm, n // bn, n_k),
        in_specs=[pl.BlockSpec((tk, bm), lambda i, j, s: (s, i)), pl.BlockSpec((tk, bn), lambda i, j, s: (s, j))]
        + ([] if after is None else [_HBM]),
        out_specs=pl.BlockSpec((bm, bn), lambda i, j, s: (i, j)),
        out_shape=jax.ShapeDtypeStruct((m, n), BF16),
        scratch_shapes=[pltpu.VMEM((bm, bn), F32)],
        compiler_params=_cparams(("parallel", "parallel", "arbitrary")),
    )(a, b, *([] if after is None else [after]))


def _ffn_wgrads(h, dg, du, a, dyh, between=None, after=None, *, name):
    grads = []
    for k, (lhs, rhs, tag) in enumerate(((dg, h, "_wg"), (du, h, "_wu"), (a, dyh, "_wd"))):
        grads.append(_wgrad(lhs, rhs, D_FF // 2, D_MODEL, after, name=name + tag))
        after = None if between is None else between(k, grads[-1])
    return grads


def _ffn_bwd(x, nw, h, g, u, wg, wu, wd, dy, *, name):
    dx, dnw, dg, du, a, dyh = _ffn_bwd_x(x, nw, g, u, wg, wu, wd, dy, name=name + "_x")
    return (dx, dnw, *_ffn_wgrads(h, dg, du, a, dyh, name=name))


_PROJ_WIDTHS = (SG_WIDTH, SG_WIDTH, 3 * DN_WIDTH, DN_WIDTH, LANES, LANES)


def _mix_in_fwd(x, nw, ws, *, name):
    t = x.shape[0]
    tm = _tm(t)

    def body(x_ref, nw_ref, *refs):
        w_refs, o_refs = refs[:6], refs[6:]
        h, _, _ = _rms_fwd(x_ref[...], nw_ref[...])
        h = h.astype(BF16)
        for w_ref, o_ref in zip(w_refs, o_refs):
            o_ref[...] = lax.dot_general(h, w_ref[...], (((1,), (1,)), ((), ())), preferred_element_type=F32)

    row = lambda i: (i, 0)
    const = lambda i: (0, 0)
    return pl.pallas_call(
        body, name=name, grid=(t // tm,),
        in_specs=[pl.BlockSpec((tm, D_MODEL), row), pl.BlockSpec((1, D_MODEL), const)]
        + [pl.BlockSpec((n, D_MODEL), const) for n in _PROJ_WIDTHS],
        out_specs=tuple(pl.BlockSpec((tm, n), row) for n in _PROJ_WIDTHS),
        out_shape=tuple(jax.ShapeDtypeStruct((t, n), F32) for n in _PROJ_WIDTHS),
        compiler_params=_cparams(("parallel",)),
    )(x, nw, *ws)


_PROJ_TOTAL = sum(_PROJ_WIDTHS)
_PROJ_OFFSETS = tuple(sum(_PROJ_WIDTHS[:k]) for k in range(len(_PROJ_WIDTHS)))


def _mix_in_bwd(x, nw, ws, dres, dps, *, name):
    t = x.shape[0]
    tm = _tm(t, 256)

    def body(x_ref, nw_ref, dres_ref, *refs):
        w_refs, dp_refs, dx_ref, dnw_ref, h_ref, dpb_ref = refs[:6], refs[6:12], refs[12], refs[13], refs[14], refs[15]
        i = pl.program_id(0)
        hf, xh, r = _rms_fwd(x_ref[...], nw_ref[...])
        h_ref[...] = hf.astype(BF16)
        dh = jnp.zeros((tm, D_MODEL), F32)
        for w_ref, dp_ref, off, width in zip(w_refs, dp_refs, _PROJ_OFFSETS, _PROJ_WIDTHS):
            dp = dp_ref[...].astype(BF16)
            dpb_ref[:, off:off + width] = dp
            dh = dh + jnp.dot(dp, w_ref[...], preferred_element_type=F32)
        dx, dnw = _rms_bwd(dh, xh, r, nw_ref[...])
        dx_ref[...] = dres_ref[...] + dx
        _acc_out(dnw_ref, i == 0, dnw)

    row = lambda i: (i, 0)
    const = lambda i: (0, 0)
    dx, dnw, h, dpb = pl.pallas_call(
        body, name=name + "_x", grid=(t // tm,),
        in_specs=[pl.BlockSpec((tm, D_MODEL), row), pl.BlockSpec((1, D_MODEL), const), pl.BlockSpec((tm, D_MODEL), row)]
        + [pl.BlockSpec((n, D_MODEL), const) for n in _PROJ_WIDTHS]
        + [pl.BlockSpec((tm, n), row) for n in _PROJ_WIDTHS],
        out_specs=(pl.BlockSpec((tm, D_MODEL), row), pl.BlockSpec((1, D_MODEL), const), pl.BlockSpec((tm, D_MODEL), row),
                   pl.BlockSpec((tm, _PROJ_TOTAL), row)),
        out_shape=(jax.ShapeDtypeStruct((t, D_MODEL), F32), jax.ShapeDtypeStruct((1, D_MODEL), F32),
                   jax.ShapeDtypeStruct((t, D_MODEL), BF16), jax.ShapeDtypeStruct((t, _PROJ_TOTAL), BF16)),
        compiler_params=_cparams(("arbitrary",)),
    )(x, nw, dres, *ws, *dps)
    return dx, dnw, _wgrad(dpb, h, _PROJ_TOTAL // 2, D_MODEL, name=name + "_w")


def _sg_fn(u, v, lng, lnb, wcs, sgbt):
    lane = lax.broadcasted_iota(jnp.int32, (1, SG_WIDTH), 1)
    lane_b = lax.broadcasted_iota(jnp.int32, (1, LANES), 1)
    rr = lax.broadcasted_iota(jnp.int32, (SG_CHUNK, SG_CHUNK), 0)
    cc = lax.broadcasted_iota(jnp.int32, (SG_CHUNK, SG_CHUNK), 1)
    gu, gv = _gelu(u), _gelu(v)
    mu = jnp.mean(gv, axis=-1, keepdims=True)
    cen = gv - mu
    var = jnp.mean(cen * cen, axis=-1, keepdims=True)
    ln = cen * lax.rsqrt(var + EPS) * lng + lnb
    vs = jnp.zeros_like(u)
    for g in range(SG_GROUPS):
        in_group = jnp.logical_and(lane >= g * SG_GROUP_DIM, lane < (g + 1) * SG_GROUP_DIM)
        w_causal = jnp.where(rr >= cc, wcs[g], 0.0)
        bias = jnp.sum(jnp.where(lane_b == g, sgbt, 0.0), axis=1, keepdims=True)
        vs = vs + jnp.where(in_group, mm(w_causal, ln) + bias, 0.0)
    return gu * vs


def _sg_fwd(u, v, lng, lnb, wc, sgbt, *, name):
    t = u.shape[0]
    tm = _tm(t)

    def body(u_ref, v_ref, lng_ref, lnb_ref, wc_ref, sgbt_ref, o_ref):
        wcs = [wc_ref[g] for g in range(SG_GROUPS)]
        for c in range(tm // SG_CHUNK):
            rows = pl.ds(c * SG_CHUNK, SG_CHUNK)
            o_ref[rows, :] = _sg_fn(u_ref[rows, :], v_ref[rows, :], lng_ref[...], lnb_ref[...], wcs, sgbt_ref[...])

    row = lambda i: (i, 0)
    const = lambda i: (0, 0)
    return pl.pallas_call(
        body, name=name, grid=(t // tm,),
        in_specs=[pl.BlockSpec((tm, SG_WIDTH), row), pl.BlockSpec((tm, SG_WIDTH), row),
                  pl.BlockSpec((1, SG_WIDTH), const), pl.BlockSpec((1, SG_WIDTH), const),
                  pl.BlockSpec((SG_GROUPS, SG_CHUNK, SG_CHUNK), lambda i: (0, 0, 0)), pl.BlockSpec((SG_CHUNK, LANES), const)],
        out_specs=pl.BlockSpec((tm, SG_WIDTH), row),
        out_shape=jax.ShapeDtypeStruct((t, SG_WIDTH), F32),
        compiler_params=_cparams(("parallel",)),
    )(u, v, lng, lnb, wc, sgbt)


def _sg_bwd(u, v, lng, lnb, wc, sgbt, dout, *, name):
    t = u.shape[0]
    tm = _tm(t)

    def body(u_ref, v_ref, lng_ref, lnb_ref, wc_ref, sgbt_ref, do_ref, du_ref, dv_ref, dlng_ref, dlnb_ref, dwc_ref, dsgbt_ref):
        i = pl.program_id(0)
        wcs = [wc_ref[g] for g in range(SG_GROUPS)]
        tot = None
        for c in range(tm // SG_CHUNK):
            rows = pl.ds(c * SG_CHUNK, SG_CHUNK)
            _, vjp = jax.vjp(_sg_fn, u_ref[rows, :], v_ref[rows, :], lng_ref[...], lnb_ref[...], wcs, sgbt_ref[...])
            du, dv, dlng, dlnb, dwcs, dsgbt = vjp(do_ref[rows, :])
            du_ref[rows, :] = du
            dv_ref[rows, :] = dv
            part = (dlng, dlnb, dwcs, dsgbt)
            tot = part if tot is None else jax.tree.map(jnp.add, tot, part)
        dlng, dlnb, dwcs, dsgbt = tot
        _acc_out(dlng_ref, i == 0, dlng)
        _acc_out(dlnb_ref, i == 0, dlnb)
        _acc_out(dsgbt_ref, i == 0, dsgbt)
        for g in range(SG_GROUPS):
            @pl.when(i == 0)
            def _(g=g):
                dwc_ref[g] = dwcs[g]

            @pl.when(i > 0)
            def _(g=g):
                dwc_ref[g] += dwcs[g]

    row = lambda i: (i, 0)
    const = lambda i: (0, 0)
    wspec = pl.BlockSpec((SG_GROUPS, SG_CHUNK, SG_CHUNK), lambda i: (0, 0, 0))
    return pl.pallas_call(
        body, name=name, grid=(t // tm,),
        in_specs=[pl.BlockSpec((tm, SG_WIDTH), row), pl.BlockSpec((tm, SG_WIDTH), row),
                  pl.BlockSpec((1, SG_WIDTH), const), pl.BlockSpec((1, SG_WIDTH), const), wspec,
                  pl.BlockSpec((SG_CHUNK, LANES), const), pl.BlockSpec((tm, SG_WIDTH), row)],
        out_specs=(pl.BlockSpec((tm, SG_WIDTH), row), pl.BlockSpec((tm, SG_WIDTH), row),
                   pl.BlockSpec((1, SG_WIDTH), const), pl.BlockSpec((1, SG_WIDTH), const), wspec,
                   pl.BlockSpec((SG_CHUNK, LANES), const)),
        out_shape=(jax.ShapeDtypeStruct((t, SG_WIDTH), F32), jax.ShapeDtypeStruct((t, SG_WIDTH), F32),
                   jax.ShapeDtypeStruct((1, SG_WIDTH), F32), jax.ShapeDtypeStruct((1, SG_WIDTH), F32),
                   jax.ShapeDtypeStruct((SG_GROUPS, SG_CHUNK, SG_CHUNK), F32), jax.ShapeDtypeStruct((SG_CHUNK, LANES), F32)),
        compiler_params=_cparams(("arbitrary",)),
    )(u, v, lng, lnb, wc, sgbt, dout)


def _conv_taps(ext, w, tm):
    y = None
    for j in range(CONV_K):
        s = CONV_K - 1 - j
        shifted = ext if s == 0 else pltpu.roll(ext, s, 0)
        term = w[j:j + 1, :] * shifted[HALO:HALO + tm, :]
        y = term if y is None else y + term
    return y


def _post_conv(yq, yk, yv, bpre, apre, alog, dtb):
    def l2(a):
        return a * lax.rsqrt(jnp.sum(a * a, axis=-1, keepdims=True) + EPS)

    q = [l2(_silu(a)) for a in yq]
    k = [l2(_silu(a)) for a in yk]
    return q, k, _silu(yv), _sigmoid(bpre), -jnp.exp(alog) * _softplus(apre + dtb)


def _chunk_tril(tm):
    rr = lax.broadcasted_iota(jnp.int32, (tm, tm), 0)
    cc = lax.broadcasted_iota(jnp.int32, (tm, tm), 1)
    shift = DN_CHUNK.bit_length() - 1
    same = jnp.right_shift(rr, shift) == jnp.right_shift(cc, shift)
    return jnp.where(jnp.logical_and(same, rr >= cc), 1.0, 0.0).astype(F32)


def _halo_specs(tm, width, n_blocks_seq, n_blocks):
    per = tm // HALO
    prev = pl.BlockSpec((HALO, width), lambda i: (jnp.maximum(i * per - 1, 0), 0))
    nxt = pl.BlockSpec((HALO, width), lambda i: (jnp.minimum((i + 1) * per, n_blocks * per - 1), 0))
    return prev, nxt


def _split_heads(ref, base):
    return [ref[:, base + h * DN_HEAD_DIM: base + (h + 1) * DN_HEAD_DIM] for h in range(DN_HEADS)]


def _dn_prep_fwd(qkv, bpre, apre, conv_w, alog, dtb, seq, *, name):
    t = qkv.shape[0]
    tm = _tm(t)
    bps = seq // tm
    cw = 3 * DN_WIDTH

    def body(x_ref, halo_ref, b_ref, a_ref, w_ref, alog_ref, dtb_ref, q_ref, k_ref, v_ref, beta_ref, gc_ref):
        i = pl.program_id(0)
        keep = jnp.where(i % bps == 0, 0.0, 1.0)
        ext = jnp.concatenate([halo_ref[...] * keep, x_ref[...]], axis=0)
        y = _conv_taps(ext, w_ref[...], tm)
        yq = [y[:, h * DN_HEAD_DIM:(h + 1) * DN_HEAD_DIM] for h in range(DN_HEADS)]
        yk = [y[:, DN_WIDTH + h * DN_HEAD_DIM: DN_WIDTH + (h + 1) * DN_HEAD_DIM] for h in range(DN_HEADS)]
        q, k, v, beta, g = _post_conv(yq, yk, y[:, 2 * DN_WIDTH:], b_ref[...], a_ref[...], alog_ref[...], dtb_ref[...])
        for h in range(DN_HEADS):
            q_ref[:, h * DN_HEAD_DIM:(h + 1) * DN_HEAD_DIM] = q[h]
            k_ref[:, h * DN_HEAD_DIM:(h + 1) * DN_HEAD_DIM] = k[h]
        v_ref[...] = v
        beta_ref[...] = beta
        gc_ref[...] = mmx(_chunk_tril(tm), g)

    row = lambda i: (i, 0)
    const = lambda i: (0, 0)
    prev, _ = _halo_specs(tm, cw, bps, t // tm)
    return pl.pallas_call(
        body, name=name, grid=(t // tm,),
        in_specs=[pl.BlockSpec((tm, cw), row), prev, pl.BlockSpec((tm, LANES), row), pl.BlockSpec((tm, LANES), row),
                  pl.BlockSpec((CONV_K, cw), const), pl.BlockSpec((1, LANES), const), pl.BlockSpec((1, LANES), const)],
        out_specs=tuple(pl.BlockSpec((tm, n), row) for n in (DN_WIDTH, DN_WIDTH, DN_WIDTH, LANES, LANES)),
        out_shape=tuple(jax.ShapeDtypeStruct((t, n), F32) for n in (DN_WIDTH, DN_WIDTH, DN_WIDTH, LANES, LANES)),
        compiler_params=_cparams(("parallel",)),
    )(qkv, qkv, bpre, apre, conv_w, alog, dtb)


def _dn_prep_bwd(qkv, bpre, apre, conv_w, alog, dtb, dq, dk, dv, dbeta, dgc, dgc2, seq, *, name):
    t = qkv.shape[0]
    tm = _tm(t)
    bps = seq // tm
    cw = 3 * DN_WIDTH

    def body(x_ref, halo_ref, b_ref, a_ref, w_ref, alog_ref, dtb_ref, dq_ref, dk_ref, dv_ref, dbeta_ref, dgc_ref, dgc2_ref,
             dy_ref, db_ref, da_ref, dalog_ref, ddtb_ref):
        i = pl.program_id(0)
        keep = jnp.where(i % bps == 0, 0.0, 1.0)
        ext = jnp.concatenate([halo_ref[...] * keep, x_ref[...]], axis=0)
        y = _conv_taps(ext, w_ref[...], tm)
        yq = [y[:, h * DN_HEAD_DIM:(h + 1) * DN_HEAD_DIM] for h in range(DN_HEADS)]
        yk = [y[:, DN_WIDTH + h * DN_HEAD_DIM: DN_WIDTH + (h + 1) * DN_HEAD_DIM] for h in range(DN_HEADS)]
        _, vjp = jax.vjp(_post_conv, yq, yk, y[:, 2 * DN_WIDTH:], b_ref[...], a_ref[...], alog_ref[...], dtb_ref[...])
        dg = mmx_tn(_chunk_tril(tm), dgc_ref[...] + dgc2_ref[...])
        dyq, dyk, dyv, db, da, dalog, ddtb = vjp((_split_heads(dq_ref, 0), _split_heads(dk_ref, 0), dv_ref[...],
                                                  dbeta_ref[...], dg))
        for h in range(DN_HEADS):
            dy_ref[:, h * DN_HEAD_DIM:(h + 1) * DN_HEAD_DIM] = dyq[h]
            dy_ref[:, DN_WIDTH + h * DN_HEAD_DIM: DN_WIDTH + (h + 1) * DN_HEAD_DIM] = dyk[h]
        dy_ref[:, 2 * DN_WIDTH:] = dyv
        db_ref[...] = db
        da_ref[...] = da
        _acc_out(dalog_ref, i == 0, dalog)
        _acc_out(ddtb_ref, i == 0, ddtb)

    row = lambda i: (i, 0)
    const = lambda i: (0, 0)
    prev, _ = _halo_specs(tm, cw, bps, t // tm)
    return pl.pallas_call(
        body, name=name, grid=(t // tm,),
        in_specs=[pl.BlockSpec((tm, cw), row), prev, pl.BlockSpec((tm, LANES), row), pl.BlockSpec((tm, LANES), row),
                  pl.BlockSpec((CONV_K, cw), const), pl.BlockSpec((1, LANES), const), pl.BlockSpec((1, LANES), const),
                  pl.BlockSpec((tm, DN_WIDTH), row), pl.BlockSpec((tm, DN_WIDTH), row), pl.BlockSpec((tm, DN_WIDTH), row),
                  pl.BlockSpec((tm, LANES), row), pl.BlockSpec((tm, LANES), row), pl.BlockSpec((tm, LANES), row)],
        out_specs=(pl.BlockSpec((tm, cw), row), pl.BlockSpec((tm, LANES), row), pl.BlockSpec((tm, LANES), row),
                   pl.BlockSpec((1, LANES), const), pl.BlockSpec((1, LANES), const)),
        out_shape=(jax.ShapeDtypeStruct((t, cw), F32), jax.ShapeDtypeStruct((t, LANES), F32), jax.ShapeDtypeStruct((t, LANES), F32),
                   jax.ShapeDtypeStruct((1, LANES), F32), jax.ShapeDtypeStruct((1, LANES), F32)),
        compiler_params=_cparams(("arbitrary",)),
    )(qkv, qkv, bpre, apre, conv_w, alog, dtb, dq, dk, dv, dbeta, dgc, dgc2)


def _conv_bwd(qkv, dy, conv_w, seq, *, name):
    t = qkv.shape[0]
    tm = _tm(t)
    bps = seq // tm
    cw = 3 * DN_WIDTH
    n_ext = tm + HALO

    def body(x_ref, halo_ref, dy_ref, dyn_ref, w_ref, dx_ref, dw_ref):
        i = pl.program_id(0)
        keep_prev = jnp.where(i % bps == 0, 0.0, 1.0)
        keep_next = jnp.where(i % bps == bps - 1, 0.0, 1.0)
        ext = jnp.concatenate([halo_ref[...] * keep_prev, x_ref[...]], axis=0)
        dy = dy_ref[...]
        dyext = jnp.concatenate([dy, dyn_ref[...] * keep_next], axis=0)
        w = w_ref[...]

        @pl.when(i == 0)
        def _():
            dw_ref[...] = jnp.zeros_like(dw_ref)

        dx = None
        for j in range(CONV_K):
            s = CONV_K - 1 - j
            fut = dyext if s == 0 else pltpu.roll(dyext, n_ext - s, 0)
            term = w[j:j + 1, :] * fut[0:tm, :]
            dx = term if dx is None else dx + term
            past = ext if s == 0 else pltpu.roll(ext, s, 0)
            dw_ref[j:j + 1, :] += jnp.sum(dy * past[HALO:HALO + tm, :], axis=0, keepdims=True)
        dx_ref[...] = dx

    row = lambda i: (i, 0)
    const = lambda i: (0, 0)
    prev, nxt = _halo_specs(tm, cw, bps, t // tm)
    return pl.pallas_call(
        body, name=name, grid=(t // tm,),
        in_specs=[pl.BlockSpec((tm, cw), row), prev, pl.BlockSpec((tm, cw), row), nxt, pl.BlockSpec((CONV_K, cw), const)],
        out_specs=(pl.BlockSpec((tm, cw), row), pl.BlockSpec((HALO, cw), const)),
        out_shape=(jax.ShapeDtypeStruct((t, cw), F32), jax.ShapeDtypeStruct((HALO, cw), F32)),
        compiler_params=_cparams(("arbitrary",)),
    )(qkv, qkv, dy, dy, conv_w)


def _inv_unit_lower(l_mats, eye):
    invs = [eye - l for l in l_mats]
    powers = list(l_mats)
    n = 2
    while n < eye.shape[0]:
        powers = [mmh(p, p) for p in powers]
        invs = [inv + mmh(inv, p) for inv, p in zip(invs, powers)]
        n *= 2
    return invs


@jax.custom_vjp
def _solve(l_mat, rhs, inv):
    return mmh(inv, rhs)


def _solve_fwd(l_mat, rhs, inv):
    sol = mmh(inv, rhs)
    return sol, (inv, sol)


def _solve_bwd(res, d_sol):
    inv, sol = res
    d_rhs = mmh_tn(inv, d_sol)
    return -mmh_nt(d_rhs, sol), d_rhs, jnp.zeros_like(inv)


_solve.defvjp(_solve_fwd, _solve_bwd)


def _prep_fn(q, k, v, gc, gr, b, inv):
    ids = range(len(q))
    c = q[0].shape[0]
    rr = lax.broadcasted_iota(jnp.int32, (c, c), 0)
    cc = lax.broadcasted_iota(jnp.int32, (c, c), 1)
    incl, strict = rr >= cc, rr > cc
    is_last = lax.broadcasted_iota(jnp.int32, (c, 1), 0) == c - 1
    qs = [q[i] * (DN_HEAD_DIM ** -0.5) for i in ids]
    decay = [jnp.where(incl, jnp.exp(jnp.where(incl, gc[i] - gr[i], 0.0)), 0.0) for i in ids]
    kb = [k[i] * b[i] for i in ids]
    vb = [v[i] * b[i] for i in ids]
    kk = [mm_nt(kb[i], k[i]) for i in ids]
    l_mat = [jnp.where(strict, kk[i] * decay[i], 0.0) for i in ids]
    eg = [jnp.exp(gc[i]) for i in ids]
    if inv is None:
        inv = _inv_unit_lower(l_mat, jnp.where(rr == cc, 1.0, 0.0).astype(F32))
    u_wy = [_solve(l_mat[i], vb[i], inv[i]) for i in ids]
    w_wy = [_solve(l_mat[i], kb[i] * eg[i], inv[i]) for i in ids]
    qk = [mm_nt(qs[i], k[i]) * decay[i] for i in ids]
    g_last = [jnp.sum(jnp.where(is_last, gc[i], 0.0), axis=0, keepdims=True) for i in ids]
    k_dec = [k[i] * jnp.exp(g_last[i] - gc[i]) for i in ids]
    egl = [jnp.broadcast_to(jnp.exp(g_last[i]), (1, LANES)) for i in ids]
    return [(w_wy[i], u_wy[i], qs[i] * eg[i], k_dec[i], qk[i], egl[i]) for i in ids], inv


def _seq_fn(w, u, qd, kd, qk, egl, s):
    ids = range(len(w))
    ws = [mm(w[i], s[i]) for i in ids]
    qs = [mm(qd[i], s[i]) for i in ids]
    v_new = [u[i] - ws[i] for i in ids]
    o = [qs[i] + mm(qk[i], v_new[i]) for i in ids]
    s_new = [s[i] * egl[i] + mm_tn(kd[i], v_new[i]) for i in ids]
    return o, s_new


def _lane_col(a, h):
    lane = lax.broadcasted_iota(jnp.int32, (1, LANES), 1)
    return jnp.sum(jnp.where(lane == h, a, 0.0), axis=1, keepdims=True)


def _col_lane(col, h):
    lane = lax.broadcasted_iota(jnp.int32, (1, LANES), 1)
    return jnp.where(lane == h, col, 0.0)


def _head_cols(h):
    return slice(h * DN_HEAD_DIM, (h + 1) * DN_HEAD_DIM)


def _chunk_rows(n):
    return pl.ds(pl.multiple_of(n * DN_CHUNK, DN_CHUNK), DN_CHUNK)


def _delta_prep(q, k, v, gc, grow, beta, *, name):
    t = q.shape[0]
    tm = _tm(t)
    cpb = tm // DN_CHUNK
    n_chunks = t // DN_CHUNK
    group = 2

    def body(q_ref, k_ref, v_ref, gc_ref, gr_ref, b_ref, w_ref, u_ref, qd_ref, kd_ref, qk_ref, egl_ref, inv_ref):
        def step(m, carry):
            probs = [(m * group + e, h) for e in range(group) for h in range(DN_HEADS)]
            gcb = [gc_ref[_chunk_rows(m * group + e), :] for e in range(group)]
            bb = [b_ref[_chunk_rows(m * group + e), :] for e in range(group)]
            grb = [gr_ref[m * group + e] for e in range(group)]
            for e in range(group):
                egl_ref[m * group + e] = jnp.zeros((HALO, LANES), F32)
            outs, invs = _prep_fn(
                [q_ref[_chunk_rows(n), _head_cols(h)] for n, h in probs], [k_ref[_chunk_rows(n), _head_cols(h)] for n, h in probs],
                [v_ref[_chunk_rows(n), _head_cols(h)] for n, h in probs],
                [_lane_col(gcb[e], h) for e in range(group) for h in range(DN_HEADS)],
                [grb[e][h:h + 1, :] for e in range(group) for h in range(DN_HEADS)],
                [_lane_col(bb[e], h) for e in range(group) for h in range(DN_HEADS)], None)
            for (n, h), (w, u, qd, kd, qk, egl), inv in zip(probs, outs, invs):
                rows, cols = _chunk_rows(n), _head_cols(h)
                w_ref[rows, cols] = w.astype(BF16)
                u_ref[rows, cols] = u
                qd_ref[rows, cols] = qd.astype(BF16)
                kd_ref[rows, cols] = kd.astype(BF16)
                qk_ref[n, h] = qk
                inv_ref[n, h] = inv
                egl_ref[n, h:h + 1, :] = egl
            return carry

        lax.fori_loop(0, cpb // group, step, 0)

    row = lambda i: (i, 0)
    tok = pl.BlockSpec((tm, DN_WIDTH), row)
    lanes = pl.BlockSpec((tm, LANES), row)
    sq = pl.BlockSpec((cpb, DN_HEADS, DN_CHUNK, DN_CHUNK), lambda i: (i, 0, 0, 0))
    return pl.pallas_call(
        body, name=name, grid=(t // tm,),
        in_specs=[tok, tok, tok, lanes, pl.BlockSpec((cpb, HALO, DN_CHUNK), lambda i: (i, 0, 0)), lanes],
        out_specs=(tok, tok, tok, tok, sq, pl.BlockSpec((cpb, HALO, LANES), lambda i: (i, 0, 0)), sq),
        out_shape=(jax.ShapeDtypeStruct((t, DN_WIDTH), BF16), jax.ShapeDtypeStruct((t, DN_WIDTH), F32),
                   jax.ShapeDtypeStruct((t, DN_WIDTH), BF16), jax.ShapeDtypeStruct((t, DN_WIDTH), BF16),
                   jax.ShapeDtypeStruct((n_chunks, DN_HEADS, DN_CHUNK, DN_CHUNK), F32),
                   jax.ShapeDtypeStruct((n_chunks, HALO, LANES), F32),
                   jax.ShapeDtypeStruct((n_chunks, DN_HEADS, DN_CHUNK, DN_CHUNK), F32)),
        compiler_params=_cparams(("parallel",)),
    )(q, k, v, gc, grow, beta)


def _delta_par_bwd(q, k, v, gc, grow, beta, inv, dw, du, dqd, dkd, dqk, degl, *, name):
    t = q.shape[0]
    tm = _tm(t)
    cpb = tm // DN_CHUNK
    n_chunks = t // DN_CHUNK
    group = 2

    def body(q_ref, k_ref, v_ref, gc_ref, gr_ref, b_ref, inv_ref, dw_ref, du_ref, dqd_ref, dkd_ref, dqk_ref, degl_ref,
             dq_ref, dk_ref, dv_ref, dgc_ref, dgr_ref, db_ref):
        def step(m, carry):
            chunks = [m * group + e for e in range(group)]
            probs = [(e, h) for e in range(group) for h in range(DN_HEADS)]
            rows = [_chunk_rows(n) for n in chunks]
            gcb, bb = [gc_ref[r, :] for r in rows], [b_ref[r, :] for r in rows]
            grb, deglb = [gr_ref[n] for n in chunks], [degl_ref[n] for n in chunks]
            for n in chunks:
                dgr_ref[n] = jnp.zeros((HALO, DN_CHUNK), F32)
            invs = [inv_ref[chunks[e], h] for e, h in probs]
            _, vjp = jax.vjp(lambda *a: _prep_fn(*a, invs)[0],
                             [q_ref[rows[e], _head_cols(h)] for e, h in probs], [k_ref[rows[e], _head_cols(h)] for e, h in probs],
                             [v_ref[rows[e], _head_cols(h)] for e, h in probs], [_lane_col(gcb[e], h) for e, h in probs],
                             [grb[e][h:h + 1, :] for e, h in probs], [_lane_col(bb[e], h) for e, h in probs])
            dq, dk, dv, dgc, dgr, db = vjp([(dw_ref[rows[e], _head_cols(h)], du_ref[rows[e], _head_cols(h)],
                                             dqd_ref[rows[e], _head_cols(h)], dkd_ref[rows[e], _head_cols(h)],
                                             dqk_ref[chunks[e], h], deglb[e][h:h + 1, :]) for e, h in probs])
            dgc_acc = [jnp.zeros((DN_CHUNK, LANES), F32) for _ in chunks]
            db_acc = [jnp.zeros((DN_CHUNK, LANES), F32) for _ in chunks]
            for i, (e, h) in enumerate(probs):
                cols = _head_cols(h)
                dq_ref[rows[e], cols] = dq[i]
                dk_ref[rows[e], cols] = dk[i]
                dv_ref[rows[e], cols] = dv[i]
                dgr_ref[chunks[e], h:h + 1, :] = dgr[i]
                dgc_acc[e] = dgc_acc[e] + _col_lane(dgc[i], h)
                db_acc[e] = db_acc[e] + _col_lane(db[i], h)
            for e in range(group):
                dgc_ref[rows[e], :] = dgc_acc[e]
                db_ref[rows[e], :] = db_acc[e]
            return carry

        lax.fori_loop(0, cpb // group, step, 0)

    row = lambda i: (i, 0)
    tok = pl.BlockSpec((tm, DN_WIDTH), row)
    lanes = pl.BlockSpec((tm, LANES), row)
    sq = pl.BlockSpec((cpb, DN_HEADS, DN_CHUNK, DN_CHUNK), lambda i: (i, 0, 0, 0))
    grs = pl.BlockSpec((cpb, HALO, DN_CHUNK), lambda i: (i, 0, 0))
    return pl.pallas_call(
        body, name=name, grid=(t // tm,),
        in_specs=[tok, tok, tok, lanes, grs, lanes, sq, tok, tok, tok, tok, sq, pl.BlockSpec((cpb, HALO, LANES), lambda i: (i, 0, 0))],
        out_specs=(tok, tok, tok, lanes, grs, lanes),
        out_shape=(jax.ShapeDtypeStruct((t, DN_WIDTH), F32),) * 3
        + (jax.ShapeDtypeStruct((t, LANES), F32), jax.ShapeDtypeStruct((n_chunks, HALO, DN_CHUNK), F32),
           jax.ShapeDtypeStruct((t, LANES), F32)),
        compiler_params=_cparams(("parallel",)),
    )(q, k, v, gc, grow, beta, inv, dw, du, dqd, dkd, dqk, degl)


def _seq_specs(n_seq, seq, reverse):
    tm = _tm(seq)
    nb = seq // tm
    cpb = tm // DN_CHUNK
    blk = (lambda b, j: b * nb + nb - 1 - j) if reverse else (lambda b, j: b * nb + j)
    tok = pl.BlockSpec((tm, DN_WIDTH), lambda b, j: (blk(b, j), 0))
    sq = pl.BlockSpec((cpb, DN_HEADS, DN_CHUNK, DN_CHUNK), lambda b, j: (blk(b, j), 0, 0, 0))
    rows8 = pl.BlockSpec((cpb, HALO, LANES), lambda b, j: (blk(b, j), 0, 0))
    state = pl.BlockSpec((cpb, DN_HEADS, DN_HEAD_DIM, DN_HEAD_DIM), lambda b, j: (blk(b, j), 0, 0, 0))
    return nb, cpb, tok, sq, rows8, state


def _delta_seq_fwd(w, u, qd, kd, qk, egl, n_seq, seq, *, name):
    nb, cpb, tok, sq, rows8, state = _seq_specs(n_seq, seq, False)
    t = n_seq * seq

    def body(w_ref, u_ref, qd_ref, kd_ref, qk_ref, egl_ref, o_ref, st_ref, s_s):
        @pl.when(pl.program_id(1) == 0)
        def _():
            s_s[...] = jnp.zeros_like(s_s)

        def step(n, carry):
            rows = _chunk_rows(n)
            heads = range(DN_HEADS)
            eglb = egl_ref[n]
            s = [s_s[h] for h in heads]
            for h in heads:
                st_ref[n, h] = s[h]
            o, s_new = _seq_fn([w_ref[rows, _head_cols(h)] for h in heads], [u_ref[rows, _head_cols(h)] for h in heads],
                               [qd_ref[rows, _head_cols(h)] for h in heads], [kd_ref[rows, _head_cols(h)] for h in heads],
                               [qk_ref[n, h] for h in heads], [eglb[h:h + 1, :] for h in heads], s)
            for h in heads:
                o_ref[rows, _head_cols(h)] = o[h]
                s_s[h] = s_new[h]
            return carry

        lax.fori_loop(0, cpb, step, 0)

    return pl.pallas_call(
        body, name=name, grid=(n_seq, nb),
        in_specs=[tok, tok, tok, tok, sq, rows8],
        out_specs=(tok, state),
        out_shape=(jax.ShapeDtypeStruct((t, DN_WIDTH), F32),
                   jax.ShapeDtypeStruct((t // DN_CHUNK, DN_HEADS, DN_HEAD_DIM, DN_HEAD_DIM), F32)),
        scratch_shapes=[pltpu.VMEM((DN_HEADS, DN_HEAD_DIM, DN_HEAD_DIM), F32)],
        compiler_params=_cparams(("parallel", "arbitrary")),
    )(w, u, qd, kd, qk, egl)


def _delta_seq_bwd(w, u, qd, kd, qk, egl, states, do, n_seq, seq, *, name):
    nb, cpb, tok, sq, rows8, state = _seq_specs(n_seq, seq, True)
    t = n_seq * seq

    def body(w_ref, u_ref, qd_ref, kd_ref, qk_ref, egl_ref, st_ref, do_ref, dw_ref, du_ref, dqd_ref, dkd_ref, dqk_ref,
             degl_ref, ds_s):
        @pl.when(pl.program_id(1) == 0)
        def _():
            ds_s[...] = jnp.zeros_like(ds_s)

        def step(m, carry):
            n = cpb - 1 - m
            rows = _chunk_rows(n)
            eglb = egl_ref[n]
            degl_ref[n] = jnp.zeros((HALO, LANES), F32)
            heads = range(DN_HEADS)
            _, vjp = jax.vjp(_seq_fn, [w_ref[rows, _head_cols(h)].astype(F32) for h in heads],
                             [u_ref[rows, _head_cols(h)] for h in heads],
                             [qd_ref[rows, _head_cols(h)].astype(F32) for h in heads],
                             [kd_ref[rows, _head_cols(h)].astype(F32) for h in heads],
                             [qk_ref[n, h] for h in heads], [eglb[h:h + 1, :] for h in heads], [st_ref[n, h] for h in heads])
            dw, du, dqd, dkd, dqk, degl, ds_in = vjp(([do_ref[rows, _head_cols(h)] for h in heads], [ds_s[h] for h in heads]))
            for h in heads:
                cols = _head_cols(h)
                dw_ref[rows, cols] = dw[h]
                du_ref[rows, cols] = du[h]
                dqd_ref[rows, cols] = dqd[h]
                dkd_ref[rows, cols] = dkd[h]
                dqk_ref[n, h] = dqk[h]
                degl_ref[n, h:h + 1, :] = degl[h]
                ds_s[h] = ds_in[h]
            return carry

        lax.fori_loop(0, cpb, step, 0)

    return pl.pallas_call(
        body, name=name, grid=(n_seq, nb),
        in_specs=[tok, tok, tok, tok, sq, rows8, state, tok],
        out_specs=(tok, tok, tok, tok, sq, rows8),
        out_shape=(jax.ShapeDtypeStruct((t, DN_WIDTH), F32),) * 4
        + (jax.ShapeDtypeStruct((t // DN_CHUNK, DN_HEADS, DN_CHUNK, DN_CHUNK), F32),
           jax.ShapeDtypeStruct((t // DN_CHUNK, HALO, LANES), F32)),
        scratch_shapes=[pltpu.VMEM((DN_HEADS, DN_HEAD_DIM, DN_HEAD_DIM), F32)],
        compiler_params=_cparams(("parallel", "arbitrary")),
    )(w, u, qd, kd, qk, egl, states, do)


def _dn_gate(o, z, dnw):
    return o * lax.rsqrt(jnp.mean(o * o, axis=-1, keepdims=True) + EPS) * dnw * _silu(z)


def _mix_out_fwd(x, sg, o, z, wo_sg, wo_dn, dnw, *, name):
    t = x.shape[0]
    tm = _tm(t)

    def body(x_ref, sg_ref, o_ref, z_ref, wsg_ref, wdn_ref, dnw_ref, y_ref, dn_s):
        for h, (oh, zh) in enumerate(zip(_split_heads(o_ref, 0), _split_heads(z_ref, 0))):
            dn_s[:, h * DN_HEAD_DIM:(h + 1) * DN_HEAD_DIM] = _dn_gate(oh, zh, dnw_ref[...]).astype(BF16)
        y_ref[...] = (x_ref[...] + jnp.dot(sg_ref[...].astype(BF16), wsg_ref[...], preferred_element_type=F32)
                      + jnp.dot(dn_s[...], wdn_ref[...], preferred_element_type=F32))

    row = lambda i: (i, 0)
    const = lambda i: (0, 0)
    half = pl.BlockSpec((tm, DN_WIDTH), row)
    return pl.pallas_call(
        body, name=name, grid=(t // tm,),
        in_specs=[pl.BlockSpec((tm, D_MODEL), row), half, half, half, pl.BlockSpec((SG_WIDTH, D_MODEL), const),
                  pl.BlockSpec((DN_WIDTH, D_MODEL), const), pl.BlockSpec((1, DN_HEAD_DIM), const)],
        out_specs=pl.BlockSpec((tm, D_MODEL), row),
        out_shape=jax.ShapeDtypeStruct((t, D_MODEL), F32),
        scratch_shapes=[pltpu.VMEM((tm, DN_WIDTH), BF16)],
        compiler_params=_cparams(("parallel",)),
    )(x, sg, o, z, wo_sg, wo_dn, dnw)


def _mix_out_bwd(dy, sg, o, z, wo_sg, wo_dn, dnw, *, name):
    t = dy.shape[0]
    tm = _tm(t)

    def body(dy_ref, sg_ref, o_ref, z_ref, wsg_ref, wdn_ref, dnw_ref, dsg_ref, do_ref, dz_ref, dwsg_ref, dwdn_ref, ddnw_ref, dn_s):
        i = pl.program_id(0)
        dyb = dy_ref[...].astype(BF16)
        nt = (((1,), (1,)), ((), ()))
        tn = (((0,), (0,)), ((), ()))
        dsg_ref[...] = lax.dot_general(dyb, wsg_ref[...], nt, preferred_element_type=F32)
        ddn = lax.dot_general(dyb, wdn_ref[...], nt, preferred_element_type=F32)
        ddnw = None
        for h, (oh, zh) in enumerate(zip(_split_heads(o_ref, 0), _split_heads(z_ref, 0))):
            cols = slice(h * DN_HEAD_DIM, (h + 1) * DN_HEAD_DIM)
            out, vjp = jax.vjp(_dn_gate, oh, zh, dnw_ref[...])
            dn_s[:, cols] = out.astype(BF16)
            doh, dzh, dw = vjp(ddn[:, cols])
            do_ref[:, cols] = doh
            dz_ref[:, cols] = dzh
            ddnw = dw if ddnw is None else ddnw + dw
        _acc_out(ddnw_ref, i == 0, ddnw)
        _acc_out(dwsg_ref, i == 0, lax.dot_general(sg_ref[...].astype(BF16), dyb, tn, preferred_element_type=F32))
        _acc_out(dwdn_ref, i == 0, lax.dot_general(dn_s[...], dyb, tn, preferred_element_type=F32))

    row = lambda i: (i, 0)
    const = lambda i: (0, 0)
    half = pl.BlockSpec((tm, DN_WIDTH), row)
    wspec = pl.BlockSpec((DN_WIDTH, D_MODEL), const)
    return pl.pallas_call(
        body, name=name, grid=(t // tm,),
        in_specs=[pl.BlockSpec((tm, D_MODEL), row), half, half, half, wspec, wspec, pl.BlockSpec((1, DN_HEAD_DIM), const)],
        out_specs=(half, half, half, wspec, wspec, pl.BlockSpec((1, DN_HEAD_DIM), const)),
        out_shape=(jax.ShapeDtypeStruct((t, DN_WIDTH), F32),) * 3 + (jax.ShapeDtypeStruct((DN_WIDTH, D_MODEL), F32),) * 2
        + (jax.ShapeDtypeStruct((1, DN_HEAD_DIM), F32),),
        scratch_shapes=[pltpu.VMEM((tm, DN_WIDTH), BF16)],
        compiler_params=_cparams(("arbitrary",)),
    )(dy, sg, o, z, wo_sg, wo_dn, dnw)


_MESH = pl.DeviceIdType.MESH
_HBM = pl.BlockSpec(memory_space=pl.ANY)


def _mesh_pos():
    x, y, c = lax.axis_index("x"), lax.axis_index("y"), lax.axis_index("c")
    return x, y, c, [(1 - x, y), (x, 1 - y), (1 - x, 1 - y)]


def _gather2(arrs, *, name):
    n = len(arrs)
    slots = N_DEV - 1

    def body(*refs):
        in_refs, out_refs = refs[:n], refs[n:2 * n]
        send_sems, recv_sems, local_sems = refs[2 * n:]
        x, y, c, chips = _mesh_pos()
        me, sibling = (x, y, c), (x, y, 1 - c)

        def copy(k, slot, block, to, src=None):
            dst = out_refs[k].at[4 * block[0] + 2 * block[1] + block[2]]
            return pltpu.make_async_remote_copy(src_ref=dst if src is None else src, dst_ref=dst,
                                                send_sem=send_sems.at[k * slots + slot], recv_sem=recv_sems.at[k * slots + slot],
                                                device_id=to, device_id_type=_MESH)

        local = [pltpu.make_async_copy(in_refs[k], out_refs[k].at[4 * x + 2 * y + c], local_sems.at[k]) for k in range(n)]
        sent = []
        for k in range(n):
            sent.append(copy(k, 0, me, sibling, src=in_refs[k]))
            sent += [copy(k, 1 + j, me, (*chip, c), src=in_refs[k]) for j, chip in enumerate(chips)]
        for cp in local + sent:
            cp.start()
        for j, chip in enumerate(chips):
            for k in range(n):
                copy(k, 1 + j, (*chip, c), me).wait_recv()
                passed = copy(k, 4 + j, (*chip, c), sibling)
                passed.start()
                sent.append(passed)
        for k in range(n):
            copy(k, 0, sibling, me).wait_recv()
            for j, chip in enumerate(chips):
                copy(k, 4 + j, (*chip, 1 - c), me).wait_recv()
        for cp in sent:
            cp.wait_send()
        for cp in local:
            cp.wait()

    return pl.pallas_call(
        body, name=name, in_specs=[_HBM] * n, out_specs=(_HBM,) * n,
        out_shape=tuple(jax.ShapeDtypeStruct((N_DEV,) + a.shape, a.dtype) for a in arrs),
        scratch_shapes=[pltpu.SemaphoreType.DMA((n * slots,)), pltpu.SemaphoreType.DMA((n * slots,)),
                        pltpu.SemaphoreType.DMA((n,))],
    )(*arrs)


def _pair_swap(arrs, *, name):
    n = len(arrs)

    def body(*refs):
        in_refs, out_refs, send_sems, recv_sems = refs[:n], refs[n:2 * n], refs[2 * n], refs[2 * n + 1]
        x, y, c, _ = _mesh_pos()
        copies = [pltpu.make_async_remote_copy(src_ref=in_refs[k].at[1 - c], dst_ref=out_refs[k], send_sem=send_sems.at[k],
                                               recv_sem=recv_sems.at[k], device_id=(x, y, 1 - c), device_id_type=_MESH)
                  for k in range(n)]
        for cp in copies:
            cp.start()
        for cp in copies:
            cp.wait()

    return pl.pallas_call(
        body, name=name, in_specs=[_HBM] * n, out_specs=(_HBM,) * n,
        out_shape=tuple(jax.ShapeDtypeStruct(a.shape[1:], a.dtype) for a in arrs),
        scratch_shapes=[pltpu.SemaphoreType.DMA((n,)), pltpu.SemaphoreType.DMA((n,))],
    )(*arrs)


def _chip_exchange(arrs, *, name):
    n = len(arrs)
    slots = 3

    def body(*refs):
        in_refs, out_refs = refs[:n], refs[n:2 * n]
        send_sems, recv_sems, local_sems = refs[2 * n:]
        x, y, c, chips = _mesh_pos()
        mine = 2 * x + y
        copies = [pltpu.make_async_copy(in_refs[k].at[mine], out_refs[k].at[mine], local_sems.at[k]) for k in range(n)]
        for j, chip in enumerate(chips):
            for k in range(n):
                copies.append(pltpu.make_async_remote_copy(
                    src_ref=in_refs[k].at[2 * chip[0] + chip[1]], dst_ref=out_refs[k].at[mine],
                    send_sem=send_sems.at[k * slots + j], recv_sem=recv_sems.at[k * slots + j],
                    device_id=(*chip, c), device_id_type=_MESH))
        for cp in copies:
            cp.start()
        for cp in copies:
            cp.wait()

    return pl.pallas_call(
        body, name=name, in_specs=[_HBM] * n, out_specs=(_HBM,) * n,
        out_shape=tuple(jax.ShapeDtypeStruct(a.shape, a.dtype) for a in arrs),
        scratch_shapes=[pltpu.SemaphoreType.DMA((n * slots,)), pltpu.SemaphoreType.DMA((n * slots,)),
                        pltpu.SemaphoreType.DMA((n,))],
    )(*arrs)


_SEM = pl.BlockSpec(memory_space=pltpu.SEMAPHORE)
_EFFECT = pltpu.SideEffectType.DATAFLOW_SIDE_EFFECTING


def _direct_copies(src_refs, land_refs, send_sems, recv_sems, gather):
    x, y, c, _ = _mesh_pos()
    me = 4 * x + 2 * y + c
    n_peer = N_DEV - 1
    copies = []
    for r in range(1, N_DEV):
        px = 1 - x if r & 4 else x
        py = 1 - y if r & 2 else y
        pc = 1 - c if r & 1 else c
        for k, (src, land) in enumerate(zip(src_refs, land_refs)):
            copies.append(pltpu.make_async_remote_copy(
                src_ref=src if gather else src.at[4 * px + 2 * py + pc], dst_ref=land.at[me],
                send_sem=send_sems.at[k * n_peer + r - 1], recv_sem=recv_sems.at[k * n_peer + r - 1],
                device_id=(px, py, pc), device_id_type=_MESH))
    return copies


def _send_start(arrs, gather, *, name):
    n = len(arrs)
    lands = [lax.empty(((N_DEV,) + a.shape) if gather else a.shape, a.dtype) for a in arrs]

    def body(*refs):
        src_refs, land_refs, send_sems, recv_sems, token = refs[:n], refs[n:2 * n], refs[2 * n], refs[2 * n + 1], refs[-1]
        for cp in _direct_copies(src_refs, land_refs, send_sems, recv_sems, gather):
            cp.start()
        token[...] = jnp.zeros_like(token)

    n_sem = n * (N_DEV - 1)
    bufs = list(arrs) + lands
    out = pl.pallas_call(
        body, name=name,
        out_shape=(pltpu.SemaphoreType.DMA((n_sem,)), pltpu.SemaphoreType.DMA((n_sem,)))
        + tuple(pltpu.HBM(b.shape, b.dtype) for b in bufs) + (jax.ShapeDtypeStruct((HALO, LANES), F32),),
        in_specs=[_HBM] * (2 * n), out_specs=(_SEM, _SEM) + (_HBM,) * (2 * n) + (pl.BlockSpec(memory_space=pltpu.VMEM),),
        input_output_aliases={i: 2 + i for i in range(2 * n)},
        compiler_params=pltpu.CompilerParams(has_side_effects=_EFFECT),
    )(*[pltpu.with_memory_space_constraint(b, pltpu.HBM) for b in bufs])
    return (out[0], out[1], list(out[2:2 + n]), list(out[2 + n:2 + 2 * n])), out[-1]


def _send_wait(started, gather, after, *, name):
    send_sems, recv_sems, srcs, lands = started
    n = len(srcs)

    def body(*refs):
        src_refs, land_refs, send_ref, recv_ref = refs[:n], refs[n:2 * n], refs[2 * n], refs[2 * n + 1]
        for cp in _direct_copies(src_refs, land_refs, send_ref, recv_ref, gather):
            cp.wait_send()
            cp.wait_recv()

    bufs = srcs + lands
    out = pl.pallas_call(
        body, name=name, out_shape=tuple(pltpu.HBM(b.shape, b.dtype) for b in bufs),
        in_specs=[_HBM] * (2 * n) + [_SEM, _SEM, _HBM], out_specs=(_HBM,) * (2 * n),
        input_output_aliases={i: i for i in range(2 * n)},
        compiler_params=pltpu.CompilerParams(has_side_effects=_EFFECT),
    )(*bufs, send_sems, recv_sems, after)
    return list(out[n:])


def _pair_add(p, r, core, *, name):
    _, n_chip, rows, cols = p.shape
    rb = _row_block(rows)

    def body(core_ref, p_ref, r_ref, o_ref):
        o_ref[...] = (p_ref[...].astype(F32) + r_ref[...].astype(F32)).astype(BF16)

    return pl.pallas_call(
        body, name=name,
        grid_spec=pltpu.PrefetchScalarGridSpec(
            num_scalar_prefetch=1, grid=(n_chip, rows // rb),
            in_specs=[pl.BlockSpec((None, None, rb, cols), lambda s, i, core_ref: (core_ref[0], s, i, 0)),
                      pl.BlockSpec((None, rb, cols), lambda s, i, core_ref: (s, i, 0))],
            out_specs=pl.BlockSpec((None, rb, cols), lambda s, i, core_ref: (s, i, 0))),
        out_shape=jax.ShapeDtypeStruct((n_chip, rows, cols), BF16),
        compiler_params=_cparams(("parallel", "parallel")),
    )(core, p, r)


def _row_block(rows, limit=256):
    best = rows
    for cand in range(8, limit + 1, 8):
        if rows % cand == 0:
            best = cand
    return best if rows > limit else rows


def _adam(gp, w, m, v, *, name):
    p, rows, cols = gp.shape
    rb = _row_block(rows)

    def body(gp_ref, w_ref, m_ref, v_ref, g_ref, d_ref, m2_ref, v2_ref):
        g = gp_ref[0].astype(F32)
        for s in range(1, p):
            g = g + gp_ref[s].astype(F32)
        m2 = ADAM_B1 * m_ref[...] + (1.0 - ADAM_B1) * g
        v2 = ADAM_B2 * v_ref[...] + (1.0 - ADAM_B2) * (g * g)
        m_hat = m2 / (1.0 - ADAM_B1 ** ADAM_STEP)
        v_hat = v2 / (1.0 - ADAM_B2 ** ADAM_STEP)
        g_ref[...] = g
        d_ref[...] = -ADAM_LR * (m_hat / (jnp.sqrt(v_hat) + ADAM_EPS) + ADAM_WD * w_ref[...])
        m2_ref[...] = m2
        v2_ref[...] = v2

    blk = pl.BlockSpec((rb, cols), lambda i: (i, 0))
    return pl.pallas_call(
        body, name=name, grid=(rows // rb,),
        in_specs=[pl.BlockSpec((p, rb, cols), lambda i: (0, i, 0)), blk, blk, blk],
        out_specs=(blk,) * 4, out_shape=(jax.ShapeDtypeStruct((rows, cols), F32),) * 4,
        compiler_params=_cparams(("parallel",)),
    )(gp, w, m, v)


def _cols_full(g):
    return jnp.transpose(g, (1, 0, 2)).reshape(g.shape[1], N_DEV * g.shape[2])


def _pad_lanes(a, width=LANES):
    return jnp.pad(a, ((0, 0), (0, width - a.shape[1])))


def _chunk_rows_of(a):
    by_chunk = jnp.transpose(a[:, :DN_HEADS].reshape(-1, DN_CHUNK, DN_HEADS), (0, 2, 1))
    return jnp.pad(by_chunk, ((0, 0), (0, HALO - DN_HEADS), (0, 0)))


_SMALL = (("ffn1_norm", D_MODEL), ("mix_norm", D_MODEL), ("ffn2_norm", D_MODEL), ("final_norm", D_MODEL), ("a_log", DN_HEADS),
          ("dt_bias", DN_HEADS), ("dn_norm", DN_HEAD_DIM), ("sg_ln_g", SG_WIDTH), ("sg_ln_b", SG_WIDTH),
          ("sg_w", SG_GROUPS * SG_CHUNK * SG_CHUNK), ("sg_b", SG_GROUPS * SG_CHUNK), ("conv_w", CONV_K * 3 * DN_WIDTH))
_SMALL_ROWS = 1128
_SMALL_SHAPES = {"ffn1_norm": (1, D_MODEL), "mix_norm": (1, D_MODEL), "ffn2_norm": (1, D_MODEL), "final_norm": (D_MODEL,),
                 "a_log": (1, DN_HEADS), "dt_bias": (1, DN_HEADS), "dn_norm": (1, DN_HEAD_DIM), "sg_ln_g": (1, SG_WIDTH),
                 "sg_ln_b": (1, SG_WIDTH), "sg_w": (1, SG_GROUPS, SG_CHUNK, SG_CHUNK), "sg_b": (1, SG_GROUPS, SG_CHUNK)}


def _pack_small(d):
    flat = jnp.concatenate([d[name].reshape(-1) for name, _ in _SMALL])
    return jnp.pad(flat, (0, _SMALL_ROWS * LANES - flat.shape[0])).reshape(_SMALL_ROWS, LANES)


def _unpack_small(a):
    flat, out, at = a.reshape(-1), {}, 0
    for name, size in _SMALL:
        out[name] = flat[at:at + size]
        at += size
    return out


def kernel(x, ffn1_norm, ffn1_w_gate, ffn1_w_up, ffn1_w_down, mix_norm, w_in, conv_w, a_log, dt_bias, dn_norm, sg_ln_g, sg_ln_b, sg_w, sg_b, w_out, ffn2_norm, ffn2_w_gate, ffn2_w_up, ffn2_w_down, final_norm, loss_target, m_ffn1_norm, m_ffn1_w_gate, m_ffn1_w_up, m_ffn1_w_down, m_mix_norm, m_w_in, m_conv_w, m_a_log, m_dt_bias, m_dn_norm, m_sg_ln_g, m_sg_ln_b, m_sg_w, m_sg_b, m_w_out, m_ffn2_norm, m_ffn2_w_gate, m_ffn2_w_up, m_ffn2_w_down, m_final_norm, v_ffn1_norm, v_ffn1_w_gate, v_ffn1_w_up, v_ffn1_w_down, v_mix_norm, v_w_in, v_conv_w, v_a_log, v_dt_bias, v_dn_norm, v_sg_ln_g, v_sg_ln_b, v_sg_w, v_sg_b, v_w_out, v_ffn2_norm, v_ffn2_w_gate, v_ffn2_w_up, v_ffn2_w_down, v_final_norm):
    weights = dict(ffn1_norm=ffn1_norm, ffn1_w_gate=ffn1_w_gate, ffn1_w_up=ffn1_w_up, ffn1_w_down=ffn1_w_down, mix_norm=mix_norm, w_in=w_in, conv_w=conv_w, a_log=a_log, dt_bias=dt_bias, dn_norm=dn_norm, sg_ln_g=sg_ln_g, sg_ln_b=sg_ln_b, sg_w=sg_w, sg_b=sg_b, w_out=w_out, ffn2_norm=ffn2_norm, ffn2_w_gate=ffn2_w_gate, ffn2_w_up=ffn2_w_up, ffn2_w_down=ffn2_w_down, final_norm=final_norm)
    mom_m = dict(ffn1_norm=m_ffn1_norm, ffn1_w_gate=m_ffn1_w_gate, ffn1_w_up=m_ffn1_w_up, ffn1_w_down=m_ffn1_w_down, mix_norm=m_mix_norm, w_in=m_w_in, conv_w=m_conv_w, a_log=m_a_log, dt_bias=m_dt_bias, dn_norm=m_dn_norm, sg_ln_g=m_sg_ln_g, sg_ln_b=m_sg_ln_b, sg_w=m_sg_w, sg_b=m_sg_b, w_out=m_w_out, ffn2_norm=m_ffn2_norm, ffn2_w_gate=m_ffn2_w_gate, ffn2_w_up=m_ffn2_w_up, ffn2_w_down=m_ffn2_w_down, final_norm=m_final_norm)
    mom_v = dict(ffn1_norm=v_ffn1_norm, ffn1_w_gate=v_ffn1_w_gate, ffn1_w_up=v_ffn1_w_up, ffn1_w_down=v_ffn1_w_down, mix_norm=v_mix_norm, w_in=v_w_in, conv_w=v_conv_w, a_log=v_a_log, dt_bias=v_dt_bias, dn_norm=v_dn_norm, sg_ln_g=v_sg_ln_g, sg_ln_b=v_sg_ln_b, sg_w=v_sg_w, sg_b=v_sg_b, w_out=v_w_out, ffn2_norm=v_ffn2_norm, ffn2_w_gate=v_ffn2_w_gate, ffn2_w_up=v_ffn2_w_up, ffn2_w_down=v_ffn2_w_down, final_norm=v_final_norm)
    order = list(weights)
    big = ("ffn1_w_gate", "ffn1_w_up", "ffn1_w_down", "w_in", "w_out", "ffn2_w_gate", "ffn2_w_up", "ffn2_w_down")
    col_sharded = ("ffn1_w_gate", "ffn1_w_up", "w_in", "ffn2_w_gate", "ffn2_w_up")

    n_seq, seq, _ = x.shape
    t = n_seq * seq
    me = 4 * lax.axis_index("x") + 2 * lax.axis_index("y") + lax.axis_index("c")
    x0 = x.reshape(t, D_MODEL)
    tgt = loss_target.reshape(t, D_MODEL)

    def fill_own(land, own_block):
        return lax.dynamic_update_index_in_dim(land, own_block, me, 0)

    def rows_view(n, a):
        return jnp.transpose(a) if n in col_sharded else a

    def as_full(n, g):
        return g.reshape(-1, g.shape[-1])

    shards = {n: rows_view(n, weights[n][0]).astype(BF16) for n in big}
    ffn1_names, mix_names, ffn2_names = big[:3], big[3:5], big[5:]
    full = {n: as_full(n, g) for n, g in zip(ffn1_names, _gather2([shards[n] for n in ffn1_names], name="gather_ffn1"))}
    mix_srcs = [shards[n] for n in mix_names] + [conv_w[0]]
    mix_started, mix_token = _send_start(mix_srcs, True, name="gather_mix_start")
    ffn2_started, ffn2_token = _send_start([shards[n] for n in ffn2_names], True, name="gather_ffn2_start")
    ffn1_norm_fwd = ffn1_norm + (mix_token[:1, :1] + ffn2_token[:1, :1])
    alog, dtb = _pad_lanes(a_log), _pad_lanes(dt_bias)
    sgbt = _pad_lanes(sg_b[0].T)
    fnw = final_norm.reshape(1, D_MODEL)

    x1, h1, g1, u1 = _ffn_fwd(x0, ffn1_norm_fwd, full["ffn1_w_gate"], full["ffn1_w_up"], full["ffn1_w_down"], name="ffn1_fwd")
    mix_lands = [fill_own(land, src) for land, src in zip(_send_wait(mix_started, True, x1, name="gather_mix_wait"), mix_srcs)]
    full.update({n: as_full(n, g) for n, g in zip(mix_names, mix_lands)})
    conv_full = _cols_full(mix_lands[-1])
    w_in_t = full["w_in"]
    offs = (0, SG_WIDTH, 2 * SG_WIDTH, 2 * SG_WIDTH + 3 * DN_WIDTH, 2 * SG_WIDTH + 4 * DN_WIDTH)
    n_proj = offs[-1]

    def pad_rows(a):
        return jnp.pad(a, ((0, LANES - a.shape[0]), (0, 0)))

    ws = [w_in_t[offs[0]:offs[1]], w_in_t[offs[1]:offs[2]], w_in_t[offs[2]:offs[3]], w_in_t[offs[3]:offs[4]],
          pad_rows(w_in_t[n_proj:n_proj + DN_HEADS]), pad_rows(w_in_t[n_proj + DN_HEADS:n_proj + 2 * DN_HEADS])]
    wo_sg, wo_dn = full["w_out"][:SG_WIDTH], full["w_out"][SG_WIDTH:]
    u, v, qkv, z, bpre, apre = _mix_in_fwd(x1, mix_norm, ws, name="mix_in_fwd")
    sg_out = _sg_fwd(u, v, sg_ln_g, sg_ln_b, sg_w[0], sgbt, name="sg_fwd")
    q, k, vv, beta, gc = _dn_prep_fwd(qkv, bpre, apre, conv_full, alog, dtb, seq, name="dn_prep_fwd")
    grow = _chunk_rows_of(gc)
    wy_w, wy_u, q_dec, k_dec, qk, egl, inv = _delta_prep(q, k, vv, gc, grow, beta, name="delta_prep")
    o, states = _delta_seq_fwd(wy_w, wy_u, q_dec, k_dec, qk, egl, n_seq, seq, name="delta_seq_fwd")
    x2 = _mix_out_fwd(x1, sg_out, o, z, wo_sg, wo_dn, dn_norm, name="mix_out_fwd")
    ffn2_lands = _send_wait(ffn2_started, True, x2, name="gather_ffn2_wait")
    full.update({n: as_full(n, fill_own(land, shards[n])) for n, land in zip(ffn2_names, ffn2_lands)})
    dx3, loss_part, d_fn, h2, g2, u2 = _ffn_fwd(x2, ffn2_norm, full["ffn2_w_gate"], full["ffn2_w_up"], full["ffn2_w_down"],
                                                tgt, fnw, name="ffn2_fwd_loss")
    loss = lax.psum(loss_part[0, 0], ("x", "y", "c"))

    dx2, d_n2, d_g2, d_u2, d_d2 = _ffn_bwd(x2, ffn2_norm, h2, g2, u2, full["ffn2_w_gate"], full["ffn2_w_up"],
                                           full["ffn2_w_down"], dx3, name="ffn2_bwd")
    def by_owner(d_rows):
        return d_rows.reshape(N_DEV, -1, D_MODEL)

    ffn2_pieces = [by_owner(d_g2), by_owner(d_u2), by_owner(d_d2)]
    ffn2_sent, sent_token = _send_start(ffn2_pieces, False, name="grads_ffn2_start")
    dsg, do, dz, d_wo_sg, d_wo_dn, d_dnw = _mix_out_bwd(dx2, sg_out, o, z, wo_sg, wo_dn, dn_norm + sent_token[:1, :1],
                                                        name="mix_out_bwd")
    d_seq = _delta_seq_bwd(wy_w, wy_u, q_dec, k_dec, qk, egl, states, do, n_seq, seq, name="delta_seq_bwd")
    dq, dk, dv, dgc_a, dgrow, dbeta = _delta_par_bwd(q, k, vv, gc, grow, beta, inv, *d_seq, name="delta_par_bwd")
    dgc_b = _pad_lanes(jnp.transpose(dgrow[:, :DN_HEADS, :], (0, 2, 1)).reshape(t, DN_HEADS))
    dy_conv, dbpre, dapre, d_alog, d_dtb = _dn_prep_bwd(qkv, bpre, apre, conv_full, alog, dtb, dq, dk, dv, dbeta, dgc_a, dgc_b,
                                                        seq, name="dn_prep_bwd")
    dqkv, d_conv = _conv_bwd(qkv, dy_conv, conv_full, seq, name="conv_bwd")
    du, dvv, d_lng, d_lnb, d_wc, d_sgbt = _sg_bwd(u, v, sg_ln_g, sg_ln_b, sg_w[0], sgbt, dsg, name="sg_bwd")
    dx1, d_mixn, d_wp = _mix_in_bwd(x1, mix_norm, ws, dx2, (du, dvv, dqkv, dz, dbpre, dapre), name="mix_in_bwd")
    d_w_in_t = jnp.concatenate([d_wp[:n_proj], d_wp[_PROJ_OFFSETS[4]:_PROJ_OFFSETS[4] + DN_HEADS],
                                d_wp[_PROJ_OFFSETS[5]:_PROJ_OFFSETS[5] + DN_HEADS]], axis=0)
    d_w_out = jnp.concatenate([d_wo_sg, d_wo_dn], axis=0)
    mix_pieces = [by_owner(d_w_in_t), by_owner(d_w_out).astype(BF16)]
    mix_sent, sent_token = _send_start(mix_pieces, False, name="grads_mix_start")
    grad_x, d_n1, dg1, du1, a1, dyh1 = _ffn_bwd_x(x0, ffn1_norm + sent_token[:1, :1], g1, u1, full["ffn1_w_gate"],
                                                  full["ffn1_w_up"], full["ffn1_w_down"], dx1, name="ffn1_bwd_x")
    small_grads = dict(ffn1_norm=d_n1, mix_norm=d_mixn, ffn2_norm=d_n2, final_norm=d_fn, a_log=d_alog[:, :DN_HEADS],
                       dt_bias=d_dtb[:, :DN_HEADS], dn_norm=d_dnw, sg_ln_g=d_lng, sg_ln_b=d_lnb, sg_w=d_wc,
                       sg_b=d_sgbt[:, :SG_GROUPS].T, conv_w=d_conv[:CONV_K])
    small_src = _pack_small(small_grads)
    small_sent, small_token = _send_start([small_src], True, name="small_grads_start")
    late = []

    def send_early(k, grad):
        if k == 2:
            return None
        piece = by_owner(grad)
        sent, token = _send_start([piece], False, name="grads_" + ffn1_names[k] + "_start")
        late.append(((ffn1_names[k],), sent, [piece]))
        return token

    _, _, d_d1 = _ffn_wgrads(h1, dg1, du1, a1, dyh1, send_early, small_token, name="ffn1_bwd")

    def by_core(p8):
        return jnp.moveaxis(p8.reshape((4, 2) + p8.shape[1:]), 1, 0)

    own = [by_core(by_owner(d_d1))]
    from_sibling = _pair_swap(own, name="grads_to_sibling")
    core = lax.axis_index("c").astype(jnp.int32).reshape(1)
    chip_sums = [_pair_add(own[0], from_sibling[0], core, name="pair_add_" + ffn1_names[2])]
    received = {ffn1_names[2]: _chip_exchange(chip_sums, name="grads_to_owner")[0]}
    for names, sent, pieces in [(ffn2_names, ffn2_sent, ffn2_pieces), (mix_names, mix_sent, mix_pieces)] + late:
        lands = _send_wait(sent, False, received[ffn1_names[2]], name="grads_" + names[0] + "_wait")
        received.update({n: fill_own(land, lax.dynamic_index_in_dim(p, me, 0, keepdims=False))
                         for n, land, p in zip(names, lands, pieces)})
    res = {}
    for n in big:
        upd = _adam(received[n], *[rows_view(n, src[n][0]) for src in (weights, mom_m, mom_v)], name="adam_" + n)
        res[n] = [rows_view(n, a) for a in upd]

    (small_land,) = _send_wait(small_sent, True, received[ffn1_names[2]], name="small_grads_wait")
    small_parts = fill_own(small_land, small_src)
    zeros_conv = jnp.zeros((CONV_K * 3 * DN_WIDTH,), F32)
    packed = [_pack_small({**{n: src[n] for n, _ in _SMALL if n != "conv_w"}, "conv_w": zeros_conv})
              for src in (weights, mom_m, mom_v)]
    small_res = [_unpack_small(a) for a in _adam(small_parts, *packed, name="adam_small")]
    conv_grad = lax.dynamic_slice_in_dim(small_res[0]["conv_w"].reshape(CONV_K, 3 * DN_WIDTH), me * (3 * DN_WIDTH // N_DEV),
                                         3 * DN_WIDTH // N_DEV, axis=1)
    res["conv_w"] = _adam(conv_grad[None], conv_w[0], m_conv_w[0], v_conv_w[0], name="adam_conv_w")

    outs = [[], [], [], []]
    for n in order:
        for kind in range(4):
            if n in res:
                outs[kind].append(res[n][kind][None])
            else:
                outs[kind].append(small_res[kind][n].reshape(_SMALL_SHAPES[n]))
    return (loss, grad_x.reshape(x.shape), *outs[0], *outs[1], *outs[2], *outs[3])
```

```python
import functools

import jax
import jax.numpy as jnp
from jax import lax
from jax.experimental import pallas as pl
from jax.experimental.pallas import tpu as pltpu

F32 = jnp.float32
BF16 = jnp.bfloat16

D_MODEL = 1024
D_FF = 2816
SG_WIDTH = 512
SG_GROUPS = 8
SG_GROUP_DIM = 64
SG_CHUNK = 128
DN_WIDTH = 512
DN_HEAD_DIM = 128
DN_HEADS = 4
DN_CHUNK = 64
CONV_K = 4
EPS = 1e-6
N_DEV = 8
LANES = 128
HALO = 8

ADAM_LR = 0.001
ADAM_B1 = 0.9
ADAM_B2 = 0.999
ADAM_EPS = 1e-08
ADAM_WD = 0.01
ADAM_STEP = 10

VMEM_LIMIT = 60 * 1024 * 1024
TOKEN_BLOCK = 512
FF_BLOCK_FWD = 1408

_HI = lax.Precision.HIGHEST


def _cparams(sem):
    return pltpu.CompilerParams(dimension_semantics=sem, vmem_limit_bytes=VMEM_LIMIT)


def _tm(t, pref=TOKEN_BLOCK):
    return min(pref, t)


def _dg(a, b, ca, cb, precision):
    if precision is not None:
        return lax.dot_general(a, b, (((ca,), (cb,)), ((), ())), precision=precision, preferred_element_type=F32)
    return lax.dot_general(a.astype(BF16), b.astype(BF16), (((ca,), (cb,)), ((), ())), preferred_element_type=F32)


def _make_mm(exact):
    @jax.custom_vjp
    def mm(a, b):
        return _dg(a, b, 1, 0, exact)

    @jax.custom_vjp
    def mm_nt(a, b):
        return _dg(a, b, 1, 1, exact)

    @jax.custom_vjp
    def mm_tn(a, b):
        return _dg(a, b, 0, 0, exact)

    mm.defvjp(lambda a, b: (mm(a, b), (a, b)), lambda r, g: (mm_nt(g, r[1]), mm_tn(r[0], g)))
    mm_nt.defvjp(lambda a, b: (mm_nt(a, b), (a, b)), lambda r, g: (mm(g, r[1]), mm_tn(g, r[0])))
    mm_tn.defvjp(lambda a, b: (mm_tn(a, b), (a, b)), lambda r, g: (mm_nt(r[1], g), mm(r[0], g)))
    return mm, mm_nt, mm_tn


mm, mm_nt, mm_tn = _make_mm(None)
mmx, mmx_nt, mmx_tn = _make_mm(_HI)
mmh, mmh_nt, mmh_tn = _make_mm(lax.Precision.HIGH)


def _sigmoid(x):
    return 1.0 / (1.0 + jnp.exp(-x))


def _silu(x):
    return x * _sigmoid(x)


def _softplus(x):
    neg_abs = jnp.where(x > 0, -x, x)
    return jnp.where(x > 0, x, 0.0) + jnp.log(1.0 + jnp.exp(neg_abs))


def _gelu(x):
    return 0.5 * x * (1.0 + jnp.tanh(0.7978845608028654 * (x + 0.044715 * (x * x * x))))


def _rms_fwd(x, g):
    r = lax.rsqrt(jnp.mean(x * x, axis=-1, keepdims=True) + EPS)
    xh = x * r
    return xh * g, xh, r


def _rms_bwd(dh, xh, r, g):
    dxh = dh * g
    dx = r * (dxh - xh * jnp.mean(dxh * xh, axis=-1, keepdims=True))
    return dx, jnp.sum(dh * xh, axis=0, keepdims=True)


def _acc_out(ref, first, val):
    @pl.when(first)
    def _():
        ref[...] = val

    @pl.when(jnp.logical_not(first))
    def _():
        ref[...] += val


def _ffn_fwd(x, nw, wg, wu, wd, tgt=None, fnw=None, *, name):
    t = x.shape[0]
    tm, fb = _tm(t), FF_BLOCK_FWD
    n_t, n_f = t // tm, D_FF // fb
    with_loss = tgt is not None

    def body(*refs):
        if with_loss:
            (x_ref, nw_ref, wg_ref, wu_ref, wd_ref, tgt_ref, fnw_ref, dy_ref, loss_ref, dfn_ref, h_ref, g_ref, u_ref,
             acc_s) = refs
        else:
            x_ref, nw_ref, wg_ref, wu_ref, wd_ref, y_ref, h_ref, g_ref, u_ref, acc_s = refs
        i, j = pl.program_id(0), pl.program_id(1)

        @pl.when(j == 0)
        def _():
            h, _, _ = _rms_fwd(x_ref[...], nw_ref[...])
            h_ref[...] = h.astype(BF16)
            acc_s[...] = jnp.zeros_like(acc_s)

        h = h_ref[...]
        nt = (((1,), (1,)), ((), ()))
        g = lax.dot_general(h, wg_ref[...], nt, preferred_element_type=F32)
        u = lax.dot_general(h, wu_ref[...], nt, preferred_element_type=F32)
        g_ref[...] = g.astype(BF16)
        u_ref[...] = u.astype(BF16)
        a = _silu(g) * u
        acc_s[...] += jnp.dot(a.astype(BF16), wd_ref[...], preferred_element_type=F32)

        @pl.when(j == n_f - 1)
        def _():
            y = x_ref[...] + 0.5 * acc_s[...]
            if not with_loss:
                y_ref[...] = y
            else:
                gf = fnw_ref[...]
                out, xh, r = _rms_fwd(y, gf)
                err = out - tgt_ref[...]
                part = 0.5 * jnp.sum(jnp.mean(err * err, axis=-1, keepdims=True), axis=0, keepdims=True)
                d_out = err * (1.0 / D_MODEL)
                dy, dgf = _rms_bwd(d_out, xh, r, gf)
                dy_ref[...] = dy
                _acc_out(loss_ref, i == 0, jnp.broadcast_to(part, loss_ref.shape))
                _acc_out(dfn_ref, i == 0, dgf)

    row = lambda i, j: (i, 0)
    const = lambda i, j: (0, 0)
    in_specs = [
        pl.BlockSpec((tm, D_MODEL), row),
        pl.BlockSpec((1, D_MODEL), const),
        pl.BlockSpec((fb, D_MODEL), lambda i, j: (j, 0)),
        pl.BlockSpec((fb, D_MODEL), lambda i, j: (j, 0)),
        pl.BlockSpec((fb, D_MODEL), lambda i, j: (j, 0)),
    ]
    args = [x, nw, wg, wu, wd]
    saved_shape = (jax.ShapeDtypeStruct((t, D_MODEL), BF16), jax.ShapeDtypeStruct((t, D_FF), BF16),
                   jax.ShapeDtypeStruct((t, D_FF), BF16))
    saved_specs = (pl.BlockSpec((tm, D_MODEL), row), pl.BlockSpec((tm, fb), lambda i, j: (i, j)),
                   pl.BlockSpec((tm, fb), lambda i, j: (i, j)))
    if with_loss:
        in_specs += [pl.BlockSpec((tm, D_MODEL), row), pl.BlockSpec((1, D_MODEL), const)]
        args += [tgt, fnw]
        out_shape = (jax.ShapeDtypeStruct((t, D_MODEL), F32), jax.ShapeDtypeStruct((8, LANES), F32),
                     jax.ShapeDtypeStruct((1, D_MODEL), F32)) + saved_shape
        out_specs = (pl.BlockSpec((tm, D_MODEL), row), pl.BlockSpec((8, LANES), const),
                     pl.BlockSpec((1, D_MODEL), const)) + saved_specs
        sem = ("arbitrary", "arbitrary")
    else:
        out_shape = (jax.ShapeDtypeStruct((t, D_MODEL), F32),) + saved_shape
        out_specs = (pl.BlockSpec((tm, D_MODEL), row),) + saved_specs
        sem = ("parallel", "arbitrary")
    return pl.pallas_call(
        body, name=name, grid=(n_t, n_f), in_specs=in_specs, out_specs=out_specs, out_shape=out_shape,
        scratch_shapes=[pltpu.VMEM((tm, D_MODEL), F32)],
        compiler_params=_cparams(sem),
    )(*args)


def _ffn_bwd_x(x, nw, g, u, wg, wu, wd, dy, *, name):
    t = x.shape[0]
    tm = _tm(t, 256)

    def body(x_ref, nw_ref, g_ref, u_ref, wg_ref, wu_ref, wd_ref, dy_ref, dx_ref, dnw_ref, dg_ref, du_ref, a_ref, dyh_ref):
        i = pl.program_id(0)
        nt = (((1,), (1,)), ((), ()))
        dy = dy_ref[...]
        dyh = (0.5 * dy).astype(BF16)
        dyh_ref[...] = dyh
        gate, up = g_ref[...].astype(F32), u_ref[...].astype(F32)
        s = _sigmoid(gate)
        gs = gate * s
        da = lax.dot_general(dyh, wd_ref[...], nt, preferred_element_type=F32)
        dg = (da * up * (s + gs * (1.0 - s))).astype(BF16)
        du = (da * gs).astype(BF16)
        dg_ref[...] = dg
        du_ref[...] = du
        a_ref[...] = (gs * up).astype(BF16)
        dh = (jnp.dot(dg, wg_ref[...], preferred_element_type=F32)
              + jnp.dot(du, wu_ref[...], preferred_element_type=F32))
        xv = x_ref[...]
        r = lax.rsqrt(jnp.mean(xv * xv, axis=-1, keepdims=True) + EPS)
        dx, dnw = _rms_bwd(dh, xv * r, r, nw_ref[...])
        dx_ref[...] = dy + dx
        _acc_out(dnw_ref, i == 0, dnw)

    row = lambda i: (i, 0)
    const = lambda i: (0, 0)
    once = pl.Buffered(1)
    wide = pl.BlockSpec((tm, D_FF), row)
    return pl.pallas_call(
        body, name=name, grid=(t // tm,),
        in_specs=[pl.BlockSpec((tm, D_MODEL), row), pl.BlockSpec((1, D_MODEL), const), wide, wide,
                  pl.BlockSpec((D_FF, D_MODEL), const, pipeline_mode=once), pl.BlockSpec((D_FF, D_MODEL), const, pipeline_mode=once),
                  pl.BlockSpec((D_FF, D_MODEL), const, pipeline_mode=once), pl.BlockSpec((tm, D_MODEL), row)],
        out_specs=(pl.BlockSpec((tm, D_MODEL), row), pl.BlockSpec((1, D_MODEL), const), wide, wide, wide,
                   pl.BlockSpec((tm, D_MODEL), row)),
        out_shape=(jax.ShapeDtypeStruct((t, D_MODEL), F32), jax.ShapeDtypeStruct((1, D_MODEL), F32),
                   jax.ShapeDtypeStruct((t, D_FF), BF16), jax.ShapeDtypeStruct((t, D_FF), BF16),
                   jax.ShapeDtypeStruct((t, D_FF), BF16), jax.ShapeDtypeStruct((t, D_MODEL), BF16)),
        compiler_params=_cparams(("arbitrary",)),
    )(x, nw, g, u, wg, wu, wd, dy)


def _wgrad(a, b, bm, bn, after=None, *, name):
    k, m = a.shape
    n = b.shape[1]
    tk = _tm(k, 2048)
    n_k = k // tk

    def body(a_ref, b_ref, *rest):
        o_ref, acc_s = rest[-2], rest[-1]
        s = pl.program_id(2)
        part = lax.dot_general(a_ref[...], b_ref[...], (((0,), (0,)), ((), ())), preferred_element_type=F32)
        _acc_out(acc_s, s == 0, part)

        @pl.when(s == n_k - 1)
        def _():
            o_ref[...] = acc_s[...].astype(BF16)

    return pl.pallas_call(
        body, name=name, grid=(m // bm, n // bn, n_k),
        in_specs=[pl.BlockSpec((tk, bm), lambda i, j, s: (s, i)), pl.BlockSpec((tk, bn), lambda i, j, s: (s, j))]
        + ([] if after is None else [_HBM]),
        out_specs=pl.BlockSpec((bm, bn), lambda i, j, s: (i, j)),
        out_shape=jax.ShapeDtypeStruct((m, n), BF16),
        scratch_shapes=[pltpu.VMEM((bm, bn), F32)],
        compiler_params=_cparams(("parallel", "parallel", "arbitrary")),
    )(a, b, *([] if after is None else [after]))


def _ffn_wgrads(h, dg, du, a, dyh, between=None, after=None, *, name):
    grads = []
    for k, (lhs, rhs, tag) in enumerate(((dg, h, "_wg"), (du, h, "_wu"), (a, dyh, "_wd"))):
        grads.append(_wgrad(lhs, rhs, D_FF // 2, D_MODEL, after, name=name + tag))
        after = None if between is None else between(k, grads[-1])
    return grads


def _ffn_bwd(x, nw, h, g, u, wg, wu, wd, dy, *, name):
    dx, dnw, dg, du, a, dyh = _ffn_bwd_x(x, nw, g, u, wg, wu, wd, dy, name=name + "_x")
    return (dx, dnw, *_ffn_wgrads(h, dg, du, a, dyh, name=name))


_PROJ_WIDTHS = (SG_WIDTH, SG_WIDTH, 3 * DN_WIDTH, DN_WIDTH, LANES, LANES)


def _mix_in_fwd(x, nw, ws, *, name):
    t = x.shape[0]
    tm = _tm(t)

    def body(x_ref, nw_ref, *refs):
        w_refs, o_refs = refs[:6], refs[6:]
        h, _, _ = _rms_fwd(x_ref[...], nw_ref[...])
        h = h.astype(BF16)
        for w_ref, o_ref in zip(w_refs, o_refs):
            o_ref[...] = lax.dot_general(h, w_ref[...], (((1,), (1,)), ((), ())), preferred_element_type=F32)

    row = lambda i: (i, 0)
    const = lambda i: (0, 0)
    return pl.pallas_call(
        body, name=name, grid=(t // tm,),
        in_specs=[pl.BlockSpec((tm, D_MODEL), row), pl.BlockSpec((1, D_MODEL), const)]
        + [pl.BlockSpec((n, D_MODEL), const) for n in _PROJ_WIDTHS],
        out_specs=tuple(pl.BlockSpec((tm, n), row) for n in _PROJ_WIDTHS),
        out_shape=tuple(jax.ShapeDtypeStruct((t, n), F32) for n in _PROJ_WIDTHS),
        compiler_params=_cparams(("parallel",)),
    )(x, nw, *ws)


_PROJ_TOTAL = sum(_PROJ_WIDTHS)
_PROJ_OFFSETS = tuple(sum(_PROJ_WIDTHS[:k]) for k in range(len(_PROJ_WIDTHS)))


def _mix_in_bwd(x, nw, ws, dres, dps, *, name):
    t = x.shape[0]
    tm = _tm(t, 256)

    def body(x_ref, nw_ref, dres_ref, *refs):
        w_refs, dp_refs, dx_ref, dnw_ref, h_ref, dpb_ref = refs[:6], refs[6:12], refs[12], refs[13], refs[14], refs[15]
        i = pl.program_id(0)
        hf, xh, r = _rms_fwd(x_ref[...], nw_ref[...])
        h_ref[...] = hf.astype(BF16)
        dh = jnp.zeros((tm, D_MODEL), F32)
        for w_ref, dp_ref, off, width in zip(w_refs, dp_refs, _PROJ_OFFSETS, _PROJ_WIDTHS):
            dp = dp_ref[...].astype(BF16)
            dpb_ref[:, off:off + width] = dp
            dh = dh + jnp.dot(dp, w_ref[...], preferred_element_type=F32)
        dx, dnw = _rms_bwd(dh, xh, r, nw_ref[...])
        dx_ref[...] = dres_ref[...] + dx
        _acc_out(dnw_ref, i == 0, dnw)

    row = lambda i: (i, 0)
    const = lambda i: (0, 0)
    dx, dnw, h, dpb = pl.pallas_call(
        body, name=name + "_x", grid=(t // tm,),
        in_specs=[pl.BlockSpec((tm, D_MODEL), row), pl.BlockSpec((1, D_MODEL), const), pl.BlockSpec((tm, D_MODEL), row)]
        + [pl.BlockSpec((n, D_MODEL), const) for n in _PROJ_WIDTHS]
        + [pl.BlockSpec((tm, n), row) for n in _PROJ_WIDTHS],
        out_specs=(pl.BlockSpec((tm, D_MODEL), row), pl.BlockSpec((1, D_MODEL), const), pl.BlockSpec((tm, D_MODEL), row),
                   pl.BlockSpec((tm, _PROJ_TOTAL), row)),
        out_shape=(jax.ShapeDtypeStruct((t, D_MODEL), F32), jax.ShapeDtypeStruct((1, D_MODEL), F32),
                   jax.ShapeDtypeStruct((t, D_MODEL), BF16), jax.ShapeDtypeStruct((t, _PROJ_TOTAL), BF16)),
        compiler_params=_cparams(("arbitrary",)),
    )(x, nw, dres, *ws, *dps)
    return dx, dnw, _wgrad(dpb, h, _PROJ_TOTAL // 2, D_MODEL, name=name + "_w")


def _sg_fn(u, v, lng, lnb, wcs, sgbt):
    lane = lax.broadcasted_iota(jnp.int32, (1, SG_WIDTH), 1)
    lane_b = lax.broadcasted_iota(jnp.int32, (1, LANES), 1)
    rr = lax.broadcasted_iota(jnp.int32, (SG_CHUNK, SG_CHUNK), 0)
    cc = lax.broadcasted_iota(jnp.int32, (SG_CHUNK, SG_CHUNK), 1)
    gu, gv = _gelu(u), _gelu(v)
    mu = jnp.mean(gv, axis=-1, keepdims=True)
    cen = gv - mu
    var = jnp.mean(cen * cen, axis=-1, keepdims=True)
    ln = cen * lax.rsqrt(var + EPS) * lng + lnb
    vs = jnp.zeros_like(u)
    for g in range(SG_GROUPS):
        in_group = jnp.logical_and(lane >= g * SG_GROUP_DIM, lane < (g + 1) * SG_GROUP_DIM)
        w_causal = jnp.where(rr >= cc, wcs[g], 0.0)
        bias = jnp.sum(jnp.where(lane_b == g, sgbt, 0.0), axis=1, keepdims=True)
        vs = vs + jnp.where(in_group, mm(w_causal, ln) + bias, 0.0)
    return gu * vs


def _sg_fwd(u, v, lng, lnb, wc, sgbt, *, name):
    t = u.shape[0]
    tm = _tm(t)

    def body(u_ref, v_ref, lng_ref, lnb_ref, wc_ref, sgbt_ref, o_ref):
        wcs = [wc_ref[g] for g in range(SG_GROUPS)]
        for c in range(tm // SG_CHUNK):
            rows = pl.ds(c * SG_CHUNK, SG_CHUNK)
            o_ref[rows, :] = _sg_fn(u_ref[rows, :], v_ref[rows, :], lng_ref[...], lnb_ref[...], wcs, sgbt_ref[...])

    row = lambda i: (i, 0)
    const = lambda i: (0, 0)
    return pl.pallas_call(
        body, name=name, grid=(t // tm,),
        in_specs=[pl.BlockSpec((tm, SG_WIDTH), row), pl.BlockSpec((tm, SG_WIDTH), row),
                  pl.BlockSpec((1, SG_WIDTH), const), pl.BlockSpec((1, SG_WIDTH), const),
                  pl.BlockSpec((SG_GROUPS, SG_CHUNK, SG_CHUNK), lambda i: (0, 0, 0)), pl.BlockSpec((SG_CHUNK, LANES), const)],
        out_specs=pl.BlockSpec((tm, SG_WIDTH), row),
        out_shape=jax.ShapeDtypeStruct((t, SG_WIDTH), F32),
        compiler_params=_cparams(("parallel",)),
    )(u, v, lng, lnb, wc, sgbt)


def _sg_bwd(u, v, lng, lnb, wc, sgbt, dout, *, name):
    t = u.shape[0]
    tm = _tm(t)

    def body(u_ref, v_ref, lng_ref, lnb_ref, wc_ref, sgbt_ref, do_ref, du_ref, dv_ref, dlng_ref, dlnb_ref, dwc_ref, dsgbt_ref):
        i = pl.program_id(0)
        wcs = [wc_ref[g] for g in range(SG_GROUPS)]
        tot = None
        for c in range(tm // SG_CHUNK):
            rows = pl.ds(c * SG_CHUNK, SG_CHUNK)
            _, vjp = jax.vjp(_sg_fn, u_ref[rows, :], v_ref[rows, :], lng_ref[...], lnb_ref[...], wcs, sgbt_ref[...])
            du, dv, dlng, dlnb, dwcs, dsgbt = vjp(do_ref[rows, :])
            du_ref[rows, :] = du
            dv_ref[rows, :] = dv
            part = (dlng, dlnb, dwcs, dsgbt)
            tot = part if tot is None else jax.tree.map(jnp.add, tot, part)
        dlng, dlnb, dwcs, dsgbt = tot
        _acc_out(dlng_ref, i == 0, dlng)
        _acc_out(dlnb_ref, i == 0, dlnb)
        _acc_out(dsgbt_ref, i == 0, dsgbt)
        for g in range(SG_GROUPS):
            @pl.when(i == 0)
            def _(g=g):
                dwc_ref[g] = dwcs[g]

            @pl.when(i > 0)
            def _(g=g):
                dwc_ref[g] += dwcs[g]

    row = lambda i: (i, 0)
    const = lambda i: (0, 0)
    wspec = pl.BlockSpec((SG_GROUPS, SG_CHUNK, SG_CHUNK), lambda i: (0, 0, 0))
    return pl.pallas_call(
        body, name=name, grid=(t // tm,),
        in_specs=[pl.BlockSpec((tm, SG_WIDTH), row), pl.BlockSpec((tm, SG_WIDTH), row),
                  pl.BlockSpec((1, SG_WIDTH), const), pl.BlockSpec((1, SG_WIDTH), const), wspec,
                  pl.BlockSpec((SG_CHUNK, LANES), const), pl.BlockSpec((tm, SG_WIDTH), row)],
        out_specs=(pl.BlockSpec((tm, SG_WIDTH), row), pl.BlockSpec((tm, SG_WIDTH), row),
                   pl.BlockSpec((1, SG_WIDTH), const), pl.BlockSpec((1, SG_WIDTH), const), wspec,
                   pl.BlockSpec((SG_CHUNK, LANES), const)),
        out_shape=(jax.ShapeDtypeStruct((t, SG_WIDTH), F32), jax.ShapeDtypeStruct((t, SG_WIDTH), F32),
                   jax.ShapeDtypeStruct((1, SG_WIDTH), F32), jax.ShapeDtypeStruct((1, SG_WIDTH), F32),
                   jax.ShapeDtypeStruct((SG_GROUPS, SG_CHUNK, SG_CHUNK), F32), jax.ShapeDtypeStruct((SG_CHUNK, LANES), F32)),
        compiler_params=_cparams(("arbitrary",)),
    )(u, v, lng, lnb, wc, sgbt, dout)


def _conv_taps(ext, w, tm):
    y = None
    for j in range(CONV_K):
        s = CONV_K - 1 - j
        shifted = ext if s == 0 else pltpu.roll(ext, s, 0)
        term = w[j:j + 1, :] * shifted[HALO:HALO + tm, :]
        y = term if y is None else y + term
    return y


def _post_conv(yq, yk, yv, bpre, apre, alog, dtb):
    def l2(a):
        return a * lax.rsqrt(jnp.sum(a * a, axis=-1, keepdims=True) + EPS)

    q = [l2(_silu(a)) for a in yq]
    k = [l2(_silu(a)) for a in yk]
    return q, k, _silu(yv), _sigmoid(bpre), -jnp.exp(alog) * _softplus(apre + dtb)


def _chunk_tril(tm):
    rr = lax.broadcasted_iota(jnp.int32, (tm, tm), 0)
    cc = lax.broadcasted_iota(jnp.int32, (tm, tm), 1)
    shift = DN_CHUNK.bit_length() - 1
    same = jnp.right_shift(rr, shift) == jnp.right_shift(cc, shift)
    return jnp.where(jnp.logical_and(same, rr >= cc), 1.0, 0.0).astype(F32)


def _halo_specs(tm, width, n_blocks_seq, n_blocks):
    per = tm // HALO
    prev = pl.BlockSpec((HALO, width), lambda i: (jnp.maximum(i * per - 1, 0), 0))
    nxt = pl.BlockSpec((HALO, width), lambda i: (jnp.minimum((i + 1) * per, n_blocks * per - 1), 0))
    return prev, nxt


def _split_heads(ref, base):
    return [ref[:, base + h * DN_HEAD_DIM: base + (h + 1) * DN_HEAD_DIM] for h in range(DN_HEADS)]


def _dn_prep_fwd(qkv, bpre, apre, conv_w, alog, dtb, seq, *, name):
    t = qkv.shape[0]
    tm = _tm(t)
    bps = seq // tm
    cw = 3 * DN_WIDTH

    def body(x_ref, halo_ref, b_ref, a_ref, w_ref, alog_ref, dtb_ref, q_ref, k_ref, v_ref, beta_ref, gc_ref):
        i = pl.program_id(0)
        keep = jnp.where(i % bps == 0, 0.0, 1.0)
        ext = jnp.concatenate([halo_ref[...] * keep, x_ref[...]], axis=0)
        y = _conv_taps(ext, w_ref[...], tm)
        yq = [y[:, h * DN_HEAD_DIM:(h + 1) * DN_HEAD_DIM] for h in range(DN_HEADS)]
        yk = [y[:, DN_WIDTH + h * DN_HEAD_DIM: DN_WIDTH + (h + 1) * DN_HEAD_DIM] for h in range(DN_HEADS)]
        q, k, v, beta, g = _post_conv(yq, yk, y[:, 2 * DN_WIDTH:], b_ref[...], a_ref[...], alog_ref[...], dtb_ref[...])
        for h in range(DN_HEADS):
            q_ref[:, h * DN_HEAD_DIM:(h + 1) * DN_HEAD_DIM] = q[h]
            k_ref[:, h * DN_HEAD_DIM:(h + 1) * DN_HEAD_DIM] = k[h]
        v_ref[...] = v
        beta_ref[...] = beta
        gc_ref[...] = mmx(_chunk_tril(tm), g)

    row = lambda i: (i, 0)
    const = lambda i: (0, 0)
    prev, _ = _halo_specs(tm, cw, bps, t // tm)
    return pl.pallas_call(
        body, name=name, grid=(t // tm,),
        in_specs=[pl.BlockSpec((tm, cw), row), prev, pl.BlockSpec((tm, LANES), row), pl.BlockSpec((tm, LANES), row),
                  pl.BlockSpec((CONV_K, cw), const), pl.BlockSpec((1, LANES), const), pl.BlockSpec((1, LANES), const)],
        out_specs=tuple(pl.BlockSpec((tm, n), row) for n in (DN_WIDTH, DN_WIDTH, DN_WIDTH, LANES, LANES)),
        out_shape=tuple(jax.ShapeDtypeStruct((t, n), F32) for n in (DN_WIDTH, DN_WIDTH, DN_WIDTH, LANES, LANES)),
        compiler_params=_cparams(("parallel",)),
    )(qkv, qkv, bpre, apre, conv_w, alog, dtb)


def _dn_prep_bwd(qkv, bpre, apre, conv_w, alog, dtb, dq, dk, dv, dbeta, dgc, dgc2, seq, *, name):
    t = qkv.shape[0]
    tm = _tm(t)
    bps = seq // tm
    cw = 3 * DN_WIDTH

    def body(x_ref, halo_ref, b_ref, a_ref, w_ref, alog_ref, dtb_ref, dq_ref, dk_ref, dv_ref, dbeta_ref, dgc_ref, dgc2_ref,
             dy_ref, db_ref, da_ref, dalog_ref, ddtb_ref):
        i = pl.program_id(0)
        keep = jnp.where(i % bps == 0, 0.0, 1.0)
        ext = jnp.concatenate([halo_ref[...] * keep, x_ref[...]], axis=0)
        y = _conv_taps(ext, w_ref[...], tm)
        yq = [y[:, h * DN_HEAD_DIM:(h + 1) * DN_HEAD_DIM] for h in range(DN_HEADS)]
        yk = [y[:, DN_WIDTH + h * DN_HEAD_DIM: DN_WIDTH + (h + 1) * DN_HEAD_DIM] for h in range(DN_HEADS)]
        _, vjp = jax.vjp(_post_conv, yq, yk, y[:, 2 * DN_WIDTH:], b_ref[...], a_ref[...], alog_ref[...], dtb_ref[...])
        dg = mmx_tn(_chunk_tril(tm), dgc_ref[...] + dgc2_ref[...])
        dyq, dyk, dyv, db, da, dalog, ddtb = vjp((_split_heads(dq_ref, 0), _split_heads(dk_ref, 0), dv_ref[...],
                                                  dbeta_ref[...], dg))
        for h in range(DN_HEADS):
            dy_ref[:, h * DN_HEAD_DIM:(h + 1) * DN_HEAD_DIM] = dyq[h]
            dy_ref[:, DN_WIDTH + h * DN_HEAD_DIM: DN_WIDTH + (h + 1) * DN_HEAD_DIM] = dyk[h]
        dy_ref[:, 2 * DN_WIDTH:] = dyv
        db_ref[...] = db
        da_ref[...] = da
        _acc_out(dalog_ref, i == 0, dalog)
        _acc_out(ddtb_ref, i == 0, ddtb)

    row = lambda i: (i, 0)
    const = lambda i: (0, 0)
    prev, _ = _halo_specs(tm, cw, bps, t // tm)
    return pl.pallas_call(
        body, name=name, grid=(t // tm,),
        in_specs=[pl.BlockSpec((tm, cw), row), prev, pl.BlockSpec((tm, LANES), row), pl.BlockSpec((tm, LANES), row),
                  pl.BlockSpec((CONV_K, cw), const), pl.BlockSpec((1, LANES), const), pl.BlockSpec((1, LANES), const),
                  pl.BlockSpec((tm, DN_WIDTH), row), pl.BlockSpec((tm, DN_WIDTH), row), pl.BlockSpec((tm, DN_WIDTH), row),
                  pl.BlockSpec((tm, LANES), row), pl.BlockSpec((tm, LANES), row), pl.BlockSpec((tm, LANES), row)],
        out_specs=(pl.BlockSpec((tm, cw), row), pl.BlockSpec((tm, LANES), row), pl.BlockSpec((tm, LANES), row),
                   pl.BlockSpec((1, LANES), const), pl.BlockSpec((1, LANES), const)),
        out_shape=(jax.ShapeDtypeStruct((t, cw), F32), jax.ShapeDtypeStruct((t, LANES), F32), jax.ShapeDtypeStruct((t, LANES), F32),
                   jax.ShapeDtypeStruct((1, LANES), F32), jax.ShapeDtypeStruct((1, LANES), F32)),
        compiler_params=_cparams(("arbitrary",)),
    )(qkv, qkv, bpre, apre, conv_w, alog, dtb, dq, dk, dv, dbeta, dgc, dgc2)


def _conv_bwd(qkv, dy, conv_w, seq, *, name):
    t = qkv.shape[0]
    tm = _tm(t)
    bps = seq // tm
    cw = 3 * DN_WIDTH
    n_ext = tm + HALO

    def body(x_ref, halo_ref, dy_ref, dyn_ref, w_ref, dx_ref, dw_ref):
        i = pl.program_id(0)
        keep_prev = jnp.where(i % bps == 0, 0.0, 1.0)
        keep_next = jnp.where(i % bps == bps - 1, 0.0, 1.0)
        ext = jnp.concatenate([halo_ref[...] * keep_prev, x_ref[...]], axis=0)
        dy = dy_ref[...]
        dyext = jnp.concatenate([dy, dyn_ref[...] * keep_next], axis=0)
        w = w_ref[...]

        @pl.when(i == 0)
        def _():
            dw_ref[...] = jnp.zeros_like(dw_ref)

        dx = None
        for j in range(CONV_K):
            s = CONV_K - 1 - j
            fut = dyext if s == 0 else pltpu.roll(dyext, n_ext - s, 0)
            term = w[j:j + 1, :] * fut[0:tm, :]
            dx = term if dx is None else dx + term
            past = ext if s == 0 else pltpu.roll(ext, s, 0)
            dw_ref[j:j + 1, :] += jnp.sum(dy * past[HALO:HALO + tm, :], axis=0, keepdims=True)
        dx_ref[...] = dx

    row = lambda i: (i, 0)
    const = lambda i: (0, 0)
    prev, nxt = _halo_specs(tm, cw, bps, t // tm)
    return pl.pallas_call(
        body, name=name, grid=(t // tm,),
        in_specs=[pl.BlockSpec((tm, cw), row), prev, pl.BlockSpec((tm, cw), row), nxt, pl.BlockSpec((CONV_K, cw), const)],
        out_specs=(pl.BlockSpec((tm, cw), row), pl.BlockSpec((HALO, cw), const)),
        out_shape=(jax.ShapeDtypeStruct((t, cw), F32), jax.ShapeDtypeStruct((HALO, cw), F32)),
        compiler_params=_cparams(("arbitrary",)),
    )(qkv, qkv, dy, dy, conv_w)


def _inv_unit_lower(l_mats, eye):
    invs = [eye - l for l in l_mats]
    powers = list(l_mats)
    n = 2
    while n < eye.shape[0]:
        powers = [mmh(p, p) for p in powers]
        invs = [inv + mmh(inv, p) for inv, p in zip(invs, powers)]
        n *= 2
    return invs


@jax.custom_vjp
def _solve(l_mat, rhs, inv):
    return mmh(inv, rhs)


def _solve_fwd(l_mat, rhs, inv):
    sol = mmh(inv, rhs)
    return sol, (inv, sol)


def _solve_bwd(res, d_sol):
    inv, sol = res
    d_rhs = mmh_tn(inv, d_sol)
    return -mmh_nt(d_rhs, sol), d_rhs, jnp.zeros_like(inv)


_solve.defvjp(_solve_fwd, _solve_bwd)


def _prep_fn(q, k, v, gc, gr, b, inv):
    ids = range(len(q))
    c = q[0].shape[0]
    rr = lax.broadcasted_iota(jnp.int32, (c, c), 0)
    cc = lax.broadcasted_iota(jnp.int32, (c, c), 1)
    incl, strict = rr >= cc, rr > cc
    is_last = lax.broadcasted_iota(jnp.int32, (c, 1), 0) == c - 1
    qs = [q[i] * (DN_HEAD_DIM ** -0.5) for i in ids]
    decay = [jnp.where(incl, jnp.exp(jnp.where(incl, gc[i] - gr[i], 0.0)), 0.0) for i in ids]
    kb = [k[i] * b[i] for i in ids]
    vb = [v[i] * b[i] for i in ids]
    kk = [mm_nt(kb[i], k[i]) for i in ids]
    l_mat = [jnp.where(strict, kk[i] * decay[i], 0.0) for i in ids]
    eg = [jnp.exp(gc[i]) for i in ids]
    if inv is None:
        inv = _inv_unit_lower(l_mat, jnp.where(rr == cc, 1.0, 0.0).astype(F32))
    u_wy = [_solve(l_mat[i], vb[i], inv[i]) for i in ids]
    w_wy = [_solve(l_mat[i], kb[i] * eg[i], inv[i]) for i in ids]
    qk = [mm_nt(qs[i], k[i]) * decay[i] for i in ids]
    g_last = [jnp.sum(jnp.where(is_last, gc[i], 0.0), axis=0, keepdims=True) for i in ids]
    k_dec = [k[i] * jnp.exp(g_last[i] - gc[i]) for i in ids]
    egl = [jnp.broadcast_to(jnp.exp(g_last[i]), (1, LANES)) for i in ids]
    return [(w_wy[i], u_wy[i], qs[i] * eg[i], k_dec[i], qk[i], egl[i]) for i in ids], inv


def _seq_fn(w, u, qd, kd, qk, egl, s):
    ids = range(len(w))
    ws = [mm(w[i], s[i]) for i in ids]
    qs = [mm(qd[i], s[i]) for i in ids]
    v_new = [u[i] - ws[i] for i in ids]
    o = [qs[i] + mm(qk[i], v_new[i]) for i in ids]
    s_new = [s[i] * egl[i] + mm_tn(kd[i], v_new[i]) for i in ids]
    return o, s_new


def _lane_col(a, h):
    lane = lax.broadcasted_iota(jnp.int32, (1, LANES), 1)
    return jnp.sum(jnp.where(lane == h, a, 0.0), axis=1, keepdims=True)


def _col_lane(col, h):
    lane = lax.broadcasted_iota(jnp.int32, (1, LANES), 1)
    return jnp.where(lane == h, col, 0.0)


def _head_cols(h):
    return slice(h * DN_HEAD_DIM, (h + 1) * DN_HEAD_DIM)


def _chunk_rows(n):
    return pl.ds(pl.multiple_of(n * DN_CHUNK, DN_CHUNK), DN_CHUNK)


def _delta_prep(q, k, v, gc, grow, beta, *, name):
    t = q.shape[0]
    tm = _tm(t)
    cpb = tm // DN_CHUNK
    n_chunks = t // DN_CHUNK
    group = 2

    def body(q_ref, k_ref, v_ref, gc_ref, gr_ref, b_ref, w_ref, u_ref, qd_ref, kd_ref, qk_ref, egl_ref, inv_ref):
        def step(m, carry):
            probs = [(m * group + e, h) for e in range(group) for h in range(DN_HEADS)]
            gcb = [gc_ref[_chunk_rows(m * group + e), :] for e in range(group)]
            bb = [b_ref[_chunk_rows(m * group + e), :] for e in range(group)]
            grb = [gr_ref[m * group + e] for e in range(group)]
            for e in range(group):
                egl_ref[m * group + e] = jnp.zeros((HALO, LANES), F32)
            outs, invs = _prep_fn(
                [q_ref[_chunk_rows(n), _head_cols(h)] for n, h in probs], [k_ref[_chunk_rows(n), _head_cols(h)] for n, h in probs],
                [v_ref[_chunk_rows(n), _head_cols(h)] for n, h in probs],
                [_lane_col(gcb[e], h) for e in range(group) for h in range(DN_HEADS)],
                [grb[e][h:h + 1, :] for e in range(group) for h in range(DN_HEADS)],
                [_lane_col(bb[e], h) for e in range(group) for h in range(DN_HEADS)], None)
            for (n, h), (w, u, qd, kd, qk, egl), inv in zip(probs, outs, invs):
                rows, cols = _chunk_rows(n), _head_cols(h)
                w_ref[rows, cols] = w.astype(BF16)
                u_ref[rows, cols] = u
                qd_ref[rows, cols] = qd.astype(BF16)
                kd_ref[rows, cols] = kd.astype(BF16)
                qk_ref[n, h] = qk
                inv_ref[n, h] = inv
                egl_ref[n, h:h + 1, :] = egl
            return carry

        lax.fori_loop(0, cpb // group, step, 0)

    row = lambda i: (i, 0)
    tok = pl.BlockSpec((tm, DN_WIDTH), row)
    lanes = pl.BlockSpec((tm, LANES), row)
    sq = pl.BlockSpec((cpb, DN_HEADS, DN_CHUNK, DN_CHUNK), lambda i: (i, 0, 0, 0))
    return pl.pallas_call(
        body, name=name, grid=(t // tm,),
        in_specs=[tok, tok, tok, lanes, pl.BlockSpec((cpb, HALO, DN_CHUNK), lambda i: (i, 0, 0)), lanes],
        out_specs=(tok, tok, tok, tok, sq, pl.BlockSpec((cpb, HALO, LANES), lambda i: (i, 0, 0)), sq),
        out_shape=(jax.ShapeDtypeStruct((t, DN_WIDTH), BF16), jax.ShapeDtypeStruct((t, DN_WIDTH), F32),
                   jax.ShapeDtypeStruct((t, DN_WIDTH), BF16), jax.ShapeDtypeStruct((t, DN_WIDTH), BF16),
                   jax.ShapeDtypeStruct((n_chunks, DN_HEADS, DN_CHUNK, DN_CHUNK), F32),
                   jax.ShapeDtypeStruct((n_chunks, HALO, LANES), F32),
                   jax.ShapeDtypeStruct((n_chunks, DN_HEADS, DN_CHUNK, DN_CHUNK), F32)),
        compiler_params=_cparams(("parallel",)),
    )(q, k, v, gc, grow, beta)


def _delta_par_bwd(q, k, v, gc, grow, beta, inv, dw, du, dqd, dkd, dqk, degl, *, name):
    t = q.shape[0]
    tm = _tm(t)
    cpb = tm // DN_CHUNK
    n_chunks = t // DN_CHUNK
    group = 2

    def body(q_ref, k_ref, v_ref, gc_ref, gr_ref, b_ref, inv_ref, dw_ref, du_ref, dqd_ref, dkd_ref, dqk_ref, degl_ref,
             dq_ref, dk_ref, dv_ref, dgc_ref, dgr_ref, db_ref):
        def step(m, carry):
            chunks = [m * group + e for e in range(group)]
            probs = [(e, h) for e in range(group) for h in range(DN_HEADS)]
            rows = [_chunk_rows(n) for n in chunks]
            gcb, bb = [gc_ref[r, :] for r in rows], [b_ref[r, :] for r in rows]
            grb, deglb = [gr_ref[n] for n in chunks], [degl_ref[n] for n in chunks]
            for n in chunks:
                dgr_ref[n] = jnp.zeros((HALO, DN_CHUNK), F32)
            invs = [inv_ref[chunks[e], h] for e, h in probs]
            _, vjp = jax.vjp(lambda *a: _prep_fn(*a, invs)[0],
                             [q_ref[rows[e], _head_cols(h)] for e, h in probs], [k_ref[rows[e], _head_cols(h)] for e, h in probs],
                             [v_ref[rows[e], _head_cols(h)] for e, h in probs], [_lane_col(gcb[e], h) for e, h in probs],
                             [grb[e][h:h + 1, :] for e, h in probs], [_lane_col(bb[e], h) for e, h in probs])
            dq, dk, dv, dgc, dgr, db = vjp([(dw_ref[rows[e], _head_cols(h)], du_ref[rows[e], _head_cols(h)],
                                             dqd_ref[rows[e], _head_cols(h)], dkd_ref[rows[e], _head_cols(h)],
                                             dqk_ref[chunks[e], h], deglb[e][h:h + 1, :]) for e, h in probs])
            dgc_acc = [jnp.zeros((DN_CHUNK, LANES), F32) for _ in chunks]
            db_acc = [jnp.zeros((DN_CHUNK, LANES), F32) for _ in chunks]
            for i, (e, h) in enumerate(probs):
                cols = _head_cols(h)
                dq_ref[rows[e], cols] = dq[i]
                dk_ref[rows[e], cols] = dk[i]
                dv_ref[rows[e], cols] = dv[i]
                dgr_ref[chunks[e], h:h + 1, :] = dgr[i]
                dgc_acc[e] = dgc_acc[e] + _col_lane(dgc[i], h)
                db_acc[e] = db_acc[e] + _col_lane(db[i], h)
            for e in range(group):
                dgc_ref[rows[e], :] = dgc_acc[e]
                db_ref[rows[e], :] = db_acc[e]
            return carry

        lax.fori_loop(0, cpb // group, step, 0)

    row = lambda i: (i, 0)
    tok = pl.BlockSpec((tm, DN_WIDTH), row)
    lanes = pl.BlockSpec((tm, LANES), row)
    sq = pl.BlockSpec((cpb, DN_HEADS, DN_CHUNK, DN_CHUNK), lambda i: (i, 0, 0, 0))
    grs = pl.BlockSpec((cpb, HALO, DN_CHUNK), lambda i: (i, 0, 0))
    return pl.pallas_call(
        body, name=name, grid=(t // tm,),
        in_specs=[tok, tok, tok, lanes, grs, lanes, sq, tok, tok, tok, tok, sq, pl.BlockSpec((cpb, HALO, LANES), lambda i: (i, 0, 0))],
        out_specs=(tok, tok, tok, lanes, grs, lanes),
        out_shape=(jax.ShapeDtypeStruct((t, DN_WIDTH), F32),) * 3
        + (jax.ShapeDtypeStruct((t, LANES), F32), jax.ShapeDtypeStruct((n_chunks, HALO, DN_CHUNK), F32),
           jax.ShapeDtypeStruct((t, LANES), F32)),
        compiler_params=_cparams(("parallel",)),
    )(q, k, v, gc, grow, beta, inv, dw, du, dqd, dkd, dqk, degl)


def _seq_specs(n_seq, seq, reverse):
    tm = _tm(seq)
    nb = seq // tm
    cpb = tm // DN_CHUNK
    blk = (lambda b, j: b * nb + nb - 1 - j) if reverse else (lambda b, j: b * nb + j)
    tok = pl.BlockSpec((tm, DN_WIDTH), lambda b, j: (blk(b, j), 0))
    sq = pl.BlockSpec((cpb, DN_HEADS, DN_CHUNK, DN_CHUNK), lambda b, j: (blk(b, j), 0, 0, 0))
    rows8 = pl.BlockSpec((cpb, HALO, LANES), lambda b, j: (blk(b, j), 0, 0))
    state = pl.BlockSpec((cpb, DN_HEADS, DN_HEAD_DIM, DN_HEAD_DIM), lambda b, j: (blk(b, j), 0, 0, 0))
    return nb, cpb, tok, sq, rows8, state


def _delta_seq_fwd(w, u, qd, kd, qk, egl, n_seq, seq, *, name):
    nb, cpb, tok, sq, rows8, state = _seq_specs(n_seq, seq, False)
    t = n_seq * seq

    def body(w_ref, u_ref, qd_ref, kd_ref, qk_ref, egl_ref, o_ref, st_ref, s_s):
        @pl.when(pl.program_id(1) == 0)
        def _():
            s_s[...] = jnp.zeros_like(s_s)

        def step(n, carry):
            rows = _chunk_rows(n)
            heads = range(DN_HEADS)
            eglb = egl_ref[n]
            s = [s_s[h] for h in heads]
            for h in heads:
                st_ref[n, h] = s[h]
            o, s_new = _seq_fn([w_ref[rows, _head_cols(h)] for h in heads], [u_ref[rows, _head_cols(h)] for h in heads],
                               [qd_ref[rows, _head_cols(h)] for h in heads], [kd_ref[rows, _head_cols(h)] for h in heads],
                               [qk_ref[n, h] for h in heads], [eglb[h:h + 1, :] for h in heads], s)
            for h in heads:
                o_ref[rows, _head_cols(h)] = o[h]
                s_s[h] = s_new[h]
            return carry

        lax.fori_loop(0, cpb, step, 0)

    return pl.pallas_call(
        body, name=name, grid=(n_seq, nb),
        in_specs=[tok, tok, tok, tok, sq, rows8],
        out_specs=(tok, state),
        out_shape=(jax.ShapeDtypeStruct((t, DN_WIDTH), F32),
                   jax.ShapeDtypeStruct((t // DN_CHUNK, DN_HEADS, DN_HEAD_DIM, DN_HEAD_DIM), F32)),
        scratch_shapes=[pltpu.VMEM((DN_HEADS, DN_HEAD_DIM, DN_HEAD_DIM), F32)],
        compiler_params=_cparams(("parallel", "arbitrary")),
    )(w, u, qd, kd, qk, egl)


def _delta_seq_bwd(w, u, qd, kd, qk, egl, states, do, n_seq, seq, *, name):
    nb, cpb, tok, sq, rows8, state = _seq_specs(n_seq, seq, True)
    t = n_seq * seq

    def body(w_ref, u_ref, qd_ref, kd_ref, qk_ref, egl_ref, st_ref, do_ref, dw_ref, du_ref, dqd_ref, dkd_ref, dqk_ref,
             degl_ref, ds_s):
        @pl.when(pl.program_id(1) == 0)
        def _():
            ds_s[...] = jnp.zeros_like(ds_s)

        def step(m, carry):
            n = cpb - 1 - m
            rows = _chunk_rows(n)
            eglb = egl_ref[n]
            degl_ref[n] = jnp.zeros((HALO, LANES), F32)
            heads = range(DN_HEADS)
            _, vjp = jax.vjp(_seq_fn, [w_ref[rows, _head_cols(h)].astype(F32) for h in heads],
                             [u_ref[rows, _head_cols(h)] for h in heads],
                             [qd_ref[rows, _head_cols(h)].astype(F32) for h in heads],
                             [kd_ref[rows, _head_cols(h)].astype(F32) for h in heads],
                             [qk_ref[n, h] for h in heads], [eglb[h:h + 1, :] for h in heads], [st_ref[n, h] for h in heads])
            dw, du, dqd, dkd, dqk, degl, ds_in = vjp(([do_ref[rows, _head_cols(h)] for h in heads], [ds_s[h] for h in heads]))
            for h in heads:
                cols = _head_cols(h)
                dw_ref[rows, cols] = dw[h]
                du_ref[rows, cols] = du[h]
                dqd_ref[rows, cols] = dqd[h]
                dkd_ref[rows, cols] = dkd[h]
                dqk_ref[n, h] = dqk[h]
                degl_ref[n, h:h + 1, :] = degl[h]
                ds_s[h] = ds_in[h]
            return carry

        lax.fori_loop(0, cpb, step, 0)

    return pl.pallas_call(
        body, name=name, grid=(n_seq, nb),
        in_specs=[tok, tok, tok, tok, sq, rows8, state, tok],
        out_specs=(tok, tok, tok, tok, sq, rows8),
        out_shape=(jax.ShapeDtypeStruct((t, DN_WIDTH), F32),) * 4
        + (jax.ShapeDtypeStruct((t // DN_CHUNK, DN_HEADS, DN_CHUNK, DN_CHUNK), F32),
           jax.ShapeDtypeStruct((t // DN_CHUNK, HALO, LANES), F32)),
        scratch_shapes=[pltpu.VMEM((DN_HEADS, DN_HEAD_DIM, DN_HEAD_DIM), F32)],
        compiler_params=_cparams(("parallel", "arbitrary")),
    )(w, u, qd, kd, qk, egl, states, do)


def _dn_gate(o, z, dnw):
    return o * lax.rsqrt(jnp.mean(o * o, axis=-1, keepdims=True) + EPS) * dnw * _silu(z)


def _mix_out_fwd(x, sg, o, z, wo_sg, wo_dn, dnw, *, name):
    t = x.shape[0]
    tm = _tm(t)

    def body(x_ref, sg_ref, o_ref, z_ref, wsg_ref, wdn_ref, dnw_ref, y_ref, dn_s):
        for h, (oh, zh) in enumerate(zip(_split_heads(o_ref, 0), _split_heads(z_ref, 0))):
            dn_s[:, h * DN_HEAD_DIM:(h + 1) * DN_HEAD_DIM] = _dn_gate(oh, zh, dnw_ref[...]).astype(BF16)
        y_ref[...] = (x_ref[...] + jnp.dot(sg_ref[...].astype(BF16), wsg_ref[...], preferred_element_type=F32)
                      + jnp.dot(dn_s[...], wdn_ref[...], preferred_element_type=F32))

    row = lambda i: (i, 0)
    const = lambda i: (0, 0)
    half = pl.BlockSpec((tm, DN_WIDTH), row)
    return pl.pallas_call(
        body, name=name, grid=(t // tm,),
        in_specs=[pl.BlockSpec((tm, D_MODEL), row), half, half, half, pl.BlockSpec((SG_WIDTH, D_MODEL), const),
                  pl.BlockSpec((DN_WIDTH, D_MODEL), const), pl.BlockSpec((1, DN_HEAD_DIM), const)],
        out_specs=pl.BlockSpec((tm, D_MODEL), row),
        out_shape=jax.ShapeDtypeStruct((t, D_MODEL), F32),
        scratch_shapes=[pltpu.VMEM((tm, DN_WIDTH), BF16)],
        compiler_params=_cparams(("parallel",)),
    )(x, sg, o, z, wo_sg, wo_dn, dnw)


def _mix_out_bwd(dy, sg, o, z, wo_sg, wo_dn, dnw, *, name):
    t = dy.shape[0]
    tm = _tm(t)

    def body(dy_ref, sg_ref, o_ref, z_ref, wsg_ref, wdn_ref, dnw_ref, dsg_ref, do_ref, dz_ref, dwsg_ref, dwdn_ref, ddnw_ref, dn_s):
        i = pl.program_id(0)
        dyb = dy_ref[...].astype(BF16)
        nt = (((1,), (1,)), ((), ()))
        tn = (((0,), (0,)), ((), ()))
        dsg_ref[...] = lax.dot_general(dyb, wsg_ref[...], nt, preferred_element_type=F32)
        ddn = lax.dot_general(dyb, wdn_ref[...], nt, preferred_element_type=F32)
        ddnw = None
        for h, (oh, zh) in enumerate(zip(_split_heads(o_ref, 0), _split_heads(z_ref, 0))):
            cols = slice(h * DN_HEAD_DIM, (h + 1) * DN_HEAD_DIM)
            out, vjp = jax.vjp(_dn_gate, oh, zh, dnw_ref[...])
            dn_s[:, cols] = out.astype(BF16)
            doh, dzh, dw = vjp(ddn[:, cols])
            do_ref[:, cols] = doh
            dz_ref[:, cols] = dzh
            ddnw = dw if ddnw is None else ddnw + dw
        _acc_out(ddnw_ref, i == 0, ddnw)
        _acc_out(dwsg_ref, i == 0, lax.dot_general(sg_ref[...].astype(BF16), dyb, tn, preferred_element_type=F32))
        _acc_out(dwdn_ref, i == 0, lax.dot_general(dn_s[...], dyb, tn, preferred_element_type=F32))

    row = lambda i: (i, 0)
    const = lambda i: (0, 0)
    half = pl.BlockSpec((tm, DN_WIDTH), row)
    wspec = pl.BlockSpec((DN_WIDTH, D_MODEL), const)
    return pl.pallas_call(
        body, name=name, grid=(t // tm,),
        in_specs=[pl.BlockSpec((tm, D_MODEL), row), half, half, half, wspec, wspec, pl.BlockSpec((1, DN_HEAD_DIM), const)],
        out_specs=(half, half, half, wspec, wspec, pl.BlockSpec((1, DN_HEAD_DIM), const)),
        out_shape=(jax.ShapeDtypeStruct((t, DN_WIDTH), F32),) * 3 + (jax.ShapeDtypeStruct((DN_WIDTH, D_MODEL), F32),) * 2
        + (jax.ShapeDtypeStruct((1, DN_HEAD_DIM), F32),),
        scratch_shapes=[pltpu.VMEM((tm, DN_WIDTH), BF16)],
        compiler_params=_cparams(("arbitrary",)),
    )(dy, sg, o, z, wo_sg, wo_dn, dnw)


_MESH = pl.DeviceIdType.MESH
_HBM = pl.BlockSpec(memory_space=pl.ANY)


def _mesh_pos():
    x, y, c = lax.axis_index("x"), lax.axis_index("y"), lax.axis_index("c")
    return x, y, c, [(1 - x, y), (x, 1 - y), (1 - x, 1 - y)]


def _gather2(arrs, *, name):
    n = len(arrs)
    slots = N_DEV - 1

    def body(*refs):
        in_refs, out_refs = refs[:n], refs[n:2 * n]
        send_sems, recv_sems, local_sems = refs[2 * n:]
        x, y, c, chips = _mesh_pos()
        me, sibling = (x, y, c), (x, y, 1 - c)

        def copy(k, slot, block, to, src=None):
            dst = out_refs[k].at[4 * block[0] + 2 * block[1] + block[2]]
            return pltpu.make_async_remote_copy(src_ref=dst if src is None else src, dst_ref=dst,
                                                send_sem=send_sems.at[k * slots + slot], recv_sem=recv_sems.at[k * slots + slot],
                                                device_id=to, device_id_type=_MESH)

        local = [pltpu.make_async_copy(in_refs[k], out_refs[k].at[4 * x + 2 * y + c], local_sems.at[k]) for k in range(n)]
        sent = []
        for k in range(n):
            sent.append(copy(k, 0, me, sibling, src=in_refs[k]))
            sent += [copy(k, 1 + j, me, (*chip, c), src=in_refs[k]) for j, chip in enumerate(chips)]
        for cp in local + sent:
            cp.start()
        for j, chip in enumerate(chips):
            for k in range(n):
                copy(k, 1 + j, (*chip, c), me).wait_recv()
                passed = copy(k, 4 + j, (*chip, c), sibling)
                passed.start()
                sent.append(passed)
        for k in range(n):
            copy(k, 0, sibling, me).wait_recv()
            for j, chip in enumerate(chips):
                copy(k, 4 + j, (*chip, 1 - c), me).wait_recv()
        for cp in sent:
            cp.wait_send()
        for cp in local:
            cp.wait()

    return pl.pallas_call(
        body, name=name, in_specs=[_HBM] * n, out_specs=(_HBM,) * n,
        out_shape=tuple(jax.ShapeDtypeStruct((N_DEV,) + a.shape, a.dtype) for a in arrs),
        scratch_shapes=[pltpu.SemaphoreType.DMA((n * slots,)), pltpu.SemaphoreType.DMA((n * slots,)),
                        pltpu.SemaphoreType.DMA((n,))],
    )(*arrs)


def _pair_swap(arrs, *, name):
    n = len(arrs)

    def body(*refs):
        in_refs, out_refs, send_sems, recv_sems = refs[:n], refs[n:2 * n], refs[2 * n], refs[2 * n + 1]
        x, y, c, _ = _mesh_pos()
        copies = [pltpu.make_async_remote_copy(src_ref=in_refs[k].at[1 - c], dst_ref=out_refs[k], send_sem=send_sems.at[k],
                                               recv_sem=recv_sems.at[k], device_id=(x, y, 1 - c), device_id_type=_MESH)
                  for k in range(n)]
        for cp in copies:
            cp.start()
        for cp in copies:
            cp.wait()

    return pl.pallas_call(
        body, name=name, in_specs=[_HBM] * n, out_specs=(_HBM,) * n,
        out_shape=tuple(jax.ShapeDtypeStruct(a.shape[1:], a.dtype) for a in arrs),
        scratch_shapes=[pltpu.SemaphoreType.DMA((n,)), pltpu.SemaphoreType.DMA((n,))],
    )(*arrs)


def _chip_exchange(arrs, *, name):
    n = len(arrs)
    slots = 3

    def body(*refs):
        in_refs, out_refs = refs[:n], refs[n:2 * n]
        send_sems, recv_sems, local_sems = refs[2 * n:]
        x, y, c, chips = _mesh_pos()
        mine = 2 * x + y
        copies = [pltpu.make_async_copy(in_refs[k].at[mine], out_refs[k].at[mine], local_sems.at[k]) for k in range(n)]
        for j, chip in enumerate(chips):
            for k in range(n):
                copies.append(pltpu.make_async_remote_copy(
                    src_ref=in_refs[k].at[2 * chip[0] + chip[1]], dst_ref=out_refs[k].at[mine],
                    send_sem=send_sems.at[k * slots + j], recv_sem=recv_sems.at[k * slots + j],
                    device_id=(*chip, c), device_id_type=_MESH))
        for cp in copies:
            cp.start()
        for cp in copies:
            cp.wait()

    return pl.pallas_call(
        body, name=name, in_specs=[_HBM] * n, out_specs=(_HBM,) * n,
        out_shape=tuple(jax.ShapeDtypeStruct(a.shape, a.dtype) for a in arrs),
        scratch_shapes=[pltpu.SemaphoreType.DMA((n * slots,)), pltpu.SemaphoreType.DMA((n * slots,)),
                        pltpu.SemaphoreType.DMA((n,))],
    )(*arrs)


_SEM = pl.BlockSpec(memory_space=pltpu.SEMAPHORE)
_EFFECT = pltpu.SideEffectType.DATAFLOW_SIDE_EFFECTING


def _direct_copies(src_refs, land_refs, send_sems, recv_sems, gather):
    x, y, c, _ = _mesh_pos()
    me = 4 * x + 2 * y + c
    n_peer = N_DEV - 1
    copies = []
    for r in range(1, N_DEV):
        px = 1 - x if r & 4 else x
        py = 1 - y if r & 2 else y
        pc = 1 - c if r & 1 else c
        for k, (src, land) in enumerate(zip(src_refs, land_refs)):
            copies.append(pltpu.make_async_remote_copy(
                src_ref=src if gather else src.at[4 * px + 2 * py + pc], dst_ref=land.at[me],
                send_sem=send_sems.at[k * n_peer + r - 1], recv_sem=recv_sems.at[k * n_peer + r - 1],
                device_id=(px, py, pc), device_id_type=_MESH))
    return copies


def _send_start(arrs, gather, after=None, *, name):
    n = len(arrs)
    lands = [lax.empty(((N_DEV,) + a.shape) if gather else a.shape, a.dtype) for a in arrs]
    n_in = 2 * n + (0 if after is None else 1)

    def body(*refs):
        src_refs, land_refs, send_sems, recv_sems, token = refs[:n], refs[n:2 * n], refs[n_in], refs[n_in + 1], refs[-1]
        for cp in _direct_copies(src_refs, land_refs, send_sems, recv_sems, gather):
            cp.start()
        token[...] = jnp.zeros_like(token)

    n_sem = n * (N_DEV - 1)
    bufs = list(arrs) + lands
    out = pl.pallas_call(
        body, name=name,
        out_shape=(pltpu.SemaphoreType.DMA((n_sem,)), pltpu.SemaphoreType.DMA((n_sem,)))
        + tuple(pltpu.HBM(b.shape, b.dtype) for b in bufs) + (jax.ShapeDtypeStruct((HALO, LANES), F32),),
        in_specs=[_HBM] * n_in, out_specs=(_SEM, _SEM) + (_HBM,) * (2 * n) + (pl.BlockSpec(memory_space=pltpu.VMEM),),
        input_output_aliases={i: 2 + i for i in range(2 * n)},
        compiler_params=pltpu.CompilerParams(has_side_effects=_EFFECT),
    )(*[pltpu.with_memory_space_constraint(b, pltpu.HBM) for b in bufs], *([] if after is None else [after]))
    return (out[0], out[1], list(out[2:2 + n]), list(out[2 + n:2 + 2 * n])), out[-1]


def _send_wait(started, gather, after, *, name):
    send_sems, recv_sems, srcs, lands = started
    n = len(srcs)

    def body(*refs):
        src_refs, land_refs, send_ref, recv_ref = refs[:n], refs[n:2 * n], refs[2 * n], refs[2 * n + 1]
        for cp in _direct_copies(src_refs, land_refs, send_ref, recv_ref, gather):
            cp.wait_send()
            cp.wait_recv()

    bufs = srcs + lands
    out = pl.pallas_call(
        body, name=name, out_shape=tuple(pltpu.HBM(b.shape, b.dtype) for b in bufs),
        in_specs=[_HBM] * (2 * n) + [_SEM, _SEM, _HBM], out_specs=(_HBM,) * (2 * n),
        input_output_aliases={i: i for i in range(2 * n)},
        compiler_params=pltpu.CompilerParams(has_side_effects=_EFFECT),
    )(*bufs, send_sems, recv_sems, after)
    return list(out[n:])


def _pair_add(p, r, core, *, name):
    _, n_chip, rows, cols = p.shape
    rb = _row_block(rows)

    def body(core_ref, p_ref, r_ref, o_ref):
        o_ref[...] = (p_ref[...].astype(F32) + r_ref[...].astype(F32)).astype(BF16)

    return pl.pallas_call(
        body, name=name,
        grid_spec=pltpu.PrefetchScalarGridSpec(
            num_scalar_prefetch=1, grid=(n_chip, rows // rb),
            in_specs=[pl.BlockSpec((None, None, rb, cols), lambda s, i, core_ref: (core_ref[0], s, i, 0)),
                      pl.BlockSpec((None, rb, cols), lambda s, i, core_ref: (s, i, 0))],
            out_specs=pl.BlockSpec((None, rb, cols), lambda s, i, core_ref: (s, i, 0))),
        out_shape=jax.ShapeDtypeStruct((n_chip, rows, cols), BF16),
        compiler_params=_cparams(("parallel", "parallel")),
    )(core, p, r)


def _row_block(rows, limit=256):
    best = rows
    for cand in range(8, limit + 1, 8):
        if rows % cand == 0:
            best = cand
    return best if rows > limit else rows


def _adam(gp, w, m, v, *, name):
    p, rows, cols = gp.shape
    rb = _row_block(rows)

    def body(gp_ref, w_ref, m_ref, v_ref, g_ref, d_ref, m2_ref, v2_ref):
        g = gp_ref[0].astype(F32)
        for s in range(1, p):
            g = g + gp_ref[s].astype(F32)
        m2 = ADAM_B1 * m_ref[...] + (1.0 - ADAM_B1) * g
        v2 = ADAM_B2 * v_ref[...] + (1.0 - ADAM_B2) * (g * g)
        m_hat = m2 / (1.0 - ADAM_B1 ** ADAM_STEP)
        v_hat = v2 / (1.0 - ADAM_B2 ** ADAM_STEP)
        g_ref[...] = g
        d_ref[...] = -ADAM_LR * (m_hat / (jnp.sqrt(v_hat) + ADAM_EPS) + ADAM_WD * w_ref[...])
        m2_ref[...] = m2
        v2_ref[...] = v2

    blk = pl.BlockSpec((rb, cols), lambda i: (i, 0))
    return pl.pallas_call(
        body, name=name, grid=(rows // rb,),
        in_specs=[pl.BlockSpec((p, rb, cols), lambda i: (0, i, 0)), blk, blk, blk],
        out_specs=(blk,) * 4, out_shape=(jax.ShapeDtypeStruct((rows, cols), F32),) * 4,
        compiler_params=_cparams(("parallel",)),
    )(gp, w, m, v)


def _cols_full(g):
    return jnp.transpose(g, (1, 0, 2)).reshape(g.shape[1], N_DEV * g.shape[2])


def _pad_lanes(a, width=LANES):
    return jnp.pad(a, ((0, 0), (0, width - a.shape[1])))


def _chunk_rows_of(a):
    by_chunk = jnp.transpose(a[:, :DN_HEADS].reshape(-1, DN_CHUNK, DN_HEADS), (0, 2, 1))
    return jnp.pad(by_chunk, ((0, 0), (0, HALO - DN_HEADS), (0, 0)))


_SMALL = (("ffn1_norm", D_MODEL), ("mix_norm", D_MODEL), ("ffn2_norm", D_MODEL), ("final_norm", D_MODEL), ("a_log", DN_HEADS),
          ("dt_bias", DN_HEADS), ("dn_norm", DN_HEAD_DIM), ("sg_ln_g", SG_WIDTH), ("sg_ln_b", SG_WIDTH),
          ("sg_w", SG_GROUPS * SG_CHUNK * SG_CHUNK), ("sg_b", SG_GROUPS * SG_CHUNK), ("conv_w", CONV_K * 3 * DN_WIDTH))
_SMALL_ROWS = 1128
_SMALL_SHAPES = {"ffn1_norm": (1, D_MODEL), "mix_norm": (1, D_MODEL), "ffn2_norm": (1, D_MODEL), "final_norm": (D_MODEL,),
                 "a_log": (1, DN_HEADS), "dt_bias": (1, DN_HEADS), "dn_norm": (1, DN_HEAD_DIM), "sg_ln_g": (1, SG_WIDTH),
                 "sg_ln_b": (1, SG_WIDTH), "sg_w": (1, SG_GROUPS, SG_CHUNK, SG_CHUNK), "sg_b": (1, SG_GROUPS, SG_CHUNK)}


def _pack_small(d):
    flat = jnp.concatenate([d[name].reshape(-1) for name, _ in _SMALL])
    return jnp.pad(flat, (0, _SMALL_ROWS * LANES - flat.shape[0])).reshape(_SMALL_ROWS, LANES)


def _unpack_small(a):
    flat, out, at = a.reshape(-1), {}, 0
    for name, size in _SMALL:
        out[name] = flat[at:at + size]
        at += size
    return out


def kernel(x, ffn1_norm, ffn1_w_gate, ffn1_w_up, ffn1_w_down, mix_norm, w_in, conv_w, a_log, dt_bias, dn_norm, sg_ln_g, sg_ln_b, sg_w, sg_b, w_out, ffn2_norm, ffn2_w_gate, ffn2_w_up, ffn2_w_down, final_norm, loss_target, m_ffn1_norm, m_ffn1_w_gate, m_ffn1_w_up, m_ffn1_w_down, m_mix_norm, m_w_in, m_conv_w, m_a_log, m_dt_bias, m_dn_norm, m_sg_ln_g, m_sg_ln_b, m_sg_w, m_sg_b, m_w_out, m_ffn2_norm, m_ffn2_w_gate, m_ffn2_w_up, m_ffn2_w_down, m_final_norm, v_ffn1_norm, v_ffn1_w_gate, v_ffn1_w_up, v_ffn1_w_down, v_mix_norm, v_w_in, v_conv_w, v_a_log, v_dt_bias, v_dn_norm, v_sg_ln_g, v_sg_ln_b, v_sg_w, v_sg_b, v_w_out, v_ffn2_norm, v_ffn2_w_gate, v_ffn2_w_up, v_ffn2_w_down, v_final_norm):
    weights = dict(ffn1_norm=ffn1_norm, ffn1_w_gate=ffn1_w_gate, ffn1_w_up=ffn1_w_up, ffn1_w_down=ffn1_w_down, mix_norm=mix_norm, w_in=w_in, conv_w=conv_w, a_log=a_log, dt_bias=dt_bias, dn_norm=dn_norm, sg_ln_g=sg_ln_g, sg_ln_b=sg_ln_b, sg_w=sg_w, sg_b=sg_b, w_out=w_out, ffn2_norm=ffn2_norm, ffn2_w_gate=ffn2_w_gate, ffn2_w_up=ffn2_w_up, ffn2_w_down=ffn2_w_down, final_norm=final_norm)
    mom_m = dict(ffn1_norm=m_ffn1_norm, ffn1_w_gate=m_ffn1_w_gate, ffn1_w_up=m_ffn1_w_up, ffn1_w_down=m_ffn1_w_down, mix_norm=m_mix_norm, w_in=m_w_in, conv_w=m_conv_w, a_log=m_a_log, dt_bias=m_dt_bias, dn_norm=m_dn_norm, sg_ln_g=m_sg_ln_g, sg_ln_b=m_sg_ln_b, sg_w=m_sg_w, sg_b=m_sg_b, w_out=m_w_out, ffn2_norm=m_ffn2_norm, ffn2_w_gate=m_ffn2_w_gate, ffn2_w_up=m_ffn2_w_up, ffn2_w_down=m_ffn2_w_down, final_norm=m_final_norm)
    mom_v = dict(ffn1_norm=v_ffn1_norm, ffn1_w_gate=v_ffn1_w_gate, ffn1_w_up=v_ffn1_w_up, ffn1_w_down=v_ffn1_w_down, mix_norm=v_mix_norm, w_in=v_w_in, conv_w=v_conv_w, a_log=v_a_log, dt_bias=v_dt_bias, dn_norm=v_dn_norm, sg_ln_g=v_sg_ln_g, sg_ln_b=v_sg_ln_b, sg_w=v_sg_w, sg_b=v_sg_b, w_out=v_w_out, ffn2_norm=v_ffn2_norm, ffn2_w_gate=v_ffn2_w_gate, ffn2_w_up=v_ffn2_w_up, ffn2_w_down=v_ffn2_w_down, final_norm=v_final_norm)
    order = list(weights)
    big = ("ffn1_w_gate", "ffn1_w_up", "ffn1_w_down", "w_in", "w_out", "ffn2_w_gate", "ffn2_w_up", "ffn2_w_down")
    col_sharded = ("ffn1_w_gate", "ffn1_w_up", "w_in", "ffn2_w_gate", "ffn2_w_up")

    n_seq, seq, _ = x.shape
    t = n_seq * seq
    me = 4 * lax.axis_index("x") + 2 * lax.axis_index("y") + lax.axis_index("c")
    x0 = x.reshape(t, D_MODEL)
    tgt = loss_target.reshape(t, D_MODEL)

    def fill_own(land, own_block):
        return lax.dynamic_update_index_in_dim(land, own_block, me, 0)

    def rows_view(n, a):
        return jnp.transpose(a) if n in col_sharded else a

    def as_full(n, g):
        return g.reshape(-1, g.shape[-1])

    shards = {n: rows_view(n, weights[n][0]).astype(BF16) for n in big}
    ffn1_names, mix_names, ffn2_names = big[:3], big[3:5], big[5:]
    full = {n: as_full(n, g) for n, g in zip(ffn1_names, _gather2([shards[n] for n in ffn1_names], name="gather_ffn1"))}
    mix_srcs = [shards[n] for n in mix_names] + [conv_w[0]]
    mix_started, mix_token = _send_start(mix_srcs, True, full[ffn1_names[2]], name="gather_mix_start")
    ffn2_started, ffn2_token = _send_start([shards[n] for n in ffn2_names], True, mix_token, name="gather_ffn2_start")
    ffn1_norm_fwd = ffn1_norm + ffn2_token[:1, :1]
    alog, dtb = _pad_lanes(a_log), _pad_lanes(dt_bias)
    sgbt = _pad_lanes(sg_b[0].T)
    fnw = final_norm.reshape(1, D_MODEL)

    x1, h1, g1, u1 = _ffn_fwd(x0, ffn1_norm_fwd, full["ffn1_w_gate"], full["ffn1_w_up"], full["ffn1_w_down"], name="ffn1_fwd")
    mix_lands = [fill_own(land, src) for land, src in zip(_send_wait(mix_started, True, x1, name="gather_mix_wait"), mix_srcs)]
    full.update({n: as_full(n, g) for n, g in zip(mix_names, mix_lands)})
    conv_full = _cols_full(mix_lands[-1])
    w_in_t = full["w_in"]
    offs = (0, SG_WIDTH, 2 * SG_WIDTH, 2 * SG_WIDTH + 3 * DN_WIDTH, 2 * SG_WIDTH + 4 * DN_WIDTH)
    n_proj = offs[-1]

    def pad_rows(a):
        return jnp.pad(a, ((0, LANES - a.shape[0]), (0, 0)))

    ws = [w_in_t[offs[0]:offs[1]], w_in_t[offs[1]:offs[2]], w_in_t[offs[2]:offs[3]], w_in_t[offs[3]:offs[4]],
          pad_rows(w_in_t[n_proj:n_proj + DN_HEADS]), pad_rows(w_in_t[n_proj + DN_HEADS:n_proj + 2 * DN_HEADS])]
    wo_sg, wo_dn = full["w_out"][:SG_WIDTH], full["w_out"][SG_WIDTH:]
    u, v, qkv, z, bpre, apre = _mix_in_fwd(x1, mix_norm, ws, name="mix_in_fwd")
    sg_out = _sg_fwd(u, v, sg_ln_g, sg_ln_b, sg_w[0], sgbt, name="sg_fwd")
    q, k, vv, beta, gc = _dn_prep_fwd(qkv, bpre, apre, conv_full, alog, dtb, seq, name="dn_prep_fwd")
    grow = _chunk_rows_of(gc)
    wy_w, wy_u, q_dec, k_dec, qk, egl, inv = _delta_prep(q, k, vv, gc, grow, beta, name="delta_prep")
    o, states = _delta_seq_fwd(wy_w, wy_u, q_dec, k_dec, qk, egl, n_seq, seq, name="delta_seq_fwd")
    x2 = _mix_out_fwd(x1, sg_out, o, z, wo_sg, wo_dn, dn_norm, name="mix_out_fwd")
    ffn2_lands = _send_wait(ffn2_started, True, x2, name="gather_ffn2_wait")
    full.update({n: as_full(n, fill_own(land, shards[n])) for n, land in zip(ffn2_names, ffn2_lands)})
    dx3, loss_part, d_fn, h2, g2, u2 = _ffn_fwd(x2, ffn2_norm, full["ffn2_w_gate"], full["ffn2_w_up"], full["ffn2_w_down"],
                                                tgt, fnw, name="ffn2_fwd_loss")
    loss = lax.psum(loss_part[0, 0], ("x", "y", "c"))

    dx2, d_n2, d_g2, d_u2, d_d2 = _ffn_bwd(x2, ffn2_norm, h2, g2, u2, full["ffn2_w_gate"], full["ffn2_w_up"],
                                           full["ffn2_w_down"], dx3, name="ffn2_bwd")
    def by_owner(d_rows):
        return d_rows.reshape(N_DEV, -1, D_MODEL)

    ffn2_pieces = [by_owner(d_g2), by_owner(d_u2), by_owner(d_d2)]
    ffn2_sent, sent_token = _send_start(ffn2_pieces, False, name="grads_ffn2_start")
    dsg, do, dz, d_wo_sg, d_wo_dn, d_dnw = _mix_out_bwd(dx2, sg_out, o, z, wo_sg, wo_dn, dn_norm + sent_token[:1, :1],
                                                        name="mix_out_bwd")
    d_seq = _delta_seq_bwd(wy_w, wy_u, q_dec, k_dec, qk, egl, states, do, n_seq, seq, name="delta_seq_bwd")
    dq, dk, dv, dgc_a, dgrow, dbeta = _delta_par_bwd(q, k, vv, gc, grow, beta, inv, *d_seq, name="delta_par_bwd")
    dgc_b = _pad_lanes(jnp.transpose(dgrow[:, :DN_HEADS, :], (0, 2, 1)).reshape(t, DN_HEADS))
    dy_conv, dbpre, dapre, d_alog, d_dtb = _dn_prep_bwd(qkv, bpre, apre, conv_full, alog, dtb, dq, dk, dv, dbeta, dgc_a, dgc_b,
                                                        seq, name="dn_prep_bwd")
    dqkv, d_conv = _conv_bwd(qkv, dy_conv, conv_full, seq, name="conv_bwd")
    du, dvv, d_lng, d_lnb, d_wc, d_sgbt = _sg_bwd(u, v, sg_ln_g, sg_ln_b, sg_w[0], sgbt, dsg, name="sg_bwd")
    dx1, d_mixn, d_wp = _mix_in_bwd(x1, mix_norm, ws, dx2, (du, dvv, dqkv, dz, dbpre, dapre), name="mix_in_bwd")
    d_w_in_t = jnp.concatenate([d_wp[:n_proj], d_wp[_PROJ_OFFSETS[4]:_PROJ_OFFSETS[4] + DN_HEADS],
                                d_wp[_PROJ_OFFSETS[5]:_PROJ_OFFSETS[5] + DN_HEADS]], axis=0)
    d_w_out = jnp.concatenate([d_wo_sg, d_wo_dn], axis=0)
    mix_pieces = [by_owner(d_w_in_t), by_owner(d_w_out).astype(BF16)]
    mix_sent, sent_token = _send_start(mix_pieces, False, name="grads_mix_start")
    grad_x, d_n1, dg1, du1, a1, dyh1 = _ffn_bwd_x(x0, ffn1_norm + sent_token[:1, :1], g1, u1, full["ffn1_w_gate"],
                                                  full["ffn1_w_up"], full["ffn1_w_down"], dx1, name="ffn1_bwd_x")
    small_grads = dict(ffn1_norm=d_n1, mix_norm=d_mixn, ffn2_norm=d_n2, final_norm=d_fn, a_log=d_alog[:, :DN_HEADS],
                       dt_bias=d_dtb[:, :DN_HEADS], dn_norm=d_dnw, sg_ln_g=d_lng, sg_ln_b=d_lnb, sg_w=d_wc,
                       sg_b=d_sgbt[:, :SG_GROUPS].T, conv_w=d_conv[:CONV_K])
    small_src = _pack_small(small_grads)
    small_sent, small_token = _send_start([small_src], True, name="small_grads_start")
    late = []

    def send_early(k, grad):
        if k == 2:
            return None
        piece = by_owner(grad)
        sent, token = _send_start([piece], False, name="grads_" + ffn1_names[k] + "_start")
        late.append(((ffn1_names[k],), sent, [piece]))
        return token

    _, _, d_d1 = _ffn_wgrads(h1, dg1, du1, a1, dyh1, send_early, small_token, name="ffn1_bwd")

    def by_core(p8):
        return jnp.moveaxis(p8.reshape((4, 2) + p8.shape[1:]), 1, 0)

    own = [by_core(by_owner(d_d1))]
    from_sibling = _pair_swap(own, name="grads_to_sibling")
    core = lax.axis_index("c").astype(jnp.int32).reshape(1)
    chip_sums = [_pair_add(own[0], from_sibling[0], core, name="pair_add_" + ffn1_names[2])]
    received = {ffn1_names[2]: _chip_exchange(chip_sums, name="grads_to_owner")[0]}
    for names, sent, pieces in [(ffn2_names, ffn2_sent, ffn2_pieces), (mix_names, mix_sent, mix_pieces)] + late:
        lands = _send_wait(sent, False, received[ffn1_names[2]], name="grads_" + names[0] + "_wait")
        received.update({n: fill_own(land, lax.dynamic_index_in_dim(p, me, 0, keepdims=False))
                         for n, land, p in zip(names, lands, pieces)})
    res = {}
    for n in big:
        upd = _adam(received[n], *[rows_view(n, src[n][0]) for src in (weights, mom_m, mom_v)], name="adam_" + n)
        res[n] = [rows_view(n, a) for a in upd]

    (small_land,) = _send_wait(small_sent, True, received[ffn1_names[2]], name="small_grads_wait")
    small_parts = fill_own(small_land, small_src)
    zeros_conv = jnp.zeros((CONV_K * 3 * DN_WIDTH,), F32)
    packed = [_pack_small({**{n: src[n] for n, _ in _SMALL if n != "conv_w"}, "conv_w": zeros_conv})
              for src in (weights, mom_m, mom_v)]
    small_res = [_unpack_small(a) for a in _adam(small_parts, *packed, name="adam_small")]
    conv_grad = lax.dynamic_slice_in_dim(small_res[0]["conv_w"].reshape(CONV_K, 3 * DN_WIDTH), me * (3 * DN_WIDTH // N_DEV),
                                         3 * DN_WIDTH // N_DEV, axis=1)
    res["conv_w"] = _adam(conv_grad[None], conv_w[0], m_conv_w[0], v_conv_w[0], name="adam_conv_w")

    outs = [[], [], [], []]
    for n in order:
        for kind in range(4):
            if n in res:
                outs[kind].append(res[n][kind][None])
            else:
                outs[kind].append(small_res[kind][n].reshape(_SMALL_SHAPES[n]))
    return (loss, grad_x.reshape(x.shape), *outs[0], *outs[1], *outs[2], *outs[3])
```

```python
import functools

import jax
import jax.numpy as jnp
from jax import lax
from jax.experimental import pallas as pl
from jax.experimental.pallas import tpu as pltpu

F32 = jnp.float32
BF16 = jnp.bfloat16

D_MODEL = 1024
D_FF = 2816
SG_WIDTH = 512
SG_GROUPS = 8
SG_GROUP_DIM = 64
SG_CHUNK = 128
DN_WIDTH = 512
DN_HEAD_DIM = 128
DN_HEADS = 4
DN_CHUNK = 64
CONV_K = 4
EPS = 1e-6
N_DEV = 8
LANES = 128
HALO = 8

ADAM_LR = 0.001
ADAM_B1 = 0.9
ADAM_B2 = 0.999
ADAM_EPS = 1e-08
ADAM_WD = 0.01
ADAM_STEP = 10

VMEM_LIMIT = 60 * 1024 * 1024
TOKEN_BLOCK = 512
FF_BLOCK_FWD = 1408

_HI = lax.Precision.HIGHEST


def _cparams(sem):
    return pltpu.CompilerParams(dimension_semantics=sem, vmem_limit_bytes=VMEM_LIMIT)


def _tm(t, pref=TOKEN_BLOCK):
    return min(pref, t)


def _dg(a, b, ca, cb, precision):
    if precision is not None:
        return lax.dot_general(a, b, (((ca,), (cb,)), ((), ())), precision=precision, preferred_element_type=F32)
    return lax.dot_general(a.astype(BF16), b.astype(BF16), (((ca,), (cb,)), ((), ())), preferred_element_type=F32)


def _make_mm(exact):
    @jax.custom_vjp
    def mm(a, b):
        return _dg(a, b, 1, 0, exact)

    @jax.custom_vjp
    def mm_nt(a, b):
        return _dg(a, b, 1, 1, exact)

    @jax.custom_vjp
    def mm_tn(a, b):
        return _dg(a, b, 0, 0, exact)

    mm.defvjp(lambda a, b: (mm(a, b), (a, b)), lambda r, g: (mm_nt(g, r[1]), mm_tn(r[0], g)))
    mm_nt.defvjp(lambda a, b: (mm_nt(a, b), (a, b)), lambda r, g: (mm(g, r[1]), mm_tn(g, r[0])))
    mm_tn.defvjp(lambda a, b: (mm_tn(a, b), (a, b)), lambda r, g: (mm_nt(r[1], g), mm(r[0], g)))
    return mm, mm_nt, mm_tn


mm, mm_nt, mm_tn = _make_mm(None)
mmx, mmx_nt, mmx_tn = _make_mm(_HI)
mmh, mmh_nt, mmh_tn = _make_mm(lax.Precision.HIGH)


def _sigmoid(x):
    return 1.0 / (1.0 + jnp.exp(-x))


def _silu(x):
    return x * _sigmoid(x)


def _softplus(x):
    neg_abs = jnp.where(x > 0, -x, x)
    return jnp.where(x > 0, x, 0.0) + jnp.log(1.0 + jnp.exp(neg_abs))


def _gelu(x):
    return 0.5 * x * (1.0 + jnp.tanh(0.7978845608028654 * (x + 0.044715 * (x * x * x))))


def _rms_fwd(x, g):
    r = lax.rsqrt(jnp.mean(x * x, axis=-1, keepdims=True) + EPS)
    xh = x * r
    return xh * g, xh, r


def _rms_bwd(dh, xh, r, g):
    dxh = dh * g
    dx = r * (dxh - xh * jnp.mean(dxh * xh, axis=-1, keepdims=True))
    return dx, jnp.sum(dh * xh, axis=0, keepdims=True)


def _acc_out(ref, first, val):
    @pl.when(first)
    def _():
        ref[...] = val

    @pl.when(jnp.logical_not(first))
    def _():
        ref[...] += val


def _ffn_fwd(x, nw, wg, wu, wd, tgt=None, fnw=None, *, name):
    t = x.shape[0]
    tm, fb = _tm(t), FF_BLOCK_FWD
    n_t, n_f = t // tm, D_FF // fb
    with_loss = tgt is not None

    def body(*refs):
        if with_loss:
            (x_ref, nw_ref, wg_ref, wu_ref, wd_ref, tgt_ref, fnw_ref, dy_ref, loss_ref, dfn_ref, h_ref, g_ref, u_ref,
             acc_s) = refs
        else:
            x_ref, nw_ref, wg_ref, wu_ref, wd_ref, y_ref, h_ref, g_ref, u_ref, acc_s = refs
        i, j = pl.program_id(0), pl.program_id(1)

        @pl.when(j == 0)
        def _():
            h, _, _ = _rms_fwd(x_ref[...], nw_ref[...])
            h_ref[...] = h.astype(BF16)
            acc_s[...] = jnp.zeros_like(acc_s)

        h = h_ref[...]
        nt = (((1,), (1,)), ((), ()))
        g = lax.dot_general(h, wg_ref[...], nt, preferred_element_type=F32)
        u = lax.dot_general(h, wu_ref[...], nt, preferred_element_type=F32)
        g_ref[...] = g.astype(BF16)
        u_ref[...] = u.astype(BF16)
        a = _silu(g) * u
        acc_s[...] += jnp.dot(a.astype(BF16), wd_ref[...], preferred_element_type=F32)

        @pl.when(j == n_f - 1)
        def _():
            y = x_ref[...] + 0.5 * acc_s[...]
            if not with_loss:
                y_ref[...] = y
            else:
                gf = fnw_ref[...]
                out, xh, r = _rms_fwd(y, gf)
                err = out - tgt_ref[...]
                part = 0.5 * jnp.sum(jnp.mean(err * err, axis=-1, keepdims=True), axis=0, keepdims=True)
                d_out = err * (1.0 / D_MODEL)
                dy, dgf = _rms_bwd(d_out, xh, r, gf)
                dy_ref[...] = dy
                _acc_out(loss_ref, i == 0, jnp.broadcast_to(part, loss_ref.shape))
                _acc_out(dfn_ref, i == 0, dgf)

    row = lambda i, j: (i, 0)
    const = lambda i, j: (0, 0)
    in_specs = [
        pl.BlockSpec((tm, D_MODEL), row),
        pl.BlockSpec((1, D_MODEL), const),
        pl.BlockSpec((fb, D_MODEL), lambda i, j: (j, 0)),
        pl.BlockSpec((fb, D_MODEL), lambda i, j: (j, 0)),
        pl.BlockSpec((fb, D_MODEL), lambda i, j: (j, 0)),
    ]
    args = [x, nw, wg, wu, wd]
    saved_shape = (jax.ShapeDtypeStruct((t, D_MODEL), BF16), jax.ShapeDtypeStruct((t, D_FF), BF16),
                   jax.ShapeDtypeStruct((t, D_FF), BF16))
    saved_specs = (pl.BlockSpec((tm, D_MODEL), row), pl.BlockSpec((tm, fb), lambda i, j: (i, j)),
                   pl.BlockSpec((tm, fb), lambda i, j: (i, j)))
    if with_loss:
        in_specs += [pl.BlockSpec((tm, D_MODEL), row), pl.BlockSpec((1, D_MODEL), const)]
        args += [tgt, fnw]
        out_shape = (jax.ShapeDtypeStruct((t, D_MODEL), F32), jax.ShapeDtypeStruct((8, LANES), F32),
                     jax.ShapeDtypeStruct((1, D_MODEL), F32)) + saved_shape
        out_specs = (pl.BlockSpec((tm, D_MODEL), row), pl.BlockSpec((8, LANES), const),
                     pl.BlockSpec((1, D_MODEL), const)) + saved_specs
        sem = ("arbitrary", "arbitrary")
    else:
        out_shape = (jax.ShapeDtypeStruct((t, D_MODEL), F32),) + saved_shape
        out_specs = (pl.BlockSpec((tm, D_MODEL), row),) + saved_specs
        sem = ("parallel", "arbitrary")
    return pl.pallas_call(
        body, name=name, grid=(n_t, n_f), in_specs=in_specs, out_specs=out_specs, out_shape=out_shape,
        scratch_shapes=[pltpu.VMEM((tm, D_MODEL), F32)],
        compiler_params=_cparams(sem),
    )(*args)


def _ffn_bwd_x(x, nw, g, u, wg, wu, wd, dy, *, name):
    t = x.shape[0]
    tm = _tm(t, 256)

    def body(x_ref, nw_ref, g_ref, u_ref, wg_ref, wu_ref, wd_ref, dy_ref, dx_ref, dnw_ref, dg_ref, du_ref, a_ref, dyh_ref):
        i = pl.program_id(0)
        nt = (((1,), (1,)), ((), ()))
        dy = dy_ref[...]
        dyh = (0.5 * dy).astype(BF16)
        dyh_ref[...] = dyh
        gate, up = g_ref[...].astype(F32), u_ref[...].astype(F32)
        s = _sigmoid(gate)
        gs = gate * s
        da = lax.dot_general(dyh, wd_ref[...], nt, preferred_element_type=F32)
        dg = (da * up * (s + gs * (1.0 - s))).astype(BF16)
        du = (da * gs).astype(BF16)
        dg_ref[...] = dg
        du_ref[...] = du
        a_ref[...] = (gs * up).astype(BF16)
        dh = (jnp.dot(dg, wg_ref[...], preferred_element_type=F32)
              + jnp.dot(du, wu_ref[...], preferred_element_type=F32))
        xv = x_ref[...]
        r = lax.rsqrt(jnp.mean(xv * xv, axis=-1, keepdims=True) + EPS)
        dx, dnw = _rms_bwd(dh, xv * r, r, nw_ref[...])
        dx_ref[...] = dy + dx
        _acc_out(dnw_ref, i == 0, dnw)

    row = lambda i: (i, 0)
    const = lambda i: (0, 0)
    once = pl.Buffered(1)
    wide = pl.BlockSpec((tm, D_FF), row)
    return pl.pallas_call(
        body, name=name, grid=(t // tm,),
        in_specs=[pl.BlockSpec((tm, D_MODEL), row), pl.BlockSpec((1, D_MODEL), const), wide, wide,
                  pl.BlockSpec((D_FF, D_MODEL), const, pipeline_mode=once), pl.BlockSpec((D_FF, D_MODEL), const, pipeline_mode=once),
                  pl.BlockSpec((D_FF, D_MODEL), const, pipeline_mode=once), pl.BlockSpec((tm, D_MODEL), row)],
        out_specs=(pl.BlockSpec((tm, D_MODEL), row), pl.BlockSpec((1, D_MODEL), const), wide, wide, wide,
                   pl.BlockSpec((tm, D_MODEL), row)),
        out_shape=(jax.ShapeDtypeStruct((t, D_MODEL), F32), jax.ShapeDtypeStruct((1, D_MODEL), F32),
                   jax.ShapeDtypeStruct((t, D_FF), BF16), jax.ShapeDtypeStruct((t, D_FF), BF16),
                   jax.ShapeDtypeStruct((t, D_FF), BF16), jax.ShapeDtypeStruct((t, D_MODEL), BF16)),
        compiler_params=_cparams(("arbitrary",)),
    )(x, nw, g, u, wg, wu, wd, dy)


def _wgrad(a, b, bm, bn, after=None, *, name):
    k, m = a.shape
    n = b.shape[1]
    tk = _tm(k, 2048)
    n_k = k // tk

    def body(a_ref, b_ref, *rest):
        o_ref, acc_s = rest[-2], rest[-1]
        s = pl.program_id(2)
        part = lax.dot_general(a_ref[...], b_ref[...], (((0,), (0,)), ((), ())), preferred_element_type=F32)
        _acc_out(acc_s, s == 0, part)

        @pl.when(s == n_k - 1)
        def _():
            o_ref[...] = acc_s[...].astype(BF16)

    return pl.pallas_call(
        body, name=name, grid=(m // bm, n // bn, n_k),
        in_specs=[pl.BlockSpec((tk, bm), lambda i, j, s: (s, i)), pl.BlockSpec((tk, bn), lambda i, j, s: (s, j))]
        + ([] if after is None else [_HBM]),
        out_specs=pl.BlockSpec((bm, bn), lambda i, j, s: (i, j)),
        out_shape=jax.ShapeDtypeStruct((m, n), BF16),
        scratch_shapes=[pltpu.VMEM((bm, bn), F32)],
        compiler_params=_cparams(("parallel", "parallel", "arbitrary")),
    )(a, b, *([] if after is None else [after]))


def _ffn_wgrads(h, dg, du, a, dyh, between=None, after=None, *, name):
    grads = []
    for k, (lhs, rhs, tag) in enumerate(((dg, h, "_wg"), (du, h, "_wu"), (a, dyh, "_wd"))):
        grads.append(_wgrad(lhs, rhs, D_FF // 2, D_MODEL, after, name=name + tag))
        after = None if between is None else between(k, grads[-1])
    return grads


def _ffn_bwd(x, nw, h, g, u, wg, wu, wd, dy, *, name):
    dx, dnw, dg, du, a, dyh = _ffn_bwd_x(x, nw, g, u, wg, wu, wd, dy, name=name + "_x")
    return (dx, dnw, *_ffn_wgrads(h, dg, du, a, dyh, name=name))


_PROJ_WIDTHS = (SG_WIDTH, SG_WIDTH, 3 * DN_WIDTH, DN_WIDTH, LANES, LANES)


def _mix_in_fwd(x, nw, ws, *, name):
    t = x.shape[0]
    tm = _tm(t)

    def body(x_ref, nw_ref, *refs):
        w_refs, o_refs = refs[:6], refs[6:]
        h, _, _ = _rms_fwd(x_ref[...], nw_ref[...])
        h = h.astype(BF16)
        for w_ref, o_ref in zip(w_refs, o_refs):
            o_ref[...] = lax.dot_general(h, w_ref[...], (((1,), (1,)), ((), ())), preferred_element_type=F32)

    row = lambda i: (i, 0)
    const = lambda i: (0, 0)
    return pl.pallas_call(
        body, name=name, grid=(t // tm,),
        in_specs=[pl.BlockSpec((tm, D_MODEL), row), pl.BlockSpec((1, D_MODEL), const)]
        + [pl.BlockSpec((n, D_MODEL), const) for n in _PROJ_WIDTHS],
        out_specs=tuple(pl.BlockSpec((tm, n), row) for n in _PROJ_WIDTHS),
        out_shape=tuple(jax.ShapeDtypeStruct((t, n), F32) for n in _PROJ_WIDTHS),
        compiler_params=_cparams(("parallel",)),
    )(x, nw, *ws)


_PROJ_TOTAL = sum(_PROJ_WIDTHS)
_PROJ_OFFSETS = tuple(sum(_PROJ_WIDTHS[:k]) for k in range(len(_PROJ_WIDTHS)))


def _mix_in_bwd(x, nw, ws, dres, dps, *, name):
    t = x.shape[0]
    tm = _tm(t, 256)

    def body(x_ref, nw_ref, dres_ref, *refs):
        w_refs, dp_refs, dx_ref, dnw_ref, h_ref, dpb_ref = refs[:6], refs[6:12], refs[12], refs[13], refs[14], refs[15]
        i = pl.program_id(0)
        hf, xh, r = _rms_fwd(x_ref[...], nw_ref[...])
        h_ref[...] = hf.astype(BF16)
        dh = jnp.zeros((tm, D_MODEL), F32)
        for w_ref, dp_ref, off, width in zip(w_refs, dp_refs, _PROJ_OFFSETS, _PROJ_WIDTHS):
            dp = dp_ref[...].astype(BF16)
            dpb_ref[:, off:off + width] = dp
            dh = dh + jnp.dot(dp, w_ref[...], preferred_element_type=F32)
        dx, dnw = _rms_bwd(dh, xh, r, nw_ref[...])
        dx_ref[...] = dres_ref[...] + dx
        _acc_out(dnw_ref, i == 0, dnw)

    row = lambda i: (i, 0)
    const = lambda i: (0, 0)
    dx, dnw, h, dpb = pl.pallas_call(
        body, name=name + "_x", grid=(t // tm,),
        in_specs=[pl.BlockSpec((tm, D_MODEL), row), pl.BlockSpec((1, D_MODEL), const), pl.BlockSpec((tm, D_MODEL), row)]
        + [pl.BlockSpec((n, D_MODEL), const) for n in _PROJ_WIDTHS]
        + [pl.BlockSpec((tm, n), row) for n in _PROJ_WIDTHS],
        out_specs=(pl.BlockSpec((tm, D_MODEL), row), pl.BlockSpec((1, D_MODEL), const), pl.BlockSpec((tm, D_MODEL), row),
                   pl.BlockSpec((tm, _PROJ_TOTAL), row)),
        out_shape=(jax.ShapeDtypeStruct((t, D_MODEL), F32), jax.ShapeDtypeStruct((1, D_MODEL), F32),
                   jax.ShapeDtypeStruct((t, D_MODEL), BF16), jax.ShapeDtypeStruct((t, _PROJ_TOTAL), BF16)),
        compiler_params=_cparams(("arbitrary",)),
    )(x, nw, dres, *ws, *dps)
    return dx, dnw, _wgrad(dpb, h, _PROJ_TOTAL // 2, D_MODEL, name=name + "_w")


def _sg_fn(u, v, lng, lnb, wcs, sgbt):
    lane = lax.broadcasted_iota(jnp.int32, (1, SG_WIDTH), 1)
    lane_b = lax.broadcasted_iota(jnp.int32, (1, LANES), 1)
    rr = lax.broadcasted_iota(jnp.int32, (SG_CHUNK, SG_CHUNK), 0)
    cc = lax.broadcasted_iota(jnp.int32, (SG_CHUNK, SG_CHUNK), 1)
    gu, gv = _gelu(u), _gelu(v)
    mu = jnp.mean(gv, axis=-1, keepdims=True)
    cen = gv - mu
    var = jnp.mean(cen * cen, axis=-1, keepdims=True)
    ln = cen * lax.rsqrt(var + EPS) * lng + lnb
    vs = jnp.zeros_like(u)
    for g in range(SG_GROUPS):
        in_group = jnp.logical_and(lane >= g * SG_GROUP_DIM, lane < (g + 1) * SG_GROUP_DIM)
        w_causal = jnp.where(rr >= cc, wcs[g], 0.0)
        bias = jnp.sum(jnp.where(lane_b == g, sgbt, 0.0), axis=1, keepdims=True)
        vs = vs + jnp.where(in_group, mm(w_causal, ln) + bias, 0.0)
    return gu * vs


def _sg_fwd(u, v, lng, lnb, wc, sgbt, *, name):
    t = u.shape[0]
    tm = _tm(t)

    def body(u_ref, v_ref, lng_ref, lnb_ref, wc_ref, sgbt_ref, o_ref):
        wcs = [wc_ref[g] for g in range(SG_GROUPS)]
        for c in range(tm // SG_CHUNK):
            rows = pl.ds(c * SG_CHUNK, SG_CHUNK)
            o_ref[rows, :] = _sg_fn(u_ref[rows, :], v_ref[rows, :], lng_ref[...], lnb_ref[...], wcs, sgbt_ref[...])

    row = lambda i: (i, 0)
    const = lambda i: (0, 0)
    return pl.pallas_call(
        body, name=name, grid=(t // tm,),
        in_specs=[pl.BlockSpec((tm, SG_WIDTH), row), pl.BlockSpec((tm, SG_WIDTH), row),
                  pl.BlockSpec((1, SG_WIDTH), const), pl.BlockSpec((1, SG_WIDTH), const),
                  pl.BlockSpec((SG_GROUPS, SG_CHUNK, SG_CHUNK), lambda i: (0, 0, 0)), pl.BlockSpec((SG_CHUNK, LANES), const)],
        out_specs=pl.BlockSpec((tm, SG_WIDTH), row),
        out_shape=jax.ShapeDtypeStruct((t, SG_WIDTH), F32),
        compiler_params=_cparams(("parallel",)),
    )(u, v, lng, lnb, wc, sgbt)


def _sg_bwd(u, v, lng, lnb, wc, sgbt, dout, *, name):
    t = u.shape[0]
    tm = _tm(t)

    def body(u_ref, v_ref, lng_ref, lnb_ref, wc_ref, sgbt_ref, do_ref, du_ref, dv_ref, dlng_ref, dlnb_ref, dwc_ref, dsgbt_ref):
        i = pl.program_id(0)
        wcs = [wc_ref[g] for g in range(SG_GROUPS)]
        tot = None
        for c in range(tm // SG_CHUNK):
            rows = pl.ds(c * SG_CHUNK, SG_CHUNK)
            _, vjp = jax.vjp(_sg_fn, u_ref[rows, :], v_ref[rows, :], lng_ref[...], lnb_ref[...], wcs, sgbt_ref[...])
            du, dv, dlng, dlnb, dwcs, dsgbt = vjp(do_ref[rows, :])
            du_ref[rows, :] = du
            dv_ref[rows, :] = dv
            part = (dlng, dlnb, dwcs, dsgbt)
            tot = part if tot is None else jax.tree.map(jnp.add, tot, part)
        dlng, dlnb, dwcs, dsgbt = tot
        _acc_out(dlng_ref, i == 0, dlng)
        _acc_out(dlnb_ref, i == 0, dlnb)
        _acc_out(dsgbt_ref, i == 0, dsgbt)
        for g in range(SG_GROUPS):
            @pl.when(i == 0)
            def _(g=g):
                dwc_ref[g] = dwcs[g]

            @pl.when(i > 0)
            def _(g=g):
                dwc_ref[g] += dwcs[g]

    row = lambda i: (i, 0)
    const = lambda i: (0, 0)
    wspec = pl.BlockSpec((SG_GROUPS, SG_CHUNK, SG_CHUNK), lambda i: (0, 0, 0))
    return pl.pallas_call(
        body, name=name, grid=(t // tm,),
        in_specs=[pl.BlockSpec((tm, SG_WIDTH), row), pl.BlockSpec((tm, SG_WIDTH), row),
                  pl.BlockSpec((1, SG_WIDTH), const), pl.BlockSpec((1, SG_WIDTH), const), wspec,
                  pl.BlockSpec((SG_CHUNK, LANES), const), pl.BlockSpec((tm, SG_WIDTH), row)],
        out_specs=(pl.BlockSpec((tm, SG_WIDTH), row), pl.BlockSpec((tm, SG_WIDTH), row),
                   pl.BlockSpec((1, SG_WIDTH), const), pl.BlockSpec((1, SG_WIDTH), const), wspec,
                   pl.BlockSpec((SG_CHUNK, LANES), const)),
        out_shape=(jax.ShapeDtypeStruct((t, SG_WIDTH), F32), jax.ShapeDtypeStruct((t, SG_WIDTH), F32),
                   jax.ShapeDtypeStruct((1, SG_WIDTH), F32), jax.ShapeDtypeStruct((1, SG_WIDTH), F32),
                   jax.ShapeDtypeStruct((SG_GROUPS, SG_CHUNK, SG_CHUNK), F32), jax.ShapeDtypeStruct((SG_CHUNK, LANES), F32)),
        compiler_params=_cparams(("arbitrary",)),
    )(u, v, lng, lnb, wc, sgbt, dout)


def _conv_taps(ext, w, tm):
    y = None
    for j in range(CONV_K):
        s = CONV_K - 1 - j
        shifted = ext if s == 0 else pltpu.roll(ext, s, 0)
        term = w[j:j + 1, :] * shifted[HALO:HALO + tm, :]
        y = term if y is None else y + term
    return y


def _post_conv(yq, yk, yv, bpre, apre, alog, dtb):
    def l2(a):
        return a * lax.rsqrt(jnp.sum(a * a, axis=-1, keepdims=True) + EPS)

    q = [l2(_silu(a)) for a in yq]
    k = [l2(_silu(a)) for a in yk]
    return q, k, _silu(yv), _sigmoid(bpre), -jnp.exp(alog) * _softplus(apre + dtb)


def _chunk_tril(tm):
    rr = lax.broadcasted_iota(jnp.int32, (tm, tm), 0)
    cc = lax.broadcasted_iota(jnp.int32, (tm, tm), 1)
    shift = DN_CHUNK.bit_length() - 1
    same = jnp.right_shift(rr, shift) == jnp.right_shift(cc, shift)
    return jnp.where(jnp.logical_and(same, rr >= cc), 1.0, 0.0).astype(F32)


def _halo_specs(tm, width, n_blocks_seq, n_blocks):
    per = tm // HALO
    prev = pl.BlockSpec((HALO, width), lambda i: (jnp.maximum(i * per - 1, 0), 0))
    nxt = pl.BlockSpec((HALO, width), lambda i: (jnp.minimum((i + 1) * per, n_blocks * per - 1), 0))
    return prev, nxt


def _split_heads(ref, base):
    return [ref[:, base + h * DN_HEAD_DIM: base + (h + 1) * DN_HEAD_DIM] for h in range(DN_HEADS)]


def _dn_prep_fwd(qkv, bpre, apre, conv_w, alog, dtb, seq, *, name):
    t = qkv.shape[0]
    tm = _tm(t)
    bps = seq // tm
    cw = 3 * DN_WIDTH

    def body(x_ref, halo_ref, b_ref, a_ref, w_ref, alog_ref, dtb_ref, q_ref, k_ref, v_ref, beta_ref, gc_ref):
        i = pl.program_id(0)
        keep = jnp.where(i % bps == 0, 0.0, 1.0)
        ext = jnp.concatenate([halo_ref[...] * keep, x_ref[...]], axis=0)
        y = _conv_taps(ext, w_ref[...], tm)
        yq = [y[:, h * DN_HEAD_DIM:(h + 1) * DN_HEAD_DIM] for h in range(DN_HEADS)]
        yk = [y[:, DN_WIDTH + h * DN_HEAD_DIM: DN_WIDTH + (h + 1) * DN_HEAD_DIM] for h in range(DN_HEADS)]
        q, k, v, beta, g = _post_conv(yq, yk, y[:, 2 * DN_WIDTH:], b_ref[...], a_ref[...], alog_ref[...], dtb_ref[...])
        for h in range(DN_HEADS):
            q_ref[:, h * DN_HEAD_DIM:(h + 1) * DN_HEAD_DIM] = q[h]
            k_ref[:, h * DN_HEAD_DIM:(h + 1) * DN_HEAD_DIM] = k[h]
        v_ref[...] = v
        beta_ref[...] = beta
        gc_ref[...] = mmx(_chunk_tril(tm), g)

    row = lambda i: (i, 0)
    const = lambda i: (0, 0)
    prev, _ = _halo_specs(tm, cw, bps, t // tm)
    return pl.pallas_call(
        body, name=name, grid=(t // tm,),
        in_specs=[pl.BlockSpec((tm, cw), row), prev, pl.BlockSpec((tm, LANES), row), pl.BlockSpec((tm, LANES), row),
                  pl.BlockSpec((CONV_K, cw), const), pl.BlockSpec((1, LANES), const), pl.BlockSpec((1, LANES), const)],
        out_specs=tuple(pl.BlockSpec((tm, n), row) for n in (DN_WIDTH, DN_WIDTH, DN_WIDTH, LANES, LANES)),
        out_shape=tuple(jax.ShapeDtypeStruct((t, n), F32) for n in (DN_WIDTH, DN_WIDTH, DN_WIDTH, LANES, LANES)),
        compiler_params=_cparams(("parallel",)),
    )(qkv, qkv, bpre, apre, conv_w, alog, dtb)


def _dn_prep_bwd(qkv, bpre, apre, conv_w, alog, dtb, dq, dk, dv, dbeta, dgc, dgc2, seq, *, name):
    t = qkv.shape[0]
    tm = _tm(t)
    bps = seq // tm
    cw = 3 * DN_WIDTH

    def body(x_ref, halo_ref, b_ref, a_ref, w_ref, alog_ref, dtb_ref, dq_ref, dk_ref, dv_ref, dbeta_ref, dgc_ref, dgc2_ref,
             dy_ref, db_ref, da_ref, dalog_ref, ddtb_ref):
        i = pl.program_id(0)
        keep = jnp.where(i % bps == 0, 0.0, 1.0)
        ext = jnp.concatenate([halo_ref[...] * keep, x_ref[...]], axis=0)
        y = _conv_taps(ext, w_ref[...], tm)
        yq = [y[:, h * DN_HEAD_DIM:(h + 1) * DN_HEAD_DIM] for h in range(DN_HEADS)]
        yk = [y[:, DN_WIDTH + h * DN_HEAD_DIM: DN_WIDTH + (h + 1) * DN_HEAD_DIM] for h in range(DN_HEADS)]
        _, vjp = jax.vjp(_post_conv, yq, yk, y[:, 2 * DN_WIDTH:], b_ref[...], a_ref[...], alog_ref[...], dtb_ref[...])
        dg = mmx_tn(_chunk_tril(tm), dgc_ref[...] + dgc2_ref[...])
        dyq, dyk, dyv, db, da, dalog, ddtb = vjp((_split_heads(dq_ref, 0), _split_heads(dk_ref, 0), dv_ref[...],
                                                  dbeta_ref[...], dg))
        for h in range(DN_HEADS):
            dy_ref[:, h * DN_HEAD_DIM:(h + 1) * DN_HEAD_DIM] = dyq[h]
            dy_ref[:, DN_WIDTH + h * DN_HEAD_DIM: DN_WIDTH + (h + 1) * DN_HEAD_DIM] = dyk[h]
        dy_ref[:, 2 * DN_WIDTH:] = dyv
        db_ref[...] = db
        da_ref[...] = da
        _acc_out(dalog_ref, i == 0, dalog)
        _acc_out(ddtb_ref, i == 0, ddtb)

    row = lambda i: (i, 0)
    const = lambda i: (0, 0)
    prev, _ = _halo_specs(tm, cw, bps, t // tm)
    return pl.pallas_call(
        body, name=name, grid=(t // tm,),
        in_specs=[pl.BlockSpec((tm, cw), row), prev, pl.BlockSpec((tm, LANES), row), pl.BlockSpec((tm, LANES), row),
                  pl.BlockSpec((CONV_K, cw), const), pl.BlockSpec((1, LANES), const), pl.BlockSpec((1, LANES), const),
                  pl.BlockSpec((tm, DN_WIDTH), row), pl.BlockSpec((tm, DN_WIDTH), row), pl.BlockSpec((tm, DN_WIDTH), row),
                  pl.BlockSpec((tm, LANES), row), pl.BlockSpec((tm, LANES), row), pl.BlockSpec((tm, LANES), row)],
        out_specs=(pl.BlockSpec((tm, cw), row), pl.BlockSpec((tm, LANES), row), pl.BlockSpec((tm, LANES), row),
                   pl.BlockSpec((1, LANES), const), pl.BlockSpec((1, LANES), const)),
        out_shape=(jax.ShapeDtypeStruct((t, cw), F32), jax.ShapeDtypeStruct((t, LANES), F32), jax.ShapeDtypeStruct((t, LANES), F32),
                   jax.ShapeDtypeStruct((1, LANES), F32), jax.ShapeDtypeStruct((1, LANES), F32)),
        compiler_params=_cparams(("arbitrary",)),
    )(qkv, qkv, bpre, apre, conv_w, alog, dtb, dq, dk, dv, dbeta, dgc, dgc2)


def _conv_bwd(qkv, dy, conv_w, seq, *, name):
    t = qkv.shape[0]
    tm = _tm(t)
    bps = seq // tm
    cw = 3 * DN_WIDTH
    n_ext = tm + HALO

    def body(x_ref, halo_ref, dy_ref, dyn_ref, w_ref, dx_ref, dw_ref):
        i = pl.program_id(0)
        keep_prev = jnp.where(i % bps == 0, 0.0, 1.0)
        keep_next = jnp.where(i % bps == bps - 1, 0.0, 1.0)
        ext = jnp.concatenate([halo_ref[...] * keep_prev, x_ref[...]], axis=0)
        dy = dy_ref[...]
        dyext = jnp.concatenate([dy, dyn_ref[...] * keep_next], axis=0)
        w = w_ref[...]

        @pl.when(i == 0)
        def _():
            dw_ref[...] = jnp.zeros_like(dw_ref)

        dx = None
        for j in range(CONV_K):
            s = CONV_K - 1 - j
            fut = dyext if s == 0 else pltpu.roll(dyext, n_ext - s, 0)
            term = w[j:j + 1, :] * fut[0:tm, :]
            dx = term if dx is None else dx + term
            past = ext if s == 0 else pltpu.roll(ext, s, 0)
            dw_ref[j:j + 1, :] += jnp.sum(dy * past[HALO:HALO + tm, :], axis=0, keepdims=True)
        dx_ref[...] = dx

    row = lambda i: (i, 0)
    const = lambda i: (0, 0)
    prev, nxt = _halo_specs(tm, cw, bps, t // tm)
    return pl.pallas_call(
        body, name=name, grid=(t // tm,),
        in_specs=[pl.BlockSpec((tm, cw), row), prev, pl.BlockSpec((tm, cw), row), nxt, pl.BlockSpec((CONV_K, cw), const)],
        out_specs=(pl.BlockSpec((tm, cw), row), pl.BlockSpec((HALO, cw), const)),
        out_shape=(jax.ShapeDtypeStruct((t, cw), F32), jax.ShapeDtypeStruct((HALO, cw), F32)),
        compiler_params=_cparams(("arbitrary",)),
    )(qkv, qkv, dy, dy, conv_w)


def _inv_unit_lower(l_mats, eye):
    invs = [eye - l for l in l_mats]
    powers = list(l_mats)
    n = 2
    while n < eye.shape[0]:
        powers = [mmh(p, p) for p in powers]
        invs = [inv + mmh(inv, p) for inv, p in zip(invs, powers)]
        n *= 2
    return invs


@jax.custom_vjp
def _solve(l_mat, rhs, inv):
    return mmh(inv, rhs)


def _solve_fwd(l_mat, rhs, inv):
    sol = mmh(inv, rhs)
    return sol, (inv, sol)


def _solve_bwd(res, d_sol):
    inv, sol = res
    d_rhs = mm_tn(inv, d_sol)
    return -mm_nt(d_rhs, sol), d_rhs, jnp.zeros_like(inv)


_solve.defvjp(_solve_fwd, _solve_bwd)


def _prep_fn(q, k, v, gc, gr, b, inv):
    ids = range(len(q))
    c = q[0].shape[0]
    rr = lax.broadcasted_iota(jnp.int32, (c, c), 0)
    cc = lax.broadcasted_iota(jnp.int32, (c, c), 1)
    incl, strict = rr >= cc, rr > cc
    is_last = lax.broadcasted_iota(jnp.int32, (c, 1), 0) == c - 1
    qs = [q[i] * (DN_HEAD_DIM ** -0.5) for i in ids]
    decay = [jnp.where(incl, jnp.exp(jnp.where(incl, gc[i] - gr[i], 0.0)), 0.0) for i in ids]
    kb = [k[i] * b[i] for i in ids]
    vb = [v[i] * b[i] for i in ids]
    kk = [mm_nt(kb[i], k[i]) for i in ids]
    l_mat = [jnp.where(strict, kk[i] * decay[i], 0.0) for i in ids]
    eg = [jnp.exp(gc[i]) for i in ids]
    if inv is None:
        inv = _inv_unit_lower(l_mat, jnp.where(rr == cc, 1.0, 0.0).astype(F32))
    u_wy = [_solve(l_mat[i], vb[i], inv[i]) for i in ids]
    w_wy = [_solve(l_mat[i], kb[i] * eg[i], inv[i]) for i in ids]
    qk = [mm_nt(qs[i], k[i]) * decay[i] for i in ids]
    g_last = [jnp.sum(jnp.where(is_last, gc[i], 0.0), axis=0, keepdims=True) for i in ids]
    k_dec = [k[i] * jnp.exp(g_last[i] - gc[i]) for i in ids]
    egl = [jnp.broadcast_to(jnp.exp(g_last[i]), (1, LANES)) for i in ids]
    return [(w_wy[i], u_wy[i], qs[i] * eg[i], k_dec[i], qk[i], egl[i]) for i in ids], inv


def _seq_fn(w, u, qd, kd, qk, egl, s):
    ids = range(len(w))
    ws = [mm(w[i], s[i]) for i in ids]
    qs = [mm(qd[i], s[i]) for i in ids]
    v_new = [u[i] - ws[i] for i in ids]
    o = [qs[i] + mm(qk[i], v_new[i]) for i in ids]
    s_new = [s[i] * egl[i] + mm_tn(kd[i], v_new[i]) for i in ids]
    return o, s_new


def _lane_col(a, h):
    lane = lax.broadcasted_iota(jnp.int32, (1, LANES), 1)
    return jnp.sum(jnp.where(lane == h, a, 0.0), axis=1, keepdims=True)


def _col_lane(col, h):
    lane = lax.broadcasted_iota(jnp.int32, (1, LANES), 1)
    return jnp.where(lane == h, col, 0.0)


def _head_cols(h):
    return slice(h * DN_HEAD_DIM, (h + 1) * DN_HEAD_DIM)


def _chunk_rows(n):
    return pl.ds(pl.multiple_of(n * DN_CHUNK, DN_CHUNK), DN_CHUNK)


def _delta_prep(q, k, v, gc, grow, beta, *, name):
    t = q.shape[0]
    tm = _tm(t)
    cpb = tm // DN_CHUNK
    n_chunks = t // DN_CHUNK
    group = 2

    def body(q_ref, k_ref, v_ref, gc_ref, gr_ref, b_ref, w_ref, u_ref, qd_ref, kd_ref, qk_ref, egl_ref, inv_ref):
        def step(m, carry):
            probs = [(m * group + e, h) for e in range(group) for h in range(DN_HEADS)]
            gcb = [gc_ref[_chunk_rows(m * group + e), :] for e in range(group)]
            bb = [b_ref[_chunk_rows(m * group + e), :] for e in range(group)]
            grb = [gr_ref[m * group + e] for e in range(group)]
            for e in range(group):
                egl_ref[m * group + e] = jnp.zeros((HALO, LANES), F32)
            outs, invs = _prep_fn(
                [q_ref[_chunk_rows(n), _head_cols(h)] for n, h in probs], [k_ref[_chunk_rows(n), _head_cols(h)] for n, h in probs],
                [v_ref[_chunk_rows(n), _head_cols(h)] for n, h in probs],
                [_lane_col(gcb[e], h) for e in range(group) for h in range(DN_HEADS)],
                [grb[e][h:h + 1, :] for e in range(group) for h in range(DN_HEADS)],
                [_lane_col(bb[e], h) for e in range(group) for h in range(DN_HEADS)], None)
            for (n, h), (w, u, qd, kd, qk, egl), inv in zip(probs, outs, invs):
                rows, cols = _chunk_rows(n), _head_cols(h)
                w_ref[rows, cols] = w.astype(BF16)
                u_ref[rows, cols] = u
                qd_ref[rows, cols] = qd.astype(BF16)
                kd_ref[rows, cols] = kd.astype(BF16)
                qk_ref[n, h] = qk
                inv_ref[n, h] = inv
                egl_ref[n, h:h + 1, :] = egl
            return carry

        lax.fori_loop(0, cpb // group, step, 0)

    row = lambda i: (i, 0)
    tok = pl.BlockSpec((tm, DN_WIDTH), row)
    lanes = pl.BlockSpec((tm, LANES), row)
    sq = pl.BlockSpec((cpb, DN_HEADS, DN_CHUNK, DN_CHUNK), lambda i: (i, 0, 0, 0))
    return pl.pallas_call(
        body, name=name, grid=(t // tm,),
        in_specs=[tok, tok, tok, lanes, pl.BlockSpec((cpb, HALO, DN_CHUNK), lambda i: (i, 0, 0)), lanes],
        out_specs=(tok, tok, tok, tok, sq, pl.BlockSpec((cpb, HALO, LANES), lambda i: (i, 0, 0)), sq),
        out_shape=(jax.ShapeDtypeStruct((t, DN_WIDTH), BF16), jax.ShapeDtypeStruct((t, DN_WIDTH), F32),
                   jax.ShapeDtypeStruct((t, DN_WIDTH), BF16), jax.ShapeDtypeStruct((t, DN_WIDTH), BF16),
                   jax.ShapeDtypeStruct((n_chunks, DN_HEADS, DN_CHUNK, DN_CHUNK), F32),
                   jax.ShapeDtypeStruct((n_chunks, HALO, LANES), F32),
                   jax.ShapeDtypeStruct((n_chunks, DN_HEADS, DN_CHUNK, DN_CHUNK), F32)),
        compiler_params=_cparams(("parallel",)),
    )(q, k, v, gc, grow, beta)


def _delta_par_bwd(q, k, v, gc, grow, beta, inv, dw, du, dqd, dkd, dqk, degl, *, name):
    t = q.shape[0]
    tm = _tm(t)
    cpb = tm // DN_CHUNK
    n_chunks = t // DN_CHUNK
    group = 2

    def body(q_ref, k_ref, v_ref, gc_ref, gr_ref, b_ref, inv_ref, dw_ref, du_ref, dqd_ref, dkd_ref, dqk_ref, degl_ref,
             dq_ref, dk_ref, dv_ref, dgc_ref, dgr_ref, db_ref):
        def step(m, carry):
            chunks = [m * group + e for e in range(group)]
            probs = [(e, h) for e in range(group) for h in range(DN_HEADS)]
            rows = [_chunk_rows(n) for n in chunks]
            gcb, bb = [gc_ref[r, :] for r in rows], [b_ref[r, :] for r in rows]
            grb, deglb = [gr_ref[n] for n in chunks], [degl_ref[n] for n in chunks]
            for n in chunks:
                dgr_ref[n] = jnp.zeros((HALO, DN_CHUNK), F32)
            invs = [inv_ref[chunks[e], h] for e, h in probs]
            _, vjp = jax.vjp(lambda *a: _prep_fn(*a, invs)[0],
                             [q_ref[rows[e], _head_cols(h)] for e, h in probs], [k_ref[rows[e], _head_cols(h)] for e, h in probs],
                             [v_ref[rows[e], _head_cols(h)] for e, h in probs], [_lane_col(gcb[e], h) for e, h in probs],
                             [grb[e][h:h + 1, :] for e, h in probs], [_lane_col(bb[e], h) for e, h in probs])
            dq, dk, dv, dgc, dgr, db = vjp([(dw_ref[rows[e], _head_cols(h)], du_ref[rows[e], _head_cols(h)],
                                             dqd_ref[rows[e], _head_cols(h)], dkd_ref[rows[e], _head_cols(h)],
                                             dqk_ref[chunks[e], h], deglb[e][h:h + 1, :]) for e, h in probs])
            dgc_acc = [jnp.zeros((DN_CHUNK, LANES), F32) for _ in chunks]
            db_acc = [jnp.zeros((DN_CHUNK, LANES), F32) for _ in chunks]
            for i, (e, h) in enumerate(probs):
                cols = _head_cols(h)
                dq_ref[rows[e], cols] = dq[i]
                dk_ref[rows[e], cols] = dk[i]
                dv_ref[rows[e], cols] = dv[i]
                dgr_ref[chunks[e], h:h + 1, :] = dgr[i]
                dgc_acc[e] = dgc_acc[e] + _col_lane(dgc[i], h)
                db_acc[e] = db_acc[e] + _col_lane(db[i], h)
            for e in range(group):
                dgc_ref[rows[e], :] = dgc_acc[e]
                db_ref[rows[e], :] = db_acc[e]
            return carry

        lax.fori_loop(0, cpb // group, step, 0)

    row = lambda i: (i, 0)
    tok = pl.BlockSpec((tm, DN_WIDTH), row)
    lanes = pl.BlockSpec((tm, LANES), row)
    sq = pl.BlockSpec((cpb, DN_HEADS, DN_CHUNK, DN_CHUNK), lambda i: (i, 0, 0, 0))
    grs = pl.BlockSpec((cpb, HALO, DN_CHUNK), lambda i: (i, 0, 0))
    return pl.pallas_call(
        body, name=name, grid=(t // tm,),
        in_specs=[tok, tok, tok, lanes, grs, lanes, sq, tok, tok, tok, tok, sq, pl.BlockSpec((cpb, HALO, LANES), lambda i: (i, 0, 0))],
        out_specs=(tok, tok, tok, lanes, grs, lanes),
        out_shape=(jax.ShapeDtypeStruct((t, DN_WIDTH), F32),) * 3
        + (jax.ShapeDtypeStruct((t, LANES), F32), jax.ShapeDtypeStruct((n_chunks, HALO, DN_CHUNK), F32),
           jax.ShapeDtypeStruct((t, LANES), F32)),
        compiler_params=_cparams(("parallel",)),
    )(q, k, v, gc, grow, beta, inv, dw, du, dqd, dkd, dqk, degl)


def _seq_specs(n_seq, seq, reverse):
    tm = _tm(seq)
    nb = seq // tm
    cpb = tm // DN_CHUNK
    blk = (lambda b, j: b * nb + nb - 1 - j) if reverse else (lambda b, j: b * nb + j)
    tok = pl.BlockSpec((tm, DN_WIDTH), lambda b, j: (blk(b, j), 0))
    sq = pl.BlockSpec((cpb, DN_HEADS, DN_CHUNK, DN_CHUNK), lambda b, j: (blk(b, j), 0, 0, 0))
    rows8 = pl.BlockSpec((cpb, HALO, LANES), lambda b, j: (blk(b, j), 0, 0))
    state = pl.BlockSpec((cpb, DN_HEADS, DN_HEAD_DIM, DN_HEAD_DIM), lambda b, j: (blk(b, j), 0, 0, 0))
    return nb, cpb, tok, sq, rows8, state


def _delta_seq_fwd(w, u, qd, kd, qk, egl, n_seq, seq, *, name):
    nb, cpb, tok, sq, rows8, state = _seq_specs(n_seq, seq, False)
    t = n_seq * seq

    def body(w_ref, u_ref, qd_ref, kd_ref, qk_ref, egl_ref, o_ref, st_ref, s_s):
        @pl.when(pl.program_id(1) == 0)
        def _():
            s_s[...] = jnp.zeros_like(s_s)

        def step(n, carry):
            rows = _chunk_rows(n)
            heads = range(DN_HEADS)
            eglb = egl_ref[n]
            s = [s_s[h] for h in heads]
            for h in heads:
                st_ref[n, h] = s[h]
            o, s_new = _seq_fn([w_ref[rows, _head_cols(h)] for h in heads], [u_ref[rows, _head_cols(h)] for h in heads],
                               [qd_ref[rows, _head_cols(h)] for h in heads], [kd_ref[rows, _head_cols(h)] for h in heads],
                               [qk_ref[n, h] for h in heads], [eglb[h:h + 1, :] for h in heads], s)
            for h in heads:
                o_ref[rows, _head_cols(h)] = o[h]
                s_s[h] = s_new[h]
            return carry

        lax.fori_loop(0, cpb, step, 0)

    return pl.pallas_call(
        body, name=name, grid=(n_seq, nb),
        in_specs=[tok, tok, tok, tok, sq, rows8],
        out_specs=(tok, state),
        out_shape=(jax.ShapeDtypeStruct((t, DN_WIDTH), F32),
                   jax.ShapeDtypeStruct((t // DN_CHUNK, DN_HEADS, DN_HEAD_DIM, DN_HEAD_DIM), F32)),
        scratch_shapes=[pltpu.VMEM((DN_HEADS, DN_HEAD_DIM, DN_HEAD_DIM), F32)],
        compiler_params=_cparams(("parallel", "arbitrary")),
    )(w, u, qd, kd, qk, egl)


def _delta_seq_bwd(w, u, qd, kd, qk, egl, states, do, n_seq, seq, *, name):
    nb, cpb, tok, sq, rows8, state = _seq_specs(n_seq, seq, True)
    t = n_seq * seq

    def body(w_ref, u_ref, qd_ref, kd_ref, qk_ref, egl_ref, st_ref, do_ref, dw_ref, du_ref, dqd_ref, dkd_ref, dqk_ref,
             degl_ref, ds_s):
        @pl.when(pl.program_id(1) == 0)
        def _():
            ds_s[...] = jnp.zeros_like(ds_s)

        def step(m, carry):
            n = cpb - 1 - m
            rows = _chunk_rows(n)
            eglb = egl_ref[n]
            degl_ref[n] = jnp.zeros((HALO, LANES), F32)
            heads = range(DN_HEADS)
            _, vjp = jax.vjp(_seq_fn, [w_ref[rows, _head_cols(h)].astype(F32) for h in heads],
                             [u_ref[rows, _head_cols(h)] for h in heads],
                             [qd_ref[rows, _head_cols(h)].astype(F32) for h in heads],
                             [kd_ref[rows, _head_cols(h)].astype(F32) for h in heads],
                             [qk_ref[n, h] for h in heads], [eglb[h:h + 1, :] for h in heads], [st_ref[n, h] for h in heads])
            dw, du, dqd, dkd, dqk, degl, ds_in = vjp(([do_ref[rows, _head_cols(h)] for h in heads], [ds_s[h] for h in heads]))
            for h in heads:
                cols = _head_cols(h)
                dw_ref[rows, cols] = dw[h]
                du_ref[rows, cols] = du[h]
                dqd_ref[rows, cols] = dqd[h]
                dkd_ref[rows, cols] = dkd[h]
                dqk_ref[n, h] = dqk[h]
                degl_ref[n, h:h + 1, :] = degl[h]
                ds_s[h] = ds_in[h]
            return carry

        lax.fori_loop(0, cpb, step, 0)

    return pl.pallas_call(
        body, name=name, grid=(n_seq, nb),
        in_specs=[tok, tok, tok, tok, sq, rows8, state, tok],
        out_specs=(tok, tok, tok, tok, sq, rows8),
        out_shape=(jax.ShapeDtypeStruct((t, DN_WIDTH), F32),) * 4
        + (jax.ShapeDtypeStruct((t // DN_CHUNK, DN_HEADS, DN_CHUNK, DN_CHUNK), F32),
           jax.ShapeDtypeStruct((t // DN_CHUNK, HALO, LANES), F32)),
        scratch_shapes=[pltpu.VMEM((DN_HEADS, DN_HEAD_DIM, DN_HEAD_DIM), F32)],
        compiler_params=_cparams(("parallel", "arbitrary")),
    )(w, u, qd, kd, qk, egl, states, do)


def _dn_gate(o, z, dnw):
    return o * lax.rsqrt(jnp.mean(o * o, axis=-1, keepdims=True) + EPS) * dnw * _silu(z)


def _mix_out_fwd(x, sg, o, z, wo_sg, wo_dn, dnw, *, name):
    t = x.shape[0]
    tm = _tm(t)

    def body(x_ref, sg_ref, o_ref, z_ref, wsg_ref, wdn_ref, dnw_ref, y_ref, dn_s):
        for h, (oh, zh) in enumerate(zip(_split_heads(o_ref, 0), _split_heads(z_ref, 0))):
            dn_s[:, h * DN_HEAD_DIM:(h + 1) * DN_HEAD_DIM] = _dn_gate(oh, zh, dnw_ref[...]).astype(BF16)
        y_ref[...] = (x_ref[...] + jnp.dot(sg_ref[...].astype(BF16), wsg_ref[...], preferred_element_type=F32)
                      + jnp.dot(dn_s[...], wdn_ref[...], preferred_element_type=F32))

    row = lambda i: (i, 0)
    const = lambda i: (0, 0)
    half = pl.BlockSpec((tm, DN_WIDTH), row)
    return pl.pallas_call(
        body, name=name, grid=(t // tm,),
        in_specs=[pl.BlockSpec((tm, D_MODEL), row), half, half, half, pl.BlockSpec((SG_WIDTH, D_MODEL), const),
                  pl.BlockSpec((DN_WIDTH, D_MODEL), const), pl.BlockSpec((1, DN_HEAD_DIM), const)],
        out_specs=pl.BlockSpec((tm, D_MODEL), row),
        out_shape=jax.ShapeDtypeStruct((t, D_MODEL), F32),
        scratch_shapes=[pltpu.VMEM((tm, DN_WIDTH), BF16)],
        compiler_params=_cparams(("parallel",)),
    )(x, sg, o, z, wo_sg, wo_dn, dnw)


def _mix_out_bwd(dy, sg, o, z, wo_sg, wo_dn, dnw, *, name):
    t = dy.shape[0]
    tm = _tm(t)

    def body(dy_ref, sg_ref, o_ref, z_ref, wsg_ref, wdn_ref, dnw_ref, dsg_ref, do_ref, dz_ref, dwsg_ref, dwdn_ref, ddnw_ref, dn_s):
        i = pl.program_id(0)
        dyb = dy_ref[...].astype(BF16)
        nt = (((1,), (1,)), ((), ()))
        tn = (((0,), (0,)), ((), ()))
        dsg_ref[...] = lax.dot_general(dyb, wsg_ref[...], nt, preferred_element_type=F32)
        ddn = lax.dot_general(dyb, wdn_ref[...], nt, preferred_element_type=F32)
        ddnw = None
        for h, (oh, zh) in enumerate(zip(_split_heads(o_ref, 0), _split_heads(z_ref, 0))):
            cols = slice(h * DN_HEAD_DIM, (h + 1) * DN_HEAD_DIM)
            out, vjp = jax.vjp(_dn_gate, oh, zh, dnw_ref[...])
            dn_s[:, cols] = out.astype(BF16)
            doh, dzh, dw = vjp(ddn[:, cols])
            do_ref[:, cols] = doh
            dz_ref[:, cols] = dzh
            ddnw = dw if ddnw is None else ddnw + dw
        _acc_out(ddnw_ref, i == 0, ddnw)
        _acc_out(dwsg_ref, i == 0, lax.dot_general(sg_ref[...].astype(BF16), dyb, tn, preferred_element_type=F32))
        _acc_out(dwdn_ref, i == 0, lax.dot_general(dn_s[...], dyb, tn, preferred_element_type=F32))

    row = lambda i: (i, 0)
    const = lambda i: (0, 0)
    half = pl.BlockSpec((tm, DN_WIDTH), row)
    wspec = pl.BlockSpec((DN_WIDTH, D_MODEL), const)
    return pl.pallas_call(
        body, name=name, grid=(t // tm,),
        in_specs=[pl.BlockSpec((tm, D_MODEL), row), half, half, half, wspec, wspec, pl.BlockSpec((1, DN_HEAD_DIM), const)],
        out_specs=(half, half, half, wspec, wspec, pl.BlockSpec((1, DN_HEAD_DIM), const)),
        out_shape=(jax.ShapeDtypeStruct((t, DN_WIDTH), F32),) * 3 + (jax.ShapeDtypeStruct((DN_WIDTH, D_MODEL), F32),) * 2
        + (jax.ShapeDtypeStruct((1, DN_HEAD_DIM), F32),),
        scratch_shapes=[pltpu.VMEM((tm, DN_WIDTH), BF16)],
        compiler_params=_cparams(("arbitrary",)),
    )(dy, sg, o, z, wo_sg, wo_dn, dnw)


_MESH = pl.DeviceIdType.MESH
_HBM = pl.BlockSpec(memory_space=pl.ANY)


def _mesh_pos():
    x, y, c = lax.axis_index("x"), lax.axis_index("y"), lax.axis_index("c")
    return x, y, c, [(1 - x, y), (x, 1 - y), (1 - x, 1 - y)]


def _gather2(arrs, *, name):
    n = len(arrs)
    slots = N_DEV - 1

    def body(*refs):
        in_refs, out_refs = refs[:n], refs[n:2 * n]
        send_sems, recv_sems, local_sems = refs[2 * n:]
        x, y, c, chips = _mesh_pos()
        me, sibling = (x, y, c), (x, y, 1 - c)

        def copy(k, slot, block, to, src=None):
            dst = out_refs[k].at[4 * block[0] + 2 * block[1] + block[2]]
            return pltpu.make_async_remote_copy(src_ref=dst if src is None else src, dst_ref=dst,
                                                send_sem=send_sems.at[k * slots + slot], recv_sem=recv_sems.at[k * slots + slot],
                                                device_id=to, device_id_type=_MESH)

        local = [pltpu.make_async_copy(in_refs[k], out_refs[k].at[4 * x + 2 * y + c], local_sems.at[k]) for k in range(n)]
        sent = []
        for k in range(n):
            sent.append(copy(k, 0, me, sibling, src=in_refs[k]))
            sent += [copy(k, 1 + j, me, (*chip, c), src=in_refs[k]) for j, chip in enumerate(chips)]
        for cp in local + sent:
            cp.start()
        for j, chip in enumerate(chips):
            for k in range(n):
                copy(k, 1 + j, (*chip, c), me).wait_recv()
                passed = copy(k, 4 + j, (*chip, c), sibling)
                passed.start()
                sent.append(passed)
        for k in range(n):
            copy(k, 0, sibling, me).wait_recv()
            for j, chip in enumerate(chips):
                copy(k, 4 + j, (*chip, 1 - c), me).wait_recv()
        for cp in sent:
            cp.wait_send()
        for cp in local:
            cp.wait()

    return pl.pallas_call(
        body, name=name, in_specs=[_HBM] * n, out_specs=(_HBM,) * n,
        out_shape=tuple(jax.ShapeDtypeStruct((N_DEV,) + a.shape, a.dtype) for a in arrs),
        scratch_shapes=[pltpu.SemaphoreType.DMA((n * slots,)), pltpu.SemaphoreType.DMA((n * slots,)),
                        pltpu.SemaphoreType.DMA((n,))],
    )(*arrs)


_SEM = pl.BlockSpec(memory_space=pltpu.SEMAPHORE)
_EFFECT = pltpu.SideEffectType.DATAFLOW_SIDE_EFFECTING


def _direct_copies(src_refs, land_refs, send_sems, recv_sems, gather):
    x, y, c, _ = _mesh_pos()
    me = 4 * x + 2 * y + c
    n_peer = N_DEV - 1
    copies = []
    for r in range(1, N_DEV):
        px = 1 - x if r & 4 else x
        py = 1 - y if r & 2 else y
        pc = 1 - c if r & 1 else c
        for k, (src, land) in enumerate(zip(src_refs, land_refs)):
            copies.append(pltpu.make_async_remote_copy(
                src_ref=src if gather else src.at[4 * px + 2 * py + pc], dst_ref=land.at[me],
                send_sem=send_sems.at[k * n_peer + r - 1], recv_sem=recv_sems.at[k * n_peer + r - 1],
                device_id=(px, py, pc), device_id_type=_MESH))
    return copies


def _send_start(arrs, gather, after=None, *, name):
    n = len(arrs)
    lands = [lax.empty(((N_DEV,) + a.shape) if gather else a.shape, a.dtype) for a in arrs]
    n_in = 2 * n + (0 if after is None else 1)

    def body(*refs):
        src_refs, land_refs, send_sems, recv_sems, token = refs[:n], refs[n:2 * n], refs[n_in], refs[n_in + 1], refs[-1]
        for cp in _direct_copies(src_refs, land_refs, send_sems, recv_sems, gather):
            cp.start()
        token[...] = jnp.zeros_like(token)

    n_sem = n * (N_DEV - 1)
    bufs = list(arrs) + lands
    out = pl.pallas_call(
        body, name=name,
        out_shape=(pltpu.SemaphoreType.DMA((n_sem,)), pltpu.SemaphoreType.DMA((n_sem,)))
        + tuple(pltpu.HBM(b.shape, b.dtype) for b in bufs) + (jax.ShapeDtypeStruct((HALO, LANES), F32),),
        in_specs=[_HBM] * n_in, out_specs=(_SEM, _SEM) + (_HBM,) * (2 * n) + (pl.BlockSpec(memory_space=pltpu.VMEM),),
        input_output_aliases={i: 2 + i for i in range(2 * n)},
        compiler_params=pltpu.CompilerParams(has_side_effects=_EFFECT),
    )(*[pltpu.with_memory_space_constraint(b, pltpu.HBM) for b in bufs], *([] if after is None else [after]))
    return (out[0], out[1], list(out[2:2 + n]), list(out[2 + n:2 + 2 * n])), out[-1]


def _send_wait(started, gather, after, *, name):
    send_sems, recv_sems, srcs, lands = started
    n = len(srcs)

    def body(*refs):
        src_refs, land_refs, send_ref, recv_ref = refs[:n], refs[n:2 * n], refs[2 * n], refs[2 * n + 1]
        for cp in _direct_copies(src_refs, land_refs, send_ref, recv_ref, gather):
            cp.wait_send()
            cp.wait_recv()

    bufs = srcs + lands
    out = pl.pallas_call(
        body, name=name, out_shape=tuple(pltpu.HBM(b.shape, b.dtype) for b in bufs),
        in_specs=[_HBM] * (2 * n) + [_SEM, _SEM, _HBM], out_specs=(_HBM,) * (2 * n),
        input_output_aliases={i: i for i in range(2 * n)},
        compiler_params=pltpu.CompilerParams(has_side_effects=_EFFECT),
    )(*bufs, send_sems, recv_sems, after)
    return list(out[n:])


def _row_block(rows, limit=256):
    best = rows
    for cand in range(8, limit + 1, 8):
        if rows % cand == 0:
            best = cand
    return best if rows > limit else rows


def _adam(gp, w, m, v, *, name):
    p, rows, cols = gp.shape
    rb = _row_block(rows)

    def body(gp_ref, w_ref, m_ref, v_ref, g_ref, d_ref, m2_ref, v2_ref):
        g = gp_ref[0].astype(F32)
        for s in range(1, p):
            g = g + gp_ref[s].astype(F32)
        m2 = ADAM_B1 * m_ref[...] + (1.0 - ADAM_B1) * g
        v2 = ADAM_B2 * v_ref[...] + (1.0 - ADAM_B2) * (g * g)
        m_hat = m2 / (1.0 - ADAM_B1 ** ADAM_STEP)
        v_hat = v2 / (1.0 - ADAM_B2 ** ADAM_STEP)
        g_ref[...] = g
        d_ref[...] = -ADAM_LR * (m_hat / (jnp.sqrt(v_hat) + ADAM_EPS) + ADAM_WD * w_ref[...])
        m2_ref[...] = m2
        v2_ref[...] = v2

    blk = pl.BlockSpec((rb, cols), lambda i: (i, 0))
    return pl.pallas_call(
        body, name=name, grid=(rows // rb,),
        in_specs=[pl.BlockSpec((p, rb, cols), lambda i: (0, i, 0)), blk, blk, blk],
        out_specs=(blk,) * 4, out_shape=(jax.ShapeDtypeStruct((rows, cols), F32),) * 4,
        compiler_params=_cparams(("parallel",)),
    )(gp, w, m, v)


def _cols_full(g):
    return jnp.transpose(g, (1, 0, 2)).reshape(g.shape[1], N_DEV * g.shape[2])


def _pad_lanes(a, width=LANES):
    return jnp.pad(a, ((0, 0), (0, width - a.shape[1])))


def _chunk_rows_of(a):
    by_chunk = jnp.transpose(a[:, :DN_HEADS].reshape(-1, DN_CHUNK, DN_HEADS), (0, 2, 1))
    return jnp.pad(by_chunk, ((0, 0), (0, HALO - DN_HEADS), (0, 0)))


_SMALL = (("ffn1_norm", D_MODEL), ("mix_norm", D_MODEL), ("ffn2_norm", D_MODEL), ("final_norm", D_MODEL), ("a_log", DN_HEADS),
          ("dt_bias", DN_HEADS), ("dn_norm", DN_HEAD_DIM), ("sg_ln_g", SG_WIDTH), ("sg_ln_b", SG_WIDTH),
          ("sg_w", SG_GROUPS * SG_CHUNK * SG_CHUNK), ("sg_b", SG_GROUPS * SG_CHUNK), ("conv_w", CONV_K * 3 * DN_WIDTH))
_SMALL_ROWS = 1128
_SMALL_SHAPES = {"ffn1_norm": (1, D_MODEL), "mix_norm": (1, D_MODEL), "ffn2_norm": (1, D_MODEL), "final_norm": (D_MODEL,),
                 "a_log": (1, DN_HEADS), "dt_bias": (1, DN_HEADS), "dn_norm": (1, DN_HEAD_DIM), "sg_ln_g": (1, SG_WIDTH),
                 "sg_ln_b": (1, SG_WIDTH), "sg_w": (1, SG_GROUPS, SG_CHUNK, SG_CHUNK), "sg_b": (1, SG_GROUPS, SG_CHUNK)}


def _pack_small(d):
    flat = jnp.concatenate([d[name].reshape(-1) for name, _ in _SMALL])
    return jnp.pad(flat, (0, _SMALL_ROWS * LANES - flat.shape[0])).reshape(_SMALL_ROWS, LANES)


def _unpack_small(a):
    flat, out, at = a.reshape(-1), {}, 0
    for name, size in _SMALL:
        out[name] = flat[at:at + size]
        at += size
    return out


def kernel(x, ffn1_norm, ffn1_w_gate, ffn1_w_up, ffn1_w_down, mix_norm, w_in, conv_w, a_log, dt_bias, dn_norm, sg_ln_g, sg_ln_b, sg_w, sg_b, w_out, ffn2_norm, ffn2_w_gate, ffn2_w_up, ffn2_w_down, final_norm, loss_target, m_ffn1_norm, m_ffn1_w_gate, m_ffn1_w_up, m_ffn1_w_down, m_mix_norm, m_w_in, m_conv_w, m_a_log, m_dt_bias, m_dn_norm, m_sg_ln_g, m_sg_ln_b, m_sg_w, m_sg_b, m_w_out, m_ffn2_norm, m_ffn2_w_gate, m_ffn2_w_up, m_ffn2_w_down, m_final_norm, v_ffn1_norm, v_ffn1_w_gate, v_ffn1_w_up, v_ffn1_w_down, v_mix_norm, v_w_in, v_conv_w, v_a_log, v_dt_bias, v_dn_norm, v_sg_ln_g, v_sg_ln_b, v_sg_w, v_sg_b, v_w_out, v_ffn2_norm, v_ffn2_w_gate, v_ffn2_w_up, v_ffn2_w_down, v_final_norm):
    weights = dict(ffn1_norm=ffn1_norm, ffn1_w_gate=ffn1_w_gate, ffn1_w_up=ffn1_w_up, ffn1_w_down=ffn1_w_down, mix_norm=mix_norm, w_in=w_in, conv_w=conv_w, a_log=a_log, dt_bias=dt_bias, dn_norm=dn_norm, sg_ln_g=sg_ln_g, sg_ln_b=sg_ln_b, sg_w=sg_w, sg_b=sg_b, w_out=w_out, ffn2_norm=ffn2_norm, ffn2_w_gate=ffn2_w_gate, ffn2_w_up=ffn2_w_up, ffn2_w_down=ffn2_w_down, final_norm=final_norm)
    mom_m = dict(ffn1_norm=m_ffn1_norm, ffn1_w_gate=m_ffn1_w_gate, ffn1_w_up=m_ffn1_w_up, ffn1_w_down=m_ffn1_w_down, mix_norm=m_mix_norm, w_in=m_w_in, conv_w=m_conv_w, a_log=m_a_log, dt_bias=m_dt_bias, dn_norm=m_dn_norm, sg_ln_g=m_sg_ln_g, sg_ln_b=m_sg_ln_b, sg_w=m_sg_w, sg_b=m_sg_b, w_out=m_w_out, ffn2_norm=m_ffn2_norm, ffn2_w_gate=m_ffn2_w_gate, ffn2_w_up=m_ffn2_w_up, ffn2_w_down=m_ffn2_w_down, final_norm=m_final_norm)
    mom_v = dict(ffn1_norm=v_ffn1_norm, ffn1_w_gate=v_ffn1_w_gate, ffn1_w_up=v_ffn1_w_up, ffn1_w_down=v_ffn1_w_down, mix_norm=v_mix_norm, w_in=v_w_in, conv_w=v_conv_w, a_log=v_a_log, dt_bias=v_dt_bias, dn_norm=v_dn_norm, sg_ln_g=v_sg_ln_g, sg_ln_b=v_sg_ln_b, sg_w=v_sg_w, sg_b=v_sg_b, w_out=v_w_out, ffn2_norm=v_ffn2_norm, ffn2_w_gate=v_ffn2_w_gate, ffn2_w_up=v_ffn2_w_up, ffn2_w_down=v_ffn2_w_down, final_norm=v_final_norm)
    order = list(weights)
    big = ("ffn1_w_gate", "ffn1_w_up", "ffn1_w_down", "w_in", "w_out", "ffn2_w_gate", "ffn2_w_up", "ffn2_w_down")
    col_sharded = ("ffn1_w_gate", "ffn1_w_up", "w_in", "ffn2_w_gate", "ffn2_w_up")

    n_seq, seq, _ = x.shape
    t = n_seq * seq
    me = 4 * lax.axis_index("x") + 2 * lax.axis_index("y") + lax.axis_index("c")
    x0 = x.reshape(t, D_MODEL)
    tgt = loss_target.reshape(t, D_MODEL)

    def fill_own(land, own_block):
        return lax.dynamic_update_index_in_dim(land, own_block, me, 0)

    def rows_view(n, a):
        return jnp.transpose(a) if n in col_sharded else a

    def as_full(n, g):
        return g.reshape(-1, g.shape[-1])

    shards = {n: rows_view(n, weights[n][0]).astype(BF16) for n in big}
    ffn1_names, mix_names, ffn2_names = big[:3], big[3:5], big[5:]
    full = {n: as_full(n, g) for n, g in zip(ffn1_names, _gather2([shards[n] for n in ffn1_names], name="gather_ffn1"))}
    mix_srcs = [shards[n] for n in mix_names] + [conv_w[0]]
    mix_started, mix_token = _send_start(mix_srcs, True, full[ffn1_names[2]], name="gather_mix_start")
    ffn2_started, ffn2_token = _send_start([shards[n] for n in ffn2_names], True, mix_token, name="gather_ffn2_start")
    ffn1_norm_fwd = ffn1_norm + ffn2_token[:1, :1]
    alog, dtb = _pad_lanes(a_log), _pad_lanes(dt_bias)
    sgbt = _pad_lanes(sg_b[0].T)
    fnw = final_norm.reshape(1, D_MODEL)

    x1, h1, g1, u1 = _ffn_fwd(x0, ffn1_norm_fwd, full["ffn1_w_gate"], full["ffn1_w_up"], full["ffn1_w_down"], name="ffn1_fwd")
    mix_lands = [fill_own(land, src) for land, src in zip(_send_wait(mix_started, True, x1, name="gather_mix_wait"), mix_srcs)]
    full.update({n: as_full(n, g) for n, g in zip(mix_names, mix_lands)})
    conv_full = _cols_full(mix_lands[-1])
    w_in_t = full["w_in"]
    offs = (0, SG_WIDTH, 2 * SG_WIDTH, 2 * SG_WIDTH + 3 * DN_WIDTH, 2 * SG_WIDTH + 4 * DN_WIDTH)
    n_proj = offs[-1]

    def pad_rows(a):
        return jnp.pad(a, ((0, LANES - a.shape[0]), (0, 0)))

    ws = [w_in_t[offs[0]:offs[1]], w_in_t[offs[1]:offs[2]], w_in_t[offs[2]:offs[3]], w_in_t[offs[3]:offs[4]],
          pad_rows(w_in_t[n_proj:n_proj + DN_HEADS]), pad_rows(w_in_t[n_proj + DN_HEADS:n_proj + 2 * DN_HEADS])]
    wo_sg, wo_dn = full["w_out"][:SG_WIDTH], full["w_out"][SG_WIDTH:]
    u, v, qkv, z, bpre, apre = _mix_in_fwd(x1, mix_norm, ws, name="mix_in_fwd")
    sg_out = _sg_fwd(u, v, sg_ln_g, sg_ln_b, sg_w[0], sgbt, name="sg_fwd")
    q, k, vv, beta, gc = _dn_prep_fwd(qkv, bpre, apre, conv_full, alog, dtb, seq, name="dn_prep_fwd")
    grow = _chunk_rows_of(gc)
    wy_w, wy_u, q_dec, k_dec, qk, egl, inv = _delta_prep(q, k, vv, gc, grow, beta, name="delta_prep")
    o, states = _delta_seq_fwd(wy_w, wy_u, q_dec, k_dec, qk, egl, n_seq, seq, name="delta_seq_fwd")
    x2 = _mix_out_fwd(x1, sg_out, o, z, wo_sg, wo_dn, dn_norm, name="mix_out_fwd")
    ffn2_lands = _send_wait(ffn2_started, True, x2, name="gather_ffn2_wait")
    full.update({n: as_full(n, fill_own(land, shards[n])) for n, land in zip(ffn2_names, ffn2_lands)})
    dx3, loss_part, d_fn, h2, g2, u2 = _ffn_fwd(x2, ffn2_norm, full["ffn2_w_gate"], full["ffn2_w_up"], full["ffn2_w_down"],
                                                tgt, fnw, name="ffn2_fwd_loss")
    loss = lax.psum(loss_part[0, 0], ("x", "y", "c"))

    dx2, d_n2, d_g2, d_u2, d_d2 = _ffn_bwd(x2, ffn2_norm, h2, g2, u2, full["ffn2_w_gate"], full["ffn2_w_up"],
                                           full["ffn2_w_down"], dx3, name="ffn2_bwd")
    def by_owner(d_rows):
        return d_rows.reshape(N_DEV, -1, D_MODEL)

    ffn2_pieces = [by_owner(d_g2), by_owner(d_u2), by_owner(d_d2)]
    ffn2_sent, sent_token = _send_start(ffn2_pieces, False, name="grads_ffn2_start")
    dsg, do, dz, d_wo_sg, d_wo_dn, d_dnw = _mix_out_bwd(dx2, sg_out, o, z, wo_sg, wo_dn, dn_norm + sent_token[:1, :1],
                                                        name="mix_out_bwd")
    d_seq = _delta_seq_bwd(wy_w, wy_u, q_dec, k_dec, qk, egl, states, do, n_seq, seq, name="delta_seq_bwd")
    dq, dk, dv, dgc_a, dgrow, dbeta = _delta_par_bwd(q, k, vv, gc, grow, beta, inv, *d_seq, name="delta_par_bwd")
    dgc_b = _pad_lanes(jnp.transpose(dgrow[:, :DN_HEADS, :], (0, 2, 1)).reshape(t, DN_HEADS))
    dy_conv, dbpre, dapre, d_alog, d_dtb = _dn_prep_bwd(qkv, bpre, apre, conv_full, alog, dtb, dq, dk, dv, dbeta, dgc_a, dgc_b,
                                                        seq, name="dn_prep_bwd")
    dqkv, d_conv = _conv_bwd(qkv, dy_conv, conv_full, seq, name="conv_bwd")
    du, dvv, d_lng, d_lnb, d_wc, d_sgbt = _sg_bwd(u, v, sg_ln_g, sg_ln_b, sg_w[0], sgbt, dsg, name="sg_bwd")
    dx1, d_mixn, d_wp = _mix_in_bwd(x1, mix_norm, ws, dx2, (du, dvv, dqkv, dz, dbpre, dapre), name="mix_in_bwd")
    d_w_in_t = jnp.concatenate([d_wp[:n_proj], d_wp[_PROJ_OFFSETS[4]:_PROJ_OFFSETS[4] + DN_HEADS],
                                d_wp[_PROJ_OFFSETS[5]:_PROJ_OFFSETS[5] + DN_HEADS]], axis=0)
    d_w_out = jnp.concatenate([d_wo_sg, d_wo_dn], axis=0)
    mix_pieces = [by_owner(d_w_in_t), by_owner(d_w_out).astype(BF16)]
    mix_sent, sent_token = _send_start(mix_pieces, False, name="grads_mix_start")
    grad_x, d_n1, dg1, du1, a1, dyh1 = _ffn_bwd_x(x0, ffn1_norm + sent_token[:1, :1], g1, u1, full["ffn1_w_gate"],
                                                  full["ffn1_w_up"], full["ffn1_w_down"], dx1, name="ffn1_bwd_x")
    small_grads = dict(ffn1_norm=d_n1, mix_norm=d_mixn, ffn2_norm=d_n2, final_norm=d_fn, a_log=d_alog[:, :DN_HEADS],
                       dt_bias=d_dtb[:, :DN_HEADS], dn_norm=d_dnw, sg_ln_g=d_lng, sg_ln_b=d_lnb, sg_w=d_wc,
                       sg_b=d_sgbt[:, :SG_GROUPS].T, conv_w=d_conv[:CONV_K])
    small_src = _pack_small(small_grads)
    small_sent, small_token = _send_start([small_src], True, name="small_grads_start")
    late, tokens = [], []

    def send_early(k, grad):
        piece = by_owner(grad)
        sent, token = _send_start([piece], False, name="grads_" + ffn1_names[k] + "_start")
        late.append(((ffn1_names[k],), sent, [piece]))
        tokens.append(token)
        return token

    _ffn_wgrads(h1, dg1, du1, a1, dyh1, send_early, small_token, name="ffn1_bwd")

    res = {}
    after = tokens[-1]
    for names, sent, pieces in [(ffn2_names, ffn2_sent, ffn2_pieces), (mix_names, mix_sent, mix_pieces)] + late:
        lands = _send_wait(sent, False, after, name="grads_" + names[0] + "_wait")
        for n, land, p in zip(names, lands, pieces):
            got = fill_own(land, lax.dynamic_index_in_dim(p, me, 0, keepdims=False))
            upd = _adam(got, *[rows_view(n, src[n][0]) for src in (weights, mom_m, mom_v)], name="adam_" + n)
            res[n] = [rows_view(n, a) for a in upd]
            after = upd[0]

    (small_land,) = _send_wait(small_sent, True, after, name="small_grads_wait")
    small_parts = fill_own(small_land, small_src)
    zeros_conv = jnp.zeros((CONV_K * 3 * DN_WIDTH,), F32)
    packed = [_pack_small({**{n: src[n] for n, _ in _SMALL if n != "conv_w"}, "conv_w": zeros_conv})
              for src in (weights, mom_m, mom_v)]
    small_res = [_unpack_small(a) for a in _adam(small_parts, *packed, name="adam_small")]
    conv_grad = lax.dynamic_slice_in_dim(small_res[0]["conv_w"].reshape(CONV_K, 3 * DN_WIDTH), me * (3 * DN_WIDTH // N_DEV),
                                         3 * DN_WIDTH // N_DEV, axis=1)
    res["conv_w"] = _adam(conv_grad[None], conv_w[0], m_conv_w[0], v_conv_w[0], name="adam_conv_w")

    outs = [[], [], [], []]
    for n in order:
        for kind in range(4):
            if n in res:
                outs[kind].append(res[n][kind][None])
            else:
                outs[kind].append(small_res[kind][n].reshape(_SMALL_SHAPES[n]))
    return (loss, grad_x.reshape(x.shape), *outs[0], *outs[1], *outs[2], *outs[3])
```

```python
import functools

import jax
import jax.numpy as jnp
from jax import lax
from jax.experimental import pallas as pl
from jax.experimental.pallas import tpu as pltpu

F32 = jnp.float32
BF16 = jnp.bfloat16

D_MODEL = 1024
D_FF = 2816
SG_WIDTH = 512
SG_GROUPS = 8
SG_GROUP_DIM = 64
SG_CHUNK = 128
DN_WIDTH = 512
DN_HEAD_DIM = 128
DN_HEADS = 4
DN_CHUNK = 64
CONV_K = 4
EPS = 1e-6
N_DEV = 8
LANES = 128
HALO = 8
MXU_COLS = 256

ADAM_LR = 0.001
ADAM_B1 = 0.9
ADAM_B2 = 0.999
ADAM_EPS = 1e-08
ADAM_WD = 0.01
ADAM_STEP = 10

VMEM_LIMIT = 60 * 1024 * 1024
WGRAD_VMEM_BUDGET = 52 * 1024 * 1024
TOKEN_BLOCK = 512
FF_BLOCK_FWD = 1408

_HI = lax.Precision.HIGHEST


def _cparams(sem):
    return pltpu.CompilerParams(dimension_semantics=sem, vmem_limit_bytes=VMEM_LIMIT)


def _tm(t, pref=TOKEN_BLOCK):
    return min(pref, t)


def _dg(a, b, ca, cb, precision):
    if precision is not None:
        return lax.dot_general(a, b, (((ca,), (cb,)), ((), ())), precision=precision, preferred_element_type=F32)
    return lax.dot_general(a.astype(BF16), b.astype(BF16), (((ca,), (cb,)), ((), ())), preferred_element_type=F32)


def _make_mm(exact):
    @jax.custom_vjp
    def mm(a, b):
        return _dg(a, b, 1, 0, exact)

    @jax.custom_vjp
    def mm_nt(a, b):
        return _dg(a, b, 1, 1, exact)

    @jax.custom_vjp
    def mm_tn(a, b):
        return _dg(a, b, 0, 0, exact)

    mm.defvjp(lambda a, b: (mm(a, b), (a, b)), lambda r, g: (mm_nt(g, r[1]), mm_tn(r[0], g)))
    mm_nt.defvjp(lambda a, b: (mm_nt(a, b), (a, b)), lambda r, g: (mm(g, r[1]), mm_tn(g, r[0])))
    mm_tn.defvjp(lambda a, b: (mm_tn(a, b), (a, b)), lambda r, g: (mm_nt(r[1], g), mm(r[0], g)))
    return mm, mm_nt, mm_tn


mm, mm_nt, mm_tn = _make_mm(None)
mmx, mmx_nt, mmx_tn = _make_mm(_HI)
mmh, mmh_nt, mmh_tn = _make_mm(lax.Precision.HIGH)


def _sigmoid(x):
    return 1.0 / (1.0 + jnp.exp(-x))


def _silu(x):
    return x * _sigmoid(x)


def _softplus(x):
    neg_abs = jnp.where(x > 0, -x, x)
    return jnp.where(x > 0, x, 0.0) + jnp.log(1.0 + jnp.exp(neg_abs))


def _gelu(x):
    return 0.5 * x * (1.0 + jnp.tanh(0.7978845608028654 * (x + 0.044715 * (x * x * x))))


def _rms_fwd(x, g):
    r = lax.rsqrt(jnp.mean(x * x, axis=-1, keepdims=True) + EPS)
    xh = x * r
    return xh * g, xh, r


def _rms_bwd(dh, xh, r, g):
    dxh = dh * g
    dx = r * (dxh - xh * jnp.mean(dxh * xh, axis=-1, keepdims=True))
    return dx, jnp.sum(dh * xh, axis=0, keepdims=True)


def _acc_out(ref, first, val):
    @pl.when(first)
    def _():
        ref[...] = val

    @pl.when(jnp.logical_not(first))
    def _():
        ref[...] += val


def _ffn_fwd(x, nw, wg, wu, wd, tgt=None, fnw=None, *, name):
    t = x.shape[0]
    tm, fb = _tm(t), FF_BLOCK_FWD
    n_t, n_f = t // tm, D_FF // fb
    with_loss = tgt is not None

    def body(*refs):
        if with_loss:
            (x_ref, nw_ref, wg_ref, wu_ref, wd_ref, tgt_ref, fnw_ref, dy_ref, loss_ref, dfn_ref, h_ref, g_ref, u_ref,
             acc_s) = refs
        else:
            x_ref, nw_ref, wg_ref, wu_ref, wd_ref, y_ref, h_ref, g_ref, u_ref, acc_s = refs
        i, j = pl.program_id(0), pl.program_id(1)

        @pl.when(j == 0)
        def _():
            h, _, _ = _rms_fwd(x_ref[...], nw_ref[...])
            h_ref[...] = h.astype(BF16)
            acc_s[...] = jnp.zeros_like(acc_s)

        h = h_ref[...]
        nt = (((1,), (1,)), ((), ()))
        g = lax.dot_general(h, wg_ref[...], nt, preferred_element_type=F32)
        u = lax.dot_general(h, wu_ref[...], nt, preferred_element_type=F32)
        g_ref[...] = g.astype(BF16)
        u_ref[...] = u.astype(BF16)
        a = _silu(g) * u
        acc_s[...] += jnp.dot(a.astype(BF16), wd_ref[...], preferred_element_type=F32)

        @pl.when(j == n_f - 1)
        def _():
            y = x_ref[...] + 0.5 * acc_s[...]
            if not with_loss:
                y_ref[...] = y
            else:
                gf = fnw_ref[...]
                out, xh, r = _rms_fwd(y, gf)
                err = out - tgt_ref[...]
                part = 0.5 * jnp.sum(jnp.mean(err * err, axis=-1, keepdims=True), axis=0, keepdims=True)
                d_out = err * (1.0 / D_MODEL)
                dy, dgf = _rms_bwd(d_out, xh, r, gf)
                dy_ref[...] = dy
                _acc_out(loss_ref, i == 0, jnp.broadcast_to(part, loss_ref.shape))
                _acc_out(dfn_ref, i == 0, dgf)

    row = lambda i, j: (i, 0)
    const = lambda i, j: (0, 0)
    in_specs = [
        pl.BlockSpec((tm, D_MODEL), row),
        pl.BlockSpec((1, D_MODEL), const),
        pl.BlockSpec((fb, D_MODEL), lambda i, j: (j, 0)),
        pl.BlockSpec((fb, D_MODEL), lambda i, j: (j, 0)),
        pl.BlockSpec((fb, D_MODEL), lambda i, j: (j, 0)),
    ]
    args = [x, nw, wg, wu, wd]
    saved_shape = (jax.ShapeDtypeStruct((t, D_MODEL), BF16), jax.ShapeDtypeStruct((t, D_FF), BF16),
                   jax.ShapeDtypeStruct((t, D_FF), BF16))
    saved_specs = (pl.BlockSpec((tm, D_MODEL), row), pl.BlockSpec((tm, fb), lambda i, j: (i, j)),
                   pl.BlockSpec((tm, fb), lambda i, j: (i, j)))
    if with_loss:
        in_specs += [pl.BlockSpec((tm, D_MODEL), row), pl.BlockSpec((1, D_MODEL), const)]
        args += [tgt, fnw]
        out_shape = (jax.ShapeDtypeStruct((t, D_MODEL), F32), jax.ShapeDtypeStruct((8, LANES), F32),
                     jax.ShapeDtypeStruct((1, D_MODEL), F32)) + saved_shape
        out_specs = (pl.BlockSpec((tm, D_MODEL), row), pl.BlockSpec((8, LANES), const),
                     pl.BlockSpec((1, D_MODEL), const)) + saved_specs
        sem = ("arbitrary", "arbitrary")
    else:
        out_shape = (jax.ShapeDtypeStruct((t, D_MODEL), F32),) + saved_shape
        out_specs = (pl.BlockSpec((tm, D_MODEL), row),) + saved_specs
        sem = ("parallel", "arbitrary")
    return pl.pallas_call(
        body, name=name, grid=(n_t, n_f), in_specs=in_specs, out_specs=out_specs, out_shape=out_shape,
        scratch_shapes=[pltpu.VMEM((tm, D_MODEL), F32)],
        compiler_params=_cparams(sem),
    )(*args)


def _ffn_bwd_x(x, nw, g, u, wg, wu, wd, dy, *, name):
    t = x.shape[0]
    tm = _tm(t, 256)

    def body(x_ref, nw_ref, g_ref, u_ref, wg_ref, wu_ref, wd_ref, dy_ref, dx_ref, dnw_ref, dg_ref, du_ref, a_ref, dyh_ref):
        i = pl.program_id(0)
        nt = (((1,), (1,)), ((), ()))
        dy = dy_ref[...]
        dyh = (0.5 * dy).astype(BF16)
        dyh_ref[...] = dyh
        gate, up = g_ref[...].astype(F32), u_ref[...].astype(F32)
        s = _sigmoid(gate)
        gs = gate * s
        da = lax.dot_general(dyh, wd_ref[...], nt, preferred_element_type=F32)
        dg = (da * up * (s + gs * (1.0 - s))).astype(BF16)
        du = (da * gs).astype(BF16)
        dg_ref[...] = dg
        du_ref[...] = du
        a_ref[...] = (gs * up).astype(BF16)
        dh = (jnp.dot(dg, wg_ref[...], preferred_element_type=F32)
              + jnp.dot(du, wu_ref[...], preferred_element_type=F32))
        xv = x_ref[...]
        r = lax.rsqrt(jnp.mean(xv * xv, axis=-1, keepdims=True) + EPS)
        dx, dnw = _rms_bwd(dh, xv * r, r, nw_ref[...])
        dx_ref[...] = dy + dx
        _acc_out(dnw_ref, i == 0, dnw)

    row = lambda i: (i, 0)
    const = lambda i: (0, 0)
    once = pl.Buffered(1)
    wide = pl.BlockSpec((tm, D_FF), row)
    return pl.pallas_call(
        body, name=name, grid=(t // tm,),
        in_specs=[pl.BlockSpec((tm, D_MODEL), row), pl.BlockSpec((1, D_MODEL), const), wide, wide,
                  pl.BlockSpec((D_FF, D_MODEL), const, pipeline_mode=once), pl.BlockSpec((D_FF, D_MODEL), const, pipeline_mode=once),
                  pl.BlockSpec((D_FF, D_MODEL), const, pipeline_mode=once), pl.BlockSpec((tm, D_MODEL), row)],
        out_specs=(pl.BlockSpec((tm, D_MODEL), row), pl.BlockSpec((1, D_MODEL), const), wide, wide, wide,
                   pl.BlockSpec((tm, D_MODEL), row)),
        out_shape=(jax.ShapeDtypeStruct((t, D_MODEL), F32), jax.ShapeDtypeStruct((1, D_MODEL), F32),
                   jax.ShapeDtypeStruct((t, D_FF), BF16), jax.ShapeDtypeStruct((t, D_FF), BF16),
                   jax.ShapeDtypeStruct((t, D_FF), BF16), jax.ShapeDtypeStruct((t, D_MODEL), BF16)),
        compiler_params=_cparams(("arbitrary",)),
    )(x, nw, g, u, wg, wu, wd, dy)


def _wgrad(a, b, bm, bn, after=None, *, name):
    k, m = a.shape
    n = b.shape[1]
    tk = k
    while 2 * 2 * tk * (bm + bn) + (4 + 2 * 2) * bm * bn + 4 * bm * MXU_COLS > WGRAD_VMEM_BUDGET:
        tk //= 2
    n_k = k // tk

    def body(a_ref, b_ref, *rest):
        o_ref, acc_s = rest[-2], rest[-1]
        s = pl.program_id(2)
        for c in range(bn // MXU_COLS):
            cols = slice(c * MXU_COLS, (c + 1) * MXU_COLS)
            part = lax.dot_general(a_ref[...], b_ref[:, cols], (((0,), (0,)), ((), ())), preferred_element_type=F32)
            acc_s[:, cols] = jnp.where(s == 0, 0.0, acc_s[:, cols]) + part

        @pl.when(s == n_k - 1)
        def _():
            o_ref[...] = acc_s[...].astype(BF16)

    return pl.pallas_call(
        body, name=name, grid=(m // bm, n // bn, n_k),
        in_specs=[pl.BlockSpec((tk, bm), lambda i, j, s: (s, i)), pl.BlockSpec((tk, bn), lambda i, j, s: (s, j))]
        + ([] if after is None else [_HBM]),
        out_specs=pl.BlockSpec((bm, bn), lambda i, j, s: (i, j)),
        out_shape=jax.ShapeDtypeStruct((m, n), BF16),
        scratch_shapes=[pltpu.VMEM((bm, bn), F32)],
        compiler_params=_cparams(("parallel", "parallel", "arbitrary")),
    )(a, b, *([] if after is None else [after]))


def _ffn_wgrads(h, dg, du, a, dyh, between=None, after=None, *, name):
    grads = []
    for k, (lhs, rhs, tag) in enumerate(((dg, h, "_wg"), (du, h, "_wu"), (a, dyh, "_wd"))):
        grads.append(_wgrad(lhs, rhs, D_FF // 2, D_MODEL, after, name=name + tag))
        after = None if between is None else between(k, grads[-1])
    return grads


def _ffn_bwd(x, nw, h, g, u, wg, wu, wd, dy, *, name):
    dx, dnw, dg, du, a, dyh = _ffn_bwd_x(x, nw, g, u, wg, wu, wd, dy, name=name + "_x")
    return (dx, dnw, *_ffn_wgrads(h, dg, du, a, dyh, name=name))


_PROJ_WIDTHS = (SG_WIDTH, SG_WIDTH, 3 * DN_WIDTH, DN_WIDTH, LANES, LANES)


def _mix_in_fwd(x, nw, ws, *, name):
    t = x.shape[0]
    tm = _tm(t)

    def body(x_ref, nw_ref, *refs):
        w_refs, o_refs = refs[:6], refs[6:]
        h, _, _ = _rms_fwd(x_ref[...], nw_ref[...])
        h = h.astype(BF16)
        for w_ref, o_ref in zip(w_refs, o_refs):
            o_ref[...] = lax.dot_general(h, w_ref[...], (((1,), (1,)), ((), ())), preferred_element_type=F32)

    row = lambda i: (i, 0)
    const = lambda i: (0, 0)
    return pl.pallas_call(
        body, name=name, grid=(t // tm,),
        in_specs=[pl.BlockSpec((tm, D_MODEL), row), pl.BlockSpec((1, D_MODEL), const)]
        + [pl.BlockSpec((n, D_MODEL), const) for n in _PROJ_WIDTHS],
        out_specs=tuple(pl.BlockSpec((tm, n), row) for n in _PROJ_WIDTHS),
        out_shape=tuple(jax.ShapeDtypeStruct((t, n), F32) for n in _PROJ_WIDTHS),
        compiler_params=_cparams(("parallel",)),
    )(x, nw, *ws)


_PROJ_TOTAL = sum(_PROJ_WIDTHS)
_PROJ_OFFSETS = tuple(sum(_PROJ_WIDTHS[:k]) for k in range(len(_PROJ_WIDTHS)))


def _mix_in_bwd(x, nw, ws, dres, dps, *, name):
    t = x.shape[0]
    tm = _tm(t, 256)

    def body(x_ref, nw_ref, dres_ref, *refs):
        w_refs, dp_refs, dx_ref, dnw_ref, h_ref, dpb_ref = refs[:6], refs[6:12], refs[12], refs[13], refs[14], refs[15]
        i = pl.program_id(0)
        hf, xh, r = _rms_fwd(x_ref[...], nw_ref[...])
        h_ref[...] = hf.astype(BF16)
        dh = jnp.zeros((tm, D_MODEL), F32)
        for w_ref, dp_ref, off, width in zip(w_refs, dp_refs, _PROJ_OFFSETS, _PROJ_WIDTHS):
            dp = dp_ref[...].astype(BF16)
            dpb_ref[:, off:off + width] = dp
            dh = dh + jnp.dot(dp, w_ref[...], preferred_element_type=F32)
        dx, dnw = _rms_bwd(dh, xh, r, nw_ref[...])
        dx_ref[...] = dres_ref[...] + dx
        _acc_out(dnw_ref, i == 0, dnw)

    row = lambda i: (i, 0)
    const = lambda i: (0, 0)
    dx, dnw, h, dpb = pl.pallas_call(
        body, name=name + "_x", grid=(t // tm,),
        in_specs=[pl.BlockSpec((tm, D_MODEL), row), pl.BlockSpec((1, D_MODEL), const), pl.BlockSpec((tm, D_MODEL), row)]
        + [pl.BlockSpec((n, D_MODEL), const) for n in _PROJ_WIDTHS]
        + [pl.BlockSpec((tm, n), row) for n in _PROJ_WIDTHS],
        out_specs=(pl.BlockSpec((tm, D_MODEL), row), pl.BlockSpec((1, D_MODEL), const), pl.BlockSpec((tm, D_MODEL), row),
                   pl.BlockSpec((tm, _PROJ_TOTAL), row)),
        out_shape=(jax.ShapeDtypeStruct((t, D_MODEL), F32), jax.ShapeDtypeStruct((1, D_MODEL), F32),
                   jax.ShapeDtypeStruct((t, D_MODEL), BF16), jax.ShapeDtypeStruct((t, _PROJ_TOTAL), BF16)),
        compiler_params=_cparams(("arbitrary",)),
    )(x, nw, dres, *ws, *dps)
    return dx, dnw, _wgrad(dpb, h, _PROJ_TOTAL // 2, D_MODEL, name=name + "_w")


def _sg_fn(u, v, lng, lnb, wcs, sgbt):
    lane = lax.broadcasted_iota(jnp.int32, (1, SG_WIDTH), 1)
    lane_b = lax.broadcasted_iota(jnp.int32, (1, LANES), 1)
    rr = lax.broadcasted_iota(jnp.int32, (SG_CHUNK, SG_CHUNK), 0)
    cc = lax.broadcasted_iota(jnp.int32, (SG_CHUNK, SG_CHUNK), 1)
    gu, gv = _gelu(u), _gelu(v)
    mu = jnp.mean(gv, axis=-1, keepdims=True)
    cen = gv - mu
    var = jnp.mean(cen * cen, axis=-1, keepdims=True)
    ln = cen * lax.rsqrt(var + EPS) * lng + lnb
    vs = jnp.zeros_like(u)
    for g in range(SG_GROUPS):
        in_group = jnp.logical_and(lane >= g * SG_GROUP_DIM, lane < (g + 1) * SG_GROUP_DIM)
        w_causal = jnp.where(rr >= cc, wcs[g], 0.0)
        bias = jnp.sum(jnp.where(lane_b == g, sgbt, 0.0), axis=1, keepdims=True)
        vs = vs + jnp.where(in_group, mm(w_causal, ln) + bias, 0.0)
    return gu * vs


def _sg_fwd(u, v, lng, lnb, wc, sgbt, *, name):
    t = u.shape[0]
    tm = _tm(t)

    def body(u_ref, v_ref, lng_ref, lnb_ref, wc_ref, sgbt_ref, o_ref):
        wcs = [wc_ref[g] for g in range(SG_GROUPS)]
        for c in range(tm // SG_CHUNK):
            rows = pl.ds(c * SG_CHUNK, SG_CHUNK)
            o_ref[rows, :] = _sg_fn(u_ref[rows, :], v_ref[rows, :], lng_ref[...], lnb_ref[...], wcs, sgbt_ref[...])

    row = lambda i: (i, 0)
    const = lambda i: (0, 0)
    return pl.pallas_call(
        body, name=name, grid=(t // tm,),
        in_specs=[pl.BlockSpec((tm, SG_WIDTH), row), pl.BlockSpec((tm, SG_WIDTH), row),
                  pl.BlockSpec((1, SG_WIDTH), const), pl.BlockSpec((1, SG_WIDTH), const),
                  pl.BlockSpec((SG_GROUPS, SG_CHUNK, SG_CHUNK), lambda i: (0, 0, 0)), pl.BlockSpec((SG_CHUNK, LANES), const)],
        out_specs=pl.BlockSpec((tm, SG_WIDTH), row),
        out_shape=jax.ShapeDtypeStruct((t, SG_WIDTH), F32),
        compiler_params=_cparams(("parallel",)),
    )(u, v, lng, lnb, wc, sgbt)


def _sg_bwd(u, v, lng, lnb, wc, sgbt, dout, *, name):
    t = u.shape[0]
    tm = _tm(t)

    def body(u_ref, v_ref, lng_ref, lnb_ref, wc_ref, sgbt_ref, do_ref, du_ref, dv_ref, dlng_ref, dlnb_ref, dwc_ref, dsgbt_ref):
        i = pl.program_id(0)
        wcs = [wc_ref[g] for g in range(SG_GROUPS)]
        tot = None
        for c in range(tm // SG_CHUNK):
            rows = pl.ds(c * SG_CHUNK, SG_CHUNK)
            _, vjp = jax.vjp(_sg_fn, u_ref[rows, :], v_ref[rows, :], lng_ref[...], lnb_ref[...], wcs, sgbt_ref[...])
            du, dv, dlng, dlnb, dwcs, dsgbt = vjp(do_ref[rows, :])
            du_ref[rows, :] = du
            dv_ref[rows, :] = dv
            part = (dlng, dlnb, dwcs, dsgbt)
            tot = part if tot is None else jax.tree.map(jnp.add, tot, part)
        dlng, dlnb, dwcs, dsgbt = tot
        _acc_out(dlng_ref, i == 0, dlng)
        _acc_out(dlnb_ref, i == 0, dlnb)
        _acc_out(dsgbt_ref, i == 0, dsgbt)
        for g in range(SG_GROUPS):
            @pl.when(i == 0)
            def _(g=g):
                dwc_ref[g] = dwcs[g]

            @pl.when(i > 0)
            def _(g=g):
                dwc_ref[g] += dwcs[g]

    row = lambda i: (i, 0)
    const = lambda i: (0, 0)
    wspec = pl.BlockSpec((SG_GROUPS, SG_CHUNK, SG_CHUNK), lambda i: (0, 0, 0))
    return pl.pallas_call(
        body, name=name, grid=(t // tm,),
        in_specs=[pl.BlockSpec((tm, SG_WIDTH), row), pl.BlockSpec((tm, SG_WIDTH), row),
                  pl.BlockSpec((1, SG_WIDTH), const), pl.BlockSpec((1, SG_WIDTH), const), wspec,
                  pl.BlockSpec((SG_CHUNK, LANES), const), pl.BlockSpec((tm, SG_WIDTH), row)],
        out_specs=(pl.BlockSpec((tm, SG_WIDTH), row), pl.BlockSpec((tm, SG_WIDTH), row),
                   pl.BlockSpec((1, SG_WIDTH), const), pl.BlockSpec((1, SG_WIDTH), const), wspec,
                   pl.BlockSpec((SG_CHUNK, LANES), const)),
        out_shape=(jax.ShapeDtypeStruct((t, SG_WIDTH), F32), jax.ShapeDtypeStruct((t, SG_WIDTH), F32),
                   jax.ShapeDtypeStruct((1, SG_WIDTH), F32), jax.ShapeDtypeStruct((1, SG_WIDTH), F32),
                   jax.ShapeDtypeStruct((SG_GROUPS, SG_CHUNK, SG_CHUNK), F32), jax.ShapeDtypeStruct((SG_CHUNK, LANES), F32)),
        compiler_params=_cparams(("arbitrary",)),
    )(u, v, lng, lnb, wc, sgbt, dout)


def _conv_taps(ext, w, tm):
    y = None
    for j in range(CONV_K):
        s = CONV_K - 1 - j
        shifted = ext if s == 0 else pltpu.roll(ext, s, 0)
        term = w[j:j + 1, :] * shifted[HALO:HALO + tm, :]
        y = term if y is None else y + term
    return y


def _post_conv(yq, yk, yv, bpre, apre, alog, dtb):
    def l2(a):
        return a * lax.rsqrt(jnp.sum(a * a, axis=-1, keepdims=True) + EPS)

    q = [l2(_silu(a)) for a in yq]
    k = [l2(_silu(a)) for a in yk]
    return q, k, _silu(yv), _sigmoid(bpre), -jnp.exp(alog) * _softplus(apre + dtb)


def _chunk_tril(tm):
    rr = lax.broadcasted_iota(jnp.int32, (tm, tm), 0)
    cc = lax.broadcasted_iota(jnp.int32, (tm, tm), 1)
    shift = DN_CHUNK.bit_length() - 1
    same = jnp.right_shift(rr, shift) == jnp.right_shift(cc, shift)
    return jnp.where(jnp.logical_and(same, rr >= cc), 1.0, 0.0).astype(F32)


def _halo_specs(tm, width, n_blocks_seq, n_blocks):
    per = tm // HALO
    prev = pl.BlockSpec((HALO, width), lambda i: (jnp.maximum(i * per - 1, 0), 0))
    nxt = pl.BlockSpec((HALO, width), lambda i: (jnp.minimum((i + 1) * per, n_blocks * per - 1), 0))
    return prev, nxt


def _split_heads(ref, base):
    return [ref[:, base + h * DN_HEAD_DIM: base + (h + 1) * DN_HEAD_DIM] for h in range(DN_HEADS)]


def _dn_prep_fwd(qkv, bpre, apre, conv_w, alog, dtb, seq, *, name):
    t = qkv.shape[0]
    tm = _tm(t)
    bps = seq // tm
    cw = 3 * DN_WIDTH

    def body(x_ref, halo_ref, b_ref, a_ref, w_ref, alog_ref, dtb_ref, q_ref, k_ref, v_ref, beta_ref, gc_ref):
        i = pl.program_id(0)
        keep = jnp.where(i % bps == 0, 0.0, 1.0)
        ext = jnp.concatenate([halo_ref[...] * keep, x_ref[...]], axis=0)
        y = _conv_taps(ext, w_ref[...], tm)
        yq = [y[:, h * DN_HEAD_DIM:(h + 1) * DN_HEAD_DIM] for h in range(DN_HEADS)]
        yk = [y[:, DN_WIDTH + h * DN_HEAD_DIM: DN_WIDTH + (h + 1) * DN_HEAD_DIM] for h in range(DN_HEADS)]
        q, k, v, beta, g = _post_conv(yq, yk, y[:, 2 * DN_WIDTH:], b_ref[...], a_ref[...], alog_ref[...], dtb_ref[...])
        for h in range(DN_HEADS):
            q_ref[:, h * DN_HEAD_DIM:(h + 1) * DN_HEAD_DIM] = q[h]
            k_ref[:, h * DN_HEAD_DIM:(h + 1) * DN_HEAD_DIM] = k[h]
        v_ref[...] = v
        beta_ref[...] = beta
        gc_ref[...] = mmx(_chunk_tril(tm), g)

    row = lambda i: (i, 0)
    const = lambda i: (0, 0)
    prev, _ = _halo_specs(tm, cw, bps, t // tm)
    return pl.pallas_call(
        body, name=name, grid=(t // tm,),
        in_specs=[pl.BlockSpec((tm, cw), row), prev, pl.BlockSpec((tm, LANES), row), pl.BlockSpec((tm, LANES), row),
                  pl.BlockSpec((CONV_K, cw), const), pl.BlockSpec((1, LANES), const), pl.BlockSpec((1, LANES), const)],
        out_specs=tuple(pl.BlockSpec((tm, n), row) for n in (DN_WIDTH, DN_WIDTH, DN_WIDTH, LANES, LANES)),
        out_shape=tuple(jax.ShapeDtypeStruct((t, n), F32) for n in (DN_WIDTH, DN_WIDTH, DN_WIDTH, LANES, LANES)),
        compiler_params=_cparams(("parallel",)),
    )(qkv, qkv, bpre, apre, conv_w, alog, dtb)


def _dn_prep_bwd(qkv, bpre, apre, conv_w, alog, dtb, dq, dk, dv, dbeta, dgc, dgc2, seq, *, name):
    t = qkv.shape[0]
    tm = _tm(t)
    bps = seq // tm
    cw = 3 * DN_WIDTH

    def body(x_ref, halo_ref, b_ref, a_ref, w_ref, alog_ref, dtb_ref, dq_ref, dk_ref, dv_ref, dbeta_ref, dgc_ref, dgc2_ref,
             dy_ref, db_ref, da_ref, dalog_ref, ddtb_ref):
        i = pl.program_id(0)
        keep = jnp.where(i % bps == 0, 0.0, 1.0)
        ext = jnp.concatenate([halo_ref[...] * keep, x_ref[...]], axis=0)
        y = _conv_taps(ext, w_ref[...], tm)
        yq = [y[:, h * DN_HEAD_DIM:(h + 1) * DN_HEAD_DIM] for h in range(DN_HEADS)]
        yk = [y[:, DN_WIDTH + h * DN_HEAD_DIM: DN_WIDTH + (h + 1) * DN_HEAD_DIM] for h in range(DN_HEADS)]
        _, vjp = jax.vjp(_post_conv, yq, yk, y[:, 2 * DN_WIDTH:], b_ref[...], a_ref[...], alog_ref[...], dtb_ref[...])
        dg = mmx_tn(_chunk_tril(tm), dgc_ref[...] + dgc2_ref[...])
        dyq, dyk, dyv, db, da, dalog, ddtb = vjp((_split_heads(dq_ref, 0), _split_heads(dk_ref, 0), dv_ref[...],
                                                  dbeta_ref[...], dg))
        for h in range(DN_HEADS):
            dy_ref[:, h * DN_HEAD_DIM:(h + 1) * DN_HEAD_DIM] = dyq[h]
            dy_ref[:, DN_WIDTH + h * DN_HEAD_DIM: DN_WIDTH + (h + 1) * DN_HEAD_DIM] = dyk[h]
        dy_ref[:, 2 * DN_WIDTH:] = dyv
        db_ref[...] = db
        da_ref[...] = da
        _acc_out(dalog_ref, i == 0, dalog)
        _acc_out(ddtb_ref, i == 0, ddtb)

    row = lambda i: (i, 0)
    const = lambda i: (0, 0)
    prev, _ = _halo_specs(tm, cw, bps, t // tm)
    return pl.pallas_call(
        body, name=name, grid=(t // tm,),
        in_specs=[pl.BlockSpec((tm, cw), row), prev, pl.BlockSpec((tm, LANES), row), pl.BlockSpec((tm, LANES), row),
                  pl.BlockSpec((CONV_K, cw), const), pl.BlockSpec((1, LANES), const), pl.BlockSpec((1, LANES), const),
                  pl.BlockSpec((tm, DN_WIDTH), row), pl.BlockSpec((tm, DN_WIDTH), row), pl.BlockSpec((tm, DN_WIDTH), row),
                  pl.BlockSpec((tm, LANES), row), pl.BlockSpec((tm, LANES), row), pl.BlockSpec((tm, LANES), row)],
        out_specs=(pl.BlockSpec((tm, cw), row), pl.BlockSpec((tm, LANES), row), pl.BlockSpec((tm, LANES), row),
                   pl.BlockSpec((1, LANES), const), pl.BlockSpec((1, LANES), const)),
        out_shape=(jax.ShapeDtypeStruct((t, cw), F32), jax.ShapeDtypeStruct((t, LANES), F32), jax.ShapeDtypeStruct((t, LANES), F32),
                   jax.ShapeDtypeStruct((1, LANES), F32), jax.ShapeDtypeStruct((1, LANES), F32)),
        compiler_params=_cparams(("arbitrary",)),
    )(qkv, qkv, bpre, apre, conv_w, alog, dtb, dq, dk, dv, dbeta, dgc, dgc2)


def _conv_bwd(qkv, dy, conv_w, seq, *, name):
    t = qkv.shape[0]
    tm = _tm(t)
    bps = seq // tm
    cw = 3 * DN_WIDTH
    n_ext = tm + HALO

    def body(x_ref, halo_ref, dy_ref, dyn_ref, w_ref, dx_ref, dw_ref):
        i = pl.program_id(0)
        keep_prev = jnp.where(i % bps == 0, 0.0, 1.0)
        keep_next = jnp.where(i % bps == bps - 1, 0.0, 1.0)
        ext = jnp.concatenate([halo_ref[...] * keep_prev, x_ref[...]], axis=0)
        dy = dy_ref[...]
        dyext = jnp.concatenate([dy, dyn_ref[...] * keep_next], axis=0)
        w = w_ref[...]

        @pl.when(i == 0)
        def _():
            dw_ref[...] = jnp.zeros_like(dw_ref)

        dx = None
        for j in range(CONV_K):
            s = CONV_K - 1 - j
            fut = dyext if s == 0 else pltpu.roll(dyext, n_ext - s, 0)
            term = w[j:j + 1, :] * fut[0:tm, :]
            dx = term if dx is None else dx + term
            past = ext if s == 0 else pltpu.roll(ext, s, 0)
            dw_ref[j:j + 1, :] += jnp.sum(dy * past[HALO:HALO + tm, :], axis=0, keepdims=True)
        dx_ref[...] = dx

    row = lambda i: (i, 0)
    const = lambda i: (0, 0)
    prev, nxt = _halo_specs(tm, cw, bps, t // tm)
    return pl.pallas_call(
        body, name=name, grid=(t // tm,),
        in_specs=[pl.BlockSpec((tm, cw), row), prev, pl.BlockSpec((tm, cw), row), nxt, pl.BlockSpec((CONV_K, cw), const)],
        out_specs=(pl.BlockSpec((tm, cw), row), pl.BlockSpec((HALO, cw), const)),
        out_shape=(jax.ShapeDtypeStruct((t, cw), F32), jax.ShapeDtypeStruct((HALO, cw), F32)),
        compiler_params=_cparams(("arbitrary",)),
    )(qkv, qkv, dy, dy, conv_w)


def _inv_unit_lower(l_mats, eye):
    invs = [eye - l for l in l_mats]
    powers = list(l_mats)
    n = 2
    while n < eye.shape[0]:
        powers = [mmh(p, p) for p in powers]
        invs = [inv + mmh(inv, p) for inv, p in zip(invs, powers)]
        n *= 2
    return invs


@jax.custom_vjp
def _solve(l_mat, rhs, inv):
    return mmh(inv, rhs)


def _solve_fwd(l_mat, rhs, inv):
    sol = mmh(inv, rhs)
    return sol, (inv, sol)


def _solve_bwd(res, d_sol):
    inv, sol = res
    d_rhs = mm_tn(inv, d_sol)
    return -mm_nt(d_rhs, sol), d_rhs, jnp.zeros_like(inv)


_solve.defvjp(_solve_fwd, _solve_bwd)


def _prep_fn(q, k, v, gc, gr, b, inv):
    ids = range(len(q))
    c = q[0].shape[0]
    rr = lax.broadcasted_iota(jnp.int32, (c, c), 0)
    cc = lax.broadcasted_iota(jnp.int32, (c, c), 1)
    incl, strict = rr >= cc, rr > cc
    is_last = lax.broadcasted_iota(jnp.int32, (c, 1), 0) == c - 1
    qs = [q[i] * (DN_HEAD_DIM ** -0.5) for i in ids]
    decay = [jnp.where(incl, jnp.exp(jnp.where(incl, gc[i] - gr[i], 0.0)), 0.0) for i in ids]
    kb = [k[i] * b[i] for i in ids]
    vb = [v[i] * b[i] for i in ids]
    kk = [mm_nt(kb[i], k[i]) for i in ids]
    l_mat = [jnp.where(strict, kk[i] * decay[i], 0.0) for i in ids]
    eg = [jnp.exp(gc[i]) for i in ids]
    if inv is None:
        inv = _inv_unit_lower(l_mat, jnp.where(rr == cc, 1.0, 0.0).astype(F32))
    u_wy = [_solve(l_mat[i], vb[i], inv[i]) for i in ids]
    w_wy = [_solve(l_mat[i], kb[i] * eg[i], inv[i]) for i in ids]
    qk = [mm_nt(qs[i], k[i]) * decay[i] for i in ids]
    g_last = [jnp.sum(jnp.where(is_last, gc[i], 0.0), axis=0, keepdims=True) for i in ids]
    k_dec = [k[i] * jnp.exp(g_last[i] - gc[i]) for i in ids]
    egl = [jnp.broadcast_to(jnp.exp(g_last[i]), (1, LANES)) for i in ids]
    return [(w_wy[i], u_wy[i], qs[i] * eg[i], k_dec[i], qk[i], egl[i]) for i in ids], inv


def _seq_fn(w, u, qd, kd, qk, egl, s):
    ids = range(len(w))
    ws = [mm(w[i], s[i]) for i in ids]
    qs = [mm(qd[i], s[i]) for i in ids]
    v_new = [u[i] - ws[i] for i in ids]
    o = [qs[i] + mm(qk[i], v_new[i]) for i in ids]
    s_new = [s[i] * egl[i] + mm_tn(kd[i], v_new[i]) for i in ids]
    return o, s_new


def _lane_col(a, h):
    lane = lax.broadcasted_iota(jnp.int32, (1, LANES), 1)
    return jnp.sum(jnp.where(lane == h, a, 0.0), axis=1, keepdims=True)


def _col_lane(col, h):
    lane = lax.broadcasted_iota(jnp.int32, (1, LANES), 1)
    return jnp.where(lane == h, col, 0.0)


def _head_cols(h):
    return slice(h * DN_HEAD_DIM, (h + 1) * DN_HEAD_DIM)


def _chunk_rows(n):
    return pl.ds(pl.multiple_of(n * DN_CHUNK, DN_CHUNK), DN_CHUNK)


def _delta_prep(q, k, v, gc, grow, beta, *, name):
    t = q.shape[0]
    tm = _tm(t)
    cpb = tm // DN_CHUNK
    n_chunks = t // DN_CHUNK
    group = 2

    def body(q_ref, k_ref, v_ref, gc_ref, gr_ref, b_ref, w_ref, u_ref, qd_ref, kd_ref, qk_ref, egl_ref, inv_ref):
        def step(m, carry):
            probs = [(m * group + e, h) for e in range(group) for h in range(DN_HEADS)]
            gcb = [gc_ref[_chunk_rows(m * group + e), :] for e in range(group)]
            bb = [b_ref[_chunk_rows(m * group + e), :] for e in range(group)]
            grb = [gr_ref[m * group + e] for e in range(group)]
            for e in range(group):
                egl_ref[m * group + e] = jnp.zeros((HALO, LANES), F32)
            outs, invs = _prep_fn(
                [q_ref[_chunk_rows(n), _head_cols(h)] for n, h in probs], [k_ref[_chunk_rows(n), _head_cols(h)] for n, h in probs],
                [v_ref[_chunk_rows(n), _head_cols(h)] for n, h in probs],
                [_lane_col(gcb[e], h) for e in range(group) for h in range(DN_HEADS)],
                [grb[e][h:h + 1, :] for e in range(group) for h in range(DN_HEADS)],
                [_lane_col(bb[e], h) for e in range(group) for h in range(DN_HEADS)], None)
            for (n, h), (w, u, qd, kd, qk, egl), inv in zip(probs, outs, invs):
                rows, cols = _chunk_rows(n), _head_cols(h)
                w_ref[rows, cols] = w.astype(BF16)
                u_ref[rows, cols] = u
                qd_ref[rows, cols] = qd.astype(BF16)
                kd_ref[rows, cols] = kd.astype(BF16)
                qk_ref[n, h] = qk
                inv_ref[n, h] = inv
                egl_ref[n, h:h + 1, :] = egl
            return carry

        lax.fori_loop(0, cpb // group, step, 0)

    row = lambda i: (i, 0)
    tok = pl.BlockSpec((tm, DN_WIDTH), row)
    lanes = pl.BlockSpec((tm, LANES), row)
    sq = pl.BlockSpec((cpb, DN_HEADS, DN_CHUNK, DN_CHUNK), lambda i: (i, 0, 0, 0))
    return pl.pallas_call(
        body, name=name, grid=(t // tm,),
        in_specs=[tok, tok, tok, lanes, pl.BlockSpec((cpb, HALO, DN_CHUNK), lambda i: (i, 0, 0)), lanes],
        out_specs=(tok, tok, tok, tok, sq, pl.BlockSpec((cpb, HALO, LANES), lambda i: (i, 0, 0)), sq),
        out_shape=(jax.ShapeDtypeStruct((t, DN_WIDTH), BF16), jax.ShapeDtypeStruct((t, DN_WIDTH), F32),
                   jax.ShapeDtypeStruct((t, DN_WIDTH), BF16), jax.ShapeDtypeStruct((t, DN_WIDTH), BF16),
                   jax.ShapeDtypeStruct((n_chunks, DN_HEADS, DN_CHUNK, DN_CHUNK), F32),
                   jax.ShapeDtypeStruct((n_chunks, HALO, LANES), F32),
                   jax.ShapeDtypeStruct((n_chunks, DN_HEADS, DN_CHUNK, DN_CHUNK), F32)),
        compiler_params=_cparams(("parallel",)),
    )(q, k, v, gc, grow, beta)


def _delta_par_bwd(q, k, v, gc, grow, beta, inv, dw, du, dqd, dkd, dqk, degl, *, name):
    t = q.shape[0]
    tm = _tm(t)
    cpb = tm // DN_CHUNK
    n_chunks = t // DN_CHUNK
    group = 2

    def body(q_ref, k_ref, v_ref, gc_ref, gr_ref, b_ref, inv_ref, dw_ref, du_ref, dqd_ref, dkd_ref, dqk_ref, degl_ref,
             dq_ref, dk_ref, dv_ref, dgc_ref, dgr_ref, db_ref):
        def step(m, carry):
            chunks = [m * group + e for e in range(group)]
            probs = [(e, h) for e in range(group) for h in range(DN_HEADS)]
            rows = [_chunk_rows(n) for n in chunks]
            gcb, bb = [gc_ref[r, :] for r in rows], [b_ref[r, :] for r in rows]
            grb, deglb = [gr_ref[n] for n in chunks], [degl_ref[n] for n in chunks]
            for n in chunks:
                dgr_ref[n] = jnp.zeros((HALO, DN_CHUNK), F32)
            invs = [inv_ref[chunks[e], h] for e, h in probs]
            _, vjp = jax.vjp(lambda *a: _prep_fn(*a, invs)[0],
                             [q_ref[rows[e], _head_cols(h)] for e, h in probs], [k_ref[rows[e], _head_cols(h)] for e, h in probs],
                             [v_ref[rows[e], _head_cols(h)] for e, h in probs], [_lane_col(gcb[e], h) for e, h in probs],
                             [grb[e][h:h + 1, :] for e, h in probs], [_lane_col(bb[e], h) for e, h in probs])
            dq, dk, dv, dgc, dgr, db = vjp([(dw_ref[rows[e], _head_cols(h)], du_ref[rows[e], _head_cols(h)],
                                             dqd_ref[rows[e], _head_cols(h)], dkd_ref[rows[e], _head_cols(h)],
                                             dqk_ref[chunks[e], h], deglb[e][h:h + 1, :]) for e, h in probs])
            dgc_acc = [jnp.zeros((DN_CHUNK, LANES), F32) for _ in chunks]
            db_acc = [jnp.zeros((DN_CHUNK, LANES), F32) for _ in chunks]
            for i, (e, h) in enumerate(probs):
                cols = _head_cols(h)
                dq_ref[rows[e], cols] = dq[i]
                dk_ref[rows[e], cols] = dk[i]
                dv_ref[rows[e], cols] = dv[i]
                dgr_ref[chunks[e], h:h + 1, :] = dgr[i]
                dgc_acc[e] = dgc_acc[e] + _col_lane(dgc[i], h)
                db_acc[e] = db_acc[e] + _col_lane(db[i], h)
            for e in range(group):
                dgc_ref[rows[e], :] = dgc_acc[e]
                db_ref[rows[e], :] = db_acc[e]
            return carry

        lax.fori_loop(0, cpb // group, step, 0)

    row = lambda i: (i, 0)
    tok = pl.BlockSpec((tm, DN_WIDTH), row)
    lanes = pl.BlockSpec((tm, LANES), row)
    sq = pl.BlockSpec((cpb, DN_HEADS, DN_CHUNK, DN_CHUNK), lambda i: (i, 0, 0, 0))
    grs = pl.BlockSpec((cpb, HALO, DN_CHUNK), lambda i: (i, 0, 0))
    return pl.pallas_call(
        body, name=name, grid=(t // tm,),
        in_specs=[tok, tok, tok, lanes, grs, lanes, sq, tok, tok, tok, tok, sq, pl.BlockSpec((cpb, HALO, LANES), lambda i: (i, 0, 0))],
        out_specs=(tok, tok, tok, lanes, grs, lanes),
        out_shape=(jax.ShapeDtypeStruct((t, DN_WIDTH), F32),) * 3
        + (jax.ShapeDtypeStruct((t, LANES), F32), jax.ShapeDtypeStruct((n_chunks, HALO, DN_CHUNK), F32),
           jax.ShapeDtypeStruct((t, LANES), F32)),
        compiler_params=_cparams(("parallel",)),
    )(q, k, v, gc, grow, beta, inv, dw, du, dqd, dkd, dqk, degl)


def _seq_specs(n_seq, seq, reverse):
    tm = _tm(seq)
    nb = seq // tm
    cpb = tm // DN_CHUNK
    pair = 2 if n_seq % 2 == 0 else 1
    blk = (lambda j: nb - 1 - j) if reverse else (lambda j: j)
    tok = pl.BlockSpec((pair, tm, DN_WIDTH), lambda b, j: (b, blk(j), 0))
    sq = pl.BlockSpec((pair, cpb, DN_HEADS, DN_CHUNK, DN_CHUNK), lambda b, j: (b, blk(j), 0, 0, 0))
    rows8 = pl.BlockSpec((pair, cpb, HALO, LANES), lambda b, j: (b, blk(j), 0, 0))
    state = pl.BlockSpec((pair, cpb, DN_HEADS, DN_HEAD_DIM, DN_HEAD_DIM), lambda b, j: (b, blk(j), 0, 0, 0))
    return nb, cpb, pair, tok, sq, rows8, state


def _by_seq(a, n_seq):
    return a.reshape((n_seq, a.shape[0] // n_seq) + a.shape[1:])


def _flat_seq(a):
    return a.reshape((a.shape[0] * a.shape[1],) + a.shape[2:])


def _delta_seq_fwd(w, u, qd, kd, qk, egl, n_seq, seq, *, name):
    nb, cpb, pair, tok, sq, rows8, state = _seq_specs(n_seq, seq, False)
    probs = [(e, h) for e in range(pair) for h in range(DN_HEADS)]

    def body(w_ref, u_ref, qd_ref, kd_ref, qk_ref, egl_ref, o_ref, st_ref, s_s):
        @pl.when(pl.program_id(1) == 0)
        def _():
            s_s[...] = jnp.zeros_like(s_s)

        def step(n, carry):
            rows = _chunk_rows(n)
            eglb = [egl_ref[e, n] for e in range(pair)]
            s = [s_s[e, h] for e, h in probs]
            for (e, h), s_eh in zip(probs, s):
                st_ref[e, n, h] = s_eh
            o, s_new = _seq_fn([w_ref[e, rows, _head_cols(h)] for e, h in probs], [u_ref[e, rows, _head_cols(h)] for e, h in probs],
                               [qd_ref[e, rows, _head_cols(h)] for e, h in probs], [kd_ref[e, rows, _head_cols(h)] for e, h in probs],
                               [qk_ref[e, n, h] for e, h in probs], [eglb[e][h:h + 1, :] for e, h in probs], s)
            for i, (e, h) in enumerate(probs):
                o_ref[e, rows, _head_cols(h)] = o[i]
                s_s[e, h] = s_new[i]
            return carry

        lax.fori_loop(0, cpb, step, 0)

    o, states = pl.pallas_call(
        body, name=name, grid=(n_seq // pair, nb),
        in_specs=[tok, tok, tok, tok, sq, rows8],
        out_specs=(tok, state),
        out_shape=(jax.ShapeDtypeStruct((n_seq, seq, DN_WIDTH), F32),
                   jax.ShapeDtypeStruct((n_seq, seq // DN_CHUNK, DN_HEADS, DN_HEAD_DIM, DN_HEAD_DIM), F32)),
        scratch_shapes=[pltpu.VMEM((pair, DN_HEADS, DN_HEAD_DIM, DN_HEAD_DIM), F32)],
        compiler_params=_cparams(("parallel", "arbitrary")),
    )(*[_by_seq(a, n_seq) for a in (w, u, qd, kd, qk, egl)])
    return _flat_seq(o), _flat_seq(states)


def _delta_seq_bwd(w, u, qd, kd, qk, egl, states, do, n_seq, seq, *, name):
    nb, cpb, pair, tok, sq, rows8, state = _seq_specs(n_seq, seq, True)
    probs = [(e, h) for e in range(pair) for h in range(DN_HEADS)]

    def body(w_ref, u_ref, qd_ref, kd_ref, qk_ref, egl_ref, st_ref, do_ref, dw_ref, du_ref, dqd_ref, dkd_ref, dqk_ref,
             degl_ref, ds_s):
        @pl.when(pl.program_id(1) == 0)
        def _():
            ds_s[...] = jnp.zeros_like(ds_s)

        def step(m, carry):
            n = cpb - 1 - m
            rows = _chunk_rows(n)
            eglb = [egl_ref[e, n] for e in range(pair)]
            for e in range(pair):
                degl_ref[e, n] = jnp.zeros((HALO, LANES), F32)
            _, vjp = jax.vjp(_seq_fn, [w_ref[e, rows, _head_cols(h)].astype(F32) for e, h in probs],
                             [u_ref[e, rows, _head_cols(h)] for e, h in probs],
                             [qd_ref[e, rows, _head_cols(h)].astype(F32) for e, h in probs],
                             [kd_ref[e, rows, _head_cols(h)].astype(F32) for e, h in probs],
                             [qk_ref[e, n, h] for e, h in probs], [eglb[e][h:h + 1, :] for e, h in probs],
                             [st_ref[e, n, h] for e, h in probs])
            dw, du, dqd, dkd, dqk, degl, ds_in = vjp(([do_ref[e, rows, _head_cols(h)] for e, h in probs],
                                                      [ds_s[e, h] for e, h in probs]))
            for i, (e, h) in enumerate(probs):
                cols = _head_cols(h)
                dw_ref[e, rows, cols] = dw[i]
                du_ref[e, rows, cols] = du[i]
                dqd_ref[e, rows, cols] = dqd[i]
                dkd_ref[e, rows, cols] = dkd[i]
                dqk_ref[e, n, h] = dqk[i]
                degl_ref[e, n, h:h + 1, :] = degl[i]
                ds_s[e, h] = ds_in[i]
            return carry

        lax.fori_loop(0, cpb, step, 0)

    nc = seq // DN_CHUNK
    outs = pl.pallas_call(
        body, name=name, grid=(n_seq // pair, nb),
        in_specs=[tok, tok, tok, tok, sq, rows8, state, tok],
        out_specs=(tok, tok, tok, tok, sq, rows8),
        out_shape=(jax.ShapeDtypeStruct((n_seq, seq, DN_WIDTH), F32),) * 4
        + (jax.ShapeDtypeStruct((n_seq, nc, DN_HEADS, DN_CHUNK, DN_CHUNK), F32),
           jax.ShapeDtypeStruct((n_seq, nc, HALO, LANES), F32)),
        scratch_shapes=[pltpu.VMEM((pair, DN_HEADS, DN_HEAD_DIM, DN_HEAD_DIM), F32)],
        compiler_params=_cparams(("parallel", "arbitrary")),
    )(*[_by_seq(a, n_seq) for a in (w, u, qd, kd, qk, egl, states, do)])
    return tuple(_flat_seq(a) for a in outs)


def _dn_gate(o, z, dnw):
    return o * lax.rsqrt(jnp.mean(o * o, axis=-1, keepdims=True) + EPS) * dnw * _silu(z)


def _mix_out_fwd(x, sg, o, z, wo_sg, wo_dn, dnw, *, name):
    t = x.shape[0]
    tm = _tm(t)

    def body(x_ref, sg_ref, o_ref, z_ref, wsg_ref, wdn_ref, dnw_ref, y_ref, dn_s):
        for h, (oh, zh) in enumerate(zip(_split_heads(o_ref, 0), _split_heads(z_ref, 0))):
            dn_s[:, h * DN_HEAD_DIM:(h + 1) * DN_HEAD_DIM] = _dn_gate(oh, zh, dnw_ref[...]).astype(BF16)
        y_ref[...] = (x_ref[...] + jnp.dot(sg_ref[...].astype(BF16), wsg_ref[...], preferred_element_type=F32)
                      + jnp.dot(dn_s[...], wdn_ref[...], preferred_element_type=F32))

    row = lambda i: (i, 0)
    const = lambda i: (0, 0)
    half = pl.BlockSpec((tm, DN_WIDTH), row)
    return pl.pallas_call(
        body, name=name, grid=(t // tm,),
        in_specs=[pl.BlockSpec((tm, D_MODEL), row), half, half, half, pl.BlockSpec((SG_WIDTH, D_MODEL), const),
                  pl.BlockSpec((DN_WIDTH, D_MODEL), const), pl.BlockSpec((1, DN_HEAD_DIM), const)],
        out_specs=pl.BlockSpec((tm, D_MODEL), row),
        out_shape=jax.ShapeDtypeStruct((t, D_MODEL), F32),
        scratch_shapes=[pltpu.VMEM((tm, DN_WIDTH), BF16)],
        compiler_params=_cparams(("parallel",)),
    )(x, sg, o, z, wo_sg, wo_dn, dnw)


def _mix_out_bwd(dy, sg, o, z, wo_sg, wo_dn, dnw, *, name):
    t = dy.shape[0]
    tm = _tm(t)

    def body(dy_ref, sg_ref, o_ref, z_ref, wsg_ref, wdn_ref, dnw_ref, dsg_ref, do_ref, dz_ref, dwsg_ref, dwdn_ref, ddnw_ref, dn_s):
        i = pl.program_id(0)
        dyb = dy_ref[...].astype(BF16)
        nt = (((1,), (1,)), ((), ()))
        tn = (((0,), (0,)), ((), ()))
        dsg_ref[...] = lax.dot_general(dyb, wsg_ref[...], nt, preferred_element_type=F32)
        ddn = lax.dot_general(dyb, wdn_ref[...], nt, preferred_element_type=F32)
        ddnw = None
        for h, (oh, zh) in enumerate(zip(_split_heads(o_ref, 0), _split_heads(z_ref, 0))):
            cols = slice(h * DN_HEAD_DIM, (h + 1) * DN_HEAD_DIM)
            out, vjp = jax.vjp(_dn_gate, oh, zh, dnw_ref[...])
            dn_s[:, cols] = out.astype(BF16)
            doh, dzh, dw = vjp(ddn[:, cols])
            do_ref[:, cols] = doh
            dz_ref[:, cols] = dzh
            ddnw = dw if ddnw is None else ddnw + dw
        _acc_out(ddnw_ref, i == 0, ddnw)
        _acc_out(dwsg_ref, i == 0, lax.dot_general(sg_ref[...].astype(BF16), dyb, tn, preferred_element_type=F32))
        _acc_out(dwdn_ref, i == 0, lax.dot_general(dn_s[...], dyb, tn, preferred_element_type=F32))

    row = lambda i: (i, 0)
    const = lambda i: (0, 0)
    half = pl.BlockSpec((tm, DN_WIDTH), row)
    wspec = pl.BlockSpec((DN_WIDTH, D_MODEL), const)
    return pl.pallas_call(
        body, name=name, grid=(t // tm,),
        in_specs=[pl.BlockSpec((tm, D_MODEL), row), half, half, half, wspec, wspec, pl.BlockSpec((1, DN_HEAD_DIM), const)],
        out_specs=(half, half, half, wspec, wspec, pl.BlockSpec((1, DN_HEAD_DIM), const)),
        out_shape=(jax.ShapeDtypeStruct((t, DN_WIDTH), F32),) * 3 + (jax.ShapeDtypeStruct((DN_WIDTH, D_MODEL), F32),) * 2
        + (jax.ShapeDtypeStruct((1, DN_HEAD_DIM), F32),),
        scratch_shapes=[pltpu.VMEM((tm, DN_WIDTH), BF16)],
        compiler_params=_cparams(("arbitrary",)),
    )(dy, sg, o, z, wo_sg, wo_dn, dnw)


_MESH = pl.DeviceIdType.MESH
_HBM = pl.BlockSpec(memory_space=pl.ANY)


def _mesh_pos():
    x, y, c = lax.axis_index("x"), lax.axis_index("y"), lax.axis_index("c")
    return x, y, c, [(1 - x, y), (x, 1 - y), (1 - x, 1 - y)]


def _gather2(arrs, *, name):
    n = len(arrs)
    slots = N_DEV - 1

    def body(*refs):
        in_refs, out_refs = refs[:n], refs[n:2 * n]
        send_sems, recv_sems, local_sems = refs[2 * n:]
        x, y, c, chips = _mesh_pos()
        me, sibling = (x, y, c), (x, y, 1 - c)

        def copy(k, slot, block, to, src=None):
            dst = out_refs[k].at[4 * block[0] + 2 * block[1] + block[2]]
            return pltpu.make_async_remote_copy(src_ref=dst if src is None else src, dst_ref=dst,
                                                send_sem=send_sems.at[k * slots + slot], recv_sem=recv_sems.at[k * slots + slot],
                                                device_id=to, device_id_type=_MESH)

        local = [pltpu.make_async_copy(in_refs[k], out_refs[k].at[4 * x + 2 * y + c], local_sems.at[k]) for k in range(n)]
        sent = []
        for k in range(n):
            sent.append(copy(k, 0, me, sibling, src=in_refs[k]))
            sent += [copy(k, 1 + j, me, (*chip, c), src=in_refs[k]) for j, chip in enumerate(chips)]
        for cp in local + sent:
            cp.start()
        for j, chip in enumerate(chips):
            for k in range(n):
                copy(k, 1 + j, (*chip, c), me).wait_recv()
                passed = copy(k, 4 + j, (*chip, c), sibling)
                passed.start()
                sent.append(passed)
        for k in range(n):
            copy(k, 0, sibling, me).wait_recv()
            for j, chip in enumerate(chips):
                copy(k, 4 + j, (*chip, 1 - c), me).wait_recv()
        for cp in sent:
            cp.wait_send()
        for cp in local:
            cp.wait()

    return pl.pallas_call(
        body, name=name, in_specs=[_HBM] * n, out_specs=(_HBM,) * n,
        out_shape=tuple(jax.ShapeDtypeStruct((N_DEV,) + a.shape, a.dtype) for a in arrs),
        scratch_shapes=[pltpu.SemaphoreType.DMA((n * slots,)), pltpu.SemaphoreType.DMA((n * slots,)),
                        pltpu.SemaphoreType.DMA((n,))],
    )(*arrs)


_SEM = pl.BlockSpec(memory_space=pltpu.SEMAPHORE)
_EFFECT = pltpu.SideEffectType.DATAFLOW_SIDE_EFFECTING


def _direct_copies(src_refs, land_refs, send_sems, recv_sems, gather):
    x, y, c, _ = _mesh_pos()
    me = 4 * x + 2 * y + c
    n_peer = N_DEV - 1
    copies = []
    for r in range(1, N_DEV):
        px = 1 - x if r & 4 else x
        py = 1 - y if r & 2 else y
        pc = 1 - c if r & 1 else c
        for k, (src, land) in enumerate(zip(src_refs, land_refs)):
            copies.append(pltpu.make_async_remote_copy(
                src_ref=src if gather else src.at[4 * px + 2 * py + pc], dst_ref=land.at[me],
                send_sem=send_sems.at[k * n_peer + r - 1], recv_sem=recv_sems.at[k * n_peer + r - 1],
                device_id=(px, py, pc), device_id_type=_MESH))
    return copies


def _send_start(arrs, gather, after=None, *, name):
    n = len(arrs)
    lands = [lax.empty(((N_DEV,) + a.shape) if gather else a.shape, a.dtype) for a in arrs]
    n_in = 2 * n + (0 if after is None else 1)

    def body(*refs):
        src_refs, land_refs, send_sems, recv_sems, token = refs[:n], refs[n:2 * n], refs[n_in], refs[n_in + 1], refs[-1]
        for cp in _direct_copies(src_refs, land_refs, send_sems, recv_sems, gather):
            cp.start()
        token[...] = jnp.zeros_like(token)

    n_sem = n * (N_DEV - 1)
    bufs = list(arrs) + lands
    out = pl.pallas_call(
        body, name=name,
        out_shape=(pltpu.SemaphoreType.DMA((n_sem,)), pltpu.SemaphoreType.DMA((n_sem,)))
        + tuple(pltpu.HBM(b.shape, b.dtype) for b in bufs) + (jax.ShapeDtypeStruct((HALO, LANES), F32),),
        in_specs=[_HBM] * n_in, out_specs=(_SEM, _SEM) + (_HBM,) * (2 * n) + (pl.BlockSpec(memory_space=pltpu.VMEM),),
        input_output_aliases={i: 2 + i for i in range(2 * n)},
        compiler_params=pltpu.CompilerParams(has_side_effects=_EFFECT),
    )(*[pltpu.with_memory_space_constraint(b, pltpu.HBM) for b in bufs], *([] if after is None else [after]))
    return (out[0], out[1], list(out[2:2 + n]), list(out[2 + n:2 + 2 * n])), out[-1]


def _send_wait(started, gather, after, *, name):
    send_sems, recv_sems, srcs, lands = started
    n = len(srcs)

    def body(*refs):
        src_refs, land_refs, send_ref, recv_ref = refs[:n], refs[n:2 * n], refs[2 * n], refs[2 * n + 1]
        for cp in _direct_copies(src_refs, land_refs, send_ref, recv_ref, gather):
            cp.wait_send()
            cp.wait_recv()

    bufs = srcs + lands
    out = pl.pallas_call(
        body, name=name, out_shape=tuple(pltpu.HBM(b.shape, b.dtype) for b in bufs),
        in_specs=[_HBM] * (2 * n) + [_SEM, _SEM, _HBM], out_specs=(_HBM,) * (2 * n),
        input_output_aliases={i: i for i in range(2 * n)},
        compiler_params=pltpu.CompilerParams(has_side_effects=_EFFECT),
    )(*bufs, send_sems, recv_sems, after)
    return list(out[n:])


def _row_block(rows, limit=256):
    best = rows
    for cand in range(8, limit + 1, 8):
        if rows % cand == 0:
            best = cand
    return best if rows > limit else rows


def _adam(gp, w, m, v, *, name):
    p, rows, cols = gp.shape
    rb = _row_block(rows)

    def body(gp_ref, w_ref, m_ref, v_ref, g_ref, d_ref, m2_ref, v2_ref):
        g = gp_ref[0].astype(F32)
        for s in range(1, p):
            g = g + gp_ref[s].astype(F32)
        m2 = ADAM_B1 * m_ref[...] + (1.0 - ADAM_B1) * g
        v2 = ADAM_B2 * v_ref[...] + (1.0 - ADAM_B2) * (g * g)
        m_hat = m2 / (1.0 - ADAM_B1 ** ADAM_STEP)
        v_hat = v2 / (1.0 - ADAM_B2 ** ADAM_STEP)
        g_ref[...] = g
        d_ref[...] = -ADAM_LR * (m_hat / (jnp.sqrt(v_hat) + ADAM_EPS) + ADAM_WD * w_ref[...])
        m2_ref[...] = m2
        v2_ref[...] = v2

    blk = pl.BlockSpec((rb, cols), lambda i: (i, 0))
    return pl.pallas_call(
        body, name=name, grid=(rows // rb,),
        in_specs=[pl.BlockSpec((p, rb, cols), lambda i: (0, i, 0)), blk, blk, blk],
        out_specs=(blk,) * 4, out_shape=(jax.ShapeDtypeStruct((rows, cols), F32),) * 4,
        compiler_params=_cparams(("parallel",)),
    )(gp, w, m, v)


def _cols_full(g):
    return jnp.transpose(g, (1, 0, 2)).reshape(g.shape[1], N_DEV * g.shape[2])


def _pad_lanes(a, width=LANES):
    return jnp.pad(a, ((0, 0), (0, width - a.shape[1])))


def _chunk_rows_of(a):
    by_chunk = jnp.transpose(a[:, :DN_HEADS].reshape(-1, DN_CHUNK, DN_HEADS), (0, 2, 1))
    return jnp.pad(by_chunk, ((0, 0), (0, HALO - DN_HEADS), (0, 0)))


_SMALL = (("ffn1_norm", D_MODEL), ("mix_norm", D_MODEL), ("ffn2_norm", D_MODEL), ("final_norm", D_MODEL), ("a_log", DN_HEADS),
          ("dt_bias", DN_HEADS), ("dn_norm", DN_HEAD_DIM), ("sg_ln_g", SG_WIDTH), ("sg_ln_b", SG_WIDTH),
          ("sg_w", SG_GROUPS * SG_CHUNK * SG_CHUNK), ("sg_b", SG_GROUPS * SG_CHUNK), ("conv_w", CONV_K * 3 * DN_WIDTH))
_SMALL_ROWS = 1128
_SMALL_SHAPES = {"ffn1_norm": (1, D_MODEL), "mix_norm": (1, D_MODEL), "ffn2_norm": (1, D_MODEL), "final_norm": (D_MODEL,),
                 "a_log": (1, DN_HEADS), "dt_bias": (1, DN_HEADS), "dn_norm": (1, DN_HEAD_DIM), "sg_ln_g": (1, SG_WIDTH),
                 "sg_ln_b": (1, SG_WIDTH), "sg_w": (1, SG_GROUPS, SG_CHUNK, SG_CHUNK), "sg_b": (1, SG_GROUPS, SG_CHUNK)}


def _pack_small(d):
    flat = jnp.concatenate([d[name].reshape(-1) for name, _ in _SMALL])
    return jnp.pad(flat, (0, _SMALL_ROWS * LANES - flat.shape[0])).reshape(_SMALL_ROWS, LANES)


def _unpack_small(a):
    flat, out, at = a.reshape(-1), {}, 0
    for name, size in _SMALL:
        out[name] = flat[at:at + size]
        at += size
    return out


def kernel(x, ffn1_norm, ffn1_w_gate, ffn1_w_up, ffn1_w_down, mix_norm, w_in, conv_w, a_log, dt_bias, dn_norm, sg_ln_g, sg_ln_b, sg_w, sg_b, w_out, ffn2_norm, ffn2_w_gate, ffn2_w_up, ffn2_w_down, final_norm, loss_target, m_ffn1_norm, m_ffn1_w_gate, m_ffn1_w_up, m_ffn1_w_down, m_mix_norm, m_w_in, m_conv_w, m_a_log, m_dt_bias, m_dn_norm, m_sg_ln_g, m_sg_ln_b, m_sg_w, m_sg_b, m_w_out, m_ffn2_norm, m_ffn2_w_gate, m_ffn2_w_up, m_ffn2_w_down, m_final_norm, v_ffn1_norm, v_ffn1_w_gate, v_ffn1_w_up, v_ffn1_w_down, v_mix_norm, v_w_in, v_conv_w, v_a_log, v_dt_bias, v_dn_norm, v_sg_ln_g, v_sg_ln_b, v_sg_w, v_sg_b, v_w_out, v_ffn2_norm, v_ffn2_w_gate, v_ffn2_w_up, v_ffn2_w_down, v_final_norm):
    weights = dict(ffn1_norm=ffn1_norm, ffn1_w_gate=ffn1_w_gate, ffn1_w_up=ffn1_w_up, ffn1_w_down=ffn1_w_down, mix_norm=mix_norm, w_in=w_in, conv_w=conv_w, a_log=a_log, dt_bias=dt_bias, dn_norm=dn_norm, sg_ln_g=sg_ln_g, sg_ln_b=sg_ln_b, sg_w=sg_w, sg_b=sg_b, w_out=w_out, ffn2_norm=ffn2_norm, ffn2_w_gate=ffn2_w_gate, ffn2_w_up=ffn2_w_up, ffn2_w_down=ffn2_w_down, final_norm=final_norm)
    mom_m = dict(ffn1_norm=m_ffn1_norm, ffn1_w_gate=m_ffn1_w_gate, ffn1_w_up=m_ffn1_w_up, ffn1_w_down=m_ffn1_w_down, mix_norm=m_mix_norm, w_in=m_w_in, conv_w=m_conv_w, a_log=m_a_log, dt_bias=m_dt_bias, dn_norm=m_dn_norm, sg_ln_g=m_sg_ln_g, sg_ln_b=m_sg_ln_b, sg_w=m_sg_w, sg_b=m_sg_b, w_out=m_w_out, ffn2_norm=m_ffn2_norm, ffn2_w_gate=m_ffn2_w_gate, ffn2_w_up=m_ffn2_w_up, ffn2_w_down=m_ffn2_w_down, final_norm=m_final_norm)
    mom_v = dict(ffn1_norm=v_ffn1_norm, ffn1_w_gate=v_ffn1_w_gate, ffn1_w_up=v_ffn1_w_up, ffn1_w_down=v_ffn1_w_down, mix_norm=v_mix_norm, w_in=v_w_in, conv_w=v_conv_w, a_log=v_a_log, dt_bias=v_dt_bias, dn_norm=v_dn_norm, sg_ln_g=v_sg_ln_g, sg_ln_b=v_sg_ln_b, sg_w=v_sg_w, sg_b=v_sg_b, w_out=v_w_out, ffn2_norm=v_ffn2_norm, ffn2_w_gate=v_ffn2_w_gate, ffn2_w_up=v_ffn2_w_up, ffn2_w_down=v_ffn2_w_down, final_norm=v_final_norm)
    order = list(weights)
    big = ("ffn1_w_gate", "ffn1_w_up", "ffn1_w_down", "w_in", "w_out", "ffn2_w_gate", "ffn2_w_up", "ffn2_w_down")
    col_sharded = ("ffn1_w_gate", "ffn1_w_up", "w_in", "ffn2_w_gate", "ffn2_w_up")

    n_seq, seq, _ = x.shape
    t = n_seq * seq
    me = 4 * lax.axis_index("x") + 2 * lax.axis_index("y") + lax.axis_index("c")
    x0 = x.reshape(t, D_MODEL)
    tgt = loss_target.reshape(t, D_MODEL)

    def fill_own(land, own_block):
        return lax.dynamic_update_index_in_dim(land, own_block, me, 0)

    def rows_view(n, a):
        return jnp.transpose(a) if n in col_sharded else a

    def as_full(n, g):
        return g.reshape(-1, g.shape[-1])

    shards = {n: rows_view(n, weights[n][0]).astype(BF16) for n in big}
    ffn1_names, mix_names, ffn2_names = big[:3], big[3:5], big[5:]
    full = {n: as_full(n, g) for n, g in zip(ffn1_names, _gather2([shards[n] for n in ffn1_names], name="gather_ffn1"))}
    mix_srcs = [shards[n] for n in mix_names] + [conv_w[0]]
    mix_started, mix_token = _send_start(mix_srcs, True, full[ffn1_names[2]], name="gather_mix_start")
    ffn2_started, ffn2_token = _send_start([shards[n] for n in ffn2_names], True, mix_token, name="gather_ffn2_start")
    ffn1_norm_fwd = ffn1_norm + ffn2_token[:1, :1]
    alog, dtb = _pad_lanes(a_log), _pad_lanes(dt_bias)
    sgbt = _pad_lanes(sg_b[0].T)
    fnw = final_norm.reshape(1, D_MODEL)

    x1, h1, g1, u1 = _ffn_fwd(x0, ffn1_norm_fwd, full["ffn1_w_gate"], full["ffn1_w_up"], full["ffn1_w_down"], name="ffn1_fwd")
    mix_lands = [fill_own(land, src) for land, src in zip(_send_wait(mix_started, True, x1, name="gather_mix_wait"), mix_srcs)]
    full.update({n: as_full(n, g) for n, g in zip(mix_names, mix_lands)})
    conv_full = _cols_full(mix_lands[-1])
    w_in_t = full["w_in"]
    offs = (0, SG_WIDTH, 2 * SG_WIDTH, 2 * SG_WIDTH + 3 * DN_WIDTH, 2 * SG_WIDTH + 4 * DN_WIDTH)
    n_proj = offs[-1]

    def pad_rows(a):
        return jnp.pad(a, ((0, LANES - a.shape[0]), (0, 0)))

    ws = [w_in_t[offs[0]:offs[1]], w_in_t[offs[1]:offs[2]], w_in_t[offs[2]:offs[3]], w_in_t[offs[3]:offs[4]],
          pad_rows(w_in_t[n_proj:n_proj + DN_HEADS]), pad_rows(w_in_t[n_proj + DN_HEADS:n_proj + 2 * DN_HEADS])]
    wo_sg, wo_dn = full["w_out"][:SG_WIDTH], full["w_out"][SG_WIDTH:]
    u, v, qkv, z, bpre, apre = _mix_in_fwd(x1, mix_norm, ws, name="mix_in_fwd")
    sg_out = _sg_fwd(u, v, sg_ln_g, sg_ln_b, sg_w[0], sgbt, name="sg_fwd")
    q, k, vv, beta, gc = _dn_prep_fwd(qkv, bpre, apre, conv_full, alog, dtb, seq, name="dn_prep_fwd")
    grow = _chunk_rows_of(gc)
    wy_w, wy_u, q_dec, k_dec, qk, egl, inv = _delta_prep(q, k, vv, gc, grow, beta, name="delta_prep")
    o, states = _delta_seq_fwd(wy_w, wy_u, q_dec, k_dec, qk, egl, n_seq, seq, name="delta_seq_fwd")
    x2 = _mix_out_fwd(x1, sg_out, o, z, wo_sg, wo_dn, dn_norm, name="mix_out_fwd")
    ffn2_lands = _send_wait(ffn2_started, True, x2, name="gather_ffn2_wait")
    full.update({n: as_full(n, fill_own(land, shards[n])) for n, land in zip(ffn2_names, ffn2_lands)})
    dx3, loss_part, d_fn, h2, g2, u2 = _ffn_fwd(x2, ffn2_norm, full["ffn2_w_gate"], full["ffn2_w_up"], full["ffn2_w_down"],
                                                tgt, fnw, name="ffn2_fwd_loss")
    loss = lax.psum(loss_part[0, 0], ("x", "y", "c"))

    dx2, d_n2, d_g2, d_u2, d_d2 = _ffn_bwd(x2, ffn2_norm, h2, g2, u2, full["ffn2_w_gate"], full["ffn2_w_up"],
                                           full["ffn2_w_down"], dx3, name="ffn2_bwd")
    def by_owner(d_rows):
        return d_rows.reshape(N_DEV, -1, D_MODEL)

    ffn2_pieces = [by_owner(d_g2), by_owner(d_u2), by_owner(d_d2)]
    ffn2_sent, sent_token = _send_start(ffn2_pieces, False, name="grads_ffn2_start")
    dsg, do, dz, d_wo_sg, d_wo_dn, d_dnw = _mix_out_bwd(dx2, sg_out, o, z, wo_sg, wo_dn, dn_norm + sent_token[:1, :1],
                                                        name="mix_out_bwd")
    d_seq = _delta_seq_bwd(wy_w, wy_u, q_dec, k_dec, qk, egl, states, do, n_seq, seq, name="delta_seq_bwd")
    dq, dk, dv, dgc_a, dgrow, dbeta = _delta_par_bwd(q, k, vv, gc, grow, beta, inv, *d_seq, name="delta_par_bwd")
    dgc_b = _pad_lanes(jnp.transpose(dgrow[:, :DN_HEADS, :], (0, 2, 1)).reshape(t, DN_HEADS))
    dy_conv, dbpre, dapre, d_alog, d_dtb = _dn_prep_bwd(qkv, bpre, apre, conv_full, alog, dtb, dq, dk, dv, dbeta, dgc_a, dgc_b,
                                                        seq, name="dn_prep_bwd")
    dqkv, d_conv = _conv_bwd(qkv, dy_conv, conv_full, seq, name="conv_bwd")
    du, dvv, d_lng, d_lnb, d_wc, d_sgbt = _sg_bwd(u, v, sg_ln_g, sg_ln_b, sg_w[0], sgbt, dsg, name="sg_bwd")
    dx1, d_mixn, d_wp = _mix_in_bwd(x1, mix_norm, ws, dx2, (du, dvv, dqkv, dz, dbpre, dapre), name="mix_in_bwd")
    d_w_in_t = jnp.concatenate([d_wp[:n_proj], d_wp[_PROJ_OFFSETS[4]:_PROJ_OFFSETS[4] + DN_HEADS],
                                d_wp[_PROJ_OFFSETS[5]:_PROJ_OFFSETS[5] + DN_HEADS]], axis=0)
    d_w_out = jnp.concatenate([d_wo_sg, d_wo_dn], axis=0)
    mix_pieces = [by_owner(d_w_in_t), by_owner(d_w_out).astype(BF16)]
    mix_sent, sent_token = _send_start(mix_pieces, False, name="grads_mix_start")
    grad_x, d_n1, dg1, du1, a1, dyh1 = _ffn_bwd_x(x0, ffn1_norm + sent_token[:1, :1], g1, u1, full["ffn1_w_gate"],
                                                  full["ffn1_w_up"], full["ffn1_w_down"], dx1, name="ffn1_bwd_x")
    small_grads = dict(ffn1_norm=d_n1, mix_norm=d_mixn, ffn2_norm=d_n2, final_norm=d_fn, a_log=d_alog[:, :DN_HEADS],
                       dt_bias=d_dtb[:, :DN_HEADS], dn_norm=d_dnw, sg_ln_g=d_lng, sg_ln_b=d_lnb, sg_w=d_wc,
                       sg_b=d_sgbt[:, :SG_GROUPS].T, conv_w=d_conv[:CONV_K])
    small_src = _pack_small(small_grads)
    small_sent, small_token = _send_start([small_src], True, name="small_grads_start")
    late, tokens = [], []

    def send_early(k, grad):
        piece = by_owner(grad)
        sent, token = _send_start([piece], False, name="grads_" + ffn1_names[k] + "_start")
        late.append(((ffn1_names[k],), sent, [piece]))
        tokens.append(token)
        return token

    _ffn_wgrads(h1, dg1, du1, a1, dyh1, send_early, small_token, name="ffn1_bwd")

    res = {}
    after = tokens[-1]

    def update(names, sent, pieces, after):
        lands = _send_wait(sent, False, after, name="grads_" + names[0] + "_wait")
        for n, land, p in zip(names, lands, pieces):
            got = fill_own(land, lax.dynamic_index_in_dim(p, me, 0, keepdims=False))
            upd = _adam(got, *[rows_view(n, src[n][0]) for src in (weights, mom_m, mom_v)], name="adam_" + n)
            res[n] = [rows_view(n, a) for a in upd]
            after = upd[0]
        return after

    for group in [(ffn2_names, ffn2_sent, ffn2_pieces), (mix_names, mix_sent, mix_pieces)] + late[:-1]:
        after = update(*group, after)
    (small_land,) = _send_wait(small_sent, True, after, name="small_grads_wait")
    small_parts = fill_own(small_land, small_src)
    zeros_conv = jnp.zeros((CONV_K * 3 * DN_WIDTH,), F32)
    packed = [_pack_small({**{n: src[n] for n, _ in _SMALL if n != "conv_w"}, "conv_w": zeros_conv})
              for src in (weights, mom_m, mom_v)]
    small_upd = _adam(small_parts, *packed, name="adam_small")
    small_res = [_unpack_small(a) for a in small_upd]
    conv_grad = lax.dynamic_slice_in_dim(small_res[0]["conv_w"].reshape(CONV_K, 3 * DN_WIDTH), me * (3 * DN_WIDTH // N_DEV),
                                         3 * DN_WIDTH // N_DEV, axis=1)
    res["conv_w"] = _adam(conv_grad[None], conv_w[0], m_conv_w[0], v_conv_w[0], name="adam_conv_w")
    update(*late[-1], res["conv_w"][0])

    outs = [[], [], [], []]
    for n in order:
        for kind in range(4):
            if n in res:
                outs[kind].append(res[n][kind][None])
            else:
                outs[kind].append(small_res[kind][n].reshape(_SMALL_SHAPES[n]))
    return (loss, grad_x.reshape(x.shape), *outs[0], *outs[1], *outs[2], *outs[3])
```

```python
import functools

import jax
import jax.numpy as jnp
from jax import lax
from jax.experimental import pallas as pl
from jax.experimental.pallas import tpu as pltpu

F32 = jnp.float32
BF16 = jnp.bfloat16

D_MODEL = 1024
D_FF = 2816
SG_WIDTH = 512
SG_GROUPS = 8
SG_GROUP_DIM = 64
SG_CHUNK = 128
DN_WIDTH = 512
DN_HEAD_DIM = 128
DN_HEADS = 4
DN_CHUNK = 64
CONV_K = 4
EPS = 1e-6
N_DEV = 8
LANES = 128
HALO = 8
MXU_COLS = 256

ADAM_LR = 0.001
ADAM_B1 = 0.9
ADAM_B2 = 0.999
ADAM_EPS = 1e-08
ADAM_WD = 0.01
ADAM_STEP = 10

VMEM_LIMIT = 60 * 1024 * 1024
WGRAD_VMEM_BUDGET = 52 * 1024 * 1024
TOKEN_BLOCK = 512
FF_BLOCK_FWD = 1408

_HI = lax.Precision.HIGHEST


def _cparams(sem):
    return pltpu.CompilerParams(dimension_semantics=sem, vmem_limit_bytes=VMEM_LIMIT)


def _tm(t, pref=TOKEN_BLOCK):
    return min(pref, t)


def _dg(a, b, ca, cb, precision):
    if precision is not None:
        return lax.dot_general(a, b, (((ca,), (cb,)), ((), ())), precision=precision, preferred_element_type=F32)
    return lax.dot_general(a.astype(BF16), b.astype(BF16), (((ca,), (cb,)), ((), ())), preferred_element_type=F32)


def _make_mm(exact):
    @jax.custom_vjp
    def mm(a, b):
        return _dg(a, b, 1, 0, exact)

    @jax.custom_vjp
    def mm_nt(a, b):
        return _dg(a, b, 1, 1, exact)

    @jax.custom_vjp
    def mm_tn(a, b):
        return _dg(a, b, 0, 0, exact)

    mm.defvjp(lambda a, b: (mm(a, b), (a, b)), lambda r, g: (mm_nt(g, r[1]), mm_tn(r[0], g)))
    mm_nt.defvjp(lambda a, b: (mm_nt(a, b), (a, b)), lambda r, g: (mm(g, r[1]), mm_tn(g, r[0])))
    mm_tn.defvjp(lambda a, b: (mm_tn(a, b), (a, b)), lambda r, g: (mm_nt(r[1], g), mm(r[0], g)))
    return mm, mm_nt, mm_tn


mm, mm_nt, mm_tn = _make_mm(None)
mmx, mmx_nt, mmx_tn = _make_mm(_HI)
mmh, mmh_nt, mmh_tn = _make_mm(lax.Precision.HIGH)


def _sigmoid(x):
    return 1.0 / (1.0 + jnp.exp(-x))


def _silu(x):
    return x * _sigmoid(x)


def _softplus(x):
    neg_abs = jnp.where(x > 0, -x, x)
    return jnp.where(x > 0, x, 0.0) + jnp.log(1.0 + jnp.exp(neg_abs))


def _gelu(x):
    return 0.5 * x * (1.0 + jnp.tanh(0.7978845608028654 * (x + 0.044715 * (x * x * x))))


def _rms_fwd(x, g):
    r = lax.rsqrt(jnp.mean(x * x, axis=-1, keepdims=True) + EPS)
    xh = x * r
    return xh * g, xh, r


def _rms_bwd(dh, xh, r, g):
    dxh = dh * g
    dx = r * (dxh - xh * jnp.mean(dxh * xh, axis=-1, keepdims=True))
    return dx, jnp.sum(dh * xh, axis=0, keepdims=True)


def _acc_out(ref, first, val):
    @pl.when(first)
    def _():
        ref[...] = val

    @pl.when(jnp.logical_not(first))
    def _():
        ref[...] += val


def _ffn_fwd(x, nw, wg, wu, wd, tgt=None, fnw=None, *, name):
    t = x.shape[0]
    tm, fb = _tm(t), FF_BLOCK_FWD
    n_t, n_f = t // tm, D_FF // fb
    with_loss = tgt is not None

    def body(*refs):
        if with_loss:
            (x_ref, nw_ref, wg_ref, wu_ref, wd_ref, tgt_ref, fnw_ref, dy_ref, loss_ref, dfn_ref, h_ref, g_ref, u_ref,
             acc_s) = refs
        else:
            x_ref, nw_ref, wg_ref, wu_ref, wd_ref, y_ref, h_ref, g_ref, u_ref, acc_s = refs
        i, j = pl.program_id(0), pl.program_id(1)

        @pl.when(j == 0)
        def _():
            h, _, _ = _rms_fwd(x_ref[...], nw_ref[...])
            h_ref[...] = h.astype(BF16)
            acc_s[...] = jnp.zeros_like(acc_s)

        h = h_ref[...]
        nt = (((1,), (1,)), ((), ()))
        g = lax.dot_general(h, wg_ref[...], nt, preferred_element_type=F32)
        u = lax.dot_general(h, wu_ref[...], nt, preferred_element_type=F32)
        g_ref[...] = g.astype(BF16)
        u_ref[...] = u.astype(BF16)
        a = _silu(g) * u
        acc_s[...] += jnp.dot(a.astype(BF16), wd_ref[...], preferred_element_type=F32)

        @pl.when(j == n_f - 1)
        def _():
            y = x_ref[...] + 0.5 * acc_s[...]
            if not with_loss:
                y_ref[...] = y
            else:
                gf = fnw_ref[...]
                out, xh, r = _rms_fwd(y, gf)
                err = out - tgt_ref[...]
                part = 0.5 * jnp.sum(jnp.mean(err * err, axis=-1, keepdims=True), axis=0, keepdims=True)
                d_out = err * (1.0 / D_MODEL)
                dy, dgf = _rms_bwd(d_out, xh, r, gf)
                dy_ref[...] = dy
                _acc_out(loss_ref, i == 0, jnp.broadcast_to(part, loss_ref.shape))
                _acc_out(dfn_ref, i == 0, dgf)

    row = lambda i, j: (i, 0)
    const = lambda i, j: (0, 0)
    in_specs = [
        pl.BlockSpec((tm, D_MODEL), row),
        pl.BlockSpec((1, D_MODEL), const),
        pl.BlockSpec((fb, D_MODEL), lambda i, j: (j, 0)),
        pl.BlockSpec((fb, D_MODEL), lambda i, j: (j, 0)),
        pl.BlockSpec((fb, D_MODEL), lambda i, j: (j, 0)),
    ]
    args = [x, nw, wg, wu, wd]
    saved_shape = (jax.ShapeDtypeStruct((t, D_MODEL), BF16), jax.ShapeDtypeStruct((t, D_FF), BF16),
                   jax.ShapeDtypeStruct((t, D_FF), BF16))
    saved_specs = (pl.BlockSpec((tm, D_MODEL), row), pl.BlockSpec((tm, fb), lambda i, j: (i, j)),
                   pl.BlockSpec((tm, fb), lambda i, j: (i, j)))
    if with_loss:
        in_specs += [pl.BlockSpec((tm, D_MODEL), row), pl.BlockSpec((1, D_MODEL), const)]
        args += [tgt, fnw]
        out_shape = (jax.ShapeDtypeStruct((t, D_MODEL), F32), jax.ShapeDtypeStruct((8, LANES), F32),
                     jax.ShapeDtypeStruct((1, D_MODEL), F32)) + saved_shape
        out_specs = (pl.BlockSpec((tm, D_MODEL), row), pl.BlockSpec((8, LANES), const),
                     pl.BlockSpec((1, D_MODEL), const)) + saved_specs
        sem = ("arbitrary", "arbitrary")
    else:
        out_shape = (jax.ShapeDtypeStruct((t, D_MODEL), F32),) + saved_shape
        out_specs = (pl.BlockSpec((tm, D_MODEL), row),) + saved_specs
        sem = ("parallel", "arbitrary")
    return pl.pallas_call(
        body, name=name, grid=(n_t, n_f), in_specs=in_specs, out_specs=out_specs, out_shape=out_shape,
        scratch_shapes=[pltpu.VMEM((tm, D_MODEL), F32)],
        compiler_params=_cparams(sem),
    )(*args)


def _ffn_gate_up(x, nw, wg, wu, *, name):
    t = x.shape[0]
    tm, fb = _tm(t), FF_BLOCK_FWD

    def body(x_ref, nw_ref, wg_ref, wu_ref, h_ref, g_ref, u_ref):
        @pl.when(pl.program_id(1) == 0)
        def _():
            h_ref[...] = _rms_fwd(x_ref[...], nw_ref[...])[0].astype(BF16)

        nt = (((1,), (1,)), ((), ()))
        g_ref[...] = lax.dot_general(h_ref[...], wg_ref[...], nt, preferred_element_type=F32).astype(BF16)
        u_ref[...] = lax.dot_general(h_ref[...], wu_ref[...], nt, preferred_element_type=F32).astype(BF16)

    row = lambda i, j: (i, 0)
    wide = pl.BlockSpec((tm, fb), lambda i, j: (i, j))
    wspec = pl.BlockSpec((fb, D_MODEL), lambda i, j: (j, 0))
    return pl.pallas_call(
        body, name=name, grid=(t // tm, D_FF // fb),
        in_specs=[pl.BlockSpec((tm, D_MODEL), row), pl.BlockSpec((1, D_MODEL), lambda i, j: (0, 0)), wspec, wspec],
        out_specs=(pl.BlockSpec((tm, D_MODEL), row), wide, wide),
        out_shape=(jax.ShapeDtypeStruct((t, D_MODEL), BF16), jax.ShapeDtypeStruct((t, D_FF), BF16),
                   jax.ShapeDtypeStruct((t, D_FF), BF16)),
        compiler_params=_cparams(("parallel", "arbitrary")),
    )(x, nw, wg, wu)


def _ffn_down(x, g, u, wd, *, name):
    t = x.shape[0]
    tm, fb = _tm(t), FF_BLOCK_FWD
    n_f = D_FF // fb

    def body(x_ref, g_ref, u_ref, wd_ref, y_ref, acc_s):
        j = pl.program_id(1)
        a = (_silu(g_ref[...].astype(F32)) * u_ref[...].astype(F32)).astype(BF16)
        part = jnp.dot(a, wd_ref[...], preferred_element_type=F32)
        acc_s[...] = jnp.where(j == 0, 0.0, acc_s[...]) + part

        @pl.when(j == n_f - 1)
        def _():
            y_ref[...] = x_ref[...] + 0.5 * acc_s[...]

    row = lambda i, j: (i, 0)
    wide = pl.BlockSpec((tm, fb), lambda i, j: (i, j))
    return pl.pallas_call(
        body, name=name, grid=(t // tm, n_f),
        in_specs=[pl.BlockSpec((tm, D_MODEL), row), wide, wide, pl.BlockSpec((fb, D_MODEL), lambda i, j: (j, 0))],
        out_specs=pl.BlockSpec((tm, D_MODEL), row),
        out_shape=jax.ShapeDtypeStruct((t, D_MODEL), F32),
        scratch_shapes=[pltpu.VMEM((tm, D_MODEL), F32)],
        compiler_params=_cparams(("parallel", "arbitrary")),
    )(x, g, u, wd)


def _ffn_bwd_x(x, nw, g, u, wg, wu, wd, dy, *, name):
    t = x.shape[0]
    tm = _tm(t, 256)

    def body(x_ref, nw_ref, g_ref, u_ref, wg_ref, wu_ref, wd_ref, dy_ref, dx_ref, dnw_ref, dg_ref, du_ref, a_ref, dyh_ref):
        i = pl.program_id(0)
        nt = (((1,), (1,)), ((), ()))
        dy = dy_ref[...]
        dyh = (0.5 * dy).astype(BF16)
        dyh_ref[...] = dyh
        gate, up = g_ref[...].astype(F32), u_ref[...].astype(F32)
        s = _sigmoid(gate)
        gs = gate * s
        da = lax.dot_general(dyh, wd_ref[...], nt, preferred_element_type=F32)
        dg = (da * up * (s + gs * (1.0 - s))).astype(BF16)
        du = (da * gs).astype(BF16)
        dg_ref[...] = dg
        du_ref[...] = du
        a_ref[...] = (gs * up).astype(BF16)
        dh = (jnp.dot(dg, wg_ref[...], preferred_element_type=F32)
              + jnp.dot(du, wu_ref[...], preferred_element_type=F32))
        xv = x_ref[...]
        r = lax.rsqrt(jnp.mean(xv * xv, axis=-1, keepdims=True) + EPS)
        dx, dnw = _rms_bwd(dh, xv * r, r, nw_ref[...])
        dx_ref[...] = dy + dx
        _acc_out(dnw_ref, i == 0, dnw)

    row = lambda i: (i, 0)
    const = lambda i: (0, 0)
    once = pl.Buffered(1)
    wide = pl.BlockSpec((tm, D_FF), row)
    return pl.pallas_call(
        body, name=name, grid=(t // tm,),
        in_specs=[pl.BlockSpec((tm, D_MODEL), row), pl.BlockSpec((1, D_MODEL), const), wide, wide,
                  pl.BlockSpec((D_FF, D_MODEL), const, pipeline_mode=once), pl.BlockSpec((D_FF, D_MODEL), const, pipeline_mode=once),
                  pl.BlockSpec((D_FF, D_MODEL), const, pipeline_mode=once), pl.BlockSpec((tm, D_MODEL), row)],
        out_specs=(pl.BlockSpec((tm, D_MODEL), row), pl.BlockSpec((1, D_MODEL), const), wide, wide, wide,
                   pl.BlockSpec((tm, D_MODEL), row)),
        out_shape=(jax.ShapeDtypeStruct((t, D_MODEL), F32), jax.ShapeDtypeStruct((1, D_MODEL), F32),
                   jax.ShapeDtypeStruct((t, D_FF), BF16), jax.ShapeDtypeStruct((t, D_FF), BF16),
                   jax.ShapeDtypeStruct((t, D_FF), BF16), jax.ShapeDtypeStruct((t, D_MODEL), BF16)),
        compiler_params=_cparams(("arbitrary",)),
    )(x, nw, g, u, wg, wu, wd, dy)


def _wgrad(a, b, bm, bn, after=None, *, name):
    k, m = a.shape
    n = b.shape[1]
    tk = k
    while 2 * 2 * tk * (bm + bn) + (4 + 2 * 2) * bm * bn + 4 * bm * MXU_COLS > WGRAD_VMEM_BUDGET:
        tk //= 2
    n_k = k // tk

    def body(a_ref, b_ref, *rest):
        o_ref, acc_s = rest[-2], rest[-1]
        s = pl.program_id(2)
        for c in range(bn // MXU_COLS):
            cols = slice(c * MXU_COLS, (c + 1) * MXU_COLS)
            part = lax.dot_general(a_ref[...], b_ref[:, cols], (((0,), (0,)), ((), ())), preferred_element_type=F32)
            acc_s[:, cols] = jnp.where(s == 0, 0.0, acc_s[:, cols]) + part

        @pl.when(s == n_k - 1)
        def _():
            o_ref[...] = acc_s[...].astype(BF16)

    return pl.pallas_call(
        body, name=name, grid=(m // bm, n // bn, n_k),
        in_specs=[pl.BlockSpec((tk, bm), lambda i, j, s: (s, i)), pl.BlockSpec((tk, bn), lambda i, j, s: (s, j))]
        + ([] if after is None else [_HBM]),
        out_specs=pl.BlockSpec((bm, bn), lambda i, j, s: (i, j)),
        out_shape=jax.ShapeDtypeStruct((m, n), BF16),
        scratch_shapes=[pltpu.VMEM((bm, bn), F32)],
        compiler_params=_cparams(("parallel", "parallel", "arbitrary")),
    )(a, b, *([] if after is None else [after]))


def _ffn_wgrads(h, dg, du, a, dyh, between=None, after=None, *, name):
    grads = []
    for k, (lhs, rhs, tag) in enumerate(((dg, h, "_wg"), (du, h, "_wu"), (a, dyh, "_wd"))):
        grads.append(_wgrad(lhs, rhs, D_FF // 2, D_MODEL, after, name=name + tag))
        after = None if between is None else between(k, grads[-1])
    return grads


def _ffn_bwd(x, nw, h, g, u, wg, wu, wd, dy, *, name):
    dx, dnw, dg, du, a, dyh = _ffn_bwd_x(x, nw, g, u, wg, wu, wd, dy, name=name + "_x")
    return (dx, dnw, *_ffn_wgrads(h, dg, du, a, dyh, name=name))


_PROJ_WIDTHS = (SG_WIDTH, SG_WIDTH, 3 * DN_WIDTH, DN_WIDTH, LANES, LANES)


def _mix_in_fwd(x, nw, ws, *, name):
    t = x.shape[0]
    tm = _tm(t)

    def body(x_ref, nw_ref, *refs):
        w_refs, o_refs = refs[:6], refs[6:]
        h, _, _ = _rms_fwd(x_ref[...], nw_ref[...])
        h = h.astype(BF16)
        for w_ref, o_ref in zip(w_refs, o_refs):
            o_ref[...] = lax.dot_general(h, w_ref[...], (((1,), (1,)), ((), ())), preferred_element_type=F32)

    row = lambda i: (i, 0)
    const = lambda i: (0, 0)
    return pl.pallas_call(
        body, name=name, grid=(t // tm,),
        in_specs=[pl.BlockSpec((tm, D_MODEL), row), pl.BlockSpec((1, D_MODEL), const)]
        + [pl.BlockSpec((n, D_MODEL), const) for n in _PROJ_WIDTHS],
        out_specs=tuple(pl.BlockSpec((tm, n), row) for n in _PROJ_WIDTHS),
        out_shape=tuple(jax.ShapeDtypeStruct((t, n), F32) for n in _PROJ_WIDTHS),
        compiler_params=_cparams(("parallel",)),
    )(x, nw, *ws)


_PROJ_TOTAL = sum(_PROJ_WIDTHS)
_PROJ_OFFSETS = tuple(sum(_PROJ_WIDTHS[:k]) for k in range(len(_PROJ_WIDTHS)))


def _mix_in_bwd(x, nw, ws, dres, dps, *, name):
    t = x.shape[0]
    tm = _tm(t, 256)

    def body(x_ref, nw_ref, dres_ref, *refs):
        w_refs, dp_refs, dx_ref, dnw_ref, h_ref, dpb_ref = refs[:6], refs[6:12], refs[12], refs[13], refs[14], refs[15]
        i = pl.program_id(0)
        hf, xh, r = _rms_fwd(x_ref[...], nw_ref[...])
        h_ref[...] = hf.astype(BF16)
        dh = jnp.zeros((tm, D_MODEL), F32)
        for w_ref, dp_ref, off, width in zip(w_refs, dp_refs, _PROJ_OFFSETS, _PROJ_WIDTHS):
            dp = dp_ref[...].astype(BF16)
            dpb_ref[:, off:off + width] = dp
            dh = dh + jnp.dot(dp, w_ref[...], preferred_element_type=F32)
        dx, dnw = _rms_bwd(dh, xh, r, nw_ref[...])
        dx_ref[...] = dres_ref[...] + dx
        _acc_out(dnw_ref, i == 0, dnw)

    row = lambda i: (i, 0)
    const = lambda i: (0, 0)
    dx, dnw, h, dpb = pl.pallas_call(
        body, name=name + "_x", grid=(t // tm,),
        in_specs=[pl.BlockSpec((tm, D_MODEL), row), pl.BlockSpec((1, D_MODEL), const), pl.BlockSpec((tm, D_MODEL), row)]
        + [pl.BlockSpec((n, D_MODEL), const) for n in _PROJ_WIDTHS]
        + [pl.BlockSpec((tm, n), row) for n in _PROJ_WIDTHS],
        out_specs=(pl.BlockSpec((tm, D_MODEL), row), pl.BlockSpec((1, D_MODEL), const), pl.BlockSpec((tm, D_MODEL), row),
                   pl.BlockSpec((tm, _PROJ_TOTAL), row)),
        out_shape=(jax.ShapeDtypeStruct((t, D_MODEL), F32), jax.ShapeDtypeStruct((1, D_MODEL), F32),
                   jax.ShapeDtypeStruct((t, D_MODEL), BF16), jax.ShapeDtypeStruct((t, _PROJ_TOTAL), BF16)),
        compiler_params=_cparams(("arbitrary",)),
    )(x, nw, dres, *ws, *dps)
    return dx, dnw, _wgrad(dpb, h, _PROJ_TOTAL // 2, D_MODEL, name=name + "_w")


def _sg_fn(u, v, lng, lnb, wcs, sgbt):
    lane = lax.broadcasted_iota(jnp.int32, (1, SG_WIDTH), 1)
    lane_b = lax.broadcasted_iota(jnp.int32, (1, LANES), 1)
    rr = lax.broadcasted_iota(jnp.int32, (SG_CHUNK, SG_CHUNK), 0)
    cc = lax.broadcasted_iota(jnp.int32, (SG_CHUNK, SG_CHUNK), 1)
    gu, gv = _gelu(u), _gelu(v)
    mu = jnp.mean(gv, axis=-1, keepdims=True)
    cen = gv - mu
    var = jnp.mean(cen * cen, axis=-1, keepdims=True)
    ln = cen * lax.rsqrt(var + EPS) * lng + lnb
    vs = jnp.zeros_like(u)
    for g in range(SG_GROUPS):
        in_group = jnp.logical_and(lane >= g * SG_GROUP_DIM, lane < (g + 1) * SG_GROUP_DIM)
        w_causal = jnp.where(rr >= cc, wcs[g], 0.0)
        bias = jnp.sum(jnp.where(lane_b == g, sgbt, 0.0), axis=1, keepdims=True)
        vs = vs + jnp.where(in_group, mm(w_causal, ln) + bias, 0.0)
    return gu * vs


def _sg_fwd(u, v, lng, lnb, wc, sgbt, *, name):
    t = u.shape[0]
    tm = _tm(t)

    def body(u_ref, v_ref, lng_ref, lnb_ref, wc_ref, sgbt_ref, o_ref):
        wcs = [wc_ref[g] for g in range(SG_GROUPS)]
        for c in range(tm // SG_CHUNK):
            rows = pl.ds(c * SG_CHUNK, SG_CHUNK)
            o_ref[rows, :] = _sg_fn(u_ref[rows, :], v_ref[rows, :], lng_ref[...], lnb_ref[...], wcs, sgbt_ref[...])

    row = lambda i: (i, 0)
    const = lambda i: (0, 0)
    return pl.pallas_call(
        body, name=name, grid=(t // tm,),
        in_specs=[pl.BlockSpec((tm, SG_WIDTH), row), pl.BlockSpec((tm, SG_WIDTH), row),
                  pl.BlockSpec((1, SG_WIDTH), const), pl.BlockSpec((1, SG_WIDTH), const),
                  pl.BlockSpec((SG_GROUPS, SG_CHUNK, SG_CHUNK), lambda i: (0, 0, 0)), pl.BlockSpec((SG_CHUNK, LANES), const)],
        out_specs=pl.BlockSpec((tm, SG_WIDTH), row),
        out_shape=jax.ShapeDtypeStruct((t, SG_WIDTH), F32),
        compiler_params=_cparams(("parallel",)),
    )(u, v, lng, lnb, wc, sgbt)


def _sg_bwd(u, v, lng, lnb, wc, sgbt, dout, *, name):
    t = u.shape[0]
    tm = _tm(t)

    def body(u_ref, v_ref, lng_ref, lnb_ref, wc_ref, sgbt_ref, do_ref, du_ref, dv_ref, dlng_ref, dlnb_ref, dwc_ref, dsgbt_ref):
        i = pl.program_id(0)
        wcs = [wc_ref[g] for g in range(SG_GROUPS)]
        tot = None
        for c in range(tm // SG_CHUNK):
            rows = pl.ds(c * SG_CHUNK, SG_CHUNK)
            _, vjp = jax.vjp(_sg_fn, u_ref[rows, :], v_ref[rows, :], lng_ref[...], lnb_ref[...], wcs, sgbt_ref[...])
            du, dv, dlng, dlnb, dwcs, dsgbt = vjp(do_ref[rows, :])
            du_ref[rows, :] = du.astype(BF16)
            dv_ref[rows, :] = dv.astype(BF16)
            part = (dlng, dlnb, dwcs, dsgbt)
            tot = part if tot is None else jax.tree.map(jnp.add, tot, part)
        dlng, dlnb, dwcs, dsgbt = tot
        _acc_out(dlng_ref, i == 0, dlng)
        _acc_out(dlnb_ref, i == 0, dlnb)
        _acc_out(dsgbt_ref, i == 0, dsgbt)
        for g in range(SG_GROUPS):
            @pl.when(i == 0)
            def _(g=g):
                dwc_ref[g] = dwcs[g]

            @pl.when(i > 0)
            def _(g=g):
                dwc_ref[g] += dwcs[g]

    row = lambda i: (i, 0)
    const = lambda i: (0, 0)
    wspec = pl.BlockSpec((SG_GROUPS, SG_CHUNK, SG_CHUNK), lambda i: (0, 0, 0))
    return pl.pallas_call(
        body, name=name, grid=(t // tm,),
        in_specs=[pl.BlockSpec((tm, SG_WIDTH), row), pl.BlockSpec((tm, SG_WIDTH), row),
                  pl.BlockSpec((1, SG_WIDTH), const), pl.BlockSpec((1, SG_WIDTH), const), wspec,
                  pl.BlockSpec((SG_CHUNK, LANES), const), pl.BlockSpec((tm, SG_WIDTH), row)],
        out_specs=(pl.BlockSpec((tm, SG_WIDTH), row), pl.BlockSpec((tm, SG_WIDTH), row),
                   pl.BlockSpec((1, SG_WIDTH), const), pl.BlockSpec((1, SG_WIDTH), const), wspec,
                   pl.BlockSpec((SG_CHUNK, LANES), const)),
        out_shape=(jax.ShapeDtypeStruct((t, SG_WIDTH), BF16), jax.ShapeDtypeStruct((t, SG_WIDTH), BF16),
                   jax.ShapeDtypeStruct((1, SG_WIDTH), F32), jax.ShapeDtypeStruct((1, SG_WIDTH), F32),
                   jax.ShapeDtypeStruct((SG_GROUPS, SG_CHUNK, SG_CHUNK), F32), jax.ShapeDtypeStruct((SG_CHUNK, LANES), F32)),
        compiler_params=_cparams(("arbitrary",)),
    )(u, v, lng, lnb, wc, sgbt, dout)


def _conv_taps(ext, w, tm):
    y = None
    for j in range(CONV_K):
        s = CONV_K - 1 - j
        shifted = ext if s == 0 else pltpu.roll(ext, s, 0)
        term = w[j:j + 1, :] * shifted[HALO:HALO + tm, :]
        y = term if y is None else y + term
    return y


def _post_conv(yq, yk, yv, bpre, apre, alog, dtb):
    def l2(a):
        return a * lax.rsqrt(jnp.sum(a * a, axis=-1, keepdims=True) + EPS)

    q = [l2(_silu(a)) for a in yq]
    k = [l2(_silu(a)) for a in yk]
    return q, k, _silu(yv), _sigmoid(bpre), -jnp.exp(alog) * _softplus(apre + dtb)


def _chunk_tril(tm):
    rr = lax.broadcasted_iota(jnp.int32, (tm, tm), 0)
    cc = lax.broadcasted_iota(jnp.int32, (tm, tm), 1)
    shift = DN_CHUNK.bit_length() - 1
    same = jnp.right_shift(rr, shift) == jnp.right_shift(cc, shift)
    return jnp.where(jnp.logical_and(same, rr >= cc), 1.0, 0.0).astype(F32)


def _halo_specs(tm, width, n_blocks_seq, n_blocks):
    per = tm // HALO
    prev = pl.BlockSpec((HALO, width), lambda i: (jnp.maximum(i * per - 1, 0), 0))
    nxt = pl.BlockSpec((HALO, width), lambda i: (jnp.minimum((i + 1) * per, n_blocks * per - 1), 0))
    return prev, nxt


def _split_heads(ref, base):
    return [ref[:, base + h * DN_HEAD_DIM: base + (h + 1) * DN_HEAD_DIM] for h in range(DN_HEADS)]


def _dn_prep_fwd(qkv, bpre, apre, conv_w, alog, dtb, seq, *, name):
    t = qkv.shape[0]
    tm = _tm(t)
    bps = seq // tm
    cw = 3 * DN_WIDTH

    def body(x_ref, halo_ref, b_ref, a_ref, w_ref, alog_ref, dtb_ref, q_ref, k_ref, v_ref, beta_ref, gc_ref):
        i = pl.program_id(0)
        keep = jnp.where(i % bps == 0, 0.0, 1.0)
        ext = jnp.concatenate([halo_ref[...] * keep, x_ref[...]], axis=0)
        y = _conv_taps(ext, w_ref[...], tm)
        yq = [y[:, h * DN_HEAD_DIM:(h + 1) * DN_HEAD_DIM] for h in range(DN_HEADS)]
        yk = [y[:, DN_WIDTH + h * DN_HEAD_DIM: DN_WIDTH + (h + 1) * DN_HEAD_DIM] for h in range(DN_HEADS)]
        q, k, v, beta, g = _post_conv(yq, yk, y[:, 2 * DN_WIDTH:], b_ref[...], a_ref[...], alog_ref[...], dtb_ref[...])
        for h in range(DN_HEADS):
            q_ref[:, h * DN_HEAD_DIM:(h + 1) * DN_HEAD_DIM] = q[h]
            k_ref[:, h * DN_HEAD_DIM:(h + 1) * DN_HEAD_DIM] = k[h]
        v_ref[...] = v
        beta_ref[...] = beta
        gc_ref[...] = mmx(_chunk_tril(tm), g)

    row = lambda i: (i, 0)
    const = lambda i: (0, 0)
    prev, _ = _halo_specs(tm, cw, bps, t // tm)
    return pl.pallas_call(
        body, name=name, grid=(t // tm,),
        in_specs=[pl.BlockSpec((tm, cw), row), prev, pl.BlockSpec((tm, LANES), row), pl.BlockSpec((tm, LANES), row),
                  pl.BlockSpec((CONV_K, cw), const), pl.BlockSpec((1, LANES), const), pl.BlockSpec((1, LANES), const)],
        out_specs=tuple(pl.BlockSpec((tm, n), row) for n in (DN_WIDTH, DN_WIDTH, DN_WIDTH, LANES, LANES)),
        out_shape=tuple(jax.ShapeDtypeStruct((t, n), F32) for n in (DN_WIDTH, DN_WIDTH, DN_WIDTH, LANES, LANES)),
        compiler_params=_cparams(("parallel",)),
    )(qkv, qkv, bpre, apre, conv_w, alog, dtb)


def _dn_prep_bwd(qkv, bpre, apre, conv_w, alog, dtb, dq, dk, dv, dbeta, dgc, dgc2, seq, *, name):
    t = qkv.shape[0]
    tm = _tm(t)
    bps = seq // tm
    cw = 3 * DN_WIDTH

    def body(x_ref, halo_ref, b_ref, a_ref, w_ref, alog_ref, dtb_ref, dq_ref, dk_ref, dv_ref, dbeta_ref, dgc_ref, dgc2_ref,
             dy_ref, db_ref, da_ref, dalog_ref, ddtb_ref):
        i = pl.program_id(0)
        keep = jnp.where(i % bps == 0, 0.0, 1.0)
        ext = jnp.concatenate([halo_ref[...] * keep, x_ref[...]], axis=0)
        y = _conv_taps(ext, w_ref[...], tm)
        yq = [y[:, h * DN_HEAD_DIM:(h + 1) * DN_HEAD_DIM] for h in range(DN_HEADS)]
        yk = [y[:, DN_WIDTH + h * DN_HEAD_DIM: DN_WIDTH + (h + 1) * DN_HEAD_DIM] for h in range(DN_HEADS)]
        _, vjp = jax.vjp(_post_conv, yq, yk, y[:, 2 * DN_WIDTH:], b_ref[...], a_ref[...], alog_ref[...], dtb_ref[...])
        dg = mmx_tn(_chunk_tril(tm), dgc_ref[...] + dgc2_ref[...])
        dyq, dyk, dyv, db, da, dalog, ddtb = vjp((_split_heads(dq_ref, 0), _split_heads(dk_ref, 0), dv_ref[...],
                                                  dbeta_ref[...], dg))
        for h in range(DN_HEADS):
            dy_ref[:, h * DN_HEAD_DIM:(h + 1) * DN_HEAD_DIM] = dyq[h]
            dy_ref[:, DN_WIDTH + h * DN_HEAD_DIM: DN_WIDTH + (h + 1) * DN_HEAD_DIM] = dyk[h]
        dy_ref[:, 2 * DN_WIDTH:] = dyv
        db_ref[...] = db.astype(BF16)
        da_ref[...] = da.astype(BF16)
        _acc_out(dalog_ref, i == 0, dalog)
        _acc_out(ddtb_ref, i == 0, ddtb)

    row = lambda i: (i, 0)
    const = lambda i: (0, 0)
    prev, _ = _halo_specs(tm, cw, bps, t // tm)
    return pl.pallas_call(
        body, name=name, grid=(t // tm,),
        in_specs=[pl.BlockSpec((tm, cw), row), prev, pl.BlockSpec((tm, LANES), row), pl.BlockSpec((tm, LANES), row),
                  pl.BlockSpec((CONV_K, cw), const), pl.BlockSpec((1, LANES), const), pl.BlockSpec((1, LANES), const),
                  pl.BlockSpec((tm, DN_WIDTH), row), pl.BlockSpec((tm, DN_WIDTH), row), pl.BlockSpec((tm, DN_WIDTH), row),
                  pl.BlockSpec((tm, LANES), row), pl.BlockSpec((tm, LANES), row), pl.BlockSpec((tm, LANES), row)],
        out_specs=(pl.BlockSpec((tm, cw), row), pl.BlockSpec((tm, LANES), row), pl.BlockSpec((tm, LANES), row),
                   pl.BlockSpec((1, LANES), const), pl.BlockSpec((1, LANES), const)),
        out_shape=(jax.ShapeDtypeStruct((t, cw), F32), jax.ShapeDtypeStruct((t, LANES), BF16), jax.ShapeDtypeStruct((t, LANES), BF16),
                   jax.ShapeDtypeStruct((1, LANES), F32), jax.ShapeDtypeStruct((1, LANES), F32)),
        compiler_params=_cparams(("arbitrary",)),
    )(qkv, qkv, bpre, apre, conv_w, alog, dtb, dq, dk, dv, dbeta, dgc, dgc2)


def _conv_bwd(qkv, dy, conv_w, seq, *, name):
    t = qkv.shape[0]
    tm = _tm(t)
    bps = seq // tm
    cw = 3 * DN_WIDTH
    n_ext = tm + HALO

    def body(x_ref, halo_ref, dy_ref, dyn_ref, w_ref, dx_ref, dw_ref):
        i = pl.program_id(0)
        keep_prev = jnp.where(i % bps == 0, 0.0, 1.0)
        keep_next = jnp.where(i % bps == bps - 1, 0.0, 1.0)
        ext = jnp.concatenate([halo_ref[...] * keep_prev, x_ref[...]], axis=0)
        dy = dy_ref[...]
        dyext = jnp.concatenate([dy, dyn_ref[...] * keep_next], axis=0)
        w = w_ref[...]

        @pl.when(i == 0)
        def _():
            dw_ref[...] = jnp.zeros_like(dw_ref)

        dx = None
        for j in range(CONV_K):
            s = CONV_K - 1 - j
            fut = dyext if s == 0 else pltpu.roll(dyext, n_ext - s, 0)
            term = w[j:j + 1, :] * fut[0:tm, :]
            dx = term if dx is None else dx + term
            past = ext if s == 0 else pltpu.roll(ext, s, 0)
            dw_ref[j:j + 1, :] += jnp.sum(dy * past[HALO:HALO + tm, :], axis=0, keepdims=True)
        dx_ref[...] = dx.astype(BF16)

    row = lambda i: (i, 0)
    const = lambda i: (0, 0)
    prev, nxt = _halo_specs(tm, cw, bps, t // tm)
    return pl.pallas_call(
        body, name=name, grid=(t // tm,),
        in_specs=[pl.BlockSpec((tm, cw), row), prev, pl.BlockSpec((tm, cw), row), nxt, pl.BlockSpec((CONV_K, cw), const)],
        out_specs=(pl.BlockSpec((tm, cw), row), pl.BlockSpec((HALO, cw), const)),
        out_shape=(jax.ShapeDtypeStruct((t, cw), BF16), jax.ShapeDtypeStruct((HALO, cw), F32)),
        compiler_params=_cparams(("arbitrary",)),
    )(qkv, qkv, dy, dy, conv_w)


def _inv_unit_lower(l_mats, eye):
    invs = [eye - l for l in l_mats]
    powers = list(l_mats)
    n = 2
    while n < eye.shape[0]:
        powers = [mmh(p, p) for p in powers]
        invs = [inv + mmh(inv, p) for inv, p in zip(invs, powers)]
        n *= 2
    return invs


@jax.custom_vjp
def _solve(l_mat, rhs, inv):
    return mmh(inv, rhs)


def _solve_fwd(l_mat, rhs, inv):
    sol = mmh(inv, rhs)
    return sol, (inv, sol)


def _solve_bwd(res, d_sol):
    inv, sol = res
    d_rhs = mm_tn(inv, d_sol)
    return -mm_nt(d_rhs, sol), d_rhs, jnp.zeros_like(inv)


_solve.defvjp(_solve_fwd, _solve_bwd)


def _prep_fn(q, k, v, gc, gr, b, inv):
    ids = range(len(q))
    c = q[0].shape[0]
    rr = lax.broadcasted_iota(jnp.int32, (c, c), 0)
    cc = lax.broadcasted_iota(jnp.int32, (c, c), 1)
    incl, strict = rr >= cc, rr > cc
    is_last = lax.broadcasted_iota(jnp.int32, (c, 1), 0) == c - 1
    qs = [q[i] * (DN_HEAD_DIM ** -0.5) for i in ids]
    decay = [jnp.where(incl, jnp.exp(jnp.where(incl, gc[i] - gr[i], 0.0)), 0.0) for i in ids]
    kb = [k[i] * b[i] for i in ids]
    vb = [v[i] * b[i] for i in ids]
    kk = [mm_nt(kb[i], k[i]) for i in ids]
    l_mat = [jnp.where(strict, kk[i] * decay[i], 0.0) for i in ids]
    eg = [jnp.exp(gc[i]) for i in ids]
    if inv is None:
        inv = _inv_unit_lower(l_mat, jnp.where(rr == cc, 1.0, 0.0).astype(F32))
    u_wy = [_solve(l_mat[i], vb[i], inv[i]) for i in ids]
    w_wy = [_solve(l_mat[i], kb[i] * eg[i], inv[i]) for i in ids]
    qk = [mm_nt(qs[i], k[i]) * decay[i] for i in ids]
    g_last = [jnp.sum(jnp.where(is_last, gc[i], 0.0), axis=0, keepdims=True) for i in ids]
    k_dec = [k[i] * jnp.exp(g_last[i] - gc[i]) for i in ids]
    egl = [jnp.broadcast_to(jnp.exp(g_last[i]), (1, LANES)) for i in ids]
    return [(w_wy[i], u_wy[i], qs[i] * eg[i], k_dec[i], qk[i], egl[i]) for i in ids], inv


def _seq_fn(w, u, qd, kd, qk, egl, s):
    ids = range(len(w))
    ws = [mm(w[i], s[i]) for i in ids]
    qs = [mm(qd[i], s[i]) for i in ids]
    v_new = [u[i] - ws[i] for i in ids]
    o = [qs[i] + mm(qk[i], v_new[i]) for i in ids]
    s_new = [s[i] * egl[i] + mm_tn(kd[i], v_new[i]) for i in ids]
    return o, s_new


def _lane_col(a, h):
    lane = lax.broadcasted_iota(jnp.int32, (1, LANES), 1)
    return jnp.sum(jnp.where(lane == h, a, 0.0), axis=1, keepdims=True)


def _col_lane(col, h):
    lane = lax.broadcasted_iota(jnp.int32, (1, LANES), 1)
    return jnp.where(lane == h, col, 0.0)


def _head_cols(h):
    return slice(h * DN_HEAD_DIM, (h + 1) * DN_HEAD_DIM)


def _chunk_rows(n):
    return pl.ds(pl.multiple_of(n * DN_CHUNK, DN_CHUNK), DN_CHUNK)


def _delta_prep(q, k, v, gc, grow, beta, *, name):
    t = q.shape[0]
    tm = _tm(t)
    cpb = tm // DN_CHUNK
    n_chunks = t // DN_CHUNK
    group = 2

    def body(q_ref, k_ref, v_ref, gc_ref, gr_ref, b_ref, w_ref, u_ref, qd_ref, kd_ref, qk_ref, egl_ref, inv_ref):
        def step(m, carry):
            probs = [(m * group + e, h) for e in range(group) for h in range(DN_HEADS)]
            gcb = [gc_ref[_chunk_rows(m * group + e), :] for e in range(group)]
            bb = [b_ref[_chunk_rows(m * group + e), :] for e in range(group)]
            grb = [gr_ref[m * group + e] for e in range(group)]
            for e in range(group):
                egl_ref[m * group + e] = jnp.zeros((HALO, LANES), F32)
            outs, invs = _prep_fn(
                [q_ref[_chunk_rows(n), _head_cols(h)] for n, h in probs], [k_ref[_chunk_rows(n), _head_cols(h)] for n, h in probs],
                [v_ref[_chunk_rows(n), _head_cols(h)] for n, h in probs],
                [_lane_col(gcb[e], h) for e in range(group) for h in range(DN_HEADS)],
                [grb[e][h:h + 1, :] for e in range(group) for h in range(DN_HEADS)],
                [_lane_col(bb[e], h) for e in range(group) for h in range(DN_HEADS)], None)
            for (n, h), (w, u, qd, kd, qk, egl), inv in zip(probs, outs, invs):
                rows, cols = _chunk_rows(n), _head_cols(h)
                w_ref[rows, cols] = w.astype(BF16)
                u_ref[rows, cols] = u
                qd_ref[rows, cols] = qd.astype(BF16)
                kd_ref[rows, cols] = kd.astype(BF16)
                qk_ref[n, h] = qk
                inv_ref[n, h] = inv
                egl_ref[n, h:h + 1, :] = egl
            return carry

        lax.fori_loop(0, cpb // group, step, 0)

    row = lambda i: (i, 0)
    tok = pl.BlockSpec((tm, DN_WIDTH), row)
    lanes = pl.BlockSpec((tm, LANES), row)
    sq = pl.BlockSpec((cpb, DN_HEADS, DN_CHUNK, DN_CHUNK), lambda i: (i, 0, 0, 0))
    return pl.pallas_call(
        body, name=name, grid=(t // tm,),
        in_specs=[tok, tok, tok, lanes, pl.BlockSpec((cpb, HALO, DN_CHUNK), lambda i: (i, 0, 0)), lanes],
        out_specs=(tok, tok, tok, tok, sq, pl.BlockSpec((cpb, HALO, LANES), lambda i: (i, 0, 0)), sq),
        out_shape=(jax.ShapeDtypeStruct((t, DN_WIDTH), BF16), jax.ShapeDtypeStruct((t, DN_WIDTH), F32),
                   jax.ShapeDtypeStruct((t, DN_WIDTH), BF16), jax.ShapeDtypeStruct((t, DN_WIDTH), BF16),
                   jax.ShapeDtypeStruct((n_chunks, DN_HEADS, DN_CHUNK, DN_CHUNK), F32),
                   jax.ShapeDtypeStruct((n_chunks, HALO, LANES), F32),
                   jax.ShapeDtypeStruct((n_chunks, DN_HEADS, DN_CHUNK, DN_CHUNK), F32)),
        compiler_params=_cparams(("parallel",)),
    )(q, k, v, gc, grow, beta)


def _delta_par_bwd(q, k, v, gc, grow, beta, inv, dw, du, dqd, dkd, dqk, degl, *, name):
    t = q.shape[0]
    tm = _tm(t)
    cpb = tm // DN_CHUNK
    n_chunks = t // DN_CHUNK
    group = 2

    def body(q_ref, k_ref, v_ref, gc_ref, gr_ref, b_ref, inv_ref, dw_ref, du_ref, dqd_ref, dkd_ref, dqk_ref, degl_ref,
             dq_ref, dk_ref, dv_ref, dgc_ref, dgr_ref, db_ref):
        def step(m, carry):
            chunks = [m * group + e for e in range(group)]
            probs = [(e, h) for e in range(group) for h in range(DN_HEADS)]
            rows = [_chunk_rows(n) for n in chunks]
            gcb, bb = [gc_ref[r, :] for r in rows], [b_ref[r, :] for r in rows]
            grb, deglb = [gr_ref[n] for n in chunks], [degl_ref[n] for n in chunks]
            for n in chunks:
                dgr_ref[n] = jnp.zeros((HALO, DN_CHUNK), F32)
            invs = [inv_ref[chunks[e], h] for e, h in probs]
            _, vjp = jax.vjp(lambda *a: _prep_fn(*a, invs)[0],
                             [q_ref[rows[e], _head_cols(h)] for e, h in probs], [k_ref[rows[e], _head_cols(h)] for e, h in probs],
                             [v_ref[rows[e], _head_cols(h)] for e, h in probs], [_lane_col(gcb[e], h) for e, h in probs],
                             [grb[e][h:h + 1, :] for e, h in probs], [_lane_col(bb[e], h) for e, h in probs])
            dq, dk, dv, dgc, dgr, db = vjp([(dw_ref[rows[e], _head_cols(h)], du_ref[rows[e], _head_cols(h)],
                                             dqd_ref[rows[e], _head_cols(h)], dkd_ref[rows[e], _head_cols(h)],
                                             dqk_ref[chunks[e], h], deglb[e][h:h + 1, :]) for e, h in probs])
            dgc_acc = [jnp.zeros((DN_CHUNK, LANES), F32) for _ in chunks]
            db_acc = [jnp.zeros((DN_CHUNK, LANES), F32) for _ in chunks]
            for i, (e, h) in enumerate(probs):
                cols = _head_cols(h)
                dq_ref[rows[e], cols] = dq[i]
                dk_ref[rows[e], cols] = dk[i]
                dv_ref[rows[e], cols] = dv[i]
                dgr_ref[chunks[e], h:h + 1, :] = dgr[i]
                dgc_acc[e] = dgc_acc[e] + _col_lane(dgc[i], h)
                db_acc[e] = db_acc[e] + _col_lane(db[i], h)
            for e in range(group):
                dgc_ref[rows[e], :] = dgc_acc[e]
                db_ref[rows[e], :] = db_acc[e]
            return carry

        lax.fori_loop(0, cpb // group, step, 0)

    row = lambda i: (i, 0)
    tok = pl.BlockSpec((tm, DN_WIDTH), row)
    lanes = pl.BlockSpec((tm, LANES), row)
    sq = pl.BlockSpec((cpb, DN_HEADS, DN_CHUNK, DN_CHUNK), lambda i: (i, 0, 0, 0))
    grs = pl.BlockSpec((cpb, HALO, DN_CHUNK), lambda i: (i, 0, 0))
    return pl.pallas_call(
        body, name=name, grid=(t // tm,),
        in_specs=[tok, tok, tok, lanes, grs, lanes, sq, tok, tok, tok, tok, sq, pl.BlockSpec((cpb, HALO, LANES), lambda i: (i, 0, 0))],
        out_specs=(tok, tok, tok, lanes, grs, lanes),
        out_shape=(jax.ShapeDtypeStruct((t, DN_WIDTH), F32),) * 3
        + (jax.ShapeDtypeStruct((t, LANES), F32), jax.ShapeDtypeStruct((n_chunks, HALO, DN_CHUNK), F32),
           jax.ShapeDtypeStruct((t, LANES), F32)),
        compiler_params=_cparams(("parallel",)),
    )(q, k, v, gc, grow, beta, inv, dw, du, dqd, dkd, dqk, degl)


def _seq_specs(n_seq, seq, reverse):
    tm = _tm(seq)
    nb = seq // tm
    cpb = tm // DN_CHUNK
    pair = 2 if n_seq % 2 == 0 else 1
    blk = (lambda j: nb - 1 - j) if reverse else (lambda j: j)
    tok = pl.BlockSpec((pair, tm, DN_WIDTH), lambda b, j: (b, blk(j), 0))
    sq = pl.BlockSpec((pair, cpb, DN_HEADS, DN_CHUNK, DN_CHUNK), lambda b, j: (b, blk(j), 0, 0, 0))
    rows8 = pl.BlockSpec((pair, cpb, HALO, LANES), lambda b, j: (b, blk(j), 0, 0))
    state = pl.BlockSpec((pair, cpb, DN_HEADS, DN_HEAD_DIM, DN_HEAD_DIM), lambda b, j: (b, blk(j), 0, 0, 0))
    return nb, cpb, pair, tok, sq, rows8, state


def _by_seq(a, n_seq):
    return a.reshape((n_seq, a.shape[0] // n_seq) + a.shape[1:])


def _flat_seq(a):
    return a.reshape((a.shape[0] * a.shape[1],) + a.shape[2:])


def _delta_seq_fwd(w, u, qd, kd, qk, egl, n_seq, seq, *, name):
    nb, cpb, pair, tok, sq, rows8, state = _seq_specs(n_seq, seq, False)
    probs = [(e, h) for e in range(pair) for h in range(DN_HEADS)]

    def body(w_ref, u_ref, qd_ref, kd_ref, qk_ref, egl_ref, o_ref, st_ref, s_s):
        @pl.when(pl.program_id(1) == 0)
        def _():
            s_s[...] = jnp.zeros_like(s_s)

        def step(n, carry):
            rows = _chunk_rows(n)
            eglb = [egl_ref[e, n] for e in range(pair)]
            s = [s_s[e, h] for e, h in probs]
            for (e, h), s_eh in zip(probs, s):
                st_ref[e, n, h] = s_eh
            o, s_new = _seq_fn([w_ref[e, rows, _head_cols(h)] for e, h in probs], [u_ref[e, rows, _head_cols(h)] for e, h in probs],
                               [qd_ref[e, rows, _head_cols(h)] for e, h in probs], [kd_ref[e, rows, _head_cols(h)] for e, h in probs],
                               [qk_ref[e, n, h] for e, h in probs], [eglb[e][h:h + 1, :] for e, h in probs], s)
            for i, (e, h) in enumerate(probs):
                o_ref[e, rows, _head_cols(h)] = o[i]
                s_s[e, h] = s_new[i]
            return carry

        lax.fori_loop(0, cpb, step, 0)

    o, states = pl.pallas_call(
        body, name=name, grid=(n_seq // pair, nb),
        in_specs=[tok, tok, tok, tok, sq, rows8],
        out_specs=(tok, state),
        out_shape=(jax.ShapeDtypeStruct((n_seq, seq, DN_WIDTH), F32),
                   jax.ShapeDtypeStruct((n_seq, seq // DN_CHUNK, DN_HEADS, DN_HEAD_DIM, DN_HEAD_DIM), F32)),
        scratch_shapes=[pltpu.VMEM((pair, DN_HEADS, DN_HEAD_DIM, DN_HEAD_DIM), F32)],
        compiler_params=_cparams(("parallel", "arbitrary")),
    )(*[_by_seq(a, n_seq) for a in (w, u, qd, kd, qk, egl)])
    return _flat_seq(o), _flat_seq(states)


def _delta_seq_bwd(w, u, qd, kd, qk, egl, states, do, n_seq, seq, *, name):
    nb, cpb, pair, tok, sq, rows8, state = _seq_specs(n_seq, seq, True)
    probs = [(e, h) for e in range(pair) for h in range(DN_HEADS)]

    def body(w_ref, u_ref, qd_ref, kd_ref, qk_ref, egl_ref, st_ref, do_ref, dw_ref, du_ref, dqd_ref, dkd_ref, dqk_ref,
             degl_ref, ds_s):
        @pl.when(pl.program_id(1) == 0)
        def _():
            ds_s[...] = jnp.zeros_like(ds_s)

        def step(m, carry):
            n = cpb - 1 - m
            rows = _chunk_rows(n)
            eglb = [egl_ref[e, n] for e in range(pair)]
            for e in range(pair):
                degl_ref[e, n] = jnp.zeros((HALO, LANES), F32)
            _, vjp = jax.vjp(_seq_fn, [w_ref[e, rows, _head_cols(h)].astype(F32) for e, h in probs],
                             [u_ref[e, rows, _head_cols(h)] for e, h in probs],
                             [qd_ref[e, rows, _head_cols(h)].astype(F32) for e, h in probs],
                             [kd_ref[e, rows, _head_cols(h)].astype(F32) for e, h in probs],
                             [qk_ref[e, n, h] for e, h in probs], [eglb[e][h:h + 1, :] for e, h in probs],
                             [st_ref[e, n, h] for e, h in probs])
            dw, du, dqd, dkd, dqk, degl, ds_in = vjp(([do_ref[e, rows, _head_cols(h)] for e, h in probs],
                                                      [ds_s[e, h] for e, h in probs]))
            for i, (e, h) in enumerate(probs):
                cols = _head_cols(h)
                dw_ref[e, rows, cols] = dw[i]
                du_ref[e, rows, cols] = du[i]
                dqd_ref[e, rows, cols] = dqd[i]
                dkd_ref[e, rows, cols] = dkd[i]
                dqk_ref[e, n, h] = dqk[i]
                degl_ref[e, n, h:h + 1, :] = degl[i]
                ds_s[e, h] = ds_in[i]
            return carry

        lax.fori_loop(0, cpb, step, 0)

    nc = seq // DN_CHUNK
    outs = pl.pallas_call(
        body, name=name, grid=(n_seq // pair, nb),
        in_specs=[tok, tok, tok, tok, sq, rows8, state, tok],
        out_specs=(tok, tok, tok, tok, sq, rows8),
        out_shape=(jax.ShapeDtypeStruct((n_seq, seq, DN_WIDTH), F32),) * 4
        + (jax.ShapeDtypeStruct((n_seq, nc, DN_HEADS, DN_CHUNK, DN_CHUNK), F32),
           jax.ShapeDtypeStruct((n_seq, nc, HALO, LANES), F32)),
        scratch_shapes=[pltpu.VMEM((pair, DN_HEADS, DN_HEAD_DIM, DN_HEAD_DIM), F32)],
        compiler_params=_cparams(("parallel", "arbitrary")),
    )(*[_by_seq(a, n_seq) for a in (w, u, qd, kd, qk, egl, states, do)])
    return tuple(_flat_seq(a) for a in outs)


def _dn_gate(o, z, dnw):
    return o * lax.rsqrt(jnp.mean(o * o, axis=-1, keepdims=True) + EPS) * dnw * _silu(z)


def _mix_out_fwd(x, sg, o, z, wo_sg, wo_dn, dnw, *, name):
    t = x.shape[0]
    tm = _tm(t)

    def body(x_ref, sg_ref, o_ref, z_ref, wsg_ref, wdn_ref, dnw_ref, y_ref, dn_s):
        for h, (oh, zh) in enumerate(zip(_split_heads(o_ref, 0), _split_heads(z_ref, 0))):
            dn_s[:, h * DN_HEAD_DIM:(h + 1) * DN_HEAD_DIM] = _dn_gate(oh, zh, dnw_ref[...]).astype(BF16)
        y_ref[...] = (x_ref[...] + jnp.dot(sg_ref[...].astype(BF16), wsg_ref[...], preferred_element_type=F32)
                      + jnp.dot(dn_s[...], wdn_ref[...], preferred_element_type=F32))

    row = lambda i: (i, 0)
    const = lambda i: (0, 0)
    half = pl.BlockSpec((tm, DN_WIDTH), row)
    return pl.pallas_call(
        body, name=name, grid=(t // tm,),
        in_specs=[pl.BlockSpec((tm, D_MODEL), row), half, half, half, pl.BlockSpec((SG_WIDTH, D_MODEL), const),
                  pl.BlockSpec((DN_WIDTH, D_MODEL), const), pl.BlockSpec((1, DN_HEAD_DIM), const)],
        out_specs=pl.BlockSpec((tm, D_MODEL), row),
        out_shape=jax.ShapeDtypeStruct((t, D_MODEL), F32),
        scratch_shapes=[pltpu.VMEM((tm, DN_WIDTH), BF16)],
        compiler_params=_cparams(("parallel",)),
    )(x, sg, o, z, wo_sg, wo_dn, dnw)


def _mix_out_bwd(dy, sg, o, z, wo_sg, wo_dn, dnw, *, name):
    t = dy.shape[0]
    tm = _tm(t)

    def body(dy_ref, sg_ref, o_ref, z_ref, wsg_ref, wdn_ref, dnw_ref, dsg_ref, do_ref, dz_ref, dwsg_ref, dwdn_ref, ddnw_ref, dn_s):
        i = pl.program_id(0)
        dyb = dy_ref[...].astype(BF16)
        nt = (((1,), (1,)), ((), ()))
        tn = (((0,), (0,)), ((), ()))
        dsg_ref[...] = lax.dot_general(dyb, wsg_ref[...], nt, preferred_element_type=F32)
        ddn = lax.dot_general(dyb, wdn_ref[...], nt, preferred_element_type=F32)
        ddnw = None
        for h, (oh, zh) in enumerate(zip(_split_heads(o_ref, 0), _split_heads(z_ref, 0))):
            cols = slice(h * DN_HEAD_DIM, (h + 1) * DN_HEAD_DIM)
            out, vjp = jax.vjp(_dn_gate, oh, zh, dnw_ref[...])
            dn_s[:, cols] = out.astype(BF16)
            doh, dzh, dw = vjp(ddn[:, cols])
            do_ref[:, cols] = doh
            dz_ref[:, cols] = dzh.astype(BF16)
            ddnw = dw if ddnw is None else ddnw + dw
        _acc_out(ddnw_ref, i == 0, ddnw)
        _acc_out(dwsg_ref, i == 0, lax.dot_general(sg_ref[...].astype(BF16), dyb, tn, preferred_element_type=F32))
        _acc_out(dwdn_ref, i == 0, lax.dot_general(dn_s[...], dyb, tn, preferred_element_type=F32))

    row = lambda i: (i, 0)
    const = lambda i: (0, 0)
    half = pl.BlockSpec((tm, DN_WIDTH), row)
    wspec = pl.BlockSpec((DN_WIDTH, D_MODEL), const)
    return pl.pallas_call(
        body, name=name, grid=(t // tm,),
        in_specs=[pl.BlockSpec((tm, D_MODEL), row), half, half, half, wspec, wspec, pl.BlockSpec((1, DN_HEAD_DIM), const)],
        out_specs=(half, half, half, wspec, wspec, pl.BlockSpec((1, DN_HEAD_DIM), const)),
        out_shape=(jax.ShapeDtypeStruct((t, DN_WIDTH), F32),) * 2 + (jax.ShapeDtypeStruct((t, DN_WIDTH), BF16),)
        + (jax.ShapeDtypeStruct((DN_WIDTH, D_MODEL), F32),) * 2 + (jax.ShapeDtypeStruct((1, DN_HEAD_DIM), F32),),
        scratch_shapes=[pltpu.VMEM((tm, DN_WIDTH), BF16)],
        compiler_params=_cparams(("arbitrary",)),
    )(dy, sg, o, z, wo_sg, wo_dn, dnw)


_MESH = pl.DeviceIdType.MESH
_HBM = pl.BlockSpec(memory_space=pl.ANY)


def _mesh_pos():
    x, y, c = lax.axis_index("x"), lax.axis_index("y"), lax.axis_index("c")
    return x, y, c, [(1 - x, y), (x, 1 - y), (1 - x, 1 - y)]


def _gather2(arrs, *, name):
    n = len(arrs)
    slots = N_DEV - 1

    def body(*refs):
        in_refs, out_refs = refs[:n], refs[n:2 * n]
        send_sems, recv_sems, local_sems = refs[2 * n:]
        x, y, c, chips = _mesh_pos()
        me, sibling = (x, y, c), (x, y, 1 - c)

        def copy(k, slot, block, to, src=None):
            dst = out_refs[k].at[4 * block[0] + 2 * block[1] + block[2]]
            return pltpu.make_async_remote_copy(src_ref=dst if src is None else src, dst_ref=dst,
                                                send_sem=send_sems.at[k * slots + slot], recv_sem=recv_sems.at[k * slots + slot],
                                                device_id=to, device_id_type=_MESH)

        local = [pltpu.make_async_copy(in_refs[k], out_refs[k].at[4 * x + 2 * y + c], local_sems.at[k]) for k in range(n)]
        sent = []
        for k in range(n):
            sent.append(copy(k, 0, me, sibling, src=in_refs[k]))
            sent += [copy(k, 1 + j, me, (*chip, c), src=in_refs[k]) for j, chip in enumerate(chips)]
        for cp in local + sent:
            cp.start()
        for j, chip in enumerate(chips):
            for k in range(n):
                copy(k, 1 + j, (*chip, c), me).wait_recv()
                passed = copy(k, 4 + j, (*chip, c), sibling)
                passed.start()
                sent.append(passed)
        for k in range(n):
            copy(k, 0, sibling, me).wait_recv()
            for j, chip in enumerate(chips):
                copy(k, 4 + j, (*chip, 1 - c), me).wait_recv()
        for cp in sent:
            cp.wait_send()
        for cp in local:
            cp.wait()

    return pl.pallas_call(
        body, name=name, in_specs=[_HBM] * n, out_specs=(_HBM,) * n,
        out_shape=tuple(jax.ShapeDtypeStruct((N_DEV,) + a.shape, a.dtype) for a in arrs),
        scratch_shapes=[pltpu.SemaphoreType.DMA((n * slots,)), pltpu.SemaphoreType.DMA((n * slots,)),
                        pltpu.SemaphoreType.DMA((n,))],
    )(*arrs)


_SEM = pl.BlockSpec(memory_space=pltpu.SEMAPHORE)
_EFFECT = pltpu.SideEffectType.DATAFLOW_SIDE_EFFECTING


def _direct_copies(src_refs, land_refs, send_sems, recv_sems, gather):
    x, y, c, _ = _mesh_pos()
    me = 4 * x + 2 * y + c
    n_peer = N_DEV - 1
    copies = []
    for r in range(1, N_DEV):
        px = 1 - x if r & 4 else x
        py = 1 - y if r & 2 else y
        pc = 1 - c if r & 1 else c
        for k, (src, land) in enumerate(zip(src_refs, land_refs)):
            copies.append(pltpu.make_async_remote_copy(
                src_ref=src if gather else src.at[4 * px + 2 * py + pc], dst_ref=land.at[me],
                send_sem=send_sems.at[k * n_peer + r - 1], recv_sem=recv_sems.at[k * n_peer + r - 1],
                device_id=(px, py, pc), device_id_type=_MESH))
    return copies


def _send_start(arrs, gather, after=None, *, name):
    n = len(arrs)
    lands = [lax.empty(((N_DEV,) + a.shape) if gather else a.shape, a.dtype) for a in arrs]
    n_in = 2 * n + (0 if after is None else 1)

    def body(*refs):
        src_refs, land_refs, send_sems, recv_sems, token = refs[:n], refs[n:2 * n], refs[n_in], refs[n_in + 1], refs[-1]
        for cp in _direct_copies(src_refs, land_refs, send_sems, recv_sems, gather):
            cp.start()
        token[...] = jnp.zeros_like(token)

    n_sem = n * (N_DEV - 1)
    bufs = list(arrs) + lands
    out = pl.pallas_call(
        body, name=name,
        out_shape=(pltpu.SemaphoreType.DMA((n_sem,)), pltpu.SemaphoreType.DMA((n_sem,)))
        + tuple(pltpu.HBM(b.shape, b.dtype) for b in bufs) + (jax.ShapeDtypeStruct((HALO, LANES), F32),),
        in_specs=[_HBM] * n_in, out_specs=(_SEM, _SEM) + (_HBM,) * (2 * n) + (pl.BlockSpec(memory_space=pltpu.VMEM),),
        input_output_aliases={i: 2 + i for i in range(2 * n)},
        compiler_params=pltpu.CompilerParams(has_side_effects=_EFFECT),
    )(*[pltpu.with_memory_space_constraint(b, pltpu.HBM) for b in bufs], *([] if after is None else [after]))
    return (out[0], out[1], list(out[2:2 + n]), list(out[2 + n:2 + 2 * n])), out[-1]


def _send_wait(started, gather, after, *, name):
    send_sems, recv_sems, srcs, lands = started
    n = len(srcs)

    def body(*refs):
        src_refs, land_refs, send_ref, recv_ref = refs[:n], refs[n:2 * n], refs[2 * n], refs[2 * n + 1]
        for cp in _direct_copies(src_refs, land_refs, send_ref, recv_ref, gather):
            cp.wait_send()
            cp.wait_recv()

    bufs = srcs + lands
    out = pl.pallas_call(
        body, name=name, out_shape=tuple(pltpu.HBM(b.shape, b.dtype) for b in bufs),
        in_specs=[_HBM] * (2 * n) + [_SEM, _SEM, _HBM], out_specs=(_HBM,) * (2 * n),
        input_output_aliases={i: i for i in range(2 * n)},
        compiler_params=pltpu.CompilerParams(has_side_effects=_EFFECT),
    )(*bufs, send_sems, recv_sems, after)
    return list(out[n:])


def _row_block(rows, limit=256):
    best = rows
    for cand in range(8, limit + 1, 8):
        if rows % cand == 0:
            best = cand
    return best if rows > limit else rows


def _adam(gp, w, m, v, *, name):
    p, rows, cols = gp.shape
    rb = _row_block(rows)

    def body(gp_ref, w_ref, m_ref, v_ref, g_ref, d_ref, m2_ref, v2_ref):
        g = gp_ref[0].astype(F32)
        for s in range(1, p):
            g = g + gp_ref[s].astype(F32)
        m2 = ADAM_B1 * m_ref[...] + (1.0 - ADAM_B1) * g
        v2 = ADAM_B2 * v_ref[...] + (1.0 - ADAM_B2) * (g * g)
        m_hat = m2 / (1.0 - ADAM_B1 ** ADAM_STEP)
        v_hat = v2 / (1.0 - ADAM_B2 ** ADAM_STEP)
        g_ref[...] = g
        d_ref[...] = -ADAM_LR * (m_hat / (jnp.sqrt(v_hat) + ADAM_EPS) + ADAM_WD * w_ref[...])
        m2_ref[...] = m2
        v2_ref[...] = v2

    blk = pl.BlockSpec((rb, cols), lambda i: (i, 0))
    return pl.pallas_call(
        body, name=name, grid=(rows // rb,),
        in_specs=[pl.BlockSpec((p, rb, cols), lambda i: (0, i, 0)), blk, blk, blk],
        out_specs=(blk,) * 4, out_shape=(jax.ShapeDtypeStruct((rows, cols), F32),) * 4,
        compiler_params=_cparams(("parallel",)),
    )(gp, w, m, v)


def _cols_full(g):
    return jnp.transpose(g, (1, 0, 2)).reshape(g.shape[1], N_DEV * g.shape[2])


def _pad_lanes(a, width=LANES):
    return jnp.pad(a, ((0, 0), (0, width - a.shape[1])))


def _chunk_rows_of(a):
    by_chunk = jnp.transpose(a[:, :DN_HEADS].reshape(-1, DN_CHUNK, DN_HEADS), (0, 2, 1))
    return jnp.pad(by_chunk, ((0, 0), (0, HALO - DN_HEADS), (0, 0)))


_SMALL = (("ffn1_norm", D_MODEL), ("mix_norm", D_MODEL), ("ffn2_norm", D_MODEL), ("final_norm", D_MODEL), ("a_log", DN_HEADS),
          ("dt_bias", DN_HEADS), ("dn_norm", DN_HEAD_DIM), ("sg_ln_g", SG_WIDTH), ("sg_ln_b", SG_WIDTH),
          ("sg_w", SG_GROUPS * SG_CHUNK * SG_CHUNK), ("sg_b", SG_GROUPS * SG_CHUNK), ("conv_w", CONV_K * 3 * DN_WIDTH))
_SMALL_ROWS = 1128
_SMALL_SHAPES = {"ffn1_norm": (1, D_MODEL), "mix_norm": (1, D_MODEL), "ffn2_norm": (1, D_MODEL), "final_norm": (D_MODEL,),
                 "a_log": (1, DN_HEADS), "dt_bias": (1, DN_HEADS), "dn_norm": (1, DN_HEAD_DIM), "sg_ln_g": (1, SG_WIDTH),
                 "sg_ln_b": (1, SG_WIDTH), "sg_w": (1, SG_GROUPS, SG_CHUNK, SG_CHUNK), "sg_b": (1, SG_GROUPS, SG_CHUNK)}


def _pack_small(d):
    flat = jnp.concatenate([d[name].reshape(-1) for name, _ in _SMALL])
    return jnp.pad(flat, (0, _SMALL_ROWS * LANES - flat.shape[0])).reshape(_SMALL_ROWS, LANES)


def _unpack_small(a):
    flat, out, at = a.reshape(-1), {}, 0
    for name, size in _SMALL:
        out[name] = flat[at:at + size]
        at += size
    return out


def kernel(x, ffn1_norm, ffn1_w_gate, ffn1_w_up, ffn1_w_down, mix_norm, w_in, conv_w, a_log, dt_bias, dn_norm, sg_ln_g, sg_ln_b, sg_w, sg_b, w_out, ffn2_norm, ffn2_w_gate, ffn2_w_up, ffn2_w_down, final_norm, loss_target, m_ffn1_norm, m_ffn1_w_gate, m_ffn1_w_up, m_ffn1_w_down, m_mix_norm, m_w_in, m_conv_w, m_a_log, m_dt_bias, m_dn_norm, m_sg_ln_g, m_sg_ln_b, m_sg_w, m_sg_b, m_w_out, m_ffn2_norm, m_ffn2_w_gate, m_ffn2_w_up, m_ffn2_w_down, m_final_norm, v_ffn1_norm, v_ffn1_w_gate, v_ffn1_w_up, v_ffn1_w_down, v_mix_norm, v_w_in, v_conv_w, v_a_log, v_dt_bias, v_dn_norm, v_sg_ln_g, v_sg_ln_b, v_sg_w, v_sg_b, v_w_out, v_ffn2_norm, v_ffn2_w_gate, v_ffn2_w_up, v_ffn2_w_down, v_final_norm):
    weights = dict(ffn1_norm=ffn1_norm, ffn1_w_gate=ffn1_w_gate, ffn1_w_up=ffn1_w_up, ffn1_w_down=ffn1_w_down, mix_norm=mix_norm, w_in=w_in, conv_w=conv_w, a_log=a_log, dt_bias=dt_bias, dn_norm=dn_norm, sg_ln_g=sg_ln_g, sg_ln_b=sg_ln_b, sg_w=sg_w, sg_b=sg_b, w_out=w_out, ffn2_norm=ffn2_norm, ffn2_w_gate=ffn2_w_gate, ffn2_w_up=ffn2_w_up, ffn2_w_down=ffn2_w_down, final_norm=final_norm)
    mom_m = dict(ffn1_norm=m_ffn1_norm, ffn1_w_gate=m_ffn1_w_gate, ffn1_w_up=m_ffn1_w_up, ffn1_w_down=m_ffn1_w_down, mix_norm=m_mix_norm, w_in=m_w_in, conv_w=m_conv_w, a_log=m_a_log, dt_bias=m_dt_bias, dn_norm=m_dn_norm, sg_ln_g=m_sg_ln_g, sg_ln_b=m_sg_ln_b, sg_w=m_sg_w, sg_b=m_sg_b, w_out=m_w_out, ffn2_norm=m_ffn2_norm, ffn2_w_gate=m_ffn2_w_gate, ffn2_w_up=m_ffn2_w_up, ffn2_w_down=m_ffn2_w_down, final_norm=m_final_norm)
    mom_v = dict(ffn1_norm=v_ffn1_norm, ffn1_w_gate=v_ffn1_w_gate, ffn1_w_up=v_ffn1_w_up, ffn1_w_down=v_ffn1_w_down, mix_norm=v_mix_norm, w_in=v_w_in, conv_w=v_conv_w, a_log=v_a_log, dt_bias=v_dt_bias, dn_norm=v_dn_norm, sg_ln_g=v_sg_ln_g, sg_ln_b=v_sg_ln_b, sg_w=v_sg_w, sg_b=v_sg_b, w_out=v_w_out, ffn2_norm=v_ffn2_norm, ffn2_w_gate=v_ffn2_w_gate, ffn2_w_up=v_ffn2_w_up, ffn2_w_down=v_ffn2_w_down, final_norm=v_final_norm)
    order = list(weights)
    big = ("ffn1_w_gate", "ffn1_w_up", "ffn1_w_down", "w_in", "w_out", "ffn2_w_gate", "ffn2_w_up", "ffn2_w_down")
    col_sharded = ("ffn1_w_gate", "ffn1_w_up", "w_in", "ffn2_w_gate", "ffn2_w_up")

    n_seq, seq, _ = x.shape
    t = n_seq * seq
    me = 4 * lax.axis_index("x") + 2 * lax.axis_index("y") + lax.axis_index("c")
    x0 = x.reshape(t, D_MODEL)
    tgt = loss_target.reshape(t, D_MODEL)

    def fill_own(land, own_block):
        return lax.dynamic_update_index_in_dim(land, own_block, me, 0)

    def rows_view(n, a):
        return jnp.transpose(a) if n in col_sharded else a

    def as_full(n, g):
        return g.reshape(-1, g.shape[-1])

    shards = {n: rows_view(n, weights[n][0]).astype(BF16) for n in big}
    ffn1_names, mix_names, ffn2_names = big[:3], big[3:5], big[5:]
    full = {n: as_full(n, g) for n, g in zip(ffn1_names[:2], _gather2([shards[n] for n in ffn1_names[:2]], name="gather_ffn1"))}
    down_src = shards[ffn1_names[2]]
    down_started, down_token = _send_start([down_src], True, full[ffn1_names[1]], name="gather_down_start")
    mix_srcs = [shards[n] for n in mix_names] + [conv_w[0]]
    mix_started, mix_token = _send_start(mix_srcs, True, down_token, name="gather_mix_start")
    ffn2_started, ffn2_token = _send_start([shards[n] for n in ffn2_names], True, mix_token, name="gather_ffn2_start")
    ffn1_norm_fwd = ffn1_norm + ffn2_token[:1, :1]
    alog, dtb = _pad_lanes(a_log), _pad_lanes(dt_bias)
    sgbt = _pad_lanes(sg_b[0].T)
    fnw = final_norm.reshape(1, D_MODEL)

    h1, g1, u1 = _ffn_gate_up(x0, ffn1_norm_fwd, full["ffn1_w_gate"], full["ffn1_w_up"], name="ffn1_gate_up")
    (down_land,) = _send_wait(down_started, True, g1, name="gather_down_wait")
    full[ffn1_names[2]] = as_full(ffn1_names[2], fill_own(down_land, down_src))
    x1 = _ffn_down(x0, g1, u1, full["ffn1_w_down"], name="ffn1_down")
    mix_lands = [fill_own(land, src) for land, src in zip(_send_wait(mix_started, True, x1, name="gather_mix_wait"), mix_srcs)]
    full.update({n: as_full(n, g) for n, g in zip(mix_names, mix_lands)})
    conv_full = _cols_full(mix_lands[-1])
    w_in_t = full["w_in"]
    offs = (0, SG_WIDTH, 2 * SG_WIDTH, 2 * SG_WIDTH + 3 * DN_WIDTH, 2 * SG_WIDTH + 4 * DN_WIDTH)
    n_proj = offs[-1]

    def pad_rows(a):
        return jnp.pad(a, ((0, LANES - a.shape[0]), (0, 0)))

    ws = [w_in_t[offs[0]:offs[1]], w_in_t[offs[1]:offs[2]], w_in_t[offs[2]:offs[3]], w_in_t[offs[3]:offs[4]],
          pad_rows(w_in_t[n_proj:n_proj + DN_HEADS]), pad_rows(w_in_t[n_proj + DN_HEADS:n_proj + 2 * DN_HEADS])]
    wo_sg, wo_dn = full["w_out"][:SG_WIDTH], full["w_out"][SG_WIDTH:]
    u, v, qkv, z, bpre, apre = _mix_in_fwd(x1, mix_norm, ws, name="mix_in_fwd")
    sg_out = _sg_fwd(u, v, sg_ln_g, sg_ln_b, sg_w[0], sgbt, name="sg_fwd")
    q, k, vv, beta, gc = _dn_prep_fwd(qkv, bpre, apre, conv_full, alog, dtb, seq, name="dn_prep_fwd")
    grow = _chunk_rows_of(gc)
    wy_w, wy_u, q_dec, k_dec, qk, egl, inv = _delta_prep(q, k, vv, gc, grow, beta, name="delta_prep")
    o, states = _delta_seq_fwd(wy_w, wy_u, q_dec, k_dec, qk, egl, n_seq, seq, name="delta_seq_fwd")
    x2 = _mix_out_fwd(x1, sg_out, o, z, wo_sg, wo_dn, dn_norm, name="mix_out_fwd")
    ffn2_lands = _send_wait(ffn2_started, True, x2, name="gather_ffn2_wait")
    full.update({n: as_full(n, fill_own(land, shards[n])) for n, land in zip(ffn2_names, ffn2_lands)})
    dx3, loss_part, d_fn, h2, g2, u2 = _ffn_fwd(x2, ffn2_norm, full["ffn2_w_gate"], full["ffn2_w_up"], full["ffn2_w_down"],
                                                tgt, fnw, name="ffn2_fwd_loss")
    loss = lax.psum(loss_part[0, 0], ("x", "y", "c"))

    dx2, d_n2, d_g2, d_u2, d_d2 = _ffn_bwd(x2, ffn2_norm, h2, g2, u2, full["ffn2_w_gate"], full["ffn2_w_up"],
                                           full["ffn2_w_down"], dx3, name="ffn2_bwd")
    def by_owner(d_rows):
        return d_rows.reshape(N_DEV, -1, D_MODEL)

    ffn2_pieces = [by_owner(d_g2), by_owner(d_u2), by_owner(d_d2)]
    ffn2_sent, sent_token = _send_start(ffn2_pieces, False, name="grads_ffn2_start")
    dsg, do, dz, d_wo_sg, d_wo_dn, d_dnw = _mix_out_bwd(dx2, sg_out, o, z, wo_sg, wo_dn, dn_norm + sent_token[:1, :1],
                                                        name="mix_out_bwd")
    d_seq = _delta_seq_bwd(wy_w, wy_u, q_dec, k_dec, qk, egl, states, do, n_seq, seq, name="delta_seq_bwd")
    dq, dk, dv, dgc_a, dgrow, dbeta = _delta_par_bwd(q, k, vv, gc, grow, beta, inv, *d_seq, name="delta_par_bwd")
    dgc_b = _pad_lanes(jnp.transpose(dgrow[:, :DN_HEADS, :], (0, 2, 1)).reshape(t, DN_HEADS))
    dy_conv, dbpre, dapre, d_alog, d_dtb = _dn_prep_bwd(qkv, bpre, apre, conv_full, alog, dtb, dq, dk, dv, dbeta, dgc_a, dgc_b,
                                                        seq, name="dn_prep_bwd")
    dqkv, d_conv = _conv_bwd(qkv, dy_conv, conv_full, seq, name="conv_bwd")
    du, dvv, d_lng, d_lnb, d_wc, d_sgbt = _sg_bwd(u, v, sg_ln_g, sg_ln_b, sg_w[0], sgbt, dsg, name="sg_bwd")
    dx1, d_mixn, d_wp = _mix_in_bwd(x1, mix_norm, ws, dx2, (du, dvv, dqkv, dz, dbpre, dapre), name="mix_in_bwd")
    d_w_in_t = jnp.concatenate([d_wp[:n_proj], d_wp[_PROJ_OFFSETS[4]:_PROJ_OFFSETS[4] + DN_HEADS],
                                d_wp[_PROJ_OFFSETS[5]:_PROJ_OFFSETS[5] + DN_HEADS]], axis=0)
    d_w_out = jnp.concatenate([d_wo_sg, d_wo_dn], axis=0)
    mix_pieces = [by_owner(d_w_in_t), by_owner(d_w_out).astype(BF16)]
    mix_sent, sent_token = _send_start(mix_pieces, False, name="grads_mix_start")
    grad_x, d_n1, dg1, du1, a1, dyh1 = _ffn_bwd_x(x0, ffn1_norm + sent_token[:1, :1], g1, u1, full["ffn1_w_gate"],
                                                  full["ffn1_w_up"], full["ffn1_w_down"], dx1, name="ffn1_bwd_x")
    small_grads = dict(ffn1_norm=d_n1, mix_norm=d_mixn, ffn2_norm=d_n2, final_norm=d_fn, a_log=d_alog[:, :DN_HEADS],
                       dt_bias=d_dtb[:, :DN_HEADS], dn_norm=d_dnw, sg_ln_g=d_lng, sg_ln_b=d_lnb, sg_w=d_wc,
                       sg_b=d_sgbt[:, :SG_GROUPS].T, conv_w=d_conv[:CONV_K])
    small_src = _pack_small(small_grads)
    small_sent, small_token = _send_start([small_src], True, name="small_grads_start")
    late, tokens = [], []

    def send_early(k, grad):
        piece = by_owner(grad)
        sent, token = _send_start([piece], False, name="grads_" + ffn1_names[k] + "_start")
        late.append(((ffn1_names[k],), sent, [piece]))
        tokens.append(token)
        return token

    _ffn_wgrads(h1, dg1, du1, a1, dyh1, send_early, small_token, name="ffn1_bwd")

    res = {}
    after = tokens[-1]

    def update(names, sent, pieces, after):
        lands = _send_wait(sent, False, after, name="grads_" + names[0] + "_wait")
        for n, land, p in zip(names, lands, pieces):
            got = fill_own(land, lax.dynamic_index_in_dim(p, me, 0, keepdims=False))
            upd = _adam(got, *[rows_view(n, src[n][0]) for src in (weights, mom_m, mom_v)], name="adam_" + n)
            res[n] = [rows_view(n, a) for a in upd]
            after = upd[0]
        return after

    for group in [(ffn2_names, ffn2_sent, ffn2_pieces), (mix_names, mix_sent, mix_pieces)] + late[:-1]:
        after = update(*group, after)
    (small_land,) = _send_wait(small_sent, True, after, name="small_grads_wait")
    small_parts = fill_own(small_land, small_src)
    zeros_conv = jnp.zeros((CONV_K * 3 * DN_WIDTH,), F32)
    packed = [_pack_small({**{n: src[n] for n, _ in _SMALL if n != "conv_w"}, "conv_w": zeros_conv})
              for src in (weights, mom_m, mom_v)]
    small_upd = _adam(small_parts, *packed, name="adam_small")
    small_res = [_unpack_small(a) for a in small_upd]
    conv_grad = lax.dynamic_slice_in_dim(small_res[0]["conv_w"].reshape(CONV_K, 3 * DN_WIDTH), me * (3 * DN_WIDTH // N_DEV),
                                         3 * DN_WIDTH // N_DEV, axis=1)
    res["conv_w"] = _adam(conv_grad[None], conv_w[0], m_conv_w[0], v_conv_w[0], name="adam_conv_w")
    update(*late[-1], res["conv_w"][0])

    outs = [[], [], [], []]
    for n in order:
        for kind in range(4):
            if n in res:
                outs[kind].append(res[n][kind][None])
            else:
                outs[kind].append(small_res[kind][n].reshape(_SMALL_SHAPES[n]))
    return (loss, grad_x.reshape(x.shape), *outs[0], *outs[1], *outs[2], *outs[3])
```

```python
import functools

import jax
import jax.numpy as jnp
from jax import lax
from jax.experimental import pallas as pl
from jax.experimental.pallas import tpu as pltpu

F32 = jnp.float32
BF16 = jnp.bfloat16

D_MODEL = 1024
D_FF = 2816
SG_WIDTH = 512
SG_GROUPS = 8
SG_GROUP_DIM = 64
SG_CHUNK = 128
DN_WIDTH = 512
DN_HEAD_DIM = 128
DN_HEADS = 4
DN_CHUNK = 64
CONV_K = 4
EPS = 1e-6
N_DEV = 8
LANES = 128
HALO = 8
MXU_COLS = 256

ADAM_LR = 0.001
ADAM_B1 = 0.9
ADAM_B2 = 0.999
ADAM_EPS = 1e-08
ADAM_WD = 0.01
ADAM_STEP = 10

VMEM_LIMIT = 60 * 1024 * 1024
WGRAD_VMEM_BUDGET = 52 * 1024 * 1024
TOKEN_BLOCK = 512
FF_BLOCK_FWD = 1408

_HI = lax.Precision.HIGHEST


def _cparams(sem):
    return pltpu.CompilerParams(dimension_semantics=sem, vmem_limit_bytes=VMEM_LIMIT)


def _tm(t, pref=TOKEN_BLOCK):
    return min(pref, t)


def _dg(a, b, ca, cb, precision):
    if precision is not None:
        return lax.dot_general(a, b, (((ca,), (cb,)), ((), ())), precision=precision, preferred_element_type=F32)
    return lax.dot_general(a.astype(BF16), b.astype(BF16), (((ca,), (cb,)), ((), ())), preferred_element_type=F32)


def _make_mm(exact):
    @jax.custom_vjp
    def mm(a, b):
        return _dg(a, b, 1, 0, exact)

    @jax.custom_vjp
    def mm_nt(a, b):
        return _dg(a, b, 1, 1, exact)

    @jax.custom_vjp
    def mm_tn(a, b):
        return _dg(a, b, 0, 0, exact)

    mm.defvjp(lambda a, b: (mm(a, b), (a, b)), lambda r, g: (mm_nt(g, r[1]), mm_tn(r[0], g)))
    mm_nt.defvjp(lambda a, b: (mm_nt(a, b), (a, b)), lambda r, g: (mm(g, r[1]), mm_tn(g, r[0])))
    mm_tn.defvjp(lambda a, b: (mm_tn(a, b), (a, b)), lambda r, g: (mm_nt(r[1], g), mm(r[0], g)))
    return mm, mm_nt, mm_tn


mm, mm_nt, mm_tn = _make_mm(None)
mmx, mmx_nt, mmx_tn = _make_mm(_HI)
mmh, mmh_nt, mmh_tn = _make_mm(lax.Precision.HIGH)


def _sigmoid(x):
    return 1.0 / (1.0 + jnp.exp(-x))


def _silu(x):
    return x * _sigmoid(x)


def _softplus(x):
    neg_abs = jnp.where(x > 0, -x, x)
    return jnp.where(x > 0, x, 0.0) + jnp.log(1.0 + jnp.exp(neg_abs))


def _gelu(x):
    return 0.5 * x * (1.0 + jnp.tanh(0.7978845608028654 * (x + 0.044715 * (x * x * x))))


def _rms_fwd(x, g):
    r = lax.rsqrt(jnp.mean(x * x, axis=-1, keepdims=True) + EPS)
    xh = x * r
    return xh * g, xh, r


def _rms_bwd(dh, xh, r, g):
    dxh = dh * g
    dx = r * (dxh - xh * jnp.mean(dxh * xh, axis=-1, keepdims=True))
    return dx, jnp.sum(dh * xh, axis=0, keepdims=True)


def _acc_out(ref, first, val):
    @pl.when(first)
    def _():
        ref[...] = val

    @pl.when(jnp.logical_not(first))
    def _():
        ref[...] += val


def _ffn_fwd(x, nw, wg, wu, wd, tgt=None, fnw=None, *, name):
    t = x.shape[0]
    tm, fb = _tm(t), FF_BLOCK_FWD
    n_t, n_f = t // tm, D_FF // fb
    with_loss = tgt is not None

    def body(*refs):
        if with_loss:
            (x_ref, nw_ref, wg_ref, wu_ref, wd_ref, tgt_ref, fnw_ref, dy_ref, loss_ref, dfn_ref, h_ref, g_ref, u_ref,
             acc_s) = refs
        else:
            x_ref, nw_ref, wg_ref, wu_ref, wd_ref, y_ref, h_ref, g_ref, u_ref, acc_s = refs
        i, j = pl.program_id(0), pl.program_id(1)

        @pl.when(j == 0)
        def _():
            h, _, _ = _rms_fwd(x_ref[...], nw_ref[...])
            h_ref[...] = h.astype(BF16)
            acc_s[...] = jnp.zeros_like(acc_s)

        h = h_ref[...]
        nt = (((1,), (1,)), ((), ()))
        g = lax.dot_general(h, wg_ref[...], nt, preferred_element_type=F32)
        u = lax.dot_general(h, wu_ref[...], nt, preferred_element_type=F32)
        g_ref[...] = g.astype(BF16)
        u_ref[...] = u.astype(BF16)
        a = _silu(g) * u
        acc_s[...] += jnp.dot(a.astype(BF16), wd_ref[...], preferred_element_type=F32)

        @pl.when(j == n_f - 1)
        def _():
            y = x_ref[...] + 0.5 * acc_s[...]
            if not with_loss:
                y_ref[...] = y
            else:
                gf = fnw_ref[...]
                out, xh, r = _rms_fwd(y, gf)
                err = out - tgt_ref[...]
                part = 0.5 * jnp.sum(jnp.mean(err * err, axis=-1, keepdims=True), axis=0, keepdims=True)
                d_out = err * (1.0 / D_MODEL)
                dy, dgf = _rms_bwd(d_out, xh, r, gf)
                dy_ref[...] = dy
                _acc_out(loss_ref, i == 0, jnp.broadcast_to(part, loss_ref.shape))
                _acc_out(dfn_ref, i == 0, dgf)

    row = lambda i, j: (i, 0)
    const = lambda i, j: (0, 0)
    in_specs = [
        pl.BlockSpec((tm, D_MODEL), row),
        pl.BlockSpec((1, D_MODEL), const),
        pl.BlockSpec((fb, D_MODEL), lambda i, j: (j, 0)),
        pl.BlockSpec((fb, D_MODEL), lambda i, j: (j, 0)),
        pl.BlockSpec((fb, D_MODEL), lambda i, j: (j, 0)),
    ]
    args = [x, nw, wg, wu, wd]
    saved_shape = (jax.ShapeDtypeStruct((t, D_MODEL), BF16), jax.ShapeDtypeStruct((t, D_FF), BF16),
                   jax.ShapeDtypeStruct((t, D_FF), BF16))
    saved_specs = (pl.BlockSpec((tm, D_MODEL), row), pl.BlockSpec((tm, fb), lambda i, j: (i, j)),
                   pl.BlockSpec((tm, fb), lambda i, j: (i, j)))
    if with_loss:
        in_specs += [pl.BlockSpec((tm, D_MODEL), row), pl.BlockSpec((1, D_MODEL), const)]
        args += [tgt, fnw]
        out_shape = (jax.ShapeDtypeStruct((t, D_MODEL), F32), jax.ShapeDtypeStruct((8, LANES), F32),
                     jax.ShapeDtypeStruct((1, D_MODEL), F32)) + saved_shape
        out_specs = (pl.BlockSpec((tm, D_MODEL), row), pl.BlockSpec((8, LANES), const),
                     pl.BlockSpec((1, D_MODEL), const)) + saved_specs
        sem = ("arbitrary", "arbitrary")
    else:
        out_shape = (jax.ShapeDtypeStruct((t, D_MODEL), F32),) + saved_shape
        out_specs = (pl.BlockSpec((tm, D_MODEL), row),) + saved_specs
        sem = ("parallel", "arbitrary")
    return pl.pallas_call(
        body, name=name, grid=(n_t, n_f), in_specs=in_specs, out_specs=out_specs, out_shape=out_shape,
        scratch_shapes=[pltpu.VMEM((tm, D_MODEL), F32)],
        compiler_params=_cparams(sem),
    )(*args)


def _ffn_bwd_x(x, nw, g, u, wg, wu, wd, dy, *, name):
    t = x.shape[0]
    tm = _tm(t, 256)

    def body(x_ref, nw_ref, g_ref, u_ref, wg_ref, wu_ref, wd_ref, dy_ref, dx_ref, dnw_ref, dg_ref, du_ref, a_ref, dyh_ref):
        i = pl.program_id(0)
        nt = (((1,), (1,)), ((), ()))
        dy = dy_ref[...]
        dyh = (0.5 * dy).astype(BF16)
        dyh_ref[...] = dyh
        gate, up = g_ref[...].astype(F32), u_ref[...].astype(F32)
        s = _sigmoid(gate)
        gs = gate * s
        da = lax.dot_general(dyh, wd_ref[...], nt, preferred_element_type=F32)
        dg = (da * up * (s + gs * (1.0 - s))).astype(BF16)
        du = (da * gs).astype(BF16)
        dg_ref[...] = dg
        du_ref[...] = du
        a_ref[...] = (gs * up).astype(BF16)
        dh = (jnp.dot(dg, wg_ref[...], preferred_element_type=F32)
              + jnp.dot(du, wu_ref[...], preferred_element_type=F32))
        xv = x_ref[...]
        r = lax.rsqrt(jnp.mean(xv * xv, axis=-1, keepdims=True) + EPS)
        dx, dnw = _rms_bwd(dh, xv * r, r, nw_ref[...])
        dx_ref[...] = dy + dx
        _acc_out(dnw_ref, i == 0, dnw)

    row = lambda i: (i, 0)
    const = lambda i: (0, 0)
    once = pl.Buffered(1)
    wide = pl.BlockSpec((tm, D_FF), row)
    return pl.pallas_call(
        body, name=name, grid=(t // tm,),
        in_specs=[pl.BlockSpec((tm, D_MODEL), row), pl.BlockSpec((1, D_MODEL), const), wide, wide,
                  pl.BlockSpec((D_FF, D_MODEL), const, pipeline_mode=once), pl.BlockSpec((D_FF, D_MODEL), const, pipeline_mode=once),
                  pl.BlockSpec((D_FF, D_MODEL), const, pipeline_mode=once), pl.BlockSpec((tm, D_MODEL), row)],
        out_specs=(pl.BlockSpec((tm, D_MODEL), row), pl.BlockSpec((1, D_MODEL), const), wide, wide, wide,
                   pl.BlockSpec((tm, D_MODEL), row)),
        out_shape=(jax.ShapeDtypeStruct((t, D_MODEL), F32), jax.ShapeDtypeStruct((1, D_MODEL), F32),
                   jax.ShapeDtypeStruct((t, D_FF), BF16), jax.ShapeDtypeStruct((t, D_FF), BF16),
                   jax.ShapeDtypeStruct((t, D_FF), BF16), jax.ShapeDtypeStruct((t, D_MODEL), BF16)),
        compiler_params=_cparams(("arbitrary",)),
    )(x, nw, g, u, wg, wu, wd, dy)


def _wgrad(a, b, bm, bn, after=None, *, name):
    k, m = a.shape
    n = b.shape[1]
    tk = k
    while 2 * 2 * tk * (bm + bn) + (4 + 2 * 2) * bm * bn + 4 * bm * MXU_COLS > WGRAD_VMEM_BUDGET:
        tk //= 2
    n_k = k // tk

    def body(a_ref, b_ref, *rest):
        o_ref, acc_s = rest[-2], rest[-1]
        s = pl.program_id(2)
        for c in range(bn // MXU_COLS):
            cols = slice(c * MXU_COLS, (c + 1) * MXU_COLS)
            part = lax.dot_general(a_ref[...], b_ref[:, cols], (((0,), (0,)), ((), ())), preferred_element_type=F32)
            acc_s[:, cols] = jnp.where(s == 0, 0.0, acc_s[:, cols]) + part

        @pl.when(s == n_k - 1)
        def _():
            o_ref[...] = acc_s[...].astype(BF16)

    return pl.pallas_call(
        body, name=name, grid=(m // bm, n // bn, n_k),
        in_specs=[pl.BlockSpec((tk, bm), lambda i, j, s: (s, i)), pl.BlockSpec((tk, bn), lambda i, j, s: (s, j))]
        + ([] if after is None else [_HBM]),
        out_specs=pl.BlockSpec((bm, bn), lambda i, j, s: (i, j)),
        out_shape=jax.ShapeDtypeStruct((m, n), BF16),
        scratch_shapes=[pltpu.VMEM((bm, bn), F32)],
        compiler_params=_cparams(("parallel", "parallel", "arbitrary")),
    )(a, b, *([] if after is None else [after]))


def _ffn_wgrads(h, dg, du, a, dyh, between=None, after=None, *, name):
    grads = []
    for k, (lhs, rhs, tag) in enumerate(((dg, h, "_wg"), (du, h, "_wu"), (a, dyh, "_wd"))):
        grads.append(_wgrad(lhs, rhs, D_FF // 2, D_MODEL, after, name=name + tag))
        after = None if between is None else between(k, grads[-1])
    return grads


def _ffn_bwd(x, nw, h, g, u, wg, wu, wd, dy, *, name):
    dx, dnw, dg, du, a, dyh = _ffn_bwd_x(x, nw, g, u, wg, wu, wd, dy, name=name + "_x")
    return (dx, dnw, *_ffn_wgrads(h, dg, du, a, dyh, name=name))


_PROJ_WIDTHS = (SG_WIDTH, SG_WIDTH, 3 * DN_WIDTH, DN_WIDTH, LANES, LANES)


def _mix_in_fwd(x, nw, ws, *, name):
    t = x.shape[0]
    tm = _tm(t)

    def body(x_ref, nw_ref, *refs):
        w_refs, o_refs = refs[:6], refs[6:]
        h, _, _ = _rms_fwd(x_ref[...], nw_ref[...])
        h = h.astype(BF16)
        for w_ref, o_ref in zip(w_refs, o_refs):
            o_ref[...] = lax.dot_general(h, w_ref[...], (((1,), (1,)), ((), ())), preferred_element_type=F32)

    row = lambda i: (i, 0)
    const = lambda i: (0, 0)
    return pl.pallas_call(
        body, name=name, grid=(t // tm,),
        in_specs=[pl.BlockSpec((tm, D_MODEL), row), pl.BlockSpec((1, D_MODEL), const)]
        + [pl.BlockSpec((n, D_MODEL), const) for n in _PROJ_WIDTHS],
        out_specs=tuple(pl.BlockSpec((tm, n), row) for n in _PROJ_WIDTHS),
        out_shape=tuple(jax.ShapeDtypeStruct((t, n), F32) for n in _PROJ_WIDTHS),
        compiler_params=_cparams(("parallel",)),
    )(x, nw, *ws)


_PROJ_TOTAL = sum(_PROJ_WIDTHS)
_PROJ_OFFSETS = tuple(sum(_PROJ_WIDTHS[:k]) for k in range(len(_PROJ_WIDTHS)))


def _mix_in_bwd(x, nw, ws, dres, dps, *, name):
    t = x.shape[0]
    tm = _tm(t, 256)

    def body(x_ref, nw_ref, dres_ref, *refs):
        w_refs, dp_refs, dx_ref, dnw_ref, h_ref, dpb_ref = refs[:6], refs[6:12], refs[12], refs[13], refs[14], refs[15]
        i = pl.program_id(0)
        hf, xh, r = _rms_fwd(x_ref[...], nw_ref[...])
        h_ref[...] = hf.astype(BF16)
        dh = jnp.zeros((tm, D_MODEL), F32)
        for w_ref, dp_ref, off, width in zip(w_refs, dp_refs, _PROJ_OFFSETS, _PROJ_WIDTHS):
            dp = dp_ref[...].astype(BF16)
            dpb_ref[:, off:off + width] = dp
            dh = dh + jnp.dot(dp, w_ref[...], preferred_element_type=F32)
        dx, dnw = _rms_bwd(dh, xh, r, nw_ref[...])
        dx_ref[...] = dres_ref[...] + dx
        _acc_out(dnw_ref, i == 0, dnw)

    row = lambda i: (i, 0)
    const = lambda i: (0, 0)
    dx, dnw, h, dpb = pl.pallas_call(
        body, name=name + "_x", grid=(t // tm,),
        in_specs=[pl.BlockSpec((tm, D_MODEL), row), pl.BlockSpec((1, D_MODEL), const), pl.BlockSpec((tm, D_MODEL), row)]
        + [pl.BlockSpec((n, D_MODEL), const) for n in _PROJ_WIDTHS]
        + [pl.BlockSpec((tm, n), row) for n in _PROJ_WIDTHS],
        out_specs=(pl.BlockSpec((tm, D_MODEL), row), pl.BlockSpec((1, D_MODEL), const), pl.BlockSpec((tm, D_MODEL), row),
                   pl.BlockSpec((tm, _PROJ_TOTAL), row)),
        out_shape=(jax.ShapeDtypeStruct((t, D_MODEL), F32), jax.ShapeDtypeStruct((1, D_MODEL), F32),
                   jax.ShapeDtypeStruct((t, D_MODEL), BF16), jax.ShapeDtypeStruct((t, _PROJ_TOTAL), BF16)),
        compiler_params=_cparams(("arbitrary",)),
    )(x, nw, dres, *ws, *dps)
    return dx, dnw, _wgrad(dpb, h, _PROJ_TOTAL // 2, D_MODEL, name=name + "_w")


def _sg_fn(u, v, lng, lnb, wcs, sgbt):
    lane = lax.broadcasted_iota(jnp.int32, (1, SG_WIDTH), 1)
    lane_b = lax.broadcasted_iota(jnp.int32, (1, LANES), 1)
    rr = lax.broadcasted_iota(jnp.int32, (SG_CHUNK, SG_CHUNK), 0)
    cc = lax.broadcasted_iota(jnp.int32, (SG_CHUNK, SG_CHUNK), 1)
    gu, gv = _gelu(u), _gelu(v)
    mu = jnp.mean(gv, axis=-1, keepdims=True)
    cen = gv - mu
    var = jnp.mean(cen * cen, axis=-1, keepdims=True)
    ln = cen * lax.rsqrt(var + EPS) * lng + lnb
    vs = jnp.zeros_like(u)
    for g in range(SG_GROUPS):
        in_group = jnp.logical_and(lane >= g * SG_GROUP_DIM, lane < (g + 1) * SG_GROUP_DIM)
        w_causal = jnp.where(rr >= cc, wcs[g], 0.0)
        bias = jnp.sum(jnp.where(lane_b == g, sgbt, 0.0), axis=1, keepdims=True)
        vs = vs + jnp.where(in_group, mm(w_causal, ln) + bias, 0.0)
    return gu * vs


def _sg_fwd(u, v, lng, lnb, wc, sgbt, *, name):
    t = u.shape[0]
    tm = _tm(t)

    def body(u_ref, v_ref, lng_ref, lnb_ref, wc_ref, sgbt_ref, o_ref):
        wcs = [wc_ref[g] for g in range(SG_GROUPS)]
        for c in range(tm // SG_CHUNK):
            rows = pl.ds(c * SG_CHUNK, SG_CHUNK)
            o_ref[rows, :] = _sg_fn(u_ref[rows, :], v_ref[rows, :], lng_ref[...], lnb_ref[...], wcs, sgbt_ref[...])

    row = lambda i: (i, 0)
    const = lambda i: (0, 0)
    return pl.pallas_call(
        body, name=name, grid=(t // tm,),
        in_specs=[pl.BlockSpec((tm, SG_WIDTH), row), pl.BlockSpec((tm, SG_WIDTH), row),
                  pl.BlockSpec((1, SG_WIDTH), const), pl.BlockSpec((1, SG_WIDTH), const),
                  pl.BlockSpec((SG_GROUPS, SG_CHUNK, SG_CHUNK), lambda i: (0, 0, 0)), pl.BlockSpec((SG_CHUNK, LANES), const)],
        out_specs=pl.BlockSpec((tm, SG_WIDTH), row),
        out_shape=jax.ShapeDtypeStruct((t, SG_WIDTH), F32),
        compiler_params=_cparams(("parallel",)),
    )(u, v, lng, lnb, wc, sgbt)


def _sg_bwd(u, v, lng, lnb, wc, sgbt, dout, *, name):
    t = u.shape[0]
    tm = _tm(t)

    def body(u_ref, v_ref, lng_ref, lnb_ref, wc_ref, sgbt_ref, do_ref, du_ref, dv_ref, dlng_ref, dlnb_ref, dwc_ref, dsgbt_ref):
        i = pl.program_id(0)
        wcs = [wc_ref[g] for g in range(SG_GROUPS)]
        tot = None
        for c in range(tm // SG_CHUNK):
            rows = pl.ds(c * SG_CHUNK, SG_CHUNK)
            _, vjp = jax.vjp(_sg_fn, u_ref[rows, :], v_ref[rows, :], lng_ref[...], lnb_ref[...], wcs, sgbt_ref[...])
            du, dv, dlng, dlnb, dwcs, dsgbt = vjp(do_ref[rows, :])
            du_ref[rows, :] = du.astype(BF16)
            dv_ref[rows, :] = dv.astype(BF16)
            part = (dlng, dlnb, dwcs, dsgbt)
            tot = part if tot is None else jax.tree.map(jnp.add, tot, part)
        dlng, dlnb, dwcs, dsgbt = tot
        _acc_out(dlng_ref, i == 0, dlng)
        _acc_out(dlnb_ref, i == 0, dlnb)
        _acc_out(dsgbt_ref, i == 0, dsgbt)
        for g in range(SG_GROUPS):
            @pl.when(i == 0)
            def _(g=g):
                dwc_ref[g] = dwcs[g]

            @pl.when(i > 0)
            def _(g=g):
                dwc_ref[g] += dwcs[g]

    row = lambda i: (i, 0)
    const = lambda i: (0, 0)
    wspec = pl.BlockSpec((SG_GROUPS, SG_CHUNK, SG_CHUNK), lambda i: (0, 0, 0))
    return pl.pallas_call(
        body, name=name, grid=(t // tm,),
        in_specs=[pl.BlockSpec((tm, SG_WIDTH), row), pl.BlockSpec((tm, SG_WIDTH), row),
                  pl.BlockSpec((1, SG_WIDTH), const), pl.BlockSpec((1, SG_WIDTH), const), wspec,
                  pl.BlockSpec((SG_CHUNK, LANES), const), pl.BlockSpec((tm, SG_WIDTH), row)],
        out_specs=(pl.BlockSpec((tm, SG_WIDTH), row), pl.BlockSpec((tm, SG_WIDTH), row),
                   pl.BlockSpec((1, SG_WIDTH), const), pl.BlockSpec((1, SG_WIDTH), const), wspec,
                   pl.BlockSpec((SG_CHUNK, LANES), const)),
        out_shape=(jax.ShapeDtypeStruct((t, SG_WIDTH), BF16), jax.ShapeDtypeStruct((t, SG_WIDTH), BF16),
                   jax.ShapeDtypeStruct((1, SG_WIDTH), F32), jax.ShapeDtypeStruct((1, SG_WIDTH), F32),
                   jax.ShapeDtypeStruct((SG_GROUPS, SG_CHUNK, SG_CHUNK), F32), jax.ShapeDtypeStruct((SG_CHUNK, LANES), F32)),
        compiler_params=_cparams(("arbitrary",)),
    )(u, v, lng, lnb, wc, sgbt, dout)


def _conv_taps(ext, w, tm):
    y = None
    for j in range(CONV_K):
        s = CONV_K - 1 - j
        shifted = ext if s == 0 else pltpu.roll(ext, s, 0)
        term = w[j:j + 1, :] * shifted[HALO:HALO + tm, :]
        y = term if y is None else y + term
    return y


def _post_conv(yq, yk, yv, bpre, apre, alog, dtb):
    def l2(a):
        return a * lax.rsqrt(jnp.sum(a * a, axis=-1, keepdims=True) + EPS)

    q = [l2(_silu(a)) for a in yq]
    k = [l2(_silu(a)) for a in yk]
    return q, k, _silu(yv), _sigmoid(bpre), -jnp.exp(alog) * _softplus(apre + dtb)


def _chunk_tril(tm):
    rr = lax.broadcasted_iota(jnp.int32, (tm, tm), 0)
    cc = lax.broadcasted_iota(jnp.int32, (tm, tm), 1)
    shift = DN_CHUNK.bit_length() - 1
    same = jnp.right_shift(rr, shift) == jnp.right_shift(cc, shift)
    return jnp.where(jnp.logical_and(same, rr >= cc), 1.0, 0.0).astype(F32)


def _halo_specs(tm, width, n_blocks_seq, n_blocks):
    per = tm // HALO
    prev = pl.BlockSpec((HALO, width), lambda i: (jnp.maximum(i * per - 1, 0), 0))
    nxt = pl.BlockSpec((HALO, width), lambda i: (jnp.minimum((i + 1) * per, n_blocks * per - 1), 0))
    return prev, nxt


def _split_heads(ref, base):
    return [ref[:, base + h * DN_HEAD_DIM: base + (h + 1) * DN_HEAD_DIM] for h in range(DN_HEADS)]


def _dn_prep_fwd(qkv, bpre, apre, conv_w, alog, dtb, seq, *, name):
    t = qkv.shape[0]
    tm = _tm(t)
    bps = seq // tm
    cw = 3 * DN_WIDTH

    def body(x_ref, halo_ref, b_ref, a_ref, w_ref, alog_ref, dtb_ref, q_ref, k_ref, v_ref, beta_ref, gc_ref):
        i = pl.program_id(0)
        keep = jnp.where(i % bps == 0, 0.0, 1.0)
        ext = jnp.concatenate([halo_ref[...] * keep, x_ref[...]], axis=0)
        y = _conv_taps(ext, w_ref[...], tm)
        yq = [y[:, h * DN_HEAD_DIM:(h + 1) * DN_HEAD_DIM] for h in range(DN_HEADS)]
        yk = [y[:, DN_WIDTH + h * DN_HEAD_DIM: DN_WIDTH + (h + 1) * DN_HEAD_DIM] for h in range(DN_HEADS)]
        q, k, v, beta, g = _post_conv(yq, yk, y[:, 2 * DN_WIDTH:], b_ref[...], a_ref[...], alog_ref[...], dtb_ref[...])
        for h in range(DN_HEADS):
            q_ref[:, h * DN_HEAD_DIM:(h + 1) * DN_HEAD_DIM] = q[h]
            k_ref[:, h * DN_HEAD_DIM:(h + 1) * DN_HEAD_DIM] = k[h]
        v_ref[...] = v
        beta_ref[...] = beta
        gc_ref[...] = mmx(_chunk_tril(tm), g)

    row = lambda i: (i, 0)
    const = lambda i: (0, 0)
    prev, _ = _halo_specs(tm, cw, bps, t // tm)
    return pl.pallas_call(
        body, name=name, grid=(t // tm,),
        in_specs=[pl.BlockSpec((tm, cw), row), prev, pl.BlockSpec((tm, LANES), row), pl.BlockSpec((tm, LANES), row),
                  pl.BlockSpec((CONV_K, cw), const), pl.BlockSpec((1, LANES), const), pl.BlockSpec((1, LANES), const)],
        out_specs=tuple(pl.BlockSpec((tm, n), row) for n in (DN_WIDTH, DN_WIDTH, DN_WIDTH, LANES, LANES)),
        out_shape=tuple(jax.ShapeDtypeStruct((t, n), F32) for n in (DN_WIDTH, DN_WIDTH, DN_WIDTH, LANES, LANES)),
        compiler_params=_cparams(("parallel",)),
    )(qkv, qkv, bpre, apre, conv_w, alog, dtb)


def _dn_prep_bwd(qkv, bpre, apre, conv_w, alog, dtb, dq, dk, dv, dbeta, dgc, dgc2, seq, *, name):
    t = qkv.shape[0]
    tm = _tm(t)
    bps = seq // tm
    cw = 3 * DN_WIDTH

    def body(x_ref, halo_ref, b_ref, a_ref, w_ref, alog_ref, dtb_ref, dq_ref, dk_ref, dv_ref, dbeta_ref, dgc_ref, dgc2_ref,
             dy_ref, db_ref, da_ref, dalog_ref, ddtb_ref):
        i = pl.program_id(0)
        keep = jnp.where(i % bps == 0, 0.0, 1.0)
        ext = jnp.concatenate([halo_ref[...] * keep, x_ref[...]], axis=0)
        y = _conv_taps(ext, w_ref[...], tm)
        yq = [y[:, h * DN_HEAD_DIM:(h + 1) * DN_HEAD_DIM] for h in range(DN_HEADS)]
        yk = [y[:, DN_WIDTH + h * DN_HEAD_DIM: DN_WIDTH + (h + 1) * DN_HEAD_DIM] for h in range(DN_HEADS)]
        _, vjp = jax.vjp(_post_conv, yq, yk, y[:, 2 * DN_WIDTH:], b_ref[...], a_ref[...], alog_ref[...], dtb_ref[...])
        dg = mmx_tn(_chunk_tril(tm), dgc_ref[...] + dgc2_ref[...])
        dyq, dyk, dyv, db, da, dalog, ddtb = vjp((_split_heads(dq_ref, 0), _split_heads(dk_ref, 0), dv_ref[...],
                                                  dbeta_ref[...], dg))
        for h in range(DN_HEADS):
            dy_ref[:, h * DN_HEAD_DIM:(h + 1) * DN_HEAD_DIM] = dyq[h]
            dy_ref[:, DN_WIDTH + h * DN_HEAD_DIM: DN_WIDTH + (h + 1) * DN_HEAD_DIM] = dyk[h]
        dy_ref[:, 2 * DN_WIDTH:] = dyv
        db_ref[...] = db.astype(BF16)
        da_ref[...] = da.astype(BF16)
        _acc_out(dalog_ref, i == 0, dalog)
        _acc_out(ddtb_ref, i == 0, ddtb)

    row = lambda i: (i, 0)
    const = lambda i: (0, 0)
    prev, _ = _halo_specs(tm, cw, bps, t // tm)
    return pl.pallas_call(
        body, name=name, grid=(t // tm,),
        in_specs=[pl.BlockSpec((tm, cw), row), prev, pl.BlockSpec((tm, LANES), row), pl.BlockSpec((tm, LANES), row),
                  pl.BlockSpec((CONV_K, cw), const), pl.BlockSpec((1, LANES), const), pl.BlockSpec((1, LANES), const),
                  pl.BlockSpec((tm, DN_WIDTH), row), pl.BlockSpec((tm, DN_WIDTH), row), pl.BlockSpec((tm, DN_WIDTH), row),
                  pl.BlockSpec((tm, LANES), row), pl.BlockSpec((tm, LANES), row), pl.BlockSpec((tm, LANES), row)],
        out_specs=(pl.BlockSpec((tm, cw), row), pl.BlockSpec((tm, LANES), row), pl.BlockSpec((tm, LANES), row),
                   pl.BlockSpec((1, LANES), const), pl.BlockSpec((1, LANES), const)),
        out_shape=(jax.ShapeDtypeStruct((t, cw), F32), jax.ShapeDtypeStruct((t, LANES), BF16), jax.ShapeDtypeStruct((t, LANES), BF16),
                   jax.ShapeDtypeStruct((1, LANES), F32), jax.ShapeDtypeStruct((1, LANES), F32)),
        compiler_params=_cparams(("arbitrary",)),
    )(qkv, qkv, bpre, apre, conv_w, alog, dtb, dq, dk, dv, dbeta, dgc, dgc2)


def _conv_bwd(qkv, dy, conv_w, seq, *, name):
    t = qkv.shape[0]
    tm = _tm(t)
    bps = seq // tm
    cw = 3 * DN_WIDTH
    n_ext = tm + HALO

    def body(x_ref, halo_ref, dy_ref, dyn_ref, w_ref, dx_ref, dw_ref):
        i = pl.program_id(0)
        keep_prev = jnp.where(i % bps == 0, 0.0, 1.0)
        keep_next = jnp.where(i % bps == bps - 1, 0.0, 1.0)
        ext = jnp.concatenate([halo_ref[...] * keep_prev, x_ref[...]], axis=0)
        dy = dy_ref[...]
        dyext = jnp.concatenate([dy, dyn_ref[...] * keep_next], axis=0)
        w = w_ref[...]

        @pl.when(i == 0)
        def _():
            dw_ref[...] = jnp.zeros_like(dw_ref)

        dx = None
        for j in range(CONV_K):
            s = CONV_K - 1 - j
            fut = dyext if s == 0 else pltpu.roll(dyext, n_ext - s, 0)
            term = w[j:j + 1, :] * fut[0:tm, :]
            dx = term if dx is None else dx + term
            past = ext if s == 0 else pltpu.roll(ext, s, 0)
            dw_ref[j:j + 1, :] += jnp.sum(dy * past[HALO:HALO + tm, :], axis=0, keepdims=True)
        dx_ref[...] = dx.astype(BF16)

    row = lambda i: (i, 0)
    const = lambda i: (0, 0)
    prev, nxt = _halo_specs(tm, cw, bps, t // tm)
    return pl.pallas_call(
        body, name=name, grid=(t // tm,),
        in_specs=[pl.BlockSpec((tm, cw), row), prev, pl.BlockSpec((tm, cw), row), nxt, pl.BlockSpec((CONV_K, cw), const)],
        out_specs=(pl.BlockSpec((tm, cw), row), pl.BlockSpec((HALO, cw), const)),
        out_shape=(jax.ShapeDtypeStruct((t, cw), BF16), jax.ShapeDtypeStruct((HALO, cw), F32)),
        compiler_params=_cparams(("arbitrary",)),
    )(qkv, qkv, dy, dy, conv_w)


def _inv_unit_lower(l_mats, eye):
    invs = [eye - l for l in l_mats]
    powers = list(l_mats)
    n = 2
    while n < eye.shape[0]:
        powers = [mmh(p, p) for p in powers]
        invs = [inv + mmh(inv, p) for inv, p in zip(invs, powers)]
        n *= 2
    return invs


@jax.custom_vjp
def _solve(l_mat, rhs, inv):
    return mmh(inv, rhs)


def _solve_fwd(l_mat, rhs, inv):
    sol = mmh(inv, rhs)
    return sol, (inv, sol)


def _solve_bwd(res, d_sol):
    inv, sol = res
    d_rhs = mm_tn(inv, d_sol)
    return -mm_nt(d_rhs, sol), d_rhs, jnp.zeros_like(inv)


_solve.defvjp(_solve_fwd, _solve_bwd)


def _prep_fn(q, k, v, gc, gr, b, inv):
    ids = range(len(q))
    c = q[0].shape[0]
    rr = lax.broadcasted_iota(jnp.int32, (c, c), 0)
    cc = lax.broadcasted_iota(jnp.int32, (c, c), 1)
    incl, strict = rr >= cc, rr > cc
    is_last = lax.broadcasted_iota(jnp.int32, (c, 1), 0) == c - 1
    qs = [q[i] * (DN_HEAD_DIM ** -0.5) for i in ids]
    decay = [jnp.where(incl, jnp.exp(jnp.where(incl, gc[i] - gr[i], 0.0)), 0.0) for i in ids]
    kb = [k[i] * b[i] for i in ids]
    vb = [v[i] * b[i] for i in ids]
    kk = [mm_nt(kb[i], k[i]) for i in ids]
    l_mat = [jnp.where(strict, kk[i] * decay[i], 0.0) for i in ids]
    eg = [jnp.exp(gc[i]) for i in ids]
    if inv is None:
        inv = _inv_unit_lower(l_mat, jnp.where(rr == cc, 1.0, 0.0).astype(F32))
    u_wy = [_solve(l_mat[i], vb[i], inv[i]) for i in ids]
    w_wy = [_solve(l_mat[i], kb[i] * eg[i], inv[i]) for i in ids]
    qk = [mm_nt(qs[i], k[i]) * decay[i] for i in ids]
    g_last = [jnp.sum(jnp.where(is_last, gc[i], 0.0), axis=0, keepdims=True) for i in ids]
    k_dec = [k[i] * jnp.exp(g_last[i] - gc[i]) for i in ids]
    egl = [jnp.broadcast_to(jnp.exp(g_last[i]), (1, LANES)) for i in ids]
    return [(w_wy[i], u_wy[i], qs[i] * eg[i], k_dec[i], qk[i], egl[i]) for i in ids], inv


def _seq_fn(w, u, qd, kd, qk, egl, s):
    ids = range(len(w))
    ws = [mm(w[i], s[i]) for i in ids]
    qs = [mm(qd[i], s[i]) for i in ids]
    v_new = [u[i] - ws[i] for i in ids]
    o = [qs[i] + mm(qk[i], v_new[i]) for i in ids]
    s_new = [s[i] * egl[i] + mm_tn(kd[i], v_new[i]) for i in ids]
    return o, s_new


def _lane_col(a, h):
    lane = lax.broadcasted_iota(jnp.int32, (1, LANES), 1)
    return jnp.sum(jnp.where(lane == h, a, 0.0), axis=1, keepdims=True)


def _col_lane(col, h):
    lane = lax.broadcasted_iota(jnp.int32, (1, LANES), 1)
    return jnp.where(lane == h, col, 0.0)


def _head_cols(h):
    return slice(h * DN_HEAD_DIM, (h + 1) * DN_HEAD_DIM)


def _chunk_rows(n):
    return pl.ds(pl.multiple_of(n * DN_CHUNK, DN_CHUNK), DN_CHUNK)


def _delta_prep(q, k, v, gc, grow, beta, *, name):
    t = q.shape[0]
    tm = _tm(t)
    cpb = tm // DN_CHUNK
    n_chunks = t // DN_CHUNK
    group = 2

    def body(q_ref, k_ref, v_ref, gc_ref, gr_ref, b_ref, w_ref, u_ref, qd_ref, kd_ref, qk_ref, egl_ref, inv_ref):
        def step(m, carry):
            probs = [(m * group + e, h) for e in range(group) for h in range(DN_HEADS)]
            gcb = [gc_ref[_chunk_rows(m * group + e), :] for e in range(group)]
            bb = [b_ref[_chunk_rows(m * group + e), :] for e in range(group)]
            grb = [gr_ref[m * group + e] for e in range(group)]
            for e in range(group):
                egl_ref[m * group + e] = jnp.zeros((HALO, LANES), F32)
            outs, invs = _prep_fn(
                [q_ref[_chunk_rows(n), _head_cols(h)] for n, h in probs], [k_ref[_chunk_rows(n), _head_cols(h)] for n, h in probs],
                [v_ref[_chunk_rows(n), _head_cols(h)] for n, h in probs],
                [_lane_col(gcb[e], h) for e in range(group) for h in range(DN_HEADS)],
                [grb[e][h:h + 1, :] for e in range(group) for h in range(DN_HEADS)],
                [_lane_col(bb[e], h) for e in range(group) for h in range(DN_HEADS)], None)
            for (n, h), (w, u, qd, kd, qk, egl), inv in zip(probs, outs, invs):
                rows, cols = _chunk_rows(n), _head_cols(h)
                w_ref[rows, cols] = w.astype(BF16)
                u_ref[rows, cols] = u
                qd_ref[rows, cols] = qd.astype(BF16)
                kd_ref[rows, cols] = kd.astype(BF16)
                qk_ref[n, h] = qk
                inv_ref[n, h] = inv
                egl_ref[n, h:h + 1, :] = egl
            return carry

        lax.fori_loop(0, cpb // group, step, 0)

    row = lambda i: (i, 0)
    tok = pl.BlockSpec((tm, DN_WIDTH), row)
    lanes = pl.BlockSpec((tm, LANES), row)
    sq = pl.BlockSpec((cpb, DN_HEADS, DN_CHUNK, DN_CHUNK), lambda i: (i, 0, 0, 0))
    return pl.pallas_call(
        body, name=name, grid=(t // tm,),
        in_specs=[tok, tok, tok, lanes, pl.BlockSpec((cpb, HALO, DN_CHUNK), lambda i: (i, 0, 0)), lanes],
        out_specs=(tok, tok, tok, tok, sq, pl.BlockSpec((cpb, HALO, LANES), lambda i: (i, 0, 0)), sq),
        out_shape=(jax.ShapeDtypeStruct((t, DN_WIDTH), BF16), jax.ShapeDtypeStruct((t, DN_WIDTH), F32),
                   jax.ShapeDtypeStruct((t, DN_WIDTH), BF16), jax.ShapeDtypeStruct((t, DN_WIDTH), BF16),
                   jax.ShapeDtypeStruct((n_chunks, DN_HEADS, DN_CHUNK, DN_CHUNK), F32),
                   jax.ShapeDtypeStruct((n_chunks, HALO, LANES), F32),
                   jax.ShapeDtypeStruct((n_chunks, DN_HEADS, DN_CHUNK, DN_CHUNK), F32)),
        compiler_params=_cparams(("parallel",)),
    )(q, k, v, gc, grow, beta)


def _delta_par_bwd(q, k, v, gc, grow, beta, inv, dw, du, dqd, dkd, dqk, degl, *, name):
    t = q.shape[0]
    tm = _tm(t)
    cpb = tm // DN_CHUNK
    n_chunks = t // DN_CHUNK
    group = 2

    def body(q_ref, k_ref, v_ref, gc_ref, gr_ref, b_ref, inv_ref, dw_ref, du_ref, dqd_ref, dkd_ref, dqk_ref, degl_ref,
             dq_ref, dk_ref, dv_ref, dgc_ref, dgr_ref, db_ref):
        def step(m, carry):
            chunks = [m * group + e for e in range(group)]
            probs = [(e, h) for e in range(group) for h in range(DN_HEADS)]
            rows = [_chunk_rows(n) for n in chunks]
            gcb, bb = [gc_ref[r, :] for r in rows], [b_ref[r, :] for r in rows]
            grb, deglb = [gr_ref[n] for n in chunks], [degl_ref[n] for n in chunks]
            for n in chunks:
                dgr_ref[n] = jnp.zeros((HALO, DN_CHUNK), F32)
            invs = [inv_ref[chunks[e], h] for e, h in probs]
            _, vjp = jax.vjp(lambda *a: _prep_fn(*a, invs)[0],
                             [q_ref[rows[e], _head_cols(h)] for e, h in probs], [k_ref[rows[e], _head_cols(h)] for e, h in probs],
                             [v_ref[rows[e], _head_cols(h)] for e, h in probs], [_lane_col(gcb[e], h) for e, h in probs],
                             [grb[e][h:h + 1, :] for e, h in probs], [_lane_col(bb[e], h) for e, h in probs])
            dq, dk, dv, dgc, dgr, db = vjp([(dw_ref[rows[e], _head_cols(h)], du_ref[rows[e], _head_cols(h)],
                                             dqd_ref[rows[e], _head_cols(h)], dkd_ref[rows[e], _head_cols(h)],
                                             dqk_ref[chunks[e], h], deglb[e][h:h + 1, :]) for e, h in probs])
            dgc_acc = [jnp.zeros((DN_CHUNK, LANES), F32) for _ in chunks]
            db_acc = [jnp.zeros((DN_CHUNK, LANES), F32) for _ in chunks]
            for i, (e, h) in enumerate(probs):
                cols = _head_cols(h)
                dq_ref[rows[e], cols] = dq[i]
                dk_ref[rows[e], cols] = dk[i]
                dv_ref[rows[e], cols] = dv[i]
                dgr_ref[chunks[e], h:h + 1, :] = dgr[i]
                dgc_acc[e] = dgc_acc[e] + _col_lane(dgc[i], h)
                db_acc[e] = db_acc[e] + _col_lane(db[i], h)
            for e in range(group):
                dgc_ref[rows[e], :] = dgc_acc[e]
                db_ref[rows[e], :] = db_acc[e]
            return carry

        lax.fori_loop(0, cpb // group, step, 0)

    row = lambda i: (i, 0)
    tok = pl.BlockSpec((tm, DN_WIDTH), row)
    lanes = pl.BlockSpec((tm, LANES), row)
    sq = pl.BlockSpec((cpb, DN_HEADS, DN_CHUNK, DN_CHUNK), lambda i: (i, 0, 0, 0))
    grs = pl.BlockSpec((cpb, HALO, DN_CHUNK), lambda i: (i, 0, 0))
    return pl.pallas_call(
        body, name=name, grid=(t // tm,),
        in_specs=[tok, tok, tok, lanes, grs, lanes, sq, tok, tok, tok, tok, sq, pl.BlockSpec((cpb, HALO, LANES), lambda i: (i, 0, 0))],
        out_specs=(tok, tok, tok, lanes, grs, lanes),
        out_shape=(jax.ShapeDtypeStruct((t, DN_WIDTH), F32),) * 3
        + (jax.ShapeDtypeStruct((t, LANES), F32), jax.ShapeDtypeStruct((n_chunks, HALO, DN_CHUNK), F32),
           jax.ShapeDtypeStruct((t, LANES), F32)),
        compiler_params=_cparams(("parallel",)),
    )(q, k, v, gc, grow, beta, inv, dw, du, dqd, dkd, dqk, degl)


def _seq_specs(n_seq, seq, reverse):
    tm = _tm(seq)
    nb = seq // tm
    cpb = tm // DN_CHUNK
    pair = 2 if n_seq % 2 == 0 else 1
    blk = (lambda j: nb - 1 - j) if reverse else (lambda j: j)
    tok = pl.BlockSpec((pair, tm, DN_WIDTH), lambda b, j: (b, blk(j), 0))
    sq = pl.BlockSpec((pair, cpb, DN_HEADS, DN_CHUNK, DN_CHUNK), lambda b, j: (b, blk(j), 0, 0, 0))
    rows8 = pl.BlockSpec((pair, cpb, HALO, LANES), lambda b, j: (b, blk(j), 0, 0))
    state = pl.BlockSpec((pair, cpb, DN_HEADS, DN_HEAD_DIM, DN_HEAD_DIM), lambda b, j: (b, blk(j), 0, 0, 0))
    return nb, cpb, pair, tok, sq, rows8, state


def _by_seq(a, n_seq):
    return a.reshape((n_seq, a.shape[0] // n_seq) + a.shape[1:])


def _flat_seq(a):
    return a.reshape((a.shape[0] * a.shape[1],) + a.shape[2:])


def _delta_seq_fwd(w, u, qd, kd, qk, egl, n_seq, seq, *, name):
    nb, cpb, pair, tok, sq, rows8, state = _seq_specs(n_seq, seq, False)
    probs = [(e, h) for e in range(pair) for h in range(DN_HEADS)]

    def body(w_ref, u_ref, qd_ref, kd_ref, qk_ref, egl_ref, o_ref, st_ref, s_s):
        @pl.when(pl.program_id(1) == 0)
        def _():
            s_s[...] = jnp.zeros_like(s_s)

        def step(n, carry):
            rows = _chunk_rows(n)
            eglb = [egl_ref[e, n] for e in range(pair)]
            s = [s_s[e, h] for e, h in probs]
            for (e, h), s_eh in zip(probs, s):
                st_ref[e, n, h] = s_eh
            o, s_new = _seq_fn([w_ref[e, rows, _head_cols(h)] for e, h in probs], [u_ref[e, rows, _head_cols(h)] for e, h in probs],
                               [qd_ref[e, rows, _head_cols(h)] for e, h in probs], [kd_ref[e, rows, _head_cols(h)] for e, h in probs],
                               [qk_ref[e, n, h] for e, h in probs], [eglb[e][h:h + 1, :] for e, h in probs], s)
            for i, (e, h) in enumerate(probs):
                o_ref[e, rows, _head_cols(h)] = o[i]
                s_s[e, h] = s_new[i]
            return carry

        lax.fori_loop(0, cpb, step, 0)

    o, states = pl.pallas_call(
        body, name=name, grid=(n_seq // pair, nb),
        in_specs=[tok, tok, tok, tok, sq, rows8],
        out_specs=(tok, state),
        out_shape=(jax.ShapeDtypeStruct((n_seq, seq, DN_WIDTH), F32),
                   jax.ShapeDtypeStruct((n_seq, seq // DN_CHUNK, DN_HEADS, DN_HEAD_DIM, DN_HEAD_DIM), F32)),
        scratch_shapes=[pltpu.VMEM((pair, DN_HEADS, DN_HEAD_DIM, DN_HEAD_DIM), F32)],
        compiler_params=_cparams(("parallel", "arbitrary")),
    )(*[_by_seq(a, n_seq) for a in (w, u, qd, kd, qk, egl)])
    return _flat_seq(o), _flat_seq(states)


def _delta_seq_bwd(w, u, qd, kd, qk, egl, states, do, n_seq, seq, *, name):
    nb, cpb, pair, tok, sq, rows8, state = _seq_specs(n_seq, seq, True)
    probs = [(e, h) for e in range(pair) for h in range(DN_HEADS)]

    def body(w_ref, u_ref, qd_ref, kd_ref, qk_ref, egl_ref, st_ref, do_ref, dw_ref, du_ref, dqd_ref, dkd_ref, dqk_ref,
             degl_ref, ds_s):
        @pl.when(pl.program_id(1) == 0)
        def _():
            ds_s[...] = jnp.zeros_like(ds_s)

        def step(m, carry):
            n = cpb - 1 - m
            rows = _chunk_rows(n)
            eglb = [egl_ref[e, n] for e in range(pair)]
            for e in range(pair):
                degl_ref[e, n] = jnp.zeros((HALO, LANES), F32)
            _, vjp = jax.vjp(_seq_fn, [w_ref[e, rows, _head_cols(h)].astype(F32) for e, h in probs],
                             [u_ref[e, rows, _head_cols(h)] for e, h in probs],
                             [qd_ref[e, rows, _head_cols(h)].astype(F32) for e, h in probs],
                             [kd_ref[e, rows, _head_cols(h)].astype(F32) for e, h in probs],
                             [qk_ref[e, n, h] for e, h in probs], [eglb[e][h:h + 1, :] for e, h in probs],
                             [st_ref[e, n, h] for e, h in probs])
            dw, du, dqd, dkd, dqk, degl, ds_in = vjp(([do_ref[e, rows, _head_cols(h)] for e, h in probs],
                                                      [ds_s[e, h] for e, h in probs]))
            for i, (e, h) in enumerate(probs):
                cols = _head_cols(h)
                dw_ref[e, rows, cols] = dw[i]
                du_ref[e, rows, cols] = du[i]
                dqd_ref[e, rows, cols] = dqd[i]
                dkd_ref[e, rows, cols] = dkd[i]
                dqk_ref[e, n, h] = dqk[i]
                degl_ref[e, n, h:h + 1, :] = degl[i]
                ds_s[e, h] = ds_in[i]
            return carry

        lax.fori_loop(0, cpb, step, 0)

    nc = seq // DN_CHUNK
    outs = pl.pallas_call(
        body, name=name, grid=(n_seq // pair, nb),
        in_specs=[tok, tok, tok, tok, sq, rows8, state, tok],
        out_specs=(tok, tok, tok, tok, sq, rows8),
        out_shape=(jax.ShapeDtypeStruct((n_seq, seq, DN_WIDTH), F32),) * 4
        + (jax.ShapeDtypeStruct((n_seq, nc, DN_HEADS, DN_CHUNK, DN_CHUNK), F32),
           jax.ShapeDtypeStruct((n_seq, nc, HALO, LANES), F32)),
        scratch_shapes=[pltpu.VMEM((pair, DN_HEADS, DN_HEAD_DIM, DN_HEAD_DIM), F32)],
        compiler_params=_cparams(("parallel", "arbitrary")),
    )(*[_by_seq(a, n_seq) for a in (w, u, qd, kd, qk, egl, states, do)])
    return tuple(_flat_seq(a) for a in outs)


def _dn_gate(o, z, dnw):
    return o * lax.rsqrt(jnp.mean(o * o, axis=-1, keepdims=True) + EPS) * dnw * _silu(z)


def _mix_out_fwd(x, sg, o, z, wo_sg, wo_dn, dnw, *, name):
    t = x.shape[0]
    tm = _tm(t)

    def body(x_ref, sg_ref, o_ref, z_ref, wsg_ref, wdn_ref, dnw_ref, y_ref, dn_s):
        for h, (oh, zh) in enumerate(zip(_split_heads(o_ref, 0), _split_heads(z_ref, 0))):
            dn_s[:, h * DN_HEAD_DIM:(h + 1) * DN_HEAD_DIM] = _dn_gate(oh, zh, dnw_ref[...]).astype(BF16)
        y_ref[...] = (x_ref[...] + jnp.dot(sg_ref[...].astype(BF16), wsg_ref[...], preferred_element_type=F32)
                      + jnp.dot(dn_s[...], wdn_ref[...], preferred_element_type=F32))

    row = lambda i: (i, 0)
    const = lambda i: (0, 0)
    half = pl.BlockSpec((tm, DN_WIDTH), row)
    return pl.pallas_call(
        body, name=name, grid=(t // tm,),
        in_specs=[pl.BlockSpec((tm, D_MODEL), row), half, half, half, pl.BlockSpec((SG_WIDTH, D_MODEL), const),
                  pl.BlockSpec((DN_WIDTH, D_MODEL), const), pl.BlockSpec((1, DN_HEAD_DIM), const)],
        out_specs=pl.BlockSpec((tm, D_MODEL), row),
        out_shape=jax.ShapeDtypeStruct((t, D_MODEL), F32),
        scratch_shapes=[pltpu.VMEM((tm, DN_WIDTH), BF16)],
        compiler_params=_cparams(("parallel",)),
    )(x, sg, o, z, wo_sg, wo_dn, dnw)


def _mix_out_bwd(dy, sg, o, z, wo_sg, wo_dn, dnw, *, name):
    t = dy.shape[0]
    tm = _tm(t)

    def body(dy_ref, sg_ref, o_ref, z_ref, wsg_ref, wdn_ref, dnw_ref, dsg_ref, do_ref, dz_ref, dwsg_ref, dwdn_ref, ddnw_ref, dn_s):
        i = pl.program_id(0)
        dyb = dy_ref[...].astype(BF16)
        nt = (((1,), (1,)), ((), ()))
        tn = (((0,), (0,)), ((), ()))
        dsg_ref[...] = lax.dot_general(dyb, wsg_ref[...], nt, preferred_element_type=F32)
        ddn = lax.dot_general(dyb, wdn_ref[...], nt, preferred_element_type=F32)
        ddnw = None
        for h, (oh, zh) in enumerate(zip(_split_heads(o_ref, 0), _split_heads(z_ref, 0))):
            cols = slice(h * DN_HEAD_DIM, (h + 1) * DN_HEAD_DIM)
            out, vjp = jax.vjp(_dn_gate, oh, zh, dnw_ref[...])
            dn_s[:, cols] = out.astype(BF16)
            doh, dzh, dw = vjp(ddn[:, cols])
            do_ref[:, cols] = doh
            dz_ref[:, cols] = dzh.astype(BF16)
            ddnw = dw if ddnw is None else ddnw + dw
        _acc_out(ddnw_ref, i == 0, ddnw)
        _acc_out(dwsg_ref, i == 0, lax.dot_general(sg_ref[...].astype(BF16), dyb, tn, preferred_element_type=F32))
        _acc_out(dwdn_ref, i == 0, lax.dot_general(dn_s[...], dyb, tn, preferred_element_type=F32))

    row = lambda i: (i, 0)
    const = lambda i: (0, 0)
    half = pl.BlockSpec((tm, DN_WIDTH), row)
    wspec = pl.BlockSpec((DN_WIDTH, D_MODEL), const)
    return pl.pallas_call(
        body, name=name, grid=(t // tm,),
        in_specs=[pl.BlockSpec((tm, D_MODEL), row), half, half, half, wspec, wspec, pl.BlockSpec((1, DN_HEAD_DIM), const)],
        out_specs=(half, half, half, wspec, wspec, pl.BlockSpec((1, DN_HEAD_DIM), const)),
        out_shape=(jax.ShapeDtypeStruct((t, DN_WIDTH), F32),) * 2 + (jax.ShapeDtypeStruct((t, DN_WIDTH), BF16),)
        + (jax.ShapeDtypeStruct((DN_WIDTH, D_MODEL), F32),) * 2 + (jax.ShapeDtypeStruct((1, DN_HEAD_DIM), F32),),
        scratch_shapes=[pltpu.VMEM((tm, DN_WIDTH), BF16)],
        compiler_params=_cparams(("arbitrary",)),
    )(dy, sg, o, z, wo_sg, wo_dn, dnw)


_MESH = pl.DeviceIdType.MESH
_HBM = pl.BlockSpec(memory_space=pl.ANY)


def _mesh_pos():
    x, y, c = lax.axis_index("x"), lax.axis_index("y"), lax.axis_index("c")
    return x, y, c, [(1 - x, y), (x, 1 - y), (1 - x, 1 - y)]


def _gather2(arrs, *, name):
    n = len(arrs)
    slots = N_DEV - 1

    def body(*refs):
        in_refs, out_refs = refs[:n], refs[n:2 * n]
        send_sems, recv_sems, local_sems = refs[2 * n:]
        x, y, c, chips = _mesh_pos()
        me, sibling = (x, y, c), (x, y, 1 - c)

        def copy(k, slot, block, to, src=None):
            dst = out_refs[k].at[4 * block[0] + 2 * block[1] + block[2]]
            return pltpu.make_async_remote_copy(src_ref=dst if src is None else src, dst_ref=dst,
                                                send_sem=send_sems.at[k * slots + slot], recv_sem=recv_sems.at[k * slots + slot],
                                                device_id=to, device_id_type=_MESH)

        local = [pltpu.make_async_copy(in_refs[k], out_refs[k].at[4 * x + 2 * y + c], local_sems.at[k]) for k in range(n)]
        sent = []
        for k in range(n):
            sent.append(copy(k, 0, me, sibling, src=in_refs[k]))
            sent += [copy(k, 1 + j, me, (*chip, c), src=in_refs[k]) for j, chip in enumerate(chips)]
        for cp in local + sent:
            cp.start()
        for j, chip in enumerate(chips):
            for k in range(n):
                copy(k, 1 + j, (*chip, c), me).wait_recv()
                passed = copy(k, 4 + j, (*chip, c), sibling)
                passed.start()
                sent.append(passed)
        for k in range(n):
            copy(k, 0, sibling, me).wait_recv()
            for j, chip in enumerate(chips):
                copy(k, 4 + j, (*chip, 1 - c), me).wait_recv()
        for cp in sent:
            cp.wait_send()
        for cp in local:
            cp.wait()

    return pl.pallas_call(
        body, name=name, in_specs=[_HBM] * n, out_specs=(_HBM,) * n,
        out_shape=tuple(jax.ShapeDtypeStruct((N_DEV,) + a.shape, a.dtype) for a in arrs),
        scratch_shapes=[pltpu.SemaphoreType.DMA((n * slots,)), pltpu.SemaphoreType.DMA((n * slots,)),
                        pltpu.SemaphoreType.DMA((n,))],
    )(*arrs)


_SEM = pl.BlockSpec(memory_space=pltpu.SEMAPHORE)
_EFFECT = pltpu.SideEffectType.DATAFLOW_SIDE_EFFECTING


def _direct_copies(src_refs, land_refs, send_sems, recv_sems, gather):
    x, y, c, _ = _mesh_pos()
    me = 4 * x + 2 * y + c
    n_peer = N_DEV - 1
    copies = []
    for r in range(1, N_DEV):
        px = 1 - x if r & 4 else x
        py = 1 - y if r & 2 else y
        pc = 1 - c if r & 1 else c
        for k, (src, land) in enumerate(zip(src_refs, land_refs)):
            copies.append(pltpu.make_async_remote_copy(
                src_ref=src if gather else src.at[4 * px + 2 * py + pc], dst_ref=land.at[me],
                send_sem=send_sems.at[k * n_peer + r - 1], recv_sem=recv_sems.at[k * n_peer + r - 1],
                device_id=(px, py, pc), device_id_type=_MESH))
    return copies


def _send_start(arrs, gather, after=None, *, name):
    n = len(arrs)
    lands = [lax.empty(((N_DEV,) + a.shape) if gather else a.shape, a.dtype) for a in arrs]
    n_in = 2 * n + (0 if after is None else 1)

    def body(*refs):
        src_refs, land_refs, send_sems, recv_sems, token = refs[:n], refs[n:2 * n], refs[n_in], refs[n_in + 1], refs[-1]
        for cp in _direct_copies(src_refs, land_refs, send_sems, recv_sems, gather):
            cp.start()
        token[...] = jnp.zeros_like(token)

    n_sem = n * (N_DEV - 1)
    bufs = list(arrs) + lands
    out = pl.pallas_call(
        body, name=name,
        out_shape=(pltpu.SemaphoreType.DMA((n_sem,)), pltpu.SemaphoreType.DMA((n_sem,)))
        + tuple(pltpu.HBM(b.shape, b.dtype) for b in bufs) + (jax.ShapeDtypeStruct((HALO, LANES), F32),),
        in_specs=[_HBM] * n_in, out_specs=(_SEM, _SEM) + (_HBM,) * (2 * n) + (pl.BlockSpec(memory_space=pltpu.VMEM),),
        input_output_aliases={i: 2 + i for i in range(2 * n)},
        compiler_params=pltpu.CompilerParams(has_side_effects=_EFFECT),
    )(*[pltpu.with_memory_space_constraint(b, pltpu.HBM) for b in bufs], *([] if after is None else [after]))
    return (out[0], out[1], list(out[2:2 + n]), list(out[2 + n:2 + 2 * n])), out[-1]


def _send_wait(started, gather, after, *, name):
    send_sems, recv_sems, srcs, lands = started
    n = len(srcs)

    def body(*refs):
        src_refs, land_refs, send_ref, recv_ref = refs[:n], refs[n:2 * n], refs[2 * n], refs[2 * n + 1]
        for cp in _direct_copies(src_refs, land_refs, send_ref, recv_ref, gather):
            cp.wait_send()
            cp.wait_recv()

    bufs = srcs + lands
    out = pl.pallas_call(
        body, name=name, out_shape=tuple(pltpu.HBM(b.shape, b.dtype) for b in bufs),
        in_specs=[_HBM] * (2 * n) + [_SEM, _SEM, _HBM], out_specs=(_HBM,) * (2 * n),
        input_output_aliases={i: i for i in range(2 * n)},
        compiler_params=pltpu.CompilerParams(has_side_effects=_EFFECT),
    )(*bufs, send_sems, recv_sems, after)
    return list(out[n:])


def _row_block(rows, limit=256):
    best = rows
    for cand in range(8, limit + 1, 8):
        if rows % cand == 0:
            best = cand
    return best if rows > limit else rows


def _adam(gp, w, m, v, *, name):
    p, rows, cols = gp.shape
    rb = _row_block(rows)

    def body(gp_ref, w_ref, m_ref, v_ref, g_ref, d_ref, m2_ref, v2_ref):
        g = gp_ref[0].astype(F32)
        for s in range(1, p):
            g = g + gp_ref[s].astype(F32)
        m2 = ADAM_B1 * m_ref[...] + (1.0 - ADAM_B1) * g
        v2 = ADAM_B2 * v_ref[...] + (1.0 - ADAM_B2) * (g * g)
        m_hat = m2 / (1.0 - ADAM_B1 ** ADAM_STEP)
        v_hat = v2 / (1.0 - ADAM_B2 ** ADAM_STEP)
        g_ref[...] = g
        d_ref[...] = -ADAM_LR * (m_hat / (jnp.sqrt(v_hat) + ADAM_EPS) + ADAM_WD * w_ref[...])
        m2_ref[...] = m2
        v2_ref[...] = v2

    blk = pl.BlockSpec((rb, cols), lambda i: (i, 0))
    return pl.pallas_call(
        body, name=name, grid=(rows // rb,),
        in_specs=[pl.BlockSpec((p, rb, cols), lambda i: (0, i, 0)), blk, blk, blk],
        out_specs=(blk,) * 4, out_shape=(jax.ShapeDtypeStruct((rows, cols), F32),) * 4,
        compiler_params=_cparams(("parallel",)),
    )(gp, w, m, v)


def _cols_full(g):
    return jnp.transpose(g, (1, 0, 2)).reshape(g.shape[1], N_DEV * g.shape[2])


def _pad_lanes(a, width=LANES):
    return jnp.pad(a, ((0, 0), (0, width - a.shape[1])))


def _chunk_rows_of(a):
    by_chunk = jnp.transpose(a[:, :DN_HEADS].reshape(-1, DN_CHUNK, DN_HEADS), (0, 2, 1))
    return jnp.pad(by_chunk, ((0, 0), (0, HALO - DN_HEADS), (0, 0)))


_SMALL = (("ffn1_norm", D_MODEL), ("mix_norm", D_MODEL), ("ffn2_norm", D_MODEL), ("final_norm", D_MODEL), ("a_log", DN_HEADS),
          ("dt_bias", DN_HEADS), ("dn_norm", DN_HEAD_DIM), ("sg_ln_g", SG_WIDTH), ("sg_ln_b", SG_WIDTH),
          ("sg_w", SG_GROUPS * SG_CHUNK * SG_CHUNK), ("sg_b", SG_GROUPS * SG_CHUNK), ("conv_w", CONV_K * 3 * DN_WIDTH))
_SMALL_ROWS = 1128
_SMALL_SHAPES = {"ffn1_norm": (1, D_MODEL), "mix_norm": (1, D_MODEL), "ffn2_norm": (1, D_MODEL), "final_norm": (D_MODEL,),
                 "a_log": (1, DN_HEADS), "dt_bias": (1, DN_HEADS), "dn_norm": (1, DN_HEAD_DIM), "sg_ln_g": (1, SG_WIDTH),
                 "sg_ln_b": (1, SG_WIDTH), "sg_w": (1, SG_GROUPS, SG_CHUNK, SG_CHUNK), "sg_b": (1, SG_GROUPS, SG_CHUNK)}


def _pack_small(d):
    flat = jnp.concatenate([d[name].reshape(-1) for name, _ in _SMALL])
    return jnp.pad(flat, (0, _SMALL_ROWS * LANES - flat.shape[0])).reshape(_SMALL_ROWS, LANES)


def _unpack_small(a):
    flat, out, at = a.reshape(-1), {}, 0
    for name, size in _SMALL:
        out[name] = flat[at:at + size]
        at += size
    return out


def kernel(x, ffn1_norm, ffn1_w_gate, ffn1_w_up, ffn1_w_down, mix_norm, w_in, conv_w, a_log, dt_bias, dn_norm, sg_ln_g, sg_ln_b, sg_w, sg_b, w_out, ffn2_norm, ffn2_w_gate, ffn2_w_up, ffn2_w_down, final_norm, loss_target, m_ffn1_norm, m_ffn1_w_gate, m_ffn1_w_up, m_ffn1_w_down, m_mix_norm, m_w_in, m_conv_w, m_a_log, m_dt_bias, m_dn_norm, m_sg_ln_g, m_sg_ln_b, m_sg_w, m_sg_b, m_w_out, m_ffn2_norm, m_ffn2_w_gate, m_ffn2_w_up, m_ffn2_w_down, m_final_norm, v_ffn1_norm, v_ffn1_w_gate, v_ffn1_w_up, v_ffn1_w_down, v_mix_norm, v_w_in, v_conv_w, v_a_log, v_dt_bias, v_dn_norm, v_sg_ln_g, v_sg_ln_b, v_sg_w, v_sg_b, v_w_out, v_ffn2_norm, v_ffn2_w_gate, v_ffn2_w_up, v_ffn2_w_down, v_final_norm):
    weights = dict(ffn1_norm=ffn1_norm, ffn1_w_gate=ffn1_w_gate, ffn1_w_up=ffn1_w_up, ffn1_w_down=ffn1_w_down, mix_norm=mix_norm, w_in=w_in, conv_w=conv_w, a_log=a_log, dt_bias=dt_bias, dn_norm=dn_norm, sg_ln_g=sg_ln_g, sg_ln_b=sg_ln_b, sg_w=sg_w, sg_b=sg_b, w_out=w_out, ffn2_norm=ffn2_norm, ffn2_w_gate=ffn2_w_gate, ffn2_w_up=ffn2_w_up, ffn2_w_down=ffn2_w_down, final_norm=final_norm)
    mom_m = dict(ffn1_norm=m_ffn1_norm, ffn1_w_gate=m_ffn1_w_gate, ffn1_w_up=m_ffn1_w_up, ffn1_w_down=m_ffn1_w_down, mix_norm=m_mix_norm, w_in=m_w_in, conv_w=m_conv_w, a_log=m_a_log, dt_bias=m_dt_bias, dn_norm=m_dn_norm, sg_ln_g=m_sg_ln_g, sg_ln_b=m_sg_ln_b, sg_w=m_sg_w, sg_b=m_sg_b, w_out=m_w_out, ffn2_norm=m_ffn2_norm, ffn2_w_gate=m_ffn2_w_gate, ffn2_w_up=m_ffn2_w_up, ffn2_w_down=m_ffn2_w_down, final_norm=m_final_norm)
    mom_v = dict(ffn1_norm=v_ffn1_norm, ffn1_w_gate=v_ffn1_w_gate, ffn1_w_up=v_ffn1_w_up, ffn1_w_down=v_ffn1_w_down, mix_norm=v_mix_norm, w_in=v_w_in, conv_w=v_conv_w, a_log=v_a_log, dt_bias=v_dt_bias, dn_norm=v_dn_norm, sg_ln_g=v_sg_ln_g, sg_ln_b=v_sg_ln_b, sg_w=v_sg_w, sg_b=v_sg_b, w_out=v_w_out, ffn2_norm=v_ffn2_norm, ffn2_w_gate=v_ffn2_w_gate, ffn2_w_up=v_ffn2_w_up, ffn2_w_down=v_ffn2_w_down, final_norm=v_final_norm)
    order = list(weights)
    big = ("ffn1_w_gate", "ffn1_w_up", "ffn1_w_down", "w_in", "w_out", "ffn2_w_gate", "ffn2_w_up", "ffn2_w_down")
    col_sharded = ("ffn1_w_gate", "ffn1_w_up", "w_in", "ffn2_w_gate", "ffn2_w_up")

    n_seq, seq, _ = x.shape
    t = n_seq * seq
    me = 4 * lax.axis_index("x") + 2 * lax.axis_index("y") + lax.axis_index("c")
    x0 = x.reshape(t, D_MODEL)
    tgt = loss_target.reshape(t, D_MODEL)

    def fill_own(land, own_block):
        return lax.dynamic_update_index_in_dim(land, own_block, me, 0)

    def rows_view(n, a):
        return jnp.transpose(a) if n in col_sharded else a

    def as_full(n, g):
        return g.reshape(-1, g.shape[-1])

    shards = {n: rows_view(n, weights[n][0]).astype(BF16) for n in big}
    ffn1_names, mix_names, ffn2_names = big[:3], big[3:5], big[5:]
    full = {n: as_full(n, g) for n, g in zip(ffn1_names, _gather2([shards[n] for n in ffn1_names], name="gather_ffn1"))}
    mix_srcs = [shards[n] for n in mix_names] + [conv_w[0]]
    mix_started, mix_token = _send_start(mix_srcs, True, full[ffn1_names[2]], name="gather_mix_start")
    ffn2_started, ffn2_token = _send_start([shards[n] for n in ffn2_names], True, mix_token, name="gather_ffn2_start")
    ffn1_norm_fwd = ffn1_norm + ffn2_token[:1, :1]
    alog, dtb = _pad_lanes(a_log), _pad_lanes(dt_bias)
    sgbt = _pad_lanes(sg_b[0].T)
    fnw = final_norm.reshape(1, D_MODEL)

    x1, h1, g1, u1 = _ffn_fwd(x0, ffn1_norm_fwd, full["ffn1_w_gate"], full["ffn1_w_up"], full["ffn1_w_down"], name="ffn1_fwd")
    mix_lands = [fill_own(land, src) for land, src in zip(_send_wait(mix_started, True, x1, name="gather_mix_wait"), mix_srcs)]
    full.update({n: as_full(n, g) for n, g in zip(mix_names, mix_lands)})
    conv_full = _cols_full(mix_lands[-1])
    w_in_t = full["w_in"]
    offs = (0, SG_WIDTH, 2 * SG_WIDTH, 2 * SG_WIDTH + 3 * DN_WIDTH, 2 * SG_WIDTH + 4 * DN_WIDTH)
    n_proj = offs[-1]

    def pad_rows(a):
        return jnp.pad(a, ((0, LANES - a.shape[0]), (0, 0)))

    ws = [w_in_t[offs[0]:offs[1]], w_in_t[offs[1]:offs[2]], w_in_t[offs[2]:offs[3]], w_in_t[offs[3]:offs[4]],
          pad_rows(w_in_t[n_proj:n_proj + DN_HEADS]), pad_rows(w_in_t[n_proj + DN_HEADS:n_proj + 2 * DN_HEADS])]
    wo_sg, wo_dn = full["w_out"][:SG_WIDTH], full["w_out"][SG_WIDTH:]
    u, v, qkv, z, bpre, apre = _mix_in_fwd(x1, mix_norm, ws, name="mix_in_fwd")
    sg_out = _sg_fwd(u, v, sg_ln_g, sg_ln_b, sg_w[0], sgbt, name="sg_fwd")
    q, k, vv, beta, gc = _dn_prep_fwd(qkv, bpre, apre, conv_full, alog, dtb, seq, name="dn_prep_fwd")
    grow = _chunk_rows_of(gc)
    wy_w, wy_u, q_dec, k_dec, qk, egl, inv = _delta_prep(q, k, vv, gc, grow, beta, name="delta_prep")
    o, states = _delta_seq_fwd(wy_w, wy_u, q_dec, k_dec, qk, egl, n_seq, seq, name="delta_seq_fwd")
    x2 = _mix_out_fwd(x1, sg_out, o, z, wo_sg, wo_dn, dn_norm, name="mix_out_fwd")
    ffn2_lands = _send_wait(ffn2_started, True, x2, name="gather_ffn2_wait")
    full.update({n: as_full(n, fill_own(land, shards[n])) for n, land in zip(ffn2_names, ffn2_lands)})
    dx3, loss_part, d_fn, h2, g2, u2 = _ffn_fwd(x2, ffn2_norm, full["ffn2_w_gate"], full["ffn2_w_up"], full["ffn2_w_down"],
                                                tgt, fnw, name="ffn2_fwd_loss")
    loss = lax.psum(loss_part[0, 0], ("x", "y", "c"))

    dx2, d_n2, d_g2, d_u2, d_d2 = _ffn_bwd(x2, ffn2_norm, h2, g2, u2, full["ffn2_w_gate"], full["ffn2_w_up"],
                                           full["ffn2_w_down"], dx3, name="ffn2_bwd")
    def by_owner(d_rows):
        return d_rows.reshape(N_DEV, -1, D_MODEL)

    ffn2_pieces = [by_owner(d_g2), by_owner(d_u2), by_owner(d_d2)]
    ffn2_sent, sent_token = _send_start(ffn2_pieces, False, name="grads_ffn2_start")
    dsg, do, dz, d_wo_sg, d_wo_dn, d_dnw = _mix_out_bwd(dx2, sg_out, o, z, wo_sg, wo_dn, dn_norm + sent_token[:1, :1],
                                                        name="mix_out_bwd")
    d_seq = _delta_seq_bwd(wy_w, wy_u, q_dec, k_dec, qk, egl, states, do, n_seq, seq, name="delta_seq_bwd")
    dq, dk, dv, dgc_a, dgrow, dbeta = _delta_par_bwd(q, k, vv, gc, grow, beta, inv, *d_seq, name="delta_par_bwd")
    dgc_b = _pad_lanes(jnp.transpose(dgrow[:, :DN_HEADS, :], (0, 2, 1)).reshape(t, DN_HEADS))
    dy_conv, dbpre, dapre, d_alog, d_dtb = _dn_prep_bwd(qkv, bpre, apre, conv_full, alog, dtb, dq, dk, dv, dbeta, dgc_a, dgc_b,
                                                        seq, name="dn_prep_bwd")
    dqkv, d_conv = _conv_bwd(qkv, dy_conv, conv_full, seq, name="conv_bwd")
    du, dvv, d_lng, d_lnb, d_wc, d_sgbt = _sg_bwd(u, v, sg_ln_g, sg_ln_b, sg_w[0], sgbt, dsg, name="sg_bwd")
    dx1, d_mixn, d_wp = _mix_in_bwd(x1, mix_norm, ws, dx2, (du, dvv, dqkv, dz, dbpre, dapre), name="mix_in_bwd")
    d_w_in_t = jnp.concatenate([d_wp[:n_proj], d_wp[_PROJ_OFFSETS[4]:_PROJ_OFFSETS[4] + DN_HEADS],
                                d_wp[_PROJ_OFFSETS[5]:_PROJ_OFFSETS[5] + DN_HEADS]], axis=0)
    d_w_out = jnp.concatenate([d_wo_sg, d_wo_dn], axis=0)
    mix_pieces = [by_owner(d_w_in_t), by_owner(d_w_out).astype(BF16)]
    mix_sent, sent_token = _send_start(mix_pieces, False, name="grads_mix_start")
    grad_x, d_n1, dg1, du1, a1, dyh1 = _ffn_bwd_x(x0, ffn1_norm + sent_token[:1, :1], g1, u1, full["ffn1_w_gate"],
                                                  full["ffn1_w_up"], full["ffn1_w_down"], dx1, name="ffn1_bwd_x")
    small_grads = dict(ffn1_norm=d_n1, mix_norm=d_mixn, ffn2_norm=d_n2, final_norm=d_fn, a_log=d_alog[:, :DN_HEADS],
                       dt_bias=d_dtb[:, :DN_HEADS], dn_norm=d_dnw, sg_ln_g=d_lng, sg_ln_b=d_lnb, sg_w=d_wc,
                       sg_b=d_sgbt[:, :SG_GROUPS].T, conv_w=d_conv[:CONV_K])
    small_src = _pack_small(small_grads)
    small_sent, small_token = _send_start([small_src], True, name="small_grads_start")
    late, tokens = [], []

    def send_early(k, grad):
        piece = by_owner(grad)
        sent, token = _send_start([piece], False, name="grads_" + ffn1_names[k] + "_start")
        late.append(((ffn1_names[k],), sent, [piece]))
        tokens.append(token)
        return token

    _ffn_wgrads(h1, dg1, du1, a1, dyh1, send_early, small_token, name="ffn1_bwd")

    res = {}
    after = tokens[-1]

    def update(names, sent, pieces, after):
        lands = _send_wait(sent, False, after, name="grads_" + names[0] + "_wait")
        for n, land, p in zip(names, lands, pieces):
            got = fill_own(land, lax.dynamic_index_in_dim(p, me, 0, keepdims=False))
            upd = _adam(got, *[rows_view(n, src[n][0]) for src in (weights, mom_m, mom_v)], name="adam_" + n)
            res[n] = [rows_view(n, a) for a in upd]
            after = upd[0]
        return after

    for group in [(ffn2_names, ffn2_sent, ffn2_pieces), (mix_names, mix_sent, mix_pieces)] + late[:-1]:
        after = update(*group, after)
    (small_land,) = _send_wait(small_sent, True, after, name="small_grads_wait")
    small_parts = fill_own(small_land, small_src)
    zeros_conv = jnp.zeros((CONV_K * 3 * DN_WIDTH,), F32)
    packed = [_pack_small({**{n: src[n] for n, _ in _SMALL if n != "conv_w"}, "conv_w": zeros_conv})
              for src in (weights, mom_m, mom_v)]
    small_upd = _adam(small_parts, *packed, name="adam_small")
    small_res = [_unpack_small(a) for a in small_upd]
    conv_grad = lax.dynamic_slice_in_dim(small_res[0]["conv_w"].reshape(CONV_K, 3 * DN_WIDTH), me * (3 * DN_WIDTH // N_DEV),
                                         3 * DN_WIDTH // N_DEV, axis=1)
    res["conv_w"] = _adam(conv_grad[None], conv_w[0], m_conv_w[0], v_conv_w[0], name="adam_conv_w")
    update(*late[-1], res["conv_w"][0])

    outs = [[], [], [], []]
    for n in order:
        for kind in range(4):
            if n in res:
                outs[kind].append(res[n][kind][None])
            else:
                outs[kind].append(small_res[kind][n].reshape(_SMALL_SHAPES[n]))
    return (loss, grad_x.reshape(x.shape), *outs[0], *outs[1], *outs[2], *outs[3])
```

```python
import jax
import jax.numpy as jnp
from jax import lax
from jax.experimental import pallas as pl
from jax.experimental.pallas import tpu as pltpu

F32 = jnp.float32
BF16 = jnp.bfloat16

D_MODEL = 1024
D_FF = 2816
SG_WIDTH = 512
SG_GROUPS = 8
SG_GROUP_DIM = 64
SG_CHUNK = 128
DN_WIDTH = 512
DN_HEAD_DIM = 128
DN_HEADS = 4
DN_CHUNK = 64
CONV_K = 4
EPS = 1e-6
N_DEV = 8
LANES = 128
HALO = 8
MXU_COLS = 256

ADAM_LR = 0.001
ADAM_B1 = 0.9
ADAM_B2 = 0.999
ADAM_EPS = 1e-08
ADAM_WD = 0.01
ADAM_STEP = 10

VMEM_LIMIT = 60 * 1024 * 1024
WGRAD_K_TILE = 2048
TOKEN_BLOCK = 512
FF_BLOCK_FWD = 1408

_HI = lax.Precision.HIGHEST


def _cparams(sem):
    return pltpu.CompilerParams(dimension_semantics=sem, vmem_limit_bytes=VMEM_LIMIT)


def _tm(t, pref=TOKEN_BLOCK):
    return min(pref, t)


def _dg(a, b, ca, cb, precision):
    if precision is not None:
        return lax.dot_general(a, b, (((ca,), (cb,)), ((), ())), precision=precision, preferred_element_type=F32)
    return lax.dot_general(a.astype(BF16), b.astype(BF16), (((ca,), (cb,)), ((), ())), preferred_element_type=F32)


def _make_mm(precision):
    @jax.custom_vjp
    def mm(a, b):
        return _dg(a, b, 1, 0, precision)

    @jax.custom_vjp
    def mm_nt(a, b):
        return _dg(a, b, 1, 1, precision)

    @jax.custom_vjp
    def mm_tn(a, b):
        return _dg(a, b, 0, 0, precision)

    mm.defvjp(lambda a, b: (mm(a, b), (a, b)), lambda r, g: (mm_nt(g, r[1]), mm_tn(r[0], g)))
    mm_nt.defvjp(lambda a, b: (mm_nt(a, b), (a, b)), lambda r, g: (mm(g, r[1]), mm_tn(g, r[0])))
    mm_tn.defvjp(lambda a, b: (mm_tn(a, b), (a, b)), lambda r, g: (mm_nt(r[1], g), mm(r[0], g)))
    return mm, mm_nt, mm_tn


mm, mm_nt, mm_tn = _make_mm(None)
mmx, mmx_nt, mmx_tn = _make_mm(_HI)
mmh, mmh_nt, mmh_tn = _make_mm(lax.Precision.HIGH)


def _sigmoid(x):
    return 1.0 / (1.0 + jnp.exp(-x))


def _silu(x):
    return x * _sigmoid(x)


def _softplus(x):
    neg_abs = jnp.where(x > 0, -x, x)
    return jnp.where(x > 0, x, 0.0) + jnp.log(1.0 + jnp.exp(neg_abs))


def _gelu(x):
    return 0.5 * x * (1.0 + jnp.tanh(0.7978845608028654 * (x + 0.044715 * (x * x * x))))


def _rms_fwd(x, g):
    r = lax.rsqrt(jnp.mean(x * x, axis=-1, keepdims=True) + EPS)
    xh = x * r
    return xh * g, xh, r


def _rms_bwd(dh, xh, r, g):
    dxh = dh * g
    dx = r * (dxh - xh * jnp.mean(dxh * xh, axis=-1, keepdims=True))
    return dx, jnp.sum(dh * xh, axis=0, keepdims=True)


def _acc_out(ref, first, val):
    @pl.when(first)
    def _():
        ref[...] = val

    @pl.when(jnp.logical_not(first))
    def _():
        ref[...] += val


def _ffn_fwd(x, nw, wg, wu, wd, tgt=None, fnw=None, *, name):
    t = x.shape[0]
    tm, fb = _tm(t), FF_BLOCK_FWD
    n_t, n_f = t // tm, D_FF // fb
    with_loss = tgt is not None

    def body(*refs):
        if with_loss:
            (x_ref, nw_ref, wg_ref, wu_ref, wd_ref, tgt_ref, fnw_ref, dy_ref, loss_ref, dfn_ref, h_ref, g_ref, u_ref,
             acc_s) = refs
        else:
            x_ref, nw_ref, wg_ref, wu_ref, wd_ref, y_ref, h_ref, g_ref, u_ref, acc_s = refs
        i, j = pl.program_id(0), pl.program_id(1)

        @pl.when(j == 0)
        def _():
            h, _, _ = _rms_fwd(x_ref[...], nw_ref[...])
            h_ref[...] = h.astype(BF16)
            acc_s[...] = jnp.zeros_like(acc_s)

        h = h_ref[...]
        nt = (((1,), (1,)), ((), ()))
        g = lax.dot_general(h, wg_ref[...], nt, preferred_element_type=F32)
        u = lax.dot_general(h, wu_ref[...], nt, preferred_element_type=F32)
        g_ref[...] = g.astype(BF16)
        u_ref[...] = u.astype(BF16)
        a = _silu(g) * u
        acc_s[...] += jnp.dot(a.astype(BF16), wd_ref[...], preferred_element_type=F32)

        @pl.when(j == n_f - 1)
        def _():
            y = x_ref[...] + 0.5 * acc_s[...]
            if not with_loss:
                y_ref[...] = y
            else:
                gf = fnw_ref[...]
                out, xh, r = _rms_fwd(y, gf)
                err = out - tgt_ref[...]
                part = 0.5 * jnp.sum(jnp.mean(err * err, axis=-1, keepdims=True), axis=0, keepdims=True)
                d_out = err * (1.0 / D_MODEL)
                dy, dgf = _rms_bwd(d_out, xh, r, gf)
                dy_ref[...] = dy
                _acc_out(loss_ref, i == 0, jnp.broadcast_to(part, loss_ref.shape))
                _acc_out(dfn_ref, i == 0, dgf)

    row = lambda i, j: (i, 0)
    const = lambda i, j: (0, 0)
    in_specs = [
        pl.BlockSpec((tm, D_MODEL), row),
        pl.BlockSpec((1, D_MODEL), const),
        pl.BlockSpec((fb, D_MODEL), lambda i, j: (j, 0)),
        pl.BlockSpec((fb, D_MODEL), lambda i, j: (j, 0)),
        pl.BlockSpec((fb, D_MODEL), lambda i, j: (j, 0)),
    ]
    args = [x, nw, wg, wu, wd]
    saved_shape = (jax.ShapeDtypeStruct((t, D_MODEL), BF16), jax.ShapeDtypeStruct((t, D_FF), BF16),
                   jax.ShapeDtypeStruct((t, D_FF), BF16))
    saved_specs = (pl.BlockSpec((tm, D_MODEL), row), pl.BlockSpec((tm, fb), lambda i, j: (i, j)),
                   pl.BlockSpec((tm, fb), lambda i, j: (i, j)))
    if with_loss:
        in_specs += [pl.BlockSpec((tm, D_MODEL), row), pl.BlockSpec((1, D_MODEL), const)]
        args += [tgt, fnw]
        out_shape = (jax.ShapeDtypeStruct((t, D_MODEL), F32), jax.ShapeDtypeStruct((8, LANES), F32),
                     jax.ShapeDtypeStruct((1, D_MODEL), F32)) + saved_shape
        out_specs = (pl.BlockSpec((tm, D_MODEL), row), pl.BlockSpec((8, LANES), const),
                     pl.BlockSpec((1, D_MODEL), const)) + saved_specs
        sem = ("arbitrary", "arbitrary")
    else:
        out_shape = (jax.ShapeDtypeStruct((t, D_MODEL), F32),) + saved_shape
        out_specs = (pl.BlockSpec((tm, D_MODEL), row),) + saved_specs
        sem = ("parallel", "arbitrary")
    return pl.pallas_call(
        body, name=name, grid=(n_t, n_f), in_specs=in_specs, out_specs=out_specs, out_shape=out_shape,
        scratch_shapes=[pltpu.VMEM((tm, D_MODEL), F32)],
        compiler_params=_cparams(sem),
    )(*args)


def _ffn_bwd_x(x, nw, g, u, wg, wu, wd, dy, *, name):
    t = x.shape[0]
    tm = _tm(t, 256)

    def body(x_ref, nw_ref, g_ref, u_ref, wg_ref, wu_ref, wd_ref, dy_ref, dx_ref, dnw_ref, dg_ref, du_ref, a_ref, dyh_ref):
        i = pl.program_id(0)
        nt = (((1,), (1,)), ((), ()))
        dy = dy_ref[...]
        dyh = (0.5 * dy).astype(BF16)
        dyh_ref[...] = dyh
        gate, up = g_ref[...].astype(F32), u_ref[...].astype(F32)
        s = _sigmoid(gate)
        gs = gate * s
        da = lax.dot_general(dyh, wd_ref[...], nt, preferred_element_type=F32)
        dg = (da * up * (s + gs * (1.0 - s))).astype(BF16)
        du = (da * gs).astype(BF16)
        dg_ref[...] = dg
        du_ref[...] = du
        a_ref[...] = (gs * up).astype(BF16)
        dh = (jnp.dot(dg, wg_ref[...], preferred_element_type=F32)
              + jnp.dot(du, wu_ref[...], preferred_element_type=F32))
        xv = x_ref[...]
        r = lax.rsqrt(jnp.mean(xv * xv, axis=-1, keepdims=True) + EPS)
        dx, dnw = _rms_bwd(dh, xv * r, r, nw_ref[...])
        dx_ref[...] = dy + dx
        _acc_out(dnw_ref, i == 0, dnw)

    row = lambda i: (i, 0)
    const = lambda i: (0, 0)
    once = pl.Buffered(1)
    wide = pl.BlockSpec((tm, D_FF), row)
    return pl.pallas_call(
        body, name=name, grid=(t // tm,),
        in_specs=[pl.BlockSpec((tm, D_MODEL), row), pl.BlockSpec((1, D_MODEL), const), wide, wide,
                  pl.BlockSpec((D_FF, D_MODEL), const, pipeline_mode=once), pl.BlockSpec((D_FF, D_MODEL), const, pipeline_mode=once),
                  pl.BlockSpec((D_FF, D_MODEL), const, pipeline_mode=once), pl.BlockSpec((tm, D_MODEL), row)],
        out_specs=(pl.BlockSpec((tm, D_MODEL), row), pl.BlockSpec((1, D_MODEL), const), wide, wide, wide,
                   pl.BlockSpec((tm, D_MODEL), row)),
        out_shape=(jax.ShapeDtypeStruct((t, D_MODEL), F32), jax.ShapeDtypeStruct((1, D_MODEL), F32),
                   jax.ShapeDtypeStruct((t, D_FF), BF16), jax.ShapeDtypeStruct((t, D_FF), BF16),
                   jax.ShapeDtypeStruct((t, D_FF), BF16), jax.ShapeDtypeStruct((t, D_MODEL), BF16)),
        compiler_params=_cparams(("arbitrary",)),
    )(x, nw, g, u, wg, wu, wd, dy)


def _wgrad(a, b, bm, bn, after=None, *, name):
    k, m = a.shape
    n = b.shape[1]
    tk = _tm(k, WGRAD_K_TILE)
    n_k = k // tk

    def body(a_ref, b_ref, *rest):
        o_ref, acc_s = rest[-2], rest[-1]
        s = pl.program_id(2)
        for c in range(bn // MXU_COLS):
            cols = slice(c * MXU_COLS, (c + 1) * MXU_COLS)
            part = lax.dot_general(a_ref[...], b_ref[:, cols], (((0,), (0,)), ((), ())), preferred_element_type=F32)
            acc_s[:, cols] = jnp.where(s == 0, 0.0, acc_s[:, cols]) + part

        @pl.when(s == n_k - 1)
        def _():
            o_ref[...] = acc_s[...].astype(BF16)

    return pl.pallas_call(
        body, name=name, grid=(m // bm, n // bn, n_k),
        in_specs=[pl.BlockSpec((tk, bm), lambda i, j, s: (s, i)), pl.BlockSpec((tk, bn), lambda i, j, s: (s, j))]
        + ([] if after is None else [_HBM]),
        out_specs=pl.BlockSpec((bm, bn), lambda i, j, s: (i, j)),
        out_shape=jax.ShapeDtypeStruct((m, n), BF16),
        scratch_shapes=[pltpu.VMEM((bm, bn), F32)],
        compiler_params=_cparams(("parallel", "parallel", "arbitrary")),
    )(a, b, *([] if after is None else [after]))


def _ffn_wgrads(h, dg, du, a, dyh, between=None, after=None, *, name):
    grads = []
    for k, (lhs, rhs, tag) in enumerate(((dg, h, "_wg"), (du, h, "_wu"), (a, dyh, "_wd"))):
        grads.append(_wgrad(lhs, rhs, D_FF // 2, D_MODEL, after, name=name + tag))
        after = None if between is None else between(k, grads[-1])
    return grads


def _ffn_bwd(x, nw, h, g, u, wg, wu, wd, dy, *, name):
    dx, dnw, dg, du, a, dyh = _ffn_bwd_x(x, nw, g, u, wg, wu, wd, dy, name=name + "_x")
    return (dx, dnw, *_ffn_wgrads(h, dg, du, a, dyh, name=name))


_PROJ_WIDTHS = (SG_WIDTH, SG_WIDTH, 3 * DN_WIDTH, DN_WIDTH, LANES, LANES)


def _mix_in_fwd(x, nw, ws, *, name):
    t = x.shape[0]
    tm = _tm(t)

    def body(x_ref, nw_ref, *refs):
        w_refs, o_refs = refs[:6], refs[6:]
        h, _, _ = _rms_fwd(x_ref[...], nw_ref[...])
        h = h.astype(BF16)
        for w_ref, o_ref in zip(w_refs, o_refs):
            o_ref[...] = lax.dot_general(h, w_ref[...], (((1,), (1,)), ((), ())), preferred_element_type=F32)

    row = lambda i: (i, 0)
    const = lambda i: (0, 0)
    return pl.pallas_call(
        body, name=name, grid=(t // tm,),
        in_specs=[pl.BlockSpec((tm, D_MODEL), row), pl.BlockSpec((1, D_MODEL), const)]
        + [pl.BlockSpec((n, D_MODEL), const) for n in _PROJ_WIDTHS],
        out_specs=tuple(pl.BlockSpec((tm, n), row) for n in _PROJ_WIDTHS),
        out_shape=tuple(jax.ShapeDtypeStruct((t, n), F32) for n in _PROJ_WIDTHS),
        compiler_params=_cparams(("parallel",)),
    )(x, nw, *ws)


_PROJ_TOTAL = sum(_PROJ_WIDTHS)
_PROJ_OFFSETS = tuple(sum(_PROJ_WIDTHS[:k]) for k in range(len(_PROJ_WIDTHS)))


def _mix_in_bwd(x, nw, ws, dres, dps, *, name):
    t = x.shape[0]
    tm = _tm(t, 256)

    def body(x_ref, nw_ref, dres_ref, *refs):
        w_refs, dp_refs, dx_ref, dnw_ref, h_ref, dpb_ref = refs[:6], refs[6:12], refs[12], refs[13], refs[14], refs[15]
        i = pl.program_id(0)
        hf, xh, r = _rms_fwd(x_ref[...], nw_ref[...])
        h_ref[...] = hf.astype(BF16)
        dh = jnp.zeros((tm, D_MODEL), F32)
        for w_ref, dp_ref, off, width in zip(w_refs, dp_refs, _PROJ_OFFSETS, _PROJ_WIDTHS):
            dp = dp_ref[...].astype(BF16)
            dpb_ref[:, off:off + width] = dp
            dh = dh + jnp.dot(dp, w_ref[...], preferred_element_type=F32)
        dx, dnw = _rms_bwd(dh, xh, r, nw_ref[...])
        dx_ref[...] = dres_ref[...] + dx
        _acc_out(dnw_ref, i == 0, dnw)

    row = lambda i: (i, 0)
    const = lambda i: (0, 0)
    dx, dnw, h, dpb = pl.pallas_call(
        body, name=name + "_x", grid=(t // tm,),
        in_specs=[pl.BlockSpec((tm, D_MODEL), row), pl.BlockSpec((1, D_MODEL), const), pl.BlockSpec((tm, D_MODEL), row)]
        + [pl.BlockSpec((n, D_MODEL), const) for n in _PROJ_WIDTHS]
        + [pl.BlockSpec((tm, n), row) for n in _PROJ_WIDTHS],
        out_specs=(pl.BlockSpec((tm, D_MODEL), row), pl.BlockSpec((1, D_MODEL), const), pl.BlockSpec((tm, D_MODEL), row),
                   pl.BlockSpec((tm, _PROJ_TOTAL), row)),
        out_shape=(jax.ShapeDtypeStruct((t, D_MODEL), F32), jax.ShapeDtypeStruct((1, D_MODEL), F32),
                   jax.ShapeDtypeStruct((t, D_MODEL), BF16), jax.ShapeDtypeStruct((t, _PROJ_TOTAL), BF16)),
        compiler_params=_cparams(("arbitrary",)),
    )(x, nw, dres, *ws, *dps)
    return dx, dnw, _wgrad(dpb, h, _PROJ_TOTAL // 2, D_MODEL, name=name + "_w")


def _sg_fn(u, v, lng, lnb, wcs, sgbt):
    lane = lax.broadcasted_iota(jnp.int32, (1, SG_WIDTH), 1)
    lane_b = lax.broadcasted_iota(jnp.int32, (1, LANES), 1)
    rr = lax.broadcasted_iota(jnp.int32, (SG_CHUNK, SG_CHUNK), 0)
    cc = lax.broadcasted_iota(jnp.int32, (SG_CHUNK, SG_CHUNK), 1)
    gu, gv = _gelu(u), _gelu(v)
    mu = jnp.mean(gv, axis=-1, keepdims=True)
    cen = gv - mu
    var = jnp.mean(cen * cen, axis=-1, keepdims=True)
    ln = cen * lax.rsqrt(var + EPS) * lng + lnb
    vs = jnp.zeros_like(u)
    for g in range(SG_GROUPS):
        in_group = jnp.logical_and(lane >= g * SG_GROUP_DIM, lane < (g + 1) * SG_GROUP_DIM)
        w_causal = jnp.where(rr >= cc, wcs[g], 0.0)
        bias = jnp.sum(jnp.where(lane_b == g, sgbt, 0.0), axis=1, keepdims=True)
        vs = vs + jnp.where(in_group, mm(w_causal, ln) + bias, 0.0)
    return gu * vs


def _sg_fwd(u, v, lng, lnb, wc, sgbt, *, name):
    t = u.shape[0]
    tm = _tm(t)

    def body(u_ref, v_ref, lng_ref, lnb_ref, wc_ref, sgbt_ref, o_ref):
        wcs = [wc_ref[g] for g in range(SG_GROUPS)]
        for c in range(tm // SG_CHUNK):
            rows = pl.ds(c * SG_CHUNK, SG_CHUNK)
            o_ref[rows, :] = _sg_fn(u_ref[rows, :], v_ref[rows, :], lng_ref[...], lnb_ref[...], wcs, sgbt_ref[...])

    row = lambda i: (i, 0)
    const = lambda i: (0, 0)
    return pl.pallas_call(
        body, name=name, grid=(t // tm,),
        in_specs=[pl.BlockSpec((tm, SG_WIDTH), row), pl.BlockSpec((tm, SG_WIDTH), row),
                  pl.BlockSpec((1, SG_WIDTH), const), pl.BlockSpec((1, SG_WIDTH), const),
                  pl.BlockSpec((SG_GROUPS, SG_CHUNK, SG_CHUNK), lambda i: (0, 0, 0)), pl.BlockSpec((SG_CHUNK, LANES), const)],
        out_specs=pl.BlockSpec((tm, SG_WIDTH), row),
        out_shape=jax.ShapeDtypeStruct((t, SG_WIDTH), F32),
        compiler_params=_cparams(("parallel",)),
    )(u, v, lng, lnb, wc, sgbt)


def _sg_bwd(u, v, lng, lnb, wc, sgbt, dout, *, name):
    t = u.shape[0]
    tm = _tm(t)

    def body(u_ref, v_ref, lng_ref, lnb_ref, wc_ref, sgbt_ref, do_ref, du_ref, dv_ref, dlng_ref, dlnb_ref, dwc_ref, dsgbt_ref):
        i = pl.program_id(0)
        wcs = [wc_ref[g] for g in range(SG_GROUPS)]
        tot = None
        for c in range(tm // SG_CHUNK):
            rows = pl.ds(c * SG_CHUNK, SG_CHUNK)
            _, vjp = jax.vjp(_sg_fn, u_ref[rows, :], v_ref[rows, :], lng_ref[...], lnb_ref[...], wcs, sgbt_ref[...])
            du, dv, dlng, dlnb, dwcs, dsgbt = vjp(do_ref[rows, :])
            du_ref[rows, :] = du.astype(BF16)
            dv_ref[rows, :] = dv.astype(BF16)
            part = (dlng, dlnb, dwcs, dsgbt)
            tot = part if tot is None else jax.tree.map(jnp.add, tot, part)
        dlng, dlnb, dwcs, dsgbt = tot
        _acc_out(dlng_ref, i == 0, dlng)
        _acc_out(dlnb_ref, i == 0, dlnb)
        _acc_out(dsgbt_ref, i == 0, dsgbt)
        for g in range(SG_GROUPS):
            @pl.when(i == 0)
            def _(g=g):
                dwc_ref[g] = dwcs[g]

            @pl.when(i > 0)
            def _(g=g):
                dwc_ref[g] += dwcs[g]

    row = lambda i: (i, 0)
    const = lambda i: (0, 0)
    wspec = pl.BlockSpec((SG_GROUPS, SG_CHUNK, SG_CHUNK), lambda i: (0, 0, 0))
    return pl.pallas_call(
        body, name=name, grid=(t // tm,),
        in_specs=[pl.BlockSpec((tm, SG_WIDTH), row), pl.BlockSpec((tm, SG_WIDTH), row),
                  pl.BlockSpec((1, SG_WIDTH), const), pl.BlockSpec((1, SG_WIDTH), const), wspec,
                  pl.BlockSpec((SG_CHUNK, LANES), const), pl.BlockSpec((tm, SG_WIDTH), row)],
        out_specs=(pl.BlockSpec((tm, SG_WIDTH), row), pl.BlockSpec((tm, SG_WIDTH), row),
                   pl.BlockSpec((1, SG_WIDTH), const), pl.BlockSpec((1, SG_WIDTH), const), wspec,
                   pl.BlockSpec((SG_CHUNK, LANES), const)),
        out_shape=(jax.ShapeDtypeStruct((t, SG_WIDTH), BF16), jax.ShapeDtypeStruct((t, SG_WIDTH), BF16),
                   jax.ShapeDtypeStruct((1, SG_WIDTH), F32), jax.ShapeDtypeStruct((1, SG_WIDTH), F32),
                   jax.ShapeDtypeStruct((SG_GROUPS, SG_CHUNK, SG_CHUNK), F32), jax.ShapeDtypeStruct((SG_CHUNK, LANES), F32)),
        compiler_params=_cparams(("arbitrary",)),
    )(u, v, lng, lnb, wc, sgbt, dout)


def _conv_taps(ext, w, tm):
    y = None
    for j in range(CONV_K):
        s = CONV_K - 1 - j
        shifted = ext if s == 0 else pltpu.roll(ext, s, 0)
        term = w[j:j + 1, :] * shifted[HALO:HALO + tm, :]
        y = term if y is None else y + term
    return y


def _post_conv(yq, yk, yv, bpre, apre, alog, dtb):
    def l2(a):
        return a * lax.rsqrt(jnp.sum(a * a, axis=-1, keepdims=True) + EPS)

    q = [l2(_silu(a)) for a in yq]
    k = [l2(_silu(a)) for a in yk]
    return q, k, _silu(yv), _sigmoid(bpre), -jnp.exp(alog) * _softplus(apre + dtb)


def _chunk_tril(tm):
    rr = lax.broadcasted_iota(jnp.int32, (tm, tm), 0)
    cc = lax.broadcasted_iota(jnp.int32, (tm, tm), 1)
    shift = DN_CHUNK.bit_length() - 1
    same = jnp.right_shift(rr, shift) == jnp.right_shift(cc, shift)
    return jnp.where(jnp.logical_and(same, rr >= cc), 1.0, 0.0).astype(F32)


def _halo_specs(tm, width, n_blocks_seq, n_blocks):
    per = tm // HALO
    prev = pl.BlockSpec((HALO, width), lambda i: (jnp.maximum(i * per - 1, 0), 0))
    nxt = pl.BlockSpec((HALO, width), lambda i: (jnp.minimum((i + 1) * per, n_blocks * per - 1), 0))
    return prev, nxt


def _split_heads(ref, base):
    return [ref[:, base + h * DN_HEAD_DIM: base + (h + 1) * DN_HEAD_DIM] for h in range(DN_HEADS)]


def _dn_prep_fwd(qkv, bpre, apre, conv_w, alog, dtb, seq, *, name):
    t = qkv.shape[0]
    tm = _tm(t)
    bps = seq // tm
    cw = 3 * DN_WIDTH

    def body(x_ref, halo_ref, b_ref, a_ref, w_ref, alog_ref, dtb_ref, q_ref, k_ref, v_ref, beta_ref, gc_ref):
        i = pl.program_id(0)
        keep = jnp.where(i % bps == 0, 0.0, 1.0)
        ext = jnp.concatenate([halo_ref[...] * keep, x_ref[...]], axis=0)
        y = _conv_taps(ext, w_ref[...], tm)
        yq = [y[:, h * DN_HEAD_DIM:(h + 1) * DN_HEAD_DIM] for h in range(DN_HEADS)]
        yk = [y[:, DN_WIDTH + h * DN_HEAD_DIM: DN_WIDTH + (h + 1) * DN_HEAD_DIM] for h in range(DN_HEADS)]
        q, k, v, beta, g = _post_conv(yq, yk, y[:, 2 * DN_WIDTH:], b_ref[...], a_ref[...], alog_ref[...], dtb_ref[...])
        for h in range(DN_HEADS):
            q_ref[:, h * DN_HEAD_DIM:(h + 1) * DN_HEAD_DIM] = q[h]
            k_ref[:, h * DN_HEAD_DIM:(h + 1) * DN_HEAD_DIM] = k[h]
        v_ref[...] = v
        beta_ref[...] = beta
        gc_ref[...] = mmx(_chunk_tril(tm), g)

    row = lambda i: (i, 0)
    const = lambda i: (0, 0)
    prev, _ = _halo_specs(tm, cw, bps, t // tm)
    return pl.pallas_call(
        body, name=name, grid=(t // tm,),
        in_specs=[pl.BlockSpec((tm, cw), row), prev, pl.BlockSpec((tm, LANES), row), pl.BlockSpec((tm, LANES), row),
                  pl.BlockSpec((CONV_K, cw), const), pl.BlockSpec((1, LANES), const), pl.BlockSpec((1, LANES), const)],
        out_specs=tuple(pl.BlockSpec((tm, n), row) for n in (DN_WIDTH, DN_WIDTH, DN_WIDTH, LANES, LANES)),
        out_shape=tuple(jax.ShapeDtypeStruct((t, n), F32) for n in (DN_WIDTH, DN_WIDTH, DN_WIDTH, LANES, LANES)),
        compiler_params=_cparams(("parallel",)),
    )(qkv, qkv, bpre, apre, conv_w, alog, dtb)


def _dn_prep_bwd(qkv, bpre, apre, conv_w, alog, dtb, dq, dk, dv, dbeta, dgc, dgc2, seq, *, name):
    t = qkv.shape[0]
    tm = _tm(t)
    bps = seq // tm
    cw = 3 * DN_WIDTH

    def body(x_ref, halo_ref, b_ref, a_ref, w_ref, alog_ref, dtb_ref, dq_ref, dk_ref, dv_ref, dbeta_ref, dgc_ref, dgc2_ref,
             dy_ref, db_ref, da_ref, dalog_ref, ddtb_ref):
        i = pl.program_id(0)
        keep = jnp.where(i % bps == 0, 0.0, 1.0)
        ext = jnp.concatenate([halo_ref[...] * keep, x_ref[...]], axis=0)
        y = _conv_taps(ext, w_ref[...], tm)
        yq = [y[:, h * DN_HEAD_DIM:(h + 1) * DN_HEAD_DIM] for h in range(DN_HEADS)]
        yk = [y[:, DN_WIDTH + h * DN_HEAD_DIM: DN_WIDTH + (h + 1) * DN_HEAD_DIM] for h in range(DN_HEADS)]
        _, vjp = jax.vjp(_post_conv, yq, yk, y[:, 2 * DN_WIDTH:], b_ref[...], a_ref[...], alog_ref[...], dtb_ref[...])
        dg = mmx_tn(_chunk_tril(tm), dgc_ref[...] + dgc2_ref[...])
        dyq, dyk, dyv, db, da, dalog, ddtb = vjp((_split_heads(dq_ref, 0), _split_heads(dk_ref, 0), dv_ref[...],
                                                  dbeta_ref[...], dg))
        for h in range(DN_HEADS):
            dy_ref[:, h * DN_HEAD_DIM:(h + 1) * DN_HEAD_DIM] = dyq[h]
            dy_ref[:, DN_WIDTH + h * DN_HEAD_DIM: DN_WIDTH + (h + 1) * DN_HEAD_DIM] = dyk[h]
        dy_ref[:, 2 * DN_WIDTH:] = dyv
        db_ref[...] = db.astype(BF16)
        da_ref[...] = da.astype(BF16)
        _acc_out(dalog_ref, i == 0, dalog)
        _acc_out(ddtb_ref, i == 0, ddtb)

    row = lambda i: (i, 0)
    const = lambda i: (0, 0)
    prev, _ = _halo_specs(tm, cw, bps, t // tm)
    return pl.pallas_call(
        body, name=name, grid=(t // tm,),
        in_specs=[pl.BlockSpec((tm, cw), row), prev, pl.BlockSpec((tm, LANES), row), pl.BlockSpec((tm, LANES), row),
                  pl.BlockSpec((CONV_K, cw), const), pl.BlockSpec((1, LANES), const), pl.BlockSpec((1, LANES), const),
                  pl.BlockSpec((tm, DN_WIDTH), row), pl.BlockSpec((tm, DN_WIDTH), row), pl.BlockSpec((tm, DN_WIDTH), row),
                  pl.BlockSpec((tm, LANES), row), pl.BlockSpec((tm, LANES), row), pl.BlockSpec((tm, LANES), row)],
        out_specs=(pl.BlockSpec((tm, cw), row), pl.BlockSpec((tm, LANES), row), pl.BlockSpec((tm, LANES), row),
                   pl.BlockSpec((1, LANES), const), pl.BlockSpec((1, LANES), const)),
        out_shape=(jax.ShapeDtypeStruct((t, cw), F32), jax.ShapeDtypeStruct((t, LANES), BF16), jax.ShapeDtypeStruct((t, LANES), BF16),
                   jax.ShapeDtypeStruct((1, LANES), F32), jax.ShapeDtypeStruct((1, LANES), F32)),
        compiler_params=_cparams(("arbitrary",)),
    )(qkv, qkv, bpre, apre, conv_w, alog, dtb, dq, dk, dv, dbeta, dgc, dgc2)


def _conv_bwd(qkv, dy, conv_w, seq, *, name):
    t = qkv.shape[0]
    tm = _tm(t)
    bps = seq // tm
    cw = 3 * DN_WIDTH
    n_ext = tm + HALO

    def body(x_ref, halo_ref, dy_ref, dyn_ref, w_ref, dx_ref, dw_ref):
        i = pl.program_id(0)
        keep_prev = jnp.where(i % bps == 0, 0.0, 1.0)
        keep_next = jnp.where(i % bps == bps - 1, 0.0, 1.0)
        ext = jnp.concatenate([halo_ref[...] * keep_prev, x_ref[...]], axis=0)
        dy = dy_ref[...]
        dyext = jnp.concatenate([dy, dyn_ref[...] * keep_next], axis=0)
        w = w_ref[...]

        @pl.when(i == 0)
        def _():
            dw_ref[...] = jnp.zeros_like(dw_ref)

        dx = None
        for j in range(CONV_K):
            s = CONV_K - 1 - j
            fut = dyext if s == 0 else pltpu.roll(dyext, n_ext - s, 0)
            term = w[j:j + 1, :] * fut[0:tm, :]
            dx = term if dx is None else dx + term
            past = ext if s == 0 else pltpu.roll(ext, s, 0)
            dw_ref[j:j + 1, :] += jnp.sum(dy * past[HALO:HALO + tm, :], axis=0, keepdims=True)
        dx_ref[...] = dx.astype(BF16)

    row = lambda i: (i, 0)
    const = lambda i: (0, 0)
    prev, nxt = _halo_specs(tm, cw, bps, t // tm)
    return pl.pallas_call(
        body, name=name, grid=(t // tm,),
        in_specs=[pl.BlockSpec((tm, cw), row), prev, pl.BlockSpec((tm, cw), row), nxt, pl.BlockSpec((CONV_K, cw), const)],
        out_specs=(pl.BlockSpec((tm, cw), row), pl.BlockSpec((HALO, cw), const)),
        out_shape=(jax.ShapeDtypeStruct((t, cw), BF16), jax.ShapeDtypeStruct((HALO, cw), F32)),
        compiler_params=_cparams(("arbitrary",)),
    )(qkv, qkv, dy, dy, conv_w)


def _inv_unit_lower(l_mats, eye):
    invs = [eye - l for l in l_mats]
    powers = list(l_mats)
    n = 2
    while n < eye.shape[0]:
        powers = [mmh(p, p) for p in powers]
        invs = [inv + mmh(inv, p) for inv, p in zip(invs, powers)]
        n *= 2
    return invs


@jax.custom_vjp
def _solve(l_mat, rhs, inv):
    return mmh(inv, rhs)


def _solve_fwd(l_mat, rhs, inv):
    sol = mmh(inv, rhs)
    return sol, (inv, sol)


def _solve_bwd(res, d_sol):
    inv, sol = res
    d_rhs = mm_tn(inv, d_sol)
    return -mm_nt(d_rhs, sol), d_rhs, jnp.zeros_like(inv)


_solve.defvjp(_solve_fwd, _solve_bwd)


def _prep_fn(q, k, v, gc, gr, b, inv):
    ids = range(len(q))
    c = q[0].shape[0]
    rr = lax.broadcasted_iota(jnp.int32, (c, c), 0)
    cc = lax.broadcasted_iota(jnp.int32, (c, c), 1)
    incl, strict = rr >= cc, rr > cc
    is_last = lax.broadcasted_iota(jnp.int32, (c, 1), 0) == c - 1
    qs = [q[i] * (DN_HEAD_DIM ** -0.5) for i in ids]
    decay = [jnp.where(incl, jnp.exp(jnp.where(incl, gc[i] - gr[i], 0.0)), 0.0) for i in ids]
    kb = [k[i] * b[i] for i in ids]
    vb = [v[i] * b[i] for i in ids]
    kk = [mm_nt(kb[i], k[i]) for i in ids]
    l_mat = [jnp.where(strict, kk[i] * decay[i], 0.0) for i in ids]
    eg = [jnp.exp(gc[i]) for i in ids]
    if inv is None:
        inv = _inv_unit_lower(l_mat, jnp.where(rr == cc, 1.0, 0.0).astype(F32))
    u_wy = [_solve(l_mat[i], vb[i], inv[i]) for i in ids]
    w_wy = [_solve(l_mat[i], kb[i] * eg[i], inv[i]) for i in ids]
    qk = [mm_nt(qs[i], k[i]) * decay[i] for i in ids]
    g_last = [jnp.sum(jnp.where(is_last, gc[i], 0.0), axis=0, keepdims=True) for i in ids]
    k_dec = [k[i] * jnp.exp(g_last[i] - gc[i]) for i in ids]
    egl = [jnp.broadcast_to(jnp.exp(g_last[i]), (1, LANES)) for i in ids]
    return [(w_wy[i], u_wy[i], qs[i] * eg[i], k_dec[i], qk[i], egl[i]) for i in ids], inv


def _seq_fn(w, u, qd, kd, qk, egl, s):
    ids = range(len(w))
    ws = [mm(w[i], s[i]) for i in ids]
    qs = [mm(qd[i], s[i]) for i in ids]
    v_new = [u[i] - ws[i] for i in ids]
    o = [qs[i] + mm(qk[i], v_new[i]) for i in ids]
    s_new = [s[i] * egl[i] + mm_tn(kd[i], v_new[i]) for i in ids]
    return o, s_new


def _lane_col(a, h):
    lane = lax.broadcasted_iota(jnp.int32, (1, LANES), 1)
    return jnp.sum(jnp.where(lane == h, a, 0.0), axis=1, keepdims=True)


def _col_lane(col, h):
    lane = lax.broadcasted_iota(jnp.int32, (1, LANES), 1)
    return jnp.where(lane == h, col, 0.0)


def _head_cols(h):
    return slice(h * DN_HEAD_DIM, (h + 1) * DN_HEAD_DIM)


def _chunk_rows(n):
    return pl.ds(pl.multiple_of(n * DN_CHUNK, DN_CHUNK), DN_CHUNK)


def _delta_prep(q, k, v, gc, grow, beta, *, name):
    t = q.shape[0]
    tm = _tm(t)
    cpb = tm // DN_CHUNK
    n_chunks = t // DN_CHUNK
    group = 2

    def body(q_ref, k_ref, v_ref, gc_ref, gr_ref, b_ref, w_ref, u_ref, qd_ref, kd_ref, qk_ref, egl_ref, inv_ref):
        def step(m, carry):
            probs = [(m * group + e, h) for e in range(group) for h in range(DN_HEADS)]
            gcb = [gc_ref[_chunk_rows(m * group + e), :] for e in range(group)]
            bb = [b_ref[_chunk_rows(m * group + e), :] for e in range(group)]
            grb = [gr_ref[m * group + e] for e in range(group)]
            for e in range(group):
                egl_ref[m * group + e] = jnp.zeros((HALO, LANES), F32)
            outs, invs = _prep_fn(
                [q_ref[_chunk_rows(n), _head_cols(h)] for n, h in probs], [k_ref[_chunk_rows(n), _head_cols(h)] for n, h in probs],
                [v_ref[_chunk_rows(n), _head_cols(h)] for n, h in probs],
                [_lane_col(gcb[e], h) for e in range(group) for h in range(DN_HEADS)],
                [grb[e][h:h + 1, :] for e in range(group) for h in range(DN_HEADS)],
                [_lane_col(bb[e], h) for e in range(group) for h in range(DN_HEADS)], None)
            for (n, h), (w, u, qd, kd, qk, egl), inv in zip(probs, outs, invs):
                rows, cols = _chunk_rows(n), _head_cols(h)
                w_ref[rows, cols] = w.astype(BF16)
                u_ref[rows, cols] = u
                qd_ref[rows, cols] = qd.astype(BF16)
                kd_ref[rows, cols] = kd.astype(BF16)
                qk_ref[n, h] = qk
                inv_ref[n, h] = inv
                egl_ref[n, h:h + 1, :] = egl
            return carry

        lax.fori_loop(0, cpb // group, step, 0)

    row = lambda i: (i, 0)
    tok = pl.BlockSpec((tm, DN_WIDTH), row)
    lanes = pl.BlockSpec((tm, LANES), row)
    sq = pl.BlockSpec((cpb, DN_HEADS, DN_CHUNK, DN_CHUNK), lambda i: (i, 0, 0, 0))
    return pl.pallas_call(
        body, name=name, grid=(t // tm,),
        in_specs=[tok, tok, tok, lanes, pl.BlockSpec((cpb, HALO, DN_CHUNK), lambda i: (i, 0, 0)), lanes],
        out_specs=(tok, tok, tok, tok, sq, pl.BlockSpec((cpb, HALO, LANES), lambda i: (i, 0, 0)), sq),
        out_shape=(jax.ShapeDtypeStruct((t, DN_WIDTH), BF16), jax.ShapeDtypeStruct((t, DN_WIDTH), F32),
                   jax.ShapeDtypeStruct((t, DN_WIDTH), BF16), jax.ShapeDtypeStruct((t, DN_WIDTH), BF16),
                   jax.ShapeDtypeStruct((n_chunks, DN_HEADS, DN_CHUNK, DN_CHUNK), F32),
                   jax.ShapeDtypeStruct((n_chunks, HALO, LANES), F32),
                   jax.ShapeDtypeStruct((n_chunks, DN_HEADS, DN_CHUNK, DN_CHUNK), F32)),
        compiler_params=_cparams(("parallel",)),
    )(q, k, v, gc, grow, beta)


def _delta_par_bwd(q, k, v, gc, grow, beta, inv, dw, du, dqd, dkd, dqk, degl, *, name):
    t = q.shape[0]
    tm = _tm(t)
    cpb = tm // DN_CHUNK
    n_chunks = t // DN_CHUNK
    group = 2

    def body(q_ref, k_ref, v_ref, gc_ref, gr_ref, b_ref, inv_ref, dw_ref, du_ref, dqd_ref, dkd_ref, dqk_ref, degl_ref,
             dq_ref, dk_ref, dv_ref, dgc_ref, dgr_ref, db_ref):
        def step(m, carry):
            chunks = [m * group + e for e in range(group)]
            probs = [(e, h) for e in range(group) for h in range(DN_HEADS)]
            rows = [_chunk_rows(n) for n in chunks]
            gcb, bb = [gc_ref[r, :] for r in rows], [b_ref[r, :] for r in rows]
            grb, deglb = [gr_ref[n] for n in chunks], [degl_ref[n] for n in chunks]
            for n in chunks:
                dgr_ref[n] = jnp.zeros((HALO, DN_CHUNK), F32)
            invs = [inv_ref[chunks[e], h] for e, h in probs]
            _, vjp = jax.vjp(lambda *a: _prep_fn(*a, invs)[0],
                             [q_ref[rows[e], _head_cols(h)] for e, h in probs], [k_ref[rows[e], _head_cols(h)] for e, h in probs],
                             [v_ref[rows[e], _head_cols(h)] for e, h in probs], [_lane_col(gcb[e], h) for e, h in probs],
                             [grb[e][h:h + 1, :] for e, h in probs], [_lane_col(bb[e], h) for e, h in probs])
            dq, dk, dv, dgc, dgr, db = vjp([(dw_ref[rows[e], _head_cols(h)], du_ref[rows[e], _head_cols(h)],
                                             dqd_ref[rows[e], _head_cols(h)], dkd_ref[rows[e], _head_cols(h)],
                                             dqk_ref[chunks[e], h], deglb[e][h:h + 1, :]) for e, h in probs])
            dgc_acc = [jnp.zeros((DN_CHUNK, LANES), F32) for _ in chunks]
            db_acc = [jnp.zeros((DN_CHUNK, LANES), F32) for _ in chunks]
            for i, (e, h) in enumerate(probs):
                cols = _head_cols(h)
                dq_ref[rows[e], cols] = dq[i]
                dk_ref[rows[e], cols] = dk[i]
                dv_ref[rows[e], cols] = dv[i]
                dgr_ref[chunks[e], h:h + 1, :] = dgr[i]
                dgc_acc[e] = dgc_acc[e] + _col_lane(dgc[i], h)
                db_acc[e] = db_acc[e] + _col_lane(db[i], h)
            for e in range(group):
                dgc_ref[rows[e], :] = dgc_acc[e]
                db_ref[rows[e], :] = db_acc[e]
            return carry

        lax.fori_loop(0, cpb // group, step, 0)

    row = lambda i: (i, 0)
    tok = pl.BlockSpec((tm, DN_WIDTH), row)
    lanes = pl.BlockSpec((tm, LANES), row)
    sq = pl.BlockSpec((cpb, DN_HEADS, DN_CHUNK, DN_CHUNK), lambda i: (i, 0, 0, 0))
    grs = pl.BlockSpec((cpb, HALO, DN_CHUNK), lambda i: (i, 0, 0))
    return pl.pallas_call(
        body, name=name, grid=(t // tm,),
        in_specs=[tok, tok, tok, lanes, grs, lanes, sq, tok, tok, tok, tok, sq, pl.BlockSpec((cpb, HALO, LANES), lambda i: (i, 0, 0))],
        out_specs=(tok, tok, tok, lanes, grs, lanes),
        out_shape=(jax.ShapeDtypeStruct((t, DN_WIDTH), F32),) * 3
        + (jax.ShapeDtypeStruct((t, LANES), F32), jax.ShapeDtypeStruct((n_chunks, HALO, DN_CHUNK), F32),
           jax.ShapeDtypeStruct((t, LANES), F32)),
        compiler_params=_cparams(("parallel",)),
    )(q, k, v, gc, grow, beta, inv, dw, du, dqd, dkd, dqk, degl)


def _seq_specs(n_seq, seq, reverse):
    tm = _tm(seq)
    nb = seq // tm
    cpb = tm // DN_CHUNK
    pair = 2 if n_seq % 2 == 0 else 1
    blk = (lambda j: nb - 1 - j) if reverse else (lambda j: j)
    tok = pl.BlockSpec((pair, tm, DN_WIDTH), lambda b, j: (b, blk(j), 0))
    sq = pl.BlockSpec((pair, cpb, DN_HEADS, DN_CHUNK, DN_CHUNK), lambda b, j: (b, blk(j), 0, 0, 0))
    rows8 = pl.BlockSpec((pair, cpb, HALO, LANES), lambda b, j: (b, blk(j), 0, 0))
    state = pl.BlockSpec((pair, cpb, DN_HEADS, DN_HEAD_DIM, DN_HEAD_DIM), lambda b, j: (b, blk(j), 0, 0, 0))
    return nb, cpb, pair, tok, sq, rows8, state


def _by_seq(a, n_seq):
    return a.reshape((n_seq, a.shape[0] // n_seq) + a.shape[1:])


def _flat_seq(a):
    return a.reshape((a.shape[0] * a.shape[1],) + a.shape[2:])


def _delta_seq_fwd(w, u, qd, kd, qk, egl, n_seq, seq, *, name):
    nb, cpb, pair, tok, sq, rows8, state = _seq_specs(n_seq, seq, False)
    probs = [(e, h) for e in range(pair) for h in range(DN_HEADS)]

    def body(w_ref, u_ref, qd_ref, kd_ref, qk_ref, egl_ref, o_ref, st_ref, s_s):
        @pl.when(pl.program_id(1) == 0)
        def _():
            s_s[...] = jnp.zeros_like(s_s)

        def step(n, carry):
            rows = _chunk_rows(n)
            eglb = [egl_ref[e, n] for e in range(pair)]
            s = [s_s[e, h] for e, h in probs]
            for (e, h), s_eh in zip(probs, s):
                st_ref[e, n, h] = s_eh
            o, s_new = _seq_fn([w_ref[e, rows, _head_cols(h)] for e, h in probs], [u_ref[e, rows, _head_cols(h)] for e, h in probs],
                               [qd_ref[e, rows, _head_cols(h)] for e, h in probs], [kd_ref[e, rows, _head_cols(h)] for e, h in probs],
                               [qk_ref[e, n, h] for e, h in probs], [eglb[e][h:h + 1, :] for e, h in probs], s)
            for i, (e, h) in enumerate(probs):
                o_ref[e, rows, _head_cols(h)] = o[i]
                s_s[e, h] = s_new[i]
            return carry

        lax.fori_loop(0, cpb, step, 0)

    o, states = pl.pallas_call(
        body, name=name, grid=(n_seq // pair, nb),
        in_specs=[tok, tok, tok, tok, sq, rows8],
        out_specs=(tok, state),
        out_shape=(jax.ShapeDtypeStruct((n_seq, seq, DN_WIDTH), F32),
                   jax.ShapeDtypeStruct((n_seq, seq // DN_CHUNK, DN_HEADS, DN_HEAD_DIM, DN_HEAD_DIM), F32)),
        scratch_shapes=[pltpu.VMEM((pair, DN_HEADS, DN_HEAD_DIM, DN_HEAD_DIM), F32)],
        compiler_params=_cparams(("parallel", "arbitrary")),
    )(*[_by_seq(a, n_seq) for a in (w, u, qd, kd, qk, egl)])
    return _flat_seq(o), _flat_seq(states)


def _delta_seq_bwd(w, u, qd, kd, qk, egl, states, do, n_seq, seq, *, name):
    nb, cpb, pair, tok, sq, rows8, state = _seq_specs(n_seq, seq, True)
    probs = [(e, h) for e in range(pair) for h in range(DN_HEADS)]

    def body(w_ref, u_ref, qd_ref, kd_ref, qk_ref, egl_ref, st_ref, do_ref, dw_ref, du_ref, dqd_ref, dkd_ref, dqk_ref,
             degl_ref, ds_s):
        @pl.when(pl.program_id(1) == 0)
        def _():
            ds_s[...] = jnp.zeros_like(ds_s)

        def step(m, carry):
            n = cpb - 1 - m
            rows = _chunk_rows(n)
            eglb = [egl_ref[e, n] for e in range(pair)]
            for e in range(pair):
                degl_ref[e, n] = jnp.zeros((HALO, LANES), F32)
            _, vjp = jax.vjp(_seq_fn, [w_ref[e, rows, _head_cols(h)].astype(F32) for e, h in probs],
                             [u_ref[e, rows, _head_cols(h)] for e, h in probs],
                             [qd_ref[e, rows, _head_cols(h)].astype(F32) for e, h in probs],
                             [kd_ref[e, rows, _head_cols(h)].astype(F32) for e, h in probs],
                             [qk_ref[e, n, h] for e, h in probs], [eglb[e][h:h + 1, :] for e, h in probs],
                             [st_ref[e, n, h] for e, h in probs])
            dw, du, dqd, dkd, dqk, degl, ds_in = vjp(([do_ref[e, rows, _head_cols(h)] for e, h in probs],
                                                      [ds_s[e, h] for e, h in probs]))
            for i, (e, h) in enumerate(probs):
                cols = _head_cols(h)
                dw_ref[e, rows, cols] = dw[i]
                du_ref[e, rows, cols] = du[i]
                dqd_ref[e, rows, cols] = dqd[i]
                dkd_ref[e, rows, cols] = dkd[i]
                dqk_ref[e, n, h] = dqk[i]
                degl_ref[e, n, h:h + 1, :] = degl[i]
                ds_s[e, h] = ds_in[i]
            return carry

        lax.fori_loop(0, cpb, step, 0)

    nc = seq // DN_CHUNK
    outs = pl.pallas_call(
        body, name=name, grid=(n_seq // pair, nb),
        in_specs=[tok, tok, tok, tok, sq, rows8, state, tok],
        out_specs=(tok, tok, tok, tok, sq, rows8),
        out_shape=(jax.ShapeDtypeStruct((n_seq, seq, DN_WIDTH), F32),) * 4
        + (jax.ShapeDtypeStruct((n_seq, nc, DN_HEADS, DN_CHUNK, DN_CHUNK), F32),
           jax.ShapeDtypeStruct((n_seq, nc, HALO, LANES), F32)),
        scratch_shapes=[pltpu.VMEM((pair, DN_HEADS, DN_HEAD_DIM, DN_HEAD_DIM), F32)],
        compiler_params=_cparams(("parallel", "arbitrary")),
    )(*[_by_seq(a, n_seq) for a in (w, u, qd, kd, qk, egl, states, do)])
    return tuple(_flat_seq(a) for a in outs)


def _dn_gate(o, z, dnw):
    return o * lax.rsqrt(jnp.mean(o * o, axis=-1, keepdims=True) + EPS) * dnw * _silu(z)


def _mix_out_fwd(x, sg, o, z, wo_sg, wo_dn, dnw, *, name):
    t = x.shape[0]
    tm = _tm(t)

    def body(x_ref, sg_ref, o_ref, z_ref, wsg_ref, wdn_ref, dnw_ref, y_ref, dn_s):
        for h, (oh, zh) in enumerate(zip(_split_heads(o_ref, 0), _split_heads(z_ref, 0))):
            dn_s[:, h * DN_HEAD_DIM:(h + 1) * DN_HEAD_DIM] = _dn_gate(oh, zh, dnw_ref[...]).astype(BF16)
        y_ref[...] = (x_ref[...] + jnp.dot(sg_ref[...].astype(BF16), wsg_ref[...], preferred_element_type=F32)
                      + jnp.dot(dn_s[...], wdn_ref[...], preferred_element_type=F32))

    row = lambda i: (i, 0)
    const = lambda i: (0, 0)
    half = pl.BlockSpec((tm, DN_WIDTH), row)
    return pl.pallas_call(
        body, name=name, grid=(t // tm,),
        in_specs=[pl.BlockSpec((tm, D_MODEL), row), half, half, half, pl.BlockSpec((SG_WIDTH, D_MODEL), const),
                  pl.BlockSpec((DN_WIDTH, D_MODEL), const), pl.BlockSpec((1, DN_HEAD_DIM), const)],
        out_specs=pl.BlockSpec((tm, D_MODEL), row),
        out_shape=jax.ShapeDtypeStruct((t, D_MODEL), F32),
        scratch_shapes=[pltpu.VMEM((tm, DN_WIDTH), BF16)],
        compiler_params=_cparams(("parallel",)),
    )(x, sg, o, z, wo_sg, wo_dn, dnw)


def _mix_out_bwd(dy, sg, o, z, wo_sg, wo_dn, dnw, *, name):
    t = dy.shape[0]
    tm = _tm(t)

    def body(dy_ref, sg_ref, o_ref, z_ref, wsg_ref, wdn_ref, dnw_ref, dsg_ref, do_ref, dz_ref, dwsg_ref, dwdn_ref, ddnw_ref, dn_s):
        i = pl.program_id(0)
        dyb = dy_ref[...].astype(BF16)
        nt = (((1,), (1,)), ((), ()))
        tn = (((0,), (0,)), ((), ()))
        dsg_ref[...] = lax.dot_general(dyb, wsg_ref[...], nt, preferred_element_type=F32)
        ddn = lax.dot_general(dyb, wdn_ref[...], nt, preferred_element_type=F32)
        ddnw = None
        for h, (oh, zh) in enumerate(zip(_split_heads(o_ref, 0), _split_heads(z_ref, 0))):
            cols = slice(h * DN_HEAD_DIM, (h + 1) * DN_HEAD_DIM)
            out, vjp = jax.vjp(_dn_gate, oh, zh, dnw_ref[...])
            dn_s[:, cols] = out.astype(BF16)
            doh, dzh, dw = vjp(ddn[:, cols])
            do_ref[:, cols] = doh
            dz_ref[:, cols] = dzh.astype(BF16)
            ddnw = dw if ddnw is None else ddnw + dw
        _acc_out(ddnw_ref, i == 0, ddnw)
        _acc_out(dwsg_ref, i == 0, lax.dot_general(sg_ref[...].astype(BF16), dyb, tn, preferred_element_type=F32))
        _acc_out(dwdn_ref, i == 0, lax.dot_general(dn_s[...], dyb, tn, preferred_element_type=F32))

    row = lambda i: (i, 0)
    const = lambda i: (0, 0)
    half = pl.BlockSpec((tm, DN_WIDTH), row)
    wspec = pl.BlockSpec((DN_WIDTH, D_MODEL), const)
    return pl.pallas_call(
        body, name=name, grid=(t // tm,),
        in_specs=[pl.BlockSpec((tm, D_MODEL), row), half, half, half, wspec, wspec, pl.BlockSpec((1, DN_HEAD_DIM), const)],
        out_specs=(half, half, half, wspec, wspec, pl.BlockSpec((1, DN_HEAD_DIM), const)),
        out_shape=(jax.ShapeDtypeStruct((t, DN_WIDTH), F32),) * 2 + (jax.ShapeDtypeStruct((t, DN_WIDTH), BF16),)
        + (jax.ShapeDtypeStruct((DN_WIDTH, D_MODEL), F32),) * 2 + (jax.ShapeDtypeStruct((1, DN_HEAD_DIM), F32),),
        scratch_shapes=[pltpu.VMEM((tm, DN_WIDTH), BF16)],
        compiler_params=_cparams(("arbitrary",)),
    )(dy, sg, o, z, wo_sg, wo_dn, dnw)


_MESH = pl.DeviceIdType.MESH
_HBM = pl.BlockSpec(memory_space=pl.ANY)


def _mesh_pos():
    x, y, c = lax.axis_index("x"), lax.axis_index("y"), lax.axis_index("c")
    return x, y, c, [(1 - x, y), (x, 1 - y), (1 - x, 1 - y)]


def _gather2(arrs, *, name):
    n = len(arrs)
    slots = N_DEV - 1

    def body(*refs):
        in_refs, out_refs = refs[:n], refs[n:2 * n]
        send_sems, recv_sems, local_sems = refs[2 * n:]
        x, y, c, chips = _mesh_pos()
        me, sibling = (x, y, c), (x, y, 1 - c)

        def copy(k, slot, block, to, src=None):
            dst = out_refs[k].at[4 * block[0] + 2 * block[1] + block[2]]
            return pltpu.make_async_remote_copy(src_ref=dst if src is None else src, dst_ref=dst,
                                                send_sem=send_sems.at[k * slots + slot], recv_sem=recv_sems.at[k * slots + slot],
                                                device_id=to, device_id_type=_MESH)

        local = [pltpu.make_async_copy(in_refs[k], out_refs[k].at[4 * x + 2 * y + c], local_sems.at[k]) for k in range(n)]
        sent = []
        for k in range(n):
            sent.append(copy(k, 0, me, sibling, src=in_refs[k]))
            sent += [copy(k, 1 + j, me, (*chip, c), src=in_refs[k]) for j, chip in enumerate(chips)]
        for cp in local + sent:
            cp.start()
        for j, chip in enumerate(chips):
            for k in range(n):
                copy(k, 1 + j, (*chip, c), me).wait_recv()
                passed = copy(k, 4 + j, (*chip, c), sibling)
                passed.start()
                sent.append(passed)
        for k in range(n):
            copy(k, 0, sibling, me).wait_recv()
            for j, chip in enumerate(chips):
                copy(k, 4 + j, (*chip, 1 - c), me).wait_recv()
        for cp in sent:
            cp.wait_send()
        for cp in local:
            cp.wait()

    return pl.pallas_call(
        body, name=name, in_specs=[_HBM] * n, out_specs=(_HBM,) * n,
        out_shape=tuple(jax.ShapeDtypeStruct((N_DEV,) + a.shape, a.dtype) for a in arrs),
        scratch_shapes=[pltpu.SemaphoreType.DMA((n * slots,)), pltpu.SemaphoreType.DMA((n * slots,)),
                        pltpu.SemaphoreType.DMA((n,))],
    )(*arrs)


_SEM = pl.BlockSpec(memory_space=pltpu.SEMAPHORE)
_EFFECT = pltpu.SideEffectType.DATAFLOW_SIDE_EFFECTING


def _direct_copies(src_refs, land_refs, send_sems, recv_sems, gather):
    x, y, c, _ = _mesh_pos()
    me = 4 * x + 2 * y + c
    n_peer = N_DEV - 1
    copies = []
    for r in range(1, N_DEV):
        px = 1 - x if r & 4 else x
        py = 1 - y if r & 2 else y
        pc = 1 - c if r & 1 else c
        for k, (src, land) in enumerate(zip(src_refs, land_refs)):
            copies.append(pltpu.make_async_remote_copy(
                src_ref=src if gather else src.at[4 * px + 2 * py + pc], dst_ref=land.at[me],
                send_sem=send_sems.at[k * n_peer + r - 1], recv_sem=recv_sems.at[k * n_peer + r - 1],
                device_id=(px, py, pc), device_id_type=_MESH))
    return copies


def _send_start(arrs, gather, after=None, *, name):
    n = len(arrs)
    lands = [lax.empty(((N_DEV,) + a.shape) if gather else a.shape, a.dtype) for a in arrs]
    n_in = 2 * n + (0 if after is None else 1)

    def body(*refs):
        src_refs, land_refs, send_sems, recv_sems, token = refs[:n], refs[n:2 * n], refs[n_in], refs[n_in + 1], refs[-1]
        for cp in _direct_copies(src_refs, land_refs, send_sems, recv_sems, gather):
            cp.start()
        token[...] = jnp.zeros_like(token)

    n_sem = n * (N_DEV - 1)
    bufs = list(arrs) + lands
    out = pl.pallas_call(
        body, name=name,
        out_shape=(pltpu.SemaphoreType.DMA((n_sem,)), pltpu.SemaphoreType.DMA((n_sem,)))
        + tuple(pltpu.HBM(b.shape, b.dtype) for b in bufs) + (jax.ShapeDtypeStruct((HALO, LANES), F32),),
        in_specs=[_HBM] * n_in, out_specs=(_SEM, _SEM) + (_HBM,) * (2 * n) + (pl.BlockSpec(memory_space=pltpu.VMEM),),
        input_output_aliases={i: 2 + i for i in range(2 * n)},
        compiler_params=pltpu.CompilerParams(has_side_effects=_EFFECT),
    )(*[pltpu.with_memory_space_constraint(b, pltpu.HBM) for b in bufs], *([] if after is None else [after]))
    return (out[0], out[1], list(out[2:2 + n]), list(out[2 + n:2 + 2 * n])), out[-1]


def _send_wait(started, gather, after, *, name):
    send_sems, recv_sems, srcs, lands = started
    n = len(srcs)

    def body(*refs):
        src_refs, land_refs, send_ref, recv_ref = refs[:n], refs[n:2 * n], refs[2 * n], refs[2 * n + 1]
        for cp in _direct_copies(src_refs, land_refs, send_ref, recv_ref, gather):
            cp.wait_send()
            cp.wait_recv()

    bufs = srcs + lands
    out = pl.pallas_call(
        body, name=name, out_shape=tuple(pltpu.HBM(b.shape, b.dtype) for b in bufs),
        in_specs=[_HBM] * (2 * n) + [_SEM, _SEM, _HBM], out_specs=(_HBM,) * (2 * n),
        input_output_aliases={i: i for i in range(2 * n)},
        compiler_params=pltpu.CompilerParams(has_side_effects=_EFFECT),
    )(*bufs, send_sems, recv_sems, after)
    return list(out[n:])


def _row_block(rows, limit=256):
    best = rows
    for cand in range(8, limit + 1, 8):
        if rows % cand == 0:
            best = cand
    return best if rows > limit else rows


def _adam(gp, w, m, v, *, name):
    p, rows, cols = gp.shape
    rb = _row_block(rows)

    def body(gp_ref, w_ref, m_ref, v_ref, g_ref, d_ref, m2_ref, v2_ref):
        g = gp_ref[0].astype(F32)
        for s in range(1, p):
            g = g + gp_ref[s].astype(F32)
        m2 = ADAM_B1 * m_ref[...] + (1.0 - ADAM_B1) * g
        v2 = ADAM_B2 * v_ref[...] + (1.0 - ADAM_B2) * (g * g)
        m_hat = m2 / (1.0 - ADAM_B1 ** ADAM_STEP)
        v_hat = v2 / (1.0 - ADAM_B2 ** ADAM_STEP)
        g_ref[...] = g
        d_ref[...] = -ADAM_LR * (m_hat / (jnp.sqrt(v_hat) + ADAM_EPS) + ADAM_WD * w_ref[...])
        m2_ref[...] = m2
        v2_ref[...] = v2

    blk = pl.BlockSpec((rb, cols), lambda i: (i, 0))
    return pl.pallas_call(
        body, name=name, grid=(rows // rb,),
        in_specs=[pl.BlockSpec((p, rb, cols), lambda i: (0, i, 0)), blk, blk, blk],
        out_specs=(blk,) * 4, out_shape=(jax.ShapeDtypeStruct((rows, cols), F32),) * 4,
        compiler_params=_cparams(("parallel",)),
    )(gp, w, m, v)


def _cols_full(g):
    return jnp.transpose(g, (1, 0, 2)).reshape(g.shape[1], N_DEV * g.shape[2])


def _pad_lanes(a, width=LANES):
    return jnp.pad(a, ((0, 0), (0, width - a.shape[1])))


def _chunk_rows_of(a):
    by_chunk = jnp.transpose(a[:, :DN_HEADS].reshape(-1, DN_CHUNK, DN_HEADS), (0, 2, 1))
    return jnp.pad(by_chunk, ((0, 0), (0, HALO - DN_HEADS), (0, 0)))


_SMALL = (("ffn1_norm", D_MODEL), ("mix_norm", D_MODEL), ("ffn2_norm", D_MODEL), ("final_norm", D_MODEL), ("a_log", DN_HEADS),
          ("dt_bias", DN_HEADS), ("dn_norm", DN_HEAD_DIM), ("sg_ln_g", SG_WIDTH), ("sg_ln_b", SG_WIDTH),
          ("sg_w", SG_GROUPS * SG_CHUNK * SG_CHUNK), ("sg_b", SG_GROUPS * SG_CHUNK), ("conv_w", CONV_K * 3 * DN_WIDTH))
_SMALL_ROWS = 1128
_SMALL_SHAPES = {"ffn1_norm": (1, D_MODEL), "mix_norm": (1, D_MODEL), "ffn2_norm": (1, D_MODEL), "final_norm": (D_MODEL,),
                 "a_log": (1, DN_HEADS), "dt_bias": (1, DN_HEADS), "dn_norm": (1, DN_HEAD_DIM), "sg_ln_g": (1, SG_WIDTH),
                 "sg_ln_b": (1, SG_WIDTH), "sg_w": (1, SG_GROUPS, SG_CHUNK, SG_CHUNK), "sg_b": (1, SG_GROUPS, SG_CHUNK)}


def _pack_small(d):
    flat = jnp.concatenate([d[name].reshape(-1) for name, _ in _SMALL])
    return jnp.pad(flat, (0, _SMALL_ROWS * LANES - flat.shape[0])).reshape(_SMALL_ROWS, LANES)


def _unpack_small(a):
    flat, out, at = a.reshape(-1), {}, 0
    for name, size in _SMALL:
        out[name] = flat[at:at + size]
        at += size
    return out


def kernel(x, ffn1_norm, ffn1_w_gate, ffn1_w_up, ffn1_w_down, mix_norm, w_in, conv_w, a_log, dt_bias, dn_norm, sg_ln_g, sg_ln_b, sg_w, sg_b, w_out, ffn2_norm, ffn2_w_gate, ffn2_w_up, ffn2_w_down, final_norm, loss_target, m_ffn1_norm, m_ffn1_w_gate, m_ffn1_w_up, m_ffn1_w_down, m_mix_norm, m_w_in, m_conv_w, m_a_log, m_dt_bias, m_dn_norm, m_sg_ln_g, m_sg_ln_b, m_sg_w, m_sg_b, m_w_out, m_ffn2_norm, m_ffn2_w_gate, m_ffn2_w_up, m_ffn2_w_down, m_final_norm, v_ffn1_norm, v_ffn1_w_gate, v_ffn1_w_up, v_ffn1_w_down, v_mix_norm, v_w_in, v_conv_w, v_a_log, v_dt_bias, v_dn_norm, v_sg_ln_g, v_sg_ln_b, v_sg_w, v_sg_b, v_w_out, v_ffn2_norm, v_ffn2_w_gate, v_ffn2_w_up, v_ffn2_w_down, v_final_norm):
    weights = dict(ffn1_norm=ffn1_norm, ffn1_w_gate=ffn1_w_gate, ffn1_w_up=ffn1_w_up, ffn1_w_down=ffn1_w_down, mix_norm=mix_norm, w_in=w_in, conv_w=conv_w, a_log=a_log, dt_bias=dt_bias, dn_norm=dn_norm, sg_ln_g=sg_ln_g, sg_ln_b=sg_ln_b, sg_w=sg_w, sg_b=sg_b, w_out=w_out, ffn2_norm=ffn2_norm, ffn2_w_gate=ffn2_w_gate, ffn2_w_up=ffn2_w_up, ffn2_w_down=ffn2_w_down, final_norm=final_norm)
    mom_m = dict(ffn1_norm=m_ffn1_norm, ffn1_w_gate=m_ffn1_w_gate, ffn1_w_up=m_ffn1_w_up, ffn1_w_down=m_ffn1_w_down, mix_norm=m_mix_norm, w_in=m_w_in, conv_w=m_conv_w, a_log=m_a_log, dt_bias=m_dt_bias, dn_norm=m_dn_norm, sg_ln_g=m_sg_ln_g, sg_ln_b=m_sg_ln_b, sg_w=m_sg_w, sg_b=m_sg_b, w_out=m_w_out, ffn2_norm=m_ffn2_norm, ffn2_w_gate=m_ffn2_w_gate, ffn2_w_up=m_ffn2_w_up, ffn2_w_down=m_ffn2_w_down, final_norm=m_final_norm)
    mom_v = dict(ffn1_norm=v_ffn1_norm, ffn1_w_gate=v_ffn1_w_gate, ffn1_w_up=v_ffn1_w_up, ffn1_w_down=v_ffn1_w_down, mix_norm=v_mix_norm, w_in=v_w_in, conv_w=v_conv_w, a_log=v_a_log, dt_bias=v_dt_bias, dn_norm=v_dn_norm, sg_ln_g=v_sg_ln_g, sg_ln_b=v_sg_ln_b, sg_w=v_sg_w, sg_b=v_sg_b, w_out=v_w_out, ffn2_norm=v_ffn2_norm, ffn2_w_gate=v_ffn2_w_gate, ffn2_w_up=v_ffn2_w_up, ffn2_w_down=v_ffn2_w_down, final_norm=v_final_norm)
    order = list(weights)
    big = ("ffn1_w_gate", "ffn1_w_up", "ffn1_w_down", "w_in", "w_out", "ffn2_w_gate", "ffn2_w_up", "ffn2_w_down")
    col_sharded = ("ffn1_w_gate", "ffn1_w_up", "w_in", "ffn2_w_gate", "ffn2_w_up")

    n_seq, seq, _ = x.shape
    t = n_seq * seq
    me = 4 * lax.axis_index("x") + 2 * lax.axis_index("y") + lax.axis_index("c")
    x0 = x.reshape(t, D_MODEL)
    tgt = loss_target.reshape(t, D_MODEL)

    def fill_own(land, own_block):
        return lax.dynamic_update_index_in_dim(land, own_block, me, 0)

    def rows_view(n, a):
        return jnp.transpose(a) if n in col_sharded else a

    def as_full(n, g):
        return g.reshape(-1, g.shape[-1])

    shards = {n: rows_view(n, weights[n][0]).astype(BF16) for n in big}
    ffn1_names, mix_names, ffn2_names = big[:3], big[3:5], big[5:]
    full = {n: as_full(n, g) for n, g in zip(ffn1_names, _gather2([shards[n] for n in ffn1_names], name="gather_ffn1"))}
    mix_srcs = [shards[n] for n in mix_names] + [conv_w[0]]
    mix_started, mix_token = _send_start(mix_srcs, True, full[ffn1_names[2]], name="gather_mix_start")
    ffn2_started, ffn2_token = _send_start([shards[n] for n in ffn2_names], True, mix_token, name="gather_ffn2_start")
    ffn1_norm_fwd = ffn1_norm + ffn2_token[:1, :1]
    alog, dtb = _pad_lanes(a_log), _pad_lanes(dt_bias)
    sgbt = _pad_lanes(sg_b[0].T)
    fnw = final_norm.reshape(1, D_MODEL)

    x1, h1, g1, u1 = _ffn_fwd(x0, ffn1_norm_fwd, full["ffn1_w_gate"], full["ffn1_w_up"], full["ffn1_w_down"], name="ffn1_fwd")
    mix_lands = [fill_own(land, src) for land, src in zip(_send_wait(mix_started, True, x1, name="gather_mix_wait"), mix_srcs)]
    full.update({n: as_full(n, g) for n, g in zip(mix_names, mix_lands)})
    conv_full = _cols_full(mix_lands[-1])
    w_in_t = full["w_in"]
    offs = (0, SG_WIDTH, 2 * SG_WIDTH, 2 * SG_WIDTH + 3 * DN_WIDTH, 2 * SG_WIDTH + 4 * DN_WIDTH)
    n_proj = offs[-1]

    def pad_rows(a):
        return jnp.pad(a, ((0, LANES - a.shape[0]), (0, 0)))

    ws = [w_in_t[offs[0]:offs[1]], w_in_t[offs[1]:offs[2]], w_in_t[offs[2]:offs[3]], w_in_t[offs[3]:offs[4]],
          pad_rows(w_in_t[n_proj:n_proj + DN_HEADS]), pad_rows(w_in_t[n_proj + DN_HEADS:n_proj + 2 * DN_HEADS])]
    wo_sg, wo_dn = full["w_out"][:SG_WIDTH], full["w_out"][SG_WIDTH:]
    u, v, qkv, z, bpre, apre = _mix_in_fwd(x1, mix_norm, ws, name="mix_in_fwd")
    sg_out = _sg_fwd(u, v, sg_ln_g, sg_ln_b, sg_w[0], sgbt, name="sg_fwd")
    q, k, vv, beta, gc = _dn_prep_fwd(qkv, bpre, apre, conv_full, alog, dtb, seq, name="dn_prep_fwd")
    grow = _chunk_rows_of(gc)
    wy_w, wy_u, q_dec, k_dec, qk, egl, inv = _delta_prep(q, k, vv, gc, grow, beta, name="delta_prep")
    o, states = _delta_seq_fwd(wy_w, wy_u, q_dec, k_dec, qk, egl, n_seq, seq, name="delta_seq_fwd")
    x2 = _mix_out_fwd(x1, sg_out, o, z, wo_sg, wo_dn, dn_norm, name="mix_out_fwd")
    ffn2_lands = _send_wait(ffn2_started, True, x2, name="gather_ffn2_wait")
    full.update({n: as_full(n, fill_own(land, shards[n])) for n, land in zip(ffn2_names, ffn2_lands)})
    dx3, loss_part, d_fn, h2, g2, u2 = _ffn_fwd(x2, ffn2_norm, full["ffn2_w_gate"], full["ffn2_w_up"], full["ffn2_w_down"],
                                                tgt, fnw, name="ffn2_fwd_loss")
    loss = lax.psum(loss_part[0, 0], ("x", "y", "c"))

    dx2, d_n2, d_g2, d_u2, d_d2 = _ffn_bwd(x2, ffn2_norm, h2, g2, u2, full["ffn2_w_gate"], full["ffn2_w_up"],
                                           full["ffn2_w_down"], dx3, name="ffn2_bwd")
    def by_owner(d_rows):
        return d_rows.reshape(N_DEV, -1, D_MODEL)

    ffn2_pieces = [by_owner(d_g2), by_owner(d_u2), by_owner(d_d2)]
    ffn2_sent, sent_token = _send_start(ffn2_pieces, False, name="grads_ffn2_start")
    dsg, do, dz, d_wo_sg, d_wo_dn, d_dnw = _mix_out_bwd(dx2, sg_out, o, z, wo_sg, wo_dn, dn_norm + sent_token[:1, :1],
                                                        name="mix_out_bwd")
    d_seq = _delta_seq_bwd(wy_w, wy_u, q_dec, k_dec, qk, egl, states, do, n_seq, seq, name="delta_seq_bwd")
    dq, dk, dv, dgc_a, dgrow, dbeta = _delta_par_bwd(q, k, vv, gc, grow, beta, inv, *d_seq, name="delta_par_bwd")
    dgc_b = _pad_lanes(jnp.transpose(dgrow[:, :DN_HEADS, :], (0, 2, 1)).reshape(t, DN_HEADS))
    dy_conv, dbpre, dapre, d_alog, d_dtb = _dn_prep_bwd(qkv, bpre, apre, conv_full, alog, dtb, dq, dk, dv, dbeta, dgc_a, dgc_b,
                                                        seq, name="dn_prep_bwd")
    dqkv, d_conv = _conv_bwd(qkv, dy_conv, conv_full, seq, name="conv_bwd")
    du, dvv, d_lng, d_lnb, d_wc, d_sgbt = _sg_bwd(u, v, sg_ln_g, sg_ln_b, sg_w[0], sgbt, dsg, name="sg_bwd")
    dx1, d_mixn, d_wp = _mix_in_bwd(x1, mix_norm, ws, dx2, (du, dvv, dqkv, dz, dbpre, dapre), name="mix_in_bwd")
    d_w_in_t = jnp.concatenate([d_wp[:n_proj], d_wp[_PROJ_OFFSETS[4]:_PROJ_OFFSETS[4] + DN_HEADS],
                                d_wp[_PROJ_OFFSETS[5]:_PROJ_OFFSETS[5] + DN_HEADS]], axis=0)
    d_w_out = jnp.concatenate([d_wo_sg, d_wo_dn], axis=0)
    mix_pieces = [by_owner(d_w_in_t), by_owner(d_w_out).astype(BF16)]
    mix_sent, sent_token = _send_start(mix_pieces, False, name="grads_mix_start")
    grad_x, d_n1, dg1, du1, a1, dyh1 = _ffn_bwd_x(x0, ffn1_norm + sent_token[:1, :1], g1, u1, full["ffn1_w_gate"],
                                                  full["ffn1_w_up"], full["ffn1_w_down"], dx1, name="ffn1_bwd_x")
    small_grads = dict(ffn1_norm=d_n1, mix_norm=d_mixn, ffn2_norm=d_n2, final_norm=d_fn, a_log=d_alog[:, :DN_HEADS],
                       dt_bias=d_dtb[:, :DN_HEADS], dn_norm=d_dnw, sg_ln_g=d_lng, sg_ln_b=d_lnb, sg_w=d_wc,
                       sg_b=d_sgbt[:, :SG_GROUPS].T, conv_w=d_conv[:CONV_K])
    small_src = _pack_small(small_grads)
    small_sent, small_token = _send_start([small_src], True, name="small_grads_start")
    late, tokens = [], []

    def send_early(k, grad):
        piece = by_owner(grad)
        sent, token = _send_start([piece], False, name="grads_" + ffn1_names[k] + "_start")
        late.append(((ffn1_names[k],), sent, [piece]))
        tokens.append(token)
        return token

    _ffn_wgrads(h1, dg1, du1, a1, dyh1, send_early, small_token, name="ffn1_bwd")

    res = {}
    after = tokens[-1]

    def update(names, sent, pieces, after):
        lands = _send_wait(sent, False, after, name="grads_" + names[0] + "_wait")
        for n, land, p in zip(names, lands, pieces):
            got = fill_own(land, lax.dynamic_index_in_dim(p, me, 0, keepdims=False))
            upd = _adam(got, *[rows_view(n, src[n][0]) for src in (weights, mom_m, mom_v)], name="adam_" + n)
            res[n] = [rows_view(n, a) for a in upd]
            after = upd[0]
        return after

    for group in [(ffn2_names, ffn2_sent, ffn2_pieces), (mix_names, mix_sent, mix_pieces)] + late[:-1]:
        after = update(*group, after)
    (small_land,) = _send_wait(small_sent, True, after, name="small_grads_wait")
    small_parts = fill_own(small_land, small_src)
    zeros_conv = jnp.zeros((CONV_K * 3 * DN_WIDTH,), F32)
    packed = [_pack_small({**{n: src[n] for n, _ in _SMALL if n != "conv_w"}, "conv_w": zeros_conv})
              for src in (weights, mom_m, mom_v)]
    small_upd = _adam(small_parts, *packed, name="adam_small")
    small_res = [_unpack_small(a) for a in small_upd]
    conv_grad = lax.dynamic_slice_in_dim(small_res[0]["conv_w"].reshape(CONV_K, 3 * DN_WIDTH), me * (3 * DN_WIDTH // N_DEV),
                                         3 * DN_WIDTH // N_DEV, axis=1)
    res["conv_w"] = _adam(conv_grad[None], conv_w[0], m_conv_w[0], v_conv_w[0], name="adam_conv_w")
    update(*late[-1], res["conv_w"][0])

    outs = [[], [], [], []]
    for n in order:
        for kind in range(4):
            if n in res:
                outs[kind].append(res[n][kind][None])
            else:
                outs[kind].append(small_res[kind][n].reshape(_SMALL_SHAPES[n]))
    return (loss, grad_x.reshape(x.shape), *outs[0], *outs[1], *outs[2], *outs[3])
```

```python
import jax
import jax.numpy as jnp
from jax import lax
from jax.experimental import pallas as pl
from jax.experimental.pallas import tpu as pltpu

F32 = jnp.float32
BF16 = jnp.bfloat16

D_MODEL = 1024
D_FF = 2816
SG_WIDTH = 512
SG_GROUPS = 8
SG_GROUP_DIM = 64
SG_CHUNK = 128
DN_WIDTH = 512
DN_HEAD_DIM = 128
DN_HEADS = 4
DN_CHUNK = 64
CONV_K = 4
EPS = 1e-6
N_DEV = 8
LANES = 128
HALO = 8
MXU_COLS = 256

ADAM_LR = 0.001
ADAM_B1 = 0.9
ADAM_B2 = 0.999
ADAM_EPS = 1e-08
ADAM_WD = 0.01
ADAM_STEP = 10

VMEM_LIMIT = 60 * 1024 * 1024
WGRAD_K_TILE = 2048
TOKEN_BLOCK = 512
FF_BLOCK_FWD = 1408

_HI = lax.Precision.HIGHEST


def _cparams(sem):
    return pltpu.CompilerParams(dimension_semantics=sem, vmem_limit_bytes=VMEM_LIMIT)


def _tm(t, pref=TOKEN_BLOCK):
    return min(pref, t)


def _dg(a, b, ca, cb, precision):
    if precision is not None:
        return lax.dot_general(a, b, (((ca,), (cb,)), ((), ())), precision=precision, preferred_element_type=F32)
    return lax.dot_general(a.astype(BF16), b.astype(BF16), (((ca,), (cb,)), ((), ())), preferred_element_type=F32)


def _make_mm(precision):
    @jax.custom_vjp
    def mm(a, b):
        return _dg(a, b, 1, 0, precision)

    @jax.custom_vjp
    def mm_nt(a, b):
        return _dg(a, b, 1, 1, precision)

    @jax.custom_vjp
    def mm_tn(a, b):
        return _dg(a, b, 0, 0, precision)

    mm.defvjp(lambda a, b: (mm(a, b), (a, b)), lambda r, g: (mm_nt(g, r[1]), mm_tn(r[0], g)))
    mm_nt.defvjp(lambda a, b: (mm_nt(a, b), (a, b)), lambda r, g: (mm(g, r[1]), mm_tn(g, r[0])))
    mm_tn.defvjp(lambda a, b: (mm_tn(a, b), (a, b)), lambda r, g: (mm_nt(r[1], g), mm(r[0], g)))
    return mm, mm_nt, mm_tn


mm, mm_nt, mm_tn = _make_mm(None)
mmx, mmx_nt, mmx_tn = _make_mm(_HI)
mmh, mmh_nt, mmh_tn = _make_mm(lax.Precision.HIGH)


def _sigmoid(x):
    return 1.0 / (1.0 + jnp.exp(-x))


def _silu(x):
    return x * _sigmoid(x)


def _softplus(x):
    neg_abs = jnp.where(x > 0, -x, x)
    return jnp.where(x > 0, x, 0.0) + jnp.log(1.0 + jnp.exp(neg_abs))


def _gelu(x):
    return 0.5 * x * (1.0 + jnp.tanh(0.7978845608028654 * (x + 0.044715 * (x * x * x))))


def _rms_fwd(x, g):
    r = lax.rsqrt(jnp.mean(x * x, axis=-1, keepdims=True) + EPS)
    xh = x * r
    return xh * g, xh, r


def _rms_bwd(dh, xh, r, g):
    dxh = dh * g
    dx = r * (dxh - xh * jnp.mean(dxh * xh, axis=-1, keepdims=True))
    return dx, jnp.sum(dh * xh, axis=0, keepdims=True)


def _acc_out(ref, first, val):
    @pl.when(first)
    def _():
        ref[...] = val

    @pl.when(jnp.logical_not(first))
    def _():
        ref[...] += val


def _ffn_fwd(x, nw, wg, wu, wd, tgt=None, fnw=None, *, name):
    t = x.shape[0]
    tm, fb = _tm(t), FF_BLOCK_FWD
    n_t, n_f = t // tm, D_FF // fb
    with_loss = tgt is not None

    def body(*refs):
        if with_loss:
            (x_ref, nw_ref, wg_ref, wu_ref, wd_ref, tgt_ref, fnw_ref, dy_ref, loss_ref, dfn_ref, h_ref, g_ref, u_ref,
             acc_s) = refs
        else:
            x_ref, nw_ref, wg_ref, wu_ref, wd_ref, y_ref, h_ref, g_ref, u_ref, acc_s = refs
        i, j = pl.program_id(0), pl.program_id(1)

        @pl.when(j == 0)
        def _():
            h, _, _ = _rms_fwd(x_ref[...], nw_ref[...])
            h_ref[...] = h.astype(BF16)
            acc_s[...] = jnp.zeros_like(acc_s)

        h = h_ref[...]
        nt = (((1,), (1,)), ((), ()))
        g = lax.dot_general(h, wg_ref[...], nt, preferred_element_type=F32)
        u = lax.dot_general(h, wu_ref[...], nt, preferred_element_type=F32)
        g_ref[...] = g.astype(BF16)
        u_ref[...] = u.astype(BF16)
        a = _silu(g) * u
        acc_s[...] += jnp.dot(a.astype(BF16), wd_ref[...], preferred_element_type=F32)

        @pl.when(j == n_f - 1)
        def _():
            y = x_ref[...] + 0.5 * acc_s[...]
            if not with_loss:
                y_ref[...] = y
            else:
                gf = fnw_ref[...]
                out, xh, r = _rms_fwd(y, gf)
                err = out - tgt_ref[...]
                part = 0.5 * jnp.sum(jnp.mean(err * err, axis=-1, keepdims=True), axis=0, keepdims=True)
                d_out = err * (1.0 / D_MODEL)
                dy, dgf = _rms_bwd(d_out, xh, r, gf)
                dy_ref[...] = dy
                _acc_out(loss_ref, i == 0, jnp.broadcast_to(part, loss_ref.shape))
                _acc_out(dfn_ref, i == 0, dgf)

    row = lambda i, j: (i, 0)
    const = lambda i, j: (0, 0)
    in_specs = [
        pl.BlockSpec((tm, D_MODEL), row),
        pl.BlockSpec((1, D_MODEL), const),
        pl.BlockSpec((fb, D_MODEL), lambda i, j: (j, 0)),
        pl.BlockSpec((fb, D_MODEL), lambda i, j: (j, 0)),
        pl.BlockSpec((fb, D_MODEL), lambda i, j: (j, 0)),
    ]
    args = [x, nw, wg, wu, wd]
    saved_shape = (jax.ShapeDtypeStruct((t, D_MODEL), BF16), jax.ShapeDtypeStruct((t, D_FF), BF16),
                   jax.ShapeDtypeStruct((t, D_FF), BF16))
    saved_specs = (pl.BlockSpec((tm, D_MODEL), row), pl.BlockSpec((tm, fb), lambda i, j: (i, j)),
                   pl.BlockSpec((tm, fb), lambda i, j: (i, j)))
    if with_loss:
        in_specs += [pl.BlockSpec((tm, D_MODEL), row), pl.BlockSpec((1, D_MODEL), const)]
        args += [tgt, fnw]
        out_shape = (jax.ShapeDtypeStruct((t, D_MODEL), F32), jax.ShapeDtypeStruct((8, LANES), F32),
                     jax.ShapeDtypeStruct((1, D_MODEL), F32)) + saved_shape
        out_specs = (pl.BlockSpec((tm, D_MODEL), row), pl.BlockSpec((8, LANES), const),
                     pl.BlockSpec((1, D_MODEL), const)) + saved_specs
        sem = ("arbitrary", "arbitrary")
    else:
        out_shape = (jax.ShapeDtypeStruct((t, D_MODEL), F32),) + saved_shape
        out_specs = (pl.BlockSpec((tm, D_MODEL), row),) + saved_specs
        sem = ("parallel", "arbitrary")
    return pl.pallas_call(
        body, name=name, grid=(n_t, n_f), in_specs=in_specs, out_specs=out_specs, out_shape=out_shape,
        scratch_shapes=[pltpu.VMEM((tm, D_MODEL), F32)],
        compiler_params=_cparams(sem),
    )(*args)


def _ffn_bwd_x(x, nw, g, u, wg, wu, wd, dy, *, name):
    t = x.shape[0]
    tm = _tm(t, 256)

    def body(x_ref, nw_ref, g_ref, u_ref, wg_ref, wu_ref, wd_ref, dy_ref, dx_ref, dnw_ref, dg_ref, du_ref, a_ref, dyh_ref):
        i = pl.program_id(0)
        nt = (((1,), (1,)), ((), ()))
        dy = dy_ref[...]
        dyh = (0.5 * dy).astype(BF16)
        dyh_ref[...] = dyh
        gate, up = g_ref[...].astype(F32), u_ref[...].astype(F32)
        s = _sigmoid(gate)
        gs = gate * s
        da = lax.dot_general(dyh, wd_ref[...], nt, preferred_element_type=F32)
        dg = (da * up * (s + gs * (1.0 - s))).astype(BF16)
        du = (da * gs).astype(BF16)
        dg_ref[...] = dg
        du_ref[...] = du
        a_ref[...] = (gs * up).astype(BF16)
        dh = (jnp.dot(dg, wg_ref[...], preferred_element_type=F32)
              + jnp.dot(du, wu_ref[...], preferred_element_type=F32))
        xv = x_ref[...]
        r = lax.rsqrt(jnp.mean(xv * xv, axis=-1, keepdims=True) + EPS)
        dx, dnw = _rms_bwd(dh, xv * r, r, nw_ref[...])
        dx_ref[...] = dy + dx
        _acc_out(dnw_ref, i == 0, dnw)

    row = lambda i: (i, 0)
    const = lambda i: (0, 0)
    once = pl.Buffered(1)
    wide = pl.BlockSpec((tm, D_FF), row)
    return pl.pallas_call(
        body, name=name, grid=(t // tm,),
        in_specs=[pl.BlockSpec((tm, D_MODEL), row), pl.BlockSpec((1, D_MODEL), const), wide, wide,
                  pl.BlockSpec((D_FF, D_MODEL), const, pipeline_mode=once), pl.BlockSpec((D_FF, D_MODEL), const, pipeline_mode=once),
                  pl.BlockSpec((D_FF, D_MODEL), const, pipeline_mode=once), pl.BlockSpec((tm, D_MODEL), row)],
        out_specs=(pl.BlockSpec((tm, D_MODEL), row), pl.BlockSpec((1, D_MODEL), const), wide, wide, wide,
                   pl.BlockSpec((tm, D_MODEL), row)),
        out_shape=(jax.ShapeDtypeStruct((t, D_MODEL), F32), jax.ShapeDtypeStruct((1, D_MODEL), F32),
                   jax.ShapeDtypeStruct((t, D_FF), BF16), jax.ShapeDtypeStruct((t, D_FF), BF16),
                   jax.ShapeDtypeStruct((t, D_FF), BF16), jax.ShapeDtypeStruct((t, D_MODEL), BF16)),
        compiler_params=_cparams(("arbitrary",)),
    )(x, nw, g, u, wg, wu, wd, dy)


def _wgrad(a, b, bm, bn, after=None, *, name):
    k, m = a.shape
    n = b.shape[1]
    tk = _tm(k, WGRAD_K_TILE)
    n_k = k // tk

    def body(a_ref, b_ref, *rest):
        o_ref, acc_s = rest[-2], rest[-1]
        s = pl.program_id(2)
        for c in range(bn // MXU_COLS):
            cols = slice(c * MXU_COLS, (c + 1) * MXU_COLS)
            part = lax.dot_general(a_ref[...], b_ref[:, cols], (((0,), (0,)), ((), ())), preferred_element_type=F32)
            acc_s[:, cols] = jnp.where(s == 0, 0.0, acc_s[:, cols]) + part

        @pl.when(s == n_k - 1)
        def _():
            o_ref[...] = acc_s[...].astype(BF16)

    return pl.pallas_call(
        body, name=name, grid=(m // bm, n // bn, n_k),
        in_specs=[pl.BlockSpec((tk, bm), lambda i, j, s: (s, i)), pl.BlockSpec((tk, bn), lambda i, j, s: (s, j))]
        + ([] if after is None else [_HBM]),
        out_specs=pl.BlockSpec((bm, bn), lambda i, j, s: (i, j)),
        out_shape=jax.ShapeDtypeStruct((m, n), BF16),
        scratch_shapes=[pltpu.VMEM((bm, bn), F32)],
        compiler_params=_cparams(("parallel", "parallel", "arbitrary")),
    )(a, b, *([] if after is None else [after]))


def _ffn_wgrads(h, dg, du, a, dyh, between=None, after=None, *, name):
    grads = []
    for k, (lhs, rhs, tag) in enumerate(((dg, h, "_wg"), (du, h, "_wu"), (a, dyh, "_wd"))):
        grads.append(_wgrad(lhs, rhs, D_FF // 2, D_MODEL, after, name=name + tag))
        after = None if between is None else between(k, grads[-1])
    return grads


def _ffn_bwd(x, nw, h, g, u, wg, wu, wd, dy, *, name):
    dx, dnw, dg, du, a, dyh = _ffn_bwd_x(x, nw, g, u, wg, wu, wd, dy, name=name + "_x")
    return (dx, dnw, *_ffn_wgrads(h, dg, du, a, dyh, name=name))


_PROJ_WIDTHS = (SG_WIDTH, SG_WIDTH, 3 * DN_WIDTH, DN_WIDTH, LANES, LANES)


def _mix_in_fwd(x, nw, ws, *, name):
    t = x.shape[0]
    tm = _tm(t)

    def body(x_ref, nw_ref, *refs):
        w_refs, o_refs = refs[:6], refs[6:]
        h, _, _ = _rms_fwd(x_ref[...], nw_ref[...])
        h = h.astype(BF16)
        for w_ref, o_ref in zip(w_refs, o_refs):
            o_ref[...] = lax.dot_general(h, w_ref[...], (((1,), (1,)), ((), ())), preferred_element_type=F32)

    row = lambda i: (i, 0)
    const = lambda i: (0, 0)
    return pl.pallas_call(
        body, name=name, grid=(t // tm,),
        in_specs=[pl.BlockSpec((tm, D_MODEL), row), pl.BlockSpec((1, D_MODEL), const)]
        + [pl.BlockSpec((n, D_MODEL), const) for n in _PROJ_WIDTHS],
        out_specs=tuple(pl.BlockSpec((tm, n), row) for n in _PROJ_WIDTHS),
        out_shape=tuple(jax.ShapeDtypeStruct((t, n), F32) for n in _PROJ_WIDTHS),
        compiler_params=_cparams(("parallel",)),
    )(x, nw, *ws)


_PROJ_TOTAL = sum(_PROJ_WIDTHS)
_PROJ_OFFSETS = tuple(sum(_PROJ_WIDTHS[:k]) for k in range(len(_PROJ_WIDTHS)))


def _mix_in_bwd(x, nw, ws, dres, dps, *, name):
    t = x.shape[0]
    tm = _tm(t, 256)

    def body(x_ref, nw_ref, dres_ref, *refs):
        w_refs, dp_refs, dx_ref, dnw_ref, h_ref, dpb_ref = refs[:6], refs[6:12], refs[12], refs[13], refs[14], refs[15]
        i = pl.program_id(0)
        hf, xh, r = _rms_fwd(x_ref[...], nw_ref[...])
        h_ref[...] = hf.astype(BF16)
        dh = jnp.zeros((tm, D_MODEL), F32)
        for w_ref, dp_ref, off, width in zip(w_refs, dp_refs, _PROJ_OFFSETS, _PROJ_WIDTHS):
            dp = dp_ref[...].astype(BF16)
            dpb_ref[:, off:off + width] = dp
            dh = dh + jnp.dot(dp, w_ref[...], preferred_element_type=F32)
        dx, dnw = _rms_bwd(dh, xh, r, nw_ref[...])
        dx_ref[...] = dres_ref[...] + dx
        _acc_out(dnw_ref, i == 0, dnw)

    row = lambda i: (i, 0)
    const = lambda i: (0, 0)
    dx, dnw, h, dpb = pl.pallas_call(
        body, name=name + "_x", grid=(t // tm,),
        in_specs=[pl.BlockSpec((tm, D_MODEL), row), pl.BlockSpec((1, D_MODEL), const), pl.BlockSpec((tm, D_MODEL), row)]
        + [pl.BlockSpec((n, D_MODEL), const) for n in _PROJ_WIDTHS]
        + [pl.BlockSpec((tm, n), row) for n in _PROJ_WIDTHS],
        out_specs=(pl.BlockSpec((tm, D_MODEL), row), pl.BlockSpec((1, D_MODEL), const), pl.BlockSpec((tm, D_MODEL), row),
                   pl.BlockSpec((tm, _PROJ_TOTAL), row)),
        out_shape=(jax.ShapeDtypeStruct((t, D_MODEL), F32), jax.ShapeDtypeStruct((1, D_MODEL), F32),
                   jax.ShapeDtypeStruct((t, D_MODEL), BF16), jax.ShapeDtypeStruct((t, _PROJ_TOTAL), BF16)),
        compiler_params=_cparams(("arbitrary",)),
    )(x, nw, dres, *ws, *dps)
    return dx, dnw, _wgrad(dpb, h, _PROJ_TOTAL // 2, D_MODEL, name=name + "_w")


def _sg_fn(u, v, lng, lnb, wcs, sgbt):
    lane = lax.broadcasted_iota(jnp.int32, (1, SG_WIDTH), 1)
    lane_b = lax.broadcasted_iota(jnp.int32, (1, LANES), 1)
    rr = lax.broadcasted_iota(jnp.int32, (SG_CHUNK, SG_CHUNK), 0)
    cc = lax.broadcasted_iota(jnp.int32, (SG_CHUNK, SG_CHUNK), 1)
    gu, gv = _gelu(u), _gelu(v)
    mu = jnp.mean(gv, axis=-1, keepdims=True)
    cen = gv - mu
    var = jnp.mean(cen * cen, axis=-1, keepdims=True)
    ln = cen * lax.rsqrt(var + EPS) * lng + lnb
    vs = jnp.zeros_like(u)
    for g in range(SG_GROUPS):
        in_group = jnp.logical_and(lane >= g * SG_GROUP_DIM, lane < (g + 1) * SG_GROUP_DIM)
        w_causal = jnp.where(rr >= cc, wcs[g], 0.0)
        bias = jnp.sum(jnp.where(lane_b == g, sgbt, 0.0), axis=1, keepdims=True)
        vs = vs + jnp.where(in_group, mm(w_causal, ln) + bias, 0.0)
    return gu * vs


def _sg_fwd(u, v, lng, lnb, wc, sgbt, *, name):
    t = u.shape[0]
    tm = _tm(t)

    def body(u_ref, v_ref, lng_ref, lnb_ref, wc_ref, sgbt_ref, o_ref):
        wcs = [wc_ref[g] for g in range(SG_GROUPS)]
        for c in range(tm // SG_CHUNK):
            rows = pl.ds(c * SG_CHUNK, SG_CHUNK)
            o_ref[rows, :] = _sg_fn(u_ref[rows, :], v_ref[rows, :], lng_ref[...], lnb_ref[...], wcs, sgbt_ref[...])

    row = lambda i: (i, 0)
    const = lambda i: (0, 0)
    return pl.pallas_call(
        body, name=name, grid=(t // tm,),
        in_specs=[pl.BlockSpec((tm, SG_WIDTH), row), pl.BlockSpec((tm, SG_WIDTH), row),
                  pl.BlockSpec((1, SG_WIDTH), const), pl.BlockSpec((1, SG_WIDTH), const),
                  pl.BlockSpec((SG_GROUPS, SG_CHUNK, SG_CHUNK), lambda i: (0, 0, 0)), pl.BlockSpec((SG_CHUNK, LANES), const)],
        out_specs=pl.BlockSpec((tm, SG_WIDTH), row),
        out_shape=jax.ShapeDtypeStruct((t, SG_WIDTH), F32),
        compiler_params=_cparams(("parallel",)),
    )(u, v, lng, lnb, wc, sgbt)


def _sg_bwd(u, v, lng, lnb, wc, sgbt, dout, *, name):
    t = u.shape[0]
    tm = _tm(t)

    def body(u_ref, v_ref, lng_ref, lnb_ref, wc_ref, sgbt_ref, do_ref, du_ref, dv_ref, dlng_ref, dlnb_ref, dwc_ref, dsgbt_ref):
        i = pl.program_id(0)
        wcs = [wc_ref[g] for g in range(SG_GROUPS)]
        tot = None
        for c in range(tm // SG_CHUNK):
            rows = pl.ds(c * SG_CHUNK, SG_CHUNK)
            _, vjp = jax.vjp(_sg_fn, u_ref[rows, :], v_ref[rows, :], lng_ref[...], lnb_ref[...], wcs, sgbt_ref[...])
            du, dv, dlng, dlnb, dwcs, dsgbt = vjp(do_ref[rows, :])
            du_ref[rows, :] = du.astype(BF16)
            dv_ref[rows, :] = dv.astype(BF16)
            part = (dlng, dlnb, dwcs, dsgbt)
            tot = part if tot is None else jax.tree.map(jnp.add, tot, part)
        dlng, dlnb, dwcs, dsgbt = tot
        _acc_out(dlng_ref, i == 0, dlng)
        _acc_out(dlnb_ref, i == 0, dlnb)
        _acc_out(dsgbt_ref, i == 0, dsgbt)
        for g in range(SG_GROUPS):
            @pl.when(i == 0)
            def _(g=g):
                dwc_ref[g] = dwcs[g]

            @pl.when(i > 0)
            def _(g=g):
                dwc_ref[g] += dwcs[g]

    row = lambda i: (i, 0)
    const = lambda i: (0, 0)
    wspec = pl.BlockSpec((SG_GROUPS, SG_CHUNK, SG_CHUNK), lambda i: (0, 0, 0))
    return pl.pallas_call(
        body, name=name, grid=(t // tm,),
        in_specs=[pl.BlockSpec((tm, SG_WIDTH), row), pl.BlockSpec((tm, SG_WIDTH), row),
                  pl.BlockSpec((1, SG_WIDTH), const), pl.BlockSpec((1, SG_WIDTH), const), wspec,
                  pl.BlockSpec((SG_CHUNK, LANES), const), pl.BlockSpec((tm, SG_WIDTH), row)],
        out_specs=(pl.BlockSpec((tm, SG_WIDTH), row), pl.BlockSpec((tm, SG_WIDTH), row),
                   pl.BlockSpec((1, SG_WIDTH), const), pl.BlockSpec((1, SG_WIDTH), const), wspec,
                   pl.BlockSpec((SG_CHUNK, LANES), const)),
        out_shape=(jax.ShapeDtypeStruct((t, SG_WIDTH), BF16), jax.ShapeDtypeStruct((t, SG_WIDTH), BF16),
                   jax.ShapeDtypeStruct((1, SG_WIDTH), F32), jax.ShapeDtypeStruct((1, SG_WIDTH), F32),
                   jax.ShapeDtypeStruct((SG_GROUPS, SG_CHUNK, SG_CHUNK), F32), jax.ShapeDtypeStruct((SG_CHUNK, LANES), F32)),
        compiler_params=_cparams(("arbitrary",)),
    )(u, v, lng, lnb, wc, sgbt, dout)


def _conv_taps(ext, w, tm):
    y = None
    for j in range(CONV_K):
        s = CONV_K - 1 - j
        shifted = ext if s == 0 else pltpu.roll(ext, s, 0)
        term = w[j:j + 1, :] * shifted[HALO:HALO + tm, :]
        y = term if y is None else y + term
    return y


def _post_conv(yq, yk, yv, bpre, apre, alog, dtb):
    def l2(a):
        return a * lax.rsqrt(jnp.sum(a * a, axis=-1, keepdims=True) + EPS)

    q = [l2(_silu(a)) for a in yq]
    k = [l2(_silu(a)) for a in yk]
    return q, k, _silu(yv), _sigmoid(bpre), -jnp.exp(alog) * _softplus(apre + dtb)


def _chunk_tril(tm):
    rr = lax.broadcasted_iota(jnp.int32, (tm, tm), 0)
    cc = lax.broadcasted_iota(jnp.int32, (tm, tm), 1)
    shift = DN_CHUNK.bit_length() - 1
    same = jnp.right_shift(rr, shift) == jnp.right_shift(cc, shift)
    return jnp.where(jnp.logical_and(same, rr >= cc), 1.0, 0.0).astype(F32)


def _halo_specs(tm, width, n_blocks_seq, n_blocks):
    per = tm // HALO
    prev = pl.BlockSpec((HALO, width), lambda i: (jnp.maximum(i * per - 1, 0), 0))
    nxt = pl.BlockSpec((HALO, width), lambda i: (jnp.minimum((i + 1) * per, n_blocks * per - 1), 0))
    return prev, nxt


def _split_heads(ref, base):
    return [ref[:, base + h * DN_HEAD_DIM: base + (h + 1) * DN_HEAD_DIM] for h in range(DN_HEADS)]


def _dn_prep_fwd(qkv, bpre, apre, conv_w, alog, dtb, seq, *, name):
    t = qkv.shape[0]
    tm = _tm(t)
    bps = seq // tm
    cw = 3 * DN_WIDTH

    def body(x_ref, halo_ref, b_ref, a_ref, w_ref, alog_ref, dtb_ref, q_ref, k_ref, v_ref, beta_ref, gc_ref):
        i = pl.program_id(0)
        keep = jnp.where(i % bps == 0, 0.0, 1.0)
        ext = jnp.concatenate([halo_ref[...] * keep, x_ref[...]], axis=0)
        y = _conv_taps(ext, w_ref[...], tm)
        yq = [y[:, h * DN_HEAD_DIM:(h + 1) * DN_HEAD_DIM] for h in range(DN_HEADS)]
        yk = [y[:, DN_WIDTH + h * DN_HEAD_DIM: DN_WIDTH + (h + 1) * DN_HEAD_DIM] for h in range(DN_HEADS)]
        q, k, v, beta, g = _post_conv(yq, yk, y[:, 2 * DN_WIDTH:], b_ref[...], a_ref[...], alog_ref[...], dtb_ref[...])
        for h in range(DN_HEADS):
            q_ref[:, h * DN_HEAD_DIM:(h + 1) * DN_HEAD_DIM] = q[h]
            k_ref[:, h * DN_HEAD_DIM:(h + 1) * DN_HEAD_DIM] = k[h]
        v_ref[...] = v
        beta_ref[...] = beta
        gc_ref[...] = mmx(_chunk_tril(tm), g)

    row = lambda i: (i, 0)
    const = lambda i: (0, 0)
    prev, _ = _halo_specs(tm, cw, bps, t // tm)
    return pl.pallas_call(
        body, name=name, grid=(t // tm,),
        in_specs=[pl.BlockSpec((tm, cw), row), prev, pl.BlockSpec((tm, LANES), row), pl.BlockSpec((tm, LANES), row),
                  pl.BlockSpec((CONV_K, cw), const), pl.BlockSpec((1, LANES), const), pl.BlockSpec((1, LANES), const)],
        out_specs=tuple(pl.BlockSpec((tm, n), row) for n in (DN_WIDTH, DN_WIDTH, DN_WIDTH, LANES, LANES)),
        out_shape=tuple(jax.ShapeDtypeStruct((t, n), F32) for n in (DN_WIDTH, DN_WIDTH, DN_WIDTH, LANES, LANES)),
        compiler_params=_cparams(("parallel",)),
    )(qkv, qkv, bpre, apre, conv_w, alog, dtb)


def _y_heads(y):
    yq = [y[:, h * DN_HEAD_DIM:(h + 1) * DN_HEAD_DIM] for h in range(DN_HEADS)]
    yk = [y[:, DN_WIDTH + h * DN_HEAD_DIM: DN_WIDTH + (h + 1) * DN_HEAD_DIM] for h in range(DN_HEADS)]
    return yq, yk, y[:, 2 * DN_WIDTH:]


def _dn_prep_bwd(qkv, bpre, apre, conv_w, alog, dtb, dq, dk, dv, dbeta, dgc, dgc2, seq, *, name):
    t = qkv.shape[0]
    tm = _tm(t)
    bps = seq // tm
    cw = 3 * DN_WIDTH
    n_ext = tm + HALO

    def body(x_ref, halo_ref, xn_ref, b_ref, a_ref, w_ref, alog_ref, dtb_ref, dq_ref, dk_ref, dv_ref, dqn_ref, dkn_ref,
             dvn_ref, dbeta_ref, dgc_ref, dgc2_ref, dx_ref, dw_ref, db_ref, da_ref, dalog_ref, ddtb_ref):
        i = pl.program_id(0)
        keep_prev = jnp.where(i % bps == 0, 0.0, 1.0)
        keep_next = jnp.where(i % bps == bps - 1, 0.0, 1.0)
        w = w_ref[...]
        x = x_ref[...]
        ext = jnp.concatenate([halo_ref[...] * keep_prev, x], axis=0)
        yq, yk, yv = _y_heads(_conv_taps(ext, w, tm))
        _, vjp = jax.vjp(_post_conv, yq, yk, yv, b_ref[...], a_ref[...], alog_ref[...], dtb_ref[...])
        dg = mmx_tn(_chunk_tril(tm), dgc_ref[...] + dgc2_ref[...])
        dyq, dyk, dyv, db, da, dalog, ddtb = vjp((_split_heads(dq_ref, 0), _split_heads(dk_ref, 0), dv_ref[...],
                                                  dbeta_ref[...], dg))
        dy = jnp.concatenate(dyq + dyk + [dyv], axis=1)
        ext_n = jnp.concatenate([x[tm - HALO:, :], xn_ref[...]], axis=0)
        _, vjp_n = jax.vjp(lambda *ys: _post_conv(*ys, b_ref[:HALO, :], a_ref[:HALO, :], alog_ref[...], dtb_ref[...])[:3],
                           *_y_heads(_conv_taps(ext_n, w, HALO)))
        dyq_n, dyk_n, dyv_n = vjp_n((_split_heads(dqn_ref, 0), _split_heads(dkn_ref, 0), dvn_ref[...]))
        dyext = jnp.concatenate([dy, jnp.concatenate(dyq_n + dyk_n + [dyv_n], axis=1) * keep_next], axis=0)

        @pl.when(i == 0)
        def _():
            dw_ref[...] = jnp.zeros_like(dw_ref)

        dx = None
        for j in range(CONV_K):
            s = CONV_K - 1 - j
            fut = dyext if s == 0 else pltpu.roll(dyext, n_ext - s, 0)
            term = w[j:j + 1, :] * fut[0:tm, :]
            dx = term if dx is None else dx + term
            past = ext if s == 0 else pltpu.roll(ext, s, 0)
            dw_ref[j:j + 1, :] += jnp.sum(dy * past[HALO:HALO + tm, :], axis=0, keepdims=True)
        dx_ref[...] = dx.astype(BF16)
        db_ref[...] = db.astype(BF16)
        da_ref[...] = da.astype(BF16)
        _acc_out(dalog_ref, i == 0, dalog)
        _acc_out(ddtb_ref, i == 0, ddtb)

    row = lambda i: (i, 0)
    const = lambda i: (0, 0)
    prev, nxt = _halo_specs(tm, cw, bps, t // tm)
    _, nxt_h = _halo_specs(tm, DN_WIDTH, bps, t // tm)
    tok = pl.BlockSpec((tm, DN_WIDTH), row)
    lanes = pl.BlockSpec((tm, LANES), row)
    return pl.pallas_call(
        body, name=name, grid=(t // tm,),
        in_specs=[pl.BlockSpec((tm, cw), row), prev, nxt, lanes, lanes,
                  pl.BlockSpec((CONV_K, cw), const), pl.BlockSpec((1, LANES), const), pl.BlockSpec((1, LANES), const),
                  tok, tok, tok, nxt_h, nxt_h, nxt_h, lanes, lanes, lanes],
        out_specs=(pl.BlockSpec((tm, cw), row), pl.BlockSpec((HALO, cw), const), lanes, lanes,
                   pl.BlockSpec((1, LANES), const), pl.BlockSpec((1, LANES), const)),
        out_shape=(jax.ShapeDtypeStruct((t, cw), BF16), jax.ShapeDtypeStruct((HALO, cw), F32),
                   jax.ShapeDtypeStruct((t, LANES), BF16), jax.ShapeDtypeStruct((t, LANES), BF16),
                   jax.ShapeDtypeStruct((1, LANES), F32), jax.ShapeDtypeStruct((1, LANES), F32)),
        compiler_params=_cparams(("arbitrary",)),
    )(qkv, qkv, qkv, bpre, apre, conv_w, alog, dtb, dq, dk, dv, dq, dk, dv, dbeta, dgc, dgc2)


def _inv_unit_lower(l_mats, eye):
    invs = [eye - l for l in l_mats]
    powers = list(l_mats)
    n = 2
    while n < eye.shape[0]:
        powers = [mmh(p, p) for p in powers]
        invs = [inv + mmh(inv, p) for inv, p in zip(invs, powers)]
        n *= 2
    return invs


@jax.custom_vjp
def _solve(l_mat, rhs, inv):
    return mmh(inv, rhs)


def _solve_fwd(l_mat, rhs, inv):
    sol = mmh(inv, rhs)
    return sol, (inv, sol)


def _solve_bwd(res, d_sol):
    inv, sol = res
    d_rhs = mm_tn(inv, d_sol)
    return -mm_nt(d_rhs, sol), d_rhs, jnp.zeros_like(inv)


_solve.defvjp(_solve_fwd, _solve_bwd)


def _prep_fn(q, k, v, gc, gr, b, inv):
    ids = range(len(q))
    c = q[0].shape[0]
    rr = lax.broadcasted_iota(jnp.int32, (c, c), 0)
    cc = lax.broadcasted_iota(jnp.int32, (c, c), 1)
    incl, strict = rr >= cc, rr > cc
    is_last = lax.broadcasted_iota(jnp.int32, (c, 1), 0) == c - 1
    qs = [q[i] * (DN_HEAD_DIM ** -0.5) for i in ids]
    decay = [jnp.where(incl, jnp.exp(jnp.where(incl, gc[i] - gr[i], 0.0)), 0.0) for i in ids]
    kb = [k[i] * b[i] for i in ids]
    vb = [v[i] * b[i] for i in ids]
    kk = [mm_nt(kb[i], k[i]) for i in ids]
    l_mat = [jnp.where(strict, kk[i] * decay[i], 0.0) for i in ids]
    eg = [jnp.exp(gc[i]) for i in ids]
    if inv is None:
        inv = _inv_unit_lower(l_mat, jnp.where(rr == cc, 1.0, 0.0).astype(F32))
    u_wy = [_solve(l_mat[i], vb[i], inv[i]) for i in ids]
    w_wy = [_solve(l_mat[i], kb[i] * eg[i], inv[i]) for i in ids]
    qk = [mm_nt(qs[i], k[i]) * decay[i] for i in ids]
    g_last = [jnp.sum(jnp.where(is_last, gc[i], 0.0), axis=0, keepdims=True) for i in ids]
    k_dec = [k[i] * jnp.exp(g_last[i] - gc[i]) for i in ids]
    egl = [jnp.broadcast_to(jnp.exp(g_last[i]), (1, LANES)) for i in ids]
    return [(w_wy[i], u_wy[i], qs[i] * eg[i], k_dec[i], qk[i], egl[i]) for i in ids], inv


def _seq_fn(w, u, qd, kd, qk, egl, s):
    ids = range(len(w))
    ws = [mm(w[i], s[i]) for i in ids]
    qs = [mm(qd[i], s[i]) for i in ids]
    v_new = [u[i] - ws[i] for i in ids]
    o = [qs[i] + mm(qk[i], v_new[i]) for i in ids]
    s_new = [s[i] * egl[i] + mm_tn(kd[i], v_new[i]) for i in ids]
    return o, s_new


def _lane_col(a, h):
    lane = lax.broadcasted_iota(jnp.int32, (1, LANES), 1)
    return jnp.sum(jnp.where(lane == h, a, 0.0), axis=1, keepdims=True)


def _col_lane(col, h):
    lane = lax.broadcasted_iota(jnp.int32, (1, LANES), 1)
    return jnp.where(lane == h, col, 0.0)


def _head_cols(h):
    return slice(h * DN_HEAD_DIM, (h + 1) * DN_HEAD_DIM)


def _chunk_rows(n):
    return pl.ds(pl.multiple_of(n * DN_CHUNK, DN_CHUNK), DN_CHUNK)


def _delta_prep(q, k, v, gc, grow, beta, *, name):
    t = q.shape[0]
    tm = _tm(t)
    cpb = tm // DN_CHUNK
    n_chunks = t // DN_CHUNK
    group = 2

    def body(q_ref, k_ref, v_ref, gc_ref, gr_ref, b_ref, w_ref, u_ref, qd_ref, kd_ref, qk_ref, egl_ref, inv_ref):
        def step(m, carry):
            probs = [(m * group + e, h) for e in range(group) for h in range(DN_HEADS)]
            gcb = [gc_ref[_chunk_rows(m * group + e), :] for e in range(group)]
            bb = [b_ref[_chunk_rows(m * group + e), :] for e in range(group)]
            grb = [gr_ref[m * group + e] for e in range(group)]
            for e in range(group):
                egl_ref[m * group + e] = jnp.zeros((HALO, LANES), F32)
            outs, invs = _prep_fn(
                [q_ref[_chunk_rows(n), _head_cols(h)] for n, h in probs], [k_ref[_chunk_rows(n), _head_cols(h)] for n, h in probs],
                [v_ref[_chunk_rows(n), _head_cols(h)] for n, h in probs],
                [_lane_col(gcb[e], h) for e in range(group) for h in range(DN_HEADS)],
                [grb[e][h:h + 1, :] for e in range(group) for h in range(DN_HEADS)],
                [_lane_col(bb[e], h) for e in range(group) for h in range(DN_HEADS)], None)
            for (n, h), (w, u, qd, kd, qk, egl), inv in zip(probs, outs, invs):
                rows, cols = _chunk_rows(n), _head_cols(h)
                w_ref[rows, cols] = w.astype(BF16)
                u_ref[rows, cols] = u
                qd_ref[rows, cols] = qd.astype(BF16)
                kd_ref[rows, cols] = kd.astype(BF16)
                qk_ref[n, h] = qk
                inv_ref[n, h] = inv
                egl_ref[n, h:h + 1, :] = egl
            return carry

        lax.fori_loop(0, cpb // group, step, 0)

    row = lambda i: (i, 0)
    tok = pl.BlockSpec((tm, DN_WIDTH), row)
    lanes = pl.BlockSpec((tm, LANES), row)
    sq = pl.BlockSpec((cpb, DN_HEADS, DN_CHUNK, DN_CHUNK), lambda i: (i, 0, 0, 0))
    return pl.pallas_call(
        body, name=name, grid=(t // tm,),
        in_specs=[tok, tok, tok, lanes, pl.BlockSpec((cpb, HALO, DN_CHUNK), lambda i: (i, 0, 0)), lanes],
        out_specs=(tok, tok, tok, tok, sq, pl.BlockSpec((cpb, HALO, LANES), lambda i: (i, 0, 0)), sq),
        out_shape=(jax.ShapeDtypeStruct((t, DN_WIDTH), BF16), jax.ShapeDtypeStruct((t, DN_WIDTH), F32),
                   jax.ShapeDtypeStruct((t, DN_WIDTH), BF16), jax.ShapeDtypeStruct((t, DN_WIDTH), BF16),
                   jax.ShapeDtypeStruct((n_chunks, DN_HEADS, DN_CHUNK, DN_CHUNK), F32),
                   jax.ShapeDtypeStruct((n_chunks, HALO, LANES), F32),
                   jax.ShapeDtypeStruct((n_chunks, DN_HEADS, DN_CHUNK, DN_CHUNK), F32)),
        compiler_params=_cparams(("parallel",)),
    )(q, k, v, gc, grow, beta)


def _delta_par_bwd(q, k, v, gc, grow, beta, inv, dw, du, dqd, dkd, dqk, degl, *, name):
    t = q.shape[0]
    tm = _tm(t)
    cpb = tm // DN_CHUNK
    n_chunks = t // DN_CHUNK
    group = 2

    def body(q_ref, k_ref, v_ref, gc_ref, gr_ref, b_ref, inv_ref, dw_ref, du_ref, dqd_ref, dkd_ref, dqk_ref, degl_ref,
             dq_ref, dk_ref, dv_ref, dgc_ref, dgr_ref, db_ref):
        def step(m, carry):
            chunks = [m * group + e for e in range(group)]
            probs = [(e, h) for e in range(group) for h in range(DN_HEADS)]
            rows = [_chunk_rows(n) for n in chunks]
            gcb, bb = [gc_ref[r, :] for r in rows], [b_ref[r, :] for r in rows]
            grb, deglb = [gr_ref[n] for n in chunks], [degl_ref[n] for n in chunks]
            for n in chunks:
                dgr_ref[n] = jnp.zeros((HALO, DN_CHUNK), F32)
            invs = [inv_ref[chunks[e], h] for e, h in probs]
            _, vjp = jax.vjp(lambda *a: _prep_fn(*a, invs)[0],
                             [q_ref[rows[e], _head_cols(h)] for e, h in probs], [k_ref[rows[e], _head_cols(h)] for e, h in probs],
                             [v_ref[rows[e], _head_cols(h)] for e, h in probs], [_lane_col(gcb[e], h) for e, h in probs],
                             [grb[e][h:h + 1, :] for e, h in probs], [_lane_col(bb[e], h) for e, h in probs])
            dq, dk, dv, dgc, dgr, db = vjp([(dw_ref[rows[e], _head_cols(h)], du_ref[rows[e], _head_cols(h)],
                                             dqd_ref[rows[e], _head_cols(h)], dkd_ref[rows[e], _head_cols(h)],
                                             dqk_ref[chunks[e], h], deglb[e][h:h + 1, :]) for e, h in probs])
            dgc_acc = [jnp.zeros((DN_CHUNK, LANES), F32) for _ in chunks]
            db_acc = [jnp.zeros((DN_CHUNK, LANES), F32) for _ in chunks]
            for i, (e, h) in enumerate(probs):
                cols = _head_cols(h)
                dq_ref[rows[e], cols] = dq[i]
                dk_ref[rows[e], cols] = dk[i]
                dv_ref[rows[e], cols] = dv[i]
                dgr_ref[chunks[e], h:h + 1, :] = dgr[i]
                dgc_acc[e] = dgc_acc[e] + _col_lane(dgc[i], h)
                db_acc[e] = db_acc[e] + _col_lane(db[i], h)
            for e in range(group):
                dgc_ref[rows[e], :] = dgc_acc[e]
                db_ref[rows[e], :] = db_acc[e]
            return carry

        lax.fori_loop(0, cpb // group, step, 0)

    row = lambda i: (i, 0)
    tok = pl.BlockSpec((tm, DN_WIDTH), row)
    lanes = pl.BlockSpec((tm, LANES), row)
    sq = pl.BlockSpec((cpb, DN_HEADS, DN_CHUNK, DN_CHUNK), lambda i: (i, 0, 0, 0))
    grs = pl.BlockSpec((cpb, HALO, DN_CHUNK), lambda i: (i, 0, 0))
    return pl.pallas_call(
        body, name=name, grid=(t // tm,),
        in_specs=[tok, tok, tok, lanes, grs, lanes, sq, tok, tok, tok, tok, sq, pl.BlockSpec((cpb, HALO, LANES), lambda i: (i, 0, 0))],
        out_specs=(tok, tok, tok, lanes, grs, lanes),
        out_shape=(jax.ShapeDtypeStruct((t, DN_WIDTH), F32),) * 3
        + (jax.ShapeDtypeStruct((t, LANES), F32), jax.ShapeDtypeStruct((n_chunks, HALO, DN_CHUNK), F32),
           jax.ShapeDtypeStruct((t, LANES), F32)),
        compiler_params=_cparams(("parallel",)),
    )(q, k, v, gc, grow, beta, inv, dw, du, dqd, dkd, dqk, degl)


def _seq_specs(n_seq, seq, reverse):
    tm = _tm(seq)
    nb = seq // tm
    cpb = tm // DN_CHUNK
    pair = 2 if n_seq % 2 == 0 else 1
    blk = (lambda j: nb - 1 - j) if reverse else (lambda j: j)
    tok = pl.BlockSpec((pair, tm, DN_WIDTH), lambda b, j: (b, blk(j), 0))
    sq = pl.BlockSpec((pair, cpb, DN_HEADS, DN_CHUNK, DN_CHUNK), lambda b, j: (b, blk(j), 0, 0, 0))
    rows8 = pl.BlockSpec((pair, cpb, HALO, LANES), lambda b, j: (b, blk(j), 0, 0))
    state = pl.BlockSpec((pair, cpb, DN_HEADS, DN_HEAD_DIM, DN_HEAD_DIM), lambda b, j: (b, blk(j), 0, 0, 0))
    return nb, cpb, pair, tok, sq, rows8, state


def _by_seq(a, n_seq):
    return a.reshape((n_seq, a.shape[0] // n_seq) + a.shape[1:])


def _flat_seq(a):
    return a.reshape((a.shape[0] * a.shape[1],) + a.shape[2:])


def _delta_seq_fwd(w, u, qd, kd, qk, egl, n_seq, seq, *, name):
    nb, cpb, pair, tok, sq, rows8, state = _seq_specs(n_seq, seq, False)
    probs = [(e, h) for e in range(pair) for h in range(DN_HEADS)]

    def body(w_ref, u_ref, qd_ref, kd_ref, qk_ref, egl_ref, o_ref, st_ref, s_s):
        @pl.when(pl.program_id(1) == 0)
        def _():
            s_s[...] = jnp.zeros_like(s_s)

        def step(n, carry):
            rows = _chunk_rows(n)
            eglb = [egl_ref[e, n] for e in range(pair)]
            s = [s_s[e, h] for e, h in probs]
            for (e, h), s_eh in zip(probs, s):
                st_ref[e, n, h] = s_eh
            o, s_new = _seq_fn([w_ref[e, rows, _head_cols(h)] for e, h in probs], [u_ref[e, rows, _head_cols(h)] for e, h in probs],
                               [qd_ref[e, rows, _head_cols(h)] for e, h in probs], [kd_ref[e, rows, _head_cols(h)] for e, h in probs],
                               [qk_ref[e, n, h] for e, h in probs], [eglb[e][h:h + 1, :] for e, h in probs], s)
            for i, (e, h) in enumerate(probs):
                o_ref[e, rows, _head_cols(h)] = o[i]
                s_s[e, h] = s_new[i]
            return carry

        lax.fori_loop(0, cpb, step, 0)

    o, states = pl.pallas_call(
        body, name=name, grid=(n_seq // pair, nb),
        in_specs=[tok, tok, tok, tok, sq, rows8],
        out_specs=(tok, state),
        out_shape=(jax.ShapeDtypeStruct((n_seq, seq, DN_WIDTH), F32),
                   jax.ShapeDtypeStruct((n_seq, seq // DN_CHUNK, DN_HEADS, DN_HEAD_DIM, DN_HEAD_DIM), F32)),
        scratch_shapes=[pltpu.VMEM((pair, DN_HEADS, DN_HEAD_DIM, DN_HEAD_DIM), F32)],
        compiler_params=_cparams(("parallel", "arbitrary")),
    )(*[_by_seq(a, n_seq) for a in (w, u, qd, kd, qk, egl)])
    return _flat_seq(o), _flat_seq(states)


def _delta_seq_bwd(w, u, qd, kd, qk, egl, states, do, n_seq, seq, *, name):
    nb, cpb, pair, tok, sq, rows8, state = _seq_specs(n_seq, seq, True)
    probs = [(e, h) for e in range(pair) for h in range(DN_HEADS)]

    def body(w_ref, u_ref, qd_ref, kd_ref, qk_ref, egl_ref, st_ref, do_ref, dw_ref, du_ref, dqd_ref, dkd_ref, dqk_ref,
             degl_ref, ds_s):
        @pl.when(pl.program_id(1) == 0)
        def _():
            ds_s[...] = jnp.zeros_like(ds_s)

        def step(m, carry):
            n = cpb - 1 - m
            rows = _chunk_rows(n)
            eglb = [egl_ref[e, n] for e in range(pair)]
            for e in range(pair):
                degl_ref[e, n] = jnp.zeros((HALO, LANES), F32)
            _, vjp = jax.vjp(_seq_fn, [w_ref[e, rows, _head_cols(h)].astype(F32) for e, h in probs],
                             [u_ref[e, rows, _head_cols(h)] for e, h in probs],
                             [qd_ref[e, rows, _head_cols(h)].astype(F32) for e, h in probs],
                             [kd_ref[e, rows, _head_cols(h)].astype(F32) for e, h in probs],
                             [qk_ref[e, n, h] for e, h in probs], [eglb[e][h:h + 1, :] for e, h in probs],
                             [st_ref[e, n, h] for e, h in probs])
            dw, du, dqd, dkd, dqk, degl, ds_in = vjp(([do_ref[e, rows, _head_cols(h)] for e, h in probs],
                                                      [ds_s[e, h] for e, h in probs]))
            for i, (e, h) in enumerate(probs):
                cols = _head_cols(h)
                dw_ref[e, rows, cols] = dw[i]
                du_ref[e, rows, cols] = du[i]
                dqd_ref[e, rows, cols] = dqd[i]
                dkd_ref[e, rows, cols] = dkd[i]
                dqk_ref[e, n, h] = dqk[i]
                degl_ref[e, n, h:h + 1, :] = degl[i]
                ds_s[e, h] = ds_in[i]
            return carry

        lax.fori_loop(0, cpb, step, 0)

    nc = seq // DN_CHUNK
    outs = pl.pallas_call(
        body, name=name, grid=(n_seq // pair, nb),
        in_specs=[tok, tok, tok, tok, sq, rows8, state, tok],
        out_specs=(tok, tok, tok, tok, sq, rows8),
        out_shape=(jax.ShapeDtypeStruct((n_seq, seq, DN_WIDTH), F32),) * 4
        + (jax.ShapeDtypeStruct((n_seq, nc, DN_HEADS, DN_CHUNK, DN_CHUNK), F32),
           jax.ShapeDtypeStruct((n_seq, nc, HALO, LANES), F32)),
        scratch_shapes=[pltpu.VMEM((pair, DN_HEADS, DN_HEAD_DIM, DN_HEAD_DIM), F32)],
        compiler_params=_cparams(("parallel", "arbitrary")),
    )(*[_by_seq(a, n_seq) for a in (w, u, qd, kd, qk, egl, states, do)])
    return tuple(_flat_seq(a) for a in outs)


def _dn_gate(o, z, dnw):
    return o * lax.rsqrt(jnp.mean(o * o, axis=-1, keepdims=True) + EPS) * dnw * _silu(z)


def _mix_out_fwd(x, sg, o, z, wo_sg, wo_dn, dnw, *, name):
    t = x.shape[0]
    tm = _tm(t)

    def body(x_ref, sg_ref, o_ref, z_ref, wsg_ref, wdn_ref, dnw_ref, y_ref, dn_s):
        for h, (oh, zh) in enumerate(zip(_split_heads(o_ref, 0), _split_heads(z_ref, 0))):
            dn_s[:, h * DN_HEAD_DIM:(h + 1) * DN_HEAD_DIM] = _dn_gate(oh, zh, dnw_ref[...]).astype(BF16)
        y_ref[...] = (x_ref[...] + jnp.dot(sg_ref[...].astype(BF16), wsg_ref[...], preferred_element_type=F32)
                      + jnp.dot(dn_s[...], wdn_ref[...], preferred_element_type=F32))

    row = lambda i: (i, 0)
    const = lambda i: (0, 0)
    half = pl.BlockSpec((tm, DN_WIDTH), row)
    return pl.pallas_call(
        body, name=name, grid=(t // tm,),
        in_specs=[pl.BlockSpec((tm, D_MODEL), row), half, half, half, pl.BlockSpec((SG_WIDTH, D_MODEL), const),
                  pl.BlockSpec((DN_WIDTH, D_MODEL), const), pl.BlockSpec((1, DN_HEAD_DIM), const)],
        out_specs=pl.BlockSpec((tm, D_MODEL), row),
        out_shape=jax.ShapeDtypeStruct((t, D_MODEL), F32),
        scratch_shapes=[pltpu.VMEM((tm, DN_WIDTH), BF16)],
        compiler_params=_cparams(("parallel",)),
    )(x, sg, o, z, wo_sg, wo_dn, dnw)


def _mix_out_bwd(dy, sg, o, z, wo_sg, wo_dn, dnw, *, name):
    t = dy.shape[0]
    tm = _tm(t)

    def body(dy_ref, sg_ref, o_ref, z_ref, wsg_ref, wdn_ref, dnw_ref, dsg_ref, do_ref, dz_ref, dwsg_ref, dwdn_ref, ddnw_ref, dn_s):
        i = pl.program_id(0)
        dyb = dy_ref[...].astype(BF16)
        nt = (((1,), (1,)), ((), ()))
        tn = (((0,), (0,)), ((), ()))
        dsg_ref[...] = lax.dot_general(dyb, wsg_ref[...], nt, preferred_element_type=F32)
        ddn = lax.dot_general(dyb, wdn_ref[...], nt, preferred_element_type=F32)
        ddnw = None
        for h, (oh, zh) in enumerate(zip(_split_heads(o_ref, 0), _split_heads(z_ref, 0))):
            cols = slice(h * DN_HEAD_DIM, (h + 1) * DN_HEAD_DIM)
            out, vjp = jax.vjp(_dn_gate, oh, zh, dnw_ref[...])
            dn_s[:, cols] = out.astype(BF16)
            doh, dzh, dw = vjp(ddn[:, cols])
            do_ref[:, cols] = doh
            dz_ref[:, cols] = dzh.astype(BF16)
            ddnw = dw if ddnw is None else ddnw + dw
        _acc_out(ddnw_ref, i == 0, ddnw)
        _acc_out(dwsg_ref, i == 0, lax.dot_general(sg_ref[...].astype(BF16), dyb, tn, preferred_element_type=F32))
        _acc_out(dwdn_ref, i == 0, lax.dot_general(dn_s[...], dyb, tn, preferred_element_type=F32))

    row = lambda i: (i, 0)
    const = lambda i: (0, 0)
    half = pl.BlockSpec((tm, DN_WIDTH), row)
    wspec = pl.BlockSpec((DN_WIDTH, D_MODEL), const)
    return pl.pallas_call(
        body, name=name, grid=(t // tm,),
        in_specs=[pl.BlockSpec((tm, D_MODEL), row), half, half, half, wspec, wspec, pl.BlockSpec((1, DN_HEAD_DIM), const)],
        out_specs=(half, half, half, wspec, wspec, pl.BlockSpec((1, DN_HEAD_DIM), const)),
        out_shape=(jax.ShapeDtypeStruct((t, DN_WIDTH), F32),) * 2 + (jax.ShapeDtypeStruct((t, DN_WIDTH), BF16),)
        + (jax.ShapeDtypeStruct((DN_WIDTH, D_MODEL), F32),) * 2 + (jax.ShapeDtypeStruct((1, DN_HEAD_DIM), F32),),
        scratch_shapes=[pltpu.VMEM((tm, DN_WIDTH), BF16)],
        compiler_params=_cparams(("arbitrary",)),
    )(dy, sg, o, z, wo_sg, wo_dn, dnw)


_MESH = pl.DeviceIdType.MESH
_HBM = pl.BlockSpec(memory_space=pl.ANY)


def _mesh_pos():
    x, y, c = lax.axis_index("x"), lax.axis_index("y"), lax.axis_index("c")
    return x, y, c, [(1 - x, y), (x, 1 - y), (1 - x, 1 - y)]


def _gather2(arrs, *, name):
    n = len(arrs)
    slots = N_DEV - 1

    def body(*refs):
        in_refs, out_refs = refs[:n], refs[n:2 * n]
        send_sems, recv_sems, local_sems = refs[2 * n:]
        x, y, c, chips = _mesh_pos()
        me, sibling = (x, y, c), (x, y, 1 - c)

        def copy(k, slot, block, to, src=None):
            dst = out_refs[k].at[4 * block[0] + 2 * block[1] + block[2]]
            return pltpu.make_async_remote_copy(src_ref=dst if src is None else src, dst_ref=dst,
                                                send_sem=send_sems.at[k * slots + slot], recv_sem=recv_sems.at[k * slots + slot],
                                                device_id=to, device_id_type=_MESH)

        local = [pltpu.make_async_copy(in_refs[k], out_refs[k].at[4 * x + 2 * y + c], local_sems.at[k]) for k in range(n)]
        sent = []
        for k in range(n):
            sent.append(copy(k, 0, me, sibling, src=in_refs[k]))
            sent += [copy(k, 1 + j, me, (*chip, c), src=in_refs[k]) for j, chip in enumerate(chips)]
        for cp in local + sent:
            cp.start()
        for j, chip in enumerate(chips):
            for k in range(n):
                copy(k, 1 + j, (*chip, c), me).wait_recv()
                passed = copy(k, 4 + j, (*chip, c), sibling)
                passed.start()
                sent.append(passed)
        for k in range(n):
            copy(k, 0, sibling, me).wait_recv()
            for j, chip in enumerate(chips):
                copy(k, 4 + j, (*chip, 1 - c), me).wait_recv()
        for cp in sent:
            cp.wait_send()
        for cp in local:
            cp.wait()

    return pl.pallas_call(
        body, name=name, in_specs=[_HBM] * n, out_specs=(_HBM,) * n,
        out_shape=tuple(jax.ShapeDtypeStruct((N_DEV,) + a.shape, a.dtype) for a in arrs),
        scratch_shapes=[pltpu.SemaphoreType.DMA((n * slots,)), pltpu.SemaphoreType.DMA((n * slots,)),
                        pltpu.SemaphoreType.DMA((n,))],
    )(*arrs)


_SEM = pl.BlockSpec(memory_space=pltpu.SEMAPHORE)
_EFFECT = pltpu.SideEffectType.DATAFLOW_SIDE_EFFECTING


def _direct_copies(src_refs, land_refs, send_sems, recv_sems, gather):
    x, y, c, _ = _mesh_pos()
    me = 4 * x + 2 * y + c
    n_peer = N_DEV - 1
    copies = []
    for r in range(1, N_DEV):
        px = 1 - x if r & 4 else x
        py = 1 - y if r & 2 else y
        pc = 1 - c if r & 1 else c
        for k, (src, land) in enumerate(zip(src_refs, land_refs)):
            copies.append(pltpu.make_async_remote_copy(
                src_ref=src if gather else src.at[4 * px + 2 * py + pc], dst_ref=land.at[me],
                send_sem=send_sems.at[k * n_peer + r - 1], recv_sem=recv_sems.at[k * n_peer + r - 1],
                device_id=(px, py, pc), device_id_type=_MESH))
    return copies


def _send_start(arrs, gather, after=None, *, name):
    n = len(arrs)
    lands = [lax.empty(((N_DEV,) + a.shape) if gather else a.shape, a.dtype) for a in arrs]
    n_in = 2 * n + (0 if after is None else 1)

    def body(*refs):
        src_refs, land_refs, send_sems, recv_sems, token = refs[:n], refs[n:2 * n], refs[n_in], refs[n_in + 1], refs[-1]
        for cp in _direct_copies(src_refs, land_refs, send_sems, recv_sems, gather):
            cp.start()
        token[...] = jnp.zeros_like(token)

    n_sem = n * (N_DEV - 1)
    bufs = list(arrs) + lands
    out = pl.pallas_call(
        body, name=name,
        out_shape=(pltpu.SemaphoreType.DMA((n_sem,)), pltpu.SemaphoreType.DMA((n_sem,)))
        + tuple(pltpu.HBM(b.shape, b.dtype) for b in bufs) + (jax.ShapeDtypeStruct((HALO, LANES), F32),),
        in_specs=[_HBM] * n_in, out_specs=(_SEM, _SEM) + (_HBM,) * (2 * n) + (pl.BlockSpec(memory_space=pltpu.VMEM),),
        input_output_aliases={i: 2 + i for i in range(2 * n)},
        compiler_params=pltpu.CompilerParams(has_side_effects=_EFFECT),
    )(*[pltpu.with_memory_space_constraint(b, pltpu.HBM) for b in bufs], *([] if after is None else [after]))
    return (out[0], out[1], list(out[2:2 + n]), list(out[2 + n:2 + 2 * n])), out[-1]


def _send_wait(started, gather, after, *, name):
    send_sems, recv_sems, srcs, lands = started
    n = len(srcs)

    def body(*refs):
        src_refs, land_refs, send_ref, recv_ref = refs[:n], refs[n:2 * n], refs[2 * n], refs[2 * n + 1]
        for cp in _direct_copies(src_refs, land_refs, send_ref, recv_ref, gather):
            cp.wait_send()
            cp.wait_recv()

    bufs = srcs + lands
    out = pl.pallas_call(
        body, name=name, out_shape=tuple(pltpu.HBM(b.shape, b.dtype) for b in bufs),
        in_specs=[_HBM] * (2 * n) + [_SEM, _SEM, _HBM], out_specs=(_HBM,) * (2 * n),
        input_output_aliases={i: i for i in range(2 * n)},
        compiler_params=pltpu.CompilerParams(has_side_effects=_EFFECT),
    )(*bufs, send_sems, recv_sems, after)
    return list(out[n:])


def _row_block(rows, limit=256):
    best = rows
    for cand in range(8, limit + 1, 8):
        if rows % cand == 0:
            best = cand
    return best if rows > limit else rows


def _adam(gp, w, m, v, *, name):
    p, rows, cols = gp.shape
    rb = _row_block(rows)

    def body(gp_ref, w_ref, m_ref, v_ref, g_ref, d_ref, m2_ref, v2_ref):
        g = gp_ref[0].astype(F32)
        for s in range(1, p):
            g = g + gp_ref[s].astype(F32)
        m2 = ADAM_B1 * m_ref[...] + (1.0 - ADAM_B1) * g
        v2 = ADAM_B2 * v_ref[...] + (1.0 - ADAM_B2) * (g * g)
        m_hat = m2 / (1.0 - ADAM_B1 ** ADAM_STEP)
        v_hat = v2 / (1.0 - ADAM_B2 ** ADAM_STEP)
        g_ref[...] = g
        d_ref[...] = -ADAM_LR * (m_hat / (jnp.sqrt(v_hat) + ADAM_EPS) + ADAM_WD * w_ref[...])
        m2_ref[...] = m2
        v2_ref[...] = v2

    blk = pl.BlockSpec((rb, cols), lambda i: (i, 0))
    return pl.pallas_call(
        body, name=name, grid=(rows // rb,),
        in_specs=[pl.BlockSpec((p, rb, cols), lambda i: (0, i, 0)), blk, blk, blk],
        out_specs=(blk,) * 4, out_shape=(jax.ShapeDtypeStruct((rows, cols), F32),) * 4,
        compiler_params=_cparams(("parallel",)),
    )(gp, w, m, v)


def _cols_full(g):
    return jnp.transpose(g, (1, 0, 2)).reshape(g.shape[1], N_DEV * g.shape[2])


def _pad_lanes(a, width=LANES):
    return jnp.pad(a, ((0, 0), (0, width - a.shape[1])))


def _chunk_rows_of(a):
    by_chunk = jnp.transpose(a[:, :DN_HEADS].reshape(-1, DN_CHUNK, DN_HEADS), (0, 2, 1))
    return jnp.pad(by_chunk, ((0, 0), (0, HALO - DN_HEADS), (0, 0)))


_SMALL = (("ffn1_norm", D_MODEL), ("mix_norm", D_MODEL), ("ffn2_norm", D_MODEL), ("final_norm", D_MODEL), ("a_log", DN_HEADS),
          ("dt_bias", DN_HEADS), ("dn_norm", DN_HEAD_DIM), ("sg_ln_g", SG_WIDTH), ("sg_ln_b", SG_WIDTH),
          ("sg_w", SG_GROUPS * SG_CHUNK * SG_CHUNK), ("sg_b", SG_GROUPS * SG_CHUNK), ("conv_w", CONV_K * 3 * DN_WIDTH))
_SMALL_ROWS = 1128
_SMALL_SHAPES = {"ffn1_norm": (1, D_MODEL), "mix_norm": (1, D_MODEL), "ffn2_norm": (1, D_MODEL), "final_norm": (D_MODEL,),
                 "a_log": (1, DN_HEADS), "dt_bias": (1, DN_HEADS), "dn_norm": (1, DN_HEAD_DIM), "sg_ln_g": (1, SG_WIDTH),
                 "sg_ln_b": (1, SG_WIDTH), "sg_w": (1, SG_GROUPS, SG_CHUNK, SG_CHUNK), "sg_b": (1, SG_GROUPS, SG_CHUNK)}


def _pack_small(d):
    flat = jnp.concatenate([d[name].reshape(-1) for name, _ in _SMALL])
    return jnp.pad(flat, (0, _SMALL_ROWS * LANES - flat.shape[0])).reshape(_SMALL_ROWS, LANES)


def _unpack_small(a):
    flat, out, at = a.reshape(-1), {}, 0
    for name, size in _SMALL:
        out[name] = flat[at:at + size]
        at += size
    return out


def kernel(x, ffn1_norm, ffn1_w_gate, ffn1_w_up, ffn1_w_down, mix_norm, w_in, conv_w, a_log, dt_bias, dn_norm, sg_ln_g, sg_ln_b, sg_w, sg_b, w_out, ffn2_norm, ffn2_w_gate, ffn2_w_up, ffn2_w_down, final_norm, loss_target, m_ffn1_norm, m_ffn1_w_gate, m_ffn1_w_up, m_ffn1_w_down, m_mix_norm, m_w_in, m_conv_w, m_a_log, m_dt_bias, m_dn_norm, m_sg_ln_g, m_sg_ln_b, m_sg_w, m_sg_b, m_w_out, m_ffn2_norm, m_ffn2_w_gate, m_ffn2_w_up, m_ffn2_w_down, m_final_norm, v_ffn1_norm, v_ffn1_w_gate, v_ffn1_w_up, v_ffn1_w_down, v_mix_norm, v_w_in, v_conv_w, v_a_log, v_dt_bias, v_dn_norm, v_sg_ln_g, v_sg_ln_b, v_sg_w, v_sg_b, v_w_out, v_ffn2_norm, v_ffn2_w_gate, v_ffn2_w_up, v_ffn2_w_down, v_final_norm):
    weights = dict(ffn1_norm=ffn1_norm, ffn1_w_gate=ffn1_w_gate, ffn1_w_up=ffn1_w_up, ffn1_w_down=ffn1_w_down, mix_norm=mix_norm, w_in=w_in, conv_w=conv_w, a_log=a_log, dt_bias=dt_bias, dn_norm=dn_norm, sg_ln_g=sg_ln_g, sg_ln_b=sg_ln_b, sg_w=sg_w, sg_b=sg_b, w_out=w_out, ffn2_norm=ffn2_norm, ffn2_w_gate=ffn2_w_gate, ffn2_w_up=ffn2_w_up, ffn2_w_down=ffn2_w_down, final_norm=final_norm)
    mom_m = dict(ffn1_norm=m_ffn1_norm, ffn1_w_gate=m_ffn1_w_gate, ffn1_w_up=m_ffn1_w_up, ffn1_w_down=m_ffn1_w_down, mix_norm=m_mix_norm, w_in=m_w_in, conv_w=m_conv_w, a_log=m_a_log, dt_bias=m_dt_bias, dn_norm=m_dn_norm, sg_ln_g=m_sg_ln_g, sg_ln_b=m_sg_ln_b, sg_w=m_sg_w, sg_b=m_sg_b, w_out=m_w_out, ffn2_norm=m_ffn2_norm, ffn2_w_gate=m_ffn2_w_gate, ffn2_w_up=m_ffn2_w_up, ffn2_w_down=m_ffn2_w_down, final_norm=m_final_norm)
    mom_v = dict(ffn1_norm=v_ffn1_norm, ffn1_w_gate=v_ffn1_w_gate, ffn1_w_up=v_ffn1_w_up, ffn1_w_down=v_ffn1_w_down, mix_norm=v_mix_norm, w_in=v_w_in, conv_w=v_conv_w, a_log=v_a_log, dt_bias=v_dt_bias, dn_norm=v_dn_norm, sg_ln_g=v_sg_ln_g, sg_ln_b=v_sg_ln_b, sg_w=v_sg_w, sg_b=v_sg_b, w_out=v_w_out, ffn2_norm=v_ffn2_norm, ffn2_w_gate=v_ffn2_w_gate, ffn2_w_up=v_ffn2_w_up, ffn2_w_down=v_ffn2_w_down, final_norm=v_final_norm)
    order = list(weights)
    big = ("ffn1_w_gate", "ffn1_w_up", "ffn1_w_down", "w_in", "w_out", "ffn2_w_gate", "ffn2_w_up", "ffn2_w_down")
    col_sharded = ("ffn1_w_gate", "ffn1_w_up", "w_in", "ffn2_w_gate", "ffn2_w_up")

    n_seq, seq, _ = x.shape
    t = n_seq * seq
    me = 4 * lax.axis_index("x") + 2 * lax.axis_index("y") + lax.axis_index("c")
    x0 = x.reshape(t, D_MODEL)
    tgt = loss_target.reshape(t, D_MODEL)

    def fill_own(land, own_block):
        return lax.dynamic_update_index_in_dim(land, own_block, me, 0)

    def rows_view(n, a):
        return jnp.transpose(a) if n in col_sharded else a

    def as_full(n, g):
        return g.reshape(-1, g.shape[-1])

    shards = {n: rows_view(n, weights[n][0]).astype(BF16) for n in big}
    ffn1_names, mix_names, ffn2_names = big[:3], big[3:5], big[5:]
    full = {n: as_full(n, g) for n, g in zip(ffn1_names, _gather2([shards[n] for n in ffn1_names], name="gather_ffn1"))}
    mix_srcs = [shards[n] for n in mix_names] + [conv_w[0]]
    mix_started, mix_token = _send_start(mix_srcs, True, full[ffn1_names[2]], name="gather_mix_start")
    ffn2_started, ffn2_token = _send_start([shards[n] for n in ffn2_names], True, mix_token, name="gather_ffn2_start")
    ffn1_norm_fwd = ffn1_norm + ffn2_token[:1, :1]
    alog, dtb = _pad_lanes(a_log), _pad_lanes(dt_bias)
    sgbt = _pad_lanes(sg_b[0].T)
    fnw = final_norm.reshape(1, D_MODEL)

    x1, h1, g1, u1 = _ffn_fwd(x0, ffn1_norm_fwd, full["ffn1_w_gate"], full["ffn1_w_up"], full["ffn1_w_down"], name="ffn1_fwd")
    mix_lands = [fill_own(land, src) for land, src in zip(_send_wait(mix_started, True, x1, name="gather_mix_wait"), mix_srcs)]
    full.update({n: as_full(n, g) for n, g in zip(mix_names, mix_lands)})
    conv_full = _cols_full(mix_lands[-1])
    w_in_t = full["w_in"]
    offs = (0, SG_WIDTH, 2 * SG_WIDTH, 2 * SG_WIDTH + 3 * DN_WIDTH, 2 * SG_WIDTH + 4 * DN_WIDTH)
    n_proj = offs[-1]

    def pad_rows(a):
        return jnp.pad(a, ((0, LANES - a.shape[0]), (0, 0)))

    ws = [w_in_t[offs[0]:offs[1]], w_in_t[offs[1]:offs[2]], w_in_t[offs[2]:offs[3]], w_in_t[offs[3]:offs[4]],
          pad_rows(w_in_t[n_proj:n_proj + DN_HEADS]), pad_rows(w_in_t[n_proj + DN_HEADS:n_proj + 2 * DN_HEADS])]
    wo_sg, wo_dn = full["w_out"][:SG_WIDTH], full["w_out"][SG_WIDTH:]
    u, v, qkv, z, bpre, apre = _mix_in_fwd(x1, mix_norm, ws, name="mix_in_fwd")
    sg_out = _sg_fwd(u, v, sg_ln_g, sg_ln_b, sg_w[0], sgbt, name="sg_fwd")
    q, k, vv, beta, gc = _dn_prep_fwd(qkv, bpre, apre, conv_full, alog, dtb, seq, name="dn_prep_fwd")
    grow = _chunk_rows_of(gc)
    wy_w, wy_u, q_dec, k_dec, qk, egl, inv = _delta_prep(q, k, vv, gc, grow, beta, name="delta_prep")
    o, states = _delta_seq_fwd(wy_w, wy_u, q_dec, k_dec, qk, egl, n_seq, seq, name="delta_seq_fwd")
    x2 = _mix_out_fwd(x1, sg_out, o, z, wo_sg, wo_dn, dn_norm, name="mix_out_fwd")
    ffn2_lands = _send_wait(ffn2_started, True, x2, name="gather_ffn2_wait")
    full.update({n: as_full(n, fill_own(land, shards[n])) for n, land in zip(ffn2_names, ffn2_lands)})
    dx3, loss_part, d_fn, h2, g2, u2 = _ffn_fwd(x2, ffn2_norm, full["ffn2_w_gate"], full["ffn2_w_up"], full["ffn2_w_down"],
                                                tgt, fnw, name="ffn2_fwd_loss")
    loss = lax.psum(loss_part[0, 0], ("x", "y", "c"))

    dx2, d_n2, d_g2, d_u2, d_d2 = _ffn_bwd(x2, ffn2_norm, h2, g2, u2, full["ffn2_w_gate"], full["ffn2_w_up"],
                                           full["ffn2_w_down"], dx3, name="ffn2_bwd")
    def by_owner(d_rows):
        return d_rows.reshape(N_DEV, -1, D_MODEL)

    ffn2_pieces = [by_owner(d_g2), by_owner(d_u2), by_owner(d_d2)]
    ffn2_sent, sent_token = _send_start(ffn2_pieces, False, name="grads_ffn2_start")
    dsg, do, dz, d_wo_sg, d_wo_dn, d_dnw = _mix_out_bwd(dx2, sg_out, o, z, wo_sg, wo_dn, dn_norm + sent_token[:1, :1],
                                                        name="mix_out_bwd")
    d_seq = _delta_seq_bwd(wy_w, wy_u, q_dec, k_dec, qk, egl, states, do, n_seq, seq, name="delta_seq_bwd")
    dq, dk, dv, dgc_a, dgrow, dbeta = _delta_par_bwd(q, k, vv, gc, grow, beta, inv, *d_seq, name="delta_par_bwd")
    dgc_b = _pad_lanes(jnp.transpose(dgrow[:, :DN_HEADS, :], (0, 2, 1)).reshape(t, DN_HEADS))
    dqkv, d_conv, dbpre, dapre, d_alog, d_dtb = _dn_prep_bwd(qkv, bpre, apre, conv_full, alog, dtb, dq, dk, dv, dbeta, dgc_a,
                                                             dgc_b, seq, name="dn_prep_bwd")
    du, dvv, d_lng, d_lnb, d_wc, d_sgbt = _sg_bwd(u, v, sg_ln_g, sg_ln_b, sg_w[0], sgbt, dsg, name="sg_bwd")
    dx1, d_mixn, d_wp = _mix_in_bwd(x1, mix_norm, ws, dx2, (du, dvv, dqkv, dz, dbpre, dapre), name="mix_in_bwd")
    d_w_in_t = jnp.concatenate([d_wp[:n_proj], d_wp[_PROJ_OFFSETS[4]:_PROJ_OFFSETS[4] + DN_HEADS],
                                d_wp[_PROJ_OFFSETS[5]:_PROJ_OFFSETS[5] + DN_HEADS]], axis=0)
    d_w_out = jnp.concatenate([d_wo_sg, d_wo_dn], axis=0)
    mix_pieces = [by_owner(d_w_in_t), by_owner(d_w_out).astype(BF16)]
    mix_sent, sent_token = _send_start(mix_pieces, False, name="grads_mix_start")
    grad_x, d_n1, dg1, du1, a1, dyh1 = _ffn_bwd_x(x0, ffn1_norm + sent_token[:1, :1], g1, u1, full["ffn1_w_gate"],
                                                  full["ffn1_w_up"], full["ffn1_w_down"], dx1, name="ffn1_bwd_x")
    small_grads = dict(ffn1_norm=d_n1, mix_norm=d_mixn, ffn2_norm=d_n2, final_norm=d_fn, a_log=d_alog[:, :DN_HEADS],
                       dt_bias=d_dtb[:, :DN_HEADS], dn_norm=d_dnw, sg_ln_g=d_lng, sg_ln_b=d_lnb, sg_w=d_wc,
                       sg_b=d_sgbt[:, :SG_GROUPS].T, conv_w=d_conv[:CONV_K])
    small_src = _pack_small(small_grads)
    small_sent, small_token = _send_start([small_src], True, name="small_grads_start")
    late, tokens = [], []

    def send_early(k, grad):
        piece = by_owner(grad)
        sent, token = _send_start([piece], False, name="grads_" + ffn1_names[k] + "_start")
        late.append(((ffn1_names[k],), sent, [piece]))
        tokens.append(token)
        return token

    _ffn_wgrads(h1, dg1, du1, a1, dyh1, send_early, small_token, name="ffn1_bwd")

    res = {}
    after = tokens[-1]

    def update(names, sent, pieces, after):
        lands = _send_wait(sent, False, after, name="grads_" + names[0] + "_wait")
        for n, land, p in zip(names, lands, pieces):
            got = fill_own(land, lax.dynamic_index_in_dim(p, me, 0, keepdims=False))
            upd = _adam(got, *[rows_view(n, src[n][0]) for src in (weights, mom_m, mom_v)], name="adam_" + n)
            res[n] = [rows_view(n, a) for a in upd]
            after = upd[0]
        return after

    for group in [(ffn2_names, ffn2_sent, ffn2_pieces), (mix_names, mix_sent, mix_pieces)] + late[:-1]:
        after = update(*group, after)
    (small_land,) = _send_wait(small_sent, True, after, name="small_grads_wait")
    small_parts = fill_own(small_land, small_src)
    zeros_conv = jnp.zeros((CONV_K * 3 * DN_WIDTH,), F32)
    packed = [_pack_small({**{n: src[n] for n, _ in _SMALL if n != "conv_w"}, "conv_w": zeros_conv})
              for src in (weights, mom_m, mom_v)]
    small_upd = _adam(small_parts, *packed, name="adam_small")
    small_res = [_unpack_small(a) for a in small_upd]
    conv_grad = lax.dynamic_slice_in_dim(small_res[0]["conv_w"].reshape(CONV_K, 3 * DN_WIDTH), me * (3 * DN_WIDTH // N_DEV),
                                         3 * DN_WIDTH // N_DEV, axis=1)
    res["conv_w"] = _adam(conv_grad[None], conv_w[0], m_conv_w[0], v_conv_w[0], name="adam_conv_w")
    update(*late[-1], res["conv_w"][0])

    outs = [[], [], [], []]
    for n in order:
        for kind in range(4):
            if n in res:
                outs[kind].append(res[n][kind][None])
            else:
                outs[kind].append(small_res[kind][n].reshape(_SMALL_SHAPES[n]))
    return (loss, grad_x.reshape(x.shape), *outs[0], *outs[1], *outs[2], *outs[3])
```

```python
import jax
import jax.numpy as jnp
from jax import lax
from jax.experimental import pallas as pl
from jax.experimental.pallas import tpu as pltpu

F32 = jnp.float32
BF16 = jnp.bfloat16

D_MODEL = 1024
D_FF = 2816
SG_WIDTH = 512
SG_GROUPS = 8
SG_GROUP_DIM = 64
SG_CHUNK = 128
DN_WIDTH = 512
DN_HEAD_DIM = 128
DN_HEADS = 4
DN_CHUNK = 64
CONV_K = 4
EPS = 1e-6
N_DEV = 8
LANES = 128
HALO = 8
MXU_COLS = 256

ADAM_LR = 0.001
ADAM_B1 = 0.9
ADAM_B2 = 0.999
ADAM_EPS = 1e-08
ADAM_WD = 0.01
ADAM_STEP = 10

VMEM_LIMIT = 60 * 1024 * 1024
WGRAD_K_TILE = 2048
TOKEN_BLOCK = 512
FF_BLOCK_FWD = 1408

_HI = lax.Precision.HIGHEST


def _cparams(sem):
    return pltpu.CompilerParams(dimension_semantics=sem, vmem_limit_bytes=VMEM_LIMIT)


def _tm(t, pref=TOKEN_BLOCK):
    return min(pref, t)


def _dg(a, b, ca, cb, precision):
    if precision is not None:
        return lax.dot_general(a, b, (((ca,), (cb,)), ((), ())), precision=precision, preferred_element_type=F32)
    return lax.dot_general(a.astype(BF16), b.astype(BF16), (((ca,), (cb,)), ((), ())), preferred_element_type=F32)


def _make_mm(precision):
    @jax.custom_vjp
    def mm(a, b):
        return _dg(a, b, 1, 0, precision)

    @jax.custom_vjp
    def mm_nt(a, b):
        return _dg(a, b, 1, 1, precision)

    @jax.custom_vjp
    def mm_tn(a, b):
        return _dg(a, b, 0, 0, precision)

    mm.defvjp(lambda a, b: (mm(a, b), (a, b)), lambda r, g: (mm_nt(g, r[1]), mm_tn(r[0], g)))
    mm_nt.defvjp(lambda a, b: (mm_nt(a, b), (a, b)), lambda r, g: (mm(g, r[1]), mm_tn(g, r[0])))
    mm_tn.defvjp(lambda a, b: (mm_tn(a, b), (a, b)), lambda r, g: (mm_nt(r[1], g), mm(r[0], g)))
    return mm, mm_nt, mm_tn


mm, mm_nt, mm_tn = _make_mm(None)
mmx, mmx_nt, mmx_tn = _make_mm(_HI)
mmh, mmh_nt, mmh_tn = _make_mm(lax.Precision.HIGH)


def _sigmoid(x):
    return 1.0 / (1.0 + jnp.exp(-x))


def _silu(x):
    return x * _sigmoid(x)


def _softplus(x):
    neg_abs = jnp.where(x > 0, -x, x)
    return jnp.where(x > 0, x, 0.0) + jnp.log(1.0 + jnp.exp(neg_abs))


def _gelu(x):
    return 0.5 * x * (1.0 + jnp.tanh(0.7978845608028654 * (x + 0.044715 * (x * x * x))))


def _rms_fwd(x, g):
    r = lax.rsqrt(jnp.mean(x * x, axis=-1, keepdims=True) + EPS)
    xh = x * r
    return xh * g, xh, r


def _rms_bwd(dh, xh, r, g):
    dxh = dh * g
    dx = r * (dxh - xh * jnp.mean(dxh * xh, axis=-1, keepdims=True))
    return dx, jnp.sum(dh * xh, axis=0, keepdims=True)


def _acc_out(ref, first, val):
    @pl.when(first)
    def _():
        ref[...] = val

    @pl.when(jnp.logical_not(first))
    def _():
        ref[...] += val


def _ffn_fwd(x, nw, wg, wu, wd, tgt=None, fnw=None, *, name):
    t = x.shape[0]
    tm, fb = _tm(t), FF_BLOCK_FWD
    n_t, n_f = t // tm, D_FF // fb
    with_loss = tgt is not None

    def body(*refs):
        if with_loss:
            (x_ref, nw_ref, wg_ref, wu_ref, wd_ref, tgt_ref, fnw_ref, dy_ref, loss_ref, dfn_ref, h_ref, g_ref, u_ref,
             acc_s) = refs
        else:
            x_ref, nw_ref, wg_ref, wu_ref, wd_ref, y_ref, h_ref, g_ref, u_ref, acc_s = refs
        i, j = pl.program_id(0), pl.program_id(1)

        @pl.when(j == 0)
        def _():
            h, _, _ = _rms_fwd(x_ref[...], nw_ref[...])
            h_ref[...] = h.astype(BF16)
            acc_s[...] = jnp.zeros_like(acc_s)

        h = h_ref[...]
        nt = (((1,), (1,)), ((), ()))
        g = lax.dot_general(h, wg_ref[...], nt, preferred_element_type=F32)
        u = lax.dot_general(h, wu_ref[...], nt, preferred_element_type=F32)
        g_ref[...] = g.astype(BF16)
        u_ref[...] = u.astype(BF16)
        a = _silu(g) * u
        acc_s[...] += jnp.dot(a.astype(BF16), wd_ref[...], preferred_element_type=F32)

        @pl.when(j == n_f - 1)
        def _():
            y = x_ref[...] + 0.5 * acc_s[...]
            if not with_loss:
                y_ref[...] = y
            else:
                gf = fnw_ref[...]
                out, xh, r = _rms_fwd(y, gf)
                err = out - tgt_ref[...]
                part = 0.5 * jnp.sum(jnp.mean(err * err, axis=-1, keepdims=True), axis=0, keepdims=True)
                d_out = err * (1.0 / D_MODEL)
                dy, dgf = _rms_bwd(d_out, xh, r, gf)
                dy_ref[...] = dy
                _acc_out(loss_ref, i == 0, jnp.broadcast_to(part, loss_ref.shape))
                _acc_out(dfn_ref, i == 0, dgf)

    row = lambda i, j: (i, 0)
    const = lambda i, j: (0, 0)
    in_specs = [
        pl.BlockSpec((tm, D_MODEL), row),
        pl.BlockSpec((1, D_MODEL), const),
        pl.BlockSpec((fb, D_MODEL), lambda i, j: (j, 0)),
        pl.BlockSpec((fb, D_MODEL), lambda i, j: (j, 0)),
        pl.BlockSpec((fb, D_MODEL), lambda i, j: (j, 0)),
    ]
    args = [x, nw, wg, wu, wd]
    saved_shape = (jax.ShapeDtypeStruct((t, D_MODEL), BF16), jax.ShapeDtypeStruct((t, D_FF), BF16),
                   jax.ShapeDtypeStruct((t, D_FF), BF16))
    saved_specs = (pl.BlockSpec((tm, D_MODEL), row), pl.BlockSpec((tm, fb), lambda i, j: (i, j)),
                   pl.BlockSpec((tm, fb), lambda i, j: (i, j)))
    if with_loss:
        in_specs += [pl.BlockSpec((tm, D_MODEL), row), pl.BlockSpec((1, D_MODEL), const)]
        args += [tgt, fnw]
        out_shape = (jax.ShapeDtypeStruct((t, D_MODEL), F32), jax.ShapeDtypeStruct((8, LANES), F32),
                     jax.ShapeDtypeStruct((1, D_MODEL), F32)) + saved_shape
        out_specs = (pl.BlockSpec((tm, D_MODEL), row), pl.BlockSpec((8, LANES), const),
                     pl.BlockSpec((1, D_MODEL), const)) + saved_specs
        sem = ("arbitrary", "arbitrary")
    else:
        out_shape = (jax.ShapeDtypeStruct((t, D_MODEL), F32),) + saved_shape
        out_specs = (pl.BlockSpec((tm, D_MODEL), row),) + saved_specs
        sem = ("parallel", "arbitrary")
    return pl.pallas_call(
        body, name=name, grid=(n_t, n_f), in_specs=in_specs, out_specs=out_specs, out_shape=out_shape,
        scratch_shapes=[pltpu.VMEM((tm, D_MODEL), F32)],
        compiler_params=_cparams(sem),
    )(*args)


def _ffn_bwd_x(x, nw, g, u, wg, wu, wd, dy, *, name):
    t = x.shape[0]
    tm = _tm(t, 256)

    def body(x_ref, nw_ref, g_ref, u_ref, wg_ref, wu_ref, wd_ref, dy_ref, dx_ref, dnw_ref, dg_ref, du_ref, a_ref, dyh_ref):
        i = pl.program_id(0)
        nt = (((1,), (1,)), ((), ()))
        dy = dy_ref[...]
        dyh = (0.5 * dy).astype(BF16)
        dyh_ref[...] = dyh
        gate, up = g_ref[...].astype(F32), u_ref[...].astype(F32)
        s = _sigmoid(gate)
        gs = gate * s
        da = lax.dot_general(dyh, wd_ref[...], nt, preferred_element_type=F32)
        dg = (da * up * (s + gs * (1.0 - s))).astype(BF16)
        du = (da * gs).astype(BF16)
        dg_ref[...] = dg
        du_ref[...] = du
        a_ref[...] = (gs * up).astype(BF16)
        dh = (jnp.dot(dg, wg_ref[...], preferred_element_type=F32)
              + jnp.dot(du, wu_ref[...], preferred_element_type=F32))
        xv = x_ref[...]
        r = lax.rsqrt(jnp.mean(xv * xv, axis=-1, keepdims=True) + EPS)
        dx, dnw = _rms_bwd(dh, xv * r, r, nw_ref[...])
        dx_ref[...] = dy + dx
        _acc_out(dnw_ref, i == 0, dnw)

    row = lambda i: (i, 0)
    const = lambda i: (0, 0)
    once = pl.Buffered(1)
    wide = pl.BlockSpec((tm, D_FF), row)
    return pl.pallas_call(
        body, name=name, grid=(t // tm,),
        in_specs=[pl.BlockSpec((tm, D_MODEL), row), pl.BlockSpec((1, D_MODEL), const), wide, wide,
                  pl.BlockSpec((D_FF, D_MODEL), const, pipeline_mode=once), pl.BlockSpec((D_FF, D_MODEL), const, pipeline_mode=once),
                  pl.BlockSpec((D_FF, D_MODEL), const, pipeline_mode=once), pl.BlockSpec((tm, D_MODEL), row)],
        out_specs=(pl.BlockSpec((tm, D_MODEL), row), pl.BlockSpec((1, D_MODEL), const), wide, wide, wide,
                   pl.BlockSpec((tm, D_MODEL), row)),
        out_shape=(jax.ShapeDtypeStruct((t, D_MODEL), F32), jax.ShapeDtypeStruct((1, D_MODEL), F32),
                   jax.ShapeDtypeStruct((t, D_FF), BF16), jax.ShapeDtypeStruct((t, D_FF), BF16),
                   jax.ShapeDtypeStruct((t, D_FF), BF16), jax.ShapeDtypeStruct((t, D_MODEL), BF16)),
        compiler_params=_cparams(("arbitrary",)),
    )(x, nw, g, u, wg, wu, wd, dy)


def _wgrad(a, b, bm, bn, after=None, *, name):
    k, m = a.shape
    n = b.shape[1]
    tk = _tm(k, WGRAD_K_TILE)
    n_k = k // tk

    def body(a_ref, b_ref, *rest):
        o_ref, acc_s = rest[-2], rest[-1]
        s = pl.program_id(2)
        for c in range(bn // MXU_COLS):
            cols = slice(c * MXU_COLS, (c + 1) * MXU_COLS)
            part = lax.dot_general(a_ref[...], b_ref[:, cols], (((0,), (0,)), ((), ())), preferred_element_type=F32)
            acc_s[:, cols] = jnp.where(s == 0, 0.0, acc_s[:, cols]) + part

        @pl.when(s == n_k - 1)
        def _():
            o_ref[...] = acc_s[...].astype(BF16)

    return pl.pallas_call(
        body, name=name, grid=(m // bm, n // bn, n_k),
        in_specs=[pl.BlockSpec((tk, bm), lambda i, j, s: (s, i)), pl.BlockSpec((tk, bn), lambda i, j, s: (s, j))]
        + ([] if after is None else [_HBM]),
        out_specs=pl.BlockSpec((bm, bn), lambda i, j, s: (i, j)),
        out_shape=jax.ShapeDtypeStruct((m, n), BF16),
        scratch_shapes=[pltpu.VMEM((bm, bn), F32)],
        compiler_params=_cparams(("parallel", "parallel", "arbitrary")),
    )(a, b, *([] if after is None else [after]))


def _ffn_wgrads(h, dg, du, a, dyh, between=None, after=None, *, name):
    grads = []
    for k, (lhs, rhs, tag) in enumerate(((dg, h, "_wg"), (du, h, "_wu"), (a, dyh, "_wd"))):
        grads.append(_wgrad(lhs, rhs, D_FF // 2, D_MODEL, after, name=name + tag))
        after = None if between is None else between(k, grads[-1])
    return grads


def _ffn_bwd(x, nw, h, g, u, wg, wu, wd, dy, *, name):
    dx, dnw, dg, du, a, dyh = _ffn_bwd_x(x, nw, g, u, wg, wu, wd, dy, name=name + "_x")
    return (dx, dnw, *_ffn_wgrads(h, dg, du, a, dyh, name=name))


_PROJ_WIDTHS = (SG_WIDTH, SG_WIDTH, 3 * DN_WIDTH, DN_WIDTH, LANES, LANES)


def _mix_in_fwd(x, nw, ws, *, name):
    t = x.shape[0]
    tm = _tm(t)

    def body(x_ref, nw_ref, *refs):
        w_refs, o_refs = refs[:6], refs[6:]
        h, _, _ = _rms_fwd(x_ref[...], nw_ref[...])
        h = h.astype(BF16)
        for w_ref, o_ref in zip(w_refs, o_refs):
            o_ref[...] = lax.dot_general(h, w_ref[...], (((1,), (1,)), ((), ())), preferred_element_type=F32)

    row = lambda i: (i, 0)
    const = lambda i: (0, 0)
    return pl.pallas_call(
        body, name=name, grid=(t // tm,),
        in_specs=[pl.BlockSpec((tm, D_MODEL), row), pl.BlockSpec((1, D_MODEL), const)]
        + [pl.BlockSpec((n, D_MODEL), const) for n in _PROJ_WIDTHS],
        out_specs=tuple(pl.BlockSpec((tm, n), row) for n in _PROJ_WIDTHS),
        out_shape=tuple(jax.ShapeDtypeStruct((t, n), F32) for n in _PROJ_WIDTHS),
        compiler_params=_cparams(("parallel",)),
    )(x, nw, *ws)


_PROJ_TOTAL = sum(_PROJ_WIDTHS)
_PROJ_OFFSETS = tuple(sum(_PROJ_WIDTHS[:k]) for k in range(len(_PROJ_WIDTHS)))


def _mix_in_bwd(x, nw, ws, dres, dps, *, name):
    t = x.shape[0]
    tm = _tm(t, 256)

    def body(x_ref, nw_ref, dres_ref, *refs):
        w_refs, dp_refs, dx_ref, dnw_ref, h_ref, dpb_ref = refs[:6], refs[6:12], refs[12], refs[13], refs[14], refs[15]
        i = pl.program_id(0)
        hf, xh, r = _rms_fwd(x_ref[...], nw_ref[...])
        h_ref[...] = hf.astype(BF16)
        dh = jnp.zeros((tm, D_MODEL), F32)
        for w_ref, dp_ref, off, width in zip(w_refs, dp_refs, _PROJ_OFFSETS, _PROJ_WIDTHS):
            dp = dp_ref[...].astype(BF16)
            dpb_ref[:, off:off + width] = dp
            dh = dh + jnp.dot(dp, w_ref[...], preferred_element_type=F32)
        dx, dnw = _rms_bwd(dh, xh, r, nw_ref[...])
        dx_ref[...] = dres_ref[...] + dx
        _acc_out(dnw_ref, i == 0, dnw)

    row = lambda i: (i, 0)
    const = lambda i: (0, 0)
    dx, dnw, h, dpb = pl.pallas_call(
        body, name=name + "_x", grid=(t // tm,),
        in_specs=[pl.BlockSpec((tm, D_MODEL), row), pl.BlockSpec((1, D_MODEL), const), pl.BlockSpec((tm, D_MODEL), row)]
        + [pl.BlockSpec((n, D_MODEL), const) for n in _PROJ_WIDTHS]
        + [pl.BlockSpec((tm, n), row) for n in _PROJ_WIDTHS],
        out_specs=(pl.BlockSpec((tm, D_MODEL), row), pl.BlockSpec((1, D_MODEL), const), pl.BlockSpec((tm, D_MODEL), row),
                   pl.BlockSpec((tm, _PROJ_TOTAL), row)),
        out_shape=(jax.ShapeDtypeStruct((t, D_MODEL), F32), jax.ShapeDtypeStruct((1, D_MODEL), F32),
                   jax.ShapeDtypeStruct((t, D_MODEL), BF16), jax.ShapeDtypeStruct((t, _PROJ_TOTAL), BF16)),
        compiler_params=_cparams(("arbitrary",)),
    )(x, nw, dres, *ws, *dps)
    return dx, dnw, _wgrad(dpb, h, _PROJ_TOTAL // 2, D_MODEL, name=name + "_w")


_SG_TILES = SG_WIDTH // LANES


def _lane_tiles(ref, rows=slice(None)):
    return [ref[rows, p * LANES:(p + 1) * LANES] for p in range(_SG_TILES)]


def _sg_fn(u, v, lng, lnb, wcs, sgbt):
    lane = lax.broadcasted_iota(jnp.int32, (1, LANES), 1)
    rr = lax.broadcasted_iota(jnp.int32, (SG_CHUNK, SG_CHUNK), 0)
    cc = lax.broadcasted_iota(jnp.int32, (SG_CHUNK, SG_CHUNK), 1)
    per_tile = LANES // SG_GROUP_DIM
    gu, gv = [_gelu(a) for a in u], [_gelu(a) for a in v]
    mu = sum(jnp.sum(a, axis=-1, keepdims=True) for a in gv) * (1.0 / SG_WIDTH)
    cen = [a - mu for a in gv]
    var = sum(jnp.sum(a * a, axis=-1, keepdims=True) for a in cen) * (1.0 / SG_WIDTH)
    rstd = lax.rsqrt(var + EPS)
    ln = [a * rstd * g + b for a, g, b in zip(cen, lng, lnb)]
    out = []
    for p in range(_SG_TILES):
        vs = None
        for e in range(per_tile):
            g = p * per_tile + e
            in_group = jnp.logical_and(lane >= e * SG_GROUP_DIM, lane < (e + 1) * SG_GROUP_DIM)
            w_causal = jnp.where(rr >= cc, wcs[g], 0.0)
            bias = jnp.sum(jnp.where(lane == g, sgbt, 0.0), axis=1, keepdims=True)
            term = jnp.where(in_group, mm(w_causal, ln[p]) + bias, 0.0)
            vs = term if vs is None else vs + term
        out.append(gu[p] * vs)
    return out


def _sg_fwd(u, v, lng, lnb, wc, sgbt, *, name):
    t = u.shape[0]
    tm = _tm(t)

    def body(u_ref, v_ref, lng_ref, lnb_ref, wc_ref, sgbt_ref, o_ref):
        wcs = [wc_ref[g] for g in range(SG_GROUPS)]
        for c in range(tm // SG_CHUNK):
            rows = pl.ds(c * SG_CHUNK, SG_CHUNK)
            out = _sg_fn(_lane_tiles(u_ref, rows), _lane_tiles(v_ref, rows), _lane_tiles(lng_ref), _lane_tiles(lnb_ref),
                         wcs, sgbt_ref[...])
            for p in range(_SG_TILES):
                o_ref[rows, p * LANES:(p + 1) * LANES] = out[p]

    row = lambda i: (i, 0)
    const = lambda i: (0, 0)
    return pl.pallas_call(
        body, name=name, grid=(t // tm,),
        in_specs=[pl.BlockSpec((tm, SG_WIDTH), row), pl.BlockSpec((tm, SG_WIDTH), row),
                  pl.BlockSpec((1, SG_WIDTH), const), pl.BlockSpec((1, SG_WIDTH), const),
                  pl.BlockSpec((SG_GROUPS, SG_CHUNK, SG_CHUNK), lambda i: (0, 0, 0)), pl.BlockSpec((SG_CHUNK, LANES), const)],
        out_specs=pl.BlockSpec((tm, SG_WIDTH), row),
        out_shape=jax.ShapeDtypeStruct((t, SG_WIDTH), F32),
        compiler_params=_cparams(("parallel",)),
    )(u, v, lng, lnb, wc, sgbt)


def _sg_bwd(u, v, lng, lnb, wc, sgbt, dout, *, name):
    t = u.shape[0]
    tm = _tm(t)

    def body(u_ref, v_ref, lng_ref, lnb_ref, wc_ref, sgbt_ref, do_ref, du_ref, dv_ref, dlng_ref, dlnb_ref, dwc_ref, dsgbt_ref):
        i = pl.program_id(0)
        wcs = [wc_ref[g] for g in range(SG_GROUPS)]
        tot = None
        for c in range(tm // SG_CHUNK):
            rows = pl.ds(c * SG_CHUNK, SG_CHUNK)
            _, vjp = jax.vjp(_sg_fn, _lane_tiles(u_ref, rows), _lane_tiles(v_ref, rows), _lane_tiles(lng_ref),
                             _lane_tiles(lnb_ref), wcs, sgbt_ref[...])
            du, dv, dlng, dlnb, dwcs, dsgbt = vjp(_lane_tiles(do_ref, rows))
            for p in range(_SG_TILES):
                du_ref[rows, p * LANES:(p + 1) * LANES] = du[p].astype(BF16)
                dv_ref[rows, p * LANES:(p + 1) * LANES] = dv[p].astype(BF16)
            part = (dlng, dlnb, dwcs, dsgbt)
            tot = part if tot is None else jax.tree.map(jnp.add, tot, part)
        dlng, dlnb, dwcs, dsgbt = tot
        _acc_out(dlng_ref, i == 0, jnp.concatenate(dlng, axis=1))
        _acc_out(dlnb_ref, i == 0, jnp.concatenate(dlnb, axis=1))
        _acc_out(dsgbt_ref, i == 0, dsgbt)
        for g in range(SG_GROUPS):
            @pl.when(i == 0)
            def _(g=g):
                dwc_ref[g] = dwcs[g]

            @pl.when(i > 0)
            def _(g=g):
                dwc_ref[g] += dwcs[g]

    row = lambda i: (i, 0)
    const = lambda i: (0, 0)
    wspec = pl.BlockSpec((SG_GROUPS, SG_CHUNK, SG_CHUNK), lambda i: (0, 0, 0))
    return pl.pallas_call(
        body, name=name, grid=(t // tm,),
        in_specs=[pl.BlockSpec((tm, SG_WIDTH), row), pl.BlockSpec((tm, SG_WIDTH), row),
                  pl.BlockSpec((1, SG_WIDTH), const), pl.BlockSpec((1, SG_WIDTH), const), wspec,
                  pl.BlockSpec((SG_CHUNK, LANES), const), pl.BlockSpec((tm, SG_WIDTH), row)],
        out_specs=(pl.BlockSpec((tm, SG_WIDTH), row), pl.BlockSpec((tm, SG_WIDTH), row),
                   pl.BlockSpec((1, SG_WIDTH), const), pl.BlockSpec((1, SG_WIDTH), const), wspec,
                   pl.BlockSpec((SG_CHUNK, LANES), const)),
        out_shape=(jax.ShapeDtypeStruct((t, SG_WIDTH), BF16), jax.ShapeDtypeStruct((t, SG_WIDTH), BF16),
                   jax.ShapeDtypeStruct((1, SG_WIDTH), F32), jax.ShapeDtypeStruct((1, SG_WIDTH), F32),
                   jax.ShapeDtypeStruct((SG_GROUPS, SG_CHUNK, SG_CHUNK), F32), jax.ShapeDtypeStruct((SG_CHUNK, LANES), F32)),
        compiler_params=_cparams(("arbitrary",)),
    )(u, v, lng, lnb, wc, sgbt, dout)


def _conv_taps(ext, w, tm):
    y = None
    for j in range(CONV_K):
        s = CONV_K - 1 - j
        shifted = ext if s == 0 else pltpu.roll(ext, s, 0)
        term = w[j:j + 1, :] * shifted[HALO:HALO + tm, :]
        y = term if y is None else y + term
    return y


def _post_conv(yq, yk, yv, bpre, apre, alog, dtb):
    def l2(a):
        return a * lax.rsqrt(jnp.sum(a * a, axis=-1, keepdims=True) + EPS)

    q = [l2(_silu(a)) for a in yq]
    k = [l2(_silu(a)) for a in yk]
    return q, k, _silu(yv), _sigmoid(bpre), -jnp.exp(alog) * _softplus(apre + dtb)


def _chunk_tril(tm):
    rr = lax.broadcasted_iota(jnp.int32, (tm, tm), 0)
    cc = lax.broadcasted_iota(jnp.int32, (tm, tm), 1)
    shift = DN_CHUNK.bit_length() - 1
    same = jnp.right_shift(rr, shift) == jnp.right_shift(cc, shift)
    return jnp.where(jnp.logical_and(same, rr >= cc), 1.0, 0.0).astype(F32)


def _halo_specs(tm, width, n_blocks_seq, n_blocks):
    per = tm // HALO
    prev = pl.BlockSpec((HALO, width), lambda i: (jnp.maximum(i * per - 1, 0), 0))
    nxt = pl.BlockSpec((HALO, width), lambda i: (jnp.minimum((i + 1) * per, n_blocks * per - 1), 0))
    return prev, nxt


def _split_heads(ref, base):
    return [ref[:, base + h * DN_HEAD_DIM: base + (h + 1) * DN_HEAD_DIM] for h in range(DN_HEADS)]


def _dn_prep_fwd(qkv, bpre, apre, conv_w, alog, dtb, seq, *, name):
    t = qkv.shape[0]
    tm = _tm(t)
    bps = seq // tm
    cw = 3 * DN_WIDTH

    def body(x_ref, halo_ref, b_ref, a_ref, w_ref, alog_ref, dtb_ref, q_ref, k_ref, v_ref, beta_ref, gc_ref):
        i = pl.program_id(0)
        keep = jnp.where(i % bps == 0, 0.0, 1.0)
        ext = jnp.concatenate([halo_ref[...] * keep, x_ref[...]], axis=0)
        y = _conv_taps(ext, w_ref[...], tm)
        yq = [y[:, h * DN_HEAD_DIM:(h + 1) * DN_HEAD_DIM] for h in range(DN_HEADS)]
        yk = [y[:, DN_WIDTH + h * DN_HEAD_DIM: DN_WIDTH + (h + 1) * DN_HEAD_DIM] for h in range(DN_HEADS)]
        q, k, v, beta, g = _post_conv(yq, yk, y[:, 2 * DN_WIDTH:], b_ref[...], a_ref[...], alog_ref[...], dtb_ref[...])
        for h in range(DN_HEADS):
            q_ref[:, h * DN_HEAD_DIM:(h + 1) * DN_HEAD_DIM] = q[h]
            k_ref[:, h * DN_HEAD_DIM:(h + 1) * DN_HEAD_DIM] = k[h]
        v_ref[...] = v
        beta_ref[...] = beta
        gc_ref[...] = mmx(_chunk_tril(tm), g)

    row = lambda i: (i, 0)
    const = lambda i: (0, 0)
    prev, _ = _halo_specs(tm, cw, bps, t // tm)
    return pl.pallas_call(
        body, name=name, grid=(t // tm,),
        in_specs=[pl.BlockSpec((tm, cw), row), prev, pl.BlockSpec((tm, LANES), row), pl.BlockSpec((tm, LANES), row),
                  pl.BlockSpec((CONV_K, cw), const), pl.BlockSpec((1, LANES), const), pl.BlockSpec((1, LANES), const)],
        out_specs=tuple(pl.BlockSpec((tm, n), row) for n in (DN_WIDTH, DN_WIDTH, DN_WIDTH, LANES, LANES)),
        out_shape=tuple(jax.ShapeDtypeStruct((t, n), F32) for n in (DN_WIDTH, DN_WIDTH, DN_WIDTH, LANES, LANES)),
        compiler_params=_cparams(("parallel",)),
    )(qkv, qkv, bpre, apre, conv_w, alog, dtb)


def _y_heads(y):
    yq = [y[:, h * DN_HEAD_DIM:(h + 1) * DN_HEAD_DIM] for h in range(DN_HEADS)]
    yk = [y[:, DN_WIDTH + h * DN_HEAD_DIM: DN_WIDTH + (h + 1) * DN_HEAD_DIM] for h in range(DN_HEADS)]
    return yq, yk, y[:, 2 * DN_WIDTH:]


def _dn_prep_bwd(qkv, bpre, apre, conv_w, alog, dtb, dq, dk, dv, dbeta, dgc, dgc2, seq, *, name):
    t = qkv.shape[0]
    tm = _tm(t)
    bps = seq // tm
    cw = 3 * DN_WIDTH
    n_ext = tm + HALO

    def body(x_ref, halo_ref, xn_ref, b_ref, a_ref, w_ref, alog_ref, dtb_ref, dq_ref, dk_ref, dv_ref, dqn_ref, dkn_ref,
             dvn_ref, dbeta_ref, dgc_ref, dgc2_ref, dx_ref, dw_ref, db_ref, da_ref, dalog_ref, ddtb_ref):
        i = pl.program_id(0)
        keep_prev = jnp.where(i % bps == 0, 0.0, 1.0)
        keep_next = jnp.where(i % bps == bps - 1, 0.0, 1.0)
        w = w_ref[...]
        x = x_ref[...]
        ext = jnp.concatenate([halo_ref[...] * keep_prev, x], axis=0)
        yq, yk, yv = _y_heads(_conv_taps(ext, w, tm))
        _, vjp = jax.vjp(_post_conv, yq, yk, yv, b_ref[...], a_ref[...], alog_ref[...], dtb_ref[...])
        dg = mmx_tn(_chunk_tril(tm), dgc_ref[...] + dgc2_ref[...])
        dyq, dyk, dyv, db, da, dalog, ddtb = vjp((_split_heads(dq_ref, 0), _split_heads(dk_ref, 0), dv_ref[...],
                                                  dbeta_ref[...], dg))
        dy = jnp.concatenate(dyq + dyk + [dyv], axis=1)
        ext_n = jnp.concatenate([x[tm - HALO:, :], xn_ref[...]], axis=0)
        _, vjp_n = jax.vjp(lambda *ys: _post_conv(*ys, b_ref[:HALO, :], a_ref[:HALO, :], alog_ref[...], dtb_ref[...])[:3],
                           *_y_heads(_conv_taps(ext_n, w, HALO)))
        dyq_n, dyk_n, dyv_n = vjp_n((_split_heads(dqn_ref, 0), _split_heads(dkn_ref, 0), dvn_ref[...]))
        dyext = jnp.concatenate([dy, jnp.concatenate(dyq_n + dyk_n + [dyv_n], axis=1) * keep_next], axis=0)

        @pl.when(i == 0)
        def _():
            dw_ref[...] = jnp.zeros_like(dw_ref)

        dx = None
        for j in range(CONV_K):
            s = CONV_K - 1 - j
            fut = dyext if s == 0 else pltpu.roll(dyext, n_ext - s, 0)
            term = w[j:j + 1, :] * fut[0:tm, :]
            dx = term if dx is None else dx + term
            past = ext if s == 0 else pltpu.roll(ext, s, 0)
            dw_ref[j:j + 1, :] += jnp.sum(dy * past[HALO:HALO + tm, :], axis=0, keepdims=True)
        dx_ref[...] = dx.astype(BF16)
        db_ref[...] = db.astype(BF16)
        da_ref[...] = da.astype(BF16)
        _acc_out(dalog_ref, i == 0, dalog)
        _acc_out(ddtb_ref, i == 0, ddtb)

    row = lambda i: (i, 0)
    const = lambda i: (0, 0)
    prev, nxt = _halo_specs(tm, cw, bps, t // tm)
    _, nxt_h = _halo_specs(tm, DN_WIDTH, bps, t // tm)
    tok = pl.BlockSpec((tm, DN_WIDTH), row)
    lanes = pl.BlockSpec((tm, LANES), row)
    return pl.pallas_call(
        body, name=name, grid=(t // tm,),
        in_specs=[pl.BlockSpec((tm, cw), row), prev, nxt, lanes, lanes,
                  pl.BlockSpec((CONV_K, cw), const), pl.BlockSpec((1, LANES), const), pl.BlockSpec((1, LANES), const),
                  tok, tok, tok, nxt_h, nxt_h, nxt_h, lanes, lanes, lanes],
        out_specs=(pl.BlockSpec((tm, cw), row), pl.BlockSpec((HALO, cw), const), lanes, lanes,
                   pl.BlockSpec((1, LANES), const), pl.BlockSpec((1, LANES), const)),
        out_shape=(jax.ShapeDtypeStruct((t, cw), BF16), jax.ShapeDtypeStruct((HALO, cw), F32),
                   jax.ShapeDtypeStruct((t, LANES), BF16), jax.ShapeDtypeStruct((t, LANES), BF16),
                   jax.ShapeDtypeStruct((1, LANES), F32), jax.ShapeDtypeStruct((1, LANES), F32)),
        compiler_params=_cparams(("arbitrary",)),
    )(qkv, qkv, qkv, bpre, apre, conv_w, alog, dtb, dq, dk, dv, dq, dk, dv, dbeta, dgc, dgc2)


def _inv_unit_lower(l_mats, eye):
    invs = [eye - l for l in l_mats]
    powers = list(l_mats)
    n = 2
    while n < eye.shape[0]:
        powers = [mmh(p, p) for p in powers]
        invs = [inv + mmh(inv, p) for inv, p in zip(invs, powers)]
        n *= 2
    return invs


@jax.custom_vjp
def _solve(l_mat, rhs, inv):
    return mmh(inv, rhs)


def _solve_fwd(l_mat, rhs, inv):
    sol = mmh(inv, rhs)
    return sol, (inv, sol)


def _solve_bwd(res, d_sol):
    inv, sol = res
    d_rhs = mm_tn(inv, d_sol)
    return -mm_nt(d_rhs, sol), d_rhs, jnp.zeros_like(inv)


_solve.defvjp(_solve_fwd, _solve_bwd)


def _prep_fn(q, k, v, gc, gr, b, inv):
    ids = range(len(q))
    c = q[0].shape[0]
    rr = lax.broadcasted_iota(jnp.int32, (c, c), 0)
    cc = lax.broadcasted_iota(jnp.int32, (c, c), 1)
    incl, strict = rr >= cc, rr > cc
    is_last = lax.broadcasted_iota(jnp.int32, (c, 1), 0) == c - 1
    qs = [q[i] * (DN_HEAD_DIM ** -0.5) for i in ids]
    decay = [jnp.where(incl, jnp.exp(jnp.where(incl, gc[i] - gr[i], 0.0)), 0.0) for i in ids]
    kb = [k[i] * b[i] for i in ids]
    vb = [v[i] * b[i] for i in ids]
    kk = [mm_nt(kb[i], k[i]) for i in ids]
    l_mat = [jnp.where(strict, kk[i] * decay[i], 0.0) for i in ids]
    eg = [jnp.exp(gc[i]) for i in ids]
    if inv is None:
        inv = _inv_unit_lower(l_mat, jnp.where(rr == cc, 1.0, 0.0).astype(F32))
    u_wy = [_solve(l_mat[i], vb[i], inv[i]) for i in ids]
    w_wy = [_solve(l_mat[i], kb[i] * eg[i], inv[i]) for i in ids]
    qk = [mm_nt(qs[i], k[i]) * decay[i] for i in ids]
    g_last = [jnp.sum(jnp.where(is_last, gc[i], 0.0), axis=0, keepdims=True) for i in ids]
    k_dec = [k[i] * jnp.exp(g_last[i] - gc[i]) for i in ids]
    egl = [jnp.broadcast_to(jnp.exp(g_last[i]), (1, LANES)) for i in ids]
    return [(w_wy[i], u_wy[i], qs[i] * eg[i], k_dec[i], qk[i], egl[i]) for i in ids], inv


def _seq_fn(w, u, qd, kd, qk, egl, s):
    ids = range(len(w))
    ws = [mm(w[i], s[i]) for i in ids]
    qs = [mm(qd[i], s[i]) for i in ids]
    v_new = [u[i] - ws[i] for i in ids]
    o = [qs[i] + mm(qk[i], v_new[i]) for i in ids]
    s_new = [s[i] * egl[i] + mm_tn(kd[i], v_new[i]) for i in ids]
    return o, s_new


def _lane_col(a, h):
    lane = lax.broadcasted_iota(jnp.int32, (1, LANES), 1)
    return jnp.sum(jnp.where(lane == h, a, 0.0), axis=1, keepdims=True)


def _col_lane(col, h):
    lane = lax.broadcasted_iota(jnp.int32, (1, LANES), 1)
    return jnp.where(lane == h, col, 0.0)


def _head_cols(h):
    return slice(h * DN_HEAD_DIM, (h + 1) * DN_HEAD_DIM)


def _chunk_rows(n):
    return pl.ds(pl.multiple_of(n * DN_CHUNK, DN_CHUNK), DN_CHUNK)


def _delta_prep(q, k, v, gc, grow, beta, *, name):
    t = q.shape[0]
    tm = _tm(t)
    cpb = tm // DN_CHUNK
    n_chunks = t // DN_CHUNK
    group = 2

    def body(q_ref, k_ref, v_ref, gc_ref, gr_ref, b_ref, w_ref, u_ref, qd_ref, kd_ref, qk_ref, egl_ref, inv_ref):
        def step(m, carry):
            probs = [(m * group + e, h) for e in range(group) for h in range(DN_HEADS)]
            gcb = [gc_ref[_chunk_rows(m * group + e), :] for e in range(group)]
            bb = [b_ref[_chunk_rows(m * group + e), :] for e in range(group)]
            grb = [gr_ref[m * group + e] for e in range(group)]
            for e in range(group):
                egl_ref[m * group + e] = jnp.zeros((HALO, LANES), F32)
            outs, invs = _prep_fn(
                [q_ref[_chunk_rows(n), _head_cols(h)] for n, h in probs], [k_ref[_chunk_rows(n), _head_cols(h)] for n, h in probs],
                [v_ref[_chunk_rows(n), _head_cols(h)] for n, h in probs],
                [_lane_col(gcb[e], h) for e in range(group) for h in range(DN_HEADS)],
                [grb[e][h:h + 1, :] for e in range(group) for h in range(DN_HEADS)],
                [_lane_col(bb[e], h) for e in range(group) for h in range(DN_HEADS)], None)
            for (n, h), (w, u, qd, kd, qk, egl), inv in zip(probs, outs, invs):
                rows, cols = _chunk_rows(n), _head_cols(h)
                w_ref[rows, cols] = w.astype(BF16)
                u_ref[rows, cols] = u
                qd_ref[rows, cols] = qd.astype(BF16)
                kd_ref[rows, cols] = kd.astype(BF16)
                qk_ref[n, h] = qk
                inv_ref[n, h] = inv
                egl_ref[n, h:h + 1, :] = egl
            return carry

        lax.fori_loop(0, cpb // group, step, 0)

    row = lambda i: (i, 0)
    tok = pl.BlockSpec((tm, DN_WIDTH), row)
    lanes = pl.BlockSpec((tm, LANES), row)
    sq = pl.BlockSpec((cpb, DN_HEADS, DN_CHUNK, DN_CHUNK), lambda i: (i, 0, 0, 0))
    return pl.pallas_call(
        body, name=name, grid=(t // tm,),
        in_specs=[tok, tok, tok, lanes, pl.BlockSpec((cpb, HALO, DN_CHUNK), lambda i: (i, 0, 0)), lanes],
        out_specs=(tok, tok, tok, tok, sq, pl.BlockSpec((cpb, HALO, LANES), lambda i: (i, 0, 0)), sq),
        out_shape=(jax.ShapeDtypeStruct((t, DN_WIDTH), BF16), jax.ShapeDtypeStruct((t, DN_WIDTH), F32),
                   jax.ShapeDtypeStruct((t, DN_WIDTH), BF16), jax.ShapeDtypeStruct((t, DN_WIDTH), BF16),
                   jax.ShapeDtypeStruct((n_chunks, DN_HEADS, DN_CHUNK, DN_CHUNK), F32),
                   jax.ShapeDtypeStruct((n_chunks, HALO, LANES), F32),
                   jax.ShapeDtypeStruct((n_chunks, DN_HEADS, DN_CHUNK, DN_CHUNK), F32)),
        compiler_params=_cparams(("parallel",)),
    )(q, k, v, gc, grow, beta)


def _delta_par_bwd(q, k, v, gc, grow, beta, inv, dw, du, dqd, dkd, dqk, degl, *, name):
    t = q.shape[0]
    tm = _tm(t)
    cpb = tm // DN_CHUNK
    n_chunks = t // DN_CHUNK
    group = 2

    def body(q_ref, k_ref, v_ref, gc_ref, gr_ref, b_ref, inv_ref, dw_ref, du_ref, dqd_ref, dkd_ref, dqk_ref, degl_ref,
             dq_ref, dk_ref, dv_ref, dgc_ref, dgr_ref, db_ref):
        def step(m, carry):
            chunks = [m * group + e for e in range(group)]
            probs = [(e, h) for e in range(group) for h in range(DN_HEADS)]
            rows = [_chunk_rows(n) for n in chunks]
            gcb, bb = [gc_ref[r, :] for r in rows], [b_ref[r, :] for r in rows]
            grb, deglb = [gr_ref[n] for n in chunks], [degl_ref[n] for n in chunks]
            for n in chunks:
                dgr_ref[n] = jnp.zeros((HALO, DN_CHUNK), F32)
            invs = [inv_ref[chunks[e], h] for e, h in probs]
            _, vjp = jax.vjp(lambda *a: _prep_fn(*a, invs)[0],
                             [q_ref[rows[e], _head_cols(h)] for e, h in probs], [k_ref[rows[e], _head_cols(h)] for e, h in probs],
                             [v_ref[rows[e], _head_cols(h)] for e, h in probs], [_lane_col(gcb[e], h) for e, h in probs],
                             [grb[e][h:h + 1, :] for e, h in probs], [_lane_col(bb[e], h) for e, h in probs])
            dq, dk, dv, dgc, dgr, db = vjp([(dw_ref[rows[e], _head_cols(h)], du_ref[rows[e], _head_cols(h)],
                                             dqd_ref[rows[e], _head_cols(h)], dkd_ref[rows[e], _head_cols(h)],
                                             dqk_ref[chunks[e], h], deglb[e][h:h + 1, :]) for e, h in probs])
            dgc_acc = [jnp.zeros((DN_CHUNK, LANES), F32) for _ in chunks]
            db_acc = [jnp.zeros((DN_CHUNK, LANES), F32) for _ in chunks]
            for i, (e, h) in enumerate(probs):
                cols = _head_cols(h)
                dq_ref[rows[e], cols] = dq[i]
                dk_ref[rows[e], cols] = dk[i]
                dv_ref[rows[e], cols] = dv[i]
                dgr_ref[chunks[e], h:h + 1, :] = dgr[i]
                dgc_acc[e] = dgc_acc[e] + _col_lane(dgc[i], h)
                db_acc[e] = db_acc[e] + _col_lane(db[i], h)
            for e in range(group):
                dgc_ref[rows[e], :] = dgc_acc[e]
                db_ref[rows[e], :] = db_acc[e]
            return carry

        lax.fori_loop(0, cpb // group, step, 0)

    row = lambda i: (i, 0)
    tok = pl.BlockSpec((tm, DN_WIDTH), row)
    lanes = pl.BlockSpec((tm, LANES), row)
    sq = pl.BlockSpec((cpb, DN_HEADS, DN_CHUNK, DN_CHUNK), lambda i: (i, 0, 0, 0))
    grs = pl.BlockSpec((cpb, HALO, DN_CHUNK), lambda i: (i, 0, 0))
    return pl.pallas_call(
        body, name=name, grid=(t // tm,),
        in_specs=[tok, tok, tok, lanes, grs, lanes, sq, tok, tok, tok, tok, sq, pl.BlockSpec((cpb, HALO, LANES), lambda i: (i, 0, 0))],
        out_specs=(tok, tok, tok, lanes, grs, lanes),
        out_shape=(jax.ShapeDtypeStruct((t, DN_WIDTH), F32),) * 3
        + (jax.ShapeDtypeStruct((t, LANES), F32), jax.ShapeDtypeStruct((n_chunks, HALO, DN_CHUNK), F32),
           jax.ShapeDtypeStruct((t, LANES), F32)),
        compiler_params=_cparams(("parallel",)),
    )(q, k, v, gc, grow, beta, inv, dw, du, dqd, dkd, dqk, degl)


def _seq_specs(n_seq, seq, reverse):
    tm = _tm(seq)
    nb = seq // tm
    cpb = tm // DN_CHUNK
    pair = 2 if n_seq % 2 == 0 else 1
    blk = (lambda j: nb - 1 - j) if reverse else (lambda j: j)
    tok = pl.BlockSpec((pair, tm, DN_WIDTH), lambda b, j: (b, blk(j), 0))
    sq = pl.BlockSpec((pair, cpb, DN_HEADS, DN_CHUNK, DN_CHUNK), lambda b, j: (b, blk(j), 0, 0, 0))
    rows8 = pl.BlockSpec((pair, cpb, HALO, LANES), lambda b, j: (b, blk(j), 0, 0))
    state = pl.BlockSpec((pair, cpb, DN_HEADS, DN_HEAD_DIM, DN_HEAD_DIM), lambda b, j: (b, blk(j), 0, 0, 0))
    return nb, cpb, pair, tok, sq, rows8, state


def _by_seq(a, n_seq):
    return a.reshape((n_seq, a.shape[0] // n_seq) + a.shape[1:])


def _flat_seq(a):
    return a.reshape((a.shape[0] * a.shape[1],) + a.shape[2:])


def _delta_seq_fwd(w, u, qd, kd, qk, egl, n_seq, seq, *, name):
    nb, cpb, pair, tok, sq, rows8, state = _seq_specs(n_seq, seq, False)
    probs = [(e, h) for e in range(pair) for h in range(DN_HEADS)]

    def body(w_ref, u_ref, qd_ref, kd_ref, qk_ref, egl_ref, o_ref, st_ref, s_s):
        @pl.when(pl.program_id(1) == 0)
        def _():
            s_s[...] = jnp.zeros_like(s_s)

        def step(n, carry):
            rows = _chunk_rows(n)
            eglb = [egl_ref[e, n] for e in range(pair)]
            s = [s_s[e, h] for e, h in probs]
            for (e, h), s_eh in zip(probs, s):
                st_ref[e, n, h] = s_eh
            o, s_new = _seq_fn([w_ref[e, rows, _head_cols(h)] for e, h in probs], [u_ref[e, rows, _head_cols(h)] for e, h in probs],
                               [qd_ref[e, rows, _head_cols(h)] for e, h in probs], [kd_ref[e, rows, _head_cols(h)] for e, h in probs],
                               [qk_ref[e, n, h] for e, h in probs], [eglb[e][h:h + 1, :] for e, h in probs], s)
            for i, (e, h) in enumerate(probs):
                o_ref[e, rows, _head_cols(h)] = o[i]
                s_s[e, h] = s_new[i]
            return carry

        lax.fori_loop(0, cpb, step, 0)

    o, states = pl.pallas_call(
        body, name=name, grid=(n_seq // pair, nb),
        in_specs=[tok, tok, tok, tok, sq, rows8],
        out_specs=(tok, state),
        out_shape=(jax.ShapeDtypeStruct((n_seq, seq, DN_WIDTH), F32),
                   jax.ShapeDtypeStruct((n_seq, seq // DN_CHUNK, DN_HEADS, DN_HEAD_DIM, DN_HEAD_DIM), F32)),
        scratch_shapes=[pltpu.VMEM((pair, DN_HEADS, DN_HEAD_DIM, DN_HEAD_DIM), F32)],
        compiler_params=_cparams(("parallel", "arbitrary")),
    )(*[_by_seq(a, n_seq) for a in (w, u, qd, kd, qk, egl)])
    return _flat_seq(o), _flat_seq(states)


def _delta_seq_bwd(w, u, qd, kd, qk, egl, states, do, n_seq, seq, *, name):
    nb, cpb, pair, tok, sq, rows8, state = _seq_specs(n_seq, seq, True)
    probs = [(e, h) for e in range(pair) for h in range(DN_HEADS)]

    def body(w_ref, u_ref, qd_ref, kd_ref, qk_ref, egl_ref, st_ref, do_ref, dw_ref, du_ref, dqd_ref, dkd_ref, dqk_ref,
             degl_ref, ds_s):
        @pl.when(pl.program_id(1) == 0)
        def _():
            ds_s[...] = jnp.zeros_like(ds_s)

        def step(m, carry):
            n = cpb - 1 - m
            rows = _chunk_rows(n)
            eglb = [egl_ref[e, n] for e in range(pair)]
            for e in range(pair):
                degl_ref[e, n] = jnp.zeros((HALO, LANES), F32)
            _, vjp = jax.vjp(_seq_fn, [w_ref[e, rows, _head_cols(h)].astype(F32) for e, h in probs],
                             [u_ref[e, rows, _head_cols(h)] for e, h in probs],
                             [qd_ref[e, rows, _head_cols(h)].astype(F32) for e, h in probs],
                             [kd_ref[e, rows, _head_cols(h)].astype(F32) for e, h in probs],
                             [qk_ref[e, n, h] for e, h in probs], [eglb[e][h:h + 1, :] for e, h in probs],
                             [st_ref[e, n, h] for e, h in probs])
            dw, du, dqd, dkd, dqk, degl, ds_in = vjp(([do_ref[e, rows, _head_cols(h)] for e, h in probs],
                                                      [ds_s[e, h] for e, h in probs]))
            for i, (e, h) in enumerate(probs):
                cols = _head_cols(h)
                dw_ref[e, rows, cols] = dw[i]
                du_ref[e, rows, cols] = du[i]
                dqd_ref[e, rows, cols] = dqd[i]
                dkd_ref[e, rows, cols] = dkd[i]
                dqk_ref[e, n, h] = dqk[i]
                degl_ref[e, n, h:h + 1, :] = degl[i]
                ds_s[e, h] = ds_in[i]
            return carry

        lax.fori_loop(0, cpb, step, 0)

    nc = seq // DN_CHUNK
    outs = pl.pallas_call(
        body, name=name, grid=(n_seq // pair, nb),
        in_specs=[tok, tok, tok, tok, sq, rows8, state, tok],
        out_specs=(tok, tok, tok, tok, sq, rows8),
        out_shape=(jax.ShapeDtypeStruct((n_seq, seq, DN_WIDTH), F32),) * 4
        + (jax.ShapeDtypeStruct((n_seq, nc, DN_HEADS, DN_CHUNK, DN_CHUNK), F32),
           jax.ShapeDtypeStruct((n_seq, nc, HALO, LANES), F32)),
        scratch_shapes=[pltpu.VMEM((pair, DN_HEADS, DN_HEAD_DIM, DN_HEAD_DIM), F32)],
        compiler_params=_cparams(("parallel", "arbitrary")),
    )(*[_by_seq(a, n_seq) for a in (w, u, qd, kd, qk, egl, states, do)])
    return tuple(_flat_seq(a) for a in outs)


def _dn_gate(o, z, dnw):
    return o * lax.rsqrt(jnp.mean(o * o, axis=-1, keepdims=True) + EPS) * dnw * _silu(z)


def _mix_out_fwd(x, sg, o, z, wo_sg, wo_dn, dnw, *, name):
    t = x.shape[0]
    tm = _tm(t)

    def body(x_ref, sg_ref, o_ref, z_ref, wsg_ref, wdn_ref, dnw_ref, y_ref, dn_s):
        for h, (oh, zh) in enumerate(zip(_split_heads(o_ref, 0), _split_heads(z_ref, 0))):
            dn_s[:, h * DN_HEAD_DIM:(h + 1) * DN_HEAD_DIM] = _dn_gate(oh, zh, dnw_ref[...]).astype(BF16)
        y_ref[...] = (x_ref[...] + jnp.dot(sg_ref[...].astype(BF16), wsg_ref[...], preferred_element_type=F32)
                      + jnp.dot(dn_s[...], wdn_ref[...], preferred_element_type=F32))

    row = lambda i: (i, 0)
    const = lambda i: (0, 0)
    half = pl.BlockSpec((tm, DN_WIDTH), row)
    return pl.pallas_call(
        body, name=name, grid=(t // tm,),
        in_specs=[pl.BlockSpec((tm, D_MODEL), row), half, half, half, pl.BlockSpec((SG_WIDTH, D_MODEL), const),
                  pl.BlockSpec((DN_WIDTH, D_MODEL), const), pl.BlockSpec((1, DN_HEAD_DIM), const)],
        out_specs=pl.BlockSpec((tm, D_MODEL), row),
        out_shape=jax.ShapeDtypeStruct((t, D_MODEL), F32),
        scratch_shapes=[pltpu.VMEM((tm, DN_WIDTH), BF16)],
        compiler_params=_cparams(("parallel",)),
    )(x, sg, o, z, wo_sg, wo_dn, dnw)


def _mix_out_bwd(dy, sg, o, z, wo_sg, wo_dn, dnw, *, name):
    t = dy.shape[0]
    tm = _tm(t)

    def body(dy_ref, sg_ref, o_ref, z_ref, wsg_ref, wdn_ref, dnw_ref, dsg_ref, do_ref, dz_ref, dwsg_ref, dwdn_ref, ddnw_ref, dn_s):
        i = pl.program_id(0)
        dyb = dy_ref[...].astype(BF16)
        nt = (((1,), (1,)), ((), ()))
        tn = (((0,), (0,)), ((), ()))
        dsg_ref[...] = lax.dot_general(dyb, wsg_ref[...], nt, preferred_element_type=F32)
        ddn = lax.dot_general(dyb, wdn_ref[...], nt, preferred_element_type=F32)
        ddnw = None
        for h, (oh, zh) in enumerate(zip(_split_heads(o_ref, 0), _split_heads(z_ref, 0))):
            cols = slice(h * DN_HEAD_DIM, (h + 1) * DN_HEAD_DIM)
            out, vjp = jax.vjp(_dn_gate, oh, zh, dnw_ref[...])
            dn_s[:, cols] = out.astype(BF16)
            doh, dzh, dw = vjp(ddn[:, cols])
            do_ref[:, cols] = doh
            dz_ref[:, cols] = dzh.astype(BF16)
            ddnw = dw if ddnw is None else ddnw + dw
        _acc_out(ddnw_ref, i == 0, ddnw)
        _acc_out(dwsg_ref, i == 0, lax.dot_general(sg_ref[...].astype(BF16), dyb, tn, preferred_element_type=F32))
        _acc_out(dwdn_ref, i == 0, lax.dot_general(dn_s[...], dyb, tn, preferred_element_type=F32))

    row = lambda i: (i, 0)
    const = lambda i: (0, 0)
    half = pl.BlockSpec((tm, DN_WIDTH), row)
    wspec = pl.BlockSpec((DN_WIDTH, D_MODEL), const)
    return pl.pallas_call(
        body, name=name, grid=(t // tm,),
        in_specs=[pl.BlockSpec((tm, D_MODEL), row), half, half, half, wspec, wspec, pl.BlockSpec((1, DN_HEAD_DIM), const)],
        out_specs=(half, half, half, wspec, wspec, pl.BlockSpec((1, DN_HEAD_DIM), const)),
        out_shape=(jax.ShapeDtypeStruct((t, DN_WIDTH), F32),) * 2 + (jax.ShapeDtypeStruct((t, DN_WIDTH), BF16),)
        + (jax.ShapeDtypeStruct((DN_WIDTH, D_MODEL), F32),) * 2 + (jax.ShapeDtypeStruct((1, DN_HEAD_DIM), F32),),
        scratch_shapes=[pltpu.VMEM((tm, DN_WIDTH), BF16)],
        compiler_params=_cparams(("arbitrary",)),
    )(dy, sg, o, z, wo_sg, wo_dn, dnw)


_MESH = pl.DeviceIdType.MESH
_HBM = pl.BlockSpec(memory_space=pl.ANY)


def _mesh_pos():
    x, y, c = lax.axis_index("x"), lax.axis_index("y"), lax.axis_index("c")
    return x, y, c, [(1 - x, y), (x, 1 - y), (1 - x, 1 - y)]


def _gather2(arrs, *, name):
    n = len(arrs)
    slots = N_DEV - 1

    def body(*refs):
        in_refs, out_refs = refs[:n], refs[n:2 * n]
        send_sems, recv_sems, local_sems = refs[2 * n:]
        x, y, c, chips = _mesh_pos()
        me, sibling = (x, y, c), (x, y, 1 - c)

        def copy(k, slot, block, to, src=None):
            dst = out_refs[k].at[4 * block[0] + 2 * block[1] + block[2]]
            return pltpu.make_async_remote_copy(src_ref=dst if src is None else src, dst_ref=dst,
                                                send_sem=send_sems.at[k * slots + slot], recv_sem=recv_sems.at[k * slots + slot],
                                                device_id=to, device_id_type=_MESH)

        local = [pltpu.make_async_copy(in_refs[k], out_refs[k].at[4 * x + 2 * y + c], local_sems.at[k]) for k in range(n)]
        sent = []
        for k in range(n):
            sent.append(copy(k, 0, me, sibling, src=in_refs[k]))
            sent += [copy(k, 1 + j, me, (*chip, c), src=in_refs[k]) for j, chip in enumerate(chips)]
        for cp in local + sent:
            cp.start()
        for j, chip in enumerate(chips):
            for k in range(n):
                copy(k, 1 + j, (*chip, c), me).wait_recv()
                passed = copy(k, 4 + j, (*chip, c), sibling)
                passed.start()
                sent.append(passed)
        for k in range(n):
            copy(k, 0, sibling, me).wait_recv()
            for j, chip in enumerate(chips):
                copy(k, 4 + j, (*chip, 1 - c), me).wait_recv()
        for cp in sent:
            cp.wait_send()
        for cp in local:
            cp.wait()

    return pl.pallas_call(
        body, name=name, in_specs=[_HBM] * n, out_specs=(_HBM,) * n,
        out_shape=tuple(jax.ShapeDtypeStruct((N_DEV,) + a.shape, a.dtype) for a in arrs),
        scratch_shapes=[pltpu.SemaphoreType.DMA((n * slots,)), pltpu.SemaphoreType.DMA((n * slots,)),
                        pltpu.SemaphoreType.DMA((n,))],
    )(*arrs)


_SEM = pl.BlockSpec(memory_space=pltpu.SEMAPHORE)
_EFFECT = pltpu.SideEffectType.DATAFLOW_SIDE_EFFECTING


def _direct_copies(src_refs, land_refs, send_sems, recv_sems, gather):
    x, y, c, _ = _mesh_pos()
    me = 4 * x + 2 * y + c
    n_peer = N_DEV - 1
    copies = []
    for r in range(1, N_DEV):
        px = 1 - x if r & 4 else x
        py = 1 - y if r & 2 else y
        pc = 1 - c if r & 1 else c
        for k, (src, land) in enumerate(zip(src_refs, land_refs)):
            copies.append(pltpu.make_async_remote_copy(
                src_ref=src if gather else src.at[4 * px + 2 * py + pc], dst_ref=land.at[me],
                send_sem=send_sems.at[k * n_peer + r - 1], recv_sem=recv_sems.at[k * n_peer + r - 1],
                device_id=(px, py, pc), device_id_type=_MESH))
    return copies


def _send_start(arrs, gather, after=None, *, name):
    n = len(arrs)
    lands = [lax.empty(((N_DEV,) + a.shape) if gather else a.shape, a.dtype) for a in arrs]
    n_in = 2 * n + (0 if after is None else 1)

    def body(*refs):
        src_refs, land_refs, send_sems, recv_sems, token = refs[:n], refs[n:2 * n], refs[n_in], refs[n_in + 1], refs[-1]
        for cp in _direct_copies(src_refs, land_refs, send_sems, recv_sems, gather):
            cp.start()
        token[...] = jnp.zeros_like(token)

    n_sem = n * (N_DEV - 1)
    bufs = list(arrs) + lands
    out = pl.pallas_call(
        body, name=name,
        out_shape=(pltpu.SemaphoreType.DMA((n_sem,)), pltpu.SemaphoreType.DMA((n_sem,)))
        + tuple(pltpu.HBM(b.shape, b.dtype) for b in bufs) + (jax.ShapeDtypeStruct((HALO, LANES), F32),),
        in_specs=[_HBM] * n_in, out_specs=(_SEM, _SEM) + (_HBM,) * (2 * n) + (pl.BlockSpec(memory_space=pltpu.VMEM),),
        input_output_aliases={i: 2 + i for i in range(2 * n)},
        compiler_params=pltpu.CompilerParams(has_side_effects=_EFFECT),
    )(*[pltpu.with_memory_space_constraint(b, pltpu.HBM) for b in bufs], *([] if after is None else [after]))
    return (out[0], out[1], list(out[2:2 + n]), list(out[2 + n:2 + 2 * n])), out[-1]


def _send_wait(started, gather, after, *, name):
    send_sems, recv_sems, srcs, lands = started
    n = len(srcs)

    def body(*refs):
        src_refs, land_refs, send_ref, recv_ref = refs[:n], refs[n:2 * n], refs[2 * n], refs[2 * n + 1]
        for cp in _direct_copies(src_refs, land_refs, send_ref, recv_ref, gather):
            cp.wait_send()
            cp.wait_recv()

    bufs = srcs + lands
    out = pl.pallas_call(
        body, name=name, out_shape=tuple(pltpu.HBM(b.shape, b.dtype) for b in bufs),
        in_specs=[_HBM] * (2 * n) + [_SEM, _SEM, _HBM], out_specs=(_HBM,) * (2 * n),
        input_output_aliases={i: i for i in range(2 * n)},
        compiler_params=pltpu.CompilerParams(has_side_effects=_EFFECT),
    )(*bufs, send_sems, recv_sems, after)
    return list(out[n:])


def _row_block(rows, limit=256):
    best = rows
    for cand in range(8, limit + 1, 8):
        if rows % cand == 0:
            best = cand
    return best if rows > limit else rows


def _adam(gp, w, m, v, *, name):
    p, rows, cols = gp.shape
    rb = _row_block(rows)

    def body(gp_ref, w_ref, m_ref, v_ref, g_ref, d_ref, m2_ref, v2_ref):
        g = gp_ref[0].astype(F32)
        for s in range(1, p):
            g = g + gp_ref[s].astype(F32)
        m2 = ADAM_B1 * m_ref[...] + (1.0 - ADAM_B1) * g
        v2 = ADAM_B2 * v_ref[...] + (1.0 - ADAM_B2) * (g * g)
        m_hat = m2 / (1.0 - ADAM_B1 ** ADAM_STEP)
        v_hat = v2 / (1.0 - ADAM_B2 ** ADAM_STEP)
        g_ref[...] = g
        d_ref[...] = -ADAM_LR * (m_hat / (jnp.sqrt(v_hat) + ADAM_EPS) + ADAM_WD * w_ref[...])
        m2_ref[...] = m2
        v2_ref[...] = v2

    blk = pl.BlockSpec((rb, cols), lambda i: (i, 0))
    return pl.pallas_call(
        body, name=name, grid=(rows // rb,),
        in_specs=[pl.BlockSpec((p, rb, cols), lambda i: (0, i, 0)), blk, blk, blk],
        out_specs=(blk,) * 4, out_shape=(jax.ShapeDtypeStruct((rows, cols), F32),) * 4,
        compiler_params=_cparams(("parallel",)),
    )(gp, w, m, v)


def _cols_full(g):
    return jnp.transpose(g, (1, 0, 2)).reshape(g.shape[1], N_DEV * g.shape[2])


def _pad_lanes(a, width=LANES):
    return jnp.pad(a, ((0, 0), (0, width - a.shape[1])))


def _chunk_rows_of(a):
    by_chunk = jnp.transpose(a[:, :DN_HEADS].reshape(-1, DN_CHUNK, DN_HEADS), (0, 2, 1))
    return jnp.pad(by_chunk, ((0, 0), (0, HALO - DN_HEADS), (0, 0)))


_SMALL = (("ffn1_norm", D_MODEL), ("mix_norm", D_MODEL), ("ffn2_norm", D_MODEL), ("final_norm", D_MODEL), ("a_log", DN_HEADS),
          ("dt_bias", DN_HEADS), ("dn_norm", DN_HEAD_DIM), ("sg_ln_g", SG_WIDTH), ("sg_ln_b", SG_WIDTH),
          ("sg_w", SG_GROUPS * SG_CHUNK * SG_CHUNK), ("sg_b", SG_GROUPS * SG_CHUNK), ("conv_w", CONV_K * 3 * DN_WIDTH))
_SMALL_ROWS = 1128
_SMALL_SHAPES = {"ffn1_norm": (1, D_MODEL), "mix_norm": (1, D_MODEL), "ffn2_norm": (1, D_MODEL), "final_norm": (D_MODEL,),
                 "a_log": (1, DN_HEADS), "dt_bias": (1, DN_HEADS), "dn_norm": (1, DN_HEAD_DIM), "sg_ln_g": (1, SG_WIDTH),
                 "sg_ln_b": (1, SG_WIDTH), "sg_w": (1, SG_GROUPS, SG_CHUNK, SG_CHUNK), "sg_b": (1, SG_GROUPS, SG_CHUNK)}


def _pack_small(d):
    flat = jnp.concatenate([d[name].reshape(-1) for name, _ in _SMALL])
    return jnp.pad(flat, (0, _SMALL_ROWS * LANES - flat.shape[0])).reshape(_SMALL_ROWS, LANES)


def _unpack_small(a):
    flat, out, at = a.reshape(-1), {}, 0
    for name, size in _SMALL:
        out[name] = flat[at:at + size]
        at += size
    return out


def kernel(x, ffn1_norm, ffn1_w_gate, ffn1_w_up, ffn1_w_down, mix_norm, w_in, conv_w, a_log, dt_bias, dn_norm, sg_ln_g, sg_ln_b, sg_w, sg_b, w_out, ffn2_norm, ffn2_w_gate, ffn2_w_up, ffn2_w_down, final_norm, loss_target, m_ffn1_norm, m_ffn1_w_gate, m_ffn1_w_up, m_ffn1_w_down, m_mix_norm, m_w_in, m_conv_w, m_a_log, m_dt_bias, m_dn_norm, m_sg_ln_g, m_sg_ln_b, m_sg_w, m_sg_b, m_w_out, m_ffn2_norm, m_ffn2_w_gate, m_ffn2_w_up, m_ffn2_w_down, m_final_norm, v_ffn1_norm, v_ffn1_w_gate, v_ffn1_w_up, v_ffn1_w_down, v_mix_norm, v_w_in, v_conv_w, v_a_log, v_dt_bias, v_dn_norm, v_sg_ln_g, v_sg_ln_b, v_sg_w, v_sg_b, v_w_out, v_ffn2_norm, v_ffn2_w_gate, v_ffn2_w_up, v_ffn2_w_down, v_final_norm):
    weights = dict(ffn1_norm=ffn1_norm, ffn1_w_gate=ffn1_w_gate, ffn1_w_up=ffn1_w_up, ffn1_w_down=ffn1_w_down, mix_norm=mix_norm, w_in=w_in, conv_w=conv_w, a_log=a_log, dt_bias=dt_bias, dn_norm=dn_norm, sg_ln_g=sg_ln_g, sg_ln_b=sg_ln_b, sg_w=sg_w, sg_b=sg_b, w_out=w_out, ffn2_norm=ffn2_norm, ffn2_w_gate=ffn2_w_gate, ffn2_w_up=ffn2_w_up, ffn2_w_down=ffn2_w_down, final_norm=final_norm)
    mom_m = dict(ffn1_norm=m_ffn1_norm, ffn1_w_gate=m_ffn1_w_gate, ffn1_w_up=m_ffn1_w_up, ffn1_w_down=m_ffn1_w_down, mix_norm=m_mix_norm, w_in=m_w_in, conv_w=m_conv_w, a_log=m_a_log, dt_bias=m_dt_bias, dn_norm=m_dn_norm, sg_ln_g=m_sg_ln_g, sg_ln_b=m_sg_ln_b, sg_w=m_sg_w, sg_b=m_sg_b, w_out=m_w_out, ffn2_norm=m_ffn2_norm, ffn2_w_gate=m_ffn2_w_gate, ffn2_w_up=m_ffn2_w_up, ffn2_w_down=m_ffn2_w_down, final_norm=m_final_norm)
    mom_v = dict(ffn1_norm=v_ffn1_norm, ffn1_w_gate=v_ffn1_w_gate, ffn1_w_up=v_ffn1_w_up, ffn1_w_down=v_ffn1_w_down, mix_norm=v_mix_norm, w_in=v_w_in, conv_w=v_conv_w, a_log=v_a_log, dt_bias=v_dt_bias, dn_norm=v_dn_norm, sg_ln_g=v_sg_ln_g, sg_ln_b=v_sg_ln_b, sg_w=v_sg_w, sg_b=v_sg_b, w_out=v_w_out, ffn2_norm=v_ffn2_norm, ffn2_w_gate=v_ffn2_w_gate, ffn2_w_up=v_ffn2_w_up, ffn2_w_down=v_ffn2_w_down, final_norm=v_final_norm)
    order = list(weights)
    big = ("ffn1_w_gate", "ffn1_w_up", "ffn1_w_down", "w_in", "w_out", "ffn2_w_gate", "ffn2_w_up", "ffn2_w_down")
    col_sharded = ("ffn1_w_gate", "ffn1_w_up", "w_in", "ffn2_w_gate", "ffn2_w_up")

    n_seq, seq, _ = x.shape
    t = n_seq * seq
    me = 4 * lax.axis_index("x") + 2 * lax.axis_index("y") + lax.axis_index("c")
    x0 = x.reshape(t, D_MODEL)
    tgt = loss_target.reshape(t, D_MODEL)

    def fill_own(land, own_block):
        return lax.dynamic_update_index_in_dim(land, own_block, me, 0)

    def rows_view(n, a):
        return jnp.transpose(a) if n in col_sharded else a

    def as_full(n, g):
        return g.reshape(-1, g.shape[-1])

    shards = {n: rows_view(n, weights[n][0]).astype(BF16) for n in big}
    ffn1_names, mix_names, ffn2_names = big[:3], big[3:5], big[5:]
    full = {n: as_full(n, g) for n, g in zip(ffn1_names, _gather2([shards[n] for n in ffn1_names], name="gather_ffn1"))}
    mix_srcs = [shards[n] for n in mix_names] + [conv_w[0]]
    mix_started, mix_token = _send_start(mix_srcs, True, full[ffn1_names[2]], name="gather_mix_start")
    ffn2_started, ffn2_token = _send_start([shards[n] for n in ffn2_names], True, mix_token, name="gather_ffn2_start")
    ffn1_norm_fwd = ffn1_norm + ffn2_token[:1, :1]
    alog, dtb = _pad_lanes(a_log), _pad_lanes(dt_bias)
    sgbt = _pad_lanes(sg_b[0].T)
    fnw = final_norm.reshape(1, D_MODEL)

    x1, h1, g1, u1 = _ffn_fwd(x0, ffn1_norm_fwd, full["ffn1_w_gate"], full["ffn1_w_up"], full["ffn1_w_down"], name="ffn1_fwd")
    mix_lands = [fill_own(land, src) for land, src in zip(_send_wait(mix_started, True, x1, name="gather_mix_wait"), mix_srcs)]
    full.update({n: as_full(n, g) for n, g in zip(mix_names, mix_lands)})
    conv_full = _cols_full(mix_lands[-1])
    w_in_t = full["w_in"]
    offs = (0, SG_WIDTH, 2 * SG_WIDTH, 2 * SG_WIDTH + 3 * DN_WIDTH, 2 * SG_WIDTH + 4 * DN_WIDTH)
    n_proj = offs[-1]

    def pad_rows(a):
        return jnp.pad(a, ((0, LANES - a.shape[0]), (0, 0)))

    ws = [w_in_t[offs[0]:offs[1]], w_in_t[offs[1]:offs[2]], w_in_t[offs[2]:offs[3]], w_in_t[offs[3]:offs[4]],
          pad_rows(w_in_t[n_proj:n_proj + DN_HEADS]), pad_rows(w_in_t[n_proj + DN_HEADS:n_proj + 2 * DN_HEADS])]
    wo_sg, wo_dn = full["w_out"][:SG_WIDTH], full["w_out"][SG_WIDTH:]
    u, v, qkv, z, bpre, apre = _mix_in_fwd(x1, mix_norm, ws, name="mix_in_fwd")
    sg_out = _sg_fwd(u, v, sg_ln_g, sg_ln_b, sg_w[0], sgbt, name="sg_fwd")
    q, k, vv, beta, gc = _dn_prep_fwd(qkv, bpre, apre, conv_full, alog, dtb, seq, name="dn_prep_fwd")
    grow = _chunk_rows_of(gc)
    wy_w, wy_u, q_dec, k_dec, qk, egl, inv = _delta_prep(q, k, vv, gc, grow, beta, name="delta_prep")
    o, states = _delta_seq_fwd(wy_w, wy_u, q_dec, k_dec, qk, egl, n_seq, seq, name="delta_seq_fwd")
    x2 = _mix_out_fwd(x1, sg_out, o, z, wo_sg, wo_dn, dn_norm, name="mix_out_fwd")
    ffn2_lands = _send_wait(ffn2_started, True, x2, name="gather_ffn2_wait")
    full.update({n: as_full(n, fill_own(land, shards[n])) for n, land in zip(ffn2_names, ffn2_lands)})
    dx3, loss_part, d_fn, h2, g2, u2 = _ffn_fwd(x2, ffn2_norm, full["ffn2_w_gate"], full["ffn2_w_up"], full["ffn2_w_down"],
                                                tgt, fnw, name="ffn2_fwd_loss")
    loss = lax.psum(loss_part[0, 0], ("x", "y", "c"))

    dx2, d_n2, d_g2, d_u2, d_d2 = _ffn_bwd(x2, ffn2_norm, h2, g2, u2, full["ffn2_w_gate"], full["ffn2_w_up"],
                                           full["ffn2_w_down"], dx3, name="ffn2_bwd")
    def by_owner(d_rows):
        return d_rows.reshape(N_DEV, -1, D_MODEL)

    ffn2_pieces = [by_owner(d_g2), by_owner(d_u2), by_owner(d_d2)]
    ffn2_sent, sent_token = _send_start(ffn2_pieces, False, name="grads_ffn2_start")
    dsg, do, dz, d_wo_sg, d_wo_dn, d_dnw = _mix_out_bwd(dx2, sg_out, o, z, wo_sg, wo_dn, dn_norm + sent_token[:1, :1],
                                                        name="mix_out_bwd")
    d_seq = _delta_seq_bwd(wy_w, wy_u, q_dec, k_dec, qk, egl, states, do, n_seq, seq, name="delta_seq_bwd")
    dq, dk, dv, dgc_a, dgrow, dbeta = _delta_par_bwd(q, k, vv, gc, grow, beta, inv, *d_seq, name="delta_par_bwd")
    dgc_b = _pad_lanes(jnp.transpose(dgrow[:, :DN_HEADS, :], (0, 2, 1)).reshape(t, DN_HEADS))
    dqkv, d_conv, dbpre, dapre, d_alog, d_dtb = _dn_prep_bwd(qkv, bpre, apre, conv_full, alog, dtb, dq, dk, dv, dbeta, dgc_a,
                                                             dgc_b, seq, name="dn_prep_bwd")
    du, dvv, d_lng, d_lnb, d_wc, d_sgbt = _sg_bwd(u, v, sg_ln_g, sg_ln_b, sg_w[0], sgbt, dsg, name="sg_bwd")
    dx1, d_mixn, d_wp = _mix_in_bwd(x1, mix_norm, ws, dx2, (du, dvv, dqkv, dz, dbpre, dapre), name="mix_in_bwd")
    d_w_in_t = jnp.concatenate([d_wp[:n_proj], d_wp[_PROJ_OFFSETS[4]:_PROJ_OFFSETS[4] + DN_HEADS],
                                d_wp[_PROJ_OFFSETS[5]:_PROJ_OFFSETS[5] + DN_HEADS]], axis=0)
    d_w_out = jnp.concatenate([d_wo_sg, d_wo_dn], axis=0)
    mix_pieces = [by_owner(d_w_in_t), by_owner(d_w_out).astype(BF16)]
    mix_sent, sent_token = _send_start(mix_pieces, False, name="grads_mix_start")
    grad_x, d_n1, dg1, du1, a1, dyh1 = _ffn_bwd_x(x0, ffn1_norm + sent_token[:1, :1], g1, u1, full["ffn1_w_gate"],
                                                  full["ffn1_w_up"], full["ffn1_w_down"], dx1, name="ffn1_bwd_x")
    small_grads = dict(ffn1_norm=d_n1, mix_norm=d_mixn, ffn2_norm=d_n2, final_norm=d_fn, a_log=d_alog[:, :DN_HEADS],
                       dt_bias=d_dtb[:, :DN_HEADS], dn_norm=d_dnw, sg_ln_g=d_lng, sg_ln_b=d_lnb, sg_w=d_wc,
                       sg_b=d_sgbt[:, :SG_GROUPS].T, conv_w=d_conv[:CONV_K])
    small_src = _pack_small(small_grads)
    small_sent, small_token = _send_start([small_src], True, name="small_grads_start")
    late, tokens = [], []

    def send_early(k, grad):
        piece = by_owner(grad)
        sent, token = _send_start([piece], False, name="grads_" + ffn1_names[k] + "_start")
        late.append(((ffn1_names[k],), sent, [piece]))
        tokens.append(token)
        return token

    _ffn_wgrads(h1, dg1, du1, a1, dyh1, send_early, small_token, name="ffn1_bwd")

    res = {}
    after = tokens[-1]

    def update(names, sent, pieces, after):
        lands = _send_wait(sent, False, after, name="grads_" + names[0] + "_wait")
        for n, land, p in zip(names, lands, pieces):
            got = fill_own(land, lax.dynamic_index_in_dim(p, me, 0, keepdims=False))
            upd = _adam(got, *[rows_view(n, src[n][0]) for src in (weights, mom_m, mom_v)], name="adam_" + n)
            res[n] = [rows_view(n, a) for a in upd]
            after = upd[0]
        return after

    for group in [(ffn2_names, ffn2_sent, ffn2_pieces), (mix_names, mix_sent, mix_pieces)] + late[:-1]:
        after = update(*group, after)
    (small_land,) = _send_wait(small_sent, True, after, name="small_grads_wait")
    small_parts = fill_own(small_land, small_src)
    zeros_conv = jnp.zeros((CONV_K * 3 * DN_WIDTH,), F32)
    packed = [_pack_small({**{n: src[n] for n, _ in _SMALL if n != "conv_w"}, "conv_w": zeros_conv})
              for src in (weights, mom_m, mom_v)]
    small_upd = _adam(small_parts, *packed, name="adam_small")
    small_res = [_unpack_small(a) for a in small_upd]
    conv_grad = lax.dynamic_slice_in_dim(small_res[0]["conv_w"].reshape(CONV_K, 3 * DN_WIDTH), me * (3 * DN_WIDTH // N_DEV),
                                         3 * DN_WIDTH // N_DEV, axis=1)
    res["conv_w"] = _adam(conv_grad[None], conv_w[0], m_conv_w[0], v_conv_w[0], name="adam_conv_w")
    update(*late[-1], res["conv_w"][0])

    outs = [[], [], [], []]
    for n in order:
        for kind in range(4):
            if n in res:
                outs[kind].append(res[n][kind][None])
            else:
                outs[kind].append(small_res[kind][n].reshape(_SMALL_SHAPES[n]))
    return (loss, grad_x.reshape(x.shape), *outs[0], *outs[1], *outs[2], *outs[3])
```

```python
import jax
import jax.numpy as jnp
from jax import lax
from jax.experimental import pallas as pl
from jax.experimental.pallas import tpu as pltpu

F32 = jnp.float32
BF16 = jnp.bfloat16

D_MODEL = 1024
D_FF = 2816
SG_WIDTH = 512
SG_GROUPS = 8
SG_GROUP_DIM = 64
SG_CHUNK = 128
DN_WIDTH = 512
DN_HEAD_DIM = 128
DN_HEADS = 4
DN_CHUNK = 64
CONV_K = 4
EPS = 1e-6
N_DEV = 8
LANES = 128
HALO = 8
MXU_COLS = 256

ADAM_LR = 0.001
ADAM_B1 = 0.9
ADAM_B2 = 0.999
ADAM_EPS = 1e-08
ADAM_WD = 0.01
ADAM_STEP = 10

VMEM_LIMIT = 60 * 1024 * 1024
WGRAD_K_TILE = 2048
TOKEN_BLOCK = 512
FF_BLOCK_FWD = 1408

_HI = lax.Precision.HIGHEST


def _cparams(sem):
    return pltpu.CompilerParams(dimension_semantics=sem, vmem_limit_bytes=VMEM_LIMIT)


def _tm(t, pref=TOKEN_BLOCK):
    return min(pref, t)


def _dg(a, b, ca, cb, precision):
    if precision is not None:
        return lax.dot_general(a, b, (((ca,), (cb,)), ((), ())), precision=precision, preferred_element_type=F32)
    return lax.dot_general(a.astype(BF16), b.astype(BF16), (((ca,), (cb,)), ((), ())), preferred_element_type=F32)


def _make_mm(precision):
    @jax.custom_vjp
    def mm(a, b):
        return _dg(a, b, 1, 0, precision)

    @jax.custom_vjp
    def mm_nt(a, b):
        return _dg(a, b, 1, 1, precision)

    @jax.custom_vjp
    def mm_tn(a, b):
        return _dg(a, b, 0, 0, precision)

    mm.defvjp(lambda a, b: (mm(a, b), (a, b)), lambda r, g: (mm_nt(g, r[1]), mm_tn(r[0], g)))
    mm_nt.defvjp(lambda a, b: (mm_nt(a, b), (a, b)), lambda r, g: (mm(g, r[1]), mm_tn(g, r[0])))
    mm_tn.defvjp(lambda a, b: (mm_tn(a, b), (a, b)), lambda r, g: (mm_nt(r[1], g), mm(r[0], g)))
    return mm, mm_nt, mm_tn


mm, mm_nt, mm_tn = _make_mm(None)
mmx, mmx_nt, mmx_tn = _make_mm(_HI)
mmh, mmh_nt, mmh_tn = _make_mm(lax.Precision.HIGH)


def _sigmoid(x):
    return 1.0 / (1.0 + jnp.exp(-x))


def _silu(x):
    return x * _sigmoid(x)


def _softplus(x):
    neg_abs = jnp.where(x > 0, -x, x)
    return jnp.where(x > 0, x, 0.0) + jnp.log(1.0 + jnp.exp(neg_abs))


def _gelu(x):
    return 0.5 * x * (1.0 + jnp.tanh(0.7978845608028654 * (x + 0.044715 * (x * x * x))))


def _rms_fwd(x, g):
    r = lax.rsqrt(jnp.mean(x * x, axis=-1, keepdims=True) + EPS)
    xh = x * r
    return xh * g, xh, r


def _rms_bwd(dh, xh, r, g):
    dxh = dh * g
    dx = r * (dxh - xh * jnp.mean(dxh * xh, axis=-1, keepdims=True))
    return dx, jnp.sum(dh * xh, axis=0, keepdims=True)


def _acc_out(ref, first, val):
    @pl.when(first)
    def _():
        ref[...] = val

    @pl.when(jnp.logical_not(first))
    def _():
        ref[...] += val


def _ffn_fwd(x, nw, wg, wu, wd, tgt=None, fnw=None, *, name):
    t = x.shape[0]
    tm, fb = _tm(t), FF_BLOCK_FWD
    n_t, n_f = t // tm, D_FF // fb
    with_loss = tgt is not None

    def body(*refs):
        if with_loss:
            (x_ref, nw_ref, wg_ref, wu_ref, wd_ref, tgt_ref, fnw_ref, dy_ref, loss_ref, dfn_ref, h_ref, g_ref, u_ref,
             acc_s) = refs
        else:
            x_ref, nw_ref, wg_ref, wu_ref, wd_ref, y_ref, h_ref, g_ref, u_ref, acc_s = refs
        i, j = pl.program_id(0), pl.program_id(1)

        @pl.when(j == 0)
        def _():
            h, _, _ = _rms_fwd(x_ref[...], nw_ref[...])
            h_ref[...] = h.astype(BF16)
            acc_s[...] = jnp.zeros_like(acc_s)

        h = h_ref[...]
        nt = (((1,), (1,)), ((), ()))
        g = lax.dot_general(h, wg_ref[...], nt, preferred_element_type=F32)
        u = lax.dot_general(h, wu_ref[...], nt, preferred_element_type=F32)
        g_ref[...] = g.astype(BF16)
        u_ref[...] = u.astype(BF16)
        a = _silu(g) * u
        acc_s[...] += jnp.dot(a.astype(BF16), wd_ref[...], preferred_element_type=F32)

        @pl.when(j == n_f - 1)
        def _():
            y = x_ref[...] + 0.5 * acc_s[...]
            if not with_loss:
                y_ref[...] = y
            else:
                gf = fnw_ref[...]
                out, xh, r = _rms_fwd(y, gf)
                err = out - tgt_ref[...]
                part = 0.5 * jnp.sum(jnp.mean(err * err, axis=-1, keepdims=True), axis=0, keepdims=True)
                d_out = err * (1.0 / D_MODEL)
                dy, dgf = _rms_bwd(d_out, xh, r, gf)
                dy_ref[...] = dy
                _acc_out(loss_ref, i == 0, jnp.broadcast_to(part, loss_ref.shape))
                _acc_out(dfn_ref, i == 0, dgf)

    row = lambda i, j: (i, 0)
    const = lambda i, j: (0, 0)
    in_specs = [
        pl.BlockSpec((tm, D_MODEL), row),
        pl.BlockSpec((1, D_MODEL), const),
        pl.BlockSpec((fb, D_MODEL), lambda i, j: (j, 0)),
        pl.BlockSpec((fb, D_MODEL), lambda i, j: (j, 0)),
        pl.BlockSpec((fb, D_MODEL), lambda i, j: (j, 0)),
    ]
    args = [x, nw, wg, wu, wd]
    saved_shape = (jax.ShapeDtypeStruct((t, D_MODEL), BF16), jax.ShapeDtypeStruct((t, D_FF), BF16),
                   jax.ShapeDtypeStruct((t, D_FF), BF16))
    saved_specs = (pl.BlockSpec((tm, D_MODEL), row), pl.BlockSpec((tm, fb), lambda i, j: (i, j)),
                   pl.BlockSpec((tm, fb), lambda i, j: (i, j)))
    if with_loss:
        in_specs += [pl.BlockSpec((tm, D_MODEL), row), pl.BlockSpec((1, D_MODEL), const)]
        args += [tgt, fnw]
        out_shape = (jax.ShapeDtypeStruct((t, D_MODEL), F32), jax.ShapeDtypeStruct((8, LANES), F32),
                     jax.ShapeDtypeStruct((1, D_MODEL), F32)) + saved_shape
        out_specs = (pl.BlockSpec((tm, D_MODEL), row), pl.BlockSpec((8, LANES), const),
                     pl.BlockSpec((1, D_MODEL), const)) + saved_specs
        sem = ("arbitrary", "arbitrary")
    else:
        out_shape = (jax.ShapeDtypeStruct((t, D_MODEL), F32),) + saved_shape
        out_specs = (pl.BlockSpec((tm, D_MODEL), row),) + saved_specs
        sem = ("parallel", "arbitrary")
    return pl.pallas_call(
        body, name=name, grid=(n_t, n_f), in_specs=in_specs, out_specs=out_specs, out_shape=out_shape,
        scratch_shapes=[pltpu.VMEM((tm, D_MODEL), F32)],
        compiler_params=_cparams(sem),
    )(*args)


def _ffn_bwd_x(x, nw, g, u, wg, wu, wd, dy, *, name):
    t = x.shape[0]
    tm = _tm(t, 256)

    def body(x_ref, nw_ref, g_ref, u_ref, wg_ref, wu_ref, wd_ref, dy_ref, dx_ref, dnw_ref, dg_ref, du_ref, a_ref, dyh_ref):
        i = pl.program_id(0)
        nt = (((1,), (1,)), ((), ()))
        dy = dy_ref[...]
        dyh = (0.5 * dy).astype(BF16)
        dyh_ref[...] = dyh
        gate, up = g_ref[...].astype(F32), u_ref[...].astype(F32)
        s = _sigmoid(gate)
        gs = gate * s
        da = lax.dot_general(dyh, wd_ref[...], nt, preferred_element_type=F32)
        dg = (da * up * (s + gs * (1.0 - s))).astype(BF16)
        du = (da * gs).astype(BF16)
        dg_ref[...] = dg
        du_ref[...] = du
        a_ref[...] = (gs * up).astype(BF16)
        dh = (jnp.dot(dg, wg_ref[...], preferred_element_type=F32)
              + jnp.dot(du, wu_ref[...], preferred_element_type=F32))
        xv = x_ref[...]
        r = lax.rsqrt(jnp.mean(xv * xv, axis=-1, keepdims=True) + EPS)
        dx, dnw = _rms_bwd(dh, xv * r, r, nw_ref[...])
        dx_ref[...] = dy + dx
        _acc_out(dnw_ref, i == 0, dnw)

    row = lambda i: (i, 0)
    const = lambda i: (0, 0)
    once = pl.Buffered(1)
    wide = pl.BlockSpec((tm, D_FF), row)
    return pl.pallas_call(
        body, name=name, grid=(t // tm,),
        in_specs=[pl.BlockSpec((tm, D_MODEL), row), pl.BlockSpec((1, D_MODEL), const), wide, wide,
                  pl.BlockSpec((D_FF, D_MODEL), const, pipeline_mode=once), pl.BlockSpec((D_FF, D_MODEL), const, pipeline_mode=once),
                  pl.BlockSpec((D_FF, D_MODEL), const, pipeline_mode=once), pl.BlockSpec((tm, D_MODEL), row)],
        out_specs=(pl.BlockSpec((tm, D_MODEL), row), pl.BlockSpec((1, D_MODEL), const), wide, wide, wide,
                   pl.BlockSpec((tm, D_MODEL), row)),
        out_shape=(jax.ShapeDtypeStruct((t, D_MODEL), F32), jax.ShapeDtypeStruct((1, D_MODEL), F32),
                   jax.ShapeDtypeStruct((t, D_FF), BF16), jax.ShapeDtypeStruct((t, D_FF), BF16),
                   jax.ShapeDtypeStruct((t, D_FF), BF16), jax.ShapeDtypeStruct((t, D_MODEL), BF16)),
        compiler_params=_cparams(("arbitrary",)),
    )(x, nw, g, u, wg, wu, wd, dy)


def _wgrad(a, b, bm, bn, after=None, *, name):
    k, m = a.shape
    n = b.shape[1]
    tk = _tm(k, WGRAD_K_TILE)
    n_k = k // tk

    def body(a_ref, b_ref, *rest):
        o_ref, acc_s = rest[-2], rest[-1]
        s = pl.program_id(2)
        for c in range(bn // MXU_COLS):
            cols = slice(c * MXU_COLS, (c + 1) * MXU_COLS)
            part = lax.dot_general(a_ref[...], b_ref[:, cols], (((0,), (0,)), ((), ())), preferred_element_type=F32)
            acc_s[:, cols] = jnp.where(s == 0, 0.0, acc_s[:, cols]) + part

        @pl.when(s == n_k - 1)
        def _():
            o_ref[...] = acc_s[...].astype(BF16)

    return pl.pallas_call(
        body, name=name, grid=(m // bm, n // bn, n_k),
        in_specs=[pl.BlockSpec((tk, bm), lambda i, j, s: (s, i)), pl.BlockSpec((tk, bn), lambda i, j, s: (s, j))]
        + ([] if after is None else [_HBM]),
        out_specs=pl.BlockSpec((bm, bn), lambda i, j, s: (i, j)),
        out_shape=jax.ShapeDtypeStruct((m, n), BF16),
        scratch_shapes=[pltpu.VMEM((bm, bn), F32)],
        compiler_params=_cparams(("parallel", "parallel", "arbitrary")),
    )(a, b, *([] if after is None else [after]))


def _ffn_wgrads(h, dg, du, a, dyh, between=None, after=None, *, name):
    grads = []
    for k, (lhs, rhs, tag) in enumerate(((dg, h, "_wg"), (du, h, "_wu"), (a, dyh, "_wd"))):
        grads.append(_wgrad(lhs, rhs, D_FF // 2, D_MODEL, after, name=name + tag))
        after = None if between is None else between(k, grads[-1])
    return grads


def _ffn_bwd(x, nw, h, g, u, wg, wu, wd, dy, *, name):
    dx, dnw, dg, du, a, dyh = _ffn_bwd_x(x, nw, g, u, wg, wu, wd, dy, name=name + "_x")
    return (dx, dnw, *_ffn_wgrads(h, dg, du, a, dyh, name=name))


_PROJ_WIDTHS = (SG_WIDTH, SG_WIDTH, 3 * DN_WIDTH, DN_WIDTH, LANES, LANES)


def _mix_in_fwd(x, nw, ws, *, name):
    t = x.shape[0]
    tm = _tm(t)

    def body(x_ref, nw_ref, *refs):
        w_refs, o_refs = refs[:6], refs[6:]
        h, _, _ = _rms_fwd(x_ref[...], nw_ref[...])
        h = h.astype(BF16)
        for w_ref, o_ref in zip(w_refs, o_refs):
            o_ref[...] = lax.dot_general(h, w_ref[...], (((1,), (1,)), ((), ())), preferred_element_type=F32)

    row = lambda i: (i, 0)
    const = lambda i: (0, 0)
    return pl.pallas_call(
        body, name=name, grid=(t // tm,),
        in_specs=[pl.BlockSpec((tm, D_MODEL), row), pl.BlockSpec((1, D_MODEL), const)]
        + [pl.BlockSpec((n, D_MODEL), const) for n in _PROJ_WIDTHS],
        out_specs=tuple(pl.BlockSpec((tm, n), row) for n in _PROJ_WIDTHS),
        out_shape=tuple(jax.ShapeDtypeStruct((t, n), F32) for n in _PROJ_WIDTHS),
        compiler_params=_cparams(("parallel",)),
    )(x, nw, *ws)


_PROJ_TOTAL = sum(_PROJ_WIDTHS)
_PROJ_OFFSETS = tuple(sum(_PROJ_WIDTHS[:k]) for k in range(len(_PROJ_WIDTHS)))


def _mix_in_bwd(x, nw, ws, dres, dps, *, name):
    t = x.shape[0]
    tm = _tm(t, 256)

    def body(x_ref, nw_ref, dres_ref, *refs):
        w_refs, dp_refs, dx_ref, dnw_ref, h_ref, dpb_ref = refs[:6], refs[6:12], refs[12], refs[13], refs[14], refs[15]
        i = pl.program_id(0)
        hf, xh, r = _rms_fwd(x_ref[...], nw_ref[...])
        h_ref[...] = hf.astype(BF16)
        dh = jnp.zeros((tm, D_MODEL), F32)
        for w_ref, dp_ref, off, width in zip(w_refs, dp_refs, _PROJ_OFFSETS, _PROJ_WIDTHS):
            dp = dp_ref[...].astype(BF16)
            dpb_ref[:, off:off + width] = dp
            dh = dh + jnp.dot(dp, w_ref[...], preferred_element_type=F32)
        dx, dnw = _rms_bwd(dh, xh, r, nw_ref[...])
        dx_ref[...] = dres_ref[...] + dx
        _acc_out(dnw_ref, i == 0, dnw)

    row = lambda i: (i, 0)
    const = lambda i: (0, 0)
    dx, dnw, h, dpb = pl.pallas_call(
        body, name=name + "_x", grid=(t // tm,),
        in_specs=[pl.BlockSpec((tm, D_MODEL), row), pl.BlockSpec((1, D_MODEL), const), pl.BlockSpec((tm, D_MODEL), row)]
        + [pl.BlockSpec((n, D_MODEL), const) for n in _PROJ_WIDTHS]
        + [pl.BlockSpec((tm, n), row) for n in _PROJ_WIDTHS],
        out_specs=(pl.BlockSpec((tm, D_MODEL), row), pl.BlockSpec((1, D_MODEL), const), pl.BlockSpec((tm, D_MODEL), row),
                   pl.BlockSpec((tm, _PROJ_TOTAL), row)),
        out_shape=(jax.ShapeDtypeStruct((t, D_MODEL), F32), jax.ShapeDtypeStruct((1, D_MODEL), F32),
                   jax.ShapeDtypeStruct((t, D_MODEL), BF16), jax.ShapeDtypeStruct((t, _PROJ_TOTAL), BF16)),
        compiler_params=_cparams(("arbitrary",)),
    )(x, nw, dres, *ws, *dps)
    return dx, dnw, _wgrad(dpb, h, _PROJ_TOTAL // 2, D_MODEL, name=name + "_w")


_SG_TILES = SG_WIDTH // LANES


def _lane_tiles(ref, rows=slice(None)):
    return [ref[rows, p * LANES:(p + 1) * LANES] for p in range(_SG_TILES)]


def _sg_fn(u, v, lng, lnb, wcs, sgbt):
    lane = lax.broadcasted_iota(jnp.int32, (1, LANES), 1)
    rr = lax.broadcasted_iota(jnp.int32, (SG_CHUNK, SG_CHUNK), 0)
    cc = lax.broadcasted_iota(jnp.int32, (SG_CHUNK, SG_CHUNK), 1)
    per_tile = LANES // SG_GROUP_DIM
    gu, gv = [_gelu(a) for a in u], [_gelu(a) for a in v]
    mu = sum(jnp.sum(a, axis=-1, keepdims=True) for a in gv) * (1.0 / SG_WIDTH)
    cen = [a - mu for a in gv]
    var = sum(jnp.sum(a * a, axis=-1, keepdims=True) for a in cen) * (1.0 / SG_WIDTH)
    rstd = lax.rsqrt(var + EPS)
    ln = [a * rstd * g + b for a, g, b in zip(cen, lng, lnb)]
    out = []
    for p in range(_SG_TILES):
        vs = None
        for e in range(per_tile):
            g = p * per_tile + e
            in_group = jnp.logical_and(lane >= e * SG_GROUP_DIM, lane < (e + 1) * SG_GROUP_DIM)
            w_causal = jnp.where(rr >= cc, wcs[g], 0.0)
            bias = jnp.sum(jnp.where(lane == g, sgbt, 0.0), axis=1, keepdims=True)
            term = jnp.where(in_group, mm(w_causal, ln[p]) + bias, 0.0)
            vs = term if vs is None else vs + term
        out.append(gu[p] * vs)
    return out


def _sg_fwd(u, v, lng, lnb, wc, sgbt, *, name):
    t = u.shape[0]
    tm = _tm(t)

    def body(u_ref, v_ref, lng_ref, lnb_ref, wc_ref, sgbt_ref, o_ref):
        wcs = [wc_ref[g] for g in range(SG_GROUPS)]
        for c in range(tm // SG_CHUNK):
            rows = pl.ds(c * SG_CHUNK, SG_CHUNK)
            out = _sg_fn(_lane_tiles(u_ref, rows), _lane_tiles(v_ref, rows), _lane_tiles(lng_ref), _lane_tiles(lnb_ref),
                         wcs, sgbt_ref[...])
            for p in range(_SG_TILES):
                o_ref[rows, p * LANES:(p + 1) * LANES] = out[p]

    row = lambda i: (i, 0)
    const = lambda i: (0, 0)
    return pl.pallas_call(
        body, name=name, grid=(t // tm,),
        in_specs=[pl.BlockSpec((tm, SG_WIDTH), row), pl.BlockSpec((tm, SG_WIDTH), row),
                  pl.BlockSpec((1, SG_WIDTH), const), pl.BlockSpec((1, SG_WIDTH), const),
                  pl.BlockSpec((SG_GROUPS, SG_CHUNK, SG_CHUNK), lambda i: (0, 0, 0)), pl.BlockSpec((SG_CHUNK, LANES), const)],
        out_specs=pl.BlockSpec((tm, SG_WIDTH), row),
        out_shape=jax.ShapeDtypeStruct((t, SG_WIDTH), F32),
        compiler_params=_cparams(("parallel",)),
    )(u, v, lng, lnb, wc, sgbt)


def _sg_bwd(u, v, lng, lnb, wc, sgbt, dout, *, name):
    t = u.shape[0]
    tm = _tm(t)

    def body(u_ref, v_ref, lng_ref, lnb_ref, wc_ref, sgbt_ref, do_ref, du_ref, dv_ref, dlng_ref, dlnb_ref, dwc_ref, dsgbt_ref):
        i = pl.program_id(0)
        wcs = [wc_ref[g] for g in range(SG_GROUPS)]
        tot = None
        for c in range(tm // SG_CHUNK):
            rows = pl.ds(c * SG_CHUNK, SG_CHUNK)
            _, vjp = jax.vjp(_sg_fn, _lane_tiles(u_ref, rows), _lane_tiles(v_ref, rows), _lane_tiles(lng_ref),
                             _lane_tiles(lnb_ref), wcs, sgbt_ref[...])
            du, dv, dlng, dlnb, dwcs, dsgbt = vjp(_lane_tiles(do_ref, rows))
            for p in range(_SG_TILES):
                du_ref[rows, p * LANES:(p + 1) * LANES] = du[p].astype(BF16)
                dv_ref[rows, p * LANES:(p + 1) * LANES] = dv[p].astype(BF16)
            part = (dlng, dlnb, dwcs, dsgbt)
            tot = part if tot is None else jax.tree.map(jnp.add, tot, part)
        dlng, dlnb, dwcs, dsgbt = tot
        _acc_out(dlng_ref, i == 0, jnp.concatenate(dlng, axis=1))
        _acc_out(dlnb_ref, i == 0, jnp.concatenate(dlnb, axis=1))
        _acc_out(dsgbt_ref, i == 0, dsgbt)
        for g in range(SG_GROUPS):
            @pl.when(i == 0)
            def _(g=g):
                dwc_ref[g] = dwcs[g]

            @pl.when(i > 0)
            def _(g=g):
                dwc_ref[g] += dwcs[g]

    row = lambda i: (i, 0)
    const = lambda i: (0, 0)
    wspec = pl.BlockSpec((SG_GROUPS, SG_CHUNK, SG_CHUNK), lambda i: (0, 0, 0))
    return pl.pallas_call(
        body, name=name, grid=(t // tm,),
        in_specs=[pl.BlockSpec((tm, SG_WIDTH), row), pl.BlockSpec((tm, SG_WIDTH), row),
                  pl.BlockSpec((1, SG_WIDTH), const), pl.BlockSpec((1, SG_WIDTH), const), wspec,
                  pl.BlockSpec((SG_CHUNK, LANES), const), pl.BlockSpec((tm, SG_WIDTH), row)],
        out_specs=(pl.BlockSpec((tm, SG_WIDTH), row), pl.BlockSpec((tm, SG_WIDTH), row),
                   pl.BlockSpec((1, SG_WIDTH), const), pl.BlockSpec((1, SG_WIDTH), const), wspec,
                   pl.BlockSpec((SG_CHUNK, LANES), const)),
        out_shape=(jax.ShapeDtypeStruct((t, SG_WIDTH), BF16), jax.ShapeDtypeStruct((t, SG_WIDTH), BF16),
                   jax.ShapeDtypeStruct((1, SG_WIDTH), F32), jax.ShapeDtypeStruct((1, SG_WIDTH), F32),
                   jax.ShapeDtypeStruct((SG_GROUPS, SG_CHUNK, SG_CHUNK), F32), jax.ShapeDtypeStruct((SG_CHUNK, LANES), F32)),
        compiler_params=_cparams(("arbitrary",)),
    )(u, v, lng, lnb, wc, sgbt, dout)


def _conv_taps(ext, w, tm):
    y = None
    for j in range(CONV_K):
        s = CONV_K - 1 - j
        shifted = ext if s == 0 else pltpu.roll(ext, s, 0)
        term = w[j:j + 1, :] * shifted[HALO:HALO + tm, :]
        y = term if y is None else y + term
    return y


def _post_conv(yq, yk, yv, bpre, apre, alog, dtb):
    def l2(a):
        return a * lax.rsqrt(jnp.sum(a * a, axis=-1, keepdims=True) + EPS)

    q = [l2(_silu(a)) for a in yq]
    k = [l2(_silu(a)) for a in yk]
    return q, k, _silu(yv), _sigmoid(bpre), -jnp.exp(alog) * _softplus(apre + dtb)


def _chunk_tril(tm):
    rr = lax.broadcasted_iota(jnp.int32, (tm, tm), 0)
    cc = lax.broadcasted_iota(jnp.int32, (tm, tm), 1)
    shift = DN_CHUNK.bit_length() - 1
    same = jnp.right_shift(rr, shift) == jnp.right_shift(cc, shift)
    return jnp.where(jnp.logical_and(same, rr >= cc), 1.0, 0.0).astype(F32)


def _halo_specs(tm, width, n_blocks_seq, n_blocks):
    per = tm // HALO
    prev = pl.BlockSpec((HALO, width), lambda i: (jnp.maximum(i * per - 1, 0), 0))
    nxt = pl.BlockSpec((HALO, width), lambda i: (jnp.minimum((i + 1) * per, n_blocks * per - 1), 0))
    return prev, nxt


def _split_heads(ref, base):
    return [ref[:, base + h * DN_HEAD_DIM: base + (h + 1) * DN_HEAD_DIM] for h in range(DN_HEADS)]


def _dn_prep_fwd(qkv, bpre, apre, conv_w, alog, dtb, seq, *, name):
    t = qkv.shape[0]
    tm = _tm(t)
    bps = seq // tm
    cw = 3 * DN_WIDTH

    def body(x_ref, halo_ref, b_ref, a_ref, w_ref, alog_ref, dtb_ref, q_ref, k_ref, v_ref, beta_ref, gc_ref):
        i = pl.program_id(0)
        keep = jnp.where(i % bps == 0, 0.0, 1.0)
        ext = jnp.concatenate([halo_ref[...] * keep, x_ref[...]], axis=0)
        y = _conv_taps(ext, w_ref[...], tm)
        yq = [y[:, h * DN_HEAD_DIM:(h + 1) * DN_HEAD_DIM] for h in range(DN_HEADS)]
        yk = [y[:, DN_WIDTH + h * DN_HEAD_DIM: DN_WIDTH + (h + 1) * DN_HEAD_DIM] for h in range(DN_HEADS)]
        q, k, v, beta, g = _post_conv(yq, yk, y[:, 2 * DN_WIDTH:], b_ref[...], a_ref[...], alog_ref[...], dtb_ref[...])
        for h in range(DN_HEADS):
            q_ref[:, h * DN_HEAD_DIM:(h + 1) * DN_HEAD_DIM] = q[h]
            k_ref[:, h * DN_HEAD_DIM:(h + 1) * DN_HEAD_DIM] = k[h]
        v_ref[...] = v
        beta_ref[...] = beta
        gc_ref[...] = mmx(_chunk_tril(tm), g)

    row = lambda i: (i, 0)
    const = lambda i: (0, 0)
    prev, _ = _halo_specs(tm, cw, bps, t // tm)
    return pl.pallas_call(
        body, name=name, grid=(t // tm,),
        in_specs=[pl.BlockSpec((tm, cw), row), prev, pl.BlockSpec((tm, LANES), row), pl.BlockSpec((tm, LANES), row),
                  pl.BlockSpec((CONV_K, cw), const), pl.BlockSpec((1, LANES), const), pl.BlockSpec((1, LANES), const)],
        out_specs=tuple(pl.BlockSpec((tm, n), row) for n in (DN_WIDTH, DN_WIDTH, DN_WIDTH, LANES, LANES)),
        out_shape=tuple(jax.ShapeDtypeStruct((t, n), F32) for n in (DN_WIDTH, DN_WIDTH, DN_WIDTH, LANES, LANES)),
        compiler_params=_cparams(("parallel",)),
    )(qkv, qkv, bpre, apre, conv_w, alog, dtb)


def _y_heads(y):
    yq = [y[:, h * DN_HEAD_DIM:(h + 1) * DN_HEAD_DIM] for h in range(DN_HEADS)]
    yk = [y[:, DN_WIDTH + h * DN_HEAD_DIM: DN_WIDTH + (h + 1) * DN_HEAD_DIM] for h in range(DN_HEADS)]
    return yq, yk, y[:, 2 * DN_WIDTH:]


def _dn_prep_bwd(qkv, bpre, apre, conv_w, alog, dtb, dq, dk, dv, dbeta, dgc, dgc2, seq, *, name):
    t = qkv.shape[0]
    tm = _tm(t)
    bps = seq // tm
    cw = 3 * DN_WIDTH
    n_ext = tm + HALO

    def body(x_ref, halo_ref, xn_ref, b_ref, a_ref, w_ref, alog_ref, dtb_ref, dq_ref, dk_ref, dv_ref, dqn_ref, dkn_ref,
             dvn_ref, dbeta_ref, dgc_ref, dgc2_ref, dx_ref, dw_ref, db_ref, da_ref, dalog_ref, ddtb_ref):
        i = pl.program_id(0)
        keep_prev = jnp.where(i % bps == 0, 0.0, 1.0)
        keep_next = jnp.where(i % bps == bps - 1, 0.0, 1.0)
        w = w_ref[...]
        x = x_ref[...]
        ext = jnp.concatenate([halo_ref[...] * keep_prev, x], axis=0)
        yq, yk, yv = _y_heads(_conv_taps(ext, w, tm))
        _, vjp = jax.vjp(_post_conv, yq, yk, yv, b_ref[...], a_ref[...], alog_ref[...], dtb_ref[...])
        dg = mmx_tn(_chunk_tril(tm), dgc_ref[...] + dgc2_ref[...])
        dyq, dyk, dyv, db, da, dalog, ddtb = vjp((_split_heads(dq_ref, 0), _split_heads(dk_ref, 0), dv_ref[...],
                                                  dbeta_ref[...], dg))
        dy = jnp.concatenate(dyq + dyk + [dyv], axis=1)
        ext_n = jnp.concatenate([x[tm - HALO:, :], xn_ref[...]], axis=0)
        _, vjp_n = jax.vjp(lambda *ys: _post_conv(*ys, b_ref[:HALO, :], a_ref[:HALO, :], alog_ref[...], dtb_ref[...])[:3],
                           *_y_heads(_conv_taps(ext_n, w, HALO)))
        dyq_n, dyk_n, dyv_n = vjp_n((_split_heads(dqn_ref, 0), _split_heads(dkn_ref, 0), dvn_ref[...]))
        dyext = jnp.concatenate([dy, jnp.concatenate(dyq_n + dyk_n + [dyv_n], axis=1) * keep_next], axis=0)

        @pl.when(i == 0)
        def _():
            dw_ref[...] = jnp.zeros_like(dw_ref)

        dx = None
        for j in range(CONV_K):
            s = CONV_K - 1 - j
            fut = dyext if s == 0 else pltpu.roll(dyext, n_ext - s, 0)
            term = w[j:j + 1, :] * fut[0:tm, :]
            dx = term if dx is None else dx + term
            past = ext if s == 0 else pltpu.roll(ext, s, 0)
            dw_ref[j:j + 1, :] += jnp.sum(dy * past[HALO:HALO + tm, :], axis=0, keepdims=True)
        dx_ref[...] = dx.astype(BF16)
        db_ref[...] = db.astype(BF16)
        da_ref[...] = da.astype(BF16)
        _acc_out(dalog_ref, i == 0, dalog)
        _acc_out(ddtb_ref, i == 0, ddtb)

    row = lambda i: (i, 0)
    const = lambda i: (0, 0)
    prev, nxt = _halo_specs(tm, cw, bps, t // tm)
    _, nxt_h = _halo_specs(tm, DN_WIDTH, bps, t // tm)
    tok = pl.BlockSpec((tm, DN_WIDTH), row)
    lanes = pl.BlockSpec((tm, LANES), row)
    return pl.pallas_call(
        body, name=name, grid=(t // tm,),
        in_specs=[pl.BlockSpec((tm, cw), row), prev, nxt, lanes, lanes,
                  pl.BlockSpec((CONV_K, cw), const), pl.BlockSpec((1, LANES), const), pl.BlockSpec((1, LANES), const),
                  tok, tok, tok, nxt_h, nxt_h, nxt_h, lanes, lanes, lanes],
        out_specs=(pl.BlockSpec((tm, cw), row), pl.BlockSpec((HALO, cw), const), lanes, lanes,
                   pl.BlockSpec((1, LANES), const), pl.BlockSpec((1, LANES), const)),
        out_shape=(jax.ShapeDtypeStruct((t, cw), BF16), jax.ShapeDtypeStruct((HALO, cw), F32),
                   jax.ShapeDtypeStruct((t, LANES), BF16), jax.ShapeDtypeStruct((t, LANES), BF16),
                   jax.ShapeDtypeStruct((1, LANES), F32), jax.ShapeDtypeStruct((1, LANES), F32)),
        compiler_params=_cparams(("arbitrary",)),
    )(qkv, qkv, qkv, bpre, apre, conv_w, alog, dtb, dq, dk, dv, dq, dk, dv, dbeta, dgc, dgc2)


def _inv_unit_lower(l_mats, eye):
    invs = [eye - l for l in l_mats]
    powers = list(l_mats)
    n = 2
    while n < eye.shape[0]:
        powers = [mmh(p, p) for p in powers]
        invs = [inv + mmh(inv, p) for inv, p in zip(invs, powers)]
        n *= 2
    return invs


@jax.custom_vjp
def _solve(l_mat, rhs, inv):
    return mmh(inv, rhs)


def _solve_fwd(l_mat, rhs, inv):
    sol = mmh(inv, rhs)
    return sol, (inv, sol)


def _solve_bwd(res, d_sol):
    inv, sol = res
    d_rhs = mm_tn(inv, d_sol)
    return -mm_nt(d_rhs, sol), d_rhs, jnp.zeros_like(inv)


_solve.defvjp(_solve_fwd, _solve_bwd)


def _prep_fn(q, k, v, gc, gr, b, inv):
    ids = range(len(q))
    c = q[0].shape[0]
    rr = lax.broadcasted_iota(jnp.int32, (c, c), 0)
    cc = lax.broadcasted_iota(jnp.int32, (c, c), 1)
    incl, strict = rr >= cc, rr > cc
    is_last = lax.broadcasted_iota(jnp.int32, (c, 1), 0) == c - 1
    qs = [q[i] * (DN_HEAD_DIM ** -0.5) for i in ids]
    decay = [jnp.where(incl, jnp.exp(jnp.where(incl, gc[i] - gr[i], 0.0)), 0.0) for i in ids]
    kb = [k[i] * b[i] for i in ids]
    vb = [v[i] * b[i] for i in ids]
    kk = [mm_nt(kb[i], k[i]) for i in ids]
    l_mat = [jnp.where(strict, kk[i] * decay[i], 0.0) for i in ids]
    eg = [jnp.exp(gc[i]) for i in ids]
    if inv is None:
        inv = _inv_unit_lower(l_mat, jnp.where(rr == cc, 1.0, 0.0).astype(F32))
    u_wy = [_solve(l_mat[i], vb[i], inv[i]) for i in ids]
    w_wy = [_solve(l_mat[i], kb[i] * eg[i], inv[i]) for i in ids]
    qk = [mm_nt(qs[i], k[i]) * decay[i] for i in ids]
    g_last = [jnp.sum(jnp.where(is_last, gc[i], 0.0), axis=0, keepdims=True) for i in ids]
    k_dec = [k[i] * jnp.exp(g_last[i] - gc[i]) for i in ids]
    egl = [jnp.broadcast_to(jnp.exp(g_last[i]), (1, LANES)) for i in ids]
    return [(w_wy[i], u_wy[i], qs[i] * eg[i], k_dec[i], qk[i], egl[i]) for i in ids], inv


def _seq_fn(w, u, qd, kd, qk, egl, s):
    ids = range(len(w))
    ws = [mm(w[i], s[i]) for i in ids]
    qs = [mm(qd[i], s[i]) for i in ids]
    v_new = [u[i] - ws[i] for i in ids]
    o = [qs[i] + mm(qk[i], v_new[i]) for i in ids]
    s_new = [s[i] * egl[i] + mm_tn(kd[i], v_new[i]) for i in ids]
    return o, s_new


def _lane_col(a, h):
    lane = lax.broadcasted_iota(jnp.int32, (1, LANES), 1)
    return jnp.sum(jnp.where(lane == h, a, 0.0), axis=1, keepdims=True)


def _col_lane(col, h):
    lane = lax.broadcasted_iota(jnp.int32, (1, LANES), 1)
    return jnp.where(lane == h, col, 0.0)


def _head_cols(h):
    return slice(h * DN_HEAD_DIM, (h + 1) * DN_HEAD_DIM)


def _chunk_rows(n):
    return pl.ds(pl.multiple_of(n * DN_CHUNK, DN_CHUNK), DN_CHUNK)


def _delta_prep(q, k, v, gc, grow, beta, *, name):
    t = q.shape[0]
    tm = _tm(t)
    cpb = tm // DN_CHUNK
    n_chunks = t // DN_CHUNK
    group = 2

    def body(q_ref, k_ref, v_ref, gc_ref, gr_ref, b_ref, w_ref, u_ref, qd_ref, kd_ref, qk_ref, egl_ref, inv_ref):
        def step(m, carry):
            probs = [(m * group + e, h) for e in range(group) for h in range(DN_HEADS)]
            gcb = [gc_ref[_chunk_rows(m * group + e), :] for e in range(group)]
            bb = [b_ref[_chunk_rows(m * group + e), :] for e in range(group)]
            grb = [gr_ref[m * group + e] for e in range(group)]
            for e in range(group):
                egl_ref[m * group + e] = jnp.zeros((HALO, LANES), F32)
            outs, invs = _prep_fn(
                [q_ref[_chunk_rows(n), _head_cols(h)] for n, h in probs], [k_ref[_chunk_rows(n), _head_cols(h)] for n, h in probs],
                [v_ref[_chunk_rows(n), _head_cols(h)] for n, h in probs],
                [_lane_col(gcb[e], h) for e in range(group) for h in range(DN_HEADS)],
                [grb[e][h:h + 1, :] for e in range(group) for h in range(DN_HEADS)],
                [_lane_col(bb[e], h) for e in range(group) for h in range(DN_HEADS)], None)
            for (n, h), (w, u, qd, kd, qk, egl), inv in zip(probs, outs, invs):
                rows, cols = _chunk_rows(n), _head_cols(h)
                w_ref[rows, cols] = w.astype(BF16)
                u_ref[rows, cols] = u
                qd_ref[rows, cols] = qd.astype(BF16)
                kd_ref[rows, cols] = kd.astype(BF16)
                qk_ref[n, h] = qk
                inv_ref[n, h] = inv
                egl_ref[n, h:h + 1, :] = egl
            return carry

        lax.fori_loop(0, cpb // group, step, 0)

    row = lambda i: (i, 0)
    tok = pl.BlockSpec((tm, DN_WIDTH), row)
    lanes = pl.BlockSpec((tm, LANES), row)
    sq = pl.BlockSpec((cpb, DN_HEADS, DN_CHUNK, DN_CHUNK), lambda i: (i, 0, 0, 0))
    return pl.pallas_call(
        body, name=name, grid=(t // tm,),
        in_specs=[tok, tok, tok, lanes, pl.BlockSpec((cpb, HALO, DN_CHUNK), lambda i: (i, 0, 0)), lanes],
        out_specs=(tok, tok, tok, tok, sq, pl.BlockSpec((cpb, HALO, LANES), lambda i: (i, 0, 0)), sq),
        out_shape=(jax.ShapeDtypeStruct((t, DN_WIDTH), BF16), jax.ShapeDtypeStruct((t, DN_WIDTH), F32),
                   jax.ShapeDtypeStruct((t, DN_WIDTH), BF16), jax.ShapeDtypeStruct((t, DN_WIDTH), BF16),
                   jax.ShapeDtypeStruct((n_chunks, DN_HEADS, DN_CHUNK, DN_CHUNK), F32),
                   jax.ShapeDtypeStruct((n_chunks, HALO, LANES), F32),
                   jax.ShapeDtypeStruct((n_chunks, DN_HEADS, DN_CHUNK, DN_CHUNK), F32)),
        compiler_params=_cparams(("parallel",)),
    )(q, k, v, gc, grow, beta)


def _delta_par_bwd(q, k, v, gc, grow, beta, inv, dw, du, dqd, dkd, dqk, degl, *, name):
    t = q.shape[0]
    tm = _tm(t)
    cpb = tm // DN_CHUNK
    n_chunks = t // DN_CHUNK
    group = 2

    def body(q_ref, k_ref, v_ref, gc_ref, gr_ref, b_ref, inv_ref, dw_ref, du_ref, dqd_ref, dkd_ref, dqk_ref, degl_ref,
             dq_ref, dk_ref, dv_ref, dgc_ref, dgr_ref, db_ref):
        def step(m, carry):
            chunks = [m * group + e for e in range(group)]
            probs = [(e, h) for e in range(group) for h in range(DN_HEADS)]
            rows = [_chunk_rows(n) for n in chunks]
            gcb, bb = [gc_ref[r, :] for r in rows], [b_ref[r, :] for r in rows]
            grb, deglb = [gr_ref[n] for n in chunks], [degl_ref[n] for n in chunks]
            for n in chunks:
                dgr_ref[n] = jnp.zeros((HALO, DN_CHUNK), F32)
            invs = [inv_ref[chunks[e], h] for e, h in probs]
            _, vjp = jax.vjp(lambda *a: _prep_fn(*a, invs)[0],
                             [q_ref[rows[e], _head_cols(h)] for e, h in probs], [k_ref[rows[e], _head_cols(h)] for e, h in probs],
                             [v_ref[rows[e], _head_cols(h)] for e, h in probs], [_lane_col(gcb[e], h) for e, h in probs],
                             [grb[e][h:h + 1, :] for e, h in probs], [_lane_col(bb[e], h) for e, h in probs])
            dq, dk, dv, dgc, dgr, db = vjp([(dw_ref[rows[e], _head_cols(h)], du_ref[rows[e], _head_cols(h)],
                                             dqd_ref[rows[e], _head_cols(h)], dkd_ref[rows[e], _head_cols(h)],
                                             dqk_ref[chunks[e], h], deglb[e][h:h + 1, :]) for e, h in probs])
            dgc_acc = [jnp.zeros((DN_CHUNK, LANES), F32) for _ in chunks]
            db_acc = [jnp.zeros((DN_CHUNK, LANES), F32) for _ in chunks]
            for i, (e, h) in enumerate(probs):
                cols = _head_cols(h)
                dq_ref[rows[e], cols] = dq[i]
                dk_ref[rows[e], cols] = dk[i]
                dv_ref[rows[e], cols] = dv[i]
                dgr_ref[chunks[e], h:h + 1, :] = dgr[i]
                dgc_acc[e] = dgc_acc[e] + _col_lane(dgc[i], h)
                db_acc[e] = db_acc[e] + _col_lane(db[i], h)
            for e in range(group):
                dgc_ref[rows[e], :] = dgc_acc[e]
                db_ref[rows[e], :] = db_acc[e]
            return carry

        lax.fori_loop(0, cpb // group, step, 0)

    row = lambda i: (i, 0)
    tok = pl.BlockSpec((tm, DN_WIDTH), row)
    lanes = pl.BlockSpec((tm, LANES), row)
    sq = pl.BlockSpec((cpb, DN_HEADS, DN_CHUNK, DN_CHUNK), lambda i: (i, 0, 0, 0))
    grs = pl.BlockSpec((cpb, HALO, DN_CHUNK), lambda i: (i, 0, 0))
    return pl.pallas_call(
        body, name=name, grid=(t // tm,),
        in_specs=[tok, tok, tok, lanes, grs, lanes, sq, tok, tok, tok, tok, sq, pl.BlockSpec((cpb, HALO, LANES), lambda i: (i, 0, 0))],
        out_specs=(tok, tok, tok, lanes, grs, lanes),
        out_shape=(jax.ShapeDtypeStruct((t, DN_WIDTH), F32),) * 3
        + (jax.ShapeDtypeStruct((t, LANES), F32), jax.ShapeDtypeStruct((n_chunks, HALO, DN_CHUNK), F32),
           jax.ShapeDtypeStruct((t, LANES), F32)),
        compiler_params=_cparams(("parallel",)),
    )(q, k, v, gc, grow, beta, inv, dw, du, dqd, dkd, dqk, degl)


def _seq_specs(n_seq, seq, reverse):
    tm = _tm(seq)
    nb = seq // tm
    cpb = tm // DN_CHUNK
    pair = 2 if n_seq % 2 == 0 else 1
    blk = (lambda j: nb - 1 - j) if reverse else (lambda j: j)
    tok = pl.BlockSpec((pair, tm, DN_WIDTH), lambda b, j: (b, blk(j), 0))
    sq = pl.BlockSpec((pair, cpb, DN_HEADS, DN_CHUNK, DN_CHUNK), lambda b, j: (b, blk(j), 0, 0, 0))
    rows8 = pl.BlockSpec((pair, cpb, HALO, LANES), lambda b, j: (b, blk(j), 0, 0))
    state = pl.BlockSpec((pair, cpb, DN_HEADS, DN_HEAD_DIM, DN_HEAD_DIM), lambda b, j: (b, blk(j), 0, 0, 0))
    return nb, cpb, pair, tok, sq, rows8, state


def _by_seq(a, n_seq):
    return a.reshape((n_seq, a.shape[0] // n_seq) + a.shape[1:])


def _flat_seq(a):
    return a.reshape((a.shape[0] * a.shape[1],) + a.shape[2:])


def _delta_seq_fwd(w, u, qd, kd, qk, egl, n_seq, seq, *, name):
    nb, cpb, pair, tok, sq, rows8, state = _seq_specs(n_seq, seq, False)
    probs = [(e, h) for e in range(pair) for h in range(DN_HEADS)]

    def body(w_ref, u_ref, qd_ref, kd_ref, qk_ref, egl_ref, o_ref, st_ref, s_s):
        @pl.when(pl.program_id(1) == 0)
        def _():
            s_s[...] = jnp.zeros_like(s_s)

        def step(n, carry):
            rows = _chunk_rows(n)
            eglb = [egl_ref[e, n] for e in range(pair)]
            s = [s_s[e, h] for e, h in probs]
            for (e, h), s_eh in zip(probs, s):
                st_ref[e, n, h] = s_eh
            o, s_new = _seq_fn([w_ref[e, rows, _head_cols(h)] for e, h in probs], [u_ref[e, rows, _head_cols(h)] for e, h in probs],
                               [qd_ref[e, rows, _head_cols(h)] for e, h in probs], [kd_ref[e, rows, _head_cols(h)] for e, h in probs],
                               [qk_ref[e, n, h] for e, h in probs], [eglb[e][h:h + 1, :] for e, h in probs], s)
            for i, (e, h) in enumerate(probs):
                o_ref[e, rows, _head_cols(h)] = o[i]
                s_s[e, h] = s_new[i]
            return carry

        lax.fori_loop(0, cpb, step, 0)

    o, states = pl.pallas_call(
        body, name=name, grid=(n_seq // pair, nb),
        in_specs=[tok, tok, tok, tok, sq, rows8],
        out_specs=(tok, state),
        out_shape=(jax.ShapeDtypeStruct((n_seq, seq, DN_WIDTH), F32),
                   jax.ShapeDtypeStruct((n_seq, seq // DN_CHUNK, DN_HEADS, DN_HEAD_DIM, DN_HEAD_DIM), F32)),
        scratch_shapes=[pltpu.VMEM((pair, DN_HEADS, DN_HEAD_DIM, DN_HEAD_DIM), F32)],
        compiler_params=_cparams(("parallel", "arbitrary")),
    )(*[_by_seq(a, n_seq) for a in (w, u, qd, kd, qk, egl)])
    return _flat_seq(o), _flat_seq(states)


def _delta_seq_bwd(w, u, qd, kd, qk, egl, states, do, n_seq, seq, *, name):
    nb, cpb, pair, tok, sq, rows8, state = _seq_specs(n_seq, seq, True)
    probs = [(e, h) for e in range(pair) for h in range(DN_HEADS)]

    def body(w_ref, u_ref, qd_ref, kd_ref, qk_ref, egl_ref, st_ref, do_ref, dw_ref, du_ref, dqd_ref, dkd_ref, dqk_ref,
             degl_ref, ds_s):
        @pl.when(pl.program_id(1) == 0)
        def _():
            ds_s[...] = jnp.zeros_like(ds_s)

        def step(m, carry):
            n = cpb - 1 - m
            rows = _chunk_rows(n)
            eglb = [egl_ref[e, n] for e in range(pair)]
            for e in range(pair):
                degl_ref[e, n] = jnp.zeros((HALO, LANES), F32)
            _, vjp = jax.vjp(_seq_fn, [w_ref[e, rows, _head_cols(h)].astype(F32) for e, h in probs],
                             [u_ref[e, rows, _head_cols(h)] for e, h in probs],
                             [qd_ref[e, rows, _head_cols(h)].astype(F32) for e, h in probs],
                             [kd_ref[e, rows, _head_cols(h)].astype(F32) for e, h in probs],
                             [qk_ref[e, n, h] for e, h in probs], [eglb[e][h:h + 1, :] for e, h in probs],
                             [st_ref[e, n, h] for e, h in probs])
            dw, du, dqd, dkd, dqk, degl, ds_in = vjp(([do_ref[e, rows, _head_cols(h)] for e, h in probs],
                                                      [ds_s[e, h] for e, h in probs]))
            for i, (e, h) in enumerate(probs):
                cols = _head_cols(h)
                dw_ref[e, rows, cols] = dw[i]
                du_ref[e, rows, cols] = du[i]
                dqd_ref[e, rows, cols] = dqd[i]
                dkd_ref[e, rows, cols] = dkd[i]
                dqk_ref[e, n, h] = dqk[i]
                degl_ref[e, n, h:h + 1, :] = degl[i]
                ds_s[e, h] = ds_in[i]
            return carry

        lax.fori_loop(0, cpb, step, 0)

    nc = seq // DN_CHUNK
    outs = pl.pallas_call(
        body, name=name, grid=(n_seq // pair, nb),
        in_specs=[tok, tok, tok, tok, sq, rows8, state, tok],
        out_specs=(tok, tok, tok, tok, sq, rows8),
        out_shape=(jax.ShapeDtypeStruct((n_seq, seq, DN_WIDTH), F32),) * 4
        + (jax.ShapeDtypeStruct((n_seq, nc, DN_HEADS, DN_CHUNK, DN_CHUNK), F32),
           jax.ShapeDtypeStruct((n_seq, nc, HALO, LANES), F32)),
        scratch_shapes=[pltpu.VMEM((pair, DN_HEADS, DN_HEAD_DIM, DN_HEAD_DIM), F32)],
        compiler_params=_cparams(("parallel", "arbitrary")),
    )(*[_by_seq(a, n_seq) for a in (w, u, qd, kd, qk, egl, states, do)])
    return tuple(_flat_seq(a) for a in outs)


def _dn_gate(o, z, dnw):
    return o * lax.rsqrt(jnp.mean(o * o, axis=-1, keepdims=True) + EPS) * dnw * _silu(z)


def _mix_out_fwd(x, sg, o, z, wo_sg, wo_dn, dnw, *, name):
    t = x.shape[0]
    tm = _tm(t)

    def body(x_ref, sg_ref, o_ref, z_ref, wsg_ref, wdn_ref, dnw_ref, y_ref, dn_s):
        for h, (oh, zh) in enumerate(zip(_split_heads(o_ref, 0), _split_heads(z_ref, 0))):
            dn_s[:, h * DN_HEAD_DIM:(h + 1) * DN_HEAD_DIM] = _dn_gate(oh, zh, dnw_ref[...]).astype(BF16)
        y_ref[...] = (x_ref[...] + jnp.dot(sg_ref[...].astype(BF16), wsg_ref[...], preferred_element_type=F32)
                      + jnp.dot(dn_s[...], wdn_ref[...], preferred_element_type=F32))

    row = lambda i: (i, 0)
    const = lambda i: (0, 0)
    half = pl.BlockSpec((tm, DN_WIDTH), row)
    return pl.pallas_call(
        body, name=name, grid=(t // tm,),
        in_specs=[pl.BlockSpec((tm, D_MODEL), row), half, half, half, pl.BlockSpec((SG_WIDTH, D_MODEL), const),
                  pl.BlockSpec((DN_WIDTH, D_MODEL), const), pl.BlockSpec((1, DN_HEAD_DIM), const)],
        out_specs=pl.BlockSpec((tm, D_MODEL), row),
        out_shape=jax.ShapeDtypeStruct((t, D_MODEL), F32),
        scratch_shapes=[pltpu.VMEM((tm, DN_WIDTH), BF16)],
        compiler_params=_cparams(("parallel",)),
    )(x, sg, o, z, wo_sg, wo_dn, dnw)


def _mix_out_bwd(dy, sg, o, z, wo_sg, wo_dn, dnw, *, name):
    t = dy.shape[0]
    tm = _tm(t)

    def body(dy_ref, sg_ref, o_ref, z_ref, wsg_ref, wdn_ref, dnw_ref, dsg_ref, do_ref, dz_ref, dwsg_ref, dwdn_ref, ddnw_ref, dn_s):
        i = pl.program_id(0)
        dyb = dy_ref[...].astype(BF16)
        nt = (((1,), (1,)), ((), ()))
        tn = (((0,), (0,)), ((), ()))
        dsg_ref[...] = lax.dot_general(dyb, wsg_ref[...], nt, preferred_element_type=F32)
        ddn = lax.dot_general(dyb, wdn_ref[...], nt, preferred_element_type=F32)
        ddnw = None
        for h, (oh, zh) in enumerate(zip(_split_heads(o_ref, 0), _split_heads(z_ref, 0))):
            cols = slice(h * DN_HEAD_DIM, (h + 1) * DN_HEAD_DIM)
            out, vjp = jax.vjp(_dn_gate, oh, zh, dnw_ref[...])
            dn_s[:, cols] = out.astype(BF16)
            doh, dzh, dw = vjp(ddn[:, cols])
            do_ref[:, cols] = doh
            dz_ref[:, cols] = dzh.astype(BF16)
            ddnw = dw if ddnw is None else ddnw + dw
        _acc_out(ddnw_ref, i == 0, ddnw)
        _acc_out(dwsg_ref, i == 0, lax.dot_general(sg_ref[...].astype(BF16), dyb, tn, preferred_element_type=F32))
        _acc_out(dwdn_ref, i == 0, lax.dot_general(dn_s[...], dyb, tn, preferred_element_type=F32))

    row = lambda i: (i, 0)
    const = lambda i: (0, 0)
    half = pl.BlockSpec((tm, DN_WIDTH), row)
    wspec = pl.BlockSpec((DN_WIDTH, D_MODEL), const)
    return pl.pallas_call(
        body, name=name, grid=(t // tm,),
        in_specs=[pl.BlockSpec((tm, D_MODEL), row), half, half, half, wspec, wspec, pl.BlockSpec((1, DN_HEAD_DIM), const)],
        out_specs=(half, half, half, wspec, wspec, pl.BlockSpec((1, DN_HEAD_DIM), const)),
        out_shape=(jax.ShapeDtypeStruct((t, DN_WIDTH), F32),) * 2 + (jax.ShapeDtypeStruct((t, DN_WIDTH), BF16),)
        + (jax.ShapeDtypeStruct((DN_WIDTH, D_MODEL), F32),) * 2 + (jax.ShapeDtypeStruct((1, DN_HEAD_DIM), F32),),
        scratch_shapes=[pltpu.VMEM((tm, DN_WIDTH), BF16)],
        compiler_params=_cparams(("arbitrary",)),
    )(dy, sg, o, z, wo_sg, wo_dn, dnw)


_MESH = pl.DeviceIdType.MESH
_HBM = pl.BlockSpec(memory_space=pl.ANY)


def _mesh_pos():
    x, y, c = lax.axis_index("x"), lax.axis_index("y"), lax.axis_index("c")
    return x, y, c, [(1 - x, y), (x, 1 - y), (1 - x, 1 - y)]


def _gather2(arrs, *, name):
    n = len(arrs)
    slots = N_DEV - 1

    def body(*refs):
        in_refs, out_refs = refs[:n], refs[n:2 * n]
        send_sems, recv_sems, local_sems = refs[2 * n:]
        x, y, c, chips = _mesh_pos()
        me, sibling = (x, y, c), (x, y, 1 - c)

        def copy(k, slot, block, to, src=None):
            dst = out_refs[k].at[4 * block[0] + 2 * block[1] + block[2]]
            return pltpu.make_async_remote_copy(src_ref=dst if src is None else src, dst_ref=dst,
                                                send_sem=send_sems.at[k * slots + slot], recv_sem=recv_sems.at[k * slots + slot],
                                                device_id=to, device_id_type=_MESH)

        local = [pltpu.make_async_copy(in_refs[k], out_refs[k].at[4 * x + 2 * y + c], local_sems.at[k]) for k in range(n)]
        sent = []
        for k in range(n):
            sent.append(copy(k, 0, me, sibling, src=in_refs[k]))
            sent += [copy(k, 1 + j, me, (*chip, c), src=in_refs[k]) for j, chip in enumerate(chips)]
        for cp in local + sent:
            cp.start()
        for j, chip in enumerate(chips):
            for k in range(n):
                copy(k, 1 + j, (*chip, c), me).wait_recv()
                passed = copy(k, 4 + j, (*chip, c), sibling)
                passed.start()
                sent.append(passed)
        for k in range(n):
            copy(k, 0, sibling, me).wait_recv()
            for j, chip in enumerate(chips):
                copy(k, 4 + j, (*chip, 1 - c), me).wait_recv()
        for cp in sent:
            cp.wait_send()
        for cp in local:
            cp.wait()

    return pl.pallas_call(
        body, name=name, in_specs=[_HBM] * n, out_specs=(_HBM,) * n,
        out_shape=tuple(jax.ShapeDtypeStruct((N_DEV,) + a.shape, a.dtype) for a in arrs),
        scratch_shapes=[pltpu.SemaphoreType.DMA((n * slots,)), pltpu.SemaphoreType.DMA((n * slots,)),
                        pltpu.SemaphoreType.DMA((n,))],
    )(*arrs)


_SEM = pl.BlockSpec(memory_space=pltpu.SEMAPHORE)
_EFFECT = pltpu.SideEffectType.DATAFLOW_SIDE_EFFECTING


def _direct_copies(src_refs, land_refs, send_sems, recv_sems, gather):
    x, y, c, _ = _mesh_pos()
    me = 4 * x + 2 * y + c
    n_peer = N_DEV - 1
    copies = []
    for r in range(1, N_DEV):
        px = 1 - x if r & 4 else x
        py = 1 - y if r & 2 else y
        pc = 1 - c if r & 1 else c
        for k, (src, land) in enumerate(zip(src_refs, land_refs)):
            copies.append(pltpu.make_async_remote_copy(
                src_ref=src if gather else src.at[4 * px + 2 * py + pc], dst_ref=land.at[me],
                send_sem=send_sems.at[k * n_peer + r - 1], recv_sem=recv_sems.at[k * n_peer + r - 1],
                device_id=(px, py, pc), device_id_type=_MESH))
    return copies


def _send_start(arrs, gather, after=None, *, name):
    n = len(arrs)
    lands = [lax.empty(((N_DEV,) + a.shape) if gather else a.shape, a.dtype) for a in arrs]
    n_in = 2 * n + (0 if after is None else 1)

    def body(*refs):
        src_refs, land_refs, send_sems, recv_sems, token = refs[:n], refs[n:2 * n], refs[n_in], refs[n_in + 1], refs[-1]
        for cp in _direct_copies(src_refs, land_refs, send_sems, recv_sems, gather):
            cp.start()
        token[...] = jnp.zeros_like(token)

    n_sem = n * (N_DEV - 1)
    bufs = list(arrs) + lands
    out = pl.pallas_call(
        body, name=name,
        out_shape=(pltpu.SemaphoreType.DMA((n_sem,)), pltpu.SemaphoreType.DMA((n_sem,)))
        + tuple(pltpu.HBM(b.shape, b.dtype) for b in bufs) + (jax.ShapeDtypeStruct((HALO, LANES), F32),),
        in_specs=[_HBM] * n_in, out_specs=(_SEM, _SEM) + (_HBM,) * (2 * n) + (pl.BlockSpec(memory_space=pltpu.VMEM),),
        input_output_aliases={i: 2 + i for i in range(2 * n)},
        compiler_params=pltpu.CompilerParams(has_side_effects=_EFFECT),
    )(*[pltpu.with_memory_space_constraint(b, pltpu.HBM) for b in bufs], *([] if after is None else [after]))
    return (out[0], out[1], list(out[2:2 + n]), list(out[2 + n:2 + 2 * n])), out[-1]


def _send_wait(started, gather, after, *, name):
    send_sems, recv_sems, srcs, lands = started
    n = len(srcs)

    def body(*refs):
        src_refs, land_refs, send_ref, recv_ref = refs[:n], refs[n:2 * n], refs[2 * n], refs[2 * n + 1]
        for cp in _direct_copies(src_refs, land_refs, send_ref, recv_ref, gather):
            cp.wait_send()
            cp.wait_recv()

    bufs = srcs + lands
    out = pl.pallas_call(
        body, name=name, out_shape=tuple(pltpu.HBM(b.shape, b.dtype) for b in bufs),
        in_specs=[_HBM] * (2 * n) + [_SEM, _SEM, _HBM], out_specs=(_HBM,) * (2 * n),
        input_output_aliases={i: i for i in range(2 * n)},
        compiler_params=pltpu.CompilerParams(has_side_effects=_EFFECT),
    )(*bufs, send_sems, recv_sems, after)
    return list(out[:n]), list(out[n:])


def _row_block(rows, limit=256):
    best = rows
    for cand in range(8, limit + 1, 8):
        if rows % cand == 0:
            best = cand
    return best if rows > limit else rows


def _adam(gp, w, m, v, *, name):
    p, rows, cols = gp.shape
    rb = _row_block(rows)

    def body(gp_ref, w_ref, m_ref, v_ref, g_ref, d_ref, m2_ref, v2_ref):
        g = gp_ref[0].astype(F32)
        for s in range(1, p):
            g = g + gp_ref[s].astype(F32)
        m2 = ADAM_B1 * m_ref[...] + (1.0 - ADAM_B1) * g
        v2 = ADAM_B2 * v_ref[...] + (1.0 - ADAM_B2) * (g * g)
        m_hat = m2 / (1.0 - ADAM_B1 ** ADAM_STEP)
        v_hat = v2 / (1.0 - ADAM_B2 ** ADAM_STEP)
        g_ref[...] = g
        d_ref[...] = -ADAM_LR * (m_hat / (jnp.sqrt(v_hat) + ADAM_EPS) + ADAM_WD * w_ref[...])
        m2_ref[...] = m2
        v2_ref[...] = v2

    blk = pl.BlockSpec((rb, cols), lambda i: (i, 0))
    return pl.pallas_call(
        body, name=name, grid=(rows // rb,),
        in_specs=[pl.BlockSpec((p, rb, cols), lambda i: (0, i, 0)), blk, blk, blk],
        out_specs=(blk,) * 4, out_shape=(jax.ShapeDtypeStruct((rows, cols), F32),) * 4,
        compiler_params=_cparams(("parallel",)),
    )(gp, w, m, v)


def _cols_full(g):
    return jnp.transpose(g, (1, 0, 2)).reshape(g.shape[1], N_DEV * g.shape[2])


def _pad_lanes(a, width=LANES):
    return jnp.pad(a, ((0, 0), (0, width - a.shape[1])))


def _chunk_rows_of(a):
    by_chunk = jnp.transpose(a[:, :DN_HEADS].reshape(-1, DN_CHUNK, DN_HEADS), (0, 2, 1))
    return jnp.pad(by_chunk, ((0, 0), (0, HALO - DN_HEADS), (0, 0)))


_SMALL = (("ffn1_norm", D_MODEL), ("mix_norm", D_MODEL), ("ffn2_norm", D_MODEL), ("final_norm", D_MODEL), ("a_log", DN_HEADS),
          ("dt_bias", DN_HEADS), ("dn_norm", DN_HEAD_DIM), ("sg_ln_g", SG_WIDTH), ("sg_ln_b", SG_WIDTH),
          ("sg_w", SG_GROUPS * SG_CHUNK * SG_CHUNK), ("sg_b", SG_GROUPS * SG_CHUNK), ("conv_w", CONV_K * 3 * DN_WIDTH))
_SMALL_ROWS = 1128
_SMALL_SHAPES = {"ffn1_norm": (1, D_MODEL), "mix_norm": (1, D_MODEL), "ffn2_norm": (1, D_MODEL), "final_norm": (D_MODEL,),
                 "a_log": (1, DN_HEADS), "dt_bias": (1, DN_HEADS), "dn_norm": (1, DN_HEAD_DIM), "sg_ln_g": (1, SG_WIDTH),
                 "sg_ln_b": (1, SG_WIDTH), "sg_w": (1, SG_GROUPS, SG_CHUNK, SG_CHUNK), "sg_b": (1, SG_GROUPS, SG_CHUNK)}


def _pack_small(d):
    flat = jnp.concatenate([d[name].reshape(-1) for name, _ in _SMALL])
    return jnp.pad(flat, (0, _SMALL_ROWS * LANES - flat.shape[0])).reshape(_SMALL_ROWS, LANES)


def _unpack_small(a):
    flat, out, at = a.reshape(-1), {}, 0
    for name, size in _SMALL:
        out[name] = flat[at:at + size]
        at += size
    return out


def kernel(x, ffn1_norm, ffn1_w_gate, ffn1_w_up, ffn1_w_down, mix_norm, w_in, conv_w, a_log, dt_bias, dn_norm, sg_ln_g, sg_ln_b, sg_w, sg_b, w_out, ffn2_norm, ffn2_w_gate, ffn2_w_up, ffn2_w_down, final_norm, loss_target, m_ffn1_norm, m_ffn1_w_gate, m_ffn1_w_up, m_ffn1_w_down, m_mix_norm, m_w_in, m_conv_w, m_a_log, m_dt_bias, m_dn_norm, m_sg_ln_g, m_sg_ln_b, m_sg_w, m_sg_b, m_w_out, m_ffn2_norm, m_ffn2_w_gate, m_ffn2_w_up, m_ffn2_w_down, m_final_norm, v_ffn1_norm, v_ffn1_w_gate, v_ffn1_w_up, v_ffn1_w_down, v_mix_norm, v_w_in, v_conv_w, v_a_log, v_dt_bias, v_dn_norm, v_sg_ln_g, v_sg_ln_b, v_sg_w, v_sg_b, v_w_out, v_ffn2_norm, v_ffn2_w_gate, v_ffn2_w_up, v_ffn2_w_down, v_final_norm):
    weights = dict(ffn1_norm=ffn1_norm, ffn1_w_gate=ffn1_w_gate, ffn1_w_up=ffn1_w_up, ffn1_w_down=ffn1_w_down, mix_norm=mix_norm, w_in=w_in, conv_w=conv_w, a_log=a_log, dt_bias=dt_bias, dn_norm=dn_norm, sg_ln_g=sg_ln_g, sg_ln_b=sg_ln_b, sg_w=sg_w, sg_b=sg_b, w_out=w_out, ffn2_norm=ffn2_norm, ffn2_w_gate=ffn2_w_gate, ffn2_w_up=ffn2_w_up, ffn2_w_down=ffn2_w_down, final_norm=final_norm)
    mom_m = dict(ffn1_norm=m_ffn1_norm, ffn1_w_gate=m_ffn1_w_gate, ffn1_w_up=m_ffn1_w_up, ffn1_w_down=m_ffn1_w_down, mix_norm=m_mix_norm, w_in=m_w_in, conv_w=m_conv_w, a_log=m_a_log, dt_bias=m_dt_bias, dn_norm=m_dn_norm, sg_ln_g=m_sg_ln_g, sg_ln_b=m_sg_ln_b, sg_w=m_sg_w, sg_b=m_sg_b, w_out=m_w_out, ffn2_norm=m_ffn2_norm, ffn2_w_gate=m_ffn2_w_gate, ffn2_w_up=m_ffn2_w_up, ffn2_w_down=m_ffn2_w_down, final_norm=m_final_norm)
    mom_v = dict(ffn1_norm=v_ffn1_norm, ffn1_w_gate=v_ffn1_w_gate, ffn1_w_up=v_ffn1_w_up, ffn1_w_down=v_ffn1_w_down, mix_norm=v_mix_norm, w_in=v_w_in, conv_w=v_conv_w, a_log=v_a_log, dt_bias=v_dt_bias, dn_norm=v_dn_norm, sg_ln_g=v_sg_ln_g, sg_ln_b=v_sg_ln_b, sg_w=v_sg_w, sg_b=v_sg_b, w_out=v_w_out, ffn2_norm=v_ffn2_norm, ffn2_w_gate=v_ffn2_w_gate, ffn2_w_up=v_ffn2_w_up, ffn2_w_down=v_ffn2_w_down, final_norm=v_final_norm)
    order = list(weights)
    big = ("ffn1_w_gate", "ffn1_w_up", "ffn1_w_down", "w_in", "w_out", "ffn2_w_gate", "ffn2_w_up", "ffn2_w_down")
    col_sharded = ("ffn1_w_gate", "ffn1_w_up", "w_in", "ffn2_w_gate", "ffn2_w_up")

    n_seq, seq, _ = x.shape
    t = n_seq * seq
    me = 4 * lax.axis_index("x") + 2 * lax.axis_index("y") + lax.axis_index("c")
    x0 = x.reshape(t, D_MODEL)
    tgt = loss_target.reshape(t, D_MODEL)

    def fill_own(land, own_block):
        return lax.dynamic_update_index_in_dim(land, own_block, me, 0)

    def rows_view(n, a):
        return jnp.transpose(a) if n in col_sharded else a

    def as_full(n, g):
        return g.reshape(-1, g.shape[-1])

    shards = {n: rows_view(n, weights[n][0]).astype(BF16) for n in big}
    ffn1_names, mix_names, ffn2_names = big[:3], big[3:5], big[5:]
    full = {n: as_full(n, g) for n, g in zip(ffn1_names, _gather2([shards[n] for n in ffn1_names], name="gather_ffn1"))}
    mix_srcs = [shards[n] for n in mix_names] + [conv_w[0]]
    mix_started, mix_token = _send_start(mix_srcs, True, full[ffn1_names[2]], name="gather_mix_start")
    ffn2_started, ffn2_token = _send_start([shards[n] for n in ffn2_names], True, mix_token, name="gather_ffn2_start")
    ffn1_norm_fwd = ffn1_norm + ffn2_token[:1, :1]
    alog, dtb = _pad_lanes(a_log), _pad_lanes(dt_bias)
    sgbt = _pad_lanes(sg_b[0].T)
    fnw = final_norm.reshape(1, D_MODEL)

    x1, h1, g1, u1 = _ffn_fwd(x0, ffn1_norm_fwd, full["ffn1_w_gate"], full["ffn1_w_up"], full["ffn1_w_down"], name="ffn1_fwd")
    mix_lands = [fill_own(land, src) for src, land in zip(*_send_wait(mix_started, True, x1, name="gather_mix_wait"))]
    full.update({n: as_full(n, g) for n, g in zip(mix_names, mix_lands)})
    conv_full = _cols_full(mix_lands[-1])
    w_in_t = full["w_in"]
    offs = (0, SG_WIDTH, 2 * SG_WIDTH, 2 * SG_WIDTH + 3 * DN_WIDTH, 2 * SG_WIDTH + 4 * DN_WIDTH)
    n_proj = offs[-1]

    def pad_rows(a):
        return jnp.pad(a, ((0, LANES - a.shape[0]), (0, 0)))

    ws = [w_in_t[offs[0]:offs[1]], w_in_t[offs[1]:offs[2]], w_in_t[offs[2]:offs[3]], w_in_t[offs[3]:offs[4]],
          pad_rows(w_in_t[n_proj:n_proj + DN_HEADS]), pad_rows(w_in_t[n_proj + DN_HEADS:n_proj + 2 * DN_HEADS])]
    wo_sg, wo_dn = full["w_out"][:SG_WIDTH], full["w_out"][SG_WIDTH:]
    u, v, qkv, z, bpre, apre = _mix_in_fwd(x1, mix_norm, ws, name="mix_in_fwd")
    sg_out = _sg_fwd(u, v, sg_ln_g, sg_ln_b, sg_w[0], sgbt, name="sg_fwd")
    q, k, vv, beta, gc = _dn_prep_fwd(qkv, bpre, apre, conv_full, alog, dtb, seq, name="dn_prep_fwd")
    grow = _chunk_rows_of(gc)
    wy_w, wy_u, q_dec, k_dec, qk, egl, inv = _delta_prep(q, k, vv, gc, grow, beta, name="delta_prep")
    o, states = _delta_seq_fwd(wy_w, wy_u, q_dec, k_dec, qk, egl, n_seq, seq, name="delta_seq_fwd")
    x2 = _mix_out_fwd(x1, sg_out, o, z, wo_sg, wo_dn, dn_norm, name="mix_out_fwd")
    ffn2_srcs, ffn2_lands = _send_wait(ffn2_started, True, x2, name="gather_ffn2_wait")
    full.update({n: as_full(n, fill_own(land, src)) for n, src, land in zip(ffn2_names, ffn2_srcs, ffn2_lands)})
    dx3, loss_part, d_fn, h2, g2, u2 = _ffn_fwd(x2, ffn2_norm, full["ffn2_w_gate"], full["ffn2_w_up"], full["ffn2_w_down"],
                                                tgt, fnw, name="ffn2_fwd_loss")
    loss = lax.psum(loss_part[0, 0], ("x", "y", "c"))

    dx2, d_n2, d_g2, d_u2, d_d2 = _ffn_bwd(x2, ffn2_norm, h2, g2, u2, full["ffn2_w_gate"], full["ffn2_w_up"],
                                           full["ffn2_w_down"], dx3, name="ffn2_bwd")
    def by_owner(d_rows):
        return d_rows.reshape(N_DEV, -1, D_MODEL)

    ffn2_pieces = [by_owner(d_g2), by_owner(d_u2), by_owner(d_d2)]
    ffn2_sent, sent_token = _send_start(ffn2_pieces, False, name="grads_ffn2_start")
    dsg, do, dz, d_wo_sg, d_wo_dn, d_dnw = _mix_out_bwd(dx2, sg_out, o, z, wo_sg, wo_dn, dn_norm + sent_token[:1, :1],
                                                        name="mix_out_bwd")
    d_seq = _delta_seq_bwd(wy_w, wy_u, q_dec, k_dec, qk, egl, states, do, n_seq, seq, name="delta_seq_bwd")
    dq, dk, dv, dgc_a, dgrow, dbeta = _delta_par_bwd(q, k, vv, gc, grow, beta, inv, *d_seq, name="delta_par_bwd")
    dgc_b = _pad_lanes(jnp.transpose(dgrow[:, :DN_HEADS, :], (0, 2, 1)).reshape(t, DN_HEADS))
    dqkv, d_conv, dbpre, dapre, d_alog, d_dtb = _dn_prep_bwd(qkv, bpre, apre, conv_full, alog, dtb, dq, dk, dv, dbeta, dgc_a,
                                                             dgc_b, seq, name="dn_prep_bwd")
    du, dvv, d_lng, d_lnb, d_wc, d_sgbt = _sg_bwd(u, v, sg_ln_g, sg_ln_b, sg_w[0], sgbt, dsg, name="sg_bwd")
    dx1, d_mixn, d_wp = _mix_in_bwd(x1, mix_norm, ws, dx2, (du, dvv, dqkv, dz, dbpre, dapre), name="mix_in_bwd")
    d_w_in_t = jnp.concatenate([d_wp[:n_proj], d_wp[_PROJ_OFFSETS[4]:_PROJ_OFFSETS[4] + DN_HEADS],
                                d_wp[_PROJ_OFFSETS[5]:_PROJ_OFFSETS[5] + DN_HEADS]], axis=0)
    d_w_out = jnp.concatenate([d_wo_sg, d_wo_dn], axis=0)
    mix_pieces = [by_owner(d_w_in_t), by_owner(d_w_out).astype(BF16)]
    mix_sent, sent_token = _send_start(mix_pieces, False, name="grads_mix_start")
    grad_x, d_n1, dg1, du1, a1, dyh1 = _ffn_bwd_x(x0, ffn1_norm + sent_token[:1, :1], g1, u1, full["ffn1_w_gate"],
                                                  full["ffn1_w_up"], full["ffn1_w_down"], dx1, name="ffn1_bwd_x")
    small_grads = dict(ffn1_norm=d_n1, mix_norm=d_mixn, ffn2_norm=d_n2, final_norm=d_fn, a_log=d_alog[:, :DN_HEADS],
                       dt_bias=d_dtb[:, :DN_HEADS], dn_norm=d_dnw, sg_ln_g=d_lng, sg_ln_b=d_lnb, sg_w=d_wc,
                       sg_b=d_sgbt[:, :SG_GROUPS].T, conv_w=d_conv[:CONV_K])
    small_src = _pack_small(small_grads)
    small_sent, small_token = _send_start([small_src], True, name="small_grads_start")
    late, tokens = [], []

    def send_early(k, grad):
        piece = by_owner(grad)
        sent, token = _send_start([piece], False, name="grads_" + ffn1_names[k] + "_start")
        late.append(((ffn1_names[k],), sent))
        tokens.append(token)
        return token

    _ffn_wgrads(h1, dg1, du1, a1, dyh1, send_early, small_token, name="ffn1_bwd")

    res = {}
    after = tokens[-1]

    def update(names, sent, after):
        pieces, lands = _send_wait(sent, False, after, name="grads_" + names[0] + "_wait")
        for n, land, p in zip(names, lands, pieces):
            got = fill_own(land, lax.dynamic_index_in_dim(p, me, 0, keepdims=False))
            upd = _adam(got, *[rows_view(n, src[n][0]) for src in (weights, mom_m, mom_v)], name="adam_" + n)
            res[n] = [rows_view(n, a) for a in upd]
            after = upd[0]
        return after

    for group in [(ffn2_names, ffn2_sent), (mix_names, mix_sent)] + late[:-1]:
        after = update(*group, after)
    (small_src,), (small_land,) = _send_wait(small_sent, True, after, name="small_grads_wait")
    small_parts = fill_own(small_land, small_src)
    zeros_conv = jnp.zeros((CONV_K * 3 * DN_WIDTH,), F32)
    packed = [_pack_small({**{n: src[n] for n, _ in _SMALL if n != "conv_w"}, "conv_w": zeros_conv})
              for src in (weights, mom_m, mom_v)]
    small_upd = _adam(small_parts, *packed, name="adam_small")
    small_res = [_unpack_small(a) for a in small_upd]
    conv_grad = lax.dynamic_slice_in_dim(small_res[0]["conv_w"].reshape(CONV_K, 3 * DN_WIDTH), me * (3 * DN_WIDTH // N_DEV),
                                         3 * DN_WIDTH // N_DEV, axis=1)
    res["conv_w"] = _adam(conv_grad[None], conv_w[0], m_conv_w[0], v_conv_w[0], name="adam_conv_w")
    update(*late[-1], res["conv_w"][0])

    outs = [[], [], [], []]
    for n in order:
        for kind in range(4):
            if n in res:
                outs[kind].append(res[n][kind][None])
            else:
                outs[kind].append(small_res[kind][n].reshape(_SMALL_SHAPES[n]))
    return (loss, grad_x.reshape(x.shape), *outs[0], *outs[1], *outs[2], *outs[3])
```

```python
import jax
import jax.numpy as jnp
from jax import lax
from jax.experimental import pallas as pl
from jax.experimental.pallas import tpu as pltpu

F32 = jnp.float32
BF16 = jnp.bfloat16

D_MODEL = 1024
D_FF = 2816
SG_WIDTH = 512
SG_GROUPS = 8
SG_GROUP_DIM = 64
SG_CHUNK = 128
DN_WIDTH = 512
DN_HEAD_DIM = 128
DN_HEADS = 4
DN_CHUNK = 64
CONV_K = 4
EPS = 1e-6
N_DEV = 8
LANES = 128
HALO = 8
MXU_COLS = 256

ADAM_LR = 0.001
ADAM_B1 = 0.9
ADAM_B2 = 0.999
ADAM_EPS = 1e-08
ADAM_WD = 0.01
ADAM_STEP = 10

VMEM_LIMIT = 60 * 1024 * 1024
WGRAD_K_TILE = 2048
TOKEN_BLOCK = 512
FF_BLOCK_FWD = 1408

_HI = lax.Precision.HIGHEST


def _cparams(sem):
    return pltpu.CompilerParams(dimension_semantics=sem, vmem_limit_bytes=VMEM_LIMIT)


def _tm(t, pref=TOKEN_BLOCK):
    return min(pref, t)


def _dg(a, b, ca, cb, precision):
    if precision is not None:
        return lax.dot_general(a, b, (((ca,), (cb,)), ((), ())), precision=precision, preferred_element_type=F32)
    return lax.dot_general(a.astype(BF16), b.astype(BF16), (((ca,), (cb,)), ((), ())), preferred_element_type=F32)


def _make_mm(precision):
    @jax.custom_vjp
    def mm(a, b):
        return _dg(a, b, 1, 0, precision)

    @jax.custom_vjp
    def mm_nt(a, b):
        return _dg(a, b, 1, 1, precision)

    @jax.custom_vjp
    def mm_tn(a, b):
        return _dg(a, b, 0, 0, precision)

    mm.defvjp(lambda a, b: (mm(a, b), (a, b)), lambda r, g: (mm_nt(g, r[1]), mm_tn(r[0], g)))
    mm_nt.defvjp(lambda a, b: (mm_nt(a, b), (a, b)), lambda r, g: (mm(g, r[1]), mm_tn(g, r[0])))
    mm_tn.defvjp(lambda a, b: (mm_tn(a, b), (a, b)), lambda r, g: (mm_nt(r[1], g), mm(r[0], g)))
    return mm, mm_nt, mm_tn


mm, mm_nt, mm_tn = _make_mm(None)
mmx, mmx_nt, mmx_tn = _make_mm(_HI)
mmh, mmh_nt, mmh_tn = _make_mm(lax.Precision.HIGH)


def _sigmoid(x):
    return 1.0 / (1.0 + jnp.exp(-x))


def _silu(x):
    return x * _sigmoid(x)


def _softplus(x):
    neg_abs = jnp.where(x > 0, -x, x)
    return jnp.where(x > 0, x, 0.0) + jnp.log(1.0 + jnp.exp(neg_abs))


def _gelu(x):
    return 0.5 * x * (1.0 + jnp.tanh(0.7978845608028654 * (x + 0.044715 * (x * x * x))))


def _rms_fwd(x, g):
    r = lax.rsqrt(jnp.mean(x * x, axis=-1, keepdims=True) + EPS)
    xh = x * r
    return xh * g, xh, r


def _rms_bwd(dh, xh, r, g):
    dxh = dh * g
    dx = r * (dxh - xh * jnp.mean(dxh * xh, axis=-1, keepdims=True))
    return dx, jnp.sum(dh * xh, axis=0, keepdims=True)


def _acc_out(ref, first, val):
    @pl.when(first)
    def _():
        ref[...] = val

    @pl.when(jnp.logical_not(first))
    def _():
        ref[...] += val


def _ffn_fwd(x, nw, wg, wu, wd, tgt=None, fnw=None, *, name):
    t = x.shape[0]
    tm, fb = _tm(t), FF_BLOCK_FWD
    n_t, n_f = t // tm, D_FF // fb
    with_loss = tgt is not None

    def body(*refs):
        if with_loss:
            (x_ref, nw_ref, wg_ref, wu_ref, wd_ref, tgt_ref, fnw_ref, dy_ref, loss_ref, dfn_ref, h_ref, g_ref, u_ref,
             acc_s) = refs
        else:
            x_ref, nw_ref, wg_ref, wu_ref, wd_ref, y_ref, h_ref, g_ref, u_ref, acc_s = refs
        i, j = pl.program_id(0), pl.program_id(1)

        @pl.when(j == 0)
        def _():
            h, _, _ = _rms_fwd(x_ref[...], nw_ref[...])
            h_ref[...] = h.astype(BF16)
            acc_s[...] = jnp.zeros_like(acc_s)

        h = h_ref[...]
        nt = (((1,), (1,)), ((), ()))
        g = lax.dot_general(h, wg_ref[...], nt, preferred_element_type=F32)
        u = lax.dot_general(h, wu_ref[...], nt, preferred_element_type=F32)
        g_ref[...] = g.astype(BF16)
        u_ref[...] = u.astype(BF16)
        a = _silu(g) * u
        acc_s[...] += jnp.dot(a.astype(BF16), wd_ref[...], preferred_element_type=F32)

        @pl.when(j == n_f - 1)
        def _():
            y = x_ref[...] + 0.5 * acc_s[...]
            if not with_loss:
                y_ref[...] = y
            else:
                gf = fnw_ref[...]
                out, xh, r = _rms_fwd(y, gf)
                err = out - tgt_ref[...]
                part = 0.5 * jnp.sum(jnp.mean(err * err, axis=-1, keepdims=True), axis=0, keepdims=True)
                d_out = err * (1.0 / D_MODEL)
                dy, dgf = _rms_bwd(d_out, xh, r, gf)
                dy_ref[...] = dy
                _acc_out(loss_ref, i == 0, jnp.broadcast_to(part, loss_ref.shape))
                _acc_out(dfn_ref, i == 0, dgf)

    row = lambda i, j: (i, 0)
    const = lambda i, j: (0, 0)
    in_specs = [
        pl.BlockSpec((tm, D_MODEL), row),
        pl.BlockSpec((1, D_MODEL), const),
        pl.BlockSpec((fb, D_MODEL), lambda i, j: (j, 0)),
        pl.BlockSpec((fb, D_MODEL), lambda i, j: (j, 0)),
        pl.BlockSpec((fb, D_MODEL), lambda i, j: (j, 0)),
    ]
    args = [x, nw, wg, wu, wd]
    saved_shape = (jax.ShapeDtypeStruct((t, D_MODEL), BF16), jax.ShapeDtypeStruct((t, D_FF), BF16),
                   jax.ShapeDtypeStruct((t, D_FF), BF16))
    saved_specs = (pl.BlockSpec((tm, D_MODEL), row), pl.BlockSpec((tm, fb), lambda i, j: (i, j)),
                   pl.BlockSpec((tm, fb), lambda i, j: (i, j)))
    if with_loss:
        in_specs += [pl.BlockSpec((tm, D_MODEL), row), pl.BlockSpec((1, D_MODEL), const)]
        args += [tgt, fnw]
        out_shape = (jax.ShapeDtypeStruct((t, D_MODEL), F32), jax.ShapeDtypeStruct((8, LANES), F32),
                     jax.ShapeDtypeStruct((1, D_MODEL), F32)) + saved_shape
        out_specs = (pl.BlockSpec((tm, D_MODEL), row), pl.BlockSpec((8, LANES), const),
                     pl.BlockSpec((1, D_MODEL), const)) + saved_specs
        sem = ("arbitrary", "arbitrary")
    else:
        out_shape = (jax.ShapeDtypeStruct((t, D_MODEL), F32),) + saved_shape
        out_specs = (pl.BlockSpec((tm, D_MODEL), row),) + saved_specs
        sem = ("parallel", "arbitrary")
    return pl.pallas_call(
        body, name=name, grid=(n_t, n_f), in_specs=in_specs, out_specs=out_specs, out_shape=out_shape,
        scratch_shapes=[pltpu.VMEM((tm, D_MODEL), F32)],
        compiler_params=_cparams(sem),
    )(*args)


def _ffn_bwd_x(x, nw, g, u, wg, wu, wd, dy, *, name):
    t = x.shape[0]
    tm = _tm(t, 256)

    def body(x_ref, nw_ref, g_ref, u_ref, wg_ref, wu_ref, wd_ref, dy_ref, dx_ref, dnw_ref, dg_ref, du_ref, a_ref, dyh_ref):
        i = pl.program_id(0)
        nt = (((1,), (1,)), ((), ()))
        dy = dy_ref[...]
        dyh = (0.5 * dy).astype(BF16)
        dyh_ref[...] = dyh
        gate, up = g_ref[...].astype(F32), u_ref[...].astype(F32)
        s = _sigmoid(gate)
        gs = gate * s
        da = lax.dot_general(dyh, wd_ref[...], nt, preferred_element_type=F32)
        dg = (da * up * (s + gs * (1.0 - s))).astype(BF16)
        du = (da * gs).astype(BF16)
        dg_ref[...] = dg
        du_ref[...] = du
        a_ref[...] = (gs * up).astype(BF16)
        dh = (jnp.dot(dg, wg_ref[...], preferred_element_type=F32)
              + jnp.dot(du, wu_ref[...], preferred_element_type=F32))
        xv = x_ref[...]
        r = lax.rsqrt(jnp.mean(xv * xv, axis=-1, keepdims=True) + EPS)
        dx, dnw = _rms_bwd(dh, xv * r, r, nw_ref[...])
        dx_ref[...] = dy + dx
        _acc_out(dnw_ref, i == 0, dnw)

    row = lambda i: (i, 0)
    const = lambda i: (0, 0)
    once = pl.Buffered(1)
    wide = pl.BlockSpec((tm, D_FF), row)
    return pl.pallas_call(
        body, name=name, grid=(t // tm,),
        in_specs=[pl.BlockSpec((tm, D_MODEL), row), pl.BlockSpec((1, D_MODEL), const), wide, wide,
                  pl.BlockSpec((D_FF, D_MODEL), const, pipeline_mode=once), pl.BlockSpec((D_FF, D_MODEL), const, pipeline_mode=once),
                  pl.BlockSpec((D_FF, D_MODEL), const, pipeline_mode=once), pl.BlockSpec((tm, D_MODEL), row)],
        out_specs=(pl.BlockSpec((tm, D_MODEL), row), pl.BlockSpec((1, D_MODEL), const), wide, wide, wide,
                   pl.BlockSpec((tm, D_MODEL), row)),
        out_shape=(jax.ShapeDtypeStruct((t, D_MODEL), F32), jax.ShapeDtypeStruct((1, D_MODEL), F32),
                   jax.ShapeDtypeStruct((t, D_FF), BF16), jax.ShapeDtypeStruct((t, D_FF), BF16),
                   jax.ShapeDtypeStruct((t, D_FF), BF16), jax.ShapeDtypeStruct((t, D_MODEL), BF16)),
        compiler_params=_cparams(("arbitrary",)),
    )(x, nw, g, u, wg, wu, wd, dy)


def _wgrad(a, b, bm, bn, after=None, *, name):
    k, m = a.shape
    n = b.shape[1]
    tk = _tm(k, WGRAD_K_TILE)
    n_k = k // tk

    def body(a_ref, b_ref, *rest):
        o_ref, acc_s = rest[-2], rest[-1]
        s = pl.program_id(2)
        for c in range(bn // MXU_COLS):
            cols = slice(c * MXU_COLS, (c + 1) * MXU_COLS)
            part = lax.dot_general(a_ref[...], b_ref[:, cols], (((0,), (0,)), ((), ())), preferred_element_type=F32)
            acc_s[:, cols] = jnp.where(s == 0, 0.0, acc_s[:, cols]) + part

        @pl.when(s == n_k - 1)
        def _():
            o_ref[...] = acc_s[...].astype(BF16)

    return pl.pallas_call(
        body, name=name, grid=(m // bm, n // bn, n_k),
        in_specs=[pl.BlockSpec((tk, bm), lambda i, j, s: (s, i)), pl.BlockSpec((tk, bn), lambda i, j, s: (s, j))]
        + ([] if after is None else [_HBM]),
        out_specs=pl.BlockSpec((bm, bn), lambda i, j, s: (i, j)),
        out_shape=jax.ShapeDtypeStruct((m, n), BF16),
        scratch_shapes=[pltpu.VMEM((bm, bn), F32)],
        compiler_params=_cparams(("parallel", "parallel", "arbitrary")),
    )(a, b, *([] if after is None else [after]))


def _ffn_wgrads(h, dg, du, a, dyh, between=None, after=None, *, name):
    grads = []
    for k, (lhs, rhs, tag) in enumerate(((dg, h, "_wg"), (du, h, "_wu"), (a, dyh, "_wd"))):
        grads.append(_wgrad(lhs, rhs, D_FF // 2, D_MODEL, after, name=name + tag))
        after = None if between is None else between(k, grads[-1])
    return grads


def _ffn_bwd(x, nw, h, g, u, wg, wu, wd, dy, *, name):
    dx, dnw, dg, du, a, dyh = _ffn_bwd_x(x, nw, g, u, wg, wu, wd, dy, name=name + "_x")
    return (dx, dnw, *_ffn_wgrads(h, dg, du, a, dyh, name=name))


_PROJ_WIDTHS = (SG_WIDTH, SG_WIDTH, 3 * DN_WIDTH, DN_WIDTH, LANES, LANES)


def _mix_in_fwd(x, nw, ws, *, name):
    t = x.shape[0]
    tm = _tm(t)

    def body(x_ref, nw_ref, *refs):
        w_refs, o_refs = refs[:6], refs[6:]
        h, _, _ = _rms_fwd(x_ref[...], nw_ref[...])
        h = h.astype(BF16)
        for w_ref, o_ref in zip(w_refs, o_refs):
            o_ref[...] = lax.dot_general(h, w_ref[...], (((1,), (1,)), ((), ())), preferred_element_type=F32)

    row = lambda i: (i, 0)
    const = lambda i: (0, 0)
    return pl.pallas_call(
        body, name=name, grid=(t // tm,),
        in_specs=[pl.BlockSpec((tm, D_MODEL), row), pl.BlockSpec((1, D_MODEL), const)]
        + [pl.BlockSpec((n, D_MODEL), const) for n in _PROJ_WIDTHS],
        out_specs=tuple(pl.BlockSpec((tm, n), row) for n in _PROJ_WIDTHS),
        out_shape=tuple(jax.ShapeDtypeStruct((t, n), F32) for n in _PROJ_WIDTHS),
        compiler_params=_cparams(("parallel",)),
    )(x, nw, *ws)


_PROJ_TOTAL = sum(_PROJ_WIDTHS)
_PROJ_OFFSETS = tuple(sum(_PROJ_WIDTHS[:k]) for k in range(len(_PROJ_WIDTHS)))


def _mix_in_bwd(x, nw, ws, dres, dps, *, name):
    t = x.shape[0]
    tm = _tm(t, 256)

    def body(x_ref, nw_ref, dres_ref, *refs):
        w_refs, dp_refs, dx_ref, dnw_ref, h_ref, dpb_ref = refs[:6], refs[6:12], refs[12], refs[13], refs[14], refs[15]
        i = pl.program_id(0)
        hf, xh, r = _rms_fwd(x_ref[...], nw_ref[...])
        h_ref[...] = hf.astype(BF16)
        dh = jnp.zeros((tm, D_MODEL), F32)
        for w_ref, dp_ref, off, width in zip(w_refs, dp_refs, _PROJ_OFFSETS, _PROJ_WIDTHS):
            dp = dp_ref[...].astype(BF16)
            dpb_ref[:, off:off + width] = dp
            dh = dh + jnp.dot(dp, w_ref[...], preferred_element_type=F32)
        dx, dnw = _rms_bwd(dh, xh, r, nw_ref[...])
        dx_ref[...] = dres_ref[...] + dx
        _acc_out(dnw_ref, i == 0, dnw)

    row = lambda i: (i, 0)
    const = lambda i: (0, 0)
    dx, dnw, h, dpb = pl.pallas_call(
        body, name=name + "_x", grid=(t // tm,),
        in_specs=[pl.BlockSpec((tm, D_MODEL), row), pl.BlockSpec((1, D_MODEL), const), pl.BlockSpec((tm, D_MODEL), row)]
        + [pl.BlockSpec((n, D_MODEL), const) for n in _PROJ_WIDTHS]
        + [pl.BlockSpec((tm, n), row) for n in _PROJ_WIDTHS],
        out_specs=(pl.BlockSpec((tm, D_MODEL), row), pl.BlockSpec((1, D_MODEL), const), pl.BlockSpec((tm, D_MODEL), row),
                   pl.BlockSpec((tm, _PROJ_TOTAL), row)),
        out_shape=(jax.ShapeDtypeStruct((t, D_MODEL), F32), jax.ShapeDtypeStruct((1, D_MODEL), F32),
                   jax.ShapeDtypeStruct((t, D_MODEL), BF16), jax.ShapeDtypeStruct((t, _PROJ_TOTAL), BF16)),
        compiler_params=_cparams(("arbitrary",)),
    )(x, nw, dres, *ws, *dps)
    return dx, dnw, _wgrad(dpb, h, _PROJ_TOTAL // 2, D_MODEL, name=name + "_w")


_SG_TILES = SG_WIDTH // LANES


def _lane_tiles(ref, rows=slice(None)):
    return [ref[rows, p * LANES:(p + 1) * LANES] for p in range(_SG_TILES)]


def _sg_fn(u, v, lng, lnb, wcs, sgbt):
    lane = lax.broadcasted_iota(jnp.int32, (1, LANES), 1)
    rr = lax.broadcasted_iota(jnp.int32, (SG_CHUNK, SG_CHUNK), 0)
    cc = lax.broadcasted_iota(jnp.int32, (SG_CHUNK, SG_CHUNK), 1)
    per_tile = LANES // SG_GROUP_DIM
    gu, gv = [_gelu(a) for a in u], [_gelu(a) for a in v]
    mu = sum(jnp.sum(a, axis=-1, keepdims=True) for a in gv) * (1.0 / SG_WIDTH)
    cen = [a - mu for a in gv]
    var = sum(jnp.sum(a * a, axis=-1, keepdims=True) for a in cen) * (1.0 / SG_WIDTH)
    rstd = lax.rsqrt(var + EPS)
    ln = [a * rstd * g + b for a, g, b in zip(cen, lng, lnb)]
    out = []
    for p in range(_SG_TILES):
        vs = None
        for e in range(per_tile):
            g = p * per_tile + e
            in_group = jnp.logical_and(lane >= e * SG_GROUP_DIM, lane < (e + 1) * SG_GROUP_DIM)
            w_causal = jnp.where(rr >= cc, wcs[g], 0.0)
            bias = jnp.sum(jnp.where(lane == g, sgbt, 0.0), axis=1, keepdims=True)
            term = jnp.where(in_group, mm(w_causal, ln[p]) + bias, 0.0)
            vs = term if vs is None else vs + term
        out.append(gu[p] * vs)
    return out


def _sg_fwd(u, v, lng, lnb, wc, sgbt, *, name):
    t = u.shape[0]
    tm = _tm(t)

    def body(u_ref, v_ref, lng_ref, lnb_ref, wc_ref, sgbt_ref, o_ref):
        wcs = [wc_ref[g] for g in range(SG_GROUPS)]
        for c in range(tm // SG_CHUNK):
            rows = pl.ds(c * SG_CHUNK, SG_CHUNK)
            out = _sg_fn(_lane_tiles(u_ref, rows), _lane_tiles(v_ref, rows), _lane_tiles(lng_ref), _lane_tiles(lnb_ref),
                         wcs, sgbt_ref[...])
            for p in range(_SG_TILES):
                o_ref[rows, p * LANES:(p + 1) * LANES] = out[p]

    row = lambda i: (i, 0)
    const = lambda i: (0, 0)
    return pl.pallas_call(
        body, name=name, grid=(t // tm,),
        in_specs=[pl.BlockSpec((tm, SG_WIDTH), row), pl.BlockSpec((tm, SG_WIDTH), row),
                  pl.BlockSpec((1, SG_WIDTH), const), pl.BlockSpec((1, SG_WIDTH), const),
                  pl.BlockSpec((SG_GROUPS, SG_CHUNK, SG_CHUNK), lambda i: (0, 0, 0)), pl.BlockSpec((SG_CHUNK, LANES), const)],
        out_specs=pl.BlockSpec((tm, SG_WIDTH), row),
        out_shape=jax.ShapeDtypeStruct((t, SG_WIDTH), F32),
        compiler_params=_cparams(("parallel",)),
    )(u, v, lng, lnb, wc, sgbt)


def _sg_bwd(u, v, lng, lnb, wc, sgbt, dout, *, name):
    t = u.shape[0]
    tm = _tm(t)

    def body(u_ref, v_ref, lng_ref, lnb_ref, wc_ref, sgbt_ref, do_ref, du_ref, dv_ref, dlng_ref, dlnb_ref, dwc_ref, dsgbt_ref):
        i = pl.program_id(0)
        wcs = [wc_ref[g] for g in range(SG_GROUPS)]
        tot = None
        for c in range(tm // SG_CHUNK):
            rows = pl.ds(c * SG_CHUNK, SG_CHUNK)
            _, vjp = jax.vjp(_sg_fn, _lane_tiles(u_ref, rows), _lane_tiles(v_ref, rows), _lane_tiles(lng_ref),
                             _lane_tiles(lnb_ref), wcs, sgbt_ref[...])
            du, dv, dlng, dlnb, dwcs, dsgbt = vjp(_lane_tiles(do_ref, rows))
            for p in range(_SG_TILES):
                du_ref[rows, p * LANES:(p + 1) * LANES] = du[p].astype(BF16)
                dv_ref[rows, p * LANES:(p + 1) * LANES] = dv[p].astype(BF16)
            part = (dlng, dlnb, dwcs, dsgbt)
            tot = part if tot is None else jax.tree.map(jnp.add, tot, part)
        dlng, dlnb, dwcs, dsgbt = tot
        _acc_out(dlng_ref, i == 0, jnp.concatenate(dlng, axis=1))
        _acc_out(dlnb_ref, i == 0, jnp.concatenate(dlnb, axis=1))
        _acc_out(dsgbt_ref, i == 0, dsgbt)
        for g in range(SG_GROUPS):
            @pl.when(i == 0)
            def _(g=g):
                dwc_ref[g] = dwcs[g]

            @pl.when(i > 0)
            def _(g=g):
                dwc_ref[g] += dwcs[g]

    row = lambda i: (i, 0)
    const = lambda i: (0, 0)
    wspec = pl.BlockSpec((SG_GROUPS, SG_CHUNK, SG_CHUNK), lambda i: (0, 0, 0))
    return pl.pallas_call(
        body, name=name, grid=(t // tm,),
        in_specs=[pl.BlockSpec((tm, SG_WIDTH), row), pl.BlockSpec((tm, SG_WIDTH), row),
                  pl.BlockSpec((1, SG_WIDTH), const), pl.BlockSpec((1, SG_WIDTH), const), wspec,
                  pl.BlockSpec((SG_CHUNK, LANES), const), pl.BlockSpec((tm, SG_WIDTH), row)],
        out_specs=(pl.BlockSpec((tm, SG_WIDTH), row), pl.BlockSpec((tm, SG_WIDTH), row),
                   pl.BlockSpec((1, SG_WIDTH), const), pl.BlockSpec((1, SG_WIDTH), const), wspec,
                   pl.BlockSpec((SG_CHUNK, LANES), const)),
        out_shape=(jax.ShapeDtypeStruct((t, SG_WIDTH), BF16), jax.ShapeDtypeStruct((t, SG_WIDTH), BF16),
                   jax.ShapeDtypeStruct((1, SG_WIDTH), F32), jax.ShapeDtypeStruct((1, SG_WIDTH), F32),
                   jax.ShapeDtypeStruct((SG_GROUPS, SG_CHUNK, SG_CHUNK), F32), jax.ShapeDtypeStruct((SG_CHUNK, LANES), F32)),
        compiler_params=_cparams(("arbitrary",)),
    )(u, v, lng, lnb, wc, sgbt, dout)


def _conv_taps(ext, w, tm):
    y = None
    for j in range(CONV_K):
        s = CONV_K - 1 - j
        shifted = ext if s == 0 else pltpu.roll(ext, s, 0)
        term = w[j:j + 1, :] * shifted[HALO:HALO + tm, :]
        y = term if y is None else y + term
    return y


def _post_conv(yq, yk, yv, bpre, apre, alog, dtb):
    def l2(a):
        return a * lax.rsqrt(jnp.sum(a * a, axis=-1, keepdims=True) + EPS)

    q = [l2(_silu(a)) for a in yq]
    k = [l2(_silu(a)) for a in yk]
    return q, k, _silu(yv), _sigmoid(bpre), -jnp.exp(alog) * _softplus(apre + dtb)


def _chunk_tril(tm):
    rr = lax.broadcasted_iota(jnp.int32, (tm, tm), 0)
    cc = lax.broadcasted_iota(jnp.int32, (tm, tm), 1)
    shift = DN_CHUNK.bit_length() - 1
    same = jnp.right_shift(rr, shift) == jnp.right_shift(cc, shift)
    return jnp.where(jnp.logical_and(same, rr >= cc), 1.0, 0.0).astype(F32)


def _halo_specs(tm, width, n_blocks_seq, n_blocks):
    per = tm // HALO
    prev = pl.BlockSpec((HALO, width), lambda i: (jnp.maximum(i * per - 1, 0), 0))
    nxt = pl.BlockSpec((HALO, width), lambda i: (jnp.minimum((i + 1) * per, n_blocks * per - 1), 0))
    return prev, nxt


def _split_heads(ref, base):
    return [ref[:, base + h * DN_HEAD_DIM: base + (h + 1) * DN_HEAD_DIM] for h in range(DN_HEADS)]


def _dn_prep_fwd(qkv, bpre, apre, conv_w, alog, dtb, seq, *, name):
    t = qkv.shape[0]
    tm = _tm(t)
    bps = seq // tm
    cw = 3 * DN_WIDTH

    def body(x_ref, halo_ref, b_ref, a_ref, w_ref, alog_ref, dtb_ref, q_ref, k_ref, v_ref, beta_ref, gc_ref):
        i = pl.program_id(0)
        keep = jnp.where(i % bps == 0, 0.0, 1.0)
        ext = jnp.concatenate([halo_ref[...] * keep, x_ref[...]], axis=0)
        y = _conv_taps(ext, w_ref[...], tm)
        yq = [y[:, h * DN_HEAD_DIM:(h + 1) * DN_HEAD_DIM] for h in range(DN_HEADS)]
        yk = [y[:, DN_WIDTH + h * DN_HEAD_DIM: DN_WIDTH + (h + 1) * DN_HEAD_DIM] for h in range(DN_HEADS)]
        q, k, v, beta, g = _post_conv(yq, yk, y[:, 2 * DN_WIDTH:], b_ref[...], a_ref[...], alog_ref[...], dtb_ref[...])
        for h in range(DN_HEADS):
            q_ref[:, h * DN_HEAD_DIM:(h + 1) * DN_HEAD_DIM] = q[h]
            k_ref[:, h * DN_HEAD_DIM:(h + 1) * DN_HEAD_DIM] = k[h]
        v_ref[...] = v
        beta_ref[...] = beta
        gc_ref[...] = mmx(_chunk_tril(tm), g)

    row = lambda i: (i, 0)
    const = lambda i: (0, 0)
    prev, _ = _halo_specs(tm, cw, bps, t // tm)
    return pl.pallas_call(
        body, name=name, grid=(t // tm,),
        in_specs=[pl.BlockSpec((tm, cw), row), prev, pl.BlockSpec((tm, LANES), row), pl.BlockSpec((tm, LANES), row),
                  pl.BlockSpec((CONV_K, cw), const), pl.BlockSpec((1, LANES), const), pl.BlockSpec((1, LANES), const)],
        out_specs=tuple(pl.BlockSpec((tm, n), row) for n in (DN_WIDTH, DN_WIDTH, DN_WIDTH, LANES, LANES)),
        out_shape=tuple(jax.ShapeDtypeStruct((t, n), F32) for n in (DN_WIDTH, DN_WIDTH, DN_WIDTH, LANES, LANES)),
        compiler_params=_cparams(("parallel",)),
    )(qkv, qkv, bpre, apre, conv_w, alog, dtb)


def _y_heads(y):
    yq = [y[:, h * DN_HEAD_DIM:(h + 1) * DN_HEAD_DIM] for h in range(DN_HEADS)]
    yk = [y[:, DN_WIDTH + h * DN_HEAD_DIM: DN_WIDTH + (h + 1) * DN_HEAD_DIM] for h in range(DN_HEADS)]
    return yq, yk, y[:, 2 * DN_WIDTH:]


def _dn_prep_bwd(qkv, bpre, apre, conv_w, alog, dtb, dq, dk, dv, dbeta, dgc, dgc2, seq, *, name):
    t = qkv.shape[0]
    tm = _tm(t)
    bps = seq // tm
    cw = 3 * DN_WIDTH
    n_ext = tm + HALO

    def body(x_ref, halo_ref, xn_ref, b_ref, a_ref, w_ref, alog_ref, dtb_ref, dq_ref, dk_ref, dv_ref, dqn_ref, dkn_ref,
             dvn_ref, dbeta_ref, dgc_ref, dgc2_ref, dx_ref, dw_ref, db_ref, da_ref, dalog_ref, ddtb_ref):
        i = pl.program_id(0)
        keep_prev = jnp.where(i % bps == 0, 0.0, 1.0)
        keep_next = jnp.where(i % bps == bps - 1, 0.0, 1.0)
        w = w_ref[...]
        x = x_ref[...]
        ext = jnp.concatenate([halo_ref[...] * keep_prev, x], axis=0)
        yq, yk, yv = _y_heads(_conv_taps(ext, w, tm))
        _, vjp = jax.vjp(_post_conv, yq, yk, yv, b_ref[...], a_ref[...], alog_ref[...], dtb_ref[...])
        dg = mmx_tn(_chunk_tril(tm), dgc_ref[...] + dgc2_ref[...])
        dyq, dyk, dyv, db, da, dalog, ddtb = vjp((_split_heads(dq_ref, 0), _split_heads(dk_ref, 0), dv_ref[...],
                                                  dbeta_ref[...], dg))
        dy = jnp.concatenate(dyq + dyk + [dyv], axis=1)
        ext_n = jnp.concatenate([x[tm - HALO:, :], xn_ref[...]], axis=0)
        _, vjp_n = jax.vjp(lambda *ys: _post_conv(*ys, b_ref[:HALO, :], a_ref[:HALO, :], alog_ref[...], dtb_ref[...])[:3],
                           *_y_heads(_conv_taps(ext_n, w, HALO)))
        dyq_n, dyk_n, dyv_n = vjp_n((_split_heads(dqn_ref, 0), _split_heads(dkn_ref, 0), dvn_ref[...]))
        dyext = jnp.concatenate([dy, jnp.concatenate(dyq_n + dyk_n + [dyv_n], axis=1) * keep_next], axis=0)

        @pl.when(i == 0)
        def _():
            dw_ref[...] = jnp.zeros_like(dw_ref)

        dx = None
        for j in range(CONV_K):
            s = CONV_K - 1 - j
            fut = dyext if s == 0 else pltpu.roll(dyext, n_ext - s, 0)
            term = w[j:j + 1, :] * fut[0:tm, :]
            dx = term if dx is None else dx + term
            past = ext if s == 0 else pltpu.roll(ext, s, 0)
            dw_ref[j:j + 1, :] += jnp.sum(dy * past[HALO:HALO + tm, :], axis=0, keepdims=True)
        dx_ref[...] = dx.astype(BF16)
        db_ref[...] = db.astype(BF16)
        da_ref[...] = da.astype(BF16)
        _acc_out(dalog_ref, i == 0, dalog)
        _acc_out(ddtb_ref, i == 0, ddtb)

    row = lambda i: (i, 0)
    const = lambda i: (0, 0)
    prev, nxt = _halo_specs(tm, cw, bps, t // tm)
    _, nxt_h = _halo_specs(tm, DN_WIDTH, bps, t // tm)
    tok = pl.BlockSpec((tm, DN_WIDTH), row)
    lanes = pl.BlockSpec((tm, LANES), row)
    return pl.pallas_call(
        body, name=name, grid=(t // tm,),
        in_specs=[pl.BlockSpec((tm, cw), row), prev, nxt, lanes, lanes,
                  pl.BlockSpec((CONV_K, cw), const), pl.BlockSpec((1, LANES), const), pl.BlockSpec((1, LANES), const),
                  tok, tok, tok, nxt_h, nxt_h, nxt_h, lanes, lanes, lanes],
        out_specs=(pl.BlockSpec((tm, cw), row), pl.BlockSpec((HALO, cw), const), lanes, lanes,
                   pl.BlockSpec((1, LANES), const), pl.BlockSpec((1, LANES), const)),
        out_shape=(jax.ShapeDtypeStruct((t, cw), BF16), jax.ShapeDtypeStruct((HALO, cw), F32),
                   jax.ShapeDtypeStruct((t, LANES), BF16), jax.ShapeDtypeStruct((t, LANES), BF16),
                   jax.ShapeDtypeStruct((1, LANES), F32), jax.ShapeDtypeStruct((1, LANES), F32)),
        compiler_params=_cparams(("arbitrary",)),
    )(qkv, qkv, qkv, bpre, apre, conv_w, alog, dtb, dq, dk, dv, dq, dk, dv, dbeta, dgc, dgc2)


def _inv_unit_lower(l_mats, eye):
    invs = [eye - l for l in l_mats]
    powers = list(l_mats)
    n = 2
    while n < eye.shape[0]:
        f = mmh if n == 2 else mm
        powers = [f(p, p) for p in powers]
        invs = [inv + f(inv, p) for inv, p in zip(invs, powers)]
        n *= 2
    return invs


@jax.custom_vjp
def _solve(l_mat, rhs, inv):
    return mmh(inv, rhs)


def _solve_fwd(l_mat, rhs, inv):
    sol = mmh(inv, rhs)
    return sol, (inv, sol)


def _solve_bwd(res, d_sol):
    inv, sol = res
    d_rhs = mm_tn(inv, d_sol)
    return -mm_nt(d_rhs, sol), d_rhs, jnp.zeros_like(inv)


_solve.defvjp(_solve_fwd, _solve_bwd)


def _prep_fn(q, k, v, gc, gr, b, inv):
    ids = range(len(q))
    c = q[0].shape[0]
    rr = lax.broadcasted_iota(jnp.int32, (c, c), 0)
    cc = lax.broadcasted_iota(jnp.int32, (c, c), 1)
    incl, strict = rr >= cc, rr > cc
    is_last = lax.broadcasted_iota(jnp.int32, (c, 1), 0) == c - 1
    qs = [q[i] * (DN_HEAD_DIM ** -0.5) for i in ids]
    decay = [jnp.where(incl, jnp.exp(jnp.where(incl, gc[i] - gr[i], 0.0)), 0.0) for i in ids]
    kb = [k[i] * b[i] for i in ids]
    vb = [v[i] * b[i] for i in ids]
    kk = [mm_nt(kb[i], k[i]) for i in ids]
    l_mat = [jnp.where(strict, kk[i] * decay[i], 0.0) for i in ids]
    eg = [jnp.exp(gc[i]) for i in ids]
    if inv is None:
        inv = _inv_unit_lower(l_mat, jnp.where(rr == cc, 1.0, 0.0).astype(F32))
    u_wy = [_solve(l_mat[i], vb[i], inv[i]) for i in ids]
    w_wy = [_solve(l_mat[i], kb[i] * eg[i], inv[i]) for i in ids]
    qk = [mm_nt(qs[i], k[i]) * decay[i] for i in ids]
    g_last = [jnp.sum(jnp.where(is_last, gc[i], 0.0), axis=0, keepdims=True) for i in ids]
    k_dec = [k[i] * jnp.exp(g_last[i] - gc[i]) for i in ids]
    egl = [jnp.broadcast_to(jnp.exp(g_last[i]), (1, LANES)) for i in ids]
    return [(w_wy[i], u_wy[i], qs[i] * eg[i], k_dec[i], qk[i], egl[i]) for i in ids], inv


def _seq_fn(w, u, qd, kd, qk, egl, s):
    ids = range(len(w))
    ws = [mm(w[i], s[i]) for i in ids]
    qs = [mm(qd[i], s[i]) for i in ids]
    v_new = [u[i] - ws[i] for i in ids]
    o = [qs[i] + mm(qk[i], v_new[i]) for i in ids]
    s_new = [s[i] * egl[i] + mm_tn(kd[i], v_new[i]) for i in ids]
    return o, s_new


def _lane_col(a, h):
    lane = lax.broadcasted_iota(jnp.int32, (1, LANES), 1)
    return jnp.sum(jnp.where(lane == h, a, 0.0), axis=1, keepdims=True)


def _col_lane(col, h):
    lane = lax.broadcasted_iota(jnp.int32, (1, LANES), 1)
    return jnp.where(lane == h, col, 0.0)


def _head_cols(h):
    return slice(h * DN_HEAD_DIM, (h + 1) * DN_HEAD_DIM)


def _chunk_rows(n):
    return pl.ds(pl.multiple_of(n * DN_CHUNK, DN_CHUNK), DN_CHUNK)


def _delta_prep(q, k, v, gc, grow, beta, *, name):
    t = q.shape[0]
    tm = _tm(t)
    cpb = tm // DN_CHUNK
    n_chunks = t // DN_CHUNK
    group = 2

    def body(q_ref, k_ref, v_ref, gc_ref, gr_ref, b_ref, w_ref, u_ref, qd_ref, kd_ref, qk_ref, egl_ref, inv_ref):
        def step(m, carry):
            probs = [(m * group + e, h) for e in range(group) for h in range(DN_HEADS)]
            gcb = [gc_ref[_chunk_rows(m * group + e), :] for e in range(group)]
            bb = [b_ref[_chunk_rows(m * group + e), :] for e in range(group)]
            grb = [gr_ref[m * group + e] for e in range(group)]
            for e in range(group):
                egl_ref[m * group + e] = jnp.zeros((HALO, LANES), F32)
            outs, invs = _prep_fn(
                [q_ref[_chunk_rows(n), _head_cols(h)] for n, h in probs], [k_ref[_chunk_rows(n), _head_cols(h)] for n, h in probs],
                [v_ref[_chunk_rows(n), _head_cols(h)] for n, h in probs],
                [_lane_col(gcb[e], h) for e in range(group) for h in range(DN_HEADS)],
                [grb[e][h:h + 1, :] for e in range(group) for h in range(DN_HEADS)],
                [_lane_col(bb[e], h) for e in range(group) for h in range(DN_HEADS)], None)
            for (n, h), (w, u, qd, kd, qk, egl), inv in zip(probs, outs, invs):
                rows, cols = _chunk_rows(n), _head_cols(h)
                w_ref[rows, cols] = w.astype(BF16)
                u_ref[rows, cols] = u
                qd_ref[rows, cols] = qd.astype(BF16)
                kd_ref[rows, cols] = kd.astype(BF16)
                qk_ref[n, h] = qk
                inv_ref[n, h] = inv
                egl_ref[n, h:h + 1, :] = egl
            return carry

        lax.fori_loop(0, cpb // group, step, 0)

    row = lambda i: (i, 0)
    tok = pl.BlockSpec((tm, DN_WIDTH), row)
    lanes = pl.BlockSpec((tm, LANES), row)
    sq = pl.BlockSpec((cpb, DN_HEADS, DN_CHUNK, DN_CHUNK), lambda i: (i, 0, 0, 0))
    return pl.pallas_call(
        body, name=name, grid=(t // tm,),
        in_specs=[tok, tok, tok, lanes, pl.BlockSpec((cpb, HALO, DN_CHUNK), lambda i: (i, 0, 0)), lanes],
        out_specs=(tok, tok, tok, tok, sq, pl.BlockSpec((cpb, HALO, LANES), lambda i: (i, 0, 0)), sq),
        out_shape=(jax.ShapeDtypeStruct((t, DN_WIDTH), BF16), jax.ShapeDtypeStruct((t, DN_WIDTH), F32),
                   jax.ShapeDtypeStruct((t, DN_WIDTH), BF16), jax.ShapeDtypeStruct((t, DN_WIDTH), BF16),
                   jax.ShapeDtypeStruct((n_chunks, DN_HEADS, DN_CHUNK, DN_CHUNK), F32),
                   jax.ShapeDtypeStruct((n_chunks, HALO, LANES), F32),
                   jax.ShapeDtypeStruct((n_chunks, DN_HEADS, DN_CHUNK, DN_CHUNK), F32)),
        compiler_params=_cparams(("parallel",)),
    )(q, k, v, gc, grow, beta)


def _delta_par_bwd(q, k, v, gc, grow, beta, inv, dw, du, dqd, dkd, dqk, degl, *, name):
    t = q.shape[0]
    tm = _tm(t)
    cpb = tm // DN_CHUNK
    n_chunks = t // DN_CHUNK
    group = 2

    def body(q_ref, k_ref, v_ref, gc_ref, gr_ref, b_ref, inv_ref, dw_ref, du_ref, dqd_ref, dkd_ref, dqk_ref, degl_ref,
             dq_ref, dk_ref, dv_ref, dgc_ref, dgr_ref, db_ref):
        def step(m, carry):
            chunks = [m * group + e for e in range(group)]
            probs = [(e, h) for e in range(group) for h in range(DN_HEADS)]
            rows = [_chunk_rows(n) for n in chunks]
            gcb, bb = [gc_ref[r, :] for r in rows], [b_ref[r, :] for r in rows]
            grb, deglb = [gr_ref[n] for n in chunks], [degl_ref[n] for n in chunks]
            for n in chunks:
                dgr_ref[n] = jnp.zeros((HALO, DN_CHUNK), F32)
            invs = [inv_ref[chunks[e], h] for e, h in probs]
            _, vjp = jax.vjp(lambda *a: _prep_fn(*a, invs)[0],
                             [q_ref[rows[e], _head_cols(h)] for e, h in probs], [k_ref[rows[e], _head_cols(h)] for e, h in probs],
                             [v_ref[rows[e], _head_cols(h)] for e, h in probs], [_lane_col(gcb[e], h) for e, h in probs],
                             [grb[e][h:h + 1, :] for e, h in probs], [_lane_col(bb[e], h) for e, h in probs])
            dq, dk, dv, dgc, dgr, db = vjp([(dw_ref[rows[e], _head_cols(h)], du_ref[rows[e], _head_cols(h)],
                                             dqd_ref[rows[e], _head_cols(h)], dkd_ref[rows[e], _head_cols(h)],
                                             dqk_ref[chunks[e], h], deglb[e][h:h + 1, :]) for e, h in probs])
            dgc_acc = [jnp.zeros((DN_CHUNK, LANES), F32) for _ in chunks]
            db_acc = [jnp.zeros((DN_CHUNK, LANES), F32) for _ in chunks]
            for i, (e, h) in enumerate(probs):
                cols = _head_cols(h)
                dq_ref[rows[e], cols] = dq[i]
                dk_ref[rows[e], cols] = dk[i]
                dv_ref[rows[e], cols] = dv[i]
                dgr_ref[chunks[e], h:h + 1, :] = dgr[i]
                dgc_acc[e] = dgc_acc[e] + _col_lane(dgc[i], h)
                db_acc[e] = db_acc[e] + _col_lane(db[i], h)
            for e in range(group):
                dgc_ref[rows[e], :] = dgc_acc[e]
                db_ref[rows[e], :] = db_acc[e]
            return carry

        lax.fori_loop(0, cpb // group, step, 0)

    row = lambda i: (i, 0)
    tok = pl.BlockSpec((tm, DN_WIDTH), row)
    lanes = pl.BlockSpec((tm, LANES), row)
    sq = pl.BlockSpec((cpb, DN_HEADS, DN_CHUNK, DN_CHUNK), lambda i: (i, 0, 0, 0))
    grs = pl.BlockSpec((cpb, HALO, DN_CHUNK), lambda i: (i, 0, 0))
    return pl.pallas_call(
        body, name=name, grid=(t // tm,),
        in_specs=[tok, tok, tok, lanes, grs, lanes, sq, tok, tok, tok, tok, sq, pl.BlockSpec((cpb, HALO, LANES), lambda i: (i, 0, 0))],
        out_specs=(tok, tok, tok, lanes, grs, lanes),
        out_shape=(jax.ShapeDtypeStruct((t, DN_WIDTH), F32),) * 3
        + (jax.ShapeDtypeStruct((t, LANES), F32), jax.ShapeDtypeStruct((n_chunks, HALO, DN_CHUNK), F32),
           jax.ShapeDtypeStruct((t, LANES), F32)),
        compiler_params=_cparams(("parallel",)),
    )(q, k, v, gc, grow, beta, inv, dw, du, dqd, dkd, dqk, degl)


def _seq_specs(n_seq, seq, reverse):
    tm = _tm(seq)
    nb = seq // tm
    cpb = tm // DN_CHUNK
    pair = 2 if n_seq % 2 == 0 else 1
    blk = (lambda j: nb - 1 - j) if reverse else (lambda j: j)
    tok = pl.BlockSpec((pair, tm, DN_WIDTH), lambda b, j: (b, blk(j), 0))
    sq = pl.BlockSpec((pair, cpb, DN_HEADS, DN_CHUNK, DN_CHUNK), lambda b, j: (b, blk(j), 0, 0, 0))
    rows8 = pl.BlockSpec((pair, cpb, HALO, LANES), lambda b, j: (b, blk(j), 0, 0))
    state = pl.BlockSpec((pair, cpb, DN_HEADS, DN_HEAD_DIM, DN_HEAD_DIM), lambda b, j: (b, blk(j), 0, 0, 0))
    return nb, cpb, pair, tok, sq, rows8, state


def _by_seq(a, n_seq):
    return a.reshape((n_seq, a.shape[0] // n_seq) + a.shape[1:])


def _flat_seq(a):
    return a.reshape((a.shape[0] * a.shape[1],) + a.shape[2:])


def _delta_seq_fwd(w, u, qd, kd, qk, egl, n_seq, seq, *, name):
    nb, cpb, pair, tok, sq, rows8, state = _seq_specs(n_seq, seq, False)
    probs = [(e, h) for e in range(pair) for h in range(DN_HEADS)]

    def body(w_ref, u_ref, qd_ref, kd_ref, qk_ref, egl_ref, o_ref, st_ref, s_s):
        @pl.when(pl.program_id(1) == 0)
        def _():
            s_s[...] = jnp.zeros_like(s_s)

        def step(n, carry):
            rows = _chunk_rows(n)
            eglb = [egl_ref[e, n] for e in range(pair)]
            s = [s_s[e, h] for e, h in probs]
            for (e, h), s_eh in zip(probs, s):
                st_ref[e, n, h] = s_eh
            o, s_new = _seq_fn([w_ref[e, rows, _head_cols(h)] for e, h in probs], [u_ref[e, rows, _head_cols(h)] for e, h in probs],
                               [qd_ref[e, rows, _head_cols(h)] for e, h in probs], [kd_ref[e, rows, _head_cols(h)] for e, h in probs],
                               [qk_ref[e, n, h] for e, h in probs], [eglb[e][h:h + 1, :] for e, h in probs], s)
            for i, (e, h) in enumerate(probs):
                o_ref[e, rows, _head_cols(h)] = o[i]
                s_s[e, h] = s_new[i]
            return carry

        lax.fori_loop(0, cpb, step, 0)

    o, states = pl.pallas_call(
        body, name=name, grid=(n_seq // pair, nb),
        in_specs=[tok, tok, tok, tok, sq, rows8],
        out_specs=(tok, state),
        out_shape=(jax.ShapeDtypeStruct((n_seq, seq, DN_WIDTH), F32),
                   jax.ShapeDtypeStruct((n_seq, seq // DN_CHUNK, DN_HEADS, DN_HEAD_DIM, DN_HEAD_DIM), F32)),
        scratch_shapes=[pltpu.VMEM((pair, DN_HEADS, DN_HEAD_DIM, DN_HEAD_DIM), F32)],
        compiler_params=_cparams(("parallel", "arbitrary")),
    )(*[_by_seq(a, n_seq) for a in (w, u, qd, kd, qk, egl)])
    return _flat_seq(o), _flat_seq(states)


def _delta_seq_bwd(w, u, qd, kd, qk, egl, states, do, n_seq, seq, *, name):
    nb, cpb, pair, tok, sq, rows8, state = _seq_specs(n_seq, seq, True)
    probs = [(e, h) for e in range(pair) for h in range(DN_HEADS)]

    def body(w_ref, u_ref, qd_ref, kd_ref, qk_ref, egl_ref, st_ref, do_ref, dw_ref, du_ref, dqd_ref, dkd_ref, dqk_ref,
             degl_ref, ds_s):
        @pl.when(pl.program_id(1) == 0)
        def _():
            ds_s[...] = jnp.zeros_like(ds_s)

        def step(m, carry):
            n = cpb - 1 - m
            rows = _chunk_rows(n)
            eglb = [egl_ref[e, n] for e in range(pair)]
            for e in range(pair):
                degl_ref[e, n] = jnp.zeros((HALO, LANES), F32)
            _, vjp = jax.vjp(_seq_fn, [w_ref[e, rows, _head_cols(h)].astype(F32) for e, h in probs],
                             [u_ref[e, rows, _head_cols(h)] for e, h in probs],
                             [qd_ref[e, rows, _head_cols(h)].astype(F32) for e, h in probs],
                             [kd_ref[e, rows, _head_cols(h)].astype(F32) for e, h in probs],
                             [qk_ref[e, n, h] for e, h in probs], [eglb[e][h:h + 1, :] for e, h in probs],
                             [st_ref[e, n, h] for e, h in probs])
            dw, du, dqd, dkd, dqk, degl, ds_in = vjp(([do_ref[e, rows, _head_cols(h)] for e, h in probs],
                                                      [ds_s[e, h] for e, h in probs]))
            for i, (e, h) in enumerate(probs):
                cols = _head_cols(h)
                dw_ref[e, rows, cols] = dw[i]
                du_ref[e, rows, cols] = du[i]
                dqd_ref[e, rows, cols] = dqd[i]
                dkd_ref[e, rows, cols] = dkd[i]
                dqk_ref[e, n, h] = dqk[i]
                degl_ref[e, n, h:h + 1, :] = degl[i]
                ds_s[e, h] = ds_in[i]
            return carry

        lax.fori_loop(0, cpb, step, 0)

    nc = seq // DN_CHUNK
    outs = pl.pallas_call(
        body, name=name, grid=(n_seq // pair, nb),
        in_specs=[tok, tok, tok, tok, sq, rows8, state, tok],
        out_specs=(tok, tok, tok, tok, sq, rows8),
        out_shape=(jax.ShapeDtypeStruct((n_seq, seq, DN_WIDTH), F32),) * 4
        + (jax.ShapeDtypeStruct((n_seq, nc, DN_HEADS, DN_CHUNK, DN_CHUNK), F32),
           jax.ShapeDtypeStruct((n_seq, nc, HALO, LANES), F32)),
        scratch_shapes=[pltpu.VMEM((pair, DN_HEADS, DN_HEAD_DIM, DN_HEAD_DIM), F32)],
        compiler_params=_cparams(("parallel", "arbitrary")),
    )(*[_by_seq(a, n_seq) for a in (w, u, qd, kd, qk, egl, states, do)])
    return tuple(_flat_seq(a) for a in outs)


def _dn_gate(o, z, dnw):
    return o * lax.rsqrt(jnp.mean(o * o, axis=-1, keepdims=True) + EPS) * dnw * _silu(z)


def _mix_out_fwd(x, sg, o, z, wo_sg, wo_dn, dnw, *, name):
    t = x.shape[0]
    tm = _tm(t)

    def body(x_ref, sg_ref, o_ref, z_ref, wsg_ref, wdn_ref, dnw_ref, y_ref, dn_s):
        for h, (oh, zh) in enumerate(zip(_split_heads(o_ref, 0), _split_heads(z_ref, 0))):
            dn_s[:, h * DN_HEAD_DIM:(h + 1) * DN_HEAD_DIM] = _dn_gate(oh, zh, dnw_ref[...]).astype(BF16)
        y_ref[...] = (x_ref[...] + jnp.dot(sg_ref[...].astype(BF16), wsg_ref[...], preferred_element_type=F32)
                      + jnp.dot(dn_s[...], wdn_ref[...], preferred_element_type=F32))

    row = lambda i: (i, 0)
    const = lambda i: (0, 0)
    half = pl.BlockSpec((tm, DN_WIDTH), row)
    return pl.pallas_call(
        body, name=name, grid=(t // tm,),
        in_specs=[pl.BlockSpec((tm, D_MODEL), row), half, half, half, pl.BlockSpec((SG_WIDTH, D_MODEL), const),
                  pl.BlockSpec((DN_WIDTH, D_MODEL), const), pl.BlockSpec((1, DN_HEAD_DIM), const)],
        out_specs=pl.BlockSpec((tm, D_MODEL), row),
        out_shape=jax.ShapeDtypeStruct((t, D_MODEL), F32),
        scratch_shapes=[pltpu.VMEM((tm, DN_WIDTH), BF16)],
        compiler_params=_cparams(("parallel",)),
    )(x, sg, o, z, wo_sg, wo_dn, dnw)


def _mix_out_bwd(dy, sg, o, z, wo_sg, wo_dn, dnw, *, name):
    t = dy.shape[0]
    tm = _tm(t)

    def body(dy_ref, sg_ref, o_ref, z_ref, wsg_ref, wdn_ref, dnw_ref, dsg_ref, do_ref, dz_ref, dwsg_ref, dwdn_ref, ddnw_ref, dn_s):
        i = pl.program_id(0)
        dyb = dy_ref[...].astype(BF16)
        nt = (((1,), (1,)), ((), ()))
        tn = (((0,), (0,)), ((), ()))
        dsg_ref[...] = lax.dot_general(dyb, wsg_ref[...], nt, preferred_element_type=F32)
        ddn = lax.dot_general(dyb, wdn_ref[...], nt, preferred_element_type=F32)
        ddnw = None
        for h, (oh, zh) in enumerate(zip(_split_heads(o_ref, 0), _split_heads(z_ref, 0))):
            cols = slice(h * DN_HEAD_DIM, (h + 1) * DN_HEAD_DIM)
            out, vjp = jax.vjp(_dn_gate, oh, zh, dnw_ref[...])
            dn_s[:, cols] = out.astype(BF16)
            doh, dzh, dw = vjp(ddn[:, cols])
            do_ref[:, cols] = doh
            dz_ref[:, cols] = dzh.astype(BF16)
            ddnw = dw if ddnw is None else ddnw + dw
        _acc_out(ddnw_ref, i == 0, ddnw)
        _acc_out(dwsg_ref, i == 0, lax.dot_general(sg_ref[...].astype(BF16), dyb, tn, preferred_element_type=F32))
        _acc_out(dwdn_ref, i == 0, lax.dot_general(dn_s[...], dyb, tn, preferred_element_type=F32))

    row = lambda i: (i, 0)
    const = lambda i: (0, 0)
    half = pl.BlockSpec((tm, DN_WIDTH), row)
    wspec = pl.BlockSpec((DN_WIDTH, D_MODEL), const)
    return pl.pallas_call(
        body, name=name, grid=(t // tm,),
        in_specs=[pl.BlockSpec((tm, D_MODEL), row), half, half, half, wspec, wspec, pl.BlockSpec((1, DN_HEAD_DIM), const)],
        out_specs=(half, half, half, wspec, wspec, pl.BlockSpec((1, DN_HEAD_DIM), const)),
        out_shape=(jax.ShapeDtypeStruct((t, DN_WIDTH), F32),) * 2 + (jax.ShapeDtypeStruct((t, DN_WIDTH), BF16),)
        + (jax.ShapeDtypeStruct((DN_WIDTH, D_MODEL), F32),) * 2 + (jax.ShapeDtypeStruct((1, DN_HEAD_DIM), F32),),
        scratch_shapes=[pltpu.VMEM((tm, DN_WIDTH), BF16)],
        compiler_params=_cparams(("arbitrary",)),
    )(dy, sg, o, z, wo_sg, wo_dn, dnw)


_MESH = pl.DeviceIdType.MESH
_HBM = pl.BlockSpec(memory_space=pl.ANY)


def _mesh_pos():
    x, y, c = lax.axis_index("x"), lax.axis_index("y"), lax.axis_index("c")
    return x, y, c, [(1 - x, y), (x, 1 - y), (1 - x, 1 - y)]


def _gather2(arrs, *, name):
    n = len(arrs)
    slots = N_DEV - 1

    def body(*refs):
        in_refs, out_refs = refs[:n], refs[n:2 * n]
        send_sems, recv_sems, local_sems = refs[2 * n:]
        x, y, c, chips = _mesh_pos()
        me, sibling = (x, y, c), (x, y, 1 - c)

        def copy(k, slot, block, to, src=None):
            dst = out_refs[k].at[4 * block[0] + 2 * block[1] + block[2]]
            return pltpu.make_async_remote_copy(src_ref=dst if src is None else src, dst_ref=dst,
                                                send_sem=send_sems.at[k * slots + slot], recv_sem=recv_sems.at[k * slots + slot],
                                                device_id=to, device_id_type=_MESH)

        local = [pltpu.make_async_copy(in_refs[k], out_refs[k].at[4 * x + 2 * y + c], local_sems.at[k]) for k in range(n)]
        sent = []
        for k in range(n):
            sent.append(copy(k, 0, me, sibling, src=in_refs[k]))
            sent += [copy(k, 1 + j, me, (*chip, c), src=in_refs[k]) for j, chip in enumerate(chips)]
        for cp in local + sent:
            cp.start()
        for j, chip in enumerate(chips):
            for k in range(n):
                copy(k, 1 + j, (*chip, c), me).wait_recv()
                passed = copy(k, 4 + j, (*chip, c), sibling)
                passed.start()
                sent.append(passed)
        for k in range(n):
            copy(k, 0, sibling, me).wait_recv()
            for j, chip in enumerate(chips):
                copy(k, 4 + j, (*chip, 1 - c), me).wait_recv()
        for cp in sent:
            cp.wait_send()
        for cp in local:
            cp.wait()

    return pl.pallas_call(
        body, name=name, in_specs=[_HBM] * n, out_specs=(_HBM,) * n,
        out_shape=tuple(jax.ShapeDtypeStruct((N_DEV,) + a.shape, a.dtype) for a in arrs),
        scratch_shapes=[pltpu.SemaphoreType.DMA((n * slots,)), pltpu.SemaphoreType.DMA((n * slots,)),
                        pltpu.SemaphoreType.DMA((n,))],
    )(*arrs)


_SEM = pl.BlockSpec(memory_space=pltpu.SEMAPHORE)
_EFFECT = pltpu.SideEffectType.DATAFLOW_SIDE_EFFECTING


def _direct_copies(src_refs, land_refs, send_sems, recv_sems, gather):
    x, y, c, _ = _mesh_pos()
    me = 4 * x + 2 * y + c
    n_peer = N_DEV - 1
    copies = []
    for r in range(1, N_DEV):
        px = 1 - x if r & 4 else x
        py = 1 - y if r & 2 else y
        pc = 1 - c if r & 1 else c
        for k, (src, land) in enumerate(zip(src_refs, land_refs)):
            copies.append(pltpu.make_async_remote_copy(
                src_ref=src if gather else src.at[4 * px + 2 * py + pc], dst_ref=land.at[me],
                send_sem=send_sems.at[k * n_peer + r - 1], recv_sem=recv_sems.at[k * n_peer + r - 1],
                device_id=(px, py, pc), device_id_type=_MESH))
    return copies


def _send_start(arrs, gather, after=None, *, name):
    n = len(arrs)
    lands = [lax.empty(((N_DEV,) + a.shape) if gather else a.shape, a.dtype) for a in arrs]
    n_in = 2 * n + (0 if after is None else 1)

    def body(*refs):
        src_refs, land_refs, send_sems, recv_sems, token = refs[:n], refs[n:2 * n], refs[n_in], refs[n_in + 1], refs[-1]
        for cp in _direct_copies(src_refs, land_refs, send_sems, recv_sems, gather):
            cp.start()
        token[...] = jnp.zeros_like(token)

    n_sem = n * (N_DEV - 1)
    bufs = list(arrs) + lands
    out = pl.pallas_call(
        body, name=name,
        out_shape=(pltpu.SemaphoreType.DMA((n_sem,)), pltpu.SemaphoreType.DMA((n_sem,)))
        + tuple(pltpu.HBM(b.shape, b.dtype) for b in bufs) + (jax.ShapeDtypeStruct((HALO, LANES), F32),),
        in_specs=[_HBM] * n_in, out_specs=(_SEM, _SEM) + (_HBM,) * (2 * n) + (pl.BlockSpec(memory_space=pltpu.VMEM),),
        input_output_aliases={i: 2 + i for i in range(2 * n)},
        compiler_params=pltpu.CompilerParams(has_side_effects=_EFFECT),
    )(*[pltpu.with_memory_space_constraint(b, pltpu.HBM) for b in bufs], *([] if after is None else [after]))
    return (out[0], out[1], list(out[2:2 + n]), list(out[2 + n:2 + 2 * n])), out[-1]


def _send_wait(started, gather, after, *, name):
    send_sems, recv_sems, srcs, lands = started
    n = len(srcs)

    def body(*refs):
        src_refs, land_refs, send_ref, recv_ref = refs[:n], refs[n:2 * n], refs[2 * n], refs[2 * n + 1]
        for cp in _direct_copies(src_refs, land_refs, send_ref, recv_ref, gather):
            cp.wait_send()
            cp.wait_recv()

    bufs = srcs + lands
    out = pl.pallas_call(
        body, name=name, out_shape=tuple(pltpu.HBM(b.shape, b.dtype) for b in bufs),
        in_specs=[_HBM] * (2 * n) + [_SEM, _SEM, _HBM], out_specs=(_HBM,) * (2 * n),
        input_output_aliases={i: i for i in range(2 * n)},
        compiler_params=pltpu.CompilerParams(has_side_effects=_EFFECT),
    )(*bufs, send_sems, recv_sems, after)
    return list(out[:n]), list(out[n:])


def _row_block(rows, limit=256):
    best = rows
    for cand in range(8, limit + 1, 8):
        if rows % cand == 0:
            best = cand
    return best if rows > limit else rows


def _adam(gp, w, m, v, *, name):
    p, rows, cols = gp.shape
    rb = _row_block(rows)

    def body(gp_ref, w_ref, m_ref, v_ref, g_ref, d_ref, m2_ref, v2_ref):
        g = gp_ref[0].astype(F32)
        for s in range(1, p):
            g = g + gp_ref[s].astype(F32)
        m2 = ADAM_B1 * m_ref[...] + (1.0 - ADAM_B1) * g
        v2 = ADAM_B2 * v_ref[...] + (1.0 - ADAM_B2) * (g * g)
        m_hat = m2 / (1.0 - ADAM_B1 ** ADAM_STEP)
        v_hat = v2 / (1.0 - ADAM_B2 ** ADAM_STEP)
        g_ref[...] = g
        d_ref[...] = -ADAM_LR * (m_hat / (jnp.sqrt(v_hat) + ADAM_EPS) + ADAM_WD * w_ref[...])
        m2_ref[...] = m2
        v2_ref[...] = v2

    blk = pl.BlockSpec((rb, cols), lambda i: (i, 0))
    return pl.pallas_call(
        body, name=name, grid=(rows // rb,),
        in_specs=[pl.BlockSpec((p, rb, cols), lambda i: (0, i, 0)), blk, blk, blk],
        out_specs=(blk,) * 4, out_shape=(jax.ShapeDtypeStruct((rows, cols), F32),) * 4,
        compiler_params=_cparams(("parallel",)),
    )(gp, w, m, v)


def _cols_full(g):
    return jnp.transpose(g, (1, 0, 2)).reshape(g.shape[1], N_DEV * g.shape[2])


def _pad_lanes(a, width=LANES):
    return jnp.pad(a, ((0, 0), (0, width - a.shape[1])))


def _chunk_rows_of(a):
    by_chunk = jnp.transpose(a[:, :DN_HEADS].reshape(-1, DN_CHUNK, DN_HEADS), (0, 2, 1))
    return jnp.pad(by_chunk, ((0, 0), (0, HALO - DN_HEADS), (0, 0)))


_SMALL = (("ffn1_norm", D_MODEL), ("mix_norm", D_MODEL), ("ffn2_norm", D_MODEL), ("final_norm", D_MODEL), ("a_log", DN_HEADS),
          ("dt_bias", DN_HEADS), ("dn_norm", DN_HEAD_DIM), ("sg_ln_g", SG_WIDTH), ("sg_ln_b", SG_WIDTH),
          ("sg_w", SG_GROUPS * SG_CHUNK * SG_CHUNK), ("sg_b", SG_GROUPS * SG_CHUNK), ("conv_w", CONV_K * 3 * DN_WIDTH))
_SMALL_ROWS = 1128
_SMALL_SHAPES = {"ffn1_norm": (1, D_MODEL), "mix_norm": (1, D_MODEL), "ffn2_norm": (1, D_MODEL), "final_norm": (D_MODEL,),
                 "a_log": (1, DN_HEADS), "dt_bias": (1, DN_HEADS), "dn_norm": (1, DN_HEAD_DIM), "sg_ln_g": (1, SG_WIDTH),
                 "sg_ln_b": (1, SG_WIDTH), "sg_w": (1, SG_GROUPS, SG_CHUNK, SG_CHUNK), "sg_b": (1, SG_GROUPS, SG_CHUNK)}


def _pack_small(d):
    flat = jnp.concatenate([d[name].reshape(-1) for name, _ in _SMALL])
    return jnp.pad(flat, (0, _SMALL_ROWS * LANES - flat.shape[0])).reshape(_SMALL_ROWS, LANES)


def _unpack_small(a):
    flat, out, at = a.reshape(-1), {}, 0
    for name, size in _SMALL:
        out[name] = flat[at:at + size]
        at += size
    return out


def kernel(x, ffn1_norm, ffn1_w_gate, ffn1_w_up, ffn1_w_down, mix_norm, w_in, conv_w, a_log, dt_bias, dn_norm, sg_ln_g, sg_ln_b, sg_w, sg_b, w_out, ffn2_norm, ffn2_w_gate, ffn2_w_up, ffn2_w_down, final_norm, loss_target, m_ffn1_norm, m_ffn1_w_gate, m_ffn1_w_up, m_ffn1_w_down, m_mix_norm, m_w_in, m_conv_w, m_a_log, m_dt_bias, m_dn_norm, m_sg_ln_g, m_sg_ln_b, m_sg_w, m_sg_b, m_w_out, m_ffn2_norm, m_ffn2_w_gate, m_ffn2_w_up, m_ffn2_w_down, m_final_norm, v_ffn1_norm, v_ffn1_w_gate, v_ffn1_w_up, v_ffn1_w_down, v_mix_norm, v_w_in, v_conv_w, v_a_log, v_dt_bias, v_dn_norm, v_sg_ln_g, v_sg_ln_b, v_sg_w, v_sg_b, v_w_out, v_ffn2_norm, v_ffn2_w_gate, v_ffn2_w_up, v_ffn2_w_down, v_final_norm):
    weights = dict(ffn1_norm=ffn1_norm, ffn1_w_gate=ffn1_w_gate, ffn1_w_up=ffn1_w_up, ffn1_w_down=ffn1_w_down, mix_norm=mix_norm, w_in=w_in, conv_w=conv_w, a_log=a_log, dt_bias=dt_bias, dn_norm=dn_norm, sg_ln_g=sg_ln_g, sg_ln_b=sg_ln_b, sg_w=sg_w, sg_b=sg_b, w_out=w_out, ffn2_norm=ffn2_norm, ffn2_w_gate=ffn2_w_gate, ffn2_w_up=ffn2_w_up, ffn2_w_down=ffn2_w_down, final_norm=final_norm)
    mom_m = dict(ffn1_norm=m_ffn1_norm, ffn1_w_gate=m_ffn1_w_gate, ffn1_w_up=m_ffn1_w_up, ffn1_w_down=m_ffn1_w_down, mix_norm=m_mix_norm, w_in=m_w_in, conv_w=m_conv_w, a_log=m_a_log, dt_bias=m_dt_bias, dn_norm=m_dn_norm, sg_ln_g=m_sg_ln_g, sg_ln_b=m_sg_ln_b, sg_w=m_sg_w, sg_b=m_sg_b, w_out=m_w_out, ffn2_norm=m_ffn2_norm, ffn2_w_gate=m_ffn2_w_gate, ffn2_w_up=m_ffn2_w_up, ffn2_w_down=m_ffn2_w_down, final_norm=m_final_norm)
    mom_v = dict(ffn1_norm=v_ffn1_norm, ffn1_w_gate=v_ffn1_w_gate, ffn1_w_up=v_ffn1_w_up, ffn1_w_down=v_ffn1_w_down, mix_norm=v_mix_norm, w_in=v_w_in, conv_w=v_conv_w, a_log=v_a_log, dt_bias=v_dt_bias, dn_norm=v_dn_norm, sg_ln_g=v_sg_ln_g, sg_ln_b=v_sg_ln_b, sg_w=v_sg_w, sg_b=v_sg_b, w_out=v_w_out, ffn2_norm=v_ffn2_norm, ffn2_w_gate=v_ffn2_w_gate, ffn2_w_up=v_ffn2_w_up, ffn2_w_down=v_ffn2_w_down, final_norm=v_final_norm)
    order = list(weights)
    big = ("ffn1_w_gate", "ffn1_w_up", "ffn1_w_down", "w_in", "w_out", "ffn2_w_gate", "ffn2_w_up", "ffn2_w_down")
    col_sharded = ("ffn1_w_gate", "ffn1_w_up", "w_in", "ffn2_w_gate", "ffn2_w_up")

    n_seq, seq, _ = x.shape
    t = n_seq * seq
    me = 4 * lax.axis_index("x") + 2 * lax.axis_index("y") + lax.axis_index("c")
    x0 = x.reshape(t, D_MODEL)
    tgt = loss_target.reshape(t, D_MODEL)

    def fill_own(land, own_block):
        return lax.dynamic_update_index_in_dim(land, own_block, me, 0)

    def rows_view(n, a):
        return jnp.transpose(a) if n in col_sharded else a

    def as_full(n, g):
        return g.reshape(-1, g.shape[-1])

    shards = {n: rows_view(n, weights[n][0]).astype(BF16) for n in big}
    ffn1_names, mix_names, ffn2_names = big[:3], big[3:5], big[5:]
    full = {n: as_full(n, g) for n, g in zip(ffn1_names, _gather2([shards[n] for n in ffn1_names], name="gather_ffn1"))}
    mix_srcs = [shards[n] for n in mix_names] + [conv_w[0]]
    mix_started, mix_token = _send_start(mix_srcs, True, full[ffn1_names[2]], name="gather_mix_start")
    ffn2_started, ffn2_token = _send_start([shards[n] for n in ffn2_names], True, mix_token, name="gather_ffn2_start")
    ffn1_norm_fwd = ffn1_norm + ffn2_token[:1, :1]
    alog, dtb = _pad_lanes(a_log), _pad_lanes(dt_bias)
    sgbt = _pad_lanes(sg_b[0].T)
    fnw = final_norm.reshape(1, D_MODEL)

    x1, h1, g1, u1 = _ffn_fwd(x0, ffn1_norm_fwd, full["ffn1_w_gate"], full["ffn1_w_up"], full["ffn1_w_down"], name="ffn1_fwd")
    mix_lands = [fill_own(land, src) for src, land in zip(*_send_wait(mix_started, True, x1, name="gather_mix_wait"))]
    full.update({n: as_full(n, g) for n, g in zip(mix_names, mix_lands)})
    conv_full = _cols_full(mix_lands[-1])
    w_in_t = full["w_in"]
    offs = (0, SG_WIDTH, 2 * SG_WIDTH, 2 * SG_WIDTH + 3 * DN_WIDTH, 2 * SG_WIDTH + 4 * DN_WIDTH)
    n_proj = offs[-1]

    def pad_rows(a):
        return jnp.pad(a, ((0, LANES - a.shape[0]), (0, 0)))

    ws = [w_in_t[offs[0]:offs[1]], w_in_t[offs[1]:offs[2]], w_in_t[offs[2]:offs[3]], w_in_t[offs[3]:offs[4]],
          pad_rows(w_in_t[n_proj:n_proj + DN_HEADS]), pad_rows(w_in_t[n_proj + DN_HEADS:n_proj + 2 * DN_HEADS])]
    wo_sg, wo_dn = full["w_out"][:SG_WIDTH], full["w_out"][SG_WIDTH:]
    u, v, qkv, z, bpre, apre = _mix_in_fwd(x1, mix_norm, ws, name="mix_in_fwd")
    sg_out = _sg_fwd(u, v, sg_ln_g, sg_ln_b, sg_w[0], sgbt, name="sg_fwd")
    q, k, vv, beta, gc = _dn_prep_fwd(qkv, bpre, apre, conv_full, alog, dtb, seq, name="dn_prep_fwd")
    grow = _chunk_rows_of(gc)
    wy_w, wy_u, q_dec, k_dec, qk, egl, inv = _delta_prep(q, k, vv, gc, grow, beta, name="delta_prep")
    o, states = _delta_seq_fwd(wy_w, wy_u, q_dec, k_dec, qk, egl, n_seq, seq, name="delta_seq_fwd")
    x2 = _mix_out_fwd(x1, sg_out, o, z, wo_sg, wo_dn, dn_norm, name="mix_out_fwd")
    ffn2_srcs, ffn2_lands = _send_wait(ffn2_started, True, x2, name="gather_ffn2_wait")
    full.update({n: as_full(n, fill_own(land, src)) for n, src, land in zip(ffn2_names, ffn2_srcs, ffn2_lands)})
    dx3, loss_part, d_fn, h2, g2, u2 = _ffn_fwd(x2, ffn2_norm, full["ffn2_w_gate"], full["ffn2_w_up"], full["ffn2_w_down"],
                                                tgt, fnw, name="ffn2_fwd_loss")
    loss = lax.psum(loss_part[0, 0], ("x", "y", "c"))

    dx2, d_n2, d_g2, d_u2, d_d2 = _ffn_bwd(x2, ffn2_norm, h2, g2, u2, full["ffn2_w_gate"], full["ffn2_w_up"],
                                           full["ffn2_w_down"], dx3, name="ffn2_bwd")
    def by_owner(d_rows):
        return d_rows.reshape(N_DEV, -1, D_MODEL)

    ffn2_pieces = [by_owner(d_g2), by_owner(d_u2), by_owner(d_d2)]
    ffn2_sent, sent_token = _send_start(ffn2_pieces, False, name="grads_ffn2_start")
    dsg, do, dz, d_wo_sg, d_wo_dn, d_dnw = _mix_out_bwd(dx2, sg_out, o, z, wo_sg, wo_dn, dn_norm + sent_token[:1, :1],
                                                        name="mix_out_bwd")
    d_seq = _delta_seq_bwd(wy_w, wy_u, q_dec, k_dec, qk, egl, states, do, n_seq, seq, name="delta_seq_bwd")
    dq, dk, dv, dgc_a, dgrow, dbeta = _delta_par_bwd(q, k, vv, gc, grow, beta, inv, *d_seq, name="delta_par_bwd")
    dgc_b = _pad_lanes(jnp.transpose(dgrow[:, :DN_HEADS, :], (0, 2, 1)).reshape(t, DN_HEADS))
    dqkv, d_conv, dbpre, dapre, d_alog, d_dtb = _dn_prep_bwd(qkv, bpre, apre, conv_full, alog, dtb, dq, dk, dv, dbeta, dgc_a,
                                                             dgc_b, seq, name="dn_prep_bwd")
    du, dvv, d_lng, d_lnb, d_wc, d_sgbt = _sg_bwd(u, v, sg_ln_g, sg_ln_b, sg_w[0], sgbt, dsg, name="sg_bwd")
    dx1, d_mixn, d_wp = _mix_in_bwd(x1, mix_norm, ws, dx2, (du, dvv, dqkv, dz, dbpre, dapre), name="mix_in_bwd")
    d_w_in_t = jnp.concatenate([d_wp[:n_proj], d_wp[_PROJ_OFFSETS[4]:_PROJ_OFFSETS[4] + DN_HEADS],
                                d_wp[_PROJ_OFFSETS[5]:_PROJ_OFFSETS[5] + DN_HEADS]], axis=0)
    d_w_out = jnp.concatenate([d_wo_sg, d_wo_dn], axis=0)
    mix_pieces = [by_owner(d_w_in_t), by_owner(d_w_out).astype(BF16)]
    mix_sent, sent_token = _send_start(mix_pieces, False, name="grads_mix_start")
    grad_x, d_n1, dg1, du1, a1, dyh1 = _ffn_bwd_x(x0, ffn1_norm + sent_token[:1, :1], g1, u1, full["ffn1_w_gate"],
                                                  full["ffn1_w_up"], full["ffn1_w_down"], dx1, name="ffn1_bwd_x")
    small_grads = dict(ffn1_norm=d_n1, mix_norm=d_mixn, ffn2_norm=d_n2, final_norm=d_fn, a_log=d_alog[:, :DN_HEADS],
                       dt_bias=d_dtb[:, :DN_HEADS], dn_norm=d_dnw, sg_ln_g=d_lng, sg_ln_b=d_lnb, sg_w=d_wc,
                       sg_b=d_sgbt[:, :SG_GROUPS].T, conv_w=d_conv[:CONV_K])
    small_src = _pack_small(small_grads)
    small_sent, small_token = _send_start([small_src], True, name="small_grads_start")
    late, tokens = [], []

    def send_early(k, grad):
        piece = by_owner(grad)
        sent, token = _send_start([piece], False, name="grads_" + ffn1_names[k] + "_start")
        late.append(((ffn1_names[k],), sent))
        tokens.append(token)
        return token

    _ffn_wgrads(h1, dg1, du1, a1, dyh1, send_early, small_token, name="ffn1_bwd")

    res = {}
    after = tokens[-1]

    def update(names, sent, after):
        pieces, lands = _send_wait(sent, False, after, name="grads_" + names[0] + "_wait")
        for n, land, p in zip(names, lands, pieces):
            got = fill_own(land, lax.dynamic_index_in_dim(p, me, 0, keepdims=False))
            upd = _adam(got, *[rows_view(n, src[n][0]) for src in (weights, mom_m, mom_v)], name="adam_" + n)
            res[n] = [rows_view(n, a) for a in upd]
            after = upd[0]
        return after

    for group in [(ffn2_names, ffn2_sent), (mix_names, mix_sent)] + late[:-1]:
        after = update(*group, after)
    (small_src,), (small_land,) = _send_wait(small_sent, True, after, name="small_grads_wait")
    small_parts = fill_own(small_land, small_src)
    zeros_conv = jnp.zeros((CONV_K * 3 * DN_WIDTH,), F32)
    packed = [_pack_small({**{n: src[n] for n, _ in _SMALL if n != "conv_w"}, "conv_w": zeros_conv})
              for src in (weights, mom_m, mom_v)]
    small_upd = _adam(small_parts, *packed, name="adam_small")
    small_res = [_unpack_small(a) for a in small_upd]
    conv_grad = lax.dynamic_slice_in_dim(small_res[0]["conv_w"].reshape(CONV_K, 3 * DN_WIDTH), me * (3 * DN_WIDTH // N_DEV),
                                         3 * DN_WIDTH // N_DEV, axis=1)
    res["conv_w"] = _adam(conv_grad[None], conv_w[0], m_conv_w[0], v_conv_w[0], name="adam_conv_w")
    update(*late[-1], res["conv_w"][0])

    outs = [[], [], [], []]
    for n in order:
        for kind in range(4):
            if n in res:
                outs[kind].append(res[n][kind][None])
            else:
                outs[kind].append(small_res[kind][n].reshape(_SMALL_SHAPES[n]))
    return (loss, grad_x.reshape(x.shape), *outs[0], *outs[1], *outs[2], *outs[3])
```

```python
import jax
import jax.numpy as jnp
from jax import lax
from jax.experimental import pallas as pl
from jax.experimental.pallas import tpu as pltpu

F32 = jnp.float32
BF16 = jnp.bfloat16

D_MODEL = 1024
D_FF = 2816
SG_WIDTH = 512
SG_GROUPS = 8
SG_GROUP_DIM = 64
SG_CHUNK = 128
DN_WIDTH = 512
DN_HEAD_DIM = 128
DN_HEADS = 4
DN_CHUNK = 64
CONV_K = 4
EPS = 1e-6
N_DEV = 8
LANES = 128
HALO = 8
MXU_COLS = 256

ADAM_LR = 0.001
ADAM_B1 = 0.9
ADAM_B2 = 0.999
ADAM_EPS = 1e-08
ADAM_WD = 0.01
ADAM_STEP = 10

VMEM_LIMIT = 60 * 1024 * 1024
WGRAD_K_TILE = 2048
TOKEN_BLOCK = 512
FF_BLOCK_FWD = 1408

_HI = lax.Precision.HIGHEST


def _cparams(sem):
    return pltpu.CompilerParams(dimension_semantics=sem, vmem_limit_bytes=VMEM_LIMIT)


def _tm(t, pref=TOKEN_BLOCK):
    return min(pref, t)


def _dg(a, b, ca, cb, precision):
    if precision is not None:
        return lax.dot_general(a, b, (((ca,), (cb,)), ((), ())), precision=precision, preferred_element_type=F32)
    return lax.dot_general(a.astype(BF16), b.astype(BF16), (((ca,), (cb,)), ((), ())), preferred_element_type=F32)


def _make_mm(precision):
    @jax.custom_vjp
    def mm(a, b):
        return _dg(a, b, 1, 0, precision)

    @jax.custom_vjp
    def mm_nt(a, b):
        return _dg(a, b, 1, 1, precision)

    @jax.custom_vjp
    def mm_tn(a, b):
        return _dg(a, b, 0, 0, precision)

    mm.defvjp(lambda a, b: (mm(a, b), (a, b)), lambda r, g: (mm_nt(g, r[1]), mm_tn(r[0], g)))
    mm_nt.defvjp(lambda a, b: (mm_nt(a, b), (a, b)), lambda r, g: (mm(g, r[1]), mm_tn(g, r[0])))
    mm_tn.defvjp(lambda a, b: (mm_tn(a, b), (a, b)), lambda r, g: (mm_nt(r[1], g), mm(r[0], g)))
    return mm, mm_nt, mm_tn


mm, mm_nt, mm_tn = _make_mm(None)
mmx, mmx_nt, mmx_tn = _make_mm(_HI)
mmh, mmh_nt, mmh_tn = _make_mm(lax.Precision.HIGH)


def _sigmoid(x):
    return 1.0 / (1.0 + jnp.exp(-x))


def _silu(x):
    return x * _sigmoid(x)


def _softplus(x):
    neg_abs = jnp.where(x > 0, -x, x)
    return jnp.where(x > 0, x, 0.0) + jnp.log(1.0 + jnp.exp(neg_abs))


def _gelu(x):
    return 0.5 * x * (1.0 + jnp.tanh(0.7978845608028654 * (x + 0.044715 * (x * x * x))))


def _rms_fwd(x, g):
    r = lax.rsqrt(jnp.mean(x * x, axis=-1, keepdims=True) + EPS)
    xh = x * r
    return xh * g, xh, r


def _rms_bwd(dh, xh, r, g):
    dxh = dh * g
    dx = r * (dxh - xh * jnp.mean(dxh * xh, axis=-1, keepdims=True))
    return dx, jnp.sum(dh * xh, axis=0, keepdims=True)


def _acc_out(ref, first, val):
    @pl.when(first)
    def _():
        ref[...] = val

    @pl.when(jnp.logical_not(first))
    def _():
        ref[...] += val


def _ffn_fwd(x, nw, wg, wu, wd, tgt=None, fnw=None, *, name):
    t = x.shape[0]
    tm, fb = _tm(t), FF_BLOCK_FWD
    n_t, n_f = t // tm, D_FF // fb
    with_loss = tgt is not None

    def body(*refs):
        if with_loss:
            (x_ref, nw_ref, wg_ref, wu_ref, wd_ref, tgt_ref, fnw_ref, dy_ref, loss_ref, dfn_ref, h_ref, g_ref, u_ref,
             acc_s) = refs
        else:
            x_ref, nw_ref, wg_ref, wu_ref, wd_ref, y_ref, h_ref, g_ref, u_ref, acc_s = refs
        i, j = pl.program_id(0), pl.program_id(1)

        @pl.when(j == 0)
        def _():
            h, _, _ = _rms_fwd(x_ref[...], nw_ref[...])
            h_ref[...] = h.astype(BF16)
            acc_s[...] = jnp.zeros_like(acc_s)

        h = h_ref[...]
        nt = (((1,), (1,)), ((), ()))
        g = lax.dot_general(h, wg_ref[...], nt, preferred_element_type=F32)
        u = lax.dot_general(h, wu_ref[...], nt, preferred_element_type=F32)
        g_ref[...] = g.astype(BF16)
        u_ref[...] = u.astype(BF16)
        a = _silu(g) * u
        acc_s[...] += jnp.dot(a.astype(BF16), wd_ref[...], preferred_element_type=F32)

        @pl.when(j == n_f - 1)
        def _():
            y = x_ref[...] + 0.5 * acc_s[...]
            if not with_loss:
                y_ref[...] = y
            else:
                gf = fnw_ref[...]
                out, xh, r = _rms_fwd(y, gf)
                err = out - tgt_ref[...]
                part = 0.5 * jnp.sum(jnp.mean(err * err, axis=-1, keepdims=True), axis=0, keepdims=True)
                d_out = err * (1.0 / D_MODEL)
                dy, dgf = _rms_bwd(d_out, xh, r, gf)
                dy_ref[...] = dy
                _acc_out(loss_ref, i == 0, jnp.broadcast_to(part, loss_ref.shape))
                _acc_out(dfn_ref, i == 0, dgf)

    row = lambda i, j: (i, 0)
    const = lambda i, j: (0, 0)
    in_specs = [
        pl.BlockSpec((tm, D_MODEL), row),
        pl.BlockSpec((1, D_MODEL), const),
        pl.BlockSpec((fb, D_MODEL), lambda i, j: (j, 0)),
        pl.BlockSpec((fb, D_MODEL), lambda i, j: (j, 0)),
        pl.BlockSpec((fb, D_MODEL), lambda i, j: (j, 0)),
    ]
    args = [x, nw, wg, wu, wd]
    saved_shape = (jax.ShapeDtypeStruct((t, D_MODEL), BF16), jax.ShapeDtypeStruct((t, D_FF), BF16),
                   jax.ShapeDtypeStruct((t, D_FF), BF16))
    saved_specs = (pl.BlockSpec((tm, D_MODEL), row), pl.BlockSpec((tm, fb), lambda i, j: (i, j)),
                   pl.BlockSpec((tm, fb), lambda i, j: (i, j)))
    if with_loss:
        in_specs += [pl.BlockSpec((tm, D_MODEL), row), pl.BlockSpec((1, D_MODEL), const)]
        args += [tgt, fnw]
        out_shape = (jax.ShapeDtypeStruct((t, D_MODEL), F32), jax.ShapeDtypeStruct((8, LANES), F32),
                     jax.ShapeDtypeStruct((1, D_MODEL), F32)) + saved_shape
        out_specs = (pl.BlockSpec((tm, D_MODEL), row), pl.BlockSpec((8, LANES), const),
                     pl.BlockSpec((1, D_MODEL), const)) + saved_specs
        sem = ("arbitrary", "arbitrary")
    else:
        out_shape = (jax.ShapeDtypeStruct((t, D_MODEL), F32),) + saved_shape
        out_specs = (pl.BlockSpec((tm, D_MODEL), row),) + saved_specs
        sem = ("parallel", "arbitrary")
    return pl.pallas_call(
        body, name=name, grid=(n_t, n_f), in_specs=in_specs, out_specs=out_specs, out_shape=out_shape,
        scratch_shapes=[pltpu.VMEM((tm, D_MODEL), F32)],
        compiler_params=_cparams(sem),
    )(*args)


def _ffn_bwd_x(x, nw, g, u, wg, wu, wd, dy, *, name):
    t = x.shape[0]
    tm = _tm(t, 256)

    def body(x_ref, nw_ref, g_ref, u_ref, wg_ref, wu_ref, wd_ref, dy_ref, dx_ref, dnw_ref, dg_ref, du_ref, a_ref, dyh_ref):
        i = pl.program_id(0)
        nt = (((1,), (1,)), ((), ()))
        dy = dy_ref[...]
        dyh = (0.5 * dy).astype(BF16)
        dyh_ref[...] = dyh
        gate, up = g_ref[...].astype(F32), u_ref[...].astype(F32)
        s = _sigmoid(gate)
        gs = gate * s
        da = lax.dot_general(dyh, wd_ref[...], nt, preferred_element_type=F32)
        dg = (da * up * (s + gs * (1.0 - s))).astype(BF16)
        du = (da * gs).astype(BF16)
        dg_ref[...] = dg
        du_ref[...] = du
        a_ref[...] = (gs * up).astype(BF16)
        dh = (jnp.dot(dg, wg_ref[...], preferred_element_type=F32)
              + jnp.dot(du, wu_ref[...], preferred_element_type=F32))
        xv = x_ref[...]
        r = lax.rsqrt(jnp.mean(xv * xv, axis=-1, keepdims=True) + EPS)
        dx, dnw = _rms_bwd(dh, xv * r, r, nw_ref[...])
        dx_ref[...] = dy + dx
        _acc_out(dnw_ref, i == 0, dnw)

    row = lambda i: (i, 0)
    const = lambda i: (0, 0)
    once = pl.Buffered(1)
    wide = pl.BlockSpec((tm, D_FF), row)
    return pl.pallas_call(
        body, name=name, grid=(t // tm,),
        in_specs=[pl.BlockSpec((tm, D_MODEL), row), pl.BlockSpec((1, D_MODEL), const), wide, wide,
                  pl.BlockSpec((D_FF, D_MODEL), const, pipeline_mode=once), pl.BlockSpec((D_FF, D_MODEL), const, pipeline_mode=once),
                  pl.BlockSpec((D_FF, D_MODEL), const, pipeline_mode=once), pl.BlockSpec((tm, D_MODEL), row)],
        out_specs=(pl.BlockSpec((tm, D_MODEL), row), pl.BlockSpec((1, D_MODEL), const), wide, wide, wide,
                   pl.BlockSpec((tm, D_MODEL), row)),
        out_shape=(jax.ShapeDtypeStruct((t, D_MODEL), F32), jax.ShapeDtypeStruct((1, D_MODEL), F32),
                   jax.ShapeDtypeStruct((t, D_FF), BF16), jax.ShapeDtypeStruct((t, D_FF), BF16),
                   jax.ShapeDtypeStruct((t, D_FF), BF16), jax.ShapeDtypeStruct((t, D_MODEL), BF16)),
        compiler_params=_cparams(("arbitrary",)),
    )(x, nw, g, u, wg, wu, wd, dy)


def _wgrad(a, b, bm, bn, after=None, *, name):
    k, m = a.shape
    n = b.shape[1]
    tk = _tm(k, WGRAD_K_TILE)
    n_k = k // tk

    def body(a_ref, b_ref, *rest):
        o_ref, acc_s = rest[-2], rest[-1]
        s = pl.program_id(2)
        for c in range(bn // MXU_COLS):
            cols = slice(c * MXU_COLS, (c + 1) * MXU_COLS)
            part = lax.dot_general(a_ref[...], b_ref[:, cols], (((0,), (0,)), ((), ())), preferred_element_type=F32)
            acc_s[:, cols] = jnp.where(s == 0, 0.0, acc_s[:, cols]) + part

        @pl.when(s == n_k - 1)
        def _():
            o_ref[...] = acc_s[...].astype(BF16)

    return pl.pallas_call(
        body, name=name, grid=(m // bm, n // bn, n_k),
        in_specs=[pl.BlockSpec((tk, bm), lambda i, j, s: (s, i)), pl.BlockSpec((tk, bn), lambda i, j, s: (s, j))]
        + ([] if after is None else [_HBM]),
        out_specs=pl.BlockSpec((bm, bn), lambda i, j, s: (i, j)),
        out_shape=jax.ShapeDtypeStruct((m, n), BF16),
        scratch_shapes=[pltpu.VMEM((bm, bn), F32)],
        compiler_params=_cparams(("parallel", "parallel", "arbitrary")),
    )(a, b, *([] if after is None else [after]))


def _ffn_wgrads(h, dg, du, a, dyh, between=None, after=None, *, name):
    grads = []
    for k, (lhs, rhs, tag) in enumerate(((dg, h, "_wg"), (du, h, "_wu"), (a, dyh, "_wd"))):
        grads.append(_wgrad(lhs, rhs, D_FF // 2, D_MODEL, after, name=name + tag))
        after = None if between is None else between(k, grads[-1])
    return grads


def _ffn_bwd(x, nw, h, g, u, wg, wu, wd, dy, *, name):
    dx, dnw, dg, du, a, dyh = _ffn_bwd_x(x, nw, g, u, wg, wu, wd, dy, name=name + "_x")
    return (dx, dnw, *_ffn_wgrads(h, dg, du, a, dyh, name=name))


_PROJ_WIDTHS = (SG_WIDTH, SG_WIDTH, 3 * DN_WIDTH, DN_WIDTH, LANES, LANES)


def _mix_in_fwd(x, nw, ws, *, name):
    t = x.shape[0]
    tm = _tm(t)

    def body(x_ref, nw_ref, *refs):
        w_refs, o_refs = refs[:6], refs[6:]
        h, _, _ = _rms_fwd(x_ref[...], nw_ref[...])
        h = h.astype(BF16)
        for w_ref, o_ref in zip(w_refs, o_refs):
            o_ref[...] = lax.dot_general(h, w_ref[...], (((1,), (1,)), ((), ())), preferred_element_type=F32)

    row = lambda i: (i, 0)
    const = lambda i: (0, 0)
    return pl.pallas_call(
        body, name=name, grid=(t // tm,),
        in_specs=[pl.BlockSpec((tm, D_MODEL), row), pl.BlockSpec((1, D_MODEL), const)]
        + [pl.BlockSpec((n, D_MODEL), const) for n in _PROJ_WIDTHS],
        out_specs=tuple(pl.BlockSpec((tm, n), row) for n in _PROJ_WIDTHS),
        out_shape=tuple(jax.ShapeDtypeStruct((t, n), F32) for n in _PROJ_WIDTHS),
        compiler_params=_cparams(("parallel",)),
    )(x, nw, *ws)


_PROJ_TOTAL = sum(_PROJ_WIDTHS)
_PROJ_OFFSETS = tuple(sum(_PROJ_WIDTHS[:k]) for k in range(len(_PROJ_WIDTHS)))


def _mix_in_bwd(x, nw, ws, dres, dps, *, name):
    t = x.shape[0]
    tm = _tm(t)

    def body(x_ref, nw_ref, dres_ref, *refs):
        w_refs, dp_refs, dx_ref, dnw_ref, h_ref, dpb_ref = refs[:6], refs[6:12], refs[12], refs[13], refs[14], refs[15]
        i = pl.program_id(0)
        hf, xh, r = _rms_fwd(x_ref[...], nw_ref[...])
        h_ref[...] = hf.astype(BF16)
        dh = jnp.zeros((tm, D_MODEL), F32)
        for w_ref, dp_ref, off, width in zip(w_refs, dp_refs, _PROJ_OFFSETS, _PROJ_WIDTHS):
            dp = dp_ref[...].astype(BF16)
            dpb_ref[:, off:off + width] = dp
            dh = dh + jnp.dot(dp, w_ref[...], preferred_element_type=F32)
        dx, dnw = _rms_bwd(dh, xh, r, nw_ref[...])
        dx_ref[...] = dres_ref[...] + dx
        _acc_out(dnw_ref, i == 0, dnw)

    row = lambda i: (i, 0)
    const = lambda i: (0, 0)
    dx, dnw, h, dpb = pl.pallas_call(
        body, name=name + "_x", grid=(t // tm,),
        in_specs=[pl.BlockSpec((tm, D_MODEL), row), pl.BlockSpec((1, D_MODEL), const), pl.BlockSpec((tm, D_MODEL), row)]
        + [pl.BlockSpec((n, D_MODEL), const) for n in _PROJ_WIDTHS]
        + [pl.BlockSpec((tm, n), row) for n in _PROJ_WIDTHS],
        out_specs=(pl.BlockSpec((tm, D_MODEL), row), pl.BlockSpec((1, D_MODEL), const), pl.BlockSpec((tm, D_MODEL), row),
                   pl.BlockSpec((tm, _PROJ_TOTAL), row)),
        out_shape=(jax.ShapeDtypeStruct((t, D_MODEL), F32), jax.ShapeDtypeStruct((1, D_MODEL), F32),
                   jax.ShapeDtypeStruct((t, D_MODEL), BF16), jax.ShapeDtypeStruct((t, _PROJ_TOTAL), BF16)),
        compiler_params=_cparams(("arbitrary",)),
    )(x, nw, dres, *ws, *dps)
    return dx, dnw, _wgrad(dpb, h, _PROJ_TOTAL // 2, D_MODEL, name=name + "_w")


_SG_TILES = SG_WIDTH // LANES


def _lane_tiles(ref, rows=slice(None)):
    return [ref[rows, p * LANES:(p + 1) * LANES] for p in range(_SG_TILES)]


def _sg_fn(u, v, lng, lnb, wcs, sgbt):
    lane = lax.broadcasted_iota(jnp.int32, (1, LANES), 1)
    rr = lax.broadcasted_iota(jnp.int32, (SG_CHUNK, SG_CHUNK), 0)
    cc = lax.broadcasted_iota(jnp.int32, (SG_CHUNK, SG_CHUNK), 1)
    per_tile = LANES // SG_GROUP_DIM
    gu, gv = [_gelu(a) for a in u], [_gelu(a) for a in v]
    mu = sum(jnp.sum(a, axis=-1, keepdims=True) for a in gv) * (1.0 / SG_WIDTH)
    cen = [a - mu for a in gv]
    var = sum(jnp.sum(a * a, axis=-1, keepdims=True) for a in cen) * (1.0 / SG_WIDTH)
    rstd = lax.rsqrt(var + EPS)
    ln = [a * rstd * g + b for a, g, b in zip(cen, lng, lnb)]
    out = []
    for p in range(_SG_TILES):
        vs = None
        for e in range(per_tile):
            g = p * per_tile + e
            in_group = jnp.logical_and(lane >= e * SG_GROUP_DIM, lane < (e + 1) * SG_GROUP_DIM)
            w_causal = jnp.where(rr >= cc, wcs[g], 0.0)
            bias = jnp.sum(jnp.where(lane == g, sgbt, 0.0), axis=1, keepdims=True)
            term = jnp.where(in_group, mm(w_causal, ln[p]) + bias, 0.0)
            vs = term if vs is None else vs + term
        out.append(gu[p] * vs)
    return out


def _sg_fwd(u, v, lng, lnb, wc, sgbt, *, name):
    t = u.shape[0]
    tm = _tm(t)

    def body(u_ref, v_ref, lng_ref, lnb_ref, wc_ref, sgbt_ref, o_ref):
        wcs = [wc_ref[g] for g in range(SG_GROUPS)]
        for c in range(tm // SG_CHUNK):
            rows = pl.ds(c * SG_CHUNK, SG_CHUNK)
            out = _sg_fn(_lane_tiles(u_ref, rows), _lane_tiles(v_ref, rows), _lane_tiles(lng_ref), _lane_tiles(lnb_ref),
                         wcs, sgbt_ref[...])
            for p in range(_SG_TILES):
                o_ref[rows, p * LANES:(p + 1) * LANES] = out[p]

    row = lambda i: (i, 0)
    const = lambda i: (0, 0)
    return pl.pallas_call(
        body, name=name, grid=(t // tm,),
        in_specs=[pl.BlockSpec((tm, SG_WIDTH), row), pl.BlockSpec((tm, SG_WIDTH), row),
                  pl.BlockSpec((1, SG_WIDTH), const), pl.BlockSpec((1, SG_WIDTH), const),
                  pl.BlockSpec((SG_GROUPS, SG_CHUNK, SG_CHUNK), lambda i: (0, 0, 0)), pl.BlockSpec((SG_CHUNK, LANES), const)],
        out_specs=pl.BlockSpec((tm, SG_WIDTH), row),
        out_shape=jax.ShapeDtypeStruct((t, SG_WIDTH), F32),
        compiler_params=_cparams(("parallel",)),
    )(u, v, lng, lnb, wc, sgbt)


def _sg_bwd(u, v, lng, lnb, wc, sgbt, dout, *, name):
    t = u.shape[0]
    tm = _tm(t)

    def body(u_ref, v_ref, lng_ref, lnb_ref, wc_ref, sgbt_ref, do_ref, du_ref, dv_ref, dlng_ref, dlnb_ref, dwc_ref, dsgbt_ref):
        i = pl.program_id(0)
        wcs = [wc_ref[g] for g in range(SG_GROUPS)]
        tot = None
        for c in range(tm // SG_CHUNK):
            rows = pl.ds(c * SG_CHUNK, SG_CHUNK)
            _, vjp = jax.vjp(_sg_fn, _lane_tiles(u_ref, rows), _lane_tiles(v_ref, rows), _lane_tiles(lng_ref),
                             _lane_tiles(lnb_ref), wcs, sgbt_ref[...])
            du, dv, dlng, dlnb, dwcs, dsgbt = vjp(_lane_tiles(do_ref, rows))
            for p in range(_SG_TILES):
                du_ref[rows, p * LANES:(p + 1) * LANES] = du[p].astype(BF16)
                dv_ref[rows, p * LANES:(p + 1) * LANES] = dv[p].astype(BF16)
            part = (dlng, dlnb, dwcs, dsgbt)
            tot = part if tot is None else jax.tree.map(jnp.add, tot, part)
        dlng, dlnb, dwcs, dsgbt = tot
        _acc_out(dlng_ref, i == 0, jnp.concatenate(dlng, axis=1))
        _acc_out(dlnb_ref, i == 0, jnp.concatenate(dlnb, axis=1))
        _acc_out(dsgbt_ref, i == 0, dsgbt)
        for g in range(SG_GROUPS):
            @pl.when(i == 0)
            def _(g=g):
                dwc_ref[g] = dwcs[g]

            @pl.when(i > 0)
            def _(g=g):
                dwc_ref[g] += dwcs[g]

    row = lambda i: (i, 0)
    const = lambda i: (0, 0)
    wspec = pl.BlockSpec((SG_GROUPS, SG_CHUNK, SG_CHUNK), lambda i: (0, 0, 0))
    return pl.pallas_call(
        body, name=name, grid=(t // tm,),
        in_specs=[pl.BlockSpec((tm, SG_WIDTH), row), pl.BlockSpec((tm, SG_WIDTH), row),
                  pl.BlockSpec((1, SG_WIDTH), const), pl.BlockSpec((1, SG_WIDTH), const), wspec,
                  pl.BlockSpec((SG_CHUNK, LANES), const), pl.BlockSpec((tm, SG_WIDTH), row)],
        out_specs=(pl.BlockSpec((tm, SG_WIDTH), row), pl.BlockSpec((tm, SG_WIDTH), row),
                   pl.BlockSpec((1, SG_WIDTH), const), pl.BlockSpec((1, SG_WIDTH), const), wspec,
                   pl.BlockSpec((SG_CHUNK, LANES), const)),
        out_shape=(jax.ShapeDtypeStruct((t, SG_WIDTH), BF16), jax.ShapeDtypeStruct((t, SG_WIDTH), BF16),
                   jax.ShapeDtypeStruct((1, SG_WIDTH), F32), jax.ShapeDtypeStruct((1, SG_WIDTH), F32),
                   jax.ShapeDtypeStruct((SG_GROUPS, SG_CHUNK, SG_CHUNK), F32), jax.ShapeDtypeStruct((SG_CHUNK, LANES), F32)),
        compiler_params=_cparams(("arbitrary",)),
    )(u, v, lng, lnb, wc, sgbt, dout)


def _conv_taps(ext, w, tm):
    y = None
    for j in range(CONV_K):
        s = CONV_K - 1 - j
        shifted = ext if s == 0 else pltpu.roll(ext, s, 0)
        term = w[j:j + 1, :] * shifted[HALO:HALO + tm, :]
        y = term if y is None else y + term
    return y


def _post_conv(yq, yk, yv, bpre, apre, alog, dtb):
    def l2(a):
        return a * lax.rsqrt(jnp.sum(a * a, axis=-1, keepdims=True) + EPS)

    q = [l2(_silu(a)) for a in yq]
    k = [l2(_silu(a)) for a in yk]
    return q, k, _silu(yv), _sigmoid(bpre), -jnp.exp(alog) * _softplus(apre + dtb)


def _chunk_tril(tm):
    rr = lax.broadcasted_iota(jnp.int32, (tm, tm), 0)
    cc = lax.broadcasted_iota(jnp.int32, (tm, tm), 1)
    shift = DN_CHUNK.bit_length() - 1
    same = jnp.right_shift(rr, shift) == jnp.right_shift(cc, shift)
    return jnp.where(jnp.logical_and(same, rr >= cc), 1.0, 0.0).astype(F32)


def _halo_specs(tm, width, n_blocks_seq, n_blocks):
    per = tm // HALO
    prev = pl.BlockSpec((HALO, width), lambda i: (jnp.maximum(i * per - 1, 0), 0))
    nxt = pl.BlockSpec((HALO, width), lambda i: (jnp.minimum((i + 1) * per, n_blocks * per - 1), 0))
    return prev, nxt


def _split_heads(ref, base):
    return [ref[:, base + h * DN_HEAD_DIM: base + (h + 1) * DN_HEAD_DIM] for h in range(DN_HEADS)]


def _dn_prep_fwd(qkv, bpre, apre, conv_w, alog, dtb, seq, *, name):
    t = qkv.shape[0]
    tm = _tm(t)
    bps = seq // tm
    cw = 3 * DN_WIDTH

    def body(x_ref, halo_ref, b_ref, a_ref, w_ref, alog_ref, dtb_ref, q_ref, k_ref, v_ref, beta_ref, gc_ref):
        i = pl.program_id(0)
        keep = jnp.where(i % bps == 0, 0.0, 1.0)
        ext = jnp.concatenate([halo_ref[...] * keep, x_ref[...]], axis=0)
        y = _conv_taps(ext, w_ref[...], tm)
        yq = [y[:, h * DN_HEAD_DIM:(h + 1) * DN_HEAD_DIM] for h in range(DN_HEADS)]
        yk = [y[:, DN_WIDTH + h * DN_HEAD_DIM: DN_WIDTH + (h + 1) * DN_HEAD_DIM] for h in range(DN_HEADS)]
        q, k, v, beta, g = _post_conv(yq, yk, y[:, 2 * DN_WIDTH:], b_ref[...], a_ref[...], alog_ref[...], dtb_ref[...])
        for h in range(DN_HEADS):
            q_ref[:, h * DN_HEAD_DIM:(h + 1) * DN_HEAD_DIM] = q[h]
            k_ref[:, h * DN_HEAD_DIM:(h + 1) * DN_HEAD_DIM] = k[h]
        v_ref[...] = v
        beta_ref[...] = beta
        gc_ref[...] = mmx(_chunk_tril(tm), g)

    row = lambda i: (i, 0)
    const = lambda i: (0, 0)
    prev, _ = _halo_specs(tm, cw, bps, t // tm)
    return pl.pallas_call(
        body, name=name, grid=(t // tm,),
        in_specs=[pl.BlockSpec((tm, cw), row), prev, pl.BlockSpec((tm, LANES), row), pl.BlockSpec((tm, LANES), row),
                  pl.BlockSpec((CONV_K, cw), const), pl.BlockSpec((1, LANES), const), pl.BlockSpec((1, LANES), const)],
        out_specs=tuple(pl.BlockSpec((tm, n), row) for n in (DN_WIDTH, DN_WIDTH, DN_WIDTH, LANES, LANES)),
        out_shape=tuple(jax.ShapeDtypeStruct((t, n), F32) for n in (DN_WIDTH, DN_WIDTH, DN_WIDTH, LANES, LANES)),
        compiler_params=_cparams(("parallel",)),
    )(qkv, qkv, bpre, apre, conv_w, alog, dtb)


def _y_heads(y):
    yq = [y[:, h * DN_HEAD_DIM:(h + 1) * DN_HEAD_DIM] for h in range(DN_HEADS)]
    yk = [y[:, DN_WIDTH + h * DN_HEAD_DIM: DN_WIDTH + (h + 1) * DN_HEAD_DIM] for h in range(DN_HEADS)]
    return yq, yk, y[:, 2 * DN_WIDTH:]


def _dn_prep_bwd(qkv, bpre, apre, conv_w, alog, dtb, dq, dk, dv, dbeta, dgc, dgc2, seq, *, name):
    t = qkv.shape[0]
    tm = _tm(t)
    bps = seq // tm
    cw = 3 * DN_WIDTH
    n_ext = tm + HALO

    def body(x_ref, halo_ref, xn_ref, b_ref, a_ref, w_ref, alog_ref, dtb_ref, dq_ref, dk_ref, dv_ref, dqn_ref, dkn_ref,
             dvn_ref, dbeta_ref, dgc_ref, dgc2_ref, dx_ref, dw_ref, db_ref, da_ref, dalog_ref, ddtb_ref):
        i = pl.program_id(0)
        keep_prev = jnp.where(i % bps == 0, 0.0, 1.0)
        keep_next = jnp.where(i % bps == bps - 1, 0.0, 1.0)
        w = w_ref[...]
        x = x_ref[...]
        ext = jnp.concatenate([halo_ref[...] * keep_prev, x], axis=0)
        yq, yk, yv = _y_heads(_conv_taps(ext, w, tm))
        _, vjp = jax.vjp(_post_conv, yq, yk, yv, b_ref[...], a_ref[...], alog_ref[...], dtb_ref[...])
        dg = mmx_tn(_chunk_tril(tm), dgc_ref[...] + dgc2_ref[...])
        dyq, dyk, dyv, db, da, dalog, ddtb = vjp((_split_heads(dq_ref, 0), _split_heads(dk_ref, 0), dv_ref[...],
                                                  dbeta_ref[...], dg))
        dy = jnp.concatenate(dyq + dyk + [dyv], axis=1)
        ext_n = jnp.concatenate([x[tm - HALO:, :], xn_ref[...]], axis=0)
        _, vjp_n = jax.vjp(lambda *ys: _post_conv(*ys, b_ref[:HALO, :], a_ref[:HALO, :], alog_ref[...], dtb_ref[...])[:3],
                           *_y_heads(_conv_taps(ext_n, w, HALO)))
        dyq_n, dyk_n, dyv_n = vjp_n((_split_heads(dqn_ref, 0), _split_heads(dkn_ref, 0), dvn_ref[...]))
        dyext = jnp.concatenate([dy, jnp.concatenate(dyq_n + dyk_n + [dyv_n], axis=1) * keep_next], axis=0)

        @pl.when(i == 0)
        def _():
            dw_ref[...] = jnp.zeros_like(dw_ref)

        dx = None
        for j in range(CONV_K):
            s = CONV_K - 1 - j
            fut = dyext if s == 0 else pltpu.roll(dyext, n_ext - s, 0)
            term = w[j:j + 1, :] * fut[0:tm, :]
            dx = term if dx is None else dx + term
            past = ext if s == 0 else pltpu.roll(ext, s, 0)
            dw_ref[j:j + 1, :] += jnp.sum(dy * past[HALO:HALO + tm, :], axis=0, keepdims=True)
        dx_ref[...] = dx.astype(BF16)
        db_ref[...] = db.astype(BF16)
        da_ref[...] = da.astype(BF16)
        _acc_out(dalog_ref, i == 0, dalog)
        _acc_out(ddtb_ref, i == 0, ddtb)

    row = lambda i: (i, 0)
    const = lambda i: (0, 0)
    prev, nxt = _halo_specs(tm, cw, bps, t // tm)
    _, nxt_h = _halo_specs(tm, DN_WIDTH, bps, t // tm)
    tok = pl.BlockSpec((tm, DN_WIDTH), row)
    lanes = pl.BlockSpec((tm, LANES), row)
    return pl.pallas_call(
        body, name=name, grid=(t // tm,),
        in_specs=[pl.BlockSpec((tm, cw), row), prev, nxt, lanes, lanes,
                  pl.BlockSpec((CONV_K, cw), const), pl.BlockSpec((1, LANES), const), pl.BlockSpec((1, LANES), const),
                  tok, tok, tok, nxt_h, nxt_h, nxt_h, lanes, lanes, lanes],
        out_specs=(pl.BlockSpec((tm, cw), row), pl.BlockSpec((HALO, cw), const), lanes, lanes,
                   pl.BlockSpec((1, LANES), const), pl.BlockSpec((1, LANES), const)),
        out_shape=(jax.ShapeDtypeStruct((t, cw), BF16), jax.ShapeDtypeStruct((HALO, cw), F32),
                   jax.ShapeDtypeStruct((t, LANES), BF16), jax.ShapeDtypeStruct((t, LANES), BF16),
                   jax.ShapeDtypeStruct((1, LANES), F32), jax.ShapeDtypeStruct((1, LANES), F32)),
        compiler_params=_cparams(("arbitrary",)),
    )(qkv, qkv, qkv, bpre, apre, conv_w, alog, dtb, dq, dk, dv, dq, dk, dv, dbeta, dgc, dgc2)


def _inv_unit_lower(l_mats, eye):
    invs = [eye - l for l in l_mats]
    powers = list(l_mats)
    n = 2
    while n < eye.shape[0]:
        f = mmh if n == 2 else mm
        powers = [f(p, p) for p in powers]
        invs = [inv + f(inv, p) for inv, p in zip(invs, powers)]
        n *= 2
    return invs


@jax.custom_vjp
def _solve(l_mat, rhs, inv):
    return mmh(inv, rhs)


def _solve_fwd(l_mat, rhs, inv):
    sol = mmh(inv, rhs)
    return sol, (inv, sol)


def _solve_bwd(res, d_sol):
    inv, sol = res
    d_rhs = mm_tn(inv, d_sol)
    return -mm_nt(d_rhs, sol), d_rhs, jnp.zeros_like(inv)


_solve.defvjp(_solve_fwd, _solve_bwd)


def _prep_fn(q, k, v, gc, gr, b, inv):
    ids = range(len(q))
    c = q[0].shape[0]
    rr = lax.broadcasted_iota(jnp.int32, (c, c), 0)
    cc = lax.broadcasted_iota(jnp.int32, (c, c), 1)
    incl, strict = rr >= cc, rr > cc
    is_last = lax.broadcasted_iota(jnp.int32, (c, 1), 0) == c - 1
    qs = [q[i] * (DN_HEAD_DIM ** -0.5) for i in ids]
    decay = [jnp.where(incl, jnp.exp(jnp.where(incl, gc[i] - gr[i], 0.0)), 0.0) for i in ids]
    kb = [k[i] * b[i] for i in ids]
    vb = [v[i] * b[i] for i in ids]
    kk = [mm_nt(kb[i], k[i]) for i in ids]
    l_mat = [jnp.where(strict, kk[i] * decay[i], 0.0) for i in ids]
    eg = [jnp.exp(gc[i]) for i in ids]
    if inv is None:
        inv = _inv_unit_lower(l_mat, jnp.where(rr == cc, 1.0, 0.0).astype(F32))
    u_wy = [_solve(l_mat[i], vb[i], inv[i]) for i in ids]
    w_wy = [_solve(l_mat[i], kb[i] * eg[i], inv[i]) for i in ids]
    qk = [mm_nt(qs[i], k[i]) * decay[i] for i in ids]
    g_last = [jnp.sum(jnp.where(is_last, gc[i], 0.0), axis=0, keepdims=True) for i in ids]
    k_dec = [k[i] * jnp.exp(g_last[i] - gc[i]) for i in ids]
    egl = [jnp.broadcast_to(jnp.exp(g_last[i]), (1, LANES)) for i in ids]
    return [(w_wy[i], u_wy[i], qs[i] * eg[i], k_dec[i], qk[i], egl[i]) for i in ids], inv


def _seq_fn(w, u, qd, kd, qk, egl, s):
    ids = range(len(w))
    ws = [mm(w[i], s[i]) for i in ids]
    qs = [mm(qd[i], s[i]) for i in ids]
    v_new = [u[i] - ws[i] for i in ids]
    o = [qs[i] + mm(qk[i], v_new[i]) for i in ids]
    s_new = [s[i] * egl[i] + mm_tn(kd[i], v_new[i]) for i in ids]
    return o, s_new


def _lane_col(a, h):
    lane = lax.broadcasted_iota(jnp.int32, (1, LANES), 1)
    return jnp.sum(jnp.where(lane == h, a, 0.0), axis=1, keepdims=True)


def _col_lane(col, h):
    lane = lax.broadcasted_iota(jnp.int32, (1, LANES), 1)
    return jnp.where(lane == h, col, 0.0)


def _head_cols(h):
    return slice(h * DN_HEAD_DIM, (h + 1) * DN_HEAD_DIM)


def _chunk_rows(n):
    return pl.ds(pl.multiple_of(n * DN_CHUNK, DN_CHUNK), DN_CHUNK)


def _delta_prep(q, k, v, gc, grow, beta, *, name):
    t = q.shape[0]
    tm = _tm(t)
    cpb = tm // DN_CHUNK
    n_chunks = t // DN_CHUNK
    group = 2

    def body(q_ref, k_ref, v_ref, gc_ref, gr_ref, b_ref, w_ref, u_ref, qd_ref, kd_ref, qk_ref, egl_ref, inv_ref):
        def step(m, carry):
            probs = [(m * group + e, h) for e in range(group) for h in range(DN_HEADS)]
            gcb = [gc_ref[_chunk_rows(m * group + e), :] for e in range(group)]
            bb = [b_ref[_chunk_rows(m * group + e), :] for e in range(group)]
            grb = [gr_ref[m * group + e] for e in range(group)]
            for e in range(group):
                egl_ref[m * group + e] = jnp.zeros((HALO, LANES), F32)
            outs, invs = _prep_fn(
                [q_ref[_chunk_rows(n), _head_cols(h)] for n, h in probs], [k_ref[_chunk_rows(n), _head_cols(h)] for n, h in probs],
                [v_ref[_chunk_rows(n), _head_cols(h)] for n, h in probs],
                [_lane_col(gcb[e], h) for e in range(group) for h in range(DN_HEADS)],
                [grb[e][h:h + 1, :] for e in range(group) for h in range(DN_HEADS)],
                [_lane_col(bb[e], h) for e in range(group) for h in range(DN_HEADS)], None)
            for (n, h), (w, u, qd, kd, qk, egl), inv in zip(probs, outs, invs):
                rows, cols = _chunk_rows(n), _head_cols(h)
                w_ref[rows, cols] = w.astype(BF16)
                u_ref[rows, cols] = u
                qd_ref[rows, cols] = qd.astype(BF16)
                kd_ref[rows, cols] = kd.astype(BF16)
                qk_ref[n, h] = qk
                inv_ref[n, h] = inv
                egl_ref[n, h:h + 1, :] = egl
            return carry

        lax.fori_loop(0, cpb // group, step, 0)

    row = lambda i: (i, 0)
    tok = pl.BlockSpec((tm, DN_WIDTH), row)
    lanes = pl.BlockSpec((tm, LANES), row)
    sq = pl.BlockSpec((cpb, DN_HEADS, DN_CHUNK, DN_CHUNK), lambda i: (i, 0, 0, 0))
    return pl.pallas_call(
        body, name=name, grid=(t // tm,),
        in_specs=[tok, tok, tok, lanes, pl.BlockSpec((cpb, HALO, DN_CHUNK), lambda i: (i, 0, 0)), lanes],
        out_specs=(tok, tok, tok, tok, sq, pl.BlockSpec((cpb, HALO, LANES), lambda i: (i, 0, 0)), sq),
        out_shape=(jax.ShapeDtypeStruct((t, DN_WIDTH), BF16), jax.ShapeDtypeStruct((t, DN_WIDTH), F32),
                   jax.ShapeDtypeStruct((t, DN_WIDTH), BF16), jax.ShapeDtypeStruct((t, DN_WIDTH), BF16),
                   jax.ShapeDtypeStruct((n_chunks, DN_HEADS, DN_CHUNK, DN_CHUNK), F32),
                   jax.ShapeDtypeStruct((n_chunks, HALO, LANES), F32),
                   jax.ShapeDtypeStruct((n_chunks, DN_HEADS, DN_CHUNK, DN_CHUNK), F32)),
        compiler_params=_cparams(("parallel",)),
    )(q, k, v, gc, grow, beta)


def _delta_par_bwd(q, k, v, gc, grow, beta, inv, dw, du, dqd, dkd, dqk, degl, *, name):
    t = q.shape[0]
    tm = _tm(t)
    cpb = tm // DN_CHUNK
    n_chunks = t // DN_CHUNK
    group = 2

    def body(q_ref, k_ref, v_ref, gc_ref, gr_ref, b_ref, inv_ref, dw_ref, du_ref, dqd_ref, dkd_ref, dqk_ref, degl_ref,
             dq_ref, dk_ref, dv_ref, dgc_ref, dgr_ref, db_ref):
        def step(m, carry):
            chunks = [m * group + e for e in range(group)]
            probs = [(e, h) for e in range(group) for h in range(DN_HEADS)]
            rows = [_chunk_rows(n) for n in chunks]
            gcb, bb = [gc_ref[r, :] for r in rows], [b_ref[r, :] for r in rows]
            grb, deglb = [gr_ref[n] for n in chunks], [degl_ref[n] for n in chunks]
            for n in chunks:
                dgr_ref[n] = jnp.zeros((HALO, DN_CHUNK), F32)
            invs = [inv_ref[chunks[e], h] for e, h in probs]
            _, vjp = jax.vjp(lambda *a: _prep_fn(*a, invs)[0],
                             [q_ref[rows[e], _head_cols(h)] for e, h in probs], [k_ref[rows[e], _head_cols(h)] for e, h in probs],
                             [v_ref[rows[e], _head_cols(h)] for e, h in probs], [_lane_col(gcb[e], h) for e, h in probs],
                             [grb[e][h:h + 1, :] for e, h in probs], [_lane_col(bb[e], h) for e, h in probs])
            dq, dk, dv, dgc, dgr, db = vjp([(dw_ref[rows[e], _head_cols(h)], du_ref[rows[e], _head_cols(h)],
                                             dqd_ref[rows[e], _head_cols(h)], dkd_ref[rows[e], _head_cols(h)],
                                             dqk_ref[chunks[e], h], deglb[e][h:h + 1, :]) for e, h in probs])
            dgc_acc = [jnp.zeros((DN_CHUNK, LANES), F32) for _ in chunks]
            db_acc = [jnp.zeros((DN_CHUNK, LANES), F32) for _ in chunks]
            for i, (e, h) in enumerate(probs):
                cols = _head_cols(h)
                dq_ref[rows[e], cols] = dq[i]
                dk_ref[rows[e], cols] = dk[i]
                dv_ref[rows[e], cols] = dv[i]
                dgr_ref[chunks[e], h:h + 1, :] = dgr[i]
                dgc_acc[e] = dgc_acc[e] + _col_lane(dgc[i], h)
                db_acc[e] = db_acc[e] + _col_lane(db[i], h)
            for e in range(group):
                dgc_ref[rows[e], :] = dgc_acc[e]
                db_ref[rows[e], :] = db_acc[e]
            return carry

        lax.fori_loop(0, cpb // group, step, 0)

    row = lambda i: (i, 0)
    tok = pl.BlockSpec((tm, DN_WIDTH), row)
    lanes = pl.BlockSpec((tm, LANES), row)
    sq = pl.BlockSpec((cpb, DN_HEADS, DN_CHUNK, DN_CHUNK), lambda i: (i, 0, 0, 0))
    grs = pl.BlockSpec((cpb, HALO, DN_CHUNK), lambda i: (i, 0, 0))
    return pl.pallas_call(
        body, name=name, grid=(t // tm,),
        in_specs=[tok, tok, tok, lanes, grs, lanes, sq, tok, tok, tok, tok, sq, pl.BlockSpec((cpb, HALO, LANES), lambda i: (i, 0, 0))],
        out_specs=(tok, tok, tok, lanes, grs, lanes),
        out_shape=(jax.ShapeDtypeStruct((t, DN_WIDTH), F32),) * 3
        + (jax.ShapeDtypeStruct((t, LANES), F32), jax.ShapeDtypeStruct((n_chunks, HALO, DN_CHUNK), F32),
           jax.ShapeDtypeStruct((t, LANES), F32)),
        compiler_params=_cparams(("parallel",)),
    )(q, k, v, gc, grow, beta, inv, dw, du, dqd, dkd, dqk, degl)


def _seq_specs(n_seq, seq, reverse):
    tm = _tm(seq)
    nb = seq // tm
    cpb = tm // DN_CHUNK
    pair = 2 if n_seq % 2 == 0 else 1
    blk = (lambda j: nb - 1 - j) if reverse else (lambda j: j)
    tok = pl.BlockSpec((pair, tm, DN_WIDTH), lambda b, j: (b, blk(j), 0))
    sq = pl.BlockSpec((pair, cpb, DN_HEADS, DN_CHUNK, DN_CHUNK), lambda b, j: (b, blk(j), 0, 0, 0))
    rows8 = pl.BlockSpec((pair, cpb, HALO, LANES), lambda b, j: (b, blk(j), 0, 0))
    state = pl.BlockSpec((pair, cpb, DN_HEADS, DN_HEAD_DIM, DN_HEAD_DIM), lambda b, j: (b, blk(j), 0, 0, 0))
    return nb, cpb, pair, tok, sq, rows8, state


def _by_seq(a, n_seq):
    return a.reshape((n_seq, a.shape[0] // n_seq) + a.shape[1:])


def _flat_seq(a):
    return a.reshape((a.shape[0] * a.shape[1],) + a.shape[2:])


def _delta_seq_fwd(w, u, qd, kd, qk, egl, n_seq, seq, *, name):
    nb, cpb, pair, tok, sq, rows8, state = _seq_specs(n_seq, seq, False)
    probs = [(e, h) for e in range(pair) for h in range(DN_HEADS)]

    def body(w_ref, u_ref, qd_ref, kd_ref, qk_ref, egl_ref, o_ref, st_ref, s_s):
        @pl.when(pl.program_id(1) == 0)
        def _():
            s_s[...] = jnp.zeros_like(s_s)

        def step(n, carry):
            rows = _chunk_rows(n)
            eglb = [egl_ref[e, n] for e in range(pair)]
            s = [s_s[e, h] for e, h in probs]
            for (e, h), s_eh in zip(probs, s):
                st_ref[e, n, h] = s_eh
            o, s_new = _seq_fn([w_ref[e, rows, _head_cols(h)] for e, h in probs], [u_ref[e, rows, _head_cols(h)] for e, h in probs],
                               [qd_ref[e, rows, _head_cols(h)] for e, h in probs], [kd_ref[e, rows, _head_cols(h)] for e, h in probs],
                               [qk_ref[e, n, h] for e, h in probs], [eglb[e][h:h + 1, :] for e, h in probs], s)
            for i, (e, h) in enumerate(probs):
                o_ref[e, rows, _head_cols(h)] = o[i]
                s_s[e, h] = s_new[i]
            return carry

        lax.fori_loop(0, cpb, step, 0)

    o, states = pl.pallas_call(
        body, name=name, grid=(n_seq // pair, nb),
        in_specs=[tok, tok, tok, tok, sq, rows8],
        out_specs=(tok, state),
        out_shape=(jax.ShapeDtypeStruct((n_seq, seq, DN_WIDTH), F32),
                   jax.ShapeDtypeStruct((n_seq, seq // DN_CHUNK, DN_HEADS, DN_HEAD_DIM, DN_HEAD_DIM), F32)),
        scratch_shapes=[pltpu.VMEM((pair, DN_HEADS, DN_HEAD_DIM, DN_HEAD_DIM), F32)],
        compiler_params=_cparams(("parallel", "arbitrary")),
    )(*[_by_seq(a, n_seq) for a in (w, u, qd, kd, qk, egl)])
    return _flat_seq(o), _flat_seq(states)


def _delta_seq_bwd(w, u, qd, kd, qk, egl, states, do, n_seq, seq, *, name):
    nb, cpb, pair, tok, sq, rows8, state = _seq_specs(n_seq, seq, True)
    probs = [(e, h) for e in range(pair) for h in range(DN_HEADS)]

    def body(w_ref, u_ref, qd_ref, kd_ref, qk_ref, egl_ref, st_ref, do_ref, dw_ref, du_ref, dqd_ref, dkd_ref, dqk_ref,
             degl_ref, ds_s):
        @pl.when(pl.program_id(1) == 0)
        def _():
            ds_s[...] = jnp.zeros_like(ds_s)

        def step(m, carry):
            n = cpb - 1 - m
            rows = _chunk_rows(n)
            eglb = [egl_ref[e, n] for e in range(pair)]
            for e in range(pair):
                degl_ref[e, n] = jnp.zeros((HALO, LANES), F32)
            _, vjp = jax.vjp(_seq_fn, [w_ref[e, rows, _head_cols(h)].astype(F32) for e, h in probs],
                             [u_ref[e, rows, _head_cols(h)] for e, h in probs],
                             [qd_ref[e, rows, _head_cols(h)].astype(F32) for e, h in probs],
                             [kd_ref[e, rows, _head_cols(h)].astype(F32) for e, h in probs],
                             [qk_ref[e, n, h] for e, h in probs], [eglb[e][h:h + 1, :] for e, h in probs],
                             [st_ref[e, n, h] for e, h in probs])
            dw, du, dqd, dkd, dqk, degl, ds_in = vjp(([do_ref[e, rows, _head_cols(h)] for e, h in probs],
                                                      [ds_s[e, h] for e, h in probs]))
            for i, (e, h) in enumerate(probs):
                cols = _head_cols(h)
                dw_ref[e, rows, cols] = dw[i]
                du_ref[e, rows, cols] = du[i]
                dqd_ref[e, rows, cols] = dqd[i]
                dkd_ref[e, rows, cols] = dkd[i]
                dqk_ref[e, n, h] = dqk[i]
                degl_ref[e, n, h:h + 1, :] = degl[i]
                ds_s[e, h] = ds_in[i]
            return carry

        lax.fori_loop(0, cpb, step, 0)

    nc = seq // DN_CHUNK
    outs = pl.pallas_call(
        body, name=name, grid=(n_seq // pair, nb),
        in_specs=[tok, tok, tok, tok, sq, rows8, state, tok],
        out_specs=(tok, tok, tok, tok, sq, rows8),
        out_shape=(jax.ShapeDtypeStruct((n_seq, seq, DN_WIDTH), F32),) * 4
        + (jax.ShapeDtypeStruct((n_seq, nc, DN_HEADS, DN_CHUNK, DN_CHUNK), F32),
           jax.ShapeDtypeStruct((n_seq, nc, HALO, LANES), F32)),
        scratch_shapes=[pltpu.VMEM((pair, DN_HEADS, DN_HEAD_DIM, DN_HEAD_DIM), F32)],
        compiler_params=_cparams(("parallel", "arbitrary")),
    )(*[_by_seq(a, n_seq) for a in (w, u, qd, kd, qk, egl, states, do)])
    return tuple(_flat_seq(a) for a in outs)


def _dn_gate(o, z, dnw):
    return o * lax.rsqrt(jnp.mean(o * o, axis=-1, keepdims=True) + EPS) * dnw * _silu(z)


def _mix_out_fwd(x, sg, o, z, wo_sg, wo_dn, dnw, *, name):
    t = x.shape[0]
    tm = _tm(t)

    def body(x_ref, sg_ref, o_ref, z_ref, wsg_ref, wdn_ref, dnw_ref, y_ref, dn_s):
        for h, (oh, zh) in enumerate(zip(_split_heads(o_ref, 0), _split_heads(z_ref, 0))):
            dn_s[:, h * DN_HEAD_DIM:(h + 1) * DN_HEAD_DIM] = _dn_gate(oh, zh, dnw_ref[...]).astype(BF16)
        y_ref[...] = (x_ref[...] + jnp.dot(sg_ref[...].astype(BF16), wsg_ref[...], preferred_element_type=F32)
                      + jnp.dot(dn_s[...], wdn_ref[...], preferred_element_type=F32))

    row = lambda i: (i, 0)
    const = lambda i: (0, 0)
    half = pl.BlockSpec((tm, DN_WIDTH), row)
    return pl.pallas_call(
        body, name=name, grid=(t // tm,),
        in_specs=[pl.BlockSpec((tm, D_MODEL), row), half, half, half, pl.BlockSpec((SG_WIDTH, D_MODEL), const),
                  pl.BlockSpec((DN_WIDTH, D_MODEL), const), pl.BlockSpec((1, DN_HEAD_DIM), const)],
        out_specs=pl.BlockSpec((tm, D_MODEL), row),
        out_shape=jax.ShapeDtypeStruct((t, D_MODEL), F32),
        scratch_shapes=[pltpu.VMEM((tm, DN_WIDTH), BF16)],
        compiler_params=_cparams(("parallel",)),
    )(x, sg, o, z, wo_sg, wo_dn, dnw)


def _mix_out_bwd(dy, sg, o, z, wo_sg, wo_dn, dnw, *, name):
    t = dy.shape[0]
    tm = _tm(t)

    def body(dy_ref, sg_ref, o_ref, z_ref, wsg_ref, wdn_ref, dnw_ref, dsg_ref, do_ref, dz_ref, dwsg_ref, dwdn_ref, ddnw_ref, dn_s):
        i = pl.program_id(0)
        dyb = dy_ref[...].astype(BF16)
        nt = (((1,), (1,)), ((), ()))
        tn = (((0,), (0,)), ((), ()))
        dsg_ref[...] = lax.dot_general(dyb, wsg_ref[...], nt, preferred_element_type=F32)
        ddn = lax.dot_general(dyb, wdn_ref[...], nt, preferred_element_type=F32)
        ddnw = None
        for h, (oh, zh) in enumerate(zip(_split_heads(o_ref, 0), _split_heads(z_ref, 0))):
            cols = slice(h * DN_HEAD_DIM, (h + 1) * DN_HEAD_DIM)
            out, vjp = jax.vjp(_dn_gate, oh, zh, dnw_ref[...])
            dn_s[:, cols] = out.astype(BF16)
            doh, dzh, dw = vjp(ddn[:, cols])
            do_ref[:, cols] = doh
            dz_ref[:, cols] = dzh.astype(BF16)
            ddnw = dw if ddnw is None else ddnw + dw
        _acc_out(ddnw_ref, i == 0, ddnw)
        _acc_out(dwsg_ref, i == 0, lax.dot_general(sg_ref[...].astype(BF16), dyb, tn, preferred_element_type=F32))
        _acc_out(dwdn_ref, i == 0, lax.dot_general(dn_s[...], dyb, tn, preferred_element_type=F32))

    row = lambda i: (i, 0)
    const = lambda i: (0, 0)
    half = pl.BlockSpec((tm, DN_WIDTH), row)
    wspec = pl.BlockSpec((DN_WIDTH, D_MODEL), const)
    return pl.pallas_call(
        body, name=name, grid=(t // tm,),
        in_specs=[pl.BlockSpec((tm, D_MODEL), row), half, half, half, wspec, wspec, pl.BlockSpec((1, DN_HEAD_DIM), const)],
        out_specs=(half, half, half, wspec, wspec, pl.BlockSpec((1, DN_HEAD_DIM), const)),
        out_shape=(jax.ShapeDtypeStruct((t, DN_WIDTH), F32),) * 2 + (jax.ShapeDtypeStruct((t, DN_WIDTH), BF16),)
        + (jax.ShapeDtypeStruct((DN_WIDTH, D_MODEL), F32),) * 2 + (jax.ShapeDtypeStruct((1, DN_HEAD_DIM), F32),),
        scratch_shapes=[pltpu.VMEM((tm, DN_WIDTH), BF16)],
        compiler_params=_cparams(("arbitrary",)),
    )(dy, sg, o, z, wo_sg, wo_dn, dnw)


_MESH = pl.DeviceIdType.MESH
_HBM = pl.BlockSpec(memory_space=pl.ANY)


def _mesh_pos():
    x, y, c = lax.axis_index("x"), lax.axis_index("y"), lax.axis_index("c")
    return x, y, c, [(1 - x, y), (x, 1 - y), (1 - x, 1 - y)]


def _gather2(arrs, *, name):
    n = len(arrs)
    slots = N_DEV - 1

    def body(*refs):
        in_refs, out_refs = refs[:n], refs[n:2 * n]
        send_sems, recv_sems, local_sems = refs[2 * n:]
        x, y, c, chips = _mesh_pos()
        me, sibling = (x, y, c), (x, y, 1 - c)

        def copy(k, slot, block, to, src=None):
            dst = out_refs[k].at[4 * block[0] + 2 * block[1] + block[2]]
            return pltpu.make_async_remote_copy(src_ref=dst if src is None else src, dst_ref=dst,
                                                send_sem=send_sems.at[k * slots + slot], recv_sem=recv_sems.at[k * slots + slot],
                                                device_id=to, device_id_type=_MESH)

        local = [pltpu.make_async_copy(in_refs[k], out_refs[k].at[4 * x + 2 * y + c], local_sems.at[k]) for k in range(n)]
        sent = []
        for k in range(n):
            sent.append(copy(k, 0, me, sibling, src=in_refs[k]))
            sent += [copy(k, 1 + j, me, (*chip, c), src=in_refs[k]) for j, chip in enumerate(chips)]
        for cp in local + sent:
            cp.start()
        for j, chip in enumerate(chips):
            for k in range(n):
                copy(k, 1 + j, (*chip, c), me).wait_recv()
                passed = copy(k, 4 + j, (*chip, c), sibling)
                passed.start()
                sent.append(passed)
        for k in range(n):
            copy(k, 0, sibling, me).wait_recv()
            for j, chip in enumerate(chips):
                copy(k, 4 + j, (*chip, 1 - c), me).wait_recv()
        for cp in sent:
            cp.wait_send()
        for cp in local:
            cp.wait()

    return pl.pallas_call(
        body, name=name, in_specs=[_HBM] * n, out_specs=(_HBM,) * n,
        out_shape=tuple(jax.ShapeDtypeStruct((N_DEV,) + a.shape, a.dtype) for a in arrs),
        scratch_shapes=[pltpu.SemaphoreType.DMA((n * slots,)), pltpu.SemaphoreType.DMA((n * slots,)),
                        pltpu.SemaphoreType.DMA((n,))],
    )(*arrs)


_SEM = pl.BlockSpec(memory_space=pltpu.SEMAPHORE)
_EFFECT = pltpu.SideEffectType.DATAFLOW_SIDE_EFFECTING


def _direct_copies(src_refs, land_refs, send_sems, recv_sems, gather):
    x, y, c, _ = _mesh_pos()
    me = 4 * x + 2 * y + c
    n_peer = N_DEV - 1
    copies = []
    for r in range(1, N_DEV):
        px = 1 - x if r & 4 else x
        py = 1 - y if r & 2 else y
        pc = 1 - c if r & 1 else c
        for k, (src, land) in enumerate(zip(src_refs, land_refs)):
            copies.append(pltpu.make_async_remote_copy(
                src_ref=src if gather else src.at[4 * px + 2 * py + pc], dst_ref=land.at[me],
                send_sem=send_sems.at[k * n_peer + r - 1], recv_sem=recv_sems.at[k * n_peer + r - 1],
                device_id=(px, py, pc), device_id_type=_MESH))
    return copies


def _send_start(arrs, gather, after=None, *, name):
    n = len(arrs)
    lands = [lax.empty(((N_DEV,) + a.shape) if gather else a.shape, a.dtype) for a in arrs]
    n_in = 2 * n + (0 if after is None else 1)

    def body(*refs):
        src_refs, land_refs, send_sems, recv_sems, token = refs[:n], refs[n:2 * n], refs[n_in], refs[n_in + 1], refs[-1]
        for cp in _direct_copies(src_refs, land_refs, send_sems, recv_sems, gather):
            cp.start()
        token[...] = jnp.zeros_like(token)

    n_sem = n * (N_DEV - 1)
    bufs = list(arrs) + lands
    out = pl.pallas_call(
        body, name=name,
        out_shape=(pltpu.SemaphoreType.DMA((n_sem,)), pltpu.SemaphoreType.DMA((n_sem,)))
        + tuple(pltpu.HBM(b.shape, b.dtype) for b in bufs) + (jax.ShapeDtypeStruct((HALO, LANES), F32),),
        in_specs=[_HBM] * n_in, out_specs=(_SEM, _SEM) + (_HBM,) * (2 * n) + (pl.BlockSpec(memory_space=pltpu.VMEM),),
        input_output_aliases={i: 2 + i for i in range(2 * n)},
        compiler_params=pltpu.CompilerParams(has_side_effects=_EFFECT),
    )(*[pltpu.with_memory_space_constraint(b, pltpu.HBM) for b in bufs], *([] if after is None else [after]))
    return (out[0], out[1], list(out[2:2 + n]), list(out[2 + n:2 + 2 * n])), out[-1]


def _send_wait(started, gather, after, *, name):
    send_sems, recv_sems, srcs, lands = started
    n = len(srcs)

    def body(*refs):
        src_refs, land_refs, send_ref, recv_ref = refs[:n], refs[n:2 * n], refs[2 * n], refs[2 * n + 1]
        for cp in _direct_copies(src_refs, land_refs, send_ref, recv_ref, gather):
            cp.wait_send()
            cp.wait_recv()

    bufs = srcs + lands
    out = pl.pallas_call(
        body, name=name, out_shape=tuple(pltpu.HBM(b.shape, b.dtype) for b in bufs),
        in_specs=[_HBM] * (2 * n) + [_SEM, _SEM, _HBM], out_specs=(_HBM,) * (2 * n),
        input_output_aliases={i: i for i in range(2 * n)},
        compiler_params=pltpu.CompilerParams(has_side_effects=_EFFECT),
    )(*bufs, send_sems, recv_sems, after)
    return list(out[:n]), list(out[n:])


def _row_block(rows, limit=256):
    best = rows
    for cand in range(8, limit + 1, 8):
        if rows % cand == 0:
            best = cand
    return best if rows > limit else rows


def _adam(gp, w, m, v, *, name):
    p, rows, cols = gp.shape
    rb = _row_block(rows)

    def body(gp_ref, w_ref, m_ref, v_ref, g_ref, d_ref, m2_ref, v2_ref):
        g = gp_ref[0].astype(F32)
        for s in range(1, p):
            g = g + gp_ref[s].astype(F32)
        m2 = ADAM_B1 * m_ref[...] + (1.0 - ADAM_B1) * g
        v2 = ADAM_B2 * v_ref[...] + (1.0 - ADAM_B2) * (g * g)
        m_hat = m2 / (1.0 - ADAM_B1 ** ADAM_STEP)
        v_hat = v2 / (1.0 - ADAM_B2 ** ADAM_STEP)
        g_ref[...] = g
        d_ref[...] = -ADAM_LR * (m_hat / (jnp.sqrt(v_hat) + ADAM_EPS) + ADAM_WD * w_ref[...])
        m2_ref[...] = m2
        v2_ref[...] = v2

    blk = pl.BlockSpec((rb, cols), lambda i: (i, 0))
    return pl.pallas_call(
        body, name=name, grid=(rows // rb,),
        in_specs=[pl.BlockSpec((p, rb, cols), lambda i: (0, i, 0)), blk, blk, blk],
        out_specs=(blk,) * 4, out_shape=(jax.ShapeDtypeStruct((rows, cols), F32),) * 4,
        compiler_params=_cparams(("parallel",)),
    )(gp, w, m, v)


def _cols_full(g):
    return jnp.transpose(g, (1, 0, 2)).reshape(g.shape[1], N_DEV * g.shape[2])


def _pad_lanes(a, width=LANES):
    return jnp.pad(a, ((0, 0), (0, width - a.shape[1])))


def _chunk_rows_of(a):
    by_chunk = jnp.transpose(a[:, :DN_HEADS].reshape(-1, DN_CHUNK, DN_HEADS), (0, 2, 1))
    return jnp.pad(by_chunk, ((0, 0), (0, HALO - DN_HEADS), (0, 0)))


_SMALL = (("ffn1_norm", D_MODEL), ("mix_norm", D_MODEL), ("ffn2_norm", D_MODEL), ("final_norm", D_MODEL), ("a_log", DN_HEADS),
          ("dt_bias", DN_HEADS), ("dn_norm", DN_HEAD_DIM), ("sg_ln_g", SG_WIDTH), ("sg_ln_b", SG_WIDTH),
          ("sg_w", SG_GROUPS * SG_CHUNK * SG_CHUNK), ("sg_b", SG_GROUPS * SG_CHUNK), ("conv_w", CONV_K * 3 * DN_WIDTH))
_SMALL_ROWS = 1128
_SMALL_SHAPES = {"ffn1_norm": (1, D_MODEL), "mix_norm": (1, D_MODEL), "ffn2_norm": (1, D_MODEL), "final_norm": (D_MODEL,),
                 "a_log": (1, DN_HEADS), "dt_bias": (1, DN_HEADS), "dn_norm": (1, DN_HEAD_DIM), "sg_ln_g": (1, SG_WIDTH),
                 "sg_ln_b": (1, SG_WIDTH), "sg_w": (1, SG_GROUPS, SG_CHUNK, SG_CHUNK), "sg_b": (1, SG_GROUPS, SG_CHUNK)}


def _pack_small(d):
    flat = jnp.concatenate([d[name].reshape(-1) for name, _ in _SMALL])
    return jnp.pad(flat, (0, _SMALL_ROWS * LANES - flat.shape[0])).reshape(_SMALL_ROWS, LANES)


def _unpack_small(a):
    flat, out, at = a.reshape(-1), {}, 0
    for name, size in _SMALL:
        out[name] = flat[at:at + size]
        at += size
    return out


def kernel(x, ffn1_norm, ffn1_w_gate, ffn1_w_up, ffn1_w_down, mix_norm, w_in, conv_w, a_log, dt_bias, dn_norm, sg_ln_g, sg_ln_b, sg_w, sg_b, w_out, ffn2_norm, ffn2_w_gate, ffn2_w_up, ffn2_w_down, final_norm, loss_target, m_ffn1_norm, m_ffn1_w_gate, m_ffn1_w_up, m_ffn1_w_down, m_mix_norm, m_w_in, m_conv_w, m_a_log, m_dt_bias, m_dn_norm, m_sg_ln_g, m_sg_ln_b, m_sg_w, m_sg_b, m_w_out, m_ffn2_norm, m_ffn2_w_gate, m_ffn2_w_up, m_ffn2_w_down, m_final_norm, v_ffn1_norm, v_ffn1_w_gate, v_ffn1_w_up, v_ffn1_w_down, v_mix_norm, v_w_in, v_conv_w, v_a_log, v_dt_bias, v_dn_norm, v_sg_ln_g, v_sg_ln_b, v_sg_w, v_sg_b, v_w_out, v_ffn2_norm, v_ffn2_w_gate, v_ffn2_w_up, v_ffn2_w_down, v_final_norm):
    weights = dict(ffn1_norm=ffn1_norm, ffn1_w_gate=ffn1_w_gate, ffn1_w_up=ffn1_w_up, ffn1_w_down=ffn1_w_down, mix_norm=mix_norm, w_in=w_in, conv_w=conv_w, a_log=a_log, dt_bias=dt_bias, dn_norm=dn_norm, sg_ln_g=sg_ln_g, sg_ln_b=sg_ln_b, sg_w=sg_w, sg_b=sg_b, w_out=w_out, ffn2_norm=ffn2_norm, ffn2_w_gate=ffn2_w_gate, ffn2_w_up=ffn2_w_up, ffn2_w_down=ffn2_w_down, final_norm=final_norm)
    mom_m = dict(ffn1_norm=m_ffn1_norm, ffn1_w_gate=m_ffn1_w_gate, ffn1_w_up=m_ffn1_w_up, ffn1_w_down=m_ffn1_w_down, mix_norm=m_mix_norm, w_in=m_w_in, conv_w=m_conv_w, a_log=m_a_log, dt_bias=m_dt_bias, dn_norm=m_dn_norm, sg_ln_g=m_sg_ln_g, sg_ln_b=m_sg_ln_b, sg_w=m_sg_w, sg_b=m_sg_b, w_out=m_w_out, ffn2_norm=m_ffn2_norm, ffn2_w_gate=m_ffn2_w_gate, ffn2_w_up=m_ffn2_w_up, ffn2_w_down=m_ffn2_w_down, final_norm=m_final_norm)
    mom_v = dict(ffn1_norm=v_ffn1_norm, ffn1_w_gate=v_ffn1_w_gate, ffn1_w_up=v_ffn1_w_up, ffn1_w_down=v_ffn1_w_down, mix_norm=v_mix_norm, w_in=v_w_in, conv_w=v_conv_w, a_log=v_a_log, dt_bias=v_dt_bias, dn_norm=v_dn_norm, sg_ln_g=v_sg_ln_g, sg_ln_b=v_sg_ln_b, sg_w=v_sg_w, sg_b=v_sg_b, w_out=v_w_out, ffn2_norm=v_ffn2_norm, ffn2_w_gate=v_ffn2_w_gate, ffn2_w_up=v_ffn2_w_up, ffn2_w_down=v_ffn2_w_down, final_norm=v_final_norm)
    order = list(weights)
    big = ("ffn1_w_gate", "ffn1_w_up", "ffn1_w_down", "w_in", "w_out", "ffn2_w_gate", "ffn2_w_up", "ffn2_w_down")
    col_sharded = ("ffn1_w_gate", "ffn1_w_up", "w_in", "ffn2_w_gate", "ffn2_w_up")

    n_seq, seq, _ = x.shape
    t = n_seq * seq
    me = 4 * lax.axis_index("x") + 2 * lax.axis_index("y") + lax.axis_index("c")
    x0 = x.reshape(t, D_MODEL)
    tgt = loss_target.reshape(t, D_MODEL)

    def fill_own(land, own_block):
        return lax.dynamic_update_index_in_dim(land, own_block, me, 0)

    def rows_view(n, a):
        return jnp.transpose(a) if n in col_sharded else a

    def as_full(n, g):
        return g.reshape(-1, g.shape[-1])

    shards = {n: rows_view(n, weights[n][0]).astype(BF16) for n in big}
    ffn1_names, mix_names, ffn2_names = big[:3], big[3:5], big[5:]
    full = {n: as_full(n, g) for n, g in zip(ffn1_names, _gather2([shards[n] for n in ffn1_names], name="gather_ffn1"))}
    mix_srcs = [shards[n] for n in mix_names] + [conv_w[0]]
    mix_started, mix_token = _send_start(mix_srcs, True, full[ffn1_names[2]], name="gather_mix_start")
    ffn2_started, ffn2_token = _send_start([shards[n] for n in ffn2_names], True, mix_token, name="gather_ffn2_start")
    ffn1_norm_fwd = ffn1_norm + ffn2_token[:1, :1]
    alog, dtb = _pad_lanes(a_log), _pad_lanes(dt_bias)
    sgbt = _pad_lanes(sg_b[0].T)
    fnw = final_norm.reshape(1, D_MODEL)

    x1, h1, g1, u1 = _ffn_fwd(x0, ffn1_norm_fwd, full["ffn1_w_gate"], full["ffn1_w_up"], full["ffn1_w_down"], name="ffn1_fwd")
    mix_lands = [fill_own(land, src) for src, land in zip(*_send_wait(mix_started, True, x1, name="gather_mix_wait"))]
    full.update({n: as_full(n, g) for n, g in zip(mix_names, mix_lands)})
    conv_full = _cols_full(mix_lands[-1])
    w_in_t = full["w_in"]
    offs = (0, SG_WIDTH, 2 * SG_WIDTH, 2 * SG_WIDTH + 3 * DN_WIDTH, 2 * SG_WIDTH + 4 * DN_WIDTH)
    n_proj = offs[-1]

    def pad_rows(a):
        return jnp.pad(a, ((0, LANES - a.shape[0]), (0, 0)))

    ws = [w_in_t[offs[0]:offs[1]], w_in_t[offs[1]:offs[2]], w_in_t[offs[2]:offs[3]], w_in_t[offs[3]:offs[4]],
          pad_rows(w_in_t[n_proj:n_proj + DN_HEADS]), pad_rows(w_in_t[n_proj + DN_HEADS:n_proj + 2 * DN_HEADS])]
    wo_sg, wo_dn = full["w_out"][:SG_WIDTH], full["w_out"][SG_WIDTH:]
    u, v, qkv, z, bpre, apre = _mix_in_fwd(x1, mix_norm, ws, name="mix_in_fwd")
    sg_out = _sg_fwd(u, v, sg_ln_g, sg_ln_b, sg_w[0], sgbt, name="sg_fwd")
    q, k, vv, beta, gc = _dn_prep_fwd(qkv, bpre, apre, conv_full, alog, dtb, seq, name="dn_prep_fwd")
    grow = _chunk_rows_of(gc)
    wy_w, wy_u, q_dec, k_dec, qk, egl, inv = _delta_prep(q, k, vv, gc, grow, beta, name="delta_prep")
    o, states = _delta_seq_fwd(wy_w, wy_u, q_dec, k_dec, qk, egl, n_seq, seq, name="delta_seq_fwd")
    x2 = _mix_out_fwd(x1, sg_out, o, z, wo_sg, wo_dn, dn_norm, name="mix_out_fwd")
    ffn2_srcs, ffn2_lands = _send_wait(ffn2_started, True, x2, name="gather_ffn2_wait")
    full.update({n: as_full(n, fill_own(land, src)) for n, src, land in zip(ffn2_names, ffn2_srcs, ffn2_lands)})
    dx3, loss_part, d_fn, h2, g2, u2 = _ffn_fwd(x2, ffn2_norm, full["ffn2_w_gate"], full["ffn2_w_up"], full["ffn2_w_down"],
                                                tgt, fnw, name="ffn2_fwd_loss")
    loss = lax.psum(loss_part[0, 0], ("x", "y", "c"))

    dx2, d_n2, d_g2, d_u2, d_d2 = _ffn_bwd(x2, ffn2_norm, h2, g2, u2, full["ffn2_w_gate"], full["ffn2_w_up"],
                                           full["ffn2_w_down"], dx3, name="ffn2_bwd")
    def by_owner(d_rows):
        return d_rows.reshape(N_DEV, -1, D_MODEL)

    ffn2_pieces = [by_owner(d_g2), by_owner(d_u2), by_owner(d_d2)]
    ffn2_sent, sent_token = _send_start(ffn2_pieces, False, name="grads_ffn2_start")
    dsg, do, dz, d_wo_sg, d_wo_dn, d_dnw = _mix_out_bwd(dx2, sg_out, o, z, wo_sg, wo_dn, dn_norm + sent_token[:1, :1],
                                                        name="mix_out_bwd")
    d_seq = _delta_seq_bwd(wy_w, wy_u, q_dec, k_dec, qk, egl, states, do, n_seq, seq, name="delta_seq_bwd")
    dq, dk, dv, dgc_a, dgrow, dbeta = _delta_par_bwd(q, k, vv, gc, grow, beta, inv, *d_seq, name="delta_par_bwd")
    dgc_b = _pad_lanes(jnp.transpose(dgrow[:, :DN_HEADS, :], (0, 2, 1)).reshape(t, DN_HEADS))
    dqkv, d_conv, dbpre, dapre, d_alog, d_dtb = _dn_prep_bwd(qkv, bpre, apre, conv_full, alog, dtb, dq, dk, dv, dbeta, dgc_a,
                                                             dgc_b, seq, name="dn_prep_bwd")
    du, dvv, d_lng, d_lnb, d_wc, d_sgbt = _sg_bwd(u, v, sg_ln_g, sg_ln_b, sg_w[0], sgbt, dsg, name="sg_bwd")
    dx1, d_mixn, d_wp = _mix_in_bwd(x1, mix_norm, ws, dx2, (du, dvv, dqkv, dz, dbpre, dapre), name="mix_in_bwd")
    d_w_in_t = jnp.concatenate([d_wp[:n_proj], d_wp[_PROJ_OFFSETS[4]:_PROJ_OFFSETS[4] + DN_HEADS],
                                d_wp[_PROJ_OFFSETS[5]:_PROJ_OFFSETS[5] + DN_HEADS]], axis=0)
    d_w_out = jnp.concatenate([d_wo_sg, d_wo_dn], axis=0)
    mix_pieces = [by_owner(d_w_in_t), by_owner(d_w_out).astype(BF16)]
    mix_sent, sent_token = _send_start(mix_pieces, False, name="grads_mix_start")
    grad_x, d_n1, dg1, du1, a1, dyh1 = _ffn_bwd_x(x0, ffn1_norm + sent_token[:1, :1], g1, u1, full["ffn1_w_gate"],
                                                  full["ffn1_w_up"], full["ffn1_w_down"], dx1, name="ffn1_bwd_x")
    small_grads = dict(ffn1_norm=d_n1, mix_norm=d_mixn, ffn2_norm=d_n2, final_norm=d_fn, a_log=d_alog[:, :DN_HEADS],
                       dt_bias=d_dtb[:, :DN_HEADS], dn_norm=d_dnw, sg_ln_g=d_lng, sg_ln_b=d_lnb, sg_w=d_wc,
                       sg_b=d_sgbt[:, :SG_GROUPS].T, conv_w=d_conv[:CONV_K])
    small_src = _pack_small(small_grads)
    small_sent, small_token = _send_start([small_src], True, name="small_grads_start")
    late, tokens = [], []

    def send_early(k, grad):
        piece = by_owner(grad)
        sent, token = _send_start([piece], False, name="grads_" + ffn1_names[k] + "_start")
        late.append(((ffn1_names[k],), sent))
        tokens.append(token)
        return token

    _ffn_wgrads(h1, dg1, du1, a1, dyh1, send_early, small_token, name="ffn1_bwd")

    res = {}
    after = tokens[-1]

    def update(names, sent, after):
        pieces, lands = _send_wait(sent, False, after, name="grads_" + names[0] + "_wait")
        for n, land, p in zip(names, lands, pieces):
            got = fill_own(land, lax.dynamic_index_in_dim(p, me, 0, keepdims=False))
            upd = _adam(got, *[rows_view(n, src[n][0]) for src in (weights, mom_m, mom_v)], name="adam_" + n)
            res[n] = [rows_view(n, a) for a in upd]
            after = upd[0]
        return after

    for group in [(ffn2_names, ffn2_sent), (mix_names, mix_sent)] + late[:-1]:
        after = update(*group, after)
    (small_src,), (small_land,) = _send_wait(small_sent, True, after, name="small_grads_wait")
    small_parts = fill_own(small_land, small_src)
    zeros_conv = jnp.zeros((CONV_K * 3 * DN_WIDTH,), F32)
    packed = [_pack_small({**{n: src[n] for n, _ in _SMALL if n != "conv_w"}, "conv_w": zeros_conv})
              for src in (weights, mom_m, mom_v)]
    small_upd = _adam(small_parts, *packed, name="adam_small")
    small_res = [_unpack_small(a) for a in small_upd]
    conv_grad = lax.dynamic_slice_in_dim(small_res[0]["conv_w"].reshape(CONV_K, 3 * DN_WIDTH), me * (3 * DN_WIDTH // N_DEV),
                                         3 * DN_WIDTH // N_DEV, axis=1)
    res["conv_w"] = _adam(conv_grad[None], conv_w[0], m_conv_w[0], v_conv_w[0], name="adam_conv_w")
    update(*late[-1], res["conv_w"][0])

    outs = [[], [], [], []]
    for n in order:
        for kind in range(4):
            if n in res:
                outs[kind].append(res[n][kind][None])
            else:
                outs[kind].append(small_res[kind][n].reshape(_SMALL_SHAPES[n]))
    return (loss, grad_x.reshape(x.shape), *outs[0], *outs[1], *outs[2], *outs[3])
```

```python
import jax
import jax.numpy as jnp
from jax import lax
from jax.experimental import pallas as pl
from jax.experimental.pallas import tpu as pltpu

F32 = jnp.float32
BF16 = jnp.bfloat16

D_MODEL = 1024
D_FF = 2816
SG_WIDTH = 512
SG_GROUPS = 8
SG_GROUP_DIM = 64
SG_CHUNK = 128
DN_WIDTH = 512
DN_HEAD_DIM = 128
DN_HEADS = 4
DN_CHUNK = 64
CONV_K = 4
EPS = 1e-6
N_DEV = 8
LANES = 128
HALO = 8
MXU_COLS = 256

ADAM_LR = 0.001
ADAM_B1 = 0.9
ADAM_B2 = 0.999
ADAM_EPS = 1e-08
ADAM_WD = 0.01
ADAM_STEP = 10

VMEM_LIMIT = 60 * 1024 * 1024
WGRAD_K_TILE = 2048
TOKEN_BLOCK = 512
FF_BLOCK_FWD = 1408

_HI = lax.Precision.HIGHEST


def _cparams(sem):
    return pltpu.CompilerParams(dimension_semantics=sem, vmem_limit_bytes=VMEM_LIMIT)


def _tm(t, pref=TOKEN_BLOCK):
    return min(pref, t)


def _dg(a, b, ca, cb, precision):
    if precision is not None:
        return lax.dot_general(a, b, (((ca,), (cb,)), ((), ())), precision=precision, preferred_element_type=F32)
    return lax.dot_general(a.astype(BF16), b.astype(BF16), (((ca,), (cb,)), ((), ())), preferred_element_type=F32)


def _make_mm(precision):
    @jax.custom_vjp
    def mm(a, b):
        return _dg(a, b, 1, 0, precision)

    @jax.custom_vjp
    def mm_nt(a, b):
        return _dg(a, b, 1, 1, precision)

    @jax.custom_vjp
    def mm_tn(a, b):
        return _dg(a, b, 0, 0, precision)

    mm.defvjp(lambda a, b: (mm(a, b), (a, b)), lambda r, g: (mm_nt(g, r[1]), mm_tn(r[0], g)))
    mm_nt.defvjp(lambda a, b: (mm_nt(a, b), (a, b)), lambda r, g: (mm(g, r[1]), mm_tn(g, r[0])))
    mm_tn.defvjp(lambda a, b: (mm_tn(a, b), (a, b)), lambda r, g: (mm_nt(r[1], g), mm(r[0], g)))
    return mm, mm_nt, mm_tn


mm, mm_nt, mm_tn = _make_mm(None)
mmx, mmx_nt, mmx_tn = _make_mm(_HI)
mmh, mmh_nt, mmh_tn = _make_mm(lax.Precision.HIGH)


def _sigmoid(x):
    return 1.0 / (1.0 + jnp.exp(-x))


def _silu(x):
    return x * _sigmoid(x)


def _softplus(x):
    neg_abs = jnp.where(x > 0, -x, x)
    return jnp.where(x > 0, x, 0.0) + jnp.log(1.0 + jnp.exp(neg_abs))


def _gelu(x):
    return 0.5 * x * (1.0 + jnp.tanh(0.7978845608028654 * (x + 0.044715 * (x * x * x))))


def _rms_fwd(x, g):
    r = lax.rsqrt(jnp.mean(x * x, axis=-1, keepdims=True) + EPS)
    xh = x * r
    return xh * g, xh, r


def _rms_bwd(dh, xh, r, g):
    dxh = dh * g
    dx = r * (dxh - xh * jnp.mean(dxh * xh, axis=-1, keepdims=True))
    return dx, jnp.sum(dh * xh, axis=0, keepdims=True)


def _acc_out(ref, first, val):
    @pl.when(first)
    def _():
        ref[...] = val

    @pl.when(jnp.logical_not(first))
    def _():
        ref[...] += val


def _ffn_fwd(x, nw, wg, wu, wd, tgt=None, fnw=None, *, name):
    t = x.shape[0]
    tm, fb = _tm(t), FF_BLOCK_FWD
    n_t, n_f = t // tm, D_FF // fb
    with_loss = tgt is not None

    def body(*refs):
        if with_loss:
            (x_ref, nw_ref, wg_ref, wu_ref, wd_ref, tgt_ref, fnw_ref, dy_ref, loss_ref, dfn_ref, h_ref, g_ref, u_ref,
             acc_s) = refs
        else:
            x_ref, nw_ref, wg_ref, wu_ref, wd_ref, y_ref, h_ref, g_ref, u_ref, acc_s = refs
        i, j = pl.program_id(0), pl.program_id(1)

        @pl.when(j == 0)
        def _():
            h, _, _ = _rms_fwd(x_ref[...], nw_ref[...])
            h_ref[...] = h.astype(BF16)
            acc_s[...] = jnp.zeros_like(acc_s)

        h = h_ref[...]
        nt = (((1,), (1,)), ((), ()))
        g = lax.dot_general(h, wg_ref[...], nt, preferred_element_type=F32)
        u = lax.dot_general(h, wu_ref[...], nt, preferred_element_type=F32)
        g_ref[...] = g.astype(BF16)
        u_ref[...] = u.astype(BF16)
        a = _silu(g) * u
        acc_s[...] += jnp.dot(a.astype(BF16), wd_ref[...], preferred_element_type=F32)

        @pl.when(j == n_f - 1)
        def _():
            y = x_ref[...] + 0.5 * acc_s[...]
            if not with_loss:
                y_ref[...] = y
            else:
                gf = fnw_ref[...]
                out, xh, r = _rms_fwd(y, gf)
                err = out - tgt_ref[...]
                part = 0.5 * jnp.sum(jnp.mean(err * err, axis=-1, keepdims=True), axis=0, keepdims=True)
                d_out = err * (1.0 / D_MODEL)
                dy, dgf = _rms_bwd(d_out, xh, r, gf)
                dy_ref[...] = dy
                _acc_out(loss_ref, i == 0, jnp.broadcast_to(part, loss_ref.shape))
                _acc_out(dfn_ref, i == 0, dgf)

    row = lambda i, j: (i, 0)
    const = lambda i, j: (0, 0)
    in_specs = [
        pl.BlockSpec((tm, D_MODEL), row),
        pl.BlockSpec((1, D_MODEL), const),
        pl.BlockSpec((fb, D_MODEL), lambda i, j: (j, 0)),
        pl.BlockSpec((fb, D_MODEL), lambda i, j: (j, 0)),
        pl.BlockSpec((fb, D_MODEL), lambda i, j: (j, 0)),
    ]
    args = [x, nw, wg, wu, wd]
    saved_shape = (jax.ShapeDtypeStruct((t, D_MODEL), BF16), jax.ShapeDtypeStruct((t, D_FF), BF16),
                   jax.ShapeDtypeStruct((t, D_FF), BF16))
    saved_specs = (pl.BlockSpec((tm, D_MODEL), row), pl.BlockSpec((tm, fb), lambda i, j: (i, j)),
                   pl.BlockSpec((tm, fb), lambda i, j: (i, j)))
    if with_loss:
        in_specs += [pl.BlockSpec((tm, D_MODEL), row), pl.BlockSpec((1, D_MODEL), const)]
        args += [tgt, fnw]
        out_shape = (jax.ShapeDtypeStruct((t, D_MODEL), F32), jax.ShapeDtypeStruct((8, LANES), F32),
                     jax.ShapeDtypeStruct((1, D_MODEL), F32)) + saved_shape
        out_specs = (pl.BlockSpec((tm, D_MODEL), row), pl.BlockSpec((8, LANES), const),
                     pl.BlockSpec((1, D_MODEL), const)) + saved_specs
        sem = ("arbitrary", "arbitrary")
    else:
        out_shape = (jax.ShapeDtypeStruct((t, D_MODEL), F32),) + saved_shape
        out_specs = (pl.BlockSpec((tm, D_MODEL), row),) + saved_specs
        sem = ("parallel", "arbitrary")
    return pl.pallas_call(
        body, name=name, grid=(n_t, n_f), in_specs=in_specs, out_specs=out_specs, out_shape=out_shape,
        scratch_shapes=[pltpu.VMEM((tm, D_MODEL), F32)],
        compiler_params=_cparams(sem),
    )(*args)


def _ffn_bwd_x(x, nw, g, u, wg, wu, wd, dy, *, name):
    t = x.shape[0]
    tm = _tm(t, 256)

    def body(x_ref, nw_ref, g_ref, u_ref, wg_ref, wu_ref, wd_ref, dy_ref, dx_ref, dnw_ref, dg_ref, du_ref, a_ref, dyh_ref):
        i = pl.program_id(0)
        nt = (((1,), (1,)), ((), ()))
        dy = dy_ref[...]
        dyh = (0.5 * dy).astype(BF16)
        dyh_ref[...] = dyh
        gate, up = g_ref[...].astype(F32), u_ref[...].astype(F32)
        s = _sigmoid(gate)
        gs = gate * s
        da = lax.dot_general(dyh, wd_ref[...], nt, preferred_element_type=F32)
        dg = (da * up * (s + gs * (1.0 - s))).astype(BF16)
        du = (da * gs).astype(BF16)
        dg_ref[...] = dg
        du_ref[...] = du
        a_ref[...] = (gs * up).astype(BF16)
        dh = (jnp.dot(dg, wg_ref[...], preferred_element_type=F32)
              + jnp.dot(du, wu_ref[...], preferred_element_type=F32))
        xv = x_ref[...]
        r = lax.rsqrt(jnp.mean(xv * xv, axis=-1, keepdims=True) + EPS)
        dx, dnw = _rms_bwd(dh, xv * r, r, nw_ref[...])
        dx_ref[...] = dy + dx
        _acc_out(dnw_ref, i == 0, dnw)

    row = lambda i: (i, 0)
    const = lambda i: (0, 0)
    once = pl.Buffered(1)
    wide = pl.BlockSpec((tm, D_FF), row)
    return pl.pallas_call(
        body, name=name, grid=(t // tm,),
        in_specs=[pl.BlockSpec((tm, D_MODEL), row), pl.BlockSpec((1, D_MODEL), const), wide, wide,
                  pl.BlockSpec((D_FF, D_MODEL), const, pipeline_mode=once), pl.BlockSpec((D_FF, D_MODEL), const, pipeline_mode=once),
                  pl.BlockSpec((D_FF, D_MODEL), const, pipeline_mode=once), pl.BlockSpec((tm, D_MODEL), row)],
        out_specs=(pl.BlockSpec((tm, D_MODEL), row), pl.BlockSpec((1, D_MODEL), const), wide, wide, wide,
                   pl.BlockSpec((tm, D_MODEL), row)),
        out_shape=(jax.ShapeDtypeStruct((t, D_MODEL), F32), jax.ShapeDtypeStruct((1, D_MODEL), F32),
                   jax.ShapeDtypeStruct((t, D_FF), BF16), jax.ShapeDtypeStruct((t, D_FF), BF16),
                   jax.ShapeDtypeStruct((t, D_FF), BF16), jax.ShapeDtypeStruct((t, D_MODEL), BF16)),
        compiler_params=_cparams(("arbitrary",)),
    )(x, nw, g, u, wg, wu, wd, dy)


def _wgrad(a, b, bm, bn, after=None, *, name):
    k, m = a.shape
    n = b.shape[1]
    tk = _tm(k, WGRAD_K_TILE)
    n_k = k // tk

    def body(a_ref, b_ref, *rest):
        o_ref, acc_s = rest[-2], rest[-1]
        s = pl.program_id(2)
        for c in range(bn // MXU_COLS):
            cols = slice(c * MXU_COLS, (c + 1) * MXU_COLS)
            part = lax.dot_general(a_ref[...], b_ref[:, cols], (((0,), (0,)), ((), ())), preferred_element_type=F32)
            acc_s[:, cols] = jnp.where(s == 0, 0.0, acc_s[:, cols]) + part

        @pl.when(s == n_k - 1)
        def _():
            o_ref[...] = acc_s[...].astype(BF16)

    return pl.pallas_call(
        body, name=name, grid=(m // bm, n // bn, n_k),
        in_specs=[pl.BlockSpec((tk, bm), lambda i, j, s: (s, i)), pl.BlockSpec((tk, bn), lambda i, j, s: (s, j))]
        + ([] if after is None else [_HBM]),
        out_specs=pl.BlockSpec((bm, bn), lambda i, j, s: (i, j)),
        out_shape=jax.ShapeDtypeStruct((m, n), BF16),
        scratch_shapes=[pltpu.VMEM((bm, bn), F32)],
        compiler_params=_cparams(("parallel", "parallel", "arbitrary")),
    )(a, b, *([] if after is None else [after]))


def _ffn_wgrads(h, dg, du, a, dyh, between=None, after=None, *, name):
    grads = []
    for k, (lhs, rhs, tag) in enumerate(((dg, h, "_wg"), (du, h, "_wu"), (a, dyh, "_wd"))):
        grads.append(_wgrad(lhs, rhs, D_FF // 2, D_MODEL, after, name=name + tag))
        after = None if between is None else between(k, grads[-1])
    return grads


def _ffn_bwd(x, nw, h, g, u, wg, wu, wd, dy, *, name):
    dx, dnw, dg, du, a, dyh = _ffn_bwd_x(x, nw, g, u, wg, wu, wd, dy, name=name + "_x")
    return (dx, dnw, *_ffn_wgrads(h, dg, du, a, dyh, name=name))


_PROJ_WIDTHS = (SG_WIDTH, SG_WIDTH, 3 * DN_WIDTH, DN_WIDTH, LANES, LANES)


def _mix_in_fwd(x, nw, ws, *, name):
    t = x.shape[0]
    tm = _tm(t)

    def body(x_ref, nw_ref, *refs):
        w_refs, o_refs = refs[:6], refs[6:]
        h, _, _ = _rms_fwd(x_ref[...], nw_ref[...])
        h = h.astype(BF16)
        for w_ref, o_ref in zip(w_refs, o_refs):
            o_ref[...] = lax.dot_general(h, w_ref[...], (((1,), (1,)), ((), ())), preferred_element_type=F32)

    row = lambda i: (i, 0)
    const = lambda i: (0, 0)
    return pl.pallas_call(
        body, name=name, grid=(t // tm,),
        in_specs=[pl.BlockSpec((tm, D_MODEL), row), pl.BlockSpec((1, D_MODEL), const)]
        + [pl.BlockSpec((n, D_MODEL), const) for n in _PROJ_WIDTHS],
        out_specs=tuple(pl.BlockSpec((tm, n), row) for n in _PROJ_WIDTHS),
        out_shape=tuple(jax.ShapeDtypeStruct((t, n), F32) for n in _PROJ_WIDTHS),
        compiler_params=_cparams(("parallel",)),
    )(x, nw, *ws)


_PROJ_TOTAL = sum(_PROJ_WIDTHS)
_PROJ_OFFSETS = tuple(sum(_PROJ_WIDTHS[:k]) for k in range(len(_PROJ_WIDTHS)))


def _mix_in_bwd(x, nw, ws, dres, dps, *, name):
    t = x.shape[0]
    tm = _tm(t)

    def body(x_ref, nw_ref, dres_ref, *refs):
        w_refs, dp_refs, dx_ref, dnw_ref, h_ref, dpb_ref = refs[:6], refs[6:12], refs[12], refs[13], refs[14], refs[15]
        i = pl.program_id(0)
        hf, xh, r = _rms_fwd(x_ref[...], nw_ref[...])
        h_ref[...] = hf.astype(BF16)
        dh = jnp.zeros((tm, D_MODEL), F32)
        for w_ref, dp_ref, off, width in zip(w_refs, dp_refs, _PROJ_OFFSETS, _PROJ_WIDTHS):
            dp = dp_ref[...].astype(BF16)
            dpb_ref[:, off:off + width] = dp
            dh = dh + jnp.dot(dp, w_ref[...], preferred_element_type=F32)
        dx, dnw = _rms_bwd(dh, xh, r, nw_ref[...])
        dx_ref[...] = dres_ref[...] + dx
        _acc_out(dnw_ref, i == 0, dnw)

    row = lambda i: (i, 0)
    const = lambda i: (0, 0)
    dx, dnw, h, dpb = pl.pallas_call(
        body, name=name + "_x", grid=(t // tm,),
        in_specs=[pl.BlockSpec((tm, D_MODEL), row), pl.BlockSpec((1, D_MODEL), const), pl.BlockSpec((tm, D_MODEL), row)]
        + [pl.BlockSpec((n, D_MODEL), const) for n in _PROJ_WIDTHS]
        + [pl.BlockSpec((tm, n), row) for n in _PROJ_WIDTHS],
        out_specs=(pl.BlockSpec((tm, D_MODEL), row), pl.BlockSpec((1, D_MODEL), const), pl.BlockSpec((tm, D_MODEL), row),
                   pl.BlockSpec((tm, _PROJ_TOTAL), row)),
        out_shape=(jax.ShapeDtypeStruct((t, D_MODEL), F32), jax.ShapeDtypeStruct((1, D_MODEL), F32),
                   jax.ShapeDtypeStruct((t, D_MODEL), BF16), jax.ShapeDtypeStruct((t, _PROJ_TOTAL), BF16)),
        compiler_params=_cparams(("arbitrary",)),
    )(x, nw, dres, *ws, *dps)
    return dx, dnw, _wgrad(dpb, h, _PROJ_TOTAL // 2, D_MODEL, name=name + "_w")


_SG_TILES = SG_WIDTH // LANES


def _lane_tiles(ref, rows=slice(None)):
    return [ref[rows, p * LANES:(p + 1) * LANES] for p in range(_SG_TILES)]


def _sg_fn(u, v, lng, lnb, wcs, sgbt):
    lane = lax.broadcasted_iota(jnp.int32, (1, LANES), 1)
    rr = lax.broadcasted_iota(jnp.int32, (SG_CHUNK, SG_CHUNK), 0)
    cc = lax.broadcasted_iota(jnp.int32, (SG_CHUNK, SG_CHUNK), 1)
    per_tile = LANES // SG_GROUP_DIM
    gu, gv = [_gelu(a) for a in u], [_gelu(a) for a in v]
    mu = sum(jnp.sum(a, axis=-1, keepdims=True) for a in gv) * (1.0 / SG_WIDTH)
    cen = [a - mu for a in gv]
    var = sum(jnp.sum(a * a, axis=-1, keepdims=True) for a in cen) * (1.0 / SG_WIDTH)
    rstd = lax.rsqrt(var + EPS)
    ln = [a * rstd * g + b for a, g, b in zip(cen, lng, lnb)]
    out = []
    for p in range(_SG_TILES):
        vs = None
        for e in range(per_tile):
            g = p * per_tile + e
            in_group = jnp.logical_and(lane >= e * SG_GROUP_DIM, lane < (e + 1) * SG_GROUP_DIM)
            w_causal = jnp.where(rr >= cc, wcs[g], 0.0)
            bias = jnp.sum(jnp.where(lane == g, sgbt, 0.0), axis=1, keepdims=True)
            term = jnp.where(in_group, mm(w_causal, ln[p]) + bias, 0.0)
            vs = term if vs is None else vs + term
        out.append(gu[p] * vs)
    return out


def _sg_fwd(u, v, lng, lnb, wc, sgbt, *, name):
    t = u.shape[0]
    tm = _tm(t)

    def body(u_ref, v_ref, lng_ref, lnb_ref, wc_ref, sgbt_ref, o_ref):
        wcs = [wc_ref[g] for g in range(SG_GROUPS)]
        for c in range(tm // SG_CHUNK):
            rows = pl.ds(c * SG_CHUNK, SG_CHUNK)
            out = _sg_fn(_lane_tiles(u_ref, rows), _lane_tiles(v_ref, rows), _lane_tiles(lng_ref), _lane_tiles(lnb_ref),
                         wcs, sgbt_ref[...])
            for p in range(_SG_TILES):
                o_ref[rows, p * LANES:(p + 1) * LANES] = out[p]

    row = lambda i: (i, 0)
    const = lambda i: (0, 0)
    return pl.pallas_call(
        body, name=name, grid=(t // tm,),
        in_specs=[pl.BlockSpec((tm, SG_WIDTH), row), pl.BlockSpec((tm, SG_WIDTH), row),
                  pl.BlockSpec((1, SG_WIDTH), const), pl.BlockSpec((1, SG_WIDTH), const),
                  pl.BlockSpec((SG_GROUPS, SG_CHUNK, SG_CHUNK), lambda i: (0, 0, 0)), pl.BlockSpec((SG_CHUNK, LANES), const)],
        out_specs=pl.BlockSpec((tm, SG_WIDTH), row),
        out_shape=jax.ShapeDtypeStruct((t, SG_WIDTH), F32),
        compiler_params=_cparams(("parallel",)),
    )(u, v, lng, lnb, wc, sgbt)


def _sg_bwd(u, v, lng, lnb, wc, sgbt, dout, *, name):
    t = u.shape[0]
    tm = _tm(t)

    def body(u_ref, v_ref, lng_ref, lnb_ref, wc_ref, sgbt_ref, do_ref, du_ref, dv_ref, dlng_ref, dlnb_ref, dwc_ref, dsgbt_ref):
        i = pl.program_id(0)
        wcs = [wc_ref[g] for g in range(SG_GROUPS)]
        tot = None
        for c in range(tm // SG_CHUNK):
            rows = pl.ds(c * SG_CHUNK, SG_CHUNK)
            _, vjp = jax.vjp(_sg_fn, _lane_tiles(u_ref, rows), _lane_tiles(v_ref, rows), _lane_tiles(lng_ref),
                             _lane_tiles(lnb_ref), wcs, sgbt_ref[...])
            du, dv, dlng, dlnb, dwcs, dsgbt = vjp(_lane_tiles(do_ref, rows))
            for p in range(_SG_TILES):
                du_ref[rows, p * LANES:(p + 1) * LANES] = du[p].astype(BF16)
                dv_ref[rows, p * LANES:(p + 1) * LANES] = dv[p].astype(BF16)
            part = (dlng, dlnb, dwcs, dsgbt)
            tot = part if tot is None else jax.tree.map(jnp.add, tot, part)
        dlng, dlnb, dwcs, dsgbt = tot
        _acc_out(dlng_ref, i == 0, jnp.concatenate(dlng, axis=1))
        _acc_out(dlnb_ref, i == 0, jnp.concatenate(dlnb, axis=1))
        _acc_out(dsgbt_ref, i == 0, dsgbt)
        for g in range(SG_GROUPS):
            @pl.when(i == 0)
            def _(g=g):
                dwc_ref[g] = dwcs[g]

            @pl.when(i > 0)
            def _(g=g):
                dwc_ref[g] += dwcs[g]

    row = lambda i: (i, 0)
    const = lambda i: (0, 0)
    wspec = pl.BlockSpec((SG_GROUPS, SG_CHUNK, SG_CHUNK), lambda i: (0, 0, 0))
    return pl.pallas_call(
        body, name=name, grid=(t // tm,),
        in_specs=[pl.BlockSpec((tm, SG_WIDTH), row), pl.BlockSpec((tm, SG_WIDTH), row),
                  pl.BlockSpec((1, SG_WIDTH), const), pl.BlockSpec((1, SG_WIDTH), const), wspec,
                  pl.BlockSpec((SG_CHUNK, LANES), const), pl.BlockSpec((tm, SG_WIDTH), row)],
        out_specs=(pl.BlockSpec((tm, SG_WIDTH), row), pl.BlockSpec((tm, SG_WIDTH), row),
                   pl.BlockSpec((1, SG_WIDTH), const), pl.BlockSpec((1, SG_WIDTH), const), wspec,
                   pl.BlockSpec((SG_CHUNK, LANES), const)),
        out_shape=(jax.ShapeDtypeStruct((t, SG_WIDTH), BF16), jax.ShapeDtypeStruct((t, SG_WIDTH), BF16),
                   jax.ShapeDtypeStruct((1, SG_WIDTH), F32), jax.ShapeDtypeStruct((1, SG_WIDTH), F32),
                   jax.ShapeDtypeStruct((SG_GROUPS, SG_CHUNK, SG_CHUNK), F32), jax.ShapeDtypeStruct((SG_CHUNK, LANES), F32)),
        compiler_params=_cparams(("arbitrary",)),
    )(u, v, lng, lnb, wc, sgbt, dout)


def _conv_taps(ext, w, tm):
    y = None
    for j in range(CONV_K):
        s = CONV_K - 1 - j
        shifted = ext if s == 0 else pltpu.roll(ext, s, 0)
        term = w[j:j + 1, :] * shifted[HALO:HALO + tm, :]
        y = term if y is None else y + term
    return y


def _post_conv(yq, yk, yv, bpre, apre, alog, dtb):
    def l2(a):
        return a * lax.rsqrt(jnp.sum(a * a, axis=-1, keepdims=True) + EPS)

    q = [l2(_silu(a)) for a in yq]
    k = [l2(_silu(a)) for a in yk]
    return q, k, _silu(yv), _sigmoid(bpre), -jnp.exp(alog) * _softplus(apre + dtb)


def _chunk_tril(tm):
    rr = lax.broadcasted_iota(jnp.int32, (tm, tm), 0)
    cc = lax.broadcasted_iota(jnp.int32, (tm, tm), 1)
    shift = DN_CHUNK.bit_length() - 1
    same = jnp.right_shift(rr, shift) == jnp.right_shift(cc, shift)
    return jnp.where(jnp.logical_and(same, rr >= cc), 1.0, 0.0).astype(F32)


def _halo_specs(tm, width, n_blocks_seq, n_blocks):
    per = tm // HALO
    prev = pl.BlockSpec((HALO, width), lambda i: (jnp.maximum(i * per - 1, 0), 0))
    nxt = pl.BlockSpec((HALO, width), lambda i: (jnp.minimum((i + 1) * per, n_blocks * per - 1), 0))
    return prev, nxt


def _split_heads(ref, base):
    return [ref[:, base + h * DN_HEAD_DIM: base + (h + 1) * DN_HEAD_DIM] for h in range(DN_HEADS)]


def _dn_prep_fwd(qkv, bpre, apre, conv_w, alog, dtb, seq, *, name):
    t = qkv.shape[0]
    tm = _tm(t)
    bps = seq // tm
    cw = 3 * DN_WIDTH

    def body(x_ref, halo_ref, b_ref, a_ref, w_ref, alog_ref, dtb_ref, q_ref, k_ref, v_ref, beta_ref, gc_ref):
        i = pl.program_id(0)
        keep = jnp.where(i % bps == 0, 0.0, 1.0)
        ext = jnp.concatenate([halo_ref[...] * keep, x_ref[...]], axis=0)
        y = _conv_taps(ext, w_ref[...], tm)
        yq = [y[:, h * DN_HEAD_DIM:(h + 1) * DN_HEAD_DIM] for h in range(DN_HEADS)]
        yk = [y[:, DN_WIDTH + h * DN_HEAD_DIM: DN_WIDTH + (h + 1) * DN_HEAD_DIM] for h in range(DN_HEADS)]
        q, k, v, beta, g = _post_conv(yq, yk, y[:, 2 * DN_WIDTH:], b_ref[...], a_ref[...], alog_ref[...], dtb_ref[...])
        for h in range(DN_HEADS):
            q_ref[:, h * DN_HEAD_DIM:(h + 1) * DN_HEAD_DIM] = q[h]
            k_ref[:, h * DN_HEAD_DIM:(h + 1) * DN_HEAD_DIM] = k[h]
        v_ref[...] = v
        beta_ref[...] = beta
        gc_ref[...] = mmx(_chunk_tril(tm), g)

    row = lambda i: (i, 0)
    const = lambda i: (0, 0)
    prev, _ = _halo_specs(tm, cw, bps, t // tm)
    return pl.pallas_call(
        body, name=name, grid=(t // tm,),
        in_specs=[pl.BlockSpec((tm, cw), row), prev, pl.BlockSpec((tm, LANES), row), pl.BlockSpec((tm, LANES), row),
                  pl.BlockSpec((CONV_K, cw), const), pl.BlockSpec((1, LANES), const), pl.BlockSpec((1, LANES), const)],
        out_specs=tuple(pl.BlockSpec((tm, n), row) for n in (DN_WIDTH, DN_WIDTH, DN_WIDTH, LANES, LANES)),
        out_shape=tuple(jax.ShapeDtypeStruct((t, n), F32) for n in (DN_WIDTH, DN_WIDTH, DN_WIDTH, LANES, LANES)),
        compiler_params=_cparams(("parallel",)),
    )(qkv, qkv, bpre, apre, conv_w, alog, dtb)


def _y_heads(y):
    yq = [y[:, h * DN_HEAD_DIM:(h + 1) * DN_HEAD_DIM] for h in range(DN_HEADS)]
    yk = [y[:, DN_WIDTH + h * DN_HEAD_DIM: DN_WIDTH + (h + 1) * DN_HEAD_DIM] for h in range(DN_HEADS)]
    return yq, yk, y[:, 2 * DN_WIDTH:]


def _dn_prep_bwd(qkv, bpre, apre, conv_w, alog, dtb, dq, dk, dv, dbeta, dgc, dgc2, seq, *, name):
    t = qkv.shape[0]
    tm = _tm(t)
    bps = seq // tm
    cw = 3 * DN_WIDTH
    n_ext = tm + HALO

    def body(x_ref, halo_ref, xn_ref, b_ref, a_ref, w_ref, alog_ref, dtb_ref, dq_ref, dk_ref, dv_ref, dqn_ref, dkn_ref,
             dvn_ref, dbeta_ref, dgc_ref, dgc2_ref, dx_ref, dw_ref, db_ref, da_ref, dalog_ref, ddtb_ref):
        i = pl.program_id(0)
        keep_prev = jnp.where(i % bps == 0, 0.0, 1.0)
        keep_next = jnp.where(i % bps == bps - 1, 0.0, 1.0)
        w = w_ref[...]
        x = x_ref[...]
        ext = jnp.concatenate([halo_ref[...] * keep_prev, x], axis=0)
        yq, yk, yv = _y_heads(_conv_taps(ext, w, tm))
        _, vjp = jax.vjp(_post_conv, yq, yk, yv, b_ref[...], a_ref[...], alog_ref[...], dtb_ref[...])
        dg = mmx_tn(_chunk_tril(tm), dgc_ref[...] + dgc2_ref[...])
        dyq, dyk, dyv, db, da, dalog, ddtb = vjp((_split_heads(dq_ref, 0), _split_heads(dk_ref, 0), dv_ref[...],
                                                  dbeta_ref[...], dg))
        dy = jnp.concatenate(dyq + dyk + [dyv], axis=1)
        ext_n = jnp.concatenate([x[tm - HALO:, :], xn_ref[...]], axis=0)
        _, vjp_n = jax.vjp(lambda *ys: _post_conv(*ys, b_ref[:HALO, :], a_ref[:HALO, :], alog_ref[...], dtb_ref[...])[:3],
                           *_y_heads(_conv_taps(ext_n, w, HALO)))
        dyq_n, dyk_n, dyv_n = vjp_n((_split_heads(dqn_ref, 0), _split_heads(dkn_ref, 0), dvn_ref[...]))
        dyext = jnp.concatenate([dy, jnp.concatenate(dyq_n + dyk_n + [dyv_n], axis=1) * keep_next], axis=0)

        @pl.when(i == 0)
        def _():
            dw_ref[...] = jnp.zeros_like(dw_ref)

        dx = None
        for j in range(CONV_K):
            s = CONV_K - 1 - j
            fut = dyext if s == 0 else pltpu.roll(dyext, n_ext - s, 0)
            term = w[j:j + 1, :] * fut[0:tm, :]
            dx = term if dx is None else dx + term
            past = ext if s == 0 else pltpu.roll(ext, s, 0)
            dw_ref[j:j + 1, :] += jnp.sum(dy * past[HALO:HALO + tm, :], axis=0, keepdims=True)
        dx_ref[...] = dx.astype(BF16)
        db_ref[...] = db.astype(BF16)
        da_ref[...] = da.astype(BF16)
        _acc_out(dalog_ref, i == 0, dalog)
        _acc_out(ddtb_ref, i == 0, ddtb)

    row = lambda i: (i, 0)
    const = lambda i: (0, 0)
    prev, nxt = _halo_specs(tm, cw, bps, t // tm)
    _, nxt_h = _halo_specs(tm, DN_WIDTH, bps, t // tm)
    tok = pl.BlockSpec((tm, DN_WIDTH), row)
    lanes = pl.BlockSpec((tm, LANES), row)
    return pl.pallas_call(
        body, name=name, grid=(t // tm,),
        in_specs=[pl.BlockSpec((tm, cw), row), prev, nxt, lanes, lanes,
                  pl.BlockSpec((CONV_K, cw), const), pl.BlockSpec((1, LANES), const), pl.BlockSpec((1, LANES), const),
                  tok, tok, tok, nxt_h, nxt_h, nxt_h, lanes, lanes, lanes],
        out_specs=(pl.BlockSpec((tm, cw), row), pl.BlockSpec((HALO, cw), const), lanes, lanes,
                   pl.BlockSpec((1, LANES), const), pl.BlockSpec((1, LANES), const)),
        out_shape=(jax.ShapeDtypeStruct((t, cw), BF16), jax.ShapeDtypeStruct((HALO, cw), F32),
                   jax.ShapeDtypeStruct((t, LANES), BF16), jax.ShapeDtypeStruct((t, LANES), BF16),
                   jax.ShapeDtypeStruct((1, LANES), F32), jax.ShapeDtypeStruct((1, LANES), F32)),
        compiler_params=_cparams(("arbitrary",)),
    )(qkv, qkv, qkv, bpre, apre, conv_w, alog, dtb, dq, dk, dv, dq, dk, dv, dbeta, dgc, dgc2)


def _inv_unit_lower(l_mats, eye):
    invs = [eye - l for l in l_mats]
    powers = list(l_mats)
    n = 2
    while n < eye.shape[0]:
        f = mmh if n == 2 else mm
        powers = [f(p, p) for p in powers]
        invs = [inv + f(inv, p) for inv, p in zip(invs, powers)]
        n *= 2
    return invs


@jax.custom_vjp
def _solve(l_mat, rhs, inv):
    return mmh(inv, rhs)


def _solve_fwd(l_mat, rhs, inv):
    sol = mmh(inv, rhs)
    return sol, (inv, sol)


def _solve_bwd(res, d_sol):
    inv, sol = res
    d_rhs = mm_tn(inv, d_sol)
    return -mm_nt(d_rhs, sol), d_rhs, jnp.zeros_like(inv)


_solve.defvjp(_solve_fwd, _solve_bwd)


def _prep_fn(q, k, v, gc, gr, b, inv):
    ids = range(len(q))
    c = q[0].shape[0]
    rr = lax.broadcasted_iota(jnp.int32, (c, c), 0)
    cc = lax.broadcasted_iota(jnp.int32, (c, c), 1)
    incl, strict = rr >= cc, rr > cc
    is_last = lax.broadcasted_iota(jnp.int32, (c, 1), 0) == c - 1
    qs = [q[i] * (DN_HEAD_DIM ** -0.5) for i in ids]
    decay = [jnp.where(incl, jnp.exp(jnp.where(incl, gc[i] - gr[i], 0.0)), 0.0) for i in ids]
    kb = [k[i] * b[i] for i in ids]
    vb = [v[i] * b[i] for i in ids]
    kk = [mm_nt(kb[i], k[i]) for i in ids]
    l_mat = [jnp.where(strict, kk[i] * decay[i], 0.0) for i in ids]
    eg = [jnp.exp(gc[i]) for i in ids]
    if inv is None:
        inv = _inv_unit_lower(l_mat, jnp.where(rr == cc, 1.0, 0.0).astype(F32))
    u_wy = [_solve(l_mat[i], vb[i], inv[i]) for i in ids]
    w_wy = [_solve(l_mat[i], kb[i] * eg[i], inv[i]) for i in ids]
    qk = [mm_nt(qs[i], k[i]) * decay[i] for i in ids]
    g_last = [jnp.sum(jnp.where(is_last, gc[i], 0.0), axis=0, keepdims=True) for i in ids]
    k_dec = [k[i] * jnp.exp(g_last[i] - gc[i]) for i in ids]
    egl = [jnp.broadcast_to(jnp.exp(g_last[i]), (1, LANES)) for i in ids]
    return [(w_wy[i], u_wy[i], qs[i] * eg[i], k_dec[i], qk[i], egl[i]) for i in ids], inv


def _seq_fn(w, u, qd, kd, qk, egl, s):
    ids = range(len(w))
    ws = [mm(w[i], s[i]) for i in ids]
    qs = [mm(qd[i], s[i]) for i in ids]
    v_new = [u[i] - ws[i] for i in ids]
    o = [qs[i] + mm(qk[i], v_new[i]) for i in ids]
    s_new = [s[i] * egl[i] + mm_tn(kd[i], v_new[i]) for i in ids]
    return o, s_new


def _lane_col(a, h):
    lane = lax.broadcasted_iota(jnp.int32, (1, LANES), 1)
    return jnp.sum(jnp.where(lane == h, a, 0.0), axis=1, keepdims=True)


def _col_lane(col, h):
    lane = lax.broadcasted_iota(jnp.int32, (1, LANES), 1)
    return jnp.where(lane == h, col, 0.0)


def _head_cols(h):
    return slice(h * DN_HEAD_DIM, (h + 1) * DN_HEAD_DIM)


def _chunk_rows(n):
    return pl.ds(pl.multiple_of(n * DN_CHUNK, DN_CHUNK), DN_CHUNK)


def _delta_prep(q, k, v, gc, grow, beta, *, name):
    t = q.shape[0]
    tm = _tm(t)
    cpb = tm // DN_CHUNK
    n_chunks = t // DN_CHUNK
    group = 4 if cpb % 4 == 0 else 2

    def body(q_ref, k_ref, v_ref, gc_ref, gr_ref, b_ref, w_ref, u_ref, qd_ref, kd_ref, qk_ref, egl_ref, inv_ref):
        def step(m, carry):
            probs = [(m * group + e, h) for e in range(group) for h in range(DN_HEADS)]
            gcb = [gc_ref[_chunk_rows(m * group + e), :] for e in range(group)]
            bb = [b_ref[_chunk_rows(m * group + e), :] for e in range(group)]
            grb = [gr_ref[m * group + e] for e in range(group)]
            for e in range(group):
                egl_ref[m * group + e] = jnp.zeros((HALO, LANES), F32)
            outs, invs = _prep_fn(
                [q_ref[_chunk_rows(n), _head_cols(h)] for n, h in probs], [k_ref[_chunk_rows(n), _head_cols(h)] for n, h in probs],
                [v_ref[_chunk_rows(n), _head_cols(h)] for n, h in probs],
                [_lane_col(gcb[e], h) for e in range(group) for h in range(DN_HEADS)],
                [grb[e][h:h + 1, :] for e in range(group) for h in range(DN_HEADS)],
                [_lane_col(bb[e], h) for e in range(group) for h in range(DN_HEADS)], None)
            for (n, h), (w, u, qd, kd, qk, egl), inv in zip(probs, outs, invs):
                rows, cols = _chunk_rows(n), _head_cols(h)
                w_ref[rows, cols] = w.astype(BF16)
                u_ref[rows, cols] = u
                qd_ref[rows, cols] = qd.astype(BF16)
                kd_ref[rows, cols] = kd.astype(BF16)
                qk_ref[n, h] = qk
                inv_ref[n, h] = inv
                egl_ref[n, h:h + 1, :] = egl
            return carry

        lax.fori_loop(0, cpb // group, step, 0)

    row = lambda i: (i, 0)
    tok = pl.BlockSpec((tm, DN_WIDTH), row)
    lanes = pl.BlockSpec((tm, LANES), row)
    sq = pl.BlockSpec((cpb, DN_HEADS, DN_CHUNK, DN_CHUNK), lambda i: (i, 0, 0, 0))
    return pl.pallas_call(
        body, name=name, grid=(t // tm,),
        in_specs=[tok, tok, tok, lanes, pl.BlockSpec((cpb, HALO, DN_CHUNK), lambda i: (i, 0, 0)), lanes],
        out_specs=(tok, tok, tok, tok, sq, pl.BlockSpec((cpb, HALO, LANES), lambda i: (i, 0, 0)), sq),
        out_shape=(jax.ShapeDtypeStruct((t, DN_WIDTH), BF16), jax.ShapeDtypeStruct((t, DN_WIDTH), F32),
                   jax.ShapeDtypeStruct((t, DN_WIDTH), BF16), jax.ShapeDtypeStruct((t, DN_WIDTH), BF16),
                   jax.ShapeDtypeStruct((n_chunks, DN_HEADS, DN_CHUNK, DN_CHUNK), F32),
                   jax.ShapeDtypeStruct((n_chunks, HALO, LANES), F32),
                   jax.ShapeDtypeStruct((n_chunks, DN_HEADS, DN_CHUNK, DN_CHUNK), F32)),
        compiler_params=_cparams(("parallel",)),
    )(q, k, v, gc, grow, beta)


def _delta_par_bwd(q, k, v, gc, grow, beta, inv, dw, du, dqd, dkd, dqk, degl, *, name):
    t = q.shape[0]
    tm = _tm(t)
    cpb = tm // DN_CHUNK
    n_chunks = t // DN_CHUNK
    group = 4 if cpb % 4 == 0 else 2

    def body(q_ref, k_ref, v_ref, gc_ref, gr_ref, b_ref, inv_ref, dw_ref, du_ref, dqd_ref, dkd_ref, dqk_ref, degl_ref,
             dq_ref, dk_ref, dv_ref, dgc_ref, dgr_ref, db_ref):
        def step(m, carry):
            chunks = [m * group + e for e in range(group)]
            probs = [(e, h) for e in range(group) for h in range(DN_HEADS)]
            rows = [_chunk_rows(n) for n in chunks]
            gcb, bb = [gc_ref[r, :] for r in rows], [b_ref[r, :] for r in rows]
            grb, deglb = [gr_ref[n] for n in chunks], [degl_ref[n] for n in chunks]
            for n in chunks:
                dgr_ref[n] = jnp.zeros((HALO, DN_CHUNK), F32)
            invs = [inv_ref[chunks[e], h] for e, h in probs]
            _, vjp = jax.vjp(lambda *a: _prep_fn(*a, invs)[0],
                             [q_ref[rows[e], _head_cols(h)] for e, h in probs], [k_ref[rows[e], _head_cols(h)] for e, h in probs],
                             [v_ref[rows[e], _head_cols(h)] for e, h in probs], [_lane_col(gcb[e], h) for e, h in probs],
                             [grb[e][h:h + 1, :] for e, h in probs], [_lane_col(bb[e], h) for e, h in probs])
            dq, dk, dv, dgc, dgr, db = vjp([(dw_ref[rows[e], _head_cols(h)], du_ref[rows[e], _head_cols(h)],
                                             dqd_ref[rows[e], _head_cols(h)], dkd_ref[rows[e], _head_cols(h)],
                                             dqk_ref[chunks[e], h], deglb[e][h:h + 1, :]) for e, h in probs])
            dgc_acc = [jnp.zeros((DN_CHUNK, LANES), F32) for _ in chunks]
            db_acc = [jnp.zeros((DN_CHUNK, LANES), F32) for _ in chunks]
            for i, (e, h) in enumerate(probs):
                cols = _head_cols(h)
                dq_ref[rows[e], cols] = dq[i]
                dk_ref[rows[e], cols] = dk[i]
                dv_ref[rows[e], cols] = dv[i]
                dgr_ref[chunks[e], h:h + 1, :] = dgr[i]
                dgc_acc[e] = dgc_acc[e] + _col_lane(dgc[i], h)
                db_acc[e] = db_acc[e] + _col_lane(db[i], h)
            for e in range(group):
                dgc_ref[rows[e], :] = dgc_acc[e]
                db_ref[rows[e], :] = db_acc[e]
            return carry

        lax.fori_loop(0, cpb // group, step, 0)

    row = lambda i: (i, 0)
    tok = pl.BlockSpec((tm, DN_WIDTH), row)
    lanes = pl.BlockSpec((tm, LANES), row)
    sq = pl.BlockSpec((cpb, DN_HEADS, DN_CHUNK, DN_CHUNK), lambda i: (i, 0, 0, 0))
    grs = pl.BlockSpec((cpb, HALO, DN_CHUNK), lambda i: (i, 0, 0))
    return pl.pallas_call(
        body, name=name, grid=(t // tm,),
        in_specs=[tok, tok, tok, lanes, grs, lanes, sq, tok, tok, tok, tok, sq, pl.BlockSpec((cpb, HALO, LANES), lambda i: (i, 0, 0))],
        out_specs=(tok, tok, tok, lanes, grs, lanes),
        out_shape=(jax.ShapeDtypeStruct((t, DN_WIDTH), F32),) * 3
        + (jax.ShapeDtypeStruct((t, LANES), F32), jax.ShapeDtypeStruct((n_chunks, HALO, DN_CHUNK), F32),
           jax.ShapeDtypeStruct((t, LANES), F32)),
        compiler_params=_cparams(("parallel",)),
    )(q, k, v, gc, grow, beta, inv, dw, du, dqd, dkd, dqk, degl)


def _seq_specs(n_seq, seq, reverse):
    tm = _tm(seq)
    nb = seq // tm
    cpb = tm // DN_CHUNK
    pair = 2 if n_seq % 2 == 0 else 1
    blk = (lambda j: nb - 1 - j) if reverse else (lambda j: j)
    tok = pl.BlockSpec((pair, tm, DN_WIDTH), lambda b, j: (b, blk(j), 0))
    sq = pl.BlockSpec((pair, cpb, DN_HEADS, DN_CHUNK, DN_CHUNK), lambda b, j: (b, blk(j), 0, 0, 0))
    rows8 = pl.BlockSpec((pair, cpb, HALO, LANES), lambda b, j: (b, blk(j), 0, 0))
    state = pl.BlockSpec((pair, cpb, DN_HEADS, DN_HEAD_DIM, DN_HEAD_DIM), lambda b, j: (b, blk(j), 0, 0, 0))
    return nb, cpb, pair, tok, sq, rows8, state


def _by_seq(a, n_seq):
    return a.reshape((n_seq, a.shape[0] // n_seq) + a.shape[1:])


def _flat_seq(a):
    return a.reshape((a.shape[0] * a.shape[1],) + a.shape[2:])


def _delta_seq_fwd(w, u, qd, kd, qk, egl, n_seq, seq, *, name):
    nb, cpb, pair, tok, sq, rows8, state = _seq_specs(n_seq, seq, False)
    probs = [(e, h) for e in range(pair) for h in range(DN_HEADS)]

    def body(w_ref, u_ref, qd_ref, kd_ref, qk_ref, egl_ref, o_ref, st_ref, s_s):
        @pl.when(pl.program_id(1) == 0)
        def _():
            s_s[...] = jnp.zeros_like(s_s)

        def step(n, carry):
            rows = _chunk_rows(n)
            eglb = [egl_ref[e, n] for e in range(pair)]
            s = [s_s[e, h] for e, h in probs]
            for (e, h), s_eh in zip(probs, s):
                st_ref[e, n, h] = s_eh
            o, s_new = _seq_fn([w_ref[e, rows, _head_cols(h)] for e, h in probs], [u_ref[e, rows, _head_cols(h)] for e, h in probs],
                               [qd_ref[e, rows, _head_cols(h)] for e, h in probs], [kd_ref[e, rows, _head_cols(h)] for e, h in probs],
                               [qk_ref[e, n, h] for e, h in probs], [eglb[e][h:h + 1, :] for e, h in probs], s)
            for i, (e, h) in enumerate(probs):
                o_ref[e, rows, _head_cols(h)] = o[i]
                s_s[e, h] = s_new[i]
            return carry

        lax.fori_loop(0, cpb, step, 0)

    o, states = pl.pallas_call(
        body, name=name, grid=(n_seq // pair, nb),
        in_specs=[tok, tok, tok, tok, sq, rows8],
        out_specs=(tok, state),
        out_shape=(jax.ShapeDtypeStruct((n_seq, seq, DN_WIDTH), F32),
                   jax.ShapeDtypeStruct((n_seq, seq // DN_CHUNK, DN_HEADS, DN_HEAD_DIM, DN_HEAD_DIM), F32)),
        scratch_shapes=[pltpu.VMEM((pair, DN_HEADS, DN_HEAD_DIM, DN_HEAD_DIM), F32)],
        compiler_params=_cparams(("parallel", "arbitrary")),
    )(*[_by_seq(a, n_seq) for a in (w, u, qd, kd, qk, egl)])
    return _flat_seq(o), _flat_seq(states)


def _delta_seq_bwd(w, u, qd, kd, qk, egl, states, do, n_seq, seq, *, name):
    nb, cpb, pair, tok, sq, rows8, state = _seq_specs(n_seq, seq, True)
    probs = [(e, h) for e in range(pair) for h in range(DN_HEADS)]

    def body(w_ref, u_ref, qd_ref, kd_ref, qk_ref, egl_ref, st_ref, do_ref, dw_ref, du_ref, dqd_ref, dkd_ref, dqk_ref,
             degl_ref, ds_s):
        @pl.when(pl.program_id(1) == 0)
        def _():
            ds_s[...] = jnp.zeros_like(ds_s)

        def step(m, carry):
            n = cpb - 1 - m
            rows = _chunk_rows(n)
            eglb = [egl_ref[e, n] for e in range(pair)]
            for e in range(pair):
                degl_ref[e, n] = jnp.zeros((HALO, LANES), F32)
            _, vjp = jax.vjp(_seq_fn, [w_ref[e, rows, _head_cols(h)].astype(F32) for e, h in probs],
                             [u_ref[e, rows, _head_cols(h)] for e, h in probs],
                             [qd_ref[e, rows, _head_cols(h)].astype(F32) for e, h in probs],
                             [kd_ref[e, rows, _head_cols(h)].astype(F32) for e, h in probs],
                             [qk_ref[e, n, h] for e, h in probs], [eglb[e][h:h + 1, :] for e, h in probs],
                             [st_ref[e, n, h] for e, h in probs])
            dw, du, dqd, dkd, dqk, degl, ds_in = vjp(([do_ref[e, rows, _head_cols(h)] for e, h in probs],
                                                      [ds_s[e, h] for e, h in probs]))
            for i, (e, h) in enumerate(probs):
                cols = _head_cols(h)
                dw_ref[e, rows, cols] = dw[i]
                du_ref[e, rows, cols] = du[i]
                dqd_ref[e, rows, cols] = dqd[i]
                dkd_ref[e, rows, cols] = dkd[i]
                dqk_ref[e, n, h] = dqk[i]
                degl_ref[e, n, h:h + 1, :] = degl[i]
                ds_s[e, h] = ds_in[i]
            return carry

        lax.fori_loop(0, cpb, step, 0)

    nc = seq // DN_CHUNK
    outs = pl.pallas_call(
        body, name=name, grid=(n_seq // pair, nb),
        in_specs=[tok, tok, tok, tok, sq, rows8, state, tok],
        out_specs=(tok, tok, tok, tok, sq, rows8),
        out_shape=(jax.ShapeDtypeStruct((n_seq, seq, DN_WIDTH), F32),) * 4
        + (jax.ShapeDtypeStruct((n_seq, nc, DN_HEADS, DN_CHUNK, DN_CHUNK), F32),
           jax.ShapeDtypeStruct((n_seq, nc, HALO, LANES), F32)),
        scratch_shapes=[pltpu.VMEM((pair, DN_HEADS, DN_HEAD_DIM, DN_HEAD_DIM), F32)],
        compiler_params=_cparams(("parallel", "arbitrary")),
    )(*[_by_seq(a, n_seq) for a in (w, u, qd, kd, qk, egl, states, do)])
    return tuple(_flat_seq(a) for a in outs)


def _dn_gate(o, z, dnw):
    return o * lax.rsqrt(jnp.mean(o * o, axis=-1, keepdims=True) + EPS) * dnw * _silu(z)


def _mix_out_fwd(x, sg, o, z, wo_sg, wo_dn, dnw, *, name):
    t = x.shape[0]
    tm = _tm(t)

    def body(x_ref, sg_ref, o_ref, z_ref, wsg_ref, wdn_ref, dnw_ref, y_ref, dn_s):
        for h, (oh, zh) in enumerate(zip(_split_heads(o_ref, 0), _split_heads(z_ref, 0))):
            dn_s[:, h * DN_HEAD_DIM:(h + 1) * DN_HEAD_DIM] = _dn_gate(oh, zh, dnw_ref[...]).astype(BF16)
        y_ref[...] = (x_ref[...] + jnp.dot(sg_ref[...].astype(BF16), wsg_ref[...], preferred_element_type=F32)
                      + jnp.dot(dn_s[...], wdn_ref[...], preferred_element_type=F32))

    row = lambda i: (i, 0)
    const = lambda i: (0, 0)
    half = pl.BlockSpec((tm, DN_WIDTH), row)
    return pl.pallas_call(
        body, name=name, grid=(t // tm,),
        in_specs=[pl.BlockSpec((tm, D_MODEL), row), half, half, half, pl.BlockSpec((SG_WIDTH, D_MODEL), const),
                  pl.BlockSpec((DN_WIDTH, D_MODEL), const), pl.BlockSpec((1, DN_HEAD_DIM), const)],
        out_specs=pl.BlockSpec((tm, D_MODEL), row),
        out_shape=jax.ShapeDtypeStruct((t, D_MODEL), F32),
        scratch_shapes=[pltpu.VMEM((tm, DN_WIDTH), BF16)],
        compiler_params=_cparams(("parallel",)),
    )(x, sg, o, z, wo_sg, wo_dn, dnw)


def _mix_out_bwd(dy, sg, o, z, wo_sg, wo_dn, dnw, *, name):
    t = dy.shape[0]
    tm = _tm(t)

    def body(dy_ref, sg_ref, o_ref, z_ref, wsg_ref, wdn_ref, dnw_ref, dsg_ref, do_ref, dz_ref, dwsg_ref, dwdn_ref, ddnw_ref, dn_s):
        i = pl.program_id(0)
        dyb = dy_ref[...].astype(BF16)
        nt = (((1,), (1,)), ((), ()))
        tn = (((0,), (0,)), ((), ()))
        dsg_ref[...] = lax.dot_general(dyb, wsg_ref[...], nt, preferred_element_type=F32)
        ddn = lax.dot_general(dyb, wdn_ref[...], nt, preferred_element_type=F32)
        ddnw = None
        for h, (oh, zh) in enumerate(zip(_split_heads(o_ref, 0), _split_heads(z_ref, 0))):
            cols = slice(h * DN_HEAD_DIM, (h + 1) * DN_HEAD_DIM)
            out, vjp = jax.vjp(_dn_gate, oh, zh, dnw_ref[...])
            dn_s[:, cols] = out.astype(BF16)
            doh, dzh, dw = vjp(ddn[:, cols])
            do_ref[:, cols] = doh
            dz_ref[:, cols] = dzh.astype(BF16)
            ddnw = dw if ddnw is None else ddnw + dw
        _acc_out(ddnw_ref, i == 0, ddnw)
        _acc_out(dwsg_ref, i == 0, lax.dot_general(sg_ref[...].astype(BF16), dyb, tn, preferred_element_type=F32))
        _acc_out(dwdn_ref, i == 0, lax.dot_general(dn_s[...], dyb, tn, preferred_element_type=F32))

    row = lambda i: (i, 0)
    const = lambda i: (0, 0)
    half = pl.BlockSpec((tm, DN_WIDTH), row)
    wspec = pl.BlockSpec((DN_WIDTH, D_MODEL), const)
    return pl.pallas_call(
        body, name=name, grid=(t // tm,),
        in_specs=[pl.BlockSpec((tm, D_MODEL), row), half, half, half, wspec, wspec, pl.BlockSpec((1, DN_HEAD_DIM), const)],
        out_specs=(half, half, half, wspec, wspec, pl.BlockSpec((1, DN_HEAD_DIM), const)),
        out_shape=(jax.ShapeDtypeStruct((t, DN_WIDTH), F32),) * 2 + (jax.ShapeDtypeStruct((t, DN_WIDTH), BF16),)
        + (jax.ShapeDtypeStruct((DN_WIDTH, D_MODEL), F32),) * 2 + (jax.ShapeDtypeStruct((1, DN_HEAD_DIM), F32),),
        scratch_shapes=[pltpu.VMEM((tm, DN_WIDTH), BF16)],
        compiler_params=_cparams(("arbitrary",)),
    )(dy, sg, o, z, wo_sg, wo_dn, dnw)


_MESH = pl.DeviceIdType.MESH
_HBM = pl.BlockSpec(memory_space=pl.ANY)


def _mesh_pos():
    x, y, c = lax.axis_index("x"), lax.axis_index("y"), lax.axis_index("c")
    return x, y, c, [(1 - x, y), (x, 1 - y), (1 - x, 1 - y)]


def _gather2(arrs, *, name):
    n = len(arrs)
    slots = N_DEV - 1

    def body(*refs):
        in_refs, out_refs = refs[:n], refs[n:2 * n]
        send_sems, recv_sems, local_sems = refs[2 * n:]
        x, y, c, chips = _mesh_pos()
        me, sibling = (x, y, c), (x, y, 1 - c)

        def copy(k, slot, block, to, src=None):
            dst = out_refs[k].at[4 * block[0] + 2 * block[1] + block[2]]
            return pltpu.make_async_remote_copy(src_ref=dst if src is None else src, dst_ref=dst,
                                                send_sem=send_sems.at[k * slots + slot], recv_sem=recv_sems.at[k * slots + slot],
                                                device_id=to, device_id_type=_MESH)

        local = [pltpu.make_async_copy(in_refs[k], out_refs[k].at[4 * x + 2 * y + c], local_sems.at[k]) for k in range(n)]
        sent = []
        for k in range(n):
            sent.append(copy(k, 0, me, sibling, src=in_refs[k]))
            sent += [copy(k, 1 + j, me, (*chip, c), src=in_refs[k]) for j, chip in enumerate(chips)]
        for cp in local + sent:
            cp.start()
        for j, chip in enumerate(chips):
            for k in range(n):
                copy(k, 1 + j, (*chip, c), me).wait_recv()
                passed = copy(k, 4 + j, (*chip, c), sibling)
                passed.start()
                sent.append(passed)
        for k in range(n):
            copy(k, 0, sibling, me).wait_recv()
            for j, chip in enumerate(chips):
                copy(k, 4 + j, (*chip, 1 - c), me).wait_recv()
        for cp in sent:
            cp.wait_send()
        for cp in local:
            cp.wait()

    return pl.pallas_call(
        body, name=name, in_specs=[_HBM] * n, out_specs=(_HBM,) * n,
        out_shape=tuple(jax.ShapeDtypeStruct((N_DEV,) + a.shape, a.dtype) for a in arrs),
        scratch_shapes=[pltpu.SemaphoreType.DMA((n * slots,)), pltpu.SemaphoreType.DMA((n * slots,)),
                        pltpu.SemaphoreType.DMA((n,))],
    )(*arrs)


_SEM = pl.BlockSpec(memory_space=pltpu.SEMAPHORE)
_EFFECT = pltpu.SideEffectType.DATAFLOW_SIDE_EFFECTING


def _direct_copies(src_refs, land_refs, send_sems, recv_sems, gather):
    x, y, c, _ = _mesh_pos()
    me = 4 * x + 2 * y + c
    n_peer = N_DEV - 1
    copies = []
    for r in range(1, N_DEV):
        px = 1 - x if r & 4 else x
        py = 1 - y if r & 2 else y
        pc = 1 - c if r & 1 else c
        for k, (src, land) in enumerate(zip(src_refs, land_refs)):
            copies.append(pltpu.make_async_remote_copy(
                src_ref=src if gather else src.at[4 * px + 2 * py + pc], dst_ref=land.at[me],
                send_sem=send_sems.at[k * n_peer + r - 1], recv_sem=recv_sems.at[k * n_peer + r - 1],
                device_id=(px, py, pc), device_id_type=_MESH))
    return copies


def _send_start(arrs, gather, after=None, *, name):
    n = len(arrs)
    lands = [lax.empty(((N_DEV,) + a.shape) if gather else a.shape, a.dtype) for a in arrs]
    n_in = 2 * n + (0 if after is None else 1)

    def body(*refs):
        src_refs, land_refs, send_sems, recv_sems, token = refs[:n], refs[n:2 * n], refs[n_in], refs[n_in + 1], refs[-1]
        for cp in _direct_copies(src_refs, land_refs, send_sems, recv_sems, gather):
            cp.start()
        token[...] = jnp.zeros_like(token)

    n_sem = n * (N_DEV - 1)
    bufs = list(arrs) + lands
    out = pl.pallas_call(
        body, name=name,
        out_shape=(pltpu.SemaphoreType.DMA((n_sem,)), pltpu.SemaphoreType.DMA((n_sem,)))
        + tuple(pltpu.HBM(b.shape, b.dtype) for b in bufs) + (jax.ShapeDtypeStruct((HALO, LANES), F32),),
        in_specs=[_HBM] * n_in, out_specs=(_SEM, _SEM) + (_HBM,) * (2 * n) + (pl.BlockSpec(memory_space=pltpu.VMEM),),
        input_output_aliases={i: 2 + i for i in range(2 * n)},
        compiler_params=pltpu.CompilerParams(has_side_effects=_EFFECT),
    )(*[pltpu.with_memory_space_constraint(b, pltpu.HBM) for b in bufs], *([] if after is None else [after]))
    return (out[0], out[1], list(out[2:2 + n]), list(out[2 + n:2 + 2 * n])), out[-1]


def _send_wait(started, gather, after, *, name):
    send_sems, recv_sems, srcs, lands = started
    n = len(srcs)

    def body(*refs):
        src_refs, land_refs, send_ref, recv_ref = refs[:n], refs[n:2 * n], refs[2 * n], refs[2 * n + 1]
        for cp in _direct_copies(src_refs, land_refs, send_ref, recv_ref, gather):
            cp.wait_send()
            cp.wait_recv()

    bufs = srcs + lands
    out = pl.pallas_call(
        body, name=name, out_shape=tuple(pltpu.HBM(b.shape, b.dtype) for b in bufs),
        in_specs=[_HBM] * (2 * n) + [_SEM, _SEM, _HBM], out_specs=(_HBM,) * (2 * n),
        input_output_aliases={i: i for i in range(2 * n)},
        compiler_params=pltpu.CompilerParams(has_side_effects=_EFFECT),
    )(*bufs, send_sems, recv_sems, after)
    return list(out[:n]), list(out[n:])


def _row_block(rows, limit=256):
    best = rows
    for cand in range(8, limit + 1, 8):
        if rows % cand == 0:
            best = cand
    return best if rows > limit else rows


def _adam(gp, w, m, v, *, name):
    p, rows, cols = gp.shape
    rb = _row_block(rows)

    def body(gp_ref, w_ref, m_ref, v_ref, g_ref, d_ref, m2_ref, v2_ref):
        g = gp_ref[0].astype(F32)
        for s in range(1, p):
            g = g + gp_ref[s].astype(F32)
        m2 = ADAM_B1 * m_ref[...] + (1.0 - ADAM_B1) * g
        v2 = ADAM_B2 * v_ref[...] + (1.0 - ADAM_B2) * (g * g)
        m_hat = m2 / (1.0 - ADAM_B1 ** ADAM_STEP)
        v_hat = v2 / (1.0 - ADAM_B2 ** ADAM_STEP)
        g_ref[...] = g
        d_ref[...] = -ADAM_LR * (m_hat / (jnp.sqrt(v_hat) + ADAM_EPS) + ADAM_WD * w_ref[...])
        m2_ref[...] = m2
        v2_ref[...] = v2

    blk = pl.BlockSpec((rb, cols), lambda i: (i, 0))
    return pl.pallas_call(
        body, name=name, grid=(rows // rb,),
        in_specs=[pl.BlockSpec((p, rb, cols), lambda i: (0, i, 0)), blk, blk, blk],
        out_specs=(blk,) * 4, out_shape=(jax.ShapeDtypeStruct((rows, cols), F32),) * 4,
        compiler_params=_cparams(("parallel",)),
    )(gp, w, m, v)


def _cols_full(g):
    return jnp.transpose(g, (1, 0, 2)).reshape(g.shape[1], N_DEV * g.shape[2])


def _pad_lanes(a, width=LANES):
    return jnp.pad(a, ((0, 0), (0, width - a.shape[1])))


def _chunk_rows_of(a):
    by_chunk = jnp.transpose(a[:, :DN_HEADS].reshape(-1, DN_CHUNK, DN_HEADS), (0, 2, 1))
    return jnp.pad(by_chunk, ((0, 0), (0, HALO - DN_HEADS), (0, 0)))


_SMALL = (("ffn1_norm", D_MODEL), ("mix_norm", D_MODEL), ("ffn2_norm", D_MODEL), ("final_norm", D_MODEL), ("a_log", DN_HEADS),
          ("dt_bias", DN_HEADS), ("dn_norm", DN_HEAD_DIM), ("sg_ln_g", SG_WIDTH), ("sg_ln_b", SG_WIDTH),
          ("sg_w", SG_GROUPS * SG_CHUNK * SG_CHUNK), ("sg_b", SG_GROUPS * SG_CHUNK), ("conv_w", CONV_K * 3 * DN_WIDTH))
_SMALL_ROWS = 1128
_SMALL_SHAPES = {"ffn1_norm": (1, D_MODEL), "mix_norm": (1, D_MODEL), "ffn2_norm": (1, D_MODEL), "final_norm": (D_MODEL,),
                 "a_log": (1, DN_HEADS), "dt_bias": (1, DN_HEADS), "dn_norm": (1, DN_HEAD_DIM), "sg_ln_g": (1, SG_WIDTH),
                 "sg_ln_b": (1, SG_WIDTH), "sg_w": (1, SG_GROUPS, SG_CHUNK, SG_CHUNK), "sg_b": (1, SG_GROUPS, SG_CHUNK)}


def _pack_small(d):
    flat = jnp.concatenate([d[name].reshape(-1) for name, _ in _SMALL])
    return jnp.pad(flat, (0, _SMALL_ROWS * LANES - flat.shape[0])).reshape(_SMALL_ROWS, LANES)


def _unpack_small(a):
    flat, out, at = a.reshape(-1), {}, 0
    for name, size in _SMALL:
        out[name] = flat[at:at + size]
        at += size
    return out


def kernel(x, ffn1_norm, ffn1_w_gate, ffn1_w_up, ffn1_w_down, mix_norm, w_in, conv_w, a_log, dt_bias, dn_norm, sg_ln_g, sg_ln_b, sg_w, sg_b, w_out, ffn2_norm, ffn2_w_gate, ffn2_w_up, ffn2_w_down, final_norm, loss_target, m_ffn1_norm, m_ffn1_w_gate, m_ffn1_w_up, m_ffn1_w_down, m_mix_norm, m_w_in, m_conv_w, m_a_log, m_dt_bias, m_dn_norm, m_sg_ln_g, m_sg_ln_b, m_sg_w, m_sg_b, m_w_out, m_ffn2_norm, m_ffn2_w_gate, m_ffn2_w_up, m_ffn2_w_down, m_final_norm, v_ffn1_norm, v_ffn1_w_gate, v_ffn1_w_up, v_ffn1_w_down, v_mix_norm, v_w_in, v_conv_w, v_a_log, v_dt_bias, v_dn_norm, v_sg_ln_g, v_sg_ln_b, v_sg_w, v_sg_b, v_w_out, v_ffn2_norm, v_ffn2_w_gate, v_ffn2_w_up, v_ffn2_w_down, v_final_norm):
    weights = dict(ffn1_norm=ffn1_norm, ffn1_w_gate=ffn1_w_gate, ffn1_w_up=ffn1_w_up, ffn1_w_down=ffn1_w_down, mix_norm=mix_norm, w_in=w_in, conv_w=conv_w, a_log=a_log, dt_bias=dt_bias, dn_norm=dn_norm, sg_ln_g=sg_ln_g, sg_ln_b=sg_ln_b, sg_w=sg_w, sg_b=sg_b, w_out=w_out, ffn2_norm=ffn2_norm, ffn2_w_gate=ffn2_w_gate, ffn2_w_up=ffn2_w_up, ffn2_w_down=ffn2_w_down, final_norm=final_norm)
    mom_m = dict(ffn1_norm=m_ffn1_norm, ffn1_w_gate=m_ffn1_w_gate, ffn1_w_up=m_ffn1_w_up, ffn1_w_down=m_ffn1_w_down, mix_norm=m_mix_norm, w_in=m_w_in, conv_w=m_conv_w, a_log=m_a_log, dt_bias=m_dt_bias, dn_norm=m_dn_norm, sg_ln_g=m_sg_ln_g, sg_ln_b=m_sg_ln_b, sg_w=m_sg_w, sg_b=m_sg_b, w_out=m_w_out, ffn2_norm=m_ffn2_norm, ffn2_w_gate=m_ffn2_w_gate, ffn2_w_up=m_ffn2_w_up, ffn2_w_down=m_ffn2_w_down, final_norm=m_final_norm)
    mom_v = dict(ffn1_norm=v_ffn1_norm, ffn1_w_gate=v_ffn1_w_gate, ffn1_w_up=v_ffn1_w_up, ffn1_w_down=v_ffn1_w_down, mix_norm=v_mix_norm, w_in=v_w_in, conv_w=v_conv_w, a_log=v_a_log, dt_bias=v_dt_bias, dn_norm=v_dn_norm, sg_ln_g=v_sg_ln_g, sg_ln_b=v_sg_ln_b, sg_w=v_sg_w, sg_b=v_sg_b, w_out=v_w_out, ffn2_norm=v_ffn2_norm, ffn2_w_gate=v_ffn2_w_gate, ffn2_w_up=v_ffn2_w_up, ffn2_w_down=v_ffn2_w_down, final_norm=v_final_norm)
    order = list(weights)
    big = ("ffn1_w_gate", "ffn1_w_up", "ffn1_w_down", "w_in", "w_out", "ffn2_w_gate", "ffn2_w_up", "ffn2_w_down")
    col_sharded = ("ffn1_w_gate", "ffn1_w_up", "w_in", "ffn2_w_gate", "ffn2_w_up")

    n_seq, seq, _ = x.shape
    t = n_seq * seq
    me = 4 * lax.axis_index("x") + 2 * lax.axis_index("y") + lax.axis_index("c")
    x0 = x.reshape(t, D_MODEL)
    tgt = loss_target.reshape(t, D_MODEL)

    def fill_own(land, own_block):
        return lax.dynamic_update_index_in_dim(land, own_block, me, 0)

    def rows_view(n, a):
        return jnp.transpose(a) if n in col_sharded else a

    def as_full(n, g):
        return g.reshape(-1, g.shape[-1])

    shards = {n: rows_view(n, weights[n][0]).astype(BF16) for n in big}
    ffn1_names, mix_names, ffn2_names = big[:3], big[3:5], big[5:]
    full = {n: as_full(n, g) for n, g in zip(ffn1_names, _gather2([shards[n] for n in ffn1_names], name="gather_ffn1"))}
    mix_srcs = [shards[n] for n in mix_names] + [conv_w[0]]
    mix_started, mix_token = _send_start(mix_srcs, True, full[ffn1_names[2]], name="gather_mix_start")
    ffn2_started, ffn2_token = _send_start([shards[n] for n in ffn2_names], True, mix_token, name="gather_ffn2_start")
    ffn1_norm_fwd = ffn1_norm + ffn2_token[:1, :1]
    alog, dtb = _pad_lanes(a_log), _pad_lanes(dt_bias)
    sgbt = _pad_lanes(sg_b[0].T)
    fnw = final_norm.reshape(1, D_MODEL)

    x1, h1, g1, u1 = _ffn_fwd(x0, ffn1_norm_fwd, full["ffn1_w_gate"], full["ffn1_w_up"], full["ffn1_w_down"], name="ffn1_fwd")
    mix_lands = [fill_own(land, src) for src, land in zip(*_send_wait(mix_started, True, x1, name="gather_mix_wait"))]
    full.update({n: as_full(n, g) for n, g in zip(mix_names, mix_lands)})
    conv_full = _cols_full(mix_lands[-1])
    w_in_t = full["w_in"]
    offs = (0, SG_WIDTH, 2 * SG_WIDTH, 2 * SG_WIDTH + 3 * DN_WIDTH, 2 * SG_WIDTH + 4 * DN_WIDTH)
    n_proj = offs[-1]

    def pad_rows(a):
        return jnp.pad(a, ((0, LANES - a.shape[0]), (0, 0)))

    ws = [w_in_t[offs[0]:offs[1]], w_in_t[offs[1]:offs[2]], w_in_t[offs[2]:offs[3]], w_in_t[offs[3]:offs[4]],
          pad_rows(w_in_t[n_proj:n_proj + DN_HEADS]), pad_rows(w_in_t[n_proj + DN_HEADS:n_proj + 2 * DN_HEADS])]
    wo_sg, wo_dn = full["w_out"][:SG_WIDTH], full["w_out"][SG_WIDTH:]
    u, v, qkv, z, bpre, apre = _mix_in_fwd(x1, mix_norm, ws, name="mix_in_fwd")
    sg_out = _sg_fwd(u, v, sg_ln_g, sg_ln_b, sg_w[0], sgbt, name="sg_fwd")
    q, k, vv, beta, gc = _dn_prep_fwd(qkv, bpre, apre, conv_full, alog, dtb, seq, name="dn_prep_fwd")
    grow = _chunk_rows_of(gc)
    wy_w, wy_u, q_dec, k_dec, qk, egl, inv = _delta_prep(q, k, vv, gc, grow, beta, name="delta_prep")
    o, states = _delta_seq_fwd(wy_w, wy_u, q_dec, k_dec, qk, egl, n_seq, seq, name="delta_seq_fwd")
    x2 = _mix_out_fwd(x1, sg_out, o, z, wo_sg, wo_dn, dn_norm, name="mix_out_fwd")
    ffn2_srcs, ffn2_lands = _send_wait(ffn2_started, True, x2, name="gather_ffn2_wait")
    full.update({n: as_full(n, fill_own(land, src)) for n, src, land in zip(ffn2_names, ffn2_srcs, ffn2_lands)})
    dx3, loss_part, d_fn, h2, g2, u2 = _ffn_fwd(x2, ffn2_norm, full["ffn2_w_gate"], full["ffn2_w_up"], full["ffn2_w_down"],
                                                tgt, fnw, name="ffn2_fwd_loss")
    loss = lax.psum(loss_part[0, 0], ("x", "y", "c"))

    dx2, d_n2, d_g2, d_u2, d_d2 = _ffn_bwd(x2, ffn2_norm, h2, g2, u2, full["ffn2_w_gate"], full["ffn2_w_up"],
                                           full["ffn2_w_down"], dx3, name="ffn2_bwd")
    def by_owner(d_rows):
        return d_rows.reshape(N_DEV, -1, D_MODEL)

    ffn2_pieces = [by_owner(d_g2), by_owner(d_u2), by_owner(d_d2)]
    ffn2_sent, sent_token = _send_start(ffn2_pieces, False, name="grads_ffn2_start")
    dsg, do, dz, d_wo_sg, d_wo_dn, d_dnw = _mix_out_bwd(dx2, sg_out, o, z, wo_sg, wo_dn, dn_norm + sent_token[:1, :1],
                                                        name="mix_out_bwd")
    d_seq = _delta_seq_bwd(wy_w, wy_u, q_dec, k_dec, qk, egl, states, do, n_seq, seq, name="delta_seq_bwd")
    dq, dk, dv, dgc_a, dgrow, dbeta = _delta_par_bwd(q, k, vv, gc, grow, beta, inv, *d_seq, name="delta_par_bwd")
    dgc_b = _pad_lanes(jnp.transpose(dgrow[:, :DN_HEADS, :], (0, 2, 1)).reshape(t, DN_HEADS))
    dqkv, d_conv, dbpre, dapre, d_alog, d_dtb = _dn_prep_bwd(qkv, bpre, apre, conv_full, alog, dtb, dq, dk, dv, dbeta, dgc_a,
                                                             dgc_b, seq, name="dn_prep_bwd")
    du, dvv, d_lng, d_lnb, d_wc, d_sgbt = _sg_bwd(u, v, sg_ln_g, sg_ln_b, sg_w[0], sgbt, dsg, name="sg_bwd")
    dx1, d_mixn, d_wp = _mix_in_bwd(x1, mix_norm, ws, dx2, (du, dvv, dqkv, dz, dbpre, dapre), name="mix_in_bwd")
    d_w_in_t = jnp.concatenate([d_wp[:n_proj], d_wp[_PROJ_OFFSETS[4]:_PROJ_OFFSETS[4] + DN_HEADS],
                                d_wp[_PROJ_OFFSETS[5]:_PROJ_OFFSETS[5] + DN_HEADS]], axis=0)
    d_w_out = jnp.concatenate([d_wo_sg, d_wo_dn], axis=0)
    mix_pieces = [by_owner(d_w_in_t), by_owner(d_w_out).astype(BF16)]
    mix_sent, sent_token = _send_start(mix_pieces, False, name="grads_mix_start")
    grad_x, d_n1, dg1, du1, a1, dyh1 = _ffn_bwd_x(x0, ffn1_norm + sent_token[:1, :1], g1, u1, full["ffn1_w_gate"],
                                                  full["ffn1_w_up"], full["ffn1_w_down"], dx1, name="ffn1_bwd_x")
    small_grads = dict(ffn1_norm=d_n1, mix_norm=d_mixn, ffn2_norm=d_n2, final_norm=d_fn, a_log=d_alog[:, :DN_HEADS],
                       dt_bias=d_dtb[:, :DN_HEADS], dn_norm=d_dnw, sg_ln_g=d_lng, sg_ln_b=d_lnb, sg_w=d_wc,
                       sg_b=d_sgbt[:, :SG_GROUPS].T, conv_w=d_conv[:CONV_K])
    small_src = _pack_small(small_grads)
    small_sent, small_token = _send_start([small_src], True, name="small_grads_start")
    late, tokens = [], []

    def send_early(k, grad):
        piece = by_owner(grad)
        sent, token = _send_start([piece], False, name="grads_" + ffn1_names[k] + "_start")
        late.append(((ffn1_names[k],), sent))
        tokens.append(token)
        return token

    _ffn_wgrads(h1, dg1, du1, a1, dyh1, send_early, small_token, name="ffn1_bwd")

    res = {}
    after = tokens[-1]

    def update(names, sent, after):
        pieces, lands = _send_wait(sent, False, after, name="grads_" + names[0] + "_wait")
        for n, land, p in zip(names, lands, pieces):
            got = fill_own(land, lax.dynamic_index_in_dim(p, me, 0, keepdims=False))
            upd = _adam(got, *[rows_view(n, src[n][0]) for src in (weights, mom_m, mom_v)], name="adam_" + n)
            res[n] = [rows_view(n, a) for a in upd]
            after = upd[0]
        return after

    for group in [(ffn2_names, ffn2_sent), (mix_names, mix_sent)] + late[:-1]:
        after = update(*group, after)
    (small_src,), (small_land,) = _send_wait(small_sent, True, after, name="small_grads_wait")
    small_parts = fill_own(small_land, small_src)
    zeros_conv = jnp.zeros((CONV_K * 3 * DN_WIDTH,), F32)
    packed = [_pack_small({**{n: src[n] for n, _ in _SMALL if n != "conv_w"}, "conv_w": zeros_conv})
              for src in (weights, mom_m, mom_v)]
    small_upd = _adam(small_parts, *packed, name="adam_small")
    small_res = [_unpack_small(a) for a in small_upd]
    conv_grad = lax.dynamic_slice_in_dim(small_res[0]["conv_w"].reshape(CONV_K, 3 * DN_WIDTH), me * (3 * DN_WIDTH // N_DEV),
                                         3 * DN_WIDTH // N_DEV, axis=1)
    res["conv_w"] = _adam(conv_grad[None], conv_w[0], m_conv_w[0], v_conv_w[0], name="adam_conv_w")
    update(*late[-1], res["conv_w"][0])

    outs = [[], [], [], []]
    for n in order:
        for kind in range(4):
            if n in res:
                outs[kind].append(res[n][kind][None])
            else:
                outs[kind].append(small_res[kind][n].reshape(_SMALL_SHAPES[n]))
    return (loss, grad_x.reshape(x.shape), *outs[0], *outs[1], *outs[2], *outs[3])
```

```python
import jax
import jax.numpy as jnp
from jax import lax
from jax.experimental import pallas as pl
from jax.experimental.pallas import tpu as pltpu

F32 = jnp.float32
BF16 = jnp.bfloat16

D_MODEL = 1024
D_FF = 2816
SG_WIDTH = 512
SG_GROUPS = 8
SG_GROUP_DIM = 64
SG_CHUNK = 128
DN_WIDTH = 512
DN_HEAD_DIM = 128
DN_HEADS = 4
DN_CHUNK = 64
CONV_K = 4
EPS = 1e-6
N_DEV = 8
LANES = 128
HALO = 8
MXU_COLS = 256

ADAM_LR = 0.001
ADAM_B1 = 0.9
ADAM_B2 = 0.999
ADAM_EPS = 1e-08
ADAM_WD = 0.01
ADAM_STEP = 10

VMEM_LIMIT = 60 * 1024 * 1024
WGRAD_K_TILE = 2048
TOKEN_BLOCK = 512
FF_BLOCK_FWD = 1408

_HI = lax.Precision.HIGHEST


def _cparams(sem):
    return pltpu.CompilerParams(dimension_semantics=sem, vmem_limit_bytes=VMEM_LIMIT)


def _tm(t, pref=TOKEN_BLOCK):
    return min(pref, t)


def _dg(a, b, ca, cb, precision):
    if precision is not None:
        return lax.dot_general(a, b, (((ca,), (cb,)), ((), ())), precision=precision, preferred_element_type=F32)
    return lax.dot_general(a.astype(BF16), b.astype(BF16), (((ca,), (cb,)), ((), ())), preferred_element_type=F32)


def _make_mm(precision):
    @jax.custom_vjp
    def mm(a, b):
        return _dg(a, b, 1, 0, precision)

    @jax.custom_vjp
    def mm_nt(a, b):
        return _dg(a, b, 1, 1, precision)

    @jax.custom_vjp
    def mm_tn(a, b):
        return _dg(a, b, 0, 0, precision)

    mm.defvjp(lambda a, b: (mm(a, b), (a, b)), lambda r, g: (mm_nt(g, r[1]), mm_tn(r[0], g)))
    mm_nt.defvjp(lambda a, b: (mm_nt(a, b), (a, b)), lambda r, g: (mm(g, r[1]), mm_tn(g, r[0])))
    mm_tn.defvjp(lambda a, b: (mm_tn(a, b), (a, b)), lambda r, g: (mm_nt(r[1], g), mm(r[0], g)))
    return mm, mm_nt, mm_tn


mm, mm_nt, mm_tn = _make_mm(None)
mmx, mmx_nt, mmx_tn = _make_mm(_HI)
mmh, mmh_nt, mmh_tn = _make_mm(lax.Precision.HIGH)


def _sigmoid(x):
    return 1.0 / (1.0 + jnp.exp(-x))


def _silu(x):
    return x * _sigmoid(x)


def _softplus(x):
    neg_abs = jnp.where(x > 0, -x, x)
    return jnp.where(x > 0, x, 0.0) + jnp.log(1.0 + jnp.exp(neg_abs))


def _gelu(x):
    return 0.5 * x * (1.0 + jnp.tanh(0.7978845608028654 * (x + 0.044715 * (x * x * x))))


def _rms_fwd(x, g):
    r = lax.rsqrt(jnp.mean(x * x, axis=-1, keepdims=True) + EPS)
    xh = x * r
    return xh * g, xh, r


def _rms_bwd(dh, xh, r, g):
    dxh = dh * g
    dx = r * (dxh - xh * jnp.mean(dxh * xh, axis=-1, keepdims=True))
    return dx, jnp.sum(dh * xh, axis=0, keepdims=True)


def _acc_out(ref, first, val):
    @pl.when(first)
    def _():
        ref[...] = val

    @pl.when(jnp.logical_not(first))
    def _():
        ref[...] += val


def _ffn_fwd(x, nw, wg, wu, wd, tgt=None, fnw=None, *, name):
    t = x.shape[0]
    tm, fb = _tm(t), FF_BLOCK_FWD
    n_t, n_f = t // tm, D_FF // fb
    with_loss = tgt is not None

    def body(*refs):
        if with_loss:
            (x_ref, nw_ref, wg_ref, wu_ref, wd_ref, tgt_ref, fnw_ref, dy_ref, loss_ref, dfn_ref, h_ref, g_ref, u_ref,
             acc_s) = refs
        else:
            x_ref, nw_ref, wg_ref, wu_ref, wd_ref, y_ref, h_ref, g_ref, u_ref, acc_s = refs
        i, j = pl.program_id(0), pl.program_id(1)

        @pl.when(j == 0)
        def _():
            h, _, _ = _rms_fwd(x_ref[...], nw_ref[...])
            h_ref[...] = h.astype(BF16)
            acc_s[...] = jnp.zeros_like(acc_s)

        h = h_ref[...]
        nt = (((1,), (1,)), ((), ()))
        g = lax.dot_general(h, wg_ref[...], nt, preferred_element_type=F32)
        u = lax.dot_general(h, wu_ref[...], nt, preferred_element_type=F32)
        g_ref[...] = g.astype(BF16)
        u_ref[...] = u.astype(BF16)
        a = _silu(g) * u
        acc_s[...] += jnp.dot(a.astype(BF16), wd_ref[...], preferred_element_type=F32)

        @pl.when(j == n_f - 1)
        def _():
            y = x_ref[...] + 0.5 * acc_s[...]
            if not with_loss:
                y_ref[...] = y
            else:
                gf = fnw_ref[...]
                out, xh, r = _rms_fwd(y, gf)
                err = out - tgt_ref[...]
                part = 0.5 * jnp.sum(jnp.mean(err * err, axis=-1, keepdims=True), axis=0, keepdims=True)
                d_out = err * (1.0 / D_MODEL)
                dy, dgf = _rms_bwd(d_out, xh, r, gf)
                dy_ref[...] = dy
                _acc_out(loss_ref, i == 0, jnp.broadcast_to(part, loss_ref.shape))
                _acc_out(dfn_ref, i == 0, dgf)

    row = lambda i, j: (i, 0)
    const = lambda i, j: (0, 0)
    in_specs = [
        pl.BlockSpec((tm, D_MODEL), row),
        pl.BlockSpec((1, D_MODEL), const),
        pl.BlockSpec((fb, D_MODEL), lambda i, j: (j, 0)),
        pl.BlockSpec((fb, D_MODEL), lambda i, j: (j, 0)),
        pl.BlockSpec((fb, D_MODEL), lambda i, j: (j, 0)),
    ]
    args = [x, nw, wg, wu, wd]
    saved_shape = (jax.ShapeDtypeStruct((t, D_MODEL), BF16), jax.ShapeDtypeStruct((t, D_FF), BF16),
                   jax.ShapeDtypeStruct((t, D_FF), BF16))
    saved_specs = (pl.BlockSpec((tm, D_MODEL), row), pl.BlockSpec((tm, fb), lambda i, j: (i, j)),
                   pl.BlockSpec((tm, fb), lambda i, j: (i, j)))
    if with_loss:
        in_specs += [pl.BlockSpec((tm, D_MODEL), row), pl.BlockSpec((1, D_MODEL), const)]
        args += [tgt, fnw]
        out_shape = (jax.ShapeDtypeStruct((t, D_MODEL), F32), jax.ShapeDtypeStruct((8, LANES), F32),
                     jax.ShapeDtypeStruct((1, D_MODEL), F32)) + saved_shape
        out_specs = (pl.BlockSpec((tm, D_MODEL), row), pl.BlockSpec((8, LANES), const),
                     pl.BlockSpec((1, D_MODEL), const)) + saved_specs
        sem = ("arbitrary", "arbitrary")
    else:
        out_shape = (jax.ShapeDtypeStruct((t, D_MODEL), F32),) + saved_shape
        out_specs = (pl.BlockSpec((tm, D_MODEL), row),) + saved_specs
        sem = ("parallel", "arbitrary")
    return pl.pallas_call(
        body, name=name, grid=(n_t, n_f), in_specs=in_specs, out_specs=out_specs, out_shape=out_shape,
        scratch_shapes=[pltpu.VMEM((tm, D_MODEL), F32)],
        compiler_params=_cparams(sem),
    )(*args)


def _ffn_bwd_x(x, nw, g, u, wg, wu, wd, dy, *, name):
    t = x.shape[0]
    tm = _tm(t, 256)

    def body(x_ref, nw_ref, g_ref, u_ref, wg_ref, wu_ref, wd_ref, dy_ref, dx_ref, dnw_ref, dg_ref, du_ref, a_ref, dyh_ref):
        i = pl.program_id(0)
        nt = (((1,), (1,)), ((), ()))
        dy = dy_ref[...]
        dyh = (0.5 * dy).astype(BF16)
        dyh_ref[...] = dyh
        gate, up = g_ref[...].astype(F32), u_ref[...].astype(F32)
        s = _sigmoid(gate)
        gs = gate * s
        da = lax.dot_general(dyh, wd_ref[...], nt, preferred_element_type=F32)
        dg = (da * up * (s + gs * (1.0 - s))).astype(BF16)
        du = (da * gs).astype(BF16)
        dg_ref[...] = dg
        du_ref[...] = du
        a_ref[...] = (gs * up).astype(BF16)
        dh = (jnp.dot(dg, wg_ref[...], preferred_element_type=F32)
              + jnp.dot(du, wu_ref[...], preferred_element_type=F32))
        xv = x_ref[...]
        r = lax.rsqrt(jnp.mean(xv * xv, axis=-1, keepdims=True) + EPS)
        dx, dnw = _rms_bwd(dh, xv * r, r, nw_ref[...])
        dx_ref[...] = dy + dx
        _acc_out(dnw_ref, i == 0, dnw)

    row = lambda i: (i, 0)
    const = lambda i: (0, 0)
    once = pl.Buffered(1)
    wide = pl.BlockSpec((tm, D_FF), row)
    return pl.pallas_call(
        body, name=name, grid=(t // tm,),
        in_specs=[pl.BlockSpec((tm, D_MODEL), row), pl.BlockSpec((1, D_MODEL), const), wide, wide,
                  pl.BlockSpec((D_FF, D_MODEL), const, pipeline_mode=once), pl.BlockSpec((D_FF, D_MODEL), const, pipeline_mode=once),
                  pl.BlockSpec((D_FF, D_MODEL), const, pipeline_mode=once), pl.BlockSpec((tm, D_MODEL), row)],
        out_specs=(pl.BlockSpec((tm, D_MODEL), row), pl.BlockSpec((1, D_MODEL), const), wide, wide, wide,
                   pl.BlockSpec((tm, D_MODEL), row)),
        out_shape=(jax.ShapeDtypeStruct((t, D_MODEL), F32), jax.ShapeDtypeStruct((1, D_MODEL), F32),
                   jax.ShapeDtypeStruct((t, D_FF), BF16), jax.ShapeDtypeStruct((t, D_FF), BF16),
                   jax.ShapeDtypeStruct((t, D_FF), BF16), jax.ShapeDtypeStruct((t, D_MODEL), BF16)),
        compiler_params=_cparams(("arbitrary",)),
    )(x, nw, g, u, wg, wu, wd, dy)


def _wgrad(a, b, bm, bn, after=None, *, name):
    k, m = a.shape
    n = b.shape[1]
    tk = _tm(k, WGRAD_K_TILE)
    n_k = k // tk

    def body(a_ref, b_ref, *rest):
        o_ref, acc_s = rest[-2], rest[-1]
        s = pl.program_id(2)
        for c in range(bn // MXU_COLS):
            cols = slice(c * MXU_COLS, (c + 1) * MXU_COLS)
            part = lax.dot_general(a_ref[...], b_ref[:, cols], (((0,), (0,)), ((), ())), preferred_element_type=F32)
            acc_s[:, cols] = jnp.where(s == 0, 0.0, acc_s[:, cols]) + part

        @pl.when(s == n_k - 1)
        def _():
            o_ref[...] = acc_s[...].astype(BF16)

    return pl.pallas_call(
        body, name=name, grid=(m // bm, n // bn, n_k),
        in_specs=[pl.BlockSpec((tk, bm), lambda i, j, s: (s, i)), pl.BlockSpec((tk, bn), lambda i, j, s: (s, j))]
        + ([] if after is None else [_HBM]),
        out_specs=pl.BlockSpec((bm, bn), lambda i, j, s: (i, j)),
        out_shape=jax.ShapeDtypeStruct((m, n), BF16),
        scratch_shapes=[pltpu.VMEM((bm, bn), F32)],
        compiler_params=_cparams(("parallel", "parallel", "arbitrary")),
    )(a, b, *([] if after is None else [after]))


def _ffn_wgrads(h, dg, du, a, dyh, between=None, after=None, *, name):
    grads = []
    for k, (lhs, rhs, tag) in enumerate(((dg, h, "_wg"), (du, h, "_wu"), (a, dyh, "_wd"))):
        grads.append(_wgrad(lhs, rhs, D_FF // 2, D_MODEL, after, name=name + tag))
        after = None if between is None else between(k, grads[-1])
    return grads


def _ffn_bwd(x, nw, h, g, u, wg, wu, wd, dy, *, name):
    dx, dnw, dg, du, a, dyh = _ffn_bwd_x(x, nw, g, u, wg, wu, wd, dy, name=name + "_x")
    return (dx, dnw, *_ffn_wgrads(h, dg, du, a, dyh, name=name))


_PROJ_WIDTHS = (SG_WIDTH, SG_WIDTH, 3 * DN_WIDTH, DN_WIDTH, LANES, LANES)


def _mix_in_fwd(x, nw, ws, *, name):
    t = x.shape[0]
    tm = _tm(t)

    def body(x_ref, nw_ref, *refs):
        w_refs, o_refs = refs[:6], refs[6:]
        h, _, _ = _rms_fwd(x_ref[...], nw_ref[...])
        h = h.astype(BF16)
        for w_ref, o_ref in zip(w_refs, o_refs):
            o_ref[...] = lax.dot_general(h, w_ref[...], (((1,), (1,)), ((), ())), preferred_element_type=F32)

    row = lambda i: (i, 0)
    const = lambda i: (0, 0)
    return pl.pallas_call(
        body, name=name, grid=(t // tm,),
        in_specs=[pl.BlockSpec((tm, D_MODEL), row), pl.BlockSpec((1, D_MODEL), const)]
        + [pl.BlockSpec((n, D_MODEL), const) for n in _PROJ_WIDTHS],
        out_specs=tuple(pl.BlockSpec((tm, n), row) for n in _PROJ_WIDTHS),
        out_shape=tuple(jax.ShapeDtypeStruct((t, n), F32) for n in _PROJ_WIDTHS),
        compiler_params=_cparams(("parallel",)),
    )(x, nw, *ws)


_PROJ_TOTAL = sum(_PROJ_WIDTHS)
_PROJ_OFFSETS = tuple(sum(_PROJ_WIDTHS[:k]) for k in range(len(_PROJ_WIDTHS)))


def _mix_in_bwd(x, nw, ws, dres, dps, *, name):
    t = x.shape[0]
    tm = _tm(t)

    def body(x_ref, nw_ref, dres_ref, *refs):
        w_refs, dp_refs, dx_ref, dnw_ref, h_ref, dpb_ref = refs[:6], refs[6:12], refs[12], refs[13], refs[14], refs[15]
        i = pl.program_id(0)
        hf, xh, r = _rms_fwd(x_ref[...], nw_ref[...])
        h_ref[...] = hf.astype(BF16)
        dh = jnp.zeros((tm, D_MODEL), F32)
        for w_ref, dp_ref, off, width in zip(w_refs, dp_refs, _PROJ_OFFSETS, _PROJ_WIDTHS):
            dp = dp_ref[...].astype(BF16)
            dpb_ref[:, off:off + width] = dp
            dh = dh + jnp.dot(dp, w_ref[...], preferred_element_type=F32)
        dx, dnw = _rms_bwd(dh, xh, r, nw_ref[...])
        dx_ref[...] = dres_ref[...] + dx
        _acc_out(dnw_ref, i == 0, dnw)

    row = lambda i: (i, 0)
    const = lambda i: (0, 0)
    dx, dnw, h, dpb = pl.pallas_call(
        body, name=name + "_x", grid=(t // tm,),
        in_specs=[pl.BlockSpec((tm, D_MODEL), row), pl.BlockSpec((1, D_MODEL), const), pl.BlockSpec((tm, D_MODEL), row)]
        + [pl.BlockSpec((n, D_MODEL), const) for n in _PROJ_WIDTHS]
        + [pl.BlockSpec((tm, n), row) for n in _PROJ_WIDTHS],
        out_specs=(pl.BlockSpec((tm, D_MODEL), row), pl.BlockSpec((1, D_MODEL), const), pl.BlockSpec((tm, D_MODEL), row),
                   pl.BlockSpec((tm, _PROJ_TOTAL), row)),
        out_shape=(jax.ShapeDtypeStruct((t, D_MODEL), F32), jax.ShapeDtypeStruct((1, D_MODEL), F32),
                   jax.ShapeDtypeStruct((t, D_MODEL), BF16), jax.ShapeDtypeStruct((t, _PROJ_TOTAL), BF16)),
        compiler_params=_cparams(("arbitrary",)),
    )(x, nw, dres, *ws, *dps)
    return dx, dnw, _wgrad(dpb, h, _PROJ_TOTAL // 2, D_MODEL, name=name + "_w")


_SG_TILES = SG_WIDTH // LANES


def _lane_tiles(ref, rows=slice(None)):
    return [ref[rows, p * LANES:(p + 1) * LANES] for p in range(_SG_TILES)]


def _sg_fn(u, v, lng, lnb, wcs, sgbt):
    lane = lax.broadcasted_iota(jnp.int32, (1, LANES), 1)
    rr = lax.broadcasted_iota(jnp.int32, (SG_CHUNK, SG_CHUNK), 0)
    cc = lax.broadcasted_iota(jnp.int32, (SG_CHUNK, SG_CHUNK), 1)
    per_tile = LANES // SG_GROUP_DIM
    gu, gv = [_gelu(a) for a in u], [_gelu(a) for a in v]
    mu = sum(jnp.sum(a, axis=-1, keepdims=True) for a in gv) * (1.0 / SG_WIDTH)
    cen = [a - mu for a in gv]
    var = sum(jnp.sum(a * a, axis=-1, keepdims=True) for a in cen) * (1.0 / SG_WIDTH)
    rstd = lax.rsqrt(var + EPS)
    ln = [a * rstd * g + b for a, g, b in zip(cen, lng, lnb)]
    out = []
    for p in range(_SG_TILES):
        vs = None
        for e in range(per_tile):
            g = p * per_tile + e
            in_group = jnp.logical_and(lane >= e * SG_GROUP_DIM, lane < (e + 1) * SG_GROUP_DIM)
            w_causal = jnp.where(rr >= cc, wcs[g], 0.0)
            bias = jnp.sum(jnp.where(lane == g, sgbt, 0.0), axis=1, keepdims=True)
            term = jnp.where(in_group, mm(w_causal, ln[p]) + bias, 0.0)
            vs = term if vs is None else vs + term
        out.append(gu[p] * vs)
    return out


def _sg_fwd(u, v, lng, lnb, wc, sgbt, *, name):
    t = u.shape[0]
    tm = _tm(t)

    def body(u_ref, v_ref, lng_ref, lnb_ref, wc_ref, sgbt_ref, o_ref):
        wcs = [wc_ref[g] for g in range(SG_GROUPS)]
        for c in range(tm // SG_CHUNK):
            rows = pl.ds(c * SG_CHUNK, SG_CHUNK)
            out = _sg_fn(_lane_tiles(u_ref, rows), _lane_tiles(v_ref, rows), _lane_tiles(lng_ref), _lane_tiles(lnb_ref),
                         wcs, sgbt_ref[...])
            for p in range(_SG_TILES):
                o_ref[rows, p * LANES:(p + 1) * LANES] = out[p]

    row = lambda i: (i, 0)
    const = lambda i: (0, 0)
    return pl.pallas_call(
        body, name=name, grid=(t // tm,),
        in_specs=[pl.BlockSpec((tm, SG_WIDTH), row), pl.BlockSpec((tm, SG_WIDTH), row),
                  pl.BlockSpec((1, SG_WIDTH), const), pl.BlockSpec((1, SG_WIDTH), const),
                  pl.BlockSpec((SG_GROUPS, SG_CHUNK, SG_CHUNK), lambda i: (0, 0, 0)), pl.BlockSpec((SG_CHUNK, LANES), const)],
        out_specs=pl.BlockSpec((tm, SG_WIDTH), row),
        out_shape=jax.ShapeDtypeStruct((t, SG_WIDTH), F32),
        compiler_params=_cparams(("parallel",)),
    )(u, v, lng, lnb, wc, sgbt)


def _sg_bwd(u, v, lng, lnb, wc, sgbt, dout, *, name):
    t = u.shape[0]
    tm = _tm(t)

    def body(u_ref, v_ref, lng_ref, lnb_ref, wc_ref, sgbt_ref, do_ref, du_ref, dv_ref, dlng_ref, dlnb_ref, dwc_ref, dsgbt_ref):
        i = pl.program_id(0)
        wcs = [wc_ref[g] for g in range(SG_GROUPS)]
        tot = None
        for c in range(tm // SG_CHUNK):
            rows = pl.ds(c * SG_CHUNK, SG_CHUNK)
            _, vjp = jax.vjp(_sg_fn, _lane_tiles(u_ref, rows), _lane_tiles(v_ref, rows), _lane_tiles(lng_ref),
                             _lane_tiles(lnb_ref), wcs, sgbt_ref[...])
            du, dv, dlng, dlnb, dwcs, dsgbt = vjp(_lane_tiles(do_ref, rows))
            for p in range(_SG_TILES):
                du_ref[rows, p * LANES:(p + 1) * LANES] = du[p].astype(BF16)
                dv_ref[rows, p * LANES:(p + 1) * LANES] = dv[p].astype(BF16)
            part = (dlng, dlnb, dwcs, dsgbt)
            tot = part if tot is None else jax.tree.map(jnp.add, tot, part)
        dlng, dlnb, dwcs, dsgbt = tot
        _acc_out(dlng_ref, i == 0, jnp.concatenate(dlng, axis=1))
        _acc_out(dlnb_ref, i == 0, jnp.concatenate(dlnb, axis=1))
        _acc_out(dsgbt_ref, i == 0, dsgbt)
        for g in range(SG_GROUPS):
            @pl.when(i == 0)
            def _(g=g):
                dwc_ref[g] = dwcs[g]

            @pl.when(i > 0)
            def _(g=g):
                dwc_ref[g] += dwcs[g]

    row = lambda i: (i, 0)
    const = lambda i: (0, 0)
    wspec = pl.BlockSpec((SG_GROUPS, SG_CHUNK, SG_CHUNK), lambda i: (0, 0, 0))
    return pl.pallas_call(
        body, name=name, grid=(t // tm,),
        in_specs=[pl.BlockSpec((tm, SG_WIDTH), row), pl.BlockSpec((tm, SG_WIDTH), row),
                  pl.BlockSpec((1, SG_WIDTH), const), pl.BlockSpec((1, SG_WIDTH), const), wspec,
                  pl.BlockSpec((SG_CHUNK, LANES), const), pl.BlockSpec((tm, SG_WIDTH), row)],
        out_specs=(pl.BlockSpec((tm, SG_WIDTH), row), pl.BlockSpec((tm, SG_WIDTH), row),
                   pl.BlockSpec((1, SG_WIDTH), const), pl.BlockSpec((1, SG_WIDTH), const), wspec,
                   pl.BlockSpec((SG_CHUNK, LANES), const)),
        out_shape=(jax.ShapeDtypeStruct((t, SG_WIDTH), BF16), jax.ShapeDtypeStruct((t, SG_WIDTH), BF16),
                   jax.ShapeDtypeStruct((1, SG_WIDTH), F32), jax.ShapeDtypeStruct((1, SG_WIDTH), F32),
                   jax.ShapeDtypeStruct((SG_GROUPS, SG_CHUNK, SG_CHUNK), F32), jax.ShapeDtypeStruct((SG_CHUNK, LANES), F32)),
        compiler_params=_cparams(("arbitrary",)),
    )(u, v, lng, lnb, wc, sgbt, dout)


def _conv_taps(ext, w, tm):
    y = None
    for j in range(CONV_K):
        s = CONV_K - 1 - j
        shifted = ext if s == 0 else pltpu.roll(ext, s, 0)
        term = w[j:j + 1, :] * shifted[HALO:HALO + tm, :]
        y = term if y is None else y + term
    return y


def _post_conv(yq, yk, yv, bpre, apre, alog, dtb):
    def l2(a):
        return a * lax.rsqrt(jnp.sum(a * a, axis=-1, keepdims=True) + EPS)

    q = [l2(_silu(a)) for a in yq]
    k = [l2(_silu(a)) for a in yk]
    return q, k, _silu(yv), _sigmoid(bpre), -jnp.exp(alog) * _softplus(apre + dtb)


def _chunk_tril(tm):
    rr = lax.broadcasted_iota(jnp.int32, (tm, tm), 0)
    cc = lax.broadcasted_iota(jnp.int32, (tm, tm), 1)
    shift = DN_CHUNK.bit_length() - 1
    same = jnp.right_shift(rr, shift) == jnp.right_shift(cc, shift)
    return jnp.where(jnp.logical_and(same, rr >= cc), 1.0, 0.0).astype(F32)


def _halo_specs(tm, width, n_blocks_seq, n_blocks):
    per = tm // HALO
    prev = pl.BlockSpec((HALO, width), lambda i: (jnp.maximum(i * per - 1, 0), 0))
    nxt = pl.BlockSpec((HALO, width), lambda i: (jnp.minimum((i + 1) * per, n_blocks * per - 1), 0))
    return prev, nxt


def _split_heads(ref, base):
    return [ref[:, base + h * DN_HEAD_DIM: base + (h + 1) * DN_HEAD_DIM] for h in range(DN_HEADS)]


def _dn_prep_fwd(qkv, bpre, apre, conv_w, alog, dtb, seq, *, name):
    t = qkv.shape[0]
    tm = _tm(t)
    bps = seq // tm
    cw = 3 * DN_WIDTH

    def body(x_ref, halo_ref, b_ref, a_ref, w_ref, alog_ref, dtb_ref, q_ref, k_ref, v_ref, beta_ref, gc_ref):
        i = pl.program_id(0)
        keep = jnp.where(i % bps == 0, 0.0, 1.0)
        ext = jnp.concatenate([halo_ref[...] * keep, x_ref[...]], axis=0)
        y = _conv_taps(ext, w_ref[...], tm)
        yq = [y[:, h * DN_HEAD_DIM:(h + 1) * DN_HEAD_DIM] for h in range(DN_HEADS)]
        yk = [y[:, DN_WIDTH + h * DN_HEAD_DIM: DN_WIDTH + (h + 1) * DN_HEAD_DIM] for h in range(DN_HEADS)]
        q, k, v, beta, g = _post_conv(yq, yk, y[:, 2 * DN_WIDTH:], b_ref[...], a_ref[...], alog_ref[...], dtb_ref[...])
        for h in range(DN_HEADS):
            q_ref[:, h * DN_HEAD_DIM:(h + 1) * DN_HEAD_DIM] = q[h]
            k_ref[:, h * DN_HEAD_DIM:(h + 1) * DN_HEAD_DIM] = k[h]
        v_ref[...] = v
        beta_ref[...] = beta
        gc_ref[...] = mmx(_chunk_tril(tm), g)

    row = lambda i: (i, 0)
    const = lambda i: (0, 0)
    prev, _ = _halo_specs(tm, cw, bps, t // tm)
    return pl.pallas_call(
        body, name=name, grid=(t // tm,),
        in_specs=[pl.BlockSpec((tm, cw), row), prev, pl.BlockSpec((tm, LANES), row), pl.BlockSpec((tm, LANES), row),
                  pl.BlockSpec((CONV_K, cw), const), pl.BlockSpec((1, LANES), const), pl.BlockSpec((1, LANES), const)],
        out_specs=tuple(pl.BlockSpec((tm, n), row) for n in (DN_WIDTH, DN_WIDTH, DN_WIDTH, LANES, LANES)),
        out_shape=tuple(jax.ShapeDtypeStruct((t, n), F32) for n in (DN_WIDTH, DN_WIDTH, DN_WIDTH, LANES, LANES)),
        compiler_params=_cparams(("parallel",)),
    )(qkv, qkv, bpre, apre, conv_w, alog, dtb)


def _y_heads(y):
    yq = [y[:, h * DN_HEAD_DIM:(h + 1) * DN_HEAD_DIM] for h in range(DN_HEADS)]
    yk = [y[:, DN_WIDTH + h * DN_HEAD_DIM: DN_WIDTH + (h + 1) * DN_HEAD_DIM] for h in range(DN_HEADS)]
    return yq, yk, y[:, 2 * DN_WIDTH:]


def _dn_prep_bwd(qkv, bpre, apre, conv_w, alog, dtb, dq, dk, dv, dbeta, dgc, dgc2, seq, *, name):
    t = qkv.shape[0]
    tm = _tm(t)
    bps = seq // tm
    cw = 3 * DN_WIDTH
    n_ext = tm + HALO

    def body(x_ref, halo_ref, xn_ref, b_ref, a_ref, w_ref, alog_ref, dtb_ref, dq_ref, dk_ref, dv_ref, dqn_ref, dkn_ref,
             dvn_ref, dbeta_ref, dgc_ref, dgc2_ref, dx_ref, dw_ref, db_ref, da_ref, dalog_ref, ddtb_ref):
        i = pl.program_id(0)
        keep_prev = jnp.where(i % bps == 0, 0.0, 1.0)
        keep_next = jnp.where(i % bps == bps - 1, 0.0, 1.0)
        w = w_ref[...]
        x = x_ref[...]
        ext = jnp.concatenate([halo_ref[...] * keep_prev, x], axis=0)
        yq, yk, yv = _y_heads(_conv_taps(ext, w, tm))
        _, vjp = jax.vjp(_post_conv, yq, yk, yv, b_ref[...], a_ref[...], alog_ref[...], dtb_ref[...])
        dg = mmx_tn(_chunk_tril(tm), dgc_ref[...] + dgc2_ref[...])
        dyq, dyk, dyv, db, da, dalog, ddtb = vjp((_split_heads(dq_ref, 0), _split_heads(dk_ref, 0), dv_ref[...],
                                                  dbeta_ref[...], dg))
        dy = jnp.concatenate(dyq + dyk + [dyv], axis=1)
        ext_n = jnp.concatenate([x[tm - HALO:, :], xn_ref[...]], axis=0)
        _, vjp_n = jax.vjp(lambda *ys: _post_conv(*ys, b_ref[:HALO, :], a_ref[:HALO, :], alog_ref[...], dtb_ref[...])[:3],
                           *_y_heads(_conv_taps(ext_n, w, HALO)))
        dyq_n, dyk_n, dyv_n = vjp_n((_split_heads(dqn_ref, 0), _split_heads(dkn_ref, 0), dvn_ref[...]))
        dyext = jnp.concatenate([dy, jnp.concatenate(dyq_n + dyk_n + [dyv_n], axis=1) * keep_next], axis=0)

        @pl.when(i == 0)
        def _():
            dw_ref[...] = jnp.zeros_like(dw_ref)

        dx = None
        for j in range(CONV_K):
            s = CONV_K - 1 - j
            fut = dyext if s == 0 else pltpu.roll(dyext, n_ext - s, 0)
            term = w[j:j + 1, :] * fut[0:tm, :]
            dx = term if dx is None else dx + term
            past = ext if s == 0 else pltpu.roll(ext, s, 0)
            dw_ref[j:j + 1, :] += jnp.sum(dy * past[HALO:HALO + tm, :], axis=0, keepdims=True)
        dx_ref[...] = dx.astype(BF16)
        db_ref[...] = db.astype(BF16)
        da_ref[...] = da.astype(BF16)
        _acc_out(dalog_ref, i == 0, dalog)
        _acc_out(ddtb_ref, i == 0, ddtb)

    row = lambda i: (i, 0)
    const = lambda i: (0, 0)
    prev, nxt = _halo_specs(tm, cw, bps, t // tm)
    _, nxt_h = _halo_specs(tm, DN_WIDTH, bps, t // tm)
    tok = pl.BlockSpec((tm, DN_WIDTH), row)
    lanes = pl.BlockSpec((tm, LANES), row)
    return pl.pallas_call(
        body, name=name, grid=(t // tm,),
        in_specs=[pl.BlockSpec((tm, cw), row), prev, nxt, lanes, lanes,
                  pl.BlockSpec((CONV_K, cw), const), pl.BlockSpec((1, LANES), const), pl.BlockSpec((1, LANES), const),
                  tok, tok, tok, nxt_h, nxt_h, nxt_h, lanes, lanes, lanes],
        out_specs=(pl.BlockSpec((tm, cw), row), pl.BlockSpec((HALO, cw), const), lanes, lanes,
                   pl.BlockSpec((1, LANES), const), pl.BlockSpec((1, LANES), const)),
        out_shape=(jax.ShapeDtypeStruct((t, cw), BF16), jax.ShapeDtypeStruct((HALO, cw), F32),
                   jax.ShapeDtypeStruct((t, LANES), BF16), jax.ShapeDtypeStruct((t, LANES), BF16),
                   jax.ShapeDtypeStruct((1, LANES), F32), jax.ShapeDtypeStruct((1, LANES), F32)),
        compiler_params=_cparams(("arbitrary",)),
    )(qkv, qkv, qkv, bpre, apre, conv_w, alog, dtb, dq, dk, dv, dq, dk, dv, dbeta, dgc, dgc2)


def _inv_unit_lower(l_mats, eye):
    invs = [eye - l for l in l_mats]
    powers = list(l_mats)
    n = 2
    while n < eye.shape[0]:
        f = mmh if n == 2 else mm
        powers = [f(p, p) for p in powers]
        invs = [inv + f(inv, p) for inv, p in zip(invs, powers)]
        n *= 2
    return invs


@jax.custom_vjp
def _solve(l_mat, rhs, inv):
    return mmh(inv, rhs)


def _solve_fwd(l_mat, rhs, inv):
    sol = mmh(inv, rhs)
    return sol, (inv, sol)


def _solve_bwd(res, d_sol):
    inv, sol = res
    d_rhs = mm_tn(inv, d_sol)
    return -mm_nt(d_rhs, sol), d_rhs, jnp.zeros_like(inv)


_solve.defvjp(_solve_fwd, _solve_bwd)


def _prep_fn(q, k, v, gc, gr, b, inv):
    ids = range(len(q))
    c = q[0].shape[0]
    rr = lax.broadcasted_iota(jnp.int32, (c, c), 0)
    cc = lax.broadcasted_iota(jnp.int32, (c, c), 1)
    incl, strict = rr >= cc, rr > cc
    is_last = lax.broadcasted_iota(jnp.int32, (c, 1), 0) == c - 1
    qs = [q[i] * (DN_HEAD_DIM ** -0.5) for i in ids]
    decay = [jnp.where(incl, jnp.exp(jnp.where(incl, gc[i] - gr[i], 0.0)), 0.0) for i in ids]
    kb = [k[i] * b[i] for i in ids]
    vb = [v[i] * b[i] for i in ids]
    kk = [mm_nt(kb[i], k[i]) for i in ids]
    l_mat = [jnp.where(strict, kk[i] * decay[i], 0.0) for i in ids]
    eg = [jnp.exp(gc[i]) for i in ids]
    if inv is None:
        inv = _inv_unit_lower(l_mat, jnp.where(rr == cc, 1.0, 0.0).astype(F32))
    u_wy = [_solve(l_mat[i], vb[i], inv[i]) for i in ids]
    w_wy = [_solve(l_mat[i], kb[i] * eg[i], inv[i]) for i in ids]
    qk = [mm_nt(qs[i], k[i]) * decay[i] for i in ids]
    g_last = [jnp.sum(jnp.where(is_last, gc[i], 0.0), axis=0, keepdims=True) for i in ids]
    k_dec = [k[i] * jnp.exp(g_last[i] - gc[i]) for i in ids]
    egl = [jnp.broadcast_to(jnp.exp(g_last[i]), (1, LANES)) for i in ids]
    return [(w_wy[i], u_wy[i], qs[i] * eg[i], k_dec[i], qk[i], egl[i]) for i in ids], inv


def _seq_fn(w, u, qd, kd, qk, egl, s):
    ids = range(len(w))
    ws = [mm(w[i], s[i]) for i in ids]
    qs = [mm(qd[i], s[i]) for i in ids]
    v_new = [u[i] - ws[i] for i in ids]
    o = [qs[i] + mm(qk[i], v_new[i]) for i in ids]
    s_new = [s[i] * egl[i] + mm_tn(kd[i], v_new[i]) for i in ids]
    return o, s_new


def _lane_col(a, h):
    lane = lax.broadcasted_iota(jnp.int32, (1, LANES), 1)
    return jnp.sum(jnp.where(lane == h, a, 0.0), axis=1, keepdims=True)


def _col_lane(col, h):
    lane = lax.broadcasted_iota(jnp.int32, (1, LANES), 1)
    return jnp.where(lane == h, col, 0.0)


def _head_cols(h):
    return slice(h * DN_HEAD_DIM, (h + 1) * DN_HEAD_DIM)


def _chunk_rows(n):
    return pl.ds(pl.multiple_of(n * DN_CHUNK, DN_CHUNK), DN_CHUNK)


def _delta_prep(q, k, v, gc, grow, beta, *, name):
    t = q.shape[0]
    tm = _tm(t)
    cpb = tm // DN_CHUNK
    n_chunks = t // DN_CHUNK
    group = 4 if cpb % 4 == 0 else 2

    def body(q_ref, k_ref, v_ref, gc_ref, gr_ref, b_ref, w_ref, u_ref, qd_ref, kd_ref, qk_ref, egl_ref, inv_ref):
        def step(m, carry):
            probs = [(m * group + e, h) for e in range(group) for h in range(DN_HEADS)]
            gcb = [gc_ref[_chunk_rows(m * group + e), :] for e in range(group)]
            bb = [b_ref[_chunk_rows(m * group + e), :] for e in range(group)]
            grb = [gr_ref[m * group + e] for e in range(group)]
            for e in range(group):
                egl_ref[m * group + e] = jnp.zeros((HALO, LANES), F32)
            outs, invs = _prep_fn(
                [q_ref[_chunk_rows(n), _head_cols(h)] for n, h in probs], [k_ref[_chunk_rows(n), _head_cols(h)] for n, h in probs],
                [v_ref[_chunk_rows(n), _head_cols(h)] for n, h in probs],
                [_lane_col(gcb[e], h) for e in range(group) for h in range(DN_HEADS)],
                [grb[e][h:h + 1, :] for e in range(group) for h in range(DN_HEADS)],
                [_lane_col(bb[e], h) for e in range(group) for h in range(DN_HEADS)], None)
            for (n, h), (w, u, qd, kd, qk, egl), inv in zip(probs, outs, invs):
                rows, cols = _chunk_rows(n), _head_cols(h)
                w_ref[rows, cols] = w.astype(BF16)
                u_ref[rows, cols] = u
                qd_ref[rows, cols] = qd.astype(BF16)
                kd_ref[rows, cols] = kd.astype(BF16)
                qk_ref[n, h] = qk
                inv_ref[n, h] = inv
                egl_ref[n, h:h + 1, :] = egl
            return carry

        lax.fori_loop(0, cpb // group, step, 0)

    row = lambda i: (i, 0)
    tok = pl.BlockSpec((tm, DN_WIDTH), row)
    lanes = pl.BlockSpec((tm, LANES), row)
    sq = pl.BlockSpec((cpb, DN_HEADS, DN_CHUNK, DN_CHUNK), lambda i: (i, 0, 0, 0))
    return pl.pallas_call(
        body, name=name, grid=(t // tm,),
        in_specs=[tok, tok, tok, lanes, pl.BlockSpec((cpb, HALO, DN_CHUNK), lambda i: (i, 0, 0)), lanes],
        out_specs=(tok, tok, tok, tok, sq, pl.BlockSpec((cpb, HALO, LANES), lambda i: (i, 0, 0)), sq),
        out_shape=(jax.ShapeDtypeStruct((t, DN_WIDTH), BF16), jax.ShapeDtypeStruct((t, DN_WIDTH), F32),
                   jax.ShapeDtypeStruct((t, DN_WIDTH), BF16), jax.ShapeDtypeStruct((t, DN_WIDTH), BF16),
                   jax.ShapeDtypeStruct((n_chunks, DN_HEADS, DN_CHUNK, DN_CHUNK), F32),
                   jax.ShapeDtypeStruct((n_chunks, HALO, LANES), F32),
                   jax.ShapeDtypeStruct((n_chunks, DN_HEADS, DN_CHUNK, DN_CHUNK), F32)),
        compiler_params=_cparams(("parallel",)),
    )(q, k, v, gc, grow, beta)


def _delta_par_bwd(q, k, v, gc, grow, beta, inv, dw, du, dqd, dkd, dqk, degl, *, name):
    t = q.shape[0]
    tm = _tm(t)
    cpb = tm // DN_CHUNK
    n_chunks = t // DN_CHUNK
    group = 4 if cpb % 4 == 0 else 2

    def body(q_ref, k_ref, v_ref, gc_ref, gr_ref, b_ref, inv_ref, dw_ref, du_ref, dqd_ref, dkd_ref, dqk_ref, degl_ref,
             dq_ref, dk_ref, dv_ref, dgc_ref, dgr_ref, db_ref):
        def step(m, carry):
            chunks = [m * group + e for e in range(group)]
            probs = [(e, h) for e in range(group) for h in range(DN_HEADS)]
            rows = [_chunk_rows(n) for n in chunks]
            gcb, bb = [gc_ref[r, :] for r in rows], [b_ref[r, :] for r in rows]
            grb, deglb = [gr_ref[n] for n in chunks], [degl_ref[n] for n in chunks]
            for n in chunks:
                dgr_ref[n] = jnp.zeros((HALO, DN_CHUNK), F32)
            invs = [inv_ref[chunks[e], h] for e, h in probs]
            _, vjp = jax.vjp(lambda *a: _prep_fn(*a, invs)[0],
                             [q_ref[rows[e], _head_cols(h)] for e, h in probs], [k_ref[rows[e], _head_cols(h)] for e, h in probs],
                             [v_ref[rows[e], _head_cols(h)] for e, h in probs], [_lane_col(gcb[e], h) for e, h in probs],
                             [grb[e][h:h + 1, :] for e, h in probs], [_lane_col(bb[e], h) for e, h in probs])
            dq, dk, dv, dgc, dgr, db = vjp([(dw_ref[rows[e], _head_cols(h)], du_ref[rows[e], _head_cols(h)],
                                             dqd_ref[rows[e], _head_cols(h)], dkd_ref[rows[e], _head_cols(h)],
                                             dqk_ref[chunks[e], h], deglb[e][h:h + 1, :]) for e, h in probs])
            dgc_acc = [jnp.zeros((DN_CHUNK, LANES), F32) for _ in chunks]
            db_acc = [jnp.zeros((DN_CHUNK, LANES), F32) for _ in chunks]
            for i, (e, h) in enumerate(probs):
                cols = _head_cols(h)
                dq_ref[rows[e], cols] = dq[i]
                dk_ref[rows[e], cols] = dk[i]
                dv_ref[rows[e], cols] = dv[i]
                dgr_ref[chunks[e], h:h + 1, :] = dgr[i]
                dgc_acc[e] = dgc_acc[e] + _col_lane(dgc[i], h)
                db_acc[e] = db_acc[e] + _col_lane(db[i], h)
            for e in range(group):
                dgc_ref[rows[e], :] = dgc_acc[e]
                db_ref[rows[e], :] = db_acc[e]
            return carry

        lax.fori_loop(0, cpb // group, step, 0)

    row = lambda i: (i, 0)
    tok = pl.BlockSpec((tm, DN_WIDTH), row)
    lanes = pl.BlockSpec((tm, LANES), row)
    sq = pl.BlockSpec((cpb, DN_HEADS, DN_CHUNK, DN_CHUNK), lambda i: (i, 0, 0, 0))
    grs = pl.BlockSpec((cpb, HALO, DN_CHUNK), lambda i: (i, 0, 0))
    return pl.pallas_call(
        body, name=name, grid=(t // tm,),
        in_specs=[tok, tok, tok, lanes, grs, lanes, sq, tok, tok, tok, tok, sq, pl.BlockSpec((cpb, HALO, LANES), lambda i: (i, 0, 0))],
        out_specs=(tok, tok, tok, lanes, grs, lanes),
        out_shape=(jax.ShapeDtypeStruct((t, DN_WIDTH), F32),) * 3
        + (jax.ShapeDtypeStruct((t, LANES), F32), jax.ShapeDtypeStruct((n_chunks, HALO, DN_CHUNK), F32),
           jax.ShapeDtypeStruct((t, LANES), F32)),
        compiler_params=_cparams(("parallel",)),
    )(q, k, v, gc, grow, beta, inv, dw, du, dqd, dkd, dqk, degl)


def _seq_specs(n_seq, seq, reverse):
    pair = 4 if n_seq % 4 == 0 else (2 if n_seq % 2 == 0 else 1)
    tm = _tm(seq, TOKEN_BLOCK * 2 // max(pair, 2))
    nb = seq // tm
    cpb = tm // DN_CHUNK
    blk = (lambda j: nb - 1 - j) if reverse else (lambda j: j)
    tok = pl.BlockSpec((pair, tm, DN_WIDTH), lambda b, j: (b, blk(j), 0))
    sq = pl.BlockSpec((pair, cpb, DN_HEADS, DN_CHUNK, DN_CHUNK), lambda b, j: (b, blk(j), 0, 0, 0))
    rows8 = pl.BlockSpec((pair, cpb, HALO, LANES), lambda b, j: (b, blk(j), 0, 0))
    state = pl.BlockSpec((pair, cpb, DN_HEADS, DN_HEAD_DIM, DN_HEAD_DIM), lambda b, j: (b, blk(j), 0, 0, 0))
    return nb, cpb, pair, tok, sq, rows8, state


def _by_seq(a, n_seq):
    return a.reshape((n_seq, a.shape[0] // n_seq) + a.shape[1:])


def _flat_seq(a):
    return a.reshape((a.shape[0] * a.shape[1],) + a.shape[2:])


def _delta_seq_fwd(w, u, qd, kd, qk, egl, n_seq, seq, *, name):
    nb, cpb, pair, tok, sq, rows8, state = _seq_specs(n_seq, seq, False)
    probs = [(e, h) for e in range(pair) for h in range(DN_HEADS)]

    def body(w_ref, u_ref, qd_ref, kd_ref, qk_ref, egl_ref, o_ref, st_ref, s_s):
        @pl.when(pl.program_id(1) == 0)
        def _():
            s_s[...] = jnp.zeros_like(s_s)

        def step(n, carry):
            rows = _chunk_rows(n)
            eglb = [egl_ref[e, n] for e in range(pair)]
            s = [s_s[e, h] for e, h in probs]
            for (e, h), s_eh in zip(probs, s):
                st_ref[e, n, h] = s_eh
            o, s_new = _seq_fn([w_ref[e, rows, _head_cols(h)] for e, h in probs], [u_ref[e, rows, _head_cols(h)] for e, h in probs],
                               [qd_ref[e, rows, _head_cols(h)] for e, h in probs], [kd_ref[e, rows, _head_cols(h)] for e, h in probs],
                               [qk_ref[e, n, h] for e, h in probs], [eglb[e][h:h + 1, :] for e, h in probs], s)
            for i, (e, h) in enumerate(probs):
                o_ref[e, rows, _head_cols(h)] = o[i]
                s_s[e, h] = s_new[i]
            return carry

        lax.fori_loop(0, cpb, step, 0)

    o, states = pl.pallas_call(
        body, name=name, grid=(n_seq // pair, nb),
        in_specs=[tok, tok, tok, tok, sq, rows8],
        out_specs=(tok, state),
        out_shape=(jax.ShapeDtypeStruct((n_seq, seq, DN_WIDTH), F32),
                   jax.ShapeDtypeStruct((n_seq, seq // DN_CHUNK, DN_HEADS, DN_HEAD_DIM, DN_HEAD_DIM), F32)),
        scratch_shapes=[pltpu.VMEM((pair, DN_HEADS, DN_HEAD_DIM, DN_HEAD_DIM), F32)],
        compiler_params=_cparams(("parallel", "arbitrary")),
    )(*[_by_seq(a, n_seq) for a in (w, u, qd, kd, qk, egl)])
    return _flat_seq(o), _flat_seq(states)


def _delta_seq_bwd(w, u, qd, kd, qk, egl, states, do, n_seq, seq, *, name):
    nb, cpb, pair, tok, sq, rows8, state = _seq_specs(n_seq, seq, True)
    probs = [(e, h) for e in range(pair) for h in range(DN_HEADS)]

    def body(w_ref, u_ref, qd_ref, kd_ref, qk_ref, egl_ref, st_ref, do_ref, dw_ref, du_ref, dqd_ref, dkd_ref, dqk_ref,
             degl_ref, ds_s):
        @pl.when(pl.program_id(1) == 0)
        def _():
            ds_s[...] = jnp.zeros_like(ds_s)

        def step(m, carry):
            n = cpb - 1 - m
            rows = _chunk_rows(n)
            eglb = [egl_ref[e, n] for e in range(pair)]
            for e in range(pair):
                degl_ref[e, n] = jnp.zeros((HALO, LANES), F32)
            _, vjp = jax.vjp(_seq_fn, [w_ref[e, rows, _head_cols(h)].astype(F32) for e, h in probs],
                             [u_ref[e, rows, _head_cols(h)] for e, h in probs],
                             [qd_ref[e, rows, _head_cols(h)].astype(F32) for e, h in probs],
                             [kd_ref[e, rows, _head_cols(h)].astype(F32) for e, h in probs],
                             [qk_ref[e, n, h] for e, h in probs], [eglb[e][h:h + 1, :] for e, h in probs],
                             [st_ref[e, n, h] for e, h in probs])
            dw, du, dqd, dkd, dqk, degl, ds_in = vjp(([do_ref[e, rows, _head_cols(h)] for e, h in probs],
                                                      [ds_s[e, h] for e, h in probs]))
            for i, (e, h) in enumerate(probs):
                cols = _head_cols(h)
                dw_ref[e, rows, cols] = dw[i]
                du_ref[e, rows, cols] = du[i]
                dqd_ref[e, rows, cols] = dqd[i]
                dkd_ref[e, rows, cols] = dkd[i]
                dqk_ref[e, n, h] = dqk[i]
                degl_ref[e, n, h:h + 1, :] = degl[i]
                ds_s[e, h] = ds_in[i]
            return carry

        lax.fori_loop(0, cpb, step, 0)

    nc = seq // DN_CHUNK
    outs = pl.pallas_call(
        body, name=name, grid=(n_seq // pair, nb),
        in_specs=[tok, tok, tok, tok, sq, rows8, state, tok],
        out_specs=(tok, tok, tok, tok, sq, rows8),
        out_shape=(jax.ShapeDtypeStruct((n_seq, seq, DN_WIDTH), F32),) * 4
        + (jax.ShapeDtypeStruct((n_seq, nc, DN_HEADS, DN_CHUNK, DN_CHUNK), F32),
           jax.ShapeDtypeStruct((n_seq, nc, HALO, LANES), F32)),
        scratch_shapes=[pltpu.VMEM((pair, DN_HEADS, DN_HEAD_DIM, DN_HEAD_DIM), F32)],
        compiler_params=_cparams(("parallel", "arbitrary")),
    )(*[_by_seq(a, n_seq) for a in (w, u, qd, kd, qk, egl, states, do)])
    return tuple(_flat_seq(a) for a in outs)


def _dn_gate(o, z, dnw):
    return o * lax.rsqrt(jnp.mean(o * o, axis=-1, keepdims=True) + EPS) * dnw * _silu(z)


def _mix_out_fwd(x, sg, o, z, wo_sg, wo_dn, dnw, *, name):
    t = x.shape[0]
    tm = _tm(t)

    def body(x_ref, sg_ref, o_ref, z_ref, wsg_ref, wdn_ref, dnw_ref, y_ref, dn_s):
        for h, (oh, zh) in enumerate(zip(_split_heads(o_ref, 0), _split_heads(z_ref, 0))):
            dn_s[:, h * DN_HEAD_DIM:(h + 1) * DN_HEAD_DIM] = _dn_gate(oh, zh, dnw_ref[...]).astype(BF16)
        y_ref[...] = (x_ref[...] + jnp.dot(sg_ref[...].astype(BF16), wsg_ref[...], preferred_element_type=F32)
                      + jnp.dot(dn_s[...], wdn_ref[...], preferred_element_type=F32))

    row = lambda i: (i, 0)
    const = lambda i: (0, 0)
    half = pl.BlockSpec((tm, DN_WIDTH), row)
    return pl.pallas_call(
        body, name=name, grid=(t // tm,),
        in_specs=[pl.BlockSpec((tm, D_MODEL), row), half, half, half, pl.BlockSpec((SG_WIDTH, D_MODEL), const),
                  pl.BlockSpec((DN_WIDTH, D_MODEL), const), pl.BlockSpec((1, DN_HEAD_DIM), const)],
        out_specs=pl.BlockSpec((tm, D_MODEL), row),
        out_shape=jax.ShapeDtypeStruct((t, D_MODEL), F32),
        scratch_shapes=[pltpu.VMEM((tm, DN_WIDTH), BF16)],
        compiler_params=_cparams(("parallel",)),
    )(x, sg, o, z, wo_sg, wo_dn, dnw)


def _mix_out_bwd(dy, sg, o, z, wo_sg, wo_dn, dnw, *, name):
    t = dy.shape[0]
    tm = _tm(t)

    def body(dy_ref, sg_ref, o_ref, z_ref, wsg_ref, wdn_ref, dnw_ref, dsg_ref, do_ref, dz_ref, dwsg_ref, dwdn_ref, ddnw_ref, dn_s):
        i = pl.program_id(0)
        dyb = dy_ref[...].astype(BF16)
        nt = (((1,), (1,)), ((), ()))
        tn = (((0,), (0,)), ((), ()))
        dsg_ref[...] = lax.dot_general(dyb, wsg_ref[...], nt, preferred_element_type=F32)
        ddn = lax.dot_general(dyb, wdn_ref[...], nt, preferred_element_type=F32)
        ddnw = None
        for h, (oh, zh) in enumerate(zip(_split_heads(o_ref, 0), _split_heads(z_ref, 0))):
            cols = slice(h * DN_HEAD_DIM, (h + 1) * DN_HEAD_DIM)
            out, vjp = jax.vjp(_dn_gate, oh, zh, dnw_ref[...])
            dn_s[:, cols] = out.astype(BF16)
            doh, dzh, dw = vjp(ddn[:, cols])
            do_ref[:, cols] = doh
            dz_ref[:, cols] = dzh.astype(BF16)
            ddnw = dw if ddnw is None else ddnw + dw
        _acc_out(ddnw_ref, i == 0, ddnw)
        _acc_out(dwsg_ref, i == 0, lax.dot_general(sg_ref[...].astype(BF16), dyb, tn, preferred_element_type=F32))
        _acc_out(dwdn_ref, i == 0, lax.dot_general(dn_s[...], dyb, tn, preferred_element_type=F32))

    row = lambda i: (i, 0)
    const = lambda i: (0, 0)
    half = pl.BlockSpec((tm, DN_WIDTH), row)
    wspec = pl.BlockSpec((DN_WIDTH, D_MODEL), const)
    return pl.pallas_call(
        body, name=name, grid=(t // tm,),
        in_specs=[pl.BlockSpec((tm, D_MODEL), row), half, half, half, wspec, wspec, pl.BlockSpec((1, DN_HEAD_DIM), const)],
        out_specs=(half, half, half, wspec, wspec, pl.BlockSpec((1, DN_HEAD_DIM), const)),
        out_shape=(jax.ShapeDtypeStruct((t, DN_WIDTH), F32),) * 2 + (jax.ShapeDtypeStruct((t, DN_WIDTH), BF16),)
        + (jax.ShapeDtypeStruct((DN_WIDTH, D_MODEL), F32),) * 2 + (jax.ShapeDtypeStruct((1, DN_HEAD_DIM), F32),),
        scratch_shapes=[pltpu.VMEM((tm, DN_WIDTH), BF16)],
        compiler_params=_cparams(("arbitrary",)),
    )(dy, sg, o, z, wo_sg, wo_dn, dnw)


_MESH = pl.DeviceIdType.MESH
_HBM = pl.BlockSpec(memory_space=pl.ANY)


def _mesh_pos():
    x, y, c = lax.axis_index("x"), lax.axis_index("y"), lax.axis_index("c")
    return x, y, c, [(1 - x, y), (x, 1 - y), (1 - x, 1 - y)]


def _gather2(arrs, *, name):
    n = len(arrs)
    slots = N_DEV - 1

    def body(*refs):
        in_refs, out_refs = refs[:n], refs[n:2 * n]
        send_sems, recv_sems, local_sems = refs[2 * n:]
        x, y, c, chips = _mesh_pos()
        me, sibling = (x, y, c), (x, y, 1 - c)

        def copy(k, slot, block, to, src=None):
            dst = out_refs[k].at[4 * block[0] + 2 * block[1] + block[2]]
            return pltpu.make_async_remote_copy(src_ref=dst if src is None else src, dst_ref=dst,
                                                send_sem=send_sems.at[k * slots + slot], recv_sem=recv_sems.at[k * slots + slot],
                                                device_id=to, device_id_type=_MESH)

        local = [pltpu.make_async_copy(in_refs[k], out_refs[k].at[4 * x + 2 * y + c], local_sems.at[k]) for k in range(n)]
        sent = []
        for k in range(n):
            sent.append(copy(k, 0, me, sibling, src=in_refs[k]))
            sent += [copy(k, 1 + j, me, (*chip, c), src=in_refs[k]) for j, chip in enumerate(chips)]
        for cp in local + sent:
            cp.start()
        for j, chip in enumerate(chips):
            for k in range(n):
                copy(k, 1 + j, (*chip, c), me).wait_recv()
                passed = copy(k, 4 + j, (*chip, c), sibling)
                passed.start()
                sent.append(passed)
        for k in range(n):
            copy(k, 0, sibling, me).wait_recv()
            for j, chip in enumerate(chips):
                copy(k, 4 + j, (*chip, 1 - c), me).wait_recv()
        for cp in sent:
            cp.wait_send()
        for cp in local:
            cp.wait()

    return pl.pallas_call(
        body, name=name, in_specs=[_HBM] * n, out_specs=(_HBM,) * n,
        out_shape=tuple(jax.ShapeDtypeStruct((N_DEV,) + a.shape, a.dtype) for a in arrs),
        scratch_shapes=[pltpu.SemaphoreType.DMA((n * slots,)), pltpu.SemaphoreType.DMA((n * slots,)),
                        pltpu.SemaphoreType.DMA((n,))],
    )(*arrs)


_SEM = pl.BlockSpec(memory_space=pltpu.SEMAPHORE)
_EFFECT = pltpu.SideEffectType.DATAFLOW_SIDE_EFFECTING


def _direct_copies(src_refs, land_refs, send_sems, recv_sems, gather):
    x, y, c, _ = _mesh_pos()
    me = 4 * x + 2 * y + c
    n_peer = N_DEV - 1
    copies = []
    for r in range(1, N_DEV):
        px = 1 - x if r & 4 else x
        py = 1 - y if r & 2 else y
        pc = 1 - c if r & 1 else c
        for k, (src, land) in enumerate(zip(src_refs, land_refs)):
            copies.append(pltpu.make_async_remote_copy(
                src_ref=src if gather else src.at[4 * px + 2 * py + pc], dst_ref=land.at[me],
                send_sem=send_sems.at[k * n_peer + r - 1], recv_sem=recv_sems.at[k * n_peer + r - 1],
                device_id=(px, py, pc), device_id_type=_MESH))
    return copies


def _send_start(arrs, gather, after=None, *, name):
    n = len(arrs)
    lands = [lax.empty(((N_DEV,) + a.shape) if gather else a.shape, a.dtype) for a in arrs]
    n_in = 2 * n + (0 if after is None else 1)

    def body(*refs):
        src_refs, land_refs, send_sems, recv_sems, token = refs[:n], refs[n:2 * n], refs[n_in], refs[n_in + 1], refs[-1]
        for cp in _direct_copies(src_refs, land_refs, send_sems, recv_sems, gather):
            cp.start()
        token[...] = jnp.zeros_like(token)

    n_sem = n * (N_DEV - 1)
    bufs = list(arrs) + lands
    out = pl.pallas_call(
        body, name=name,
        out_shape=(pltpu.SemaphoreType.DMA((n_sem,)), pltpu.SemaphoreType.DMA((n_sem,)))
        + tuple(pltpu.HBM(b.shape, b.dtype) for b in bufs) + (jax.ShapeDtypeStruct((HALO, LANES), F32),),
        in_specs=[_HBM] * n_in, out_specs=(_SEM, _SEM) + (_HBM,) * (2 * n) + (pl.BlockSpec(memory_space=pltpu.VMEM),),
        input_output_aliases={i: 2 + i for i in range(2 * n)},
        compiler_params=pltpu.CompilerParams(has_side_effects=_EFFECT),
    )(*[pltpu.with_memory_space_constraint(b, pltpu.HBM) for b in bufs], *([] if after is None else [after]))
    return (out[0], out[1], list(out[2:2 + n]), list(out[2 + n:2 + 2 * n])), out[-1]


def _send_wait(started, gather, after, *, name):
    send_sems, recv_sems, srcs, lands = started
    n = len(srcs)

    def body(*refs):
        src_refs, land_refs, send_ref, recv_ref = refs[:n], refs[n:2 * n], refs[2 * n], refs[2 * n + 1]
        for cp in _direct_copies(src_refs, land_refs, send_ref, recv_ref, gather):
            cp.wait_send()
            cp.wait_recv()

    bufs = srcs + lands
    out = pl.pallas_call(
        body, name=name, out_shape=tuple(pltpu.HBM(b.shape, b.dtype) for b in bufs),
        in_specs=[_HBM] * (2 * n) + [_SEM, _SEM, _HBM], out_specs=(_HBM,) * (2 * n),
        input_output_aliases={i: i for i in range(2 * n)},
        compiler_params=pltpu.CompilerParams(has_side_effects=_EFFECT),
    )(*bufs, send_sems, recv_sems, after)
    return list(out[:n]), list(out[n:])


def _row_block(rows, limit=256):
    best = rows
    for cand in range(8, limit + 1, 8):
        if rows % cand == 0:
            best = cand
    return best if rows > limit else rows


def _adam(gp, w, m, v, *, name):
    p, rows, cols = gp.shape
    rb = _row_block(rows)

    def body(gp_ref, w_ref, m_ref, v_ref, g_ref, d_ref, m2_ref, v2_ref):
        g = gp_ref[0].astype(F32)
        for s in range(1, p):
            g = g + gp_ref[s].astype(F32)
        m2 = ADAM_B1 * m_ref[...] + (1.0 - ADAM_B1) * g
        v2 = ADAM_B2 * v_ref[...] + (1.0 - ADAM_B2) * (g * g)
        m_hat = m2 / (1.0 - ADAM_B1 ** ADAM_STEP)
        v_hat = v2 / (1.0 - ADAM_B2 ** ADAM_STEP)
        g_ref[...] = g
        d_ref[...] = -ADAM_LR * (m_hat / (jnp.sqrt(v_hat) + ADAM_EPS) + ADAM_WD * w_ref[...])
        m2_ref[...] = m2
        v2_ref[...] = v2

    blk = pl.BlockSpec((rb, cols), lambda i: (i, 0))
    return pl.pallas_call(
        body, name=name, grid=(rows // rb,),
        in_specs=[pl.BlockSpec((p, rb, cols), lambda i: (0, i, 0)), blk, blk, blk],
        out_specs=(blk,) * 4, out_shape=(jax.ShapeDtypeStruct((rows, cols), F32),) * 4,
        compiler_params=_cparams(("parallel",)),
    )(gp, w, m, v)


def _cols_full(g):
    return jnp.transpose(g, (1, 0, 2)).reshape(g.shape[1], N_DEV * g.shape[2])


def _pad_lanes(a, width=LANES):
    return jnp.pad(a, ((0, 0), (0, width - a.shape[1])))


def _chunk_rows_of(a):
    by_chunk = jnp.transpose(a[:, :DN_HEADS].reshape(-1, DN_CHUNK, DN_HEADS), (0, 2, 1))
    return jnp.pad(by_chunk, ((0, 0), (0, HALO - DN_HEADS), (0, 0)))


_SMALL = (("ffn1_norm", D_MODEL), ("mix_norm", D_MODEL), ("ffn2_norm", D_MODEL), ("final_norm", D_MODEL), ("a_log", DN_HEADS),
          ("dt_bias", DN_HEADS), ("dn_norm", DN_HEAD_DIM), ("sg_ln_g", SG_WIDTH), ("sg_ln_b", SG_WIDTH),
          ("sg_w", SG_GROUPS * SG_CHUNK * SG_CHUNK), ("sg_b", SG_GROUPS * SG_CHUNK), ("conv_w", CONV_K * 3 * DN_WIDTH))
_SMALL_ROWS = 1128
_SMALL_SHAPES = {"ffn1_norm": (1, D_MODEL), "mix_norm": (1, D_MODEL), "ffn2_norm": (1, D_MODEL), "final_norm": (D_MODEL,),
                 "a_log": (1, DN_HEADS), "dt_bias": (1, DN_HEADS), "dn_norm": (1, DN_HEAD_DIM), "sg_ln_g": (1, SG_WIDTH),
                 "sg_ln_b": (1, SG_WIDTH), "sg_w": (1, SG_GROUPS, SG_CHUNK, SG_CHUNK), "sg_b": (1, SG_GROUPS, SG_CHUNK)}


def _pack_small(d):
    flat = jnp.concatenate([d[name].reshape(-1) for name, _ in _SMALL])
    return jnp.pad(flat, (0, _SMALL_ROWS * LANES - flat.shape[0])).reshape(_SMALL_ROWS, LANES)


def _unpack_small(a):
    flat, out, at = a.reshape(-1), {}, 0
    for name, size in _SMALL:
        out[name] = flat[at:at + size]
        at += size
    return out


def kernel(x, ffn1_norm, ffn1_w_gate, ffn1_w_up, ffn1_w_down, mix_norm, w_in, conv_w, a_log, dt_bias, dn_norm, sg_ln_g, sg_ln_b, sg_w, sg_b, w_out, ffn2_norm, ffn2_w_gate, ffn2_w_up, ffn2_w_down, final_norm, loss_target, m_ffn1_norm, m_ffn1_w_gate, m_ffn1_w_up, m_ffn1_w_down, m_mix_norm, m_w_in, m_conv_w, m_a_log, m_dt_bias, m_dn_norm, m_sg_ln_g, m_sg_ln_b, m_sg_w, m_sg_b, m_w_out, m_ffn2_norm, m_ffn2_w_gate, m_ffn2_w_up, m_ffn2_w_down, m_final_norm, v_ffn1_norm, v_ffn1_w_gate, v_ffn1_w_up, v_ffn1_w_down, v_mix_norm, v_w_in, v_conv_w, v_a_log, v_dt_bias, v_dn_norm, v_sg_ln_g, v_sg_ln_b, v_sg_w, v_sg_b, v_w_out, v_ffn2_norm, v_ffn2_w_gate, v_ffn2_w_up, v_ffn2_w_down, v_final_norm):
    weights = dict(ffn1_norm=ffn1_norm, ffn1_w_gate=ffn1_w_gate, ffn1_w_up=ffn1_w_up, ffn1_w_down=ffn1_w_down, mix_norm=mix_norm, w_in=w_in, conv_w=conv_w, a_log=a_log, dt_bias=dt_bias, dn_norm=dn_norm, sg_ln_g=sg_ln_g, sg_ln_b=sg_ln_b, sg_w=sg_w, sg_b=sg_b, w_out=w_out, ffn2_norm=ffn2_norm, ffn2_w_gate=ffn2_w_gate, ffn2_w_up=ffn2_w_up, ffn2_w_down=ffn2_w_down, final_norm=final_norm)
    mom_m = dict(ffn1_norm=m_ffn1_norm, ffn1_w_gate=m_ffn1_w_gate, ffn1_w_up=m_ffn1_w_up, ffn1_w_down=m_ffn1_w_down, mix_norm=m_mix_norm, w_in=m_w_in, conv_w=m_conv_w, a_log=m_a_log, dt_bias=m_dt_bias, dn_norm=m_dn_norm, sg_ln_g=m_sg_ln_g, sg_ln_b=m_sg_ln_b, sg_w=m_sg_w, sg_b=m_sg_b, w_out=m_w_out, ffn2_norm=m_ffn2_norm, ffn2_w_gate=m_ffn2_w_gate, ffn2_w_up=m_ffn2_w_up, ffn2_w_down=m_ffn2_w_down, final_norm=m_final_norm)
    mom_v = dict(ffn1_norm=v_ffn1_norm, ffn1_w_gate=v_ffn1_w_gate, ffn1_w_up=v_ffn1_w_up, ffn1_w_down=v_ffn1_w_down, mix_norm=v_mix_norm, w_in=v_w_in, conv_w=v_conv_w, a_log=v_a_log, dt_bias=v_dt_bias, dn_norm=v_dn_norm, sg_ln_g=v_sg_ln_g, sg_ln_b=v_sg_ln_b, sg_w=v_sg_w, sg_b=v_sg_b, w_out=v_w_out, ffn2_norm=v_ffn2_norm, ffn2_w_gate=v_ffn2_w_gate, ffn2_w_up=v_ffn2_w_up, ffn2_w_down=v_ffn2_w_down, final_norm=v_final_norm)
    order = list(weights)
    big = ("ffn1_w_gate", "ffn1_w_up", "ffn1_w_down", "w_in", "w_out", "ffn2_w_gate", "ffn2_w_up", "ffn2_w_down")
    col_sharded = ("ffn1_w_gate", "ffn1_w_up", "w_in", "ffn2_w_gate", "ffn2_w_up")

    n_seq, seq, _ = x.shape
    t = n_seq * seq
    me = 4 * lax.axis_index("x") + 2 * lax.axis_index("y") + lax.axis_index("c")
    x0 = x.reshape(t, D_MODEL)
    tgt = loss_target.reshape(t, D_MODEL)

    def fill_own(land, own_block):
        return lax.dynamic_update_index_in_dim(land, own_block, me, 0)

    def rows_view(n, a):
        return jnp.transpose(a) if n in col_sharded else a

    def as_full(n, g):
        return g.reshape(-1, g.shape[-1])

    shards = {n: rows_view(n, weights[n][0]).astype(BF16) for n in big}
    ffn1_names, mix_names, ffn2_names = big[:3], big[3:5], big[5:]
    full = {n: as_full(n, g) for n, g in zip(ffn1_names, _gather2([shards[n] for n in ffn1_names], name="gather_ffn1"))}
    mix_srcs = [shards[n] for n in mix_names] + [conv_w[0]]
    mix_started, mix_token = _send_start(mix_srcs, True, full[ffn1_names[2]], name="gather_mix_start")
    ffn2_started, ffn2_token = _send_start([shards[n] for n in ffn2_names], True, mix_token, name="gather_ffn2_start")
    ffn1_norm_fwd = ffn1_norm + ffn2_token[:1, :1]
    alog, dtb = _pad_lanes(a_log), _pad_lanes(dt_bias)
    sgbt = _pad_lanes(sg_b[0].T)
    fnw = final_norm.reshape(1, D_MODEL)

    x1, h1, g1, u1 = _ffn_fwd(x0, ffn1_norm_fwd, full["ffn1_w_gate"], full["ffn1_w_up"], full["ffn1_w_down"], name="ffn1_fwd")
    mix_lands = [fill_own(land, src) for src, land in zip(*_send_wait(mix_started, True, x1, name="gather_mix_wait"))]
    full.update({n: as_full(n, g) for n, g in zip(mix_names, mix_lands)})
    conv_full = _cols_full(mix_lands[-1])
    w_in_t = full["w_in"]
    offs = (0, SG_WIDTH, 2 * SG_WIDTH, 2 * SG_WIDTH + 3 * DN_WIDTH, 2 * SG_WIDTH + 4 * DN_WIDTH)
    n_proj = offs[-1]

    def pad_rows(a):
        return jnp.pad(a, ((0, LANES - a.shape[0]), (0, 0)))

    ws = [w_in_t[offs[0]:offs[1]], w_in_t[offs[1]:offs[2]], w_in_t[offs[2]:offs[3]], w_in_t[offs[3]:offs[4]],
          pad_rows(w_in_t[n_proj:n_proj + DN_HEADS]), pad_rows(w_in_t[n_proj + DN_HEADS:n_proj + 2 * DN_HEADS])]
    wo_sg, wo_dn = full["w_out"][:SG_WIDTH], full["w_out"][SG_WIDTH:]
    u, v, qkv, z, bpre, apre = _mix_in_fwd(x1, mix_norm, ws, name="mix_in_fwd")
    sg_out = _sg_fwd(u, v, sg_ln_g, sg_ln_b, sg_w[0], sgbt, name="sg_fwd")
    q, k, vv, beta, gc = _dn_prep_fwd(qkv, bpre, apre, conv_full, alog, dtb, seq, name="dn_prep_fwd")
    grow = _chunk_rows_of(gc)
    wy_w, wy_u, q_dec, k_dec, qk, egl, inv = _delta_prep(q, k, vv, gc, grow, beta, name="delta_prep")
    o, states = _delta_seq_fwd(wy_w, wy_u, q_dec, k_dec, qk, egl, n_seq, seq, name="delta_seq_fwd")
    x2 = _mix_out_fwd(x1, sg_out, o, z, wo_sg, wo_dn, dn_norm, name="mix_out_fwd")
    ffn2_srcs, ffn2_lands = _send_wait(ffn2_started, True, x2, name="gather_ffn2_wait")
    full.update({n: as_full(n, fill_own(land, src)) for n, src, land in zip(ffn2_names, ffn2_srcs, ffn2_lands)})
    dx3, loss_part, d_fn, h2, g2, u2 = _ffn_fwd(x2, ffn2_norm, full["ffn2_w_gate"], full["ffn2_w_up"], full["ffn2_w_down"],
                                                tgt, fnw, name="ffn2_fwd_loss")
    loss = lax.psum(loss_part[0, 0], ("x", "y", "c"))

    dx2, d_n2, d_g2, d_u2, d_d2 = _ffn_bwd(x2, ffn2_norm, h2, g2, u2, full["ffn2_w_gate"], full["ffn2_w_up"],
                                           full["ffn2_w_down"], dx3, name="ffn2_bwd")
    def by_owner(d_rows):
        return d_rows.reshape(N_DEV, -1, D_MODEL)

    ffn2_pieces = [by_owner(d_g2), by_owner(d_u2), by_owner(d_d2)]
    ffn2_sent, sent_token = _send_start(ffn2_pieces, False, name="grads_ffn2_start")
    dsg, do, dz, d_wo_sg, d_wo_dn, d_dnw = _mix_out_bwd(dx2, sg_out, o, z, wo_sg, wo_dn, dn_norm + sent_token[:1, :1],
                                                        name="mix_out_bwd")
    d_seq = _delta_seq_bwd(wy_w, wy_u, q_dec, k_dec, qk, egl, states, do, n_seq, seq, name="delta_seq_bwd")
    dq, dk, dv, dgc_a, dgrow, dbeta = _delta_par_bwd(q, k, vv, gc, grow, beta, inv, *d_seq, name="delta_par_bwd")
    dgc_b = _pad_lanes(jnp.transpose(dgrow[:, :DN_HEADS, :], (0, 2, 1)).reshape(t, DN_HEADS))
    dqkv, d_conv, dbpre, dapre, d_alog, d_dtb = _dn_prep_bwd(qkv, bpre, apre, conv_full, alog, dtb, dq, dk, dv, dbeta, dgc_a,
                                                             dgc_b, seq, name="dn_prep_bwd")
    du, dvv, d_lng, d_lnb, d_wc, d_sgbt = _sg_bwd(u, v, sg_ln_g, sg_ln_b, sg_w[0], sgbt, dsg, name="sg_bwd")
    dx1, d_mixn, d_wp = _mix_in_bwd(x1, mix_norm, ws, dx2, (du, dvv, dqkv, dz, dbpre, dapre), name="mix_in_bwd")
    d_w_in_t = jnp.concatenate([d_wp[:n_proj], d_wp[_PROJ_OFFSETS[4]:_PROJ_OFFSETS[4] + DN_HEADS],
                                d_wp[_PROJ_OFFSETS[5]:_PROJ_OFFSETS[5] + DN_HEADS]], axis=0)
    d_w_out = jnp.concatenate([d_wo_sg, d_wo_dn], axis=0)
    mix_pieces = [by_owner(d_w_in_t), by_owner(d_w_out).astype(BF16)]
    mix_sent, sent_token = _send_start(mix_pieces, False, name="grads_mix_start")
    grad_x, d_n1, dg1, du1, a1, dyh1 = _ffn_bwd_x(x0, ffn1_norm + sent_token[:1, :1], g1, u1, full["ffn1_w_gate"],
                                                  full["ffn1_w_up"], full["ffn1_w_down"], dx1, name="ffn1_bwd_x")
    small_grads = dict(ffn1_norm=d_n1, mix_norm=d_mixn, ffn2_norm=d_n2, final_norm=d_fn, a_log=d_alog[:, :DN_HEADS],
                       dt_bias=d_dtb[:, :DN_HEADS], dn_norm=d_dnw, sg_ln_g=d_lng, sg_ln_b=d_lnb, sg_w=d_wc,
                       sg_b=d_sgbt[:, :SG_GROUPS].T, conv_w=d_conv[:CONV_K])
    small_src = _pack_small(small_grads)
    small_sent, small_token = _send_start([small_src], True, name="small_grads_start")
    late, tokens = [], []

    def send_early(k, grad):
        piece = by_owner(grad)
        sent, token = _send_start([piece], False, name="grads_" + ffn1_names[k] + "_start")
        late.append(((ffn1_names[k],), sent))
        tokens.append(token)
        return token

    _ffn_wgrads(h1, dg1, du1, a1, dyh1, send_early, small_token, name="ffn1_bwd")

    res = {}
    after = tokens[-1]

    def update(names, sent, after):
        pieces, lands = _send_wait(sent, False, after, name="grads_" + names[0] + "_wait")
        for n, land, p in zip(names, lands, pieces):
            got = fill_own(land, lax.dynamic_index_in_dim(p, me, 0, keepdims=False))
            upd = _adam(got, *[rows_view(n, src[n][0]) for src in (weights, mom_m, mom_v)], name="adam_" + n)
            res[n] = [rows_view(n, a) for a in upd]
            after = upd[0]
        return after

    for group in [(ffn2_names, ffn2_sent), (mix_names, mix_sent)] + late[:-1]:
        after = update(*group, after)
    (small_src,), (small_land,) = _send_wait(small_sent, True, after, name="small_grads_wait")
    small_parts = fill_own(small_land, small_src)
    zeros_conv = jnp.zeros((CONV_K * 3 * DN_WIDTH,), F32)
    packed = [_pack_small({**{n: src[n] for n, _ in _SMALL if n != "conv_w"}, "conv_w": zeros_conv})
              for src in (weights, mom_m, mom_v)]
    small_upd = _adam(small_parts, *packed, name="adam_small")
    small_res = [_unpack_small(a) for a in small_upd]
    conv_grad = lax.dynamic_slice_in_dim(small_res[0]["conv_w"].reshape(CONV_K, 3 * DN_WIDTH), me * (3 * DN_WIDTH // N_DEV),
                                         3 * DN_WIDTH // N_DEV, axis=1)
    res["conv_w"] = _adam(conv_grad[None], conv_w[0], m_conv_w[0], v_conv_w[0], name="adam_conv_w")
    update(*late[-1], res["conv_w"][0])

    outs = [[], [], [], []]
    for n in order:
        for kind in range(4):
            if n in res:
                outs[kind].append(res[n][kind][None])
            else:
                outs[kind].append(small_res[kind][n].reshape(_SMALL_SHAPES[n]))
    return (loss, grad_x.reshape(x.shape), *outs[0], *outs[1], *outs[2], *outs[3])
```
